```python
import jax, jax.numpy as jnp
from jax import lax
import numpy as np


D_MODEL = 1024
BATCH = 16
SEQ = 2048
DEPTH = 1

CONV_CH = D_MODEL
CONV_WIDTH = 31
N_HEADS = 16
HEAD_DIM = 64
ATTN_DIM = N_HEADS * HEAD_DIM
DILATION_GROUPS = ((128, 1), (512, 4), (2048, 16))
ATTN_BLOCK = 128
D_FF = 2816
FFN_CONV_WIDTH = 3
EPS = 1e-6
SPLITS = (2 * CONV_CH, 2 * CONV_CH + ATTN_DIM, 2 * CONV_CH + 2 * ATTN_DIM, 2 * CONV_CH + 3 * ATTN_DIM)
IN_COLS = 2 * CONV_CH + 3 * ATTN_DIM + 2 * D_MODEL

kernel_name = "hybrid_conformer_conv_dilated_attn_block"


def rms_norm(x, g):
    xf = x.astype(jnp.float32)
    xf = xf * lax.rsqrt(jnp.mean(xf * xf, axis=-1, keepdims=True) + EPS)
    return (xf * g.astype(jnp.float32)).astype(x.dtype)


def causal_depthwise_conv(u, w, b):
    width, ch = w.shape
    out = lax.conv_general_dilated(
        u, w[:, None, :].astype(u.dtype), window_strides=(1,), padding=[(width - 1, 0)],
        dimension_numbers=('NWC', 'WIO', 'NWC'), feature_group_count=ch)
    return out + b.astype(u.dtype)


def alibi_slopes(n_heads):
    return 2.0 ** (-8.0 * jnp.arange(1, n_heads + 1, dtype=jnp.float32) / n_heads)


def dilated_window_attention(q, k, v, slopes, window, dilation):
    B, S, H, Dh = q.shape
    n_back = window // dilation
    L = S // dilation
    nb = -(-L // ATTN_BLOCK)
    Lp = nb * ATTN_BLOCK

    def to_sub(t):
        t = t.reshape(B, L, dilation, H, Dh).transpose(0, 2, 1, 3, 4).reshape(B * dilation, L, H, Dh)
        t = jnp.pad(t, ((0, 0), (0, Lp - L), (0, 0), (0, 0)))
        return t.reshape(B * dilation, nb, ATTN_BLOCK, H, Dh)

    def with_prev(t):
        prev = jnp.pad(t, ((0, 0), (1, 0), (0, 0), (0, 0), (0, 0)))[:, :-1]
        return jnp.concatenate([prev, t], axis=2)

    qb = to_sub(q)
    kc = with_prev(to_sub(k))
    vc = with_prev(to_sub(v))

    scores = jnp.einsum('nbqhd,nbkhd->nbhqk', qb, kc)
    steps = (jnp.arange(ATTN_BLOCK)[:, None] + ATTN_BLOCK) - jnp.arange(2 * ATTN_BLOCK)[None, :]
    valid = (steps >= 0) & (steps <= n_back)
    first_block = (jnp.arange(nb) == 0)[:, None, None]
    prev_cols = (jnp.arange(2 * ATTN_BLOCK) < ATTN_BLOCK)[None, None, :]
    valid = valid[None] & ~(first_block & prev_cols)
    dist = (steps * dilation).astype(jnp.float32)
    scores = scores - slopes[:, None, None] * dist
    scores = jnp.where(valid[None, :, None], scores, -jnp.inf)
    m = jnp.max(scores, axis=-1, keepdims=True)
    p = jnp.exp(scores - m)
    den = jnp.sum(p, axis=-1)
    o = jnp.einsum('nbhqk,nbkhd->nbqhd', p, vc) / jnp.swapaxes(den, 2, 3)[..., None]
    lse = jnp.swapaxes(m[..., 0] + jnp.log(den), 2, 3)

    def from_sub(t):
        rest = t.shape[3:]
        t = t.reshape(B, dilation, Lp, *rest)[:, :, :L]
        return jnp.swapaxes(t, 1, 2).reshape(B, S, *rest)

    return from_sub(o), from_sub(lse)


def _fwd_setup_inputs(seed: int = 0) -> dict:
    key = jax.random.key(seed)
    ks = jax.random.split(key, 18)
    f32 = jnp.float32

    def nrm(k, shape, scale):
        return jax.random.normal(k, shape, f32) * scale

    return {
        'x': nrm(ks[0], (BATCH, SEQ, D_MODEL), 1.0),
        'norm1_g': 1.0 + nrm(ks[1], (DEPTH, D_MODEL), 0.1),
        'w_in': nrm(ks[2], (DEPTH, D_MODEL, IN_COLS), D_MODEL ** -0.5),
        'gate_b': nrm(ks[3], (DEPTH, 2 * D_MODEL), 0.02),
        'conv_w': nrm(ks[4], (DEPTH, CONV_WIDTH, CONV_CH), CONV_WIDTH ** -0.5),
        'conv_b': nrm(ks[5], (DEPTH, CONV_CH), 0.02),
        'conv_norm_g': 1.0 + nrm(ks[6], (DEPTH, CONV_CH), 0.1),
        'w_conv_out': nrm(ks[7], (DEPTH, CONV_CH, D_MODEL), CONV_CH ** -0.5),
        'q_norm_g': 1.0 + nrm(ks[8], (DEPTH, HEAD_DIM), 0.1),
        'k_norm_g': 1.0 + nrm(ks[9], (DEPTH, HEAD_DIM), 0.1),
        'w_attn_out': nrm(ks[10], (DEPTH, ATTN_DIM, D_MODEL), ATTN_DIM ** -0.5),
        'w_out': nrm(ks[11], (DEPTH, D_MODEL, D_MODEL), D_MODEL ** -0.5),
        'norm2_g': 1.0 + nrm(ks[12], (DEPTH, D_MODEL), 0.1),
        'w_up': nrm(ks[13], (DEPTH, D_MODEL, 2 * D_FF), D_MODEL ** -0.5),
        'ffn_conv_w': nrm(ks[14], (DEPTH, FFN_CONV_WIDTH, 2 * D_FF), FFN_CONV_WIDTH ** -0.5),
        'ffn_conv_b': nrm(ks[15], (DEPTH, 2 * D_FF), 0.02),
        'w_down': nrm(ks[16], (DEPTH, D_FF, D_MODEL), D_FF ** -0.5),
    }


def _fwd_reference(x, norm1_g, w_in, gate_b, conv_w, conv_b, conv_norm_g, w_conv_out,
              q_norm_g, k_norm_g, w_attn_out, w_out, norm2_g, w_up, ffn_conv_w,
              ffn_conv_b, w_down):
    B, S, _ = x.shape
    slopes = alibi_slopes(N_HEADS)
    for l in range(DEPTH):
        h = rms_norm(x, norm1_g[l])
        z = h @ w_in[l].astype(h.dtype)
        glu_in, q, k, v, gate_logits = jnp.split(z, SPLITS, axis=-1)

        a_val, a_gate = jnp.split(glu_in, 2, axis=-1)
        a = a_val * jax.nn.sigmoid(a_gate)
        a = causal_depthwise_conv(a, conv_w[l], conv_b[l])
        a = jax.nn.silu(rms_norm(a, conv_norm_g[l]))
        y_a = a @ w_conv_out[l].astype(a.dtype)

        q = rms_norm(q.reshape(B, S, N_HEADS, HEAD_DIM), q_norm_g[l]).astype(jnp.float32) * HEAD_DIM ** -0.5
        k = rms_norm(k.reshape(B, S, N_HEADS, HEAD_DIM), k_norm_g[l]).astype(jnp.float32)
        v = v.reshape(B, S, N_HEADS, HEAD_DIM).astype(jnp.float32)
        outs, lses = [], []
        for window, dilation in DILATION_GROUPS:
            o_g, lse_g = dilated_window_attention(q, k, v, slopes, window, dilation)
            outs.append(o_g)
            lses.append(lse_g)
        mix = jax.nn.softmax(jnp.stack(lses), axis=0)
        o = jnp.einsum('gbsh,gbshd->bshd', mix, jnp.stack(outs))
        o = o.astype(x.dtype).reshape(B, S, ATTN_DIM)
        y_b = o @ w_attn_out[l].astype(o.dtype)

        g = jax.nn.sigmoid(gate_logits + gate_b[l].astype(gate_logits.dtype))
        g_a, g_b = jnp.split(g, 2, axis=-1)
        x = x + (g_a * y_a + g_b * y_b) @ w_out[l].astype(x.dtype)

        h = rms_norm(x, norm2_g[l])
        u = causal_depthwise_conv(h @ w_up[l].astype(h.dtype), ffn_conv_w[l], ffn_conv_b[l])
        u_val, u_gate = jnp.split(u, 2, axis=-1)
        x = x + (jax.nn.silu(u_gate) * u_val) @ w_down[l].astype(x.dtype)
    return x


import jax as _jax
import jax.numpy as _jnp

TWIN_FORMAT = 'train_step'
FWD_PARAMS = ['x', 'norm1_g', 'w_in', 'gate_b', 'conv_w', 'conv_b', 'conv_norm_g', 'w_conv_out', 'q_norm_g', 'k_norm_g', 'w_attn_out', 'w_out', 'norm2_g', 'w_up', 'ffn_conv_w', 'ffn_conv_b', 'w_down']
TWIN_WEIGHTS = ['norm1_g', 'w_in', 'gate_b', 'conv_w', 'conv_b', 'conv_norm_g', 'w_conv_out', 'q_norm_g', 'k_norm_g', 'w_attn_out', 'w_out', 'norm2_g', 'w_up', 'ffn_conv_w', 'ffn_conv_b', 'w_down']
TWIN_DIFF_INPUT = 'x'
TWIN_INPUTS = ['x', 'norm1_g', 'w_in', 'gate_b', 'conv_w', 'conv_b', 'conv_norm_g', 'w_conv_out', 'q_norm_g', 'k_norm_g', 'w_attn_out', 'w_out', 'norm2_g', 'w_up', 'ffn_conv_w', 'ffn_conv_b', 'w_down', 'loss_target', 'm_norm1_g', 'm_w_in', 'm_gate_b', 'm_conv_w', 'm_conv_b', 'm_conv_norm_g', 'm_w_conv_out', 'm_q_norm_g', 'm_k_norm_g', 'm_w_attn_out', 'm_w_out', 'm_norm2_g', 'm_w_up', 'm_ffn_conv_w', 'm_ffn_conv_b', 'm_w_down', 'v_norm1_g', 'v_w_in', 'v_gate_b', 'v_conv_w', 'v_conv_b', 'v_conv_norm_g', 'v_w_conv_out', 'v_q_norm_g', 'v_k_norm_g', 'v_w_attn_out', 'v_w_out', 'v_norm2_g', 'v_w_up', 'v_ffn_conv_w', 'v_ffn_conv_b', 'v_w_down']
TWIN_OUTPUTS = ['loss', 'grad_x', 'grad_norm1_g', 'grad_w_in', 'grad_gate_b', 'grad_conv_w', 'grad_conv_b', 'grad_conv_norm_g', 'grad_w_conv_out', 'grad_q_norm_g', 'grad_k_norm_g', 'grad_w_attn_out', 'grad_w_out', 'grad_norm2_g', 'grad_w_up', 'grad_ffn_conv_w', 'grad_ffn_conv_b', 'grad_w_down', 'delta_norm1_g', 'delta_w_in', 'delta_gate_b', 'delta_conv_w', 'delta_conv_b', 'delta_conv_norm_g', 'delta_w_conv_out', 'delta_q_norm_g', 'delta_k_norm_g', 'delta_w_attn_out', 'delta_w_out', 'delta_norm2_g', 'delta_w_up', 'delta_ffn_conv_w', 'delta_ffn_conv_b', 'delta_w_down', 'new_m_norm1_g', 'new_m_w_in', 'new_m_gate_b', 'new_m_conv_w', 'new_m_conv_b', 'new_m_conv_norm_g', 'new_m_w_conv_out', 'new_m_q_norm_g', 'new_m_k_norm_g', 'new_m_w_attn_out', 'new_m_w_out', 'new_m_norm2_g', 'new_m_w_up', 'new_m_ffn_conv_w', 'new_m_ffn_conv_b', 'new_m_w_down', 'new_v_norm1_g', 'new_v_w_in', 'new_v_gate_b', 'new_v_conv_w', 'new_v_conv_b', 'new_v_conv_norm_g', 'new_v_w_conv_out', 'new_v_q_norm_g', 'new_v_k_norm_g', 'new_v_w_attn_out', 'new_v_w_out', 'new_v_norm2_g', 'new_v_w_up', 'new_v_ffn_conv_w', 'new_v_ffn_conv_b', 'new_v_w_down']
TWIN_LEAF_KINDS = {'loss': 'loss', 'grad_x': 'grad_x', 'grad_norm1_g': 'grad_w', 'grad_w_in': 'grad_w', 'grad_gate_b': 'grad_w', 'grad_conv_w': 'grad_w', 'grad_conv_b': 'grad_w', 'grad_conv_norm_g': 'grad_w', 'grad_w_conv_out': 'grad_w', 'grad_q_norm_g': 'grad_w', 'grad_k_norm_g': 'grad_w', 'grad_w_attn_out': 'grad_w', 'grad_w_out': 'grad_w', 'grad_norm2_g': 'grad_w', 'grad_w_up': 'grad_w', 'grad_ffn_conv_w': 'grad_w', 'grad_ffn_conv_b': 'grad_w', 'grad_w_down': 'grad_w', 'delta_norm1_g': 'delta_w', 'delta_w_in': 'delta_w', 'delta_gate_b': 'delta_w', 'delta_conv_w': 'delta_w', 'delta_conv_b': 'delta_w', 'delta_conv_norm_g': 'delta_w', 'delta_w_conv_out': 'delta_w', 'delta_q_norm_g': 'delta_w', 'delta_k_norm_g': 'delta_w', 'delta_w_attn_out': 'delta_w', 'delta_w_out': 'delta_w', 'delta_norm2_g': 'delta_w', 'delta_w_up': 'delta_w', 'delta_ffn_conv_w': 'delta_w', 'delta_ffn_conv_b': 'delta_w', 'delta_w_down': 'delta_w', 'new_m_norm1_g': 'new_m', 'new_m_w_in': 'new_m', 'new_m_gate_b': 'new_m', 'new_m_conv_w': 'new_m', 'new_m_conv_b': 'new_m', 'new_m_conv_norm_g': 'new_m', 'new_m_w_conv_out': 'new_m', 'new_m_q_norm_g': 'new_m', 'new_m_k_norm_g': 'new_m', 'new_m_w_attn_out': 'new_m', 'new_m_w_out': 'new_m', 'new_m_norm2_g': 'new_m', 'new_m_w_up': 'new_m', 'new_m_ffn_conv_w': 'new_m', 'new_m_ffn_conv_b': 'new_m', 'new_m_w_down': 'new_m', 'new_v_norm1_g': 'new_v', 'new_v_w_in': 'new_v', 'new_v_gate_b': 'new_v', 'new_v_conv_w': 'new_v', 'new_v_conv_b': 'new_v', 'new_v_conv_norm_g': 'new_v', 'new_v_w_conv_out': 'new_v', 'new_v_q_norm_g': 'new_v', 'new_v_k_norm_g': 'new_v', 'new_v_w_attn_out': 'new_v', 'new_v_w_out': 'new_v', 'new_v_norm2_g': 'new_v', 'new_v_w_up': 'new_v', 'new_v_ffn_conv_w': 'new_v', 'new_v_ffn_conv_b': 'new_v', 'new_v_w_down': 'new_v'}


def _forward(args):
    return _fwd_reference(*[args[k] for k in FWD_PARAMS])


def _output_shape():
    out = _jax.eval_shape(lambda: _forward(_fwd_setup_inputs(0)))
    return out.shape, out.dtype

N_MICROBATCH = 1
ADAM_LR = 0.001
ADAM_B1 = 0.9
ADAM_B2 = 0.999
ADAM_EPS = 1e-08
ADAM_WD = 0.01
ADAM_STEP = 10
PER_EXAMPLE_BATCH_AXIS = {'x': 0, 'loss_target': 0}
SHARED_INPUTS = []
_WEIGHT_DTYPES = {'norm1_g': _jnp.float32, 'w_in': _jnp.float32, 'gate_b': _jnp.float32, 'conv_w': _jnp.float32, 'conv_b': _jnp.float32, 'conv_norm_g': _jnp.float32, 'w_conv_out': _jnp.float32, 'q_norm_g': _jnp.float32, 'k_norm_g': _jnp.float32, 'w_attn_out': _jnp.float32, 'w_out': _jnp.float32, 'norm2_g': _jnp.float32, 'w_up': _jnp.float32, 'ffn_conv_w': _jnp.float32, 'ffn_conv_b': _jnp.float32, 'w_down': _jnp.float32}
MOMENT_SCALE = {'norm1_g': 1.594994e+00, 'w_in': 1.019534e-01, 'gate_b': 9.819763e-01, 'conv_w': 3.202315e-01, 'conv_b': 7.156704e+00, 'conv_norm_g': 4.689314e+00, 'w_conv_out': 1.200811e+00, 'q_norm_g': 3.526136e+00, 'k_norm_g': 3.490375e+00, 'w_attn_out': 1.447376e-01, 'w_out': 1.095522e+00, 'norm2_g': 2.616016e+01, 'w_up': 4.288787e-01, 'ffn_conv_w': 3.777185e+00, 'ffn_conv_b': 3.318418e+00, 'w_down': 3.334147e-01}


def _to_microbatches(a, axis):
    t = _jnp.moveaxis(a, axis, 0)
    t = t.reshape((N_MICROBATCH, t.shape[0] // N_MICROBATCH) + t.shape[1:])
    return _jnp.moveaxis(t, 1, axis + 1)


def setup_inputs(seed: int = 0) -> dict:
    inp = _fwd_setup_inputs(seed)
    key = _jax.random.fold_in(_jax.random.key(seed), 7919)
    shape, _ = _output_shape()
    out = dict(inp)
    out["loss_target"] = _jax.random.normal(_jax.random.fold_in(key, 0), shape, _jnp.float32)
    for i, name in enumerate(TWIN_WEIGHTS):
        w = inp[name].astype(_jnp.float32)
        if MOMENT_SCALE is None:
            s = _jnp.sqrt(_jnp.mean(_jnp.square(w)) + 1e-30)
        else:
            s = MOMENT_SCALE[name]
        km, kv = _jax.random.split(_jax.random.fold_in(key, i + 1))
        out[name] = w
        out["m_" + name] = s * _jax.random.normal(km, w.shape, _jnp.float32)
        out["v_" + name] = (s * s) * _jax.random.uniform(kv, w.shape, _jnp.float32, 0.5, 1.5)
    if N_MICROBATCH > 1:
        for name, axis in PER_EXAMPLE_BATCH_AXIS.items():
            out[name] = _to_microbatches(out[name], axis)
    return {'x': out['x'], 'norm1_g': out['norm1_g'], 'w_in': out['w_in'], 'gate_b': out['gate_b'], 'conv_w': out['conv_w'], 'conv_b': out['conv_b'], 'conv_norm_g': out['conv_norm_g'], 'w_conv_out': out['w_conv_out'], 'q_norm_g': out['q_norm_g'], 'k_norm_g': out['k_norm_g'], 'w_attn_out': out['w_attn_out'], 'w_out': out['w_out'], 'norm2_g': out['norm2_g'], 'w_up': out['w_up'], 'ffn_conv_w': out['ffn_conv_w'], 'ffn_conv_b': out['ffn_conv_b'], 'w_down': out['w_down'], 'loss_target': out['loss_target'], 'm_norm1_g': out['m_norm1_g'], 'm_w_in': out['m_w_in'], 'm_gate_b': out['m_gate_b'], 'm_conv_w': out['m_conv_w'], 'm_conv_b': out['m_conv_b'], 'm_conv_norm_g': out['m_conv_norm_g'], 'm_w_conv_out': out['m_w_conv_out'], 'm_q_norm_g': out['m_q_norm_g'], 'm_k_norm_g': out['m_k_norm_g'], 'm_w_attn_out': out['m_w_attn_out'], 'm_w_out': out['m_w_out'], 'm_norm2_g': out['m_norm2_g'], 'm_w_up': out['m_w_up'], 'm_ffn_conv_w': out['m_ffn_conv_w'], 'm_ffn_conv_b': out['m_ffn_conv_b'], 'm_w_down': out['m_w_down'], 'v_norm1_g': out['v_norm1_g'], 'v_w_in': out['v_w_in'], 'v_gate_b': out['v_gate_b'], 'v_conv_w': out['v_conv_w'], 'v_conv_b': out['v_conv_b'], 'v_conv_norm_g': out['v_conv_norm_g'], 'v_w_conv_out': out['v_w_conv_out'], 'v_q_norm_g': out['v_q_norm_g'], 'v_k_norm_g': out['v_k_norm_g'], 'v_w_attn_out': out['v_w_attn_out'], 'v_w_out': out['v_w_out'], 'v_norm2_g': out['v_norm2_g'], 'v_w_up': out['v_w_up'], 'v_ffn_conv_w': out['v_ffn_conv_w'], 'v_ffn_conv_b': out['v_ffn_conv_b'], 'v_w_down': out['v_w_down']}


def _loss(weights, diff, rest, loss_target):
    with _jax.named_scope("forward"):
        args = {**rest, TWIN_DIFF_INPUT: diff, **{k: w.astype(_WEIGHT_DTYPES[k]) for k, w in weights.items()}}
        y = _forward(args)
    with _jax.named_scope("loss_head"):
        err = _jnp.square(y.astype(_jnp.float32) - loss_target)
        return 0.5 * _jnp.sum(_jnp.mean(err, axis=-1)) if err.ndim else 0.5 * err


def _adamw(w, g, m, v):
    m = ADAM_B1 * m + (1.0 - ADAM_B1) * g
    v = ADAM_B2 * v + (1.0 - ADAM_B2) * _jnp.square(g)
    m_hat = m / (1.0 - ADAM_B1 ** ADAM_STEP)
    v_hat = v / (1.0 - ADAM_B2 ** ADAM_STEP)
    delta = -ADAM_LR * (m_hat / (_jnp.sqrt(v_hat) + ADAM_EPS) + ADAM_WD * w)
    return delta, m, v


def reference(x, norm1_g, w_in, gate_b, conv_w, conv_b, conv_norm_g, w_conv_out, q_norm_g, k_norm_g, w_attn_out, w_out, norm2_g, w_up, ffn_conv_w, ffn_conv_b, w_down, loss_target, m_norm1_g, m_w_in, m_gate_b, m_conv_w, m_conv_b, m_conv_norm_g, m_w_conv_out, m_q_norm_g, m_k_norm_g, m_w_attn_out, m_w_out, m_norm2_g, m_w_up, m_ffn_conv_w, m_ffn_conv_b, m_w_down, v_norm1_g, v_w_in, v_gate_b, v_conv_w, v_conv_b, v_conv_norm_g, v_w_conv_out, v_q_norm_g, v_k_norm_g, v_w_attn_out, v_w_out, v_norm2_g, v_w_up, v_ffn_conv_w, v_ffn_conv_b, v_w_down):
    given = dict(x=x, norm1_g=norm1_g, w_in=w_in, gate_b=gate_b, conv_w=conv_w, conv_b=conv_b, conv_norm_g=conv_norm_g, w_conv_out=w_conv_out, q_norm_g=q_norm_g, k_norm_g=k_norm_g, w_attn_out=w_attn_out, w_out=w_out, norm2_g=norm2_g, w_up=w_up, ffn_conv_w=ffn_conv_w, ffn_conv_b=ffn_conv_b, w_down=w_down, loss_target=loss_target, m_norm1_g=m_norm1_g, m_w_in=m_w_in, m_gate_b=m_gate_b, m_conv_w=m_conv_w, m_conv_b=m_conv_b, m_conv_norm_g=m_conv_norm_g, m_w_conv_out=m_w_conv_out, m_q_norm_g=m_q_norm_g, m_k_norm_g=m_k_norm_g, m_w_attn_out=m_w_attn_out, m_w_out=m_w_out, m_norm2_g=m_norm2_g, m_w_up=m_w_up, m_ffn_conv_w=m_ffn_conv_w, m_ffn_conv_b=m_ffn_conv_b, m_w_down=m_w_down, v_norm1_g=v_norm1_g, v_w_in=v_w_in, v_gate_b=v_gate_b, v_conv_w=v_conv_w, v_conv_b=v_conv_b, v_conv_norm_g=v_conv_norm_g, v_w_conv_out=v_w_conv_out, v_q_norm_g=v_q_norm_g, v_k_norm_g=v_k_norm_g, v_w_attn_out=v_w_attn_out, v_w_out=v_w_out, v_norm2_g=v_norm2_g, v_w_up=v_w_up, v_ffn_conv_w=v_ffn_conv_w, v_ffn_conv_b=v_ffn_conv_b, v_w_down=v_w_down)
    weights = {n: given[n] for n in TWIN_WEIGHTS}
    shared = {n: given[n] for n in SHARED_INPUTS}
    per_example = {n: given[n] for n in ['x']}
    grad_fn = _jax.value_and_grad(_loss, argnums=(0, 1))

    def one_microbatch(ex, loss_target):
        ex = dict(ex)
        diff = ex.pop(TWIN_DIFF_INPUT)
        return grad_fn(weights, diff, {**shared, **ex}, loss_target)

    if N_MICROBATCH == 1:
        loss, (grad_w, grad_x) = one_microbatch(per_example, given["loss_target"])
    else:
        def body(carry, xs):
            loss_sum, grad_sum = carry
            l_k, (gw_k, gx_k) = one_microbatch(xs[0], xs[1])
            with _jax.named_scope("update"):
                return (loss_sum + l_k, _jax.tree.map(_jnp.add, grad_sum, gw_k)), gx_k

        init = (_jnp.zeros((), _jnp.float32), _jax.tree.map(_jnp.zeros_like, weights))
        (loss, grad_w), grad_x = _jax.lax.scan(body, init, (per_example, given["loss_target"]))
    with _jax.named_scope("update"):
        delta_w, new_m, new_v = {}, {}, {}
        for n in TWIN_WEIGHTS:
            delta_w[n], new_m[n], new_v[n] = _adamw(weights[n], grad_w[n], given["m_" + n], given["v_" + n])
    return (loss, grad_x, *[grad_w[n] for n in TWIN_WEIGHTS], *[delta_w[n] for n in TWIN_WEIGHTS],
            *[new_m[n] for n in TWIN_WEIGHTS], *[new_v[n] for n in TWIN_WEIGHTS])
```

```python
import functools
import math
from typing import NamedTuple

import jax
import jax.numpy as jnp
from jax import lax
from jax.experimental import pallas as pl
from jax.experimental.pallas import tpu as pltpu

F32 = jnp.float32
BF16 = jnp.bfloat16

RMS_EPS = 1e-6
MASKED_SCORE = -1e30
ATTN_BLOCK = 128
DILATIONS = (1, 4, 16)
CONV_HALO = 32
FFN_HALO = 8
ADAM_LR, ADAM_B1, ADAM_B2, ADAM_EPS, ADAM_WD, ADAM_STEP = 0.001, 0.9, 0.999, 1e-08, 0.01, 10
V7X_VMEM_LIMIT_BYTES = 56 * 2 ** 20
N_CHIPS = 4
MESH = pl.DeviceIdType.MESH


class Dims(NamedTuple):
    d_model: int = 1024
    n_heads: int = 16
    head_dim: int = 64
    d_ff: int = 2816
    seq: int = 2048
    batch_local: int = 2
    conv_width: int = 31
    ffn_conv_width: int = 3

    @property
    def tokens(self):
        return self.seq * self.batch_local


def _params(*semantics):
    return pltpu.CompilerParams(dimension_semantics=semantics, vmem_limit_bytes=V7X_VMEM_LIMIT_BYTES)


def _pick(n, target, mult=128):
    if n <= target:
        return n
    best = None
    for t in range(mult, target + 1, mult):
        if n % t == 0:
            best = t
    assert best is not None, (n, target, mult)
    return best


def _sigmoid(v):
    return 1.0 / (1.0 + jnp.exp(-v))


def _mm_nn(a, w, *, out_dtype, name, residual=None, tm=1024, tn=1408, tk=2816):
    m, k = a.shape
    nsh, k2, c = w.shape
    assert k == k2 and a.dtype == BF16 and w.dtype == BF16
    n = nsh * c
    tm, tn, tk = _pick(m, tm, 8), _pick(c, tn), _pick(k, tk)
    nk, cpn = k // tk, c // tn

    def body(*refs):
        if residual is None:
            a_ref, w_ref, o_ref, acc = refs
        else:
            a_ref, w_ref, r_ref, o_ref, acc = refs
        prod = jnp.dot(a_ref[...], w_ref[...], preferred_element_type=F32)

        def finish(total):
            if residual is not None:
                total = total + r_ref[...]
            o_ref[...] = total.astype(out_dtype)

        if nk == 1:
            finish(prod)
        else:
            kk = pl.program_id(2)

            @pl.when(kk == 0)
            def _():
                acc[...] = prod

            @pl.when(kk > 0)
            def _():
                acc[...] += prod

            @pl.when(kk == nk - 1)
            def _():
                finish(acc[...])

    in_specs = [pl.BlockSpec((tm, tk), lambda i, j, kk: (i, kk)),
                pl.BlockSpec((None, tk, tn), lambda i, j, kk: (j // cpn, kk, j % cpn))]
    args = [a, w]
    if residual is not None:
        in_specs.append(pl.BlockSpec((tm, tn), lambda i, j, kk: (i, j)))
        args.append(residual)
    return pl.pallas_call(
        body, name=name, grid=(m // tm, n // tn, nk),
        in_specs=in_specs, out_specs=pl.BlockSpec((tm, tn), lambda i, j, kk: (i, j)),
        out_shape=jax.ShapeDtypeStruct((m, n), out_dtype),
        scratch_shapes=[pltpu.VMEM((tm, tn) if nk > 1 else (8, 128), F32)],
        compiler_params=_params("parallel", "parallel", "arbitrary"),
    )(*args)


def _mm_nt(a, w, *, out_dtype, name, tm=1024, tn=1408, tk=1792):
    m, k = a.shape
    nsh, r, c = w.shape
    assert k == nsh * c and a.dtype == BF16 and w.dtype == BF16
    tm, tn, tk = _pick(m, tm, 8), _pick(r, tn), _pick(c, tk)
    nk, cpk = k // tk, c // tk

    def body(a_ref, w_ref, o_ref, acc):
        prod = lax.dot_general(a_ref[...], w_ref[...], (((1,), (1,)), ((), ())), preferred_element_type=F32)
        if nk == 1:
            o_ref[...] = prod.astype(out_dtype)
        else:
            kk = pl.program_id(2)

            @pl.when(kk == 0)
            def _():
                acc[...] = prod

            @pl.when(kk > 0)
            def _():
                acc[...] += prod

            @pl.when(kk == nk - 1)
            def _():
                o_ref[...] = acc[...].astype(out_dtype)

    return pl.pallas_call(
        body, name=name, grid=(m // tm, r // tn, nk),
        in_specs=[pl.BlockSpec((tm, tk), lambda i, j, kk: (i, kk)),
                  pl.BlockSpec((None, tn, tk), lambda i, j, kk: (kk // cpk, j, kk % cpk))],
        out_specs=pl.BlockSpec((tm, tn), lambda i, j, kk: (i, j)),
        out_shape=jax.ShapeDtypeStruct((m, r), out_dtype),
        scratch_shapes=[pltpu.VMEM((tm, tn) if nk > 1 else (8, 128), F32)],
        compiler_params=_params("parallel", "parallel", "arbitrary"),
    )(a, w)


def _mm_tn(a, b, *, n_shards, name, tm=1408, tn=1408, tk=512):
    t, m = a.shape
    t2, n = b.shape
    assert t == t2 and a.dtype == BF16 and b.dtype == BF16
    c = n // n_shards
    tm, tn, tk = _pick(m, tm), _pick(c, tn), _pick(t, tk, 8)
    nk, cpn = t // tk, c // tn

    def body(a_ref, b_ref, o_ref, acc):
        kk = pl.program_id(2)
        prod = lax.dot_general(a_ref[...], b_ref[...], (((0,), (0,)), ((), ())), preferred_element_type=F32)

        @pl.when(kk == 0)
        def _():
            acc[...] = prod

        @pl.when(kk > 0)
        def _():
            acc[...] += prod

        @pl.when(kk == nk - 1)
        def _():
            o_ref[...] = acc[...]

    return pl.pallas_call(
        body, name=name, grid=(m // tm, n // tn, nk),
        in_specs=[pl.BlockSpec((tk, tm), lambda i, j, kk: (kk, i)),
                  pl.BlockSpec((tk, tn), lambda i, j, kk: (kk, j))],
        out_specs=pl.BlockSpec((None, tm, tn), lambda i, j, kk: (j // cpn, i, j % cpn)),
        out_shape=jax.ShapeDtypeStruct((n_shards, m, c), F32),
        scratch_shapes=[pltpu.VMEM((tm, tn), F32)],
        compiler_params=_params("parallel", "parallel", "arbitrary"),
    )(a, b)


def _row_spec(tr, width, col=0):
    return pl.BlockSpec((tr, width), lambda i, col=col: (i, col))


def _vec_spec(width, col=0):
    return pl.BlockSpec((1, width), lambda i, col=col: (0, col))


def _accumulate(ref, value, first):
    @pl.when(first)
    def _():
        ref[...] = value

    @pl.when(jnp.logical_not(first))
    def _():
        ref[...] += value


def _rmsnorm_fwd(x, g, *, name, tr=512):
    t, d = x.shape
    tr = _pick(t, tr, 8)

    def body(x_ref, g_ref, o_ref):
        xv = x_ref[...]
        r = lax.rsqrt(jnp.mean(xv * xv, axis=-1, keepdims=True) + RMS_EPS)
        o_ref[...] = (xv * r * g_ref[...]).astype(BF16)

    return pl.pallas_call(
        body, name=name, grid=(t // tr,),
        in_specs=[_row_spec(tr, d), _vec_spec(d)], out_specs=_row_spec(tr, d),
        out_shape=jax.ShapeDtypeStruct((t, d), BF16), compiler_params=_params("parallel"),
    )(x, g)


def _rmsnorm_bwd(x, g, dy, dres, *, name, want_bf16, tr=512):
    t, d = x.shape
    tr = _pick(t, tr, 8)

    def body(x_ref, g_ref, dy_ref, dres_ref, *outs):
        dx_ref, dg_ref = outs[0], outs[-1]
        xv, dyv = x_ref[...], dy_ref[...].astype(F32)
        r = lax.rsqrt(jnp.mean(xv * xv, axis=-1, keepdims=True) + RMS_EPS)
        gy = dyv * g_ref[...]
        dx = dres_ref[...] + r * gy - xv * (r * r * r) * jnp.mean(xv * gy, axis=-1, keepdims=True)
        dx_ref[...] = dx
        if want_bf16:
            outs[1][...] = dx.astype(BF16)
        _accumulate(dg_ref, jnp.sum(dyv * xv * r, axis=0, keepdims=True), pl.program_id(0) == 0)

    out_shape = [jax.ShapeDtypeStruct((t, d), F32)]
    out_specs = [_row_spec(tr, d)]
    if want_bf16:
        out_shape.append(jax.ShapeDtypeStruct((t, d), BF16))
        out_specs.append(_row_spec(tr, d))
    out_shape.append(jax.ShapeDtypeStruct((1, d), F32))
    out_specs.append(_vec_spec(d))
    return pl.pallas_call(
        body, name=name, grid=(t // tr,),
        in_specs=[_row_spec(tr, d), _vec_spec(d), _row_spec(tr, d), _row_spec(tr, d)],
        out_specs=out_specs, out_shape=out_shape, compiler_params=_params("arbitrary"),
    )(x, g, dy, dres)


def _head_mean(v, ones_ref, head_dim):
    hi = v.astype(BF16)
    lo = (v - hi.astype(F32)).astype(BF16)
    e = ones_ref[...]
    total = jnp.dot(hi, e, preferred_element_type=F32) + jnp.dot(lo, e, preferred_element_type=F32)
    return total * (1.0 / head_dim)


def _qkv_fwd(z, gq, gk, head_ones, dims, *, name, tr=256):
    t = z.shape[0]
    a = dims.n_heads * dims.head_dim
    tr = _pick(t, tr, 8)
    q_scale = dims.head_dim ** -0.5

    def body(q_ref, k_ref, v_ref, gq_ref, gk_ref, e_ref, qo_ref, ko_ref, vo_ref):
        qv, kv = q_ref[...], k_ref[...]
        rq = lax.rsqrt(_head_mean(qv * qv, e_ref, dims.head_dim) + RMS_EPS)
        rk = lax.rsqrt(_head_mean(kv * kv, e_ref, dims.head_dim) + RMS_EPS)
        qo_ref[...] = (qv * rq * gq_ref[...] * q_scale).astype(BF16)
        ko_ref[...] = (kv * rk * gk_ref[...]).astype(BF16)
        vo_ref[...] = v_ref[...].astype(BF16)

    return pl.pallas_call(
        body, name=name, grid=(t // tr,),
        in_specs=[_row_spec(tr, a, 2), _row_spec(tr, a, 3), _row_spec(tr, a, 4), _vec_spec(a), _vec_spec(a),
                  pl.BlockSpec((a, a), lambda i: (0, 0))],
        out_specs=[_row_spec(tr, a)] * 3, out_shape=[jax.ShapeDtypeStruct((t, a), BF16)] * 3,
        compiler_params=_params("parallel"),
    )(z, z, z, gq, gk, head_ones)


def _qkv_bwd(z, dqs, dks, dvs, gq, gk, head_ones, dims, *, name, tr=256):
    t = z.shape[0]
    a = dims.n_heads * dims.head_dim
    tr = _pick(t, tr, 8)
    q_scale = dims.head_dim ** -0.5
    ng = len(dqs)

    def body(*refs):
        q_ref, k_ref = refs[:2]
        dq_refs, dk_refs, dv_refs = refs[2:2 + ng], refs[2 + ng:2 + 2 * ng], refs[2 + 2 * ng:2 + 3 * ng]
        gq_ref, gk_ref, e_ref = refs[2 + 3 * ng:5 + 3 * ng]
        dz_ref, dgq_ref, dgk_ref = refs[5 + 3 * ng:]
        first = pl.program_id(0) == 0

        def norm_bwd(x_ref, d_refs, g_ref, scale, col, dg_ref):
            xv = x_ref[...]
            dy = sum(r[...] for r in d_refs) * scale
            r = lax.rsqrt(_head_mean(xv * xv, e_ref, dims.head_dim) + RMS_EPS)
            gy = dy * g_ref[...]
            dx = r * gy - xv * (r * r * r) * _head_mean(xv * gy, e_ref, dims.head_dim)
            dz_ref[:, col * a:(col + 1) * a] = dx.astype(BF16)
            _accumulate(dg_ref, jnp.sum(dy * xv * r, axis=0, keepdims=True), first)

        norm_bwd(q_ref, dq_refs, gq_ref, q_scale, 0, dgq_ref)
        norm_bwd(k_ref, dk_refs, gk_ref, 1.0, 1, dgk_ref)
        dz_ref[:, 2 * a:3 * a] = sum(r[...] for r in dv_refs).astype(BF16)

    in_specs = ([_row_spec(tr, a, 2), _row_spec(tr, a, 3)] + [_row_spec(tr, a)] * (3 * ng)
                + [_vec_spec(a), _vec_spec(a), pl.BlockSpec((a, a), lambda i: (0, 0))])
    return pl.pallas_call(
        body, name=name, grid=(t // tr,), in_specs=in_specs,
        out_specs=[_row_spec(tr, 3 * a), _vec_spec(a), _vec_spec(a)],
        out_shape=[jax.ShapeDtypeStruct((t, 3 * a), BF16)] + [jax.ShapeDtypeStruct((1, a), F32)] * 2,
        compiler_params=_params("arbitrary"),
    )(z, z, *dqs, *dks, *dvs, gq, gk, head_ones)


CONV_ROWS = 16


def _seq_specs(dims, ts, width, halo, col, *, nxt=False):
    nst, per = dims.seq // ts, ts // halo
    last = dims.tokens // halo - 1
    cur = pl.BlockSpec((ts, width), lambda b, i: (b * nst + i, col))
    if nxt:
        edge = pl.BlockSpec((halo, width), lambda b, i: (jnp.minimum((b * nst + i + 1) * per, last), col))
    else:
        edge = pl.BlockSpec((halo, width), lambda b, i: (jnp.maximum((b * nst + i) * per - 1, 0), col))
    return cur, edge


def _conv_branch_fwd(z, w, b, g, dims, *, name, ts=128):
    t, c, kw = z.shape[0], dims.d_model, dims.conv_width
    base = CONV_HALO - (kw - 1)

    def body(av_ref, hv_ref, ag_ref, hg_ref, w_ref, b_ref, g_ref, a1_ref, a3_ref, buf):
        i = pl.program_id(1)
        buf[CONV_HALO:, :] = av_ref[...] * _sigmoid(ag_ref[...])
        buf[0:CONV_HALO, :] = jnp.where(i > 0, hv_ref[...] * _sigmoid(hg_ref[...]), 0.0)
        for r0 in range(0, ts, CONV_ROWS):
            acc = jnp.broadcast_to(b_ref[...], (CONV_ROWS, c))
            for k in range(kw):
                acc = acc + w_ref[k:k + 1, :] * buf[pl.ds(r0 + base + k, CONV_ROWS), :]
            a1_ref[r0:r0 + CONV_ROWS, :] = acc
            a2 = acc * lax.rsqrt(jnp.mean(acc * acc, axis=-1, keepdims=True) + RMS_EPS) * g_ref[...]
            a3_ref[r0:r0 + CONV_ROWS, :] = (a2 * _sigmoid(a2)).astype(BF16)

    vec = pl.BlockSpec((1, c), lambda b, i: (0, 0))
    out = pl.BlockSpec((ts, c), lambda b, i: (b * (dims.seq // ts) + i, 0))
    return pl.pallas_call(
        body, name=name, grid=(dims.batch_local, dims.seq // ts),
        in_specs=[*_seq_specs(dims, ts, c, CONV_HALO, 0), *_seq_specs(dims, ts, c, CONV_HALO, 1),
                  pl.BlockSpec((CONV_HALO, c), lambda b, i: (0, 0)), vec, vec],
        out_specs=[out, out],
        out_shape=[jax.ShapeDtypeStruct((t, c), F32), jax.ShapeDtypeStruct((t, c), BF16)],
        scratch_shapes=[pltpu.VMEM((CONV_HALO + ts, c), F32)],
        compiler_params=_params("parallel", "parallel"),
    )(z, z, z, z, w, b, g)


def _conv_norm_bwd(da3, a1, g, *, name, tr=256):
    t, c = a1.shape
    tr = _pick(t, tr, 8)

    def body(d_ref, a_ref, g_ref, o_ref, dg_ref):
        a1v, gv = a_ref[...], g_ref[...]
        r = lax.rsqrt(jnp.mean(a1v * a1v, axis=-1, keepdims=True) + RMS_EPS)
        a2 = a1v * r * gv
        sg = _sigmoid(a2)
        da2 = d_ref[...].astype(F32) * sg * (1.0 + a2 * (1.0 - sg))
        gy = da2 * gv
        o_ref[...] = r * gy - a1v * (r * r * r) * jnp.mean(a1v * gy, axis=-1, keepdims=True)
        _accumulate(dg_ref, jnp.sum(da2 * a1v * r, axis=0, keepdims=True), pl.program_id(0) == 0)

    return pl.pallas_call(
        body, name=name, grid=(t // tr,),
        in_specs=[_row_spec(tr, c), _row_spec(tr, c), _vec_spec(c)],
        out_specs=[_row_spec(tr, c), _vec_spec(c)],
        out_shape=[jax.ShapeDtypeStruct((t, c), F32), jax.ShapeDtypeStruct((1, c), F32)],
        compiler_params=_params("arbitrary"),
    )(da3, a1, g)


def _conv_branch_bwd(da1, z, w, dims, *, name, ts=128):
    t, c, kw = z.shape[0], dims.d_model, dims.conv_width
    nst = dims.seq // ts
    base = CONV_HALO - (kw - 1)

    def body(d_ref, dn_ref, av_ref, hv_ref, ag_ref, hg_ref, w_ref, dz_ref, dw_ref, db_ref, abuf, dbuf):
        i = pl.program_id(1)
        first = jnp.logical_and(pl.program_id(0) == 0, i == 0)
        abuf[CONV_HALO:, :] = av_ref[...] * _sigmoid(ag_ref[...])
        abuf[0:CONV_HALO, :] = jnp.where(i > 0, hv_ref[...] * _sigmoid(hg_ref[...]), 0.0)
        d1 = d_ref[...]
        dbuf[0:ts, :] = d1
        dbuf[ts:, :] = jnp.where(i < nst - 1, dn_ref[...], 0.0)

        @pl.when(first)
        def _():
            dw_ref[...] = jnp.zeros_like(dw_ref)
            db_ref[...] = jnp.zeros_like(db_ref)

        db_ref[...] += jnp.sum(d1, axis=0, keepdims=True)
        for k in range(kw):
            dw_ref[k:k + 1, :] += jnp.sum(d1 * abuf[pl.ds(base + k, ts), :], axis=0, keepdims=True)
        for r0 in range(0, ts, CONV_ROWS):
            acc = jnp.zeros((CONV_ROWS, c), F32)
            for k in range(kw):
                acc = acc + w_ref[k:k + 1, :] * dbuf[pl.ds(r0 + (kw - 1) - k, CONV_ROWS), :]
            av = av_ref[r0:r0 + CONV_ROWS, :]
            sg = _sigmoid(ag_ref[r0:r0 + CONV_ROWS, :])
            dz_ref[r0:r0 + CONV_ROWS, 0:c] = (acc * sg).astype(BF16)
            dz_ref[r0:r0 + CONV_ROWS, c:2 * c] = (acc * av * sg * (1.0 - sg)).astype(BF16)

    cur, nxt = _seq_specs(dims, ts, c, CONV_HALO, 0, nxt=True)
    return pl.pallas_call(
        body, name=name, grid=(dims.batch_local, nst),
        in_specs=[cur, nxt, *_seq_specs(dims, ts, c, CONV_HALO, 0), *_seq_specs(dims, ts, c, CONV_HALO, 1),
                  pl.BlockSpec((CONV_HALO, c), lambda b, i: (0, 0))],
        out_specs=[pl.BlockSpec((ts, 2 * c), lambda b, i: (b * nst + i, 0)),
                   pl.BlockSpec((CONV_HALO, c), lambda b, i: (0, 0)), pl.BlockSpec((1, c), lambda b, i: (0, 0))],
        out_shape=[jax.ShapeDtypeStruct((t, 2 * c), BF16), jax.ShapeDtypeStruct((CONV_HALO, c), F32),
                   jax.ShapeDtypeStruct((1, c), F32)],
        scratch_shapes=[pltpu.VMEM((CONV_HALO + ts, c), F32), pltpu.VMEM((ts + CONV_HALO, c), F32)],
        compiler_params=_params("arbitrary", "arbitrary"),
    )(da1, da1, z, z, z, z, w)


FFN_ROWS = 16
FFN_COLS = 256


def _ffn_chunks(ts, f):
    cw = _pick(f, FFN_COLS)
    return [(r0, c0, cw) for r0 in range(0, ts, FFN_ROWS) for c0 in range(0, f, cw)]


def _ffn_conv(buf, w_ref, b_ref, r0, cols, kw):
    base = FFN_HALO - (kw - 1)
    u = jnp.broadcast_to(b_ref[:, cols], (FFN_ROWS, cols.stop - cols.start))
    for k in range(kw):
        u = u + w_ref[k:k + 1, cols] * buf[pl.ds(r0 + base + k, FFN_ROWS), cols]
    return u


def _ffn_act_fwd(up, w, b, dims, *, name, ts=128):
    t, f, kw = up.shape[0], dims.d_ff, dims.ffn_conv_width

    def body(up_ref, h_ref, w_ref, b_ref, o_ref, buf):
        buf[FFN_HALO:, :] = up_ref[...]
        buf[0:FFN_HALO, :] = jnp.where(pl.program_id(1) > 0, h_ref[...], 0.0)
        for r0, c0, cw in _ffn_chunks(ts, f):
            uv = _ffn_conv(buf, w_ref, b_ref, r0, slice(c0, c0 + cw), kw)
            ug = _ffn_conv(buf, w_ref, b_ref, r0, slice(f + c0, f + c0 + cw), kw)
            o_ref[r0:r0 + FFN_ROWS, c0:c0 + cw] = (ug * _sigmoid(ug) * uv).astype(BF16)

    full = lambda rows: pl.BlockSpec((rows, 2 * f), lambda b_, i: (0, 0))
    return pl.pallas_call(
        body, name=name, grid=(dims.batch_local, dims.seq // ts),
        in_specs=[*_seq_specs(dims, ts, 2 * f, FFN_HALO, 0), full(FFN_HALO), full(1)],
        out_specs=pl.BlockSpec((ts, f), lambda b_, i: (b_ * (dims.seq // ts) + i, 0)),
        out_shape=jax.ShapeDtypeStruct((t, f), BF16),
        scratch_shapes=[pltpu.VMEM((FFN_HALO + ts, 2 * f), F32)],
        compiler_params=_params("parallel", "parallel"),
    )(up, up, w, b)


def _ffn_act_bwd(dact, up, w, b, dims, *, name, ts=128):
    t, f, kw = up.shape[0], dims.d_ff, dims.ffn_conv_width
    base = FFN_HALO - (kw - 1)

    def body(d_ref, up_ref, h_ref, w_ref, b_ref, du_ref, dw_ref, db_ref, buf):
        i = pl.program_id(1)
        first = jnp.logical_and(pl.program_id(0) == 0, i == 0)
        buf[FFN_HALO:, :] = up_ref[...]
        buf[0:FFN_HALO, :] = jnp.where(i > 0, h_ref[...], 0.0)
        for r0, c0, cw in _ffn_chunks(ts, f):
            vcols, gcols = slice(c0, c0 + cw), slice(f + c0, f + c0 + cw)
            uv = _ffn_conv(buf, w_ref, b_ref, r0, vcols, kw)
            ug = _ffn_conv(buf, w_ref, b_ref, r0, gcols, kw)
            d = d_ref[r0:r0 + FFN_ROWS, vcols].astype(F32)
            sg = _sigmoid(ug)
            du_ref[r0:r0 + FFN_ROWS, vcols] = d * ug * sg
            du_ref[r0:r0 + FFN_ROWS, gcols] = d * uv * sg * (1.0 + ug * (1.0 - sg))

        @pl.when(first)
        def _():
            dw_ref[...] = jnp.zeros_like(dw_ref)
            db_ref[...] = jnp.zeros_like(db_ref)

        du = du_ref[...]
        db_ref[...] += jnp.sum(du, axis=0, keepdims=True)
        for k in range(kw):
            dw_ref[k:k + 1, :] += jnp.sum(du * buf[pl.ds(base + k, ts), :], axis=0, keepdims=True)

    nst = dims.seq // ts
    full = lambda rows: pl.BlockSpec((rows, 2 * f), lambda b_, i: (0, 0))
    return pl.pallas_call(
        body, name=name, grid=(dims.batch_local, nst),
        in_specs=[pl.BlockSpec((ts, f), lambda b_, i: (b_ * nst + i, 0)),
                  *_seq_specs(dims, ts, 2 * f, FFN_HALO, 0), full(FFN_HALO), full(1)],
        out_specs=[pl.BlockSpec((ts, 2 * f), lambda b_, i: (b_ * nst + i, 0)), full(FFN_HALO), full(1)],
        out_shape=[jax.ShapeDtypeStruct((t, 2 * f), F32), jax.ShapeDtypeStruct((FFN_HALO, 2 * f), F32),
                   jax.ShapeDtypeStruct((1, 2 * f), F32)],
        scratch_shapes=[pltpu.VMEM((FFN_HALO + ts, 2 * f), F32)],
        compiler_params=_params("arbitrary", "arbitrary"),
    )(dact, up, up, w, b)


def _ffn_conv_bwd(du, w, dims, *, name, ts=128):
    t, f2 = du.shape
    kw = dims.ffn_conv_width
    nst = dims.seq // ts

    def body(d_ref, dn_ref, w_ref, o_ref, buf):
        buf[0:ts, :] = d_ref[...]
        buf[ts:, :] = jnp.where(pl.program_id(1) < nst - 1, dn_ref[...], 0.0)
        for r0, c0, cw in _ffn_chunks(ts, f2):
            cols = slice(c0, c0 + cw)
            acc = jnp.zeros((FFN_ROWS, cw), F32)
            for k in range(kw):
                acc = acc + w_ref[k:k + 1, cols] * buf[pl.ds(r0 + (kw - 1) - k, FFN_ROWS), cols]
            o_ref[r0:r0 + FFN_ROWS, cols] = acc.astype(BF16)

    return pl.pallas_call(
        body, name=name, grid=(dims.batch_local, nst),
        in_specs=[*_seq_specs(dims, ts, f2, FFN_HALO, 0, nxt=True), pl.BlockSpec((FFN_HALO, f2), lambda b_, i: (0, 0))],
        out_specs=pl.BlockSpec((ts, f2), lambda b_, i: (b_ * nst + i, 0)),
        out_shape=jax.ShapeDtypeStruct((t, f2), BF16),
        scratch_shapes=[pltpu.VMEM((ts + FFN_HALO, f2), F32)],
        compiler_params=_params("parallel", "parallel"),
    )(du, du, w)


def _alibi_slope(h, n_heads):
    return 2.0 ** (-8.0 * (h + 1) / n_heads)


def _dot_nt(a, b):
    return lax.dot_general(a, b, (((1,), (1,)), ((), ())), preferred_element_type=F32)


def _dot_tn(a, b):
    return lax.dot_general(a, b, (((0,), (0,)), ((), ())), preferred_element_type=F32)


def _attn_view(x, dims, dil):
    return x.reshape(dims.batch_local, dims.seq // dil, dil * x.shape[-1])


def _attn_fwd_group(q, k, v, state, dims, dil, *, last, name):
    t, a = q.shape
    assert 2 * dims.head_dim == 128 and dims.n_heads % 2 == 0
    blk, hd = ATTN_BLOCK, dims.head_dim
    nb = dims.seq // dil // blk
    has_prev = nb > 1
    nkeys = 2 * blk if has_prev else blk

    def body(*refs):
        it = iter(refs)
        q_ref, kc_ref, vc_ref = next(it), next(it), next(it)
        kp_ref, vp_ref = (next(it), next(it)) if has_prev else (None, None)
        m_in, l_in, acc_in = (next(it), next(it), next(it)) if state is not None else (None, None, None)
        outs = list(it)
        iq = lax.broadcasted_iota(jnp.int32, (blk, nkeys), 0)
        jk = lax.broadcasted_iota(jnp.int32, (blk, nkeys), 1)
        if has_prev:
            steps = iq + blk - jk
            valid = (steps >= 0) & (steps <= blk) & ((jk >= blk) | (pl.program_id(2) > 0))
        else:
            steps = iq - jk
            valid = steps >= 0
        dist = steps.astype(F32) * float(dil)
        low = lax.broadcasted_iota(jnp.int32, (blk, 2 * hd), 1) < hd
        for hp in range(dims.n_heads // 2):
            sl = slice(2 * hd * hp, 2 * hd * (hp + 1))
            q2 = q_ref[:, sl]
            if has_prev:
                kcat = jnp.concatenate([kp_ref[:, sl], kc_ref[:, sl]], axis=0)
                vcat = jnp.concatenate([vp_ref[:, sl], vc_ref[:, sl]], axis=0)
            else:
                kcat, vcat = kc_ref[:, sl], vc_ref[:, sl]
            halves = []
            for half in range(2):
                col = 2 * hd * hp + hd * half
                qh = jnp.where(low if half == 0 else jnp.logical_not(low), q2, jnp.zeros_like(q2))
                sc = _dot_nt(qh, kcat) - _alibi_slope(2 * hp + half, dims.n_heads) * dist
                sc = jnp.where(valid, sc, MASKED_SCORE)
                row_max = jnp.max(sc, axis=-1, keepdims=True)
                if state is None:
                    m_new = row_max
                    p = jnp.exp(sc - m_new)
                    alpha = None
                    l_new = jnp.sum(p, axis=-1, keepdims=True)
                else:
                    m_old = m_in[:, col:col + 1]
                    m_new = jnp.maximum(m_old, row_max)
                    p = jnp.exp(sc - m_new)
                    alpha = jnp.exp(m_old - m_new)
                    l_new = alpha * l_in[:, col:col + 1] + jnp.sum(p, axis=-1, keepdims=True)
                pv = jnp.dot(p.astype(BF16), vcat, preferred_element_type=F32)
                halves.append((m_new, l_new, alpha, pv))
            (m_a, l_a, al_a, pv_a), (m_b, l_b, al_b, pv_b) = halves
            if state is None:
                acc = jnp.where(low, pv_a, pv_b)
            else:
                old = acc_in[:, sl]
                acc = jnp.where(low, al_a * old + pv_a, al_b * old + pv_b)
            m2 = jnp.where(low, m_a, m_b)
            l2 = jnp.where(low, l_a, l_b)
            if last:
                outs[0][:, sl] = (acc / l2).astype(BF16)
                outs[1][:, sl] = m2 + jnp.log(l2)
            else:
                outs[0][:, sl] = m2
                outs[1][:, sl] = l2
                outs[2][:, sl] = acc

    cur = pl.BlockSpec((None, blk, a), lambda b, r, i: (b, i, r))
    prev = pl.BlockSpec((None, blk, a), lambda b, r, i: (b, jnp.maximum(i - 1, 0), r))
    args, in_specs = [q, k, v], [cur, cur, cur]
    if has_prev:
        args += [k, v]
        in_specs += [prev, prev]
    if state is not None:
        args += list(state)
        in_specs += [cur] * 3
    shape = lambda dt: jax.ShapeDtypeStruct((dims.batch_local, dims.seq // dil, dil * a), dt)
    out_shape = [shape(BF16), shape(F32)] if last else [shape(F32)] * 3
    outs = pl.pallas_call(
        body, name=name, grid=(dims.batch_local, dil, nb),
        in_specs=in_specs, out_specs=[cur] * len(out_shape), out_shape=out_shape,
        compiler_params=_params("parallel", "parallel", "parallel"),
    )(*[_attn_view(x, dims, dil) for x in args])
    return tuple(o.reshape(t, a) for o in outs)


def _attn_delta(do, o, head_ones, dims, *, name, tr=512):
    t, a = o.shape
    tr = _pick(t, tr, 8)

    def body(do_ref, o_ref, e_ref, d_ref):
        prod = do_ref[...].astype(F32) * o_ref[...].astype(F32)
        d_ref[...] = _head_mean(prod, e_ref, dims.head_dim) * float(dims.head_dim)

    return pl.pallas_call(
        body, name=name, grid=(t // tr,),
        in_specs=[_row_spec(tr, a), _row_spec(tr, a), pl.BlockSpec((a, a), lambda i: (0, 0))],
        out_specs=_row_spec(tr, a), out_shape=jax.ShapeDtypeStruct((t, a), F32),
        compiler_params=_params("parallel"),
    )(do, o, head_ones)


def _attn_bwd_group(q, k, v, do, lse, delta, dims, dil, *, name):
    t, a = q.shape
    blk, hd = ATTN_BLOCK, dims.head_dim
    nb = dims.seq // dil // blk
    has_next = nb > 1

    def body(*refs):
        k_ref, v_ref, q_ref, do_ref, lse_ref, dl_ref = refs[:6]
        if has_next:
            qn_ref, don_ref, lsen_ref, dln_ref = refs[6:10]
            dq_ref, dk_ref, dv_ref, carry = refs[10:]
        else:
            dq_ref, dk_ref, dv_ref = refs[6:]
        j = pl.program_id(2)
        iq = lax.broadcasted_iota(jnp.int32, (blk, blk), 0)
        jk = lax.broadcasted_iota(jnp.int32, (blk, blk), 1)
        low = lax.broadcasted_iota(jnp.int32, (blk, 2 * hd), 1) < hd

        def pair(hp, qr, dor, lser, dlr, steps, valid):
            sl = slice(2 * hd * hp, 2 * hd * (hp + 1))
            q2, do2, k2, v2 = qr[:, sl], dor[:, sl], k_ref[:, sl], v_ref[:, sl]
            dist = steps.astype(F32) * float(dil)
            dq_h, dk2, dv2 = [], None, None
            for half in range(2):
                col = 2 * hd * hp + hd * half
                mask = low if half == 0 else jnp.logical_not(low)
                qh = jnp.where(mask, q2, jnp.zeros_like(q2))
                doh = jnp.where(mask, do2, jnp.zeros_like(do2))
                sc = _dot_nt(qh, k2) - _alibi_slope(2 * hp + half, dims.n_heads) * dist
                p = jnp.where(valid, jnp.exp(sc - lser[:, col:col + 1]), 0.0)
                ds = p * (_dot_nt(doh, v2) - dlr[:, col:col + 1])
                ds_b, p_b = ds.astype(BF16), p.astype(BF16)
                dq_h.append(jnp.dot(ds_b, k2, preferred_element_type=F32))
                dk_h, dv_h = _dot_tn(ds_b, qh), _dot_tn(p_b, doh)
                dk2 = dk_h if dk2 is None else dk2 + dk_h
                dv2 = dv_h if dv2 is None else dv2 + dv_h
            return sl, jnp.where(low, dq_h[0], dq_h[1]), dk2, dv2

        if has_next:
            @pl.when(j == 0)
            def _():
                carry[...] = jnp.zeros_like(carry)

        for hp in range(dims.n_heads // 2):
            sl, dq2, dk2, dv2 = pair(hp, q_ref, do_ref, lse_ref, dl_ref, iq - jk, iq >= jk)
            dq_ref[:, sl] = (carry[:, sl] + dq2) if has_next else dq2
            dk_ref[:, sl] = dk2
            dv_ref[:, sl] = dv2

        if has_next:
            @pl.when(j + 1 < nb)
            def _():
                for hp in range(dims.n_heads // 2):
                    sl, dq2, dk2, dv2 = pair(hp, qn_ref, don_ref, lsen_ref, dln_ref, iq - jk + blk, jk >= iq)
                    carry[:, sl] = dq2
                    dk_ref[:, sl] += dk2
                    dv_ref[:, sl] += dv2

    cur = pl.BlockSpec((None, blk, a), lambda b, r, j: (b, j, r))
    nxt = pl.BlockSpec((None, blk, a), lambda b, r, j: (b, jnp.minimum(j + 1, nb - 1), r))
    args, in_specs = [k, v, q, do, lse, delta], [cur] * 6
    if has_next:
        args += [q, do, lse, delta]
        in_specs += [nxt] * 4
    shape = jax.ShapeDtypeStruct((dims.batch_local, dims.seq // dil, dil * a), F32)
    outs = pl.pallas_call(
        body, name=name, grid=(dims.batch_local, dil, nb),
        in_specs=in_specs, out_specs=[cur] * 3, out_shape=[shape] * 3,
        scratch_shapes=[pltpu.VMEM((blk, a), F32)] if has_next else [],
        compiler_params=_params("parallel", "parallel", "arbitrary"),
    )(*[_attn_view(x, dims, dil) for x in args])
    return tuple(o.reshape(t, a) for o in outs)


def _mix_fwd(ya, yb, z, gate_b, dims, *, name, tr=512):
    t, d = ya.shape
    tr = _pick(t, tr, 8)
    first_gate_col = z.shape[1] // d - 2

    def body(ya_ref, yb_ref, ga_ref, gb_ref, ba_ref, bb_ref, o_ref):
        g_a = _sigmoid(ga_ref[...] + ba_ref[...])
        g_b = _sigmoid(gb_ref[...] + bb_ref[...])
        o_ref[...] = (g_a * ya_ref[...] + g_b * yb_ref[...]).astype(BF16)

    return pl.pallas_call(
        body, name=name, grid=(t // tr,),
        in_specs=[_row_spec(tr, d), _row_spec(tr, d), _row_spec(tr, d, first_gate_col),
                  _row_spec(tr, d, first_gate_col + 1), _vec_spec(d, 0), _vec_spec(d, 1)],
        out_specs=_row_spec(tr, d), out_shape=jax.ShapeDtypeStruct((t, d), BF16),
        compiler_params=_params("parallel"),
    )(ya, yb, z, z, gate_b, gate_b)


def _mix_bwd(dmix, ya, yb, z, gate_b, dims, *, name, tr=512):
    t, d = ya.shape
    tr = _pick(t, tr, 8)
    first_gate_col = z.shape[1] // d - 2

    def body(dm_ref, ya_ref, yb_ref, ga_ref, gb_ref, ba_ref, bb_ref, dya_ref, dyb_ref, dz_ref, db_ref):
        dm = dm_ref[...].astype(F32)
        g_a = _sigmoid(ga_ref[...] + ba_ref[...])
        g_b = _sigmoid(gb_ref[...] + bb_ref[...])
        dya_ref[...] = (dm * g_a).astype(BF16)
        dyb_ref[...] = (dm * g_b).astype(BF16)
        dl_a = dm * ya_ref[...] * g_a * (1.0 - g_a)
        dl_b = dm * yb_ref[...] * g_b * (1.0 - g_b)
        dz_ref[:, 0:d] = dl_a.astype(BF16)
        dz_ref[:, d:2 * d] = dl_b.astype(BF16)
        first = pl.program_id(0) == 0
        sums = jnp.concatenate([jnp.sum(dl_a, axis=0, keepdims=True), jnp.sum(dl_b, axis=0, keepdims=True)], axis=1)
        _accumulate(db_ref, sums, first)

    return pl.pallas_call(
        body, name=name, grid=(t // tr,),
        in_specs=[_row_spec(tr, d), _row_spec(tr, d), _row_spec(tr, d), _row_spec(tr, d, first_gate_col),
                  _row_spec(tr, d, first_gate_col + 1), _vec_spec(d, 0), _vec_spec(d, 1)],
        out_specs=[_row_spec(tr, d), _row_spec(tr, d), _row_spec(tr, 2 * d), _vec_spec(2 * d)],
        out_shape=[jax.ShapeDtypeStruct((t, d), BF16)] * 2 + [jax.ShapeDtypeStruct((t, 2 * d), BF16),
                                                              jax.ShapeDtypeStruct((1, 2 * d), F32)],
        compiler_params=_params("arbitrary"),
    )(dmix, ya, yb, z, z, gate_b, gate_b)


def _loss_head(y, target, *, name, tr=512):
    t, d = y.shape
    tr = _pick(t, tr, 8)

    def body(y_ref, t_ref, dy_ref, dyb_ref, loss_ref):
        err = y_ref[...] - t_ref[...]
        dy = err * (1.0 / d)
        dy_ref[...] = dy
        dyb_ref[...] = dy.astype(BF16)
        part = jnp.sum(jnp.sum(err * err, axis=-1, keepdims=True), axis=0, keepdims=True) * (0.5 / d)
        _accumulate(loss_ref, jnp.broadcast_to(part, (8, 128)), pl.program_id(0) == 0)

    return pl.pallas_call(
        body, name=name, grid=(t // tr,),
        in_specs=[_row_spec(tr, d), _row_spec(tr, d)],
        out_specs=[_row_spec(tr, d), _row_spec(tr, d), pl.BlockSpec((8, 128), lambda i: (0, 0))],
        out_shape=[jax.ShapeDtypeStruct((t, d), F32), jax.ShapeDtypeStruct((t, d), BF16),
                   jax.ShapeDtypeStruct((8, 128), F32)],
        compiler_params=_params("arbitrary"),
    )(y, target)


def _adamw(w, grads, m, v, *, name, tr=256):
    r, c = w.shape
    tr = _pick(r, tr, 8)
    ng = len(grads)
    c1 = 1.0 - ADAM_B1 ** ADAM_STEP
    c2 = 1.0 - ADAM_B2 ** ADAM_STEP

    def body(*refs):
        w_ref, g_refs, m_ref, v_ref = refs[0], refs[1:1 + ng], refs[1 + ng], refs[2 + ng]
        g_out, d_out, m_out, v_out = refs[3 + ng:]
        g = g_refs[0][...]
        for extra in g_refs[1:]:
            g = g + extra[...]
        m_new = ADAM_B1 * m_ref[...] + (1.0 - ADAM_B1) * g
        v_new = ADAM_B2 * v_ref[...] + (1.0 - ADAM_B2) * (g * g)
        g_out[...] = g
        m_out[...] = m_new
        v_out[...] = v_new
        d_out[...] = -ADAM_LR * ((m_new / c1) / (jnp.sqrt(v_new / c2) + ADAM_EPS) + ADAM_WD * w_ref[...])

    spec = pl.BlockSpec((tr, c), lambda i: (i, 0))
    return pl.pallas_call(
        body, name=name, grid=(r // tr,),
        in_specs=[spec] * (3 + ng), out_specs=[spec] * 4, out_shape=[jax.ShapeDtypeStruct((r, c), F32)] * 4,
        compiler_params=_params("parallel"),
    )(w, *grads, m, v)


ANY = pl.BlockSpec(memory_space=pl.ANY)
CHIP_PEERS = ((1, 0), (0, 1), (1, 1))


def _place():
    return lax.axis_index("x"), lax.axis_index("y"), lax.axis_index("c")


def _cast_shards(shards, *, name):
    n = len(shards)

    def body(*refs):
        for src, dst in zip(refs[:n], refs[n:]):
            dst[...] = src[...].astype(BF16)

    return pl.pallas_call(
        body, name=name, out_shape=[jax.ShapeDtypeStruct(s.shape, BF16) for s in shards],
        compiler_params=pltpu.CompilerParams(vmem_limit_bytes=V7X_VMEM_LIMIT_BYTES),
    )(*shards)


def _gather_shards(shards, *, name):
    n = len(shards)

    def body(*refs):
        ins, outs = refs[:n], refs[n:2 * n]
        send, recv, local = refs[2 * n:]
        x, y, c = _place()
        me = 2 * x + y
        started = []
        for a in range(n):
            cp = pltpu.make_async_copy(ins[a], outs[a].at[me], local.at[a])
            cp.start()
            started.append(cp)
            for p, (fx, fy) in enumerate(CHIP_PEERS):
                rc = pltpu.make_async_remote_copy(
                    src_ref=ins[a], dst_ref=outs[a].at[me], send_sem=send.at[a, p], recv_sem=recv.at[a, p],
                    device_id=(x ^ fx, y ^ fy, c), device_id_type=MESH)
                rc.start()
                started.append(rc)
        for cp in started:
            cp.wait()

    return pl.pallas_call(
        body, name=name, in_specs=[ANY] * n, out_specs=[ANY] * n,
        out_shape=[jax.ShapeDtypeStruct((N_CHIPS,) + s.shape, s.dtype) for s in shards],
        scratch_shapes=[pltpu.SemaphoreType.DMA((n, 3)), pltpu.SemaphoreType.DMA((n, 3)), pltpu.SemaphoreType.DMA((n,))],
    )(*shards)


def _scatter_grads(grads, *, name):
    n = len(grads)

    def body(*refs):
        ins, outs = refs[:n], refs[n:2 * n]
        send, recv, local = refs[2 * n:]
        x, y, c = _place()
        started = []
        for a in range(n):
            cp = pltpu.make_async_copy(ins[a].at[2 * x + y], outs[a].at[3], local.at[a])
            cp.start()
            started.append(cp)
            for p, (fx, fy) in enumerate(CHIP_PEERS):
                px, py = x ^ fx, y ^ fy
                rc = pltpu.make_async_remote_copy(
                    src_ref=ins[a].at[2 * px + py], dst_ref=outs[a].at[p], send_sem=send.at[a, p],
                    recv_sem=recv.at[a, p], device_id=(px, py, c), device_id_type=MESH)
                rc.start()
                started.append(rc)
        for cp in started:
            cp.wait()

    return pl.pallas_call(
        body, name=name, in_specs=[ANY] * n, out_specs=[ANY] * n,
        out_shape=[jax.ShapeDtypeStruct(g.shape, g.dtype) for g in grads],
        scratch_shapes=[pltpu.SemaphoreType.DMA((n, 3)), pltpu.SemaphoreType.DMA((n, 3)), pltpu.SemaphoreType.DMA((n,))],
    )(*grads)


def _swap_sibling(arrays, *, name):
    n = len(arrays)

    def body(*refs):
        ins, outs = refs[:n], refs[n:2 * n]
        send, recv = refs[2 * n:]
        x, y, c = _place()
        started = []
        for a in range(n):
            rc = pltpu.make_async_remote_copy(
                src_ref=ins[a], dst_ref=outs[a], send_sem=send.at[a], recv_sem=recv.at[a],
                device_id=(x, y, 1 - c), device_id_type=MESH)
            rc.start()
            started.append(rc)
        for rc in started:
            rc.wait()

    return pl.pallas_call(
        body, name=name, in_specs=[ANY] * n, out_specs=[ANY] * n,
        out_shape=[jax.ShapeDtypeStruct(g.shape, g.dtype) for g in arrays],
        scratch_shapes=[pltpu.SemaphoreType.DMA((n,)), pltpu.SemaphoreType.DMA((n,))],
    )(*arrays)


def _sum_blocks(g, *, name, tr=256):
    nb, r, c = g.shape
    tr = _pick(r, tr, 8)

    def body(g_ref, o_ref):
        total = g_ref[0]
        for s in range(1, nb):
            total = total + g_ref[s]
        o_ref[...] = total

    return pl.pallas_call(
        body, name=name, grid=(r // tr,),
        in_specs=[pl.BlockSpec((nb, tr, c), lambda i: (0, i, 0))], out_specs=pl.BlockSpec((tr, c), lambda i: (i, 0)),
        out_shape=jax.ShapeDtypeStruct((r, c), F32), compiler_params=_params("parallel"),
    )(g)


def _allreduce_small(packed, *, name):
    r, d = packed.shape
    n_dev = 8

    def body(src_ref, out_ref, buf, send, recv):
        x, y, c = _place()
        me = 4 * x + 2 * y + c
        started = []
        for p in range(1, n_dev):
            rc = pltpu.make_async_remote_copy(
                src_ref=src_ref, dst_ref=buf.at[me], send_sem=send.at[p - 1], recv_sem=recv.at[p - 1],
                device_id=(x ^ (p >> 2), y ^ ((p >> 1) & 1), c ^ (p & 1)), device_id_type=MESH)
            rc.start()
            started.append(rc)
        buf[me] = src_ref[...]
        for rc in started:
            rc.wait()
        total = buf[0]
        for s in range(1, n_dev):
            total = total + buf[s]
        out_ref[...] = total

    vmem = pl.BlockSpec(memory_space=pltpu.VMEM)
    return pl.pallas_call(
        body, name=name, in_specs=[vmem], out_specs=vmem, out_shape=jax.ShapeDtypeStruct((r, d), F32),
        scratch_shapes=[pltpu.VMEM((n_dev, r, d), F32), pltpu.SemaphoreType.DMA((n_dev - 1,)),
                        pltpu.SemaphoreType.DMA((n_dev - 1,))],
    )(packed)


def _pack_rows(arrays, d):
    rows = []
    for arr in arrays:
        flat = arr.reshape(-1).astype(F32)
        n = -(-flat.shape[0] // d)
        rows.append(jnp.pad(flat, (0, n * d - flat.shape[0])).reshape(n, d))
    total = sum(r.shape[0] for r in rows)
    pad = -total % 8
    if pad:
        rows.append(jnp.zeros((pad, d), F32))
    return jnp.concatenate(rows, axis=0)


def _unpack_rows(packed, shapes, d):
    out, row = [], 0
    for shape in shapes:
        size = math.prod(shape)
        n = -(-size // d)
        out.append(packed[row:row + n].reshape(-1)[:size].reshape(shape))
        row += n
    return out


SMALL = ("norm1_g", "gate_b", "conv_b", "conv_norm_g", "q_norm_g", "k_norm_g", "norm2_g", "ffn_conv_b")
LARGE = ("w_in", "w_conv_out", "w_attn_out", "w_out", "w_up", "w_down")
WEIGHTS = ("norm1_g", "w_in", "gate_b", "conv_w", "conv_b", "conv_norm_g", "w_conv_out", "q_norm_g", "k_norm_g",
           "w_attn_out", "w_out", "norm2_g", "w_up", "ffn_conv_w", "ffn_conv_b", "w_down")


def _head_ones(dims):
    a = dims.n_heads * dims.head_dim
    head = jnp.arange(a, dtype=jnp.int32) // dims.head_dim
    return (head[:, None] == head[None, :]).astype(BF16)


def _local_step(dims, x, target, full, small):
    d, f, heads = dims.d_model, dims.d_ff, dims.n_heads
    row = lambda name: small[name].reshape(1, -1)
    ones = _head_ones(dims)
    gq = jnp.tile(row("q_norm_g"), (1, heads))
    gk = jnp.tile(row("k_norm_g"), (1, heads))
    conv_w = jnp.pad(full["conv_w"], ((0, CONV_HALO - dims.conv_width), (0, 0)))
    ffn_w = jnp.pad(full["ffn_conv_w"], ((0, FFN_HALO - dims.ffn_conv_width), (0, 0)))
    one_shard = lambda w: w.reshape(1, -1, w.shape[-1])
    w_in, w_up = full["w_in"], full["w_up"]
    w_co, w_ao, w_o, w_dn = (one_shard(full[k]) for k in ("w_conv_out", "w_attn_out", "w_out", "w_down"))

    h = _rmsnorm_fwd(x, row("norm1_g"), name="norm1")
    z = _mm_nn(h, w_in, out_dtype=F32, name="in_proj")
    a1, a3 = _conv_branch_fwd(z, conv_w, row("conv_b"), row("conv_norm_g"), dims, name="conv_branch")
    ya = _mm_nn(a3, w_co, out_dtype=F32, name="conv_out_proj")
    qn, kn, vb = _qkv_fwd(z, gq, gk, ones, dims, name="qk_norm")
    state = None
    for gi, dil in enumerate(DILATIONS):
        state = _attn_fwd_group(qn, kn, vb, state, dims, dil, last=gi == len(DILATIONS) - 1, name=f"attn_fwd_d{dil}")
    o, lse = state
    yb = _mm_nn(o, w_ao, out_dtype=F32, name="attn_out_proj")
    mixed = _mix_fwd(ya, yb, z, row("gate_b"), dims, name="gate_mix")
    x1 = _mm_nn(mixed, w_o, out_dtype=F32, residual=x, name="out_proj")
    h2 = _rmsnorm_fwd(x1, row("norm2_g"), name="norm2")
    up = _mm_nn(h2, w_up, out_dtype=F32, name="up_proj")
    act = _ffn_act_fwd(up, ffn_w, row("ffn_conv_b"), dims, name="ffn_act")
    x2 = _mm_nn(act, w_dn, out_dtype=F32, residual=x1, name="down_proj")
    dy, dy_b, loss = _loss_head(x2, target, name="loss_head")

    grads = {}
    grads["w_down"] = _mm_tn(act, dy_b, n_shards=1, name="dw_down")
    dact = _mm_nt(dy_b, w_dn, out_dtype=BF16, name="d_act")
    du, dfw, dfb = _ffn_act_bwd(dact, up, ffn_w, row("ffn_conv_b"), dims, name="ffn_act_bwd")
    grads["ffn_conv_w"], grads["ffn_conv_b"] = dfw[:dims.ffn_conv_width], dfb
    dup = _ffn_conv_bwd(du, ffn_w, dims, name="ffn_conv_bwd")
    grads["w_up"] = _mm_tn(h2, dup, n_shards=N_CHIPS, name="dw_up")
    dh2 = _mm_nt(dup, w_up, out_dtype=F32, name="d_h2")
    dx1, dx1_b, grads["norm2_g"] = _rmsnorm_bwd(x1, row("norm2_g"), dh2, dy, want_bf16=True, name="norm2_bwd")
    grads["w_out"] = _mm_tn(mixed, dx1_b, n_shards=1, name="dw_out")
    dmix = _mm_nt(dx1_b, w_o, out_dtype=F32, name="d_mix")
    dya, dyb, dz_gate, grads["gate_b"] = _mix_bwd(dmix, ya, yb, z, row("gate_b"), dims, name="gate_mix_bwd")
    grads["w_attn_out"] = _mm_tn(o, dyb, n_shards=1, name="dw_attn_out")
    do = _mm_nt(dyb, w_ao, out_dtype=BF16, name="d_attn")
    delta = _attn_delta(do, o, ones, dims, name="attn_delta")
    per_group = [_attn_bwd_group(qn, kn, vb, do, lse, delta, dims, dil, name=f"attn_bwd_d{dil}") for dil in DILATIONS]
    dqs, dks, dvs = zip(*per_group)
    dz_qkv, dgq, dgk = _qkv_bwd(z, dqs, dks, dvs, gq, gk, ones, dims, name="qk_norm_bwd")
    grads["q_norm_g"] = dgq.reshape(heads, dims.head_dim).sum(axis=0)
    grads["k_norm_g"] = dgk.reshape(heads, dims.head_dim).sum(axis=0)
    grads["w_conv_out"] = _mm_tn(a3, dya, n_shards=1, name="dw_conv_out")
    da3 = _mm_nt(dya, w_co, out_dtype=F32, name="d_conv_act")
    da1, grads["conv_norm_g"] = _conv_norm_bwd(da3, a1, row("conv_norm_g"), name="conv_norm_bwd")
    dz_glu, dcw, grads["conv_b"] = _conv_branch_bwd(da1, z, conv_w, dims, name="conv_branch_bwd")
    grads["conv_w"] = dcw[:dims.conv_width]
    dz = jnp.concatenate([dz_glu, dz_qkv, dz_gate], axis=1)
    grads["w_in"] = _mm_tn(h, dz, n_shards=N_CHIPS, name="dw_in")
    dh = _mm_nt(dz, w_in, out_dtype=F32, name="d_h")
    dx, grads["norm1_g"] = _rmsnorm_bwd(x, row("norm1_g"), dh, dx1, want_bf16=False, name="norm1_bwd")
    return loss, dx, grads


def _step(dims, x, target, w, m, v):
    d = dims.d_model
    t = dims.tokens
    sq = lambda a: a.reshape(a.shape[1:])
    w2, m2, v2 = ({k: sq(a) for k, a in grp.items()} for grp in (w, m, v))

    casted = _cast_shards([w2[k] for k in LARGE], name="cast_weights")
    conv_pad = jnp.pad(w2["conv_w"], ((0, CONV_HALO - dims.conv_width), (0, 0)))
    ffn_pad = jnp.pad(w2["ffn_conv_w"], ((0, FFN_HALO - dims.ffn_conv_width), (0, 0)))
    gathered = _gather_shards([*casted, conv_pad, ffn_pad], name="gather_weights")
    full = dict(zip(LARGE, gathered[:len(LARGE)]))
    cols = lambda g, rows: jnp.moveaxis(g, 0, 1).reshape(g.shape[1], -1)[:rows]
    full["conv_w"] = cols(gathered[-2], dims.conv_width)
    full["ffn_conv_w"] = cols(gathered[-1], dims.ffn_conv_width)

    loss, dx, grads = _local_step(dims, x.reshape(t, d), target.reshape(t, d), full, {k: w2[k] for k in SMALL})

    blocks = [grads[k].reshape(N_CHIPS, -1, grads[k].shape[-1]) for k in LARGE]
    received = _scatter_grads(blocks, name="scatter_grads")
    mine = [_sum_blocks(g, name=f"sum_{k}") for k, g in zip(LARGE, received)]
    theirs = _swap_sibling(mine, name="swap_sibling")

    small_names = SMALL + ("conv_w", "ffn_conv_w")
    packed = _pack_rows([grads[k] for k in small_names] + [loss[0, 0]], d)
    reduced = _allreduce_small(packed, name="allreduce_small")
    shapes = [grads[k].shape for k in small_names] + [()]
    *small_g, loss_total = _unpack_rows(reduced, shapes, d)
    small_g = dict(zip(small_names, small_g))
    chip = 2 * lax.axis_index("x") + lax.axis_index("y")
    for k in ("conv_w", "ffn_conv_w"):
        width = w2[k].shape[1]
        small_g[k] = lax.dynamic_slice_in_dim(small_g[k], chip * width, width, axis=1)

    out = {}
    for k, a, b in zip(LARGE, mine, theirs):
        out[k] = _adamw(w2[k], [a, b], m2[k], v2[k], name=f"adamw_{k}")
    small_shapes = [w2[k].shape for k in small_names]
    pack = lambda grp: _pack_rows([grp[k] for k in small_names], d)
    results = _adamw(pack(w2), [pack(small_g)], pack(m2), pack(v2), name="adamw_small")
    unpacked = [_unpack_rows(r, small_shapes, d) for r in results]
    for i, k in enumerate(small_names):
        out[k] = tuple(u[i] for u in unpacked)

    lead = lambda a: a.reshape((1,) + a.shape)
    ordered = [[lead(out[k][j].reshape(w2[k].shape)) for k in WEIGHTS] for j in range(4)]
    return (loss_total, dx.reshape(x.shape), *ordered[0], *ordered[1], *ordered[2], *ordered[3])


def kernel(x, norm1_g, w_in, gate_b, conv_w, conv_b, conv_norm_g, w_conv_out, q_norm_g, k_norm_g, w_attn_out, w_out, norm2_g, w_up, ffn_conv_w, ffn_conv_b, w_down, loss_target, m_norm1_g, m_w_in, m_gate_b, m_conv_w, m_conv_b, m_conv_norm_g, m_w_conv_out, m_q_norm_g, m_k_norm_g, m_w_attn_out, m_w_out, m_norm2_g, m_w_up, m_ffn_conv_w, m_ffn_conv_b, m_w_down, v_norm1_g, v_w_in, v_gate_b, v_conv_w, v_conv_b, v_conv_norm_g, v_w_conv_out, v_q_norm_g, v_k_norm_g, v_w_attn_out, v_w_out, v_norm2_g, v_w_up, v_ffn_conv_w, v_ffn_conv_b, v_w_down):
    w = dict(zip(WEIGHTS, (norm1_g, w_in, gate_b, conv_w, conv_b, conv_norm_g, w_conv_out, q_norm_g, k_norm_g,
                           w_attn_out, w_out, norm2_g, w_up, ffn_conv_w, ffn_conv_b, w_down)))
    m = dict(zip(WEIGHTS, (m_norm1_g, m_w_in, m_gate_b, m_conv_w, m_conv_b, m_conv_norm_g, m_w_conv_out, m_q_norm_g,
                           m_k_norm_g, m_w_attn_out, m_w_out, m_norm2_g, m_w_up, m_ffn_conv_w, m_ffn_conv_b, m_w_down)))
    v = dict(zip(WEIGHTS, (v_norm1_g, v_w_in, v_gate_b, v_conv_w, v_conv_b, v_conv_norm_g, v_w_conv_out, v_q_norm_g,
                           v_k_norm_g, v_w_attn_out, v_w_out, v_norm2_g, v_w_up, v_ffn_conv_w, v_ffn_conv_b, v_w_down)))
    dims = Dims(d_model=x.shape[-1], batch_local=x.shape[0], seq=x.shape[1], d_ff=w_down.shape[1] * N_CHIPS)
    return _step(dims, x, loss_target, w, m, v)
```

```python
import functools
import math
from typing import NamedTuple

import jax
import jax.numpy as jnp
from jax import lax
from jax.experimental import pallas as pl
from jax.experimental.pallas import tpu as pltpu

F32 = jnp.float32
BF16 = jnp.bfloat16

RMS_EPS = 1e-6
MASKED_SCORE = -1e30
ATTN_BLOCK = 128
DILATIONS = (1, 4, 16)
CONV_HALO = 32
FFN_HALO = 8
ADAM_LR, ADAM_B1, ADAM_B2, ADAM_EPS, ADAM_WD, ADAM_STEP = 0.001, 0.9, 0.999, 1e-08, 0.01, 10
V7X_VMEM_LIMIT_BYTES = 56 * 2 ** 20
N_CHIPS = 4
MESH = pl.DeviceIdType.MESH


class Dims(NamedTuple):
    d_model: int = 1024
    n_heads: int = 16
    head_dim: int = 64
    d_ff: int = 2816
    seq: int = 2048
    batch_local: int = 2
    conv_width: int = 31
    ffn_conv_width: int = 3

    @property
    def tokens(self):
        return self.seq * self.batch_local


def _params(*semantics):
    return pltpu.CompilerParams(dimension_semantics=semantics, vmem_limit_bytes=V7X_VMEM_LIMIT_BYTES)


def _pick(n, target, mult=128):
    if n <= target:
        return n
    best = None
    for t in range(mult, target + 1, mult):
        if n % t == 0:
            best = t
    assert best is not None, (n, target, mult)
    return best


def _sigmoid(v):
    return 1.0 / (1.0 + jnp.exp(-v))


def _mm_nn(a, w, *, out_dtype, name, residual=None, tm=1024, tn=1408, tk=2816):
    m, k = a.shape
    nsh, k2, c = w.shape
    assert k == k2 and a.dtype == BF16 and w.dtype == BF16
    n = nsh * c
    tm, tn, tk = _pick(m, tm, 8), _pick(c, tn), _pick(k, tk)
    nk, cpn = k // tk, c // tn

    def body(*refs):
        if residual is None:
            a_ref, w_ref, o_ref, acc = refs
        else:
            a_ref, w_ref, r_ref, o_ref, acc = refs
        prod = jnp.dot(a_ref[...], w_ref[...], preferred_element_type=F32)

        def finish(total):
            if residual is not None:
                total = total + r_ref[...]
            o_ref[...] = total.astype(out_dtype)

        if nk == 1:
            finish(prod)
        else:
            kk = pl.program_id(2)

            @pl.when(kk == 0)
            def _():
                acc[...] = prod

            @pl.when(kk > 0)
            def _():
                acc[...] += prod

            @pl.when(kk == nk - 1)
            def _():
                finish(acc[...])

    in_specs = [pl.BlockSpec((tm, tk), lambda i, j, kk: (i, kk)),
                pl.BlockSpec((None, tk, tn), lambda i, j, kk: (j // cpn, kk, j % cpn))]
    args = [a, w]
    if residual is not None:
        in_specs.append(pl.BlockSpec((tm, tn), lambda i, j, kk: (i, j)))
        args.append(residual)
    return pl.pallas_call(
        body, name=name, grid=(m // tm, n // tn, nk),
        in_specs=in_specs, out_specs=pl.BlockSpec((tm, tn), lambda i, j, kk: (i, j)),
        out_shape=jax.ShapeDtypeStruct((m, n), out_dtype),
        scratch_shapes=[pltpu.VMEM((tm, tn) if nk > 1 else (8, 128), F32)],
        compiler_params=_params("parallel", "parallel", "arbitrary"),
    )(*args)


def _mm_nt(a, w, *, out_dtype, name, tm=1024, tn=1408, tk=1792):
    m, k = a.shape
    nsh, r, c = w.shape
    assert k == nsh * c and a.dtype == BF16 and w.dtype == BF16
    tm, tn, tk = _pick(m, tm, 8), _pick(r, tn), _pick(c, tk)
    nk, cpk = k // tk, c // tk

    def body(a_ref, w_ref, o_ref, acc):
        prod = lax.dot_general(a_ref[...], w_ref[...], (((1,), (1,)), ((), ())), preferred_element_type=F32)
        if nk == 1:
            o_ref[...] = prod.astype(out_dtype)
        else:
            kk = pl.program_id(2)

            @pl.when(kk == 0)
            def _():
                acc[...] = prod

            @pl.when(kk > 0)
            def _():
                acc[...] += prod

            @pl.when(kk == nk - 1)
            def _():
                o_ref[...] = acc[...].astype(out_dtype)

    return pl.pallas_call(
        body, name=name, grid=(m // tm, r // tn, nk),
        in_specs=[pl.BlockSpec((tm, tk), lambda i, j, kk: (i, kk)),
                  pl.BlockSpec((None, tn, tk), lambda i, j, kk: (kk // cpk, j, kk % cpk))],
        out_specs=pl.BlockSpec((tm, tn), lambda i, j, kk: (i, j)),
        out_shape=jax.ShapeDtypeStruct((m, r), out_dtype),
        scratch_shapes=[pltpu.VMEM((tm, tn) if nk > 1 else (8, 128), F32)],
        compiler_params=_params("parallel", "parallel", "arbitrary"),
    )(a, w)


def _mm_tn(a, b, *, n_shards, name, tm=1408, tn=1408, tk=512):
    t, m = a.shape
    t2, n = b.shape
    assert t == t2 and a.dtype == BF16 and b.dtype == BF16
    c = n // n_shards
    tm, tn, tk = _pick(m, tm), _pick(c, tn), _pick(t, tk, 8)
    nk, cpn = t // tk, c // tn

    def body(a_ref, b_ref, o_ref, acc):
        kk = pl.program_id(2)
        prod = lax.dot_general(a_ref[...], b_ref[...], (((0,), (0,)), ((), ())), preferred_element_type=F32)

        @pl.when(kk == 0)
        def _():
            acc[...] = prod

        @pl.when(kk > 0)
        def _():
            acc[...] += prod

        @pl.when(kk == nk - 1)
        def _():
            o_ref[...] = acc[...]

    return pl.pallas_call(
        body, name=name, grid=(m // tm, n // tn, nk),
        in_specs=[pl.BlockSpec((tk, tm), lambda i, j, kk: (kk, i)),
                  pl.BlockSpec((tk, tn), lambda i, j, kk: (kk, j))],
        out_specs=pl.BlockSpec((None, tm, tn), lambda i, j, kk: (j // cpn, i, j % cpn)),
        out_shape=jax.ShapeDtypeStruct((n_shards, m, c), F32),
        scratch_shapes=[pltpu.VMEM((tm, tn), F32)],
        compiler_params=_params("parallel", "parallel", "arbitrary"),
    )(a, b)


def _row_spec(tr, width, col=0):
    return pl.BlockSpec((tr, width), lambda i, col=col: (i, col))


def _vec_spec(width, col=0):
    return pl.BlockSpec((1, width), lambda i, col=col: (0, col))


def _accumulate(ref, value, first):
    @pl.when(first)
    def _():
        ref[...] = value

    @pl.when(jnp.logical_not(first))
    def _():
        ref[...] += value


def _rmsnorm_fwd(x, g, *, name, tr=512):
    t, d = x.shape
    tr = _pick(t, tr, 8)

    def body(x_ref, g_ref, o_ref):
        xv = x_ref[...]
        r = lax.rsqrt(jnp.mean(xv * xv, axis=-1, keepdims=True) + RMS_EPS)
        o_ref[...] = (xv * r * g_ref[...]).astype(BF16)

    return pl.pallas_call(
        body, name=name, grid=(t // tr,),
        in_specs=[_row_spec(tr, d), _vec_spec(d)], out_specs=_row_spec(tr, d),
        out_shape=jax.ShapeDtypeStruct((t, d), BF16), compiler_params=_params("parallel"),
    )(x, g)


def _rmsnorm_bwd(x, g, dy, dres, *, name, want_bf16, tr=512):
    t, d = x.shape
    tr = _pick(t, tr, 8)

    def body(x_ref, g_ref, dy_ref, dres_ref, *outs):
        dx_ref, dg_ref = outs[0], outs[-1]
        xv, dyv = x_ref[...], dy_ref[...].astype(F32)
        r = lax.rsqrt(jnp.mean(xv * xv, axis=-1, keepdims=True) + RMS_EPS)
        gy = dyv * g_ref[...]
        dx = dres_ref[...] + r * gy - xv * (r * r * r) * jnp.mean(xv * gy, axis=-1, keepdims=True)
        dx_ref[...] = dx
        if want_bf16:
            outs[1][...] = dx.astype(BF16)
        _accumulate(dg_ref, jnp.sum(dyv * xv * r, axis=0, keepdims=True), pl.program_id(0) == 0)

    out_shape = [jax.ShapeDtypeStruct((t, d), F32)]
    out_specs = [_row_spec(tr, d)]
    if want_bf16:
        out_shape.append(jax.ShapeDtypeStruct((t, d), BF16))
        out_specs.append(_row_spec(tr, d))
    out_shape.append(jax.ShapeDtypeStruct((1, d), F32))
    out_specs.append(_vec_spec(d))
    return pl.pallas_call(
        body, name=name, grid=(t // tr,),
        in_specs=[_row_spec(tr, d), _vec_spec(d), _row_spec(tr, d), _row_spec(tr, d)],
        out_specs=out_specs, out_shape=out_shape, compiler_params=_params("arbitrary"),
    )(x, g, dy, dres)


def _head_mean(v, ones_ref, head_dim):
    hi = v.astype(BF16)
    lo = (v - hi.astype(F32)).astype(BF16)
    e = ones_ref[...]
    total = jnp.dot(hi, e, preferred_element_type=F32) + jnp.dot(lo, e, preferred_element_type=F32)
    return total * (1.0 / head_dim)


def _qkv_fwd(z, gq, gk, head_ones, dims, *, name, tr=256):
    t = z.shape[0]
    a = dims.n_heads * dims.head_dim
    tr = _pick(t, tr, 8)
    q_scale = dims.head_dim ** -0.5

    def body(q_ref, k_ref, v_ref, gq_ref, gk_ref, e_ref, qo_ref, ko_ref, vo_ref):
        qv, kv = q_ref[...], k_ref[...]
        rq = lax.rsqrt(_head_mean(qv * qv, e_ref, dims.head_dim) + RMS_EPS)
        rk = lax.rsqrt(_head_mean(kv * kv, e_ref, dims.head_dim) + RMS_EPS)
        qo_ref[...] = (qv * rq * gq_ref[...] * q_scale).astype(BF16)
        ko_ref[...] = (kv * rk * gk_ref[...]).astype(BF16)
        vo_ref[...] = v_ref[...].astype(BF16)

    return pl.pallas_call(
        body, name=name, grid=(t // tr,),
        in_specs=[_row_spec(tr, a, 2), _row_spec(tr, a, 3), _row_spec(tr, a, 4), _vec_spec(a), _vec_spec(a),
                  pl.BlockSpec((a, a), lambda i: (0, 0))],
        out_specs=[_row_spec(tr, a)] * 3, out_shape=[jax.ShapeDtypeStruct((t, a), BF16)] * 3,
        compiler_params=_params("parallel"),
    )(z, z, z, gq, gk, head_ones)


def _qkv_bwd(z, dqs, dks, dvs, gq, gk, head_ones, dims, *, name, tr=256):
    t = z.shape[0]
    a = dims.n_heads * dims.head_dim
    tr = _pick(t, tr, 8)
    q_scale = dims.head_dim ** -0.5
    ng = len(dqs)

    def body(*refs):
        q_ref, k_ref = refs[:2]
        dq_refs, dk_refs, dv_refs = refs[2:2 + ng], refs[2 + ng:2 + 2 * ng], refs[2 + 2 * ng:2 + 3 * ng]
        gq_ref, gk_ref, e_ref = refs[2 + 3 * ng:5 + 3 * ng]
        dz_ref, dgq_ref, dgk_ref = refs[5 + 3 * ng:]
        first = pl.program_id(0) == 0

        def norm_bwd(x_ref, d_refs, g_ref, scale, col, dg_ref):
            xv = x_ref[...]
            dy = sum(r[...] for r in d_refs) * scale
            r = lax.rsqrt(_head_mean(xv * xv, e_ref, dims.head_dim) + RMS_EPS)
            gy = dy * g_ref[...]
            dx = r * gy - xv * (r * r * r) * _head_mean(xv * gy, e_ref, dims.head_dim)
            dz_ref[:, col * a:(col + 1) * a] = dx.astype(BF16)
            _accumulate(dg_ref, jnp.sum(dy * xv * r, axis=0, keepdims=True), first)

        norm_bwd(q_ref, dq_refs, gq_ref, q_scale, 0, dgq_ref)
        norm_bwd(k_ref, dk_refs, gk_ref, 1.0, 1, dgk_ref)
        dz_ref[:, 2 * a:3 * a] = sum(r[...] for r in dv_refs).astype(BF16)

    in_specs = ([_row_spec(tr, a, 2), _row_spec(tr, a, 3)] + [_row_spec(tr, a)] * (3 * ng)
                + [_vec_spec(a), _vec_spec(a), pl.BlockSpec((a, a), lambda i: (0, 0))])
    return pl.pallas_call(
        body, name=name, grid=(t // tr,), in_specs=in_specs,
        out_specs=[_row_spec(tr, 3 * a), _vec_spec(a), _vec_spec(a)],
        out_shape=[jax.ShapeDtypeStruct((t, 3 * a), BF16)] + [jax.ShapeDtypeStruct((1, a), F32)] * 2,
        compiler_params=_params("arbitrary"),
    )(z, z, *dqs, *dks, *dvs, gq, gk, head_ones)


CONV_ROWS = 16


def _seq_specs(dims, ts, width, halo, col, *, nxt=False):
    nst, per = dims.seq // ts, ts // halo
    last = dims.tokens // halo - 1
    cur = pl.BlockSpec((ts, width), lambda b, i: (b * nst + i, col))
    if nxt:
        edge = pl.BlockSpec((halo, width), lambda b, i: (jnp.minimum((b * nst + i + 1) * per, last), col))
    else:
        edge = pl.BlockSpec((halo, width), lambda b, i: (jnp.maximum((b * nst + i) * per - 1, 0), col))
    return cur, edge


def _conv_branch_fwd(z, w, b, g, dims, *, name, ts=128):
    t, c, kw = z.shape[0], dims.d_model, dims.conv_width
    base = CONV_HALO - (kw - 1)

    def body(av_ref, hv_ref, ag_ref, hg_ref, w_ref, b_ref, g_ref, a1_ref, a3_ref, buf):
        i = pl.program_id(1)
        buf[CONV_HALO:, :] = av_ref[...] * _sigmoid(ag_ref[...])
        buf[0:CONV_HALO, :] = jnp.where(i > 0, hv_ref[...] * _sigmoid(hg_ref[...]), 0.0)
        for r0 in range(0, ts, CONV_ROWS):
            acc = jnp.broadcast_to(b_ref[...], (CONV_ROWS, c))
            for k in range(kw):
                acc = acc + w_ref[k:k + 1, :] * buf[pl.ds(r0 + base + k, CONV_ROWS), :]
            a1_ref[r0:r0 + CONV_ROWS, :] = acc
            a2 = acc * lax.rsqrt(jnp.mean(acc * acc, axis=-1, keepdims=True) + RMS_EPS) * g_ref[...]
            a3_ref[r0:r0 + CONV_ROWS, :] = (a2 * _sigmoid(a2)).astype(BF16)

    vec = pl.BlockSpec((1, c), lambda b, i: (0, 0))
    out = pl.BlockSpec((ts, c), lambda b, i: (b * (dims.seq // ts) + i, 0))
    return pl.pallas_call(
        body, name=name, grid=(dims.batch_local, dims.seq // ts),
        in_specs=[*_seq_specs(dims, ts, c, CONV_HALO, 0), *_seq_specs(dims, ts, c, CONV_HALO, 1),
                  pl.BlockSpec((CONV_HALO, c), lambda b, i: (0, 0)), vec, vec],
        out_specs=[out, out],
        out_shape=[jax.ShapeDtypeStruct((t, c), F32), jax.ShapeDtypeStruct((t, c), BF16)],
        scratch_shapes=[pltpu.VMEM((CONV_HALO + ts, c), F32)],
        compiler_params=_params("parallel", "parallel"),
    )(z, z, z, z, w, b, g)


def _conv_norm_bwd(da3, a1, g, *, name, tr=256):
    t, c = a1.shape
    tr = _pick(t, tr, 8)

    def body(d_ref, a_ref, g_ref, o_ref, dg_ref):
        a1v, gv = a_ref[...], g_ref[...]
        r = lax.rsqrt(jnp.mean(a1v * a1v, axis=-1, keepdims=True) + RMS_EPS)
        a2 = a1v * r * gv
        sg = _sigmoid(a2)
        da2 = d_ref[...].astype(F32) * sg * (1.0 + a2 * (1.0 - sg))
        gy = da2 * gv
        o_ref[...] = r * gy - a1v * (r * r * r) * jnp.mean(a1v * gy, axis=-1, keepdims=True)
        _accumulate(dg_ref, jnp.sum(da2 * a1v * r, axis=0, keepdims=True), pl.program_id(0) == 0)

    return pl.pallas_call(
        body, name=name, grid=(t // tr,),
        in_specs=[_row_spec(tr, c), _row_spec(tr, c), _vec_spec(c)],
        out_specs=[_row_spec(tr, c), _vec_spec(c)],
        out_shape=[jax.ShapeDtypeStruct((t, c), F32), jax.ShapeDtypeStruct((1, c), F32)],
        compiler_params=_params("arbitrary"),
    )(da3, a1, g)


def _conv_branch_bwd(da1, z, w, dims, *, name, ts=128):
    t, c, kw = z.shape[0], dims.d_model, dims.conv_width
    nst = dims.seq // ts
    base = CONV_HALO - (kw - 1)

    def body(d_ref, dn_ref, av_ref, hv_ref, ag_ref, hg_ref, w_ref, dz_ref, dw_ref, db_ref, abuf, dbuf):
        i = pl.program_id(1)
        first = jnp.logical_and(pl.program_id(0) == 0, i == 0)
        abuf[CONV_HALO:, :] = av_ref[...] * _sigmoid(ag_ref[...])
        abuf[0:CONV_HALO, :] = jnp.where(i > 0, hv_ref[...] * _sigmoid(hg_ref[...]), 0.0)
        d1 = d_ref[...]
        dbuf[0:ts, :] = d1
        dbuf[ts:, :] = jnp.where(i < nst - 1, dn_ref[...], 0.0)

        @pl.when(first)
        def _():
            dw_ref[...] = jnp.zeros_like(dw_ref)
            db_ref[...] = jnp.zeros_like(db_ref)

        db_ref[...] += jnp.sum(d1, axis=0, keepdims=True)
        for k in range(kw):
            dw_ref[k:k + 1, :] += jnp.sum(d1 * abuf[pl.ds(base + k, ts), :], axis=0, keepdims=True)
        for r0 in range(0, ts, CONV_ROWS):
            acc = jnp.zeros((CONV_ROWS, c), F32)
            for k in range(kw):
                acc = acc + w_ref[k:k + 1, :] * dbuf[pl.ds(r0 + (kw - 1) - k, CONV_ROWS), :]
            av = av_ref[r0:r0 + CONV_ROWS, :]
            sg = _sigmoid(ag_ref[r0:r0 + CONV_ROWS, :])
            dz_ref[r0:r0 + CONV_ROWS, 0:c] = (acc * sg).astype(BF16)
            dz_ref[r0:r0 + CONV_ROWS, c:2 * c] = (acc * av * sg * (1.0 - sg)).astype(BF16)

    cur, nxt = _seq_specs(dims, ts, c, CONV_HALO, 0, nxt=True)
    return pl.pallas_call(
        body, name=name, grid=(dims.batch_local, nst),
        in_specs=[cur, nxt, *_seq_specs(dims, ts, c, CONV_HALO, 0), *_seq_specs(dims, ts, c, CONV_HALO, 1),
                  pl.BlockSpec((CONV_HALO, c), lambda b, i: (0, 0))],
        out_specs=[pl.BlockSpec((ts, 2 * c), lambda b, i: (b * nst + i, 0)),
                   pl.BlockSpec((CONV_HALO, c), lambda b, i: (0, 0)), pl.BlockSpec((1, c), lambda b, i: (0, 0))],
        out_shape=[jax.ShapeDtypeStruct((t, 2 * c), BF16), jax.ShapeDtypeStruct((CONV_HALO, c), F32),
                   jax.ShapeDtypeStruct((1, c), F32)],
        scratch_shapes=[pltpu.VMEM((CONV_HALO + ts, c), F32), pltpu.VMEM((ts + CONV_HALO, c), F32)],
        compiler_params=_params("arbitrary", "arbitrary"),
    )(da1, da1, z, z, z, z, w)


FFN_ROWS = 16
FFN_COLS = 256


def _ffn_chunks(ts, f):
    cw = _pick(f, FFN_COLS)
    return [(r0, c0, cw) for r0 in range(0, ts, FFN_ROWS) for c0 in range(0, f, cw)]


def _ffn_conv(buf, w_ref, b_ref, r0, cols, kw):
    base = FFN_HALO - (kw - 1)
    u = jnp.broadcast_to(b_ref[:, cols], (FFN_ROWS, cols.stop - cols.start))
    for k in range(kw):
        u = u + w_ref[k:k + 1, cols] * buf[pl.ds(r0 + base + k, FFN_ROWS), cols]
    return u


def _ffn_act_fwd(up, w, b, dims, *, name, ts=128):
    t, f, kw = up.shape[0], dims.d_ff, dims.ffn_conv_width

    def body(up_ref, h_ref, w_ref, b_ref, o_ref, buf):
        buf[FFN_HALO:, :] = up_ref[...]
        buf[0:FFN_HALO, :] = jnp.where(pl.program_id(1) > 0, h_ref[...], 0.0)
        for r0, c0, cw in _ffn_chunks(ts, f):
            uv = _ffn_conv(buf, w_ref, b_ref, r0, slice(c0, c0 + cw), kw)
            ug = _ffn_conv(buf, w_ref, b_ref, r0, slice(f + c0, f + c0 + cw), kw)
            o_ref[r0:r0 + FFN_ROWS, c0:c0 + cw] = (ug * _sigmoid(ug) * uv).astype(BF16)

    full = lambda rows: pl.BlockSpec((rows, 2 * f), lambda b_, i: (0, 0))
    return pl.pallas_call(
        body, name=name, grid=(dims.batch_local, dims.seq // ts),
        in_specs=[*_seq_specs(dims, ts, 2 * f, FFN_HALO, 0), full(FFN_HALO), full(1)],
        out_specs=pl.BlockSpec((ts, f), lambda b_, i: (b_ * (dims.seq // ts) + i, 0)),
        out_shape=jax.ShapeDtypeStruct((t, f), BF16),
        scratch_shapes=[pltpu.VMEM((FFN_HALO + ts, 2 * f), F32)],
        compiler_params=_params("parallel", "parallel"),
    )(up, up, w, b)


def _ffn_act_bwd(dact, up, w, b, dims, *, name, ts=128):
    t, f, kw = up.shape[0], dims.d_ff, dims.ffn_conv_width
    base = FFN_HALO - (kw - 1)

    def body(d_ref, up_ref, h_ref, w_ref, b_ref, du_ref, dw_ref, db_ref, buf):
        i = pl.program_id(1)
        first = jnp.logical_and(pl.program_id(0) == 0, i == 0)
        buf[FFN_HALO:, :] = up_ref[...]
        buf[0:FFN_HALO, :] = jnp.where(i > 0, h_ref[...], 0.0)
        for r0, c0, cw in _ffn_chunks(ts, f):
            vcols, gcols = slice(c0, c0 + cw), slice(f + c0, f + c0 + cw)
            uv = _ffn_conv(buf, w_ref, b_ref, r0, vcols, kw)
            ug = _ffn_conv(buf, w_ref, b_ref, r0, gcols, kw)
            d = d_ref[r0:r0 + FFN_ROWS, vcols].astype(F32)
            sg = _sigmoid(ug)
            du_ref[r0:r0 + FFN_ROWS, vcols] = d * ug * sg
            du_ref[r0:r0 + FFN_ROWS, gcols] = d * uv * sg * (1.0 + ug * (1.0 - sg))

        @pl.when(first)
        def _():
            dw_ref[...] = jnp.zeros_like(dw_ref)
            db_ref[...] = jnp.zeros_like(db_ref)

        du = du_ref[...]
        db_ref[...] += jnp.sum(du, axis=0, keepdims=True)
        for k in range(kw):
            dw_ref[k:k + 1, :] += jnp.sum(du * buf[pl.ds(base + k, ts), :], axis=0, keepdims=True)

    nst = dims.seq // ts
    full = lambda rows: pl.BlockSpec((rows, 2 * f), lambda b_, i: (0, 0))
    return pl.pallas_call(
        body, name=name, grid=(dims.batch_local, nst),
        in_specs=[pl.BlockSpec((ts, f), lambda b_, i: (b_ * nst + i, 0)),
                  *_seq_specs(dims, ts, 2 * f, FFN_HALO, 0), full(FFN_HALO), full(1)],
        out_specs=[pl.BlockSpec((ts, 2 * f), lambda b_, i: (b_ * nst + i, 0)), full(FFN_HALO), full(1)],
        out_shape=[jax.ShapeDtypeStruct((t, 2 * f), F32), jax.ShapeDtypeStruct((FFN_HALO, 2 * f), F32),
                   jax.ShapeDtypeStruct((1, 2 * f), F32)],
        scratch_shapes=[pltpu.VMEM((FFN_HALO + ts, 2 * f), F32)],
        compiler_params=_params("arbitrary", "arbitrary"),
    )(dact, up, up, w, b)


def _ffn_conv_bwd(du, w, dims, *, name, ts=128):
    t, f2 = du.shape
    kw = dims.ffn_conv_width
    nst = dims.seq // ts

    def body(d_ref, dn_ref, w_ref, o_ref, buf):
        buf[0:ts, :] = d_ref[...]
        buf[ts:, :] = jnp.where(pl.program_id(1) < nst - 1, dn_ref[...], 0.0)
        for r0, c0, cw in _ffn_chunks(ts, f2):
            cols = slice(c0, c0 + cw)
            acc = jnp.zeros((FFN_ROWS, cw), F32)
            for k in range(kw):
                acc = acc + w_ref[k:k + 1, cols] * buf[pl.ds(r0 + (kw - 1) - k, FFN_ROWS), cols]
            o_ref[r0:r0 + FFN_ROWS, cols] = acc.astype(BF16)

    return pl.pallas_call(
        body, name=name, grid=(dims.batch_local, nst),
        in_specs=[*_seq_specs(dims, ts, f2, FFN_HALO, 0, nxt=True), pl.BlockSpec((FFN_HALO, f2), lambda b_, i: (0, 0))],
        out_specs=pl.BlockSpec((ts, f2), lambda b_, i: (b_ * nst + i, 0)),
        out_shape=jax.ShapeDtypeStruct((t, f2), BF16),
        scratch_shapes=[pltpu.VMEM((ts + FFN_HALO, f2), F32)],
        compiler_params=_params("parallel", "parallel"),
    )(du, du, w)


def _alibi_slope(h, n_heads):
    return 2.0 ** (-8.0 * (h + 1) / n_heads)


def _dot_nt(a, b):
    return lax.dot_general(a, b, (((1,), (1,)), ((), ())), preferred_element_type=F32)


def _dot_tn(a, b):
    return lax.dot_general(a, b, (((0,), (0,)), ((), ())), preferred_element_type=F32)


def _attn_view(x, dims, dil):
    return x.reshape(dims.batch_local, dims.seq // dil, dil * x.shape[-1])


def _attn_fwd_group(q, k, v, state, dims, dil, *, last, name):
    t, a = q.shape
    assert 2 * dims.head_dim == 128 and dims.n_heads % 2 == 0
    blk, hd = ATTN_BLOCK, dims.head_dim
    nb = dims.seq // dil // blk
    has_prev = nb > 1
    nkeys = 2 * blk if has_prev else blk

    def body(*refs):
        it = iter(refs)
        q_ref, kc_ref, vc_ref = next(it), next(it), next(it)
        kp_ref, vp_ref = (next(it), next(it)) if has_prev else (None, None)
        m_in, l_in, acc_in = (next(it), next(it), next(it)) if state is not None else (None, None, None)
        outs = list(it)
        iq = lax.broadcasted_iota(jnp.int32, (blk, nkeys), 0)
        jk = lax.broadcasted_iota(jnp.int32, (blk, nkeys), 1)
        if has_prev:
            steps = iq + blk - jk
            valid = (steps >= 0) & (steps <= blk) & ((jk >= blk) | (pl.program_id(2) > 0))
        else:
            steps = iq - jk
            valid = steps >= 0
        dist = steps.astype(F32) * float(dil)
        low = lax.broadcasted_iota(jnp.int32, (blk, 2 * hd), 1) < hd
        for hp in range(dims.n_heads // 2):
            sl = slice(2 * hd * hp, 2 * hd * (hp + 1))
            q2 = q_ref[:, sl]
            if has_prev:
                kcat = jnp.concatenate([kp_ref[:, sl], kc_ref[:, sl]], axis=0)
                vcat = jnp.concatenate([vp_ref[:, sl], vc_ref[:, sl]], axis=0)
            else:
                kcat, vcat = kc_ref[:, sl], vc_ref[:, sl]
            halves = []
            for half in range(2):
                col = 2 * hd * hp + hd * half
                qh = jnp.where(low if half == 0 else jnp.logical_not(low), q2, jnp.zeros_like(q2))
                sc = _dot_nt(qh, kcat) - _alibi_slope(2 * hp + half, dims.n_heads) * dist
                sc = jnp.where(valid, sc, MASKED_SCORE)
                row_max = jnp.max(sc, axis=-1, keepdims=True)
                if state is None:
                    m_new = row_max
                    p = jnp.exp(sc - m_new)
                    alpha = None
                    l_new = jnp.sum(p, axis=-1, keepdims=True)
                else:
                    m_old = m_in[:, col:col + 1]
                    m_new = jnp.maximum(m_old, row_max)
                    p = jnp.exp(sc - m_new)
                    alpha = jnp.exp(m_old - m_new)
                    l_new = alpha * l_in[:, col:col + 1] + jnp.sum(p, axis=-1, keepdims=True)
                pv = jnp.dot(p.astype(BF16), vcat, preferred_element_type=F32)
                halves.append((m_new, l_new, alpha, pv))
            (m_a, l_a, al_a, pv_a), (m_b, l_b, al_b, pv_b) = halves
            if state is None:
                acc = jnp.where(low, pv_a, pv_b)
            else:
                old = acc_in[:, sl]
                acc = jnp.where(low, al_a * old + pv_a, al_b * old + pv_b)
            m2 = jnp.where(low, m_a, m_b)
            l2 = jnp.where(low, l_a, l_b)
            if last:
                outs[0][:, sl] = (acc / l2).astype(BF16)
                outs[1][:, sl] = m2 + jnp.log(l2)
            else:
                outs[0][:, sl] = m2
                outs[1][:, sl] = l2
                outs[2][:, sl] = acc

    cur = pl.BlockSpec((None, blk, a), lambda b, r, i: (b, i, r))
    prev = pl.BlockSpec((None, blk, a), lambda b, r, i: (b, jnp.maximum(i - 1, 0), r))
    args, in_specs = [q, k, v], [cur, cur, cur]
    if has_prev:
        args += [k, v]
        in_specs += [prev, prev]
    if state is not None:
        args += list(state)
        in_specs += [cur] * 3
    shape = lambda dt: jax.ShapeDtypeStruct((dims.batch_local, dims.seq // dil, dil * a), dt)
    out_shape = [shape(BF16), shape(F32)] if last else [shape(F32)] * 3
    outs = pl.pallas_call(
        body, name=name, grid=(dims.batch_local, dil, nb),
        in_specs=in_specs, out_specs=[cur] * len(out_shape), out_shape=out_shape,
        compiler_params=_params("parallel", "parallel", "parallel"),
    )(*[_attn_view(x, dims, dil) for x in args])
    return tuple(o.reshape(t, a) for o in outs)


def _attn_delta(do, o, head_ones, dims, *, name, tr=512):
    t, a = o.shape
    tr = _pick(t, tr, 8)

    def body(do_ref, o_ref, e_ref, d_ref):
        prod = do_ref[...].astype(F32) * o_ref[...].astype(F32)
        d_ref[...] = _head_mean(prod, e_ref, dims.head_dim) * float(dims.head_dim)

    return pl.pallas_call(
        body, name=name, grid=(t // tr,),
        in_specs=[_row_spec(tr, a), _row_spec(tr, a), pl.BlockSpec((a, a), lambda i: (0, 0))],
        out_specs=_row_spec(tr, a), out_shape=jax.ShapeDtypeStruct((t, a), F32),
        compiler_params=_params("parallel"),
    )(do, o, head_ones)


def _attn_bwd_group(q, k, v, do, lse, delta, dims, dil, *, name):
    t, a = q.shape
    blk, hd = ATTN_BLOCK, dims.head_dim
    nb = dims.seq // dil // blk
    has_next = nb > 1

    def body(*refs):
        k_ref, v_ref, q_ref, do_ref, lse_ref, dl_ref = refs[:6]
        if has_next:
            qn_ref, don_ref, lsen_ref, dln_ref = refs[6:10]
            dq_ref, dk_ref, dv_ref, carry = refs[10:]
        else:
            dq_ref, dk_ref, dv_ref = refs[6:]
        j = pl.program_id(2)
        iq = lax.broadcasted_iota(jnp.int32, (blk, blk), 0)
        jk = lax.broadcasted_iota(jnp.int32, (blk, blk), 1)
        low = lax.broadcasted_iota(jnp.int32, (blk, 2 * hd), 1) < hd

        def pair(hp, qr, dor, lser, dlr, steps, valid):
            sl = slice(2 * hd * hp, 2 * hd * (hp + 1))
            q2, do2, k2, v2 = qr[:, sl], dor[:, sl], k_ref[:, sl], v_ref[:, sl]
            dist = steps.astype(F32) * float(dil)
            dq_h, dk2, dv2 = [], None, None
            for half in range(2):
                col = 2 * hd * hp + hd * half
                mask = low if half == 0 else jnp.logical_not(low)
                qh = jnp.where(mask, q2, jnp.zeros_like(q2))
                doh = jnp.where(mask, do2, jnp.zeros_like(do2))
                sc = _dot_nt(qh, k2) - _alibi_slope(2 * hp + half, dims.n_heads) * dist
                p = jnp.where(valid, jnp.exp(sc - lser[:, col:col + 1]), 0.0)
                ds = p * (_dot_nt(doh, v2) - dlr[:, col:col + 1])
                ds_b, p_b = ds.astype(BF16), p.astype(BF16)
                dq_h.append(jnp.dot(ds_b, k2, preferred_element_type=F32))
                dk_h, dv_h = _dot_tn(ds_b, qh), _dot_tn(p_b, doh)
                dk2 = dk_h if dk2 is None else dk2 + dk_h
                dv2 = dv_h if dv2 is None else dv2 + dv_h
            return sl, jnp.where(low, dq_h[0], dq_h[1]), dk2, dv2

        if has_next:
            @pl.when(j == 0)
            def _():
                carry[...] = jnp.zeros_like(carry)

        for hp in range(dims.n_heads // 2):
            sl, dq2, dk2, dv2 = pair(hp, q_ref, do_ref, lse_ref, dl_ref, iq - jk, iq >= jk)
            dq_ref[:, sl] = (carry[:, sl] + dq2) if has_next else dq2
            dk_ref[:, sl] = dk2
            dv_ref[:, sl] = dv2

        if has_next:
            @pl.when(j + 1 < nb)
            def _():
                for hp in range(dims.n_heads // 2):
                    sl, dq2, dk2, dv2 = pair(hp, qn_ref, don_ref, lsen_ref, dln_ref, iq - jk + blk, jk >= iq)
                    carry[:, sl] = dq2
                    dk_ref[:, sl] += dk2
                    dv_ref[:, sl] += dv2

    cur = pl.BlockSpec((None, blk, a), lambda b, r, j: (b, j, r))
    nxt = pl.BlockSpec((None, blk, a), lambda b, r, j: (b, jnp.minimum(j + 1, nb - 1), r))
    args, in_specs = [k, v, q, do, lse, delta], [cur] * 6
    if has_next:
        args += [q, do, lse, delta]
        in_specs += [nxt] * 4
    shape = jax.ShapeDtypeStruct((dims.batch_local, dims.seq // dil, dil * a), F32)
    outs = pl.pallas_call(
        body, name=name, grid=(dims.batch_local, dil, nb),
        in_specs=in_specs, out_specs=[cur] * 3, out_shape=[shape] * 3,
        scratch_shapes=[pltpu.VMEM((blk, a), F32)] if has_next else [],
        compiler_params=_params("parallel", "parallel", "arbitrary"),
    )(*[_attn_view(x, dims, dil) for x in args])
    return tuple(o.reshape(t, a) for o in outs)


def _mix_fwd(ya, yb, z, gate_b, dims, *, name, tr=512):
    t, d = ya.shape
    tr = _pick(t, tr, 8)
    first_gate_col = z.shape[1] // d - 2

    def body(ya_ref, yb_ref, ga_ref, gb_ref, ba_ref, bb_ref, o_ref):
        g_a = _sigmoid(ga_ref[...] + ba_ref[...])
        g_b = _sigmoid(gb_ref[...] + bb_ref[...])
        o_ref[...] = (g_a * ya_ref[...] + g_b * yb_ref[...]).astype(BF16)

    return pl.pallas_call(
        body, name=name, grid=(t // tr,),
        in_specs=[_row_spec(tr, d), _row_spec(tr, d), _row_spec(tr, d, first_gate_col),
                  _row_spec(tr, d, first_gate_col + 1), _vec_spec(d, 0), _vec_spec(d, 1)],
        out_specs=_row_spec(tr, d), out_shape=jax.ShapeDtypeStruct((t, d), BF16),
        compiler_params=_params("parallel"),
    )(ya, yb, z, z, gate_b, gate_b)


def _mix_bwd(dmix, ya, yb, z, gate_b, dims, *, name, tr=512):
    t, d = ya.shape
    tr = _pick(t, tr, 8)
    first_gate_col = z.shape[1] // d - 2

    def body(dm_ref, ya_ref, yb_ref, ga_ref, gb_ref, ba_ref, bb_ref, dya_ref, dyb_ref, dz_ref, db_ref):
        dm = dm_ref[...].astype(F32)
        g_a = _sigmoid(ga_ref[...] + ba_ref[...])
        g_b = _sigmoid(gb_ref[...] + bb_ref[...])
        dya_ref[...] = (dm * g_a).astype(BF16)
        dyb_ref[...] = (dm * g_b).astype(BF16)
        dl_a = dm * ya_ref[...] * g_a * (1.0 - g_a)
        dl_b = dm * yb_ref[...] * g_b * (1.0 - g_b)
        dz_ref[:, 0:d] = dl_a.astype(BF16)
        dz_ref[:, d:2 * d] = dl_b.astype(BF16)
        first = pl.program_id(0) == 0
        sums = jnp.concatenate([jnp.sum(dl_a, axis=0, keepdims=True), jnp.sum(dl_b, axis=0, keepdims=True)], axis=1)
        _accumulate(db_ref, sums, first)

    return pl.pallas_call(
        body, name=name, grid=(t // tr,),
        in_specs=[_row_spec(tr, d), _row_spec(tr, d), _row_spec(tr, d), _row_spec(tr, d, first_gate_col),
                  _row_spec(tr, d, first_gate_col + 1), _vec_spec(d, 0), _vec_spec(d, 1)],
        out_specs=[_row_spec(tr, d), _row_spec(tr, d), _row_spec(tr, 2 * d), _vec_spec(2 * d)],
        out_shape=[jax.ShapeDtypeStruct((t, d), BF16)] * 2 + [jax.ShapeDtypeStruct((t, 2 * d), BF16),
                                                              jax.ShapeDtypeStruct((1, 2 * d), F32)],
        compiler_params=_params("arbitrary"),
    )(dmix, ya, yb, z, z, gate_b, gate_b)


def _loss_head(y, target, *, name, tr=512):
    t, d = y.shape
    tr = _pick(t, tr, 8)

    def body(y_ref, t_ref, dy_ref, dyb_ref, loss_ref):
        err = y_ref[...] - t_ref[...]
        dy = err * (1.0 / d)
        dy_ref[...] = dy
        dyb_ref[...] = dy.astype(BF16)
        part = jnp.sum(jnp.sum(err * err, axis=-1, keepdims=True), axis=0, keepdims=True) * (0.5 / d)
        _accumulate(loss_ref, jnp.broadcast_to(part, (8, 128)), pl.program_id(0) == 0)

    return pl.pallas_call(
        body, name=name, grid=(t // tr,),
        in_specs=[_row_spec(tr, d), _row_spec(tr, d)],
        out_specs=[_row_spec(tr, d), _row_spec(tr, d), pl.BlockSpec((8, 128), lambda i: (0, 0))],
        out_shape=[jax.ShapeDtypeStruct((t, d), F32), jax.ShapeDtypeStruct((t, d), BF16),
                   jax.ShapeDtypeStruct((8, 128), F32)],
        compiler_params=_params("arbitrary"),
    )(y, target)


def _adamw(w, grads, m, v, *, name, tr=256):
    r, c = w.shape
    tr = _pick(r, tr, 8)
    ng = len(grads)
    c1 = 1.0 - ADAM_B1 ** ADAM_STEP
    c2 = 1.0 - ADAM_B2 ** ADAM_STEP

    def body(*refs):
        w_ref, g_refs, m_ref, v_ref = refs[0], refs[1:1 + ng], refs[1 + ng], refs[2 + ng]
        g_out, d_out, m_out, v_out = refs[3 + ng:]
        g = g_refs[0][...]
        for extra in g_refs[1:]:
            g = g + extra[...]
        m_new = ADAM_B1 * m_ref[...] + (1.0 - ADAM_B1) * g
        v_new = ADAM_B2 * v_ref[...] + (1.0 - ADAM_B2) * (g * g)
        g_out[...] = g
        m_out[...] = m_new
        v_out[...] = v_new
        d_out[...] = -ADAM_LR * ((m_new / c1) / (jnp.sqrt(v_new / c2) + ADAM_EPS) + ADAM_WD * w_ref[...])

    spec = pl.BlockSpec((tr, c), lambda i: (i, 0))
    return pl.pallas_call(
        body, name=name, grid=(r // tr,),
        in_specs=[spec] * (3 + ng), out_specs=[spec] * 4, out_shape=[jax.ShapeDtypeStruct((r, c), F32)] * 4,
        compiler_params=_params("parallel"),
    )(w, *grads, m, v)


ANY = pl.BlockSpec(memory_space=pl.ANY)
CHIP_PEERS = ((1, 0), (0, 1), (1, 1))


def _place():
    return lax.axis_index("x"), lax.axis_index("y"), lax.axis_index("c")


HBM = pl.BlockSpec(memory_space=pltpu.HBM)
SEM = pl.BlockSpec(memory_space=pltpu.SEMAPHORE)
IN_FLIGHT = pltpu.SideEffectType.DATAFLOW_SIDE_EFFECTING


def _in_hbm(a):
    return pltpu.with_memory_space_constraint(a, pltpu.HBM)


def _cast_to_lands(shards, dtypes, *, name):
    n = len(shards)

    def body(*refs):
        ins, outs, bufs, sems = refs[:n], refs[n:2 * n], refs[2 * n:3 * n], refs[3 * n]
        x, y, _ = _place()
        copies = []
        for a in range(n):
            bufs[a][...] = ins[a][...].astype(dtypes[a])
            cp = pltpu.make_async_copy(bufs[a], outs[a].at[2 * x + y], sems.at[a])
            cp.start()
            copies.append(cp)
        for cp in copies:
            cp.wait()

    return pl.pallas_call(
        body, name=name, in_specs=[pl.BlockSpec(memory_space=pltpu.VMEM)] * n, out_specs=[ANY] * n,
        out_shape=[jax.ShapeDtypeStruct((N_CHIPS,) + s.shape, dt) for s, dt in zip(shards, dtypes)],
        scratch_shapes=[pltpu.VMEM(s.shape, dt) for s, dt in zip(shards, dtypes)] + [pltpu.SemaphoreType.DMA((n,))],
        compiler_params=pltpu.CompilerParams(vmem_limit_bytes=V7X_VMEM_LIMIT_BYTES),
    )(*shards)


def _chip_copy(src, dst, send, recv, flip, place):
    x, y, c = place
    return pltpu.make_async_remote_copy(src_ref=src, dst_ref=dst, send_sem=send, recv_sem=recv,
                                        device_id=(x ^ flip[0], y ^ flip[1], c), device_id_type=MESH)


def _gather_start(lands, *, name):
    n = len(lands)

    def body(*refs):
        ins, send, recv, token = refs[:n], refs[n], refs[n + 1], refs[-1]
        place = _place()
        me = 2 * place[0] + place[1]
        for a in range(n):
            for p, flip in enumerate(CHIP_PEERS):
                k = 3 * a + p
                _chip_copy(ins[a].at[me], ins[a].at[me], send.at[k], recv.at[k], flip, place).start()
        token[...] = jnp.zeros_like(token)

    outs = pl.pallas_call(
        body, name=name, in_specs=[HBM] * n,
        out_specs=(SEM, SEM, *[HBM] * n, pl.BlockSpec(memory_space=pltpu.VMEM)),
        out_shape=(pltpu.SemaphoreType.DMA((3 * n,)), pltpu.SemaphoreType.DMA((3 * n,)),
                   *[pltpu.HBM(l.shape, l.dtype) for l in lands], jax.ShapeDtypeStruct((8, 128), F32)),
        input_output_aliases={a: 2 + a for a in range(n)},
        compiler_params=pltpu.CompilerParams(has_side_effects=IN_FLIGHT),
    )(*[_in_hbm(l) for l in lands])
    return outs[0], outs[1], list(outs[2:2 + n]), outs[-1]


def _gather_wait(send, recv, lands, after, *, name):
    n = len(lands)

    def body(*refs):
        ins, send_ref, recv_ref = refs[:n], refs[n], refs[n + 1]
        place = _place()
        me = 2 * place[0] + place[1]
        for a in range(n):
            for p, flip in enumerate(CHIP_PEERS):
                k = 3 * a + p
                cp = _chip_copy(ins[a].at[me], ins[a].at[me], send_ref.at[k], recv_ref.at[k], flip, place)
                cp.wait_send()
                cp.wait_recv()

    return pl.pallas_call(
        body, name=name, in_specs=[HBM] * n + [SEM, SEM, ANY], out_specs=[HBM] * n,
        out_shape=[pltpu.HBM(l.shape, l.dtype) for l in lands],
        input_output_aliases={a: a for a in range(n)},
        compiler_params=pltpu.CompilerParams(has_side_effects=IN_FLIGHT),
    )(*lands, send, recv, after)


def _scatter_start(grad, *, name):
    def body(g_ref, land_ref, send, recv, g_thru, land_thru, token):
        place = _place()
        for p, flip in enumerate(CHIP_PEERS):
            peer_chip = 2 * (place[0] ^ flip[0]) + (place[1] ^ flip[1])
            _chip_copy(g_ref.at[peer_chip], land_ref.at[p], send.at[p], recv.at[p], flip, place).start()
        token[...] = jnp.zeros_like(token)

    land = lax.empty((3,) + grad.shape[1:], grad.dtype)
    return pl.pallas_call(
        body, name=name, in_specs=[HBM, HBM],
        out_specs=(SEM, SEM, HBM, HBM, pl.BlockSpec(memory_space=pltpu.VMEM)),
        out_shape=(pltpu.SemaphoreType.DMA((3,)), pltpu.SemaphoreType.DMA((3,)), pltpu.HBM(grad.shape, grad.dtype),
                   pltpu.HBM(land.shape, land.dtype), jax.ShapeDtypeStruct((8, 128), F32)),
        input_output_aliases={0: 2, 1: 3},
        compiler_params=pltpu.CompilerParams(has_side_effects=IN_FLIGHT),
    )(_in_hbm(grad), _in_hbm(land))


def _scatter_wait(started, after, *, name):
    n = len(started)

    def body(*refs):
        grads, lands = refs[:n], refs[n:2 * n]
        sends, recvs = refs[2 * n:3 * n], refs[3 * n:4 * n]
        place = _place()
        for a in range(n):
            for p, flip in enumerate(CHIP_PEERS):
                cp = _chip_copy(grads[a].at[0], lands[a].at[p], sends[a].at[p], recvs[a].at[p], flip, place)
                cp.wait_send()
                cp.wait_recv()

    grads, lands = [s[2] for s in started], [s[3] for s in started]
    outs = pl.pallas_call(
        body, name=name, in_specs=[HBM] * (2 * n) + [SEM] * (2 * n) + [ANY], out_specs=[HBM] * (2 * n),
        out_shape=[pltpu.HBM(a.shape, a.dtype) for a in grads + lands],
        input_output_aliases={a: a for a in range(2 * n)},
        compiler_params=pltpu.CompilerParams(has_side_effects=IN_FLIGHT),
    )(*grads, *lands, *[s[0] for s in started], *[s[1] for s in started], after)
    return list(zip(outs[:n], outs[n:]))


def _swap_sibling(arrays, *, name):
    n = len(arrays)

    def body(*refs):
        ins, outs = refs[:n], refs[n:2 * n]
        send, recv = refs[2 * n:]
        x, y, c = _place()
        started = []
        for a in range(n):
            rc = pltpu.make_async_remote_copy(
                src_ref=ins[a], dst_ref=outs[a], send_sem=send.at[a], recv_sem=recv.at[a],
                device_id=(x, y, 1 - c), device_id_type=MESH)
            rc.start()
            started.append(rc)
        for rc in started:
            rc.wait()

    return pl.pallas_call(
        body, name=name, in_specs=[ANY] * n, out_specs=[ANY] * n,
        out_shape=[jax.ShapeDtypeStruct(g.shape, g.dtype) for g in arrays],
        scratch_shapes=[pltpu.SemaphoreType.DMA((n,)), pltpu.SemaphoreType.DMA((n,))],
    )(*arrays)


def _sum_received(grad, land, *, name, tr=256):
    _, r, c = grad.shape
    tr = _pick(r, tr, 8)

    def body(chip_ref, g_ref, l_ref, o_ref):
        o_ref[...] = ((g_ref[...] + l_ref[0]) + l_ref[1]) + l_ref[2]

    chip = (2 * lax.axis_index("x") + lax.axis_index("y")).astype(jnp.int32).reshape(1)
    return pl.pallas_call(
        body, name=name,
        grid_spec=pltpu.PrefetchScalarGridSpec(
            num_scalar_prefetch=1, grid=(r // tr,),
            in_specs=[pl.BlockSpec((None, tr, c), lambda i, chip_ref: (chip_ref[0], i, 0)),
                      pl.BlockSpec((3, tr, c), lambda i, chip_ref: (0, i, 0))],
            out_specs=pl.BlockSpec((tr, c), lambda i, chip_ref: (i, 0))),
        out_shape=jax.ShapeDtypeStruct((r, c), F32), compiler_params=_params("parallel"),
    )(chip, grad, land)


def _allreduce_small(packed, *, name):
    r, d = packed.shape
    n_dev = 8

    def body(src_ref, out_ref, buf, send, recv):
        x, y, c = _place()
        me = 4 * x + 2 * y + c
        started = []
        for p in range(1, n_dev):
            rc = pltpu.make_async_remote_copy(
                src_ref=src_ref, dst_ref=buf.at[me], send_sem=send.at[p - 1], recv_sem=recv.at[p - 1],
                device_id=(x ^ (p >> 2), y ^ ((p >> 1) & 1), c ^ (p & 1)), device_id_type=MESH)
            rc.start()
            started.append(rc)
        buf[me] = src_ref[...]
        for rc in started:
            rc.wait()
        total = buf[0]
        for s in range(1, n_dev):
            total = total + buf[s]
        out_ref[...] = total

    vmem = pl.BlockSpec(memory_space=pltpu.VMEM)
    return pl.pallas_call(
        body, name=name, in_specs=[vmem], out_specs=vmem, out_shape=jax.ShapeDtypeStruct((r, d), F32),
        scratch_shapes=[pltpu.VMEM((n_dev, r, d), F32), pltpu.SemaphoreType.DMA((n_dev - 1,)),
                        pltpu.SemaphoreType.DMA((n_dev - 1,))],
    )(packed)


def _packed_rows(size, d):
    return -(-size // (8 * d)) * 8


def _pack_rows(arrays, d):
    rows = []
    for arr in arrays:
        flat = arr.reshape(-1).astype(F32)
        n = _packed_rows(flat.shape[0], d)
        rows.append(jnp.pad(flat, (0, n * d - flat.shape[0])).reshape(n, d))
    return jnp.concatenate(rows, axis=0)


def _unpack_rows(packed, shapes, d):
    out, row = [], 0
    for shape in shapes:
        size = math.prod(shape)
        n = _packed_rows(size, d)
        out.append(packed[row:row + n].reshape(-1)[:size].reshape(shape))
        row += n
    return out


SMALL = ("norm1_g", "gate_b", "conv_b", "conv_norm_g", "q_norm_g", "k_norm_g", "norm2_g", "ffn_conv_b")
LARGE = ("w_in", "w_conv_out", "w_attn_out", "w_out", "w_up", "w_down")
WEIGHTS = ("norm1_g", "w_in", "gate_b", "conv_w", "conv_b", "conv_norm_g", "w_conv_out", "q_norm_g", "k_norm_g",
           "w_attn_out", "w_out", "norm2_g", "w_up", "ffn_conv_w", "ffn_conv_b", "w_down")


def _head_ones(dims):
    a = dims.n_heads * dims.head_dim
    head = jnp.arange(a, dtype=jnp.int32) // dims.head_dim
    return (head[:, None] == head[None, :]).astype(BF16)


def _after(vec, token):
    return vec if token is None else vec + token[0:1, 0:1]


def _local_step(dims, x, target, small, first_weights, other_weights, send_grad):
    d, f, heads = dims.d_model, dims.d_ff, dims.n_heads
    small = dict(small)
    row = lambda name: small[name].reshape(1, -1)
    ones = _head_ones(dims)
    gq = jnp.tile(row("q_norm_g"), (1, heads))
    gk = jnp.tile(row("k_norm_g"), (1, heads))
    one_shard = lambda w: w.reshape(1, -1, w.shape[-1])

    h = _rmsnorm_fwd(x, row("norm1_g"), name="norm1")
    full = first_weights(h)
    w_in = full["w_in"]
    conv_w = jnp.pad(full["conv_w"], ((0, CONV_HALO - dims.conv_width), (0, 0)))
    ffn_w = jnp.pad(full["ffn_conv_w"], ((0, FFN_HALO - dims.ffn_conv_width), (0, 0)))
    z = _mm_nn(h, w_in, out_dtype=F32, name="in_proj")
    a1, a3 = _conv_branch_fwd(z, conv_w, row("conv_b"), row("conv_norm_g"), dims, name="conv_branch")
    qn, kn, vb = _qkv_fwd(z, gq, gk, ones, dims, name="qk_norm")
    state = None
    for gi, dil in enumerate(DILATIONS):
        state = _attn_fwd_group(qn, kn, vb, state, dims, dil, last=gi == len(DILATIONS) - 1, name=f"attn_fwd_d{dil}")
    o, lse = state
    full = other_weights(o)
    w_up = full["w_up"]
    w_co, w_ao, w_o, w_dn = (one_shard(full[k]) for k in ("w_conv_out", "w_attn_out", "w_out", "w_down"))
    ya = _mm_nn(a3, w_co, out_dtype=F32, name="conv_out_proj")
    yb = _mm_nn(o, w_ao, out_dtype=F32, name="attn_out_proj")
    mixed = _mix_fwd(ya, yb, z, row("gate_b"), dims, name="gate_mix")
    x1 = _mm_nn(mixed, w_o, out_dtype=F32, residual=x, name="out_proj")
    h2 = _rmsnorm_fwd(x1, row("norm2_g"), name="norm2")
    up = _mm_nn(h2, w_up, out_dtype=F32, name="up_proj")
    act = _ffn_act_fwd(up, ffn_w, row("ffn_conv_b"), dims, name="ffn_act")
    x2 = _mm_nn(act, w_dn, out_dtype=F32, residual=x1, name="down_proj")
    dy, dy_b, loss = _loss_head(x2, target, name="loss_head")

    grads = {}

    def large(name, g, then):
        grads[name] = g
        small[then] = _after(small[then].reshape(1, -1), send_grad(name, g))

    large("w_down", _mm_tn(act, dy_b, n_shards=1, name="dw_down"), "ffn_conv_b")
    dact = _mm_nt(dy_b, w_dn, out_dtype=BF16, name="d_act")
    du, dfw, dfb = _ffn_act_bwd(dact, up, ffn_w, row("ffn_conv_b"), dims, name="ffn_act_bwd")
    grads["ffn_conv_w"], grads["ffn_conv_b"] = dfw[:dims.ffn_conv_width], dfb
    dup = _ffn_conv_bwd(du, ffn_w, dims, name="ffn_conv_bwd")
    large("w_up", _mm_tn(h2, dup, n_shards=N_CHIPS, name="dw_up"), "norm2_g")
    dh2 = _mm_nt(dup, w_up, out_dtype=F32, name="d_h2")
    dx1, dx1_b, grads["norm2_g"] = _rmsnorm_bwd(x1, row("norm2_g"), dh2, dy, want_bf16=True, name="norm2_bwd")
    large("w_out", _mm_tn(mixed, dx1_b, n_shards=1, name="dw_out"), "gate_b")
    dmix = _mm_nt(dx1_b, w_o, out_dtype=F32, name="d_mix")
    dya, dyb, dz_gate, grads["gate_b"] = _mix_bwd(dmix, ya, yb, z, row("gate_b"), dims, name="gate_mix_bwd")
    large("w_attn_out", _mm_tn(o, dyb, n_shards=1, name="dw_attn_out"), "conv_norm_g")
    large("w_conv_out", _mm_tn(a3, dya, n_shards=1, name="dw_conv_out"), "conv_norm_g")
    da3 = _mm_nt(dya, w_co, out_dtype=F32, name="d_conv_act")
    da1, grads["conv_norm_g"] = _conv_norm_bwd(da3, a1, row("conv_norm_g"), name="conv_norm_bwd")
    dz_glu, dcw, grads["conv_b"] = _conv_branch_bwd(da1, z, conv_w, dims, name="conv_branch_bwd")
    grads["conv_w"] = dcw[:dims.conv_width]
    do = _mm_nt(dyb, w_ao, out_dtype=BF16, name="d_attn")
    delta = _attn_delta(do, o, ones, dims, name="attn_delta")
    per_group = [_attn_bwd_group(qn, kn, vb, do, lse, delta, dims, dil, name=f"attn_bwd_d{dil}") for dil in DILATIONS]
    dqs, dks, dvs = zip(*per_group)
    dz_qkv, dgq, dgk = _qkv_bwd(z, dqs, dks, dvs, gq, gk, ones, dims, name="qk_norm_bwd")
    grads["q_norm_g"] = dgq.reshape(heads, dims.head_dim).sum(axis=0)
    grads["k_norm_g"] = dgk.reshape(heads, dims.head_dim).sum(axis=0)
    dz = jnp.concatenate([dz_glu, dz_qkv, dz_gate], axis=1)
    large("w_in", _mm_tn(h, dz, n_shards=N_CHIPS, name="dw_in"), "norm1_g")
    dh = _mm_nt(dz, w_in, out_dtype=F32, name="d_h")
    dx, grads["norm1_g"] = _rmsnorm_bwd(x, row("norm1_g"), dh, dx1, want_bf16=False, name="norm1_bwd")
    return loss, dx, grads


def _step(dims, x, target, w, m, v):
    d = dims.d_model
    t = dims.tokens
    sq = lambda a: a.reshape(a.shape[1:])
    w2, m2, v2 = ({k: sq(a) for k, a in grp.items()} for grp in (w, m, v))

    conv_pad = jnp.pad(w2["conv_w"], ((0, CONV_HALO - dims.conv_width), (0, 0)))
    ffn_pad = jnp.pad(w2["ffn_conv_w"], ((0, FFN_HALO - dims.ffn_conv_width), (0, 0)))
    gathered_names = LARGE + ("conv_w", "ffn_conv_w")
    lands = dict(zip(gathered_names, _cast_to_lands([w2[k] for k in LARGE] + [conv_pad, ffn_pad],
                                                   [BF16] * len(LARGE) + [F32, F32], name="cast_weights")))
    first_names = ("w_in", "conv_w", "ffn_conv_w")
    other_names = tuple(k for k in gathered_names if k not in first_names)
    first = _gather_start([lands[k] for k in first_names], name="gather_start_first")
    other = _gather_start([lands[k] for k in other_names], name="gather_start_other")
    cols = lambda g, rows: jnp.moveaxis(g, 0, 1).reshape(g.shape[1], -1)[:rows]

    def first_weights(after):
        got = dict(zip(first_names, _gather_wait(*first[:3], after, name="gather_wait_first")))
        got["conv_w"] = cols(got["conv_w"], dims.conv_width)
        got["ffn_conv_w"] = cols(got["ffn_conv_w"], dims.ffn_conv_width)
        return got

    def other_weights(after):
        return dict(zip(other_names, _gather_wait(*other[:3], after, name="gather_wait_other")))

    started = {}

    def send_grad(name, g):
        send, recv, g_thru, land, token = _scatter_start(g.reshape(N_CHIPS, -1, g.shape[-1]), name=f"scatter_start_{name}")
        started[name] = (send, recv, g_thru, land)
        return token

    small = {k: w2[k] for k in SMALL}
    small["norm1_g"] = _after(_after(small["norm1_g"].reshape(1, -1), first[3]), other[3])
    loss, dx, grads = _local_step(dims, x.reshape(t, d), target.reshape(t, d), small, first_weights, other_weights, send_grad)

    arrived = _scatter_wait([started[k] for k in LARGE], dx, name="scatter_wait")
    mine = [_sum_received(g, land, name=f"sum_{k}") for k, (g, land) in zip(LARGE, arrived)]
    theirs = _swap_sibling(mine, name="swap_sibling")

    small_names = SMALL + ("conv_w", "ffn_conv_w")
    packed = _pack_rows([grads[k] for k in small_names] + [loss[0, 0]], d)
    reduced = _allreduce_small(packed, name="allreduce_small")
    shapes = [grads[k].shape for k in small_names] + [()]
    *small_g, loss_total = _unpack_rows(reduced, shapes, d)
    small_g = dict(zip(small_names, small_g))
    chip = 2 * lax.axis_index("x") + lax.axis_index("y")
    for k in ("conv_w", "ffn_conv_w"):
        width = w2[k].shape[1]
        small_g[k] = lax.dynamic_slice_in_dim(small_g[k], chip * width, width, axis=1)

    out = {}
    for k, a, b in zip(LARGE, mine, theirs):
        out[k] = _adamw(w2[k], [a, b], m2[k], v2[k], name=f"adamw_{k}")
    small_shapes = [w2[k].shape for k in small_names]
    pack = lambda grp: _pack_rows([grp[k] for k in small_names], d)
    results = _adamw(pack(w2), [pack(small_g)], pack(m2), pack(v2), name="adamw_small")
    unpacked = [_unpack_rows(r, small_shapes, d) for r in results]
    for i, k in enumerate(small_names):
        out[k] = tuple(u[i] for u in unpacked)

    lead = lambda a: a.reshape((1,) + a.shape)
    ordered = [[lead(out[k][j].reshape(w2[k].shape)) for k in WEIGHTS] for j in range(4)]
    return (loss_total, dx.reshape(x.shape), *ordered[0], *ordered[1], *ordered[2], *ordered[3])


def kernel(x, norm1_g, w_in, gate_b, conv_w, conv_b, conv_norm_g, w_conv_out, q_norm_g, k_norm_g, w_attn_out, w_out, norm2_g, w_up, ffn_conv_w, ffn_conv_b, w_down, loss_target, m_norm1_g, m_w_in, m_gate_b, m_conv_w, m_conv_b, m_conv_norm_g, m_w_conv_out, m_q_norm_g, m_k_norm_g, m_w_attn_out, m_w_out, m_norm2_g, m_w_up, m_ffn_conv_w, m_ffn_conv_b, m_w_down, v_norm1_g, v_w_in, v_gate_b, v_conv_w, v_conv_b, v_conv_norm_g, v_w_conv_out, v_q_norm_g, v_k_norm_g, v_w_attn_out, v_w_out, v_norm2_g, v_w_up, v_ffn_conv_w, v_ffn_conv_b, v_w_down):
    w = dict(zip(WEIGHTS, (norm1_g, w_in, gate_b, conv_w, conv_b, conv_norm_g, w_conv_out, q_norm_g, k_norm_g,
                           w_attn_out, w_out, norm2_g, w_up, ffn_conv_w, ffn_conv_b, w_down)))
    m = dict(zip(WEIGHTS, (m_norm1_g, m_w_in, m_gate_b, m_conv_w, m_conv_b, m_conv_norm_g, m_w_conv_out, m_q_norm_g,
                           m_k_norm_g, m_w_attn_out, m_w_out, m_norm2_g, m_w_up, m_ffn_conv_w, m_ffn_conv_b, m_w_down)))
    v = dict(zip(WEIGHTS, (v_norm1_g, v_w_in, v_gate_b, v_conv_w, v_conv_b, v_conv_norm_g, v_w_conv_out, v_q_norm_g,
                           v_k_norm_g, v_w_attn_out, v_w_out, v_norm2_g, v_w_up, v_ffn_conv_w, v_ffn_conv_b, v_w_down)))
    dims = Dims(d_model=x.shape[-1], batch_local=x.shape[0], seq=x.shape[1], d_ff=w_down.shape[1] * N_CHIPS)
    return _step(dims, x, loss_target, w, m, v)
```

```python
import functools
import math
from typing import NamedTuple

import jax
import jax.numpy as jnp
from jax import lax
from jax.experimental import pallas as pl
from jax.experimental.pallas import tpu as pltpu

F32 = jnp.float32
BF16 = jnp.bfloat16

RMS_EPS = 1e-6
MASKED_SCORE = -1e30
ATTN_BLOCK = 128
DILATIONS = (1, 4, 16)
CONV_HALO = 32
FFN_HALO = 8
ADAM_LR, ADAM_B1, ADAM_B2, ADAM_EPS, ADAM_WD, ADAM_STEP = 0.001, 0.9, 0.999, 1e-08, 0.01, 10
V7X_VMEM_LIMIT_BYTES = 56 * 2 ** 20
N_CHIPS = 4
MESH = pl.DeviceIdType.MESH


class Dims(NamedTuple):
    d_model: int = 1024
    n_heads: int = 16
    head_dim: int = 64
    d_ff: int = 2816
    seq: int = 2048
    batch_local: int = 2
    conv_width: int = 31
    ffn_conv_width: int = 3

    @property
    def tokens(self):
        return self.seq * self.batch_local


def _params(*semantics):
    return pltpu.CompilerParams(dimension_semantics=semantics, vmem_limit_bytes=V7X_VMEM_LIMIT_BYTES)


def _pick(n, target, mult=128):
    if n <= target:
        return n
    best = None
    for t in range(mult, target + 1, mult):
        if n % t == 0:
            best = t
    assert best is not None, (n, target, mult)
    return best


def _sigmoid(v):
    return 1.0 / (1.0 + jnp.exp(-v))


def _mm_nn(a, w, *, out_dtype, name, residual=None, tm=1024, tn=1408, tk=2816):
    m, k = a.shape
    nsh, k2, c = w.shape
    assert k == k2 and a.dtype == BF16 and w.dtype == BF16
    n = nsh * c
    tm, tn, tk = _pick(m, tm, 8), _pick(c, tn), _pick(k, tk)
    nk, cpn = k // tk, c // tn

    def body(*refs):
        if residual is None:
            a_ref, w_ref, o_ref, acc = refs
        else:
            a_ref, w_ref, r_ref, o_ref, acc = refs
        prod = jnp.dot(a_ref[...], w_ref[...], preferred_element_type=F32)

        def finish(total):
            if residual is not None:
                total = total + r_ref[...]
            o_ref[...] = total.astype(out_dtype)

        if nk == 1:
            finish(prod)
        else:
            kk = pl.program_id(2)

            @pl.when(kk == 0)
            def _():
                acc[...] = prod

            @pl.when(kk > 0)
            def _():
                acc[...] += prod

            @pl.when(kk == nk - 1)
            def _():
                finish(acc[...])

    in_specs = [pl.BlockSpec((tm, tk), lambda i, j, kk: (i, kk)),
                pl.BlockSpec((None, tk, tn), lambda i, j, kk: (j // cpn, kk, j % cpn))]
    args = [a, w]
    if residual is not None:
        in_specs.append(pl.BlockSpec((tm, tn), lambda i, j, kk: (i, j)))
        args.append(residual)
    return pl.pallas_call(
        body, name=name, grid=(m // tm, n // tn, nk),
        in_specs=in_specs, out_specs=pl.BlockSpec((tm, tn), lambda i, j, kk: (i, j)),
        out_shape=jax.ShapeDtypeStruct((m, n), out_dtype),
        scratch_shapes=[pltpu.VMEM((tm, tn) if nk > 1 else (8, 128), F32)],
        compiler_params=_params("parallel", "parallel", "arbitrary"),
    )(*args)


def _mm_nt(a, w, *, out_dtype, name, tm=1024, tn=1408, tk=1792):
    m, k = a.shape
    nsh, r, c = w.shape
    assert k == nsh * c and a.dtype == BF16 and w.dtype == BF16
    tm, tn, tk = _pick(m, tm, 8), _pick(r, tn), _pick(c, tk)
    nk, cpk = k // tk, c // tk

    def body(a_ref, w_ref, o_ref, acc):
        prod = lax.dot_general(a_ref[...], w_ref[...], (((1,), (1,)), ((), ())), preferred_element_type=F32)
        if nk == 1:
            o_ref[...] = prod.astype(out_dtype)
        else:
            kk = pl.program_id(2)

            @pl.when(kk == 0)
            def _():
                acc[...] = prod

            @pl.when(kk > 0)
            def _():
                acc[...] += prod

            @pl.when(kk == nk - 1)
            def _():
                o_ref[...] = acc[...].astype(out_dtype)

    return pl.pallas_call(
        body, name=name, grid=(m // tm, r // tn, nk),
        in_specs=[pl.BlockSpec((tm, tk), lambda i, j, kk: (i, kk)),
                  pl.BlockSpec((None, tn, tk), lambda i, j, kk: (kk // cpk, j, kk % cpk))],
        out_specs=pl.BlockSpec((tm, tn), lambda i, j, kk: (i, j)),
        out_shape=jax.ShapeDtypeStruct((m, r), out_dtype),
        scratch_shapes=[pltpu.VMEM((tm, tn) if nk > 1 else (8, 128), F32)],
        compiler_params=_params("parallel", "parallel", "arbitrary"),
    )(a, w)


def _mm_tn(a, b, *, n_shards, name, tm=1408, tn=1408, tk=512):
    t, m = a.shape
    t2, n = b.shape
    assert t == t2 and a.dtype == BF16 and b.dtype == BF16
    c = n // n_shards
    tm, tn, tk = _pick(m, tm), _pick(c, tn), _pick(t, tk, 8)
    nk, cpn = t // tk, c // tn

    def body(a_ref, b_ref, o_ref, ob_ref, acc):
        kk = pl.program_id(2)
        prod = lax.dot_general(a_ref[...], b_ref[...], (((0,), (0,)), ((), ())), preferred_element_type=F32)

        @pl.when(kk == 0)
        def _():
            acc[...] = prod

        @pl.when(kk > 0)
        def _():
            acc[...] += prod

        @pl.when(kk == nk - 1)
        def _():
            total = acc[...]
            o_ref[...] = total
            ob_ref[...] = total.astype(BF16)

    out_spec = pl.BlockSpec((None, tm, tn), lambda i, j, kk: (j // cpn, i, j % cpn))
    return pl.pallas_call(
        body, name=name, grid=(m // tm, n // tn, nk),
        in_specs=[pl.BlockSpec((tk, tm), lambda i, j, kk: (kk, i)),
                  pl.BlockSpec((tk, tn), lambda i, j, kk: (kk, j))],
        out_specs=[out_spec, out_spec],
        out_shape=[jax.ShapeDtypeStruct((n_shards, m, c), F32), jax.ShapeDtypeStruct((n_shards, m, c), BF16)],
        scratch_shapes=[pltpu.VMEM((tm, tn), F32)],
        compiler_params=_params("parallel", "parallel", "arbitrary"),
    )(a, b)


def _row_spec(tr, width, col=0):
    return pl.BlockSpec((tr, width), lambda i, col=col: (i, col))


def _vec_spec(width, col=0):
    return pl.BlockSpec((1, width), lambda i, col=col: (0, col))


def _accumulate(ref, value, first):
    @pl.when(first)
    def _():
        ref[...] = value

    @pl.when(jnp.logical_not(first))
    def _():
        ref[...] += value


def _rmsnorm_fwd(x, g, *, name, tr=512):
    t, d = x.shape
    tr = _pick(t, tr, 8)

    def body(x_ref, g_ref, o_ref):
        xv = x_ref[...]
        r = lax.rsqrt(jnp.mean(xv * xv, axis=-1, keepdims=True) + RMS_EPS)
        o_ref[...] = (xv * r * g_ref[...]).astype(BF16)

    return pl.pallas_call(
        body, name=name, grid=(t // tr,),
        in_specs=[_row_spec(tr, d), _vec_spec(d)], out_specs=_row_spec(tr, d),
        out_shape=jax.ShapeDtypeStruct((t, d), BF16), compiler_params=_params("parallel"),
    )(x, g)


def _rmsnorm_bwd(x, g, dy, dres, *, name, want_bf16, tr=512):
    t, d = x.shape
    tr = _pick(t, tr, 8)

    def body(x_ref, g_ref, dy_ref, dres_ref, *outs):
        dx_ref, dg_ref = outs[0], outs[-1]
        xv, dyv = x_ref[...], dy_ref[...].astype(F32)
        r = lax.rsqrt(jnp.mean(xv * xv, axis=-1, keepdims=True) + RMS_EPS)
        gy = dyv * g_ref[...]
        dx = dres_ref[...] + r * gy - xv * (r * r * r) * jnp.mean(xv * gy, axis=-1, keepdims=True)
        dx_ref[...] = dx
        if want_bf16:
            outs[1][...] = dx.astype(BF16)
        _accumulate(dg_ref, jnp.sum(dyv * xv * r, axis=0, keepdims=True), pl.program_id(0) == 0)

    out_shape = [jax.ShapeDtypeStruct((t, d), F32)]
    out_specs = [_row_spec(tr, d)]
    if want_bf16:
        out_shape.append(jax.ShapeDtypeStruct((t, d), BF16))
        out_specs.append(_row_spec(tr, d))
    out_shape.append(jax.ShapeDtypeStruct((1, d), F32))
    out_specs.append(_vec_spec(d))
    return pl.pallas_call(
        body, name=name, grid=(t // tr,),
        in_specs=[_row_spec(tr, d), _vec_spec(d), _row_spec(tr, d), _row_spec(tr, d)],
        out_specs=out_specs, out_shape=out_shape, compiler_params=_params("arbitrary"),
    )(x, g, dy, dres)


def _head_mean(v, ones_ref, head_dim):
    hi = v.astype(BF16)
    lo = (v - hi.astype(F32)).astype(BF16)
    e = ones_ref[...]
    total = jnp.dot(hi, e, preferred_element_type=F32) + jnp.dot(lo, e, preferred_element_type=F32)
    return total * (1.0 / head_dim)


def _qkv_fwd(z, gq, gk, head_ones, dims, *, name, tr=256):
    t = z.shape[0]
    a = dims.n_heads * dims.head_dim
    tr = _pick(t, tr, 8)
    q_scale = dims.head_dim ** -0.5

    def body(q_ref, k_ref, v_ref, gq_ref, gk_ref, e_ref, qo_ref, ko_ref, vo_ref):
        qv, kv = q_ref[...], k_ref[...]
        rq = lax.rsqrt(_head_mean(qv * qv, e_ref, dims.head_dim) + RMS_EPS)
        rk = lax.rsqrt(_head_mean(kv * kv, e_ref, dims.head_dim) + RMS_EPS)
        qo_ref[...] = (qv * rq * gq_ref[...] * q_scale).astype(BF16)
        ko_ref[...] = (kv * rk * gk_ref[...]).astype(BF16)
        vo_ref[...] = v_ref[...].astype(BF16)

    return pl.pallas_call(
        body, name=name, grid=(t // tr,),
        in_specs=[_row_spec(tr, a, 2), _row_spec(tr, a, 3), _row_spec(tr, a, 4), _vec_spec(a), _vec_spec(a),
                  pl.BlockSpec((a, a), lambda i: (0, 0))],
        out_specs=[_row_spec(tr, a)] * 3, out_shape=[jax.ShapeDtypeStruct((t, a), BF16)] * 3,
        compiler_params=_params("parallel"),
    )(z, z, z, gq, gk, head_ones)


def _qkv_bwd(z, dqs, dks, dvs, gq, gk, head_ones, dims, *, name, tr=256):
    t = z.shape[0]
    a = dims.n_heads * dims.head_dim
    tr = _pick(t, tr, 8)
    q_scale = dims.head_dim ** -0.5
    ng = len(dqs)

    def body(*refs):
        q_ref, k_ref = refs[:2]
        dq_refs, dk_refs, dv_refs = refs[2:2 + ng], refs[2 + ng:2 + 2 * ng], refs[2 + 2 * ng:2 + 3 * ng]
        gq_ref, gk_ref, e_ref = refs[2 + 3 * ng:5 + 3 * ng]
        dz_ref, dgq_ref, dgk_ref = refs[5 + 3 * ng:]
        first = pl.program_id(0) == 0

        def norm_bwd(x_ref, d_refs, g_ref, scale, col, dg_ref):
            xv = x_ref[...]
            dy = sum(r[...] for r in d_refs) * scale
            r = lax.rsqrt(_head_mean(xv * xv, e_ref, dims.head_dim) + RMS_EPS)
            gy = dy * g_ref[...]
            dx = r * gy - xv * (r * r * r) * _head_mean(xv * gy, e_ref, dims.head_dim)
            dz_ref[:, col * a:(col + 1) * a] = dx.astype(BF16)
            _accumulate(dg_ref, jnp.sum(dy * xv * r, axis=0, keepdims=True), first)

        norm_bwd(q_ref, dq_refs, gq_ref, q_scale, 0, dgq_ref)
        norm_bwd(k_ref, dk_refs, gk_ref, 1.0, 1, dgk_ref)
        dz_ref[:, 2 * a:3 * a] = sum(r[...] for r in dv_refs).astype(BF16)

    in_specs = ([_row_spec(tr, a, 2), _row_spec(tr, a, 3)] + [_row_spec(tr, a)] * (3 * ng)
                + [_vec_spec(a), _vec_spec(a), pl.BlockSpec((a, a), lambda i: (0, 0))])
    return pl.pallas_call(
        body, name=name, grid=(t // tr,), in_specs=in_specs,
        out_specs=[_row_spec(tr, 3 * a), _vec_spec(a), _vec_spec(a)],
        out_shape=[jax.ShapeDtypeStruct((t, 3 * a), BF16)] + [jax.ShapeDtypeStruct((1, a), F32)] * 2,
        compiler_params=_params("arbitrary"),
    )(z, z, *dqs, *dks, *dvs, gq, gk, head_ones)


CONV_ROWS = 16


def _seq_specs(dims, ts, width, halo, col, *, nxt=False):
    nst, per = dims.seq // ts, ts // halo
    last = dims.tokens // halo - 1
    cur = pl.BlockSpec((ts, width), lambda b, i: (b * nst + i, col))
    if nxt:
        edge = pl.BlockSpec((halo, width), lambda b, i: (jnp.minimum((b * nst + i + 1) * per, last), col))
    else:
        edge = pl.BlockSpec((halo, width), lambda b, i: (jnp.maximum((b * nst + i) * per - 1, 0), col))
    return cur, edge


def _conv_branch_fwd(z, w, b, g, dims, *, name, ts=128):
    t, c, kw = z.shape[0], dims.d_model, dims.conv_width
    base = CONV_HALO - (kw - 1)

    def body(av_ref, hv_ref, ag_ref, hg_ref, w_ref, b_ref, g_ref, a1_ref, a3_ref, buf):
        i = pl.program_id(1)
        buf[CONV_HALO:, :] = av_ref[...] * _sigmoid(ag_ref[...])
        buf[0:CONV_HALO, :] = jnp.where(i > 0, hv_ref[...] * _sigmoid(hg_ref[...]), 0.0)
        for r0 in range(0, ts, CONV_ROWS):
            acc = jnp.broadcast_to(b_ref[...], (CONV_ROWS, c))
            for k in range(kw):
                acc = acc + w_ref[k:k + 1, :] * buf[pl.ds(r0 + base + k, CONV_ROWS), :]
            a1_ref[r0:r0 + CONV_ROWS, :] = acc
            a2 = acc * lax.rsqrt(jnp.mean(acc * acc, axis=-1, keepdims=True) + RMS_EPS) * g_ref[...]
            a3_ref[r0:r0 + CONV_ROWS, :] = (a2 * _sigmoid(a2)).astype(BF16)

    vec = pl.BlockSpec((1, c), lambda b, i: (0, 0))
    out = pl.BlockSpec((ts, c), lambda b, i: (b * (dims.seq // ts) + i, 0))
    return pl.pallas_call(
        body, name=name, grid=(dims.batch_local, dims.seq // ts),
        in_specs=[*_seq_specs(dims, ts, c, CONV_HALO, 0), *_seq_specs(dims, ts, c, CONV_HALO, 1),
                  pl.BlockSpec((CONV_HALO, c), lambda b, i: (0, 0)), vec, vec],
        out_specs=[out, out],
        out_shape=[jax.ShapeDtypeStruct((t, c), F32), jax.ShapeDtypeStruct((t, c), BF16)],
        scratch_shapes=[pltpu.VMEM((CONV_HALO + ts, c), F32)],
        compiler_params=_params("parallel", "parallel"),
    )(z, z, z, z, w, b, g)


def _conv_norm_bwd(da3, a1, g, *, name, tr=256):
    t, c = a1.shape
    tr = _pick(t, tr, 8)

    def body(d_ref, a_ref, g_ref, o_ref, dg_ref):
        a1v, gv = a_ref[...], g_ref[...]
        r = lax.rsqrt(jnp.mean(a1v * a1v, axis=-1, keepdims=True) + RMS_EPS)
        a2 = a1v * r * gv
        sg = _sigmoid(a2)
        da2 = d_ref[...].astype(F32) * sg * (1.0 + a2 * (1.0 - sg))
        gy = da2 * gv
        o_ref[...] = r * gy - a1v * (r * r * r) * jnp.mean(a1v * gy, axis=-1, keepdims=True)
        _accumulate(dg_ref, jnp.sum(da2 * a1v * r, axis=0, keepdims=True), pl.program_id(0) == 0)

    return pl.pallas_call(
        body, name=name, grid=(t // tr,),
        in_specs=[_row_spec(tr, c), _row_spec(tr, c), _vec_spec(c)],
        out_specs=[_row_spec(tr, c), _vec_spec(c)],
        out_shape=[jax.ShapeDtypeStruct((t, c), F32), jax.ShapeDtypeStruct((1, c), F32)],
        compiler_params=_params("arbitrary"),
    )(da3, a1, g)


def _conv_branch_bwd(da1, z, w, dims, *, name, ts=128):
    t, c, kw = z.shape[0], dims.d_model, dims.conv_width
    nst = dims.seq // ts
    base = CONV_HALO - (kw - 1)

    def body(d_ref, dn_ref, av_ref, hv_ref, ag_ref, hg_ref, w_ref, dz_ref, dw_ref, db_ref, abuf, dbuf):
        i = pl.program_id(1)
        first = jnp.logical_and(pl.program_id(0) == 0, i == 0)
        abuf[CONV_HALO:, :] = av_ref[...] * _sigmoid(ag_ref[...])
        abuf[0:CONV_HALO, :] = jnp.where(i > 0, hv_ref[...] * _sigmoid(hg_ref[...]), 0.0)
        d1 = d_ref[...]
        dbuf[0:ts, :] = d1
        dbuf[ts:, :] = jnp.where(i < nst - 1, dn_ref[...], 0.0)

        @pl.when(first)
        def _():
            dw_ref[...] = jnp.zeros_like(dw_ref)
            db_ref[...] = jnp.zeros_like(db_ref)

        db_ref[...] += jnp.sum(d1, axis=0, keepdims=True)
        for k in range(kw):
            dw_ref[k:k + 1, :] += jnp.sum(d1 * abuf[pl.ds(base + k, ts), :], axis=0, keepdims=True)
        for r0 in range(0, ts, CONV_ROWS):
            acc = jnp.zeros((CONV_ROWS, c), F32)
            for k in range(kw):
                acc = acc + w_ref[k:k + 1, :] * dbuf[pl.ds(r0 + (kw - 1) - k, CONV_ROWS), :]
            av = av_ref[r0:r0 + CONV_ROWS, :]
            sg = _sigmoid(ag_ref[r0:r0 + CONV_ROWS, :])
            dz_ref[r0:r0 + CONV_ROWS, 0:c] = (acc * sg).astype(BF16)
            dz_ref[r0:r0 + CONV_ROWS, c:2 * c] = (acc * av * sg * (1.0 - sg)).astype(BF16)

    cur, nxt = _seq_specs(dims, ts, c, CONV_HALO, 0, nxt=True)
    return pl.pallas_call(
        body, name=name, grid=(dims.batch_local, nst),
        in_specs=[cur, nxt, *_seq_specs(dims, ts, c, CONV_HALO, 0), *_seq_specs(dims, ts, c, CONV_HALO, 1),
                  pl.BlockSpec((CONV_HALO, c), lambda b, i: (0, 0))],
        out_specs=[pl.BlockSpec((ts, 2 * c), lambda b, i: (b * nst + i, 0)),
                   pl.BlockSpec((CONV_HALO, c), lambda b, i: (0, 0)), pl.BlockSpec((1, c), lambda b, i: (0, 0))],
        out_shape=[jax.ShapeDtypeStruct((t, 2 * c), BF16), jax.ShapeDtypeStruct((CONV_HALO, c), F32),
                   jax.ShapeDtypeStruct((1, c), F32)],
        scratch_shapes=[pltpu.VMEM((CONV_HALO + ts, c), F32), pltpu.VMEM((ts + CONV_HALO, c), F32)],
        compiler_params=_params("arbitrary", "arbitrary"),
    )(da1, da1, z, z, z, z, w)


FFN_ROWS = 16
FFN_COLS = 256


def _ffn_chunks(ts, f):
    cw = _pick(f, FFN_COLS)
    return [(r0, c0, cw) for r0 in range(0, ts, FFN_ROWS) for c0 in range(0, f, cw)]


def _ffn_conv(buf, w_ref, b_ref, r0, cols, kw):
    base = FFN_HALO - (kw - 1)
    u = jnp.broadcast_to(b_ref[:, cols], (FFN_ROWS, cols.stop - cols.start))
    for k in range(kw):
        u = u + w_ref[k:k + 1, cols] * buf[pl.ds(r0 + base + k, FFN_ROWS), cols]
    return u


def _ffn_act_fwd(up, w, b, dims, *, name, ts=128):
    t, f, kw = up.shape[0], dims.d_ff, dims.ffn_conv_width

    def body(up_ref, h_ref, w_ref, b_ref, o_ref, buf):
        buf[FFN_HALO:, :] = up_ref[...]
        buf[0:FFN_HALO, :] = jnp.where(pl.program_id(1) > 0, h_ref[...], 0.0)
        for r0, c0, cw in _ffn_chunks(ts, f):
            uv = _ffn_conv(buf, w_ref, b_ref, r0, slice(c0, c0 + cw), kw)
            ug = _ffn_conv(buf, w_ref, b_ref, r0, slice(f + c0, f + c0 + cw), kw)
            o_ref[r0:r0 + FFN_ROWS, c0:c0 + cw] = (ug * _sigmoid(ug) * uv).astype(BF16)

    full = lambda rows: pl.BlockSpec((rows, 2 * f), lambda b_, i: (0, 0))
    return pl.pallas_call(
        body, name=name, grid=(dims.batch_local, dims.seq // ts),
        in_specs=[*_seq_specs(dims, ts, 2 * f, FFN_HALO, 0), full(FFN_HALO), full(1)],
        out_specs=pl.BlockSpec((ts, f), lambda b_, i: (b_ * (dims.seq // ts) + i, 0)),
        out_shape=jax.ShapeDtypeStruct((t, f), BF16),
        scratch_shapes=[pltpu.VMEM((FFN_HALO + ts, 2 * f), F32)],
        compiler_params=_params("parallel", "parallel"),
    )(up, up, w, b)


def _ffn_act_bwd(dact, up, w, b, dims, *, name, ts=128):
    t, f, kw = up.shape[0], dims.d_ff, dims.ffn_conv_width
    base = FFN_HALO - (kw - 1)

    def body(d_ref, up_ref, h_ref, w_ref, b_ref, du_ref, dw_ref, db_ref, buf):
        i = pl.program_id(1)
        first = jnp.logical_and(pl.program_id(0) == 0, i == 0)
        buf[FFN_HALO:, :] = up_ref[...]
        buf[0:FFN_HALO, :] = jnp.where(i > 0, h_ref[...], 0.0)
        for r0, c0, cw in _ffn_chunks(ts, f):
            vcols, gcols = slice(c0, c0 + cw), slice(f + c0, f + c0 + cw)
            uv = _ffn_conv(buf, w_ref, b_ref, r0, vcols, kw)
            ug = _ffn_conv(buf, w_ref, b_ref, r0, gcols, kw)
            d = d_ref[r0:r0 + FFN_ROWS, vcols].astype(F32)
            sg = _sigmoid(ug)
            du_ref[r0:r0 + FFN_ROWS, vcols] = d * ug * sg
            du_ref[r0:r0 + FFN_ROWS, gcols] = d * uv * sg * (1.0 + ug * (1.0 - sg))

        @pl.when(first)
        def _():
            dw_ref[...] = jnp.zeros_like(dw_ref)
            db_ref[...] = jnp.zeros_like(db_ref)

        du = du_ref[...]
        db_ref[...] += jnp.sum(du, axis=0, keepdims=True)
        for k in range(kw):
            dw_ref[k:k + 1, :] += jnp.sum(du * buf[pl.ds(base + k, ts), :], axis=0, keepdims=True)

    nst = dims.seq // ts
    full = lambda rows: pl.BlockSpec((rows, 2 * f), lambda b_, i: (0, 0))
    return pl.pallas_call(
        body, name=name, grid=(dims.batch_local, nst),
        in_specs=[pl.BlockSpec((ts, f), lambda b_, i: (b_ * nst + i, 0)),
                  *_seq_specs(dims, ts, 2 * f, FFN_HALO, 0), full(FFN_HALO), full(1)],
        out_specs=[pl.BlockSpec((ts, 2 * f), lambda b_, i: (b_ * nst + i, 0)), full(FFN_HALO), full(1)],
        out_shape=[jax.ShapeDtypeStruct((t, 2 * f), F32), jax.ShapeDtypeStruct((FFN_HALO, 2 * f), F32),
                   jax.ShapeDtypeStruct((1, 2 * f), F32)],
        scratch_shapes=[pltpu.VMEM((FFN_HALO + ts, 2 * f), F32)],
        compiler_params=_params("arbitrary", "arbitrary"),
    )(dact, up, up, w, b)


def _ffn_conv_bwd(du, w, dims, *, name, ts=128):
    t, f2 = du.shape
    kw = dims.ffn_conv_width
    nst = dims.seq // ts

    def body(d_ref, dn_ref, w_ref, o_ref, buf):
        buf[0:ts, :] = d_ref[...]
        buf[ts:, :] = jnp.where(pl.program_id(1) < nst - 1, dn_ref[...], 0.0)
        for r0, c0, cw in _ffn_chunks(ts, f2):
            cols = slice(c0, c0 + cw)
            acc = jnp.zeros((FFN_ROWS, cw), F32)
            for k in range(kw):
                acc = acc + w_ref[k:k + 1, cols] * buf[pl.ds(r0 + (kw - 1) - k, FFN_ROWS), cols]
            o_ref[r0:r0 + FFN_ROWS, cols] = acc.astype(BF16)

    return pl.pallas_call(
        body, name=name, grid=(dims.batch_local, nst),
        in_specs=[*_seq_specs(dims, ts, f2, FFN_HALO, 0, nxt=True), pl.BlockSpec((FFN_HALO, f2), lambda b_, i: (0, 0))],
        out_specs=pl.BlockSpec((ts, f2), lambda b_, i: (b_ * nst + i, 0)),
        out_shape=jax.ShapeDtypeStruct((t, f2), BF16),
        scratch_shapes=[pltpu.VMEM((ts + FFN_HALO, f2), F32)],
        compiler_params=_params("parallel", "parallel"),
    )(du, du, w)


def _alibi_slope(h, n_heads):
    return 2.0 ** (-8.0 * (h + 1) / n_heads)


def _dot_nt(a, b):
    return lax.dot_general(a, b, (((1,), (1,)), ((), ())), preferred_element_type=F32)


def _dot_tn(a, b):
    return lax.dot_general(a, b, (((0,), (0,)), ((), ())), preferred_element_type=F32)


def _attn_view(x, dims, dil):
    return x.reshape(dims.batch_local, dims.seq // dil, dil * x.shape[-1])


def _attn_fwd_group(q, k, v, state, dims, dil, *, last, name):
    t, a = q.shape
    assert 2 * dims.head_dim == 128 and dims.n_heads % 2 == 0
    blk, hd = ATTN_BLOCK, dims.head_dim
    nb = dims.seq // dil // blk
    has_prev = nb > 1
    nkeys = 2 * blk if has_prev else blk

    def body(*refs):
        it = iter(refs)
        q_ref, kc_ref, vc_ref = next(it), next(it), next(it)
        kp_ref, vp_ref = (next(it), next(it)) if has_prev else (None, None)
        m_in, l_in, acc_in = (next(it), next(it), next(it)) if state is not None else (None, None, None)
        outs = list(it)
        iq = lax.broadcasted_iota(jnp.int32, (blk, nkeys), 0)
        jk = lax.broadcasted_iota(jnp.int32, (blk, nkeys), 1)
        if has_prev:
            steps = iq + blk - jk
            valid = (steps >= 0) & (steps <= blk) & ((jk >= blk) | (pl.program_id(2) > 0))
        else:
            steps = iq - jk
            valid = steps >= 0
        dist = steps.astype(F32) * float(dil)
        low = lax.broadcasted_iota(jnp.int32, (blk, 2 * hd), 1) < hd
        for hp in range(dims.n_heads // 2):
            sl = slice(2 * hd * hp, 2 * hd * (hp + 1))
            q2 = q_ref[:, sl]
            if has_prev:
                kcat = jnp.concatenate([kp_ref[:, sl], kc_ref[:, sl]], axis=0)
                vcat = jnp.concatenate([vp_ref[:, sl], vc_ref[:, sl]], axis=0)
            else:
                kcat, vcat = kc_ref[:, sl], vc_ref[:, sl]
            halves = []
            for half in range(2):
                col = 2 * hd * hp + hd * half
                qh = jnp.where(low if half == 0 else jnp.logical_not(low), q2, jnp.zeros_like(q2))
                sc = _dot_nt(qh, kcat) - _alibi_slope(2 * hp + half, dims.n_heads) * dist
                sc = jnp.where(valid, sc, MASKED_SCORE)
                row_max = jnp.max(sc, axis=-1, keepdims=True)
                if state is None:
                    m_new = row_max
                    p = jnp.exp(sc - m_new)
                    alpha = None
                    l_new = jnp.sum(p, axis=-1, keepdims=True)
                else:
                    m_old = m_in[:, col:col + 1]
                    m_new = jnp.maximum(m_old, row_max)
                    p = jnp.exp(sc - m_new)
                    alpha = jnp.exp(m_old - m_new)
                    l_new = alpha * l_in[:, col:col + 1] + jnp.sum(p, axis=-1, keepdims=True)
                pv = jnp.dot(p.astype(BF16), vcat, preferred_element_type=F32)
                halves.append((m_new, l_new, alpha, pv))
            (m_a, l_a, al_a, pv_a), (m_b, l_b, al_b, pv_b) = halves
            if state is None:
                acc = jnp.where(low, pv_a, pv_b)
            else:
                old = acc_in[:, sl]
                acc = jnp.where(low, al_a * old + pv_a, al_b * old + pv_b)
            m2 = jnp.where(low, m_a, m_b)
            l2 = jnp.where(low, l_a, l_b)
            if last:
                outs[0][:, sl] = (acc / l2).astype(BF16)
                outs[1][:, sl] = m2 + jnp.log(l2)
            else:
                outs[0][:, sl] = m2
                outs[1][:, sl] = l2
                outs[2][:, sl] = acc

    cur = pl.BlockSpec((None, blk, a), lambda b, r, i: (b, i, r))
    prev = pl.BlockSpec((None, blk, a), lambda b, r, i: (b, jnp.maximum(i - 1, 0), r))
    args, in_specs = [q, k, v], [cur, cur, cur]
    if has_prev:
        args += [k, v]
        in_specs += [prev, prev]
    if state is not None:
        args += list(state)
        in_specs += [cur] * 3
    shape = lambda dt: jax.ShapeDtypeStruct((dims.batch_local, dims.seq // dil, dil * a), dt)
    out_shape = [shape(BF16), shape(F32)] if last else [shape(F32)] * 3
    outs = pl.pallas_call(
        body, name=name, grid=(dims.batch_local, dil, nb),
        in_specs=in_specs, out_specs=[cur] * len(out_shape), out_shape=out_shape,
        compiler_params=_params("parallel", "parallel", "parallel"),
    )(*[_attn_view(x, dims, dil) for x in args])
    return tuple(o.reshape(t, a) for o in outs)


def _attn_delta(do, o, head_ones, dims, *, name, tr=512):
    t, a = o.shape
    tr = _pick(t, tr, 8)

    def body(do_ref, o_ref, e_ref, d_ref):
        prod = do_ref[...].astype(F32) * o_ref[...].astype(F32)
        d_ref[...] = _head_mean(prod, e_ref, dims.head_dim) * float(dims.head_dim)

    return pl.pallas_call(
        body, name=name, grid=(t // tr,),
        in_specs=[_row_spec(tr, a), _row_spec(tr, a), pl.BlockSpec((a, a), lambda i: (0, 0))],
        out_specs=_row_spec(tr, a), out_shape=jax.ShapeDtypeStruct((t, a), F32),
        compiler_params=_params("parallel"),
    )(do, o, head_ones)


def _attn_bwd_group(q, k, v, do, lse, delta, dims, dil, *, name):
    t, a = q.shape
    blk, hd = ATTN_BLOCK, dims.head_dim
    nb = dims.seq // dil // blk
    has_next = nb > 1

    def body(*refs):
        k_ref, v_ref, q_ref, do_ref, lse_ref, dl_ref = refs[:6]
        if has_next:
            qn_ref, don_ref, lsen_ref, dln_ref = refs[6:10]
            dq_ref, dk_ref, dv_ref, carry = refs[10:]
        else:
            dq_ref, dk_ref, dv_ref = refs[6:]
        j = pl.program_id(2)
        iq = lax.broadcasted_iota(jnp.int32, (blk, blk), 0)
        jk = lax.broadcasted_iota(jnp.int32, (blk, blk), 1)
        low = lax.broadcasted_iota(jnp.int32, (blk, 2 * hd), 1) < hd

        def pair(hp, qr, dor, lser, dlr, steps, valid):
            sl = slice(2 * hd * hp, 2 * hd * (hp + 1))
            q2, do2, k2, v2 = qr[:, sl], dor[:, sl], k_ref[:, sl], v_ref[:, sl]
            dist = steps.astype(F32) * float(dil)
            dq_h, dk2, dv2 = [], None, None
            for half in range(2):
                col = 2 * hd * hp + hd * half
                mask = low if half == 0 else jnp.logical_not(low)
                qh = jnp.where(mask, q2, jnp.zeros_like(q2))
                doh = jnp.where(mask, do2, jnp.zeros_like(do2))
                sc = _dot_nt(qh, k2) - _alibi_slope(2 * hp + half, dims.n_heads) * dist
                p = jnp.where(valid, jnp.exp(sc - lser[:, col:col + 1]), 0.0)
                ds = p * (_dot_nt(doh, v2) - dlr[:, col:col + 1])
                ds_b, p_b = ds.astype(BF16), p.astype(BF16)
                dq_h.append(jnp.dot(ds_b, k2, preferred_element_type=F32))
                dk_h, dv_h = _dot_tn(ds_b, qh), _dot_tn(p_b, doh)
                dk2 = dk_h if dk2 is None else dk2 + dk_h
                dv2 = dv_h if dv2 is None else dv2 + dv_h
            return sl, jnp.where(low, dq_h[0], dq_h[1]), dk2, dv2

        if has_next:
            @pl.when(j == 0)
            def _():
                carry[...] = jnp.zeros_like(carry)

        for hp in range(dims.n_heads // 2):
            sl, dq2, dk2, dv2 = pair(hp, q_ref, do_ref, lse_ref, dl_ref, iq - jk, iq >= jk)
            dq_ref[:, sl] = (carry[:, sl] + dq2) if has_next else dq2
            dk_ref[:, sl] = dk2
            dv_ref[:, sl] = dv2

        if has_next:
            @pl.when(j + 1 < nb)
            def _():
                for hp in range(dims.n_heads // 2):
                    sl, dq2, dk2, dv2 = pair(hp, qn_ref, don_ref, lsen_ref, dln_ref, iq - jk + blk, jk >= iq)
                    carry[:, sl] = dq2
                    dk_ref[:, sl] += dk2
                    dv_ref[:, sl] += dv2

    cur = pl.BlockSpec((None, blk, a), lambda b, r, j: (b, j, r))
    nxt = pl.BlockSpec((None, blk, a), lambda b, r, j: (b, jnp.minimum(j + 1, nb - 1), r))
    args, in_specs = [k, v, q, do, lse, delta], [cur] * 6
    if has_next:
        args += [q, do, lse, delta]
        in_specs += [nxt] * 4
    shape = jax.ShapeDtypeStruct((dims.batch_local, dims.seq // dil, dil * a), F32)
    outs = pl.pallas_call(
        body, name=name, grid=(dims.batch_local, dil, nb),
        in_specs=in_specs, out_specs=[cur] * 3, out_shape=[shape] * 3,
        scratch_shapes=[pltpu.VMEM((blk, a), F32)] if has_next else [],
        compiler_params=_params("parallel", "parallel", "arbitrary"),
    )(*[_attn_view(x, dims, dil) for x in args])
    return tuple(o.reshape(t, a) for o in outs)


def _mix_fwd(ya, yb, z, gate_b, dims, *, name, tr=512):
    t, d = ya.shape
    tr = _pick(t, tr, 8)
    first_gate_col = z.shape[1] // d - 2

    def body(ya_ref, yb_ref, ga_ref, gb_ref, ba_ref, bb_ref, o_ref):
        g_a = _sigmoid(ga_ref[...] + ba_ref[...])
        g_b = _sigmoid(gb_ref[...] + bb_ref[...])
        o_ref[...] = (g_a * ya_ref[...] + g_b * yb_ref[...]).astype(BF16)

    return pl.pallas_call(
        body, name=name, grid=(t // tr,),
        in_specs=[_row_spec(tr, d), _row_spec(tr, d), _row_spec(tr, d, first_gate_col),
                  _row_spec(tr, d, first_gate_col + 1), _vec_spec(d, 0), _vec_spec(d, 1)],
        out_specs=_row_spec(tr, d), out_shape=jax.ShapeDtypeStruct((t, d), BF16),
        compiler_params=_params("parallel"),
    )(ya, yb, z, z, gate_b, gate_b)


def _mix_bwd(dmix, ya, yb, z, gate_b, dims, *, name, tr=512):
    t, d = ya.shape
    tr = _pick(t, tr, 8)
    first_gate_col = z.shape[1] // d - 2

    def body(dm_ref, ya_ref, yb_ref, ga_ref, gb_ref, ba_ref, bb_ref, dya_ref, dyb_ref, dz_ref, db_ref):
        dm = dm_ref[...].astype(F32)
        g_a = _sigmoid(ga_ref[...] + ba_ref[...])
        g_b = _sigmoid(gb_ref[...] + bb_ref[...])
        dya_ref[...] = (dm * g_a).astype(BF16)
        dyb_ref[...] = (dm * g_b).astype(BF16)
        dl_a = dm * ya_ref[...] * g_a * (1.0 - g_a)
        dl_b = dm * yb_ref[...] * g_b * (1.0 - g_b)
        dz_ref[:, 0:d] = dl_a.astype(BF16)
        dz_ref[:, d:2 * d] = dl_b.astype(BF16)
        first = pl.program_id(0) == 0
        sums = jnp.concatenate([jnp.sum(dl_a, axis=0, keepdims=True), jnp.sum(dl_b, axis=0, keepdims=True)], axis=1)
        _accumulate(db_ref, sums, first)

    return pl.pallas_call(
        body, name=name, grid=(t // tr,),
        in_specs=[_row_spec(tr, d), _row_spec(tr, d), _row_spec(tr, d), _row_spec(tr, d, first_gate_col),
                  _row_spec(tr, d, first_gate_col + 1), _vec_spec(d, 0), _vec_spec(d, 1)],
        out_specs=[_row_spec(tr, d), _row_spec(tr, d), _row_spec(tr, 2 * d), _vec_spec(2 * d)],
        out_shape=[jax.ShapeDtypeStruct((t, d), BF16)] * 2 + [jax.ShapeDtypeStruct((t, 2 * d), BF16),
                                                              jax.ShapeDtypeStruct((1, 2 * d), F32)],
        compiler_params=_params("arbitrary"),
    )(dmix, ya, yb, z, z, gate_b, gate_b)


def _loss_head(y, target, *, name, tr=512):
    t, d = y.shape
    tr = _pick(t, tr, 8)

    def body(y_ref, t_ref, dy_ref, dyb_ref, loss_ref):
        err = y_ref[...] - t_ref[...]
        dy = err * (1.0 / d)
        dy_ref[...] = dy
        dyb_ref[...] = dy.astype(BF16)
        part = jnp.sum(jnp.sum(err * err, axis=-1, keepdims=True), axis=0, keepdims=True) * (0.5 / d)
        _accumulate(loss_ref, jnp.broadcast_to(part, (8, 128)), pl.program_id(0) == 0)

    return pl.pallas_call(
        body, name=name, grid=(t // tr,),
        in_specs=[_row_spec(tr, d), _row_spec(tr, d)],
        out_specs=[_row_spec(tr, d), _row_spec(tr, d), pl.BlockSpec((8, 128), lambda i: (0, 0))],
        out_shape=[jax.ShapeDtypeStruct((t, d), F32), jax.ShapeDtypeStruct((t, d), BF16),
                   jax.ShapeDtypeStruct((8, 128), F32)],
        compiler_params=_params("arbitrary"),
    )(y, target)


def _adamw(w, grads, m, v, *, name, tr=256):
    r, c = w.shape
    tr = _pick(r, tr, 8)
    ng = len(grads)
    c1 = 1.0 - ADAM_B1 ** ADAM_STEP
    c2 = 1.0 - ADAM_B2 ** ADAM_STEP

    def body(*refs):
        w_ref, g_refs, m_ref, v_ref = refs[0], refs[1:1 + ng], refs[1 + ng], refs[2 + ng]
        g_out, d_out, m_out, v_out = refs[3 + ng:]
        g = g_refs[0][...]
        for extra in g_refs[1:]:
            g = g + extra[...]
        m_new = ADAM_B1 * m_ref[...] + (1.0 - ADAM_B1) * g
        v_new = ADAM_B2 * v_ref[...] + (1.0 - ADAM_B2) * (g * g)
        g_out[...] = g
        m_out[...] = m_new
        v_out[...] = v_new
        d_out[...] = -ADAM_LR * ((m_new / c1) / (jnp.sqrt(v_new / c2) + ADAM_EPS) + ADAM_WD * w_ref[...])

    spec = pl.BlockSpec((tr, c), lambda i: (i, 0))
    return pl.pallas_call(
        body, name=name, grid=(r // tr,),
        in_specs=[spec] * (3 + ng), out_specs=[spec] * 4, out_shape=[jax.ShapeDtypeStruct((r, c), F32)] * 4,
        compiler_params=_params("parallel"),
    )(w, *grads, m, v)


ANY = pl.BlockSpec(memory_space=pl.ANY)
CHIP_PEERS = ((1, 0), (0, 1), (1, 1))


def _place():
    return lax.axis_index("x"), lax.axis_index("y"), lax.axis_index("c")


HBM = pl.BlockSpec(memory_space=pltpu.HBM)
SEM = pl.BlockSpec(memory_space=pltpu.SEMAPHORE)
IN_FLIGHT = pltpu.SideEffectType.DATAFLOW_SIDE_EFFECTING


def _in_hbm(a):
    return pltpu.with_memory_space_constraint(a, pltpu.HBM)


def _cast_to_lands(shards, dtypes, *, name):
    n = len(shards)

    def body(*refs):
        ins, outs, bufs, sems = refs[:n], refs[n:2 * n], refs[2 * n:3 * n], refs[3 * n]
        x, y, _ = _place()
        copies = []
        for a in range(n):
            bufs[a][...] = ins[a][...].astype(dtypes[a])
            cp = pltpu.make_async_copy(bufs[a], outs[a].at[2 * x + y], sems.at[a])
            cp.start()
            copies.append(cp)
        for cp in copies:
            cp.wait()

    return pl.pallas_call(
        body, name=name, in_specs=[pl.BlockSpec(memory_space=pltpu.VMEM)] * n, out_specs=[ANY] * n,
        out_shape=[jax.ShapeDtypeStruct((N_CHIPS,) + s.shape, dt) for s, dt in zip(shards, dtypes)],
        scratch_shapes=[pltpu.VMEM(s.shape, dt) for s, dt in zip(shards, dtypes)] + [pltpu.SemaphoreType.DMA((n,))],
        compiler_params=pltpu.CompilerParams(vmem_limit_bytes=V7X_VMEM_LIMIT_BYTES),
    )(*shards)


def _chip_copy(src, dst, send, recv, flip, place):
    x, y, c = place
    return pltpu.make_async_remote_copy(src_ref=src, dst_ref=dst, send_sem=send, recv_sem=recv,
                                        device_id=(x ^ flip[0], y ^ flip[1], c), device_id_type=MESH)


def _gather_start(lands, after, *, name):
    n = len(lands)

    def body(*refs):
        ins, send, recv, token = refs[:n], refs[n + 1], refs[n + 2], refs[-1]
        place = _place()
        me = 2 * place[0] + place[1]
        for a in range(n):
            for p, flip in enumerate(CHIP_PEERS):
                k = 3 * a + p
                _chip_copy(ins[a].at[me], ins[a].at[me], send.at[k], recv.at[k], flip, place).start()
        token[...] = jnp.zeros_like(token)

    outs = pl.pallas_call(
        body, name=name, in_specs=[HBM] * n + [ANY],
        out_specs=(SEM, SEM, *[HBM] * n, pl.BlockSpec(memory_space=pltpu.VMEM)),
        out_shape=(pltpu.SemaphoreType.DMA((3 * n,)), pltpu.SemaphoreType.DMA((3 * n,)),
                   *[pltpu.HBM(l.shape, l.dtype) for l in lands], jax.ShapeDtypeStruct((8, 128), F32)),
        input_output_aliases={a: 2 + a for a in range(n)},
        compiler_params=pltpu.CompilerParams(has_side_effects=IN_FLIGHT),
    )(*[_in_hbm(l) for l in lands], after)
    return outs[0], outs[1], list(outs[2:2 + n]), outs[-1]


def _gather_wait(send, recv, lands, after, *, name):
    n = len(lands)

    def body(*refs):
        ins, send_ref, recv_ref = refs[:n], refs[n], refs[n + 1]
        place = _place()
        me = 2 * place[0] + place[1]
        for a in range(n):
            for p, flip in enumerate(CHIP_PEERS):
                k = 3 * a + p
                cp = _chip_copy(ins[a].at[me], ins[a].at[me], send_ref.at[k], recv_ref.at[k], flip, place)
                cp.wait_send()
                cp.wait_recv()

    return pl.pallas_call(
        body, name=name, in_specs=[HBM] * n + [SEM, SEM, ANY], out_specs=[HBM] * n,
        out_shape=[pltpu.HBM(l.shape, l.dtype) for l in lands],
        input_output_aliases={a: a for a in range(n)},
        compiler_params=pltpu.CompilerParams(has_side_effects=IN_FLIGHT),
    )(*lands, send, recv, after)


def _scatter_start(grad, *, name):
    def body(g_ref, land_ref, send, recv, g_thru, land_thru, token):
        place = _place()
        for p, flip in enumerate(CHIP_PEERS):
            peer_chip = 2 * (place[0] ^ flip[0]) + (place[1] ^ flip[1])
            _chip_copy(g_ref.at[peer_chip], land_ref.at[p], send.at[p], recv.at[p], flip, place).start()
        token[...] = jnp.zeros_like(token)

    land = lax.empty((3,) + grad.shape[1:], grad.dtype)
    return pl.pallas_call(
        body, name=name, in_specs=[HBM, HBM],
        out_specs=(SEM, SEM, HBM, HBM, pl.BlockSpec(memory_space=pltpu.VMEM)),
        out_shape=(pltpu.SemaphoreType.DMA((3,)), pltpu.SemaphoreType.DMA((3,)), pltpu.HBM(grad.shape, grad.dtype),
                   pltpu.HBM(land.shape, land.dtype), jax.ShapeDtypeStruct((8, 128), F32)),
        input_output_aliases={0: 2, 1: 3},
        compiler_params=pltpu.CompilerParams(has_side_effects=IN_FLIGHT),
    )(_in_hbm(grad), _in_hbm(land))


def _scatter_wait(started, after, *, name):
    n = len(started)

    def body(*refs):
        grads, lands = refs[:n], refs[n:2 * n]
        sends, recvs = refs[2 * n:3 * n], refs[3 * n:4 * n]
        place = _place()
        for a in range(n):
            for p, flip in enumerate(CHIP_PEERS):
                cp = _chip_copy(grads[a].at[0], lands[a].at[p], sends[a].at[p], recvs[a].at[p], flip, place)
                cp.wait_send()
                cp.wait_recv()

    grads, lands = [s[2] for s in started], [s[3] for s in started]
    outs = pl.pallas_call(
        body, name=name, in_specs=[HBM] * (2 * n) + [SEM] * (2 * n) + [ANY], out_specs=[HBM] * (2 * n),
        out_shape=[pltpu.HBM(a.shape, a.dtype) for a in grads + lands],
        input_output_aliases={a: a for a in range(2 * n)},
        compiler_params=pltpu.CompilerParams(has_side_effects=IN_FLIGHT),
    )(*grads, *lands, *[s[0] for s in started], *[s[1] for s in started], after)
    return list(zip(outs[:n], outs[n:]))


def _swap_sibling(arrays, *, name):
    n = len(arrays)

    def body(*refs):
        ins, outs = refs[:n], refs[n:2 * n]
        send, recv = refs[2 * n:]
        x, y, c = _place()
        started = []
        for a in range(n):
            rc = pltpu.make_async_remote_copy(
                src_ref=ins[a], dst_ref=outs[a], send_sem=send.at[a], recv_sem=recv.at[a],
                device_id=(x, y, 1 - c), device_id_type=MESH)
            rc.start()
            started.append(rc)
        for rc in started:
            rc.wait()

    return pl.pallas_call(
        body, name=name, in_specs=[ANY] * n, out_specs=[ANY] * n,
        out_shape=[jax.ShapeDtypeStruct(g.shape, g.dtype) for g in arrays],
        scratch_shapes=[pltpu.SemaphoreType.DMA((n,)), pltpu.SemaphoreType.DMA((n,))],
    )(*arrays)


def _sum_received(grad, land, *, name, tr=256):
    _, r, c = grad.shape
    tr = _pick(r, tr, 8)

    def body(chip_ref, g_ref, l_ref, o_ref):
        o_ref[...] = ((g_ref[...] + l_ref[0].astype(F32)) + l_ref[1].astype(F32)) + l_ref[2].astype(F32)

    chip = (2 * lax.axis_index("x") + lax.axis_index("y")).astype(jnp.int32).reshape(1)
    return pl.pallas_call(
        body, name=name,
        grid_spec=pltpu.PrefetchScalarGridSpec(
            num_scalar_prefetch=1, grid=(r // tr,),
            in_specs=[pl.BlockSpec((None, tr, c), lambda i, chip_ref: (chip_ref[0], i, 0)),
                      pl.BlockSpec((3, tr, c), lambda i, chip_ref: (0, i, 0))],
            out_specs=pl.BlockSpec((tr, c), lambda i, chip_ref: (i, 0))),
        out_shape=jax.ShapeDtypeStruct((r, c), F32), compiler_params=_params("parallel"),
    )(chip, grad, land)


def _allreduce_small(packed, *, name):
    r, d = packed.shape
    n_dev = 8

    def body(src_ref, out_ref, buf, send, recv):
        x, y, c = _place()
        me = 4 * x + 2 * y + c
        started = []
        for p in range(1, n_dev):
            rc = pltpu.make_async_remote_copy(
                src_ref=src_ref, dst_ref=buf.at[me], send_sem=send.at[p - 1], recv_sem=recv.at[p - 1],
                device_id=(x ^ (p >> 2), y ^ ((p >> 1) & 1), c ^ (p & 1)), device_id_type=MESH)
            rc.start()
            started.append(rc)
        buf[me] = src_ref[...]
        for rc in started:
            rc.wait()
        total = buf[0]
        for s in range(1, n_dev):
            total = total + buf[s]
        out_ref[...] = total

    vmem = pl.BlockSpec(memory_space=pltpu.VMEM)
    return pl.pallas_call(
        body, name=name, in_specs=[vmem], out_specs=vmem, out_shape=jax.ShapeDtypeStruct((r, d), F32),
        scratch_shapes=[pltpu.VMEM((n_dev, r, d), F32), pltpu.SemaphoreType.DMA((n_dev - 1,)),
                        pltpu.SemaphoreType.DMA((n_dev - 1,))],
    )(packed)


def _packed_rows(size, d):
    return -(-size // (8 * d)) * 8


def _pack_rows(arrays, d):
    rows = []
    for arr in arrays:
        flat = arr.reshape(-1).astype(F32)
        n = _packed_rows(flat.shape[0], d)
        rows.append(jnp.pad(flat, (0, n * d - flat.shape[0])).reshape(n, d))
    return jnp.concatenate(rows, axis=0)


def _unpack_rows(packed, shapes, d):
    out, row = [], 0
    for shape in shapes:
        size = math.prod(shape)
        n = _packed_rows(size, d)
        out.append(packed[row:row + n].reshape(-1)[:size].reshape(shape))
        row += n
    return out


SMALL = ("norm1_g", "gate_b", "conv_b", "conv_norm_g", "q_norm_g", "k_norm_g", "norm2_g", "ffn_conv_b")
LARGE = ("w_in", "w_conv_out", "w_attn_out", "w_out", "w_up", "w_down")
WEIGHTS = ("norm1_g", "w_in", "gate_b", "conv_w", "conv_b", "conv_norm_g", "w_conv_out", "q_norm_g", "k_norm_g",
           "w_attn_out", "w_out", "norm2_g", "w_up", "ffn_conv_w", "ffn_conv_b", "w_down")


def _head_ones(dims):
    a = dims.n_heads * dims.head_dim
    head = jnp.arange(a, dtype=jnp.int32) // dims.head_dim
    return (head[:, None] == head[None, :]).astype(BF16)


def _after(vec, token):
    return vec if token is None else vec + token[0:1, 0:1]


def _local_step(dims, x, target, small, first_weights, other_weights, send_grad):
    d, f, heads = dims.d_model, dims.d_ff, dims.n_heads
    small = dict(small)
    row = lambda name: small[name].reshape(1, -1)
    ones = _head_ones(dims)
    gq = jnp.tile(row("q_norm_g"), (1, heads))
    gk = jnp.tile(row("k_norm_g"), (1, heads))
    one_shard = lambda w: w.reshape(1, -1, w.shape[-1])

    h = _rmsnorm_fwd(x, row("norm1_g"), name="norm1")
    full = first_weights(h)
    w_in = full["w_in"]
    small["conv_b"] = _after(row("conv_b"), full.get("token"))
    conv_w = jnp.pad(full["conv_w"], ((0, CONV_HALO - dims.conv_width), (0, 0)))
    ffn_w = jnp.pad(full["ffn_conv_w"], ((0, FFN_HALO - dims.ffn_conv_width), (0, 0)))
    z = _mm_nn(h, w_in, out_dtype=F32, name="in_proj")
    a1, a3 = _conv_branch_fwd(z, conv_w, row("conv_b"), row("conv_norm_g"), dims, name="conv_branch")
    qn, kn, vb = _qkv_fwd(z, gq, gk, ones, dims, name="qk_norm")
    state = None
    for gi, dil in enumerate(DILATIONS):
        state = _attn_fwd_group(qn, kn, vb, state, dims, dil, last=gi == len(DILATIONS) - 1, name=f"attn_fwd_d{dil}")
    o, lse = state
    full = other_weights(o)
    w_up = full["w_up"]
    w_co, w_ao, w_o, w_dn = (one_shard(full[k]) for k in ("w_conv_out", "w_attn_out", "w_out", "w_down"))
    ya = _mm_nn(a3, w_co, out_dtype=F32, name="conv_out_proj")
    yb = _mm_nn(o, w_ao, out_dtype=F32, name="attn_out_proj")
    mixed = _mix_fwd(ya, yb, z, row("gate_b"), dims, name="gate_mix")
    x1 = _mm_nn(mixed, w_o, out_dtype=F32, residual=x, name="out_proj")
    h2 = _rmsnorm_fwd(x1, row("norm2_g"), name="norm2")
    up = _mm_nn(h2, w_up, out_dtype=F32, name="up_proj")
    act = _ffn_act_fwd(up, ffn_w, row("ffn_conv_b"), dims, name="ffn_act")
    x2 = _mm_nn(act, w_dn, out_dtype=F32, residual=x1, name="down_proj")
    dy, dy_b, loss = _loss_head(x2, target, name="loss_head")

    grads = {}

    def large(name, g, then):
        grads[name], g_bf16 = g
        small[then] = _after(small[then].reshape(1, -1), send_grad(name, g_bf16))

    large("w_down", _mm_tn(act, dy_b, n_shards=1, name="dw_down"), "ffn_conv_b")
    dact = _mm_nt(dy_b, w_dn, out_dtype=BF16, name="d_act")
    du, dfw, dfb = _ffn_act_bwd(dact, up, ffn_w, row("ffn_conv_b"), dims, name="ffn_act_bwd")
    grads["ffn_conv_w"], grads["ffn_conv_b"] = dfw[:dims.ffn_conv_width], dfb
    dup = _ffn_conv_bwd(du, ffn_w, dims, name="ffn_conv_bwd")
    large("w_up", _mm_tn(h2, dup, n_shards=N_CHIPS, name="dw_up"), "norm2_g")
    dh2 = _mm_nt(dup, w_up, out_dtype=F32, name="d_h2")
    dx1, dx1_b, grads["norm2_g"] = _rmsnorm_bwd(x1, row("norm2_g"), dh2, dy, want_bf16=True, name="norm2_bwd")
    large("w_out", _mm_tn(mixed, dx1_b, n_shards=1, name="dw_out"), "gate_b")
    dmix = _mm_nt(dx1_b, w_o, out_dtype=F32, name="d_mix")
    dya, dyb, dz_gate, grads["gate_b"] = _mix_bwd(dmix, ya, yb, z, row("gate_b"), dims, name="gate_mix_bwd")
    large("w_attn_out", _mm_tn(o, dyb, n_shards=1, name="dw_attn_out"), "conv_norm_g")
    large("w_conv_out", _mm_tn(a3, dya, n_shards=1, name="dw_conv_out"), "conv_norm_g")
    da3 = _mm_nt(dya, w_co, out_dtype=F32, name="d_conv_act")
    da1, grads["conv_norm_g"] = _conv_norm_bwd(da3, a1, row("conv_norm_g"), name="conv_norm_bwd")
    dz_glu, dcw, grads["conv_b"] = _conv_branch_bwd(da1, z, conv_w, dims, name="conv_branch_bwd")
    grads["conv_w"] = dcw[:dims.conv_width]
    do = _mm_nt(dyb, w_ao, out_dtype=BF16, name="d_attn")
    delta = _attn_delta(do, o, ones, dims, name="attn_delta")
    per_group = [_attn_bwd_group(qn, kn, vb, do, lse, delta, dims, dil, name=f"attn_bwd_d{dil}") for dil in DILATIONS]
    dqs, dks, dvs = zip(*per_group)
    dz_qkv, dgq, dgk = _qkv_bwd(z, dqs, dks, dvs, gq, gk, ones, dims, name="qk_norm_bwd")
    grads["q_norm_g"] = dgq.reshape(heads, dims.head_dim).sum(axis=0)
    grads["k_norm_g"] = dgk.reshape(heads, dims.head_dim).sum(axis=0)
    dz = jnp.concatenate([dz_glu, dz_qkv, dz_gate], axis=1)
    large("w_in", _mm_tn(h, dz, n_shards=N_CHIPS, name="dw_in"), "norm1_g")
    dh = _mm_nt(dz, w_in, out_dtype=F32, name="d_h")
    dx, grads["norm1_g"] = _rmsnorm_bwd(x, row("norm1_g"), dh, dx1, want_bf16=False, name="norm1_bwd")
    return loss, dx, grads


def _step(dims, x, target, w, m, v):
    d = dims.d_model
    t = dims.tokens
    sq = lambda a: a.reshape(a.shape[1:])
    w2, m2, v2 = ({k: sq(a) for k, a in grp.items()} for grp in (w, m, v))

    conv_pad = jnp.pad(w2["conv_w"], ((0, CONV_HALO - dims.conv_width), (0, 0)))
    ffn_pad = jnp.pad(w2["ffn_conv_w"], ((0, FFN_HALO - dims.ffn_conv_width), (0, 0)))
    gathered_names = LARGE + ("conv_w", "ffn_conv_w")
    lands = dict(zip(gathered_names, _cast_to_lands([w2[k] for k in LARGE] + [conv_pad, ffn_pad],
                                                   [BF16] * len(LARGE) + [F32, F32], name="cast_weights")))
    first_names = ("w_in", "conv_w", "ffn_conv_w")
    other_names = tuple(k for k in gathered_names if k not in first_names)
    first = _gather_start([lands[k] for k in first_names], x, name="gather_start_first")
    other = []
    cols = lambda g, rows: jnp.moveaxis(g, 0, 1).reshape(g.shape[1], -1)[:rows]

    def first_weights(after):
        got = dict(zip(first_names, _gather_wait(*first[:3], after, name="gather_wait_first")))
        other.extend(_gather_start([lands[k] for k in other_names], got["w_in"], name="gather_start_other"))
        got["conv_w"] = cols(got["conv_w"], dims.conv_width)
        got["ffn_conv_w"] = cols(got["ffn_conv_w"], dims.ffn_conv_width)
        got["token"] = other[3]
        return got

    def other_weights(after):
        return dict(zip(other_names, _gather_wait(*other[:3], after, name="gather_wait_other")))

    started = {}

    def send_grad(name, g):
        send, recv, g_thru, land, token = _scatter_start(g.reshape(N_CHIPS, -1, g.shape[-1]), name=f"scatter_start_{name}")
        started[name] = (send, recv, g_thru, land)
        return token

    small = {k: w2[k] for k in SMALL}
    small["norm1_g"] = _after(small["norm1_g"].reshape(1, -1), first[3])
    loss, dx, grads = _local_step(dims, x.reshape(t, d), target.reshape(t, d), small, first_weights, other_weights, send_grad)

    def finish(names, after, tag):
        arrived = _scatter_wait([started[k] for k in names], after, name=f"scatter_wait_{tag}")
        blocks = [grads[k].reshape(N_CHIPS, -1, grads[k].shape[-1]) for k in names]
        mine = [_sum_received(g, land, name=f"sum_{k}") for k, g, (_, land) in zip(names, blocks, arrived)]
        theirs = _swap_sibling(mine, name=f"swap_sibling_{tag}")
        return {k: _adamw(w2[k], [a, b], m2[k], v2[k], name=f"adamw_{k}") for k, a, b in zip(names, mine, theirs)}

    out = finish([k for k in LARGE if k != "w_in"], dx, "others")

    small_names = SMALL + ("conv_w", "ffn_conv_w")
    packed = _pack_rows([grads[k] for k in small_names] + [loss[0, 0]], d)
    reduced = _allreduce_small(packed, name="allreduce_small")
    shapes = [grads[k].shape for k in small_names] + [()]
    *small_g, loss_total = _unpack_rows(reduced, shapes, d)
    small_g = dict(zip(small_names, small_g))
    chip = 2 * lax.axis_index("x") + lax.axis_index("y")
    for k in ("conv_w", "ffn_conv_w"):
        width = w2[k].shape[1]
        small_g[k] = lax.dynamic_slice_in_dim(small_g[k], chip * width, width, axis=1)

    small_shapes = [w2[k].shape for k in small_names]
    pack = lambda grp: _pack_rows([grp[k] for k in small_names], d)
    results = _adamw(pack(w2), [pack(small_g)], pack(m2), pack(v2), name="adamw_small")
    unpacked = [_unpack_rows(r, small_shapes, d) for r in results]
    for i, k in enumerate(small_names):
        out[k] = tuple(u[i] for u in unpacked)
    out.update(finish(["w_in"], results[1], "w_in"))

    lead =lambda a: a.reshape((1,) + a.shape)
    ordered = [[lead(out[k][j].reshape(w2[k].shape)) for k in WEIGHTS] for j in range(4)]
    return (loss_total, dx.reshape(x.shape), *ordered[0], *ordered[1], *ordered[2], *ordered[3])


def kernel(x, norm1_g, w_in, gate_b, conv_w, conv_b, conv_norm_g, w_conv_out, q_norm_g, k_norm_g, w_attn_out, w_out, norm2_g, w_up, ffn_conv_w, ffn_conv_b, w_down, loss_target, m_norm1_g, m_w_in, m_gate_b, m_conv_w, m_conv_b, m_conv_norm_g, m_w_conv_out, m_q_norm_g, m_k_norm_g, m_w_attn_out, m_w_out, m_norm2_g, m_w_up, m_ffn_conv_w, m_ffn_conv_b, m_w_down, v_norm1_g, v_w_in, v_gate_b, v_conv_w, v_conv_b, v_conv_norm_g, v_w_conv_out, v_q_norm_g, v_k_norm_g, v_w_attn_out, v_w_out, v_norm2_g, v_w_up, v_ffn_conv_w, v_ffn_conv_b, v_w_down):
    w = dict(zip(WEIGHTS, (norm1_g, w_in, gate_b, conv_w, conv_b, conv_norm_g, w_conv_out, q_norm_g, k_norm_g,
                           w_attn_out, w_out, norm2_g, w_up, ffn_conv_w, ffn_conv_b, w_down)))
    m = dict(zip(WEIGHTS, (m_norm1_g, m_w_in, m_gate_b, m_conv_w, m_conv_b, m_conv_norm_g, m_w_conv_out, m_q_norm_g,
                           m_k_norm_g, m_w_attn_out, m_w_out, m_norm2_g, m_w_up, m_ffn_conv_w, m_ffn_conv_b, m_w_down)))
    v = dict(zip(WEIGHTS, (v_norm1_g, v_w_in, v_gate_b, v_conv_w, v_conv_b, v_conv_norm_g, v_w_conv_out, v_q_norm_g,
                           v_k_norm_g, v_w_attn_out, v_w_out, v_norm2_g, v_w_up, v_ffn_conv_w, v_ffn_conv_b, v_w_down)))
    dims = Dims(d_model=x.shape[-1], batch_local=x.shape[0], seq=x.shape[1], d_ff=w_down.shape[1] * N_CHIPS)
    return _step(dims, x, loss_target, w, m, v)
```

```python
import functools
import math
from typing import NamedTuple

import jax
import jax.numpy as jnp
from jax import lax
from jax.experimental import pallas as pl
from jax.experimental.pallas import tpu as pltpu

F32 = jnp.float32
BF16 = jnp.bfloat16

RMS_EPS = 1e-6
MASKED_SCORE = -1e30
ATTN_BLOCK = 128
DILATIONS = (1, 4, 16)
CONV_HALO = 32
FFN_HALO = 8
ADAM_LR, ADAM_B1, ADAM_B2, ADAM_EPS, ADAM_WD, ADAM_STEP = 0.001, 0.9, 0.999, 1e-08, 0.01, 10
V7X_VMEM_LIMIT_BYTES = 56 * 2 ** 20
N_CHIPS = 4
MESH = pl.DeviceIdType.MESH


class Dims(NamedTuple):
    d_model: int = 1024
    n_heads: int = 16
    head_dim: int = 64
    d_ff: int = 2816
    seq: int = 2048
    batch_local: int = 2
    conv_width: int = 31
    ffn_conv_width: int = 3

    @property
    def tokens(self):
        return self.seq * self.batch_local


def _params(*semantics):
    return pltpu.CompilerParams(dimension_semantics=semantics, vmem_limit_bytes=V7X_VMEM_LIMIT_BYTES)


ANY = pl.BlockSpec(memory_space=pl.ANY)


def _ordered(body, n_inputs, after):
    after = [] if after is None else list(after) if isinstance(after, (list, tuple)) else [after]
    if not after:
        return body, [], []

    def wrapped(*refs):
        return body(*refs[:n_inputs], *refs[n_inputs + len(after):])

    return wrapped, [ANY] * len(after), after


def _pick(n, target, mult=128):
    if n <= target:
        return n
    best = None
    for t in range(mult, target + 1, mult):
        if n % t == 0:
            best = t
    assert best is not None, (n, target, mult)
    return best


def _sigmoid(v):
    return 1.0 / (1.0 + jnp.exp(-v))


def _mm_nn(a, w, *, out_dtype, name, residual=None, after=None, tm=1024, tn=1408, tk=2816):
    m, k = a.shape
    nsh, k2, c = w.shape
    assert k == k2 and a.dtype == BF16 and w.dtype == BF16
    n = nsh * c
    tm, tn, tk = _pick(m, tm, 8), _pick(c, tn), _pick(k, tk)
    nk, cpn = k // tk, c // tn

    def body(*refs):
        if residual is None:
            a_ref, w_ref, o_ref, acc = refs
        else:
            a_ref, w_ref, r_ref, o_ref, acc = refs
        prod = jnp.dot(a_ref[...], w_ref[...], preferred_element_type=F32)

        def finish(total):
            if residual is not None:
                total = total + r_ref[...]
            o_ref[...] = total.astype(out_dtype)

        if nk == 1:
            finish(prod)
        else:
            kk = pl.program_id(2)

            @pl.when(kk == 0)
            def _():
                acc[...] = prod

            @pl.when(kk > 0)
            def _():
                acc[...] += prod

            @pl.when(kk == nk - 1)
            def _():
                finish(acc[...])

    in_specs = [pl.BlockSpec((tm, tk), lambda i, j, kk: (i, kk)),
                pl.BlockSpec((None, tk, tn), lambda i, j, kk: (j // cpn, kk, j % cpn))]
    args = [a, w]
    if residual is not None:
        in_specs.append(pl.BlockSpec((tm, tn), lambda i, j, kk: (i, j)))
        args.append(residual)
    body, more_specs, more_args = _ordered(body, len(args), after)
    return pl.pallas_call(
        body, name=name, grid=(m // tm, n // tn, nk),
        in_specs=in_specs + more_specs, out_specs=pl.BlockSpec((tm, tn), lambda i, j, kk: (i, j)),
        out_shape=jax.ShapeDtypeStruct((m, n), out_dtype),
        scratch_shapes=[pltpu.VMEM((tm, tn) if nk > 1 else (8, 128), F32)],
        compiler_params=_params("parallel", "parallel", "arbitrary"),
    )(*args, *more_args)


def _mm_nt(a, w, *, out_dtype, name, after=None, tm=1024, tn=1408, tk=1792):
    m, k = a.shape
    nsh, r, c = w.shape
    assert k == nsh * c and a.dtype == BF16 and w.dtype == BF16
    tm, tn, tk = _pick(m, tm, 8), _pick(r, tn), _pick(c, tk)
    nk, cpk = k // tk, c // tk

    def body(a_ref, w_ref, o_ref, acc):
        prod = lax.dot_general(a_ref[...], w_ref[...], (((1,), (1,)), ((), ())), preferred_element_type=F32)
        if nk == 1:
            o_ref[...] = prod.astype(out_dtype)
        else:
            kk = pl.program_id(2)

            @pl.when(kk == 0)
            def _():
                acc[...] = prod

            @pl.when(kk > 0)
            def _():
                acc[...] += prod

            @pl.when(kk == nk - 1)
            def _():
                o_ref[...] = acc[...].astype(out_dtype)

    body, more_specs, more_args = _ordered(body, 2, after)
    return pl.pallas_call(
        body, name=name, grid=(m // tm, r // tn, nk),
        in_specs=[pl.BlockSpec((tm, tk), lambda i, j, kk: (i, kk)),
                  pl.BlockSpec((None, tn, tk), lambda i, j, kk: (kk // cpk, j, kk % cpk))] + more_specs,
        out_specs=pl.BlockSpec((tm, tn), lambda i, j, kk: (i, j)),
        out_shape=jax.ShapeDtypeStruct((m, r), out_dtype),
        scratch_shapes=[pltpu.VMEM((tm, tn) if nk > 1 else (8, 128), F32)],
        compiler_params=_params("parallel", "parallel", "arbitrary"),
    )(a, w, *more_args)


def _mm_tn(a, b, *, n_shards, name, tm=1408, tn=1408, tk=512):
    t, m = a.shape
    t2, n = b.shape
    assert t == t2 and a.dtype == BF16 and b.dtype == BF16
    c = n // n_shards
    tm, tn, tk = _pick(m, tm), _pick(c, tn), _pick(t, tk, 8)
    nk, cpn = t // tk, c // tn

    def body(a_ref, b_ref, o_ref, ob_ref, acc):
        kk = pl.program_id(2)
        prod = lax.dot_general(a_ref[...], b_ref[...], (((0,), (0,)), ((), ())), preferred_element_type=F32)

        @pl.when(kk == 0)
        def _():
            acc[...] = prod

        @pl.when(kk > 0)
        def _():
            acc[...] += prod

        @pl.when(kk == nk - 1)
        def _():
            total = acc[...]
            o_ref[...] = total
            ob_ref[...] = total.astype(BF16)

    out_spec = pl.BlockSpec((None, tm, tn), lambda i, j, kk: (j // cpn, i, j % cpn))
    return pl.pallas_call(
        body, name=name, grid=(m // tm, n // tn, nk),
        in_specs=[pl.BlockSpec((tk, tm), lambda i, j, kk: (kk, i)),
                  pl.BlockSpec((tk, tn), lambda i, j, kk: (kk, j))],
        out_specs=[out_spec, out_spec],
        out_shape=[jax.ShapeDtypeStruct((n_shards, m, c), F32), jax.ShapeDtypeStruct((n_shards, m, c), BF16)],
        scratch_shapes=[pltpu.VMEM((tm, tn), F32)],
        compiler_params=_params("parallel", "parallel", "arbitrary"),
    )(a, b)


def _row_spec(tr, width, col=0):
    return pl.BlockSpec((tr, width), lambda i, col=col: (i, col))


def _vec_spec(width, col=0):
    return pl.BlockSpec((1, width), lambda i, col=col: (0, col))


def _accumulate(ref, value, first):
    @pl.when(first)
    def _():
        ref[...] = value

    @pl.when(jnp.logical_not(first))
    def _():
        ref[...] += value


def _rmsnorm_fwd(x, g, *, name, tr=512):
    t, d = x.shape
    tr = _pick(t, tr, 8)

    def body(x_ref, g_ref, o_ref):
        xv = x_ref[...]
        r = lax.rsqrt(jnp.mean(xv * xv, axis=-1, keepdims=True) + RMS_EPS)
        o_ref[...] = (xv * r * g_ref[...]).astype(BF16)

    return pl.pallas_call(
        body, name=name, grid=(t // tr,),
        in_specs=[_row_spec(tr, d), _vec_spec(d)], out_specs=_row_spec(tr, d),
        out_shape=jax.ShapeDtypeStruct((t, d), BF16), compiler_params=_params("parallel"),
    )(x, g)


def _rmsnorm_bwd(x, g, dy, dres, *, name, want_bf16, tr=512):
    t, d = x.shape
    tr = _pick(t, tr, 8)

    def body(x_ref, g_ref, dy_ref, dres_ref, *outs):
        dx_ref, dg_ref = outs[0], outs[-1]
        xv, dyv = x_ref[...], dy_ref[...].astype(F32)
        r = lax.rsqrt(jnp.mean(xv * xv, axis=-1, keepdims=True) + RMS_EPS)
        gy = dyv * g_ref[...]
        dx = dres_ref[...] + r * gy - xv * (r * r * r) * jnp.mean(xv * gy, axis=-1, keepdims=True)
        dx_ref[...] = dx
        if want_bf16:
            outs[1][...] = dx.astype(BF16)
        _accumulate(dg_ref, jnp.sum(dyv * xv * r, axis=0, keepdims=True), pl.program_id(0) == 0)

    out_shape = [jax.ShapeDtypeStruct((t, d), F32)]
    out_specs = [_row_spec(tr, d)]
    if want_bf16:
        out_shape.append(jax.ShapeDtypeStruct((t, d), BF16))
        out_specs.append(_row_spec(tr, d))
    out_shape.append(jax.ShapeDtypeStruct((1, d), F32))
    out_specs.append(_vec_spec(d))
    return pl.pallas_call(
        body, name=name, grid=(t // tr,),
        in_specs=[_row_spec(tr, d), _vec_spec(d), _row_spec(tr, d), _row_spec(tr, d)],
        out_specs=out_specs, out_shape=out_shape, compiler_params=_params("arbitrary"),
    )(x, g, dy, dres)


def _head_mean(v, ones_ref, head_dim):
    hi = v.astype(BF16)
    lo = (v - hi.astype(F32)).astype(BF16)
    e = ones_ref[...]
    total = jnp.dot(hi, e, preferred_element_type=F32) + jnp.dot(lo, e, preferred_element_type=F32)
    return total * (1.0 / head_dim)


def _qkv_fwd(z, gq, gk, head_ones, dims, *, name, tr=256):
    t = z.shape[0]
    a = dims.n_heads * dims.head_dim
    tr = _pick(t, tr, 8)
    q_scale = dims.head_dim ** -0.5

    def body(q_ref, k_ref, v_ref, gq_ref, gk_ref, e_ref, qo_ref, ko_ref, vo_ref):
        qv, kv = q_ref[...], k_ref[...]
        rq = lax.rsqrt(_head_mean(qv * qv, e_ref, dims.head_dim) + RMS_EPS)
        rk = lax.rsqrt(_head_mean(kv * kv, e_ref, dims.head_dim) + RMS_EPS)
        qo_ref[...] = (qv * rq * gq_ref[...] * q_scale).astype(BF16)
        ko_ref[...] = (kv * rk * gk_ref[...]).astype(BF16)
        vo_ref[...] = v_ref[...].astype(BF16)

    return pl.pallas_call(
        body, name=name, grid=(t // tr,),
        in_specs=[_row_spec(tr, a, 2), _row_spec(tr, a, 3), _row_spec(tr, a, 4), _vec_spec(a), _vec_spec(a),
                  pl.BlockSpec((a, a), lambda i: (0, 0))],
        out_specs=[_row_spec(tr, a)] * 3, out_shape=[jax.ShapeDtypeStruct((t, a), BF16)] * 3,
        compiler_params=_params("parallel"),
    )(z, z, z, gq, gk, head_ones)


def _qkv_bwd(z, dqs, dks, dvs, gq, gk, head_ones, dims, *, name, tr=256):
    t = z.shape[0]
    a = dims.n_heads * dims.head_dim
    tr = _pick(t, tr, 8)
    q_scale = dims.head_dim ** -0.5
    ng = len(dqs)

    def body(*refs):
        q_ref, k_ref = refs[:2]
        dq_refs, dk_refs, dv_refs = refs[2:2 + ng], refs[2 + ng:2 + 2 * ng], refs[2 + 2 * ng:2 + 3 * ng]
        gq_ref, gk_ref, e_ref = refs[2 + 3 * ng:5 + 3 * ng]
        dz_ref, dgq_ref, dgk_ref = refs[5 + 3 * ng:]
        first = pl.program_id(0) == 0

        def norm_bwd(x_ref, d_refs, g_ref, scale, col, dg_ref):
            xv = x_ref[...]
            dy = sum(r[...] for r in d_refs) * scale
            r = lax.rsqrt(_head_mean(xv * xv, e_ref, dims.head_dim) + RMS_EPS)
            gy = dy * g_ref[...]
            dx = r * gy - xv * (r * r * r) * _head_mean(xv * gy, e_ref, dims.head_dim)
            dz_ref[:, col * a:(col + 1) * a] = dx.astype(BF16)
            _accumulate(dg_ref, jnp.sum(dy * xv * r, axis=0, keepdims=True), first)

        norm_bwd(q_ref, dq_refs, gq_ref, q_scale, 0, dgq_ref)
        norm_bwd(k_ref, dk_refs, gk_ref, 1.0, 1, dgk_ref)
        dz_ref[:, 2 * a:3 * a] = sum(r[...] for r in dv_refs).astype(BF16)

    in_specs = ([_row_spec(tr, a, 2), _row_spec(tr, a, 3)] + [_row_spec(tr, a)] * (3 * ng)
                + [_vec_spec(a), _vec_spec(a), pl.BlockSpec((a, a), lambda i: (0, 0))])
    return pl.pallas_call(
        body, name=name, grid=(t // tr,), in_specs=in_specs,
        out_specs=[_row_spec(tr, 3 * a), _vec_spec(a), _vec_spec(a)],
        out_shape=[jax.ShapeDtypeStruct((t, 3 * a), BF16)] + [jax.ShapeDtypeStruct((1, a), F32)] * 2,
        compiler_params=_params("arbitrary"),
    )(z, z, *dqs, *dks, *dvs, gq, gk, head_ones)


CONV_ROWS = 16


def _seq_specs(dims, ts, width, halo, col, *, nxt=False):
    nst, per = dims.seq // ts, ts // halo
    last = dims.tokens // halo - 1
    cur = pl.BlockSpec((ts, width), lambda b, i: (b * nst + i, col))
    if nxt:
        edge = pl.BlockSpec((halo, width), lambda b, i: (jnp.minimum((b * nst + i + 1) * per, last), col))
    else:
        edge = pl.BlockSpec((halo, width), lambda b, i: (jnp.maximum((b * nst + i) * per - 1, 0), col))
    return cur, edge


def _conv_branch_fwd(z, w, b, g, dims, *, name, ts=128):
    t, c, kw = z.shape[0], dims.d_model, dims.conv_width
    base = CONV_HALO - (kw - 1)

    def body(av_ref, hv_ref, ag_ref, hg_ref, w_ref, b_ref, g_ref, a1_ref, a3_ref, buf):
        i = pl.program_id(1)
        buf[CONV_HALO:, :] = av_ref[...] * _sigmoid(ag_ref[...])
        buf[0:CONV_HALO, :] = jnp.where(i > 0, hv_ref[...] * _sigmoid(hg_ref[...]), 0.0)
        for r0 in range(0, ts, CONV_ROWS):
            acc = jnp.broadcast_to(b_ref[...], (CONV_ROWS, c))
            for k in range(kw):
                acc = acc + w_ref[k:k + 1, :] * buf[pl.ds(r0 + base + k, CONV_ROWS), :]
            a1_ref[r0:r0 + CONV_ROWS, :] = acc
            a2 = acc * lax.rsqrt(jnp.mean(acc * acc, axis=-1, keepdims=True) + RMS_EPS) * g_ref[...]
            a3_ref[r0:r0 + CONV_ROWS, :] = (a2 * _sigmoid(a2)).astype(BF16)

    vec = pl.BlockSpec((1, c), lambda b, i: (0, 0))
    out = pl.BlockSpec((ts, c), lambda b, i: (b * (dims.seq // ts) + i, 0))
    return pl.pallas_call(
        body, name=name, grid=(dims.batch_local, dims.seq // ts),
        in_specs=[*_seq_specs(dims, ts, c, CONV_HALO, 0), *_seq_specs(dims, ts, c, CONV_HALO, 1),
                  pl.BlockSpec((CONV_HALO, c), lambda b, i: (0, 0)), vec, vec],
        out_specs=[out, out],
        out_shape=[jax.ShapeDtypeStruct((t, c), F32), jax.ShapeDtypeStruct((t, c), BF16)],
        scratch_shapes=[pltpu.VMEM((CONV_HALO + ts, c), F32)],
        compiler_params=_params("parallel", "parallel"),
    )(z, z, z, z, w, b, g)


def _conv_norm_bwd(da3, a1, g, *, name, tr=256):
    t, c = a1.shape
    tr = _pick(t, tr, 8)

    def body(d_ref, a_ref, g_ref, o_ref, dg_ref):
        a1v, gv = a_ref[...], g_ref[...]
        r = lax.rsqrt(jnp.mean(a1v * a1v, axis=-1, keepdims=True) + RMS_EPS)
        a2 = a1v * r * gv
        sg = _sigmoid(a2)
        da2 = d_ref[...].astype(F32) * sg * (1.0 + a2 * (1.0 - sg))
        gy = da2 * gv
        o_ref[...] = r * gy - a1v * (r * r * r) * jnp.mean(a1v * gy, axis=-1, keepdims=True)
        _accumulate(dg_ref, jnp.sum(da2 * a1v * r, axis=0, keepdims=True), pl.program_id(0) == 0)

    return pl.pallas_call(
        body, name=name, grid=(t // tr,),
        in_specs=[_row_spec(tr, c), _row_spec(tr, c), _vec_spec(c)],
        out_specs=[_row_spec(tr, c), _vec_spec(c)],
        out_shape=[jax.ShapeDtypeStruct((t, c), F32), jax.ShapeDtypeStruct((1, c), F32)],
        compiler_params=_params("arbitrary"),
    )(da3, a1, g)


def _conv_branch_bwd(da1, z, w, dims, *, name, ts=128):
    t, c, kw = z.shape[0], dims.d_model, dims.conv_width
    nst = dims.seq // ts
    base = CONV_HALO - (kw - 1)

    def body(d_ref, dn_ref, av_ref, hv_ref, ag_ref, hg_ref, w_ref, dz_ref, dw_ref, db_ref, abuf, dbuf):
        i = pl.program_id(1)
        first = jnp.logical_and(pl.program_id(0) == 0, i == 0)
        abuf[CONV_HALO:, :] = av_ref[...] * _sigmoid(ag_ref[...])
        abuf[0:CONV_HALO, :] = jnp.where(i > 0, hv_ref[...] * _sigmoid(hg_ref[...]), 0.0)
        d1 = d_ref[...]
        dbuf[0:ts, :] = d1
        dbuf[ts:, :] = jnp.where(i < nst - 1, dn_ref[...], 0.0)

        @pl.when(first)
        def _():
            dw_ref[...] = jnp.zeros_like(dw_ref)
            db_ref[...] = jnp.zeros_like(db_ref)

        db_ref[...] += jnp.sum(d1, axis=0, keepdims=True)
        for k in range(kw):
            dw_ref[k:k + 1, :] += jnp.sum(d1 * abuf[pl.ds(base + k, ts), :], axis=0, keepdims=True)
        for r0 in range(0, ts, CONV_ROWS):
            acc = jnp.zeros((CONV_ROWS, c), F32)
            for k in range(kw):
                acc = acc + w_ref[k:k + 1, :] * dbuf[pl.ds(r0 + (kw - 1) - k, CONV_ROWS), :]
            av = av_ref[r0:r0 + CONV_ROWS, :]
            sg = _sigmoid(ag_ref[r0:r0 + CONV_ROWS, :])
            dz_ref[r0:r0 + CONV_ROWS, 0:c] = (acc * sg).astype(BF16)
            dz_ref[r0:r0 + CONV_ROWS, c:2 * c] = (acc * av * sg * (1.0 - sg)).astype(BF16)

    cur, nxt = _seq_specs(dims, ts, c, CONV_HALO, 0, nxt=True)
    return pl.pallas_call(
        body, name=name, grid=(dims.batch_local, nst),
        in_specs=[cur, nxt, *_seq_specs(dims, ts, c, CONV_HALO, 0), *_seq_specs(dims, ts, c, CONV_HALO, 1),
                  pl.BlockSpec((CONV_HALO, c), lambda b, i: (0, 0))],
        out_specs=[pl.BlockSpec((ts, 2 * c), lambda b, i: (b * nst + i, 0)),
                   pl.BlockSpec((CONV_HALO, c), lambda b, i: (0, 0)), pl.BlockSpec((1, c), lambda b, i: (0, 0))],
        out_shape=[jax.ShapeDtypeStruct((t, 2 * c), BF16), jax.ShapeDtypeStruct((CONV_HALO, c), F32),
                   jax.ShapeDtypeStruct((1, c), F32)],
        scratch_shapes=[pltpu.VMEM((CONV_HALO + ts, c), F32), pltpu.VMEM((ts + CONV_HALO, c), F32)],
        compiler_params=_params("arbitrary", "arbitrary"),
    )(da1, da1, z, z, z, z, w)


FFN_ROWS = 16
FFN_COLS = 256


def _ffn_chunks(ts, f):
    cw = _pick(f, FFN_COLS)
    return [(r0, c0, cw) for r0 in range(0, ts, FFN_ROWS) for c0 in range(0, f, cw)]


def _ffn_conv(buf, w_ref, b_ref, r0, cols, kw):
    base = FFN_HALO - (kw - 1)
    u = jnp.broadcast_to(b_ref[:, cols], (FFN_ROWS, cols.stop - cols.start))
    for k in range(kw):
        u = u + w_ref[k:k + 1, cols] * buf[pl.ds(r0 + base + k, FFN_ROWS), cols]
    return u


def _ffn_act_fwd(up, w, b, dims, *, name, ts=128):
    t, f, kw = up.shape[0], dims.d_ff, dims.ffn_conv_width

    def body(up_ref, h_ref, w_ref, b_ref, o_ref, buf):
        buf[FFN_HALO:, :] = up_ref[...]
        buf[0:FFN_HALO, :] = jnp.where(pl.program_id(1) > 0, h_ref[...], 0.0)
        for r0, c0, cw in _ffn_chunks(ts, f):
            uv = _ffn_conv(buf, w_ref, b_ref, r0, slice(c0, c0 + cw), kw)
            ug = _ffn_conv(buf, w_ref, b_ref, r0, slice(f + c0, f + c0 + cw), kw)
            o_ref[r0:r0 + FFN_ROWS, c0:c0 + cw] = (ug * _sigmoid(ug) * uv).astype(BF16)

    full = lambda rows: pl.BlockSpec((rows, 2 * f), lambda b_, i: (0, 0))
    return pl.pallas_call(
        body, name=name, grid=(dims.batch_local, dims.seq // ts),
        in_specs=[*_seq_specs(dims, ts, 2 * f, FFN_HALO, 0), full(FFN_HALO), full(1)],
        out_specs=pl.BlockSpec((ts, f), lambda b_, i: (b_ * (dims.seq // ts) + i, 0)),
        out_shape=jax.ShapeDtypeStruct((t, f), BF16),
        scratch_shapes=[pltpu.VMEM((FFN_HALO + ts, 2 * f), F32)],
        compiler_params=_params("parallel", "parallel"),
    )(up, up, w, b)


def _ffn_act_bwd(dact, up, w, b, dims, *, name, ts=128):
    t, f, kw = up.shape[0], dims.d_ff, dims.ffn_conv_width
    base = FFN_HALO - (kw - 1)

    def body(d_ref, up_ref, h_ref, w_ref, b_ref, du_ref, dw_ref, db_ref, buf):
        i = pl.program_id(1)
        first = jnp.logical_and(pl.program_id(0) == 0, i == 0)
        buf[FFN_HALO:, :] = up_ref[...]
        buf[0:FFN_HALO, :] = jnp.where(i > 0, h_ref[...], 0.0)
        for r0, c0, cw in _ffn_chunks(ts, f):
            vcols, gcols = slice(c0, c0 + cw), slice(f + c0, f + c0 + cw)
            uv = _ffn_conv(buf, w_ref, b_ref, r0, vcols, kw)
            ug = _ffn_conv(buf, w_ref, b_ref, r0, gcols, kw)
            d = d_ref[r0:r0 + FFN_ROWS, vcols].astype(F32)
            sg = _sigmoid(ug)
            du_ref[r0:r0 + FFN_ROWS, vcols] = d * ug * sg
            du_ref[r0:r0 + FFN_ROWS, gcols] = d * uv * sg * (1.0 + ug * (1.0 - sg))

        @pl.when(first)
        def _():
            dw_ref[...] = jnp.zeros_like(dw_ref)
            db_ref[...] = jnp.zeros_like(db_ref)

        du = du_ref[...]
        db_ref[...] += jnp.sum(du, axis=0, keepdims=True)
        for k in range(kw):
            dw_ref[k:k + 1, :] += jnp.sum(du * buf[pl.ds(base + k, ts), :], axis=0, keepdims=True)

    nst = dims.seq // ts
    full = lambda rows: pl.BlockSpec((rows, 2 * f), lambda b_, i: (0, 0))
    return pl.pallas_call(
        body, name=name, grid=(dims.batch_local, nst),
        in_specs=[pl.BlockSpec((ts, f), lambda b_, i: (b_ * nst + i, 0)),
                  *_seq_specs(dims, ts, 2 * f, FFN_HALO, 0), full(FFN_HALO), full(1)],
        out_specs=[pl.BlockSpec((ts, 2 * f), lambda b_, i: (b_ * nst + i, 0)), full(FFN_HALO), full(1)],
        out_shape=[jax.ShapeDtypeStruct((t, 2 * f), F32), jax.ShapeDtypeStruct((FFN_HALO, 2 * f), F32),
                   jax.ShapeDtypeStruct((1, 2 * f), F32)],
        scratch_shapes=[pltpu.VMEM((FFN_HALO + ts, 2 * f), F32)],
        compiler_params=_params("arbitrary", "arbitrary"),
    )(dact, up, up, w, b)


def _ffn_conv_bwd(du, w, dims, *, name, ts=128):
    t, f2 = du.shape
    kw = dims.ffn_conv_width
    nst = dims.seq // ts

    def body(d_ref, dn_ref, w_ref, o_ref, buf):
        buf[0:ts, :] = d_ref[...]
        buf[ts:, :] = jnp.where(pl.program_id(1) < nst - 1, dn_ref[...], 0.0)
        for r0, c0, cw in _ffn_chunks(ts, f2):
            cols = slice(c0, c0 + cw)
            acc = jnp.zeros((FFN_ROWS, cw), F32)
            for k in range(kw):
                acc = acc + w_ref[k:k + 1, cols] * buf[pl.ds(r0 + (kw - 1) - k, FFN_ROWS), cols]
            o_ref[r0:r0 + FFN_ROWS, cols] = acc.astype(BF16)

    return pl.pallas_call(
        body, name=name, grid=(dims.batch_local, nst),
        in_specs=[*_seq_specs(dims, ts, f2, FFN_HALO, 0, nxt=True), pl.BlockSpec((FFN_HALO, f2), lambda b_, i: (0, 0))],
        out_specs=pl.BlockSpec((ts, f2), lambda b_, i: (b_ * nst + i, 0)),
        out_shape=jax.ShapeDtypeStruct((t, f2), BF16),
        scratch_shapes=[pltpu.VMEM((ts + FFN_HALO, f2), F32)],
        compiler_params=_params("parallel", "parallel"),
    )(du, du, w)


def _alibi_slope(h, n_heads):
    return 2.0 ** (-8.0 * (h + 1) / n_heads)


def _dot_nt(a, b):
    return lax.dot_general(a, b, (((1,), (1,)), ((), ())), preferred_element_type=F32)


def _dot_tn(a, b):
    return lax.dot_general(a, b, (((0,), (0,)), ((), ())), preferred_element_type=F32)


def _attn_view(x, dims, dil):
    return x.reshape(dims.batch_local, dims.seq // dil, dil * x.shape[-1])


def _attn_fwd_group(q, k, v, state, dims, dil, *, last, name):
    t, a = q.shape
    assert 2 * dims.head_dim == 128 and dims.n_heads % 2 == 0
    blk, hd = ATTN_BLOCK, dims.head_dim
    nb = dims.seq // dil // blk
    has_prev = nb > 1
    nkeys = 2 * blk if has_prev else blk

    def body(*refs):
        it = iter(refs)
        q_ref, kc_ref, vc_ref = next(it), next(it), next(it)
        kp_ref, vp_ref = (next(it), next(it)) if has_prev else (None, None)
        m_in, l_in, acc_in = (next(it), next(it), next(it)) if state is not None else (None, None, None)
        outs = list(it)
        iq = lax.broadcasted_iota(jnp.int32, (blk, nkeys), 0)
        jk = lax.broadcasted_iota(jnp.int32, (blk, nkeys), 1)
        if has_prev:
            steps = iq + blk - jk
            valid = (steps >= 0) & (steps <= blk) & ((jk >= blk) | (pl.program_id(2) > 0))
        else:
            steps = iq - jk
            valid = steps >= 0
        dist = steps.astype(F32) * float(dil)
        low = lax.broadcasted_iota(jnp.int32, (blk, 2 * hd), 1) < hd
        for hp in range(dims.n_heads // 2):
            sl = slice(2 * hd * hp, 2 * hd * (hp + 1))
            q2 = q_ref[:, sl]
            if has_prev:
                kcat = jnp.concatenate([kp_ref[:, sl], kc_ref[:, sl]], axis=0)
                vcat = jnp.concatenate([vp_ref[:, sl], vc_ref[:, sl]], axis=0)
            else:
                kcat, vcat = kc_ref[:, sl], vc_ref[:, sl]
            halves = []
            for half in range(2):
                col = 2 * hd * hp + hd * half
                qh = jnp.where(low if half == 0 else jnp.logical_not(low), q2, jnp.zeros_like(q2))
                sc = _dot_nt(qh, kcat) - _alibi_slope(2 * hp + half, dims.n_heads) * dist
                sc = jnp.where(valid, sc, MASKED_SCORE)
                row_max = jnp.max(sc, axis=-1, keepdims=True)
                if state is None:
                    m_new = row_max
                    p = jnp.exp(sc - m_new)
                    alpha = None
                    l_new = jnp.sum(p, axis=-1, keepdims=True)
                else:
                    m_old = m_in[:, col:col + 1]
                    m_new = jnp.maximum(m_old, row_max)
                    p = jnp.exp(sc - m_new)
                    alpha = jnp.exp(m_old - m_new)
                    l_new = alpha * l_in[:, col:col + 1] + jnp.sum(p, axis=-1, keepdims=True)
                pv = jnp.dot(p.astype(BF16), vcat, preferred_element_type=F32)
                halves.append((m_new, l_new, alpha, pv))
            (m_a, l_a, al_a, pv_a), (m_b, l_b, al_b, pv_b) = halves
            if state is None:
                acc = jnp.where(low, pv_a, pv_b)
            else:
                old = acc_in[:, sl]
                acc = jnp.where(low, al_a * old + pv_a, al_b * old + pv_b)
            m2 = jnp.where(low, m_a, m_b)
            l2 = jnp.where(low, l_a, l_b)
            if last:
                outs[0][:, sl] = (acc / l2).astype(BF16)
                outs[1][:, sl] = m2 + jnp.log(l2)
            else:
                outs[0][:, sl] = m2
                outs[1][:, sl] = l2
                outs[2][:, sl] = acc

    cur = pl.BlockSpec((None, blk, a), lambda b, r, i: (b, i, r))
    prev = pl.BlockSpec((None, blk, a), lambda b, r, i: (b, jnp.maximum(i - 1, 0), r))
    args, in_specs = [q, k, v], [cur, cur, cur]
    if has_prev:
        args += [k, v]
        in_specs += [prev, prev]
    if state is not None:
        args += list(state)
        in_specs += [cur] * 3
    shape = lambda dt: jax.ShapeDtypeStruct((dims.batch_local, dims.seq // dil, dil * a), dt)
    out_shape = [shape(BF16), shape(F32)] if last else [shape(F32)] * 3
    outs = pl.pallas_call(
        body, name=name, grid=(dims.batch_local, dil, nb),
        in_specs=in_specs, out_specs=[cur] * len(out_shape), out_shape=out_shape,
        compiler_params=_params("parallel", "parallel", "parallel"),
    )(*[_attn_view(x, dims, dil) for x in args])
    return tuple(o.reshape(t, a) for o in outs)


def _attn_delta(do, o, head_ones, dims, *, name, tr=512):
    t, a = o.shape
    tr = _pick(t, tr, 8)

    def body(do_ref, o_ref, e_ref, d_ref):
        prod = do_ref[...].astype(F32) * o_ref[...].astype(F32)
        d_ref[...] = _head_mean(prod, e_ref, dims.head_dim) * float(dims.head_dim)

    return pl.pallas_call(
        body, name=name, grid=(t // tr,),
        in_specs=[_row_spec(tr, a), _row_spec(tr, a), pl.BlockSpec((a, a), lambda i: (0, 0))],
        out_specs=_row_spec(tr, a), out_shape=jax.ShapeDtypeStruct((t, a), F32),
        compiler_params=_params("parallel"),
    )(do, o, head_ones)


def _attn_bwd_group(q, k, v, do, lse, delta, dims, dil, *, name):
    t, a = q.shape
    blk, hd = ATTN_BLOCK, dims.head_dim
    nb = dims.seq // dil // blk
    has_next = nb > 1

    def body(*refs):
        k_ref, v_ref, q_ref, do_ref, lse_ref, dl_ref = refs[:6]
        if has_next:
            qn_ref, don_ref, lsen_ref, dln_ref = refs[6:10]
            dq_ref, dk_ref, dv_ref, carry = refs[10:]
        else:
            dq_ref, dk_ref, dv_ref = refs[6:]
        j = pl.program_id(2)
        iq = lax.broadcasted_iota(jnp.int32, (blk, blk), 0)
        jk = lax.broadcasted_iota(jnp.int32, (blk, blk), 1)
        low = lax.broadcasted_iota(jnp.int32, (blk, 2 * hd), 1) < hd

        def pair(hp, qr, dor, lser, dlr, steps, valid):
            sl = slice(2 * hd * hp, 2 * hd * (hp + 1))
            q2, do2, k2, v2 = qr[:, sl], dor[:, sl], k_ref[:, sl], v_ref[:, sl]
            dist = steps.astype(F32) * float(dil)
            dq_h, dk2, dv2 = [], None, None
            for half in range(2):
                col = 2 * hd * hp + hd * half
                mask = low if half == 0 else jnp.logical_not(low)
                qh = jnp.where(mask, q2, jnp.zeros_like(q2))
                doh = jnp.where(mask, do2, jnp.zeros_like(do2))
                sc = _dot_nt(qh, k2) - _alibi_slope(2 * hp + half, dims.n_heads) * dist
                p = jnp.where(valid, jnp.exp(sc - lser[:, col:col + 1]), 0.0)
                ds = p * (_dot_nt(doh, v2) - dlr[:, col:col + 1])
                ds_b, p_b = ds.astype(BF16), p.astype(BF16)
                dq_h.append(jnp.dot(ds_b, k2, preferred_element_type=F32))
                dk_h, dv_h = _dot_tn(ds_b, qh), _dot_tn(p_b, doh)
                dk2 = dk_h if dk2 is None else dk2 + dk_h
                dv2 = dv_h if dv2 is None else dv2 + dv_h
            return sl, jnp.where(low, dq_h[0], dq_h[1]), dk2, dv2

        if has_next:
            @pl.when(j == 0)
            def _():
                carry[...] = jnp.zeros_like(carry)

        for hp in range(dims.n_heads // 2):
            sl, dq2, dk2, dv2 = pair(hp, q_ref, do_ref, lse_ref, dl_ref, iq - jk, iq >= jk)
            dq_ref[:, sl] = (carry[:, sl] + dq2) if has_next else dq2
            dk_ref[:, sl] = dk2
            dv_ref[:, sl] = dv2

        if has_next:
            @pl.when(j + 1 < nb)
            def _():
                for hp in range(dims.n_heads // 2):
                    sl, dq2, dk2, dv2 = pair(hp, qn_ref, don_ref, lsen_ref, dln_ref, iq - jk + blk, jk >= iq)
                    carry[:, sl] = dq2
                    dk_ref[:, sl] += dk2
                    dv_ref[:, sl] += dv2

    cur = pl.BlockSpec((None, blk, a), lambda b, r, j: (b, j, r))
    nxt = pl.BlockSpec((None, blk, a), lambda b, r, j: (b, jnp.minimum(j + 1, nb - 1), r))
    args, in_specs = [k, v, q, do, lse, delta], [cur] * 6
    if has_next:
        args += [q, do, lse, delta]
        in_specs += [nxt] * 4
    shape = jax.ShapeDtypeStruct((dims.batch_local, dims.seq // dil, dil * a), F32)
    outs = pl.pallas_call(
        body, name=name, grid=(dims.batch_local, dil, nb),
        in_specs=in_specs, out_specs=[cur] * 3, out_shape=[shape] * 3,
        scratch_shapes=[pltpu.VMEM((blk, a), F32)] if has_next else [],
        compiler_params=_params("parallel", "parallel", "arbitrary"),
    )(*[_attn_view(x, dims, dil) for x in args])
    return tuple(o.reshape(t, a) for o in outs)


def _mix_fwd(ya, yb, z, gate_b, dims, *, name, tr=512):
    t, d = ya.shape
    tr = _pick(t, tr, 8)
    first_gate_col = z.shape[1] // d - 2

    def body(ya_ref, yb_ref, ga_ref, gb_ref, ba_ref, bb_ref, o_ref):
        g_a = _sigmoid(ga_ref[...] + ba_ref[...])
        g_b = _sigmoid(gb_ref[...] + bb_ref[...])
        o_ref[...] = (g_a * ya_ref[...] + g_b * yb_ref[...]).astype(BF16)

    return pl.pallas_call(
        body, name=name, grid=(t // tr,),
        in_specs=[_row_spec(tr, d), _row_spec(tr, d), _row_spec(tr, d, first_gate_col),
                  _row_spec(tr, d, first_gate_col + 1), _vec_spec(d, 0), _vec_spec(d, 1)],
        out_specs=_row_spec(tr, d), out_shape=jax.ShapeDtypeStruct((t, d), BF16),
        compiler_params=_params("parallel"),
    )(ya, yb, z, z, gate_b, gate_b)


def _mix_bwd(dmix, ya, yb, z, gate_b, dims, *, name, tr=512):
    t, d = ya.shape
    tr = _pick(t, tr, 8)
    first_gate_col = z.shape[1] // d - 2

    def body(dm_ref, ya_ref, yb_ref, ga_ref, gb_ref, ba_ref, bb_ref, dya_ref, dyb_ref, dz_ref, db_ref):
        dm = dm_ref[...].astype(F32)
        g_a = _sigmoid(ga_ref[...] + ba_ref[...])
        g_b = _sigmoid(gb_ref[...] + bb_ref[...])
        dya_ref[...] = (dm * g_a).astype(BF16)
        dyb_ref[...] = (dm * g_b).astype(BF16)
        dl_a = dm * ya_ref[...] * g_a * (1.0 - g_a)
        dl_b = dm * yb_ref[...] * g_b * (1.0 - g_b)
        dz_ref[:, 0:d] = dl_a.astype(BF16)
        dz_ref[:, d:2 * d] = dl_b.astype(BF16)
        first = pl.program_id(0) == 0
        sums = jnp.concatenate([jnp.sum(dl_a, axis=0, keepdims=True), jnp.sum(dl_b, axis=0, keepdims=True)], axis=1)
        _accumulate(db_ref, sums, first)

    return pl.pallas_call(
        body, name=name, grid=(t // tr,),
        in_specs=[_row_spec(tr, d), _row_spec(tr, d), _row_spec(tr, d), _row_spec(tr, d, first_gate_col),
                  _row_spec(tr, d, first_gate_col + 1), _vec_spec(d, 0), _vec_spec(d, 1)],
        out_specs=[_row_spec(tr, d), _row_spec(tr, d), _row_spec(tr, 2 * d), _vec_spec(2 * d)],
        out_shape=[jax.ShapeDtypeStruct((t, d), BF16)] * 2 + [jax.ShapeDtypeStruct((t, 2 * d), BF16),
                                                              jax.ShapeDtypeStruct((1, 2 * d), F32)],
        compiler_params=_params("arbitrary"),
    )(dmix, ya, yb, z, z, gate_b, gate_b)


def _loss_head(y, target, *, name, tr=512):
    t, d = y.shape
    tr = _pick(t, tr, 8)

    def body(y_ref, t_ref, dy_ref, dyb_ref, loss_ref):
        err = y_ref[...] - t_ref[...]
        dy = err * (1.0 / d)
        dy_ref[...] = dy
        dyb_ref[...] = dy.astype(BF16)
        part = jnp.sum(jnp.sum(err * err, axis=-1, keepdims=True), axis=0, keepdims=True) * (0.5 / d)
        _accumulate(loss_ref, jnp.broadcast_to(part, (8, 128)), pl.program_id(0) == 0)

    return pl.pallas_call(
        body, name=name, grid=(t // tr,),
        in_specs=[_row_spec(tr, d), _row_spec(tr, d)],
        out_specs=[_row_spec(tr, d), _row_spec(tr, d), pl.BlockSpec((8, 128), lambda i: (0, 0))],
        out_shape=[jax.ShapeDtypeStruct((t, d), F32), jax.ShapeDtypeStruct((t, d), BF16),
                   jax.ShapeDtypeStruct((8, 128), F32)],
        compiler_params=_params("arbitrary"),
    )(y, target)


def _adamw(w, grads, m, v, *, name, tr=256):
    r, c = w.shape
    tr = _pick(r, tr, 8)
    ng = len(grads)
    c1 = 1.0 - ADAM_B1 ** ADAM_STEP
    c2 = 1.0 - ADAM_B2 ** ADAM_STEP

    def body(*refs):
        w_ref, g_refs, m_ref, v_ref = refs[0], refs[1:1 + ng], refs[1 + ng], refs[2 + ng]
        g_out, d_out, m_out, v_out = refs[3 + ng:]
        g = g_refs[0][...]
        for extra in g_refs[1:]:
            g = g + extra[...]
        m_new = ADAM_B1 * m_ref[...] + (1.0 - ADAM_B1) * g
        v_new = ADAM_B2 * v_ref[...] + (1.0 - ADAM_B2) * (g * g)
        g_out[...] = g
        m_out[...] = m_new
        v_out[...] = v_new
        d_out[...] = -ADAM_LR * ((m_new / c1) / (jnp.sqrt(v_new / c2) + ADAM_EPS) + ADAM_WD * w_ref[...])

    spec = pl.BlockSpec((tr, c), lambda i: (i, 0))
    return pl.pallas_call(
        body, name=name, grid=(r // tr,),
        in_specs=[spec] * (3 + ng), out_specs=[spec] * 4, out_shape=[jax.ShapeDtypeStruct((r, c), F32)] * 4,
        compiler_params=_params("parallel"),
    )(w, *grads, m, v)


CHIP_PEERS = ((1, 0), (0, 1), (1, 1))


def _place():
    return lax.axis_index("x"), lax.axis_index("y"), lax.axis_index("c")


HBM = pl.BlockSpec(memory_space=pltpu.HBM)
SEM = pl.BlockSpec(memory_space=pltpu.SEMAPHORE)
IN_FLIGHT = pltpu.SideEffectType.DATAFLOW_SIDE_EFFECTING


def _in_hbm(a):
    return pltpu.with_memory_space_constraint(a, pltpu.HBM)


def _cast_to_lands(shards, dtypes, *, name):
    n = len(shards)

    def body(*refs):
        ins, outs, bufs, sems = refs[:n], refs[n:2 * n], refs[2 * n:3 * n], refs[3 * n]
        x, y, _ = _place()
        copies = []
        for a in range(n):
            bufs[a][...] = ins[a][...].astype(dtypes[a])
            cp = pltpu.make_async_copy(bufs[a], outs[a].at[2 * x + y], sems.at[a])
            cp.start()
            copies.append(cp)
        for cp in copies:
            cp.wait()

    return pl.pallas_call(
        body, name=name, in_specs=[pl.BlockSpec(memory_space=pltpu.VMEM)] * n, out_specs=[ANY] * n,
        out_shape=[jax.ShapeDtypeStruct((N_CHIPS,) + s.shape, dt) for s, dt in zip(shards, dtypes)],
        scratch_shapes=[pltpu.VMEM(s.shape, dt) for s, dt in zip(shards, dtypes)] + [pltpu.SemaphoreType.DMA((n,))],
        compiler_params=pltpu.CompilerParams(vmem_limit_bytes=V7X_VMEM_LIMIT_BYTES),
    )(*shards)


def _chip_copy(src, dst, send, recv, flip, place):
    x, y, c = place
    return pltpu.make_async_remote_copy(src_ref=src, dst_ref=dst, send_sem=send, recv_sem=recv,
                                        device_id=(x ^ flip[0], y ^ flip[1], c), device_id_type=MESH)


def _gather_start(lands, after, *, name):
    n = len(lands)

    def body(*refs):
        ins, send, recv, token = refs[:n], refs[n + 1], refs[n + 2], refs[-1]
        place = _place()
        me = 2 * place[0] + place[1]
        for a in range(n):
            for p, flip in enumerate(CHIP_PEERS):
                k = 3 * a + p
                _chip_copy(ins[a].at[me], ins[a].at[me], send.at[k], recv.at[k], flip, place).start()
        token[...] = jnp.zeros_like(token)

    outs = pl.pallas_call(
        body, name=name, in_specs=[HBM] * n + [ANY],
        out_specs=(SEM, SEM, *[HBM] * n, pl.BlockSpec(memory_space=pltpu.VMEM)),
        out_shape=(pltpu.SemaphoreType.DMA((3 * n,)), pltpu.SemaphoreType.DMA((3 * n,)),
                   *[pltpu.HBM(l.shape, l.dtype) for l in lands], jax.ShapeDtypeStruct((8, 128), F32)),
        input_output_aliases={a: 2 + a for a in range(n)},
        compiler_params=pltpu.CompilerParams(has_side_effects=IN_FLIGHT),
    )(*[_in_hbm(l) for l in lands], after)
    return outs[0], outs[1], list(outs[2:2 + n]), outs[-1]


def _gather_wait(send, recv, lands, after, *, name):
    n = len(lands)

    def body(*refs):
        ins, send_ref, recv_ref = refs[:n], refs[n], refs[n + 1]
        place = _place()
        me = 2 * place[0] + place[1]
        for a in range(n):
            for p, flip in enumerate(CHIP_PEERS):
                k = 3 * a + p
                cp = _chip_copy(ins[a].at[me], ins[a].at[me], send_ref.at[k], recv_ref.at[k], flip, place)
                cp.wait_send()
                cp.wait_recv()

    return pl.pallas_call(
        body, name=name, in_specs=[HBM] * n + [SEM, SEM, ANY], out_specs=[HBM] * n,
        out_shape=[pltpu.HBM(l.shape, l.dtype) for l in lands],
        input_output_aliases={a: a for a in range(n)},
        compiler_params=pltpu.CompilerParams(has_side_effects=IN_FLIGHT),
    )(*lands, send, recv, after)


def _scatter_start(grad, *, name):
    def body(g_ref, land_ref, send, recv, g_thru, land_thru, token):
        place = _place()
        for p, flip in enumerate(CHIP_PEERS):
            peer_chip = 2 * (place[0] ^ flip[0]) + (place[1] ^ flip[1])
            _chip_copy(g_ref.at[peer_chip], land_ref.at[p], send.at[p], recv.at[p], flip, place).start()
        token[...] = jnp.zeros_like(token)

    land = lax.empty((3,) + grad.shape[1:], grad.dtype)
    return pl.pallas_call(
        body, name=name, in_specs=[HBM, HBM],
        out_specs=(SEM, SEM, HBM, HBM, pl.BlockSpec(memory_space=pltpu.VMEM)),
        out_shape=(pltpu.SemaphoreType.DMA((3,)), pltpu.SemaphoreType.DMA((3,)), pltpu.HBM(grad.shape, grad.dtype),
                   pltpu.HBM(land.shape, land.dtype), jax.ShapeDtypeStruct((8, 128), F32)),
        input_output_aliases={0: 2, 1: 3},
        compiler_params=pltpu.CompilerParams(has_side_effects=IN_FLIGHT),
    )(_in_hbm(grad), _in_hbm(land))


def _scatter_wait(started, after, *, name):
    n = len(started)

    def body(*refs):
        grads, lands = refs[:n], refs[n:2 * n]
        sends, recvs = refs[2 * n:3 * n], refs[3 * n:4 * n]
        place = _place()
        for a in range(n):
            for p, flip in enumerate(CHIP_PEERS):
                cp = _chip_copy(grads[a].at[0], lands[a].at[p], sends[a].at[p], recvs[a].at[p], flip, place)
                cp.wait_send()
                cp.wait_recv()

    grads, lands = [s[2] for s in started], [s[3] for s in started]
    outs = pl.pallas_call(
        body, name=name, in_specs=[HBM] * (2 * n) + [SEM] * (2 * n) + [ANY], out_specs=[HBM] * (2 * n),
        out_shape=[pltpu.HBM(a.shape, a.dtype) for a in grads + lands],
        input_output_aliases={a: a for a in range(2 * n)},
        compiler_params=pltpu.CompilerParams(has_side_effects=IN_FLIGHT),
    )(*grads, *lands, *[s[0] for s in started], *[s[1] for s in started], after)
    return list(zip(outs[:n], outs[n:]))


def _swap_sibling(arrays, *, name):
    n = len(arrays)

    def body(*refs):
        ins, outs = refs[:n], refs[n:2 * n]
        send, recv = refs[2 * n:]
        x, y, c = _place()
        started = []
        for a in range(n):
            rc = pltpu.make_async_remote_copy(
                src_ref=ins[a], dst_ref=outs[a], send_sem=send.at[a], recv_sem=recv.at[a],
                device_id=(x, y, 1 - c), device_id_type=MESH)
            rc.start()
            started.append(rc)
        for rc in started:
            rc.wait()

    return pl.pallas_call(
        body, name=name, in_specs=[ANY] * n, out_specs=[ANY] * n,
        out_shape=[jax.ShapeDtypeStruct(g.shape, g.dtype) for g in arrays],
        scratch_shapes=[pltpu.SemaphoreType.DMA((n,)), pltpu.SemaphoreType.DMA((n,))],
    )(*arrays)


def _sum_received(grad, land, *, name, tr=256):
    _, r, c = grad.shape
    tr = _pick(r, tr, 8)

    def body(chip_ref, g_ref, l_ref, o_ref):
        o_ref[...] = ((g_ref[...] + l_ref[0].astype(F32)) + l_ref[1].astype(F32)) + l_ref[2].astype(F32)

    chip = (2 * lax.axis_index("x") + lax.axis_index("y")).astype(jnp.int32).reshape(1)
    return pl.pallas_call(
        body, name=name,
        grid_spec=pltpu.PrefetchScalarGridSpec(
            num_scalar_prefetch=1, grid=(r // tr,),
            in_specs=[pl.BlockSpec((None, tr, c), lambda i, chip_ref: (chip_ref[0], i, 0)),
                      pl.BlockSpec((3, tr, c), lambda i, chip_ref: (0, i, 0))],
            out_specs=pl.BlockSpec((tr, c), lambda i, chip_ref: (i, 0))),
        out_shape=jax.ShapeDtypeStruct((r, c), F32), compiler_params=_params("parallel"),
    )(chip, grad, land)


def _allreduce_small(packed, *, name, after=None):
    r, d = packed.shape
    n_dev = 8

    def body(src_ref, out_ref, buf, send, recv):
        x, y, c = _place()
        me = 4 * x + 2 * y + c
        started = []
        for p in range(1, n_dev):
            rc = pltpu.make_async_remote_copy(
                src_ref=src_ref, dst_ref=buf.at[me], send_sem=send.at[p - 1], recv_sem=recv.at[p - 1],
                device_id=(x ^ (p >> 2), y ^ ((p >> 1) & 1), c ^ (p & 1)), device_id_type=MESH)
            rc.start()
            started.append(rc)
        buf[me] = src_ref[...]
        for rc in started:
            rc.wait()
        total = buf[0]
        for s in range(1, n_dev):
            total = total + buf[s]
        out_ref[...] = total

    vmem = pl.BlockSpec(memory_space=pltpu.VMEM)
    body, more_specs, more_args = _ordered(body, 1, after)
    return pl.pallas_call(
        body, name=name, in_specs=[vmem] + more_specs, out_specs=vmem, out_shape=jax.ShapeDtypeStruct((r, d), F32),
        scratch_shapes=[pltpu.VMEM((n_dev, r, d), F32), pltpu.SemaphoreType.DMA((n_dev - 1,)),
                        pltpu.SemaphoreType.DMA((n_dev - 1,))],
    )(packed, *more_args)


def _packed_rows(size, d):
    return -(-size // (8 * d)) * 8


def _pack_rows(arrays, d):
    rows = []
    for arr in arrays:
        flat = arr.reshape(-1).astype(F32)
        n = _packed_rows(flat.shape[0], d)
        rows.append(jnp.pad(flat, (0, n * d - flat.shape[0])).reshape(n, d))
    return jnp.concatenate(rows, axis=0)


def _unpack_rows(packed, shapes, d):
    out, row = [], 0
    for shape in shapes:
        size = math.prod(shape)
        n = _packed_rows(size, d)
        out.append(packed[row:row + n].reshape(-1)[:size].reshape(shape))
        row += n
    return out


SMALL = ("norm1_g", "gate_b", "conv_b", "conv_norm_g", "q_norm_g", "k_norm_g", "norm2_g", "ffn_conv_b")
LARGE = ("w_in", "w_conv_out", "w_attn_out", "w_out", "w_up", "w_down")
WEIGHTS = ("norm1_g", "w_in", "gate_b", "conv_w", "conv_b", "conv_norm_g", "w_conv_out", "q_norm_g", "k_norm_g",
           "w_attn_out", "w_out", "norm2_g", "w_up", "ffn_conv_w", "ffn_conv_b", "w_down")


def _head_ones(dims):
    a = dims.n_heads * dims.head_dim
    head = jnp.arange(a, dtype=jnp.int32) // dims.head_dim
    return (head[:, None] == head[None, :]).astype(BF16)


def _after(vec, token):
    return vec if token is None else vec + token[0:1, 0:1]


def _local_step(dims, x, target, small, first_weights, other_weights, send_grad):
    d, f, heads = dims.d_model, dims.d_ff, dims.n_heads
    small = dict(small)
    row = lambda name: small[name].reshape(1, -1)
    ones = _head_ones(dims)
    gq = jnp.tile(row("q_norm_g"), (1, heads))
    gk = jnp.tile(row("k_norm_g"), (1, heads))
    one_shard = lambda w: w.reshape(1, -1, w.shape[-1])

    h = _rmsnorm_fwd(x, row("norm1_g"), name="norm1")
    full = first_weights(h)
    w_in = full["w_in"]
    conv_w = jnp.pad(full["conv_w"], ((0, CONV_HALO - dims.conv_width), (0, 0)))
    ffn_w = jnp.pad(full["ffn_conv_w"], ((0, FFN_HALO - dims.ffn_conv_width), (0, 0)))
    z = _mm_nn(h, w_in, out_dtype=F32, after=full.get("token"), name="in_proj")
    a1, a3 = _conv_branch_fwd(z, conv_w, row("conv_b"), row("conv_norm_g"), dims, name="conv_branch")
    qn, kn, vb = _qkv_fwd(z, gq, gk, ones, dims, name="qk_norm")
    state = None
    for gi, dil in enumerate(DILATIONS):
        state = _attn_fwd_group(qn, kn, vb, state, dims, dil, last=gi == len(DILATIONS) - 1, name=f"attn_fwd_d{dil}")
    o, lse = state
    full = other_weights(o)
    w_up = full["w_up"]
    w_co, w_ao, w_o, w_dn = (one_shard(full[k]) for k in ("w_conv_out", "w_attn_out", "w_out", "w_down"))
    ya = _mm_nn(a3, w_co, out_dtype=F32, name="conv_out_proj")
    yb = _mm_nn(o, w_ao, out_dtype=F32, name="attn_out_proj")
    mixed = _mix_fwd(ya, yb, z, row("gate_b"), dims, name="gate_mix")
    x1 = _mm_nn(mixed, w_o, out_dtype=F32, residual=x, name="out_proj")
    h2 = _rmsnorm_fwd(x1, row("norm2_g"), name="norm2")
    up = _mm_nn(h2, w_up, out_dtype=F32, name="up_proj")
    act = _ffn_act_fwd(up, ffn_w, row("ffn_conv_b"), dims, name="ffn_act")
    x2 = _mm_nn(act, w_dn, out_dtype=F32, residual=x1, name="down_proj")
    dy, dy_b, loss = _loss_head(x2, target, name="loss_head")

    grads = {}

    def large(name, g):
        grads[name], g_bf16 = g
        return send_grad(name, g_bf16)

    sent = large("w_down", _mm_tn(act, dy_b, n_shards=1, name="dw_down"))
    dact = _mm_nt(dy_b, w_dn, out_dtype=BF16, after=sent, name="d_act")
    du, dfw, dfb = _ffn_act_bwd(dact, up, ffn_w, row("ffn_conv_b"), dims, name="ffn_act_bwd")
    grads["ffn_conv_w"], grads["ffn_conv_b"] = dfw[:dims.ffn_conv_width], dfb
    dup = _ffn_conv_bwd(du, ffn_w, dims, name="ffn_conv_bwd")
    sent = large("w_up", _mm_tn(h2, dup, n_shards=N_CHIPS, name="dw_up"))
    dh2 = _mm_nt(dup, w_up, out_dtype=F32, after=sent, name="d_h2")
    dx1, dx1_b, grads["norm2_g"] = _rmsnorm_bwd(x1, row("norm2_g"), dh2, dy, want_bf16=True, name="norm2_bwd")
    sent = large("w_out", _mm_tn(mixed, dx1_b, n_shards=1, name="dw_out"))
    dmix = _mm_nt(dx1_b, w_o, out_dtype=F32, after=sent, name="d_mix")
    dya, dyb, dz_gate, grads["gate_b"] = _mix_bwd(dmix, ya, yb, z, row("gate_b"), dims, name="gate_mix_bwd")
    sent = large("w_conv_out", _mm_tn(a3, dya, n_shards=1, name="dw_conv_out"))
    da3 = _mm_nt(dya, w_co, out_dtype=F32, after=sent, name="d_conv_act")
    da1, grads["conv_norm_g"] = _conv_norm_bwd(da3, a1, row("conv_norm_g"), name="conv_norm_bwd")
    dz_glu, dcw, grads["conv_b"] = _conv_branch_bwd(da1, z, conv_w, dims, name="conv_branch_bwd")
    grads["conv_w"] = dcw[:dims.conv_width]
    sent = large("w_attn_out", _mm_tn(o, dyb, n_shards=1, name="dw_attn_out"))
    do = _mm_nt(dyb, w_ao, out_dtype=BF16, after=sent, name="d_attn")
    delta = _attn_delta(do, o, ones, dims, name="attn_delta")
    per_group = [_attn_bwd_group(qn, kn, vb, do, lse, delta, dims, dil, name=f"attn_bwd_d{dil}") for dil in DILATIONS]
    dqs, dks, dvs = zip(*per_group)
    dz_qkv, dgq, dgk = _qkv_bwd(z, dqs, dks, dvs, gq, gk, ones, dims, name="qk_norm_bwd")
    grads["q_norm_g"] = dgq.reshape(heads, dims.head_dim).sum(axis=0)
    grads["k_norm_g"] = dgk.reshape(heads, dims.head_dim).sum(axis=0)
    dz = jnp.concatenate([dz_glu, dz_qkv, dz_gate], axis=1)
    sent = large("w_in", _mm_tn(h, dz, n_shards=N_CHIPS, name="dw_in"))
    dh = _mm_nt(dz, w_in, out_dtype=F32, after=sent, name="d_h")
    dx, grads["norm1_g"] = _rmsnorm_bwd(x, row("norm1_g"), dh, dx1, want_bf16=False, name="norm1_bwd")
    return loss, dx, grads


def _step(dims, x, target, w, m, v):
    d = dims.d_model
    t = dims.tokens
    sq = lambda a: a.reshape(a.shape[1:])
    w2, m2, v2 = ({k: sq(a) for k, a in grp.items()} for grp in (w, m, v))

    conv_pad = jnp.pad(w2["conv_w"], ((0, CONV_HALO - dims.conv_width), (0, 0)))
    ffn_pad = jnp.pad(w2["ffn_conv_w"], ((0, FFN_HALO - dims.ffn_conv_width), (0, 0)))
    gathered_names = LARGE + ("conv_w", "ffn_conv_w")
    lands = dict(zip(gathered_names, _cast_to_lands([w2[k] for k in LARGE] + [conv_pad, ffn_pad],
                                                   [BF16] * len(LARGE) + [F32, F32], name="cast_weights")))
    first_names = ("w_in", "conv_w", "ffn_conv_w")
    other_names = tuple(k for k in gathered_names if k not in first_names)
    first = _gather_start([lands[k] for k in first_names], x, name="gather_start_first")
    other = []
    cols = lambda g, rows: jnp.moveaxis(g, 0, 1).reshape(g.shape[1], -1)[:rows]

    def first_weights(after):
        got = dict(zip(first_names, _gather_wait(*first[:3], after, name="gather_wait_first")))
        other.extend(_gather_start([lands[k] for k in other_names], got["w_in"], name="gather_start_other"))
        got["conv_w"] = cols(got["conv_w"], dims.conv_width)
        got["ffn_conv_w"] = cols(got["ffn_conv_w"], dims.ffn_conv_width)
        got["token"] = other[3]
        return got

    def other_weights(after):
        return dict(zip(other_names, _gather_wait(*other[:3], after, name="gather_wait_other")))

    started = {}

    def send_grad(name, g):
        send, recv, g_thru, land, token = _scatter_start(g.reshape(N_CHIPS, -1, g.shape[-1]), name=f"scatter_start_{name}")
        started[name] = (send, recv, g_thru, land)
        return token

    small = {k: w2[k] for k in SMALL}
    small["norm1_g"] = _after(small["norm1_g"].reshape(1, -1), first[3])
    loss, dx, grads = _local_step(dims, x.reshape(t, d), target.reshape(t, d), small, first_weights, other_weights, send_grad)

    def finish(names, after, tag):
        arrived = _scatter_wait([started[k] for k in names], after, name=f"scatter_wait_{tag}")
        blocks = [grads[k].reshape(N_CHIPS, -1, grads[k].shape[-1]) for k in names]
        mine = [_sum_received(g, land, name=f"sum_{k}") for k, g, (_, land) in zip(names, blocks, arrived)]
        theirs = _swap_sibling(mine, name=f"swap_sibling_{tag}")
        return {k: _adamw(w2[k], [a, b], m2[k], v2[k], name=f"adamw_{k}") for k, a, b in zip(names, mine, theirs)}

    out = finish([k for k in LARGE if k != "w_in"], dx, "others")

    small_names = SMALL + ("conv_w", "ffn_conv_w")
    packed = _pack_rows([grads[k] for k in small_names] + [loss[0, 0]], d)
    reduced = _allreduce_small(packed, after=[upd[1] for upd in out.values()], name="allreduce_small")
    shapes = [grads[k].shape for k in small_names] + [()]
    *small_g, loss_total = _unpack_rows(reduced, shapes, d)
    small_g = dict(zip(small_names, small_g))
    chip = 2 * lax.axis_index("x") + lax.axis_index("y")
    for k in ("conv_w", "ffn_conv_w"):
        width = w2[k].shape[1]
        small_g[k] = lax.dynamic_slice_in_dim(small_g[k], chip * width, width, axis=1)

    small_shapes = [w2[k].shape for k in small_names]
    pack = lambda grp: _pack_rows([grp[k] for k in small_names], d)
    results = _adamw(pack(w2), [pack(small_g)], pack(m2), pack(v2), name="adamw_small")
    unpacked = [_unpack_rows(r, small_shapes, d) for r in results]
    for i, k in enumerate(small_names):
        out[k] = tuple(u[i] for u in unpacked)
    out.update(finish(["w_in"], results[1], "w_in"))

    lead =lambda a: a.reshape((1,) + a.shape)
    ordered = [[lead(out[k][j].reshape(w2[k].shape)) for k in WEIGHTS] for j in range(4)]
    return (loss_total, dx.reshape(x.shape), *ordered[0], *ordered[1], *ordered[2], *ordered[3])


def kernel(x, norm1_g, w_in, gate_b, conv_w, conv_b, conv_norm_g, w_conv_out, q_norm_g, k_norm_g, w_attn_out, w_out, norm2_g, w_up, ffn_conv_w, ffn_conv_b, w_down, loss_target, m_norm1_g, m_w_in, m_gate_b, m_conv_w, m_conv_b, m_conv_norm_g, m_w_conv_out, m_q_norm_g, m_k_norm_g, m_w_attn_out, m_w_out, m_norm2_g, m_w_up, m_ffn_conv_w, m_ffn_conv_b, m_w_down, v_norm1_g, v_w_in, v_gate_b, v_conv_w, v_conv_b, v_conv_norm_g, v_w_conv_out, v_q_norm_g, v_k_norm_g, v_w_attn_out, v_w_out, v_norm2_g, v_w_up, v_ffn_conv_w, v_ffn_conv_b, v_w_down):
    w = dict(zip(WEIGHTS, (norm1_g, w_in, gate_b, conv_w, conv_b, conv_norm_g, w_conv_out, q_norm_g, k_norm_g,
                           w_attn_out, w_out, norm2_g, w_up, ffn_conv_w, ffn_conv_b, w_down)))
    m = dict(zip(WEIGHTS, (m_norm1_g, m_w_in, m_gate_b, m_conv_w, m_conv_b, m_conv_norm_g, m_w_conv_out, m_q_norm_g,
                           m_k_norm_g, m_w_attn_out, m_w_out, m_norm2_g, m_w_up, m_ffn_conv_w, m_ffn_conv_b, m_w_down)))
    v = dict(zip(WEIGHTS, (v_norm1_g, v_w_in, v_gate_b, v_conv_w, v_conv_b, v_conv_norm_g, v_w_conv_out, v_q_norm_g,
                           v_k_norm_g, v_w_attn_out, v_w_out, v_norm2_g, v_w_up, v_ffn_conv_w, v_ffn_conv_b, v_w_down)))
    dims = Dims(d_model=x.shape[-1], batch_local=x.shape[0], seq=x.shape[1], d_ff=w_down.shape[1] * N_CHIPS)
    return _step(dims, x, loss_target, w, m, v)
```

```python
import functools
import math
from typing import NamedTuple

import jax
import jax.numpy as jnp
from jax import lax
from jax.experimental import pallas as pl
from jax.experimental.pallas import tpu as pltpu

F32 = jnp.float32
BF16 = jnp.bfloat16

RMS_EPS = 1e-6
MASKED_SCORE = -1e30
ATTN_BLOCK = 128
DILATIONS = (1, 4, 16)
CONV_HALO = 32
FFN_HALO = 8
ADAM_LR, ADAM_B1, ADAM_B2, ADAM_EPS, ADAM_WD, ADAM_STEP = 0.001, 0.9, 0.999, 1e-08, 0.01, 10
V7X_VMEM_LIMIT_BYTES = 56 * 2 ** 20
N_CHIPS = 4
MESH = pl.DeviceIdType.MESH


class Dims(NamedTuple):
    d_model: int = 1024
    n_heads: int = 16
    head_dim: int = 64
    d_ff: int = 2816
    seq: int = 2048
    batch_local: int = 2
    conv_width: int = 31
    ffn_conv_width: int = 3

    @property
    def tokens(self):
        return self.seq * self.batch_local


def _params(*semantics):
    return pltpu.CompilerParams(dimension_semantics=semantics, vmem_limit_bytes=V7X_VMEM_LIMIT_BYTES)


ANY = pl.BlockSpec(memory_space=pl.ANY)


def _ordered(body, n_inputs, after):
    after = [] if after is None else list(after) if isinstance(after, (list, tuple)) else [after]
    if not after:
        return body, [], []

    def wrapped(*refs):
        return body(*refs[:n_inputs], *refs[n_inputs + len(after):])

    return wrapped, [ANY] * len(after), after


def _pick(n, target, mult=128):
    if n <= target:
        return n
    best = None
    for t in range(mult, target + 1, mult):
        if n % t == 0:
            best = t
    assert best is not None, (n, target, mult)
    return best


def _sigmoid(v):
    return 1.0 / (1.0 + jnp.exp(-v))


def _mm_nn(a, w, *, out_dtype, name, residual=None, after=None, tm=1024, tn=1408, tk=2816):
    m, k = a.shape
    nsh, k2, c = w.shape
    assert k == k2 and a.dtype == BF16 and w.dtype == BF16
    n = nsh * c
    tm, tn, tk = _pick(m, tm, 8), _pick(c, tn), _pick(k, tk)
    nk, cpn = k // tk, c // tn

    def body(*refs):
        if residual is None:
            a_ref, w_ref, o_ref, acc = refs
        else:
            a_ref, w_ref, r_ref, o_ref, acc = refs
        prod = jnp.dot(a_ref[...], w_ref[...], preferred_element_type=F32)

        def finish(total):
            if residual is not None:
                total = total + r_ref[...]
            o_ref[...] = total.astype(out_dtype)

        if nk == 1:
            finish(prod)
        else:
            kk = pl.program_id(2)

            @pl.when(kk == 0)
            def _():
                acc[...] = prod

            @pl.when(kk > 0)
            def _():
                acc[...] += prod

            @pl.when(kk == nk - 1)
            def _():
                finish(acc[...])

    in_specs = [pl.BlockSpec((tm, tk), lambda i, j, kk: (i, kk)),
                pl.BlockSpec((None, tk, tn), lambda i, j, kk: (j // cpn, kk, j % cpn))]
    args = [a, w]
    if residual is not None:
        in_specs.append(pl.BlockSpec((tm, tn), lambda i, j, kk: (i, j)))
        args.append(residual)
    body, more_specs, more_args = _ordered(body, len(args), after)
    return pl.pallas_call(
        body, name=name, grid=(m // tm, n // tn, nk),
        in_specs=in_specs + more_specs, out_specs=pl.BlockSpec((tm, tn), lambda i, j, kk: (i, j)),
        out_shape=jax.ShapeDtypeStruct((m, n), out_dtype),
        scratch_shapes=[pltpu.VMEM((tm, tn) if nk > 1 else (8, 128), F32)],
        compiler_params=_params("parallel", "parallel", "arbitrary"),
    )(*args, *more_args)


def _mm_nt(a, w, *, out_dtype, name, after=None, tm=1024, tn=1408, tk=1792):
    m, k = a.shape
    nsh, r, c = w.shape
    assert k == nsh * c and a.dtype == BF16 and w.dtype == BF16
    tm, tn, tk = _pick(m, tm, 8), _pick(r, tn), _pick(c, tk)
    nk, cpk = k // tk, c // tk

    def body(a_ref, w_ref, o_ref, acc):
        prod = lax.dot_general(a_ref[...], w_ref[...], (((1,), (1,)), ((), ())), preferred_element_type=F32)
        if nk == 1:
            o_ref[...] = prod.astype(out_dtype)
        else:
            kk = pl.program_id(2)

            @pl.when(kk == 0)
            def _():
                acc[...] = prod

            @pl.when(kk > 0)
            def _():
                acc[...] += prod

            @pl.when(kk == nk - 1)
            def _():
                o_ref[...] = acc[...].astype(out_dtype)

    body, more_specs, more_args = _ordered(body, 2, after)
    return pl.pallas_call(
        body, name=name, grid=(m // tm, r // tn, nk),
        in_specs=[pl.BlockSpec((tm, tk), lambda i, j, kk: (i, kk)),
                  pl.BlockSpec((None, tn, tk), lambda i, j, kk: (kk // cpk, j, kk % cpk))] + more_specs,
        out_specs=pl.BlockSpec((tm, tn), lambda i, j, kk: (i, j)),
        out_shape=jax.ShapeDtypeStruct((m, r), out_dtype),
        scratch_shapes=[pltpu.VMEM((tm, tn) if nk > 1 else (8, 128), F32)],
        compiler_params=_params("parallel", "parallel", "arbitrary"),
    )(a, w, *more_args)


def _mm_tn(a, b, *, n_shards, name, tm=1408, tn=1408, tk=512):
    t, m = a.shape
    t2, n = b.shape
    assert t == t2 and a.dtype == BF16 and b.dtype == BF16
    c = n // n_shards
    tm, tn, tk = _pick(m, tm), _pick(c, tn), _pick(t, tk, 8)
    nk, cpn = t // tk, c // tn

    def body(a_ref, b_ref, o_ref, ob_ref, acc):
        kk = pl.program_id(2)
        prod = lax.dot_general(a_ref[...], b_ref[...], (((0,), (0,)), ((), ())), preferred_element_type=F32)

        @pl.when(kk == 0)
        def _():
            acc[...] = prod

        @pl.when(kk > 0)
        def _():
            acc[...] += prod

        @pl.when(kk == nk - 1)
        def _():
            total = acc[...]
            o_ref[...] = total
            ob_ref[...] = total.astype(BF16)

    out_spec = pl.BlockSpec((None, tm, tn), lambda i, j, kk: (j // cpn, i, j % cpn))
    return pl.pallas_call(
        body, name=name, grid=(m // tm, n // tn, nk),
        in_specs=[pl.BlockSpec((tk, tm), lambda i, j, kk: (kk, i)),
                  pl.BlockSpec((tk, tn), lambda i, j, kk: (kk, j))],
        out_specs=[out_spec, out_spec],
        out_shape=[jax.ShapeDtypeStruct((n_shards, m, c), F32), jax.ShapeDtypeStruct((n_shards, m, c), BF16)],
        scratch_shapes=[pltpu.VMEM((tm, tn), F32)],
        compiler_params=_params("parallel", "parallel", "arbitrary"),
    )(a, b)


def _row_spec(tr, width, col=0):
    return pl.BlockSpec((tr, width), lambda i, col=col: (i, col))


def _vec_spec(width, col=0):
    return pl.BlockSpec((1, width), lambda i, col=col: (0, col))


def _accumulate(ref, value, first):
    @pl.when(first)
    def _():
        ref[...] = value

    @pl.when(jnp.logical_not(first))
    def _():
        ref[...] += value


def _rmsnorm_fwd(x, g, *, name, tr=512):
    t, d = x.shape
    tr = _pick(t, tr, 8)

    def body(x_ref, g_ref, o_ref):
        xv = x_ref[...]
        r = lax.rsqrt(jnp.mean(xv * xv, axis=-1, keepdims=True) + RMS_EPS)
        o_ref[...] = (xv * r * g_ref[...]).astype(BF16)

    return pl.pallas_call(
        body, name=name, grid=(t // tr,),
        in_specs=[_row_spec(tr, d), _vec_spec(d)], out_specs=_row_spec(tr, d),
        out_shape=jax.ShapeDtypeStruct((t, d), BF16), compiler_params=_params("parallel"),
    )(x, g)


def _rmsnorm_bwd(x, g, dy, dres, *, name, want_bf16, tr=512):
    t, d = x.shape
    tr = _pick(t, tr, 8)

    def body(x_ref, g_ref, dy_ref, dres_ref, *outs):
        dx_ref, dg_ref = outs[0], outs[-1]
        xv, dyv = x_ref[...], dy_ref[...].astype(F32)
        r = lax.rsqrt(jnp.mean(xv * xv, axis=-1, keepdims=True) + RMS_EPS)
        gy = dyv * g_ref[...]
        dx = dres_ref[...] + r * gy - xv * (r * r * r) * jnp.mean(xv * gy, axis=-1, keepdims=True)
        dx_ref[...] = dx
        if want_bf16:
            outs[1][...] = dx.astype(BF16)
        _accumulate(dg_ref, jnp.sum(dyv * xv * r, axis=0, keepdims=True), pl.program_id(0) == 0)

    out_shape = [jax.ShapeDtypeStruct((t, d), F32)]
    out_specs = [_row_spec(tr, d)]
    if want_bf16:
        out_shape.append(jax.ShapeDtypeStruct((t, d), BF16))
        out_specs.append(_row_spec(tr, d))
    out_shape.append(jax.ShapeDtypeStruct((1, d), F32))
    out_specs.append(_vec_spec(d))
    return pl.pallas_call(
        body, name=name, grid=(t // tr,),
        in_specs=[_row_spec(tr, d), _vec_spec(d), _row_spec(tr, d), _row_spec(tr, d)],
        out_specs=out_specs, out_shape=out_shape, compiler_params=_params("arbitrary"),
    )(x, g, dy, dres)


def _head_mean(v, ones_ref, head_dim):
    hi = v.astype(BF16)
    lo = (v - hi.astype(F32)).astype(BF16)
    e = ones_ref[...]
    total = jnp.dot(hi, e, preferred_element_type=F32) + jnp.dot(lo, e, preferred_element_type=F32)
    return total * (1.0 / head_dim)


def _qkv_fwd(z, gq, gk, head_ones, dims, *, name, tr=256):
    t = z.shape[0]
    a = dims.n_heads * dims.head_dim
    tr = _pick(t, tr, 8)
    q_scale = dims.head_dim ** -0.5

    def body(q_ref, k_ref, v_ref, gq_ref, gk_ref, e_ref, qo_ref, ko_ref, vo_ref):
        qv, kv = q_ref[...], k_ref[...]
        rq = lax.rsqrt(_head_mean(qv * qv, e_ref, dims.head_dim) + RMS_EPS)
        rk = lax.rsqrt(_head_mean(kv * kv, e_ref, dims.head_dim) + RMS_EPS)
        qo_ref[...] = (qv * rq * gq_ref[...] * q_scale).astype(BF16)
        ko_ref[...] = (kv * rk * gk_ref[...]).astype(BF16)
        vo_ref[...] = v_ref[...].astype(BF16)

    return pl.pallas_call(
        body, name=name, grid=(t // tr,),
        in_specs=[_row_spec(tr, a, 2), _row_spec(tr, a, 3), _row_spec(tr, a, 4), _vec_spec(a), _vec_spec(a),
                  pl.BlockSpec((a, a), lambda i: (0, 0))],
        out_specs=[_row_spec(tr, a)] * 3, out_shape=[jax.ShapeDtypeStruct((t, a), BF16)] * 3,
        compiler_params=_params("parallel"),
    )(z, z, z, gq, gk, head_ones)


def _qkv_bwd(z, dqs, dks, dvs, gq, gk, head_ones, dims, *, name, tr=256):
    t = z.shape[0]
    a = dims.n_heads * dims.head_dim
    tr = _pick(t, tr, 8)
    q_scale = dims.head_dim ** -0.5
    ng = len(dqs)

    def body(*refs):
        q_ref, k_ref = refs[:2]
        dq_refs, dk_refs, dv_refs = refs[2:2 + ng], refs[2 + ng:2 + 2 * ng], refs[2 + 2 * ng:2 + 3 * ng]
        gq_ref, gk_ref, e_ref = refs[2 + 3 * ng:5 + 3 * ng]
        dz_ref, dgq_ref, dgk_ref = refs[5 + 3 * ng:]
        first = pl.program_id(0) == 0

        def norm_bwd(x_ref, d_refs, g_ref, scale, col, dg_ref):
            xv = x_ref[...]
            dy = sum(r[...] for r in d_refs) * scale
            r = lax.rsqrt(_head_mean(xv * xv, e_ref, dims.head_dim) + RMS_EPS)
            gy = dy * g_ref[...]
            dx = r * gy - xv * (r * r * r) * _head_mean(xv * gy, e_ref, dims.head_dim)
            dz_ref[:, col * a:(col + 1) * a] = dx.astype(BF16)
            _accumulate(dg_ref, jnp.sum(dy * xv * r, axis=0, keepdims=True), first)

        norm_bwd(q_ref, dq_refs, gq_ref, q_scale, 0, dgq_ref)
        norm_bwd(k_ref, dk_refs, gk_ref, 1.0, 1, dgk_ref)
        dz_ref[:, 2 * a:3 * a] = sum(r[...] for r in dv_refs).astype(BF16)

    in_specs = ([_row_spec(tr, a, 2), _row_spec(tr, a, 3)] + [_row_spec(tr, a)] * (3 * ng)
                + [_vec_spec(a), _vec_spec(a), pl.BlockSpec((a, a), lambda i: (0, 0))])
    return pl.pallas_call(
        body, name=name, grid=(t // tr,), in_specs=in_specs,
        out_specs=[_row_spec(tr, 3 * a), _vec_spec(a), _vec_spec(a)],
        out_shape=[jax.ShapeDtypeStruct((t, 3 * a), BF16)] + [jax.ShapeDtypeStruct((1, a), F32)] * 2,
        compiler_params=_params("arbitrary"),
    )(z, z, *dqs, *dks, *dvs, gq, gk, head_ones)


CONV_ROWS = 16


def _seq_specs(dims, ts, width, halo, col, *, nxt=False):
    nst, per = dims.seq // ts, ts // halo
    last = dims.tokens // halo - 1
    cur = pl.BlockSpec((ts, width), lambda b, i: (b * nst + i, col))
    if nxt:
        edge = pl.BlockSpec((halo, width), lambda b, i: (jnp.minimum((b * nst + i + 1) * per, last), col))
    else:
        edge = pl.BlockSpec((halo, width), lambda b, i: (jnp.maximum((b * nst + i) * per - 1, 0), col))
    return cur, edge


def _conv_branch_fwd(z, w, b, g, dims, *, name, ts=128):
    t, c, kw = z.shape[0], dims.d_model, dims.conv_width
    base = CONV_HALO - (kw - 1)

    def body(av_ref, hv_ref, ag_ref, hg_ref, w_ref, b_ref, g_ref, a1_ref, a3_ref, buf):
        i = pl.program_id(1)
        buf[CONV_HALO:, :] = av_ref[...] * _sigmoid(ag_ref[...])
        buf[0:CONV_HALO, :] = jnp.where(i > 0, hv_ref[...] * _sigmoid(hg_ref[...]), 0.0)
        for r0 in range(0, ts, CONV_ROWS):
            acc = jnp.broadcast_to(b_ref[...], (CONV_ROWS, c))
            for k in range(kw):
                acc = acc + w_ref[k:k + 1, :] * buf[pl.ds(r0 + base + k, CONV_ROWS), :]
            a1_ref[r0:r0 + CONV_ROWS, :] = acc
            a2 = acc * lax.rsqrt(jnp.mean(acc * acc, axis=-1, keepdims=True) + RMS_EPS) * g_ref[...]
            a3_ref[r0:r0 + CONV_ROWS, :] = (a2 * _sigmoid(a2)).astype(BF16)

    vec = pl.BlockSpec((1, c), lambda b, i: (0, 0))
    out = pl.BlockSpec((ts, c), lambda b, i: (b * (dims.seq // ts) + i, 0))
    return pl.pallas_call(
        body, name=name, grid=(dims.batch_local, dims.seq // ts),
        in_specs=[*_seq_specs(dims, ts, c, CONV_HALO, 0), *_seq_specs(dims, ts, c, CONV_HALO, 1),
                  pl.BlockSpec((CONV_HALO, c), lambda b, i: (0, 0)), vec, vec],
        out_specs=[out, out],
        out_shape=[jax.ShapeDtypeStruct((t, c), F32), jax.ShapeDtypeStruct((t, c), BF16)],
        scratch_shapes=[pltpu.VMEM((CONV_HALO + ts, c), F32)],
        compiler_params=_params("parallel", "parallel"),
    )(z, z, z, z, w, b, g)


def _conv_norm_bwd(da3, a1, g, *, name, tr=256):
    t, c = a1.shape
    tr = _pick(t, tr, 8)

    def body(d_ref, a_ref, g_ref, o_ref, dg_ref):
        a1v, gv = a_ref[...], g_ref[...]
        r = lax.rsqrt(jnp.mean(a1v * a1v, axis=-1, keepdims=True) + RMS_EPS)
        a2 = a1v * r * gv
        sg = _sigmoid(a2)
        da2 = d_ref[...].astype(F32) * sg * (1.0 + a2 * (1.0 - sg))
        gy = da2 * gv
        o_ref[...] = r * gy - a1v * (r * r * r) * jnp.mean(a1v * gy, axis=-1, keepdims=True)
        _accumulate(dg_ref, jnp.sum(da2 * a1v * r, axis=0, keepdims=True), pl.program_id(0) == 0)

    return pl.pallas_call(
        body, name=name, grid=(t // tr,),
        in_specs=[_row_spec(tr, c), _row_spec(tr, c), _vec_spec(c)],
        out_specs=[_row_spec(tr, c), _vec_spec(c)],
        out_shape=[jax.ShapeDtypeStruct((t, c), F32), jax.ShapeDtypeStruct((1, c), F32)],
        compiler_params=_params("arbitrary"),
    )(da3, a1, g)


def _conv_branch_bwd(da1, z, w, dims, *, name, ts=128):
    t, c, kw = z.shape[0], dims.d_model, dims.conv_width
    nst = dims.seq // ts
    base = CONV_HALO - (kw - 1)

    def body(d_ref, dn_ref, av_ref, hv_ref, ag_ref, hg_ref, w_ref, dz_ref, dw_ref, db_ref, abuf, dbuf):
        i = pl.program_id(1)
        first = jnp.logical_and(pl.program_id(0) == 0, i == 0)
        abuf[CONV_HALO:, :] = av_ref[...] * _sigmoid(ag_ref[...])
        abuf[0:CONV_HALO, :] = jnp.where(i > 0, hv_ref[...] * _sigmoid(hg_ref[...]), 0.0)
        d1 = d_ref[...]
        dbuf[0:ts, :] = d1
        dbuf[ts:, :] = jnp.where(i < nst - 1, dn_ref[...], 0.0)

        @pl.when(first)
        def _():
            dw_ref[...] = jnp.zeros_like(dw_ref)
            db_ref[...] = jnp.zeros_like(db_ref)

        db_ref[...] += jnp.sum(d1, axis=0, keepdims=True)
        for k in range(kw):
            dw_ref[k:k + 1, :] += jnp.sum(d1 * abuf[pl.ds(base + k, ts), :], axis=0, keepdims=True)
        for r0 in range(0, ts, CONV_ROWS):
            acc = jnp.zeros((CONV_ROWS, c), F32)
            for k in range(kw):
                acc = acc + w_ref[k:k + 1, :] * dbuf[pl.ds(r0 + (kw - 1) - k, CONV_ROWS), :]
            av = av_ref[r0:r0 + CONV_ROWS, :]
            sg = _sigmoid(ag_ref[r0:r0 + CONV_ROWS, :])
            dz_ref[r0:r0 + CONV_ROWS, 0:c] = (acc * sg).astype(BF16)
            dz_ref[r0:r0 + CONV_ROWS, c:2 * c] = (acc * av * sg * (1.0 - sg)).astype(BF16)

    cur, nxt = _seq_specs(dims, ts, c, CONV_HALO, 0, nxt=True)
    return pl.pallas_call(
        body, name=name, grid=(dims.batch_local, nst),
        in_specs=[cur, nxt, *_seq_specs(dims, ts, c, CONV_HALO, 0), *_seq_specs(dims, ts, c, CONV_HALO, 1),
                  pl.BlockSpec((CONV_HALO, c), lambda b, i: (0, 0))],
        out_specs=[pl.BlockSpec((ts, 2 * c), lambda b, i: (b * nst + i, 0)),
                   pl.BlockSpec((CONV_HALO, c), lambda b, i: (0, 0)), pl.BlockSpec((1, c), lambda b, i: (0, 0))],
        out_shape=[jax.ShapeDtypeStruct((t, 2 * c), BF16), jax.ShapeDtypeStruct((CONV_HALO, c), F32),
                   jax.ShapeDtypeStruct((1, c), F32)],
        scratch_shapes=[pltpu.VMEM((CONV_HALO + ts, c), F32), pltpu.VMEM((ts + CONV_HALO, c), F32)],
        compiler_params=_params("arbitrary", "arbitrary"),
    )(da1, da1, z, z, z, z, w)


FFN_ROWS = 16
FFN_COLS = 256


def _ffn_chunks(ts, f):
    cw = _pick(f, FFN_COLS)
    return [(r0, c0, cw) for r0 in range(0, ts, FFN_ROWS) for c0 in range(0, f, cw)]


def _ffn_conv(buf, w_ref, b_ref, r0, cols, kw):
    base = FFN_HALO - (kw - 1)
    u = jnp.broadcast_to(b_ref[:, cols], (FFN_ROWS, cols.stop - cols.start))
    for k in range(kw):
        u = u + w_ref[k:k + 1, cols] * buf[pl.ds(r0 + base + k, FFN_ROWS), cols]
    return u


def _ffn_act_fwd(up, w, b, dims, *, name, ts=128):
    t, f, kw = up.shape[0], dims.d_ff, dims.ffn_conv_width

    def body(up_ref, h_ref, w_ref, b_ref, o_ref, buf):
        buf[FFN_HALO:, :] = up_ref[...]
        buf[0:FFN_HALO, :] = jnp.where(pl.program_id(1) > 0, h_ref[...], 0.0)
        for r0, c0, cw in _ffn_chunks(ts, f):
            uv = _ffn_conv(buf, w_ref, b_ref, r0, slice(c0, c0 + cw), kw)
            ug = _ffn_conv(buf, w_ref, b_ref, r0, slice(f + c0, f + c0 + cw), kw)
            o_ref[r0:r0 + FFN_ROWS, c0:c0 + cw] = (ug * _sigmoid(ug) * uv).astype(BF16)

    full = lambda rows: pl.BlockSpec((rows, 2 * f), lambda b_, i: (0, 0))
    return pl.pallas_call(
        body, name=name, grid=(dims.batch_local, dims.seq // ts),
        in_specs=[*_seq_specs(dims, ts, 2 * f, FFN_HALO, 0), full(FFN_HALO), full(1)],
        out_specs=pl.BlockSpec((ts, f), lambda b_, i: (b_ * (dims.seq // ts) + i, 0)),
        out_shape=jax.ShapeDtypeStruct((t, f), BF16),
        scratch_shapes=[pltpu.VMEM((FFN_HALO + ts, 2 * f), F32)],
        compiler_params=_params("parallel", "parallel"),
    )(up, up, w, b)


def _ffn_act_bwd(dact, up, w, b, dims, *, name, ts=128):
    t, f, kw = up.shape[0], dims.d_ff, dims.ffn_conv_width
    base = FFN_HALO - (kw - 1)

    def body(d_ref, up_ref, h_ref, w_ref, b_ref, du_ref, dw_ref, db_ref, buf):
        i = pl.program_id(1)
        first = jnp.logical_and(pl.program_id(0) == 0, i == 0)
        buf[FFN_HALO:, :] = up_ref[...]
        buf[0:FFN_HALO, :] = jnp.where(i > 0, h_ref[...], 0.0)
        for r0, c0, cw in _ffn_chunks(ts, f):
            vcols, gcols = slice(c0, c0 + cw), slice(f + c0, f + c0 + cw)
            uv = _ffn_conv(buf, w_ref, b_ref, r0, vcols, kw)
            ug = _ffn_conv(buf, w_ref, b_ref, r0, gcols, kw)
            d = d_ref[r0:r0 + FFN_ROWS, vcols].astype(F32)
            sg = _sigmoid(ug)
            du_ref[r0:r0 + FFN_ROWS, vcols] = d * ug * sg
            du_ref[r0:r0 + FFN_ROWS, gcols] = d * uv * sg * (1.0 + ug * (1.0 - sg))

        @pl.when(first)
        def _():
            dw_ref[...] = jnp.zeros_like(dw_ref)
            db_ref[...] = jnp.zeros_like(db_ref)

        du = du_ref[...]
        db_ref[...] += jnp.sum(du, axis=0, keepdims=True)
        for k in range(kw):
            dw_ref[k:k + 1, :] += jnp.sum(du * buf[pl.ds(base + k, ts), :], axis=0, keepdims=True)

    nst = dims.seq // ts
    full = lambda rows: pl.BlockSpec((rows, 2 * f), lambda b_, i: (0, 0))
    return pl.pallas_call(
        body, name=name, grid=(dims.batch_local, nst),
        in_specs=[pl.BlockSpec((ts, f), lambda b_, i: (b_ * nst + i, 0)),
                  *_seq_specs(dims, ts, 2 * f, FFN_HALO, 0), full(FFN_HALO), full(1)],
        out_specs=[pl.BlockSpec((ts, 2 * f), lambda b_, i: (b_ * nst + i, 0)), full(FFN_HALO), full(1)],
        out_shape=[jax.ShapeDtypeStruct((t, 2 * f), F32), jax.ShapeDtypeStruct((FFN_HALO, 2 * f), F32),
                   jax.ShapeDtypeStruct((1, 2 * f), F32)],
        scratch_shapes=[pltpu.VMEM((FFN_HALO + ts, 2 * f), F32)],
        compiler_params=_params("arbitrary", "arbitrary"),
    )(dact, up, up, w, b)


def _ffn_conv_bwd(du, w, dims, *, name, ts=128):
    t, f2 = du.shape
    kw = dims.ffn_conv_width
    nst = dims.seq // ts

    def body(d_ref, dn_ref, w_ref, o_ref, buf):
        buf[0:ts, :] = d_ref[...]
        buf[ts:, :] = jnp.where(pl.program_id(1) < nst - 1, dn_ref[...], 0.0)
        for r0, c0, cw in _ffn_chunks(ts, f2):
            cols = slice(c0, c0 + cw)
            acc = jnp.zeros((FFN_ROWS, cw), F32)
            for k in range(kw):
                acc = acc + w_ref[k:k + 1, cols] * buf[pl.ds(r0 + (kw - 1) - k, FFN_ROWS), cols]
            o_ref[r0:r0 + FFN_ROWS, cols] = acc.astype(BF16)

    return pl.pallas_call(
        body, name=name, grid=(dims.batch_local, nst),
        in_specs=[*_seq_specs(dims, ts, f2, FFN_HALO, 0, nxt=True), pl.BlockSpec((FFN_HALO, f2), lambda b_, i: (0, 0))],
        out_specs=pl.BlockSpec((ts, f2), lambda b_, i: (b_ * nst + i, 0)),
        out_shape=jax.ShapeDtypeStruct((t, f2), BF16),
        scratch_shapes=[pltpu.VMEM((ts + FFN_HALO, f2), F32)],
        compiler_params=_params("parallel", "parallel"),
    )(du, du, w)


def _alibi_slope(h, n_heads):
    return 2.0 ** (-8.0 * (h + 1) / n_heads)


def _dot_nt(a, b):
    return lax.dot_general(a, b, (((1,), (1,)), ((), ())), preferred_element_type=F32)


def _dot_tn(a, b):
    return lax.dot_general(a, b, (((0,), (0,)), ((), ())), preferred_element_type=F32)


def _attn_view(x, dims, dil):
    return x.reshape(dims.batch_local, dims.seq // dil, dil * x.shape[-1])


def _attn_fwd_group(q, k, v, state, dims, dil, *, last, name):
    t, a = q.shape
    assert 2 * dims.head_dim == 128 and dims.n_heads % 2 == 0
    blk, hd = ATTN_BLOCK, dims.head_dim
    nb = dims.seq // dil // blk
    has_prev = nb > 1
    nkeys = 2 * blk if has_prev else blk

    def body(*refs):
        it = iter(refs)
        q_ref, kc_ref, vc_ref = next(it), next(it), next(it)
        kp_ref, vp_ref = (next(it), next(it)) if has_prev else (None, None)
        m_in, l_in, acc_in = (next(it), next(it), next(it)) if state is not None else (None, None, None)
        outs = list(it)
        iq = lax.broadcasted_iota(jnp.int32, (blk, nkeys), 0)
        jk = lax.broadcasted_iota(jnp.int32, (blk, nkeys), 1)
        if has_prev:
            steps = iq + blk - jk
            valid = (steps >= 0) & (steps <= blk) & ((jk >= blk) | (pl.program_id(2) > 0))
        else:
            steps = iq - jk
            valid = steps >= 0
        dist = steps.astype(F32) * float(dil)
        low = lax.broadcasted_iota(jnp.int32, (blk, 2 * hd), 1) < hd
        for hp in range(dims.n_heads // 2):
            sl = slice(2 * hd * hp, 2 * hd * (hp + 1))
            q2 = q_ref[:, sl]
            if has_prev:
                kcat = jnp.concatenate([kp_ref[:, sl], kc_ref[:, sl]], axis=0)
                vcat = jnp.concatenate([vp_ref[:, sl], vc_ref[:, sl]], axis=0)
            else:
                kcat, vcat = kc_ref[:, sl], vc_ref[:, sl]
            halves = []
            for half in range(2):
                col = 2 * hd * hp + hd * half
                qh = jnp.where(low if half == 0 else jnp.logical_not(low), q2, jnp.zeros_like(q2))
                sc = _dot_nt(qh, kcat) - _alibi_slope(2 * hp + half, dims.n_heads) * dist
                sc = jnp.where(valid, sc, MASKED_SCORE)
                row_max = jnp.max(sc, axis=-1, keepdims=True)
                if state is None:
                    m_new = row_max
                    p = jnp.exp(sc - m_new)
                    alpha = None
                    l_new = jnp.sum(p, axis=-1, keepdims=True)
                else:
                    m_old = m_in[:, col:col + 1]
                    m_new = jnp.maximum(m_old, row_max)
                    p = jnp.exp(sc - m_new)
                    alpha = jnp.exp(m_old - m_new)
                    l_new = alpha * l_in[:, col:col + 1] + jnp.sum(p, axis=-1, keepdims=True)
                pv = jnp.dot(p.astype(BF16), vcat, preferred_element_type=F32)
                halves.append((m_new, l_new, alpha, pv))
            (m_a, l_a, al_a, pv_a), (m_b, l_b, al_b, pv_b) = halves
            if state is None:
                acc = jnp.where(low, pv_a, pv_b)
            else:
                old = acc_in[:, sl]
                acc = jnp.where(low, al_a * old + pv_a, al_b * old + pv_b)
            m2 = jnp.where(low, m_a, m_b)
            l2 = jnp.where(low, l_a, l_b)
            if last:
                outs[0][:, sl] = (acc / l2).astype(BF16)
                outs[1][:, sl] = m2 + jnp.log(l2)
            else:
                outs[0][:, sl] = m2
                outs[1][:, sl] = l2
                outs[2][:, sl] = acc

    cur = pl.BlockSpec((None, blk, a), lambda b, r, i: (b, i, r))
    prev = pl.BlockSpec((None, blk, a), lambda b, r, i: (b, jnp.maximum(i - 1, 0), r))
    args, in_specs = [q, k, v], [cur, cur, cur]
    if has_prev:
        args += [k, v]
        in_specs += [prev, prev]
    if state is not None:
        args += list(state)
        in_specs += [cur] * 3
    shape = lambda dt: jax.ShapeDtypeStruct((dims.batch_local, dims.seq // dil, dil * a), dt)
    out_shape = [shape(BF16), shape(F32)] if last else [shape(F32)] * 3
    outs = pl.pallas_call(
        body, name=name, grid=(dims.batch_local, dil, nb),
        in_specs=in_specs, out_specs=[cur] * len(out_shape), out_shape=out_shape,
        compiler_params=_params("parallel", "parallel", "parallel"),
    )(*[_attn_view(x, dims, dil) for x in args])
    return tuple(o.reshape(t, a) for o in outs)


def _attn_delta(do, o, head_ones, dims, *, name, tr=512):
    t, a = o.shape
    tr = _pick(t, tr, 8)

    def body(do_ref, o_ref, e_ref, d_ref):
        prod = do_ref[...].astype(F32) * o_ref[...].astype(F32)
        d_ref[...] = _head_mean(prod, e_ref, dims.head_dim) * float(dims.head_dim)

    return pl.pallas_call(
        body, name=name, grid=(t // tr,),
        in_specs=[_row_spec(tr, a), _row_spec(tr, a), pl.BlockSpec((a, a), lambda i: (0, 0))],
        out_specs=_row_spec(tr, a), out_shape=jax.ShapeDtypeStruct((t, a), F32),
        compiler_params=_params("parallel"),
    )(do, o, head_ones)


def _attn_bwd_group(q, k, v, do, lse, delta, dims, dil, *, name):
    t, a = q.shape
    blk, hd = ATTN_BLOCK, dims.head_dim
    nb = dims.seq // dil // blk
    has_next = nb > 1

    def body(*refs):
        k_ref, v_ref, q_ref, do_ref, lse_ref, dl_ref = refs[:6]
        if has_next:
            qn_ref, don_ref, lsen_ref, dln_ref = refs[6:10]
            dq_ref, dk_ref, dv_ref, carry = refs[10:]
        else:
            dq_ref, dk_ref, dv_ref = refs[6:]
        j = pl.program_id(2)
        iq = lax.broadcasted_iota(jnp.int32, (blk, blk), 0)
        jk = lax.broadcasted_iota(jnp.int32, (blk, blk), 1)
        low = lax.broadcasted_iota(jnp.int32, (blk, 2 * hd), 1) < hd

        def pair(hp, qr, dor, lser, dlr, steps, valid):
            sl = slice(2 * hd * hp, 2 * hd * (hp + 1))
            q2, do2, k2, v2 = qr[:, sl], dor[:, sl], k_ref[:, sl], v_ref[:, sl]
            dist = steps.astype(F32) * float(dil)
            dq_h, dk2, dv2 = [], None, None
            for half in range(2):
                col = 2 * hd * hp + hd * half
                mask = low if half == 0 else jnp.logical_not(low)
                qh = jnp.where(mask, q2, jnp.zeros_like(q2))
                doh = jnp.where(mask, do2, jnp.zeros_like(do2))
                sc = _dot_nt(qh, k2) - _alibi_slope(2 * hp + half, dims.n_heads) * dist
                p = jnp.where(valid, jnp.exp(sc - lser[:, col:col + 1]), 0.0)
                ds = p * (_dot_nt(doh, v2) - dlr[:, col:col + 1])
                ds_b, p_b = ds.astype(BF16), p.astype(BF16)
                dq_h.append(jnp.dot(ds_b, k2, preferred_element_type=F32))
                dk_h, dv_h = _dot_tn(ds_b, qh), _dot_tn(p_b, doh)
                dk2 = dk_h if dk2 is None else dk2 + dk_h
                dv2 = dv_h if dv2 is None else dv2 + dv_h
            return sl, jnp.where(low, dq_h[0], dq_h[1]), dk2, dv2

        if has_next:
            @pl.when(j == 0)
            def _():
                carry[...] = jnp.zeros_like(carry)

        for hp in range(dims.n_heads // 2):
            sl, dq2, dk2, dv2 = pair(hp, q_ref, do_ref, lse_ref, dl_ref, iq - jk, iq >= jk)
            dq_ref[:, sl] = (carry[:, sl] + dq2) if has_next else dq2
            dk_ref[:, sl] = dk2
            dv_ref[:, sl] = dv2

        if has_next:
            @pl.when(j + 1 < nb)
            def _():
                for hp in range(dims.n_heads // 2):
                    sl, dq2, dk2, dv2 = pair(hp, qn_ref, don_ref, lsen_ref, dln_ref, iq - jk + blk, jk >= iq)
                    carry[:, sl] = dq2
                    dk_ref[:, sl] += dk2
                    dv_ref[:, sl] += dv2

    cur = pl.BlockSpec((None, blk, a), lambda b, r, j: (b, j, r))
    nxt = pl.BlockSpec((None, blk, a), lambda b, r, j: (b, jnp.minimum(j + 1, nb - 1), r))
    args, in_specs = [k, v, q, do, lse, delta], [cur] * 6
    if has_next:
        args += [q, do, lse, delta]
        in_specs += [nxt] * 4
    shape = jax.ShapeDtypeStruct((dims.batch_local, dims.seq // dil, dil * a), F32)
    outs = pl.pallas_call(
        body, name=name, grid=(dims.batch_local, dil, nb),
        in_specs=in_specs, out_specs=[cur] * 3, out_shape=[shape] * 3,
        scratch_shapes=[pltpu.VMEM((blk, a), F32)] if has_next else [],
        compiler_params=_params("parallel", "parallel", "arbitrary"),
    )(*[_attn_view(x, dims, dil) for x in args])
    return tuple(o.reshape(t, a) for o in outs)


LANES = 128
MASK_BIAS = 1e30
RESIDUE_DILATIONS = tuple(d for d in DILATIONS if d > 1)


def _rows_to_residues(value, out_ref, scr, d):
    rows, width = value.shape
    for c in range(width // LANES):
        cols = slice(LANES * c, LANES * (c + 1))
        scr[c] = value[:, cols]
        for r in range(d):
            out_ref[r, :, cols] = scr[c, pl.ds(r, rows // d, stride=d), :].astype(out_ref.dtype)


def _residues_to_rows(in_ref, scr, d):
    _, n, width = in_ref.shape
    slabs = []
    for c in range(width // LANES):
        cols = slice(LANES * c, LANES * (c + 1))
        for r in range(d):
            scr[c, pl.ds(r, n, stride=d), :] = in_ref[r, :, cols].astype(F32)
        slabs.append(scr[c])
    return slabs[0] if len(slabs) == 1 else jnp.concatenate(slabs, axis=1)


def _residue_shape(dims, d, width, dtype):
    return jax.ShapeDtypeStruct((dims.batch_local, d, dims.seq // d, width), dtype)


def _residue_spec(dims, d, tr, width):
    tiles = dims.seq // tr
    return pl.BlockSpec((None, d, tr // d, width), lambda i: (i // tiles, 0, i % tiles, 0))


def _head_sum_matrix(dims):
    a = dims.n_heads * dims.head_dim
    head = jnp.arange(a, dtype=jnp.int32) // dims.head_dim
    return (head[:, None] == jnp.arange(LANES, dtype=jnp.int32)[None, :]).astype(BF16)


def _two_pass_dot(v, m):
    hi = v.astype(BF16)
    lo = (v - hi.astype(F32)).astype(BF16)
    return jnp.dot(hi, m, preferred_element_type=F32) + jnp.dot(lo, m, preferred_element_type=F32)


def _qkv_layouts_fwd(z, gq, gk, head_ones, dims, *, name, tr=256):
    t = z.shape[0]
    a = dims.n_heads * dims.head_dim
    q_scale = dims.head_dim ** -0.5
    nres = len(RESIDUE_DILATIONS)

    def body(q_ref, k_ref, v_ref, gq_ref, gk_ref, e_ref, *rest):
        outs, scr = rest[:-1], rest[-1]
        qv, kv = q_ref[...], k_ref[...]
        rq = lax.rsqrt(_head_mean(qv * qv, e_ref, dims.head_dim) + RMS_EPS)
        rk = lax.rsqrt(_head_mean(kv * kv, e_ref, dims.head_dim) + RMS_EPS)
        values = (qv * rq * gq_ref[...] * q_scale, kv * rk * gk_ref[...], v_ref[...])
        for j, val in enumerate(values):
            outs[j][...] = val.astype(BF16)
            for g, d in enumerate(RESIDUE_DILATIONS):
                _rows_to_residues(val, outs[3 * (g + 1) + j], scr, d)

    out_specs = [_row_spec(tr, a)] * 3
    out_shape = [jax.ShapeDtypeStruct((t, a), BF16)] * 3
    for d in RESIDUE_DILATIONS:
        out_specs += [_residue_spec(dims, d, tr, a)] * 3
        out_shape += [_residue_shape(dims, d, a, BF16)] * 3
    outs = pl.pallas_call(
        body, name=name, grid=(t // tr,),
        in_specs=[_row_spec(tr, a, 2), _row_spec(tr, a, 3), _row_spec(tr, a, 4), _vec_spec(a), _vec_spec(a),
                  pl.BlockSpec((a, a), lambda i: (0, 0))],
        out_specs=out_specs, out_shape=out_shape,
        scratch_shapes=[pltpu.VMEM((a // LANES, tr, LANES), F32)],
        compiler_params=_params("parallel"),
    )(z, z, z, gq, gk, head_ones)
    return {d: tuple(outs[3 * g:3 * g + 3]) for g, d in enumerate((1,) + RESIDUE_DILATIONS)}


def _attn_specs(dims, dil, width):
    blk = ATTN_BLOCK
    nb = dims.seq // dil // blk
    if dil == 1:
        grid = (dims.batch_local, nb)
        at = lambda f: pl.BlockSpec((blk, width), lambda b, i: (b * nb + f(i), 0))
    else:
        grid = (dims.batch_local, dil, nb)
        at = lambda f: pl.BlockSpec((None, None, blk, width), lambda b, r, i: (b, r, f(i), 0))
    return grid, at(lambda i: i), at(lambda i: jnp.maximum(i - 1, 0)), at(lambda i: jnp.minimum(i + 1, nb - 1))


def _head_slopes(n_heads):
    h = lax.broadcasted_iota(jnp.int32, (n_heads, 1, 1), 0).astype(F32)
    return jnp.exp((h + 1.0) * (-8.0 / n_heads * math.log(2.0)))


def _pair_masks(hd):
    low = lax.broadcasted_iota(jnp.int32, (1, 2 * hd), 1) < hd
    return low, jnp.logical_not(low)


def _attn_fwd(q, k, v, dims, dil, *, name):
    a = dims.n_heads * dims.head_dim
    heads, hd, blk = dims.n_heads, dims.head_dim, ATTN_BLOCK
    assert 2 * hd == LANES and heads % 2 == 0 and heads <= LANES
    nb = dims.seq // dil // blk
    has_prev = nb > 1
    nkeys = 2 * blk if has_prev else blk
    grid, cur, prev, _ = _attn_specs(dims, dil, a)
    _, cur_stat, _, _ = _attn_specs(dims, dil, LANES)

    def body(*refs):
        if has_prev:
            q_ref, kc_ref, vc_ref, kp_ref, vp_ref, o_ref, lse_ref, s_scr, p_scr = refs
        else:
            q_ref, kc_ref, vc_ref, o_ref, lse_ref, s_scr, p_scr = refs
        low, high = _pair_masks(hd)

        def keys(cur_ref, prev_ref, sl):
            return jnp.concatenate([prev_ref[:, sl], cur_ref[:, sl]], axis=0) if has_prev else cur_ref[:, sl]

        for hp in range(heads // 2):
            sl = slice(LANES * hp, LANES * (hp + 1))
            q2 = q_ref[:, sl]
            kcat = keys(kc_ref, kp_ref if has_prev else None, sl)
            s_scr[2 * hp] = _dot_nt(jnp.where(low, q2, jnp.zeros_like(q2)), kcat)
            s_scr[2 * hp + 1] = _dot_nt(jnp.where(high, q2, jnp.zeros_like(q2)), kcat)

        iq = lax.broadcasted_iota(jnp.int32, (blk, nkeys), 0)
        jk = lax.broadcasted_iota(jnp.int32, (blk, nkeys), 1)
        if has_prev:
            steps = iq + blk - jk
            valid = (steps >= 0) & (steps <= blk) & ((jk >= blk) | (pl.program_id(len(grid) - 1) > 0))
        else:
            steps = iq - jk
            valid = steps >= 0
        bias = jnp.where(valid, steps.astype(F32) * (-float(dil)), -MASK_BIAS)
        s = s_scr[...] + _head_slopes(heads) * bias[None]
        m = jnp.max(s, axis=-1, keepdims=True)
        p = jnp.exp(s - m)
        l = jnp.sum(p, axis=-1, keepdims=True)
        p_scr[...] = p.astype(BF16)
        inv = 1.0 / l
        lse = m + jnp.log(l)

        lane = lax.broadcasted_iota(jnp.int32, (blk, LANES), 1)
        stat = jnp.zeros((blk, LANES), F32)
        for hp in range(heads // 2):
            sl = slice(LANES * hp, LANES * (hp + 1))
            vcat = keys(vc_ref, vp_ref if has_prev else None, sl)
            pv_a = jnp.dot(p_scr[2 * hp], vcat, preferred_element_type=F32) * inv[2 * hp]
            pv_b = jnp.dot(p_scr[2 * hp + 1], vcat, preferred_element_type=F32) * inv[2 * hp + 1]
            o_ref[:, sl] = jnp.where(low, pv_a, pv_b)
            stat = jnp.where(lane == 2 * hp, lse[2 * hp], stat)
            stat = jnp.where(lane == 2 * hp + 1, lse[2 * hp + 1], stat)
        lse_ref[...] = stat

    lead = q.shape[:-2]
    rows = q.shape[-2]
    o, lse = pl.pallas_call(
        body, name=name, grid=grid,
        in_specs=[cur, cur, cur] + ([prev, prev] if has_prev else []),
        out_specs=[cur, cur_stat],
        out_shape=[jax.ShapeDtypeStruct(lead + (rows, a), F32), jax.ShapeDtypeStruct(lead + (rows, LANES), F32)],
        scratch_shapes=[pltpu.VMEM((heads, blk, nkeys), F32), pltpu.VMEM((heads, blk, nkeys), BF16)],
        compiler_params=_params(*["parallel"] * len(grid)),
    )(q, k, v, *([k, v] if has_prev else []))
    return o, lse


def _attn_combine(groups, head_spread, dims, *, name, tr=256):
    t = dims.tokens
    a = dims.n_heads * dims.head_dim
    dils = tuple(groups)

    def body(*refs):
        ins = refs[:2 * len(dils)]
        x_ref = refs[2 * len(dils)]
        o_ref = refs[2 * len(dils) + 1]
        lse_refs = refs[2 * len(dils) + 2:-2]
        scr, scr_stat = refs[-2], refs[-1]
        outs, stats = [], []
        for g, d in enumerate(dils):
            if d == 1:
                outs.append(ins[2 * g][...])
                stats.append(ins[2 * g + 1][...])
            else:
                outs.append(_residues_to_rows(ins[2 * g], scr, d))
                stats.append(_residues_to_rows(ins[2 * g + 1], scr_stat, d))
        top = functools.reduce(jnp.maximum, stats)
        weights = [jnp.exp(s - top) for s in stats]
        total = functools.reduce(jnp.add, weights)
        joint = top + jnp.log(total)
        inv = 1.0 / total
        acc = None
        for w, o in zip(weights, outs):
            term = _two_pass_dot(w * inv, x_ref[...]) * o
            acc = term if acc is None else acc + term
        o_ref[...] = acc.astype(BF16)
        for g, d in enumerate(dils):
            if d == 1:
                lse_refs[g][...] = joint
            else:
                _rows_to_residues(joint, lse_refs[g], scr_stat, d)

    in_specs, args, lse_specs, lse_shapes = [], [], [], []
    for d in dils:
        if d == 1:
            in_specs += [_row_spec(tr, a), _row_spec(tr, LANES)]
            lse_specs.append(_row_spec(tr, LANES))
            lse_shapes.append(jax.ShapeDtypeStruct((t, LANES), F32))
        else:
            in_specs += [_residue_spec(dims, d, tr, a), _residue_spec(dims, d, tr, LANES)]
            lse_specs.append(_residue_spec(dims, d, tr, LANES))
            lse_shapes.append(_residue_shape(dims, d, LANES, F32))
        args += list(groups[d])
    outs = pl.pallas_call(
        body, name=name, grid=(t // tr,),
        in_specs=in_specs + [pl.BlockSpec((LANES, a), lambda i: (0, 0))],
        out_specs=[_row_spec(tr, a)] + lse_specs,
        out_shape=[jax.ShapeDtypeStruct((t, a), BF16)] + lse_shapes,
        scratch_shapes=[pltpu.VMEM((a // LANES, tr, LANES), F32), pltpu.VMEM((1, tr, LANES), F32)],
        compiler_params=_params("parallel"),
    )(*args, head_spread)
    return outs[0], dict(zip(dils, outs[1:]))


def _attn_bwd_prep(do, o, head_sum, dims, *, name, tr=256):
    t, a = o.shape

    def body(do_ref, o_ref, e_ref, *rest):
        outs, scr, scr_stat = rest[:-2], rest[-2], rest[-1]
        dov = do_ref[...].astype(F32)
        delta = _two_pass_dot(dov * o_ref[...].astype(F32), e_ref[...])
        outs[0][...] = delta
        for g, d in enumerate(RESIDUE_DILATIONS):
            _rows_to_residues(dov, outs[1 + 2 * g], scr, d)
            _rows_to_residues(delta, outs[2 + 2 * g], scr_stat, d)

    out_specs, out_shape = [_row_spec(tr, LANES)], [jax.ShapeDtypeStruct((t, LANES), F32)]
    for d in RESIDUE_DILATIONS:
        out_specs += [_residue_spec(dims, d, tr, a), _residue_spec(dims, d, tr, LANES)]
        out_shape += [_residue_shape(dims, d, a, BF16), _residue_shape(dims, d, LANES, F32)]
    outs = pl.pallas_call(
        body, name=name, grid=(t // tr,),
        in_specs=[_row_spec(tr, a), _row_spec(tr, a), pl.BlockSpec((a, LANES), lambda i: (0, 0))],
        out_specs=out_specs, out_shape=out_shape,
        scratch_shapes=[pltpu.VMEM((a // LANES, tr, LANES), F32), pltpu.VMEM((1, tr, LANES), F32)],
        compiler_params=_params("parallel"),
    )(do, o, head_sum)
    dos, deltas = {1: do}, {1: outs[0]}
    for g, d in enumerate(RESIDUE_DILATIONS):
        dos[d], deltas[d] = outs[1 + 2 * g], outs[2 + 2 * g]
    return dos, deltas


def _attn_bwd(q, k, v, do, lse, delta, dims, dil, *, name):
    a = dims.n_heads * dims.head_dim
    heads, hd, blk = dims.n_heads, dims.head_dim, ATTN_BLOCK
    nb = dims.seq // dil // blk
    has_next = nb > 1
    nq = 2 * blk if has_next else blk
    grid, cur, _, nxt = _attn_specs(dims, dil, a)
    _, cur_stat, _, nxt_stat = _attn_specs(dims, dil, LANES)

    def body(*refs):
        k_ref, v_ref, q_ref, do_ref, lse_ref, dl_ref = refs[:6]
        if has_next:
            qn_ref, don_ref, lsen_ref, dln_ref = refs[6:10]
            dq_ref, dk_ref, dv_ref, s_scr, dp_scr, p_scr, ds_scr, carry = refs[10:]
        else:
            dq_ref, dk_ref, dv_ref, s_scr, dp_scr, p_scr, ds_scr = refs[6:]
        j = pl.program_id(len(grid) - 1)
        low, high = _pair_masks(hd)

        def stacked(ref, nref, sl):
            return jnp.concatenate([ref[:, sl], nref[:, sl]], axis=0) if has_next else ref[:, sl]

        def halves(x):
            return jnp.where(low, x, jnp.zeros_like(x)), jnp.where(high, x, jnp.zeros_like(x))

        for hp in range(heads // 2):
            sl = slice(LANES * hp, LANES * (hp + 1))
            k2, v2 = k_ref[:, sl], v_ref[:, sl]
            q_a, q_b = halves(stacked(q_ref, qn_ref if has_next else None, sl))
            do_a, do_b = halves(stacked(do_ref, don_ref if has_next else None, sl))
            s_scr[2 * hp], s_scr[2 * hp + 1] = _dot_nt(q_a, k2), _dot_nt(q_b, k2)
            dp_scr[2 * hp], dp_scr[2 * hp + 1] = _dot_nt(do_a, v2), _dot_nt(do_b, v2)

        rq = lax.broadcasted_iota(jnp.int32, (nq, blk), 0)
        jk = lax.broadcasted_iota(jnp.int32, (nq, blk), 1)
        if has_next:
            iq = jnp.where(rq < blk, rq, rq - blk)
            steps = jnp.where(rq < blk, iq - jk, iq - jk + blk)
            valid = ((rq < blk) & (iq >= jk)) | ((rq >= blk) & (jk >= iq) & (j + 1 < nb))
        else:
            steps, valid = rq - jk, rq >= jk
        bias = jnp.where(valid, steps.astype(F32) * (-float(dil)), -MASK_BIAS)
        lse_all = stacked(lse_ref, lsen_ref if has_next else None, slice(None))
        dl_all = stacked(dl_ref, dln_ref if has_next else None, slice(None))
        lse3 = jnp.stack([lse_all[:, h:h + 1] for h in range(heads)])
        dl3 = jnp.stack([dl_all[:, h:h + 1] for h in range(heads)])
        p = jnp.exp(s_scr[...] + _head_slopes(heads) * bias[None] - lse3)
        p_scr[...] = p.astype(BF16)
        ds_scr[...] = (p * (dp_scr[...] - dl3)).astype(BF16)

        if has_next:
            @pl.when(j == 0)
            def _():
                carry[...] = jnp.zeros_like(carry)

        for hp in range(heads // 2):
            sl = slice(LANES * hp, LANES * (hp + 1))
            k2 = k_ref[:, sl]
            q_a, q_b = halves(stacked(q_ref, qn_ref if has_next else None, sl))
            do_a, do_b = halves(stacked(do_ref, don_ref if has_next else None, sl))
            ds_a, ds_b = ds_scr[2 * hp], ds_scr[2 * hp + 1]
            dq2 = jnp.where(low, jnp.dot(ds_a, k2, preferred_element_type=F32),
                            jnp.dot(ds_b, k2, preferred_element_type=F32))
            dk_ref[:, sl] = _dot_tn(ds_a, q_a) + _dot_tn(ds_b, q_b)
            dv_ref[:, sl] = _dot_tn(p_scr[2 * hp], do_a) + _dot_tn(p_scr[2 * hp + 1], do_b)
            if has_next:
                dq_ref[:, sl] = carry[:, sl] + dq2[:blk]
                carry[:, sl] = dq2[blk:]
            else:
                dq_ref[:, sl] = dq2

    args, in_specs = [k, v, q, do, lse, delta], [cur] * 4 + [cur_stat] * 2
    if has_next:
        args += [q, do, lse, delta]
        in_specs += [nxt] * 2 + [nxt_stat] * 2
    shape = jax.ShapeDtypeStruct(q.shape, F32)
    scratch = [pltpu.VMEM((heads, nq, blk), F32)] * 2 + [pltpu.VMEM((heads, nq, blk), BF16)] * 2
    if has_next:
        scratch.append(pltpu.VMEM((blk, a), F32))
    return pl.pallas_call(
        body, name=name, grid=grid, in_specs=in_specs, out_specs=[cur] * 3, out_shape=[shape] * 3,
        scratch_shapes=scratch,
        compiler_params=_params(*["parallel"] * (len(grid) - 1), "arbitrary"),
    )(*args)


def _qkv_layouts_bwd(z, grads, gq, gk, head_ones, dims, *, name, tr=256):
    t = z.shape[0]
    a = dims.n_heads * dims.head_dim
    q_scale = dims.head_dim ** -0.5
    dils = tuple(grads)

    def body(q_ref, k_ref, *rest):
        d_refs = rest[:3 * len(dils)]
        gq_ref, gk_ref, e_ref, dz_ref, dgq_ref, dgk_ref, scr = rest[3 * len(dils):]
        first = pl.program_id(0) == 0

        def total(j):
            acc = None
            for g, d in enumerate(dils):
                ref = d_refs[3 * g + j]
                part = ref[...] if d == 1 else _residues_to_rows(ref, scr, d)
                acc = part if acc is None else acc + part
            return acc

        def norm_bwd(x_ref, dy, g_ref, scale, col, dg_ref):
            xv = x_ref[...]
            dy = dy * scale
            r = lax.rsqrt(_head_mean(xv * xv, e_ref, dims.head_dim) + RMS_EPS)
            gy = dy * g_ref[...]
            dx = r * gy - xv * (r * r * r) * _head_mean(xv * gy, e_ref, dims.head_dim)
            dz_ref[:, col * a:(col + 1) * a] = dx.astype(BF16)
            _accumulate(dg_ref, jnp.sum(dy * xv * r, axis=0, keepdims=True), first)

        norm_bwd(q_ref, total(0), gq_ref, q_scale, 0, dgq_ref)
        norm_bwd(k_ref, total(1), gk_ref, 1.0, 1, dgk_ref)
        dz_ref[:, 2 * a:3 * a] = total(2).astype(BF16)

    in_specs, args = [_row_spec(tr, a, 2), _row_spec(tr, a, 3)], [z, z]
    for d in dils:
        in_specs += [_row_spec(tr, a) if d == 1 else _residue_spec(dims, d, tr, a)] * 3
        args += list(grads[d])
    in_specs += [_vec_spec(a), _vec_spec(a), pl.BlockSpec((a, a), lambda i: (0, 0))]
    return pl.pallas_call(
        body, name=name, grid=(t // tr,), in_specs=in_specs,
        out_specs=[_row_spec(tr, 3 * a), _vec_spec(a), _vec_spec(a)],
        out_shape=[jax.ShapeDtypeStruct((t, 3 * a), BF16)] + [jax.ShapeDtypeStruct((1, a), F32)] * 2,
        scratch_shapes=[pltpu.VMEM((a // LANES, tr, LANES), F32)],
        compiler_params=_params("arbitrary"),
    )(*args, gq, gk, head_ones)


def _mix_fwd(ya, yb, z, gate_b, dims, *, name, tr=512):
    t, d = ya.shape
    tr = _pick(t, tr, 8)
    first_gate_col = z.shape[1] // d - 2

    def body(ya_ref, yb_ref, ga_ref, gb_ref, ba_ref, bb_ref, o_ref):
        g_a = _sigmoid(ga_ref[...] + ba_ref[...])
        g_b = _sigmoid(gb_ref[...] + bb_ref[...])
        o_ref[...] = (g_a * ya_ref[...] + g_b * yb_ref[...]).astype(BF16)

    return pl.pallas_call(
        body, name=name, grid=(t // tr,),
        in_specs=[_row_spec(tr, d), _row_spec(tr, d), _row_spec(tr, d, first_gate_col),
                  _row_spec(tr, d, first_gate_col + 1), _vec_spec(d, 0), _vec_spec(d, 1)],
        out_specs=_row_spec(tr, d), out_shape=jax.ShapeDtypeStruct((t, d), BF16),
        compiler_params=_params("parallel"),
    )(ya, yb, z, z, gate_b, gate_b)


def _mix_bwd(dmix, ya, yb, z, gate_b, dims, *, name, tr=512):
    t, d = ya.shape
    tr = _pick(t, tr, 8)
    first_gate_col = z.shape[1] // d - 2

    def body(dm_ref, ya_ref, yb_ref, ga_ref, gb_ref, ba_ref, bb_ref, dya_ref, dyb_ref, dz_ref, db_ref):
        dm = dm_ref[...].astype(F32)
        g_a = _sigmoid(ga_ref[...] + ba_ref[...])
        g_b = _sigmoid(gb_ref[...] + bb_ref[...])
        dya_ref[...] = (dm * g_a).astype(BF16)
        dyb_ref[...] = (dm * g_b).astype(BF16)
        dl_a = dm * ya_ref[...] * g_a * (1.0 - g_a)
        dl_b = dm * yb_ref[...] * g_b * (1.0 - g_b)
        dz_ref[:, 0:d] = dl_a.astype(BF16)
        dz_ref[:, d:2 * d] = dl_b.astype(BF16)
        first = pl.program_id(0) == 0
        sums = jnp.concatenate([jnp.sum(dl_a, axis=0, keepdims=True), jnp.sum(dl_b, axis=0, keepdims=True)], axis=1)
        _accumulate(db_ref, sums, first)

    return pl.pallas_call(
        body, name=name, grid=(t // tr,),
        in_specs=[_row_spec(tr, d), _row_spec(tr, d), _row_spec(tr, d), _row_spec(tr, d, first_gate_col),
                  _row_spec(tr, d, first_gate_col + 1), _vec_spec(d, 0), _vec_spec(d, 1)],
        out_specs=[_row_spec(tr, d), _row_spec(tr, d), _row_spec(tr, 2 * d), _vec_spec(2 * d)],
        out_shape=[jax.ShapeDtypeStruct((t, d), BF16)] * 2 + [jax.ShapeDtypeStruct((t, 2 * d), BF16),
                                                              jax.ShapeDtypeStruct((1, 2 * d), F32)],
        compiler_params=_params("arbitrary"),
    )(dmix, ya, yb, z, z, gate_b, gate_b)


def _loss_head(y, target, *, name, tr=512):
    t, d = y.shape
    tr = _pick(t, tr, 8)

    def body(y_ref, t_ref, dy_ref, dyb_ref, loss_ref):
        err = y_ref[...] - t_ref[...]
        dy = err * (1.0 / d)
        dy_ref[...] = dy
        dyb_ref[...] = dy.astype(BF16)
        part = jnp.sum(jnp.sum(err * err, axis=-1, keepdims=True), axis=0, keepdims=True) * (0.5 / d)
        _accumulate(loss_ref, jnp.broadcast_to(part, (8, 128)), pl.program_id(0) == 0)

    return pl.pallas_call(
        body, name=name, grid=(t // tr,),
        in_specs=[_row_spec(tr, d), _row_spec(tr, d)],
        out_specs=[_row_spec(tr, d), _row_spec(tr, d), pl.BlockSpec((8, 128), lambda i: (0, 0))],
        out_shape=[jax.ShapeDtypeStruct((t, d), F32), jax.ShapeDtypeStruct((t, d), BF16),
                   jax.ShapeDtypeStruct((8, 128), F32)],
        compiler_params=_params("arbitrary"),
    )(y, target)


def _adamw(w, grads, m, v, *, name, tr=256):
    r, c = w.shape
    tr = _pick(r, tr, 8)
    ng = len(grads)
    c1 = 1.0 - ADAM_B1 ** ADAM_STEP
    c2 = 1.0 - ADAM_B2 ** ADAM_STEP

    def body(*refs):
        w_ref, g_refs, m_ref, v_ref = refs[0], refs[1:1 + ng], refs[1 + ng], refs[2 + ng]
        g_out, d_out, m_out, v_out = refs[3 + ng:]
        g = g_refs[0][...]
        for extra in g_refs[1:]:
            g = g + extra[...]
        m_new = ADAM_B1 * m_ref[...] + (1.0 - ADAM_B1) * g
        v_new = ADAM_B2 * v_ref[...] + (1.0 - ADAM_B2) * (g * g)
        g_out[...] = g
        m_out[...] = m_new
        v_out[...] = v_new
        d_out[...] = -ADAM_LR * ((m_new / c1) / (jnp.sqrt(v_new / c2) + ADAM_EPS) + ADAM_WD * w_ref[...])

    spec = pl.BlockSpec((tr, c), lambda i: (i, 0))
    return pl.pallas_call(
        body, name=name, grid=(r // tr,),
        in_specs=[spec] * (3 + ng), out_specs=[spec] * 4, out_shape=[jax.ShapeDtypeStruct((r, c), F32)] * 4,
        compiler_params=_params("parallel"),
    )(w, *grads, m, v)


CHIP_PEERS = ((1, 0), (0, 1), (1, 1))


def _place():
    return lax.axis_index("x"), lax.axis_index("y"), lax.axis_index("c")


HBM = pl.BlockSpec(memory_space=pltpu.HBM)
SEM = pl.BlockSpec(memory_space=pltpu.SEMAPHORE)
IN_FLIGHT = pltpu.SideEffectType.DATAFLOW_SIDE_EFFECTING


def _in_hbm(a):
    return pltpu.with_memory_space_constraint(a, pltpu.HBM)


def _cast_to_lands(shards, dtypes, *, name):
    n = len(shards)

    def body(*refs):
        ins, outs, bufs, sems = refs[:n], refs[n:2 * n], refs[2 * n:3 * n], refs[3 * n]
        x, y, _ = _place()
        copies = []
        for a in range(n):
            bufs[a][...] = ins[a][...].astype(dtypes[a])
            cp = pltpu.make_async_copy(bufs[a], outs[a].at[2 * x + y], sems.at[a])
            cp.start()
            copies.append(cp)
        for cp in copies:
            cp.wait()

    return pl.pallas_call(
        body, name=name, in_specs=[pl.BlockSpec(memory_space=pltpu.VMEM)] * n, out_specs=[ANY] * n,
        out_shape=[jax.ShapeDtypeStruct((N_CHIPS,) + s.shape, dt) for s, dt in zip(shards, dtypes)],
        scratch_shapes=[pltpu.VMEM(s.shape, dt) for s, dt in zip(shards, dtypes)] + [pltpu.SemaphoreType.DMA((n,))],
        compiler_params=pltpu.CompilerParams(vmem_limit_bytes=V7X_VMEM_LIMIT_BYTES),
    )(*shards)


def _chip_copy(src, dst, send, recv, flip, place):
    x, y, c = place
    return pltpu.make_async_remote_copy(src_ref=src, dst_ref=dst, send_sem=send, recv_sem=recv,
                                        device_id=(x ^ flip[0], y ^ flip[1], c), device_id_type=MESH)


def _gather_start(lands, after, *, name):
    n = len(lands)

    def body(*refs):
        ins, send, recv, token = refs[:n], refs[n + 1], refs[n + 2], refs[-1]
        place = _place()
        me = 2 * place[0] + place[1]
        for a in range(n):
            for p, flip in enumerate(CHIP_PEERS):
                k = 3 * a + p
                _chip_copy(ins[a].at[me], ins[a].at[me], send.at[k], recv.at[k], flip, place).start()
        token[...] = jnp.zeros_like(token)

    outs = pl.pallas_call(
        body, name=name, in_specs=[HBM] * n + [ANY],
        out_specs=(SEM, SEM, *[HBM] * n, pl.BlockSpec(memory_space=pltpu.VMEM)),
        out_shape=(pltpu.SemaphoreType.DMA((3 * n,)), pltpu.SemaphoreType.DMA((3 * n,)),
                   *[pltpu.HBM(l.shape, l.dtype) for l in lands], jax.ShapeDtypeStruct((8, 128), F32)),
        input_output_aliases={a: 2 + a for a in range(n)},
        compiler_params=pltpu.CompilerParams(has_side_effects=IN_FLIGHT),
    )(*[_in_hbm(l) for l in lands], after)
    return outs[0], outs[1], list(outs[2:2 + n]), outs[-1]


def _gather_wait(send, recv, lands, after, *, name):
    n = len(lands)

    def body(*refs):
        ins, send_ref, recv_ref = refs[:n], refs[n], refs[n + 1]
        place = _place()
        me = 2 * place[0] + place[1]
        for a in range(n):
            for p, flip in enumerate(CHIP_PEERS):
                k = 3 * a + p
                cp = _chip_copy(ins[a].at[me], ins[a].at[me], send_ref.at[k], recv_ref.at[k], flip, place)
                cp.wait_send()
                cp.wait_recv()

    return pl.pallas_call(
        body, name=name, in_specs=[HBM] * n + [SEM, SEM, ANY], out_specs=[HBM] * n,
        out_shape=[pltpu.HBM(l.shape, l.dtype) for l in lands],
        input_output_aliases={a: a for a in range(n)},
        compiler_params=pltpu.CompilerParams(has_side_effects=IN_FLIGHT),
    )(*lands, send, recv, after)


def _scatter_start(grad, *, name):
    def body(g_ref, land_ref, send, recv, g_thru, land_thru, token):
        place = _place()
        for p, flip in enumerate(CHIP_PEERS):
            peer_chip = 2 * (place[0] ^ flip[0]) + (place[1] ^ flip[1])
            _chip_copy(g_ref.at[peer_chip], land_ref.at[p], send.at[p], recv.at[p], flip, place).start()
        token[...] = jnp.zeros_like(token)

    land = lax.empty((3,) + grad.shape[1:], grad.dtype)
    return pl.pallas_call(
        body, name=name, in_specs=[HBM, HBM],
        out_specs=(SEM, SEM, HBM, HBM, pl.BlockSpec(memory_space=pltpu.VMEM)),
        out_shape=(pltpu.SemaphoreType.DMA((3,)), pltpu.SemaphoreType.DMA((3,)), pltpu.HBM(grad.shape, grad.dtype),
                   pltpu.HBM(land.shape, land.dtype), jax.ShapeDtypeStruct((8, 128), F32)),
        input_output_aliases={0: 2, 1: 3},
        compiler_params=pltpu.CompilerParams(has_side_effects=IN_FLIGHT),
    )(_in_hbm(grad), _in_hbm(land))


def _scatter_wait(started, after, *, name):
    n = len(started)

    def body(*refs):
        grads, lands = refs[:n], refs[n:2 * n]
        sends, recvs = refs[2 * n:3 * n], refs[3 * n:4 * n]
        place = _place()
        for a in range(n):
            for p, flip in enumerate(CHIP_PEERS):
                cp = _chip_copy(grads[a].at[0], lands[a].at[p], sends[a].at[p], recvs[a].at[p], flip, place)
                cp.wait_send()
                cp.wait_recv()

    grads, lands = [s[2] for s in started], [s[3] for s in started]
    outs = pl.pallas_call(
        body, name=name, in_specs=[HBM] * (2 * n) + [SEM] * (2 * n) + [ANY], out_specs=[HBM] * (2 * n),
        out_shape=[pltpu.HBM(a.shape, a.dtype) for a in grads + lands],
        input_output_aliases={a: a for a in range(2 * n)},
        compiler_params=pltpu.CompilerParams(has_side_effects=IN_FLIGHT),
    )(*grads, *lands, *[s[0] for s in started], *[s[1] for s in started], after)
    return list(zip(outs[:n], outs[n:]))


def _swap_sibling(arrays, *, name):
    n = len(arrays)

    def body(*refs):
        ins, outs = refs[:n], refs[n:2 * n]
        send, recv = refs[2 * n:]
        x, y, c = _place()
        started = []
        for a in range(n):
            rc = pltpu.make_async_remote_copy(
                src_ref=ins[a], dst_ref=outs[a], send_sem=send.at[a], recv_sem=recv.at[a],
                device_id=(x, y, 1 - c), device_id_type=MESH)
            rc.start()
            started.append(rc)
        for rc in started:
            rc.wait()

    return pl.pallas_call(
        body, name=name, in_specs=[ANY] * n, out_specs=[ANY] * n,
        out_shape=[jax.ShapeDtypeStruct(g.shape, g.dtype) for g in arrays],
        scratch_shapes=[pltpu.SemaphoreType.DMA((n,)), pltpu.SemaphoreType.DMA((n,))],
    )(*arrays)


def _sum_received(grad, land, *, name, tr=256):
    _, r, c = grad.shape
    tr = _pick(r, tr, 8)

    def body(chip_ref, g_ref, l_ref, o_ref):
        o_ref[...] = ((g_ref[...] + l_ref[0].astype(F32)) + l_ref[1].astype(F32)) + l_ref[2].astype(F32)

    chip = (2 * lax.axis_index("x") + lax.axis_index("y")).astype(jnp.int32).reshape(1)
    return pl.pallas_call(
        body, name=name,
        grid_spec=pltpu.PrefetchScalarGridSpec(
            num_scalar_prefetch=1, grid=(r // tr,),
            in_specs=[pl.BlockSpec((None, tr, c), lambda i, chip_ref: (chip_ref[0], i, 0)),
                      pl.BlockSpec((3, tr, c), lambda i, chip_ref: (0, i, 0))],
            out_specs=pl.BlockSpec((tr, c), lambda i, chip_ref: (i, 0))),
        out_shape=jax.ShapeDtypeStruct((r, c), F32), compiler_params=_params("parallel"),
    )(chip, grad, land)


def _allreduce_small(packed, *, name, after=None):
    r, d = packed.shape
    n_dev = 8

    def body(src_ref, out_ref, buf, send, recv):
        x, y, c = _place()
        me = 4 * x + 2 * y + c
        started = []
        for p in range(1, n_dev):
            rc = pltpu.make_async_remote_copy(
                src_ref=src_ref, dst_ref=buf.at[me], send_sem=send.at[p - 1], recv_sem=recv.at[p - 1],
                device_id=(x ^ (p >> 2), y ^ ((p >> 1) & 1), c ^ (p & 1)), device_id_type=MESH)
            rc.start()
            started.append(rc)
        buf[me] = src_ref[...]
        for rc in started:
            rc.wait()
        total = buf[0]
        for s in range(1, n_dev):
            total = total + buf[s]
        out_ref[...] = total

    vmem = pl.BlockSpec(memory_space=pltpu.VMEM)
    body, more_specs, more_args = _ordered(body, 1, after)
    return pl.pallas_call(
        body, name=name, in_specs=[vmem] + more_specs, out_specs=vmem, out_shape=jax.ShapeDtypeStruct((r, d), F32),
        scratch_shapes=[pltpu.VMEM((n_dev, r, d), F32), pltpu.SemaphoreType.DMA((n_dev - 1,)),
                        pltpu.SemaphoreType.DMA((n_dev - 1,))],
    )(packed, *more_args)


def _packed_rows(size, d):
    return -(-size // (8 * d)) * 8


def _pack_rows(arrays, d):
    rows = []
    for arr in arrays:
        flat = arr.reshape(-1).astype(F32)
        n = _packed_rows(flat.shape[0], d)
        rows.append(jnp.pad(flat, (0, n * d - flat.shape[0])).reshape(n, d))
    return jnp.concatenate(rows, axis=0)


def _unpack_rows(packed, shapes, d):
    out, row = [], 0
    for shape in shapes:
        size = math.prod(shape)
        n = _packed_rows(size, d)
        out.append(packed[row:row + n].reshape(-1)[:size].reshape(shape))
        row += n
    return out


SMALL = ("norm1_g", "gate_b", "conv_b", "conv_norm_g", "q_norm_g", "k_norm_g", "norm2_g", "ffn_conv_b")
LARGE = ("w_in", "w_conv_out", "w_attn_out", "w_out", "w_up", "w_down")
WEIGHTS = ("norm1_g", "w_in", "gate_b", "conv_w", "conv_b", "conv_norm_g", "w_conv_out", "q_norm_g", "k_norm_g",
           "w_attn_out", "w_out", "norm2_g", "w_up", "ffn_conv_w", "ffn_conv_b", "w_down")


def _head_ones(dims):
    a = dims.n_heads * dims.head_dim
    head = jnp.arange(a, dtype=jnp.int32) // dims.head_dim
    return (head[:, None] == head[None, :]).astype(BF16)


def _after(vec, token):
    return vec if token is None else vec + token[0:1, 0:1]


def _local_step(dims, x, target, small, first_weights, other_weights, send_grad):
    d, f, heads = dims.d_model, dims.d_ff, dims.n_heads
    small = dict(small)
    row = lambda name: small[name].reshape(1, -1)
    ones = _head_ones(dims)
    head_sum = _head_sum_matrix(dims)
    gq = jnp.tile(row("q_norm_g"), (1, heads))
    gk = jnp.tile(row("k_norm_g"), (1, heads))
    one_shard = lambda w: w.reshape(1, -1, w.shape[-1])

    h = _rmsnorm_fwd(x, row("norm1_g"), name="norm1")
    full = first_weights(h)
    w_in = full["w_in"]
    conv_w = jnp.pad(full["conv_w"], ((0, CONV_HALO - dims.conv_width), (0, 0)))
    ffn_w = jnp.pad(full["ffn_conv_w"], ((0, FFN_HALO - dims.ffn_conv_width), (0, 0)))
    z = _mm_nn(h, w_in, out_dtype=F32, after=full.get("token"), name="in_proj")
    a1, a3 = _conv_branch_fwd(z, conv_w, row("conv_b"), row("conv_norm_g"), dims, name="conv_branch")
    qkv = _qkv_layouts_fwd(z, gq, gk, ones, dims, name="qk_norm")
    per_group = {dil: _attn_fwd(*qkv[dil], dims, dil, name=f"attn_fwd_d{dil}") for dil in DILATIONS}
    o, lse = _attn_combine(per_group, jnp.transpose(head_sum), dims, name="attn_combine")
    full = other_weights(o)
    w_up = full["w_up"]
    w_co, w_ao, w_o, w_dn = (one_shard(full[k]) for k in ("w_conv_out", "w_attn_out", "w_out", "w_down"))
    ya = _mm_nn(a3, w_co, out_dtype=F32, name="conv_out_proj")
    yb = _mm_nn(o, w_ao, out_dtype=F32, name="attn_out_proj")
    mixed = _mix_fwd(ya, yb, z, row("gate_b"), dims, name="gate_mix")
    x1 = _mm_nn(mixed, w_o, out_dtype=F32, residual=x, name="out_proj")
    h2 = _rmsnorm_fwd(x1, row("norm2_g"), name="norm2")
    up = _mm_nn(h2, w_up, out_dtype=F32, name="up_proj")
    act = _ffn_act_fwd(up, ffn_w, row("ffn_conv_b"), dims, name="ffn_act")
    x2 = _mm_nn(act, w_dn, out_dtype=F32, residual=x1, name="down_proj")
    dy, dy_b, loss = _loss_head(x2, target, name="loss_head")

    grads = {}

    def large(name, g):
        grads[name], g_bf16 = g
        return send_grad(name, g_bf16)

    sent = large("w_down", _mm_tn(act, dy_b, n_shards=1, name="dw_down"))
    dact = _mm_nt(dy_b, w_dn, out_dtype=BF16, after=sent, name="d_act")
    du, dfw, dfb = _ffn_act_bwd(dact, up, ffn_w, row("ffn_conv_b"), dims, name="ffn_act_bwd")
    grads["ffn_conv_w"], grads["ffn_conv_b"] = dfw[:dims.ffn_conv_width], dfb
    dup = _ffn_conv_bwd(du, ffn_w, dims, name="ffn_conv_bwd")
    sent = large("w_up", _mm_tn(h2, dup, n_shards=N_CHIPS, name="dw_up"))
    dh2 = _mm_nt(dup, w_up, out_dtype=F32, after=sent, name="d_h2")
    dx1, dx1_b, grads["norm2_g"] = _rmsnorm_bwd(x1, row("norm2_g"), dh2, dy, want_bf16=True, name="norm2_bwd")
    sent = large("w_out", _mm_tn(mixed, dx1_b, n_shards=1, name="dw_out"))
    dmix = _mm_nt(dx1_b, w_o, out_dtype=F32, after=sent, name="d_mix")
    dya, dyb, dz_gate, grads["gate_b"] = _mix_bwd(dmix, ya, yb, z, row("gate_b"), dims, name="gate_mix_bwd")
    sent = large("w_conv_out", _mm_tn(a3, dya, n_shards=1, name="dw_conv_out"))
    da3 = _mm_nt(dya, w_co, out_dtype=F32, after=sent, name="d_conv_act")
    da1, grads["conv_norm_g"] = _conv_norm_bwd(da3, a1, row("conv_norm_g"), name="conv_norm_bwd")
    dz_glu, dcw, grads["conv_b"] = _conv_branch_bwd(da1, z, conv_w, dims, name="conv_branch_bwd")
    grads["conv_w"] = dcw[:dims.conv_width]
    sent = large("w_attn_out", _mm_tn(o, dyb, n_shards=1, name="dw_attn_out"))
    do = _mm_nt(dyb, w_ao, out_dtype=BF16, after=sent, name="d_attn")
    dos, deltas = _attn_bwd_prep(do, o, head_sum, dims, name="attn_bwd_prep")
    dqkv = {dil: _attn_bwd(*qkv[dil], dos[dil], lse[dil], deltas[dil], dims, dil, name=f"attn_bwd_d{dil}")
            for dil in DILATIONS}
    dz_qkv, dgq, dgk = _qkv_layouts_bwd(z, dqkv, gq, gk, ones, dims, name="qk_norm_bwd")
    grads["q_norm_g"] = dgq.reshape(heads, dims.head_dim).sum(axis=0)
    grads["k_norm_g"] = dgk.reshape(heads, dims.head_dim).sum(axis=0)
    dz = jnp.concatenate([dz_glu, dz_qkv, dz_gate], axis=1)
    sent = large("w_in", _mm_tn(h, dz, n_shards=N_CHIPS, name="dw_in"))
    dh = _mm_nt(dz, w_in, out_dtype=F32, after=sent, name="d_h")
    dx, grads["norm1_g"] = _rmsnorm_bwd(x, row("norm1_g"), dh, dx1, want_bf16=False, name="norm1_bwd")
    return loss, dx, grads


def _step(dims, x, target, w, m, v):
    d = dims.d_model
    t = dims.tokens
    sq = lambda a: a.reshape(a.shape[1:])
    w2, m2, v2 = ({k: sq(a) for k, a in grp.items()} for grp in (w, m, v))

    conv_pad = jnp.pad(w2["conv_w"], ((0, CONV_HALO - dims.conv_width), (0, 0)))
    ffn_pad = jnp.pad(w2["ffn_conv_w"], ((0, FFN_HALO - dims.ffn_conv_width), (0, 0)))
    gathered_names = LARGE + ("conv_w", "ffn_conv_w")
    lands = dict(zip(gathered_names, _cast_to_lands([w2[k] for k in LARGE] + [conv_pad, ffn_pad],
                                                   [BF16] * len(LARGE) + [F32, F32], name="cast_weights")))
    first_names = ("w_in", "conv_w", "ffn_conv_w")
    other_names = tuple(k for k in gathered_names if k not in first_names)
    first = _gather_start([lands[k] for k in first_names], x, name="gather_start_first")
    other = []
    cols = lambda g, rows: jnp.moveaxis(g, 0, 1).reshape(g.shape[1], -1)[:rows]

    def first_weights(after):
        got = dict(zip(first_names, _gather_wait(*first[:3], after, name="gather_wait_first")))
        other.extend(_gather_start([lands[k] for k in other_names], got["w_in"], name="gather_start_other"))
        got["conv_w"] = cols(got["conv_w"], dims.conv_width)
        got["ffn_conv_w"] = cols(got["ffn_conv_w"], dims.ffn_conv_width)
        got["token"] = other[3]
        return got

    def other_weights(after):
        return dict(zip(other_names, _gather_wait(*other[:3], after, name="gather_wait_other")))

    started = {}

    def send_grad(name, g):
        send, recv, g_thru, land, token = _scatter_start(g.reshape(N_CHIPS, -1, g.shape[-1]), name=f"scatter_start_{name}")
        started[name] = (send, recv, g_thru, land)
        return token

    small = {k: w2[k] for k in SMALL}
    small["norm1_g"] = _after(small["norm1_g"].reshape(1, -1), first[3])
    loss, dx, grads = _local_step(dims, x.reshape(t, d), target.reshape(t, d), small, first_weights, other_weights, send_grad)

    def finish(names, after, tag):
        arrived = _scatter_wait([started[k] for k in names], after, name=f"scatter_wait_{tag}")
        blocks = [grads[k].reshape(N_CHIPS, -1, grads[k].shape[-1]) for k in names]
        mine = [_sum_received(g, land, name=f"sum_{k}") for k, g, (_, land) in zip(names, blocks, arrived)]
        theirs = _swap_sibling(mine, name=f"swap_sibling_{tag}")
        return {k: _adamw(w2[k], [a, b], m2[k], v2[k], name=f"adamw_{k}") for k, a, b in zip(names, mine, theirs)}

    out = finish([k for k in LARGE if k != "w_in"], dx, "others")

    small_names = SMALL + ("conv_w", "ffn_conv_w")
    packed = _pack_rows([grads[k] for k in small_names] + [loss[0, 0]], d)
    reduced = _allreduce_small(packed, after=[upd[1] for upd in out.values()], name="allreduce_small")
    shapes = [grads[k].shape for k in small_names] + [()]
    *small_g, loss_total = _unpack_rows(reduced, shapes, d)
    small_g = dict(zip(small_names, small_g))
    chip = 2 * lax.axis_index("x") + lax.axis_index("y")
    for k in ("conv_w", "ffn_conv_w"):
        width = w2[k].shape[1]
        small_g[k] = lax.dynamic_slice_in_dim(small_g[k], chip * width, width, axis=1)

    small_shapes = [w2[k].shape for k in small_names]
    pack = lambda grp: _pack_rows([grp[k] for k in small_names], d)
    results = _adamw(pack(w2), [pack(small_g)], pack(m2), pack(v2), name="adamw_small")
    unpacked = [_unpack_rows(r, small_shapes, d) for r in results]
    for i, k in enumerate(small_names):
        out[k] = tuple(u[i] for u in unpacked)
    out.update(finish(["w_in"], results[1], "w_in"))

    lead =lambda a: a.reshape((1,) + a.shape)
    ordered = [[lead(out[k][j].reshape(w2[k].shape)) for k in WEIGHTS] for j in range(4)]
    return (loss_total, dx.reshape(x.shape), *ordered[0], *ordered[1], *ordered[2], *ordered[3])


def kernel(x, norm1_g, w_in, gate_b, conv_w, conv_b, conv_norm_g, w_conv_out, q_norm_g, k_norm_g, w_attn_out, w_out, norm2_g, w_up, ffn_conv_w, ffn_conv_b, w_down, loss_target, m_norm1_g, m_w_in, m_gate_b, m_conv_w, m_conv_b, m_conv_norm_g, m_w_conv_out, m_q_norm_g, m_k_norm_g, m_w_attn_out, m_w_out, m_norm2_g, m_w_up, m_ffn_conv_w, m_ffn_conv_b, m_w_down, v_norm1_g, v_w_in, v_gate_b, v_conv_w, v_conv_b, v_conv_norm_g, v_w_conv_out, v_q_norm_g, v_k_norm_g, v_w_attn_out, v_w_out, v_norm2_g, v_w_up, v_ffn_conv_w, v_ffn_conv_b, v_w_down):
    w = dict(zip(WEIGHTS, (norm1_g, w_in, gate_b, conv_w, conv_b, conv_norm_g, w_conv_out, q_norm_g, k_norm_g,
                           w_attn_out, w_out, norm2_g, w_up, ffn_conv_w, ffn_conv_b, w_down)))
    m = dict(zip(WEIGHTS, (m_norm1_g, m_w_in, m_gate_b, m_conv_w, m_conv_b, m_conv_norm_g, m_w_conv_out, m_q_norm_g,
                           m_k_norm_g, m_w_attn_out, m_w_out, m_norm2_g, m_w_up, m_ffn_conv_w, m_ffn_conv_b, m_w_down)))
    v = dict(zip(WEIGHTS, (v_norm1_g, v_w_in, v_gate_b, v_conv_w, v_conv_b, v_conv_norm_g, v_w_conv_out, v_q_norm_g,
                           v_k_norm_g, v_w_attn_out, v_w_out, v_norm2_g, v_w_up, v_ffn_conv_w, v_ffn_conv_b, v_w_down)))
    dims = Dims(d_model=x.shape[-1], batch_local=x.shape[0], seq=x.shape[1], d_ff=w_down.shape[1] * N_CHIPS)
    return _step(dims, x, loss_target, w, m, v)
```

```python
import functools
import math
from typing import NamedTuple

import jax
import jax.numpy as jnp
from jax import lax
from jax.experimental import pallas as pl
from jax.experimental.pallas import tpu as pltpu

F32 = jnp.float32
BF16 = jnp.bfloat16

RMS_EPS = 1e-6
MASKED_SCORE = -1e30
ATTN_BLOCK = 128
DILATIONS = (1, 4, 16)
CONV_HALO = 32
FFN_HALO = 8
ADAM_LR, ADAM_B1, ADAM_B2, ADAM_EPS, ADAM_WD, ADAM_STEP = 0.001, 0.9, 0.999, 1e-08, 0.01, 10
V7X_VMEM_LIMIT_BYTES = 56 * 2 ** 20
N_CHIPS = 4
MESH = pl.DeviceIdType.MESH


class Dims(NamedTuple):
    d_model: int = 1024
    n_heads: int = 16
    head_dim: int = 64
    d_ff: int = 2816
    seq: int = 2048
    batch_local: int = 2
    conv_width: int = 31
    ffn_conv_width: int = 3

    @property
    def tokens(self):
        return self.seq * self.batch_local


def _params(*semantics):
    return pltpu.CompilerParams(dimension_semantics=semantics, vmem_limit_bytes=V7X_VMEM_LIMIT_BYTES)


ANY = pl.BlockSpec(memory_space=pl.ANY)


def _ordered(body, n_inputs, after):
    after = [] if after is None else list(after) if isinstance(after, (list, tuple)) else [after]
    if not after:
        return body, [], []

    def wrapped(*refs):
        return body(*refs[:n_inputs], *refs[n_inputs + len(after):])

    return wrapped, [ANY] * len(after), after


def _pick(n, target, mult=128):
    if n <= target:
        return n
    best = None
    for t in range(mult, target + 1, mult):
        if n % t == 0:
            best = t
    assert best is not None, (n, target, mult)
    return best


def _sigmoid(v):
    return 1.0 / (1.0 + jnp.exp(-v))


def _mm_nn(a, w, *, out_dtype, name, residual=None, after=None, tm=1024, tn=1408, tk=2816):
    m, k = a.shape
    nsh, k2, c = w.shape
    assert k == k2 and a.dtype == BF16 and w.dtype == BF16
    n = nsh * c
    tm, tn, tk = _pick(m, tm, 8), _pick(c, tn), _pick(k, tk)
    nk, cpn = k // tk, c // tn

    def body(*refs):
        if residual is None:
            a_ref, w_ref, o_ref, acc = refs
        else:
            a_ref, w_ref, r_ref, o_ref, acc = refs
        prod = jnp.dot(a_ref[...], w_ref[...], preferred_element_type=F32)

        def finish(total):
            if residual is not None:
                total = total + r_ref[...]
            o_ref[...] = total.astype(out_dtype)

        if nk == 1:
            finish(prod)
        else:
            kk = pl.program_id(2)

            @pl.when(kk == 0)
            def _():
                acc[...] = prod

            @pl.when(kk > 0)
            def _():
                acc[...] += prod

            @pl.when(kk == nk - 1)
            def _():
                finish(acc[...])

    in_specs = [pl.BlockSpec((tm, tk), lambda i, j, kk: (i, kk)),
                pl.BlockSpec((None, tk, tn), lambda i, j, kk: (j // cpn, kk, j % cpn))]
    args = [a, w]
    if residual is not None:
        in_specs.append(pl.BlockSpec((tm, tn), lambda i, j, kk: (i, j)))
        args.append(residual)
    body, more_specs, more_args = _ordered(body, len(args), after)
    return pl.pallas_call(
        body, name=name, grid=(m // tm, n // tn, nk),
        in_specs=in_specs + more_specs, out_specs=pl.BlockSpec((tm, tn), lambda i, j, kk: (i, j)),
        out_shape=jax.ShapeDtypeStruct((m, n), out_dtype),
        scratch_shapes=[pltpu.VMEM((tm, tn) if nk > 1 else (8, 128), F32)],
        compiler_params=_params("parallel", "parallel", "arbitrary"),
    )(*args, *more_args)


def _mm_nt(a, w, *, out_dtype, name, after=None, tm=1024, tn=1408, tk=1792):
    m, k = a.shape
    nsh, r, c = w.shape
    assert k == nsh * c and a.dtype == BF16 and w.dtype == BF16
    tm, tn, tk = _pick(m, tm, 8), _pick(r, tn), _pick(c, tk)
    nk, cpk = k // tk, c // tk

    def body(a_ref, w_ref, o_ref, acc):
        prod = lax.dot_general(a_ref[...], w_ref[...], (((1,), (1,)), ((), ())), preferred_element_type=F32)
        if nk == 1:
            o_ref[...] = prod.astype(out_dtype)
        else:
            kk = pl.program_id(2)

            @pl.when(kk == 0)
            def _():
                acc[...] = prod

            @pl.when(kk > 0)
            def _():
                acc[...] += prod

            @pl.when(kk == nk - 1)
            def _():
                o_ref[...] = acc[...].astype(out_dtype)

    body, more_specs, more_args = _ordered(body, 2, after)
    return pl.pallas_call(
        body, name=name, grid=(m // tm, r // tn, nk),
        in_specs=[pl.BlockSpec((tm, tk), lambda i, j, kk: (i, kk)),
                  pl.BlockSpec((None, tn, tk), lambda i, j, kk: (kk // cpk, j, kk % cpk))] + more_specs,
        out_specs=pl.BlockSpec((tm, tn), lambda i, j, kk: (i, j)),
        out_shape=jax.ShapeDtypeStruct((m, r), out_dtype),
        scratch_shapes=[pltpu.VMEM((tm, tn) if nk > 1 else (8, 128), F32)],
        compiler_params=_params("parallel", "parallel", "arbitrary"),
    )(a, w, *more_args)


def _mm_tn(a, b, *, n_shards, name, tm=1408, tn=1408, tk=512):
    t, m = a.shape
    t2, n = b.shape
    assert t == t2 and a.dtype == BF16 and b.dtype == BF16
    c = n // n_shards
    tm, tn, tk = _pick(m, tm), _pick(c, tn), _pick(t, tk, 8)
    nk, cpn = t // tk, c // tn

    def body(a_ref, b_ref, o_ref, ob_ref, acc):
        kk = pl.program_id(2)
        prod = lax.dot_general(a_ref[...], b_ref[...], (((0,), (0,)), ((), ())), preferred_element_type=F32)

        @pl.when(kk == 0)
        def _():
            acc[...] = prod

        @pl.when(kk > 0)
        def _():
            acc[...] += prod

        @pl.when(kk == nk - 1)
        def _():
            total = acc[...]
            o_ref[...] = total
            ob_ref[...] = total.astype(BF16)

    out_spec = pl.BlockSpec((None, tm, tn), lambda i, j, kk: (j // cpn, i, j % cpn))
    return pl.pallas_call(
        body, name=name, grid=(m // tm, n // tn, nk),
        in_specs=[pl.BlockSpec((tk, tm), lambda i, j, kk: (kk, i)),
                  pl.BlockSpec((tk, tn), lambda i, j, kk: (kk, j))],
        out_specs=[out_spec, out_spec],
        out_shape=[jax.ShapeDtypeStruct((n_shards, m, c), F32), jax.ShapeDtypeStruct((n_shards, m, c), BF16)],
        scratch_shapes=[pltpu.VMEM((tm, tn), F32)],
        compiler_params=_params("parallel", "parallel", "arbitrary"),
    )(a, b)


def _row_spec(tr, width, col=0):
    return pl.BlockSpec((tr, width), lambda i, col=col: (i, col))


def _vec_spec(width, col=0):
    return pl.BlockSpec((1, width), lambda i, col=col: (0, col))


def _accumulate(ref, value, first):
    @pl.when(first)
    def _():
        ref[...] = value

    @pl.when(jnp.logical_not(first))
    def _():
        ref[...] += value


def _rmsnorm_fwd(x, g, *, name, tr=512):
    t, d = x.shape
    tr = _pick(t, tr, 8)

    def body(x_ref, g_ref, o_ref):
        xv = x_ref[...]
        r = lax.rsqrt(jnp.mean(xv * xv, axis=-1, keepdims=True) + RMS_EPS)
        o_ref[...] = (xv * r * g_ref[...]).astype(BF16)

    return pl.pallas_call(
        body, name=name, grid=(t // tr,),
        in_specs=[_row_spec(tr, d), _vec_spec(d)], out_specs=_row_spec(tr, d),
        out_shape=jax.ShapeDtypeStruct((t, d), BF16), compiler_params=_params("parallel"),
    )(x, g)


def _rmsnorm_bwd(x, g, dy, dres, *, name, want_bf16, tr=512):
    t, d = x.shape
    tr = _pick(t, tr, 8)

    def body(x_ref, g_ref, dy_ref, dres_ref, *outs):
        dx_ref, dg_ref = outs[0], outs[-1]
        xv, dyv = x_ref[...], dy_ref[...].astype(F32)
        r = lax.rsqrt(jnp.mean(xv * xv, axis=-1, keepdims=True) + RMS_EPS)
        gy = dyv * g_ref[...]
        dx = dres_ref[...] + r * gy - xv * (r * r * r) * jnp.mean(xv * gy, axis=-1, keepdims=True)
        dx_ref[...] = dx
        if want_bf16:
            outs[1][...] = dx.astype(BF16)
        _accumulate(dg_ref, jnp.sum(dyv * xv * r, axis=0, keepdims=True), pl.program_id(0) == 0)

    out_shape = [jax.ShapeDtypeStruct((t, d), F32)]
    out_specs = [_row_spec(tr, d)]
    if want_bf16:
        out_shape.append(jax.ShapeDtypeStruct((t, d), BF16))
        out_specs.append(_row_spec(tr, d))
    out_shape.append(jax.ShapeDtypeStruct((1, d), F32))
    out_specs.append(_vec_spec(d))
    return pl.pallas_call(
        body, name=name, grid=(t // tr,),
        in_specs=[_row_spec(tr, d), _vec_spec(d), _row_spec(tr, d), _row_spec(tr, d)],
        out_specs=out_specs, out_shape=out_shape, compiler_params=_params("arbitrary"),
    )(x, g, dy, dres)


def _head_mean(v, ones_ref, head_dim):
    hi = v.astype(BF16)
    lo = (v - hi.astype(F32)).astype(BF16)
    e = ones_ref[...]
    total = jnp.dot(hi, e, preferred_element_type=F32) + jnp.dot(lo, e, preferred_element_type=F32)
    return total * (1.0 / head_dim)


def _qkv_fwd(z, gq, gk, head_ones, dims, *, name, tr=256):
    t = z.shape[0]
    a = dims.n_heads * dims.head_dim
    tr = _pick(t, tr, 8)
    q_scale = dims.head_dim ** -0.5

    def body(q_ref, k_ref, v_ref, gq_ref, gk_ref, e_ref, qo_ref, ko_ref, vo_ref):
        qv, kv = q_ref[...], k_ref[...]
        rq = lax.rsqrt(_head_mean(qv * qv, e_ref, dims.head_dim) + RMS_EPS)
        rk = lax.rsqrt(_head_mean(kv * kv, e_ref, dims.head_dim) + RMS_EPS)
        qo_ref[...] = (qv * rq * gq_ref[...] * q_scale).astype(BF16)
        ko_ref[...] = (kv * rk * gk_ref[...]).astype(BF16)
        vo_ref[...] = v_ref[...].astype(BF16)

    return pl.pallas_call(
        body, name=name, grid=(t // tr,),
        in_specs=[_row_spec(tr, a, 2), _row_spec(tr, a, 3), _row_spec(tr, a, 4), _vec_spec(a), _vec_spec(a),
                  pl.BlockSpec((a, a), lambda i: (0, 0))],
        out_specs=[_row_spec(tr, a)] * 3, out_shape=[jax.ShapeDtypeStruct((t, a), BF16)] * 3,
        compiler_params=_params("parallel"),
    )(z, z, z, gq, gk, head_ones)


def _qkv_bwd(z, dqs, dks, dvs, gq, gk, head_ones, dims, *, name, tr=256):
    t = z.shape[0]
    a = dims.n_heads * dims.head_dim
    tr = _pick(t, tr, 8)
    q_scale = dims.head_dim ** -0.5
    ng = len(dqs)

    def body(*refs):
        q_ref, k_ref = refs[:2]
        dq_refs, dk_refs, dv_refs = refs[2:2 + ng], refs[2 + ng:2 + 2 * ng], refs[2 + 2 * ng:2 + 3 * ng]
        gq_ref, gk_ref, e_ref = refs[2 + 3 * ng:5 + 3 * ng]
        dz_ref, dgq_ref, dgk_ref = refs[5 + 3 * ng:]
        first = pl.program_id(0) == 0

        def norm_bwd(x_ref, d_refs, g_ref, scale, col, dg_ref):
            xv = x_ref[...]
            dy = sum(r[...] for r in d_refs) * scale
            r = lax.rsqrt(_head_mean(xv * xv, e_ref, dims.head_dim) + RMS_EPS)
            gy = dy * g_ref[...]
            dx = r * gy - xv * (r * r * r) * _head_mean(xv * gy, e_ref, dims.head_dim)
            dz_ref[:, col * a:(col + 1) * a] = dx.astype(BF16)
            _accumulate(dg_ref, jnp.sum(dy * xv * r, axis=0, keepdims=True), first)

        norm_bwd(q_ref, dq_refs, gq_ref, q_scale, 0, dgq_ref)
        norm_bwd(k_ref, dk_refs, gk_ref, 1.0, 1, dgk_ref)
        dz_ref[:, 2 * a:3 * a] = sum(r[...] for r in dv_refs).astype(BF16)

    in_specs = ([_row_spec(tr, a, 2), _row_spec(tr, a, 3)] + [_row_spec(tr, a)] * (3 * ng)
                + [_vec_spec(a), _vec_spec(a), pl.BlockSpec((a, a), lambda i: (0, 0))])
    return pl.pallas_call(
        body, name=name, grid=(t // tr,), in_specs=in_specs,
        out_specs=[_row_spec(tr, 3 * a), _vec_spec(a), _vec_spec(a)],
        out_shape=[jax.ShapeDtypeStruct((t, 3 * a), BF16)] + [jax.ShapeDtypeStruct((1, a), F32)] * 2,
        compiler_params=_params("arbitrary"),
    )(z, z, *dqs, *dks, *dvs, gq, gk, head_ones)


CONV_ROWS = 16


def _seq_specs(dims, ts, width, halo, col, *, nxt=False):
    nst, per = dims.seq // ts, ts // halo
    last = dims.tokens // halo - 1
    cur = pl.BlockSpec((ts, width), lambda b, i: (b * nst + i, col))
    if nxt:
        edge = pl.BlockSpec((halo, width), lambda b, i: (jnp.minimum((b * nst + i + 1) * per, last), col))
    else:
        edge = pl.BlockSpec((halo, width), lambda b, i: (jnp.maximum((b * nst + i) * per - 1, 0), col))
    return cur, edge


SUBLANES = 8


def _shifted_copies(buf, shifted):
    rows = shifted.shape[1]
    for s in range(1, SUBLANES):
        shifted[s - 1] = buf[pl.ds(s, rows), :]


def _window(buf, shifted, start, size):
    a, s = divmod(start, SUBLANES)
    src = buf if s == 0 else shifted.at[s - 1]
    return src[pl.ds(SUBLANES * a, size), :]


def _conv_branch_fwd(z, w, b, g, dims, *, name, ts=128):
    t, c, kw = z.shape[0], dims.d_model, dims.conv_width
    base = CONV_HALO - (kw - 1)

    def body(av_ref, hv_ref, ag_ref, hg_ref, w_ref, b_ref, g_ref, a1_ref, a3_ref, buf, shifted):
        i = pl.program_id(1)
        buf[CONV_HALO:, :] = av_ref[...] * _sigmoid(ag_ref[...])
        buf[0:CONV_HALO, :] = jnp.where(i > 0, hv_ref[...] * _sigmoid(hg_ref[...]), 0.0)
        _shifted_copies(buf, shifted)
        for r0 in range(0, ts, CONV_ROWS):
            acc = jnp.broadcast_to(b_ref[...], (CONV_ROWS, c))
            for k in range(kw):
                acc = acc + w_ref[k:k + 1, :] * _window(buf, shifted, r0 + base + k, CONV_ROWS)
            a1_ref[r0:r0 + CONV_ROWS, :] = acc
            a2 = acc * lax.rsqrt(jnp.mean(acc * acc, axis=-1, keepdims=True) + RMS_EPS) * g_ref[...]
            a3_ref[r0:r0 + CONV_ROWS, :] = (a2 * _sigmoid(a2)).astype(BF16)

    vec = pl.BlockSpec((1, c), lambda b, i: (0, 0))
    out = pl.BlockSpec((ts, c), lambda b, i: (b * (dims.seq // ts) + i, 0))
    return pl.pallas_call(
        body, name=name, grid=(dims.batch_local, dims.seq // ts),
        in_specs=[*_seq_specs(dims, ts, c, CONV_HALO, 0), *_seq_specs(dims, ts, c, CONV_HALO, 1),
                  pl.BlockSpec((CONV_HALO, c), lambda b, i: (0, 0)), vec, vec],
        out_specs=[out, out],
        out_shape=[jax.ShapeDtypeStruct((t, c), F32), jax.ShapeDtypeStruct((t, c), BF16)],
        scratch_shapes=[pltpu.VMEM((CONV_HALO + ts, c), F32),
                        pltpu.VMEM((SUBLANES - 1, CONV_HALO + ts - SUBLANES, c), F32)],
        compiler_params=_params("parallel", "parallel"),
    )(z, z, z, z, w, b, g)


def _conv_norm_bwd(da3, a1, g, *, name, tr=256):
    t, c = a1.shape
    tr = _pick(t, tr, 8)

    def body(d_ref, a_ref, g_ref, o_ref, dg_ref):
        a1v, gv = a_ref[...], g_ref[...]
        r = lax.rsqrt(jnp.mean(a1v * a1v, axis=-1, keepdims=True) + RMS_EPS)
        a2 = a1v * r * gv
        sg = _sigmoid(a2)
        da2 = d_ref[...].astype(F32) * sg * (1.0 + a2 * (1.0 - sg))
        gy = da2 * gv
        o_ref[...] = r * gy - a1v * (r * r * r) * jnp.mean(a1v * gy, axis=-1, keepdims=True)
        _accumulate(dg_ref, jnp.sum(da2 * a1v * r, axis=0, keepdims=True), pl.program_id(0) == 0)

    return pl.pallas_call(
        body, name=name, grid=(t // tr,),
        in_specs=[_row_spec(tr, c), _row_spec(tr, c), _vec_spec(c)],
        out_specs=[_row_spec(tr, c), _vec_spec(c)],
        out_shape=[jax.ShapeDtypeStruct((t, c), F32), jax.ShapeDtypeStruct((1, c), F32)],
        compiler_params=_params("arbitrary"),
    )(da3, a1, g)


def _conv_branch_bwd(da1, z, w, dims, *, name, ts=128):
    t, c, kw = z.shape[0], dims.d_model, dims.conv_width
    nst = dims.seq // ts
    base = CONV_HALO - (kw - 1)

    def body(d_ref, dn_ref, av_ref, hv_ref, ag_ref, hg_ref, w_ref, dz_ref, dw_ref, db_ref, abuf, dbuf, ashift, dshift):
        i = pl.program_id(1)
        first = jnp.logical_and(pl.program_id(0) == 0, i == 0)
        abuf[CONV_HALO:, :] = av_ref[...] * _sigmoid(ag_ref[...])
        abuf[0:CONV_HALO, :] = jnp.where(i > 0, hv_ref[...] * _sigmoid(hg_ref[...]), 0.0)
        d1 = d_ref[...]
        dbuf[0:ts, :] = d1
        dbuf[ts:, :] = jnp.where(i < nst - 1, dn_ref[...], 0.0)
        _shifted_copies(abuf, ashift)
        _shifted_copies(dbuf, dshift)

        @pl.when(first)
        def _():
            dw_ref[...] = jnp.zeros_like(dw_ref)
            db_ref[...] = jnp.zeros_like(db_ref)

        db_ref[...] += jnp.sum(d1, axis=0, keepdims=True)
        for k in range(kw):
            dw_ref[k:k + 1, :] += jnp.sum(d1 * _window(abuf, ashift, base + k, ts), axis=0, keepdims=True)
        for r0 in range(0, ts, CONV_ROWS):
            acc = jnp.zeros((CONV_ROWS, c), F32)
            for k in range(kw):
                acc = acc + w_ref[k:k + 1, :] * _window(dbuf, dshift, r0 + (kw - 1) - k, CONV_ROWS)
            av = av_ref[r0:r0 + CONV_ROWS, :]
            sg = _sigmoid(ag_ref[r0:r0 + CONV_ROWS, :])
            dz_ref[r0:r0 + CONV_ROWS, 0:c] = (acc * sg).astype(BF16)
            dz_ref[r0:r0 + CONV_ROWS, c:2 * c] = (acc * av * sg * (1.0 - sg)).astype(BF16)

    cur, nxt = _seq_specs(dims, ts, c, CONV_HALO, 0, nxt=True)
    return pl.pallas_call(
        body, name=name, grid=(dims.batch_local, nst),
        in_specs=[cur, nxt, *_seq_specs(dims, ts, c, CONV_HALO, 0), *_seq_specs(dims, ts, c, CONV_HALO, 1),
                  pl.BlockSpec((CONV_HALO, c), lambda b, i: (0, 0))],
        out_specs=[pl.BlockSpec((ts, 2 * c), lambda b, i: (b * nst + i, 0)),
                   pl.BlockSpec((CONV_HALO, c), lambda b, i: (0, 0)), pl.BlockSpec((1, c), lambda b, i: (0, 0))],
        out_shape=[jax.ShapeDtypeStruct((t, 2 * c), BF16), jax.ShapeDtypeStruct((CONV_HALO, c), F32),
                   jax.ShapeDtypeStruct((1, c), F32)],
        scratch_shapes=[pltpu.VMEM((CONV_HALO + ts, c), F32)] * 2
        + [pltpu.VMEM((SUBLANES - 1, CONV_HALO + ts - SUBLANES, c), F32)] * 2,
        compiler_params=_params("arbitrary", "arbitrary"),
    )(da1, da1, z, z, z, z, w)


FFN_ROWS = 16
FFN_COLS = 256


def _ffn_chunks(ts, f):
    cw = _pick(f, FFN_COLS)
    return [(r0, c0, cw) for r0 in range(0, ts, FFN_ROWS) for c0 in range(0, f, cw)]


def _ffn_conv(buf, w_ref, b_ref, r0, cols, kw):
    base = FFN_HALO - (kw - 1)
    u = jnp.broadcast_to(b_ref[:, cols], (FFN_ROWS, cols.stop - cols.start))
    for k in range(kw):
        u = u + w_ref[k:k + 1, cols] * buf[pl.ds(r0 + base + k, FFN_ROWS), cols]
    return u


def _ffn_act_fwd(up, w, b, dims, *, name, ts=128):
    t, f, kw = up.shape[0], dims.d_ff, dims.ffn_conv_width

    def body(up_ref, h_ref, w_ref, b_ref, o_ref, buf):
        buf[FFN_HALO:, :] = up_ref[...]
        buf[0:FFN_HALO, :] = jnp.where(pl.program_id(1) > 0, h_ref[...], 0.0)
        for r0, c0, cw in _ffn_chunks(ts, f):
            uv = _ffn_conv(buf, w_ref, b_ref, r0, slice(c0, c0 + cw), kw)
            ug = _ffn_conv(buf, w_ref, b_ref, r0, slice(f + c0, f + c0 + cw), kw)
            o_ref[r0:r0 + FFN_ROWS, c0:c0 + cw] = (ug * _sigmoid(ug) * uv).astype(BF16)

    full = lambda rows: pl.BlockSpec((rows, 2 * f), lambda b_, i: (0, 0))
    return pl.pallas_call(
        body, name=name, grid=(dims.batch_local, dims.seq // ts),
        in_specs=[*_seq_specs(dims, ts, 2 * f, FFN_HALO, 0), full(FFN_HALO), full(1)],
        out_specs=pl.BlockSpec((ts, f), lambda b_, i: (b_ * (dims.seq // ts) + i, 0)),
        out_shape=jax.ShapeDtypeStruct((t, f), BF16),
        scratch_shapes=[pltpu.VMEM((FFN_HALO + ts, 2 * f), F32)],
        compiler_params=_params("parallel", "parallel"),
    )(up, up, w, b)


def _ffn_act_bwd(dact, up, w, b, dims, *, name, ts=128):
    t, f, kw = up.shape[0], dims.d_ff, dims.ffn_conv_width
    base = FFN_HALO - (kw - 1)

    def body(d_ref, up_ref, h_ref, w_ref, b_ref, du_ref, dw_ref, db_ref, buf):
        i = pl.program_id(1)
        first = jnp.logical_and(pl.program_id(0) == 0, i == 0)
        buf[FFN_HALO:, :] = up_ref[...]
        buf[0:FFN_HALO, :] = jnp.where(i > 0, h_ref[...], 0.0)
        for r0, c0, cw in _ffn_chunks(ts, f):
            vcols, gcols = slice(c0, c0 + cw), slice(f + c0, f + c0 + cw)
            uv = _ffn_conv(buf, w_ref, b_ref, r0, vcols, kw)
            ug = _ffn_conv(buf, w_ref, b_ref, r0, gcols, kw)
            d = d_ref[r0:r0 + FFN_ROWS, vcols].astype(F32)
            sg = _sigmoid(ug)
            du_ref[r0:r0 + FFN_ROWS, vcols] = d * ug * sg
            du_ref[r0:r0 + FFN_ROWS, gcols] = d * uv * sg * (1.0 + ug * (1.0 - sg))

        @pl.when(first)
        def _():
            dw_ref[...] = jnp.zeros_like(dw_ref)
            db_ref[...] = jnp.zeros_like(db_ref)

        du = du_ref[...]
        db_ref[...] += jnp.sum(du, axis=0, keepdims=True)
        for k in range(kw):
            dw_ref[k:k + 1, :] += jnp.sum(du * buf[pl.ds(base + k, ts), :], axis=0, keepdims=True)

    nst = dims.seq // ts
    full = lambda rows: pl.BlockSpec((rows, 2 * f), lambda b_, i: (0, 0))
    return pl.pallas_call(
        body, name=name, grid=(dims.batch_local, nst),
        in_specs=[pl.BlockSpec((ts, f), lambda b_, i: (b_ * nst + i, 0)),
                  *_seq_specs(dims, ts, 2 * f, FFN_HALO, 0), full(FFN_HALO), full(1)],
        out_specs=[pl.BlockSpec((ts, 2 * f), lambda b_, i: (b_ * nst + i, 0)), full(FFN_HALO), full(1)],
        out_shape=[jax.ShapeDtypeStruct((t, 2 * f), F32), jax.ShapeDtypeStruct((FFN_HALO, 2 * f), F32),
                   jax.ShapeDtypeStruct((1, 2 * f), F32)],
        scratch_shapes=[pltpu.VMEM((FFN_HALO + ts, 2 * f), F32)],
        compiler_params=_params("arbitrary", "arbitrary"),
    )(dact, up, up, w, b)


def _ffn_conv_bwd(du, w, dims, *, name, ts=128):
    t, f2 = du.shape
    kw = dims.ffn_conv_width
    nst = dims.seq // ts

    def body(d_ref, dn_ref, w_ref, o_ref, buf):
        buf[0:ts, :] = d_ref[...]
        buf[ts:, :] = jnp.where(pl.program_id(1) < nst - 1, dn_ref[...], 0.0)
        for r0, c0, cw in _ffn_chunks(ts, f2):
            cols = slice(c0, c0 + cw)
            acc = jnp.zeros((FFN_ROWS, cw), F32)
            for k in range(kw):
                acc = acc + w_ref[k:k + 1, cols] * buf[pl.ds(r0 + (kw - 1) - k, FFN_ROWS), cols]
            o_ref[r0:r0 + FFN_ROWS, cols] = acc.astype(BF16)

    return pl.pallas_call(
        body, name=name, grid=(dims.batch_local, nst),
        in_specs=[*_seq_specs(dims, ts, f2, FFN_HALO, 0, nxt=True), pl.BlockSpec((FFN_HALO, f2), lambda b_, i: (0, 0))],
        out_specs=pl.BlockSpec((ts, f2), lambda b_, i: (b_ * nst + i, 0)),
        out_shape=jax.ShapeDtypeStruct((t, f2), BF16),
        scratch_shapes=[pltpu.VMEM((ts + FFN_HALO, f2), F32)],
        compiler_params=_params("parallel", "parallel"),
    )(du, du, w)


def _alibi_slope(h, n_heads):
    return 2.0 ** (-8.0 * (h + 1) / n_heads)


def _dot_nt(a, b):
    return lax.dot_general(a, b, (((1,), (1,)), ((), ())), preferred_element_type=F32)


def _dot_tn(a, b):
    return lax.dot_general(a, b, (((0,), (0,)), ((), ())), preferred_element_type=F32)


def _attn_view(x, dims, dil):
    return x.reshape(dims.batch_local, dims.seq // dil, dil * x.shape[-1])


def _attn_fwd_group(q, k, v, state, dims, dil, *, last, name):
    t, a = q.shape
    assert 2 * dims.head_dim == 128 and dims.n_heads % 2 == 0
    blk, hd = ATTN_BLOCK, dims.head_dim
    nb = dims.seq // dil // blk
    has_prev = nb > 1
    nkeys = 2 * blk if has_prev else blk

    def body(*refs):
        it = iter(refs)
        q_ref, kc_ref, vc_ref = next(it), next(it), next(it)
        kp_ref, vp_ref = (next(it), next(it)) if has_prev else (None, None)
        m_in, l_in, acc_in = (next(it), next(it), next(it)) if state is not None else (None, None, None)
        outs = list(it)
        iq = lax.broadcasted_iota(jnp.int32, (blk, nkeys), 0)
        jk = lax.broadcasted_iota(jnp.int32, (blk, nkeys), 1)
        if has_prev:
            steps = iq + blk - jk
            valid = (steps >= 0) & (steps <= blk) & ((jk >= blk) | (pl.program_id(2) > 0))
        else:
            steps = iq - jk
            valid = steps >= 0
        dist = steps.astype(F32) * float(dil)
        low = lax.broadcasted_iota(jnp.int32, (blk, 2 * hd), 1) < hd
        for hp in range(dims.n_heads // 2):
            sl = slice(2 * hd * hp, 2 * hd * (hp + 1))
            q2 = q_ref[:, sl]
            if has_prev:
                kcat = jnp.concatenate([kp_ref[:, sl], kc_ref[:, sl]], axis=0)
                vcat = jnp.concatenate([vp_ref[:, sl], vc_ref[:, sl]], axis=0)
            else:
                kcat, vcat = kc_ref[:, sl], vc_ref[:, sl]
            halves = []
            for half in range(2):
                col = 2 * hd * hp + hd * half
                qh = jnp.where(low if half == 0 else jnp.logical_not(low), q2, jnp.zeros_like(q2))
                sc = _dot_nt(qh, kcat) - _alibi_slope(2 * hp + half, dims.n_heads) * dist
                sc = jnp.where(valid, sc, MASKED_SCORE)
                row_max = jnp.max(sc, axis=-1, keepdims=True)
                if state is None:
                    m_new = row_max
                    p = jnp.exp(sc - m_new)
                    alpha = None
                    l_new = jnp.sum(p, axis=-1, keepdims=True)
                else:
                    m_old = m_in[:, col:col + 1]
                    m_new = jnp.maximum(m_old, row_max)
                    p = jnp.exp(sc - m_new)
                    alpha = jnp.exp(m_old - m_new)
                    l_new = alpha * l_in[:, col:col + 1] + jnp.sum(p, axis=-1, keepdims=True)
                pv = jnp.dot(p.astype(BF16), vcat, preferred_element_type=F32)
                halves.append((m_new, l_new, alpha, pv))
            (m_a, l_a, al_a, pv_a), (m_b, l_b, al_b, pv_b) = halves
            if state is None:
                acc = jnp.where(low, pv_a, pv_b)
            else:
                old = acc_in[:, sl]
                acc = jnp.where(low, al_a * old + pv_a, al_b * old + pv_b)
            m2 = jnp.where(low, m_a, m_b)
            l2 = jnp.where(low, l_a, l_b)
            if last:
                outs[0][:, sl] = (acc / l2).astype(BF16)
                outs[1][:, sl] = m2 + jnp.log(l2)
            else:
                outs[0][:, sl] = m2
                outs[1][:, sl] = l2
                outs[2][:, sl] = acc

    cur = pl.BlockSpec((None, blk, a), lambda b, r, i: (b, i, r))
    prev = pl.BlockSpec((None, blk, a), lambda b, r, i: (b, jnp.maximum(i - 1, 0), r))
    args, in_specs = [q, k, v], [cur, cur, cur]
    if has_prev:
        args += [k, v]
        in_specs += [prev, prev]
    if state is not None:
        args += list(state)
        in_specs += [cur] * 3
    shape = lambda dt: jax.ShapeDtypeStruct((dims.batch_local, dims.seq // dil, dil * a), dt)
    out_shape = [shape(BF16), shape(F32)] if last else [shape(F32)] * 3
    outs = pl.pallas_call(
        body, name=name, grid=(dims.batch_local, dil, nb),
        in_specs=in_specs, out_specs=[cur] * len(out_shape), out_shape=out_shape,
        compiler_params=_params("parallel", "parallel", "parallel"),
    )(*[_attn_view(x, dims, dil) for x in args])
    return tuple(o.reshape(t, a) for o in outs)


def _attn_delta(do, o, head_ones, dims, *, name, tr=512):
    t, a = o.shape
    tr = _pick(t, tr, 8)

    def body(do_ref, o_ref, e_ref, d_ref):
        prod = do_ref[...].astype(F32) * o_ref[...].astype(F32)
        d_ref[...] = _head_mean(prod, e_ref, dims.head_dim) * float(dims.head_dim)

    return pl.pallas_call(
        body, name=name, grid=(t // tr,),
        in_specs=[_row_spec(tr, a), _row_spec(tr, a), pl.BlockSpec((a, a), lambda i: (0, 0))],
        out_specs=_row_spec(tr, a), out_shape=jax.ShapeDtypeStruct((t, a), F32),
        compiler_params=_params("parallel"),
    )(do, o, head_ones)


def _attn_bwd_group(q, k, v, do, lse, delta, dims, dil, *, name):
    t, a = q.shape
    blk, hd = ATTN_BLOCK, dims.head_dim
    nb = dims.seq // dil // blk
    has_next = nb > 1

    def body(*refs):
        k_ref, v_ref, q_ref, do_ref, lse_ref, dl_ref = refs[:6]
        if has_next:
            qn_ref, don_ref, lsen_ref, dln_ref = refs[6:10]
            dq_ref, dk_ref, dv_ref, carry = refs[10:]
        else:
            dq_ref, dk_ref, dv_ref = refs[6:]
        j = pl.program_id(2)
        iq = lax.broadcasted_iota(jnp.int32, (blk, blk), 0)
        jk = lax.broadcasted_iota(jnp.int32, (blk, blk), 1)
        low = lax.broadcasted_iota(jnp.int32, (blk, 2 * hd), 1) < hd

        def pair(hp, qr, dor, lser, dlr, steps, valid):
            sl = slice(2 * hd * hp, 2 * hd * (hp + 1))
            q2, do2, k2, v2 = qr[:, sl], dor[:, sl], k_ref[:, sl], v_ref[:, sl]
            dist = steps.astype(F32) * float(dil)
            dq_h, dk2, dv2 = [], None, None
            for half in range(2):
                col = 2 * hd * hp + hd * half
                mask = low if half == 0 else jnp.logical_not(low)
                qh = jnp.where(mask, q2, jnp.zeros_like(q2))
                doh = jnp.where(mask, do2, jnp.zeros_like(do2))
                sc = _dot_nt(qh, k2) - _alibi_slope(2 * hp + half, dims.n_heads) * dist
                p = jnp.where(valid, jnp.exp(sc - lser[:, col:col + 1]), 0.0)
                ds = p * (_dot_nt(doh, v2) - dlr[:, col:col + 1])
                ds_b, p_b = ds.astype(BF16), p.astype(BF16)
                dq_h.append(jnp.dot(ds_b, k2, preferred_element_type=F32))
                dk_h, dv_h = _dot_tn(ds_b, qh), _dot_tn(p_b, doh)
                dk2 = dk_h if dk2 is None else dk2 + dk_h
                dv2 = dv_h if dv2 is None else dv2 + dv_h
            return sl, jnp.where(low, dq_h[0], dq_h[1]), dk2, dv2

        if has_next:
            @pl.when(j == 0)
            def _():
                carry[...] = jnp.zeros_like(carry)

        for hp in range(dims.n_heads // 2):
            sl, dq2, dk2, dv2 = pair(hp, q_ref, do_ref, lse_ref, dl_ref, iq - jk, iq >= jk)
            dq_ref[:, sl] = (carry[:, sl] + dq2) if has_next else dq2
            dk_ref[:, sl] = dk2
            dv_ref[:, sl] = dv2

        if has_next:
            @pl.when(j + 1 < nb)
            def _():
                for hp in range(dims.n_heads // 2):
                    sl, dq2, dk2, dv2 = pair(hp, qn_ref, don_ref, lsen_ref, dln_ref, iq - jk + blk, jk >= iq)
                    carry[:, sl] = dq2
                    dk_ref[:, sl] += dk2
                    dv_ref[:, sl] += dv2

    cur = pl.BlockSpec((None, blk, a), lambda b, r, j: (b, j, r))
    nxt = pl.BlockSpec((None, blk, a), lambda b, r, j: (b, jnp.minimum(j + 1, nb - 1), r))
    args, in_specs = [k, v, q, do, lse, delta], [cur] * 6
    if has_next:
        args += [q, do, lse, delta]
        in_specs += [nxt] * 4
    shape = jax.ShapeDtypeStruct((dims.batch_local, dims.seq // dil, dil * a), F32)
    outs = pl.pallas_call(
        body, name=name, grid=(dims.batch_local, dil, nb),
        in_specs=in_specs, out_specs=[cur] * 3, out_shape=[shape] * 3,
        scratch_shapes=[pltpu.VMEM((blk, a), F32)] if has_next else [],
        compiler_params=_params("parallel", "parallel", "arbitrary"),
    )(*[_attn_view(x, dims, dil) for x in args])
    return tuple(o.reshape(t, a) for o in outs)


LANES = 128
MASK_BIAS = 1e30
RESIDUE_DILATIONS = tuple(d for d in DILATIONS if d > 1)


def _rows_to_residues(value, out_ref, scr, d):
    rows, width = value.shape
    for c in range(width // LANES):
        cols = slice(LANES * c, LANES * (c + 1))
        scr[c] = value[:, cols]
        for r in range(d):
            out_ref[r, :, cols] = scr[c, pl.ds(r, rows // d, stride=d), :].astype(out_ref.dtype)


def _residues_to_rows(in_ref, scr, d):
    _, n, width = in_ref.shape
    slabs = []
    for c in range(width // LANES):
        cols = slice(LANES * c, LANES * (c + 1))
        for r in range(d):
            scr[c, pl.ds(r, n, stride=d), :] = in_ref[r, :, cols].astype(F32)
        slabs.append(scr[c])
    return slabs[0] if len(slabs) == 1 else jnp.concatenate(slabs, axis=1)


def _residue_shape(dims, d, width, dtype):
    return jax.ShapeDtypeStruct((dims.batch_local, d, dims.seq // d, width), dtype)


def _residue_spec(dims, d, tr, width):
    tiles = dims.seq // tr
    return pl.BlockSpec((None, d, tr // d, width), lambda i: (i // tiles, 0, i % tiles, 0))


def _head_sum_matrix(dims):
    a = dims.n_heads * dims.head_dim
    head = jnp.arange(a, dtype=jnp.int32) // dims.head_dim
    return (head[:, None] == jnp.arange(LANES, dtype=jnp.int32)[None, :]).astype(BF16)


def _two_pass_dot(v, m):
    hi = v.astype(BF16)
    lo = (v - hi.astype(F32)).astype(BF16)
    return jnp.dot(hi, m, preferred_element_type=F32) + jnp.dot(lo, m, preferred_element_type=F32)


def _qkv_layouts_fwd(z, gq, gk, head_ones, dims, *, name, tr=256):
    t = z.shape[0]
    a = dims.n_heads * dims.head_dim
    q_scale = dims.head_dim ** -0.5
    nres = len(RESIDUE_DILATIONS)

    def body(q_ref, k_ref, v_ref, gq_ref, gk_ref, e_ref, *rest):
        outs, scr = rest[:-1], rest[-1]
        qv, kv = q_ref[...], k_ref[...]
        rq = lax.rsqrt(_head_mean(qv * qv, e_ref, dims.head_dim) + RMS_EPS)
        rk = lax.rsqrt(_head_mean(kv * kv, e_ref, dims.head_dim) + RMS_EPS)
        values = (qv * rq * gq_ref[...] * q_scale, kv * rk * gk_ref[...], v_ref[...])
        for j, val in enumerate(values):
            outs[j][...] = val.astype(BF16)
            for g, d in enumerate(RESIDUE_DILATIONS):
                _rows_to_residues(val, outs[3 * (g + 1) + j], scr, d)

    out_specs = [_row_spec(tr, a)] * 3
    out_shape = [jax.ShapeDtypeStruct((t, a), BF16)] * 3
    for d in RESIDUE_DILATIONS:
        out_specs += [_residue_spec(dims, d, tr, a)] * 3
        out_shape += [_residue_shape(dims, d, a, BF16)] * 3
    outs = pl.pallas_call(
        body, name=name, grid=(t // tr,),
        in_specs=[_row_spec(tr, a, 2), _row_spec(tr, a, 3), _row_spec(tr, a, 4), _vec_spec(a), _vec_spec(a),
                  pl.BlockSpec((a, a), lambda i: (0, 0))],
        out_specs=out_specs, out_shape=out_shape,
        scratch_shapes=[pltpu.VMEM((a // LANES, tr, LANES), F32)],
        compiler_params=_params("parallel"),
    )(z, z, z, gq, gk, head_ones)
    return {d: tuple(outs[3 * g:3 * g + 3]) for g, d in enumerate((1,) + RESIDUE_DILATIONS)}


def _attn_specs(dims, dil, width):
    blk = ATTN_BLOCK
    nb = dims.seq // dil // blk
    if dil == 1:
        grid = (dims.batch_local, nb)
        at = lambda f: pl.BlockSpec((blk, width), lambda b, i: (b * nb + f(i), 0))
    else:
        grid = (dims.batch_local, dil, nb)
        at = lambda f: pl.BlockSpec((None, None, blk, width), lambda b, r, i: (b, r, f(i), 0))
    return grid, at(lambda i: i), at(lambda i: jnp.maximum(i - 1, 0)), at(lambda i: jnp.minimum(i + 1, nb - 1))


def _head_slopes(n_heads):
    h = lax.broadcasted_iota(jnp.int32, (n_heads, 1, 1), 0).astype(F32)
    return jnp.exp((h + 1.0) * (-8.0 / n_heads * math.log(2.0)))


def _pair_masks(hd):
    low = lax.broadcasted_iota(jnp.int32, (1, 2 * hd), 1) < hd
    return low, jnp.logical_not(low)


def _attn_fwd(q, k, v, dims, dil, *, name):
    a = dims.n_heads * dims.head_dim
    heads, hd, blk = dims.n_heads, dims.head_dim, ATTN_BLOCK
    assert 2 * hd == LANES and heads % 2 == 0 and heads <= LANES
    nb = dims.seq // dil // blk
    has_prev = nb > 1
    nkeys = 2 * blk if has_prev else blk
    grid, cur, prev, _ = _attn_specs(dims, dil, a)
    _, cur_stat, _, _ = _attn_specs(dims, dil, LANES)

    def body(*refs):
        if has_prev:
            q_ref, kc_ref, vc_ref, kp_ref, vp_ref, o_ref, lse_ref, s_scr, p_scr = refs
        else:
            q_ref, kc_ref, vc_ref, o_ref, lse_ref, s_scr, p_scr = refs
        low, high = _pair_masks(hd)

        def keys(cur_ref, prev_ref, sl):
            return jnp.concatenate([prev_ref[:, sl], cur_ref[:, sl]], axis=0) if has_prev else cur_ref[:, sl]

        for hp in range(heads // 2):
            sl = slice(LANES * hp, LANES * (hp + 1))
            q2 = q_ref[:, sl]
            kcat = keys(kc_ref, kp_ref if has_prev else None, sl)
            s_scr[2 * hp] = _dot_nt(jnp.where(low, q2, jnp.zeros_like(q2)), kcat)
            s_scr[2 * hp + 1] = _dot_nt(jnp.where(high, q2, jnp.zeros_like(q2)), kcat)

        iq = lax.broadcasted_iota(jnp.int32, (blk, nkeys), 0)
        jk = lax.broadcasted_iota(jnp.int32, (blk, nkeys), 1)
        if has_prev:
            steps = iq + blk - jk
            valid = (steps >= 0) & (steps <= blk) & ((jk >= blk) | (pl.program_id(len(grid) - 1) > 0))
        else:
            steps = iq - jk
            valid = steps >= 0
        bias = jnp.where(valid, steps.astype(F32) * (-float(dil)), -MASK_BIAS)
        s = s_scr[...] + _head_slopes(heads) * bias[None]
        m = jnp.max(s, axis=-1, keepdims=True)
        p = jnp.exp(s - m)
        l = jnp.sum(p, axis=-1, keepdims=True)
        p_scr[...] = p.astype(BF16)
        inv = 1.0 / l
        lse = m + jnp.log(l)

        lane = lax.broadcasted_iota(jnp.int32, (blk, LANES), 1)
        stat = jnp.zeros((blk, LANES), F32)
        for hp in range(heads // 2):
            sl = slice(LANES * hp, LANES * (hp + 1))
            vcat = keys(vc_ref, vp_ref if has_prev else None, sl)
            pv_a = jnp.dot(p_scr[2 * hp], vcat, preferred_element_type=F32) * inv[2 * hp]
            pv_b = jnp.dot(p_scr[2 * hp + 1], vcat, preferred_element_type=F32) * inv[2 * hp + 1]
            o_ref[:, sl] = jnp.where(low, pv_a, pv_b)
            stat = jnp.where(lane == 2 * hp, lse[2 * hp], stat)
            stat = jnp.where(lane == 2 * hp + 1, lse[2 * hp + 1], stat)
        lse_ref[...] = stat

    lead = q.shape[:-2]
    rows = q.shape[-2]
    o, lse = pl.pallas_call(
        body, name=name, grid=grid,
        in_specs=[cur, cur, cur] + ([prev, prev] if has_prev else []),
        out_specs=[cur, cur_stat],
        out_shape=[jax.ShapeDtypeStruct(lead + (rows, a), F32), jax.ShapeDtypeStruct(lead + (rows, LANES), F32)],
        scratch_shapes=[pltpu.VMEM((heads, blk, nkeys), F32), pltpu.VMEM((heads, blk, nkeys), BF16)],
        compiler_params=_params(*["parallel"] * len(grid)),
    )(q, k, v, *([k, v] if has_prev else []))
    return o, lse


def _attn_combine(groups, head_spread, dims, *, name, tr=256):
    t = dims.tokens
    a = dims.n_heads * dims.head_dim
    dils = tuple(groups)

    def body(*refs):
        ins = refs[:2 * len(dils)]
        x_ref = refs[2 * len(dils)]
        o_ref = refs[2 * len(dils) + 1]
        lse_refs = refs[2 * len(dils) + 2:-2]
        scr, scr_stat = refs[-2], refs[-1]
        outs, stats = [], []
        for g, d in enumerate(dils):
            if d == 1:
                outs.append(ins[2 * g][...])
                stats.append(ins[2 * g + 1][...])
            else:
                outs.append(_residues_to_rows(ins[2 * g], scr, d))
                stats.append(_residues_to_rows(ins[2 * g + 1], scr_stat, d))
        top = functools.reduce(jnp.maximum, stats)
        weights = [jnp.exp(s - top) for s in stats]
        total = functools.reduce(jnp.add, weights)
        joint = top + jnp.log(total)
        inv = 1.0 / total
        acc = None
        for w, o in zip(weights, outs):
            term = _two_pass_dot(w * inv, x_ref[...]) * o
            acc = term if acc is None else acc + term
        o_ref[...] = acc.astype(BF16)
        for g, d in enumerate(dils):
            if d == 1:
                lse_refs[g][...] = joint
            else:
                _rows_to_residues(joint, lse_refs[g], scr_stat, d)

    in_specs, args, lse_specs, lse_shapes = [], [], [], []
    for d in dils:
        if d == 1:
            in_specs += [_row_spec(tr, a), _row_spec(tr, LANES)]
            lse_specs.append(_row_spec(tr, LANES))
            lse_shapes.append(jax.ShapeDtypeStruct((t, LANES), F32))
        else:
            in_specs += [_residue_spec(dims, d, tr, a), _residue_spec(dims, d, tr, LANES)]
            lse_specs.append(_residue_spec(dims, d, tr, LANES))
            lse_shapes.append(_residue_shape(dims, d, LANES, F32))
        args += list(groups[d])
    outs = pl.pallas_call(
        body, name=name, grid=(t // tr,),
        in_specs=in_specs + [pl.BlockSpec((LANES, a), lambda i: (0, 0))],
        out_specs=[_row_spec(tr, a)] + lse_specs,
        out_shape=[jax.ShapeDtypeStruct((t, a), BF16)] + lse_shapes,
        scratch_shapes=[pltpu.VMEM((a // LANES, tr, LANES), F32), pltpu.VMEM((1, tr, LANES), F32)],
        compiler_params=_params("parallel"),
    )(*args, head_spread)
    return outs[0], dict(zip(dils, outs[1:]))


def _attn_bwd_prep(do, o, head_sum, dims, *, name, tr=256):
    t, a = o.shape

    def body(do_ref, o_ref, e_ref, *rest):
        outs, scr, scr_stat = rest[:-2], rest[-2], rest[-1]
        dov = do_ref[...].astype(F32)
        delta = _two_pass_dot(dov * o_ref[...].astype(F32), e_ref[...])
        outs[0][...] = delta
        for g, d in enumerate(RESIDUE_DILATIONS):
            _rows_to_residues(dov, outs[1 + 2 * g], scr, d)
            _rows_to_residues(delta, outs[2 + 2 * g], scr_stat, d)

    out_specs, out_shape = [_row_spec(tr, LANES)], [jax.ShapeDtypeStruct((t, LANES), F32)]
    for d in RESIDUE_DILATIONS:
        out_specs += [_residue_spec(dims, d, tr, a), _residue_spec(dims, d, tr, LANES)]
        out_shape += [_residue_shape(dims, d, a, BF16), _residue_shape(dims, d, LANES, F32)]
    outs = pl.pallas_call(
        body, name=name, grid=(t // tr,),
        in_specs=[_row_spec(tr, a), _row_spec(tr, a), pl.BlockSpec((a, LANES), lambda i: (0, 0))],
        out_specs=out_specs, out_shape=out_shape,
        scratch_shapes=[pltpu.VMEM((a // LANES, tr, LANES), F32), pltpu.VMEM((1, tr, LANES), F32)],
        compiler_params=_params("parallel"),
    )(do, o, head_sum)
    dos, deltas = {1: do}, {1: outs[0]}
    for g, d in enumerate(RESIDUE_DILATIONS):
        dos[d], deltas[d] = outs[1 + 2 * g], outs[2 + 2 * g]
    return dos, deltas


def _attn_bwd(q, k, v, do, lse, delta, dims, dil, *, name):
    a = dims.n_heads * dims.head_dim
    heads, hd, blk = dims.n_heads, dims.head_dim, ATTN_BLOCK
    nb = dims.seq // dil // blk
    has_next = nb > 1
    nq = 2 * blk if has_next else blk
    grid, cur, _, nxt = _attn_specs(dims, dil, a)
    _, cur_stat, _, nxt_stat = _attn_specs(dims, dil, LANES)

    def body(*refs):
        k_ref, v_ref, q_ref, do_ref, lse_ref, dl_ref = refs[:6]
        if has_next:
            qn_ref, don_ref, lsen_ref, dln_ref = refs[6:10]
            dq_ref, dk_ref, dv_ref, s_scr, dp_scr, p_scr, ds_scr, carry = refs[10:]
        else:
            dq_ref, dk_ref, dv_ref, s_scr, dp_scr, p_scr, ds_scr = refs[6:]
        j = pl.program_id(len(grid) - 1)
        low, high = _pair_masks(hd)

        def stacked(ref, nref, sl):
            return jnp.concatenate([ref[:, sl], nref[:, sl]], axis=0) if has_next else ref[:, sl]

        def halves(x):
            return jnp.where(low, x, jnp.zeros_like(x)), jnp.where(high, x, jnp.zeros_like(x))

        for hp in range(heads // 2):
            sl = slice(LANES * hp, LANES * (hp + 1))
            k2, v2 = k_ref[:, sl], v_ref[:, sl]
            q_a, q_b = halves(stacked(q_ref, qn_ref if has_next else None, sl))
            do_a, do_b = halves(stacked(do_ref, don_ref if has_next else None, sl))
            s_scr[2 * hp], s_scr[2 * hp + 1] = _dot_nt(q_a, k2), _dot_nt(q_b, k2)
            dp_scr[2 * hp], dp_scr[2 * hp + 1] = _dot_nt(do_a, v2), _dot_nt(do_b, v2)

        rq = lax.broadcasted_iota(jnp.int32, (nq, blk), 0)
        jk = lax.broadcasted_iota(jnp.int32, (nq, blk), 1)
        if has_next:
            iq = jnp.where(rq < blk, rq, rq - blk)
            steps = jnp.where(rq < blk, iq - jk, iq - jk + blk)
            valid = ((rq < blk) & (iq >= jk)) | ((rq >= blk) & (jk >= iq) & (j + 1 < nb))
        else:
            steps, valid = rq - jk, rq >= jk
        bias = jnp.where(valid, steps.astype(F32) * (-float(dil)), -MASK_BIAS)
        lse_all = stacked(lse_ref, lsen_ref if has_next else None, slice(None))
        dl_all = stacked(dl_ref, dln_ref if has_next else None, slice(None))
        lse3 = jnp.stack([lse_all[:, h:h + 1] for h in range(heads)])
        dl3 = jnp.stack([dl_all[:, h:h + 1] for h in range(heads)])
        p = jnp.exp(s_scr[...] + _head_slopes(heads) * bias[None] - lse3)
        p_scr[...] = p.astype(BF16)
        ds_scr[...] = (p * (dp_scr[...] - dl3)).astype(BF16)

        if has_next:
            @pl.when(j == 0)
            def _():
                carry[...] = jnp.zeros_like(carry)

        for hp in range(heads // 2):
            sl = slice(LANES * hp, LANES * (hp + 1))
            k2 = k_ref[:, sl]
            q_a, q_b = halves(stacked(q_ref, qn_ref if has_next else None, sl))
            do_a, do_b = halves(stacked(do_ref, don_ref if has_next else None, sl))
            ds_a, ds_b = ds_scr[2 * hp], ds_scr[2 * hp + 1]
            dq2 = jnp.where(low, jnp.dot(ds_a, k2, preferred_element_type=F32),
                            jnp.dot(ds_b, k2, preferred_element_type=F32))
            dk_ref[:, sl] = _dot_tn(ds_a, q_a) + _dot_tn(ds_b, q_b)
            dv_ref[:, sl] = _dot_tn(p_scr[2 * hp], do_a) + _dot_tn(p_scr[2 * hp + 1], do_b)
            if has_next:
                dq_ref[:, sl] = carry[:, sl] + dq2[:blk]
                carry[:, sl] = dq2[blk:]
            else:
                dq_ref[:, sl] = dq2

    args, in_specs = [k, v, q, do, lse, delta], [cur] * 4 + [cur_stat] * 2
    if has_next:
        args += [q, do, lse, delta]
        in_specs += [nxt] * 2 + [nxt_stat] * 2
    shape = jax.ShapeDtypeStruct(q.shape, F32)
    scratch = [pltpu.VMEM((heads, nq, blk), F32)] * 2 + [pltpu.VMEM((heads, nq, blk), BF16)] * 2
    if has_next:
        scratch.append(pltpu.VMEM((blk, a), F32))
    return pl.pallas_call(
        body, name=name, grid=grid, in_specs=in_specs, out_specs=[cur] * 3, out_shape=[shape] * 3,
        scratch_shapes=scratch,
        compiler_params=_params(*["parallel"] * (len(grid) - 1), "arbitrary"),
    )(*args)


def _qkv_layouts_bwd(z, grads, gq, gk, head_ones, dims, *, name, tr=256):
    t = z.shape[0]
    a = dims.n_heads * dims.head_dim
    q_scale = dims.head_dim ** -0.5
    dils = tuple(grads)

    def body(q_ref, k_ref, *rest):
        d_refs = rest[:3 * len(dils)]
        gq_ref, gk_ref, e_ref, dz_ref, dgq_ref, dgk_ref, scr = rest[3 * len(dils):]
        first = pl.program_id(0) == 0

        def total(j):
            acc = None
            for g, d in enumerate(dils):
                ref = d_refs[3 * g + j]
                part = ref[...] if d == 1 else _residues_to_rows(ref, scr, d)
                acc = part if acc is None else acc + part
            return acc

        def norm_bwd(x_ref, dy, g_ref, scale, col, dg_ref):
            xv = x_ref[...]
            dy = dy * scale
            r = lax.rsqrt(_head_mean(xv * xv, e_ref, dims.head_dim) + RMS_EPS)
            gy = dy * g_ref[...]
            dx = r * gy - xv * (r * r * r) * _head_mean(xv * gy, e_ref, dims.head_dim)
            dz_ref[:, col * a:(col + 1) * a] = dx.astype(BF16)
            _accumulate(dg_ref, jnp.sum(dy * xv * r, axis=0, keepdims=True), first)

        norm_bwd(q_ref, total(0), gq_ref, q_scale, 0, dgq_ref)
        norm_bwd(k_ref, total(1), gk_ref, 1.0, 1, dgk_ref)
        dz_ref[:, 2 * a:3 * a] = total(2).astype(BF16)

    in_specs, args = [_row_spec(tr, a, 2), _row_spec(tr, a, 3)], [z, z]
    for d in dils:
        in_specs += [_row_spec(tr, a) if d == 1 else _residue_spec(dims, d, tr, a)] * 3
        args += list(grads[d])
    in_specs += [_vec_spec(a), _vec_spec(a), pl.BlockSpec((a, a), lambda i: (0, 0))]
    return pl.pallas_call(
        body, name=name, grid=(t // tr,), in_specs=in_specs,
        out_specs=[_row_spec(tr, 3 * a), _vec_spec(a), _vec_spec(a)],
        out_shape=[jax.ShapeDtypeStruct((t, 3 * a), BF16)] + [jax.ShapeDtypeStruct((1, a), F32)] * 2,
        scratch_shapes=[pltpu.VMEM((a // LANES, tr, LANES), F32)],
        compiler_params=_params("arbitrary"),
    )(*args, gq, gk, head_ones)


def _mix_fwd(ya, yb, z, gate_b, dims, *, name, tr=512):
    t, d = ya.shape
    tr = _pick(t, tr, 8)
    first_gate_col = z.shape[1] // d - 2

    def body(ya_ref, yb_ref, ga_ref, gb_ref, ba_ref, bb_ref, o_ref):
        g_a = _sigmoid(ga_ref[...] + ba_ref[...])
        g_b = _sigmoid(gb_ref[...] + bb_ref[...])
        o_ref[...] = (g_a * ya_ref[...] + g_b * yb_ref[...]).astype(BF16)

    return pl.pallas_call(
        body, name=name, grid=(t // tr,),
        in_specs=[_row_spec(tr, d), _row_spec(tr, d), _row_spec(tr, d, first_gate_col),
                  _row_spec(tr, d, first_gate_col + 1), _vec_spec(d, 0), _vec_spec(d, 1)],
        out_specs=_row_spec(tr, d), out_shape=jax.ShapeDtypeStruct((t, d), BF16),
        compiler_params=_params("parallel"),
    )(ya, yb, z, z, gate_b, gate_b)


def _mix_bwd(dmix, ya, yb, z, gate_b, dims, *, name, tr=512):
    t, d = ya.shape
    tr = _pick(t, tr, 8)
    first_gate_col = z.shape[1] // d - 2

    def body(dm_ref, ya_ref, yb_ref, ga_ref, gb_ref, ba_ref, bb_ref, dya_ref, dyb_ref, dz_ref, db_ref):
        dm = dm_ref[...].astype(F32)
        g_a = _sigmoid(ga_ref[...] + ba_ref[...])
        g_b = _sigmoid(gb_ref[...] + bb_ref[...])
        dya_ref[...] = (dm * g_a).astype(BF16)
        dyb_ref[...] = (dm * g_b).astype(BF16)
        dl_a = dm * ya_ref[...] * g_a * (1.0 - g_a)
        dl_b = dm * yb_ref[...] * g_b * (1.0 - g_b)
        dz_ref[:, 0:d] = dl_a.astype(BF16)
        dz_ref[:, d:2 * d] = dl_b.astype(BF16)
        first = pl.program_id(0) == 0
        sums = jnp.concatenate([jnp.sum(dl_a, axis=0, keepdims=True), jnp.sum(dl_b, axis=0, keepdims=True)], axis=1)
        _accumulate(db_ref, sums, first)

    return pl.pallas_call(
        body, name=name, grid=(t // tr,),
        in_specs=[_row_spec(tr, d), _row_spec(tr, d), _row_spec(tr, d), _row_spec(tr, d, first_gate_col),
                  _row_spec(tr, d, first_gate_col + 1), _vec_spec(d, 0), _vec_spec(d, 1)],
        out_specs=[_row_spec(tr, d), _row_spec(tr, d), _row_spec(tr, 2 * d), _vec_spec(2 * d)],
        out_shape=[jax.ShapeDtypeStruct((t, d), BF16)] * 2 + [jax.ShapeDtypeStruct((t, 2 * d), BF16),
                                                              jax.ShapeDtypeStruct((1, 2 * d), F32)],
        compiler_params=_params("arbitrary"),
    )(dmix, ya, yb, z, z, gate_b, gate_b)


def _loss_head(y, target, *, name, tr=512):
    t, d = y.shape
    tr = _pick(t, tr, 8)

    def body(y_ref, t_ref, dy_ref, dyb_ref, loss_ref):
        err = y_ref[...] - t_ref[...]
        dy = err * (1.0 / d)
        dy_ref[...] = dy
        dyb_ref[...] = dy.astype(BF16)
        part = jnp.sum(jnp.sum(err * err, axis=-1, keepdims=True), axis=0, keepdims=True) * (0.5 / d)
        _accumulate(loss_ref, jnp.broadcast_to(part, (8, 128)), pl.program_id(0) == 0)

    return pl.pallas_call(
        body, name=name, grid=(t // tr,),
        in_specs=[_row_spec(tr, d), _row_spec(tr, d)],
        out_specs=[_row_spec(tr, d), _row_spec(tr, d), pl.BlockSpec((8, 128), lambda i: (0, 0))],
        out_shape=[jax.ShapeDtypeStruct((t, d), F32), jax.ShapeDtypeStruct((t, d), BF16),
                   jax.ShapeDtypeStruct((8, 128), F32)],
        compiler_params=_params("arbitrary"),
    )(y, target)


def _adamw(w, grads, m, v, *, name, tr=256):
    r, c = w.shape
    tr = _pick(r, tr, 8)
    ng = len(grads)
    c1 = 1.0 - ADAM_B1 ** ADAM_STEP
    c2 = 1.0 - ADAM_B2 ** ADAM_STEP

    def body(*refs):
        w_ref, g_refs, m_ref, v_ref = refs[0], refs[1:1 + ng], refs[1 + ng], refs[2 + ng]
        g_out, d_out, m_out, v_out = refs[3 + ng:]
        g = g_refs[0][...]
        for extra in g_refs[1:]:
            g = g + extra[...]
        m_new = ADAM_B1 * m_ref[...] + (1.0 - ADAM_B1) * g
        v_new = ADAM_B2 * v_ref[...] + (1.0 - ADAM_B2) * (g * g)
        g_out[...] = g
        m_out[...] = m_new
        v_out[...] = v_new
        d_out[...] = -ADAM_LR * ((m_new / c1) / (jnp.sqrt(v_new / c2) + ADAM_EPS) + ADAM_WD * w_ref[...])

    spec = pl.BlockSpec((tr, c), lambda i: (i, 0))
    return pl.pallas_call(
        body, name=name, grid=(r // tr,),
        in_specs=[spec] * (3 + ng), out_specs=[spec] * 4, out_shape=[jax.ShapeDtypeStruct((r, c), F32)] * 4,
        compiler_params=_params("parallel"),
    )(w, *grads, m, v)


CHIP_PEERS = ((1, 0), (0, 1), (1, 1))


def _place():
    return lax.axis_index("x"), lax.axis_index("y"), lax.axis_index("c")


HBM = pl.BlockSpec(memory_space=pltpu.HBM)
SEM = pl.BlockSpec(memory_space=pltpu.SEMAPHORE)
IN_FLIGHT = pltpu.SideEffectType.DATAFLOW_SIDE_EFFECTING


def _in_hbm(a):
    return pltpu.with_memory_space_constraint(a, pltpu.HBM)


def _cast_to_lands(shards, dtypes, *, name):
    n = len(shards)

    def body(*refs):
        ins, outs, bufs, sems = refs[:n], refs[n:2 * n], refs[2 * n:3 * n], refs[3 * n]
        x, y, _ = _place()
        copies = []
        for a in range(n):
            bufs[a][...] = ins[a][...].astype(dtypes[a])
            cp = pltpu.make_async_copy(bufs[a], outs[a].at[2 * x + y], sems.at[a])
            cp.start()
            copies.append(cp)
        for cp in copies:
            cp.wait()

    return pl.pallas_call(
        body, name=name, in_specs=[pl.BlockSpec(memory_space=pltpu.VMEM)] * n, out_specs=[ANY] * n,
        out_shape=[jax.ShapeDtypeStruct((N_CHIPS,) + s.shape, dt) for s, dt in zip(shards, dtypes)],
        scratch_shapes=[pltpu.VMEM(s.shape, dt) for s, dt in zip(shards, dtypes)] + [pltpu.SemaphoreType.DMA((n,))],
        compiler_params=pltpu.CompilerParams(vmem_limit_bytes=V7X_VMEM_LIMIT_BYTES),
    )(*shards)


def _chip_copy(src, dst, send, recv, flip, place):
    x, y, c = place
    return pltpu.make_async_remote_copy(src_ref=src, dst_ref=dst, send_sem=send, recv_sem=recv,
                                        device_id=(x ^ flip[0], y ^ flip[1], c), device_id_type=MESH)


def _gather_start(lands, after, *, name):
    n = len(lands)

    def body(*refs):
        ins, send, recv, token = refs[:n], refs[n + 1], refs[n + 2], refs[-1]
        place = _place()
        me = 2 * place[0] + place[1]
        for a in range(n):
            for p, flip in enumerate(CHIP_PEERS):
                k = 3 * a + p
                _chip_copy(ins[a].at[me], ins[a].at[me], send.at[k], recv.at[k], flip, place).start()
        token[...] = jnp.zeros_like(token)

    outs = pl.pallas_call(
        body, name=name, in_specs=[HBM] * n + [ANY],
        out_specs=(SEM, SEM, *[HBM] * n, pl.BlockSpec(memory_space=pltpu.VMEM)),
        out_shape=(pltpu.SemaphoreType.DMA((3 * n,)), pltpu.SemaphoreType.DMA((3 * n,)),
                   *[pltpu.HBM(l.shape, l.dtype) for l in lands], jax.ShapeDtypeStruct((8, 128), F32)),
        input_output_aliases={a: 2 + a for a in range(n)},
        compiler_params=pltpu.CompilerParams(has_side_effects=IN_FLIGHT),
    )(*[_in_hbm(l) for l in lands], after)
    return outs[0], outs[1], list(outs[2:2 + n]), outs[-1]


def _gather_wait(send, recv, lands, after, *, name):
    n = len(lands)

    def body(*refs):
        ins, send_ref, recv_ref = refs[:n], refs[n], refs[n + 1]
        place = _place()
        me = 2 * place[0] + place[1]
        for a in range(n):
            for p, flip in enumerate(CHIP_PEERS):
                k = 3 * a + p
                cp = _chip_copy(ins[a].at[me], ins[a].at[me], send_ref.at[k], recv_ref.at[k], flip, place)
                cp.wait_send()
                cp.wait_recv()

    return pl.pallas_call(
        body, name=name, in_specs=[HBM] * n + [SEM, SEM, ANY], out_specs=[HBM] * n,
        out_shape=[pltpu.HBM(l.shape, l.dtype) for l in lands],
        input_output_aliases={a: a for a in range(n)},
        compiler_params=pltpu.CompilerParams(has_side_effects=IN_FLIGHT),
    )(*lands, send, recv, after)


def _scatter_start(grad, *, name):
    def body(g_ref, land_ref, send, recv, g_thru, land_thru, token):
        place = _place()
        for p, flip in enumerate(CHIP_PEERS):
            peer_chip = 2 * (place[0] ^ flip[0]) + (place[1] ^ flip[1])
            _chip_copy(g_ref.at[peer_chip], land_ref.at[p], send.at[p], recv.at[p], flip, place).start()
        token[...] = jnp.zeros_like(token)

    land = lax.empty((3,) + grad.shape[1:], grad.dtype)
    return pl.pallas_call(
        body, name=name, in_specs=[HBM, HBM],
        out_specs=(SEM, SEM, HBM, HBM, pl.BlockSpec(memory_space=pltpu.VMEM)),
        out_shape=(pltpu.SemaphoreType.DMA((3,)), pltpu.SemaphoreType.DMA((3,)), pltpu.HBM(grad.shape, grad.dtype),
                   pltpu.HBM(land.shape, land.dtype), jax.ShapeDtypeStruct((8, 128), F32)),
        input_output_aliases={0: 2, 1: 3},
        compiler_params=pltpu.CompilerParams(has_side_effects=IN_FLIGHT),
    )(_in_hbm(grad), _in_hbm(land))


def _scatter_wait(started, after, *, name):
    n = len(started)

    def body(*refs):
        grads, lands = refs[:n], refs[n:2 * n]
        sends, recvs = refs[2 * n:3 * n], refs[3 * n:4 * n]
        place = _place()
        for a in range(n):
            for p, flip in enumerate(CHIP_PEERS):
                cp = _chip_copy(grads[a].at[0], lands[a].at[p], sends[a].at[p], recvs[a].at[p], flip, place)
                cp.wait_send()
                cp.wait_recv()

    grads, lands = [s[2] for s in started], [s[3] for s in started]
    outs = pl.pallas_call(
        body, name=name, in_specs=[HBM] * (2 * n) + [SEM] * (2 * n) + [ANY], out_specs=[HBM] * (2 * n),
        out_shape=[pltpu.HBM(a.shape, a.dtype) for a in grads + lands],
        input_output_aliases={a: a for a in range(2 * n)},
        compiler_params=pltpu.CompilerParams(has_side_effects=IN_FLIGHT),
    )(*grads, *lands, *[s[0] for s in started], *[s[1] for s in started], after)
    return list(zip(outs[:n], outs[n:]))


def _swap_sibling(arrays, *, name):
    n = len(arrays)

    def body(*refs):
        ins, outs = refs[:n], refs[n:2 * n]
        send, recv = refs[2 * n:]
        x, y, c = _place()
        started = []
        for a in range(n):
            rc = pltpu.make_async_remote_copy(
                src_ref=ins[a], dst_ref=outs[a], send_sem=send.at[a], recv_sem=recv.at[a],
                device_id=(x, y, 1 - c), device_id_type=MESH)
            rc.start()
            started.append(rc)
        for rc in started:
            rc.wait()

    return pl.pallas_call(
        body, name=name, in_specs=[ANY] * n, out_specs=[ANY] * n,
        out_shape=[jax.ShapeDtypeStruct(g.shape, g.dtype) for g in arrays],
        scratch_shapes=[pltpu.SemaphoreType.DMA((n,)), pltpu.SemaphoreType.DMA((n,))],
    )(*arrays)


def _sum_received(grad, land, *, name, tr=256):
    _, r, c = grad.shape
    tr = _pick(r, tr, 8)

    def body(chip_ref, g_ref, l_ref, o_ref):
        o_ref[...] = ((g_ref[...] + l_ref[0].astype(F32)) + l_ref[1].astype(F32)) + l_ref[2].astype(F32)

    chip = (2 * lax.axis_index("x") + lax.axis_index("y")).astype(jnp.int32).reshape(1)
    return pl.pallas_call(
        body, name=name,
        grid_spec=pltpu.PrefetchScalarGridSpec(
            num_scalar_prefetch=1, grid=(r // tr,),
            in_specs=[pl.BlockSpec((None, tr, c), lambda i, chip_ref: (chip_ref[0], i, 0)),
                      pl.BlockSpec((3, tr, c), lambda i, chip_ref: (0, i, 0))],
            out_specs=pl.BlockSpec((tr, c), lambda i, chip_ref: (i, 0))),
        out_shape=jax.ShapeDtypeStruct((r, c), F32), compiler_params=_params("parallel"),
    )(chip, grad, land)


def _allreduce_small(packed, *, name, after=None):
    r, d = packed.shape
    n_dev = 8

    def body(src_ref, out_ref, buf, send, recv):
        x, y, c = _place()
        me = 4 * x + 2 * y + c
        started = []
        for p in range(1, n_dev):
            rc = pltpu.make_async_remote_copy(
                src_ref=src_ref, dst_ref=buf.at[me], send_sem=send.at[p - 1], recv_sem=recv.at[p - 1],
                device_id=(x ^ (p >> 2), y ^ ((p >> 1) & 1), c ^ (p & 1)), device_id_type=MESH)
            rc.start()
            started.append(rc)
        buf[me] = src_ref[...]
        for rc in started:
            rc.wait()
        total = buf[0]
        for s in range(1, n_dev):
            total = total + buf[s]
        out_ref[...] = total

    vmem = pl.BlockSpec(memory_space=pltpu.VMEM)
    body, more_specs, more_args = _ordered(body, 1, after)
    return pl.pallas_call(
        body, name=name, in_specs=[vmem] + more_specs, out_specs=vmem, out_shape=jax.ShapeDtypeStruct((r, d), F32),
        scratch_shapes=[pltpu.VMEM((n_dev, r, d), F32), pltpu.SemaphoreType.DMA((n_dev - 1,)),
                        pltpu.SemaphoreType.DMA((n_dev - 1,))],
    )(packed, *more_args)


def _packed_rows(size, d):
    return -(-size // (8 * d)) * 8


def _pack_rows(arrays, d):
    rows = []
    for arr in arrays:
        flat = arr.reshape(-1).astype(F32)
        n = _packed_rows(flat.shape[0], d)
        rows.append(jnp.pad(flat, (0, n * d - flat.shape[0])).reshape(n, d))
    return jnp.concatenate(rows, axis=0)


def _unpack_rows(packed, shapes, d):
    out, row = [], 0
    for shape in shapes:
        size = math.prod(shape)
        n = _packed_rows(size, d)
        out.append(packed[row:row + n].reshape(-1)[:size].reshape(shape))
        row += n
    return out


SMALL = ("norm1_g", "gate_b", "conv_b", "conv_norm_g", "q_norm_g", "k_norm_g", "norm2_g", "ffn_conv_b")
LARGE = ("w_in", "w_conv_out", "w_attn_out", "w_out", "w_up", "w_down")
WEIGHTS = ("norm1_g", "w_in", "gate_b", "conv_w", "conv_b", "conv_norm_g", "w_conv_out", "q_norm_g", "k_norm_g",
           "w_attn_out", "w_out", "norm2_g", "w_up", "ffn_conv_w", "ffn_conv_b", "w_down")


def _head_ones(dims):
    a = dims.n_heads * dims.head_dim
    head = jnp.arange(a, dtype=jnp.int32) // dims.head_dim
    return (head[:, None] == head[None, :]).astype(BF16)


def _after(vec, token):
    return vec if token is None else vec + token[0:1, 0:1]


def _local_step(dims, x, target, small, first_weights, other_weights, send_grad):
    d, f, heads = dims.d_model, dims.d_ff, dims.n_heads
    small = dict(small)
    row = lambda name: small[name].reshape(1, -1)
    ones = _head_ones(dims)
    head_sum = _head_sum_matrix(dims)
    gq = jnp.tile(row("q_norm_g"), (1, heads))
    gk = jnp.tile(row("k_norm_g"), (1, heads))
    one_shard = lambda w: w.reshape(1, -1, w.shape[-1])

    h = _rmsnorm_fwd(x, row("norm1_g"), name="norm1")
    full = first_weights(h)
    w_in = full["w_in"]
    conv_w = jnp.pad(full["conv_w"], ((0, CONV_HALO - dims.conv_width), (0, 0)))
    ffn_w = jnp.pad(full["ffn_conv_w"], ((0, FFN_HALO - dims.ffn_conv_width), (0, 0)))
    z = _mm_nn(h, w_in, out_dtype=F32, after=full.get("token"), name="in_proj")
    a1, a3 = _conv_branch_fwd(z, conv_w, row("conv_b"), row("conv_norm_g"), dims, name="conv_branch")
    qkv = _qkv_layouts_fwd(z, gq, gk, ones, dims, name="qk_norm")
    per_group = {dil: _attn_fwd(*qkv[dil], dims, dil, name=f"attn_fwd_d{dil}") for dil in DILATIONS}
    o, lse = _attn_combine(per_group, jnp.transpose(head_sum), dims, name="attn_combine")
    full = other_weights(o)
    w_up = full["w_up"]
    w_co, w_ao, w_o, w_dn = (one_shard(full[k]) for k in ("w_conv_out", "w_attn_out", "w_out", "w_down"))
    ya = _mm_nn(a3, w_co, out_dtype=F32, name="conv_out_proj")
    yb = _mm_nn(o, w_ao, out_dtype=F32, name="attn_out_proj")
    mixed = _mix_fwd(ya, yb, z, row("gate_b"), dims, name="gate_mix")
    x1 = _mm_nn(mixed, w_o, out_dtype=F32, residual=x, name="out_proj")
    h2 = _rmsnorm_fwd(x1, row("norm2_g"), name="norm2")
    up = _mm_nn(h2, w_up, out_dtype=F32, name="up_proj")
    act = _ffn_act_fwd(up, ffn_w, row("ffn_conv_b"), dims, name="ffn_act")
    x2 = _mm_nn(act, w_dn, out_dtype=F32, residual=x1, name="down_proj")
    dy, dy_b, loss = _loss_head(x2, target, name="loss_head")

    grads = {}

    def large(name, g):
        grads[name], g_bf16 = g
        return send_grad(name, g_bf16)

    sent = large("w_down", _mm_tn(act, dy_b, n_shards=1, name="dw_down"))
    dact = _mm_nt(dy_b, w_dn, out_dtype=BF16, after=sent, name="d_act")
    du, dfw, dfb = _ffn_act_bwd(dact, up, ffn_w, row("ffn_conv_b"), dims, name="ffn_act_bwd")
    grads["ffn_conv_w"], grads["ffn_conv_b"] = dfw[:dims.ffn_conv_width], dfb
    dup = _ffn_conv_bwd(du, ffn_w, dims, name="ffn_conv_bwd")
    sent = large("w_up", _mm_tn(h2, dup, n_shards=N_CHIPS, name="dw_up"))
    dh2 = _mm_nt(dup, w_up, out_dtype=F32, after=sent, name="d_h2")
    dx1, dx1_b, grads["norm2_g"] = _rmsnorm_bwd(x1, row("norm2_g"), dh2, dy, want_bf16=True, name="norm2_bwd")
    sent = large("w_out", _mm_tn(mixed, dx1_b, n_shards=1, name="dw_out"))
    dmix = _mm_nt(dx1_b, w_o, out_dtype=F32, after=sent, name="d_mix")
    dya, dyb, dz_gate, grads["gate_b"] = _mix_bwd(dmix, ya, yb, z, row("gate_b"), dims, name="gate_mix_bwd")
    sent = large("w_conv_out", _mm_tn(a3, dya, n_shards=1, name="dw_conv_out"))
    da3 = _mm_nt(dya, w_co, out_dtype=F32, after=sent, name="d_conv_act")
    da1, grads["conv_norm_g"] = _conv_norm_bwd(da3, a1, row("conv_norm_g"), name="conv_norm_bwd")
    dz_glu, dcw, grads["conv_b"] = _conv_branch_bwd(da1, z, conv_w, dims, name="conv_branch_bwd")
    grads["conv_w"] = dcw[:dims.conv_width]
    sent = large("w_attn_out", _mm_tn(o, dyb, n_shards=1, name="dw_attn_out"))
    do = _mm_nt(dyb, w_ao, out_dtype=BF16, after=sent, name="d_attn")
    dos, deltas = _attn_bwd_prep(do, o, head_sum, dims, name="attn_bwd_prep")
    dqkv = {dil: _attn_bwd(*qkv[dil], dos[dil], lse[dil], deltas[dil], dims, dil, name=f"attn_bwd_d{dil}")
            for dil in DILATIONS}
    dz_qkv, dgq, dgk = _qkv_layouts_bwd(z, dqkv, gq, gk, ones, dims, name="qk_norm_bwd")
    grads["q_norm_g"] = dgq.reshape(heads, dims.head_dim).sum(axis=0)
    grads["k_norm_g"] = dgk.reshape(heads, dims.head_dim).sum(axis=0)
    dz = jnp.concatenate([dz_glu, dz_qkv, dz_gate], axis=1)
    sent = large("w_in", _mm_tn(h, dz, n_shards=N_CHIPS, name="dw_in"))
    dh = _mm_nt(dz, w_in, out_dtype=F32, after=sent, name="d_h")
    dx, grads["norm1_g"] = _rmsnorm_bwd(x, row("norm1_g"), dh, dx1, want_bf16=False, name="norm1_bwd")
    return loss, dx, grads


def _step(dims, x, target, w, m, v):
    d = dims.d_model
    t = dims.tokens
    sq = lambda a: a.reshape(a.shape[1:])
    w2, m2, v2 = ({k: sq(a) for k, a in grp.items()} for grp in (w, m, v))

    conv_pad = jnp.pad(w2["conv_w"], ((0, CONV_HALO - dims.conv_width), (0, 0)))
    ffn_pad = jnp.pad(w2["ffn_conv_w"], ((0, FFN_HALO - dims.ffn_conv_width), (0, 0)))
    gathered_names = LARGE + ("conv_w", "ffn_conv_w")
    lands = dict(zip(gathered_names, _cast_to_lands([w2[k] for k in LARGE] + [conv_pad, ffn_pad],
                                                   [BF16] * len(LARGE) + [F32, F32], name="cast_weights")))
    first_names = ("w_in", "conv_w", "ffn_conv_w")
    other_names = tuple(k for k in gathered_names if k not in first_names)
    first = _gather_start([lands[k] for k in first_names], x, name="gather_start_first")
    other = []
    cols = lambda g, rows: jnp.moveaxis(g, 0, 1).reshape(g.shape[1], -1)[:rows]

    def first_weights(after):
        got = dict(zip(first_names, _gather_wait(*first[:3], after, name="gather_wait_first")))
        other.extend(_gather_start([lands[k] for k in other_names], got["w_in"], name="gather_start_other"))
        got["conv_w"] = cols(got["conv_w"], dims.conv_width)
        got["ffn_conv_w"] = cols(got["ffn_conv_w"], dims.ffn_conv_width)
        got["token"] = other[3]
        return got

    def other_weights(after):
        return dict(zip(other_names, _gather_wait(*other[:3], after, name="gather_wait_other")))

    started = {}

    def send_grad(name, g):
        send, recv, g_thru, land, token = _scatter_start(g.reshape(N_CHIPS, -1, g.shape[-1]), name=f"scatter_start_{name}")
        started[name] = (send, recv, g_thru, land)
        return token

    small = {k: w2[k] for k in SMALL}
    small["norm1_g"] = _after(small["norm1_g"].reshape(1, -1), first[3])
    loss, dx, grads = _local_step(dims, x.reshape(t, d), target.reshape(t, d), small, first_weights, other_weights, send_grad)

    def finish(names, after, tag):
        arrived = _scatter_wait([started[k] for k in names], after, name=f"scatter_wait_{tag}")
        blocks = [grads[k].reshape(N_CHIPS, -1, grads[k].shape[-1]) for k in names]
        mine = [_sum_received(g, land, name=f"sum_{k}") for k, g, (_, land) in zip(names, blocks, arrived)]
        theirs = _swap_sibling(mine, name=f"swap_sibling_{tag}")
        return {k: _adamw(w2[k], [a, b], m2[k], v2[k], name=f"adamw_{k}") for k, a, b in zip(names, mine, theirs)}

    out = finish([k for k in LARGE if k != "w_in"], dx, "others")

    small_names = SMALL + ("conv_w", "ffn_conv_w")
    packed = _pack_rows([grads[k] for k in small_names] + [loss[0, 0]], d)
    reduced = _allreduce_small(packed, after=[upd[1] for upd in out.values()], name="allreduce_small")
    shapes = [grads[k].shape for k in small_names] + [()]
    *small_g, loss_total = _unpack_rows(reduced, shapes, d)
    small_g = dict(zip(small_names, small_g))
    chip = 2 * lax.axis_index("x") + lax.axis_index("y")
    for k in ("conv_w", "ffn_conv_w"):
        width = w2[k].shape[1]
        small_g[k] = lax.dynamic_slice_in_dim(small_g[k], chip * width, width, axis=1)

    small_shapes = [w2[k].shape for k in small_names]
    pack = lambda grp: _pack_rows([grp[k] for k in small_names], d)
    results = _adamw(pack(w2), [pack(small_g)], pack(m2), pack(v2), name="adamw_small")
    unpacked = [_unpack_rows(r, small_shapes, d) for r in results]
    for i, k in enumerate(small_names):
        out[k] = tuple(u[i] for u in unpacked)
    out.update(finish(["w_in"], results[1], "w_in"))

    lead =lambda a: a.reshape((1,) + a.shape)
    ordered = [[lead(out[k][j].reshape(w2[k].shape)) for k in WEIGHTS] for j in range(4)]
    return (loss_total, dx.reshape(x.shape), *ordered[0], *ordered[1], *ordered[2], *ordered[3])


def kernel(x, norm1_g, w_in, gate_b, conv_w, conv_b, conv_norm_g, w_conv_out, q_norm_g, k_norm_g, w_attn_out, w_out, norm2_g, w_up, ffn_conv_w, ffn_conv_b, w_down, loss_target, m_norm1_g, m_w_in, m_gate_b, m_conv_w, m_conv_b, m_conv_norm_g, m_w_conv_out, m_q_norm_g, m_k_norm_g, m_w_attn_out, m_w_out, m_norm2_g, m_w_up, m_ffn_conv_w, m_ffn_conv_b, m_w_down, v_norm1_g, v_w_in, v_gate_b, v_conv_w, v_conv_b, v_conv_norm_g, v_w_conv_out, v_q_norm_g, v_k_norm_g, v_w_attn_out, v_w_out, v_norm2_g, v_w_up, v_ffn_conv_w, v_ffn_conv_b, v_w_down):
    w = dict(zip(WEIGHTS, (norm1_g, w_in, gate_b, conv_w, conv_b, conv_norm_g, w_conv_out, q_norm_g, k_norm_g,
                           w_attn_out, w_out, norm2_g, w_up, ffn_conv_w, ffn_conv_b, w_down)))
    m = dict(zip(WEIGHTS, (m_norm1_g, m_w_in, m_gate_b, m_conv_w, m_conv_b, m_conv_norm_g, m_w_conv_out, m_q_norm_g,
                           m_k_norm_g, m_w_attn_out, m_w_out, m_norm2_g, m_w_up, m_ffn_conv_w, m_ffn_conv_b, m_w_down)))
    v = dict(zip(WEIGHTS, (v_norm1_g, v_w_in, v_gate_b, v_conv_w, v_conv_b, v_conv_norm_g, v_w_conv_out, v_q_norm_g,
                           v_k_norm_g, v_w_attn_out, v_w_out, v_norm2_g, v_w_up, v_ffn_conv_w, v_ffn_conv_b, v_w_down)))
    dims = Dims(d_model=x.shape[-1], batch_local=x.shape[0], seq=x.shape[1], d_ff=w_down.shape[1] * N_CHIPS)
    return _step(dims, x, loss_target, w, m, v)
```

```python
import functools
import math
from typing import NamedTuple

import jax
import jax.numpy as jnp
from jax import lax
from jax.experimental import pallas as pl
from jax.experimental.pallas import tpu as pltpu

F32 = jnp.float32
BF16 = jnp.bfloat16

RMS_EPS = 1e-6
MASKED_SCORE = -1e30
ATTN_BLOCK = 128
DILATIONS = (1, 4, 16)
CONV_HALO = 32
FFN_HALO = 8
ADAM_LR, ADAM_B1, ADAM_B2, ADAM_EPS, ADAM_WD, ADAM_STEP = 0.001, 0.9, 0.999, 1e-08, 0.01, 10
V7X_VMEM_LIMIT_BYTES = 56 * 2 ** 20
N_CHIPS = 4
MESH = pl.DeviceIdType.MESH


class Dims(NamedTuple):
    d_model: int = 1024
    n_heads: int = 16
    head_dim: int = 64
    d_ff: int = 2816
    seq: int = 2048
    batch_local: int = 2
    conv_width: int = 31
    ffn_conv_width: int = 3

    @property
    def tokens(self):
        return self.seq * self.batch_local


def _params(*semantics):
    return pltpu.CompilerParams(dimension_semantics=semantics, vmem_limit_bytes=V7X_VMEM_LIMIT_BYTES)


ANY = pl.BlockSpec(memory_space=pl.ANY)


def _ordered(body, n_inputs, after):
    after = [] if after is None else list(after) if isinstance(after, (list, tuple)) else [after]
    if not after:
        return body, [], []

    def wrapped(*refs):
        return body(*refs[:n_inputs], *refs[n_inputs + len(after):])

    return wrapped, [ANY] * len(after), after


def _pick(n, target, mult=128):
    if n <= target:
        return n
    best = None
    for t in range(mult, target + 1, mult):
        if n % t == 0:
            best = t
    assert best is not None, (n, target, mult)
    return best


def _sigmoid(v):
    return 1.0 / (1.0 + jnp.exp(-v))


def _mm_nn(a, w, *, out_dtype, name, residual=None, after=None, tm=1024, tn=1408, tk=2816):
    m, k = a.shape
    nsh, k2, c = w.shape
    assert k == k2 and a.dtype == BF16 and w.dtype == BF16
    n = nsh * c
    tm, tn, tk = _pick(m, tm, 8), _pick(c, tn), _pick(k, tk)
    nk, cpn = k // tk, c // tn

    def body(*refs):
        if residual is None:
            a_ref, w_ref, o_ref, acc = refs
        else:
            a_ref, w_ref, r_ref, o_ref, acc = refs
        prod = jnp.dot(a_ref[...], w_ref[...], preferred_element_type=F32)

        def finish(total):
            if residual is not None:
                total = total + r_ref[...]
            o_ref[...] = total.astype(out_dtype)

        if nk == 1:
            finish(prod)
        else:
            kk = pl.program_id(2)

            @pl.when(kk == 0)
            def _():
                acc[...] = prod

            @pl.when(kk > 0)
            def _():
                acc[...] += prod

            @pl.when(kk == nk - 1)
            def _():
                finish(acc[...])

    in_specs = [pl.BlockSpec((tm, tk), lambda i, j, kk: (i, kk)),
                pl.BlockSpec((None, tk, tn), lambda i, j, kk: (j // cpn, kk, j % cpn))]
    args = [a, w]
    if residual is not None:
        in_specs.append(pl.BlockSpec((tm, tn), lambda i, j, kk: (i, j)))
        args.append(residual)
    body, more_specs, more_args = _ordered(body, len(args), after)
    return pl.pallas_call(
        body, name=name, grid=(m // tm, n // tn, nk),
        in_specs=in_specs + more_specs, out_specs=pl.BlockSpec((tm, tn), lambda i, j, kk: (i, j)),
        out_shape=jax.ShapeDtypeStruct((m, n), out_dtype),
        scratch_shapes=[pltpu.VMEM((tm, tn) if nk > 1 else (8, 128), F32)],
        compiler_params=_params("parallel", "parallel", "arbitrary"),
    )(*args, *more_args)


def _mm_nt(a, w, *, out_dtype, name, after=None, tm=1024, tn=1408, tk=1792):
    m, k = a.shape
    nsh, r, c = w.shape
    assert k == nsh * c and a.dtype == BF16 and w.dtype == BF16
    tm, tn, tk = _pick(m, tm, 8), _pick(r, tn), _pick(c, tk)
    nk, cpk = k // tk, c // tk

    def body(a_ref, w_ref, o_ref, acc):
        prod = lax.dot_general(a_ref[...], w_ref[...], (((1,), (1,)), ((), ())), preferred_element_type=F32)
        if nk == 1:
            o_ref[...] = prod.astype(out_dtype)
        else:
            kk = pl.program_id(2)

            @pl.when(kk == 0)
            def _():
                acc[...] = prod

            @pl.when(kk > 0)
            def _():
                acc[...] += prod

            @pl.when(kk == nk - 1)
            def _():
                o_ref[...] = acc[...].astype(out_dtype)

    body, more_specs, more_args = _ordered(body, 2, after)
    return pl.pallas_call(
        body, name=name, grid=(m // tm, r // tn, nk),
        in_specs=[pl.BlockSpec((tm, tk), lambda i, j, kk: (i, kk)),
                  pl.BlockSpec((None, tn, tk), lambda i, j, kk: (kk // cpk, j, kk % cpk))] + more_specs,
        out_specs=pl.BlockSpec((tm, tn), lambda i, j, kk: (i, j)),
        out_shape=jax.ShapeDtypeStruct((m, r), out_dtype),
        scratch_shapes=[pltpu.VMEM((tm, tn) if nk > 1 else (8, 128), F32)],
        compiler_params=_params("parallel", "parallel", "arbitrary"),
    )(a, w, *more_args)


MM_TN_VMEM_BYTES = 44 * 2 ** 20


def _mm_tn(a, b, *, n_shards, name, tm=1408, tn=1408):
    t, m = a.shape
    t2, n = b.shape
    assert t == t2 and a.dtype == BF16 and b.dtype == BF16
    c = n // n_shards
    tm, tn = _pick(m, tm), _pick(c, tn)
    fixed = 2 * tm * tn * 6
    if 4 * t * (tm + tn) + fixed <= MM_TN_VMEM_BYTES:
        tk = t
    else:
        tk = _pick(t, (MM_TN_VMEM_BYTES - fixed - 4 * tm * tn) // (4 * (tm + tn)), 8)
    nk, cpn = t // tk, c // tn

    def body(a_ref, b_ref, o_ref, ob_ref, acc):
        kk = pl.program_id(2)
        prod = lax.dot_general(a_ref[...], b_ref[...], (((0,), (0,)), ((), ())), preferred_element_type=F32)

        def finish(total):
            o_ref[...] = total
            ob_ref[...] = total.astype(BF16)

        if nk == 1:
            finish(prod)
        else:
            @pl.when(kk == 0)
            def _():
                acc[...] = prod

            @pl.when(kk > 0)
            def _():
                acc[...] += prod

            @pl.when(kk == nk - 1)
            def _():
                finish(acc[...])

    out_spec = pl.BlockSpec((None, tm, tn), lambda i, j, kk: (j // cpn, i, j % cpn))
    return pl.pallas_call(
        body, name=name, grid=(m // tm, n // tn, nk),
        in_specs=[pl.BlockSpec((tk, tm), lambda i, j, kk: (kk, i)),
                  pl.BlockSpec((tk, tn), lambda i, j, kk: (kk, j))],
        out_specs=[out_spec, out_spec],
        out_shape=[jax.ShapeDtypeStruct((n_shards, m, c), F32), jax.ShapeDtypeStruct((n_shards, m, c), BF16)],
        scratch_shapes=[pltpu.VMEM((tm, tn) if nk > 1 else (8, 128), F32)],
        compiler_params=_params("parallel", "parallel", "arbitrary"),
    )(a, b)


def _row_spec(tr, width, col=0):
    return pl.BlockSpec((tr, width), lambda i, col=col: (i, col))


def _vec_spec(width, col=0):
    return pl.BlockSpec((1, width), lambda i, col=col: (0, col))


def _accumulate(ref, value, first):
    @pl.when(first)
    def _():
        ref[...] = value

    @pl.when(jnp.logical_not(first))
    def _():
        ref[...] += value


def _rmsnorm_fwd(x, g, *, name, tr=512):
    t, d = x.shape
    tr = _pick(t, tr, 8)

    def body(x_ref, g_ref, o_ref):
        xv = x_ref[...]
        r = lax.rsqrt(jnp.mean(xv * xv, axis=-1, keepdims=True) + RMS_EPS)
        o_ref[...] = (xv * r * g_ref[...]).astype(BF16)

    return pl.pallas_call(
        body, name=name, grid=(t // tr,),
        in_specs=[_row_spec(tr, d), _vec_spec(d)], out_specs=_row_spec(tr, d),
        out_shape=jax.ShapeDtypeStruct((t, d), BF16), compiler_params=_params("parallel"),
    )(x, g)


def _rmsnorm_bwd(x, g, dy, dres, *, name, want_bf16, tr=512):
    t, d = x.shape
    tr = _pick(t, tr, 8)

    def body(x_ref, g_ref, dy_ref, dres_ref, *outs):
        dx_ref, dg_ref = outs[0], outs[-1]
        xv, dyv = x_ref[...], dy_ref[...].astype(F32)
        r = lax.rsqrt(jnp.mean(xv * xv, axis=-1, keepdims=True) + RMS_EPS)
        gy = dyv * g_ref[...]
        dx = dres_ref[...] + r * gy - xv * (r * r * r) * jnp.mean(xv * gy, axis=-1, keepdims=True)
        dx_ref[...] = dx
        if want_bf16:
            outs[1][...] = dx.astype(BF16)
        _accumulate(dg_ref, jnp.sum(dyv * xv * r, axis=0, keepdims=True), pl.program_id(0) == 0)

    out_shape = [jax.ShapeDtypeStruct((t, d), F32)]
    out_specs = [_row_spec(tr, d)]
    if want_bf16:
        out_shape.append(jax.ShapeDtypeStruct((t, d), BF16))
        out_specs.append(_row_spec(tr, d))
    out_shape.append(jax.ShapeDtypeStruct((1, d), F32))
    out_specs.append(_vec_spec(d))
    return pl.pallas_call(
        body, name=name, grid=(t // tr,),
        in_specs=[_row_spec(tr, d), _vec_spec(d), _row_spec(tr, d), _row_spec(tr, d)],
        out_specs=out_specs, out_shape=out_shape, compiler_params=_params("arbitrary"),
    )(x, g, dy, dres)


def _head_mean(v, ones_ref, head_dim):
    hi = v.astype(BF16)
    lo = (v - hi.astype(F32)).astype(BF16)
    e = ones_ref[...]
    total = jnp.dot(hi, e, preferred_element_type=F32) + jnp.dot(lo, e, preferred_element_type=F32)
    return total * (1.0 / head_dim)


def _qkv_fwd(z, gq, gk, head_ones, dims, *, name, tr=256):
    t = z.shape[0]
    a = dims.n_heads * dims.head_dim
    tr = _pick(t, tr, 8)
    q_scale = dims.head_dim ** -0.5

    def body(q_ref, k_ref, v_ref, gq_ref, gk_ref, e_ref, qo_ref, ko_ref, vo_ref):
        qv, kv = q_ref[...], k_ref[...]
        rq = lax.rsqrt(_head_mean(qv * qv, e_ref, dims.head_dim) + RMS_EPS)
        rk = lax.rsqrt(_head_mean(kv * kv, e_ref, dims.head_dim) + RMS_EPS)
        qo_ref[...] = (qv * rq * gq_ref[...] * q_scale).astype(BF16)
        ko_ref[...] = (kv * rk * gk_ref[...]).astype(BF16)
        vo_ref[...] = v_ref[...].astype(BF16)

    return pl.pallas_call(
        body, name=name, grid=(t // tr,),
        in_specs=[_row_spec(tr, a, 2), _row_spec(tr, a, 3), _row_spec(tr, a, 4), _vec_spec(a), _vec_spec(a),
                  pl.BlockSpec((a, a), lambda i: (0, 0))],
        out_specs=[_row_spec(tr, a)] * 3, out_shape=[jax.ShapeDtypeStruct((t, a), BF16)] * 3,
        compiler_params=_params("parallel"),
    )(z, z, z, gq, gk, head_ones)


def _qkv_bwd(z, dqs, dks, dvs, gq, gk, head_ones, dims, *, name, tr=256):
    t = z.shape[0]
    a = dims.n_heads * dims.head_dim
    tr = _pick(t, tr, 8)
    q_scale = dims.head_dim ** -0.5
    ng = len(dqs)

    def body(*refs):
        q_ref, k_ref = refs[:2]
        dq_refs, dk_refs, dv_refs = refs[2:2 + ng], refs[2 + ng:2 + 2 * ng], refs[2 + 2 * ng:2 + 3 * ng]
        gq_ref, gk_ref, e_ref = refs[2 + 3 * ng:5 + 3 * ng]
        dz_ref, dgq_ref, dgk_ref = refs[5 + 3 * ng:]
        first = pl.program_id(0) == 0

        def norm_bwd(x_ref, d_refs, g_ref, scale, col, dg_ref):
            xv = x_ref[...]
            dy = sum(r[...] for r in d_refs) * scale
            r = lax.rsqrt(_head_mean(xv * xv, e_ref, dims.head_dim) + RMS_EPS)
            gy = dy * g_ref[...]
            dx = r * gy - xv * (r * r * r) * _head_mean(xv * gy, e_ref, dims.head_dim)
            dz_ref[:, col * a:(col + 1) * a] = dx.astype(BF16)
            _accumulate(dg_ref, jnp.sum(dy * xv * r, axis=0, keepdims=True), first)

        norm_bwd(q_ref, dq_refs, gq_ref, q_scale, 0, dgq_ref)
        norm_bwd(k_ref, dk_refs, gk_ref, 1.0, 1, dgk_ref)
        dz_ref[:, 2 * a:3 * a] = sum(r[...] for r in dv_refs).astype(BF16)

    in_specs = ([_row_spec(tr, a, 2), _row_spec(tr, a, 3)] + [_row_spec(tr, a)] * (3 * ng)
                + [_vec_spec(a), _vec_spec(a), pl.BlockSpec((a, a), lambda i: (0, 0))])
    return pl.pallas_call(
        body, name=name, grid=(t // tr,), in_specs=in_specs,
        out_specs=[_row_spec(tr, 3 * a), _vec_spec(a), _vec_spec(a)],
        out_shape=[jax.ShapeDtypeStruct((t, 3 * a), BF16)] + [jax.ShapeDtypeStruct((1, a), F32)] * 2,
        compiler_params=_params("arbitrary"),
    )(z, z, *dqs, *dks, *dvs, gq, gk, head_ones)


CONV_ROWS = 16


def _seq_specs(dims, ts, width, halo, col, *, nxt=False):
    nst, per = dims.seq // ts, ts // halo
    last = dims.tokens // halo - 1
    cur = pl.BlockSpec((ts, width), lambda b, i: (b * nst + i, col))
    if nxt:
        edge = pl.BlockSpec((halo, width), lambda b, i: (jnp.minimum((b * nst + i + 1) * per, last), col))
    else:
        edge = pl.BlockSpec((halo, width), lambda b, i: (jnp.maximum((b * nst + i) * per - 1, 0), col))
    return cur, edge


SUBLANES = 8


def _shifted_copies(buf, shifted):
    rows = shifted.shape[1]
    for s in range(1, SUBLANES):
        shifted[s - 1] = buf[pl.ds(s, rows), :]


def _window(buf, shifted, start, size):
    a, s = divmod(start, SUBLANES)
    src = buf if s == 0 else shifted.at[s - 1]
    return src[pl.ds(SUBLANES * a, size), :]


def _conv_branch_fwd(z, w, b, g, dims, *, name, ts=128):
    t, c, kw = z.shape[0], dims.d_model, dims.conv_width
    base = CONV_HALO - (kw - 1)

    def body(av_ref, hv_ref, ag_ref, hg_ref, w_ref, b_ref, g_ref, a1_ref, a3_ref, buf, shifted):
        i = pl.program_id(1)
        buf[CONV_HALO:, :] = av_ref[...] * _sigmoid(ag_ref[...])
        buf[0:CONV_HALO, :] = jnp.where(i > 0, hv_ref[...] * _sigmoid(hg_ref[...]), 0.0)
        _shifted_copies(buf, shifted)
        for r0 in range(0, ts, CONV_ROWS):
            acc = jnp.broadcast_to(b_ref[...], (CONV_ROWS, c))
            for k in range(kw):
                acc = acc + w_ref[k:k + 1, :] * _window(buf, shifted, r0 + base + k, CONV_ROWS)
            a1_ref[r0:r0 + CONV_ROWS, :] = acc
            a2 = acc * lax.rsqrt(jnp.mean(acc * acc, axis=-1, keepdims=True) + RMS_EPS) * g_ref[...]
            a3_ref[r0:r0 + CONV_ROWS, :] = (a2 * _sigmoid(a2)).astype(BF16)

    vec = pl.BlockSpec((1, c), lambda b, i: (0, 0))
    out = pl.BlockSpec((ts, c), lambda b, i: (b * (dims.seq // ts) + i, 0))
    return pl.pallas_call(
        body, name=name, grid=(dims.batch_local, dims.seq // ts),
        in_specs=[*_seq_specs(dims, ts, c, CONV_HALO, 0), *_seq_specs(dims, ts, c, CONV_HALO, 1),
                  pl.BlockSpec((CONV_HALO, c), lambda b, i: (0, 0)), vec, vec],
        out_specs=[out, out],
        out_shape=[jax.ShapeDtypeStruct((t, c), F32), jax.ShapeDtypeStruct((t, c), BF16)],
        scratch_shapes=[pltpu.VMEM((CONV_HALO + ts, c), F32),
                        pltpu.VMEM((SUBLANES - 1, CONV_HALO + ts - SUBLANES, c), F32)],
        compiler_params=_params("parallel", "parallel"),
    )(z, z, z, z, w, b, g)


def _conv_norm_bwd(da3, a1, g, *, name, tr=256):
    t, c = a1.shape
    tr = _pick(t, tr, 8)

    def body(d_ref, a_ref, g_ref, o_ref, dg_ref):
        a1v, gv = a_ref[...], g_ref[...]
        r = lax.rsqrt(jnp.mean(a1v * a1v, axis=-1, keepdims=True) + RMS_EPS)
        a2 = a1v * r * gv
        sg = _sigmoid(a2)
        da2 = d_ref[...].astype(F32) * sg * (1.0 + a2 * (1.0 - sg))
        gy = da2 * gv
        o_ref[...] = r * gy - a1v * (r * r * r) * jnp.mean(a1v * gy, axis=-1, keepdims=True)
        _accumulate(dg_ref, jnp.sum(da2 * a1v * r, axis=0, keepdims=True), pl.program_id(0) == 0)

    return pl.pallas_call(
        body, name=name, grid=(t // tr,),
        in_specs=[_row_spec(tr, c), _row_spec(tr, c), _vec_spec(c)],
        out_specs=[_row_spec(tr, c), _vec_spec(c)],
        out_shape=[jax.ShapeDtypeStruct((t, c), F32), jax.ShapeDtypeStruct((1, c), F32)],
        compiler_params=_params("arbitrary"),
    )(da3, a1, g)


def _conv_branch_bwd(da1, z, w, dims, *, name, ts=128):
    t, c, kw = z.shape[0], dims.d_model, dims.conv_width
    nst = dims.seq // ts
    base = CONV_HALO - (kw - 1)

    def body(d_ref, dn_ref, av_ref, hv_ref, ag_ref, hg_ref, w_ref, dz_ref, dw_ref, db_ref, abuf, dbuf, ashift, dshift):
        i = pl.program_id(1)
        first = jnp.logical_and(pl.program_id(0) == 0, i == 0)
        abuf[CONV_HALO:, :] = av_ref[...] * _sigmoid(ag_ref[...])
        abuf[0:CONV_HALO, :] = jnp.where(i > 0, hv_ref[...] * _sigmoid(hg_ref[...]), 0.0)
        d1 = d_ref[...]
        dbuf[0:ts, :] = d1
        dbuf[ts:, :] = jnp.where(i < nst - 1, dn_ref[...], 0.0)
        _shifted_copies(abuf, ashift)
        _shifted_copies(dbuf, dshift)

        @pl.when(first)
        def _():
            dw_ref[...] = jnp.zeros_like(dw_ref)
            db_ref[...] = jnp.zeros_like(db_ref)

        db_ref[...] += jnp.sum(d1, axis=0, keepdims=True)
        for k in range(kw):
            dw_ref[k:k + 1, :] += jnp.sum(d1 * _window(abuf, ashift, base + k, ts), axis=0, keepdims=True)
        for r0 in range(0, ts, CONV_ROWS):
            acc = jnp.zeros((CONV_ROWS, c), F32)
            for k in range(kw):
                acc = acc + w_ref[k:k + 1, :] * _window(dbuf, dshift, r0 + (kw - 1) - k, CONV_ROWS)
            av = av_ref[r0:r0 + CONV_ROWS, :]
            sg = _sigmoid(ag_ref[r0:r0 + CONV_ROWS, :])
            dz_ref[r0:r0 + CONV_ROWS, 0:c] = (acc * sg).astype(BF16)
            dz_ref[r0:r0 + CONV_ROWS, c:2 * c] = (acc * av * sg * (1.0 - sg)).astype(BF16)

    cur, nxt = _seq_specs(dims, ts, c, CONV_HALO, 0, nxt=True)
    return pl.pallas_call(
        body, name=name, grid=(dims.batch_local, nst),
        in_specs=[cur, nxt, *_seq_specs(dims, ts, c, CONV_HALO, 0), *_seq_specs(dims, ts, c, CONV_HALO, 1),
                  pl.BlockSpec((CONV_HALO, c), lambda b, i: (0, 0))],
        out_specs=[pl.BlockSpec((ts, 2 * c), lambda b, i: (b * nst + i, 0)),
                   pl.BlockSpec((CONV_HALO, c), lambda b, i: (0, 0)), pl.BlockSpec((1, c), lambda b, i: (0, 0))],
        out_shape=[jax.ShapeDtypeStruct((t, 2 * c), BF16), jax.ShapeDtypeStruct((CONV_HALO, c), F32),
                   jax.ShapeDtypeStruct((1, c), F32)],
        scratch_shapes=[pltpu.VMEM((CONV_HALO + ts, c), F32)] * 2
        + [pltpu.VMEM((SUBLANES - 1, CONV_HALO + ts - SUBLANES, c), F32)] * 2,
        compiler_params=_params("arbitrary", "arbitrary"),
    )(da1, da1, z, z, z, z, w)


FFN_ROWS = 16
FFN_COLS = 256


def _ffn_chunks(ts, f):
    cw = _pick(f, FFN_COLS)
    return [(r0, c0, cw) for r0 in range(0, ts, FFN_ROWS) for c0 in range(0, f, cw)]


def _ffn_conv(buf, w_ref, b_ref, r0, cols, kw):
    base = FFN_HALO - (kw - 1)
    u = jnp.broadcast_to(b_ref[:, cols], (FFN_ROWS, cols.stop - cols.start))
    for k in range(kw):
        u = u + w_ref[k:k + 1, cols] * buf[pl.ds(r0 + base + k, FFN_ROWS), cols]
    return u


def _ffn_act_fwd(up, w, b, dims, *, name, ts=128):
    t, f, kw = up.shape[0], dims.d_ff, dims.ffn_conv_width

    def body(up_ref, h_ref, w_ref, b_ref, o_ref, buf):
        buf[FFN_HALO:, :] = up_ref[...]
        buf[0:FFN_HALO, :] = jnp.where(pl.program_id(1) > 0, h_ref[...], 0.0)
        for r0, c0, cw in _ffn_chunks(ts, f):
            uv = _ffn_conv(buf, w_ref, b_ref, r0, slice(c0, c0 + cw), kw)
            ug = _ffn_conv(buf, w_ref, b_ref, r0, slice(f + c0, f + c0 + cw), kw)
            o_ref[r0:r0 + FFN_ROWS, c0:c0 + cw] = (ug * _sigmoid(ug) * uv).astype(BF16)

    full = lambda rows: pl.BlockSpec((rows, 2 * f), lambda b_, i: (0, 0))
    return pl.pallas_call(
        body, name=name, grid=(dims.batch_local, dims.seq // ts),
        in_specs=[*_seq_specs(dims, ts, 2 * f, FFN_HALO, 0), full(FFN_HALO), full(1)],
        out_specs=pl.BlockSpec((ts, f), lambda b_, i: (b_ * (dims.seq // ts) + i, 0)),
        out_shape=jax.ShapeDtypeStruct((t, f), BF16),
        scratch_shapes=[pltpu.VMEM((FFN_HALO + ts, 2 * f), F32)],
        compiler_params=_params("parallel", "parallel"),
    )(up, up, w, b)


def _ffn_act_bwd(dact, up, w, b, dims, *, name, ts=128):
    t, f, kw = up.shape[0], dims.d_ff, dims.ffn_conv_width
    base = FFN_HALO - (kw - 1)

    def body(d_ref, up_ref, h_ref, w_ref, b_ref, du_ref, dw_ref, db_ref, buf):
        i = pl.program_id(1)
        first = jnp.logical_and(pl.program_id(0) == 0, i == 0)
        buf[FFN_HALO:, :] = up_ref[...]
        buf[0:FFN_HALO, :] = jnp.where(i > 0, h_ref[...], 0.0)
        for r0, c0, cw in _ffn_chunks(ts, f):
            vcols, gcols = slice(c0, c0 + cw), slice(f + c0, f + c0 + cw)
            uv = _ffn_conv(buf, w_ref, b_ref, r0, vcols, kw)
            ug = _ffn_conv(buf, w_ref, b_ref, r0, gcols, kw)
            d = d_ref[r0:r0 + FFN_ROWS, vcols].astype(F32)
            sg = _sigmoid(ug)
            du_ref[r0:r0 + FFN_ROWS, vcols] = d * ug * sg
            du_ref[r0:r0 + FFN_ROWS, gcols] = d * uv * sg * (1.0 + ug * (1.0 - sg))

        @pl.when(first)
        def _():
            dw_ref[...] = jnp.zeros_like(dw_ref)
            db_ref[...] = jnp.zeros_like(db_ref)

        du = du_ref[...]
        db_ref[...] += jnp.sum(du, axis=0, keepdims=True)
        for k in range(kw):
            dw_ref[k:k + 1, :] += jnp.sum(du * buf[pl.ds(base + k, ts), :], axis=0, keepdims=True)

    nst = dims.seq // ts
    full = lambda rows: pl.BlockSpec((rows, 2 * f), lambda b_, i: (0, 0))
    return pl.pallas_call(
        body, name=name, grid=(dims.batch_local, nst),
        in_specs=[pl.BlockSpec((ts, f), lambda b_, i: (b_ * nst + i, 0)),
                  *_seq_specs(dims, ts, 2 * f, FFN_HALO, 0), full(FFN_HALO), full(1)],
        out_specs=[pl.BlockSpec((ts, 2 * f), lambda b_, i: (b_ * nst + i, 0)), full(FFN_HALO), full(1)],
        out_shape=[jax.ShapeDtypeStruct((t, 2 * f), F32), jax.ShapeDtypeStruct((FFN_HALO, 2 * f), F32),
                   jax.ShapeDtypeStruct((1, 2 * f), F32)],
        scratch_shapes=[pltpu.VMEM((FFN_HALO + ts, 2 * f), F32)],
        compiler_params=_params("arbitrary", "arbitrary"),
    )(dact, up, up, w, b)


def _ffn_conv_bwd(du, w, dims, *, name, ts=128):
    t, f2 = du.shape
    kw = dims.ffn_conv_width
    nst = dims.seq // ts

    def body(d_ref, dn_ref, w_ref, o_ref, buf):
        buf[0:ts, :] = d_ref[...]
        buf[ts:, :] = jnp.where(pl.program_id(1) < nst - 1, dn_ref[...], 0.0)
        for r0, c0, cw in _ffn_chunks(ts, f2):
            cols = slice(c0, c0 + cw)
            acc = jnp.zeros((FFN_ROWS, cw), F32)
            for k in range(kw):
                acc = acc + w_ref[k:k + 1, cols] * buf[pl.ds(r0 + (kw - 1) - k, FFN_ROWS), cols]
            o_ref[r0:r0 + FFN_ROWS, cols] = acc.astype(BF16)

    return pl.pallas_call(
        body, name=name, grid=(dims.batch_local, nst),
        in_specs=[*_seq_specs(dims, ts, f2, FFN_HALO, 0, nxt=True), pl.BlockSpec((FFN_HALO, f2), lambda b_, i: (0, 0))],
        out_specs=pl.BlockSpec((ts, f2), lambda b_, i: (b_ * nst + i, 0)),
        out_shape=jax.ShapeDtypeStruct((t, f2), BF16),
        scratch_shapes=[pltpu.VMEM((ts + FFN_HALO, f2), F32)],
        compiler_params=_params("parallel", "parallel"),
    )(du, du, w)


def _alibi_slope(h, n_heads):
    return 2.0 ** (-8.0 * (h + 1) / n_heads)


def _dot_nt(a, b):
    return lax.dot_general(a, b, (((1,), (1,)), ((), ())), preferred_element_type=F32)


def _dot_tn(a, b):
    return lax.dot_general(a, b, (((0,), (0,)), ((), ())), preferred_element_type=F32)


def _attn_view(x, dims, dil):
    return x.reshape(dims.batch_local, dims.seq // dil, dil * x.shape[-1])


def _attn_fwd_group(q, k, v, state, dims, dil, *, last, name):
    t, a = q.shape
    assert 2 * dims.head_dim == 128 and dims.n_heads % 2 == 0
    blk, hd = ATTN_BLOCK, dims.head_dim
    nb = dims.seq // dil // blk
    has_prev = nb > 1
    nkeys = 2 * blk if has_prev else blk

    def body(*refs):
        it = iter(refs)
        q_ref, kc_ref, vc_ref = next(it), next(it), next(it)
        kp_ref, vp_ref = (next(it), next(it)) if has_prev else (None, None)
        m_in, l_in, acc_in = (next(it), next(it), next(it)) if state is not None else (None, None, None)
        outs = list(it)
        iq = lax.broadcasted_iota(jnp.int32, (blk, nkeys), 0)
        jk = lax.broadcasted_iota(jnp.int32, (blk, nkeys), 1)
        if has_prev:
            steps = iq + blk - jk
            valid = (steps >= 0) & (steps <= blk) & ((jk >= blk) | (pl.program_id(2) > 0))
        else:
            steps = iq - jk
            valid = steps >= 0
        dist = steps.astype(F32) * float(dil)
        low = lax.broadcasted_iota(jnp.int32, (blk, 2 * hd), 1) < hd
        for hp in range(dims.n_heads // 2):
            sl = slice(2 * hd * hp, 2 * hd * (hp + 1))
            q2 = q_ref[:, sl]
            if has_prev:
                kcat = jnp.concatenate([kp_ref[:, sl], kc_ref[:, sl]], axis=0)
                vcat = jnp.concatenate([vp_ref[:, sl], vc_ref[:, sl]], axis=0)
            else:
                kcat, vcat = kc_ref[:, sl], vc_ref[:, sl]
            halves = []
            for half in range(2):
                col = 2 * hd * hp + hd * half
                qh = jnp.where(low if half == 0 else jnp.logical_not(low), q2, jnp.zeros_like(q2))
                sc = _dot_nt(qh, kcat) - _alibi_slope(2 * hp + half, dims.n_heads) * dist
                sc = jnp.where(valid, sc, MASKED_SCORE)
                row_max = jnp.max(sc, axis=-1, keepdims=True)
                if state is None:
                    m_new = row_max
                    p = jnp.exp(sc - m_new)
                    alpha = None
                    l_new = jnp.sum(p, axis=-1, keepdims=True)
                else:
                    m_old = m_in[:, col:col + 1]
                    m_new = jnp.maximum(m_old, row_max)
                    p = jnp.exp(sc - m_new)
                    alpha = jnp.exp(m_old - m_new)
                    l_new = alpha * l_in[:, col:col + 1] + jnp.sum(p, axis=-1, keepdims=True)
                pv = jnp.dot(p.astype(BF16), vcat, preferred_element_type=F32)
                halves.append((m_new, l_new, alpha, pv))
            (m_a, l_a, al_a, pv_a), (m_b, l_b, al_b, pv_b) = halves
            if state is None:
                acc = jnp.where(low, pv_a, pv_b)
            else:
                old = acc_in[:, sl]
                acc = jnp.where(low, al_a * old + pv_a, al_b * old + pv_b)
            m2 = jnp.where(low, m_a, m_b)
            l2 = jnp.where(low, l_a, l_b)
            if last:
                outs[0][:, sl] = (acc / l2).astype(BF16)
                outs[1][:, sl] = m2 + jnp.log(l2)
            else:
                outs[0][:, sl] = m2
                outs[1][:, sl] = l2
                outs[2][:, sl] = acc

    cur = pl.BlockSpec((None, blk, a), lambda b, r, i: (b, i, r))
    prev = pl.BlockSpec((None, blk, a), lambda b, r, i: (b, jnp.maximum(i - 1, 0), r))
    args, in_specs = [q, k, v], [cur, cur, cur]
    if has_prev:
        args += [k, v]
        in_specs += [prev, prev]
    if state is not None:
        args += list(state)
        in_specs += [cur] * 3
    shape = lambda dt: jax.ShapeDtypeStruct((dims.batch_local, dims.seq // dil, dil * a), dt)
    out_shape = [shape(BF16), shape(F32)] if last else [shape(F32)] * 3
    outs = pl.pallas_call(
        body, name=name, grid=(dims.batch_local, dil, nb),
        in_specs=in_specs, out_specs=[cur] * len(out_shape), out_shape=out_shape,
        compiler_params=_params("parallel", "parallel", "parallel"),
    )(*[_attn_view(x, dims, dil) for x in args])
    return tuple(o.reshape(t, a) for o in outs)


def _attn_delta(do, o, head_ones, dims, *, name, tr=512):
    t, a = o.shape
    tr = _pick(t, tr, 8)

    def body(do_ref, o_ref, e_ref, d_ref):
        prod = do_ref[...].astype(F32) * o_ref[...].astype(F32)
        d_ref[...] = _head_mean(prod, e_ref, dims.head_dim) * float(dims.head_dim)

    return pl.pallas_call(
        body, name=name, grid=(t // tr,),
        in_specs=[_row_spec(tr, a), _row_spec(tr, a), pl.BlockSpec((a, a), lambda i: (0, 0))],
        out_specs=_row_spec(tr, a), out_shape=jax.ShapeDtypeStruct((t, a), F32),
        compiler_params=_params("parallel"),
    )(do, o, head_ones)


def _attn_bwd_group(q, k, v, do, lse, delta, dims, dil, *, name):
    t, a = q.shape
    blk, hd = ATTN_BLOCK, dims.head_dim
    nb = dims.seq // dil // blk
    has_next = nb > 1

    def body(*refs):
        k_ref, v_ref, q_ref, do_ref, lse_ref, dl_ref = refs[:6]
        if has_next:
            qn_ref, don_ref, lsen_ref, dln_ref = refs[6:10]
            dq_ref, dk_ref, dv_ref, carry = refs[10:]
        else:
            dq_ref, dk_ref, dv_ref = refs[6:]
        j = pl.program_id(2)
        iq = lax.broadcasted_iota(jnp.int32, (blk, blk), 0)
        jk = lax.broadcasted_iota(jnp.int32, (blk, blk), 1)
        low = lax.broadcasted_iota(jnp.int32, (blk, 2 * hd), 1) < hd

        def pair(hp, qr, dor, lser, dlr, steps, valid):
            sl = slice(2 * hd * hp, 2 * hd * (hp + 1))
            q2, do2, k2, v2 = qr[:, sl], dor[:, sl], k_ref[:, sl], v_ref[:, sl]
            dist = steps.astype(F32) * float(dil)
            dq_h, dk2, dv2 = [], None, None
            for half in range(2):
                col = 2 * hd * hp + hd * half
                mask = low if half == 0 else jnp.logical_not(low)
                qh = jnp.where(mask, q2, jnp.zeros_like(q2))
                doh = jnp.where(mask, do2, jnp.zeros_like(do2))
                sc = _dot_nt(qh, k2) - _alibi_slope(2 * hp + half, dims.n_heads) * dist
                p = jnp.where(valid, jnp.exp(sc - lser[:, col:col + 1]), 0.0)
                ds = p * (_dot_nt(doh, v2) - dlr[:, col:col + 1])
                ds_b, p_b = ds.astype(BF16), p.astype(BF16)
                dq_h.append(jnp.dot(ds_b, k2, preferred_element_type=F32))
                dk_h, dv_h = _dot_tn(ds_b, qh), _dot_tn(p_b, doh)
                dk2 = dk_h if dk2 is None else dk2 + dk_h
                dv2 = dv_h if dv2 is None else dv2 + dv_h
            return sl, jnp.where(low, dq_h[0], dq_h[1]), dk2, dv2

        if has_next:
            @pl.when(j == 0)
            def _():
                carry[...] = jnp.zeros_like(carry)

        for hp in range(dims.n_heads // 2):
            sl, dq2, dk2, dv2 = pair(hp, q_ref, do_ref, lse_ref, dl_ref, iq - jk, iq >= jk)
            dq_ref[:, sl] = (carry[:, sl] + dq2) if has_next else dq2
            dk_ref[:, sl] = dk2
            dv_ref[:, sl] = dv2

        if has_next:
            @pl.when(j + 1 < nb)
            def _():
                for hp in range(dims.n_heads // 2):
                    sl, dq2, dk2, dv2 = pair(hp, qn_ref, don_ref, lsen_ref, dln_ref, iq - jk + blk, jk >= iq)
                    carry[:, sl] = dq2
                    dk_ref[:, sl] += dk2
                    dv_ref[:, sl] += dv2

    cur = pl.BlockSpec((None, blk, a), lambda b, r, j: (b, j, r))
    nxt = pl.BlockSpec((None, blk, a), lambda b, r, j: (b, jnp.minimum(j + 1, nb - 1), r))
    args, in_specs = [k, v, q, do, lse, delta], [cur] * 6
    if has_next:
        args += [q, do, lse, delta]
        in_specs += [nxt] * 4
    shape = jax.ShapeDtypeStruct((dims.batch_local, dims.seq // dil, dil * a), F32)
    outs = pl.pallas_call(
        body, name=name, grid=(dims.batch_local, dil, nb),
        in_specs=in_specs, out_specs=[cur] * 3, out_shape=[shape] * 3,
        scratch_shapes=[pltpu.VMEM((blk, a), F32)] if has_next else [],
        compiler_params=_params("parallel", "parallel", "arbitrary"),
    )(*[_attn_view(x, dims, dil) for x in args])
    return tuple(o.reshape(t, a) for o in outs)


LANES = 128
MASK_BIAS = 1e30
RESIDUE_DILATIONS = tuple(d for d in DILATIONS if d > 1)


def _rows_to_residues(value, out_ref, scr, d):
    rows, width = value.shape
    for c in range(width // LANES):
        cols = slice(LANES * c, LANES * (c + 1))
        scr[c] = value[:, cols]
        for r in range(d):
            out_ref[r, :, cols] = scr[c, pl.ds(r, rows // d, stride=d), :].astype(out_ref.dtype)


def _residues_to_rows(in_ref, scr, d):
    _, n, width = in_ref.shape
    slabs = []
    for c in range(width // LANES):
        cols = slice(LANES * c, LANES * (c + 1))
        for r in range(d):
            scr[c, pl.ds(r, n, stride=d), :] = in_ref[r, :, cols].astype(F32)
        slabs.append(scr[c])
    return slabs[0] if len(slabs) == 1 else jnp.concatenate(slabs, axis=1)


def _residue_shape(dims, d, width, dtype):
    return jax.ShapeDtypeStruct((dims.batch_local, d, dims.seq // d, width), dtype)


def _residue_spec(dims, d, tr, width):
    tiles = dims.seq // tr
    return pl.BlockSpec((None, d, tr // d, width), lambda i: (i // tiles, 0, i % tiles, 0))


def _head_sum_matrix(dims):
    a = dims.n_heads * dims.head_dim
    head = jnp.arange(a, dtype=jnp.int32) // dims.head_dim
    return (head[:, None] == jnp.arange(LANES, dtype=jnp.int32)[None, :]).astype(BF16)


def _two_pass_dot(v, m):
    hi = v.astype(BF16)
    lo = (v - hi.astype(F32)).astype(BF16)
    return jnp.dot(hi, m, preferred_element_type=F32) + jnp.dot(lo, m, preferred_element_type=F32)


def _qkv_layouts_fwd(z, gq, gk, head_ones, dims, *, name, tr=256):
    t = z.shape[0]
    a = dims.n_heads * dims.head_dim
    q_scale = dims.head_dim ** -0.5
    nres = len(RESIDUE_DILATIONS)

    def body(q_ref, k_ref, v_ref, gq_ref, gk_ref, sum_ref, spread_ref, *rest):
        outs, scr = rest[:-1], rest[-1]
        qv, kv = q_ref[...], k_ref[...]
        mean = lambda val: _two_pass_dot(_two_pass_dot(val, sum_ref[...]), spread_ref[...]) * (1.0 / dims.head_dim)
        rq = lax.rsqrt(mean(qv * qv) + RMS_EPS)
        rk = lax.rsqrt(mean(kv * kv) + RMS_EPS)
        values = (qv * rq * gq_ref[...] * q_scale, kv * rk * gk_ref[...], v_ref[...])
        for j, val in enumerate(values):
            outs[j][...] = val.astype(BF16)
            for g, d in enumerate(RESIDUE_DILATIONS):
                _rows_to_residues(val, outs[3 * (g + 1) + j], scr, d)

    out_specs = [_row_spec(tr, a)] * 3
    out_shape = [jax.ShapeDtypeStruct((t, a), BF16)] * 3
    for d in RESIDUE_DILATIONS:
        out_specs += [_residue_spec(dims, d, tr, a)] * 3
        out_shape += [_residue_shape(dims, d, a, BF16)] * 3
    outs = pl.pallas_call(
        body, name=name, grid=(t // tr,),
        in_specs=[_row_spec(tr, a, 2), _row_spec(tr, a, 3), _row_spec(tr, a, 4), _vec_spec(a), _vec_spec(a),
                  pl.BlockSpec((a, LANES), lambda i: (0, 0)), pl.BlockSpec((LANES, a), lambda i: (0, 0))],
        out_specs=out_specs, out_shape=out_shape,
        scratch_shapes=[pltpu.VMEM((a // LANES, tr, LANES), F32)],
        compiler_params=_params("parallel"),
    )(z, z, z, gq, gk, *head_ones)
    return {d: tuple(outs[3 * g:3 * g + 3]) for g, d in enumerate((1,) + RESIDUE_DILATIONS)}


def _attn_specs(dims, dil, width):
    blk = ATTN_BLOCK
    nb = dims.seq // dil // blk
    if dil == 1:
        grid = (dims.batch_local, nb)
        at = lambda f: pl.BlockSpec((blk, width), lambda b, i: (b * nb + f(i), 0))
    else:
        grid = (dims.batch_local, dil, nb)
        at = lambda f: pl.BlockSpec((None, None, blk, width), lambda b, r, i: (b, r, f(i), 0))
    return grid, at(lambda i: i), at(lambda i: jnp.maximum(i - 1, 0)), at(lambda i: jnp.minimum(i + 1, nb - 1))


def _head_slopes(n_heads):
    h = lax.broadcasted_iota(jnp.int32, (n_heads, 1, 1), 0).astype(F32)
    return jnp.exp((h + 1.0) * (-8.0 / n_heads * math.log(2.0)))


def _pair_masks(hd):
    low = lax.broadcasted_iota(jnp.int32, (1, 2 * hd), 1) < hd
    return low, jnp.logical_not(low)


def _attn_fwd(q, k, v, dims, dil, *, name):
    a = dims.n_heads * dims.head_dim
    heads, hd, blk = dims.n_heads, dims.head_dim, ATTN_BLOCK
    assert 2 * hd == LANES and heads % 2 == 0 and heads <= LANES
    nb = dims.seq // dil // blk
    has_prev = nb > 1
    nkeys = 2 * blk if has_prev else blk
    grid, cur, prev, _ = _attn_specs(dims, dil, a)
    _, cur_stat, _, _ = _attn_specs(dims, dil, LANES)

    def body(*refs):
        if has_prev:
            q_ref, kc_ref, vc_ref, kp_ref, vp_ref, o_ref, lse_ref, s_scr, p_scr = refs
        else:
            q_ref, kc_ref, vc_ref, o_ref, lse_ref, s_scr, p_scr = refs
        low, high = _pair_masks(hd)

        def keys(cur_ref, prev_ref, sl):
            return jnp.concatenate([prev_ref[:, sl], cur_ref[:, sl]], axis=0) if has_prev else cur_ref[:, sl]

        for hp in range(heads // 2):
            sl = slice(LANES * hp, LANES * (hp + 1))
            q2 = q_ref[:, sl]
            kcat = keys(kc_ref, kp_ref if has_prev else None, sl)
            s_scr[2 * hp] = _dot_nt(jnp.where(low, q2, jnp.zeros_like(q2)), kcat)
            s_scr[2 * hp + 1] = _dot_nt(jnp.where(high, q2, jnp.zeros_like(q2)), kcat)

        iq = lax.broadcasted_iota(jnp.int32, (blk, nkeys), 0)
        jk = lax.broadcasted_iota(jnp.int32, (blk, nkeys), 1)
        if has_prev:
            steps = iq + blk - jk
            valid = (steps >= 0) & (steps <= blk) & ((jk >= blk) | (pl.program_id(len(grid) - 1) > 0))
        else:
            steps = iq - jk
            valid = steps >= 0
        bias = jnp.where(valid, steps.astype(F32) * (-float(dil)), -MASK_BIAS)
        s = s_scr[...] + _head_slopes(heads) * bias[None]
        m = jnp.max(s, axis=-1, keepdims=True)
        p = jnp.exp(s - m)
        l = jnp.sum(p, axis=-1, keepdims=True)
        p_scr[...] = p.astype(BF16)
        inv = 1.0 / l
        lse = m + jnp.log(l)

        lane = lax.broadcasted_iota(jnp.int32, (blk, LANES), 1)
        stat = jnp.zeros((blk, LANES), F32)
        for hp in range(heads // 2):
            sl = slice(LANES * hp, LANES * (hp + 1))
            vcat = keys(vc_ref, vp_ref if has_prev else None, sl)
            pv_a = jnp.dot(p_scr[2 * hp], vcat, preferred_element_type=F32) * inv[2 * hp]
            pv_b = jnp.dot(p_scr[2 * hp + 1], vcat, preferred_element_type=F32) * inv[2 * hp + 1]
            o_ref[:, sl] = jnp.where(low, pv_a, pv_b)
            stat = jnp.where(lane == 2 * hp, lse[2 * hp], stat)
            stat = jnp.where(lane == 2 * hp + 1, lse[2 * hp + 1], stat)
        lse_ref[...] = stat

    lead = q.shape[:-2]
    rows = q.shape[-2]
    o, lse = pl.pallas_call(
        body, name=name, grid=grid,
        in_specs=[cur, cur, cur] + ([prev, prev] if has_prev else []),
        out_specs=[cur, cur_stat],
        out_shape=[jax.ShapeDtypeStruct(lead + (rows, a), F32), jax.ShapeDtypeStruct(lead + (rows, LANES), F32)],
        scratch_shapes=[pltpu.VMEM((heads, blk, nkeys), F32), pltpu.VMEM((heads, blk, nkeys), BF16)],
        compiler_params=_params(*["parallel"] * len(grid)),
    )(q, k, v, *([k, v] if has_prev else []))
    return o, lse


def _attn_combine(groups, head_spread, dims, *, name, tr=256):
    t = dims.tokens
    a = dims.n_heads * dims.head_dim
    dils = tuple(groups)

    def body(*refs):
        ins = refs[:2 * len(dils)]
        x_ref = refs[2 * len(dils)]
        o_ref = refs[2 * len(dils) + 1]
        lse_refs = refs[2 * len(dils) + 2:-2]
        scr, scr_stat = refs[-2], refs[-1]
        outs, stats = [], []
        for g, d in enumerate(dils):
            if d == 1:
                outs.append(ins[2 * g][...])
                stats.append(ins[2 * g + 1][...])
            else:
                outs.append(_residues_to_rows(ins[2 * g], scr, d))
                stats.append(_residues_to_rows(ins[2 * g + 1], scr_stat, d))
        top = functools.reduce(jnp.maximum, stats)
        weights = [jnp.exp(s - top) for s in stats]
        total = functools.reduce(jnp.add, weights)
        joint = top + jnp.log(total)
        inv = 1.0 / total
        acc = None
        for w, o in zip(weights, outs):
            term = _two_pass_dot(w * inv, x_ref[...]) * o
            acc = term if acc is None else acc + term
        o_ref[...] = acc.astype(BF16)
        for g, d in enumerate(dils):
            if d == 1:
                lse_refs[g][...] = joint
            else:
                _rows_to_residues(joint, lse_refs[g], scr_stat, d)

    in_specs, args, lse_specs, lse_shapes = [], [], [], []
    for d in dils:
        if d == 1:
            in_specs += [_row_spec(tr, a), _row_spec(tr, LANES)]
            lse_specs.append(_row_spec(tr, LANES))
            lse_shapes.append(jax.ShapeDtypeStruct((t, LANES), F32))
        else:
            in_specs += [_residue_spec(dims, d, tr, a), _residue_spec(dims, d, tr, LANES)]
            lse_specs.append(_residue_spec(dims, d, tr, LANES))
            lse_shapes.append(_residue_shape(dims, d, LANES, F32))
        args += list(groups[d])
    outs = pl.pallas_call(
        body, name=name, grid=(t // tr,),
        in_specs=in_specs + [pl.BlockSpec((LANES, a), lambda i: (0, 0))],
        out_specs=[_row_spec(tr, a)] + lse_specs,
        out_shape=[jax.ShapeDtypeStruct((t, a), BF16)] + lse_shapes,
        scratch_shapes=[pltpu.VMEM((a // LANES, tr, LANES), F32), pltpu.VMEM((1, tr, LANES), F32)],
        compiler_params=_params("parallel"),
    )(*args, head_spread)
    return outs[0], dict(zip(dils, outs[1:]))


def _attn_bwd_prep(do, o, head_sum, dims, *, name, tr=256):
    t, a = o.shape

    def body(do_ref, o_ref, e_ref, *rest):
        outs, scr, scr_stat = rest[:-2], rest[-2], rest[-1]
        dov = do_ref[...].astype(F32)
        delta = _two_pass_dot(dov * o_ref[...].astype(F32), e_ref[...])
        outs[0][...] = delta
        for g, d in enumerate(RESIDUE_DILATIONS):
            _rows_to_residues(dov, outs[1 + 2 * g], scr, d)
            _rows_to_residues(delta, outs[2 + 2 * g], scr_stat, d)

    out_specs, out_shape = [_row_spec(tr, LANES)], [jax.ShapeDtypeStruct((t, LANES), F32)]
    for d in RESIDUE_DILATIONS:
        out_specs += [_residue_spec(dims, d, tr, a), _residue_spec(dims, d, tr, LANES)]
        out_shape += [_residue_shape(dims, d, a, BF16), _residue_shape(dims, d, LANES, F32)]
    outs = pl.pallas_call(
        body, name=name, grid=(t // tr,),
        in_specs=[_row_spec(tr, a), _row_spec(tr, a), pl.BlockSpec((a, LANES), lambda i: (0, 0))],
        out_specs=out_specs, out_shape=out_shape,
        scratch_shapes=[pltpu.VMEM((a // LANES, tr, LANES), F32), pltpu.VMEM((1, tr, LANES), F32)],
        compiler_params=_params("parallel"),
    )(do, o, head_sum)
    dos, deltas = {1: do}, {1: outs[0]}
    for g, d in enumerate(RESIDUE_DILATIONS):
        dos[d], deltas[d] = outs[1 + 2 * g], outs[2 + 2 * g]
    return dos, deltas


def _attn_bwd(q, k, v, do, lse, delta, dims, dil, *, name):
    a = dims.n_heads * dims.head_dim
    heads, hd, blk = dims.n_heads, dims.head_dim, ATTN_BLOCK
    nb = dims.seq // dil // blk
    has_next = nb > 1
    nq = 2 * blk if has_next else blk
    grid, cur, _, nxt = _attn_specs(dims, dil, a)
    _, cur_stat, _, nxt_stat = _attn_specs(dims, dil, LANES)

    def body(*refs):
        k_ref, v_ref, q_ref, do_ref, lse_ref, dl_ref = refs[:6]
        if has_next:
            qn_ref, don_ref, lsen_ref, dln_ref = refs[6:10]
            dq_ref, dk_ref, dv_ref, s_scr, dp_scr, p_scr, ds_scr, carry = refs[10:]
        else:
            dq_ref, dk_ref, dv_ref, s_scr, dp_scr, p_scr, ds_scr = refs[6:]
        j = pl.program_id(len(grid) - 1)
        low, high = _pair_masks(hd)

        def stacked(ref, nref, sl):
            return jnp.concatenate([ref[:, sl], nref[:, sl]], axis=0) if has_next else ref[:, sl]

        def halves(x):
            return jnp.where(low, x, jnp.zeros_like(x)), jnp.where(high, x, jnp.zeros_like(x))

        for hp in range(heads // 2):
            sl = slice(LANES * hp, LANES * (hp + 1))
            k2, v2 = k_ref[:, sl], v_ref[:, sl]
            q_a, q_b = halves(stacked(q_ref, qn_ref if has_next else None, sl))
            do_a, do_b = halves(stacked(do_ref, don_ref if has_next else None, sl))
            s_scr[2 * hp], s_scr[2 * hp + 1] = _dot_nt(q_a, k2), _dot_nt(q_b, k2)
            dp_scr[2 * hp], dp_scr[2 * hp + 1] = _dot_nt(do_a, v2), _dot_nt(do_b, v2)

        rq = lax.broadcasted_iota(jnp.int32, (nq, blk), 0)
        jk = lax.broadcasted_iota(jnp.int32, (nq, blk), 1)
        if has_next:
            iq = jnp.where(rq < blk, rq, rq - blk)
            steps = jnp.where(rq < blk, iq - jk, iq - jk + blk)
            valid = ((rq < blk) & (iq >= jk)) | ((rq >= blk) & (jk >= iq) & (j + 1 < nb))
        else:
            steps, valid = rq - jk, rq >= jk
        bias = jnp.where(valid, steps.astype(F32) * (-float(dil)), -MASK_BIAS)
        lse_all = stacked(lse_ref, lsen_ref if has_next else None, slice(None))
        dl_all = stacked(dl_ref, dln_ref if has_next else None, slice(None))
        lse3 = jnp.stack([lse_all[:, h:h + 1] for h in range(heads)])
        dl3 = jnp.stack([dl_all[:, h:h + 1] for h in range(heads)])
        p = jnp.exp(s_scr[...] + _head_slopes(heads) * bias[None] - lse3)
        p_scr[...] = p.astype(BF16)
        ds_scr[...] = (p * (dp_scr[...] - dl3)).astype(BF16)

        if has_next:
            @pl.when(j == 0)
            def _():
                carry[...] = jnp.zeros_like(carry)

        for hp in range(heads // 2):
            sl = slice(LANES * hp, LANES * (hp + 1))
            k2 = k_ref[:, sl]
            q_a, q_b = halves(stacked(q_ref, qn_ref if has_next else None, sl))
            do_a, do_b = halves(stacked(do_ref, don_ref if has_next else None, sl))
            ds_a, ds_b = ds_scr[2 * hp], ds_scr[2 * hp + 1]
            dq2 = jnp.where(low, jnp.dot(ds_a, k2, preferred_element_type=F32),
                            jnp.dot(ds_b, k2, preferred_element_type=F32))
            dk_ref[:, sl] = _dot_tn(ds_a, q_a) + _dot_tn(ds_b, q_b)
            dv_ref[:, sl] = _dot_tn(p_scr[2 * hp], do_a) + _dot_tn(p_scr[2 * hp + 1], do_b)
            if has_next:
                dq_ref[:, sl] = carry[:, sl] + dq2[:blk]
                carry[:, sl] = dq2[blk:]
            else:
                dq_ref[:, sl] = dq2

    args, in_specs = [k, v, q, do, lse, delta], [cur] * 4 + [cur_stat] * 2
    if has_next:
        args += [q, do, lse, delta]
        in_specs += [nxt] * 2 + [nxt_stat] * 2
    shape = jax.ShapeDtypeStruct(q.shape, F32)
    scratch = [pltpu.VMEM((heads, nq, blk), F32)] * 2 + [pltpu.VMEM((heads, nq, blk), BF16)] * 2
    if has_next:
        scratch.append(pltpu.VMEM((blk, a), F32))
    return pl.pallas_call(
        body, name=name, grid=grid, in_specs=in_specs, out_specs=[cur] * 3, out_shape=[shape] * 3,
        scratch_shapes=scratch,
        compiler_params=_params(*["parallel"] * (len(grid) - 1), "arbitrary"),
    )(*args)


def _qkv_layouts_bwd(z, grads, gq, gk, head_ones, dims, *, name, tr=256):
    t = z.shape[0]
    a = dims.n_heads * dims.head_dim
    q_scale = dims.head_dim ** -0.5
    dils = tuple(grads)

    def body(q_ref, k_ref, *rest):
        d_refs = rest[:3 * len(dils)]
        gq_ref, gk_ref, sum_ref, spread_ref, dz_ref, dgq_ref, dgk_ref, scr = rest[3 * len(dils):]
        first = pl.program_id(0) == 0
        mean = lambda val: _two_pass_dot(_two_pass_dot(val, sum_ref[...]), spread_ref[...]) * (1.0 / dims.head_dim)

        def total(j):
            acc = None
            for g, d in enumerate(dils):
                ref = d_refs[3 * g + j]
                part = ref[...] if d == 1 else _residues_to_rows(ref, scr, d)
                acc = part if acc is None else acc + part
            return acc

        def norm_bwd(x_ref, dy, g_ref, scale, col, dg_ref):
            xv = x_ref[...]
            dy = dy * scale
            r = lax.rsqrt(mean(xv * xv) + RMS_EPS)
            gy = dy * g_ref[...]
            dx = r * gy - xv * (r * r * r) * mean(xv * gy)
            dz_ref[:, col * a:(col + 1) * a] = dx.astype(BF16)
            _accumulate(dg_ref, jnp.sum(dy * xv * r, axis=0, keepdims=True), first)

        norm_bwd(q_ref, total(0), gq_ref, q_scale, 0, dgq_ref)
        norm_bwd(k_ref, total(1), gk_ref, 1.0, 1, dgk_ref)
        dz_ref[:, 2 * a:3 * a] = total(2).astype(BF16)

    in_specs, args = [_row_spec(tr, a, 2), _row_spec(tr, a, 3)], [z, z]
    for d in dils:
        in_specs += [_row_spec(tr, a) if d == 1 else _residue_spec(dims, d, tr, a)] * 3
        args += list(grads[d])
    in_specs += [_vec_spec(a), _vec_spec(a), pl.BlockSpec((a, LANES), lambda i: (0, 0)),
                 pl.BlockSpec((LANES, a), lambda i: (0, 0))]
    return pl.pallas_call(
        body, name=name, grid=(t // tr,), in_specs=in_specs,
        out_specs=[_row_spec(tr, 3 * a), _vec_spec(a), _vec_spec(a)],
        out_shape=[jax.ShapeDtypeStruct((t, 3 * a), BF16)] + [jax.ShapeDtypeStruct((1, a), F32)] * 2,
        scratch_shapes=[pltpu.VMEM((a // LANES, tr, LANES), F32)],
        compiler_params=_params("arbitrary"),
    )(*args, gq, gk, *head_ones)


def _mix_fwd(ya, yb, z, gate_b, dims, *, name, tr=512):
    t, d = ya.shape
    tr = _pick(t, tr, 8)
    first_gate_col = z.shape[1] // d - 2

    def body(ya_ref, yb_ref, ga_ref, gb_ref, ba_ref, bb_ref, o_ref):
        g_a = _sigmoid(ga_ref[...] + ba_ref[...])
        g_b = _sigmoid(gb_ref[...] + bb_ref[...])
        o_ref[...] = (g_a * ya_ref[...] + g_b * yb_ref[...]).astype(BF16)

    return pl.pallas_call(
        body, name=name, grid=(t // tr,),
        in_specs=[_row_spec(tr, d), _row_spec(tr, d), _row_spec(tr, d, first_gate_col),
                  _row_spec(tr, d, first_gate_col + 1), _vec_spec(d, 0), _vec_spec(d, 1)],
        out_specs=_row_spec(tr, d), out_shape=jax.ShapeDtypeStruct((t, d), BF16),
        compiler_params=_params("parallel"),
    )(ya, yb, z, z, gate_b, gate_b)


def _mix_bwd(dmix, ya, yb, z, gate_b, dims, *, name, tr=512):
    t, d = ya.shape
    tr = _pick(t, tr, 8)
    first_gate_col = z.shape[1] // d - 2

    def body(dm_ref, ya_ref, yb_ref, ga_ref, gb_ref, ba_ref, bb_ref, dya_ref, dyb_ref, dz_ref, db_ref):
        dm = dm_ref[...].astype(F32)
        g_a = _sigmoid(ga_ref[...] + ba_ref[...])
        g_b = _sigmoid(gb_ref[...] + bb_ref[...])
        dya_ref[...] = (dm * g_a).astype(BF16)
        dyb_ref[...] = (dm * g_b).astype(BF16)
        dl_a = dm * ya_ref[...] * g_a * (1.0 - g_a)
        dl_b = dm * yb_ref[...] * g_b * (1.0 - g_b)
        dz_ref[:, 0:d] = dl_a.astype(BF16)
        dz_ref[:, d:2 * d] = dl_b.astype(BF16)
        first = pl.program_id(0) == 0
        sums = jnp.concatenate([jnp.sum(dl_a, axis=0, keepdims=True), jnp.sum(dl_b, axis=0, keepdims=True)], axis=1)
        _accumulate(db_ref, sums, first)

    return pl.pallas_call(
        body, name=name, grid=(t // tr,),
        in_specs=[_row_spec(tr, d), _row_spec(tr, d), _row_spec(tr, d), _row_spec(tr, d, first_gate_col),
                  _row_spec(tr, d, first_gate_col + 1), _vec_spec(d, 0), _vec_spec(d, 1)],
        out_specs=[_row_spec(tr, d), _row_spec(tr, d), _row_spec(tr, 2 * d), _vec_spec(2 * d)],
        out_shape=[jax.ShapeDtypeStruct((t, d), BF16)] * 2 + [jax.ShapeDtypeStruct((t, 2 * d), BF16),
                                                              jax.ShapeDtypeStruct((1, 2 * d), F32)],
        compiler_params=_params("arbitrary"),
    )(dmix, ya, yb, z, z, gate_b, gate_b)


def _loss_head(y, target, *, name, tr=512):
    t, d = y.shape
    tr = _pick(t, tr, 8)

    def body(y_ref, t_ref, dy_ref, dyb_ref, loss_ref):
        err = y_ref[...] - t_ref[...]
        dy = err * (1.0 / d)
        dy_ref[...] = dy
        dyb_ref[...] = dy.astype(BF16)
        part = jnp.sum(jnp.sum(err * err, axis=-1, keepdims=True), axis=0, keepdims=True) * (0.5 / d)
        _accumulate(loss_ref, jnp.broadcast_to(part, (8, 128)), pl.program_id(0) == 0)

    return pl.pallas_call(
        body, name=name, grid=(t // tr,),
        in_specs=[_row_spec(tr, d), _row_spec(tr, d)],
        out_specs=[_row_spec(tr, d), _row_spec(tr, d), pl.BlockSpec((8, 128), lambda i: (0, 0))],
        out_shape=[jax.ShapeDtypeStruct((t, d), F32), jax.ShapeDtypeStruct((t, d), BF16),
                   jax.ShapeDtypeStruct((8, 128), F32)],
        compiler_params=_params("arbitrary"),
    )(y, target)


def _adamw(w, grads, m, v, *, name, tr=256):
    r, c = w.shape
    tr = _pick(r, tr, 8)
    ng = len(grads)
    c1 = 1.0 - ADAM_B1 ** ADAM_STEP
    c2 = 1.0 - ADAM_B2 ** ADAM_STEP

    def body(*refs):
        w_ref, g_refs, m_ref, v_ref = refs[0], refs[1:1 + ng], refs[1 + ng], refs[2 + ng]
        g_out, d_out, m_out, v_out = refs[3 + ng:]
        g = g_refs[0][...]
        for extra in g_refs[1:]:
            g = g + extra[...]
        m_new = ADAM_B1 * m_ref[...] + (1.0 - ADAM_B1) * g
        v_new = ADAM_B2 * v_ref[...] + (1.0 - ADAM_B2) * (g * g)
        g_out[...] = g
        m_out[...] = m_new
        v_out[...] = v_new
        d_out[...] = -ADAM_LR * ((m_new / c1) / (jnp.sqrt(v_new / c2) + ADAM_EPS) + ADAM_WD * w_ref[...])

    spec = pl.BlockSpec((tr, c), lambda i: (i, 0))
    return pl.pallas_call(
        body, name=name, grid=(r // tr,),
        in_specs=[spec] * (3 + ng), out_specs=[spec] * 4, out_shape=[jax.ShapeDtypeStruct((r, c), F32)] * 4,
        compiler_params=_params("parallel"),
    )(w, *grads, m, v)


CHIP_PEERS = ((1, 0), (0, 1), (1, 1))


def _place():
    return lax.axis_index("x"), lax.axis_index("y"), lax.axis_index("c")


HBM = pl.BlockSpec(memory_space=pltpu.HBM)
SEM = pl.BlockSpec(memory_space=pltpu.SEMAPHORE)
IN_FLIGHT = pltpu.SideEffectType.DATAFLOW_SIDE_EFFECTING


def _in_hbm(a):
    return pltpu.with_memory_space_constraint(a, pltpu.HBM)


def _cast_to_lands(shards, dtypes, *, name):
    n = len(shards)

    def body(*refs):
        ins, outs, bufs, sems = refs[:n], refs[n:2 * n], refs[2 * n:3 * n], refs[3 * n]
        x, y, _ = _place()
        copies = []
        for a in range(n):
            bufs[a][...] = ins[a][...].astype(dtypes[a])
            cp = pltpu.make_async_copy(bufs[a], outs[a].at[2 * x + y], sems.at[a])
            cp.start()
            copies.append(cp)
        for cp in copies:
            cp.wait()

    return pl.pallas_call(
        body, name=name, in_specs=[pl.BlockSpec(memory_space=pltpu.VMEM)] * n, out_specs=[ANY] * n,
        out_shape=[jax.ShapeDtypeStruct((N_CHIPS,) + s.shape, dt) for s, dt in zip(shards, dtypes)],
        scratch_shapes=[pltpu.VMEM(s.shape, dt) for s, dt in zip(shards, dtypes)] + [pltpu.SemaphoreType.DMA((n,))],
        compiler_params=pltpu.CompilerParams(vmem_limit_bytes=V7X_VMEM_LIMIT_BYTES),
    )(*shards)


def _chip_copy(src, dst, send, recv, flip, place):
    x, y, c = place
    return pltpu.make_async_remote_copy(src_ref=src, dst_ref=dst, send_sem=send, recv_sem=recv,
                                        device_id=(x ^ flip[0], y ^ flip[1], c), device_id_type=MESH)


def _my_part(land, place, halved):
    block = land.at[2 * place[0] + place[1]]
    if not halved:
        return block
    rows = land.shape[1] // 2
    return block.at[pl.ds(pl.multiple_of(place[2] * rows, rows), rows)]


def _gather_start(lands, after, *, name, halved=()):
    n = len(lands)

    def body(*refs):
        ins, send, recv, token = refs[:n], refs[n + 1], refs[n + 2], refs[-1]
        place = _place()
        for a in range(n):
            part = _my_part(ins[a], place, a in halved)
            for p, flip in enumerate(CHIP_PEERS):
                k = 3 * a + p
                _chip_copy(part, part, send.at[k], recv.at[k], flip, place).start()
        token[...] = jnp.zeros_like(token)

    outs = pl.pallas_call(
        body, name=name, in_specs=[HBM] * n + [ANY],
        out_specs=(SEM, SEM, *[HBM] * n, pl.BlockSpec(memory_space=pltpu.VMEM)),
        out_shape=(pltpu.SemaphoreType.DMA((3 * n,)), pltpu.SemaphoreType.DMA((3 * n,)),
                   *[pltpu.HBM(l.shape, l.dtype) for l in lands], jax.ShapeDtypeStruct((8, 128), F32)),
        input_output_aliases={a: 2 + a for a in range(n)},
        compiler_params=pltpu.CompilerParams(has_side_effects=IN_FLIGHT),
    )(*[_in_hbm(l) for l in lands], after)
    return outs[0], outs[1], list(outs[2:2 + n]), outs[-1]


def _gather_wait(send, recv, lands, after, *, name, halved=()):
    n = len(lands)

    def body(*refs):
        ins, send_ref, recv_ref = refs[:n], refs[n], refs[n + 1]
        place = _place()
        for a in range(n):
            part = _my_part(ins[a], place, a in halved)
            for p, flip in enumerate(CHIP_PEERS):
                k = 3 * a + p
                cp = _chip_copy(part, part, send_ref.at[k], recv_ref.at[k], flip, place)
                cp.wait_send()
                cp.wait_recv()

    return pl.pallas_call(
        body, name=name, in_specs=[HBM] * n + [SEM, SEM, ANY], out_specs=[HBM] * n,
        out_shape=[pltpu.HBM(l.shape, l.dtype) for l in lands],
        input_output_aliases={a: a for a in range(n)},
        compiler_params=pltpu.CompilerParams(has_side_effects=IN_FLIGHT),
    )(*lands, send, recv, after)


def _forward_to_sibling(land, *, name):
    rows = land.shape[1] // 2

    def body(land_ref, out_ref, send, recv):
        x, y, c = _place()
        copies = []
        for p, (fx, fy) in enumerate(CHIP_PEERS):
            chip = 2 * (x ^ fx) + (y ^ fy)
            mine = pl.ds(pl.multiple_of(c * rows, rows), rows)
            theirs = pl.ds(pl.multiple_of((1 - c) * rows, rows), rows)
            out = pltpu.make_async_remote_copy(
                src_ref=land_ref.at[chip].at[mine], dst_ref=out_ref.at[chip].at[mine], send_sem=send.at[p],
                recv_sem=recv.at[p], device_id=(x, y, 1 - c), device_id_type=MESH)
            out.start()
            copies.append((out, pltpu.make_async_remote_copy(
                src_ref=land_ref.at[chip].at[theirs], dst_ref=out_ref.at[chip].at[theirs], send_sem=send.at[p],
                recv_sem=recv.at[p], device_id=(x, y, 1 - c), device_id_type=MESH)))
        for out, arriving in copies:
            out.wait_send()
            arriving.wait_recv()

    return pl.pallas_call(
        body, name=name, in_specs=[ANY], out_specs=ANY, out_shape=jax.ShapeDtypeStruct(land.shape, land.dtype),
        input_output_aliases={0: 0},
        scratch_shapes=[pltpu.SemaphoreType.DMA((3,)), pltpu.SemaphoreType.DMA((3,))],
    )(land)


def _scatter_start(grad, *, name):
    def body(g_ref, land_ref, send, recv, g_thru, land_thru, token):
        place = _place()
        for p, flip in enumerate(CHIP_PEERS):
            peer_chip = 2 * (place[0] ^ flip[0]) + (place[1] ^ flip[1])
            _chip_copy(g_ref.at[peer_chip], land_ref.at[p], send.at[p], recv.at[p], flip, place).start()
        token[...] = jnp.zeros_like(token)

    land = lax.empty((3,) + grad.shape[1:], grad.dtype)
    return pl.pallas_call(
        body, name=name, in_specs=[HBM, HBM],
        out_specs=(SEM, SEM, HBM, HBM, pl.BlockSpec(memory_space=pltpu.VMEM)),
        out_shape=(pltpu.SemaphoreType.DMA((3,)), pltpu.SemaphoreType.DMA((3,)), pltpu.HBM(grad.shape, grad.dtype),
                   pltpu.HBM(land.shape, land.dtype), jax.ShapeDtypeStruct((8, 128), F32)),
        input_output_aliases={0: 2, 1: 3},
        compiler_params=pltpu.CompilerParams(has_side_effects=IN_FLIGHT),
    )(_in_hbm(grad), _in_hbm(land))


def _scatter_wait(started, after, *, name):
    n = len(started)

    def body(*refs):
        grads, lands = refs[:n], refs[n:2 * n]
        sends, recvs = refs[2 * n:3 * n], refs[3 * n:4 * n]
        place = _place()
        for a in range(n):
            for p, flip in enumerate(CHIP_PEERS):
                cp = _chip_copy(grads[a].at[0], lands[a].at[p], sends[a].at[p], recvs[a].at[p], flip, place)
                cp.wait_send()
                cp.wait_recv()

    grads, lands = [s[2] for s in started], [s[3] for s in started]
    outs = pl.pallas_call(
        body, name=name, in_specs=[HBM] * (2 * n) + [SEM] * (2 * n) + [ANY], out_specs=[HBM] * (2 * n),
        out_shape=[pltpu.HBM(a.shape, a.dtype) for a in grads + lands],
        input_output_aliases={a: a for a in range(2 * n)},
        compiler_params=pltpu.CompilerParams(has_side_effects=IN_FLIGHT),
    )(*grads, *lands, *[s[0] for s in started], *[s[1] for s in started], after)
    return list(zip(outs[:n], outs[n:]))


def _swap_sibling(arrays, *, name):
    n = len(arrays)

    def body(*refs):
        ins, outs = refs[:n], refs[n:2 * n]
        send, recv = refs[2 * n:]
        x, y, c = _place()
        started = []
        for a in range(n):
            rc = pltpu.make_async_remote_copy(
                src_ref=ins[a], dst_ref=outs[a], send_sem=send.at[a], recv_sem=recv.at[a],
                device_id=(x, y, 1 - c), device_id_type=MESH)
            rc.start()
            started.append(rc)
        for rc in started:
            rc.wait()

    return pl.pallas_call(
        body, name=name, in_specs=[ANY] * n, out_specs=[ANY] * n,
        out_shape=[jax.ShapeDtypeStruct(g.shape, g.dtype) for g in arrays],
        scratch_shapes=[pltpu.SemaphoreType.DMA((n,)), pltpu.SemaphoreType.DMA((n,))],
    )(*arrays)


def _sum_received(grad, land, *, name, tr=256):
    _, r, c = grad.shape
    tr = _pick(r, tr, 8)

    def body(chip_ref, g_ref, l_ref, o_ref):
        o_ref[...] = ((g_ref[...] + l_ref[0].astype(F32)) + l_ref[1].astype(F32)) + l_ref[2].astype(F32)

    chip = (2 * lax.axis_index("x") + lax.axis_index("y")).astype(jnp.int32).reshape(1)
    return pl.pallas_call(
        body, name=name,
        grid_spec=pltpu.PrefetchScalarGridSpec(
            num_scalar_prefetch=1, grid=(r // tr,),
            in_specs=[pl.BlockSpec((None, tr, c), lambda i, chip_ref: (chip_ref[0], i, 0)),
                      pl.BlockSpec((3, tr, c), lambda i, chip_ref: (0, i, 0))],
            out_specs=pl.BlockSpec((tr, c), lambda i, chip_ref: (i, 0))),
        out_shape=jax.ShapeDtypeStruct((r, c), F32), compiler_params=_params("parallel"),
    )(chip, grad, land)


def _allreduce_small(packed, *, name, after=None):
    r, d = packed.shape
    n_dev = 8

    def body(src_ref, out_ref, buf, send, recv):
        x, y, c = _place()
        me = 4 * x + 2 * y + c
        started = []
        for p in range(1, n_dev):
            rc = pltpu.make_async_remote_copy(
                src_ref=src_ref, dst_ref=buf.at[me], send_sem=send.at[p - 1], recv_sem=recv.at[p - 1],
                device_id=(x ^ (p >> 2), y ^ ((p >> 1) & 1), c ^ (p & 1)), device_id_type=MESH)
            rc.start()
            started.append(rc)
        buf[me] = src_ref[...]
        for rc in started:
            rc.wait()
        total = buf[0]
        for s in range(1, n_dev):
            total = total + buf[s]
        out_ref[...] = total

    vmem = pl.BlockSpec(memory_space=pltpu.VMEM)
    body, more_specs, more_args = _ordered(body, 1, after)
    return pl.pallas_call(
        body, name=name, in_specs=[vmem] + more_specs, out_specs=vmem, out_shape=jax.ShapeDtypeStruct((r, d), F32),
        scratch_shapes=[pltpu.VMEM((n_dev, r, d), F32), pltpu.SemaphoreType.DMA((n_dev - 1,)),
                        pltpu.SemaphoreType.DMA((n_dev - 1,))],
    )(packed, *more_args)


def _packed_rows(size, d):
    return -(-size // (8 * d)) * 8


def _pack_rows(arrays, d):
    rows = []
    for arr in arrays:
        flat = arr.reshape(-1).astype(F32)
        n = _packed_rows(flat.shape[0], d)
        rows.append(jnp.pad(flat, (0, n * d - flat.shape[0])).reshape(n, d))
    return jnp.concatenate(rows, axis=0)


def _unpack_rows(packed, shapes, d):
    out, row = [], 0
    for shape in shapes:
        size = math.prod(shape)
        n = _packed_rows(size, d)
        out.append(packed[row:row + n].reshape(-1)[:size].reshape(shape))
        row += n
    return out


SMALL = ("norm1_g", "gate_b", "conv_b", "conv_norm_g", "q_norm_g", "k_norm_g", "norm2_g", "ffn_conv_b")
LARGE = ("w_in", "w_conv_out", "w_attn_out", "w_out", "w_up", "w_down")
WEIGHTS = ("norm1_g", "w_in", "gate_b", "conv_w", "conv_b", "conv_norm_g", "w_conv_out", "q_norm_g", "k_norm_g",
           "w_attn_out", "w_out", "norm2_g", "w_up", "ffn_conv_w", "ffn_conv_b", "w_down")


def _head_ones(dims):
    a = dims.n_heads * dims.head_dim
    head = jnp.arange(a, dtype=jnp.int32) // dims.head_dim
    return (head[:, None] == head[None, :]).astype(BF16)


def _after(vec, token):
    return vec if token is None else vec + token[0:1, 0:1]


def _local_step(dims, x, target, small, first_weights, other_weights, send_grad):
    d, f, heads = dims.d_model, dims.d_ff, dims.n_heads
    small = dict(small)
    row = lambda name: small[name].reshape(1, -1)
    head_sum = _head_sum_matrix(dims)
    head_spread = jnp.transpose(head_sum)
    ones = (head_sum, head_spread)
    gq = jnp.tile(row("q_norm_g"), (1, heads))
    gk = jnp.tile(row("k_norm_g"), (1, heads))
    one_shard = lambda w: w.reshape(1, -1, w.shape[-1])

    h = _rmsnorm_fwd(x, row("norm1_g"), name="norm1")
    full = first_weights(h)
    w_in = full["w_in"]
    conv_w = jnp.pad(full["conv_w"], ((0, CONV_HALO - dims.conv_width), (0, 0)))
    ffn_w = jnp.pad(full["ffn_conv_w"], ((0, FFN_HALO - dims.ffn_conv_width), (0, 0)))
    z = _mm_nn(h, w_in, out_dtype=F32, after=full.get("token"), name="in_proj")
    a1, a3 = _conv_branch_fwd(z, conv_w, row("conv_b"), row("conv_norm_g"), dims, name="conv_branch")
    qkv = _qkv_layouts_fwd(z, gq, gk, ones, dims, name="qk_norm")
    per_group = {dil: _attn_fwd(*qkv[dil], dims, dil, name=f"attn_fwd_d{dil}") for dil in DILATIONS}
    o, lse = _attn_combine(per_group, head_spread, dims, name="attn_combine")
    full = other_weights(o)
    w_up = full["w_up"]
    w_co, w_ao, w_o, w_dn = (one_shard(full[k]) for k in ("w_conv_out", "w_attn_out", "w_out", "w_down"))
    ya = _mm_nn(a3, w_co, out_dtype=F32, name="conv_out_proj")
    yb = _mm_nn(o, w_ao, out_dtype=F32, name="attn_out_proj")
    mixed = _mix_fwd(ya, yb, z, row("gate_b"), dims, name="gate_mix")
    x1 = _mm_nn(mixed, w_o, out_dtype=F32, residual=x, name="out_proj")
    h2 = _rmsnorm_fwd(x1, row("norm2_g"), name="norm2")
    up = _mm_nn(h2, w_up, out_dtype=F32, name="up_proj")
    act = _ffn_act_fwd(up, ffn_w, row("ffn_conv_b"), dims, name="ffn_act")
    x2 = _mm_nn(act, w_dn, out_dtype=F32, residual=x1, name="down_proj")
    dy, dy_b, loss = _loss_head(x2, target, name="loss_head")

    grads = {}

    def large(name, g):
        grads[name], g_bf16 = g
        return send_grad(name, g_bf16)

    sent = large("w_down", _mm_tn(act, dy_b, n_shards=1, name="dw_down"))
    dact = _mm_nt(dy_b, w_dn, out_dtype=BF16, after=sent, name="d_act")
    du, dfw, dfb = _ffn_act_bwd(dact, up, ffn_w, row("ffn_conv_b"), dims, name="ffn_act_bwd")
    grads["ffn_conv_w"], grads["ffn_conv_b"] = dfw[:dims.ffn_conv_width], dfb
    dup = _ffn_conv_bwd(du, ffn_w, dims, name="ffn_conv_bwd")
    sent = large("w_up", _mm_tn(h2, dup, n_shards=N_CHIPS, name="dw_up"))
    dh2 = _mm_nt(dup, w_up, out_dtype=F32, after=sent, name="d_h2")
    dx1, dx1_b, grads["norm2_g"] = _rmsnorm_bwd(x1, row("norm2_g"), dh2, dy, want_bf16=True, name="norm2_bwd")
    sent = large("w_out", _mm_tn(mixed, dx1_b, n_shards=1, name="dw_out"))
    dmix = _mm_nt(dx1_b, w_o, out_dtype=F32, after=sent, name="d_mix")
    dya, dyb, dz_gate, grads["gate_b"] = _mix_bwd(dmix, ya, yb, z, row("gate_b"), dims, name="gate_mix_bwd")
    sent = large("w_conv_out", _mm_tn(a3, dya, n_shards=1, name="dw_conv_out"))
    da3 = _mm_nt(dya, w_co, out_dtype=F32, after=sent, name="d_conv_act")
    da1, grads["conv_norm_g"] = _conv_norm_bwd(da3, a1, row("conv_norm_g"), name="conv_norm_bwd")
    dz_glu, dcw, grads["conv_b"] = _conv_branch_bwd(da1, z, conv_w, dims, name="conv_branch_bwd")
    grads["conv_w"] = dcw[:dims.conv_width]
    sent = large("w_attn_out", _mm_tn(o, dyb, n_shards=1, name="dw_attn_out"))
    do = _mm_nt(dyb, w_ao, out_dtype=BF16, after=sent, name="d_attn")
    dos, deltas = _attn_bwd_prep(do, o, head_sum, dims, name="attn_bwd_prep")
    dqkv = {dil: _attn_bwd(*qkv[dil], dos[dil], lse[dil], deltas[dil], dims, dil, name=f"attn_bwd_d{dil}")
            for dil in DILATIONS}
    dz_qkv, dgq, dgk = _qkv_layouts_bwd(z, dqkv, gq, gk, ones, dims, name="qk_norm_bwd")
    grads["q_norm_g"] = dgq.reshape(heads, dims.head_dim).sum(axis=0)
    grads["k_norm_g"] = dgk.reshape(heads, dims.head_dim).sum(axis=0)
    dz = jnp.concatenate([dz_glu, dz_qkv, dz_gate], axis=1)
    sent = large("w_in", _mm_tn(h, dz, n_shards=N_CHIPS, name="dw_in"))
    dh = _mm_nt(dz, w_in, out_dtype=F32, after=sent, name="d_h")
    dx, grads["norm1_g"] = _rmsnorm_bwd(x, row("norm1_g"), dh, dx1, want_bf16=False, name="norm1_bwd")
    return loss, dx, grads


def _step(dims, x, target, w, m, v):
    d = dims.d_model
    t = dims.tokens
    sq = lambda a: a.reshape(a.shape[1:])
    w2, m2, v2 = ({k: sq(a) for k, a in grp.items()} for grp in (w, m, v))

    conv_pad = jnp.pad(w2["conv_w"], ((0, CONV_HALO - dims.conv_width), (0, 0)))
    ffn_pad = jnp.pad(w2["ffn_conv_w"], ((0, FFN_HALO - dims.ffn_conv_width), (0, 0)))
    gathered_names = LARGE + ("conv_w", "ffn_conv_w")
    lands = dict(zip(gathered_names, _cast_to_lands([w2[k] for k in LARGE] + [conv_pad, ffn_pad],
                                                   [BF16] * len(LARGE) + [F32, F32], name="cast_weights")))
    first_names = ("w_in", "conv_w", "ffn_conv_w")
    other_names = tuple(k for k in gathered_names if k not in first_names)
    first = _gather_start([lands[k] for k in first_names], x, halved=(0,), name="gather_start_first")
    other = []
    cols = lambda g, rows: jnp.moveaxis(g, 0, 1).reshape(g.shape[1], -1)[:rows]

    def first_weights(after):
        got = dict(zip(first_names, _gather_wait(*first[:3], after, halved=(0,), name="gather_wait_first")))
        got["w_in"] = _forward_to_sibling(got["w_in"], name="forward_w_in")
        other.extend(_gather_start([lands[k] for k in other_names], got["w_in"], name="gather_start_other"))
        got["conv_w"] = cols(got["conv_w"], dims.conv_width)
        got["ffn_conv_w"] = cols(got["ffn_conv_w"], dims.ffn_conv_width)
        got["token"] = other[3]
        return got

    def other_weights(after):
        return dict(zip(other_names, _gather_wait(*other[:3], after, name="gather_wait_other")))

    started = {}

    def send_grad(name, g):
        send, recv, g_thru, land, token = _scatter_start(g.reshape(N_CHIPS, -1, g.shape[-1]), name=f"scatter_start_{name}")
        started[name] = (send, recv, g_thru, land)
        return token

    small = {k: w2[k] for k in SMALL}
    small["norm1_g"] = _after(small["norm1_g"].reshape(1, -1), first[3])
    loss, dx, grads = _local_step(dims, x.reshape(t, d), target.reshape(t, d), small, first_weights, other_weights, send_grad)

    def finish(names, after, tag):
        arrived = _scatter_wait([started[k] for k in names], after, name=f"scatter_wait_{tag}")
        blocks = [grads[k].reshape(N_CHIPS, -1, grads[k].shape[-1]) for k in names]
        mine = [_sum_received(g, land, name=f"sum_{k}") for k, g, (_, land) in zip(names, blocks, arrived)]
        theirs = _swap_sibling(mine, name=f"swap_sibling_{tag}")
        return {k: _adamw(w2[k], [a, b], m2[k], v2[k], name=f"adamw_{k}") for k, a, b in zip(names, mine, theirs)}

    out = finish([k for k in LARGE if k != "w_in"], dx, "others")

    small_names = SMALL + ("conv_w", "ffn_conv_w")
    packed = _pack_rows([grads[k] for k in small_names] + [loss[0, 0]], d)
    reduced = _allreduce_small(packed, after=[upd[1] for upd in out.values()], name="allreduce_small")
    shapes = [grads[k].shape for k in small_names] + [()]
    *small_g, loss_total = _unpack_rows(reduced, shapes, d)
    small_g = dict(zip(small_names, small_g))
    chip = 2 * lax.axis_index("x") + lax.axis_index("y")
    for k in ("conv_w", "ffn_conv_w"):
        width = w2[k].shape[1]
        small_g[k] = lax.dynamic_slice_in_dim(small_g[k], chip * width, width, axis=1)

    small_shapes = [w2[k].shape for k in small_names]
    pack = lambda grp: _pack_rows([grp[k] for k in small_names], d)
    results = _adamw(pack(w2), [pack(small_g)], pack(m2), pack(v2), name="adamw_small")
    unpacked = [_unpack_rows(r, small_shapes, d) for r in results]
    for i, k in enumerate(small_names):
        out[k] = tuple(u[i] for u in unpacked)
    out.update(finish(["w_in"], results[1], "w_in"))

    lead =lambda a: a.reshape((1,) + a.shape)
    ordered = [[lead(out[k][j].reshape(w2[k].shape)) for k in WEIGHTS] for j in range(4)]
    return (loss_total, dx.reshape(x.shape), *ordered[0], *ordered[1], *ordered[2], *ordered[3])


def kernel(x, norm1_g, w_in, gate_b, conv_w, conv_b, conv_norm_g, w_conv_out, q_norm_g, k_norm_g, w_attn_out, w_out, norm2_g, w_up, ffn_conv_w, ffn_conv_b, w_down, loss_target, m_norm1_g, m_w_in, m_gate_b, m_conv_w, m_conv_b, m_conv_norm_g, m_w_conv_out, m_q_norm_g, m_k_norm_g, m_w_attn_out, m_w_out, m_norm2_g, m_w_up, m_ffn_conv_w, m_ffn_conv_b, m_w_down, v_norm1_g, v_w_in, v_gate_b, v_conv_w, v_conv_b, v_conv_norm_g, v_w_conv_out, v_q_norm_g, v_k_norm_g, v_w_attn_out, v_w_out, v_norm2_g, v_w_up, v_ffn_conv_w, v_ffn_conv_b, v_w_down):
    w = dict(zip(WEIGHTS, (norm1_g, w_in, gate_b, conv_w, conv_b, conv_norm_g, w_conv_out, q_norm_g, k_norm_g,
                           w_attn_out, w_out, norm2_g, w_up, ffn_conv_w, ffn_conv_b, w_down)))
    m = dict(zip(WEIGHTS, (m_norm1_g, m_w_in, m_gate_b, m_conv_w, m_conv_b, m_conv_norm_g, m_w_conv_out, m_q_norm_g,
                           m_k_norm_g, m_w_attn_out, m_w_out, m_norm2_g, m_w_up, m_ffn_conv_w, m_ffn_conv_b, m_w_down)))
    v = dict(zip(WEIGHTS, (v_norm1_g, v_w_in, v_gate_b, v_conv_w, v_conv_b, v_conv_norm_g, v_w_conv_out, v_q_norm_g,
                           v_k_norm_g, v_w_attn_out, v_w_out, v_norm2_g, v_w_up, v_ffn_conv_w, v_ffn_conv_b, v_w_down)))
    dims = Dims(d_model=x.shape[-1], batch_local=x.shape[0], seq=x.shape[1], d_ff=w_down.shape[1] * N_CHIPS)
    return _step(dims, x, loss_target, w, m, v)
```

```python
import functools
import math
from typing import NamedTuple

import jax
import jax.numpy as jnp
from jax import lax
from jax.experimental import pallas as pl
from jax.experimental.pallas import tpu as pltpu

F32 = jnp.float32
BF16 = jnp.bfloat16

RMS_EPS = 1e-6
MASKED_SCORE = -1e30
ATTN_BLOCK = 128
DILATIONS = (1, 4, 16)
CONV_HALO = 32
FFN_HALO = 8
ADAM_LR, ADAM_B1, ADAM_B2, ADAM_EPS, ADAM_WD, ADAM_STEP = 0.001, 0.9, 0.999, 1e-08, 0.01, 10
V7X_VMEM_LIMIT_BYTES = 56 * 2 ** 20
N_CHIPS = 4
MESH = pl.DeviceIdType.MESH


class Dims(NamedTuple):
    d_model: int = 1024
    n_heads: int = 16
    head_dim: int = 64
    d_ff: int = 2816
    seq: int = 2048
    batch_local: int = 2
    conv_width: int = 31
    ffn_conv_width: int = 3

    @property
    def tokens(self):
        return self.seq * self.batch_local


def _params(*semantics):
    return pltpu.CompilerParams(dimension_semantics=semantics, vmem_limit_bytes=V7X_VMEM_LIMIT_BYTES)


ANY = pl.BlockSpec(memory_space=pl.ANY)


def _ordered(body, n_inputs, after):
    after = [] if after is None else list(after) if isinstance(after, (list, tuple)) else [after]
    if not after:
        return body, [], []

    def wrapped(*refs):
        return body(*refs[:n_inputs], *refs[n_inputs + len(after):])

    return wrapped, [ANY] * len(after), after


def _pick(n, target, mult=128):
    if n <= target:
        return n
    best = None
    for t in range(mult, target + 1, mult):
        if n % t == 0:
            best = t
    assert best is not None, (n, target, mult)
    return best


def _sigmoid(v):
    return 1.0 / (1.0 + jnp.exp(-v))


def _mm_nn(a, w, *, out_dtype, name, residual=None, after=None, tm=1024, tn=1408, tk=2816):
    m, k = a.shape
    nsh, k2, c = w.shape
    assert k == k2 and a.dtype == BF16 and w.dtype == BF16
    n = nsh * c
    tm, tn, tk = _pick(m, tm, 8), _pick(c, tn), _pick(k, tk)
    nk, cpn = k // tk, c // tn

    def body(*refs):
        if residual is None:
            a_ref, w_ref, o_ref, acc = refs
        else:
            a_ref, w_ref, r_ref, o_ref, acc = refs
        prod = jnp.dot(a_ref[...], w_ref[...], preferred_element_type=F32)

        def finish(total):
            if residual is not None:
                total = total + r_ref[...]
            o_ref[...] = total.astype(out_dtype)

        if nk == 1:
            finish(prod)
        else:
            kk = pl.program_id(2)

            @pl.when(kk == 0)
            def _():
                acc[...] = prod

            @pl.when(kk > 0)
            def _():
                acc[...] += prod

            @pl.when(kk == nk - 1)
            def _():
                finish(acc[...])

    in_specs = [pl.BlockSpec((tm, tk), lambda i, j, kk: (i, kk)),
                pl.BlockSpec((None, tk, tn), lambda i, j, kk: (j // cpn, kk, j % cpn))]
    args = [a, w]
    if residual is not None:
        in_specs.append(pl.BlockSpec((tm, tn), lambda i, j, kk: (i, j)))
        args.append(residual)
    body, more_specs, more_args = _ordered(body, len(args), after)
    return pl.pallas_call(
        body, name=name, grid=(m // tm, n // tn, nk),
        in_specs=in_specs + more_specs, out_specs=pl.BlockSpec((tm, tn), lambda i, j, kk: (i, j)),
        out_shape=jax.ShapeDtypeStruct((m, n), out_dtype),
        scratch_shapes=[pltpu.VMEM((tm, tn) if nk > 1 else (8, 128), F32)],
        compiler_params=_params("parallel", "parallel", "arbitrary"),
    )(*args, *more_args)


def _mm_nt(a, w, *, out_dtype, name, after=None, tm=1024, tn=1408, tk=1792):
    m, k = a.shape
    nsh, r, c = w.shape
    assert k == nsh * c and a.dtype == BF16 and w.dtype == BF16
    tm, tn, tk = _pick(m, tm, 8), _pick(r, tn), _pick(c, tk)
    nk, cpk = k // tk, c // tk

    def body(a_ref, w_ref, o_ref, acc):
        prod = lax.dot_general(a_ref[...], w_ref[...], (((1,), (1,)), ((), ())), preferred_element_type=F32)
        if nk == 1:
            o_ref[...] = prod.astype(out_dtype)
        else:
            kk = pl.program_id(2)

            @pl.when(kk == 0)
            def _():
                acc[...] = prod

            @pl.when(kk > 0)
            def _():
                acc[...] += prod

            @pl.when(kk == nk - 1)
            def _():
                o_ref[...] = acc[...].astype(out_dtype)

    body, more_specs, more_args = _ordered(body, 2, after)
    return pl.pallas_call(
        body, name=name, grid=(m // tm, r // tn, nk),
        in_specs=[pl.BlockSpec((tm, tk), lambda i, j, kk: (i, kk)),
                  pl.BlockSpec((None, tn, tk), lambda i, j, kk: (kk // cpk, j, kk % cpk))] + more_specs,
        out_specs=pl.BlockSpec((tm, tn), lambda i, j, kk: (i, j)),
        out_shape=jax.ShapeDtypeStruct((m, r), out_dtype),
        scratch_shapes=[pltpu.VMEM((tm, tn) if nk > 1 else (8, 128), F32)],
        compiler_params=_params("parallel", "parallel", "arbitrary"),
    )(a, w, *more_args)


MM_TN_VMEM_BYTES = 44 * 2 ** 20


def _mm_tn(a, b, *, n_shards, name, tm=1408, tn=1408):
    t, m = a.shape
    t2, n = b.shape
    assert t == t2 and a.dtype == BF16 and b.dtype == BF16
    c = n // n_shards
    tm, tn = _pick(m, tm), _pick(c, tn)
    fixed = 2 * tm * tn * 6
    if 4 * t * (tm + tn) + fixed <= MM_TN_VMEM_BYTES:
        tk = t
    else:
        tk = _pick(t, (MM_TN_VMEM_BYTES - fixed - 4 * tm * tn) // (4 * (tm + tn)), 8)
    nk, cpn = t // tk, c // tn

    def body(a_ref, b_ref, o_ref, ob_ref, acc):
        kk = pl.program_id(2)
        prod = lax.dot_general(a_ref[...], b_ref[...], (((0,), (0,)), ((), ())), preferred_element_type=F32)

        def finish(total):
            o_ref[...] = total
            ob_ref[...] = total.astype(BF16)

        if nk == 1:
            finish(prod)
        else:
            @pl.when(kk == 0)
            def _():
                acc[...] = prod

            @pl.when(kk > 0)
            def _():
                acc[...] += prod

            @pl.when(kk == nk - 1)
            def _():
                finish(acc[...])

    out_spec = pl.BlockSpec((None, tm, tn), lambda i, j, kk: (j // cpn, i, j % cpn))
    return pl.pallas_call(
        body, name=name, grid=(m // tm, n // tn, nk),
        in_specs=[pl.BlockSpec((tk, tm), lambda i, j, kk: (kk, i)),
                  pl.BlockSpec((tk, tn), lambda i, j, kk: (kk, j))],
        out_specs=[out_spec, out_spec],
        out_shape=[jax.ShapeDtypeStruct((n_shards, m, c), F32), jax.ShapeDtypeStruct((n_shards, m, c), BF16)],
        scratch_shapes=[pltpu.VMEM((tm, tn) if nk > 1 else (8, 128), F32)],
        compiler_params=_params("parallel", "parallel", "arbitrary"),
    )(a, b)


def _row_spec(tr, width, col=0):
    return pl.BlockSpec((tr, width), lambda i, col=col: (i, col))


def _vec_spec(width, col=0):
    return pl.BlockSpec((1, width), lambda i, col=col: (0, col))


def _accumulate(ref, value, first):
    @pl.when(first)
    def _():
        ref[...] = value

    @pl.when(jnp.logical_not(first))
    def _():
        ref[...] += value


def _rmsnorm_fwd(x, g, *, name, tr=512):
    t, d = x.shape
    tr = _pick(t, tr, 8)

    def body(x_ref, g_ref, o_ref):
        xv = x_ref[...]
        r = lax.rsqrt(jnp.mean(xv * xv, axis=-1, keepdims=True) + RMS_EPS)
        o_ref[...] = (xv * r * g_ref[...]).astype(BF16)

    return pl.pallas_call(
        body, name=name, grid=(t // tr,),
        in_specs=[_row_spec(tr, d), _vec_spec(d)], out_specs=_row_spec(tr, d),
        out_shape=jax.ShapeDtypeStruct((t, d), BF16), compiler_params=_params("parallel"),
    )(x, g)


def _rmsnorm_bwd(x, g, dy, dres, *, name, want_bf16, tr=512):
    t, d = x.shape
    tr = _pick(t, tr, 8)

    def body(x_ref, g_ref, dy_ref, dres_ref, *outs):
        dx_ref, dg_ref = outs[0], outs[-1]
        xv, dyv = x_ref[...], dy_ref[...].astype(F32)
        r = lax.rsqrt(jnp.mean(xv * xv, axis=-1, keepdims=True) + RMS_EPS)
        gy = dyv * g_ref[...]
        dx = dres_ref[...] + r * gy - xv * (r * r * r) * jnp.mean(xv * gy, axis=-1, keepdims=True)
        dx_ref[...] = dx
        if want_bf16:
            outs[1][...] = dx.astype(BF16)
        _accumulate(dg_ref, jnp.sum(dyv * xv * r, axis=0, keepdims=True), pl.program_id(0) == 0)

    out_shape = [jax.ShapeDtypeStruct((t, d), F32)]
    out_specs = [_row_spec(tr, d)]
    if want_bf16:
        out_shape.append(jax.ShapeDtypeStruct((t, d), BF16))
        out_specs.append(_row_spec(tr, d))
    out_shape.append(jax.ShapeDtypeStruct((1, d), F32))
    out_specs.append(_vec_spec(d))
    return pl.pallas_call(
        body, name=name, grid=(t // tr,),
        in_specs=[_row_spec(tr, d), _vec_spec(d), _row_spec(tr, d), _row_spec(tr, d)],
        out_specs=out_specs, out_shape=out_shape, compiler_params=_params("arbitrary"),
    )(x, g, dy, dres)


def _head_mean(v, ones_ref, head_dim):
    hi = v.astype(BF16)
    lo = (v - hi.astype(F32)).astype(BF16)
    e = ones_ref[...]
    total = jnp.dot(hi, e, preferred_element_type=F32) + jnp.dot(lo, e, preferred_element_type=F32)
    return total * (1.0 / head_dim)


def _qkv_fwd(z, gq, gk, head_ones, dims, *, name, tr=256):
    t = z.shape[0]
    a = dims.n_heads * dims.head_dim
    tr = _pick(t, tr, 8)
    q_scale = dims.head_dim ** -0.5

    def body(q_ref, k_ref, v_ref, gq_ref, gk_ref, e_ref, qo_ref, ko_ref, vo_ref):
        qv, kv = q_ref[...], k_ref[...]
        rq = lax.rsqrt(_head_mean(qv * qv, e_ref, dims.head_dim) + RMS_EPS)
        rk = lax.rsqrt(_head_mean(kv * kv, e_ref, dims.head_dim) + RMS_EPS)
        qo_ref[...] = (qv * rq * gq_ref[...] * q_scale).astype(BF16)
        ko_ref[...] = (kv * rk * gk_ref[...]).astype(BF16)
        vo_ref[...] = v_ref[...].astype(BF16)

    return pl.pallas_call(
        body, name=name, grid=(t // tr,),
        in_specs=[_row_spec(tr, a, 2), _row_spec(tr, a, 3), _row_spec(tr, a, 4), _vec_spec(a), _vec_spec(a),
                  pl.BlockSpec((a, a), lambda i: (0, 0))],
        out_specs=[_row_spec(tr, a)] * 3, out_shape=[jax.ShapeDtypeStruct((t, a), BF16)] * 3,
        compiler_params=_params("parallel"),
    )(z, z, z, gq, gk, head_ones)


def _qkv_bwd(z, dqs, dks, dvs, gq, gk, head_ones, dims, *, name, tr=256):
    t = z.shape[0]
    a = dims.n_heads * dims.head_dim
    tr = _pick(t, tr, 8)
    q_scale = dims.head_dim ** -0.5
    ng = len(dqs)

    def body(*refs):
        q_ref, k_ref = refs[:2]
        dq_refs, dk_refs, dv_refs = refs[2:2 + ng], refs[2 + ng:2 + 2 * ng], refs[2 + 2 * ng:2 + 3 * ng]
        gq_ref, gk_ref, e_ref = refs[2 + 3 * ng:5 + 3 * ng]
        dz_ref, dgq_ref, dgk_ref = refs[5 + 3 * ng:]
        first = pl.program_id(0) == 0

        def norm_bwd(x_ref, d_refs, g_ref, scale, col, dg_ref):
            xv = x_ref[...]
            dy = sum(r[...] for r in d_refs) * scale
            r = lax.rsqrt(_head_mean(xv * xv, e_ref, dims.head_dim) + RMS_EPS)
            gy = dy * g_ref[...]
            dx = r * gy - xv * (r * r * r) * _head_mean(xv * gy, e_ref, dims.head_dim)
            dz_ref[:, col * a:(col + 1) * a] = dx.astype(BF16)
            _accumulate(dg_ref, jnp.sum(dy * xv * r, axis=0, keepdims=True), first)

        norm_bwd(q_ref, dq_refs, gq_ref, q_scale, 0, dgq_ref)
        norm_bwd(k_ref, dk_refs, gk_ref, 1.0, 1, dgk_ref)
        dz_ref[:, 2 * a:3 * a] = sum(r[...] for r in dv_refs).astype(BF16)

    in_specs = ([_row_spec(tr, a, 2), _row_spec(tr, a, 3)] + [_row_spec(tr, a)] * (3 * ng)
                + [_vec_spec(a), _vec_spec(a), pl.BlockSpec((a, a), lambda i: (0, 0))])
    return pl.pallas_call(
        body, name=name, grid=(t // tr,), in_specs=in_specs,
        out_specs=[_row_spec(tr, 3 * a), _vec_spec(a), _vec_spec(a)],
        out_shape=[jax.ShapeDtypeStruct((t, 3 * a), BF16)] + [jax.ShapeDtypeStruct((1, a), F32)] * 2,
        compiler_params=_params("arbitrary"),
    )(z, z, *dqs, *dks, *dvs, gq, gk, head_ones)


CONV_ROWS = 16


def _seq_specs(dims, ts, width, halo, col, *, nxt=False):
    nst, per = dims.seq // ts, ts // halo
    last = dims.tokens // halo - 1
    cur = pl.BlockSpec((ts, width), lambda b, i: (b * nst + i, col))
    if nxt:
        edge = pl.BlockSpec((halo, width), lambda b, i: (jnp.minimum((b * nst + i + 1) * per, last), col))
    else:
        edge = pl.BlockSpec((halo, width), lambda b, i: (jnp.maximum((b * nst + i) * per - 1, 0), col))
    return cur, edge


SUBLANES = 8


def _shifted_copies(buf, shifted):
    rows = shifted.shape[1]
    for s in range(1, SUBLANES):
        shifted[s - 1] = buf[pl.ds(s, rows), :]


def _window(buf, shifted, start, size):
    a, s = divmod(start, SUBLANES)
    src = buf if s == 0 else shifted.at[s - 1]
    return src[pl.ds(SUBLANES * a, size), :]


def _conv_branch_fwd(z, w, b, g, dims, *, name, ts=128):
    t, c, kw = z.shape[0], dims.d_model, dims.conv_width
    base = CONV_HALO - (kw - 1)

    def body(av_ref, hv_ref, ag_ref, hg_ref, w_ref, b_ref, g_ref, a1_ref, a3_ref, buf, shifted):
        i = pl.program_id(1)
        buf[CONV_HALO:, :] = av_ref[...] * _sigmoid(ag_ref[...])
        buf[0:CONV_HALO, :] = jnp.where(i > 0, hv_ref[...] * _sigmoid(hg_ref[...]), 0.0)
        _shifted_copies(buf, shifted)
        for r0 in range(0, ts, CONV_ROWS):
            acc = jnp.broadcast_to(b_ref[...], (CONV_ROWS, c))
            for k in range(kw):
                acc = acc + w_ref[k:k + 1, :] * _window(buf, shifted, r0 + base + k, CONV_ROWS)
            a1_ref[r0:r0 + CONV_ROWS, :] = acc
            a2 = acc * lax.rsqrt(jnp.mean(acc * acc, axis=-1, keepdims=True) + RMS_EPS) * g_ref[...]
            a3_ref[r0:r0 + CONV_ROWS, :] = (a2 * _sigmoid(a2)).astype(BF16)

    vec = pl.BlockSpec((1, c), lambda b, i: (0, 0))
    out = pl.BlockSpec((ts, c), lambda b, i: (b * (dims.seq // ts) + i, 0))
    return pl.pallas_call(
        body, name=name, grid=(dims.batch_local, dims.seq // ts),
        in_specs=[*_seq_specs(dims, ts, c, CONV_HALO, 0), *_seq_specs(dims, ts, c, CONV_HALO, 1),
                  pl.BlockSpec((CONV_HALO, c), lambda b, i: (0, 0)), vec, vec],
        out_specs=[out, out],
        out_shape=[jax.ShapeDtypeStruct((t, c), F32), jax.ShapeDtypeStruct((t, c), BF16)],
        scratch_shapes=[pltpu.VMEM((CONV_HALO + ts, c), F32),
                        pltpu.VMEM((SUBLANES - 1, CONV_HALO + ts - SUBLANES, c), F32)],
        compiler_params=_params("parallel", "parallel"),
    )(z, z, z, z, w, b, g)


def _conv_norm_bwd(da3, a1, g, *, name, tr=256):
    t, c = a1.shape
    tr = _pick(t, tr, 8)

    def body(d_ref, a_ref, g_ref, o_ref, dg_ref):
        a1v, gv = a_ref[...], g_ref[...]
        r = lax.rsqrt(jnp.mean(a1v * a1v, axis=-1, keepdims=True) + RMS_EPS)
        a2 = a1v * r * gv
        sg = _sigmoid(a2)
        da2 = d_ref[...].astype(F32) * sg * (1.0 + a2 * (1.0 - sg))
        gy = da2 * gv
        o_ref[...] = r * gy - a1v * (r * r * r) * jnp.mean(a1v * gy, axis=-1, keepdims=True)
        _accumulate(dg_ref, jnp.sum(da2 * a1v * r, axis=0, keepdims=True), pl.program_id(0) == 0)

    return pl.pallas_call(
        body, name=name, grid=(t // tr,),
        in_specs=[_row_spec(tr, c), _row_spec(tr, c), _vec_spec(c)],
        out_specs=[_row_spec(tr, c), _vec_spec(c)],
        out_shape=[jax.ShapeDtypeStruct((t, c), F32), jax.ShapeDtypeStruct((1, c), F32)],
        compiler_params=_params("arbitrary"),
    )(da3, a1, g)


def _conv_branch_bwd(da1, z, w, dims, *, name, ts=128):
    t, c, kw = z.shape[0], dims.d_model, dims.conv_width
    nst = dims.seq // ts
    base = CONV_HALO - (kw - 1)

    def body(d_ref, dn_ref, av_ref, hv_ref, ag_ref, hg_ref, w_ref, dz_ref, dw_ref, db_ref, abuf, dbuf, ashift, dshift):
        i = pl.program_id(1)
        first = jnp.logical_and(pl.program_id(0) == 0, i == 0)
        abuf[CONV_HALO:, :] = av_ref[...] * _sigmoid(ag_ref[...])
        abuf[0:CONV_HALO, :] = jnp.where(i > 0, hv_ref[...] * _sigmoid(hg_ref[...]), 0.0)
        d1 = d_ref[...]
        dbuf[0:ts, :] = d1
        dbuf[ts:, :] = jnp.where(i < nst - 1, dn_ref[...], 0.0)
        _shifted_copies(abuf, ashift)
        _shifted_copies(dbuf, dshift)

        @pl.when(first)
        def _():
            dw_ref[...] = jnp.zeros_like(dw_ref)
            db_ref[...] = jnp.zeros_like(db_ref)

        db_ref[...] += jnp.sum(d1, axis=0, keepdims=True)
        for k in range(kw):
            dw_ref[k:k + 1, :] += jnp.sum(d1 * _window(abuf, ashift, base + k, ts), axis=0, keepdims=True)
        for r0 in range(0, ts, CONV_ROWS):
            acc = jnp.zeros((CONV_ROWS, c), F32)
            for k in range(kw):
                acc = acc + w_ref[k:k + 1, :] * _window(dbuf, dshift, r0 + (kw - 1) - k, CONV_ROWS)
            av = av_ref[r0:r0 + CONV_ROWS, :]
            sg = _sigmoid(ag_ref[r0:r0 + CONV_ROWS, :])
            dz_ref[r0:r0 + CONV_ROWS, 0:c] = (acc * sg).astype(BF16)
            dz_ref[r0:r0 + CONV_ROWS, c:2 * c] = (acc * av * sg * (1.0 - sg)).astype(BF16)

    cur, nxt = _seq_specs(dims, ts, c, CONV_HALO, 0, nxt=True)
    return pl.pallas_call(
        body, name=name, grid=(dims.batch_local, nst),
        in_specs=[cur, nxt, *_seq_specs(dims, ts, c, CONV_HALO, 0), *_seq_specs(dims, ts, c, CONV_HALO, 1),
                  pl.BlockSpec((CONV_HALO, c), lambda b, i: (0, 0))],
        out_specs=[pl.BlockSpec((ts, 2 * c), lambda b, i: (b * nst + i, 0)),
                   pl.BlockSpec((CONV_HALO, c), lambda b, i: (0, 0)), pl.BlockSpec((1, c), lambda b, i: (0, 0))],
        out_shape=[jax.ShapeDtypeStruct((t, 2 * c), BF16), jax.ShapeDtypeStruct((CONV_HALO, c), F32),
                   jax.ShapeDtypeStruct((1, c), F32)],
        scratch_shapes=[pltpu.VMEM((CONV_HALO + ts, c), F32)] * 2
        + [pltpu.VMEM((SUBLANES - 1, CONV_HALO + ts - SUBLANES, c), F32)] * 2,
        compiler_params=_params("arbitrary", "arbitrary"),
    )(da1, da1, z, z, z, z, w)


FFN_ROWS = 16
FFN_COLS = 256


def _ffn_chunks(ts, f):
    cw = _pick(f, FFN_COLS)
    return [(r0, c0, cw) for r0 in range(0, ts, FFN_ROWS) for c0 in range(0, f, cw)]


def _ffn_conv(buf, w_ref, b_ref, r0, cols, kw):
    base = FFN_HALO - (kw - 1)
    u = jnp.broadcast_to(b_ref[:, cols], (FFN_ROWS, cols.stop - cols.start))
    for k in range(kw):
        u = u + w_ref[k:k + 1, cols] * buf[pl.ds(r0 + base + k, FFN_ROWS), cols]
    return u


def _ffn_act_fwd(up, w, b, dims, *, name, ts=128):
    t, f, kw = up.shape[0], dims.d_ff, dims.ffn_conv_width

    def body(up_ref, h_ref, w_ref, b_ref, o_ref, buf):
        buf[FFN_HALO:, :] = up_ref[...]
        buf[0:FFN_HALO, :] = jnp.where(pl.program_id(1) > 0, h_ref[...], 0.0)
        for r0, c0, cw in _ffn_chunks(ts, f):
            uv = _ffn_conv(buf, w_ref, b_ref, r0, slice(c0, c0 + cw), kw)
            ug = _ffn_conv(buf, w_ref, b_ref, r0, slice(f + c0, f + c0 + cw), kw)
            o_ref[r0:r0 + FFN_ROWS, c0:c0 + cw] = (ug * _sigmoid(ug) * uv).astype(BF16)

    full = lambda rows: pl.BlockSpec((rows, 2 * f), lambda b_, i: (0, 0))
    return pl.pallas_call(
        body, name=name, grid=(dims.batch_local, dims.seq // ts),
        in_specs=[*_seq_specs(dims, ts, 2 * f, FFN_HALO, 0), full(FFN_HALO), full(1)],
        out_specs=pl.BlockSpec((ts, f), lambda b_, i: (b_ * (dims.seq // ts) + i, 0)),
        out_shape=jax.ShapeDtypeStruct((t, f), BF16),
        scratch_shapes=[pltpu.VMEM((FFN_HALO + ts, 2 * f), F32)],
        compiler_params=_params("parallel", "parallel"),
    )(up, up, w, b)


def _ffn_act_bwd(dact, up, w, b, dims, *, name, ts=128):
    t, f, kw = up.shape[0], dims.d_ff, dims.ffn_conv_width
    base = FFN_HALO - (kw - 1)

    def body(d_ref, up_ref, h_ref, w_ref, b_ref, du_ref, dw_ref, db_ref, buf):
        i = pl.program_id(1)
        first = jnp.logical_and(pl.program_id(0) == 0, i == 0)
        buf[FFN_HALO:, :] = up_ref[...]
        buf[0:FFN_HALO, :] = jnp.where(i > 0, h_ref[...], 0.0)
        for r0, c0, cw in _ffn_chunks(ts, f):
            vcols, gcols = slice(c0, c0 + cw), slice(f + c0, f + c0 + cw)
            uv = _ffn_conv(buf, w_ref, b_ref, r0, vcols, kw)
            ug = _ffn_conv(buf, w_ref, b_ref, r0, gcols, kw)
            d = d_ref[r0:r0 + FFN_ROWS, vcols].astype(F32)
            sg = _sigmoid(ug)
            du_ref[r0:r0 + FFN_ROWS, vcols] = d * ug * sg
            du_ref[r0:r0 + FFN_ROWS, gcols] = d * uv * sg * (1.0 + ug * (1.0 - sg))

        @pl.when(first)
        def _():
            dw_ref[...] = jnp.zeros_like(dw_ref)
            db_ref[...] = jnp.zeros_like(db_ref)

        du = du_ref[...]
        db_ref[...] += jnp.sum(du, axis=0, keepdims=True)
        for k in range(kw):
            dw_ref[k:k + 1, :] += jnp.sum(du * buf[pl.ds(base + k, ts), :], axis=0, keepdims=True)

    nst = dims.seq // ts
    full = lambda rows: pl.BlockSpec((rows, 2 * f), lambda b_, i: (0, 0))
    return pl.pallas_call(
        body, name=name, grid=(dims.batch_local, nst),
        in_specs=[pl.BlockSpec((ts, f), lambda b_, i: (b_ * nst + i, 0)),
                  *_seq_specs(dims, ts, 2 * f, FFN_HALO, 0), full(FFN_HALO), full(1)],
        out_specs=[pl.BlockSpec((ts, 2 * f), lambda b_, i: (b_ * nst + i, 0)), full(FFN_HALO), full(1)],
        out_shape=[jax.ShapeDtypeStruct((t, 2 * f), F32), jax.ShapeDtypeStruct((FFN_HALO, 2 * f), F32),
                   jax.ShapeDtypeStruct((1, 2 * f), F32)],
        scratch_shapes=[pltpu.VMEM((FFN_HALO + ts, 2 * f), F32)],
        compiler_params=_params("arbitrary", "arbitrary"),
    )(dact, up, up, w, b)


def _ffn_conv_bwd(du, w, dims, *, name, ts=128):
    t, f2 = du.shape
    kw = dims.ffn_conv_width
    nst = dims.seq // ts

    def body(d_ref, dn_ref, w_ref, o_ref, buf):
        buf[0:ts, :] = d_ref[...]
        buf[ts:, :] = jnp.where(pl.program_id(1) < nst - 1, dn_ref[...], 0.0)
        for r0, c0, cw in _ffn_chunks(ts, f2):
            cols = slice(c0, c0 + cw)
            acc = jnp.zeros((FFN_ROWS, cw), F32)
            for k in range(kw):
                acc = acc + w_ref[k:k + 1, cols] * buf[pl.ds(r0 + (kw - 1) - k, FFN_ROWS), cols]
            o_ref[r0:r0 + FFN_ROWS, cols] = acc.astype(BF16)

    return pl.pallas_call(
        body, name=name, grid=(dims.batch_local, nst),
        in_specs=[*_seq_specs(dims, ts, f2, FFN_HALO, 0, nxt=True), pl.BlockSpec((FFN_HALO, f2), lambda b_, i: (0, 0))],
        out_specs=pl.BlockSpec((ts, f2), lambda b_, i: (b_ * nst + i, 0)),
        out_shape=jax.ShapeDtypeStruct((t, f2), BF16),
        scratch_shapes=[pltpu.VMEM((ts + FFN_HALO, f2), F32)],
        compiler_params=_params("parallel", "parallel"),
    )(du, du, w)


def _alibi_slope(h, n_heads):
    return 2.0 ** (-8.0 * (h + 1) / n_heads)


def _dot_nt(a, b):
    return lax.dot_general(a, b, (((1,), (1,)), ((), ())), preferred_element_type=F32)


def _dot_tn(a, b):
    return lax.dot_general(a, b, (((0,), (0,)), ((), ())), preferred_element_type=F32)


def _attn_view(x, dims, dil):
    return x.reshape(dims.batch_local, dims.seq // dil, dil * x.shape[-1])


def _attn_fwd_group(q, k, v, state, dims, dil, *, last, name):
    t, a = q.shape
    assert 2 * dims.head_dim == 128 and dims.n_heads % 2 == 0
    blk, hd = ATTN_BLOCK, dims.head_dim
    nb = dims.seq // dil // blk
    has_prev = nb > 1
    nkeys = 2 * blk if has_prev else blk

    def body(*refs):
        it = iter(refs)
        q_ref, kc_ref, vc_ref = next(it), next(it), next(it)
        kp_ref, vp_ref = (next(it), next(it)) if has_prev else (None, None)
        m_in, l_in, acc_in = (next(it), next(it), next(it)) if state is not None else (None, None, None)
        outs = list(it)
        iq = lax.broadcasted_iota(jnp.int32, (blk, nkeys), 0)
        jk = lax.broadcasted_iota(jnp.int32, (blk, nkeys), 1)
        if has_prev:
            steps = iq + blk - jk
            valid = (steps >= 0) & (steps <= blk) & ((jk >= blk) | (pl.program_id(2) > 0))
        else:
            steps = iq - jk
            valid = steps >= 0
        dist = steps.astype(F32) * float(dil)
        low = lax.broadcasted_iota(jnp.int32, (blk, 2 * hd), 1) < hd
        for hp in range(dims.n_heads // 2):
            sl = slice(2 * hd * hp, 2 * hd * (hp + 1))
            q2 = q_ref[:, sl]
            if has_prev:
                kcat = jnp.concatenate([kp_ref[:, sl], kc_ref[:, sl]], axis=0)
                vcat = jnp.concatenate([vp_ref[:, sl], vc_ref[:, sl]], axis=0)
            else:
                kcat, vcat = kc_ref[:, sl], vc_ref[:, sl]
            halves = []
            for half in range(2):
                col = 2 * hd * hp + hd * half
                qh = jnp.where(low if half == 0 else jnp.logical_not(low), q2, jnp.zeros_like(q2))
                sc = _dot_nt(qh, kcat) - _alibi_slope(2 * hp + half, dims.n_heads) * dist
                sc = jnp.where(valid, sc, MASKED_SCORE)
                row_max = jnp.max(sc, axis=-1, keepdims=True)
                if state is None:
                    m_new = row_max
                    p = jnp.exp(sc - m_new)
                    alpha = None
                    l_new = jnp.sum(p, axis=-1, keepdims=True)
                else:
                    m_old = m_in[:, col:col + 1]
                    m_new = jnp.maximum(m_old, row_max)
                    p = jnp.exp(sc - m_new)
                    alpha = jnp.exp(m_old - m_new)
                    l_new = alpha * l_in[:, col:col + 1] + jnp.sum(p, axis=-1, keepdims=True)
                pv = jnp.dot(p.astype(BF16), vcat, preferred_element_type=F32)
                halves.append((m_new, l_new, alpha, pv))
            (m_a, l_a, al_a, pv_a), (m_b, l_b, al_b, pv_b) = halves
            if state is None:
                acc = jnp.where(low, pv_a, pv_b)
            else:
                old = acc_in[:, sl]
                acc = jnp.where(low, al_a * old + pv_a, al_b * old + pv_b)
            m2 = jnp.where(low, m_a, m_b)
            l2 = jnp.where(low, l_a, l_b)
            if last:
                outs[0][:, sl] = (acc / l2).astype(BF16)
                outs[1][:, sl] = m2 + jnp.log(l2)
            else:
                outs[0][:, sl] = m2
                outs[1][:, sl] = l2
                outs[2][:, sl] = acc

    cur = pl.BlockSpec((None, blk, a), lambda b, r, i: (b, i, r))
    prev = pl.BlockSpec((None, blk, a), lambda b, r, i: (b, jnp.maximum(i - 1, 0), r))
    args, in_specs = [q, k, v], [cur, cur, cur]
    if has_prev:
        args += [k, v]
        in_specs += [prev, prev]
    if state is not None:
        args += list(state)
        in_specs += [cur] * 3
    shape = lambda dt: jax.ShapeDtypeStruct((dims.batch_local, dims.seq // dil, dil * a), dt)
    out_shape = [shape(BF16), shape(F32)] if last else [shape(F32)] * 3
    outs = pl.pallas_call(
        body, name=name, grid=(dims.batch_local, dil, nb),
        in_specs=in_specs, out_specs=[cur] * len(out_shape), out_shape=out_shape,
        compiler_params=_params("parallel", "parallel", "parallel"),
    )(*[_attn_view(x, dims, dil) for x in args])
    return tuple(o.reshape(t, a) for o in outs)


def _attn_delta(do, o, head_ones, dims, *, name, tr=512):
    t, a = o.shape
    tr = _pick(t, tr, 8)

    def body(do_ref, o_ref, e_ref, d_ref):
        prod = do_ref[...].astype(F32) * o_ref[...].astype(F32)
        d_ref[...] = _head_mean(prod, e_ref, dims.head_dim) * float(dims.head_dim)

    return pl.pallas_call(
        body, name=name, grid=(t // tr,),
        in_specs=[_row_spec(tr, a), _row_spec(tr, a), pl.BlockSpec((a, a), lambda i: (0, 0))],
        out_specs=_row_spec(tr, a), out_shape=jax.ShapeDtypeStruct((t, a), F32),
        compiler_params=_params("parallel"),
    )(do, o, head_ones)


def _attn_bwd_group(q, k, v, do, lse, delta, dims, dil, *, name):
    t, a = q.shape
    blk, hd = ATTN_BLOCK, dims.head_dim
    nb = dims.seq // dil // blk
    has_next = nb > 1

    def body(*refs):
        k_ref, v_ref, q_ref, do_ref, lse_ref, dl_ref = refs[:6]
        if has_next:
            qn_ref, don_ref, lsen_ref, dln_ref = refs[6:10]
            dq_ref, dk_ref, dv_ref, carry = refs[10:]
        else:
            dq_ref, dk_ref, dv_ref = refs[6:]
        j = pl.program_id(2)
        iq = lax.broadcasted_iota(jnp.int32, (blk, blk), 0)
        jk = lax.broadcasted_iota(jnp.int32, (blk, blk), 1)
        low = lax.broadcasted_iota(jnp.int32, (blk, 2 * hd), 1) < hd

        def pair(hp, qr, dor, lser, dlr, steps, valid):
            sl = slice(2 * hd * hp, 2 * hd * (hp + 1))
            q2, do2, k2, v2 = qr[:, sl], dor[:, sl], k_ref[:, sl], v_ref[:, sl]
            dist = steps.astype(F32) * float(dil)
            dq_h, dk2, dv2 = [], None, None
            for half in range(2):
                col = 2 * hd * hp + hd * half
                mask = low if half == 0 else jnp.logical_not(low)
                qh = jnp.where(mask, q2, jnp.zeros_like(q2))
                doh = jnp.where(mask, do2, jnp.zeros_like(do2))
                sc = _dot_nt(qh, k2) - _alibi_slope(2 * hp + half, dims.n_heads) * dist
                p = jnp.where(valid, jnp.exp(sc - lser[:, col:col + 1]), 0.0)
                ds = p * (_dot_nt(doh, v2) - dlr[:, col:col + 1])
                ds_b, p_b = ds.astype(BF16), p.astype(BF16)
                dq_h.append(jnp.dot(ds_b, k2, preferred_element_type=F32))
                dk_h, dv_h = _dot_tn(ds_b, qh), _dot_tn(p_b, doh)
                dk2 = dk_h if dk2 is None else dk2 + dk_h
                dv2 = dv_h if dv2 is None else dv2 + dv_h
            return sl, jnp.where(low, dq_h[0], dq_h[1]), dk2, dv2

        if has_next:
            @pl.when(j == 0)
            def _():
                carry[...] = jnp.zeros_like(carry)

        for hp in range(dims.n_heads // 2):
            sl, dq2, dk2, dv2 = pair(hp, q_ref, do_ref, lse_ref, dl_ref, iq - jk, iq >= jk)
            dq_ref[:, sl] = (carry[:, sl] + dq2) if has_next else dq2
            dk_ref[:, sl] = dk2
            dv_ref[:, sl] = dv2

        if has_next:
            @pl.when(j + 1 < nb)
            def _():
                for hp in range(dims.n_heads // 2):
                    sl, dq2, dk2, dv2 = pair(hp, qn_ref, don_ref, lsen_ref, dln_ref, iq - jk + blk, jk >= iq)
                    carry[:, sl] = dq2
                    dk_ref[:, sl] += dk2
                    dv_ref[:, sl] += dv2

    cur = pl.BlockSpec((None, blk, a), lambda b, r, j: (b, j, r))
    nxt = pl.BlockSpec((None, blk, a), lambda b, r, j: (b, jnp.minimum(j + 1, nb - 1), r))
    args, in_specs = [k, v, q, do, lse, delta], [cur] * 6
    if has_next:
        args += [q, do, lse, delta]
        in_specs += [nxt] * 4
    shape = jax.ShapeDtypeStruct((dims.batch_local, dims.seq // dil, dil * a), F32)
    outs = pl.pallas_call(
        body, name=name, grid=(dims.batch_local, dil, nb),
        in_specs=in_specs, out_specs=[cur] * 3, out_shape=[shape] * 3,
        scratch_shapes=[pltpu.VMEM((blk, a), F32)] if has_next else [],
        compiler_params=_params("parallel", "parallel", "arbitrary"),
    )(*[_attn_view(x, dims, dil) for x in args])
    return tuple(o.reshape(t, a) for o in outs)


LANES = 128
MASK_BIAS = 1e30
RESIDUE_DILATIONS = tuple(d for d in DILATIONS if d > 1)


def _rows_to_residues(value, out_ref, scr, d):
    rows, width = value.shape
    for c in range(width // LANES):
        cols = slice(LANES * c, LANES * (c + 1))
        scr[c] = value[:, cols]
        for r in range(d):
            out_ref[r, :, cols] = scr[c, pl.ds(r, rows // d, stride=d), :].astype(out_ref.dtype)


def _residues_to_rows(in_ref, scr, d):
    _, n, width = in_ref.shape
    slabs = []
    for c in range(width // LANES):
        cols = slice(LANES * c, LANES * (c + 1))
        for r in range(d):
            scr[c, pl.ds(r, n, stride=d), :] = in_ref[r, :, cols].astype(F32)
        slabs.append(scr[c])
    return slabs[0] if len(slabs) == 1 else jnp.concatenate(slabs, axis=1)


def _residue_shape(dims, d, width, dtype):
    return jax.ShapeDtypeStruct((dims.batch_local, d, dims.seq // d, width), dtype)


def _residue_spec(dims, d, tr, width):
    tiles = dims.seq // tr
    return pl.BlockSpec((None, d, tr // d, width), lambda i: (i // tiles, 0, i % tiles, 0))


def _head_sum_matrix(dims):
    a = dims.n_heads * dims.head_dim
    head = jnp.arange(a, dtype=jnp.int32) // dims.head_dim
    return (head[:, None] == jnp.arange(LANES, dtype=jnp.int32)[None, :]).astype(BF16)


def _two_pass_dot(v, m):
    hi = v.astype(BF16)
    lo = (v - hi.astype(F32)).astype(BF16)
    return jnp.dot(hi, m, preferred_element_type=F32) + jnp.dot(lo, m, preferred_element_type=F32)


def _qkv_layouts_fwd(z, gq, gk, head_ones, dims, *, name, tr=256):
    t = z.shape[0]
    a = dims.n_heads * dims.head_dim
    q_scale = dims.head_dim ** -0.5
    nres = len(RESIDUE_DILATIONS)

    def body(q_ref, k_ref, v_ref, gq_ref, gk_ref, sum_ref, spread_ref, *rest):
        outs, scr = rest[:-1], rest[-1]
        qv, kv = q_ref[...], k_ref[...]
        mean = lambda val: _two_pass_dot(_two_pass_dot(val, sum_ref[...]), spread_ref[...]) * (1.0 / dims.head_dim)
        rq = lax.rsqrt(mean(qv * qv) + RMS_EPS)
        rk = lax.rsqrt(mean(kv * kv) + RMS_EPS)
        values = (qv * rq * gq_ref[...] * q_scale, kv * rk * gk_ref[...], v_ref[...])
        for j, val in enumerate(values):
            outs[j][...] = val.astype(BF16)
            for g, d in enumerate(RESIDUE_DILATIONS):
                _rows_to_residues(val, outs[3 * (g + 1) + j], scr, d)

    out_specs = [_row_spec(tr, a)] * 3
    out_shape = [jax.ShapeDtypeStruct((t, a), BF16)] * 3
    for d in RESIDUE_DILATIONS:
        out_specs += [_residue_spec(dims, d, tr, a)] * 3
        out_shape += [_residue_shape(dims, d, a, BF16)] * 3
    outs = pl.pallas_call(
        body, name=name, grid=(t // tr,),
        in_specs=[_row_spec(tr, a, 2), _row_spec(tr, a, 3), _row_spec(tr, a, 4), _vec_spec(a), _vec_spec(a),
                  pl.BlockSpec((a, LANES), lambda i: (0, 0)), pl.BlockSpec((LANES, a), lambda i: (0, 0))],
        out_specs=out_specs, out_shape=out_shape,
        scratch_shapes=[pltpu.VMEM((a // LANES, tr, LANES), F32)],
        compiler_params=_params("parallel"),
    )(z, z, z, gq, gk, *head_ones)
    return {d: tuple(outs[3 * g:3 * g + 3]) for g, d in enumerate((1,) + RESIDUE_DILATIONS)}


def _attn_specs(dims, dil, width):
    blk = ATTN_BLOCK
    nb = dims.seq // dil // blk
    if dil == 1:
        grid = (dims.batch_local, nb)
        at = lambda f: pl.BlockSpec((blk, width), lambda b, i: (b * nb + f(i), 0))
    else:
        grid = (dims.batch_local, dil, nb)
        at = lambda f: pl.BlockSpec((None, None, blk, width), lambda b, r, i: (b, r, f(i), 0))
    return grid, at(lambda i: i), at(lambda i: jnp.maximum(i - 1, 0)), at(lambda i: jnp.minimum(i + 1, nb - 1))


def _head_slopes(n_heads):
    h = lax.broadcasted_iota(jnp.int32, (n_heads, 1, 1), 0).astype(F32)
    return jnp.exp((h + 1.0) * (-8.0 / n_heads * math.log(2.0)))


def _pair_masks(hd):
    low = lax.broadcasted_iota(jnp.int32, (1, 2 * hd), 1) < hd
    return low, jnp.logical_not(low)


def _attn_fwd(q, k, v, dims, dil, *, name):
    a = dims.n_heads * dims.head_dim
    heads, hd, blk = dims.n_heads, dims.head_dim, ATTN_BLOCK
    assert 2 * hd == LANES and heads % 2 == 0 and heads <= LANES
    nb = dims.seq // dil // blk
    has_prev = nb > 1
    nkeys = 2 * blk if has_prev else blk
    grid, cur, prev, _ = _attn_specs(dims, dil, a)
    _, cur_stat, _, _ = _attn_specs(dims, dil, LANES)

    def body(*refs):
        if has_prev:
            q_ref, kc_ref, vc_ref, kp_ref, vp_ref, o_ref, lse_ref, s_scr, p_scr = refs
        else:
            q_ref, kc_ref, vc_ref, o_ref, lse_ref, s_scr, p_scr = refs
        low, high = _pair_masks(hd)

        def keys(cur_ref, prev_ref, sl):
            return jnp.concatenate([prev_ref[:, sl], cur_ref[:, sl]], axis=0) if has_prev else cur_ref[:, sl]

        for hp in range(heads // 2):
            sl = slice(LANES * hp, LANES * (hp + 1))
            q2 = q_ref[:, sl]
            kcat = keys(kc_ref, kp_ref if has_prev else None, sl)
            s_scr[2 * hp] = _dot_nt(jnp.where(low, q2, jnp.zeros_like(q2)), kcat)
            s_scr[2 * hp + 1] = _dot_nt(jnp.where(high, q2, jnp.zeros_like(q2)), kcat)

        iq = lax.broadcasted_iota(jnp.int32, (blk, nkeys), 0)
        jk = lax.broadcasted_iota(jnp.int32, (blk, nkeys), 1)
        if has_prev:
            steps = iq + blk - jk
            valid = (steps >= 0) & (steps <= blk) & ((jk >= blk) | (pl.program_id(len(grid) - 1) > 0))
        else:
            steps = iq - jk
            valid = steps >= 0
        bias = jnp.where(valid, steps.astype(F32) * (-float(dil)), -MASK_BIAS)
        s = s_scr[...] + _head_slopes(heads) * bias[None]
        m = jnp.max(s, axis=-1, keepdims=True)
        p = jnp.exp(s - m)
        l = jnp.sum(p, axis=-1, keepdims=True)
        p_scr[...] = p.astype(BF16)
        inv = 1.0 / l
        lse = m + jnp.log(l)

        lane = lax.broadcasted_iota(jnp.int32, (blk, LANES), 1)
        stat = jnp.zeros((blk, LANES), F32)
        for hp in range(heads // 2):
            sl = slice(LANES * hp, LANES * (hp + 1))
            vcat = keys(vc_ref, vp_ref if has_prev else None, sl)
            pv_a = jnp.dot(p_scr[2 * hp], vcat, preferred_element_type=F32) * inv[2 * hp]
            pv_b = jnp.dot(p_scr[2 * hp + 1], vcat, preferred_element_type=F32) * inv[2 * hp + 1]
            o_ref[:, sl] = jnp.where(low, pv_a, pv_b)
            stat = jnp.where(lane == 2 * hp, lse[2 * hp], stat)
            stat = jnp.where(lane == 2 * hp + 1, lse[2 * hp + 1], stat)
        lse_ref[...] = stat

    lead = q.shape[:-2]
    rows = q.shape[-2]
    o, lse = pl.pallas_call(
        body, name=name, grid=grid,
        in_specs=[cur, cur, cur] + ([prev, prev] if has_prev else []),
        out_specs=[cur, cur_stat],
        out_shape=[jax.ShapeDtypeStruct(lead + (rows, a), F32), jax.ShapeDtypeStruct(lead + (rows, LANES), F32)],
        scratch_shapes=[pltpu.VMEM((heads, blk, nkeys), F32), pltpu.VMEM((heads, blk, nkeys), BF16)],
        compiler_params=_params(*["parallel"] * len(grid)),
    )(q, k, v, *([k, v] if has_prev else []))
    return o, lse


def _attn_combine(groups, head_spread, dims, *, name, tr=256):
    t = dims.tokens
    a = dims.n_heads * dims.head_dim
    dils = tuple(groups)

    def body(*refs):
        ins = refs[:2 * len(dils)]
        x_ref = refs[2 * len(dils)]
        o_ref = refs[2 * len(dils) + 1]
        lse_refs = refs[2 * len(dils) + 2:-2]
        scr, scr_stat = refs[-2], refs[-1]
        outs, stats = [], []
        for g, d in enumerate(dils):
            if d == 1:
                outs.append(ins[2 * g][...])
                stats.append(ins[2 * g + 1][...])
            else:
                outs.append(_residues_to_rows(ins[2 * g], scr, d))
                stats.append(_residues_to_rows(ins[2 * g + 1], scr_stat, d))
        top = functools.reduce(jnp.maximum, stats)
        weights = [jnp.exp(s - top) for s in stats]
        total = functools.reduce(jnp.add, weights)
        joint = top + jnp.log(total)
        inv = 1.0 / total
        acc = None
        for w, o in zip(weights, outs):
            term = _two_pass_dot(w * inv, x_ref[...]) * o
            acc = term if acc is None else acc + term
        o_ref[...] = acc.astype(BF16)
        for g, d in enumerate(dils):
            if d == 1:
                lse_refs[g][...] = joint
            else:
                _rows_to_residues(joint, lse_refs[g], scr_stat, d)

    in_specs, args, lse_specs, lse_shapes = [], [], [], []
    for d in dils:
        if d == 1:
            in_specs += [_row_spec(tr, a), _row_spec(tr, LANES)]
            lse_specs.append(_row_spec(tr, LANES))
            lse_shapes.append(jax.ShapeDtypeStruct((t, LANES), F32))
        else:
            in_specs += [_residue_spec(dims, d, tr, a), _residue_spec(dims, d, tr, LANES)]
            lse_specs.append(_residue_spec(dims, d, tr, LANES))
            lse_shapes.append(_residue_shape(dims, d, LANES, F32))
        args += list(groups[d])
    outs = pl.pallas_call(
        body, name=name, grid=(t // tr,),
        in_specs=in_specs + [pl.BlockSpec((LANES, a), lambda i: (0, 0))],
        out_specs=[_row_spec(tr, a)] + lse_specs,
        out_shape=[jax.ShapeDtypeStruct((t, a), BF16)] + lse_shapes,
        scratch_shapes=[pltpu.VMEM((a // LANES, tr, LANES), F32), pltpu.VMEM((1, tr, LANES), F32)],
        compiler_params=_params("parallel"),
    )(*args, head_spread)
    return outs[0], dict(zip(dils, outs[1:]))


def _attn_bwd_prep(do, o, head_sum, dims, *, name, tr=256):
    t, a = o.shape

    def body(do_ref, o_ref, e_ref, *rest):
        outs, scr, scr_stat = rest[:-2], rest[-2], rest[-1]
        dov = do_ref[...].astype(F32)
        delta = _two_pass_dot(dov * o_ref[...].astype(F32), e_ref[...])
        outs[0][...] = delta
        for g, d in enumerate(RESIDUE_DILATIONS):
            _rows_to_residues(dov, outs[1 + 2 * g], scr, d)
            _rows_to_residues(delta, outs[2 + 2 * g], scr_stat, d)

    out_specs, out_shape = [_row_spec(tr, LANES)], [jax.ShapeDtypeStruct((t, LANES), F32)]
    for d in RESIDUE_DILATIONS:
        out_specs += [_residue_spec(dims, d, tr, a), _residue_spec(dims, d, tr, LANES)]
        out_shape += [_residue_shape(dims, d, a, BF16), _residue_shape(dims, d, LANES, F32)]
    outs = pl.pallas_call(
        body, name=name, grid=(t // tr,),
        in_specs=[_row_spec(tr, a), _row_spec(tr, a), pl.BlockSpec((a, LANES), lambda i: (0, 0))],
        out_specs=out_specs, out_shape=out_shape,
        scratch_shapes=[pltpu.VMEM((a // LANES, tr, LANES), F32), pltpu.VMEM((1, tr, LANES), F32)],
        compiler_params=_params("parallel"),
    )(do, o, head_sum)
    dos, deltas = {1: do}, {1: outs[0]}
    for g, d in enumerate(RESIDUE_DILATIONS):
        dos[d], deltas[d] = outs[1 + 2 * g], outs[2 + 2 * g]
    return dos, deltas


def _attn_bwd(q, k, v, do, lse, delta, dims, dil, *, name):
    a = dims.n_heads * dims.head_dim
    heads, hd, blk = dims.n_heads, dims.head_dim, ATTN_BLOCK
    nb = dims.seq // dil // blk
    has_next = nb > 1
    nq = 2 * blk if has_next else blk
    grid, cur, _, nxt = _attn_specs(dims, dil, a)
    _, cur_stat, _, nxt_stat = _attn_specs(dims, dil, LANES)

    def body(*refs):
        k_ref, v_ref, q_ref, do_ref, lse_ref, dl_ref = refs[:6]
        if has_next:
            qn_ref, don_ref, lsen_ref, dln_ref = refs[6:10]
            dq_ref, dk_ref, dv_ref, s_scr, dp_scr, p_scr, ds_scr, carry = refs[10:]
        else:
            dq_ref, dk_ref, dv_ref, s_scr, dp_scr, p_scr, ds_scr = refs[6:]
        j = pl.program_id(len(grid) - 1)
        low, high = _pair_masks(hd)

        def stacked(ref, nref, sl):
            return jnp.concatenate([ref[:, sl], nref[:, sl]], axis=0) if has_next else ref[:, sl]

        def halves(x):
            return jnp.where(low, x, jnp.zeros_like(x)), jnp.where(high, x, jnp.zeros_like(x))

        for hp in range(heads // 2):
            sl = slice(LANES * hp, LANES * (hp + 1))
            k2, v2 = k_ref[:, sl], v_ref[:, sl]
            q_a, q_b = halves(stacked(q_ref, qn_ref if has_next else None, sl))
            do_a, do_b = halves(stacked(do_ref, don_ref if has_next else None, sl))
            s_scr[2 * hp], s_scr[2 * hp + 1] = _dot_nt(q_a, k2), _dot_nt(q_b, k2)
            dp_scr[2 * hp], dp_scr[2 * hp + 1] = _dot_nt(do_a, v2), _dot_nt(do_b, v2)

        rq = lax.broadcasted_iota(jnp.int32, (nq, blk), 0)
        jk = lax.broadcasted_iota(jnp.int32, (nq, blk), 1)
        if has_next:
            iq = jnp.where(rq < blk, rq, rq - blk)
            steps = jnp.where(rq < blk, iq - jk, iq - jk + blk)
            valid = ((rq < blk) & (iq >= jk)) | ((rq >= blk) & (jk >= iq) & (j + 1 < nb))
        else:
            steps, valid = rq - jk, rq >= jk
        bias = jnp.where(valid, steps.astype(F32) * (-float(dil)), -MASK_BIAS)
        lse_all = stacked(lse_ref, lsen_ref if has_next else None, slice(None))
        dl_all = stacked(dl_ref, dln_ref if has_next else None, slice(None))
        lse3 = jnp.stack([lse_all[:, h:h + 1] for h in range(heads)])
        dl3 = jnp.stack([dl_all[:, h:h + 1] for h in range(heads)])
        p = jnp.exp(s_scr[...] + _head_slopes(heads) * bias[None] - lse3)
        p_scr[...] = p.astype(BF16)
        ds_scr[...] = (p * (dp_scr[...] - dl3)).astype(BF16)

        if has_next:
            @pl.when(j == 0)
            def _():
                carry[...] = jnp.zeros_like(carry)

        for hp in range(heads // 2):
            sl = slice(LANES * hp, LANES * (hp + 1))
            k2 = k_ref[:, sl]
            q_a, q_b = halves(stacked(q_ref, qn_ref if has_next else None, sl))
            do_a, do_b = halves(stacked(do_ref, don_ref if has_next else None, sl))
            ds_a, ds_b = ds_scr[2 * hp], ds_scr[2 * hp + 1]
            dq2 = jnp.where(low, jnp.dot(ds_a, k2, preferred_element_type=F32),
                            jnp.dot(ds_b, k2, preferred_element_type=F32))
            dk_ref[:, sl] = _dot_tn(ds_a, q_a) + _dot_tn(ds_b, q_b)
            dv_ref[:, sl] = _dot_tn(p_scr[2 * hp], do_a) + _dot_tn(p_scr[2 * hp + 1], do_b)
            if has_next:
                dq_ref[:, sl] = carry[:, sl] + dq2[:blk]
                carry[:, sl] = dq2[blk:]
            else:
                dq_ref[:, sl] = dq2

    args, in_specs = [k, v, q, do, lse, delta], [cur] * 4 + [cur_stat] * 2
    if has_next:
        args += [q, do, lse, delta]
        in_specs += [nxt] * 2 + [nxt_stat] * 2
    shape = jax.ShapeDtypeStruct(q.shape, F32)
    scratch = [pltpu.VMEM((heads, nq, blk), F32)] * 2 + [pltpu.VMEM((heads, nq, blk), BF16)] * 2
    if has_next:
        scratch.append(pltpu.VMEM((blk, a), F32))
    return pl.pallas_call(
        body, name=name, grid=grid, in_specs=in_specs, out_specs=[cur] * 3, out_shape=[shape] * 3,
        scratch_shapes=scratch,
        compiler_params=_params(*["parallel"] * (len(grid) - 1), "arbitrary"),
    )(*args)


def _qkv_layouts_bwd(z, grads, gq, gk, head_ones, dims, *, name, tr=256):
    t = z.shape[0]
    a = dims.n_heads * dims.head_dim
    q_scale = dims.head_dim ** -0.5
    dils = tuple(grads)

    def body(q_ref, k_ref, *rest):
        d_refs = rest[:3 * len(dils)]
        gq_ref, gk_ref, sum_ref, spread_ref, dz_ref, dgq_ref, dgk_ref, scr = rest[3 * len(dils):]
        first = pl.program_id(0) == 0
        mean = lambda val: _two_pass_dot(_two_pass_dot(val, sum_ref[...]), spread_ref[...]) * (1.0 / dims.head_dim)

        def total(j):
            acc = None
            for g, d in enumerate(dils):
                ref = d_refs[3 * g + j]
                part = ref[...] if d == 1 else _residues_to_rows(ref, scr, d)
                acc = part if acc is None else acc + part
            return acc

        def norm_bwd(x_ref, dy, g_ref, scale, col, dg_ref):
            xv = x_ref[...]
            dy = dy * scale
            r = lax.rsqrt(mean(xv * xv) + RMS_EPS)
            gy = dy * g_ref[...]
            dx = r * gy - xv * (r * r * r) * mean(xv * gy)
            dz_ref[:, col * a:(col + 1) * a] = dx.astype(BF16)
            _accumulate(dg_ref, jnp.sum(dy * xv * r, axis=0, keepdims=True), first)

        norm_bwd(q_ref, total(0), gq_ref, q_scale, 0, dgq_ref)
        norm_bwd(k_ref, total(1), gk_ref, 1.0, 1, dgk_ref)
        dz_ref[:, 2 * a:3 * a] = total(2).astype(BF16)

    in_specs, args = [_row_spec(tr, a, 2), _row_spec(tr, a, 3)], [z, z]
    for d in dils:
        in_specs += [_row_spec(tr, a) if d == 1 else _residue_spec(dims, d, tr, a)] * 3
        args += list(grads[d])
    in_specs += [_vec_spec(a), _vec_spec(a), pl.BlockSpec((a, LANES), lambda i: (0, 0)),
                 pl.BlockSpec((LANES, a), lambda i: (0, 0))]
    return pl.pallas_call(
        body, name=name, grid=(t // tr,), in_specs=in_specs,
        out_specs=[_row_spec(tr, 3 * a), _vec_spec(a), _vec_spec(a)],
        out_shape=[jax.ShapeDtypeStruct((t, 3 * a), BF16)] + [jax.ShapeDtypeStruct((1, a), F32)] * 2,
        scratch_shapes=[pltpu.VMEM((a // LANES, tr, LANES), F32)],
        compiler_params=_params("arbitrary"),
    )(*args, gq, gk, *head_ones)


def _mix_fwd(ya, yb, z, gate_b, dims, *, name, tr=512):
    t, d = ya.shape
    tr = _pick(t, tr, 8)
    first_gate_col = z.shape[1] // d - 2

    def body(ya_ref, yb_ref, ga_ref, gb_ref, ba_ref, bb_ref, o_ref):
        g_a = _sigmoid(ga_ref[...] + ba_ref[...])
        g_b = _sigmoid(gb_ref[...] + bb_ref[...])
        o_ref[...] = (g_a * ya_ref[...] + g_b * yb_ref[...]).astype(BF16)

    return pl.pallas_call(
        body, name=name, grid=(t // tr,),
        in_specs=[_row_spec(tr, d), _row_spec(tr, d), _row_spec(tr, d, first_gate_col),
                  _row_spec(tr, d, first_gate_col + 1), _vec_spec(d, 0), _vec_spec(d, 1)],
        out_specs=_row_spec(tr, d), out_shape=jax.ShapeDtypeStruct((t, d), BF16),
        compiler_params=_params("parallel"),
    )(ya, yb, z, z, gate_b, gate_b)


def _mix_bwd(dmix, ya, yb, z, gate_b, dims, *, name, tr=512):
    t, d = ya.shape
    tr = _pick(t, tr, 8)
    first_gate_col = z.shape[1] // d - 2

    def body(dm_ref, ya_ref, yb_ref, ga_ref, gb_ref, ba_ref, bb_ref, dya_ref, dyb_ref, dz_ref, db_ref):
        dm = dm_ref[...].astype(F32)
        g_a = _sigmoid(ga_ref[...] + ba_ref[...])
        g_b = _sigmoid(gb_ref[...] + bb_ref[...])
        dya_ref[...] = (dm * g_a).astype(BF16)
        dyb_ref[...] = (dm * g_b).astype(BF16)
        dl_a = dm * ya_ref[...] * g_a * (1.0 - g_a)
        dl_b = dm * yb_ref[...] * g_b * (1.0 - g_b)
        dz_ref[:, 0:d] = dl_a.astype(BF16)
        dz_ref[:, d:2 * d] = dl_b.astype(BF16)
        first = pl.program_id(0) == 0
        sums = jnp.concatenate([jnp.sum(dl_a, axis=0, keepdims=True), jnp.sum(dl_b, axis=0, keepdims=True)], axis=1)
        _accumulate(db_ref, sums, first)

    return pl.pallas_call(
        body, name=name, grid=(t // tr,),
        in_specs=[_row_spec(tr, d), _row_spec(tr, d), _row_spec(tr, d), _row_spec(tr, d, first_gate_col),
                  _row_spec(tr, d, first_gate_col + 1), _vec_spec(d, 0), _vec_spec(d, 1)],
        out_specs=[_row_spec(tr, d), _row_spec(tr, d), _row_spec(tr, 2 * d), _vec_spec(2 * d)],
        out_shape=[jax.ShapeDtypeStruct((t, d), BF16)] * 2 + [jax.ShapeDtypeStruct((t, 2 * d), BF16),
                                                              jax.ShapeDtypeStruct((1, 2 * d), F32)],
        compiler_params=_params("arbitrary"),
    )(dmix, ya, yb, z, z, gate_b, gate_b)


def _loss_head(y, target, *, name, tr=512):
    t, d = y.shape
    tr = _pick(t, tr, 8)

    def body(y_ref, t_ref, dy_ref, dyb_ref, loss_ref):
        err = y_ref[...] - t_ref[...]
        dy = err * (1.0 / d)
        dy_ref[...] = dy
        dyb_ref[...] = dy.astype(BF16)
        part = jnp.sum(jnp.sum(err * err, axis=-1, keepdims=True), axis=0, keepdims=True) * (0.5 / d)
        _accumulate(loss_ref, jnp.broadcast_to(part, (8, 128)), pl.program_id(0) == 0)

    return pl.pallas_call(
        body, name=name, grid=(t // tr,),
        in_specs=[_row_spec(tr, d), _row_spec(tr, d)],
        out_specs=[_row_spec(tr, d), _row_spec(tr, d), pl.BlockSpec((8, 128), lambda i: (0, 0))],
        out_shape=[jax.ShapeDtypeStruct((t, d), F32), jax.ShapeDtypeStruct((t, d), BF16),
                   jax.ShapeDtypeStruct((8, 128), F32)],
        compiler_params=_params("arbitrary"),
    )(y, target)


def _adamw(w, grads, m, v, *, name, tr=256):
    r, c = w.shape
    tr = _pick(r, tr, 8)
    ng = len(grads)
    c1 = 1.0 - ADAM_B1 ** ADAM_STEP
    c2 = 1.0 - ADAM_B2 ** ADAM_STEP

    def body(*refs):
        w_ref, g_refs, m_ref, v_ref = refs[0], refs[1:1 + ng], refs[1 + ng], refs[2 + ng]
        g_out, d_out, m_out, v_out = refs[3 + ng:]
        g = g_refs[0][...]
        for extra in g_refs[1:]:
            g = g + extra[...]
        m_new = ADAM_B1 * m_ref[...] + (1.0 - ADAM_B1) * g
        v_new = ADAM_B2 * v_ref[...] + (1.0 - ADAM_B2) * (g * g)
        g_out[...] = g
        m_out[...] = m_new
        v_out[...] = v_new
        d_out[...] = -ADAM_LR * ((m_new / c1) / (jnp.sqrt(v_new / c2) + ADAM_EPS) + ADAM_WD * w_ref[...])

    spec = pl.BlockSpec((tr, c), lambda i: (i, 0))
    return pl.pallas_call(
        body, name=name, grid=(r // tr,),
        in_specs=[spec] * (3 + ng), out_specs=[spec] * 4, out_shape=[jax.ShapeDtypeStruct((r, c), F32)] * 4,
        compiler_params=_params("parallel"),
    )(w, *grads, m, v)


CHIP_PEERS = ((1, 0), (0, 1), (1, 1))


def _place():
    return lax.axis_index("x"), lax.axis_index("y"), lax.axis_index("c")


HBM = pl.BlockSpec(memory_space=pltpu.HBM)
SEM = pl.BlockSpec(memory_space=pltpu.SEMAPHORE)
IN_FLIGHT = pltpu.SideEffectType.DATAFLOW_SIDE_EFFECTING


def _in_hbm(a):
    return pltpu.with_memory_space_constraint(a, pltpu.HBM)


def _cast_to_lands(shards, dtypes, *, name):
    n = len(shards)

    def body(*refs):
        ins, outs, bufs, sems = refs[:n], refs[n:2 * n], refs[2 * n:3 * n], refs[3 * n]
        x, y, _ = _place()
        copies = []
        for a in range(n):
            bufs[a][...] = ins[a][...].astype(dtypes[a])
            cp = pltpu.make_async_copy(bufs[a], outs[a].at[2 * x + y], sems.at[a])
            cp.start()
            copies.append(cp)
        for cp in copies:
            cp.wait()

    return pl.pallas_call(
        body, name=name, in_specs=[pl.BlockSpec(memory_space=pltpu.VMEM)] * n, out_specs=[ANY] * n,
        out_shape=[jax.ShapeDtypeStruct((N_CHIPS,) + s.shape, dt) for s, dt in zip(shards, dtypes)],
        scratch_shapes=[pltpu.VMEM(s.shape, dt) for s, dt in zip(shards, dtypes)] + [pltpu.SemaphoreType.DMA((n,))],
        compiler_params=pltpu.CompilerParams(vmem_limit_bytes=V7X_VMEM_LIMIT_BYTES),
    )(*shards)


def _chip_copy(src, dst, send, recv, flip, place):
    x, y, c = place
    return pltpu.make_async_remote_copy(src_ref=src, dst_ref=dst, send_sem=send, recv_sem=recv,
                                        device_id=(x ^ flip[0], y ^ flip[1], c), device_id_type=MESH)


def _my_part(land, place, halved):
    block = land.at[2 * place[0] + place[1]]
    if not halved:
        return block
    rows = land.shape[1] // 2
    return block.at[pl.ds(pl.multiple_of(place[2] * rows, rows), rows)]


def _gather_start(lands, after, *, name, halved=()):
    n = len(lands)

    def body(*refs):
        ins, send, recv, token = refs[:n], refs[n + 1], refs[n + 2], refs[-1]
        place = _place()
        for a in range(n):
            part = _my_part(ins[a], place, a in halved)
            for p, flip in enumerate(CHIP_PEERS):
                k = 3 * a + p
                _chip_copy(part, part, send.at[k], recv.at[k], flip, place).start()
        token[...] = jnp.zeros_like(token)

    outs = pl.pallas_call(
        body, name=name, in_specs=[HBM] * n + [ANY],
        out_specs=(SEM, SEM, *[HBM] * n, pl.BlockSpec(memory_space=pltpu.VMEM)),
        out_shape=(pltpu.SemaphoreType.DMA((3 * n,)), pltpu.SemaphoreType.DMA((3 * n,)),
                   *[pltpu.HBM(l.shape, l.dtype) for l in lands], jax.ShapeDtypeStruct((8, 128), F32)),
        input_output_aliases={a: 2 + a for a in range(n)},
        compiler_params=pltpu.CompilerParams(has_side_effects=IN_FLIGHT),
    )(*[_in_hbm(l) for l in lands], after)
    return outs[0], outs[1], list(outs[2:2 + n]), outs[-1]


def _gather_wait(send, recv, lands, after, *, name, halved=()):
    n = len(lands)

    def body(*refs):
        ins, send_ref, recv_ref = refs[:n], refs[n], refs[n + 1]
        place = _place()
        for a in range(n):
            part = _my_part(ins[a], place, a in halved)
            for p, flip in enumerate(CHIP_PEERS):
                k = 3 * a + p
                cp = _chip_copy(part, part, send_ref.at[k], recv_ref.at[k], flip, place)
                cp.wait_send()
                cp.wait_recv()

    return pl.pallas_call(
        body, name=name, in_specs=[HBM] * n + [SEM, SEM, ANY], out_specs=[HBM] * n,
        out_shape=[pltpu.HBM(l.shape, l.dtype) for l in lands],
        input_output_aliases={a: a for a in range(n)},
        compiler_params=pltpu.CompilerParams(has_side_effects=IN_FLIGHT),
    )(*lands, send, recv, after)


def _forward_to_sibling(land, *, name):
    rows = land.shape[1] // 2

    def body(land_ref, out_ref, send, recv):
        x, y, c = _place()
        copies = []
        for p, (fx, fy) in enumerate(CHIP_PEERS):
            chip = 2 * (x ^ fx) + (y ^ fy)
            mine = pl.ds(pl.multiple_of(c * rows, rows), rows)
            theirs = pl.ds(pl.multiple_of((1 - c) * rows, rows), rows)
            out = pltpu.make_async_remote_copy(
                src_ref=land_ref.at[chip].at[mine], dst_ref=out_ref.at[chip].at[mine], send_sem=send.at[p],
                recv_sem=recv.at[p], device_id=(x, y, 1 - c), device_id_type=MESH)
            out.start()
            copies.append((out, pltpu.make_async_remote_copy(
                src_ref=land_ref.at[chip].at[theirs], dst_ref=out_ref.at[chip].at[theirs], send_sem=send.at[p],
                recv_sem=recv.at[p], device_id=(x, y, 1 - c), device_id_type=MESH)))
        for out, arriving in copies:
            out.wait_send()
            arriving.wait_recv()

    return pl.pallas_call(
        body, name=name, in_specs=[ANY], out_specs=ANY, out_shape=jax.ShapeDtypeStruct(land.shape, land.dtype),
        input_output_aliases={0: 0},
        scratch_shapes=[pltpu.SemaphoreType.DMA((3,)), pltpu.SemaphoreType.DMA((3,))],
    )(land)


def _scatter_start(grad, *, name):
    def body(g_ref, land_ref, send, recv, g_thru, land_thru, token):
        place = _place()
        for p, flip in enumerate(CHIP_PEERS):
            peer_chip = 2 * (place[0] ^ flip[0]) + (place[1] ^ flip[1])
            _chip_copy(g_ref.at[peer_chip], land_ref.at[p], send.at[p], recv.at[p], flip, place).start()
        token[...] = jnp.zeros_like(token)

    land = lax.empty((3,) + grad.shape[1:], grad.dtype)
    return pl.pallas_call(
        body, name=name, in_specs=[HBM, HBM],
        out_specs=(SEM, SEM, HBM, HBM, pl.BlockSpec(memory_space=pltpu.VMEM)),
        out_shape=(pltpu.SemaphoreType.DMA((3,)), pltpu.SemaphoreType.DMA((3,)), pltpu.HBM(grad.shape, grad.dtype),
                   pltpu.HBM(land.shape, land.dtype), jax.ShapeDtypeStruct((8, 128), F32)),
        input_output_aliases={0: 2, 1: 3},
        compiler_params=pltpu.CompilerParams(has_side_effects=IN_FLIGHT),
    )(_in_hbm(grad), _in_hbm(land))


def _scatter_wait(started, after, *, name):
    n = len(started)

    def body(*refs):
        grads, lands = refs[:n], refs[n:2 * n]
        sends, recvs = refs[2 * n:3 * n], refs[3 * n:4 * n]
        place = _place()
        for a in range(n):
            for p, flip in enumerate(CHIP_PEERS):
                cp = _chip_copy(grads[a].at[0], lands[a].at[p], sends[a].at[p], recvs[a].at[p], flip, place)
                cp.wait_send()
                cp.wait_recv()

    grads, lands = [s[2] for s in started], [s[3] for s in started]
    after = list(after) if isinstance(after, (list, tuple)) else [after]
    outs = pl.pallas_call(
        body, name=name, in_specs=[HBM] * (2 * n) + [SEM] * (2 * n) + [ANY] * len(after), out_specs=[HBM] * (2 * n),
        out_shape=[pltpu.HBM(a.shape, a.dtype) for a in grads + lands],
        input_output_aliases={a: a for a in range(2 * n)},
        compiler_params=pltpu.CompilerParams(has_side_effects=IN_FLIGHT),
    )(*grads, *lands, *[s[0] for s in started], *[s[1] for s in started], *after)
    return list(zip(outs[:n], outs[n:]))


def _sibling_copy(src, dst, send, recv, place):
    x, y, c = place
    return pltpu.make_async_remote_copy(src_ref=src, dst_ref=dst, send_sem=send, recv_sem=recv,
                                        device_id=(x, y, 1 - c), device_id_type=MESH)


def _swap_start(arrays, *, name):
    n = len(arrays)

    def body(*refs):
        ins, lands, send, recv, token = refs[:n], refs[n:2 * n], refs[2 * n], refs[2 * n + 1], refs[-1]
        place = _place()
        for a in range(n):
            _sibling_copy(ins[a], lands[a], send.at[a], recv.at[a], place).start()
        token[...] = jnp.zeros_like(token)

    both = [_in_hbm(a) for a in arrays] + [_in_hbm(lax.empty(a.shape, a.dtype)) for a in arrays]
    outs = pl.pallas_call(
        body, name=name, in_specs=[HBM] * (2 * n),
        out_specs=(SEM, SEM, *[HBM] * (2 * n), pl.BlockSpec(memory_space=pltpu.VMEM)),
        out_shape=(pltpu.SemaphoreType.DMA((n,)), pltpu.SemaphoreType.DMA((n,)),
                   *[pltpu.HBM(a.shape, a.dtype) for a in both], jax.ShapeDtypeStruct((8, 128), F32)),
        input_output_aliases={a: 2 + a for a in range(2 * n)},
        compiler_params=pltpu.CompilerParams(has_side_effects=IN_FLIGHT),
    )(*both)
    return outs[0], outs[1], list(outs[2:2 + n]), list(outs[2 + n:2 + 2 * n]), outs[-1]


def _swap_wait(started, after, *, name):
    send, recv, arrays, lands = started[:4]
    n = len(arrays)

    def body(*refs):
        ins, zones, send_ref, recv_ref = refs[:n], refs[n:2 * n], refs[2 * n], refs[2 * n + 1]
        place = _place()
        for a in range(n):
            cp = _sibling_copy(ins[a], zones[a], send_ref.at[a], recv_ref.at[a], place)
            cp.wait_send()
            cp.wait_recv()

    after = list(after) if isinstance(after, (list, tuple)) else [after]
    outs = pl.pallas_call(
        body, name=name, in_specs=[HBM] * (2 * n) + [SEM, SEM] + [ANY] * len(after), out_specs=[HBM] * (2 * n),
        out_shape=[pltpu.HBM(a.shape, a.dtype) for a in arrays + lands],
        input_output_aliases={a: a for a in range(2 * n)},
        compiler_params=pltpu.CompilerParams(has_side_effects=IN_FLIGHT),
    )(*arrays, *lands, send, recv, *after)
    return list(outs[n:])


def _allreduce_start(packed, *, name):
    n_dev = 8

    def body(src_ref, land_ref, send, recv, src_thru, land_thru, token):
        x, y, c = _place()
        me = 4 * x + 2 * y + c
        for p in range(1, n_dev):
            pltpu.make_async_remote_copy(
                src_ref=src_ref, dst_ref=land_ref.at[me], send_sem=send.at[p - 1], recv_sem=recv.at[p - 1],
                device_id=(x ^ (p >> 2), y ^ ((p >> 1) & 1), c ^ (p & 1)), device_id_type=MESH).start()
        token[...] = jnp.zeros_like(token)

    land = lax.empty((n_dev,) + packed.shape, packed.dtype)
    return pl.pallas_call(
        body, name=name, in_specs=[HBM, HBM],
        out_specs=(SEM, SEM, HBM, HBM, pl.BlockSpec(memory_space=pltpu.VMEM)),
        out_shape=(pltpu.SemaphoreType.DMA((n_dev - 1,)), pltpu.SemaphoreType.DMA((n_dev - 1,)),
                   pltpu.HBM(packed.shape, packed.dtype), pltpu.HBM(land.shape, land.dtype),
                   jax.ShapeDtypeStruct((8, 128), F32)),
        input_output_aliases={0: 2, 1: 3},
        compiler_params=pltpu.CompilerParams(has_side_effects=IN_FLIGHT),
    )(_in_hbm(packed), _in_hbm(land))


def _allreduce_wait(started, after, *, name):
    send, recv, packed, land = started[:4]
    n_dev = 8

    def body(src_ref, land_ref, send_ref, recv_ref, *_):
        x, y, c = _place()
        for p in range(1, n_dev):
            cp = pltpu.make_async_remote_copy(
                src_ref=src_ref, dst_ref=land_ref.at[0], send_sem=send_ref.at[p - 1], recv_sem=recv_ref.at[p - 1],
                device_id=(x ^ (p >> 2), y ^ ((p >> 1) & 1), c ^ (p & 1)), device_id_type=MESH)
            cp.wait_send()
            cp.wait_recv()

    after = list(after) if isinstance(after, (list, tuple)) else [after]
    return pl.pallas_call(
        body, name=name, in_specs=[HBM, HBM, SEM, SEM] + [ANY] * len(after), out_specs=[HBM, HBM],
        out_shape=[pltpu.HBM(packed.shape, packed.dtype), pltpu.HBM(land.shape, land.dtype)],
        input_output_aliases={0: 0, 1: 1},
        compiler_params=pltpu.CompilerParams(has_side_effects=IN_FLIGHT),
    )(packed, land, send, recv, *after)


def _sum_devices(mine, land, *, name):
    n_dev = land.shape[0]

    def body(mine_ref, land_ref, out_ref):
        x, y, c = _place()
        me = 4 * x + 2 * y + c
        total = None
        for s in range(n_dev):
            part = jnp.where(me == s, mine_ref[...], land_ref[s])
            total = part if total is None else total + part
        out_ref[...] = total

    return pl.pallas_call(body, name=name, out_shape=jax.ShapeDtypeStruct(mine.shape, mine.dtype))(mine, land)


def _sum_received(grad, land, *, name, tr=256):
    _, r, c = grad.shape
    tr = _pick(r, tr, 8)

    def body(chip_ref, g_ref, l_ref, o_ref):
        o_ref[...] = ((g_ref[...] + l_ref[0].astype(F32)) + l_ref[1].astype(F32)) + l_ref[2].astype(F32)

    chip = (2 * lax.axis_index("x") + lax.axis_index("y")).astype(jnp.int32).reshape(1)
    return pl.pallas_call(
        body, name=name,
        grid_spec=pltpu.PrefetchScalarGridSpec(
            num_scalar_prefetch=1, grid=(r // tr,),
            in_specs=[pl.BlockSpec((None, tr, c), lambda i, chip_ref: (chip_ref[0], i, 0)),
                      pl.BlockSpec((3, tr, c), lambda i, chip_ref: (0, i, 0))],
            out_specs=pl.BlockSpec((tr, c), lambda i, chip_ref: (i, 0))),
        out_shape=jax.ShapeDtypeStruct((r, c), F32), compiler_params=_params("parallel"),
    )(chip, grad, land)


def _allreduce_small(packed, *, name, after=None):
    r, d = packed.shape
    n_dev = 8

    def body(src_ref, out_ref, buf, send, recv):
        x, y, c = _place()
        me = 4 * x + 2 * y + c
        started = []
        for p in range(1, n_dev):
            rc = pltpu.make_async_remote_copy(
                src_ref=src_ref, dst_ref=buf.at[me], send_sem=send.at[p - 1], recv_sem=recv.at[p - 1],
                device_id=(x ^ (p >> 2), y ^ ((p >> 1) & 1), c ^ (p & 1)), device_id_type=MESH)
            rc.start()
            started.append(rc)
        buf[me] = src_ref[...]
        for rc in started:
            rc.wait()
        total = buf[0]
        for s in range(1, n_dev):
            total = total + buf[s]
        out_ref[...] = total

    vmem = pl.BlockSpec(memory_space=pltpu.VMEM)
    body, more_specs, more_args = _ordered(body, 1, after)
    return pl.pallas_call(
        body, name=name, in_specs=[vmem] + more_specs, out_specs=vmem, out_shape=jax.ShapeDtypeStruct((r, d), F32),
        scratch_shapes=[pltpu.VMEM((n_dev, r, d), F32), pltpu.SemaphoreType.DMA((n_dev - 1,)),
                        pltpu.SemaphoreType.DMA((n_dev - 1,))],
    )(packed, *more_args)


def _packed_rows(size, d):
    return -(-size // (8 * d)) * 8


def _pack_rows(arrays, d):
    rows = []
    for arr in arrays:
        flat = arr.reshape(-1).astype(F32)
        n = _packed_rows(flat.shape[0], d)
        rows.append(jnp.pad(flat, (0, n * d - flat.shape[0])).reshape(n, d))
    return jnp.concatenate(rows, axis=0)


def _unpack_rows(packed, shapes, d):
    out, row = [], 0
    for shape in shapes:
        size = math.prod(shape)
        n = _packed_rows(size, d)
        out.append(packed[row:row + n].reshape(-1)[:size].reshape(shape))
        row += n
    return out


SMALL = ("norm1_g", "gate_b", "conv_b", "conv_norm_g", "q_norm_g", "k_norm_g", "norm2_g", "ffn_conv_b")
LARGE = ("w_in", "w_conv_out", "w_attn_out", "w_out", "w_up", "w_down")
WEIGHTS = ("norm1_g", "w_in", "gate_b", "conv_w", "conv_b", "conv_norm_g", "w_conv_out", "q_norm_g", "k_norm_g",
           "w_attn_out", "w_out", "norm2_g", "w_up", "ffn_conv_w", "ffn_conv_b", "w_down")


def _head_ones(dims):
    a = dims.n_heads * dims.head_dim
    head = jnp.arange(a, dtype=jnp.int32) // dims.head_dim
    return (head[:, None] == head[None, :]).astype(BF16)


def _after(vec, token):
    return vec if token is None else vec + token[0:1, 0:1]


def _local_step(dims, x, target, small, first_weights, other_weights, send_grad):
    d, f, heads = dims.d_model, dims.d_ff, dims.n_heads
    small = dict(small)
    row = lambda name: small[name].reshape(1, -1)
    head_sum = _head_sum_matrix(dims)
    head_spread = jnp.transpose(head_sum)
    ones = (head_sum, head_spread)
    gq = jnp.tile(row("q_norm_g"), (1, heads))
    gk = jnp.tile(row("k_norm_g"), (1, heads))
    one_shard = lambda w: w.reshape(1, -1, w.shape[-1])

    h = _rmsnorm_fwd(x, row("norm1_g"), name="norm1")
    full = first_weights(h)
    w_in = full["w_in"]
    conv_w = jnp.pad(full["conv_w"], ((0, CONV_HALO - dims.conv_width), (0, 0)))
    ffn_w = jnp.pad(full["ffn_conv_w"], ((0, FFN_HALO - dims.ffn_conv_width), (0, 0)))
    z = _mm_nn(h, w_in, out_dtype=F32, after=full.get("token"), name="in_proj")
    a1, a3 = _conv_branch_fwd(z, conv_w, row("conv_b"), row("conv_norm_g"), dims, name="conv_branch")
    qkv = _qkv_layouts_fwd(z, gq, gk, ones, dims, name="qk_norm")
    per_group = {dil: _attn_fwd(*qkv[dil], dims, dil, name=f"attn_fwd_d{dil}") for dil in DILATIONS}
    o, lse = _attn_combine(per_group, head_spread, dims, name="attn_combine")
    full = other_weights(o)
    w_up = full["w_up"]
    w_co, w_ao, w_o, w_dn = (one_shard(full[k]) for k in ("w_conv_out", "w_attn_out", "w_out", "w_down"))
    ya = _mm_nn(a3, w_co, out_dtype=F32, name="conv_out_proj")
    yb = _mm_nn(o, w_ao, out_dtype=F32, name="attn_out_proj")
    mixed = _mix_fwd(ya, yb, z, row("gate_b"), dims, name="gate_mix")
    x1 = _mm_nn(mixed, w_o, out_dtype=F32, residual=x, name="out_proj")
    h2 = _rmsnorm_fwd(x1, row("norm2_g"), name="norm2")
    up = _mm_nn(h2, w_up, out_dtype=F32, name="up_proj")
    act = _ffn_act_fwd(up, ffn_w, row("ffn_conv_b"), dims, name="ffn_act")
    x2 = _mm_nn(act, w_dn, out_dtype=F32, residual=x1, name="down_proj")
    dy, dy_b, loss = _loss_head(x2, target, name="loss_head")

    grads = {}

    def large(name, g):
        grads[name], g_bf16 = g
        return send_grad(name, g_bf16)

    sent = large("w_down", _mm_tn(act, dy_b, n_shards=1, name="dw_down"))
    dact = _mm_nt(dy_b, w_dn, out_dtype=BF16, after=sent, name="d_act")
    du, dfw, dfb = _ffn_act_bwd(dact, up, ffn_w, row("ffn_conv_b"), dims, name="ffn_act_bwd")
    grads["ffn_conv_w"], grads["ffn_conv_b"] = dfw[:dims.ffn_conv_width], dfb
    dup = _ffn_conv_bwd(du, ffn_w, dims, name="ffn_conv_bwd")
    sent = large("w_up", _mm_tn(h2, dup, n_shards=N_CHIPS, name="dw_up"))
    dh2 = _mm_nt(dup, w_up, out_dtype=F32, after=sent, name="d_h2")
    dx1, dx1_b, grads["norm2_g"] = _rmsnorm_bwd(x1, row("norm2_g"), dh2, dy, want_bf16=True, name="norm2_bwd")
    sent = large("w_out", _mm_tn(mixed, dx1_b, n_shards=1, name="dw_out"))
    dmix = _mm_nt(dx1_b, w_o, out_dtype=F32, after=sent, name="d_mix")
    dya, dyb, dz_gate, grads["gate_b"] = _mix_bwd(dmix, ya, yb, z, row("gate_b"), dims, name="gate_mix_bwd")
    sent = large("w_conv_out", _mm_tn(a3, dya, n_shards=1, name="dw_conv_out"))
    da3 = _mm_nt(dya, w_co, out_dtype=F32, after=sent, name="d_conv_act")
    da1, grads["conv_norm_g"] = _conv_norm_bwd(da3, a1, row("conv_norm_g"), name="conv_norm_bwd")
    dz_glu, dcw, grads["conv_b"] = _conv_branch_bwd(da1, z, conv_w, dims, name="conv_branch_bwd")
    grads["conv_w"] = dcw[:dims.conv_width]
    sent = large("w_attn_out", _mm_tn(o, dyb, n_shards=1, name="dw_attn_out"))
    do = _mm_nt(dyb, w_ao, out_dtype=BF16, after=sent, name="d_attn")
    dos, deltas = _attn_bwd_prep(do, o, head_sum, dims, name="attn_bwd_prep")
    dqkv = {dil: _attn_bwd(*qkv[dil], dos[dil], lse[dil], deltas[dil], dims, dil, name=f"attn_bwd_d{dil}")
            for dil in DILATIONS}
    dz_qkv, dgq, dgk = _qkv_layouts_bwd(z, dqkv, gq, gk, ones, dims, name="qk_norm_bwd")
    grads["q_norm_g"] = dgq.reshape(heads, dims.head_dim).sum(axis=0)
    grads["k_norm_g"] = dgk.reshape(heads, dims.head_dim).sum(axis=0)
    dz = jnp.concatenate([dz_glu, dz_qkv, dz_gate], axis=1)
    sent = large("w_in", _mm_tn(h, dz, n_shards=N_CHIPS, name="dw_in"))
    dh = _mm_nt(dz, w_in, out_dtype=F32, after=sent, name="d_h")
    dx, grads["norm1_g"] = _rmsnorm_bwd(x, row("norm1_g"), dh, dx1, want_bf16=False, name="norm1_bwd")
    return loss, dx, grads


def _step(dims, x, target, w, m, v):
    d = dims.d_model
    t = dims.tokens
    sq = lambda a: a.reshape(a.shape[1:])
    w2, m2, v2 = ({k: sq(a) for k, a in grp.items()} for grp in (w, m, v))

    conv_pad = jnp.pad(w2["conv_w"], ((0, CONV_HALO - dims.conv_width), (0, 0)))
    ffn_pad = jnp.pad(w2["ffn_conv_w"], ((0, FFN_HALO - dims.ffn_conv_width), (0, 0)))
    gathered_names = LARGE + ("conv_w", "ffn_conv_w")
    lands = dict(zip(gathered_names, _cast_to_lands([w2[k] for k in LARGE] + [conv_pad, ffn_pad],
                                                   [BF16] * len(LARGE) + [F32, F32], name="cast_weights")))
    first_names = ("w_in", "conv_w", "ffn_conv_w")
    other_names = tuple(k for k in gathered_names if k not in first_names)
    first = _gather_start([lands[k] for k in first_names], x, halved=(0,), name="gather_start_first")
    other = []
    cols = lambda g, rows: jnp.moveaxis(g, 0, 1).reshape(g.shape[1], -1)[:rows]

    def first_weights(after):
        got = dict(zip(first_names, _gather_wait(*first[:3], after, halved=(0,), name="gather_wait_first")))
        got["w_in"] = _forward_to_sibling(got["w_in"], name="forward_w_in")
        other.extend(_gather_start([lands[k] for k in other_names], got["w_in"], name="gather_start_other"))
        got["conv_w"] = cols(got["conv_w"], dims.conv_width)
        got["ffn_conv_w"] = cols(got["ffn_conv_w"], dims.ffn_conv_width)
        got["token"] = other[3]
        return got

    def other_weights(after):
        return dict(zip(other_names, _gather_wait(*other[:3], after, name="gather_wait_other")))

    started = {}

    def send_grad(name, g):
        send, recv, g_thru, land, token = _scatter_start(g.reshape(N_CHIPS, -1, g.shape[-1]), name=f"scatter_start_{name}")
        started[name] = (send, recv, g_thru, land)
        return token

    small = {k: w2[k] for k in SMALL}
    small["norm1_g"] = _after(small["norm1_g"].reshape(1, -1), first[3])
    loss, dx, grads = _local_step(dims, x.reshape(t, d), target.reshape(t, d), small, first_weights, other_weights, send_grad)

    def my_sums(names, after, tag):
        arrived = _scatter_wait([started[k] for k in names], after, name=f"scatter_wait_{tag}")
        blocks = [grads[k].reshape(N_CHIPS, -1, grads[k].shape[-1]) for k in names]
        return [_sum_received(g, land, name=f"sum_{k}") for k, g, (_, land) in zip(names, blocks, arrived)]

    def updates(names, mine, theirs):
        return {k: _adamw(w2[k], [a, b], m2[k], v2[k], name=f"adamw_{k}") for k, a, b in zip(names, mine, theirs)}

    small_names = SMALL + ("conv_w", "ffn_conv_w")
    packed = _pack_rows([grads[k] for k in small_names] + [loss[0, 0]], d)
    reducing = _allreduce_start(packed, name="allreduce_start")
    others = [k for k in LARGE if k != "w_in"]
    mine_others = my_sums(others, [dx, reducing[4]], "others")
    swapping_others = _swap_start(mine_others, name="swap_start_others")
    reduced = _sum_devices(*_allreduce_wait(reducing, swapping_others[4], name="allreduce_wait"), name="allreduce_sum")
    shapes = [grads[k].shape for k in small_names] + [()]
    *small_g, loss_total = _unpack_rows(reduced, shapes, d)
    small_g = dict(zip(small_names, small_g))
    chip = 2 * lax.axis_index("x") + lax.axis_index("y")
    for k in ("conv_w", "ffn_conv_w"):
        width = w2[k].shape[1]
        small_g[k] = lax.dynamic_slice_in_dim(small_g[k], chip * width, width, axis=1)

    small_shapes = [w2[k].shape for k in small_names]
    pack = lambda grp: _pack_rows([grp[k] for k in small_names], d)
    results = _adamw(pack(w2), [pack(small_g)], pack(m2), pack(v2), name="adamw_small")
    unpacked = [_unpack_rows(r, small_shapes, d) for r in results]
    out = {k: tuple(u[i] for u in unpacked) for i, k in enumerate(small_names)}
    mine_w_in = my_sums(["w_in"], results[1], "w_in")
    swapping_w_in = _swap_start(mine_w_in, name="swap_start_w_in")
    out.update(updates(others, mine_others, _swap_wait(swapping_others, swapping_w_in[4], name="swap_wait_others")))
    last_updates = [out[k][1] for k in others]
    out.update(updates(["w_in"], mine_w_in, _swap_wait(swapping_w_in, last_updates, name="swap_wait_w_in")))

    lead =lambda a: a.reshape((1,) + a.shape)
    ordered = [[lead(out[k][j].reshape(w2[k].shape)) for k in WEIGHTS] for j in range(4)]
    return (loss_total, dx.reshape(x.shape), *ordered[0], *ordered[1], *ordered[2], *ordered[3])


def kernel(x, norm1_g, w_in, gate_b, conv_w, conv_b, conv_norm_g, w_conv_out, q_norm_g, k_norm_g, w_attn_out, w_out, norm2_g, w_up, ffn_conv_w, ffn_conv_b, w_down, loss_target, m_norm1_g, m_w_in, m_gate_b, m_conv_w, m_conv_b, m_conv_norm_g, m_w_conv_out, m_q_norm_g, m_k_norm_g, m_w_attn_out, m_w_out, m_norm2_g, m_w_up, m_ffn_conv_w, m_ffn_conv_b, m_w_down, v_norm1_g, v_w_in, v_gate_b, v_conv_w, v_conv_b, v_conv_norm_g, v_w_conv_out, v_q_norm_g, v_k_norm_g, v_w_attn_out, v_w_out, v_norm2_g, v_w_up, v_ffn_conv_w, v_ffn_conv_b, v_w_down):
    w = dict(zip(WEIGHTS, (norm1_g, w_in, gate_b, conv_w, conv_b, conv_norm_g, w_conv_out, q_norm_g, k_norm_g,
                           w_attn_out, w_out, norm2_g, w_up, ffn_conv_w, ffn_conv_b, w_down)))
    m = dict(zip(WEIGHTS, (m_norm1_g, m_w_in, m_gate_b, m_conv_w, m_conv_b, m_conv_norm_g, m_w_conv_out, m_q_norm_g,
                           m_k_norm_g, m_w_attn_out, m_w_out, m_norm2_g, m_w_up, m_ffn_conv_w, m_ffn_conv_b, m_w_down)))
    v = dict(zip(WEIGHTS, (v_norm1_g, v_w_in, v_gate_b, v_conv_w, v_conv_b, v_conv_norm_g, v_w_conv_out, v_q_norm_g,
                           v_k_norm_g, v_w_attn_out, v_w_out, v_norm2_g, v_w_up, v_ffn_conv_w, v_ffn_conv_b, v_w_down)))
    dims = Dims(d_model=x.shape[-1], batch_local=x.shape[0], seq=x.shape[1], d_ff=w_down.shape[1] * N_CHIPS)
    return _step(dims, x, loss_target, w, m, v)
```

```python
import functools
import math
from typing import NamedTuple

import jax
import jax.numpy as jnp
from jax import lax
from jax.experimental import pallas as pl
from jax.experimental.pallas import tpu as pltpu

F32 = jnp.float32
BF16 = jnp.bfloat16

RMS_EPS = 1e-6
MASKED_SCORE = -1e30
ATTN_BLOCK = 128
DILATIONS = (1, 4, 16)
CONV_HALO = 32
FFN_HALO = 8
ADAM_LR, ADAM_B1, ADAM_B2, ADAM_EPS, ADAM_WD, ADAM_STEP = 0.001, 0.9, 0.999, 1e-08, 0.01, 10
V7X_VMEM_LIMIT_BYTES = 56 * 2 ** 20
N_CHIPS = 4
MESH = pl.DeviceIdType.MESH


class Dims(NamedTuple):
    d_model: int = 1024
    n_heads: int = 16
    head_dim: int = 64
    d_ff: int = 2816
    seq: int = 2048
    batch_local: int = 2
    conv_width: int = 31
    ffn_conv_width: int = 3

    @property
    def tokens(self):
        return self.seq * self.batch_local


def _params(*semantics):
    return pltpu.CompilerParams(dimension_semantics=semantics, vmem_limit_bytes=V7X_VMEM_LIMIT_BYTES)


ANY = pl.BlockSpec(memory_space=pl.ANY)


def _ordered(body, n_inputs, after):
    after = [] if after is None else list(after) if isinstance(after, (list, tuple)) else [after]
    if not after:
        return body, [], []

    def wrapped(*refs):
        return body(*refs[:n_inputs], *refs[n_inputs + len(after):])

    return wrapped, [ANY] * len(after), after


def _pick(n, target, mult=128):
    if n <= target:
        return n
    best = None
    for t in range(mult, target + 1, mult):
        if n % t == 0:
            best = t
    assert best is not None, (n, target, mult)
    return best


def _sigmoid(v):
    return 1.0 / (1.0 + jnp.exp(-v))


def _mm_nn(a, w, *, out_dtype, name, residual=None, after=None, tm=1024, tn=1408, tk=2816):
    m, k = a.shape
    nsh, k2, c = w.shape
    assert k == k2 and a.dtype == BF16 and w.dtype == BF16
    n = nsh * c
    tm, tn, tk = _pick(m, tm, 8), _pick(c, tn), _pick(k, tk)
    nk, cpn = k // tk, c // tn

    def body(*refs):
        if residual is None:
            a_ref, w_ref, o_ref, acc = refs
        else:
            a_ref, w_ref, r_ref, o_ref, acc = refs
        prod = jnp.dot(a_ref[...], w_ref[...], preferred_element_type=F32)

        def finish(total):
            if residual is not None:
                total = total + r_ref[...]
            o_ref[...] = total.astype(out_dtype)

        if nk == 1:
            finish(prod)
        else:
            kk = pl.program_id(2)

            @pl.when(kk == 0)
            def _():
                acc[...] = prod

            @pl.when(kk > 0)
            def _():
                acc[...] += prod

            @pl.when(kk == nk - 1)
            def _():
                finish(acc[...])

    in_specs = [pl.BlockSpec((tm, tk), lambda i, j, kk: (i, kk)),
                pl.BlockSpec((None, tk, tn), lambda i, j, kk: (j // cpn, kk, j % cpn))]
    args = [a, w]
    if residual is not None:
        in_specs.append(pl.BlockSpec((tm, tn), lambda i, j, kk: (i, j)))
        args.append(residual)
    body, more_specs, more_args = _ordered(body, len(args), after)
    return pl.pallas_call(
        body, name=name, grid=(m // tm, n // tn, nk),
        in_specs=in_specs + more_specs, out_specs=pl.BlockSpec((tm, tn), lambda i, j, kk: (i, j)),
        out_shape=jax.ShapeDtypeStruct((m, n), out_dtype),
        scratch_shapes=[pltpu.VMEM((tm, tn) if nk > 1 else (8, 128), F32)],
        compiler_params=_params("parallel", "parallel", "arbitrary"),
    )(*args, *more_args)


def _mm_nt(a, w, *, out_dtype, name, after=None, tm=1024, tn=1408, tk=1792):
    m, k = a.shape
    nsh, r, c = w.shape
    assert k == nsh * c and a.dtype == BF16 and w.dtype == BF16
    tm, tn, tk = _pick(m, tm, 8), _pick(r, tn), _pick(c, tk)
    nk, cpk = k // tk, c // tk

    def body(a_ref, w_ref, o_ref, acc):
        prod = lax.dot_general(a_ref[...], w_ref[...], (((1,), (1,)), ((), ())), preferred_element_type=F32)
        if nk == 1:
            o_ref[...] = prod.astype(out_dtype)
        else:
            kk = pl.program_id(2)

            @pl.when(kk == 0)
            def _():
                acc[...] = prod

            @pl.when(kk > 0)
            def _():
                acc[...] += prod

            @pl.when(kk == nk - 1)
            def _():
                o_ref[...] = acc[...].astype(out_dtype)

    body, more_specs, more_args = _ordered(body, 2, after)
    return pl.pallas_call(
        body, name=name, grid=(m // tm, r // tn, nk),
        in_specs=[pl.BlockSpec((tm, tk), lambda i, j, kk: (i, kk)),
                  pl.BlockSpec((None, tn, tk), lambda i, j, kk: (kk // cpk, j, kk % cpk))] + more_specs,
        out_specs=pl.BlockSpec((tm, tn), lambda i, j, kk: (i, j)),
        out_shape=jax.ShapeDtypeStruct((m, r), out_dtype),
        scratch_shapes=[pltpu.VMEM((tm, tn) if nk > 1 else (8, 128), F32)],
        compiler_params=_params("parallel", "parallel", "arbitrary"),
    )(a, w, *more_args)


MM_TN_VMEM_BYTES = 44 * 2 ** 20


def _mm_tn(a, b, *, n_shards, name, tm=1408, tn=1408):
    t, m = a.shape
    t2, n = b.shape
    assert t == t2 and a.dtype == BF16 and b.dtype == BF16
    c = n // n_shards
    tm, tn = _pick(m, tm), _pick(c, tn)
    fixed = 2 * tm * tn * 6
    if 4 * t * (tm + tn) + fixed <= MM_TN_VMEM_BYTES:
        tk = t
    else:
        tk = _pick(t, (MM_TN_VMEM_BYTES - fixed - 4 * tm * tn) // (4 * (tm + tn)), 8)
    nk, cpn = t // tk, c // tn

    def body(a_ref, b_ref, o_ref, ob_ref, acc):
        kk = pl.program_id(2)
        prod = lax.dot_general(a_ref[...], b_ref[...], (((0,), (0,)), ((), ())), preferred_element_type=F32)

        def finish(total):
            o_ref[...] = total
            ob_ref[...] = total.astype(BF16)

        if nk == 1:
            finish(prod)
        else:
            @pl.when(kk == 0)
            def _():
                acc[...] = prod

            @pl.when(kk > 0)
            def _():
                acc[...] += prod

            @pl.when(kk == nk - 1)
            def _():
                finish(acc[...])

    out_spec = pl.BlockSpec((None, tm, tn), lambda i, j, kk: (j // cpn, i, j % cpn))
    return pl.pallas_call(
        body, name=name, grid=(m // tm, n // tn, nk),
        in_specs=[pl.BlockSpec((tk, tm), lambda i, j, kk: (kk, i)),
                  pl.BlockSpec((tk, tn), lambda i, j, kk: (kk, j))],
        out_specs=[out_spec, out_spec],
        out_shape=[jax.ShapeDtypeStruct((n_shards, m, c), F32), jax.ShapeDtypeStruct((n_shards, m, c), BF16)],
        scratch_shapes=[pltpu.VMEM((tm, tn) if nk > 1 else (8, 128), F32)],
        compiler_params=_params("parallel", "parallel", "arbitrary"),
    )(a, b)


def _row_spec(tr, width, col=0):
    return pl.BlockSpec((tr, width), lambda i, col=col: (i, col))


def _vec_spec(width, col=0):
    return pl.BlockSpec((1, width), lambda i, col=col: (0, col))


def _accumulate(ref, value, first):
    @pl.when(first)
    def _():
        ref[...] = value

    @pl.when(jnp.logical_not(first))
    def _():
        ref[...] += value


def _rmsnorm_fwd(x, g, *, name, tr=512):
    t, d = x.shape
    tr = _pick(t, tr, 8)

    def body(x_ref, g_ref, o_ref):
        xv = x_ref[...]
        r = lax.rsqrt(jnp.mean(xv * xv, axis=-1, keepdims=True) + RMS_EPS)
        o_ref[...] = (xv * r * g_ref[...]).astype(BF16)

    return pl.pallas_call(
        body, name=name, grid=(t // tr,),
        in_specs=[_row_spec(tr, d), _vec_spec(d)], out_specs=_row_spec(tr, d),
        out_shape=jax.ShapeDtypeStruct((t, d), BF16), compiler_params=_params("parallel"),
    )(x, g)


def _rmsnorm_bwd(x, g, dy, dres, *, name, want_bf16, tr=512):
    t, d = x.shape
    tr = _pick(t, tr, 8)

    def body(x_ref, g_ref, dy_ref, dres_ref, *outs):
        dx_ref, dg_ref = outs[0], outs[-1]
        xv, dyv = x_ref[...], dy_ref[...].astype(F32)
        r = lax.rsqrt(jnp.mean(xv * xv, axis=-1, keepdims=True) + RMS_EPS)
        gy = dyv * g_ref[...]
        dx = dres_ref[...] + r * gy - xv * (r * r * r) * jnp.mean(xv * gy, axis=-1, keepdims=True)
        dx_ref[...] = dx
        if want_bf16:
            outs[1][...] = dx.astype(BF16)
        _accumulate(dg_ref, jnp.sum(dyv * xv * r, axis=0, keepdims=True), pl.program_id(0) == 0)

    out_shape = [jax.ShapeDtypeStruct((t, d), F32)]
    out_specs = [_row_spec(tr, d)]
    if want_bf16:
        out_shape.append(jax.ShapeDtypeStruct((t, d), BF16))
        out_specs.append(_row_spec(tr, d))
    out_shape.append(jax.ShapeDtypeStruct((1, d), F32))
    out_specs.append(_vec_spec(d))
    return pl.pallas_call(
        body, name=name, grid=(t // tr,),
        in_specs=[_row_spec(tr, d), _vec_spec(d), _row_spec(tr, d), _row_spec(tr, d)],
        out_specs=out_specs, out_shape=out_shape, compiler_params=_params("arbitrary"),
    )(x, g, dy, dres)


def _head_mean(v, ones_ref, head_dim):
    hi = v.astype(BF16)
    lo = (v - hi.astype(F32)).astype(BF16)
    e = ones_ref[...]
    total = jnp.dot(hi, e, preferred_element_type=F32) + jnp.dot(lo, e, preferred_element_type=F32)
    return total * (1.0 / head_dim)


def _qkv_fwd(z, gq, gk, head_ones, dims, *, name, tr=256):
    t = z.shape[0]
    a = dims.n_heads * dims.head_dim
    tr = _pick(t, tr, 8)
    q_scale = dims.head_dim ** -0.5

    def body(q_ref, k_ref, v_ref, gq_ref, gk_ref, e_ref, qo_ref, ko_ref, vo_ref):
        qv, kv = q_ref[...], k_ref[...]
        rq = lax.rsqrt(_head_mean(qv * qv, e_ref, dims.head_dim) + RMS_EPS)
        rk = lax.rsqrt(_head_mean(kv * kv, e_ref, dims.head_dim) + RMS_EPS)
        qo_ref[...] = (qv * rq * gq_ref[...] * q_scale).astype(BF16)
        ko_ref[...] = (kv * rk * gk_ref[...]).astype(BF16)
        vo_ref[...] = v_ref[...].astype(BF16)

    return pl.pallas_call(
        body, name=name, grid=(t // tr,),
        in_specs=[_row_spec(tr, a, 2), _row_spec(tr, a, 3), _row_spec(tr, a, 4), _vec_spec(a), _vec_spec(a),
                  pl.BlockSpec((a, a), lambda i: (0, 0))],
        out_specs=[_row_spec(tr, a)] * 3, out_shape=[jax.ShapeDtypeStruct((t, a), BF16)] * 3,
        compiler_params=_params("parallel"),
    )(z, z, z, gq, gk, head_ones)


def _qkv_bwd(z, dqs, dks, dvs, gq, gk, head_ones, dims, *, name, tr=256):
    t = z.shape[0]
    a = dims.n_heads * dims.head_dim
    tr = _pick(t, tr, 8)
    q_scale = dims.head_dim ** -0.5
    ng = len(dqs)

    def body(*refs):
        q_ref, k_ref = refs[:2]
        dq_refs, dk_refs, dv_refs = refs[2:2 + ng], refs[2 + ng:2 + 2 * ng], refs[2 + 2 * ng:2 + 3 * ng]
        gq_ref, gk_ref, e_ref = refs[2 + 3 * ng:5 + 3 * ng]
        dz_ref, dgq_ref, dgk_ref = refs[5 + 3 * ng:]
        first = pl.program_id(0) == 0

        def norm_bwd(x_ref, d_refs, g_ref, scale, col, dg_ref):
            xv = x_ref[...]
            dy = sum(r[...] for r in d_refs) * scale
            r = lax.rsqrt(_head_mean(xv * xv, e_ref, dims.head_dim) + RMS_EPS)
            gy = dy * g_ref[...]
            dx = r * gy - xv * (r * r * r) * _head_mean(xv * gy, e_ref, dims.head_dim)
            dz_ref[:, col * a:(col + 1) * a] = dx.astype(BF16)
            _accumulate(dg_ref, jnp.sum(dy * xv * r, axis=0, keepdims=True), first)

        norm_bwd(q_ref, dq_refs, gq_ref, q_scale, 0, dgq_ref)
        norm_bwd(k_ref, dk_refs, gk_ref, 1.0, 1, dgk_ref)
        dz_ref[:, 2 * a:3 * a] = sum(r[...] for r in dv_refs).astype(BF16)

    in_specs = ([_row_spec(tr, a, 2), _row_spec(tr, a, 3)] + [_row_spec(tr, a)] * (3 * ng)
                + [_vec_spec(a), _vec_spec(a), pl.BlockSpec((a, a), lambda i: (0, 0))])
    return pl.pallas_call(
        body, name=name, grid=(t // tr,), in_specs=in_specs,
        out_specs=[_row_spec(tr, 3 * a), _vec_spec(a), _vec_spec(a)],
        out_shape=[jax.ShapeDtypeStruct((t, 3 * a), BF16)] + [jax.ShapeDtypeStruct((1, a), F32)] * 2,
        compiler_params=_params("arbitrary"),
    )(z, z, *dqs, *dks, *dvs, gq, gk, head_ones)


CONV_ROWS = 16


def _seq_specs(dims, ts, width, halo, col, *, nxt=False):
    nst, per = dims.seq // ts, ts // halo
    last = dims.tokens // halo - 1
    cur = pl.BlockSpec((ts, width), lambda b, i: (b * nst + i, col))
    if nxt:
        edge = pl.BlockSpec((halo, width), lambda b, i: (jnp.minimum((b * nst + i + 1) * per, last), col))
    else:
        edge = pl.BlockSpec((halo, width), lambda b, i: (jnp.maximum((b * nst + i) * per - 1, 0), col))
    return cur, edge


SUBLANES = 8


def _shifted_copies(buf, shifted):
    rows = shifted.shape[1]
    for s in range(1, SUBLANES):
        shifted[s - 1] = buf[pl.ds(s, rows), :]


def _window(buf, shifted, start, size):
    a, s = divmod(start, SUBLANES)
    src = buf if s == 0 else shifted.at[s - 1]
    return src[pl.ds(SUBLANES * a, size), :]


def _conv_branch_fwd(z, w, b, g, dims, *, name, ts=128):
    t, c, kw = z.shape[0], dims.d_model, dims.conv_width
    base = CONV_HALO - (kw - 1)

    def body(av_ref, hv_ref, ag_ref, hg_ref, w_ref, b_ref, g_ref, a1_ref, a3_ref, buf, shifted):
        i = pl.program_id(1)
        buf[CONV_HALO:, :] = av_ref[...].astype(F32) * _sigmoid(ag_ref[...].astype(F32))
        buf[0:CONV_HALO, :] = jnp.where(i > 0, hv_ref[...].astype(F32) * _sigmoid(hg_ref[...].astype(F32)), 0.0)
        _shifted_copies(buf, shifted)
        for r0 in range(0, ts, CONV_ROWS):
            acc = jnp.broadcast_to(b_ref[...], (CONV_ROWS, c))
            for k in range(kw):
                acc = acc + w_ref[k:k + 1, :] * _window(buf, shifted, r0 + base + k, CONV_ROWS)
            a1_ref[r0:r0 + CONV_ROWS, :] = acc
            a2 = acc * lax.rsqrt(jnp.mean(acc * acc, axis=-1, keepdims=True) + RMS_EPS) * g_ref[...]
            a3_ref[r0:r0 + CONV_ROWS, :] = (a2 * _sigmoid(a2)).astype(BF16)

    vec = pl.BlockSpec((1, c), lambda b, i: (0, 0))
    out = pl.BlockSpec((ts, c), lambda b, i: (b * (dims.seq // ts) + i, 0))
    return pl.pallas_call(
        body, name=name, grid=(dims.batch_local, dims.seq // ts),
        in_specs=[*_seq_specs(dims, ts, c, CONV_HALO, 0), *_seq_specs(dims, ts, c, CONV_HALO, 1),
                  pl.BlockSpec((CONV_HALO, c), lambda b, i: (0, 0)), vec, vec],
        out_specs=[out, out],
        out_shape=[jax.ShapeDtypeStruct((t, c), F32), jax.ShapeDtypeStruct((t, c), BF16)],
        scratch_shapes=[pltpu.VMEM((CONV_HALO + ts, c), F32),
                        pltpu.VMEM((SUBLANES - 1, CONV_HALO + ts - SUBLANES, c), F32)],
        compiler_params=_params("parallel", "parallel"),
    )(z, z, z, z, w, b, g)


def _conv_norm_bwd(da3, a1, g, *, name, tr=256):
    t, c = a1.shape
    tr = _pick(t, tr, 8)

    def body(d_ref, a_ref, g_ref, o_ref, dg_ref):
        a1v, gv = a_ref[...], g_ref[...]
        r = lax.rsqrt(jnp.mean(a1v * a1v, axis=-1, keepdims=True) + RMS_EPS)
        a2 = a1v * r * gv
        sg = _sigmoid(a2)
        da2 = d_ref[...].astype(F32) * sg * (1.0 + a2 * (1.0 - sg))
        gy = da2 * gv
        o_ref[...] = r * gy - a1v * (r * r * r) * jnp.mean(a1v * gy, axis=-1, keepdims=True)
        _accumulate(dg_ref, jnp.sum(da2 * a1v * r, axis=0, keepdims=True), pl.program_id(0) == 0)

    return pl.pallas_call(
        body, name=name, grid=(t // tr,),
        in_specs=[_row_spec(tr, c), _row_spec(tr, c), _vec_spec(c)],
        out_specs=[_row_spec(tr, c), _vec_spec(c)],
        out_shape=[jax.ShapeDtypeStruct((t, c), F32), jax.ShapeDtypeStruct((1, c), F32)],
        compiler_params=_params("arbitrary"),
    )(da3, a1, g)


def _conv_branch_bwd(da1, z, w, dims, *, name, ts=128):
    t, c, kw = z.shape[0], dims.d_model, dims.conv_width
    nst = dims.seq // ts
    base = CONV_HALO - (kw - 1)

    def body(d_ref, dn_ref, av_ref, hv_ref, ag_ref, hg_ref, w_ref, dz_ref, dw_ref, db_ref, abuf, dbuf, ashift, dshift):
        i = pl.program_id(1)
        first = jnp.logical_and(pl.program_id(0) == 0, i == 0)
        abuf[CONV_HALO:, :] = av_ref[...].astype(F32) * _sigmoid(ag_ref[...].astype(F32))
        abuf[0:CONV_HALO, :] = jnp.where(i > 0, hv_ref[...].astype(F32) * _sigmoid(hg_ref[...].astype(F32)), 0.0)
        d1 = d_ref[...]
        dbuf[0:ts, :] = d1
        dbuf[ts:, :] = jnp.where(i < nst - 1, dn_ref[...], 0.0)
        _shifted_copies(abuf, ashift)
        _shifted_copies(dbuf, dshift)

        @pl.when(first)
        def _():
            dw_ref[...] = jnp.zeros_like(dw_ref)
            db_ref[...] = jnp.zeros_like(db_ref)

        db_ref[...] += jnp.sum(d1, axis=0, keepdims=True)
        for k in range(kw):
            dw_ref[k:k + 1, :] += jnp.sum(d1 * _window(abuf, ashift, base + k, ts), axis=0, keepdims=True)
        for r0 in range(0, ts, CONV_ROWS):
            acc = jnp.zeros((CONV_ROWS, c), F32)
            for k in range(kw):
                acc = acc + w_ref[k:k + 1, :] * _window(dbuf, dshift, r0 + (kw - 1) - k, CONV_ROWS)
            av = av_ref[r0:r0 + CONV_ROWS, :].astype(F32)
            sg = _sigmoid(ag_ref[r0:r0 + CONV_ROWS, :].astype(F32))
            dz_ref[r0:r0 + CONV_ROWS, 0:c] = (acc * sg).astype(BF16)
            dz_ref[r0:r0 + CONV_ROWS, c:2 * c] = (acc * av * sg * (1.0 - sg)).astype(BF16)

    cur, nxt = _seq_specs(dims, ts, c, CONV_HALO, 0, nxt=True)
    return pl.pallas_call(
        body, name=name, grid=(dims.batch_local, nst),
        in_specs=[cur, nxt, *_seq_specs(dims, ts, c, CONV_HALO, 0), *_seq_specs(dims, ts, c, CONV_HALO, 1),
                  pl.BlockSpec((CONV_HALO, c), lambda b, i: (0, 0))],
        out_specs=[pl.BlockSpec((ts, 2 * c), lambda b, i: (b * nst + i, 0)),
                   pl.BlockSpec((CONV_HALO, c), lambda b, i: (0, 0)), pl.BlockSpec((1, c), lambda b, i: (0, 0))],
        out_shape=[jax.ShapeDtypeStruct((t, 2 * c), BF16), jax.ShapeDtypeStruct((CONV_HALO, c), F32),
                   jax.ShapeDtypeStruct((1, c), F32)],
        scratch_shapes=[pltpu.VMEM((CONV_HALO + ts, c), F32)] * 2
        + [pltpu.VMEM((SUBLANES - 1, CONV_HALO + ts - SUBLANES, c), F32)] * 2,
        compiler_params=_params("arbitrary", "arbitrary"),
    )(da1, da1, z, z, z, z, w)


FFN_ROWS = 16
FFN_COLS = 256


def _ffn_chunks(ts, f):
    cw = _pick(f, FFN_COLS)
    return [(r0, c0, cw) for r0 in range(0, ts, FFN_ROWS) for c0 in range(0, f, cw)]


def _ffn_conv(buf, w_ref, b_ref, r0, cols, kw):
    base = FFN_HALO - (kw - 1)
    u = jnp.broadcast_to(b_ref[:, cols], (FFN_ROWS, cols.stop - cols.start))
    for k in range(kw):
        u = u + w_ref[k:k + 1, cols] * buf[pl.ds(r0 + base + k, FFN_ROWS), cols]
    return u


def _ffn_act_fwd(up, w, b, dims, *, name, ts=128):
    t, f, kw = up.shape[0], dims.d_ff, dims.ffn_conv_width

    def body(up_ref, h_ref, w_ref, b_ref, o_ref, buf):
        buf[FFN_HALO:, :] = up_ref[...]
        buf[0:FFN_HALO, :] = jnp.where(pl.program_id(1) > 0, h_ref[...], 0.0)
        for r0, c0, cw in _ffn_chunks(ts, f):
            uv = _ffn_conv(buf, w_ref, b_ref, r0, slice(c0, c0 + cw), kw)
            ug = _ffn_conv(buf, w_ref, b_ref, r0, slice(f + c0, f + c0 + cw), kw)
            o_ref[r0:r0 + FFN_ROWS, c0:c0 + cw] = (ug * _sigmoid(ug) * uv).astype(BF16)

    full = lambda rows: pl.BlockSpec((rows, 2 * f), lambda b_, i: (0, 0))
    return pl.pallas_call(
        body, name=name, grid=(dims.batch_local, dims.seq // ts),
        in_specs=[*_seq_specs(dims, ts, 2 * f, FFN_HALO, 0), full(FFN_HALO), full(1)],
        out_specs=pl.BlockSpec((ts, f), lambda b_, i: (b_ * (dims.seq // ts) + i, 0)),
        out_shape=jax.ShapeDtypeStruct((t, f), BF16),
        scratch_shapes=[pltpu.VMEM((FFN_HALO + ts, 2 * f), F32)],
        compiler_params=_params("parallel", "parallel"),
    )(up, up, w, b)


def _ffn_act_bwd(dact, up, w, b, dims, *, name, ts=128):
    t, f, kw = up.shape[0], dims.d_ff, dims.ffn_conv_width
    base = FFN_HALO - (kw - 1)

    def body(d_ref, up_ref, h_ref, w_ref, b_ref, du_ref, dw_ref, db_ref, buf):
        i = pl.program_id(1)
        first = jnp.logical_and(pl.program_id(0) == 0, i == 0)
        buf[FFN_HALO:, :] = up_ref[...]
        buf[0:FFN_HALO, :] = jnp.where(i > 0, h_ref[...], 0.0)
        for r0, c0, cw in _ffn_chunks(ts, f):
            vcols, gcols = slice(c0, c0 + cw), slice(f + c0, f + c0 + cw)
            uv = _ffn_conv(buf, w_ref, b_ref, r0, vcols, kw)
            ug = _ffn_conv(buf, w_ref, b_ref, r0, gcols, kw)
            d = d_ref[r0:r0 + FFN_ROWS, vcols].astype(F32)
            sg = _sigmoid(ug)
            du_ref[r0:r0 + FFN_ROWS, vcols] = d * ug * sg
            du_ref[r0:r0 + FFN_ROWS, gcols] = d * uv * sg * (1.0 + ug * (1.0 - sg))

        @pl.when(first)
        def _():
            dw_ref[...] = jnp.zeros_like(dw_ref)
            db_ref[...] = jnp.zeros_like(db_ref)

        du = du_ref[...]
        db_ref[...] += jnp.sum(du, axis=0, keepdims=True)
        for k in range(kw):
            dw_ref[k:k + 1, :] += jnp.sum(du * buf[pl.ds(base + k, ts), :], axis=0, keepdims=True)

    nst = dims.seq // ts
    full = lambda rows: pl.BlockSpec((rows, 2 * f), lambda b_, i: (0, 0))
    return pl.pallas_call(
        body, name=name, grid=(dims.batch_local, nst),
        in_specs=[pl.BlockSpec((ts, f), lambda b_, i: (b_ * nst + i, 0)),
                  *_seq_specs(dims, ts, 2 * f, FFN_HALO, 0), full(FFN_HALO), full(1)],
        out_specs=[pl.BlockSpec((ts, 2 * f), lambda b_, i: (b_ * nst + i, 0)), full(FFN_HALO), full(1)],
        out_shape=[jax.ShapeDtypeStruct((t, 2 * f), F32), jax.ShapeDtypeStruct((FFN_HALO, 2 * f), F32),
                   jax.ShapeDtypeStruct((1, 2 * f), F32)],
        scratch_shapes=[pltpu.VMEM((FFN_HALO + ts, 2 * f), F32)],
        compiler_params=_params("arbitrary", "arbitrary"),
    )(dact, up, up, w, b)


def _ffn_conv_bwd(du, w, dims, *, name, ts=128):
    t, f2 = du.shape
    kw = dims.ffn_conv_width
    nst = dims.seq // ts

    def body(d_ref, dn_ref, w_ref, o_ref, buf):
        buf[0:ts, :] = d_ref[...]
        buf[ts:, :] = jnp.where(pl.program_id(1) < nst - 1, dn_ref[...], 0.0)
        for r0, c0, cw in _ffn_chunks(ts, f2):
            cols = slice(c0, c0 + cw)
            acc = jnp.zeros((FFN_ROWS, cw), F32)
            for k in range(kw):
                acc = acc + w_ref[k:k + 1, cols] * buf[pl.ds(r0 + (kw - 1) - k, FFN_ROWS), cols]
            o_ref[r0:r0 + FFN_ROWS, cols] = acc.astype(BF16)

    return pl.pallas_call(
        body, name=name, grid=(dims.batch_local, nst),
        in_specs=[*_seq_specs(dims, ts, f2, FFN_HALO, 0, nxt=True), pl.BlockSpec((FFN_HALO, f2), lambda b_, i: (0, 0))],
        out_specs=pl.BlockSpec((ts, f2), lambda b_, i: (b_ * nst + i, 0)),
        out_shape=jax.ShapeDtypeStruct((t, f2), BF16),
        scratch_shapes=[pltpu.VMEM((ts + FFN_HALO, f2), F32)],
        compiler_params=_params("parallel", "parallel"),
    )(du, du, w)


def _alibi_slope(h, n_heads):
    return 2.0 ** (-8.0 * (h + 1) / n_heads)


def _dot_nt(a, b):
    return lax.dot_general(a, b, (((1,), (1,)), ((), ())), preferred_element_type=F32)


def _dot_tn(a, b):
    return lax.dot_general(a, b, (((0,), (0,)), ((), ())), preferred_element_type=F32)


def _attn_view(x, dims, dil):
    return x.reshape(dims.batch_local, dims.seq // dil, dil * x.shape[-1])


def _attn_fwd_group(q, k, v, state, dims, dil, *, last, name):
    t, a = q.shape
    assert 2 * dims.head_dim == 128 and dims.n_heads % 2 == 0
    blk, hd = ATTN_BLOCK, dims.head_dim
    nb = dims.seq // dil // blk
    has_prev = nb > 1
    nkeys = 2 * blk if has_prev else blk

    def body(*refs):
        it = iter(refs)
        q_ref, kc_ref, vc_ref = next(it), next(it), next(it)
        kp_ref, vp_ref = (next(it), next(it)) if has_prev else (None, None)
        m_in, l_in, acc_in = (next(it), next(it), next(it)) if state is not None else (None, None, None)
        outs = list(it)
        iq = lax.broadcasted_iota(jnp.int32, (blk, nkeys), 0)
        jk = lax.broadcasted_iota(jnp.int32, (blk, nkeys), 1)
        if has_prev:
            steps = iq + blk - jk
            valid = (steps >= 0) & (steps <= blk) & ((jk >= blk) | (pl.program_id(2) > 0))
        else:
            steps = iq - jk
            valid = steps >= 0
        dist = steps.astype(F32) * float(dil)
        low = lax.broadcasted_iota(jnp.int32, (blk, 2 * hd), 1) < hd
        for hp in range(dims.n_heads // 2):
            sl = slice(2 * hd * hp, 2 * hd * (hp + 1))
            q2 = q_ref[:, sl]
            if has_prev:
                kcat = jnp.concatenate([kp_ref[:, sl], kc_ref[:, sl]], axis=0)
                vcat = jnp.concatenate([vp_ref[:, sl], vc_ref[:, sl]], axis=0)
            else:
                kcat, vcat = kc_ref[:, sl], vc_ref[:, sl]
            halves = []
            for half in range(2):
                col = 2 * hd * hp + hd * half
                qh = jnp.where(low if half == 0 else jnp.logical_not(low), q2, jnp.zeros_like(q2))
                sc = _dot_nt(qh, kcat) - _alibi_slope(2 * hp + half, dims.n_heads) * dist
                sc = jnp.where(valid, sc, MASKED_SCORE)
                row_max = jnp.max(sc, axis=-1, keepdims=True)
                if state is None:
                    m_new = row_max
                    p = jnp.exp(sc - m_new)
                    alpha = None
                    l_new = jnp.sum(p, axis=-1, keepdims=True)
                else:
                    m_old = m_in[:, col:col + 1]
                    m_new = jnp.maximum(m_old, row_max)
                    p = jnp.exp(sc - m_new)
                    alpha = jnp.exp(m_old - m_new)
                    l_new = alpha * l_in[:, col:col + 1] + jnp.sum(p, axis=-1, keepdims=True)
                pv = jnp.dot(p.astype(BF16), vcat, preferred_element_type=F32)
                halves.append((m_new, l_new, alpha, pv))
            (m_a, l_a, al_a, pv_a), (m_b, l_b, al_b, pv_b) = halves
            if state is None:
                acc = jnp.where(low, pv_a, pv_b)
            else:
                old = acc_in[:, sl]
                acc = jnp.where(low, al_a * old + pv_a, al_b * old + pv_b)
            m2 = jnp.where(low, m_a, m_b)
            l2 = jnp.where(low, l_a, l_b)
            if last:
                outs[0][:, sl] = (acc / l2).astype(BF16)
                outs[1][:, sl] = m2 + jnp.log(l2)
            else:
                outs[0][:, sl] = m2
                outs[1][:, sl] = l2
                outs[2][:, sl] = acc

    cur = pl.BlockSpec((None, blk, a), lambda b, r, i: (b, i, r))
    prev = pl.BlockSpec((None, blk, a), lambda b, r, i: (b, jnp.maximum(i - 1, 0), r))
    args, in_specs = [q, k, v], [cur, cur, cur]
    if has_prev:
        args += [k, v]
        in_specs += [prev, prev]
    if state is not None:
        args += list(state)
        in_specs += [cur] * 3
    shape = lambda dt: jax.ShapeDtypeStruct((dims.batch_local, dims.seq // dil, dil * a), dt)
    out_shape = [shape(BF16), shape(F32)] if last else [shape(F32)] * 3
    outs = pl.pallas_call(
        body, name=name, grid=(dims.batch_local, dil, nb),
        in_specs=in_specs, out_specs=[cur] * len(out_shape), out_shape=out_shape,
        compiler_params=_params("parallel", "parallel", "parallel"),
    )(*[_attn_view(x, dims, dil) for x in args])
    return tuple(o.reshape(t, a) for o in outs)


def _attn_delta(do, o, head_ones, dims, *, name, tr=512):
    t, a = o.shape
    tr = _pick(t, tr, 8)

    def body(do_ref, o_ref, e_ref, d_ref):
        prod = do_ref[...].astype(F32) * o_ref[...].astype(F32)
        d_ref[...] = _head_mean(prod, e_ref, dims.head_dim) * float(dims.head_dim)

    return pl.pallas_call(
        body, name=name, grid=(t // tr,),
        in_specs=[_row_spec(tr, a), _row_spec(tr, a), pl.BlockSpec((a, a), lambda i: (0, 0))],
        out_specs=_row_spec(tr, a), out_shape=jax.ShapeDtypeStruct((t, a), F32),
        compiler_params=_params("parallel"),
    )(do, o, head_ones)


def _attn_bwd_group(q, k, v, do, lse, delta, dims, dil, *, name):
    t, a = q.shape
    blk, hd = ATTN_BLOCK, dims.head_dim
    nb = dims.seq // dil // blk
    has_next = nb > 1

    def body(*refs):
        k_ref, v_ref, q_ref, do_ref, lse_ref, dl_ref = refs[:6]
        if has_next:
            qn_ref, don_ref, lsen_ref, dln_ref = refs[6:10]
            dq_ref, dk_ref, dv_ref, carry = refs[10:]
        else:
            dq_ref, dk_ref, dv_ref = refs[6:]
        j = pl.program_id(2)
        iq = lax.broadcasted_iota(jnp.int32, (blk, blk), 0)
        jk = lax.broadcasted_iota(jnp.int32, (blk, blk), 1)
        low = lax.broadcasted_iota(jnp.int32, (blk, 2 * hd), 1) < hd

        def pair(hp, qr, dor, lser, dlr, steps, valid):
            sl = slice(2 * hd * hp, 2 * hd * (hp + 1))
            q2, do2, k2, v2 = qr[:, sl], dor[:, sl], k_ref[:, sl], v_ref[:, sl]
            dist = steps.astype(F32) * float(dil)
            dq_h, dk2, dv2 = [], None, None
            for half in range(2):
                col = 2 * hd * hp + hd * half
                mask = low if half == 0 else jnp.logical_not(low)
                qh = jnp.where(mask, q2, jnp.zeros_like(q2))
                doh = jnp.where(mask, do2, jnp.zeros_like(do2))
                sc = _dot_nt(qh, k2) - _alibi_slope(2 * hp + half, dims.n_heads) * dist
                p = jnp.where(valid, jnp.exp(sc - lser[:, col:col + 1]), 0.0)
                ds = p * (_dot_nt(doh, v2) - dlr[:, col:col + 1])
                ds_b, p_b = ds.astype(BF16), p.astype(BF16)
                dq_h.append(jnp.dot(ds_b, k2, preferred_element_type=F32))
                dk_h, dv_h = _dot_tn(ds_b, qh), _dot_tn(p_b, doh)
                dk2 = dk_h if dk2 is None else dk2 + dk_h
                dv2 = dv_h if dv2 is None else dv2 + dv_h
            return sl, jnp.where(low, dq_h[0], dq_h[1]), dk2, dv2

        if has_next:
            @pl.when(j == 0)
            def _():
                carry[...] = jnp.zeros_like(carry)

        for hp in range(dims.n_heads // 2):
            sl, dq2, dk2, dv2 = pair(hp, q_ref, do_ref, lse_ref, dl_ref, iq - jk, iq >= jk)
            dq_ref[:, sl] = (carry[:, sl] + dq2) if has_next else dq2
            dk_ref[:, sl] = dk2
            dv_ref[:, sl] = dv2

        if has_next:
            @pl.when(j + 1 < nb)
            def _():
                for hp in range(dims.n_heads // 2):
                    sl, dq2, dk2, dv2 = pair(hp, qn_ref, don_ref, lsen_ref, dln_ref, iq - jk + blk, jk >= iq)
                    carry[:, sl] = dq2
                    dk_ref[:, sl] += dk2
                    dv_ref[:, sl] += dv2

    cur = pl.BlockSpec((None, blk, a), lambda b, r, j: (b, j, r))
    nxt = pl.BlockSpec((None, blk, a), lambda b, r, j: (b, jnp.minimum(j + 1, nb - 1), r))
    args, in_specs = [k, v, q, do, lse, delta], [cur] * 6
    if has_next:
        args += [q, do, lse, delta]
        in_specs += [nxt] * 4
    shape = jax.ShapeDtypeStruct((dims.batch_local, dims.seq // dil, dil * a), F32)
    outs = pl.pallas_call(
        body, name=name, grid=(dims.batch_local, dil, nb),
        in_specs=in_specs, out_specs=[cur] * 3, out_shape=[shape] * 3,
        scratch_shapes=[pltpu.VMEM((blk, a), F32)] if has_next else [],
        compiler_params=_params("parallel", "parallel", "arbitrary"),
    )(*[_attn_view(x, dims, dil) for x in args])
    return tuple(o.reshape(t, a) for o in outs)


LANES = 128
MASK_BIAS = 1e30
RESIDUE_DILATIONS = tuple(d for d in DILATIONS if d > 1)


def _rows_to_residues(value, out_ref, scr, d):
    rows, width = value.shape
    for c in range(width // LANES):
        cols = slice(LANES * c, LANES * (c + 1))
        scr[c] = value[:, cols]
        for r in range(d):
            out_ref[r, :, cols] = scr[c, pl.ds(r, rows // d, stride=d), :].astype(out_ref.dtype)


def _residues_to_rows(in_ref, scr, d):
    _, n, width = in_ref.shape
    slabs = []
    for c in range(width // LANES):
        cols = slice(LANES * c, LANES * (c + 1))
        for r in range(d):
            scr[c, pl.ds(r, n, stride=d), :] = in_ref[r, :, cols].astype(F32)
        slabs.append(scr[c])
    return slabs[0] if len(slabs) == 1 else jnp.concatenate(slabs, axis=1)


def _residue_shape(dims, d, width, dtype):
    return jax.ShapeDtypeStruct((dims.batch_local, d, dims.seq // d, width), dtype)


def _residue_spec(dims, d, tr, width):
    tiles = dims.seq // tr
    return pl.BlockSpec((None, d, tr // d, width), lambda i: (i // tiles, 0, i % tiles, 0))


def _head_sum_matrix(dims):
    a = dims.n_heads * dims.head_dim
    head = jnp.arange(a, dtype=jnp.int32) // dims.head_dim
    return (head[:, None] == jnp.arange(LANES, dtype=jnp.int32)[None, :]).astype(BF16)


def _two_pass_dot(v, m):
    hi = v.astype(BF16)
    lo = (v - hi.astype(F32)).astype(BF16)
    return jnp.dot(hi, m, preferred_element_type=F32) + jnp.dot(lo, m, preferred_element_type=F32)


def _qkv_layouts_fwd(z, gq, gk, head_ones, dims, *, name, tr=256):
    t = z.shape[0]
    a = dims.n_heads * dims.head_dim
    q_scale = dims.head_dim ** -0.5
    nres = len(RESIDUE_DILATIONS)

    def body(q_ref, k_ref, v_ref, gq_ref, gk_ref, sum_ref, spread_ref, *rest):
        outs, scr = rest[:-1], rest[-1]
        qv, kv = q_ref[...].astype(F32), k_ref[...].astype(F32)
        mean = lambda val: _two_pass_dot(_two_pass_dot(val, sum_ref[...]), spread_ref[...]) * (1.0 / dims.head_dim)
        rq = lax.rsqrt(mean(qv * qv) + RMS_EPS)
        rk = lax.rsqrt(mean(kv * kv) + RMS_EPS)
        values = (qv * rq * gq_ref[...] * q_scale, kv * rk * gk_ref[...], v_ref[...].astype(F32))
        for j, val in enumerate(values):
            outs[j][...] = val.astype(BF16)
            for g, d in enumerate(RESIDUE_DILATIONS):
                _rows_to_residues(val, outs[3 * (g + 1) + j], scr, d)

    out_specs = [_row_spec(tr, a)] * 3
    out_shape = [jax.ShapeDtypeStruct((t, a), BF16)] * 3
    for d in RESIDUE_DILATIONS:
        out_specs += [_residue_spec(dims, d, tr, a)] * 3
        out_shape += [_residue_shape(dims, d, a, BF16)] * 3
    outs = pl.pallas_call(
        body, name=name, grid=(t // tr,),
        in_specs=[_row_spec(tr, a, 2), _row_spec(tr, a, 3), _row_spec(tr, a, 4), _vec_spec(a), _vec_spec(a),
                  pl.BlockSpec((a, LANES), lambda i: (0, 0)), pl.BlockSpec((LANES, a), lambda i: (0, 0))],
        out_specs=out_specs, out_shape=out_shape,
        scratch_shapes=[pltpu.VMEM((a // LANES, tr, LANES), F32)],
        compiler_params=_params("parallel"),
    )(z, z, z, gq, gk, *head_ones)
    return {d: tuple(outs[3 * g:3 * g + 3]) for g, d in enumerate((1,) + RESIDUE_DILATIONS)}


def _attn_specs(dims, dil, width):
    blk = ATTN_BLOCK
    nb = dims.seq // dil // blk
    if dil == 1:
        grid = (dims.batch_local, nb)
        at = lambda f: pl.BlockSpec((blk, width), lambda b, i: (b * nb + f(i), 0))
    else:
        grid = (dims.batch_local, dil, nb)
        at = lambda f: pl.BlockSpec((None, None, blk, width), lambda b, r, i: (b, r, f(i), 0))
    return grid, at(lambda i: i), at(lambda i: jnp.maximum(i - 1, 0)), at(lambda i: jnp.minimum(i + 1, nb - 1))


def _head_slopes(n_heads):
    h = lax.broadcasted_iota(jnp.int32, (n_heads, 1, 1), 0).astype(F32)
    return jnp.exp((h + 1.0) * (-8.0 / n_heads * math.log(2.0)))


def _pair_masks(hd):
    low = lax.broadcasted_iota(jnp.int32, (1, 2 * hd), 1) < hd
    return low, jnp.logical_not(low)


def _attn_fwd(q, k, v, dims, dil, *, name):
    a = dims.n_heads * dims.head_dim
    heads, hd, blk = dims.n_heads, dims.head_dim, ATTN_BLOCK
    assert 2 * hd == LANES and heads % 2 == 0 and heads <= LANES
    nb = dims.seq // dil // blk
    has_prev = nb > 1
    nkeys = 2 * blk if has_prev else blk
    grid, cur, prev, _ = _attn_specs(dims, dil, a)
    _, cur_stat, _, _ = _attn_specs(dims, dil, LANES)

    def body(*refs):
        if has_prev:
            q_ref, kc_ref, vc_ref, kp_ref, vp_ref, o_ref, lse_ref, s_scr, p_scr = refs
        else:
            q_ref, kc_ref, vc_ref, o_ref, lse_ref, s_scr, p_scr = refs
        low, high = _pair_masks(hd)

        def keys(cur_ref, prev_ref, sl):
            return jnp.concatenate([prev_ref[:, sl], cur_ref[:, sl]], axis=0) if has_prev else cur_ref[:, sl]

        for hp in range(heads // 2):
            sl = slice(LANES * hp, LANES * (hp + 1))
            q2 = q_ref[:, sl]
            kcat = keys(kc_ref, kp_ref if has_prev else None, sl)
            s_scr[2 * hp] = _dot_nt(jnp.where(low, q2, jnp.zeros_like(q2)), kcat)
            s_scr[2 * hp + 1] = _dot_nt(jnp.where(high, q2, jnp.zeros_like(q2)), kcat)

        iq = lax.broadcasted_iota(jnp.int32, (blk, nkeys), 0)
        jk = lax.broadcasted_iota(jnp.int32, (blk, nkeys), 1)
        if has_prev:
            steps = iq + blk - jk
            valid = (steps >= 0) & (steps <= blk) & ((jk >= blk) | (pl.program_id(len(grid) - 1) > 0))
        else:
            steps = iq - jk
            valid = steps >= 0
        bias = jnp.where(valid, steps.astype(F32) * (-float(dil)), -MASK_BIAS)
        s = s_scr[...] + _head_slopes(heads) * bias[None]
        m = jnp.max(s, axis=-1, keepdims=True)
        p = jnp.exp(s - m)
        l = jnp.sum(p, axis=-1, keepdims=True)
        p_scr[...] = p.astype(BF16)
        inv = 1.0 / l
        lse = m + jnp.log(l)

        lane = lax.broadcasted_iota(jnp.int32, (blk, LANES), 1)
        stat = jnp.zeros((blk, LANES), F32)
        for hp in range(heads // 2):
            sl = slice(LANES * hp, LANES * (hp + 1))
            vcat = keys(vc_ref, vp_ref if has_prev else None, sl)
            pv_a = jnp.dot(p_scr[2 * hp], vcat, preferred_element_type=F32) * inv[2 * hp]
            pv_b = jnp.dot(p_scr[2 * hp + 1], vcat, preferred_element_type=F32) * inv[2 * hp + 1]
            o_ref[:, sl] = jnp.where(low, pv_a, pv_b)
            stat = jnp.where(lane == 2 * hp, lse[2 * hp], stat)
            stat = jnp.where(lane == 2 * hp + 1, lse[2 * hp + 1], stat)
        lse_ref[...] = stat

    lead = q.shape[:-2]
    rows = q.shape[-2]
    o, lse = pl.pallas_call(
        body, name=name, grid=grid,
        in_specs=[cur, cur, cur] + ([prev, prev] if has_prev else []),
        out_specs=[cur, cur_stat],
        out_shape=[jax.ShapeDtypeStruct(lead + (rows, a), F32), jax.ShapeDtypeStruct(lead + (rows, LANES), F32)],
        scratch_shapes=[pltpu.VMEM((heads, blk, nkeys), F32), pltpu.VMEM((heads, blk, nkeys), BF16)],
        compiler_params=_params(*["parallel"] * len(grid)),
    )(q, k, v, *([k, v] if has_prev else []))
    return o, lse


def _attn_combine(groups, head_spread, dims, *, name, tr=256):
    t = dims.tokens
    a = dims.n_heads * dims.head_dim
    dils = tuple(groups)

    def body(*refs):
        ins = refs[:2 * len(dils)]
        x_ref = refs[2 * len(dils)]
        o_ref = refs[2 * len(dils) + 1]
        lse_refs = refs[2 * len(dils) + 2:-2]
        scr, scr_stat = refs[-2], refs[-1]
        outs, stats = [], []
        for g, d in enumerate(dils):
            if d == 1:
                outs.append(ins[2 * g][...])
                stats.append(ins[2 * g + 1][...])
            else:
                outs.append(_residues_to_rows(ins[2 * g], scr, d))
                stats.append(_residues_to_rows(ins[2 * g + 1], scr_stat, d))
        top = functools.reduce(jnp.maximum, stats)
        weights = [jnp.exp(s - top) for s in stats]
        total = functools.reduce(jnp.add, weights)
        joint = top + jnp.log(total)
        inv = 1.0 / total
        acc = None
        for w, o in zip(weights, outs):
            term = _two_pass_dot(w * inv, x_ref[...]) * o
            acc = term if acc is None else acc + term
        o_ref[...] = acc.astype(BF16)
        for g, d in enumerate(dils):
            if d == 1:
                lse_refs[g][...] = joint
            else:
                _rows_to_residues(joint, lse_refs[g], scr_stat, d)

    in_specs, args, lse_specs, lse_shapes = [], [], [], []
    for d in dils:
        if d == 1:
            in_specs += [_row_spec(tr, a), _row_spec(tr, LANES)]
            lse_specs.append(_row_spec(tr, LANES))
            lse_shapes.append(jax.ShapeDtypeStruct((t, LANES), F32))
        else:
            in_specs += [_residue_spec(dims, d, tr, a), _residue_spec(dims, d, tr, LANES)]
            lse_specs.append(_residue_spec(dims, d, tr, LANES))
            lse_shapes.append(_residue_shape(dims, d, LANES, F32))
        args += list(groups[d])
    outs = pl.pallas_call(
        body, name=name, grid=(t // tr,),
        in_specs=in_specs + [pl.BlockSpec((LANES, a), lambda i: (0, 0))],
        out_specs=[_row_spec(tr, a)] + lse_specs,
        out_shape=[jax.ShapeDtypeStruct((t, a), BF16)] + lse_shapes,
        scratch_shapes=[pltpu.VMEM((a // LANES, tr, LANES), F32), pltpu.VMEM((1, tr, LANES), F32)],
        compiler_params=_params("parallel"),
    )(*args, head_spread)
    return outs[0], dict(zip(dils, outs[1:]))


def _attn_bwd_prep(do, o, head_sum, dims, *, name, tr=256):
    t, a = o.shape

    def body(do_ref, o_ref, e_ref, *rest):
        outs, scr, scr_stat = rest[:-2], rest[-2], rest[-1]
        dov = do_ref[...].astype(F32)
        delta = _two_pass_dot(dov * o_ref[...].astype(F32), e_ref[...])
        outs[0][...] = delta
        for g, d in enumerate(RESIDUE_DILATIONS):
            _rows_to_residues(dov, outs[1 + 2 * g], scr, d)
            _rows_to_residues(delta, outs[2 + 2 * g], scr_stat, d)

    out_specs, out_shape = [_row_spec(tr, LANES)], [jax.ShapeDtypeStruct((t, LANES), F32)]
    for d in RESIDUE_DILATIONS:
        out_specs += [_residue_spec(dims, d, tr, a), _residue_spec(dims, d, tr, LANES)]
        out_shape += [_residue_shape(dims, d, a, BF16), _residue_shape(dims, d, LANES, F32)]
    outs = pl.pallas_call(
        body, name=name, grid=(t // tr,),
        in_specs=[_row_spec(tr, a), _row_spec(tr, a), pl.BlockSpec((a, LANES), lambda i: (0, 0))],
        out_specs=out_specs, out_shape=out_shape,
        scratch_shapes=[pltpu.VMEM((a // LANES, tr, LANES), F32), pltpu.VMEM((1, tr, LANES), F32)],
        compiler_params=_params("parallel"),
    )(do, o, head_sum)
    dos, deltas = {1: do}, {1: outs[0]}
    for g, d in enumerate(RESIDUE_DILATIONS):
        dos[d], deltas[d] = outs[1 + 2 * g], outs[2 + 2 * g]
    return dos, deltas


def _attn_bwd(q, k, v, do, lse, delta, dims, dil, *, name):
    a = dims.n_heads * dims.head_dim
    heads, hd, blk = dims.n_heads, dims.head_dim, ATTN_BLOCK
    nb = dims.seq // dil // blk
    has_next = nb > 1
    nq = 2 * blk if has_next else blk
    grid, cur, _, nxt = _attn_specs(dims, dil, a)
    _, cur_stat, _, nxt_stat = _attn_specs(dims, dil, LANES)

    def body(*refs):
        k_ref, v_ref, q_ref, do_ref, lse_ref, dl_ref = refs[:6]
        if has_next:
            qn_ref, don_ref, lsen_ref, dln_ref = refs[6:10]
            dq_ref, dk_ref, dv_ref, s_scr, dp_scr, p_scr, ds_scr, carry = refs[10:]
        else:
            dq_ref, dk_ref, dv_ref, s_scr, dp_scr, p_scr, ds_scr = refs[6:]
        j = pl.program_id(len(grid) - 1)
        low, high = _pair_masks(hd)

        def stacked(ref, nref, sl):
            return jnp.concatenate([ref[:, sl], nref[:, sl]], axis=0) if has_next else ref[:, sl]

        def halves(x):
            return jnp.where(low, x, jnp.zeros_like(x)), jnp.where(high, x, jnp.zeros_like(x))

        for hp in range(heads // 2):
            sl = slice(LANES * hp, LANES * (hp + 1))
            k2, v2 = k_ref[:, sl], v_ref[:, sl]
            q_a, q_b = halves(stacked(q_ref, qn_ref if has_next else None, sl))
            do_a, do_b = halves(stacked(do_ref, don_ref if has_next else None, sl))
            s_scr[2 * hp], s_scr[2 * hp + 1] = _dot_nt(q_a, k2), _dot_nt(q_b, k2)
            dp_scr[2 * hp], dp_scr[2 * hp + 1] = _dot_nt(do_a, v2), _dot_nt(do_b, v2)

        rq = lax.broadcasted_iota(jnp.int32, (nq, blk), 0)
        jk = lax.broadcasted_iota(jnp.int32, (nq, blk), 1)
        if has_next:
            iq = jnp.where(rq < blk, rq, rq - blk)
            steps = jnp.where(rq < blk, iq - jk, iq - jk + blk)
            valid = ((rq < blk) & (iq >= jk)) | ((rq >= blk) & (jk >= iq) & (j + 1 < nb))
        else:
            steps, valid = rq - jk, rq >= jk
        bias = jnp.where(valid, steps.astype(F32) * (-float(dil)), -MASK_BIAS)
        lse_all = stacked(lse_ref, lsen_ref if has_next else None, slice(None))
        dl_all = stacked(dl_ref, dln_ref if has_next else None, slice(None))
        lse3 = jnp.stack([lse_all[:, h:h + 1] for h in range(heads)])
        dl3 = jnp.stack([dl_all[:, h:h + 1] for h in range(heads)])
        p = jnp.exp(s_scr[...] + _head_slopes(heads) * bias[None] - lse3)
        p_scr[...] = p.astype(BF16)
        ds_scr[...] = (p * (dp_scr[...] - dl3)).astype(BF16)

        if has_next:
            @pl.when(j == 0)
            def _():
                carry[...] = jnp.zeros_like(carry)

        for hp in range(heads // 2):
            sl = slice(LANES * hp, LANES * (hp + 1))
            k2 = k_ref[:, sl]
            q_a, q_b = halves(stacked(q_ref, qn_ref if has_next else None, sl))
            do_a, do_b = halves(stacked(do_ref, don_ref if has_next else None, sl))
            ds_a, ds_b = ds_scr[2 * hp], ds_scr[2 * hp + 1]
            dq2 = jnp.where(low, jnp.dot(ds_a, k2, preferred_element_type=F32),
                            jnp.dot(ds_b, k2, preferred_element_type=F32))
            dk_ref[:, sl] = _dot_tn(ds_a, q_a) + _dot_tn(ds_b, q_b)
            dv_ref[:, sl] = _dot_tn(p_scr[2 * hp], do_a) + _dot_tn(p_scr[2 * hp + 1], do_b)
            if has_next:
                dq_ref[:, sl] = carry[:, sl] + dq2[:blk]
                carry[:, sl] = dq2[blk:]
            else:
                dq_ref[:, sl] = dq2

    args, in_specs = [k, v, q, do, lse, delta], [cur] * 4 + [cur_stat] * 2
    if has_next:
        args += [q, do, lse, delta]
        in_specs += [nxt] * 2 + [nxt_stat] * 2
    shape = jax.ShapeDtypeStruct(q.shape, F32)
    scratch = [pltpu.VMEM((heads, nq, blk), F32)] * 2 + [pltpu.VMEM((heads, nq, blk), BF16)] * 2
    if has_next:
        scratch.append(pltpu.VMEM((blk, a), F32))
    return pl.pallas_call(
        body, name=name, grid=grid, in_specs=in_specs, out_specs=[cur] * 3, out_shape=[shape] * 3,
        scratch_shapes=scratch,
        compiler_params=_params(*["parallel"] * (len(grid) - 1), "arbitrary"),
    )(*args)


def _qkv_layouts_bwd(z, grads, gq, gk, head_ones, dims, *, name, tr=256):
    t = z.shape[0]
    a = dims.n_heads * dims.head_dim
    q_scale = dims.head_dim ** -0.5
    dils = tuple(grads)

    def body(q_ref, k_ref, *rest):
        d_refs = rest[:3 * len(dils)]
        gq_ref, gk_ref, sum_ref, spread_ref, dz_ref, dgq_ref, dgk_ref, scr = rest[3 * len(dils):]
        first = pl.program_id(0) == 0
        mean = lambda val: _two_pass_dot(_two_pass_dot(val, sum_ref[...]), spread_ref[...]) * (1.0 / dims.head_dim)

        def total(j):
            acc = None
            for g, d in enumerate(dils):
                ref = d_refs[3 * g + j]
                part = ref[...] if d == 1 else _residues_to_rows(ref, scr, d)
                acc = part if acc is None else acc + part
            return acc

        def norm_bwd(x_ref, dy, g_ref, scale, col, dg_ref):
            xv = x_ref[...].astype(F32)
            dy = dy * scale
            r = lax.rsqrt(mean(xv * xv) + RMS_EPS)
            gy = dy * g_ref[...]
            dx = r * gy - xv * (r * r * r) * mean(xv * gy)
            dz_ref[:, col * a:(col + 1) * a] = dx.astype(BF16)
            _accumulate(dg_ref, jnp.sum(dy * xv * r, axis=0, keepdims=True), first)

        norm_bwd(q_ref, total(0), gq_ref, q_scale, 0, dgq_ref)
        norm_bwd(k_ref, total(1), gk_ref, 1.0, 1, dgk_ref)
        dz_ref[:, 2 * a:3 * a] = total(2).astype(BF16)

    in_specs, args = [_row_spec(tr, a, 2), _row_spec(tr, a, 3)], [z, z]
    for d in dils:
        in_specs += [_row_spec(tr, a) if d == 1 else _residue_spec(dims, d, tr, a)] * 3
        args += list(grads[d])
    in_specs += [_vec_spec(a), _vec_spec(a), pl.BlockSpec((a, LANES), lambda i: (0, 0)),
                 pl.BlockSpec((LANES, a), lambda i: (0, 0))]
    return pl.pallas_call(
        body, name=name, grid=(t // tr,), in_specs=in_specs,
        out_specs=[_row_spec(tr, 3 * a), _vec_spec(a), _vec_spec(a)],
        out_shape=[jax.ShapeDtypeStruct((t, 3 * a), BF16)] + [jax.ShapeDtypeStruct((1, a), F32)] * 2,
        scratch_shapes=[pltpu.VMEM((a // LANES, tr, LANES), F32)],
        compiler_params=_params("arbitrary"),
    )(*args, gq, gk, *head_ones)


def _mix_fwd(ya, yb, z, gate_b, dims, *, name, tr=512):
    t, d = ya.shape
    tr = _pick(t, tr, 8)
    first_gate_col = z.shape[1] // d - 2

    def body(ya_ref, yb_ref, ga_ref, gb_ref, ba_ref, bb_ref, o_ref):
        g_a = _sigmoid(ga_ref[...].astype(F32) + ba_ref[...])
        g_b = _sigmoid(gb_ref[...].astype(F32) + bb_ref[...])
        o_ref[...] = (g_a * ya_ref[...] + g_b * yb_ref[...]).astype(BF16)

    return pl.pallas_call(
        body, name=name, grid=(t // tr,),
        in_specs=[_row_spec(tr, d), _row_spec(tr, d), _row_spec(tr, d, first_gate_col),
                  _row_spec(tr, d, first_gate_col + 1), _vec_spec(d, 0), _vec_spec(d, 1)],
        out_specs=_row_spec(tr, d), out_shape=jax.ShapeDtypeStruct((t, d), BF16),
        compiler_params=_params("parallel"),
    )(ya, yb, z, z, gate_b, gate_b)


def _mix_bwd(dmix, ya, yb, z, gate_b, dims, *, name, tr=512):
    t, d = ya.shape
    tr = _pick(t, tr, 8)
    first_gate_col = z.shape[1] // d - 2

    def body(dm_ref, ya_ref, yb_ref, ga_ref, gb_ref, ba_ref, bb_ref, dya_ref, dyb_ref, dz_ref, db_ref):
        dm = dm_ref[...].astype(F32)
        g_a = _sigmoid(ga_ref[...].astype(F32) + ba_ref[...])
        g_b = _sigmoid(gb_ref[...].astype(F32) + bb_ref[...])
        dya_ref[...] = (dm * g_a).astype(BF16)
        dyb_ref[...] = (dm * g_b).astype(BF16)
        dl_a = dm * ya_ref[...] * g_a * (1.0 - g_a)
        dl_b = dm * yb_ref[...] * g_b * (1.0 - g_b)
        dz_ref[:, 0:d] = dl_a.astype(BF16)
        dz_ref[:, d:2 * d] = dl_b.astype(BF16)
        first = pl.program_id(0) == 0
        sums = jnp.concatenate([jnp.sum(dl_a, axis=0, keepdims=True), jnp.sum(dl_b, axis=0, keepdims=True)], axis=1)
        _accumulate(db_ref, sums, first)

    return pl.pallas_call(
        body, name=name, grid=(t // tr,),
        in_specs=[_row_spec(tr, d), _row_spec(tr, d), _row_spec(tr, d), _row_spec(tr, d, first_gate_col),
                  _row_spec(tr, d, first_gate_col + 1), _vec_spec(d, 0), _vec_spec(d, 1)],
        out_specs=[_row_spec(tr, d), _row_spec(tr, d), _row_spec(tr, 2 * d), _vec_spec(2 * d)],
        out_shape=[jax.ShapeDtypeStruct((t, d), BF16)] * 2 + [jax.ShapeDtypeStruct((t, 2 * d), BF16),
                                                              jax.ShapeDtypeStruct((1, 2 * d), F32)],
        compiler_params=_params("arbitrary"),
    )(dmix, ya, yb, z, z, gate_b, gate_b)


def _loss_head(y, target, *, name, tr=512):
    t, d = y.shape
    tr = _pick(t, tr, 8)

    def body(y_ref, t_ref, dy_ref, dyb_ref, loss_ref):
        err = y_ref[...] - t_ref[...]
        dy = err * (1.0 / d)
        dy_ref[...] = dy
        dyb_ref[...] = dy.astype(BF16)
        part = jnp.sum(jnp.sum(err * err, axis=-1, keepdims=True), axis=0, keepdims=True) * (0.5 / d)
        _accumulate(loss_ref, jnp.broadcast_to(part, (8, 128)), pl.program_id(0) == 0)

    return pl.pallas_call(
        body, name=name, grid=(t // tr,),
        in_specs=[_row_spec(tr, d), _row_spec(tr, d)],
        out_specs=[_row_spec(tr, d), _row_spec(tr, d), pl.BlockSpec((8, 128), lambda i: (0, 0))],
        out_shape=[jax.ShapeDtypeStruct((t, d), F32), jax.ShapeDtypeStruct((t, d), BF16),
                   jax.ShapeDtypeStruct((8, 128), F32)],
        compiler_params=_params("arbitrary"),
    )(y, target)


def _adamw(w, grads, m, v, *, name, tr=256):
    r, c = w.shape
    tr = _pick(r, tr, 8)
    ng = len(grads)
    c1 = 1.0 - ADAM_B1 ** ADAM_STEP
    c2 = 1.0 - ADAM_B2 ** ADAM_STEP

    def body(*refs):
        w_ref, g_refs, m_ref, v_ref = refs[0], refs[1:1 + ng], refs[1 + ng], refs[2 + ng]
        g_out, d_out, m_out, v_out = refs[3 + ng:]
        g = g_refs[0][...]
        for extra in g_refs[1:]:
            g = g + extra[...]
        m_new = ADAM_B1 * m_ref[...] + (1.0 - ADAM_B1) * g
        v_new = ADAM_B2 * v_ref[...] + (1.0 - ADAM_B2) * (g * g)
        g_out[...] = g
        m_out[...] = m_new
        v_out[...] = v_new
        d_out[...] = -ADAM_LR * ((m_new / c1) / (jnp.sqrt(v_new / c2) + ADAM_EPS) + ADAM_WD * w_ref[...])

    spec = pl.BlockSpec((tr, c), lambda i: (i, 0))
    return pl.pallas_call(
        body, name=name, grid=(r // tr,),
        in_specs=[spec] * (3 + ng), out_specs=[spec] * 4, out_shape=[jax.ShapeDtypeStruct((r, c), F32)] * 4,
        compiler_params=_params("parallel"),
    )(w, *grads, m, v)


CHIP_PEERS = ((1, 0), (0, 1), (1, 1))


def _place():
    return lax.axis_index("x"), lax.axis_index("y"), lax.axis_index("c")


HBM = pl.BlockSpec(memory_space=pltpu.HBM)
SEM = pl.BlockSpec(memory_space=pltpu.SEMAPHORE)
IN_FLIGHT = pltpu.SideEffectType.DATAFLOW_SIDE_EFFECTING


def _in_hbm(a):
    return pltpu.with_memory_space_constraint(a, pltpu.HBM)


def _cast_to_lands(shards, dtypes, *, name):
    n = len(shards)

    def body(*refs):
        ins, outs, bufs, sems = refs[:n], refs[n:2 * n], refs[2 * n:3 * n], refs[3 * n]
        x, y, _ = _place()
        copies = []
        for a in range(n):
            bufs[a][...] = ins[a][...].astype(dtypes[a])
            cp = pltpu.make_async_copy(bufs[a], outs[a].at[2 * x + y], sems.at[a])
            cp.start()
            copies.append(cp)
        for cp in copies:
            cp.wait()

    return pl.pallas_call(
        body, name=name, in_specs=[pl.BlockSpec(memory_space=pltpu.VMEM)] * n, out_specs=[ANY] * n,
        out_shape=[jax.ShapeDtypeStruct((N_CHIPS,) + s.shape, dt) for s, dt in zip(shards, dtypes)],
        scratch_shapes=[pltpu.VMEM(s.shape, dt) for s, dt in zip(shards, dtypes)] + [pltpu.SemaphoreType.DMA((n,))],
        compiler_params=pltpu.CompilerParams(vmem_limit_bytes=V7X_VMEM_LIMIT_BYTES),
    )(*shards)


def _chip_copy(src, dst, send, recv, flip, place):
    x, y, c = place
    return pltpu.make_async_remote_copy(src_ref=src, dst_ref=dst, send_sem=send, recv_sem=recv,
                                        device_id=(x ^ flip[0], y ^ flip[1], c), device_id_type=MESH)


def _my_part(land, place, halved):
    block = land.at[2 * place[0] + place[1]]
    if not halved:
        return block
    rows = land.shape[1] // 2
    return block.at[pl.ds(pl.multiple_of(place[2] * rows, rows), rows)]


def _gather_start(lands, after, *, name, halved=()):
    n = len(lands)

    def body(*refs):
        ins, send, recv, token = refs[:n], refs[n + 1], refs[n + 2], refs[-1]
        place = _place()
        for a in range(n):
            part = _my_part(ins[a], place, a in halved)
            for p, flip in enumerate(CHIP_PEERS):
                k = 3 * a + p
                _chip_copy(part, part, send.at[k], recv.at[k], flip, place).start()
        token[...] = jnp.zeros_like(token)

    outs = pl.pallas_call(
        body, name=name, in_specs=[HBM] * n + [ANY],
        out_specs=(SEM, SEM, *[HBM] * n, pl.BlockSpec(memory_space=pltpu.VMEM)),
        out_shape=(pltpu.SemaphoreType.DMA((3 * n,)), pltpu.SemaphoreType.DMA((3 * n,)),
                   *[pltpu.HBM(l.shape, l.dtype) for l in lands], jax.ShapeDtypeStruct((8, 128), F32)),
        input_output_aliases={a: 2 + a for a in range(n)},
        compiler_params=pltpu.CompilerParams(has_side_effects=IN_FLIGHT),
    )(*[_in_hbm(l) for l in lands], after)
    return outs[0], outs[1], list(outs[2:2 + n]), outs[-1]


def _gather_wait(send, recv, lands, after, *, name, halved=()):
    n = len(lands)

    def body(*refs):
        ins, send_ref, recv_ref = refs[:n], refs[n], refs[n + 1]
        place = _place()
        for a in range(n):
            part = _my_part(ins[a], place, a in halved)
            for p, flip in enumerate(CHIP_PEERS):
                k = 3 * a + p
                cp = _chip_copy(part, part, send_ref.at[k], recv_ref.at[k], flip, place)
                cp.wait_send()
                cp.wait_recv()

    return pl.pallas_call(
        body, name=name, in_specs=[HBM] * n + [SEM, SEM, ANY], out_specs=[HBM] * n,
        out_shape=[pltpu.HBM(l.shape, l.dtype) for l in lands],
        input_output_aliases={a: a for a in range(n)},
        compiler_params=pltpu.CompilerParams(has_side_effects=IN_FLIGHT),
    )(*lands, send, recv, after)


def _forward_to_sibling(land, *, name):
    rows = land.shape[1] // 2

    def body(land_ref, out_ref, send, recv):
        x, y, c = _place()
        copies = []
        for p, (fx, fy) in enumerate(CHIP_PEERS):
            chip = 2 * (x ^ fx) + (y ^ fy)
            mine = pl.ds(pl.multiple_of(c * rows, rows), rows)
            theirs = pl.ds(pl.multiple_of((1 - c) * rows, rows), rows)
            out = pltpu.make_async_remote_copy(
                src_ref=land_ref.at[chip].at[mine], dst_ref=out_ref.at[chip].at[mine], send_sem=send.at[p],
                recv_sem=recv.at[p], device_id=(x, y, 1 - c), device_id_type=MESH)
            out.start()
            copies.append((out, pltpu.make_async_remote_copy(
                src_ref=land_ref.at[chip].at[theirs], dst_ref=out_ref.at[chip].at[theirs], send_sem=send.at[p],
                recv_sem=recv.at[p], device_id=(x, y, 1 - c), device_id_type=MESH)))
        for out, arriving in copies:
            out.wait_send()
            arriving.wait_recv()

    return pl.pallas_call(
        body, name=name, in_specs=[ANY], out_specs=ANY, out_shape=jax.ShapeDtypeStruct(land.shape, land.dtype),
        input_output_aliases={0: 0},
        scratch_shapes=[pltpu.SemaphoreType.DMA((3,)), pltpu.SemaphoreType.DMA((3,))],
    )(land)


def _scatter_start(grad, *, name):
    def body(g_ref, land_ref, send, recv, g_thru, land_thru, token):
        place = _place()
        for p, flip in enumerate(CHIP_PEERS):
            peer_chip = 2 * (place[0] ^ flip[0]) + (place[1] ^ flip[1])
            _chip_copy(g_ref.at[peer_chip], land_ref.at[p], send.at[p], recv.at[p], flip, place).start()
        token[...] = jnp.zeros_like(token)

    land = lax.empty((3,) + grad.shape[1:], grad.dtype)
    return pl.pallas_call(
        body, name=name, in_specs=[HBM, HBM],
        out_specs=(SEM, SEM, HBM, HBM, pl.BlockSpec(memory_space=pltpu.VMEM)),
        out_shape=(pltpu.SemaphoreType.DMA((3,)), pltpu.SemaphoreType.DMA((3,)), pltpu.HBM(grad.shape, grad.dtype),
                   pltpu.HBM(land.shape, land.dtype), jax.ShapeDtypeStruct((8, 128), F32)),
        input_output_aliases={0: 2, 1: 3},
        compiler_params=pltpu.CompilerParams(has_side_effects=IN_FLIGHT),
    )(_in_hbm(grad), _in_hbm(land))


def _scatter_wait(started, after, *, name):
    n = len(started)

    def body(*refs):
        grads, lands = refs[:n], refs[n:2 * n]
        sends, recvs = refs[2 * n:3 * n], refs[3 * n:4 * n]
        place = _place()
        for a in range(n):
            for p, flip in enumerate(CHIP_PEERS):
                cp = _chip_copy(grads[a].at[0], lands[a].at[p], sends[a].at[p], recvs[a].at[p], flip, place)
                cp.wait_send()
                cp.wait_recv()

    grads, lands = [s[2] for s in started], [s[3] for s in started]
    after = list(after) if isinstance(after, (list, tuple)) else [after]
    outs = pl.pallas_call(
        body, name=name, in_specs=[HBM] * (2 * n) + [SEM] * (2 * n) + [ANY] * len(after), out_specs=[HBM] * (2 * n),
        out_shape=[pltpu.HBM(a.shape, a.dtype) for a in grads + lands],
        input_output_aliases={a: a for a in range(2 * n)},
        compiler_params=pltpu.CompilerParams(has_side_effects=IN_FLIGHT),
    )(*grads, *lands, *[s[0] for s in started], *[s[1] for s in started], *after)
    return list(zip(outs[:n], outs[n:]))


def _sibling_copy(src, dst, send, recv, place):
    x, y, c = place
    return pltpu.make_async_remote_copy(src_ref=src, dst_ref=dst, send_sem=send, recv_sem=recv,
                                        device_id=(x, y, 1 - c), device_id_type=MESH)


def _swap_start(arrays, *, name):
    n = len(arrays)

    def body(*refs):
        ins, lands, send, recv, token = refs[:n], refs[n:2 * n], refs[2 * n], refs[2 * n + 1], refs[-1]
        place = _place()
        for a in range(n):
            _sibling_copy(ins[a], lands[a], send.at[a], recv.at[a], place).start()
        token[...] = jnp.zeros_like(token)

    both = [_in_hbm(a) for a in arrays] + [_in_hbm(lax.empty(a.shape, a.dtype)) for a in arrays]
    outs = pl.pallas_call(
        body, name=name, in_specs=[HBM] * (2 * n),
        out_specs=(SEM, SEM, *[HBM] * (2 * n), pl.BlockSpec(memory_space=pltpu.VMEM)),
        out_shape=(pltpu.SemaphoreType.DMA((n,)), pltpu.SemaphoreType.DMA((n,)),
                   *[pltpu.HBM(a.shape, a.dtype) for a in both], jax.ShapeDtypeStruct((8, 128), F32)),
        input_output_aliases={a: 2 + a for a in range(2 * n)},
        compiler_params=pltpu.CompilerParams(has_side_effects=IN_FLIGHT),
    )(*both)
    return outs[0], outs[1], list(outs[2:2 + n]), list(outs[2 + n:2 + 2 * n]), outs[-1]


def _swap_wait(started, after, *, name):
    send, recv, arrays, lands = started[:4]
    n = len(arrays)

    def body(*refs):
        ins, zones, send_ref, recv_ref = refs[:n], refs[n:2 * n], refs[2 * n], refs[2 * n + 1]
        place = _place()
        for a in range(n):
            cp = _sibling_copy(ins[a], zones[a], send_ref.at[a], recv_ref.at[a], place)
            cp.wait_send()
            cp.wait_recv()

    after = list(after) if isinstance(after, (list, tuple)) else [after]
    outs = pl.pallas_call(
        body, name=name, in_specs=[HBM] * (2 * n) + [SEM, SEM] + [ANY] * len(after), out_specs=[HBM] * (2 * n),
        out_shape=[pltpu.HBM(a.shape, a.dtype) for a in arrays + lands],
        input_output_aliases={a: a for a in range(2 * n)},
        compiler_params=pltpu.CompilerParams(has_side_effects=IN_FLIGHT),
    )(*arrays, *lands, send, recv, *after)
    return list(outs[n:])


def _allreduce_start(packed, *, name):
    n_dev = 8

    def body(src_ref, land_ref, send, recv, src_thru, land_thru, token):
        x, y, c = _place()
        me = 4 * x + 2 * y + c
        for p in range(1, n_dev):
            pltpu.make_async_remote_copy(
                src_ref=src_ref, dst_ref=land_ref.at[me], send_sem=send.at[p - 1], recv_sem=recv.at[p - 1],
                device_id=(x ^ (p >> 2), y ^ ((p >> 1) & 1), c ^ (p & 1)), device_id_type=MESH).start()
        token[...] = jnp.zeros_like(token)

    land = lax.empty((n_dev,) + packed.shape, packed.dtype)
    return pl.pallas_call(
        body, name=name, in_specs=[HBM, HBM],
        out_specs=(SEM, SEM, HBM, HBM, pl.BlockSpec(memory_space=pltpu.VMEM)),
        out_shape=(pltpu.SemaphoreType.DMA((n_dev - 1,)), pltpu.SemaphoreType.DMA((n_dev - 1,)),
                   pltpu.HBM(packed.shape, packed.dtype), pltpu.HBM(land.shape, land.dtype),
                   jax.ShapeDtypeStruct((8, 128), F32)),
        input_output_aliases={0: 2, 1: 3},
        compiler_params=pltpu.CompilerParams(has_side_effects=IN_FLIGHT),
    )(_in_hbm(packed), _in_hbm(land))


def _allreduce_wait(started, after, *, name):
    send, recv, packed, land = started[:4]
    n_dev = 8

    def body(src_ref, land_ref, send_ref, recv_ref, *_):
        x, y, c = _place()
        for p in range(1, n_dev):
            cp = pltpu.make_async_remote_copy(
                src_ref=src_ref, dst_ref=land_ref.at[0], send_sem=send_ref.at[p - 1], recv_sem=recv_ref.at[p - 1],
                device_id=(x ^ (p >> 2), y ^ ((p >> 1) & 1), c ^ (p & 1)), device_id_type=MESH)
            cp.wait_send()
            cp.wait_recv()

    after = list(after) if isinstance(after, (list, tuple)) else [after]
    return pl.pallas_call(
        body, name=name, in_specs=[HBM, HBM, SEM, SEM] + [ANY] * len(after), out_specs=[HBM, HBM],
        out_shape=[pltpu.HBM(packed.shape, packed.dtype), pltpu.HBM(land.shape, land.dtype)],
        input_output_aliases={0: 0, 1: 1},
        compiler_params=pltpu.CompilerParams(has_side_effects=IN_FLIGHT),
    )(packed, land, send, recv, *after)


def _sum_devices(mine, land, *, name):
    n_dev = land.shape[0]

    def body(mine_ref, land_ref, out_ref):
        x, y, c = _place()
        me = 4 * x + 2 * y + c
        total = None
        for s in range(n_dev):
            part = jnp.where(me == s, mine_ref[...], land_ref[s])
            total = part if total is None else total + part
        out_ref[...] = total

    return pl.pallas_call(body, name=name, out_shape=jax.ShapeDtypeStruct(mine.shape, mine.dtype))(mine, land)


def _sum_received(grad, land, *, name, tr=256):
    _, r, c = grad.shape
    tr = _pick(r, tr, 8)

    def body(chip_ref, g_ref, l_ref, o_ref):
        o_ref[...] = ((g_ref[...] + l_ref[0].astype(F32)) + l_ref[1].astype(F32)) + l_ref[2].astype(F32)

    chip = (2 * lax.axis_index("x") + lax.axis_index("y")).astype(jnp.int32).reshape(1)
    return pl.pallas_call(
        body, name=name,
        grid_spec=pltpu.PrefetchScalarGridSpec(
            num_scalar_prefetch=1, grid=(r // tr,),
            in_specs=[pl.BlockSpec((None, tr, c), lambda i, chip_ref: (chip_ref[0], i, 0)),
                      pl.BlockSpec((3, tr, c), lambda i, chip_ref: (0, i, 0))],
            out_specs=pl.BlockSpec((tr, c), lambda i, chip_ref: (i, 0))),
        out_shape=jax.ShapeDtypeStruct((r, c), F32), compiler_params=_params("parallel"),
    )(chip, grad, land)


def _allreduce_small(packed, *, name, after=None):
    r, d = packed.shape
    n_dev = 8

    def body(src_ref, out_ref, buf, send, recv):
        x, y, c = _place()
        me = 4 * x + 2 * y + c
        started = []
        for p in range(1, n_dev):
            rc = pltpu.make_async_remote_copy(
                src_ref=src_ref, dst_ref=buf.at[me], send_sem=send.at[p - 1], recv_sem=recv.at[p - 1],
                device_id=(x ^ (p >> 2), y ^ ((p >> 1) & 1), c ^ (p & 1)), device_id_type=MESH)
            rc.start()
            started.append(rc)
        buf[me] = src_ref[...]
        for rc in started:
            rc.wait()
        total = buf[0]
        for s in range(1, n_dev):
            total = total + buf[s]
        out_ref[...] = total

    vmem = pl.BlockSpec(memory_space=pltpu.VMEM)
    body, more_specs, more_args = _ordered(body, 1, after)
    return pl.pallas_call(
        body, name=name, in_specs=[vmem] + more_specs, out_specs=vmem, out_shape=jax.ShapeDtypeStruct((r, d), F32),
        scratch_shapes=[pltpu.VMEM((n_dev, r, d), F32), pltpu.SemaphoreType.DMA((n_dev - 1,)),
                        pltpu.SemaphoreType.DMA((n_dev - 1,))],
    )(packed, *more_args)


def _packed_rows(size, d):
    return -(-size // (8 * d)) * 8


def _pack_rows(arrays, d):
    rows = []
    for arr in arrays:
        flat = arr.reshape(-1).astype(F32)
        n = _packed_rows(flat.shape[0], d)
        rows.append(jnp.pad(flat, (0, n * d - flat.shape[0])).reshape(n, d))
    return jnp.concatenate(rows, axis=0)


def _unpack_rows(packed, shapes, d):
    out, row = [], 0
    for shape in shapes:
        size = math.prod(shape)
        n = _packed_rows(size, d)
        out.append(packed[row:row + n].reshape(-1)[:size].reshape(shape))
        row += n
    return out


SMALL = ("norm1_g", "gate_b", "conv_b", "conv_norm_g", "q_norm_g", "k_norm_g", "norm2_g", "ffn_conv_b")
LARGE = ("w_in", "w_conv_out", "w_attn_out", "w_out", "w_up", "w_down")
WEIGHTS = ("norm1_g", "w_in", "gate_b", "conv_w", "conv_b", "conv_norm_g", "w_conv_out", "q_norm_g", "k_norm_g",
           "w_attn_out", "w_out", "norm2_g", "w_up", "ffn_conv_w", "ffn_conv_b", "w_down")


def _head_ones(dims):
    a = dims.n_heads * dims.head_dim
    head = jnp.arange(a, dtype=jnp.int32) // dims.head_dim
    return (head[:, None] == head[None, :]).astype(BF16)


def _after(vec, token):
    return vec if token is None else vec + token[0:1, 0:1]


def _local_step(dims, x, target, small, first_weights, other_weights, send_grad):
    d, f, heads = dims.d_model, dims.d_ff, dims.n_heads
    small = dict(small)
    row = lambda name: small[name].reshape(1, -1)
    head_sum = _head_sum_matrix(dims)
    head_spread = jnp.transpose(head_sum)
    ones = (head_sum, head_spread)
    gq = jnp.tile(row("q_norm_g"), (1, heads))
    gk = jnp.tile(row("k_norm_g"), (1, heads))
    one_shard = lambda w: w.reshape(1, -1, w.shape[-1])

    h = _rmsnorm_fwd(x, row("norm1_g"), name="norm1")
    full = first_weights(h)
    w_in = full["w_in"]
    conv_w = jnp.pad(full["conv_w"], ((0, CONV_HALO - dims.conv_width), (0, 0)))
    ffn_w = jnp.pad(full["ffn_conv_w"], ((0, FFN_HALO - dims.ffn_conv_width), (0, 0)))
    z = _mm_nn(h, w_in, out_dtype=BF16, after=full.get("token"), tm=2048, tn=1792, name="in_proj")
    a1, a3 = _conv_branch_fwd(z, conv_w, row("conv_b"), row("conv_norm_g"), dims, name="conv_branch")
    qkv = _qkv_layouts_fwd(z, gq, gk, ones, dims, name="qk_norm")
    per_group = {dil: _attn_fwd(*qkv[dil], dims, dil, name=f"attn_fwd_d{dil}") for dil in DILATIONS}
    o, lse = _attn_combine(per_group, head_spread, dims, name="attn_combine")
    full = other_weights(o)
    w_up = full["w_up"]
    w_co, w_ao, w_o, w_dn = (one_shard(full[k]) for k in ("w_conv_out", "w_attn_out", "w_out", "w_down"))
    ya = _mm_nn(a3, w_co, out_dtype=F32, name="conv_out_proj")
    yb = _mm_nn(o, w_ao, out_dtype=F32, name="attn_out_proj")
    mixed = _mix_fwd(ya, yb, z, row("gate_b"), dims, name="gate_mix")
    x1 = _mm_nn(mixed, w_o, out_dtype=F32, residual=x, name="out_proj")
    h2 = _rmsnorm_fwd(x1, row("norm2_g"), name="norm2")
    up = _mm_nn(h2, w_up, out_dtype=F32, tm=2048, name="up_proj")
    act = _ffn_act_fwd(up, ffn_w, row("ffn_conv_b"), dims, name="ffn_act")
    x2 = _mm_nn(act, w_dn, out_dtype=F32, residual=x1, name="down_proj")
    dy, dy_b, loss = _loss_head(x2, target, name="loss_head")

    grads = {}

    def large(name, g):
        grads[name], g_bf16 = g
        return send_grad(name, g_bf16)

    sent = large("w_down", _mm_tn(act, dy_b, n_shards=1, name="dw_down"))
    dact = _mm_nt(dy_b, w_dn, out_dtype=BF16, after=sent, name="d_act")
    du, dfw, dfb = _ffn_act_bwd(dact, up, ffn_w, row("ffn_conv_b"), dims, name="ffn_act_bwd")
    grads["ffn_conv_w"], grads["ffn_conv_b"] = dfw[:dims.ffn_conv_width], dfb
    dup = _ffn_conv_bwd(du, ffn_w, dims, name="ffn_conv_bwd")
    sent = large("w_up", _mm_tn(h2, dup, n_shards=N_CHIPS, name="dw_up"))
    dh2 = _mm_nt(dup, w_up, out_dtype=F32, after=sent, name="d_h2")
    dx1, dx1_b, grads["norm2_g"] = _rmsnorm_bwd(x1, row("norm2_g"), dh2, dy, want_bf16=True, name="norm2_bwd")
    sent = large("w_out", _mm_tn(mixed, dx1_b, n_shards=1, name="dw_out"))
    dmix = _mm_nt(dx1_b, w_o, out_dtype=F32, after=sent, name="d_mix")
    dya, dyb, dz_gate, grads["gate_b"] = _mix_bwd(dmix, ya, yb, z, row("gate_b"), dims, name="gate_mix_bwd")
    sent = large("w_conv_out", _mm_tn(a3, dya, n_shards=1, name="dw_conv_out"))
    da3 = _mm_nt(dya, w_co, out_dtype=F32, after=sent, name="d_conv_act")
    da1, grads["conv_norm_g"] = _conv_norm_bwd(da3, a1, row("conv_norm_g"), name="conv_norm_bwd")
    dz_glu, dcw, grads["conv_b"] = _conv_branch_bwd(da1, z, conv_w, dims, name="conv_branch_bwd")
    grads["conv_w"] = dcw[:dims.conv_width]
    sent = large("w_attn_out", _mm_tn(o, dyb, n_shards=1, name="dw_attn_out"))
    do = _mm_nt(dyb, w_ao, out_dtype=BF16, after=sent, name="d_attn")
    dos, deltas = _attn_bwd_prep(do, o, head_sum, dims, name="attn_bwd_prep")
    dqkv = {dil: _attn_bwd(*qkv[dil], dos[dil], lse[dil], deltas[dil], dims, dil, name=f"attn_bwd_d{dil}")
            for dil in DILATIONS}
    dz_qkv, dgq, dgk = _qkv_layouts_bwd(z, dqkv, gq, gk, ones, dims, name="qk_norm_bwd")
    grads["q_norm_g"] = dgq.reshape(heads, dims.head_dim).sum(axis=0)
    grads["k_norm_g"] = dgk.reshape(heads, dims.head_dim).sum(axis=0)
    dz = jnp.concatenate([dz_glu, dz_qkv, dz_gate], axis=1)
    sent = large("w_in", _mm_tn(h, dz, n_shards=N_CHIPS, name="dw_in"))
    dh = _mm_nt(dz, w_in, out_dtype=F32, after=sent, name="d_h")
    dx, grads["norm1_g"] = _rmsnorm_bwd(x, row("norm1_g"), dh, dx1, want_bf16=False, name="norm1_bwd")
    return loss, dx, grads


def _step(dims, x, target, w, m, v):
    d = dims.d_model
    t = dims.tokens
    sq = lambda a: a.reshape(a.shape[1:])
    w2, m2, v2 = ({k: sq(a) for k, a in grp.items()} for grp in (w, m, v))

    conv_pad = jnp.pad(w2["conv_w"], ((0, CONV_HALO - dims.conv_width), (0, 0)))
    ffn_pad = jnp.pad(w2["ffn_conv_w"], ((0, FFN_HALO - dims.ffn_conv_width), (0, 0)))
    gathered_names = LARGE + ("conv_w", "ffn_conv_w")
    lands = dict(zip(gathered_names, _cast_to_lands([w2[k] for k in LARGE] + [conv_pad, ffn_pad],
                                                   [BF16] * len(LARGE) + [F32, F32], name="cast_weights")))
    first_names = ("w_in", "conv_w", "ffn_conv_w")
    other_names = tuple(k for k in gathered_names if k not in first_names)
    first = _gather_start([lands[k] for k in first_names], x, halved=(0,), name="gather_start_first")
    other = []
    cols = lambda g, rows: jnp.moveaxis(g, 0, 1).reshape(g.shape[1], -1)[:rows]

    def first_weights(after):
        got = dict(zip(first_names, _gather_wait(*first[:3], after, halved=(0,), name="gather_wait_first")))
        got["w_in"] = _forward_to_sibling(got["w_in"], name="forward_w_in")
        other.extend(_gather_start([lands[k] for k in other_names], got["w_in"], name="gather_start_other"))
        got["conv_w"] = cols(got["conv_w"], dims.conv_width)
        got["ffn_conv_w"] = cols(got["ffn_conv_w"], dims.ffn_conv_width)
        got["token"] = other[3]
        return got

    def other_weights(after):
        return dict(zip(other_names, _gather_wait(*other[:3], after, name="gather_wait_other")))

    started = {}

    def send_grad(name, g):
        send, recv, g_thru, land, token = _scatter_start(g.reshape(N_CHIPS, -1, g.shape[-1]), name=f"scatter_start_{name}")
        started[name] = (send, recv, g_thru, land)
        return token

    small = {k: w2[k] for k in SMALL}
    small["norm1_g"] = _after(small["norm1_g"].reshape(1, -1), first[3])
    loss, dx, grads = _local_step(dims, x.reshape(t, d), target.reshape(t, d), small, first_weights, other_weights, send_grad)

    def my_sums(names, after, tag):
        arrived = _scatter_wait([started[k] for k in names], after, name=f"scatter_wait_{tag}")
        blocks = [grads[k].reshape(N_CHIPS, -1, grads[k].shape[-1]) for k in names]
        return [_sum_received(g, land, name=f"sum_{k}") for k, g, (_, land) in zip(names, blocks, arrived)]

    def updates(names, mine, theirs):
        return {k: _adamw(w2[k], [a, b], m2[k], v2[k], name=f"adamw_{k}") for k, a, b in zip(names, mine, theirs)}

    small_names = SMALL + ("conv_w", "ffn_conv_w")
    packed = _pack_rows([grads[k] for k in small_names] + [loss[0, 0]], d)
    reducing = _allreduce_start(packed, name="allreduce_start")
    others = [k for k in LARGE if k != "w_in"]
    mine_others = my_sums(others, [dx, reducing[4]], "others")
    swapping_others = _swap_start(mine_others, name="swap_start_others")
    reduced = _sum_devices(*_allreduce_wait(reducing, swapping_others[4], name="allreduce_wait"), name="allreduce_sum")
    shapes = [grads[k].shape for k in small_names] + [()]
    *small_g, loss_total = _unpack_rows(reduced, shapes, d)
    small_g = dict(zip(small_names, small_g))
    chip = 2 * lax.axis_index("x") + lax.axis_index("y")
    for k in ("conv_w", "ffn_conv_w"):
        width = w2[k].shape[1]
        small_g[k] = lax.dynamic_slice_in_dim(small_g[k], chip * width, width, axis=1)

    small_shapes = [w2[k].shape for k in small_names]
    pack = lambda grp: _pack_rows([grp[k] for k in small_names], d)
    results = _adamw(pack(w2), [pack(small_g)], pack(m2), pack(v2), name="adamw_small")
    unpacked = [_unpack_rows(r, small_shapes, d) for r in results]
    out = {k: tuple(u[i] for u in unpacked) for i, k in enumerate(small_names)}
    mine_w_in = my_sums(["w_in"], results[1], "w_in")
    swapping_w_in = _swap_start(mine_w_in, name="swap_start_w_in")
    out.update(updates(others, mine_others, _swap_wait(swapping_others, swapping_w_in[4], name="swap_wait_others")))
    last_updates = [out[k][1] for k in others]
    out.update(updates(["w_in"], mine_w_in, _swap_wait(swapping_w_in, last_updates, name="swap_wait_w_in")))

    lead =lambda a: a.reshape((1,) + a.shape)
    ordered = [[lead(out[k][j].reshape(w2[k].shape)) for k in WEIGHTS] for j in range(4)]
    return (loss_total, dx.reshape(x.shape), *ordered[0], *ordered[1], *ordered[2], *ordered[3])


def kernel(x, norm1_g, w_in, gate_b, conv_w, conv_b, conv_norm_g, w_conv_out, q_norm_g, k_norm_g, w_attn_out, w_out, norm2_g, w_up, ffn_conv_w, ffn_conv_b, w_down, loss_target, m_norm1_g, m_w_in, m_gate_b, m_conv_w, m_conv_b, m_conv_norm_g, m_w_conv_out, m_q_norm_g, m_k_norm_g, m_w_attn_out, m_w_out, m_norm2_g, m_w_up, m_ffn_conv_w, m_ffn_conv_b, m_w_down, v_norm1_g, v_w_in, v_gate_b, v_conv_w, v_conv_b, v_conv_norm_g, v_w_conv_out, v_q_norm_g, v_k_norm_g, v_w_attn_out, v_w_out, v_norm2_g, v_w_up, v_ffn_conv_w, v_ffn_conv_b, v_w_down):
    w = dict(zip(WEIGHTS, (norm1_g, w_in, gate_b, conv_w, conv_b, conv_norm_g, w_conv_out, q_norm_g, k_norm_g,
                           w_attn_out, w_out, norm2_g, w_up, ffn_conv_w, ffn_conv_b, w_down)))
    m = dict(zip(WEIGHTS, (m_norm1_g, m_w_in, m_gate_b, m_conv_w, m_conv_b, m_conv_norm_g, m_w_conv_out, m_q_norm_g,
                           m_k_norm_g, m_w_attn_out, m_w_out, m_norm2_g, m_w_up, m_ffn_conv_w, m_ffn_conv_b, m_w_down)))
    v = dict(zip(WEIGHTS, (v_norm1_g, v_w_in, v_gate_b, v_conv_w, v_conv_b, v_conv_norm_g, v_w_conv_out, v_q_norm_g,
                           v_k_norm_g, v_w_attn_out, v_w_out, v_norm2_g, v_w_up, v_ffn_conv_w, v_ffn_conv_b, v_w_down)))
    dims = Dims(d_model=x.shape[-1], batch_local=x.shape[0], seq=x.shape[1], d_ff=w_down.shape[1] * N_CHIPS)
    return _step(dims, x, loss_target, w, m, v)
```

```python
import functools
import math
from typing import NamedTuple

import jax
import jax.numpy as jnp
from jax import lax
from jax.experimental import pallas as pl
from jax.experimental.pallas import tpu as pltpu

F32 = jnp.float32
BF16 = jnp.bfloat16

RMS_EPS = 1e-6
MASKED_SCORE = -1e30
ATTN_BLOCK = 128
DILATIONS = (1, 4, 16)
CONV_HALO = 32
FFN_HALO = 8
ADAM_LR, ADAM_B1, ADAM_B2, ADAM_EPS, ADAM_WD, ADAM_STEP = 0.001, 0.9, 0.999, 1e-08, 0.01, 10
V7X_VMEM_LIMIT_BYTES = 56 * 2 ** 20
N_CHIPS = 4
MESH = pl.DeviceIdType.MESH


class Dims(NamedTuple):
    d_model: int = 1024
    n_heads: int = 16
    head_dim: int = 64
    d_ff: int = 2816
    seq: int = 2048
    batch_local: int = 2
    conv_width: int = 31
    ffn_conv_width: int = 3

    @property
    def tokens(self):
        return self.seq * self.batch_local


def _params(*semantics):
    return pltpu.CompilerParams(dimension_semantics=semantics, vmem_limit_bytes=V7X_VMEM_LIMIT_BYTES)


ANY = pl.BlockSpec(memory_space=pl.ANY)


def _ordered(body, n_inputs, after):
    after = [] if after is None else list(after) if isinstance(after, (list, tuple)) else [after]
    if not after:
        return body, [], []

    def wrapped(*refs):
        return body(*refs[:n_inputs], *refs[n_inputs + len(after):])

    return wrapped, [ANY] * len(after), after


def _pick(n, target, mult=128):
    if n <= target:
        return n
    best = None
    for t in range(mult, target + 1, mult):
        if n % t == 0:
            best = t
    assert best is not None, (n, target, mult)
    return best


def _sigmoid(v):
    return 1.0 / (1.0 + jnp.exp(-v))


def _mm_nn(a, w, *, out_dtype, name, residual=None, after=None, tm=1024, tn=1408, tk=2816):
    m, k = a.shape
    nsh, k2, c = w.shape
    assert k == k2 and a.dtype == BF16 and w.dtype == BF16
    n = nsh * c
    tm, tn, tk = _pick(m, tm, 8), _pick(c, tn), _pick(k, tk)
    nk, cpn = k // tk, c // tn

    def body(*refs):
        if residual is None:
            a_ref, w_ref, o_ref, acc = refs
        else:
            a_ref, w_ref, r_ref, o_ref, acc = refs
        prod = jnp.dot(a_ref[...], w_ref[...], preferred_element_type=F32)

        def finish(total):
            if residual is not None:
                total = total + r_ref[...]
            o_ref[...] = total.astype(out_dtype)

        if nk == 1:
            finish(prod)
        else:
            kk = pl.program_id(2)

            @pl.when(kk == 0)
            def _():
                acc[...] = prod

            @pl.when(kk > 0)
            def _():
                acc[...] += prod

            @pl.when(kk == nk - 1)
            def _():
                finish(acc[...])

    in_specs = [pl.BlockSpec((tm, tk), lambda i, j, kk: (i, kk)),
                pl.BlockSpec((None, tk, tn), lambda i, j, kk: (j // cpn, kk, j % cpn))]
    args = [a, w]
    if residual is not None:
        in_specs.append(pl.BlockSpec((tm, tn), lambda i, j, kk: (i, j)))
        args.append(residual)
    body, more_specs, more_args = _ordered(body, len(args), after)
    return pl.pallas_call(
        body, name=name, grid=(m // tm, n // tn, nk),
        in_specs=in_specs + more_specs, out_specs=pl.BlockSpec((tm, tn), lambda i, j, kk: (i, j)),
        out_shape=jax.ShapeDtypeStruct((m, n), out_dtype),
        scratch_shapes=[pltpu.VMEM((tm, tn) if nk > 1 else (8, 128), F32)],
        compiler_params=_params("parallel", "parallel", "arbitrary"),
    )(*args, *more_args)


def _mm_nt(a, w, *, out_dtype, name, after=None, tm=1024, tn=1408, tk=1792):
    m, k = a.shape
    nsh, r, c = w.shape
    assert k == nsh * c and a.dtype == BF16 and w.dtype == BF16
    tm, tn, tk = _pick(m, tm, 8), _pick(r, tn), _pick(c, tk)
    nk, cpk = k // tk, c // tk

    def body(a_ref, w_ref, o_ref, acc):
        prod = lax.dot_general(a_ref[...], w_ref[...], (((1,), (1,)), ((), ())), preferred_element_type=F32)
        if nk == 1:
            o_ref[...] = prod.astype(out_dtype)
        else:
            kk = pl.program_id(2)

            @pl.when(kk == 0)
            def _():
                acc[...] = prod

            @pl.when(kk > 0)
            def _():
                acc[...] += prod

            @pl.when(kk == nk - 1)
            def _():
                o_ref[...] = acc[...].astype(out_dtype)

    body, more_specs, more_args = _ordered(body, 2, after)
    return pl.pallas_call(
        body, name=name, grid=(m // tm, r // tn, nk),
        in_specs=[pl.BlockSpec((tm, tk), lambda i, j, kk: (i, kk)),
                  pl.BlockSpec((None, tn, tk), lambda i, j, kk: (kk // cpk, j, kk % cpk))] + more_specs,
        out_specs=pl.BlockSpec((tm, tn), lambda i, j, kk: (i, j)),
        out_shape=jax.ShapeDtypeStruct((m, r), out_dtype),
        scratch_shapes=[pltpu.VMEM((tm, tn) if nk > 1 else (8, 128), F32)],
        compiler_params=_params("parallel", "parallel", "arbitrary"),
    )(a, w, *more_args)


MM_TN_VMEM_BYTES = 44 * 2 ** 20


def _mm_tn(a, b, *, n_shards, name, tm=1408, tn=1408):
    t, m = a.shape
    t2, n = b.shape
    assert t == t2 and a.dtype == BF16 and b.dtype == BF16
    c = n // n_shards
    tm, tn = _pick(m, tm), _pick(c, tn)
    fixed = 2 * tm * tn * 6
    if 4 * t * (tm + tn) + fixed <= MM_TN_VMEM_BYTES:
        tk = t
    else:
        tk = _pick(t, (MM_TN_VMEM_BYTES - fixed - 4 * tm * tn) // (4 * (tm + tn)), 8)
    nk, cpn = t // tk, c // tn

    def body(a_ref, b_ref, o_ref, ob_ref, acc):
        kk = pl.program_id(2)
        prod = lax.dot_general(a_ref[...], b_ref[...], (((0,), (0,)), ((), ())), preferred_element_type=F32)

        def finish(total):
            o_ref[...] = total
            ob_ref[...] = total.astype(BF16)

        if nk == 1:
            finish(prod)
        else:
            @pl.when(kk == 0)
            def _():
                acc[...] = prod

            @pl.when(kk > 0)
            def _():
                acc[...] += prod

            @pl.when(kk == nk - 1)
            def _():
                finish(acc[...])

    out_spec = pl.BlockSpec((None, tm, tn), lambda i, j, kk: (j // cpn, i, j % cpn))
    return pl.pallas_call(
        body, name=name, grid=(m // tm, n // tn, nk),
        in_specs=[pl.BlockSpec((tk, tm), lambda i, j, kk: (kk, i)),
                  pl.BlockSpec((tk, tn), lambda i, j, kk: (kk, j))],
        out_specs=[out_spec, out_spec],
        out_shape=[jax.ShapeDtypeStruct((n_shards, m, c), F32), jax.ShapeDtypeStruct((n_shards, m, c), BF16)],
        scratch_shapes=[pltpu.VMEM((tm, tn) if nk > 1 else (8, 128), F32)],
        compiler_params=_params("parallel", "parallel", "arbitrary"),
    )(a, b)


def _row_spec(tr, width, col=0):
    return pl.BlockSpec((tr, width), lambda i, col=col: (i, col))


def _vec_spec(width, col=0):
    return pl.BlockSpec((1, width), lambda i, col=col: (0, col))


def _accumulate(ref, value, first):
    @pl.when(first)
    def _():
        ref[...] = value

    @pl.when(jnp.logical_not(first))
    def _():
        ref[...] += value


def _rmsnorm_fwd(x, g, *, name, tr=512):
    t, d = x.shape
    tr = _pick(t, tr, 8)

    def body(x_ref, g_ref, o_ref):
        xv = x_ref[...]
        r = lax.rsqrt(jnp.mean(xv * xv, axis=-1, keepdims=True) + RMS_EPS)
        o_ref[...] = (xv * r * g_ref[...]).astype(BF16)

    return pl.pallas_call(
        body, name=name, grid=(t // tr,),
        in_specs=[_row_spec(tr, d), _vec_spec(d)], out_specs=_row_spec(tr, d),
        out_shape=jax.ShapeDtypeStruct((t, d), BF16), compiler_params=_params("parallel"),
    )(x, g)


def _rmsnorm_bwd(x, g, dy, dres, *, name, want_bf16, tr=512):
    t, d = x.shape
    tr = _pick(t, tr, 8)

    def body(x_ref, g_ref, dy_ref, dres_ref, *outs):
        dx_ref, dg_ref = outs[0], outs[-1]
        xv, dyv = x_ref[...], dy_ref[...].astype(F32)
        r = lax.rsqrt(jnp.mean(xv * xv, axis=-1, keepdims=True) + RMS_EPS)
        gy = dyv * g_ref[...]
        dx = dres_ref[...] + r * gy - xv * (r * r * r) * jnp.mean(xv * gy, axis=-1, keepdims=True)
        dx_ref[...] = dx
        if want_bf16:
            outs[1][...] = dx.astype(BF16)
        _accumulate(dg_ref, jnp.sum(dyv * xv * r, axis=0, keepdims=True), pl.program_id(0) == 0)

    out_shape = [jax.ShapeDtypeStruct((t, d), F32)]
    out_specs = [_row_spec(tr, d)]
    if want_bf16:
        out_shape.append(jax.ShapeDtypeStruct((t, d), BF16))
        out_specs.append(_row_spec(tr, d))
    out_shape.append(jax.ShapeDtypeStruct((1, d), F32))
    out_specs.append(_vec_spec(d))
    return pl.pallas_call(
        body, name=name, grid=(t // tr,),
        in_specs=[_row_spec(tr, d), _vec_spec(d), _row_spec(tr, d), _row_spec(tr, d)],
        out_specs=out_specs, out_shape=out_shape, compiler_params=_params("arbitrary"),
    )(x, g, dy, dres)


def _head_mean(v, ones_ref, head_dim):
    hi = v.astype(BF16)
    lo = (v - hi.astype(F32)).astype(BF16)
    e = ones_ref[...]
    total = jnp.dot(hi, e, preferred_element_type=F32) + jnp.dot(lo, e, preferred_element_type=F32)
    return total * (1.0 / head_dim)


def _qkv_fwd(z, gq, gk, head_ones, dims, *, name, tr=256):
    t = z.shape[0]
    a = dims.n_heads * dims.head_dim
    tr = _pick(t, tr, 8)
    q_scale = dims.head_dim ** -0.5

    def body(q_ref, k_ref, v_ref, gq_ref, gk_ref, e_ref, qo_ref, ko_ref, vo_ref):
        qv, kv = q_ref[...], k_ref[...]
        rq = lax.rsqrt(_head_mean(qv * qv, e_ref, dims.head_dim) + RMS_EPS)
        rk = lax.rsqrt(_head_mean(kv * kv, e_ref, dims.head_dim) + RMS_EPS)
        qo_ref[...] = (qv * rq * gq_ref[...] * q_scale).astype(BF16)
        ko_ref[...] = (kv * rk * gk_ref[...]).astype(BF16)
        vo_ref[...] = v_ref[...].astype(BF16)

    return pl.pallas_call(
        body, name=name, grid=(t // tr,),
        in_specs=[_row_spec(tr, a, 2), _row_spec(tr, a, 3), _row_spec(tr, a, 4), _vec_spec(a), _vec_spec(a),
                  pl.BlockSpec((a, a), lambda i: (0, 0))],
        out_specs=[_row_spec(tr, a)] * 3, out_shape=[jax.ShapeDtypeStruct((t, a), BF16)] * 3,
        compiler_params=_params("parallel"),
    )(z, z, z, gq, gk, head_ones)


def _qkv_bwd(z, dqs, dks, dvs, gq, gk, head_ones, dims, *, name, tr=256):
    t = z.shape[0]
    a = dims.n_heads * dims.head_dim
    tr = _pick(t, tr, 8)
    q_scale = dims.head_dim ** -0.5
    ng = len(dqs)

    def body(*refs):
        q_ref, k_ref = refs[:2]
        dq_refs, dk_refs, dv_refs = refs[2:2 + ng], refs[2 + ng:2 + 2 * ng], refs[2 + 2 * ng:2 + 3 * ng]
        gq_ref, gk_ref, e_ref = refs[2 + 3 * ng:5 + 3 * ng]
        dz_ref, dgq_ref, dgk_ref = refs[5 + 3 * ng:]
        first = pl.program_id(0) == 0

        def norm_bwd(x_ref, d_refs, g_ref, scale, col, dg_ref):
            xv = x_ref[...]
            dy = sum(r[...] for r in d_refs) * scale
            r = lax.rsqrt(_head_mean(xv * xv, e_ref, dims.head_dim) + RMS_EPS)
            gy = dy * g_ref[...]
            dx = r * gy - xv * (r * r * r) * _head_mean(xv * gy, e_ref, dims.head_dim)
            dz_ref[:, col * a:(col + 1) * a] = dx.astype(BF16)
            _accumulate(dg_ref, jnp.sum(dy * xv * r, axis=0, keepdims=True), first)

        norm_bwd(q_ref, dq_refs, gq_ref, q_scale, 0, dgq_ref)
        norm_bwd(k_ref, dk_refs, gk_ref, 1.0, 1, dgk_ref)
        dz_ref[:, 2 * a:3 * a] = sum(r[...] for r in dv_refs).astype(BF16)

    in_specs = ([_row_spec(tr, a, 2), _row_spec(tr, a, 3)] + [_row_spec(tr, a)] * (3 * ng)
                + [_vec_spec(a), _vec_spec(a), pl.BlockSpec((a, a), lambda i: (0, 0))])
    return pl.pallas_call(
        body, name=name, grid=(t // tr,), in_specs=in_specs,
        out_specs=[_row_spec(tr, 3 * a), _vec_spec(a), _vec_spec(a)],
        out_shape=[jax.ShapeDtypeStruct((t, 3 * a), BF16)] + [jax.ShapeDtypeStruct((1, a), F32)] * 2,
        compiler_params=_params("arbitrary"),
    )(z, z, *dqs, *dks, *dvs, gq, gk, head_ones)


CONV_ROWS = 16


def _seq_specs(dims, ts, width, halo, col, *, nxt=False):
    nst, per = dims.seq // ts, ts // halo
    last = dims.tokens // halo - 1
    cur = pl.BlockSpec((ts, width), lambda b, i: (b * nst + i, col))
    if nxt:
        edge = pl.BlockSpec((halo, width), lambda b, i: (jnp.minimum((b * nst + i + 1) * per, last), col))
    else:
        edge = pl.BlockSpec((halo, width), lambda b, i: (jnp.maximum((b * nst + i) * per - 1, 0), col))
    return cur, edge


SUBLANES = 8


def _shifted_copies(buf, shifted):
    rows = shifted.shape[1]
    for s in range(1, SUBLANES):
        shifted[s - 1] = buf[pl.ds(s, rows), :]


def _window(buf, shifted, start, size):
    a, s = divmod(start, SUBLANES)
    src = buf if s == 0 else shifted.at[s - 1]
    return src[pl.ds(SUBLANES * a, size), :]


def _conv_branch_fwd(z, w, b, g, dims, *, name, ts=128):
    t, c, kw = z.shape[0], dims.d_model, dims.conv_width
    base = CONV_HALO - (kw - 1)

    def body(av_ref, hv_ref, ag_ref, hg_ref, w_ref, b_ref, g_ref, a1_ref, a3_ref, buf, shifted):
        i = pl.program_id(1)
        buf[CONV_HALO:, :] = av_ref[...].astype(F32) * _sigmoid(ag_ref[...].astype(F32))
        buf[0:CONV_HALO, :] = jnp.where(i > 0, hv_ref[...].astype(F32) * _sigmoid(hg_ref[...].astype(F32)), 0.0)
        _shifted_copies(buf, shifted)
        for r0 in range(0, ts, CONV_ROWS):
            acc = jnp.broadcast_to(b_ref[...], (CONV_ROWS, c))
            for k in range(kw):
                acc = acc + w_ref[k:k + 1, :] * _window(buf, shifted, r0 + base + k, CONV_ROWS)
            a1_ref[r0:r0 + CONV_ROWS, :] = acc
            a2 = acc * lax.rsqrt(jnp.mean(acc * acc, axis=-1, keepdims=True) + RMS_EPS) * g_ref[...]
            a3_ref[r0:r0 + CONV_ROWS, :] = (a2 * _sigmoid(a2)).astype(BF16)

    vec = pl.BlockSpec((1, c), lambda b, i: (0, 0))
    out = pl.BlockSpec((ts, c), lambda b, i: (b * (dims.seq // ts) + i, 0))
    return pl.pallas_call(
        body, name=name, grid=(dims.batch_local, dims.seq // ts),
        in_specs=[*_seq_specs(dims, ts, c, CONV_HALO, 0), *_seq_specs(dims, ts, c, CONV_HALO, 1),
                  pl.BlockSpec((CONV_HALO, c), lambda b, i: (0, 0)), vec, vec],
        out_specs=[out, out],
        out_shape=[jax.ShapeDtypeStruct((t, c), F32), jax.ShapeDtypeStruct((t, c), BF16)],
        scratch_shapes=[pltpu.VMEM((CONV_HALO + ts, c), F32),
                        pltpu.VMEM((SUBLANES - 1, CONV_HALO + ts - SUBLANES, c), F32)],
        compiler_params=_params("parallel", "parallel"),
    )(z, z, z, z, w, b, g)


def _conv_norm_bwd(da3, a1, g, *, name, tr=256):
    t, c = a1.shape
    tr = _pick(t, tr, 8)

    def body(d_ref, a_ref, g_ref, o_ref, dg_ref):
        a1v, gv = a_ref[...], g_ref[...]
        r = lax.rsqrt(jnp.mean(a1v * a1v, axis=-1, keepdims=True) + RMS_EPS)
        a2 = a1v * r * gv
        sg = _sigmoid(a2)
        da2 = d_ref[...].astype(F32) * sg * (1.0 + a2 * (1.0 - sg))
        gy = da2 * gv
        o_ref[...] = r * gy - a1v * (r * r * r) * jnp.mean(a1v * gy, axis=-1, keepdims=True)
        _accumulate(dg_ref, jnp.sum(da2 * a1v * r, axis=0, keepdims=True), pl.program_id(0) == 0)

    return pl.pallas_call(
        body, name=name, grid=(t // tr,),
        in_specs=[_row_spec(tr, c), _row_spec(tr, c), _vec_spec(c)],
        out_specs=[_row_spec(tr, c), _vec_spec(c)],
        out_shape=[jax.ShapeDtypeStruct((t, c), F32), jax.ShapeDtypeStruct((1, c), F32)],
        compiler_params=_params("arbitrary"),
    )(da3, a1, g)


def _conv_branch_bwd(da1, z, w, dims, *, name, ts=128):
    t, c, kw = z.shape[0], dims.d_model, dims.conv_width
    nst = dims.seq // ts
    base = CONV_HALO - (kw - 1)

    def body(d_ref, dn_ref, av_ref, hv_ref, ag_ref, hg_ref, w_ref, dz_ref, dw_ref, db_ref, abuf, dbuf, ashift, dshift):
        i = pl.program_id(1)
        first = jnp.logical_and(pl.program_id(0) == 0, i == 0)
        abuf[CONV_HALO:, :] = av_ref[...].astype(F32) * _sigmoid(ag_ref[...].astype(F32))
        abuf[0:CONV_HALO, :] = jnp.where(i > 0, hv_ref[...].astype(F32) * _sigmoid(hg_ref[...].astype(F32)), 0.0)
        d1 = d_ref[...]
        dbuf[0:ts, :] = d1
        dbuf[ts:, :] = jnp.where(i < nst - 1, dn_ref[...], 0.0)
        _shifted_copies(abuf, ashift)
        _shifted_copies(dbuf, dshift)

        @pl.when(first)
        def _():
            dw_ref[...] = jnp.zeros_like(dw_ref)
            db_ref[...] = jnp.zeros_like(db_ref)

        db_ref[...] += jnp.sum(d1, axis=0, keepdims=True)
        for k in range(kw):
            dw_ref[k:k + 1, :] += jnp.sum(d1 * _window(abuf, ashift, base + k, ts), axis=0, keepdims=True)
        for r0 in range(0, ts, CONV_ROWS):
            acc = jnp.zeros((CONV_ROWS, c), F32)
            for k in range(kw):
                acc = acc + w_ref[k:k + 1, :] * _window(dbuf, dshift, r0 + (kw - 1) - k, CONV_ROWS)
            av = av_ref[r0:r0 + CONV_ROWS, :].astype(F32)
            sg = _sigmoid(ag_ref[r0:r0 + CONV_ROWS, :].astype(F32))
            dz_ref[r0:r0 + CONV_ROWS, 0:c] = (acc * sg).astype(BF16)
            dz_ref[r0:r0 + CONV_ROWS, c:2 * c] = (acc * av * sg * (1.0 - sg)).astype(BF16)

    cur, nxt = _seq_specs(dims, ts, c, CONV_HALO, 0, nxt=True)
    return pl.pallas_call(
        body, name=name, grid=(dims.batch_local, nst),
        in_specs=[cur, nxt, *_seq_specs(dims, ts, c, CONV_HALO, 0), *_seq_specs(dims, ts, c, CONV_HALO, 1),
                  pl.BlockSpec((CONV_HALO, c), lambda b, i: (0, 0))],
        out_specs=[pl.BlockSpec((ts, 2 * c), lambda b, i: (b * nst + i, 0)),
                   pl.BlockSpec((CONV_HALO, c), lambda b, i: (0, 0)), pl.BlockSpec((1, c), lambda b, i: (0, 0))],
        out_shape=[jax.ShapeDtypeStruct((t, 2 * c), BF16), jax.ShapeDtypeStruct((CONV_HALO, c), F32),
                   jax.ShapeDtypeStruct((1, c), F32)],
        scratch_shapes=[pltpu.VMEM((CONV_HALO + ts, c), F32)] * 2
        + [pltpu.VMEM((SUBLANES - 1, CONV_HALO + ts - SUBLANES, c), F32)] * 2,
        compiler_params=_params("arbitrary", "arbitrary"),
    )(da1, da1, z, z, z, z, w)


FFN_ROWS = 16
FFN_COLS = 256


def _ffn_chunks(ts, f):
    cw = _pick(f, FFN_COLS)
    return [(r0, c0, cw) for r0 in range(0, ts, FFN_ROWS) for c0 in range(0, f, cw)]


def _ffn_conv(buf, w_ref, b_ref, r0, cols, kw):
    base = FFN_HALO - (kw - 1)
    u = jnp.broadcast_to(b_ref[:, cols], (FFN_ROWS, cols.stop - cols.start))
    for k in range(kw):
        u = u + w_ref[k:k + 1, cols] * buf[pl.ds(r0 + base + k, FFN_ROWS), cols]
    return u


def _ffn_act_fwd(up, w, b, dims, *, name, ts=128):
    t, f, kw = up.shape[0], dims.d_ff, dims.ffn_conv_width

    def body(up_ref, h_ref, w_ref, b_ref, o_ref, buf):
        buf[FFN_HALO:, :] = up_ref[...]
        buf[0:FFN_HALO, :] = jnp.where(pl.program_id(1) > 0, h_ref[...], 0.0)
        for r0, c0, cw in _ffn_chunks(ts, f):
            uv = _ffn_conv(buf, w_ref, b_ref, r0, slice(c0, c0 + cw), kw)
            ug = _ffn_conv(buf, w_ref, b_ref, r0, slice(f + c0, f + c0 + cw), kw)
            o_ref[r0:r0 + FFN_ROWS, c0:c0 + cw] = (ug * _sigmoid(ug) * uv).astype(BF16)

    full = lambda rows: pl.BlockSpec((rows, 2 * f), lambda b_, i: (0, 0))
    return pl.pallas_call(
        body, name=name, grid=(dims.batch_local, dims.seq // ts),
        in_specs=[*_seq_specs(dims, ts, 2 * f, FFN_HALO, 0), full(FFN_HALO), full(1)],
        out_specs=pl.BlockSpec((ts, f), lambda b_, i: (b_ * (dims.seq // ts) + i, 0)),
        out_shape=jax.ShapeDtypeStruct((t, f), BF16),
        scratch_shapes=[pltpu.VMEM((FFN_HALO + ts, 2 * f), F32)],
        compiler_params=_params("parallel", "parallel"),
    )(up, up, w, b)


def _ffn_act_bwd(dact, up, w, b, dims, *, name, ts=128):
    t, f, kw = up.shape[0], dims.d_ff, dims.ffn_conv_width
    base = FFN_HALO - (kw - 1)

    def body(d_ref, up_ref, h_ref, w_ref, b_ref, du_ref, dw_ref, db_ref, buf, moved):
        i = pl.program_id(1)
        first = jnp.logical_and(pl.program_id(0) == 0, i == 0)
        buf[FFN_HALO:, :] = up_ref[...]
        buf[0:FFN_HALO, :] = jnp.where(i > 0, h_ref[...], 0.0)
        taps, used = [], 0
        for k in range(kw):
            if (base + k) % SUBLANES:
                moved[used] = buf[pl.ds(base + k, ts), :]
                taps.append((moved.at[used], 0))
                used += 1
            else:
                taps.append((buf, base + k))

        def conv(r0, cols):
            u = jnp.broadcast_to(b_ref[:, cols], (FFN_ROWS, cols.stop - cols.start))
            for k, (src, off) in enumerate(taps):
                u = u + w_ref[k:k + 1, cols] * src[pl.ds(off + r0, FFN_ROWS), cols]
            return u

        for r0, c0, cw in _ffn_chunks(ts, f):
            vcols, gcols = slice(c0, c0 + cw), slice(f + c0, f + c0 + cw)
            uv = conv(r0, vcols)
            ug = conv(r0, gcols)
            d = d_ref[r0:r0 + FFN_ROWS, vcols].astype(F32)
            sg = _sigmoid(ug)
            du_ref[r0:r0 + FFN_ROWS, vcols] = d * ug * sg
            du_ref[r0:r0 + FFN_ROWS, gcols] = d * uv * sg * (1.0 + ug * (1.0 - sg))

        @pl.when(first)
        def _():
            dw_ref[...] = jnp.zeros_like(dw_ref)
            db_ref[...] = jnp.zeros_like(db_ref)

        du = du_ref[...]
        db_ref[...] += jnp.sum(du, axis=0, keepdims=True)
        for k, (src, off) in enumerate(taps):
            dw_ref[k:k + 1, :] += jnp.sum(du * src[pl.ds(off, ts), :], axis=0, keepdims=True)

    nst = dims.seq // ts
    n_moved = sum(1 for k in range(kw) if (base + k) % SUBLANES)
    full = lambda rows: pl.BlockSpec((rows, 2 * f), lambda b_, i: (0, 0))
    return pl.pallas_call(
        body, name=name, grid=(dims.batch_local, nst),
        in_specs=[pl.BlockSpec((ts, f), lambda b_, i: (b_ * nst + i, 0)),
                  *_seq_specs(dims, ts, 2 * f, FFN_HALO, 0), full(FFN_HALO), full(1)],
        out_specs=[pl.BlockSpec((ts, 2 * f), lambda b_, i: (b_ * nst + i, 0)), full(FFN_HALO), full(1)],
        out_shape=[jax.ShapeDtypeStruct((t, 2 * f), F32), jax.ShapeDtypeStruct((FFN_HALO, 2 * f), F32),
                   jax.ShapeDtypeStruct((1, 2 * f), F32)],
        scratch_shapes=[pltpu.VMEM((FFN_HALO + ts, 2 * f), F32), pltpu.VMEM((n_moved, ts, 2 * f), F32)],
        compiler_params=_params("arbitrary", "arbitrary"),
    )(dact, up, up, w, b)


def _ffn_conv_bwd(du, w, dims, *, name, ts=128):
    t, f2 = du.shape
    kw = dims.ffn_conv_width
    nst = dims.seq // ts

    def body(d_ref, dn_ref, w_ref, o_ref, buf):
        buf[0:ts, :] = d_ref[...]
        buf[ts:, :] = jnp.where(pl.program_id(1) < nst - 1, dn_ref[...], 0.0)
        for r0, c0, cw in _ffn_chunks(ts, f2):
            cols = slice(c0, c0 + cw)
            acc = jnp.zeros((FFN_ROWS, cw), F32)
            for k in range(kw):
                acc = acc + w_ref[k:k + 1, cols] * buf[pl.ds(r0 + (kw - 1) - k, FFN_ROWS), cols]
            o_ref[r0:r0 + FFN_ROWS, cols] = acc.astype(BF16)

    return pl.pallas_call(
        body, name=name, grid=(dims.batch_local, nst),
        in_specs=[*_seq_specs(dims, ts, f2, FFN_HALO, 0, nxt=True), pl.BlockSpec((FFN_HALO, f2), lambda b_, i: (0, 0))],
        out_specs=pl.BlockSpec((ts, f2), lambda b_, i: (b_ * nst + i, 0)),
        out_shape=jax.ShapeDtypeStruct((t, f2), BF16),
        scratch_shapes=[pltpu.VMEM((ts + FFN_HALO, f2), F32)],
        compiler_params=_params("parallel", "parallel"),
    )(du, du, w)


def _alibi_slope(h, n_heads):
    return 2.0 ** (-8.0 * (h + 1) / n_heads)


def _dot_nt(a, b):
    return lax.dot_general(a, b, (((1,), (1,)), ((), ())), preferred_element_type=F32)


def _dot_tn(a, b):
    return lax.dot_general(a, b, (((0,), (0,)), ((), ())), preferred_element_type=F32)


def _attn_view(x, dims, dil):
    return x.reshape(dims.batch_local, dims.seq // dil, dil * x.shape[-1])


def _attn_fwd_group(q, k, v, state, dims, dil, *, last, name):
    t, a = q.shape
    assert 2 * dims.head_dim == 128 and dims.n_heads % 2 == 0
    blk, hd = ATTN_BLOCK, dims.head_dim
    nb = dims.seq // dil // blk
    has_prev = nb > 1
    nkeys = 2 * blk if has_prev else blk

    def body(*refs):
        it = iter(refs)
        q_ref, kc_ref, vc_ref = next(it), next(it), next(it)
        kp_ref, vp_ref = (next(it), next(it)) if has_prev else (None, None)
        m_in, l_in, acc_in = (next(it), next(it), next(it)) if state is not None else (None, None, None)
        outs = list(it)
        iq = lax.broadcasted_iota(jnp.int32, (blk, nkeys), 0)
        jk = lax.broadcasted_iota(jnp.int32, (blk, nkeys), 1)
        if has_prev:
            steps = iq + blk - jk
            valid = (steps >= 0) & (steps <= blk) & ((jk >= blk) | (pl.program_id(2) > 0))
        else:
            steps = iq - jk
            valid = steps >= 0
        dist = steps.astype(F32) * float(dil)
        low = lax.broadcasted_iota(jnp.int32, (blk, 2 * hd), 1) < hd
        for hp in range(dims.n_heads // 2):
            sl = slice(2 * hd * hp, 2 * hd * (hp + 1))
            q2 = q_ref[:, sl]
            if has_prev:
                kcat = jnp.concatenate([kp_ref[:, sl], kc_ref[:, sl]], axis=0)
                vcat = jnp.concatenate([vp_ref[:, sl], vc_ref[:, sl]], axis=0)
            else:
                kcat, vcat = kc_ref[:, sl], vc_ref[:, sl]
            halves = []
            for half in range(2):
                col = 2 * hd * hp + hd * half
                qh = jnp.where(low if half == 0 else jnp.logical_not(low), q2, jnp.zeros_like(q2))
                sc = _dot_nt(qh, kcat) - _alibi_slope(2 * hp + half, dims.n_heads) * dist
                sc = jnp.where(valid, sc, MASKED_SCORE)
                row_max = jnp.max(sc, axis=-1, keepdims=True)
                if state is None:
                    m_new = row_max
                    p = jnp.exp(sc - m_new)
                    alpha = None
                    l_new = jnp.sum(p, axis=-1, keepdims=True)
                else:
                    m_old = m_in[:, col:col + 1]
                    m_new = jnp.maximum(m_old, row_max)
                    p = jnp.exp(sc - m_new)
                    alpha = jnp.exp(m_old - m_new)
                    l_new = alpha * l_in[:, col:col + 1] + jnp.sum(p, axis=-1, keepdims=True)
                pv = jnp.dot(p.astype(BF16), vcat, preferred_element_type=F32)
                halves.append((m_new, l_new, alpha, pv))
            (m_a, l_a, al_a, pv_a), (m_b, l_b, al_b, pv_b) = halves
            if state is None:
                acc = jnp.where(low, pv_a, pv_b)
            else:
                old = acc_in[:, sl]
                acc = jnp.where(low, al_a * old + pv_a, al_b * old + pv_b)
            m2 = jnp.where(low, m_a, m_b)
            l2 = jnp.where(low, l_a, l_b)
            if last:
                outs[0][:, sl] = (acc / l2).astype(BF16)
                outs[1][:, sl] = m2 + jnp.log(l2)
            else:
                outs[0][:, sl] = m2
                outs[1][:, sl] = l2
                outs[2][:, sl] = acc

    cur = pl.BlockSpec((None, blk, a), lambda b, r, i: (b, i, r))
    prev = pl.BlockSpec((None, blk, a), lambda b, r, i: (b, jnp.maximum(i - 1, 0), r))
    args, in_specs = [q, k, v], [cur, cur, cur]
    if has_prev:
        args += [k, v]
        in_specs += [prev, prev]
    if state is not None:
        args += list(state)
        in_specs += [cur] * 3
    shape = lambda dt: jax.ShapeDtypeStruct((dims.batch_local, dims.seq // dil, dil * a), dt)
    out_shape = [shape(BF16), shape(F32)] if last else [shape(F32)] * 3
    outs = pl.pallas_call(
        body, name=name, grid=(dims.batch_local, dil, nb),
        in_specs=in_specs, out_specs=[cur] * len(out_shape), out_shape=out_shape,
        compiler_params=_params("parallel", "parallel", "parallel"),
    )(*[_attn_view(x, dims, dil) for x in args])
    return tuple(o.reshape(t, a) for o in outs)


def _attn_delta(do, o, head_ones, dims, *, name, tr=512):
    t, a = o.shape
    tr = _pick(t, tr, 8)

    def body(do_ref, o_ref, e_ref, d_ref):
        prod = do_ref[...].astype(F32) * o_ref[...].astype(F32)
        d_ref[...] = _head_mean(prod, e_ref, dims.head_dim) * float(dims.head_dim)

    return pl.pallas_call(
        body, name=name, grid=(t // tr,),
        in_specs=[_row_spec(tr, a), _row_spec(tr, a), pl.BlockSpec((a, a), lambda i: (0, 0))],
        out_specs=_row_spec(tr, a), out_shape=jax.ShapeDtypeStruct((t, a), F32),
        compiler_params=_params("parallel"),
    )(do, o, head_ones)


def _attn_bwd_group(q, k, v, do, lse, delta, dims, dil, *, name):
    t, a = q.shape
    blk, hd = ATTN_BLOCK, dims.head_dim
    nb = dims.seq // dil // blk
    has_next = nb > 1

    def body(*refs):
        k_ref, v_ref, q_ref, do_ref, lse_ref, dl_ref = refs[:6]
        if has_next:
            qn_ref, don_ref, lsen_ref, dln_ref = refs[6:10]
            dq_ref, dk_ref, dv_ref, carry = refs[10:]
        else:
            dq_ref, dk_ref, dv_ref = refs[6:]
        j = pl.program_id(2)
        iq = lax.broadcasted_iota(jnp.int32, (blk, blk), 0)
        jk = lax.broadcasted_iota(jnp.int32, (blk, blk), 1)
        low = lax.broadcasted_iota(jnp.int32, (blk, 2 * hd), 1) < hd

        def pair(hp, qr, dor, lser, dlr, steps, valid):
            sl = slice(2 * hd * hp, 2 * hd * (hp + 1))
            q2, do2, k2, v2 = qr[:, sl], dor[:, sl], k_ref[:, sl], v_ref[:, sl]
            dist = steps.astype(F32) * float(dil)
            dq_h, dk2, dv2 = [], None, None
            for half in range(2):
                col = 2 * hd * hp + hd * half
                mask = low if half == 0 else jnp.logical_not(low)
                qh = jnp.where(mask, q2, jnp.zeros_like(q2))
                doh = jnp.where(mask, do2, jnp.zeros_like(do2))
                sc = _dot_nt(qh, k2) - _alibi_slope(2 * hp + half, dims.n_heads) * dist
                p = jnp.where(valid, jnp.exp(sc - lser[:, col:col + 1]), 0.0)
                ds = p * (_dot_nt(doh, v2) - dlr[:, col:col + 1])
                ds_b, p_b = ds.astype(BF16), p.astype(BF16)
                dq_h.append(jnp.dot(ds_b, k2, preferred_element_type=F32))
                dk_h, dv_h = _dot_tn(ds_b, qh), _dot_tn(p_b, doh)
                dk2 = dk_h if dk2 is None else dk2 + dk_h
                dv2 = dv_h if dv2 is None else dv2 + dv_h
            return sl, jnp.where(low, dq_h[0], dq_h[1]), dk2, dv2

        if has_next:
            @pl.when(j == 0)
            def _():
                carry[...] = jnp.zeros_like(carry)

        for hp in range(dims.n_heads // 2):
            sl, dq2, dk2, dv2 = pair(hp, q_ref, do_ref, lse_ref, dl_ref, iq - jk, iq >= jk)
            dq_ref[:, sl] = (carry[:, sl] + dq2) if has_next else dq2
            dk_ref[:, sl] = dk2
            dv_ref[:, sl] = dv2

        if has_next:
            @pl.when(j + 1 < nb)
            def _():
                for hp in range(dims.n_heads // 2):
                    sl, dq2, dk2, dv2 = pair(hp, qn_ref, don_ref, lsen_ref, dln_ref, iq - jk + blk, jk >= iq)
                    carry[:, sl] = dq2
                    dk_ref[:, sl] += dk2
                    dv_ref[:, sl] += dv2

    cur = pl.BlockSpec((None, blk, a), lambda b, r, j: (b, j, r))
    nxt = pl.BlockSpec((None, blk, a), lambda b, r, j: (b, jnp.minimum(j + 1, nb - 1), r))
    args, in_specs = [k, v, q, do, lse, delta], [cur] * 6
    if has_next:
        args += [q, do, lse, delta]
        in_specs += [nxt] * 4
    shape = jax.ShapeDtypeStruct((dims.batch_local, dims.seq // dil, dil * a), F32)
    outs = pl.pallas_call(
        body, name=name, grid=(dims.batch_local, dil, nb),
        in_specs=in_specs, out_specs=[cur] * 3, out_shape=[shape] * 3,
        scratch_shapes=[pltpu.VMEM((blk, a), F32)] if has_next else [],
        compiler_params=_params("parallel", "parallel", "arbitrary"),
    )(*[_attn_view(x, dims, dil) for x in args])
    return tuple(o.reshape(t, a) for o in outs)


LANES = 128
MASK_BIAS = 1e30
RESIDUE_DILATIONS = tuple(d for d in DILATIONS if d > 1)


def _rows_to_residues(value, out_ref, scr, d):
    rows, width = value.shape
    for c in range(width // LANES):
        cols = slice(LANES * c, LANES * (c + 1))
        scr[c] = value[:, cols]
        for r in range(d):
            out_ref[r, :, cols] = scr[c, pl.ds(r, rows // d, stride=d), :].astype(out_ref.dtype)


def _residues_to_rows(in_ref, scr, d):
    _, n, width = in_ref.shape
    slabs = []
    for c in range(width // LANES):
        cols = slice(LANES * c, LANES * (c + 1))
        for r in range(d):
            scr[c, pl.ds(r, n, stride=d), :] = in_ref[r, :, cols].astype(F32)
        slabs.append(scr[c])
    return slabs[0] if len(slabs) == 1 else jnp.concatenate(slabs, axis=1)


def _residue_shape(dims, d, width, dtype):
    return jax.ShapeDtypeStruct((dims.batch_local, d, dims.seq // d, width), dtype)


def _residue_spec(dims, d, tr, width):
    tiles = dims.seq // tr
    return pl.BlockSpec((None, d, tr // d, width), lambda i: (i // tiles, 0, i % tiles, 0))


def _head_sum_matrix(dims):
    a = dims.n_heads * dims.head_dim
    head = jnp.arange(a, dtype=jnp.int32) // dims.head_dim
    return (head[:, None] == jnp.arange(LANES, dtype=jnp.int32)[None, :]).astype(BF16)


def _two_pass_dot(v, m):
    hi = v.astype(BF16)
    lo = (v - hi.astype(F32)).astype(BF16)
    return jnp.dot(hi, m, preferred_element_type=F32) + jnp.dot(lo, m, preferred_element_type=F32)


def _qkv_layouts_fwd(z, gq, gk, head_ones, dims, *, name, tr=256):
    t = z.shape[0]
    a = dims.n_heads * dims.head_dim
    q_scale = dims.head_dim ** -0.5
    nres = len(RESIDUE_DILATIONS)

    def body(q_ref, k_ref, v_ref, gq_ref, gk_ref, sum_ref, spread_ref, *rest):
        outs, scr = rest[:-1], rest[-1]
        qv, kv = q_ref[...].astype(F32), k_ref[...].astype(F32)
        mean = lambda val: _two_pass_dot(_two_pass_dot(val, sum_ref[...]), spread_ref[...]) * (1.0 / dims.head_dim)
        rq = lax.rsqrt(mean(qv * qv) + RMS_EPS)
        rk = lax.rsqrt(mean(kv * kv) + RMS_EPS)
        values = (qv * rq * gq_ref[...] * q_scale, kv * rk * gk_ref[...], v_ref[...].astype(F32))
        for j, val in enumerate(values):
            outs[j][...] = val.astype(BF16)
            for g, d in enumerate(RESIDUE_DILATIONS):
                _rows_to_residues(val, outs[3 * (g + 1) + j], scr, d)

    out_specs = [_row_spec(tr, a)] * 3
    out_shape = [jax.ShapeDtypeStruct((t, a), BF16)] * 3
    for d in RESIDUE_DILATIONS:
        out_specs += [_residue_spec(dims, d, tr, a)] * 3
        out_shape += [_residue_shape(dims, d, a, BF16)] * 3
    outs = pl.pallas_call(
        body, name=name, grid=(t // tr,),
        in_specs=[_row_spec(tr, a, 2), _row_spec(tr, a, 3), _row_spec(tr, a, 4), _vec_spec(a), _vec_spec(a),
                  pl.BlockSpec((a, LANES), lambda i: (0, 0)), pl.BlockSpec((LANES, a), lambda i: (0, 0))],
        out_specs=out_specs, out_shape=out_shape,
        scratch_shapes=[pltpu.VMEM((a // LANES, tr, LANES), F32)],
        compiler_params=_params("parallel"),
    )(z, z, z, gq, gk, *head_ones)
    return {d: tuple(outs[3 * g:3 * g + 3]) for g, d in enumerate((1,) + RESIDUE_DILATIONS)}


def _attn_specs(dims, dil, width):
    blk = ATTN_BLOCK
    nb = dims.seq // dil // blk
    if dil == 1:
        grid = (dims.batch_local, nb)
        at = lambda f: pl.BlockSpec((blk, width), lambda b, i: (b * nb + f(i), 0))
    else:
        grid = (dims.batch_local, dil, nb)
        at = lambda f: pl.BlockSpec((None, None, blk, width), lambda b, r, i: (b, r, f(i), 0))
    return grid, at(lambda i: i), at(lambda i: jnp.maximum(i - 1, 0)), at(lambda i: jnp.minimum(i + 1, nb - 1))


def _head_slopes(n_heads):
    h = lax.broadcasted_iota(jnp.int32, (n_heads, 1, 1), 0).astype(F32)
    return jnp.exp((h + 1.0) * (-8.0 / n_heads * math.log(2.0)))


def _pair_masks(hd):
    low = lax.broadcasted_iota(jnp.int32, (1, 2 * hd), 1) < hd
    return low, jnp.logical_not(low)


def _attn_fwd(q, k, v, dims, dil, *, name):
    a = dims.n_heads * dims.head_dim
    heads, hd, blk = dims.n_heads, dims.head_dim, ATTN_BLOCK
    assert 2 * hd == LANES and heads % 2 == 0 and heads <= LANES
    nb = dims.seq // dil // blk
    has_prev = nb > 1
    nkeys = 2 * blk if has_prev else blk
    grid, cur, prev, _ = _attn_specs(dims, dil, a)
    _, cur_stat, _, _ = _attn_specs(dims, dil, LANES)

    def body(*refs):
        if has_prev:
            q_ref, kc_ref, vc_ref, kp_ref, vp_ref, o_ref, lse_ref, s_scr, p_scr = refs
        else:
            q_ref, kc_ref, vc_ref, o_ref, lse_ref, s_scr, p_scr = refs
        low, high = _pair_masks(hd)

        def keys(cur_ref, prev_ref, sl):
            return jnp.concatenate([prev_ref[:, sl], cur_ref[:, sl]], axis=0) if has_prev else cur_ref[:, sl]

        for hp in range(heads // 2):
            sl = slice(LANES * hp, LANES * (hp + 1))
            q2 = q_ref[:, sl]
            kcat = keys(kc_ref, kp_ref if has_prev else None, sl)
            s_scr[2 * hp] = _dot_nt(jnp.where(low, q2, jnp.zeros_like(q2)), kcat)
            s_scr[2 * hp + 1] = _dot_nt(jnp.where(high, q2, jnp.zeros_like(q2)), kcat)

        iq = lax.broadcasted_iota(jnp.int32, (blk, nkeys), 0)
        jk = lax.broadcasted_iota(jnp.int32, (blk, nkeys), 1)
        if has_prev:
            steps = iq + blk - jk
            valid = (steps >= 0) & (steps <= blk) & ((jk >= blk) | (pl.program_id(len(grid) - 1) > 0))
        else:
            steps = iq - jk
            valid = steps >= 0
        bias = jnp.where(valid, steps.astype(F32) * (-float(dil)), -MASK_BIAS)
        s = s_scr[...] + _head_slopes(heads) * bias[None]
        m = jnp.max(s, axis=-1, keepdims=True)
        p = jnp.exp(s - m)
        l = jnp.sum(p, axis=-1, keepdims=True)
        p_scr[...] = p.astype(BF16)
        inv = 1.0 / l
        lse = m + jnp.log(l)

        lane = lax.broadcasted_iota(jnp.int32, (blk, LANES), 1)
        stat = jnp.zeros((blk, LANES), F32)
        for hp in range(heads // 2):
            sl = slice(LANES * hp, LANES * (hp + 1))
            vcat = keys(vc_ref, vp_ref if has_prev else None, sl)
            pv_a = jnp.dot(p_scr[2 * hp], vcat, preferred_element_type=F32) * inv[2 * hp]
            pv_b = jnp.dot(p_scr[2 * hp + 1], vcat, preferred_element_type=F32) * inv[2 * hp + 1]
            o_ref[:, sl] = jnp.where(low, pv_a, pv_b)
            stat = jnp.where(lane == 2 * hp, lse[2 * hp], stat)
            stat = jnp.where(lane == 2 * hp + 1, lse[2 * hp + 1], stat)
        lse_ref[...] = stat

    lead = q.shape[:-2]
    rows = q.shape[-2]
    o, lse = pl.pallas_call(
        body, name=name, grid=grid,
        in_specs=[cur, cur, cur] + ([prev, prev] if has_prev else []),
        out_specs=[cur, cur_stat],
        out_shape=[jax.ShapeDtypeStruct(lead + (rows, a), F32), jax.ShapeDtypeStruct(lead + (rows, LANES), F32)],
        scratch_shapes=[pltpu.VMEM((heads, blk, nkeys), F32), pltpu.VMEM((heads, blk, nkeys), BF16)],
        compiler_params=_params(*["parallel"] * len(grid)),
    )(q, k, v, *([k, v] if has_prev else []))
    return o, lse


def _attn_combine(groups, head_spread, dims, *, name, tr=256):
    t = dims.tokens
    a = dims.n_heads * dims.head_dim
    dils = tuple(groups)

    def body(*refs):
        ins = refs[:2 * len(dils)]
        x_ref = refs[2 * len(dils)]
        o_ref = refs[2 * len(dils) + 1]
        lse_refs = refs[2 * len(dils) + 2:-2]
        scr, scr_stat = refs[-2], refs[-1]
        outs, stats = [], []
        for g, d in enumerate(dils):
            if d == 1:
                outs.append(ins[2 * g][...])
                stats.append(ins[2 * g + 1][...])
            else:
                outs.append(_residues_to_rows(ins[2 * g], scr, d))
                stats.append(_residues_to_rows(ins[2 * g + 1], scr_stat, d))
        top = functools.reduce(jnp.maximum, stats)
        weights = [jnp.exp(s - top) for s in stats]
        total = functools.reduce(jnp.add, weights)
        joint = top + jnp.log(total)
        inv = 1.0 / total
        acc = None
        for w, o in zip(weights, outs):
            term = _two_pass_dot(w * inv, x_ref[...]) * o
            acc = term if acc is None else acc + term
        o_ref[...] = acc.astype(BF16)
        for g, d in enumerate(dils):
            if d == 1:
                lse_refs[g][...] = joint
            else:
                _rows_to_residues(joint, lse_refs[g], scr_stat, d)

    in_specs, args, lse_specs, lse_shapes = [], [], [], []
    for d in dils:
        if d == 1:
            in_specs += [_row_spec(tr, a), _row_spec(tr, LANES)]
            lse_specs.append(_row_spec(tr, LANES))
            lse_shapes.append(jax.ShapeDtypeStruct((t, LANES), F32))
        else:
            in_specs += [_residue_spec(dims, d, tr, a), _residue_spec(dims, d, tr, LANES)]
            lse_specs.append(_residue_spec(dims, d, tr, LANES))
            lse_shapes.append(_residue_shape(dims, d, LANES, F32))
        args += list(groups[d])
    outs = pl.pallas_call(
        body, name=name, grid=(t // tr,),
        in_specs=in_specs + [pl.BlockSpec((LANES, a), lambda i: (0, 0))],
        out_specs=[_row_spec(tr, a)] + lse_specs,
        out_shape=[jax.ShapeDtypeStruct((t, a), BF16)] + lse_shapes,
        scratch_shapes=[pltpu.VMEM((a // LANES, tr, LANES), F32), pltpu.VMEM((1, tr, LANES), F32)],
        compiler_params=_params("parallel"),
    )(*args, head_spread)
    return outs[0], dict(zip(dils, outs[1:]))


def _attn_bwd_prep(do, o, head_sum, dims, *, name, tr=256):
    t, a = o.shape

    def body(do_ref, o_ref, e_ref, *rest):
        outs, scr, scr_stat = rest[:-2], rest[-2], rest[-1]
        dov = do_ref[...].astype(F32)
        delta = _two_pass_dot(dov * o_ref[...].astype(F32), e_ref[...])
        outs[0][...] = delta
        for g, d in enumerate(RESIDUE_DILATIONS):
            _rows_to_residues(dov, outs[1 + 2 * g], scr, d)
            _rows_to_residues(delta, outs[2 + 2 * g], scr_stat, d)

    out_specs, out_shape = [_row_spec(tr, LANES)], [jax.ShapeDtypeStruct((t, LANES), F32)]
    for d in RESIDUE_DILATIONS:
        out_specs += [_residue_spec(dims, d, tr, a), _residue_spec(dims, d, tr, LANES)]
        out_shape += [_residue_shape(dims, d, a, BF16), _residue_shape(dims, d, LANES, F32)]
    outs = pl.pallas_call(
        body, name=name, grid=(t // tr,),
        in_specs=[_row_spec(tr, a), _row_spec(tr, a), pl.BlockSpec((a, LANES), lambda i: (0, 0))],
        out_specs=out_specs, out_shape=out_shape,
        scratch_shapes=[pltpu.VMEM((a // LANES, tr, LANES), F32), pltpu.VMEM((1, tr, LANES), F32)],
        compiler_params=_params("parallel"),
    )(do, o, head_sum)
    dos, deltas = {1: do}, {1: outs[0]}
    for g, d in enumerate(RESIDUE_DILATIONS):
        dos[d], deltas[d] = outs[1 + 2 * g], outs[2 + 2 * g]
    return dos, deltas


def _attn_bwd(q, k, v, do, lse, delta, dims, dil, *, name):
    a = dims.n_heads * dims.head_dim
    heads, hd, blk = dims.n_heads, dims.head_dim, ATTN_BLOCK
    nb = dims.seq // dil // blk
    has_next = nb > 1
    nq = 2 * blk if has_next else blk
    grid, cur, _, nxt = _attn_specs(dims, dil, a)
    _, cur_stat, _, nxt_stat = _attn_specs(dims, dil, LANES)

    def body(*refs):
        k_ref, v_ref, q_ref, do_ref, lse_ref, dl_ref = refs[:6]
        if has_next:
            qn_ref, don_ref, lsen_ref, dln_ref = refs[6:10]
            dq_ref, dk_ref, dv_ref, s_scr, dp_scr, p_scr, ds_scr, carry = refs[10:]
        else:
            dq_ref, dk_ref, dv_ref, s_scr, dp_scr, p_scr, ds_scr = refs[6:]
        j = pl.program_id(len(grid) - 1)
        low, high = _pair_masks(hd)

        def stacked(ref, nref, sl):
            return jnp.concatenate([ref[:, sl], nref[:, sl]], axis=0) if has_next else ref[:, sl]

        def halves(x):
            return jnp.where(low, x, jnp.zeros_like(x)), jnp.where(high, x, jnp.zeros_like(x))

        for hp in range(heads // 2):
            sl = slice(LANES * hp, LANES * (hp + 1))
            k2, v2 = k_ref[:, sl], v_ref[:, sl]
            q_a, q_b = halves(stacked(q_ref, qn_ref if has_next else None, sl))
            do_a, do_b = halves(stacked(do_ref, don_ref if has_next else None, sl))
            s_scr[2 * hp], s_scr[2 * hp + 1] = _dot_nt(q_a, k2), _dot_nt(q_b, k2)
            dp_scr[2 * hp], dp_scr[2 * hp + 1] = _dot_nt(do_a, v2), _dot_nt(do_b, v2)

        rq = lax.broadcasted_iota(jnp.int32, (nq, blk), 0)
        jk = lax.broadcasted_iota(jnp.int32, (nq, blk), 1)
        if has_next:
            iq = jnp.where(rq < blk, rq, rq - blk)
            steps = jnp.where(rq < blk, iq - jk, iq - jk + blk)
            valid = ((rq < blk) & (iq >= jk)) | ((rq >= blk) & (jk >= iq) & (j + 1 < nb))
        else:
            steps, valid = rq - jk, rq >= jk
        bias = jnp.where(valid, steps.astype(F32) * (-float(dil)), -MASK_BIAS)
        lse_all = stacked(lse_ref, lsen_ref if has_next else None, slice(None))
        dl_all = stacked(dl_ref, dln_ref if has_next else None, slice(None))
        lse3 = jnp.stack([lse_all[:, h:h + 1] for h in range(heads)])
        dl3 = jnp.stack([dl_all[:, h:h + 1] for h in range(heads)])
        p = jnp.exp(s_scr[...] + _head_slopes(heads) * bias[None] - lse3)
        p_scr[...] = p.astype(BF16)
        ds_scr[...] = (p * (dp_scr[...] - dl3)).astype(BF16)

        if has_next:
            @pl.when(j == 0)
            def _():
                carry[...] = jnp.zeros_like(carry)

        for hp in range(heads // 2):
            sl = slice(LANES * hp, LANES * (hp + 1))
            k2 = k_ref[:, sl]
            q_a, q_b = halves(stacked(q_ref, qn_ref if has_next else None, sl))
            do_a, do_b = halves(stacked(do_ref, don_ref if has_next else None, sl))
            ds_a, ds_b = ds_scr[2 * hp], ds_scr[2 * hp + 1]
            dq2 = jnp.where(low, jnp.dot(ds_a, k2, preferred_element_type=F32),
                            jnp.dot(ds_b, k2, preferred_element_type=F32))
            dk_ref[:, sl] = _dot_tn(ds_a, q_a) + _dot_tn(ds_b, q_b)
            dv_ref[:, sl] = _dot_tn(p_scr[2 * hp], do_a) + _dot_tn(p_scr[2 * hp + 1], do_b)
            if has_next:
                dq_ref[:, sl] = carry[:, sl] + dq2[:blk]
                carry[:, sl] = dq2[blk:]
            else:
                dq_ref[:, sl] = dq2

    args, in_specs = [k, v, q, do, lse, delta], [cur] * 4 + [cur_stat] * 2
    if has_next:
        args += [q, do, lse, delta]
        in_specs += [nxt] * 2 + [nxt_stat] * 2
    shape = jax.ShapeDtypeStruct(q.shape, F32)
    scratch = [pltpu.VMEM((heads, nq, blk), F32)] * 2 + [pltpu.VMEM((heads, nq, blk), BF16)] * 2
    if has_next:
        scratch.append(pltpu.VMEM((blk, a), F32))
    return pl.pallas_call(
        body, name=name, grid=grid, in_specs=in_specs, out_specs=[cur] * 3, out_shape=[shape] * 3,
        scratch_shapes=scratch,
        compiler_params=_params(*["parallel"] * (len(grid) - 1), "arbitrary"),
    )(*args)


def _qkv_layouts_bwd(z, grads, gq, gk, head_ones, dims, *, name, tr=256):
    t = z.shape[0]
    a = dims.n_heads * dims.head_dim
    q_scale = dims.head_dim ** -0.5
    dils = tuple(grads)

    def body(q_ref, k_ref, *rest):
        d_refs = rest[:3 * len(dils)]
        gq_ref, gk_ref, sum_ref, spread_ref, dz_ref, dgq_ref, dgk_ref, scr = rest[3 * len(dils):]
        first = pl.program_id(0) == 0
        mean = lambda val: _two_pass_dot(_two_pass_dot(val, sum_ref[...]), spread_ref[...]) * (1.0 / dims.head_dim)

        def total(j):
            acc = None
            for g, d in enumerate(dils):
                ref = d_refs[3 * g + j]
                part = ref[...] if d == 1 else _residues_to_rows(ref, scr, d)
                acc = part if acc is None else acc + part
            return acc

        def norm_bwd(x_ref, dy, g_ref, scale, col, dg_ref):
            xv = x_ref[...].astype(F32)
            dy = dy * scale
            r = lax.rsqrt(mean(xv * xv) + RMS_EPS)
            gy = dy * g_ref[...]
            dx = r * gy - xv * (r * r * r) * mean(xv * gy)
            dz_ref[:, col * a:(col + 1) * a] = dx.astype(BF16)
            _accumulate(dg_ref, jnp.sum(dy * xv * r, axis=0, keepdims=True), first)

        norm_bwd(q_ref, total(0), gq_ref, q_scale, 0, dgq_ref)
        norm_bwd(k_ref, total(1), gk_ref, 1.0, 1, dgk_ref)
        dz_ref[:, 2 * a:3 * a] = total(2).astype(BF16)

    in_specs, args = [_row_spec(tr, a, 2), _row_spec(tr, a, 3)], [z, z]
    for d in dils:
        in_specs += [_row_spec(tr, a) if d == 1 else _residue_spec(dims, d, tr, a)] * 3
        args += list(grads[d])
    in_specs += [_vec_spec(a), _vec_spec(a), pl.BlockSpec((a, LANES), lambda i: (0, 0)),
                 pl.BlockSpec((LANES, a), lambda i: (0, 0))]
    return pl.pallas_call(
        body, name=name, grid=(t // tr,), in_specs=in_specs,
        out_specs=[_row_spec(tr, 3 * a), _vec_spec(a), _vec_spec(a)],
        out_shape=[jax.ShapeDtypeStruct((t, 3 * a), BF16)] + [jax.ShapeDtypeStruct((1, a), F32)] * 2,
        scratch_shapes=[pltpu.VMEM((a // LANES, tr, LANES), F32)],
        compiler_params=_params("arbitrary"),
    )(*args, gq, gk, *head_ones)


def _mix_fwd(ya, yb, z, gate_b, dims, *, name, tr=512):
    t, d = ya.shape
    tr = _pick(t, tr, 8)
    first_gate_col = z.shape[1] // d - 2

    def body(ya_ref, yb_ref, ga_ref, gb_ref, ba_ref, bb_ref, o_ref):
        g_a = _sigmoid(ga_ref[...].astype(F32) + ba_ref[...])
        g_b = _sigmoid(gb_ref[...].astype(F32) + bb_ref[...])
        o_ref[...] = (g_a * ya_ref[...] + g_b * yb_ref[...]).astype(BF16)

    return pl.pallas_call(
        body, name=name, grid=(t // tr,),
        in_specs=[_row_spec(tr, d), _row_spec(tr, d), _row_spec(tr, d, first_gate_col),
                  _row_spec(tr, d, first_gate_col + 1), _vec_spec(d, 0), _vec_spec(d, 1)],
        out_specs=_row_spec(tr, d), out_shape=jax.ShapeDtypeStruct((t, d), BF16),
        compiler_params=_params("parallel"),
    )(ya, yb, z, z, gate_b, gate_b)


def _mix_bwd(dmix, ya, yb, z, gate_b, dims, *, name, tr=512):
    t, d = ya.shape
    tr = _pick(t, tr, 8)
    first_gate_col = z.shape[1] // d - 2

    def body(dm_ref, ya_ref, yb_ref, ga_ref, gb_ref, ba_ref, bb_ref, dya_ref, dyb_ref, dz_ref, db_ref):
        dm = dm_ref[...].astype(F32)
        g_a = _sigmoid(ga_ref[...].astype(F32) + ba_ref[...])
        g_b = _sigmoid(gb_ref[...].astype(F32) + bb_ref[...])
        dya_ref[...] = (dm * g_a).astype(BF16)
        dyb_ref[...] = (dm * g_b).astype(BF16)
        dl_a = dm * ya_ref[...] * g_a * (1.0 - g_a)
        dl_b = dm * yb_ref[...] * g_b * (1.0 - g_b)
        dz_ref[:, 0:d] = dl_a.astype(BF16)
        dz_ref[:, d:2 * d] = dl_b.astype(BF16)
        first = pl.program_id(0) == 0
        sums = jnp.concatenate([jnp.sum(dl_a, axis=0, keepdims=True), jnp.sum(dl_b, axis=0, keepdims=True)], axis=1)
        _accumulate(db_ref, sums, first)

    return pl.pallas_call(
        body, name=name, grid=(t // tr,),
        in_specs=[_row_spec(tr, d), _row_spec(tr, d), _row_spec(tr, d), _row_spec(tr, d, first_gate_col),
                  _row_spec(tr, d, first_gate_col + 1), _vec_spec(d, 0), _vec_spec(d, 1)],
        out_specs=[_row_spec(tr, d), _row_spec(tr, d), _row_spec(tr, 2 * d), _vec_spec(2 * d)],
        out_shape=[jax.ShapeDtypeStruct((t, d), BF16)] * 2 + [jax.ShapeDtypeStruct((t, 2 * d), BF16),
                                                              jax.ShapeDtypeStruct((1, 2 * d), F32)],
        compiler_params=_params("arbitrary"),
    )(dmix, ya, yb, z, z, gate_b, gate_b)


def _loss_head(y, target, *, name, tr=512):
    t, d = y.shape
    tr = _pick(t, tr, 8)

    def body(y_ref, t_ref, dy_ref, dyb_ref, loss_ref):
        err = y_ref[...] - t_ref[...]
        dy = err * (1.0 / d)
        dy_ref[...] = dy
        dyb_ref[...] = dy.astype(BF16)
        part = jnp.sum(jnp.sum(err * err, axis=-1, keepdims=True), axis=0, keepdims=True) * (0.5 / d)
        _accumulate(loss_ref, jnp.broadcast_to(part, (8, 128)), pl.program_id(0) == 0)

    return pl.pallas_call(
        body, name=name, grid=(t // tr,),
        in_specs=[_row_spec(tr, d), _row_spec(tr, d)],
        out_specs=[_row_spec(tr, d), _row_spec(tr, d), pl.BlockSpec((8, 128), lambda i: (0, 0))],
        out_shape=[jax.ShapeDtypeStruct((t, d), F32), jax.ShapeDtypeStruct((t, d), BF16),
                   jax.ShapeDtypeStruct((8, 128), F32)],
        compiler_params=_params("arbitrary"),
    )(y, target)


def _adamw(w, grads, m, v, *, name, tr=256):
    r, c = w.shape
    tr = _pick(r, tr, 8)
    ng = len(grads)
    c1 = 1.0 - ADAM_B1 ** ADAM_STEP
    c2 = 1.0 - ADAM_B2 ** ADAM_STEP

    def body(*refs):
        w_ref, g_refs, m_ref, v_ref = refs[0], refs[1:1 + ng], refs[1 + ng], refs[2 + ng]
        g_out, d_out, m_out, v_out = refs[3 + ng:]
        g = g_refs[0][...]
        for extra in g_refs[1:]:
            g = g + extra[...]
        m_new = ADAM_B1 * m_ref[...] + (1.0 - ADAM_B1) * g
        v_new = ADAM_B2 * v_ref[...] + (1.0 - ADAM_B2) * (g * g)
        g_out[...] = g
        m_out[...] = m_new
        v_out[...] = v_new
        d_out[...] = -ADAM_LR * ((m_new / c1) / (jnp.sqrt(v_new / c2) + ADAM_EPS) + ADAM_WD * w_ref[...])

    spec = pl.BlockSpec((tr, c), lambda i: (i, 0))
    return pl.pallas_call(
        body, name=name, grid=(r // tr,),
        in_specs=[spec] * (3 + ng), out_specs=[spec] * 4, out_shape=[jax.ShapeDtypeStruct((r, c), F32)] * 4,
        compiler_params=_params("parallel"),
    )(w, *grads, m, v)


CHIP_PEERS = ((1, 0), (0, 1), (1, 1))


def _place():
    return lax.axis_index("x"), lax.axis_index("y"), lax.axis_index("c")


HBM = pl.BlockSpec(memory_space=pltpu.HBM)
SEM = pl.BlockSpec(memory_space=pltpu.SEMAPHORE)
IN_FLIGHT = pltpu.SideEffectType.DATAFLOW_SIDE_EFFECTING


def _in_hbm(a):
    return pltpu.with_memory_space_constraint(a, pltpu.HBM)


def _cast_to_lands(shards, dtypes, *, name):
    n = len(shards)

    def body(*refs):
        ins, outs, bufs, sems = refs[:n], refs[n:2 * n], refs[2 * n:3 * n], refs[3 * n]
        x, y, _ = _place()
        copies = []
        for a in range(n):
            bufs[a][...] = ins[a][...].astype(dtypes[a])
            cp = pltpu.make_async_copy(bufs[a], outs[a].at[2 * x + y], sems.at[a])
            cp.start()
            copies.append(cp)
        for cp in copies:
            cp.wait()

    return pl.pallas_call(
        body, name=name, in_specs=[pl.BlockSpec(memory_space=pltpu.VMEM)] * n, out_specs=[ANY] * n,
        out_shape=[jax.ShapeDtypeStruct((N_CHIPS,) + s.shape, dt) for s, dt in zip(shards, dtypes)],
        scratch_shapes=[pltpu.VMEM(s.shape, dt) for s, dt in zip(shards, dtypes)] + [pltpu.SemaphoreType.DMA((n,))],
        compiler_params=pltpu.CompilerParams(vmem_limit_bytes=V7X_VMEM_LIMIT_BYTES),
    )(*shards)


def _chip_copy(src, dst, send, recv, flip, place):
    x, y, c = place
    return pltpu.make_async_remote_copy(src_ref=src, dst_ref=dst, send_sem=send, recv_sem=recv,
                                        device_id=(x ^ flip[0], y ^ flip[1], c), device_id_type=MESH)


def _my_part(land, place, halved):
    block = land.at[2 * place[0] + place[1]]
    if not halved:
        return block
    rows = land.shape[1] // 2
    return block.at[pl.ds(pl.multiple_of(place[2] * rows, rows), rows)]


def _gather_start(lands, after, *, name, halved=()):
    n = len(lands)

    def body(*refs):
        ins, send, recv, token = refs[:n], refs[n + 1], refs[n + 2], refs[-1]
        place = _place()
        for a in range(n):
            part = _my_part(ins[a], place, a in halved)
            for p, flip in enumerate(CHIP_PEERS):
                k = 3 * a + p
                _chip_copy(part, part, send.at[k], recv.at[k], flip, place).start()
        token[...] = jnp.zeros_like(token)

    outs = pl.pallas_call(
        body, name=name, in_specs=[HBM] * n + [ANY],
        out_specs=(SEM, SEM, *[HBM] * n, pl.BlockSpec(memory_space=pltpu.VMEM)),
        out_shape=(pltpu.SemaphoreType.DMA((3 * n,)), pltpu.SemaphoreType.DMA((3 * n,)),
                   *[pltpu.HBM(l.shape, l.dtype) for l in lands], jax.ShapeDtypeStruct((8, 128), F32)),
        input_output_aliases={a: 2 + a for a in range(n)},
        compiler_params=pltpu.CompilerParams(has_side_effects=IN_FLIGHT),
    )(*[_in_hbm(l) for l in lands], after)
    return outs[0], outs[1], list(outs[2:2 + n]), outs[-1]


def _gather_wait(send, recv, lands, after, *, name, halved=()):
    n = len(lands)

    def body(*refs):
        ins, send_ref, recv_ref = refs[:n], refs[n], refs[n + 1]
        place = _place()
        for a in range(n):
            part = _my_part(ins[a], place, a in halved)
            for p, flip in enumerate(CHIP_PEERS):
                k = 3 * a + p
                cp = _chip_copy(part, part, send_ref.at[k], recv_ref.at[k], flip, place)
                cp.wait_send()
                cp.wait_recv()

    return pl.pallas_call(
        body, name=name, in_specs=[HBM] * n + [SEM, SEM, ANY], out_specs=[HBM] * n,
        out_shape=[pltpu.HBM(l.shape, l.dtype) for l in lands],
        input_output_aliases={a: a for a in range(n)},
        compiler_params=pltpu.CompilerParams(has_side_effects=IN_FLIGHT),
    )(*lands, send, recv, after)


def _forward_to_sibling(land, *, name):
    rows = land.shape[1] // 2

    def body(land_ref, out_ref, send, recv):
        x, y, c = _place()
        copies = []
        for p, (fx, fy) in enumerate(CHIP_PEERS):
            chip = 2 * (x ^ fx) + (y ^ fy)
            mine = pl.ds(pl.multiple_of(c * rows, rows), rows)
            theirs = pl.ds(pl.multiple_of((1 - c) * rows, rows), rows)
            out = pltpu.make_async_remote_copy(
                src_ref=land_ref.at[chip].at[mine], dst_ref=out_ref.at[chip].at[mine], send_sem=send.at[p],
                recv_sem=recv.at[p], device_id=(x, y, 1 - c), device_id_type=MESH)
            out.start()
            copies.append((out, pltpu.make_async_remote_copy(
                src_ref=land_ref.at[chip].at[theirs], dst_ref=out_ref.at[chip].at[theirs], send_sem=send.at[p],
                recv_sem=recv.at[p], device_id=(x, y, 1 - c), device_id_type=MESH)))
        for out, arriving in copies:
            out.wait_send()
            arriving.wait_recv()

    return pl.pallas_call(
        body, name=name, in_specs=[ANY], out_specs=ANY, out_shape=jax.ShapeDtypeStruct(land.shape, land.dtype),
        input_output_aliases={0: 0},
        scratch_shapes=[pltpu.SemaphoreType.DMA((3,)), pltpu.SemaphoreType.DMA((3,))],
    )(land)


def _scatter_start(grad, *, name):
    def body(g_ref, land_ref, send, recv, g_thru, land_thru, token):
        place = _place()
        for p, flip in enumerate(CHIP_PEERS):
            peer_chip = 2 * (place[0] ^ flip[0]) + (place[1] ^ flip[1])
            _chip_copy(g_ref.at[peer_chip], land_ref.at[p], send.at[p], recv.at[p], flip, place).start()
        token[...] = jnp.zeros_like(token)

    land = lax.empty((3,) + grad.shape[1:], grad.dtype)
    return pl.pallas_call(
        body, name=name, in_specs=[HBM, HBM],
        out_specs=(SEM, SEM, HBM, HBM, pl.BlockSpec(memory_space=pltpu.VMEM)),
        out_shape=(pltpu.SemaphoreType.DMA((3,)), pltpu.SemaphoreType.DMA((3,)), pltpu.HBM(grad.shape, grad.dtype),
                   pltpu.HBM(land.shape, land.dtype), jax.ShapeDtypeStruct((8, 128), F32)),
        input_output_aliases={0: 2, 1: 3},
        compiler_params=pltpu.CompilerParams(has_side_effects=IN_FLIGHT),
    )(_in_hbm(grad), _in_hbm(land))


def _scatter_wait(started, after, *, name):
    n = len(started)

    def body(*refs):
        grads, lands = refs[:n], refs[n:2 * n]
        sends, recvs = refs[2 * n:3 * n], refs[3 * n:4 * n]
        place = _place()
        for a in range(n):
            for p, flip in enumerate(CHIP_PEERS):
                cp = _chip_copy(grads[a].at[0], lands[a].at[p], sends[a].at[p], recvs[a].at[p], flip, place)
                cp.wait_send()
                cp.wait_recv()

    grads, lands = [s[2] for s in started], [s[3] for s in started]
    after = list(after) if isinstance(after, (list, tuple)) else [after]
    outs = pl.pallas_call(
        body, name=name, in_specs=[HBM] * (2 * n) + [SEM] * (2 * n) + [ANY] * len(after), out_specs=[HBM] * (2 * n),
        out_shape=[pltpu.HBM(a.shape, a.dtype) for a in grads + lands],
        input_output_aliases={a: a for a in range(2 * n)},
        compiler_params=pltpu.CompilerParams(has_side_effects=IN_FLIGHT),
    )(*grads, *lands, *[s[0] for s in started], *[s[1] for s in started], *after)
    return list(zip(outs[:n], outs[n:]))


def _sibling_copy(src, dst, send, recv, place):
    x, y, c = place
    return pltpu.make_async_remote_copy(src_ref=src, dst_ref=dst, send_sem=send, recv_sem=recv,
                                        device_id=(x, y, 1 - c), device_id_type=MESH)


def _swap_start(arrays, *, name):
    n = len(arrays)

    def body(*refs):
        ins, lands, send, recv, token = refs[:n], refs[n:2 * n], refs[2 * n], refs[2 * n + 1], refs[-1]
        place = _place()
        for a in range(n):
            _sibling_copy(ins[a], lands[a], send.at[a], recv.at[a], place).start()
        token[...] = jnp.zeros_like(token)

    both = [_in_hbm(a) for a in arrays] + [_in_hbm(lax.empty(a.shape, a.dtype)) for a in arrays]
    outs = pl.pallas_call(
        body, name=name, in_specs=[HBM] * (2 * n),
        out_specs=(SEM, SEM, *[HBM] * (2 * n), pl.BlockSpec(memory_space=pltpu.VMEM)),
        out_shape=(pltpu.SemaphoreType.DMA((n,)), pltpu.SemaphoreType.DMA((n,)),
                   *[pltpu.HBM(a.shape, a.dtype) for a in both], jax.ShapeDtypeStruct((8, 128), F32)),
        input_output_aliases={a: 2 + a for a in range(2 * n)},
        compiler_params=pltpu.CompilerParams(has_side_effects=IN_FLIGHT),
    )(*both)
    return outs[0], outs[1], list(outs[2:2 + n]), list(outs[2 + n:2 + 2 * n]), outs[-1]


def _swap_wait(started, after, *, name):
    send, recv, arrays, lands = started[:4]
    n = len(arrays)

    def body(*refs):
        ins, zones, send_ref, recv_ref = refs[:n], refs[n:2 * n], refs[2 * n], refs[2 * n + 1]
        place = _place()
        for a in range(n):
            cp = _sibling_copy(ins[a], zones[a], send_ref.at[a], recv_ref.at[a], place)
            cp.wait_send()
            cp.wait_recv()

    after = list(after) if isinstance(after, (list, tuple)) else [after]
    outs = pl.pallas_call(
        body, name=name, in_specs=[HBM] * (2 * n) + [SEM, SEM] + [ANY] * len(after), out_specs=[HBM] * (2 * n),
        out_shape=[pltpu.HBM(a.shape, a.dtype) for a in arrays + lands],
        input_output_aliases={a: a for a in range(2 * n)},
        compiler_params=pltpu.CompilerParams(has_side_effects=IN_FLIGHT),
    )(*arrays, *lands, send, recv, *after)
    return list(outs[n:])


def _allreduce_start(packed, *, name):
    n_dev = 8

    def body(src_ref, land_ref, send, recv, src_thru, land_thru, token):
        x, y, c = _place()
        me = 4 * x + 2 * y + c
        for p in range(1, n_dev):
            pltpu.make_async_remote_copy(
                src_ref=src_ref, dst_ref=land_ref.at[me], send_sem=send.at[p - 1], recv_sem=recv.at[p - 1],
                device_id=(x ^ (p >> 2), y ^ ((p >> 1) & 1), c ^ (p & 1)), device_id_type=MESH).start()
        token[...] = jnp.zeros_like(token)

    land = lax.empty((n_dev,) + packed.shape, packed.dtype)
    return pl.pallas_call(
        body, name=name, in_specs=[HBM, HBM],
        out_specs=(SEM, SEM, HBM, HBM, pl.BlockSpec(memory_space=pltpu.VMEM)),
        out_shape=(pltpu.SemaphoreType.DMA((n_dev - 1,)), pltpu.SemaphoreType.DMA((n_dev - 1,)),
                   pltpu.HBM(packed.shape, packed.dtype), pltpu.HBM(land.shape, land.dtype),
                   jax.ShapeDtypeStruct((8, 128), F32)),
        input_output_aliases={0: 2, 1: 3},
        compiler_params=pltpu.CompilerParams(has_side_effects=IN_FLIGHT),
    )(_in_hbm(packed), _in_hbm(land))


def _allreduce_wait(started, after, *, name):
    send, recv, packed, land = started[:4]
    n_dev = 8

    def body(src_ref, land_ref, send_ref, recv_ref, *_):
        x, y, c = _place()
        for p in range(1, n_dev):
            cp = pltpu.make_async_remote_copy(
                src_ref=src_ref, dst_ref=land_ref.at[0], send_sem=send_ref.at[p - 1], recv_sem=recv_ref.at[p - 1],
                device_id=(x ^ (p >> 2), y ^ ((p >> 1) & 1), c ^ (p & 1)), device_id_type=MESH)
            cp.wait_send()
            cp.wait_recv()

    after = list(after) if isinstance(after, (list, tuple)) else [after]
    return pl.pallas_call(
        body, name=name, in_specs=[HBM, HBM, SEM, SEM] + [ANY] * len(after), out_specs=[HBM, HBM],
        out_shape=[pltpu.HBM(packed.shape, packed.dtype), pltpu.HBM(land.shape, land.dtype)],
        input_output_aliases={0: 0, 1: 1},
        compiler_params=pltpu.CompilerParams(has_side_effects=IN_FLIGHT),
    )(packed, land, send, recv, *after)


def _sum_devices(mine, land, *, name):
    n_dev = land.shape[0]

    def body(mine_ref, land_ref, out_ref):
        x, y, c = _place()
        me = 4 * x + 2 * y + c
        total = None
        for s in range(n_dev):
            part = jnp.where(me == s, mine_ref[...], land_ref[s])
            total = part if total is None else total + part
        out_ref[...] = total

    return pl.pallas_call(body, name=name, out_shape=jax.ShapeDtypeStruct(mine.shape, mine.dtype))(mine, land)


def _sum_received(grad, land, *, name, tr=256):
    _, r, c = grad.shape
    tr = _pick(r, tr, 8)

    def body(chip_ref, g_ref, l_ref, o_ref):
        o_ref[...] = ((g_ref[...] + l_ref[0].astype(F32)) + l_ref[1].astype(F32)) + l_ref[2].astype(F32)

    chip = (2 * lax.axis_index("x") + lax.axis_index("y")).astype(jnp.int32).reshape(1)
    return pl.pallas_call(
        body, name=name,
        grid_spec=pltpu.PrefetchScalarGridSpec(
            num_scalar_prefetch=1, grid=(r // tr,),
            in_specs=[pl.BlockSpec((None, tr, c), lambda i, chip_ref: (chip_ref[0], i, 0)),
                      pl.BlockSpec((3, tr, c), lambda i, chip_ref: (0, i, 0))],
            out_specs=pl.BlockSpec((tr, c), lambda i, chip_ref: (i, 0))),
        out_shape=jax.ShapeDtypeStruct((r, c), F32), compiler_params=_params("parallel"),
    )(chip, grad, land)


def _allreduce_small(packed, *, name, after=None):
    r, d = packed.shape
    n_dev = 8

    def body(src_ref, out_ref, buf, send, recv):
        x, y, c = _place()
        me = 4 * x + 2 * y + c
        started = []
        for p in range(1, n_dev):
            rc = pltpu.make_async_remote_copy(
                src_ref=src_ref, dst_ref=buf.at[me], send_sem=send.at[p - 1], recv_sem=recv.at[p - 1],
                device_id=(x ^ (p >> 2), y ^ ((p >> 1) & 1), c ^ (p & 1)), device_id_type=MESH)
            rc.start()
            started.append(rc)
        buf[me] = src_ref[...]
        for rc in started:
            rc.wait()
        total = buf[0]
        for s in range(1, n_dev):
            total = total + buf[s]
        out_ref[...] = total

    vmem = pl.BlockSpec(memory_space=pltpu.VMEM)
    body, more_specs, more_args = _ordered(body, 1, after)
    return pl.pallas_call(
        body, name=name, in_specs=[vmem] + more_specs, out_specs=vmem, out_shape=jax.ShapeDtypeStruct((r, d), F32),
        scratch_shapes=[pltpu.VMEM((n_dev, r, d), F32), pltpu.SemaphoreType.DMA((n_dev - 1,)),
                        pltpu.SemaphoreType.DMA((n_dev - 1,))],
    )(packed, *more_args)


def _packed_rows(size, d):
    return -(-size // (8 * d)) * 8


def _pack_rows(arrays, d):
    rows = []
    for arr in arrays:
        flat = arr.reshape(-1).astype(F32)
        n = _packed_rows(flat.shape[0], d)
        rows.append(jnp.pad(flat, (0, n * d - flat.shape[0])).reshape(n, d))
    return jnp.concatenate(rows, axis=0)


def _unpack_rows(packed, shapes, d):
    out, row = [], 0
    for shape in shapes:
        size = math.prod(shape)
        n = _packed_rows(size, d)
        out.append(packed[row:row + n].reshape(-1)[:size].reshape(shape))
        row += n
    return out


SMALL = ("norm1_g", "gate_b", "conv_b", "conv_norm_g", "q_norm_g", "k_norm_g", "norm2_g", "ffn_conv_b")
LARGE = ("w_in", "w_conv_out", "w_attn_out", "w_out", "w_up", "w_down")
WEIGHTS = ("norm1_g", "w_in", "gate_b", "conv_w", "conv_b", "conv_norm_g", "w_conv_out", "q_norm_g", "k_norm_g",
           "w_attn_out", "w_out", "norm2_g", "w_up", "ffn_conv_w", "ffn_conv_b", "w_down")


def _head_ones(dims):
    a = dims.n_heads * dims.head_dim
    head = jnp.arange(a, dtype=jnp.int32) // dims.head_dim
    return (head[:, None] == head[None, :]).astype(BF16)


def _after(vec, token):
    return vec if token is None else vec + token[0:1, 0:1]


def _local_step(dims, x, target, small, first_weights, other_weights, send_grad):
    d, f, heads = dims.d_model, dims.d_ff, dims.n_heads
    small = dict(small)
    row = lambda name: small[name].reshape(1, -1)
    head_sum = _head_sum_matrix(dims)
    head_spread = jnp.transpose(head_sum)
    ones = (head_sum, head_spread)
    gq = jnp.tile(row("q_norm_g"), (1, heads))
    gk = jnp.tile(row("k_norm_g"), (1, heads))
    one_shard = lambda w: w.reshape(1, -1, w.shape[-1])

    h = _rmsnorm_fwd(x, row("norm1_g"), name="norm1")
    full = first_weights(h)
    w_in = full["w_in"]
    conv_w = jnp.pad(full["conv_w"], ((0, CONV_HALO - dims.conv_width), (0, 0)))
    ffn_w = jnp.pad(full["ffn_conv_w"], ((0, FFN_HALO - dims.ffn_conv_width), (0, 0)))
    z = _mm_nn(h, w_in, out_dtype=BF16, after=full.get("token"), tm=2048, tn=1792, name="in_proj")
    a1, a3 = _conv_branch_fwd(z, conv_w, row("conv_b"), row("conv_norm_g"), dims, name="conv_branch")
    qkv = _qkv_layouts_fwd(z, gq, gk, ones, dims, name="qk_norm")
    per_group = {dil: _attn_fwd(*qkv[dil], dims, dil, name=f"attn_fwd_d{dil}") for dil in DILATIONS}
    o, lse = _attn_combine(per_group, head_spread, dims, name="attn_combine")
    full = other_weights(o)
    w_up = full["w_up"]
    w_co, w_ao, w_o, w_dn = (one_shard(full[k]) for k in ("w_conv_out", "w_attn_out", "w_out", "w_down"))
    ya = _mm_nn(a3, w_co, out_dtype=F32, name="conv_out_proj")
    yb = _mm_nn(o, w_ao, out_dtype=F32, name="attn_out_proj")
    mixed = _mix_fwd(ya, yb, z, row("gate_b"), dims, name="gate_mix")
    x1 = _mm_nn(mixed, w_o, out_dtype=F32, residual=x, name="out_proj")
    h2 = _rmsnorm_fwd(x1, row("norm2_g"), name="norm2")
    up = _mm_nn(h2, w_up, out_dtype=F32, tm=2048, name="up_proj")
    act = _ffn_act_fwd(up, ffn_w, row("ffn_conv_b"), dims, name="ffn_act")
    x2 = _mm_nn(act, w_dn, out_dtype=F32, residual=x1, name="down_proj")
    dy, dy_b, loss = _loss_head(x2, target, name="loss_head")

    grads = {}

    def large(name, g):
        grads[name], g_bf16 = g
        return send_grad(name, g_bf16)

    sent = large("w_down", _mm_tn(act, dy_b, n_shards=1, name="dw_down"))
    dact = _mm_nt(dy_b, w_dn, out_dtype=BF16, after=sent, name="d_act")
    du, dfw, dfb = _ffn_act_bwd(dact, up, ffn_w, row("ffn_conv_b"), dims, name="ffn_act_bwd")
    grads["ffn_conv_w"], grads["ffn_conv_b"] = dfw[:dims.ffn_conv_width], dfb
    dup = _ffn_conv_bwd(du, ffn_w, dims, name="ffn_conv_bwd")
    sent = large("w_up", _mm_tn(h2, dup, n_shards=N_CHIPS, name="dw_up"))
    dh2 = _mm_nt(dup, w_up, out_dtype=F32, after=sent, name="d_h2")
    dx1, dx1_b, grads["norm2_g"] = _rmsnorm_bwd(x1, row("norm2_g"), dh2, dy, want_bf16=True, name="norm2_bwd")
    sent = large("w_out", _mm_tn(mixed, dx1_b, n_shards=1, name="dw_out"))
    dmix = _mm_nt(dx1_b, w_o, out_dtype=F32, after=sent, name="d_mix")
    dya, dyb, dz_gate, grads["gate_b"] = _mix_bwd(dmix, ya, yb, z, row("gate_b"), dims, name="gate_mix_bwd")
    sent = large("w_conv_out", _mm_tn(a3, dya, n_shards=1, name="dw_conv_out"))
    da3 = _mm_nt(dya, w_co, out_dtype=F32, after=sent, name="d_conv_act")
    da1, grads["conv_norm_g"] = _conv_norm_bwd(da3, a1, row("conv_norm_g"), name="conv_norm_bwd")
    dz_glu, dcw, grads["conv_b"] = _conv_branch_bwd(da1, z, conv_w, dims, name="conv_branch_bwd")
    grads["conv_w"] = dcw[:dims.conv_width]
    sent = large("w_attn_out", _mm_tn(o, dyb, n_shards=1, name="dw_attn_out"))
    do = _mm_nt(dyb, w_ao, out_dtype=BF16, after=sent, name="d_attn")
    dos, deltas = _attn_bwd_prep(do, o, head_sum, dims, name="attn_bwd_prep")
    dqkv = {dil: _attn_bwd(*qkv[dil], dos[dil], lse[dil], deltas[dil], dims, dil, name=f"attn_bwd_d{dil}")
            for dil in DILATIONS}
    dz_qkv, dgq, dgk = _qkv_layouts_bwd(z, dqkv, gq, gk, ones, dims, name="qk_norm_bwd")
    grads["q_norm_g"] = dgq.reshape(heads, dims.head_dim).sum(axis=0)
    grads["k_norm_g"] = dgk.reshape(heads, dims.head_dim).sum(axis=0)
    dz = jnp.concatenate([dz_glu, dz_qkv, dz_gate], axis=1)
    sent = large("w_in", _mm_tn(h, dz, n_shards=N_CHIPS, name="dw_in"))
    dh = _mm_nt(dz, w_in, out_dtype=F32, after=sent, name="d_h")
    dx, grads["norm1_g"] = _rmsnorm_bwd(x, row("norm1_g"), dh, dx1, want_bf16=False, name="norm1_bwd")
    return loss, dx, grads


def _step(dims, x, target, w, m, v):
    d = dims.d_model
    t = dims.tokens
    sq = lambda a: a.reshape(a.shape[1:])
    w2, m2, v2 = ({k: sq(a) for k, a in grp.items()} for grp in (w, m, v))

    conv_pad = jnp.pad(w2["conv_w"], ((0, CONV_HALO - dims.conv_width), (0, 0)))
    ffn_pad = jnp.pad(w2["ffn_conv_w"], ((0, FFN_HALO - dims.ffn_conv_width), (0, 0)))
    gathered_names = LARGE + ("conv_w", "ffn_conv_w")
    lands = dict(zip(gathered_names, _cast_to_lands([w2[k] for k in LARGE] + [conv_pad, ffn_pad],
                                                   [BF16] * len(LARGE) + [F32, F32], name="cast_weights")))
    first_names = ("w_in", "conv_w", "ffn_conv_w")
    other_names = tuple(k for k in gathered_names if k not in first_names)
    first = _gather_start([lands[k] for k in first_names], x, halved=(0,), name="gather_start_first")
    other = []
    cols = lambda g, rows: jnp.moveaxis(g, 0, 1).reshape(g.shape[1], -1)[:rows]

    def first_weights(after):
        got = dict(zip(first_names, _gather_wait(*first[:3], after, halved=(0,), name="gather_wait_first")))
        got["w_in"] = _forward_to_sibling(got["w_in"], name="forward_w_in")
        other.extend(_gather_start([lands[k] for k in other_names], got["w_in"], name="gather_start_other"))
        got["conv_w"] = cols(got["conv_w"], dims.conv_width)
        got["ffn_conv_w"] = cols(got["ffn_conv_w"], dims.ffn_conv_width)
        got["token"] = other[3]
        return got

    def other_weights(after):
        return dict(zip(other_names, _gather_wait(*other[:3], after, name="gather_wait_other")))

    started = {}

    def send_grad(name, g):
        send, recv, g_thru, land, token = _scatter_start(g.reshape(N_CHIPS, -1, g.shape[-1]), name=f"scatter_start_{name}")
        started[name] = (send, recv, g_thru, land)
        return token

    small = {k: w2[k] for k in SMALL}
    small["norm1_g"] = _after(small["norm1_g"].reshape(1, -1), first[3])
    loss, dx, grads = _local_step(dims, x.reshape(t, d), target.reshape(t, d), small, first_weights, other_weights, send_grad)

    def my_sums(names, after, tag):
        arrived = _scatter_wait([started[k] for k in names], after, name=f"scatter_wait_{tag}")
        blocks = [grads[k].reshape(N_CHIPS, -1, grads[k].shape[-1]) for k in names]
        return [_sum_received(g, land, name=f"sum_{k}") for k, g, (_, land) in zip(names, blocks, arrived)]

    def updates(names, mine, theirs):
        return {k: _adamw(w2[k], [a, b], m2[k], v2[k], name=f"adamw_{k}") for k, a, b in zip(names, mine, theirs)}

    small_names = SMALL + ("conv_w", "ffn_conv_w")
    packed = _pack_rows([grads[k] for k in small_names] + [loss[0, 0]], d)
    reducing = _allreduce_start(packed, name="allreduce_start")
    others = [k for k in LARGE if k != "w_in"]
    mine_others = my_sums(others, [dx, reducing[4]], "others")
    swapping_others = _swap_start(mine_others, name="swap_start_others")
    mine_w_in = my_sums(["w_in"], swapping_others[4], "w_in")
    swapping_w_in = _swap_start(mine_w_in, name="swap_start_w_in")
    out = updates(others, mine_others, _swap_wait(swapping_others, swapping_w_in[4], name="swap_wait_others"))
    last_updates = [out[k][1] for k in others]
    reduced = _sum_devices(*_allreduce_wait(reducing, last_updates, name="allreduce_wait"), name="allreduce_sum")
    shapes = [grads[k].shape for k in small_names] + [()]
    *small_g, loss_total = _unpack_rows(reduced, shapes, d)
    small_g = dict(zip(small_names, small_g))
    chip = 2 * lax.axis_index("x") + lax.axis_index("y")
    for k in ("conv_w", "ffn_conv_w"):
        width = w2[k].shape[1]
        small_g[k] = lax.dynamic_slice_in_dim(small_g[k], chip * width, width, axis=1)

    small_shapes = [w2[k].shape for k in small_names]
    pack = lambda grp: _pack_rows([grp[k] for k in small_names], d)
    results = _adamw(pack(w2), [pack(small_g)], pack(m2), pack(v2), name="adamw_small")
    unpacked = [_unpack_rows(r, small_shapes, d) for r in results]
    out.update({k: tuple(u[i] for u in unpacked) for i, k in enumerate(small_names)})
    out.update(updates(["w_in"], mine_w_in, _swap_wait(swapping_w_in, results[1], name="swap_wait_w_in")))

    lead =lambda a: a.reshape((1,) + a.shape)
    ordered = [[lead(out[k][j].reshape(w2[k].shape)) for k in WEIGHTS] for j in range(4)]
    return (loss_total, dx.reshape(x.shape), *ordered[0], *ordered[1], *ordered[2], *ordered[3])


def kernel(x, norm1_g, w_in, gate_b, conv_w, conv_b, conv_norm_g, w_conv_out, q_norm_g, k_norm_g, w_attn_out, w_out, norm2_g, w_up, ffn_conv_w, ffn_conv_b, w_down, loss_target, m_norm1_g, m_w_in, m_gate_b, m_conv_w, m_conv_b, m_conv_norm_g, m_w_conv_out, m_q_norm_g, m_k_norm_g, m_w_attn_out, m_w_out, m_norm2_g, m_w_up, m_ffn_conv_w, m_ffn_conv_b, m_w_down, v_norm1_g, v_w_in, v_gate_b, v_conv_w, v_conv_b, v_conv_norm_g, v_w_conv_out, v_q_norm_g, v_k_norm_g, v_w_attn_out, v_w_out, v_norm2_g, v_w_up, v_ffn_conv_w, v_ffn_conv_b, v_w_down):
    w = dict(zip(WEIGHTS, (norm1_g, w_in, gate_b, conv_w, conv_b, conv_norm_g, w_conv_out, q_norm_g, k_norm_g,
                           w_attn_out, w_out, norm2_g, w_up, ffn_conv_w, ffn_conv_b, w_down)))
    m = dict(zip(WEIGHTS, (m_norm1_g, m_w_in, m_gate_b, m_conv_w, m_conv_b, m_conv_norm_g, m_w_conv_out, m_q_norm_g,
                           m_k_norm_g, m_w_attn_out, m_w_out, m_norm2_g, m_w_up, m_ffn_conv_w, m_ffn_conv_b, m_w_down)))
    v = dict(zip(WEIGHTS, (v_norm1_g, v_w_in, v_gate_b, v_conv_w, v_conv_b, v_conv_norm_g, v_w_conv_out, v_q_norm_g,
                           v_k_norm_g, v_w_attn_out, v_w_out, v_norm2_g, v_w_up, v_ffn_conv_w, v_ffn_conv_b, v_w_down)))
    dims = Dims(d_model=x.shape[-1], batch_local=x.shape[0], seq=x.shape[1], d_ff=w_down.shape[1] * N_CHIPS)
    return _step(dims, x, loss_target, w, m, v)
```

```python
import functools
import math
from typing import NamedTuple

import jax
import jax.numpy as jnp
from jax import lax
from jax.experimental import pallas as pl
from jax.experimental.pallas import tpu as pltpu

F32 = jnp.float32
BF16 = jnp.bfloat16

RMS_EPS = 1e-6
MASKED_SCORE = -1e30
ATTN_BLOCK = 128
DILATIONS = (1, 4, 16)
CONV_HALO = 32
FFN_HALO = 8
ADAM_LR, ADAM_B1, ADAM_B2, ADAM_EPS, ADAM_WD, ADAM_STEP = 0.001, 0.9, 0.999, 1e-08, 0.01, 10
V7X_VMEM_LIMIT_BYTES = 56 * 2 ** 20
N_CHIPS = 4
MESH = pl.DeviceIdType.MESH


class Dims(NamedTuple):
    d_model: int = 1024
    n_heads: int = 16
    head_dim: int = 64
    d_ff: int = 2816
    seq: int = 2048
    batch_local: int = 2
    conv_width: int = 31
    ffn_conv_width: int = 3

    @property
    def tokens(self):
        return self.seq * self.batch_local


def _params(*semantics):
    return pltpu.CompilerParams(dimension_semantics=semantics, vmem_limit_bytes=V7X_VMEM_LIMIT_BYTES)


ANY = pl.BlockSpec(memory_space=pl.ANY)


def _ordered(body, n_inputs, after):
    after = [] if after is None else list(after) if isinstance(after, (list, tuple)) else [after]
    if not after:
        return body, [], []

    def wrapped(*refs):
        return body(*refs[:n_inputs], *refs[n_inputs + len(after):])

    return wrapped, [ANY] * len(after), after


def _pick(n, target, mult=128):
    if n <= target:
        return n
    best = None
    for t in range(mult, target + 1, mult):
        if n % t == 0:
            best = t
    assert best is not None, (n, target, mult)
    return best


def _sigmoid(v):
    return 1.0 / (1.0 + jnp.exp(-v))


def _mm_nn(a, w, *, out_dtype, name, residual=None, after=None, tm=1024, tn=1408, tk=2816):
    m, k = a.shape
    nsh, k2, c = w.shape
    assert k == k2 and a.dtype == BF16 and w.dtype == BF16
    n = nsh * c
    tm, tn, tk = _pick(m, tm, 8), _pick(c, tn), _pick(k, tk)
    nk, cpn = k // tk, c // tn

    def body(*refs):
        if residual is None:
            a_ref, w_ref, o_ref, acc = refs
        else:
            a_ref, w_ref, r_ref, o_ref, acc = refs
        prod = jnp.dot(a_ref[...], w_ref[...], preferred_element_type=F32)

        def finish(total):
            if residual is not None:
                total = total + r_ref[...]
            o_ref[...] = total.astype(out_dtype)

        if nk == 1:
            finish(prod)
        else:
            kk = pl.program_id(2)

            @pl.when(kk == 0)
            def _():
                acc[...] = prod

            @pl.when(kk > 0)
            def _():
                acc[...] += prod

            @pl.when(kk == nk - 1)
            def _():
                finish(acc[...])

    in_specs = [pl.BlockSpec((tm, tk), lambda i, j, kk: (i, kk)),
                pl.BlockSpec((None, tk, tn), lambda i, j, kk: (j // cpn, kk, j % cpn))]
    args = [a, w]
    if residual is not None:
        in_specs.append(pl.BlockSpec((tm, tn), lambda i, j, kk: (i, j)))
        args.append(residual)
    body, more_specs, more_args = _ordered(body, len(args), after)
    return pl.pallas_call(
        body, name=name, grid=(m // tm, n // tn, nk),
        in_specs=in_specs + more_specs, out_specs=pl.BlockSpec((tm, tn), lambda i, j, kk: (i, j)),
        out_shape=jax.ShapeDtypeStruct((m, n), out_dtype),
        scratch_shapes=[pltpu.VMEM((tm, tn) if nk > 1 else (8, 128), F32)],
        compiler_params=_params("parallel", "parallel", "arbitrary"),
    )(*args, *more_args)


def _mm_nt(a, w, *, out_dtype, name, after=None, tm=1024, tn=1408, tk=1792):
    m, k = a.shape
    nsh, r, c = w.shape
    assert k == nsh * c and a.dtype == BF16 and w.dtype == BF16
    tm, tn, tk = _pick(m, tm, 8), _pick(r, tn), _pick(c, tk)
    nk, cpk = k // tk, c // tk

    def body(a_ref, w_ref, o_ref, acc):
        prod = lax.dot_general(a_ref[...], w_ref[...], (((1,), (1,)), ((), ())), preferred_element_type=F32)
        if nk == 1:
            o_ref[...] = prod.astype(out_dtype)
        else:
            kk = pl.program_id(2)

            @pl.when(kk == 0)
            def _():
                acc[...] = prod

            @pl.when(kk > 0)
            def _():
                acc[...] += prod

            @pl.when(kk == nk - 1)
            def _():
                o_ref[...] = acc[...].astype(out_dtype)

    body, more_specs, more_args = _ordered(body, 2, after)
    return pl.pallas_call(
        body, name=name, grid=(m // tm, r // tn, nk),
        in_specs=[pl.BlockSpec((tm, tk), lambda i, j, kk: (i, kk)),
                  pl.BlockSpec((None, tn, tk), lambda i, j, kk: (kk // cpk, j, kk % cpk))] + more_specs,
        out_specs=pl.BlockSpec((tm, tn), lambda i, j, kk: (i, j)),
        out_shape=jax.ShapeDtypeStruct((m, r), out_dtype),
        scratch_shapes=[pltpu.VMEM((tm, tn) if nk > 1 else (8, 128), F32)],
        compiler_params=_params("parallel", "parallel", "arbitrary"),
    )(a, w, *more_args)


MM_TN_VMEM_BYTES = 44 * 2 ** 20


def _mm_tn(a, b, *, n_shards, name, tm=1408, tn=1408):
    t, m = a.shape
    t2, n = b.shape
    assert t == t2 and a.dtype == BF16 and b.dtype == BF16
    c = n // n_shards
    tm, tn = _pick(m, tm), _pick(c, tn)
    fixed = 2 * tm * tn * 6
    if 4 * t * (tm + tn) + fixed <= MM_TN_VMEM_BYTES:
        tk = t
    else:
        tk = _pick(t, (MM_TN_VMEM_BYTES - fixed - 4 * tm * tn) // (4 * (tm + tn)), 8)
    nk, cpn = t // tk, c // tn

    def body(a_ref, b_ref, o_ref, ob_ref, acc):
        kk = pl.program_id(2)
        prod = lax.dot_general(a_ref[...], b_ref[...], (((0,), (0,)), ((), ())), preferred_element_type=F32)

        def finish(total):
            o_ref[...] = total
            ob_ref[...] = total.astype(BF16)

        if nk == 1:
            finish(prod)
        else:
            @pl.when(kk == 0)
            def _():
                acc[...] = prod

            @pl.when(kk > 0)
            def _():
                acc[...] += prod

            @pl.when(kk == nk - 1)
            def _():
                finish(acc[...])

    out_spec = pl.BlockSpec((None, tm, tn), lambda i, j, kk: (j // cpn, i, j % cpn))
    return pl.pallas_call(
        body, name=name, grid=(m // tm, n // tn, nk),
        in_specs=[pl.BlockSpec((tk, tm), lambda i, j, kk: (kk, i)),
                  pl.BlockSpec((tk, tn), lambda i, j, kk: (kk, j))],
        out_specs=[out_spec, out_spec],
        out_shape=[jax.ShapeDtypeStruct((n_shards, m, c), F32), jax.ShapeDtypeStruct((n_shards, m, c), BF16)],
        scratch_shapes=[pltpu.VMEM((tm, tn) if nk > 1 else (8, 128), F32)],
        compiler_params=_params("parallel", "parallel", "arbitrary"),
    )(a, b)


def _row_spec(tr, width, col=0):
    return pl.BlockSpec((tr, width), lambda i, col=col: (i, col))


def _vec_spec(width, col=0):
    return pl.BlockSpec((1, width), lambda i, col=col: (0, col))


def _accumulate(ref, value, first):
    @pl.when(first)
    def _():
        ref[...] = value

    @pl.when(jnp.logical_not(first))
    def _():
        ref[...] += value


def _rmsnorm_fwd(x, g, *, name, tr=512):
    t, d = x.shape
    tr = _pick(t, tr, 8)

    def body(x_ref, g_ref, o_ref):
        xv = x_ref[...]
        r = lax.rsqrt(jnp.mean(xv * xv, axis=-1, keepdims=True) + RMS_EPS)
        o_ref[...] = (xv * r * g_ref[...]).astype(BF16)

    return pl.pallas_call(
        body, name=name, grid=(t // tr,),
        in_specs=[_row_spec(tr, d), _vec_spec(d)], out_specs=_row_spec(tr, d),
        out_shape=jax.ShapeDtypeStruct((t, d), BF16), compiler_params=_params("parallel"),
    )(x, g)


def _rmsnorm_bwd(x, g, dy, dres, *, name, want_bf16, tr=512):
    t, d = x.shape
    tr = _pick(t, tr, 8)

    def body(x_ref, g_ref, dy_ref, dres_ref, *outs):
        dx_ref, dg_ref = outs[0], outs[-1]
        xv, dyv = x_ref[...], dy_ref[...].astype(F32)
        r = lax.rsqrt(jnp.mean(xv * xv, axis=-1, keepdims=True) + RMS_EPS)
        gy = dyv * g_ref[...]
        dx = dres_ref[...] + r * gy - xv * (r * r * r) * jnp.mean(xv * gy, axis=-1, keepdims=True)
        dx_ref[...] = dx
        if want_bf16:
            outs[1][...] = dx.astype(BF16)
        _accumulate(dg_ref, jnp.sum(dyv * xv * r, axis=0, keepdims=True), pl.program_id(0) == 0)

    out_shape = [jax.ShapeDtypeStruct((t, d), F32)]
    out_specs = [_row_spec(tr, d)]
    if want_bf16:
        out_shape.append(jax.ShapeDtypeStruct((t, d), BF16))
        out_specs.append(_row_spec(tr, d))
    out_shape.append(jax.ShapeDtypeStruct((1, d), F32))
    out_specs.append(_vec_spec(d))
    return pl.pallas_call(
        body, name=name, grid=(t // tr,),
        in_specs=[_row_spec(tr, d), _vec_spec(d), _row_spec(tr, d), _row_spec(tr, d)],
        out_specs=out_specs, out_shape=out_shape, compiler_params=_params("arbitrary"),
    )(x, g, dy, dres)


def _head_mean(v, ones_ref, head_dim):
    hi = v.astype(BF16)
    lo = (v - hi.astype(F32)).astype(BF16)
    e = ones_ref[...]
    total = jnp.dot(hi, e, preferred_element_type=F32) + jnp.dot(lo, e, preferred_element_type=F32)
    return total * (1.0 / head_dim)


def _qkv_fwd(z, gq, gk, head_ones, dims, *, name, tr=256):
    t = z.shape[0]
    a = dims.n_heads * dims.head_dim
    tr = _pick(t, tr, 8)
    q_scale = dims.head_dim ** -0.5

    def body(q_ref, k_ref, v_ref, gq_ref, gk_ref, e_ref, qo_ref, ko_ref, vo_ref):
        qv, kv = q_ref[...], k_ref[...]
        rq = lax.rsqrt(_head_mean(qv * qv, e_ref, dims.head_dim) + RMS_EPS)
        rk = lax.rsqrt(_head_mean(kv * kv, e_ref, dims.head_dim) + RMS_EPS)
        qo_ref[...] = (qv * rq * gq_ref[...] * q_scale).astype(BF16)
        ko_ref[...] = (kv * rk * gk_ref[...]).astype(BF16)
        vo_ref[...] = v_ref[...].astype(BF16)

    return pl.pallas_call(
        body, name=name, grid=(t // tr,),
        in_specs=[_row_spec(tr, a, 2), _row_spec(tr, a, 3), _row_spec(tr, a, 4), _vec_spec(a), _vec_spec(a),
                  pl.BlockSpec((a, a), lambda i: (0, 0))],
        out_specs=[_row_spec(tr, a)] * 3, out_shape=[jax.ShapeDtypeStruct((t, a), BF16)] * 3,
        compiler_params=_params("parallel"),
    )(z, z, z, gq, gk, head_ones)


def _qkv_bwd(z, dqs, dks, dvs, gq, gk, head_ones, dims, *, name, tr=256):
    t = z.shape[0]
    a = dims.n_heads * dims.head_dim
    tr = _pick(t, tr, 8)
    q_scale = dims.head_dim ** -0.5
    ng = len(dqs)

    def body(*refs):
        q_ref, k_ref = refs[:2]
        dq_refs, dk_refs, dv_refs = refs[2:2 + ng], refs[2 + ng:2 + 2 * ng], refs[2 + 2 * ng:2 + 3 * ng]
        gq_ref, gk_ref, e_ref = refs[2 + 3 * ng:5 + 3 * ng]
        dz_ref, dgq_ref, dgk_ref = refs[5 + 3 * ng:]
        first = pl.program_id(0) == 0

        def norm_bwd(x_ref, d_refs, g_ref, scale, col, dg_ref):
            xv = x_ref[...]
            dy = sum(r[...] for r in d_refs) * scale
            r = lax.rsqrt(_head_mean(xv * xv, e_ref, dims.head_dim) + RMS_EPS)
            gy = dy * g_ref[...]
            dx = r * gy - xv * (r * r * r) * _head_mean(xv * gy, e_ref, dims.head_dim)
            dz_ref[:, col * a:(col + 1) * a] = dx.astype(BF16)
            _accumulate(dg_ref, jnp.sum(dy * xv * r, axis=0, keepdims=True), first)

        norm_bwd(q_ref, dq_refs, gq_ref, q_scale, 0, dgq_ref)
        norm_bwd(k_ref, dk_refs, gk_ref, 1.0, 1, dgk_ref)
        dz_ref[:, 2 * a:3 * a] = sum(r[...] for r in dv_refs).astype(BF16)

    in_specs = ([_row_spec(tr, a, 2), _row_spec(tr, a, 3)] + [_row_spec(tr, a)] * (3 * ng)
                + [_vec_spec(a), _vec_spec(a), pl.BlockSpec((a, a), lambda i: (0, 0))])
    return pl.pallas_call(
        body, name=name, grid=(t // tr,), in_specs=in_specs,
        out_specs=[_row_spec(tr, 3 * a), _vec_spec(a), _vec_spec(a)],
        out_shape=[jax.ShapeDtypeStruct((t, 3 * a), BF16)] + [jax.ShapeDtypeStruct((1, a), F32)] * 2,
        compiler_params=_params("arbitrary"),
    )(z, z, *dqs, *dks, *dvs, gq, gk, head_ones)


CONV_ROWS = 16


def _seq_specs(dims, ts, width, halo, col, *, nxt=False):
    nst, per = dims.seq // ts, ts // halo
    last = dims.tokens // halo - 1
    cur = pl.BlockSpec((ts, width), lambda b, i: (b * nst + i, col))
    if nxt:
        edge = pl.BlockSpec((halo, width), lambda b, i: (jnp.minimum((b * nst + i + 1) * per, last), col))
    else:
        edge = pl.BlockSpec((halo, width), lambda b, i: (jnp.maximum((b * nst + i) * per - 1, 0), col))
    return cur, edge


SUBLANES = 8


def _shifted_copies(buf, shifted):
    rows = shifted.shape[1]
    for s in range(1, SUBLANES):
        shifted[s - 1] = buf[pl.ds(s, rows), :]


def _window(buf, shifted, start, size):
    a, s = divmod(start, SUBLANES)
    src = buf if s == 0 else shifted.at[s - 1]
    return src[pl.ds(SUBLANES * a, size), :]


def _conv_branch_fwd(z, w, b, g, dims, *, name, ts=128):
    t, c, kw = z.shape[0], dims.d_model, dims.conv_width
    base = CONV_HALO - (kw - 1)

    def body(av_ref, hv_ref, ag_ref, hg_ref, w_ref, b_ref, g_ref, a1_ref, a3_ref, buf, shifted):
        i = pl.program_id(1)
        buf[CONV_HALO:, :] = av_ref[...].astype(F32) * _sigmoid(ag_ref[...].astype(F32))
        buf[0:CONV_HALO, :] = jnp.where(i > 0, hv_ref[...].astype(F32) * _sigmoid(hg_ref[...].astype(F32)), 0.0)
        _shifted_copies(buf, shifted)
        for r0 in range(0, ts, CONV_ROWS):
            acc = jnp.broadcast_to(b_ref[...], (CONV_ROWS, c))
            for k in range(kw):
                acc = acc + w_ref[k:k + 1, :] * _window(buf, shifted, r0 + base + k, CONV_ROWS)
            a1_ref[r0:r0 + CONV_ROWS, :] = acc
            a2 = acc * lax.rsqrt(jnp.mean(acc * acc, axis=-1, keepdims=True) + RMS_EPS) * g_ref[...]
            a3_ref[r0:r0 + CONV_ROWS, :] = (a2 * _sigmoid(a2)).astype(BF16)

    vec = pl.BlockSpec((1, c), lambda b, i: (0, 0))
    out = pl.BlockSpec((ts, c), lambda b, i: (b * (dims.seq // ts) + i, 0))
    return pl.pallas_call(
        body, name=name, grid=(dims.batch_local, dims.seq // ts),
        in_specs=[*_seq_specs(dims, ts, c, CONV_HALO, 0), *_seq_specs(dims, ts, c, CONV_HALO, 1),
                  pl.BlockSpec((CONV_HALO, c), lambda b, i: (0, 0)), vec, vec],
        out_specs=[out, out],
        out_shape=[jax.ShapeDtypeStruct((t, c), F32), jax.ShapeDtypeStruct((t, c), BF16)],
        scratch_shapes=[pltpu.VMEM((CONV_HALO + ts, c), F32),
                        pltpu.VMEM((SUBLANES - 1, CONV_HALO + ts - SUBLANES, c), F32)],
        compiler_params=_params("parallel", "parallel"),
    )(z, z, z, z, w, b, g)


def _conv_norm_bwd(da3, a1, g, *, name, tr=256):
    t, c = a1.shape
    tr = _pick(t, tr, 8)

    def body(d_ref, a_ref, g_ref, o_ref, dg_ref):
        a1v, gv = a_ref[...], g_ref[...]
        r = lax.rsqrt(jnp.mean(a1v * a1v, axis=-1, keepdims=True) + RMS_EPS)
        a2 = a1v * r * gv
        sg = _sigmoid(a2)
        da2 = d_ref[...].astype(F32) * sg * (1.0 + a2 * (1.0 - sg))
        gy = da2 * gv
        o_ref[...] = r * gy - a1v * (r * r * r) * jnp.mean(a1v * gy, axis=-1, keepdims=True)
        _accumulate(dg_ref, jnp.sum(da2 * a1v * r, axis=0, keepdims=True), pl.program_id(0) == 0)

    return pl.pallas_call(
        body, name=name, grid=(t // tr,),
        in_specs=[_row_spec(tr, c), _row_spec(tr, c), _vec_spec(c)],
        out_specs=[_row_spec(tr, c), _vec_spec(c)],
        out_shape=[jax.ShapeDtypeStruct((t, c), F32), jax.ShapeDtypeStruct((1, c), F32)],
        compiler_params=_params("arbitrary"),
    )(da3, a1, g)


def _conv_branch_bwd(da1, z, w, dims, *, name, ts=128):
    t, c, kw = z.shape[0], dims.d_model, dims.conv_width
    nst = dims.seq // ts
    base = CONV_HALO - (kw - 1)

    def body(d_ref, dn_ref, av_ref, hv_ref, ag_ref, hg_ref, w_ref, dz_ref, dw_ref, db_ref, abuf, dbuf, ashift, dshift):
        i = pl.program_id(1)
        first = jnp.logical_and(pl.program_id(0) == 0, i == 0)
        abuf[CONV_HALO:, :] = av_ref[...].astype(F32) * _sigmoid(ag_ref[...].astype(F32))
        abuf[0:CONV_HALO, :] = jnp.where(i > 0, hv_ref[...].astype(F32) * _sigmoid(hg_ref[...].astype(F32)), 0.0)
        d1 = d_ref[...]
        dbuf[0:ts, :] = d1
        dbuf[ts:, :] = jnp.where(i < nst - 1, dn_ref[...], 0.0)
        _shifted_copies(abuf, ashift)
        _shifted_copies(dbuf, dshift)

        @pl.when(first)
        def _():
            dw_ref[...] = jnp.zeros_like(dw_ref)
            db_ref[...] = jnp.zeros_like(db_ref)

        db_ref[...] += jnp.sum(d1, axis=0, keepdims=True)
        for k in range(kw):
            dw_ref[k:k + 1, :] += jnp.sum(d1 * _window(abuf, ashift, base + k, ts), axis=0, keepdims=True)
        for r0 in range(0, ts, CONV_ROWS):
            acc = jnp.zeros((CONV_ROWS, c), F32)
            for k in range(kw):
                acc = acc + w_ref[k:k + 1, :] * _window(dbuf, dshift, r0 + (kw - 1) - k, CONV_ROWS)
            av = av_ref[r0:r0 + CONV_ROWS, :].astype(F32)
            sg = _sigmoid(ag_ref[r0:r0 + CONV_ROWS, :].astype(F32))
            dz_ref[r0:r0 + CONV_ROWS, 0:c] = (acc * sg).astype(BF16)
            dz_ref[r0:r0 + CONV_ROWS, c:2 * c] = (acc * av * sg * (1.0 - sg)).astype(BF16)

    cur, nxt = _seq_specs(dims, ts, c, CONV_HALO, 0, nxt=True)
    return pl.pallas_call(
        body, name=name, grid=(dims.batch_local, nst),
        in_specs=[cur, nxt, *_seq_specs(dims, ts, c, CONV_HALO, 0), *_seq_specs(dims, ts, c, CONV_HALO, 1),
                  pl.BlockSpec((CONV_HALO, c), lambda b, i: (0, 0))],
        out_specs=[pl.BlockSpec((ts, 2 * c), lambda b, i: (b * nst + i, 0)),
                   pl.BlockSpec((CONV_HALO, c), lambda b, i: (0, 0)), pl.BlockSpec((1, c), lambda b, i: (0, 0))],
        out_shape=[jax.ShapeDtypeStruct((t, 2 * c), BF16), jax.ShapeDtypeStruct((CONV_HALO, c), F32),
                   jax.ShapeDtypeStruct((1, c), F32)],
        scratch_shapes=[pltpu.VMEM((CONV_HALO + ts, c), F32)] * 2
        + [pltpu.VMEM((SUBLANES - 1, CONV_HALO + ts - SUBLANES, c), F32)] * 2,
        compiler_params=_params("arbitrary", "arbitrary"),
    )(da1, da1, z, z, z, z, w)


FFN_ROWS = 16
FFN_COLS = 256


def _ffn_chunks(ts, f):
    cw = _pick(f, FFN_COLS)
    return [(r0, c0, cw) for r0 in range(0, ts, FFN_ROWS) for c0 in range(0, f, cw)]


def _tap_sources(buf, moved, offsets, rows):
    taps, used = [], 0
    for off in offsets:
        if off % SUBLANES:
            moved[used] = buf[pl.ds(off, rows), :]
            taps.append((moved.at[used], 0))
            used += 1
        else:
            taps.append((buf, off))
    return taps


def _moved_copies(offsets):
    return sum(1 for off in offsets if off % SUBLANES)


def _taps_sum(taps, w_ref, init, r0, cols):
    for k, (src, off) in enumerate(taps):
        init = init + w_ref[k:k + 1, cols] * src[pl.ds(off + r0, FFN_ROWS), cols]
    return init


def _ffn_act_fwd(up, w, b, dims, *, name, ts=128):
    t, f, kw = up.shape[0], dims.d_ff, dims.ffn_conv_width
    offsets = [FFN_HALO - (kw - 1) + k for k in range(kw)]

    def body(up_ref, h_ref, w_ref, b_ref, o_ref, buf, moved):
        buf[FFN_HALO:, :] = up_ref[...]
        buf[0:FFN_HALO, :] = jnp.where(pl.program_id(1) > 0, h_ref[...], 0.0)
        taps = _tap_sources(buf, moved, offsets, ts)
        for r0, c0, cw in _ffn_chunks(ts, f):
            vcols, gcols = slice(c0, c0 + cw), slice(f + c0, f + c0 + cw)
            uv = _taps_sum(taps, w_ref, jnp.broadcast_to(b_ref[:, vcols], (FFN_ROWS, cw)), r0, vcols)
            ug = _taps_sum(taps, w_ref, jnp.broadcast_to(b_ref[:, gcols], (FFN_ROWS, cw)), r0, gcols)
            o_ref[r0:r0 + FFN_ROWS, vcols] = (ug * _sigmoid(ug) * uv).astype(BF16)

    full = lambda rows: pl.BlockSpec((rows, 2 * f), lambda b_, i: (0, 0))
    return pl.pallas_call(
        body, name=name, grid=(dims.batch_local, dims.seq // ts),
        in_specs=[*_seq_specs(dims, ts, 2 * f, FFN_HALO, 0), full(FFN_HALO), full(1)],
        out_specs=pl.BlockSpec((ts, f), lambda b_, i: (b_ * (dims.seq // ts) + i, 0)),
        out_shape=jax.ShapeDtypeStruct((t, f), BF16),
        scratch_shapes=[pltpu.VMEM((FFN_HALO + ts, 2 * f), F32), pltpu.VMEM((_moved_copies(offsets), ts, 2 * f), F32)],
        compiler_params=_params("parallel", "parallel"),
    )(up, up, w, b)


def _ffn_act_bwd(dact, up, w, b, dims, *, name, ts=128):
    t, f, kw = up.shape[0], dims.d_ff, dims.ffn_conv_width
    offsets = [FFN_HALO - (kw - 1) + k for k in range(kw)]

    def body(d_ref, up_ref, h_ref, w_ref, b_ref, du_ref, dw_ref, db_ref, buf, moved):
        i = pl.program_id(1)
        first = jnp.logical_and(pl.program_id(0) == 0, i == 0)
        buf[FFN_HALO:, :] = up_ref[...]
        buf[0:FFN_HALO, :] = jnp.where(i > 0, h_ref[...], 0.0)
        taps = _tap_sources(buf, moved, offsets, ts)
        for r0, c0, cw in _ffn_chunks(ts, f):
            vcols, gcols = slice(c0, c0 + cw), slice(f + c0, f + c0 + cw)
            uv = _taps_sum(taps, w_ref, jnp.broadcast_to(b_ref[:, vcols], (FFN_ROWS, cw)), r0, vcols)
            ug = _taps_sum(taps, w_ref, jnp.broadcast_to(b_ref[:, gcols], (FFN_ROWS, cw)), r0, gcols)
            d = d_ref[r0:r0 + FFN_ROWS, vcols].astype(F32)
            sg = _sigmoid(ug)
            du_ref[r0:r0 + FFN_ROWS, vcols] = d * ug * sg
            du_ref[r0:r0 + FFN_ROWS, gcols] = d * uv * sg * (1.0 + ug * (1.0 - sg))

        @pl.when(first)
        def _():
            dw_ref[...] = jnp.zeros_like(dw_ref)
            db_ref[...] = jnp.zeros_like(db_ref)

        du = du_ref[...]
        db_ref[...] += jnp.sum(du, axis=0, keepdims=True)
        for k, (src, off) in enumerate(taps):
            dw_ref[k:k + 1, :] += jnp.sum(du * src[pl.ds(off, ts), :], axis=0, keepdims=True)

    nst = dims.seq // ts
    n_moved = _moved_copies(offsets)
    full = lambda rows: pl.BlockSpec((rows, 2 * f), lambda b_, i: (0, 0))
    return pl.pallas_call(
        body, name=name, grid=(dims.batch_local, nst),
        in_specs=[pl.BlockSpec((ts, f), lambda b_, i: (b_ * nst + i, 0)),
                  *_seq_specs(dims, ts, 2 * f, FFN_HALO, 0), full(FFN_HALO), full(1)],
        out_specs=[pl.BlockSpec((ts, 2 * f), lambda b_, i: (b_ * nst + i, 0)), full(FFN_HALO), full(1)],
        out_shape=[jax.ShapeDtypeStruct((t, 2 * f), F32), jax.ShapeDtypeStruct((FFN_HALO, 2 * f), F32),
                   jax.ShapeDtypeStruct((1, 2 * f), F32)],
        scratch_shapes=[pltpu.VMEM((FFN_HALO + ts, 2 * f), F32), pltpu.VMEM((n_moved, ts, 2 * f), F32)],
        compiler_params=_params("arbitrary", "arbitrary"),
    )(dact, up, up, w, b)


def _ffn_conv_bwd(du, w, dims, *, name, ts=128):
    t, f2 = du.shape
    kw = dims.ffn_conv_width
    nst = dims.seq // ts

    offsets = [(kw - 1) - k for k in range(kw)]

    def body(d_ref, dn_ref, w_ref, o_ref, buf, moved):
        buf[0:ts, :] = d_ref[...]
        buf[ts:, :] = jnp.where(pl.program_id(1) < nst - 1, dn_ref[...], 0.0)
        taps = _tap_sources(buf, moved, offsets, ts)
        for r0, c0, cw in _ffn_chunks(ts, f2):
            cols = slice(c0, c0 + cw)
            o_ref[r0:r0 + FFN_ROWS, cols] = _taps_sum(taps, w_ref, jnp.zeros((FFN_ROWS, cw), F32), r0, cols).astype(BF16)

    return pl.pallas_call(
        body, name=name, grid=(dims.batch_local, nst),
        in_specs=[*_seq_specs(dims, ts, f2, FFN_HALO, 0, nxt=True), pl.BlockSpec((FFN_HALO, f2), lambda b_, i: (0, 0))],
        out_specs=pl.BlockSpec((ts, f2), lambda b_, i: (b_ * nst + i, 0)),
        out_shape=jax.ShapeDtypeStruct((t, f2), BF16),
        scratch_shapes=[pltpu.VMEM((ts + FFN_HALO, f2), F32), pltpu.VMEM((_moved_copies(offsets), ts, f2), F32)],
        compiler_params=_params("parallel", "parallel"),
    )(du, du, w)


def _alibi_slope(h, n_heads):
    return 2.0 ** (-8.0 * (h + 1) / n_heads)


def _dot_nt(a, b):
    return lax.dot_general(a, b, (((1,), (1,)), ((), ())), preferred_element_type=F32)


def _dot_tn(a, b):
    return lax.dot_general(a, b, (((0,), (0,)), ((), ())), preferred_element_type=F32)


def _attn_view(x, dims, dil):
    return x.reshape(dims.batch_local, dims.seq // dil, dil * x.shape[-1])


def _attn_fwd_group(q, k, v, state, dims, dil, *, last, name):
    t, a = q.shape
    assert 2 * dims.head_dim == 128 and dims.n_heads % 2 == 0
    blk, hd = ATTN_BLOCK, dims.head_dim
    nb = dims.seq // dil // blk
    has_prev = nb > 1
    nkeys = 2 * blk if has_prev else blk

    def body(*refs):
        it = iter(refs)
        q_ref, kc_ref, vc_ref = next(it), next(it), next(it)
        kp_ref, vp_ref = (next(it), next(it)) if has_prev else (None, None)
        m_in, l_in, acc_in = (next(it), next(it), next(it)) if state is not None else (None, None, None)
        outs = list(it)
        iq = lax.broadcasted_iota(jnp.int32, (blk, nkeys), 0)
        jk = lax.broadcasted_iota(jnp.int32, (blk, nkeys), 1)
        if has_prev:
            steps = iq + blk - jk
            valid = (steps >= 0) & (steps <= blk) & ((jk >= blk) | (pl.program_id(2) > 0))
        else:
            steps = iq - jk
            valid = steps >= 0
        dist = steps.astype(F32) * float(dil)
        low = lax.broadcasted_iota(jnp.int32, (blk, 2 * hd), 1) < hd
        for hp in range(dims.n_heads // 2):
            sl = slice(2 * hd * hp, 2 * hd * (hp + 1))
            q2 = q_ref[:, sl]
            if has_prev:
                kcat = jnp.concatenate([kp_ref[:, sl], kc_ref[:, sl]], axis=0)
                vcat = jnp.concatenate([vp_ref[:, sl], vc_ref[:, sl]], axis=0)
            else:
                kcat, vcat = kc_ref[:, sl], vc_ref[:, sl]
            halves = []
            for half in range(2):
                col = 2 * hd * hp + hd * half
                qh = jnp.where(low if half == 0 else jnp.logical_not(low), q2, jnp.zeros_like(q2))
                sc = _dot_nt(qh, kcat) - _alibi_slope(2 * hp + half, dims.n_heads) * dist
                sc = jnp.where(valid, sc, MASKED_SCORE)
                row_max = jnp.max(sc, axis=-1, keepdims=True)
                if state is None:
                    m_new = row_max
                    p = jnp.exp(sc - m_new)
                    alpha = None
                    l_new = jnp.sum(p, axis=-1, keepdims=True)
                else:
                    m_old = m_in[:, col:col + 1]
                    m_new = jnp.maximum(m_old, row_max)
                    p = jnp.exp(sc - m_new)
                    alpha = jnp.exp(m_old - m_new)
                    l_new = alpha * l_in[:, col:col + 1] + jnp.sum(p, axis=-1, keepdims=True)
                pv = jnp.dot(p.astype(BF16), vcat, preferred_element_type=F32)
                halves.append((m_new, l_new, alpha, pv))
            (m_a, l_a, al_a, pv_a), (m_b, l_b, al_b, pv_b) = halves
            if state is None:
                acc = jnp.where(low, pv_a, pv_b)
            else:
                old = acc_in[:, sl]
                acc = jnp.where(low, al_a * old + pv_a, al_b * old + pv_b)
            m2 = jnp.where(low, m_a, m_b)
            l2 = jnp.where(low, l_a, l_b)
            if last:
                outs[0][:, sl] = (acc / l2).astype(BF16)
                outs[1][:, sl] = m2 + jnp.log(l2)
            else:
                outs[0][:, sl] = m2
                outs[1][:, sl] = l2
                outs[2][:, sl] = acc

    cur = pl.BlockSpec((None, blk, a), lambda b, r, i: (b, i, r))
    prev = pl.BlockSpec((None, blk, a), lambda b, r, i: (b, jnp.maximum(i - 1, 0), r))
    args, in_specs = [q, k, v], [cur, cur, cur]
    if has_prev:
        args += [k, v]
        in_specs += [prev, prev]
    if state is not None:
        args += list(state)
        in_specs += [cur] * 3
    shape = lambda dt: jax.ShapeDtypeStruct((dims.batch_local, dims.seq // dil, dil * a), dt)
    out_shape = [shape(BF16), shape(F32)] if last else [shape(F32)] * 3
    outs = pl.pallas_call(
        body, name=name, grid=(dims.batch_local, dil, nb),
        in_specs=in_specs, out_specs=[cur] * len(out_shape), out_shape=out_shape,
        compiler_params=_params("parallel", "parallel", "parallel"),
    )(*[_attn_view(x, dims, dil) for x in args])
    return tuple(o.reshape(t, a) for o in outs)


def _attn_delta(do, o, head_ones, dims, *, name, tr=512):
    t, a = o.shape
    tr = _pick(t, tr, 8)

    def body(do_ref, o_ref, e_ref, d_ref):
        prod = do_ref[...].astype(F32) * o_ref[...].astype(F32)
        d_ref[...] = _head_mean(prod, e_ref, dims.head_dim) * float(dims.head_dim)

    return pl.pallas_call(
        body, name=name, grid=(t // tr,),
        in_specs=[_row_spec(tr, a), _row_spec(tr, a), pl.BlockSpec((a, a), lambda i: (0, 0))],
        out_specs=_row_spec(tr, a), out_shape=jax.ShapeDtypeStruct((t, a), F32),
        compiler_params=_params("parallel"),
    )(do, o, head_ones)


def _attn_bwd_group(q, k, v, do, lse, delta, dims, dil, *, name):
    t, a = q.shape
    blk, hd = ATTN_BLOCK, dims.head_dim
    nb = dims.seq // dil // blk
    has_next = nb > 1

    def body(*refs):
        k_ref, v_ref, q_ref, do_ref, lse_ref, dl_ref = refs[:6]
        if has_next:
            qn_ref, don_ref, lsen_ref, dln_ref = refs[6:10]
            dq_ref, dk_ref, dv_ref, carry = refs[10:]
        else:
            dq_ref, dk_ref, dv_ref = refs[6:]
        j = pl.program_id(2)
        iq = lax.broadcasted_iota(jnp.int32, (blk, blk), 0)
        jk = lax.broadcasted_iota(jnp.int32, (blk, blk), 1)
        low = lax.broadcasted_iota(jnp.int32, (blk, 2 * hd), 1) < hd

        def pair(hp, qr, dor, lser, dlr, steps, valid):
            sl = slice(2 * hd * hp, 2 * hd * (hp + 1))
            q2, do2, k2, v2 = qr[:, sl], dor[:, sl], k_ref[:, sl], v_ref[:, sl]
            dist = steps.astype(F32) * float(dil)
            dq_h, dk2, dv2 = [], None, None
            for half in range(2):
                col = 2 * hd * hp + hd * half
                mask = low if half == 0 else jnp.logical_not(low)
                qh = jnp.where(mask, q2, jnp.zeros_like(q2))
                doh = jnp.where(mask, do2, jnp.zeros_like(do2))
                sc = _dot_nt(qh, k2) - _alibi_slope(2 * hp + half, dims.n_heads) * dist
                p = jnp.where(valid, jnp.exp(sc - lser[:, col:col + 1]), 0.0)
                ds = p * (_dot_nt(doh, v2) - dlr[:, col:col + 1])
                ds_b, p_b = ds.astype(BF16), p.astype(BF16)
                dq_h.append(jnp.dot(ds_b, k2, preferred_element_type=F32))
                dk_h, dv_h = _dot_tn(ds_b, qh), _dot_tn(p_b, doh)
                dk2 = dk_h if dk2 is None else dk2 + dk_h
                dv2 = dv_h if dv2 is None else dv2 + dv_h
            return sl, jnp.where(low, dq_h[0], dq_h[1]), dk2, dv2

        if has_next:
            @pl.when(j == 0)
            def _():
                carry[...] = jnp.zeros_like(carry)

        for hp in range(dims.n_heads // 2):
            sl, dq2, dk2, dv2 = pair(hp, q_ref, do_ref, lse_ref, dl_ref, iq - jk, iq >= jk)
            dq_ref[:, sl] = (carry[:, sl] + dq2) if has_next else dq2
            dk_ref[:, sl] = dk2
            dv_ref[:, sl] = dv2

        if has_next:
            @pl.when(j + 1 < nb)
            def _():
                for hp in range(dims.n_heads // 2):
                    sl, dq2, dk2, dv2 = pair(hp, qn_ref, don_ref, lsen_ref, dln_ref, iq - jk + blk, jk >= iq)
                    carry[:, sl] = dq2
                    dk_ref[:, sl] += dk2
                    dv_ref[:, sl] += dv2

    cur = pl.BlockSpec((None, blk, a), lambda b, r, j: (b, j, r))
    nxt = pl.BlockSpec((None, blk, a), lambda b, r, j: (b, jnp.minimum(j + 1, nb - 1), r))
    args, in_specs = [k, v, q, do, lse, delta], [cur] * 6
    if has_next:
        args += [q, do, lse, delta]
        in_specs += [nxt] * 4
    shape = jax.ShapeDtypeStruct((dims.batch_local, dims.seq // dil, dil * a), F32)
    outs = pl.pallas_call(
        body, name=name, grid=(dims.batch_local, dil, nb),
        in_specs=in_specs, out_specs=[cur] * 3, out_shape=[shape] * 3,
        scratch_shapes=[pltpu.VMEM((blk, a), F32)] if has_next else [],
        compiler_params=_params("parallel", "parallel", "arbitrary"),
    )(*[_attn_view(x, dims, dil) for x in args])
    return tuple(o.reshape(t, a) for o in outs)


LANES = 128
MASK_BIAS = 1e30
RESIDUE_DILATIONS = tuple(d for d in DILATIONS if d > 1)


def _rows_to_residues(value, out_ref, scr, d):
    rows, width = value.shape
    for c in range(width // LANES):
        cols = slice(LANES * c, LANES * (c + 1))
        scr[c] = value[:, cols]
        for r in range(d):
            out_ref[r, :, cols] = scr[c, pl.ds(r, rows // d, stride=d), :].astype(out_ref.dtype)


def _residues_to_rows(in_ref, scr, d):
    _, n, width = in_ref.shape
    slabs = []
    for c in range(width // LANES):
        cols = slice(LANES * c, LANES * (c + 1))
        for r in range(d):
            scr[c, pl.ds(r, n, stride=d), :] = in_ref[r, :, cols].astype(F32)
        slabs.append(scr[c])
    return slabs[0] if len(slabs) == 1 else jnp.concatenate(slabs, axis=1)


def _residue_shape(dims, d, width, dtype):
    return jax.ShapeDtypeStruct((dims.batch_local, d, dims.seq // d, width), dtype)


def _residue_spec(dims, d, tr, width):
    tiles = dims.seq // tr
    return pl.BlockSpec((None, d, tr // d, width), lambda i: (i // tiles, 0, i % tiles, 0))


def _head_sum_matrix(dims):
    a = dims.n_heads * dims.head_dim
    head = jnp.arange(a, dtype=jnp.int32) // dims.head_dim
    return (head[:, None] == jnp.arange(LANES, dtype=jnp.int32)[None, :]).astype(BF16)


def _two_pass_dot(v, m):
    hi = v.astype(BF16)
    lo = (v - hi.astype(F32)).astype(BF16)
    return jnp.dot(hi, m, preferred_element_type=F32) + jnp.dot(lo, m, preferred_element_type=F32)


def _qkv_layouts_fwd(z, gq, gk, head_ones, dims, *, name, tr=256):
    t = z.shape[0]
    a = dims.n_heads * dims.head_dim
    q_scale = dims.head_dim ** -0.5
    nres = len(RESIDUE_DILATIONS)

    def body(q_ref, k_ref, v_ref, gq_ref, gk_ref, sum_ref, spread_ref, *rest):
        outs, scr = rest[:-1], rest[-1]
        qv, kv = q_ref[...].astype(F32), k_ref[...].astype(F32)
        mean = lambda val: _two_pass_dot(_two_pass_dot(val, sum_ref[...]), spread_ref[...]) * (1.0 / dims.head_dim)
        rq = lax.rsqrt(mean(qv * qv) + RMS_EPS)
        rk = lax.rsqrt(mean(kv * kv) + RMS_EPS)
        values = (qv * rq * gq_ref[...] * q_scale, kv * rk * gk_ref[...], v_ref[...].astype(F32))
        for j, val in enumerate(values):
            outs[j][...] = val.astype(BF16)
            for g, d in enumerate(RESIDUE_DILATIONS):
                _rows_to_residues(val, outs[3 * (g + 1) + j], scr, d)

    out_specs = [_row_spec(tr, a)] * 3
    out_shape = [jax.ShapeDtypeStruct((t, a), BF16)] * 3
    for d in RESIDUE_DILATIONS:
        out_specs += [_residue_spec(dims, d, tr, a)] * 3
        out_shape += [_residue_shape(dims, d, a, BF16)] * 3
    outs = pl.pallas_call(
        body, name=name, grid=(t // tr,),
        in_specs=[_row_spec(tr, a, 2), _row_spec(tr, a, 3), _row_spec(tr, a, 4), _vec_spec(a), _vec_spec(a),
                  pl.BlockSpec((a, LANES), lambda i: (0, 0)), pl.BlockSpec((LANES, a), lambda i: (0, 0))],
        out_specs=out_specs, out_shape=out_shape,
        scratch_shapes=[pltpu.VMEM((a // LANES, tr, LANES), F32)],
        compiler_params=_params("parallel"),
    )(z, z, z, gq, gk, *head_ones)
    return {d: tuple(outs[3 * g:3 * g + 3]) for g, d in enumerate((1,) + RESIDUE_DILATIONS)}


def _attn_specs(dims, dil, width):
    blk = ATTN_BLOCK
    nb = dims.seq // dil // blk
    if dil == 1:
        grid = (dims.batch_local, nb)
        at = lambda f: pl.BlockSpec((blk, width), lambda b, i: (b * nb + f(i), 0))
    else:
        grid = (dims.batch_local, dil, nb)
        at = lambda f: pl.BlockSpec((None, None, blk, width), lambda b, r, i: (b, r, f(i), 0))
    return grid, at(lambda i: i), at(lambda i: jnp.maximum(i - 1, 0)), at(lambda i: jnp.minimum(i + 1, nb - 1))


def _head_slopes(n_heads):
    h = lax.broadcasted_iota(jnp.int32, (n_heads, 1, 1), 0).astype(F32)
    return jnp.exp((h + 1.0) * (-8.0 / n_heads * math.log(2.0)))


def _pair_masks(hd):
    low = lax.broadcasted_iota(jnp.int32, (1, 2 * hd), 1) < hd
    return low, jnp.logical_not(low)


def _attn_fwd(q, k, v, dims, dil, *, name):
    a = dims.n_heads * dims.head_dim
    heads, hd, blk = dims.n_heads, dims.head_dim, ATTN_BLOCK
    assert 2 * hd == LANES and heads % 2 == 0 and heads <= LANES
    nb = dims.seq // dil // blk
    has_prev = nb > 1
    nkeys = 2 * blk if has_prev else blk
    grid, cur, prev, _ = _attn_specs(dims, dil, a)
    _, cur_stat, _, _ = _attn_specs(dims, dil, LANES)

    def body(*refs):
        if has_prev:
            q_ref, kc_ref, vc_ref, kp_ref, vp_ref, o_ref, lse_ref, s_scr, p_scr = refs
        else:
            q_ref, kc_ref, vc_ref, o_ref, lse_ref, s_scr, p_scr = refs
        low, high = _pair_masks(hd)

        def keys(cur_ref, prev_ref, sl):
            return jnp.concatenate([prev_ref[:, sl], cur_ref[:, sl]], axis=0) if has_prev else cur_ref[:, sl]

        for hp in range(heads // 2):
            sl = slice(LANES * hp, LANES * (hp + 1))
            q2 = q_ref[:, sl]
            kcat = keys(kc_ref, kp_ref if has_prev else None, sl)
            s_scr[2 * hp] = _dot_nt(jnp.where(low, q2, jnp.zeros_like(q2)), kcat)
            s_scr[2 * hp + 1] = _dot_nt(jnp.where(high, q2, jnp.zeros_like(q2)), kcat)

        iq = lax.broadcasted_iota(jnp.int32, (blk, nkeys), 0)
        jk = lax.broadcasted_iota(jnp.int32, (blk, nkeys), 1)
        if has_prev:
            steps = iq + blk - jk
            valid = (steps >= 0) & (steps <= blk) & ((jk >= blk) | (pl.program_id(len(grid) - 1) > 0))
        else:
            steps = iq - jk
            valid = steps >= 0
        bias = jnp.where(valid, steps.astype(F32) * (-float(dil)), -MASK_BIAS)
        s = s_scr[...] + _head_slopes(heads) * bias[None]
        m = jnp.max(s, axis=-1, keepdims=True)
        p = jnp.exp(s - m)
        l = jnp.sum(p, axis=-1, keepdims=True)
        p_scr[...] = p.astype(BF16)
        inv = 1.0 / l
        lse = m + jnp.log(l)

        lane = lax.broadcasted_iota(jnp.int32, (blk, LANES), 1)
        stat = jnp.zeros((blk, LANES), F32)
        for hp in range(heads // 2):
            sl = slice(LANES * hp, LANES * (hp + 1))
            vcat = keys(vc_ref, vp_ref if has_prev else None, sl)
            pv_a = jnp.dot(p_scr[2 * hp], vcat, preferred_element_type=F32) * inv[2 * hp]
            pv_b = jnp.dot(p_scr[2 * hp + 1], vcat, preferred_element_type=F32) * inv[2 * hp + 1]
            o_ref[:, sl] = jnp.where(low, pv_a, pv_b)
            stat = jnp.where(lane == 2 * hp, lse[2 * hp], stat)
            stat = jnp.where(lane == 2 * hp + 1, lse[2 * hp + 1], stat)
        lse_ref[...] = stat

    lead = q.shape[:-2]
    rows = q.shape[-2]
    o, lse = pl.pallas_call(
        body, name=name, grid=grid,
        in_specs=[cur, cur, cur] + ([prev, prev] if has_prev else []),
        out_specs=[cur, cur_stat],
        out_shape=[jax.ShapeDtypeStruct(lead + (rows, a), F32), jax.ShapeDtypeStruct(lead + (rows, LANES), F32)],
        scratch_shapes=[pltpu.VMEM((heads, blk, nkeys), F32), pltpu.VMEM((heads, blk, nkeys), BF16)],
        compiler_params=_params(*["parallel"] * len(grid)),
    )(q, k, v, *([k, v] if has_prev else []))
    return o, lse


def _attn_combine(groups, head_spread, dims, *, name, tr=256):
    t = dims.tokens
    a = dims.n_heads * dims.head_dim
    dils = tuple(groups)

    def body(*refs):
        ins = refs[:2 * len(dils)]
        x_ref = refs[2 * len(dils)]
        o_ref = refs[2 * len(dils) + 1]
        lse_refs = refs[2 * len(dils) + 2:-2]
        scr, scr_stat = refs[-2], refs[-1]
        outs, stats = [], []
        for g, d in enumerate(dils):
            if d == 1:
                outs.append(ins[2 * g][...])
                stats.append(ins[2 * g + 1][...])
            else:
                outs.append(_residues_to_rows(ins[2 * g], scr, d))
                stats.append(_residues_to_rows(ins[2 * g + 1], scr_stat, d))
        top = functools.reduce(jnp.maximum, stats)
        weights = [jnp.exp(s - top) for s in stats]
        total = functools.reduce(jnp.add, weights)
        joint = top + jnp.log(total)
        inv = 1.0 / total
        acc = None
        for w, o in zip(weights, outs):
            term = _two_pass_dot(w * inv, x_ref[...]) * o
            acc = term if acc is None else acc + term
        o_ref[...] = acc.astype(BF16)
        for g, d in enumerate(dils):
            if d == 1:
                lse_refs[g][...] = joint
            else:
                _rows_to_residues(joint, lse_refs[g], scr_stat, d)

    in_specs, args, lse_specs, lse_shapes = [], [], [], []
    for d in dils:
        if d == 1:
            in_specs += [_row_spec(tr, a), _row_spec(tr, LANES)]
            lse_specs.append(_row_spec(tr, LANES))
            lse_shapes.append(jax.ShapeDtypeStruct((t, LANES), F32))
        else:
            in_specs += [_residue_spec(dims, d, tr, a), _residue_spec(dims, d, tr, LANES)]
            lse_specs.append(_residue_spec(dims, d, tr, LANES))
            lse_shapes.append(_residue_shape(dims, d, LANES, F32))
        args += list(groups[d])
    outs = pl.pallas_call(
        body, name=name, grid=(t // tr,),
        in_specs=in_specs + [pl.BlockSpec((LANES, a), lambda i: (0, 0))],
        out_specs=[_row_spec(tr, a)] + lse_specs,
        out_shape=[jax.ShapeDtypeStruct((t, a), BF16)] + lse_shapes,
        scratch_shapes=[pltpu.VMEM((a // LANES, tr, LANES), F32), pltpu.VMEM((1, tr, LANES), F32)],
        compiler_params=_params("parallel"),
    )(*args, head_spread)
    return outs[0], dict(zip(dils, outs[1:]))


def _attn_bwd_prep(do, o, head_sum, dims, *, name, tr=256):
    t, a = o.shape

    def body(do_ref, o_ref, e_ref, *rest):
        outs, scr, scr_stat = rest[:-2], rest[-2], rest[-1]
        dov = do_ref[...].astype(F32)
        delta = _two_pass_dot(dov * o_ref[...].astype(F32), e_ref[...])
        outs[0][...] = delta
        for g, d in enumerate(RESIDUE_DILATIONS):
            _rows_to_residues(dov, outs[1 + 2 * g], scr, d)
            _rows_to_residues(delta, outs[2 + 2 * g], scr_stat, d)

    out_specs, out_shape = [_row_spec(tr, LANES)], [jax.ShapeDtypeStruct((t, LANES), F32)]
    for d in RESIDUE_DILATIONS:
        out_specs += [_residue_spec(dims, d, tr, a), _residue_spec(dims, d, tr, LANES)]
        out_shape += [_residue_shape(dims, d, a, BF16), _residue_shape(dims, d, LANES, F32)]
    outs = pl.pallas_call(
        body, name=name, grid=(t // tr,),
        in_specs=[_row_spec(tr, a), _row_spec(tr, a), pl.BlockSpec((a, LANES), lambda i: (0, 0))],
        out_specs=out_specs, out_shape=out_shape,
        scratch_shapes=[pltpu.VMEM((a // LANES, tr, LANES), F32), pltpu.VMEM((1, tr, LANES), F32)],
        compiler_params=_params("parallel"),
    )(do, o, head_sum)
    dos, deltas = {1: do}, {1: outs[0]}
    for g, d in enumerate(RESIDUE_DILATIONS):
        dos[d], deltas[d] = outs[1 + 2 * g], outs[2 + 2 * g]
    return dos, deltas


def _attn_bwd(q, k, v, do, lse, delta, dims, dil, *, name):
    a = dims.n_heads * dims.head_dim
    heads, hd, blk = dims.n_heads, dims.head_dim, ATTN_BLOCK
    nb = dims.seq // dil // blk
    has_next = nb > 1
    nq = 2 * blk if has_next else blk
    grid, cur, _, nxt = _attn_specs(dims, dil, a)
    _, cur_stat, _, nxt_stat = _attn_specs(dims, dil, LANES)

    def body(*refs):
        k_ref, v_ref, q_ref, do_ref, lse_ref, dl_ref = refs[:6]
        if has_next:
            qn_ref, don_ref, lsen_ref, dln_ref = refs[6:10]
            dq_ref, dk_ref, dv_ref, s_scr, dp_scr, p_scr, ds_scr, carry = refs[10:]
        else:
            dq_ref, dk_ref, dv_ref, s_scr, dp_scr, p_scr, ds_scr = refs[6:]
        j = pl.program_id(len(grid) - 1)
        low, high = _pair_masks(hd)

        def stacked(ref, nref, sl):
            return jnp.concatenate([ref[:, sl], nref[:, sl]], axis=0) if has_next else ref[:, sl]

        def halves(x):
            return jnp.where(low, x, jnp.zeros_like(x)), jnp.where(high, x, jnp.zeros_like(x))

        for hp in range(heads // 2):
            sl = slice(LANES * hp, LANES * (hp + 1))
            k2, v2 = k_ref[:, sl], v_ref[:, sl]
            q_a, q_b = halves(stacked(q_ref, qn_ref if has_next else None, sl))
            do_a, do_b = halves(stacked(do_ref, don_ref if has_next else None, sl))
            s_scr[2 * hp], s_scr[2 * hp + 1] = _dot_nt(q_a, k2), _dot_nt(q_b, k2)
            dp_scr[2 * hp], dp_scr[2 * hp + 1] = _dot_nt(do_a, v2), _dot_nt(do_b, v2)

        rq = lax.broadcasted_iota(jnp.int32, (nq, blk), 0)
        jk = lax.broadcasted_iota(jnp.int32, (nq, blk), 1)
        if has_next:
            iq = jnp.where(rq < blk, rq, rq - blk)
            steps = jnp.where(rq < blk, iq - jk, iq - jk + blk)
            valid = ((rq < blk) & (iq >= jk)) | ((rq >= blk) & (jk >= iq) & (j + 1 < nb))
        else:
            steps, valid = rq - jk, rq >= jk
        bias = jnp.where(valid, steps.astype(F32) * (-float(dil)), -MASK_BIAS)
        lse_all = stacked(lse_ref, lsen_ref if has_next else None, slice(None))
        dl_all = stacked(dl_ref, dln_ref if has_next else None, slice(None))
        lse3 = jnp.stack([lse_all[:, h:h + 1] for h in range(heads)])
        dl3 = jnp.stack([dl_all[:, h:h + 1] for h in range(heads)])
        p = jnp.exp(s_scr[...] + _head_slopes(heads) * bias[None] - lse3)
        p_scr[...] = p.astype(BF16)
        ds_scr[...] = (p * (dp_scr[...] - dl3)).astype(BF16)

        if has_next:
            @pl.when(j == 0)
            def _():
                carry[...] = jnp.zeros_like(carry)

        for hp in range(heads // 2):
            sl = slice(LANES * hp, LANES * (hp + 1))
            k2 = k_ref[:, sl]
            q_a, q_b = halves(stacked(q_ref, qn_ref if has_next else None, sl))
            do_a, do_b = halves(stacked(do_ref, don_ref if has_next else None, sl))
            ds_a, ds_b = ds_scr[2 * hp], ds_scr[2 * hp + 1]
            dq2 = jnp.where(low, jnp.dot(ds_a, k2, preferred_element_type=F32),
                            jnp.dot(ds_b, k2, preferred_element_type=F32))
            dk_ref[:, sl] = _dot_tn(ds_a, q_a) + _dot_tn(ds_b, q_b)
            dv_ref[:, sl] = _dot_tn(p_scr[2 * hp], do_a) + _dot_tn(p_scr[2 * hp + 1], do_b)
            if has_next:
                dq_ref[:, sl] = carry[:, sl] + dq2[:blk]
                carry[:, sl] = dq2[blk:]
            else:
                dq_ref[:, sl] = dq2

    args, in_specs = [k, v, q, do, lse, delta], [cur] * 4 + [cur_stat] * 2
    if has_next:
        args += [q, do, lse, delta]
        in_specs += [nxt] * 2 + [nxt_stat] * 2
    shape = jax.ShapeDtypeStruct(q.shape, F32)
    scratch = [pltpu.VMEM((heads, nq, blk), F32)] * 2 + [pltpu.VMEM((heads, nq, blk), BF16)] * 2
    if has_next:
        scratch.append(pltpu.VMEM((blk, a), F32))
    return pl.pallas_call(
        body, name=name, grid=grid, in_specs=in_specs, out_specs=[cur] * 3, out_shape=[shape] * 3,
        scratch_shapes=scratch,
        compiler_params=_params(*["parallel"] * (len(grid) - 1), "arbitrary"),
    )(*args)


def _qkv_layouts_bwd(z, grads, gq, gk, head_ones, dims, *, name, tr=256):
    t = z.shape[0]
    a = dims.n_heads * dims.head_dim
    q_scale = dims.head_dim ** -0.5
    dils = tuple(grads)

    def body(q_ref, k_ref, *rest):
        d_refs = rest[:3 * len(dils)]
        gq_ref, gk_ref, sum_ref, spread_ref, dz_ref, dgq_ref, dgk_ref, scr = rest[3 * len(dils):]
        first = pl.program_id(0) == 0
        mean = lambda val: _two_pass_dot(_two_pass_dot(val, sum_ref[...]), spread_ref[...]) * (1.0 / dims.head_dim)

        def total(j):
            acc = None
            for g, d in enumerate(dils):
                ref = d_refs[3 * g + j]
                part = ref[...] if d == 1 else _residues_to_rows(ref, scr, d)
                acc = part if acc is None else acc + part
            return acc

        def norm_bwd(x_ref, dy, g_ref, scale, col, dg_ref):
            xv = x_ref[...].astype(F32)
            dy = dy * scale
            r = lax.rsqrt(mean(xv * xv) + RMS_EPS)
            gy = dy * g_ref[...]
            dx = r * gy - xv * (r * r * r) * mean(xv * gy)
            dz_ref[:, col * a:(col + 1) * a] = dx.astype(BF16)
            _accumulate(dg_ref, jnp.sum(dy * xv * r, axis=0, keepdims=True), first)

        norm_bwd(q_ref, total(0), gq_ref, q_scale, 0, dgq_ref)
        norm_bwd(k_ref, total(1), gk_ref, 1.0, 1, dgk_ref)
        dz_ref[:, 2 * a:3 * a] = total(2).astype(BF16)

    in_specs, args = [_row_spec(tr, a, 2), _row_spec(tr, a, 3)], [z, z]
    for d in dils:
        in_specs += [_row_spec(tr, a) if d == 1 else _residue_spec(dims, d, tr, a)] * 3
        args += list(grads[d])
    in_specs += [_vec_spec(a), _vec_spec(a), pl.BlockSpec((a, LANES), lambda i: (0, 0)),
                 pl.BlockSpec((LANES, a), lambda i: (0, 0))]
    return pl.pallas_call(
        body, name=name, grid=(t // tr,), in_specs=in_specs,
        out_specs=[_row_spec(tr, 3 * a), _vec_spec(a), _vec_spec(a)],
        out_shape=[jax.ShapeDtypeStruct((t, 3 * a), BF16)] + [jax.ShapeDtypeStruct((1, a), F32)] * 2,
        scratch_shapes=[pltpu.VMEM((a // LANES, tr, LANES), F32)],
        compiler_params=_params("arbitrary"),
    )(*args, gq, gk, *head_ones)


def _mix_fwd(ya, yb, z, gate_b, dims, *, name, tr=512):
    t, d = ya.shape
    tr = _pick(t, tr, 8)
    first_gate_col = z.shape[1] // d - 2

    def body(ya_ref, yb_ref, ga_ref, gb_ref, ba_ref, bb_ref, o_ref):
        g_a = _sigmoid(ga_ref[...].astype(F32) + ba_ref[...])
        g_b = _sigmoid(gb_ref[...].astype(F32) + bb_ref[...])
        o_ref[...] = (g_a * ya_ref[...] + g_b * yb_ref[...]).astype(BF16)

    return pl.pallas_call(
        body, name=name, grid=(t // tr,),
        in_specs=[_row_spec(tr, d), _row_spec(tr, d), _row_spec(tr, d, first_gate_col),
                  _row_spec(tr, d, first_gate_col + 1), _vec_spec(d, 0), _vec_spec(d, 1)],
        out_specs=_row_spec(tr, d), out_shape=jax.ShapeDtypeStruct((t, d), BF16),
        compiler_params=_params("parallel"),
    )(ya, yb, z, z, gate_b, gate_b)


def _mix_bwd(dmix, ya, yb, z, gate_b, dims, *, name, tr=512):
    t, d = ya.shape
    tr = _pick(t, tr, 8)
    first_gate_col = z.shape[1] // d - 2

    def body(dm_ref, ya_ref, yb_ref, ga_ref, gb_ref, ba_ref, bb_ref, dya_ref, dyb_ref, dz_ref, db_ref):
        dm = dm_ref[...].astype(F32)
        g_a = _sigmoid(ga_ref[...].astype(F32) + ba_ref[...])
        g_b = _sigmoid(gb_ref[...].astype(F32) + bb_ref[...])
        dya_ref[...] = (dm * g_a).astype(BF16)
        dyb_ref[...] = (dm * g_b).astype(BF16)
        dl_a = dm * ya_ref[...] * g_a * (1.0 - g_a)
        dl_b = dm * yb_ref[...] * g_b * (1.0 - g_b)
        dz_ref[:, 0:d] = dl_a.astype(BF16)
        dz_ref[:, d:2 * d] = dl_b.astype(BF16)
        first = pl.program_id(0) == 0
        sums = jnp.concatenate([jnp.sum(dl_a, axis=0, keepdims=True), jnp.sum(dl_b, axis=0, keepdims=True)], axis=1)
        _accumulate(db_ref, sums, first)

    return pl.pallas_call(
        body, name=name, grid=(t // tr,),
        in_specs=[_row_spec(tr, d), _row_spec(tr, d), _row_spec(tr, d), _row_spec(tr, d, first_gate_col),
                  _row_spec(tr, d, first_gate_col + 1), _vec_spec(d, 0), _vec_spec(d, 1)],
        out_specs=[_row_spec(tr, d), _row_spec(tr, d), _row_spec(tr, 2 * d), _vec_spec(2 * d)],
        out_shape=[jax.ShapeDtypeStruct((t, d), BF16)] * 2 + [jax.ShapeDtypeStruct((t, 2 * d), BF16),
                                                              jax.ShapeDtypeStruct((1, 2 * d), F32)],
        compiler_params=_params("arbitrary"),
    )(dmix, ya, yb, z, z, gate_b, gate_b)


def _loss_head(y, target, *, name, tr=512):
    t, d = y.shape
    tr = _pick(t, tr, 8)

    def body(y_ref, t_ref, dy_ref, dyb_ref, loss_ref):
        err = y_ref[...] - t_ref[...]
        dy = err * (1.0 / d)
        dy_ref[...] = dy
        dyb_ref[...] = dy.astype(BF16)
        part = jnp.sum(jnp.sum(err * err, axis=-1, keepdims=True), axis=0, keepdims=True) * (0.5 / d)
        _accumulate(loss_ref, jnp.broadcast_to(part, (8, 128)), pl.program_id(0) == 0)

    return pl.pallas_call(
        body, name=name, grid=(t // tr,),
        in_specs=[_row_spec(tr, d), _row_spec(tr, d)],
        out_specs=[_row_spec(tr, d), _row_spec(tr, d), pl.BlockSpec((8, 128), lambda i: (0, 0))],
        out_shape=[jax.ShapeDtypeStruct((t, d), F32), jax.ShapeDtypeStruct((t, d), BF16),
                   jax.ShapeDtypeStruct((8, 128), F32)],
        compiler_params=_params("arbitrary"),
    )(y, target)


def _adamw(w, grads, m, v, *, name, tr=256):
    r, c = w.shape
    tr = _pick(r, tr, 8)
    ng = len(grads)
    c1 = 1.0 - ADAM_B1 ** ADAM_STEP
    c2 = 1.0 - ADAM_B2 ** ADAM_STEP

    def body(*refs):
        w_ref, g_refs, m_ref, v_ref = refs[0], refs[1:1 + ng], refs[1 + ng], refs[2 + ng]
        g_out, d_out, m_out, v_out = refs[3 + ng:]
        g = g_refs[0][...]
        for extra in g_refs[1:]:
            g = g + extra[...]
        m_new = ADAM_B1 * m_ref[...] + (1.0 - ADAM_B1) * g
        v_new = ADAM_B2 * v_ref[...] + (1.0 - ADAM_B2) * (g * g)
        g_out[...] = g
        m_out[...] = m_new
        v_out[...] = v_new
        d_out[...] = -ADAM_LR * ((m_new / c1) / (jnp.sqrt(v_new / c2) + ADAM_EPS) + ADAM_WD * w_ref[...])

    spec = pl.BlockSpec((tr, c), lambda i: (i, 0))
    return pl.pallas_call(
        body, name=name, grid=(r // tr,),
        in_specs=[spec] * (3 + ng), out_specs=[spec] * 4, out_shape=[jax.ShapeDtypeStruct((r, c), F32)] * 4,
        compiler_params=_params("parallel"),
    )(w, *grads, m, v)


CHIP_PEERS = ((1, 0), (0, 1), (1, 1))


def _place():
    return lax.axis_index("x"), lax.axis_index("y"), lax.axis_index("c")


HBM = pl.BlockSpec(memory_space=pltpu.HBM)
SEM = pl.BlockSpec(memory_space=pltpu.SEMAPHORE)
IN_FLIGHT = pltpu.SideEffectType.DATAFLOW_SIDE_EFFECTING


def _in_hbm(a):
    return pltpu.with_memory_space_constraint(a, pltpu.HBM)


def _cast_to_lands(shards, dtypes, *, name):
    n = len(shards)

    def body(*refs):
        ins, outs, bufs, sems = refs[:n], refs[n:2 * n], refs[2 * n:3 * n], refs[3 * n]
        x, y, _ = _place()
        copies = []
        for a in range(n):
            bufs[a][...] = ins[a][...].astype(dtypes[a])
            cp = pltpu.make_async_copy(bufs[a], outs[a].at[2 * x + y], sems.at[a])
            cp.start()
            copies.append(cp)
        for cp in copies:
            cp.wait()

    return pl.pallas_call(
        body, name=name, in_specs=[pl.BlockSpec(memory_space=pltpu.VMEM)] * n, out_specs=[ANY] * n,
        out_shape=[jax.ShapeDtypeStruct((N_CHIPS,) + s.shape, dt) for s, dt in zip(shards, dtypes)],
        scratch_shapes=[pltpu.VMEM(s.shape, dt) for s, dt in zip(shards, dtypes)] + [pltpu.SemaphoreType.DMA((n,))],
        compiler_params=pltpu.CompilerParams(vmem_limit_bytes=V7X_VMEM_LIMIT_BYTES),
    )(*shards)


def _chip_copy(src, dst, send, recv, flip, place):
    x, y, c = place
    return pltpu.make_async_remote_copy(src_ref=src, dst_ref=dst, send_sem=send, recv_sem=recv,
                                        device_id=(x ^ flip[0], y ^ flip[1], c), device_id_type=MESH)


def _my_part(land, place, halved):
    block = land.at[2 * place[0] + place[1]]
    if not halved:
        return block
    rows = land.shape[1] // 2
    return block.at[pl.ds(pl.multiple_of(place[2] * rows, rows), rows)]


def _gather_start(lands, after, *, name, halved=()):
    n = len(lands)

    def body(*refs):
        ins, send, recv, token = refs[:n], refs[n + 1], refs[n + 2], refs[-1]
        place = _place()
        for a in range(n):
            part = _my_part(ins[a], place, a in halved)
            for p, flip in enumerate(CHIP_PEERS):
                k = 3 * a + p
                _chip_copy(part, part, send.at[k], recv.at[k], flip, place).start()
        token[...] = jnp.zeros_like(token)

    outs = pl.pallas_call(
        body, name=name, in_specs=[HBM] * n + [ANY],
        out_specs=(SEM, SEM, *[HBM] * n, pl.BlockSpec(memory_space=pltpu.VMEM)),
        out_shape=(pltpu.SemaphoreType.DMA((3 * n,)), pltpu.SemaphoreType.DMA((3 * n,)),
                   *[pltpu.HBM(l.shape, l.dtype) for l in lands], jax.ShapeDtypeStruct((8, 128), F32)),
        input_output_aliases={a: 2 + a for a in range(n)},
        compiler_params=pltpu.CompilerParams(has_side_effects=IN_FLIGHT),
    )(*[_in_hbm(l) for l in lands], after)
    return outs[0], outs[1], list(outs[2:2 + n]), outs[-1]


def _gather_wait(send, recv, lands, after, *, name, halved=()):
    n = len(lands)

    def body(*refs):
        ins, send_ref, recv_ref = refs[:n], refs[n], refs[n + 1]
        place = _place()
        for a in range(n):
            part = _my_part(ins[a], place, a in halved)
            for p, flip in enumerate(CHIP_PEERS):
                k = 3 * a + p
                cp = _chip_copy(part, part, send_ref.at[k], recv_ref.at[k], flip, place)
                cp.wait_send()
                cp.wait_recv()

    return pl.pallas_call(
        body, name=name, in_specs=[HBM] * n + [SEM, SEM, ANY], out_specs=[HBM] * n,
        out_shape=[pltpu.HBM(l.shape, l.dtype) for l in lands],
        input_output_aliases={a: a for a in range(n)},
        compiler_params=pltpu.CompilerParams(has_side_effects=IN_FLIGHT),
    )(*lands, send, recv, after)


def _forward_to_sibling(land, *, name):
    rows = land.shape[1] // 2

    def body(land_ref, out_ref, send, recv):
        x, y, c = _place()
        copies = []
        for p, (fx, fy) in enumerate(CHIP_PEERS):
            chip = 2 * (x ^ fx) + (y ^ fy)
            mine = pl.ds(pl.multiple_of(c * rows, rows), rows)
            theirs = pl.ds(pl.multiple_of((1 - c) * rows, rows), rows)
            out = pltpu.make_async_remote_copy(
                src_ref=land_ref.at[chip].at[mine], dst_ref=out_ref.at[chip].at[mine], send_sem=send.at[p],
                recv_sem=recv.at[p], device_id=(x, y, 1 - c), device_id_type=MESH)
            out.start()
            copies.append((out, pltpu.make_async_remote_copy(
                src_ref=land_ref.at[chip].at[theirs], dst_ref=out_ref.at[chip].at[theirs], send_sem=send.at[p],
                recv_sem=recv.at[p], device_id=(x, y, 1 - c), device_id_type=MESH)))
        for out, arriving in copies:
            out.wait_send()
            arriving.wait_recv()

    return pl.pallas_call(
        body, name=name, in_specs=[ANY], out_specs=ANY, out_shape=jax.ShapeDtypeStruct(land.shape, land.dtype),
        input_output_aliases={0: 0},
        scratch_shapes=[pltpu.SemaphoreType.DMA((3,)), pltpu.SemaphoreType.DMA((3,))],
    )(land)


def _scatter_start(grad, *, name):
    def body(g_ref, land_ref, send, recv, g_thru, land_thru, token):
        place = _place()
        for p, flip in enumerate(CHIP_PEERS):
            peer_chip = 2 * (place[0] ^ flip[0]) + (place[1] ^ flip[1])
            _chip_copy(g_ref.at[peer_chip], land_ref.at[p], send.at[p], recv.at[p], flip, place).start()
        token[...] = jnp.zeros_like(token)

    land = lax.empty((3,) + grad.shape[1:], grad.dtype)
    return pl.pallas_call(
        body, name=name, in_specs=[HBM, HBM],
        out_specs=(SEM, SEM, HBM, HBM, pl.BlockSpec(memory_space=pltpu.VMEM)),
        out_shape=(pltpu.SemaphoreType.DMA((3,)), pltpu.SemaphoreType.DMA((3,)), pltpu.HBM(grad.shape, grad.dtype),
                   pltpu.HBM(land.shape, land.dtype), jax.ShapeDtypeStruct((8, 128), F32)),
        input_output_aliases={0: 2, 1: 3},
        compiler_params=pltpu.CompilerParams(has_side_effects=IN_FLIGHT),
    )(_in_hbm(grad), _in_hbm(land))


def _scatter_wait(started, after, *, name):
    n = len(started)

    def body(*refs):
        grads, lands = refs[:n], refs[n:2 * n]
        sends, recvs = refs[2 * n:3 * n], refs[3 * n:4 * n]
        place = _place()
        for a in range(n):
            for p, flip in enumerate(CHIP_PEERS):
                cp = _chip_copy(grads[a].at[0], lands[a].at[p], sends[a].at[p], recvs[a].at[p], flip, place)
                cp.wait_send()
                cp.wait_recv()

    grads, lands = [s[2] for s in started], [s[3] for s in started]
    after = list(after) if isinstance(after, (list, tuple)) else [after]
    outs = pl.pallas_call(
        body, name=name, in_specs=[HBM] * (2 * n) + [SEM] * (2 * n) + [ANY] * len(after), out_specs=[HBM] * (2 * n),
        out_shape=[pltpu.HBM(a.shape, a.dtype) for a in grads + lands],
        input_output_aliases={a: a for a in range(2 * n)},
        compiler_params=pltpu.CompilerParams(has_side_effects=IN_FLIGHT),
    )(*grads, *lands, *[s[0] for s in started], *[s[1] for s in started], *after)
    return list(zip(outs[:n], outs[n:]))


def _sibling_copy(src, dst, send, recv, place):
    x, y, c = place
    return pltpu.make_async_remote_copy(src_ref=src, dst_ref=dst, send_sem=send, recv_sem=recv,
                                        device_id=(x, y, 1 - c), device_id_type=MESH)


def _swap_start(arrays, *, name):
    n = len(arrays)

    def body(*refs):
        ins, lands, send, recv, token = refs[:n], refs[n:2 * n], refs[2 * n], refs[2 * n + 1], refs[-1]
        place = _place()
        for a in range(n):
            _sibling_copy(ins[a], lands[a], send.at[a], recv.at[a], place).start()
        token[...] = jnp.zeros_like(token)

    both = [_in_hbm(a) for a in arrays] + [_in_hbm(lax.empty(a.shape, a.dtype)) for a in arrays]
    outs = pl.pallas_call(
        body, name=name, in_specs=[HBM] * (2 * n),
        out_specs=(SEM, SEM, *[HBM] * (2 * n), pl.BlockSpec(memory_space=pltpu.VMEM)),
        out_shape=(pltpu.SemaphoreType.DMA((n,)), pltpu.SemaphoreType.DMA((n,)),
                   *[pltpu.HBM(a.shape, a.dtype) for a in both], jax.ShapeDtypeStruct((8, 128), F32)),
        input_output_aliases={a: 2 + a for a in range(2 * n)},
        compiler_params=pltpu.CompilerParams(has_side_effects=IN_FLIGHT),
    )(*both)
    return outs[0], outs[1], list(outs[2:2 + n]), list(outs[2 + n:2 + 2 * n]), outs[-1]


def _swap_wait(started, after, *, name):
    send, recv, arrays, lands = started[:4]
    n = len(arrays)

    def body(*refs):
        ins, zones, send_ref, recv_ref = refs[:n], refs[n:2 * n], refs[2 * n], refs[2 * n + 1]
        place = _place()
        for a in range(n):
            cp = _sibling_copy(ins[a], zones[a], send_ref.at[a], recv_ref.at[a], place)
            cp.wait_send()
            cp.wait_recv()

    after = list(after) if isinstance(after, (list, tuple)) else [after]
    outs = pl.pallas_call(
        body, name=name, in_specs=[HBM] * (2 * n) + [SEM, SEM] + [ANY] * len(after), out_specs=[HBM] * (2 * n),
        out_shape=[pltpu.HBM(a.shape, a.dtype) for a in arrays + lands],
        input_output_aliases={a: a for a in range(2 * n)},
        compiler_params=pltpu.CompilerParams(has_side_effects=IN_FLIGHT),
    )(*arrays, *lands, send, recv, *after)
    return list(outs[:n]), list(outs[n:])


def _allreduce_start(packed, *, name):
    n_dev = 8

    def body(src_ref, land_ref, send, recv, src_thru, land_thru, token):
        x, y, c = _place()
        me = 4 * x + 2 * y + c
        for p in range(1, n_dev):
            pltpu.make_async_remote_copy(
                src_ref=src_ref, dst_ref=land_ref.at[me], send_sem=send.at[p - 1], recv_sem=recv.at[p - 1],
                device_id=(x ^ (p >> 2), y ^ ((p >> 1) & 1), c ^ (p & 1)), device_id_type=MESH).start()
        token[...] = jnp.zeros_like(token)

    land = lax.empty((n_dev,) + packed.shape, packed.dtype)
    return pl.pallas_call(
        body, name=name, in_specs=[HBM, HBM],
        out_specs=(SEM, SEM, HBM, HBM, pl.BlockSpec(memory_space=pltpu.VMEM)),
        out_shape=(pltpu.SemaphoreType.DMA((n_dev - 1,)), pltpu.SemaphoreType.DMA((n_dev - 1,)),
                   pltpu.HBM(packed.shape, packed.dtype), pltpu.HBM(land.shape, land.dtype),
                   jax.ShapeDtypeStruct((8, 128), F32)),
        input_output_aliases={0: 2, 1: 3},
        compiler_params=pltpu.CompilerParams(has_side_effects=IN_FLIGHT),
    )(_in_hbm(packed), _in_hbm(land))


def _allreduce_wait(started, after, *, name):
    send, recv, packed, land = started[:4]
    n_dev = 8

    def body(src_ref, land_ref, send_ref, recv_ref, *_):
        x, y, c = _place()
        for p in range(1, n_dev):
            cp = pltpu.make_async_remote_copy(
                src_ref=src_ref, dst_ref=land_ref.at[0], send_sem=send_ref.at[p - 1], recv_sem=recv_ref.at[p - 1],
                device_id=(x ^ (p >> 2), y ^ ((p >> 1) & 1), c ^ (p & 1)), device_id_type=MESH)
            cp.wait_send()
            cp.wait_recv()

    after = list(after) if isinstance(after, (list, tuple)) else [after]
    return pl.pallas_call(
        body, name=name, in_specs=[HBM, HBM, SEM, SEM] + [ANY] * len(after), out_specs=[HBM, HBM],
        out_shape=[pltpu.HBM(packed.shape, packed.dtype), pltpu.HBM(land.shape, land.dtype)],
        input_output_aliases={0: 0, 1: 1},
        compiler_params=pltpu.CompilerParams(has_side_effects=IN_FLIGHT),
    )(packed, land, send, recv, *after)


def _sum_devices(mine, land, *, name):
    n_dev = land.shape[0]

    def body(mine_ref, land_ref, out_ref):
        x, y, c = _place()
        me = 4 * x + 2 * y + c
        total = None
        for s in range(n_dev):
            part = jnp.where(me == s, mine_ref[...], land_ref[s])
            total = part if total is None else total + part
        out_ref[...] = total

    return pl.pallas_call(body, name=name, out_shape=jax.ShapeDtypeStruct(mine.shape, mine.dtype))(mine, land)


def _sum_received(grad, land, *, name, tr=256):
    _, r, c = grad.shape
    tr = _pick(r, tr, 8)

    def body(chip_ref, g_ref, l_ref, o_ref):
        o_ref[...] = ((g_ref[...] + l_ref[0].astype(F32)) + l_ref[1].astype(F32)) + l_ref[2].astype(F32)

    chip = (2 * lax.axis_index("x") + lax.axis_index("y")).astype(jnp.int32).reshape(1)
    return pl.pallas_call(
        body, name=name,
        grid_spec=pltpu.PrefetchScalarGridSpec(
            num_scalar_prefetch=1, grid=(r // tr,),
            in_specs=[pl.BlockSpec((None, tr, c), lambda i, chip_ref: (chip_ref[0], i, 0)),
                      pl.BlockSpec((3, tr, c), lambda i, chip_ref: (0, i, 0))],
            out_specs=pl.BlockSpec((tr, c), lambda i, chip_ref: (i, 0))),
        out_shape=jax.ShapeDtypeStruct((r, c), F32), compiler_params=_params("parallel"),
    )(chip, grad, land)


def _allreduce_small(packed, *, name, after=None):
    r, d = packed.shape
    n_dev = 8

    def body(src_ref, out_ref, buf, send, recv):
        x, y, c = _place()
        me = 4 * x + 2 * y + c
        started = []
        for p in range(1, n_dev):
            rc = pltpu.make_async_remote_copy(
                src_ref=src_ref, dst_ref=buf.at[me], send_sem=send.at[p - 1], recv_sem=recv.at[p - 1],
                device_id=(x ^ (p >> 2), y ^ ((p >> 1) & 1), c ^ (p & 1)), device_id_type=MESH)
            rc.start()
            started.append(rc)
        buf[me] = src_ref[...]
        for rc in started:
            rc.wait()
        total = buf[0]
        for s in range(1, n_dev):
            total = total + buf[s]
        out_ref[...] = total

    vmem = pl.BlockSpec(memory_space=pltpu.VMEM)
    body, more_specs, more_args = _ordered(body, 1, after)
    return pl.pallas_call(
        body, name=name, in_specs=[vmem] + more_specs, out_specs=vmem, out_shape=jax.ShapeDtypeStruct((r, d), F32),
        scratch_shapes=[pltpu.VMEM((n_dev, r, d), F32), pltpu.SemaphoreType.DMA((n_dev - 1,)),
                        pltpu.SemaphoreType.DMA((n_dev - 1,))],
    )(packed, *more_args)


def _packed_rows(size, d):
    return -(-size // (8 * d)) * 8


def _pack_rows(arrays, d):
    rows = []
    for arr in arrays:
        flat = arr.reshape(-1).astype(F32)
        n = _packed_rows(flat.shape[0], d)
        rows.append(jnp.pad(flat, (0, n * d - flat.shape[0])).reshape(n, d))
    return jnp.concatenate(rows, axis=0)


def _unpack_rows(packed, shapes, d):
    out, row = [], 0
    for shape in shapes:
        size = math.prod(shape)
        n = _packed_rows(size, d)
        out.append(packed[row:row + n].reshape(-1)[:size].reshape(shape))
        row += n
    return out


SMALL = ("norm1_g", "gate_b", "conv_b", "conv_norm_g", "q_norm_g", "k_norm_g", "norm2_g", "ffn_conv_b")
LARGE = ("w_in", "w_conv_out", "w_attn_out", "w_out", "w_up", "w_down")
WEIGHTS = ("norm1_g", "w_in", "gate_b", "conv_w", "conv_b", "conv_norm_g", "w_conv_out", "q_norm_g", "k_norm_g",
           "w_attn_out", "w_out", "norm2_g", "w_up", "ffn_conv_w", "ffn_conv_b", "w_down")


def _head_ones(dims):
    a = dims.n_heads * dims.head_dim
    head = jnp.arange(a, dtype=jnp.int32) // dims.head_dim
    return (head[:, None] == head[None, :]).astype(BF16)


def _after(vec, token):
    return vec if token is None else vec + token[0:1, 0:1]


def _local_step(dims, x, target, small, first_weights, other_weights, send_grad):
    d, f, heads = dims.d_model, dims.d_ff, dims.n_heads
    small = dict(small)
    row = lambda name: small[name].reshape(1, -1)
    head_sum = _head_sum_matrix(dims)
    head_spread = jnp.transpose(head_sum)
    ones = (head_sum, head_spread)
    gq = jnp.tile(row("q_norm_g"), (1, heads))
    gk = jnp.tile(row("k_norm_g"), (1, heads))
    one_shard = lambda w: w.reshape(1, -1, w.shape[-1])

    h = _rmsnorm_fwd(x, row("norm1_g"), name="norm1")
    full = first_weights(h)
    w_in = full["w_in"]
    conv_w = jnp.pad(full["conv_w"], ((0, CONV_HALO - dims.conv_width), (0, 0)))
    ffn_w = jnp.pad(full["ffn_conv_w"], ((0, FFN_HALO - dims.ffn_conv_width), (0, 0)))
    z = _mm_nn(h, w_in, out_dtype=BF16, after=full.get("token"), tm=2048, tn=1792, name="in_proj")
    a1, a3 = _conv_branch_fwd(z, conv_w, row("conv_b"), row("conv_norm_g"), dims, name="conv_branch")
    qkv = _qkv_layouts_fwd(z, gq, gk, ones, dims, name="qk_norm")
    per_group = {dil: _attn_fwd(*qkv[dil], dims, dil, name=f"attn_fwd_d{dil}") for dil in DILATIONS}
    o, lse = _attn_combine(per_group, head_spread, dims, name="attn_combine")
    full = other_weights(o)
    w_up = full["w_up"]
    w_co, w_ao, w_o, w_dn = (one_shard(full[k]) for k in ("w_conv_out", "w_attn_out", "w_out", "w_down"))
    ya = _mm_nn(a3, w_co, out_dtype=F32, name="conv_out_proj")
    yb = _mm_nn(o, w_ao, out_dtype=F32, name="attn_out_proj")
    mixed = _mix_fwd(ya, yb, z, row("gate_b"), dims, name="gate_mix")
    x1 = _mm_nn(mixed, w_o, out_dtype=F32, residual=x, name="out_proj")
    h2 = _rmsnorm_fwd(x1, row("norm2_g"), name="norm2")
    up = _mm_nn(h2, w_up, out_dtype=F32, tm=2048, name="up_proj")
    act = _ffn_act_fwd(up, ffn_w, row("ffn_conv_b"), dims, name="ffn_act")
    x2 = _mm_nn(act, w_dn, out_dtype=F32, residual=x1, name="down_proj")
    dy, dy_b, loss = _loss_head(x2, target, name="loss_head")

    grads = {}

    def large(name, g):
        grads[name], g_bf16 = g
        return send_grad(name, g_bf16)

    sent = large("w_down", _mm_tn(act, dy_b, n_shards=1, name="dw_down"))
    dact = _mm_nt(dy_b, w_dn, out_dtype=BF16, after=sent, name="d_act")
    du, dfw, dfb = _ffn_act_bwd(dact, up, ffn_w, row("ffn_conv_b"), dims, name="ffn_act_bwd")
    grads["ffn_conv_w"], grads["ffn_conv_b"] = dfw[:dims.ffn_conv_width], dfb
    dup = _ffn_conv_bwd(du, ffn_w, dims, name="ffn_conv_bwd")
    sent = large("w_up", _mm_tn(h2, dup, n_shards=N_CHIPS, name="dw_up"))
    dh2 = _mm_nt(dup, w_up, out_dtype=F32, after=sent, name="d_h2")
    dx1, dx1_b, grads["norm2_g"] = _rmsnorm_bwd(x1, row("norm2_g"), dh2, dy, want_bf16=True, name="norm2_bwd")
    sent = large("w_out", _mm_tn(mixed, dx1_b, n_shards=1, name="dw_out"))
    dmix = _mm_nt(dx1_b, w_o, out_dtype=F32, after=sent, name="d_mix")
    dya, dyb, dz_gate, grads["gate_b"] = _mix_bwd(dmix, ya, yb, z, row("gate_b"), dims, name="gate_mix_bwd")
    sent = large("w_conv_out", _mm_tn(a3, dya, n_shards=1, name="dw_conv_out"))
    da3 = _mm_nt(dya, w_co, out_dtype=F32, after=sent, name="d_conv_act")
    da1, grads["conv_norm_g"] = _conv_norm_bwd(da3, a1, row("conv_norm_g"), name="conv_norm_bwd")
    dz_glu, dcw, grads["conv_b"] = _conv_branch_bwd(da1, z, conv_w, dims, name="conv_branch_bwd")
    grads["conv_w"] = dcw[:dims.conv_width]
    sent = large("w_attn_out", _mm_tn(o, dyb, n_shards=1, name="dw_attn_out"))
    do = _mm_nt(dyb, w_ao, out_dtype=BF16, after=sent, name="d_attn")
    dos, deltas = _attn_bwd_prep(do, o, head_sum, dims, name="attn_bwd_prep")
    dqkv = {dil: _attn_bwd(*qkv[dil], dos[dil], lse[dil], deltas[dil], dims, dil, name=f"attn_bwd_d{dil}")
            for dil in DILATIONS}
    dz_qkv, dgq, dgk = _qkv_layouts_bwd(z, dqkv, gq, gk, ones, dims, name="qk_norm_bwd")
    grads["q_norm_g"] = dgq.reshape(heads, dims.head_dim).sum(axis=0)
    grads["k_norm_g"] = dgk.reshape(heads, dims.head_dim).sum(axis=0)
    dz = jnp.concatenate([dz_glu, dz_qkv, dz_gate], axis=1)
    sent = large("w_in", _mm_tn(h, dz, n_shards=N_CHIPS, name="dw_in"))
    dh = _mm_nt(dz, w_in, out_dtype=F32, after=sent, name="d_h")
    dx, grads["norm1_g"] = _rmsnorm_bwd(x, row("norm1_g"), dh, dx1, want_bf16=False, name="norm1_bwd")
    return loss, dx, grads


def _step(dims, x, target, w, m, v):
    d = dims.d_model
    t = dims.tokens
    sq = lambda a: a.reshape(a.shape[1:])
    w2, m2, v2 = ({k: sq(a) for k, a in grp.items()} for grp in (w, m, v))

    conv_pad = jnp.pad(w2["conv_w"], ((0, CONV_HALO - dims.conv_width), (0, 0)))
    ffn_pad = jnp.pad(w2["ffn_conv_w"], ((0, FFN_HALO - dims.ffn_conv_width), (0, 0)))
    gathered_names = LARGE + ("conv_w", "ffn_conv_w")
    lands = dict(zip(gathered_names, _cast_to_lands([w2[k] for k in LARGE] + [conv_pad, ffn_pad],
                                                   [BF16] * len(LARGE) + [F32, F32], name="cast_weights")))
    first_names = ("w_in", "conv_w", "ffn_conv_w")
    other_names = tuple(k for k in gathered_names if k not in first_names)
    first = _gather_start([lands[k] for k in first_names], x, halved=(0,), name="gather_start_first")
    other = []
    cols = lambda g, rows: jnp.moveaxis(g, 0, 1).reshape(g.shape[1], -1)[:rows]

    def first_weights(after):
        got = dict(zip(first_names, _gather_wait(*first[:3], after, halved=(0,), name="gather_wait_first")))
        got["w_in"] = _forward_to_sibling(got["w_in"], name="forward_w_in")
        other.extend(_gather_start([lands[k] for k in other_names], got["w_in"], name="gather_start_other"))
        got["conv_w"] = cols(got["conv_w"], dims.conv_width)
        got["ffn_conv_w"] = cols(got["ffn_conv_w"], dims.ffn_conv_width)
        got["token"] = other[3]
        return got

    def other_weights(after):
        return dict(zip(other_names, _gather_wait(*other[:3], after, name="gather_wait_other")))

    started = {}

    def send_grad(name, g):
        send, recv, g_thru, land, token = _scatter_start(g.reshape(N_CHIPS, -1, g.shape[-1]), name=f"scatter_start_{name}")
        started[name] = (send, recv, g_thru, land)
        return token

    small = {k: w2[k] for k in SMALL}
    small["norm1_g"] = _after(small["norm1_g"].reshape(1, -1), first[3])
    loss, dx, grads = _local_step(dims, x.reshape(t, d), target.reshape(t, d), small, first_weights, other_weights, send_grad)

    def my_sums(names, after, tag):
        arrived = _scatter_wait([started[k] for k in names], after, name=f"scatter_wait_{tag}")
        blocks = [grads[k].reshape(N_CHIPS, -1, grads[k].shape[-1]) for k in names]
        return [_sum_received(g, land, name=f"sum_{k}") for k, g, (_, land) in zip(names, blocks, arrived)]

    def updates(names, mine, theirs):
        return {k: _adamw(w2[k], [a, b], m2[k], v2[k], name=f"adamw_{k}") for k, a, b in zip(names, mine, theirs)}

    small_names = SMALL + ("conv_w", "ffn_conv_w")
    packed = _pack_rows([grads[k] for k in small_names] + [loss[0, 0]], d)
    reducing = _allreduce_start(packed, name="allreduce_start")
    others = [k for k in LARGE if k != "w_in"]
    mine_others = my_sums(others, [dx, reducing[4]], "others")
    swapping_others = _swap_start(mine_others, name="swap_start_others")
    mine_w_in = my_sums(["w_in"], swapping_others[4], "w_in")
    swapping_w_in = _swap_start(mine_w_in, name="swap_start_w_in")
    out = updates(others, *_swap_wait(swapping_others, swapping_w_in[4], name="swap_wait_others"))
    last_updates = [out[k][1] for k in others]
    reduced = _sum_devices(*_allreduce_wait(reducing, last_updates, name="allreduce_wait"), name="allreduce_sum")
    shapes = [grads[k].shape for k in small_names] + [()]
    *small_g, loss_total = _unpack_rows(reduced, shapes, d)
    small_g = dict(zip(small_names, small_g))
    chip = 2 * lax.axis_index("x") + lax.axis_index("y")
    for k in ("conv_w", "ffn_conv_w"):
        width = w2[k].shape[1]
        small_g[k] = lax.dynamic_slice_in_dim(small_g[k], chip * width, width, axis=1)

    small_shapes = [w2[k].shape for k in small_names]
    pack = lambda grp: _pack_rows([grp[k] for k in small_names], d)
    results = _adamw(pack(w2), [pack(small_g)], pack(m2), pack(v2), name="adamw_small")
    unpacked = [_unpack_rows(r, small_shapes, d) for r in results]
    out.update({k: tuple(u[i] for u in unpacked) for i, k in enumerate(small_names)})
    out.update(updates(["w_in"], *_swap_wait(swapping_w_in, results[1], name="swap_wait_w_in")))

    lead =lambda a: a.reshape((1,) + a.shape)
    ordered = [[lead(out[k][j].reshape(w2[k].shape)) for k in WEIGHTS] for j in range(4)]
    return (loss_total, dx.reshape(x.shape), *ordered[0], *ordered[1], *ordered[2], *ordered[3])


def kernel(x, norm1_g, w_in, gate_b, conv_w, conv_b, conv_norm_g, w_conv_out, q_norm_g, k_norm_g, w_attn_out, w_out, norm2_g, w_up, ffn_conv_w, ffn_conv_b, w_down, loss_target, m_norm1_g, m_w_in, m_gate_b, m_conv_w, m_conv_b, m_conv_norm_g, m_w_conv_out, m_q_norm_g, m_k_norm_g, m_w_attn_out, m_w_out, m_norm2_g, m_w_up, m_ffn_conv_w, m_ffn_conv_b, m_w_down, v_norm1_g, v_w_in, v_gate_b, v_conv_w, v_conv_b, v_conv_norm_g, v_w_conv_out, v_q_norm_g, v_k_norm_g, v_w_attn_out, v_w_out, v_norm2_g, v_w_up, v_ffn_conv_w, v_ffn_conv_b, v_w_down):
    w = dict(zip(WEIGHTS, (norm1_g, w_in, gate_b, conv_w, conv_b, conv_norm_g, w_conv_out, q_norm_g, k_norm_g,
                           w_attn_out, w_out, norm2_g, w_up, ffn_conv_w, ffn_conv_b, w_down)))
    m = dict(zip(WEIGHTS, (m_norm1_g, m_w_in, m_gate_b, m_conv_w, m_conv_b, m_conv_norm_g, m_w_conv_out, m_q_norm_g,
                           m_k_norm_g, m_w_attn_out, m_w_out, m_norm2_g, m_w_up, m_ffn_conv_w, m_ffn_conv_b, m_w_down)))
    v = dict(zip(WEIGHTS, (v_norm1_g, v_w_in, v_gate_b, v_conv_w, v_conv_b, v_conv_norm_g, v_w_conv_out, v_q_norm_g,
                           v_k_norm_g, v_w_attn_out, v_w_out, v_norm2_g, v_w_up, v_ffn_conv_w, v_ffn_conv_b, v_w_down)))
    dims = Dims(d_model=x.shape[-1], batch_local=x.shape[0], seq=x.shape[1], d_ff=w_down.shape[1] * N_CHIPS)
    return _step(dims, x, loss_target, w, m, v)
```

```python
import functools
import math
from typing import NamedTuple

import jax
import jax.numpy as jnp
from jax import lax
from jax.experimental import pallas as pl
from jax.experimental.pallas import tpu as pltpu

F32 = jnp.float32
BF16 = jnp.bfloat16

RMS_EPS = 1e-6
MASKED_SCORE = -1e30
ATTN_BLOCK = 128
DILATIONS = (1, 4, 16)
CONV_HALO = 32
FFN_HALO = 8
ADAM_LR, ADAM_B1, ADAM_B2, ADAM_EPS, ADAM_WD, ADAM_STEP = 0.001, 0.9, 0.999, 1e-08, 0.01, 10
V7X_VMEM_LIMIT_BYTES = 56 * 2 ** 20
N_CHIPS = 4
MESH = pl.DeviceIdType.MESH


class Dims(NamedTuple):
    d_model: int = 1024
    n_heads: int = 16
    head_dim: int = 64
    d_ff: int = 2816
    seq: int = 2048
    batch_local: int = 2
    conv_width: int = 31
    ffn_conv_width: int = 3

    @property
    def tokens(self):
        return self.seq * self.batch_local


def _params(*semantics):
    return pltpu.CompilerParams(dimension_semantics=semantics, vmem_limit_bytes=V7X_VMEM_LIMIT_BYTES)


ANY = pl.BlockSpec(memory_space=pl.ANY)


def _ordered(body, n_inputs, after):
    after = [] if after is None else list(after) if isinstance(after, (list, tuple)) else [after]
    if not after:
        return body, [], []

    def wrapped(*refs):
        return body(*refs[:n_inputs], *refs[n_inputs + len(after):])

    return wrapped, [ANY] * len(after), after


def _pick(n, target, mult=128):
    if n <= target:
        return n
    best = None
    for t in range(mult, target + 1, mult):
        if n % t == 0:
            best = t
    assert best is not None, (n, target, mult)
    return best


def _sigmoid(v):
    return 1.0 / (1.0 + jnp.exp(-v))


def _mm_nn(a, w, *, out_dtype, name, residual=None, after=None, tm=1024, tn=1408, tk=2816):
    m, k = a.shape
    nsh, k2, c = w.shape
    assert k == k2 and a.dtype == BF16 and w.dtype == BF16
    n = nsh * c
    tm, tn, tk = _pick(m, tm, 8), _pick(c, tn), _pick(k, tk)
    nk, cpn = k // tk, c // tn

    def body(*refs):
        if residual is None:
            a_ref, w_ref, o_ref, acc = refs
        else:
            a_ref, w_ref, r_ref, o_ref, acc = refs
        prod = jnp.dot(a_ref[...], w_ref[...], preferred_element_type=F32)

        def finish(total):
            if residual is not None:
                total = total + r_ref[...]
            o_ref[...] = total.astype(out_dtype)

        if nk == 1:
            finish(prod)
        else:
            kk = pl.program_id(2)

            @pl.when(kk == 0)
            def _():
                acc[...] = prod

            @pl.when(kk > 0)
            def _():
                acc[...] += prod

            @pl.when(kk == nk - 1)
            def _():
                finish(acc[...])

    in_specs = [pl.BlockSpec((tm, tk), lambda i, j, kk: (i, kk)),
                pl.BlockSpec((None, tk, tn), lambda i, j, kk: (j // cpn, kk, j % cpn))]
    args = [a, w]
    if residual is not None:
        in_specs.append(pl.BlockSpec((tm, tn), lambda i, j, kk: (i, j)))
        args.append(residual)
    body, more_specs, more_args = _ordered(body, len(args), after)
    return pl.pallas_call(
        body, name=name, grid=(m // tm, n // tn, nk),
        in_specs=in_specs + more_specs, out_specs=pl.BlockSpec((tm, tn), lambda i, j, kk: (i, j)),
        out_shape=jax.ShapeDtypeStruct((m, n), out_dtype),
        scratch_shapes=[pltpu.VMEM((tm, tn) if nk > 1 else (8, 128), F32)],
        compiler_params=_params("parallel", "parallel", "arbitrary"),
    )(*args, *more_args)


def _proj_residual_norm(a, w, residual, g, *, name, tm=1024):
    m, k = a.shape
    _, k2, n = w.shape
    assert w.shape[0] == 1 and k == k2 and a.dtype == BF16 and w.dtype == BF16
    tm = _pick(m, tm, 8)

    def body(a_ref, w_ref, r_ref, g_ref, y_ref, h_ref):
        y = r_ref[...] + jnp.dot(a_ref[...], w_ref[...], preferred_element_type=F32)
        y_ref[...] = y
        h_ref[...] = (y * lax.rsqrt(jnp.mean(y * y, axis=-1, keepdims=True) + RMS_EPS) * g_ref[...]).astype(BF16)

    rows = lambda width: pl.BlockSpec((tm, width), lambda i: (i, 0))
    return pl.pallas_call(
        body, name=name, grid=(m // tm,),
        in_specs=[rows(k), pl.BlockSpec((None, k, n), lambda i: (0, 0, 0)), rows(n), pl.BlockSpec((1, n), lambda i: (0, 0))],
        out_specs=[rows(n), rows(n)],
        out_shape=[jax.ShapeDtypeStruct((m, n), F32), jax.ShapeDtypeStruct((m, n), BF16)],
        compiler_params=_params("parallel"),
    )(a, w, residual, g)


def _proj_residual_loss(a, w, residual, target, *, name, tm=1024):
    m, k = a.shape
    _, k2, n = w.shape
    assert w.shape[0] == 1 and k == k2 and a.dtype == BF16 and w.dtype == BF16
    tm = _pick(m, tm, 8)

    def body(a_ref, w_ref, r_ref, t_ref, dy_ref, dyb_ref, loss_ref):
        err = r_ref[...] + jnp.dot(a_ref[...], w_ref[...], preferred_element_type=F32) - t_ref[...]
        dy = err * (1.0 / n)
        dy_ref[...] = dy
        dyb_ref[...] = dy.astype(BF16)
        part = jnp.sum(jnp.sum(err * err, axis=-1, keepdims=True), axis=0, keepdims=True) * (0.5 / n)
        _accumulate(loss_ref, jnp.broadcast_to(part, (8, 128)), pl.program_id(0) == 0)

    rows = lambda width: pl.BlockSpec((tm, width), lambda i: (i, 0))
    return pl.pallas_call(
        body, name=name, grid=(m // tm,),
        in_specs=[rows(k), pl.BlockSpec((None, k, n), lambda i: (0, 0, 0)), rows(n), rows(n)],
        out_specs=[rows(n), rows(n), pl.BlockSpec((8, 128), lambda i: (0, 0))],
        out_shape=[jax.ShapeDtypeStruct((m, n), F32), jax.ShapeDtypeStruct((m, n), BF16),
                   jax.ShapeDtypeStruct((8, 128), F32)],
        compiler_params=_params("arbitrary"),
    )(a, w, residual, target)


def _mm_nt(a, w, *, out_dtype, name, after=None, tm=1024, tn=1408, tk=1792):
    m, k = a.shape
    nsh, r, c = w.shape
    assert k == nsh * c and a.dtype == BF16 and w.dtype == BF16
    tm, tn, tk = _pick(m, tm, 8), _pick(r, tn), _pick(c, tk)
    nk, cpk = k // tk, c // tk

    def body(a_ref, w_ref, o_ref, acc):
        prod = lax.dot_general(a_ref[...], w_ref[...], (((1,), (1,)), ((), ())), preferred_element_type=F32)
        if nk == 1:
            o_ref[...] = prod.astype(out_dtype)
        else:
            kk = pl.program_id(2)

            @pl.when(kk == 0)
            def _():
                acc[...] = prod

            @pl.when(kk > 0)
            def _():
                acc[...] += prod

            @pl.when(kk == nk - 1)
            def _():
                o_ref[...] = acc[...].astype(out_dtype)

    body, more_specs, more_args = _ordered(body, 2, after)
    return pl.pallas_call(
        body, name=name, grid=(m // tm, r // tn, nk),
        in_specs=[pl.BlockSpec((tm, tk), lambda i, j, kk: (i, kk)),
                  pl.BlockSpec((None, tn, tk), lambda i, j, kk: (kk // cpk, j, kk % cpk))] + more_specs,
        out_specs=pl.BlockSpec((tm, tn), lambda i, j, kk: (i, j)),
        out_shape=jax.ShapeDtypeStruct((m, r), out_dtype),
        scratch_shapes=[pltpu.VMEM((tm, tn) if nk > 1 else (8, 128), F32)],
        compiler_params=_params("parallel", "parallel", "arbitrary"),
    )(a, w, *more_args)


MM_TN_VMEM_BYTES = 44 * 2 ** 20


def _mm_tn(a, b, *, n_shards, name, tm=1408, tn=1408):
    t, m = a.shape
    t2, n = b.shape
    assert t == t2 and a.dtype == BF16 and b.dtype == BF16
    c = n // n_shards
    tm, tn = _pick(m, tm), _pick(c, tn)
    fixed = 2 * tm * tn * 6
    if 4 * t * (tm + tn) + fixed <= MM_TN_VMEM_BYTES:
        tk = t
    else:
        tk = _pick(t, (MM_TN_VMEM_BYTES - fixed - 4 * tm * tn) // (4 * (tm + tn)), 8)
    nk, cpn = t // tk, c // tn

    def body(a_ref, b_ref, o_ref, ob_ref, acc):
        kk = pl.program_id(2)
        prod = lax.dot_general(a_ref[...], b_ref[...], (((0,), (0,)), ((), ())), preferred_element_type=F32)

        def finish(total):
            o_ref[...] = total
            ob_ref[...] = total.astype(BF16)

        if nk == 1:
            finish(prod)
        else:
            @pl.when(kk == 0)
            def _():
                acc[...] = prod

            @pl.when(kk > 0)
            def _():
                acc[...] += prod

            @pl.when(kk == nk - 1)
            def _():
                finish(acc[...])

    out_spec = pl.BlockSpec((None, tm, tn), lambda i, j, kk: (j // cpn, i, j % cpn))
    return pl.pallas_call(
        body, name=name, grid=(m // tm, n // tn, nk),
        in_specs=[pl.BlockSpec((tk, tm), lambda i, j, kk: (kk, i)),
                  pl.BlockSpec((tk, tn), lambda i, j, kk: (kk, j))],
        out_specs=[out_spec, out_spec],
        out_shape=[jax.ShapeDtypeStruct((n_shards, m, c), F32), jax.ShapeDtypeStruct((n_shards, m, c), BF16)],
        scratch_shapes=[pltpu.VMEM((tm, tn) if nk > 1 else (8, 128), F32)],
        compiler_params=_params("parallel", "parallel", "arbitrary"),
    )(a, b)


def _row_spec(tr, width, col=0):
    return pl.BlockSpec((tr, width), lambda i, col=col: (i, col))


def _vec_spec(width, col=0):
    return pl.BlockSpec((1, width), lambda i, col=col: (0, col))


def _accumulate(ref, value, first):
    @pl.when(first)
    def _():
        ref[...] = value

    @pl.when(jnp.logical_not(first))
    def _():
        ref[...] += value


def _rmsnorm_fwd(x, g, *, name, tr=512):
    t, d = x.shape
    tr = _pick(t, tr, 8)

    def body(x_ref, g_ref, o_ref):
        xv = x_ref[...]
        r = lax.rsqrt(jnp.mean(xv * xv, axis=-1, keepdims=True) + RMS_EPS)
        o_ref[...] = (xv * r * g_ref[...]).astype(BF16)

    return pl.pallas_call(
        body, name=name, grid=(t // tr,),
        in_specs=[_row_spec(tr, d), _vec_spec(d)], out_specs=_row_spec(tr, d),
        out_shape=jax.ShapeDtypeStruct((t, d), BF16), compiler_params=_params("parallel"),
    )(x, g)


def _rmsnorm_bwd(x, g, dy, dres, *, name, want_bf16, tr=512):
    t, d = x.shape
    tr = _pick(t, tr, 8)

    def body(x_ref, g_ref, dy_ref, dres_ref, *outs):
        dx_ref, dg_ref = outs[0], outs[-1]
        xv, dyv = x_ref[...], dy_ref[...].astype(F32)
        r = lax.rsqrt(jnp.mean(xv * xv, axis=-1, keepdims=True) + RMS_EPS)
        gy = dyv * g_ref[...]
        dx = dres_ref[...] + r * gy - xv * (r * r * r) * jnp.mean(xv * gy, axis=-1, keepdims=True)
        dx_ref[...] = dx
        if want_bf16:
            outs[1][...] = dx.astype(BF16)
        _accumulate(dg_ref, jnp.sum(dyv * xv * r, axis=0, keepdims=True), pl.program_id(0) == 0)

    out_shape = [jax.ShapeDtypeStruct((t, d), F32)]
    out_specs = [_row_spec(tr, d)]
    if want_bf16:
        out_shape.append(jax.ShapeDtypeStruct((t, d), BF16))
        out_specs.append(_row_spec(tr, d))
    out_shape.append(jax.ShapeDtypeStruct((1, d), F32))
    out_specs.append(_vec_spec(d))
    return pl.pallas_call(
        body, name=name, grid=(t // tr,),
        in_specs=[_row_spec(tr, d), _vec_spec(d), _row_spec(tr, d), _row_spec(tr, d)],
        out_specs=out_specs, out_shape=out_shape, compiler_params=_params("arbitrary"),
    )(x, g, dy, dres)


def _head_mean(v, ones_ref, head_dim):
    hi = v.astype(BF16)
    lo = (v - hi.astype(F32)).astype(BF16)
    e = ones_ref[...]
    total = jnp.dot(hi, e, preferred_element_type=F32) + jnp.dot(lo, e, preferred_element_type=F32)
    return total * (1.0 / head_dim)


def _qkv_fwd(z, gq, gk, head_ones, dims, *, name, tr=256):
    t = z.shape[0]
    a = dims.n_heads * dims.head_dim
    tr = _pick(t, tr, 8)
    q_scale = dims.head_dim ** -0.5

    def body(q_ref, k_ref, v_ref, gq_ref, gk_ref, e_ref, qo_ref, ko_ref, vo_ref):
        qv, kv = q_ref[...], k_ref[...]
        rq = lax.rsqrt(_head_mean(qv * qv, e_ref, dims.head_dim) + RMS_EPS)
        rk = lax.rsqrt(_head_mean(kv * kv, e_ref, dims.head_dim) + RMS_EPS)
        qo_ref[...] = (qv * rq * gq_ref[...] * q_scale).astype(BF16)
        ko_ref[...] = (kv * rk * gk_ref[...]).astype(BF16)
        vo_ref[...] = v_ref[...].astype(BF16)

    return pl.pallas_call(
        body, name=name, grid=(t // tr,),
        in_specs=[_row_spec(tr, a, 2), _row_spec(tr, a, 3), _row_spec(tr, a, 4), _vec_spec(a), _vec_spec(a),
                  pl.BlockSpec((a, a), lambda i: (0, 0))],
        out_specs=[_row_spec(tr, a)] * 3, out_shape=[jax.ShapeDtypeStruct((t, a), BF16)] * 3,
        compiler_params=_params("parallel"),
    )(z, z, z, gq, gk, head_ones)


def _qkv_bwd(z, dqs, dks, dvs, gq, gk, head_ones, dims, *, name, tr=256):
    t = z.shape[0]
    a = dims.n_heads * dims.head_dim
    tr = _pick(t, tr, 8)
    q_scale = dims.head_dim ** -0.5
    ng = len(dqs)

    def body(*refs):
        q_ref, k_ref = refs[:2]
        dq_refs, dk_refs, dv_refs = refs[2:2 + ng], refs[2 + ng:2 + 2 * ng], refs[2 + 2 * ng:2 + 3 * ng]
        gq_ref, gk_ref, e_ref = refs[2 + 3 * ng:5 + 3 * ng]
        dz_ref, dgq_ref, dgk_ref = refs[5 + 3 * ng:]
        first = pl.program_id(0) == 0

        def norm_bwd(x_ref, d_refs, g_ref, scale, col, dg_ref):
            xv = x_ref[...]
            dy = sum(r[...] for r in d_refs) * scale
            r = lax.rsqrt(_head_mean(xv * xv, e_ref, dims.head_dim) + RMS_EPS)
            gy = dy * g_ref[...]
            dx = r * gy - xv * (r * r * r) * _head_mean(xv * gy, e_ref, dims.head_dim)
            dz_ref[:, col * a:(col + 1) * a] = dx.astype(BF16)
            _accumulate(dg_ref, jnp.sum(dy * xv * r, axis=0, keepdims=True), first)

        norm_bwd(q_ref, dq_refs, gq_ref, q_scale, 0, dgq_ref)
        norm_bwd(k_ref, dk_refs, gk_ref, 1.0, 1, dgk_ref)
        dz_ref[:, 2 * a:3 * a] = sum(r[...] for r in dv_refs).astype(BF16)

    in_specs = ([_row_spec(tr, a, 2), _row_spec(tr, a, 3)] + [_row_spec(tr, a)] * (3 * ng)
                + [_vec_spec(a), _vec_spec(a), pl.BlockSpec((a, a), lambda i: (0, 0))])
    return pl.pallas_call(
        body, name=name, grid=(t // tr,), in_specs=in_specs,
        out_specs=[_row_spec(tr, 3 * a), _vec_spec(a), _vec_spec(a)],
        out_shape=[jax.ShapeDtypeStruct((t, 3 * a), BF16)] + [jax.ShapeDtypeStruct((1, a), F32)] * 2,
        compiler_params=_params("arbitrary"),
    )(z, z, *dqs, *dks, *dvs, gq, gk, head_ones)


CONV_ROWS = 16


def _seq_specs(dims, ts, width, halo, col, *, nxt=False):
    nst, per = dims.seq // ts, ts // halo
    last = dims.tokens // halo - 1
    cur = pl.BlockSpec((ts, width), lambda b, i: (b * nst + i, col))
    if nxt:
        edge = pl.BlockSpec((halo, width), lambda b, i: (jnp.minimum((b * nst + i + 1) * per, last), col))
    else:
        edge = pl.BlockSpec((halo, width), lambda b, i: (jnp.maximum((b * nst + i) * per - 1, 0), col))
    return cur, edge


SUBLANES = 8


def _shifted_copies(buf, shifted):
    rows = shifted.shape[1]
    for s in range(1, SUBLANES):
        shifted[s - 1] = buf[pl.ds(s, rows), :]


def _window(buf, shifted, start, size):
    a, s = divmod(start, SUBLANES)
    src = buf if s == 0 else shifted.at[s - 1]
    return src[pl.ds(SUBLANES * a, size), :]


def _conv_branch_fwd(z, w, b, g, dims, *, name, ts=128):
    t, c, kw = z.shape[0], dims.d_model, dims.conv_width
    base = CONV_HALO - (kw - 1)

    def body(av_ref, hv_ref, ag_ref, hg_ref, w_ref, b_ref, g_ref, a1_ref, a3_ref, buf, shifted):
        i = pl.program_id(1)
        buf[CONV_HALO:, :] = av_ref[...].astype(F32) * _sigmoid(ag_ref[...].astype(F32))
        buf[0:CONV_HALO, :] = jnp.where(i > 0, hv_ref[...].astype(F32) * _sigmoid(hg_ref[...].astype(F32)), 0.0)
        _shifted_copies(buf, shifted)
        for r0 in range(0, ts, CONV_ROWS):
            acc = jnp.broadcast_to(b_ref[...], (CONV_ROWS, c))
            for k in range(kw):
                acc = acc + w_ref[k:k + 1, :] * _window(buf, shifted, r0 + base + k, CONV_ROWS)
            a1_ref[r0:r0 + CONV_ROWS, :] = acc
            a2 = acc * lax.rsqrt(jnp.mean(acc * acc, axis=-1, keepdims=True) + RMS_EPS) * g_ref[...]
            a3_ref[r0:r0 + CONV_ROWS, :] = (a2 * _sigmoid(a2)).astype(BF16)

    vec = pl.BlockSpec((1, c), lambda b, i: (0, 0))
    out = pl.BlockSpec((ts, c), lambda b, i: (b * (dims.seq // ts) + i, 0))
    return pl.pallas_call(
        body, name=name, grid=(dims.batch_local, dims.seq // ts),
        in_specs=[*_seq_specs(dims, ts, c, CONV_HALO, 0), *_seq_specs(dims, ts, c, CONV_HALO, 1),
                  pl.BlockSpec((CONV_HALO, c), lambda b, i: (0, 0)), vec, vec],
        out_specs=[out, out],
        out_shape=[jax.ShapeDtypeStruct((t, c), F32), jax.ShapeDtypeStruct((t, c), BF16)],
        scratch_shapes=[pltpu.VMEM((CONV_HALO + ts, c), F32),
                        pltpu.VMEM((SUBLANES - 1, CONV_HALO + ts - SUBLANES, c), F32)],
        compiler_params=_params("parallel", "parallel"),
    )(z, z, z, z, w, b, g)


def _conv_norm_bwd(da3, a1, g, *, name, tr=256):
    t, c = a1.shape
    tr = _pick(t, tr, 8)

    def body(d_ref, a_ref, g_ref, o_ref, dg_ref):
        a1v, gv = a_ref[...], g_ref[...]
        r = lax.rsqrt(jnp.mean(a1v * a1v, axis=-1, keepdims=True) + RMS_EPS)
        a2 = a1v * r * gv
        sg = _sigmoid(a2)
        da2 = d_ref[...].astype(F32) * sg * (1.0 + a2 * (1.0 - sg))
        gy = da2 * gv
        o_ref[...] = r * gy - a1v * (r * r * r) * jnp.mean(a1v * gy, axis=-1, keepdims=True)
        _accumulate(dg_ref, jnp.sum(da2 * a1v * r, axis=0, keepdims=True), pl.program_id(0) == 0)

    return pl.pallas_call(
        body, name=name, grid=(t // tr,),
        in_specs=[_row_spec(tr, c), _row_spec(tr, c), _vec_spec(c)],
        out_specs=[_row_spec(tr, c), _vec_spec(c)],
        out_shape=[jax.ShapeDtypeStruct((t, c), F32), jax.ShapeDtypeStruct((1, c), F32)],
        compiler_params=_params("arbitrary"),
    )(da3, a1, g)


def _conv_branch_bwd(da1, z, w, dims, *, name, ts=128):
    t, c, kw = z.shape[0], dims.d_model, dims.conv_width
    nst = dims.seq // ts
    base = CONV_HALO - (kw - 1)

    def body(d_ref, dn_ref, av_ref, hv_ref, ag_ref, hg_ref, w_ref, dz_ref, dw_ref, db_ref, abuf, dbuf, ashift, dshift):
        i = pl.program_id(1)
        first = jnp.logical_and(pl.program_id(0) == 0, i == 0)
        abuf[CONV_HALO:, :] = av_ref[...].astype(F32) * _sigmoid(ag_ref[...].astype(F32))
        abuf[0:CONV_HALO, :] = jnp.where(i > 0, hv_ref[...].astype(F32) * _sigmoid(hg_ref[...].astype(F32)), 0.0)
        d1 = d_ref[...]
        dbuf[0:ts, :] = d1
        dbuf[ts:, :] = jnp.where(i < nst - 1, dn_ref[...], 0.0)
        _shifted_copies(abuf, ashift)
        _shifted_copies(dbuf, dshift)

        @pl.when(first)
        def _():
            dw_ref[...] = jnp.zeros_like(dw_ref)
            db_ref[...] = jnp.zeros_like(db_ref)

        db_ref[...] += jnp.sum(d1, axis=0, keepdims=True)
        for k in range(kw):
            dw_ref[k:k + 1, :] += jnp.sum(d1 * _window(abuf, ashift, base + k, ts), axis=0, keepdims=True)
        for r0 in range(0, ts, CONV_ROWS):
            acc = jnp.zeros((CONV_ROWS, c), F32)
            for k in range(kw):
                acc = acc + w_ref[k:k + 1, :] * _window(dbuf, dshift, r0 + (kw - 1) - k, CONV_ROWS)
            av = av_ref[r0:r0 + CONV_ROWS, :].astype(F32)
            sg = _sigmoid(ag_ref[r0:r0 + CONV_ROWS, :].astype(F32))
            dz_ref[r0:r0 + CONV_ROWS, 0:c] = (acc * sg).astype(BF16)
            dz_ref[r0:r0 + CONV_ROWS, c:2 * c] = (acc * av * sg * (1.0 - sg)).astype(BF16)

    cur, nxt = _seq_specs(dims, ts, c, CONV_HALO, 0, nxt=True)
    return pl.pallas_call(
        body, name=name, grid=(dims.batch_local, nst),
        in_specs=[cur, nxt, *_seq_specs(dims, ts, c, CONV_HALO, 0), *_seq_specs(dims, ts, c, CONV_HALO, 1),
                  pl.BlockSpec((CONV_HALO, c), lambda b, i: (0, 0))],
        out_specs=[pl.BlockSpec((ts, 2 * c), lambda b, i: (b * nst + i, 0)),
                   pl.BlockSpec((CONV_HALO, c), lambda b, i: (0, 0)), pl.BlockSpec((1, c), lambda b, i: (0, 0))],
        out_shape=[jax.ShapeDtypeStruct((t, 2 * c), BF16), jax.ShapeDtypeStruct((CONV_HALO, c), F32),
                   jax.ShapeDtypeStruct((1, c), F32)],
        scratch_shapes=[pltpu.VMEM((CONV_HALO + ts, c), F32)] * 2
        + [pltpu.VMEM((SUBLANES - 1, CONV_HALO + ts - SUBLANES, c), F32)] * 2,
        compiler_params=_params("arbitrary", "arbitrary"),
    )(da1, da1, z, z, z, z, w)


FFN_ROWS = 16
FFN_COLS = 256


def _ffn_chunks(ts, f):
    cw = _pick(f, FFN_COLS)
    return [(r0, c0, cw) for r0 in range(0, ts, FFN_ROWS) for c0 in range(0, f, cw)]


def _tap_sources(buf, moved, offsets, rows):
    taps, used = [], 0
    for off in offsets:
        if off % SUBLANES:
            moved[used] = buf[pl.ds(off, rows), :]
            taps.append((moved.at[used], 0))
            used += 1
        else:
            taps.append((buf, off))
    return taps


def _moved_copies(offsets):
    return sum(1 for off in offsets if off % SUBLANES)


def _taps_sum(taps, w_ref, init, r0, cols):
    for k, (src, off) in enumerate(taps):
        init = init + w_ref[k:k + 1, cols] * src[pl.ds(off + r0, init.shape[0]), cols]
    return init


def _ffn_bwd(dact, up, w, b, dims, *, name, ts=128):
    t, f, kw = up.shape[0], dims.d_ff, dims.ffn_conv_width
    nst = dims.seq // ts
    fwd_offsets = [FFN_HALO - (kw - 1) + k for k in range(kw)]
    bwd_offsets = [(kw - 1) - k for k in range(kw)]
    dact_halo = 2 * FFN_HALO

    def body(d_ref, dn_ref, up_ref, hp_ref, hn_ref, w_ref, b_ref, o_ref, dw_ref, db_ref, buf, moved, dbuf, dmoved):
        i = pl.program_id(1)
        first = jnp.logical_and(pl.program_id(0) == 0, i == 0)
        more = i < nst - 1
        buf[0:FFN_HALO, :] = jnp.where(i > 0, hp_ref[...], 0.0)
        buf[FFN_HALO:FFN_HALO + ts, :] = up_ref[...]
        buf[FFN_HALO + ts:, :] = hn_ref[...]
        taps = _tap_sources(buf, moved, fwd_offsets, ts + FFN_HALO)

        def du_chunk(r0, rows, c0, cw, d):
            vcols, gcols = slice(c0, c0 + cw), slice(f + c0, f + c0 + cw)
            uv = _taps_sum(taps, w_ref, jnp.broadcast_to(b_ref[:, vcols], (rows, cw)), r0, vcols)
            ug = _taps_sum(taps, w_ref, jnp.broadcast_to(b_ref[:, gcols], (rows, cw)), r0, gcols)
            sg = _sigmoid(ug)
            dbuf[r0:r0 + rows, vcols] = d * ug * sg
            dbuf[r0:r0 + rows, gcols] = d * uv * sg * (1.0 + ug * (1.0 - sg))

        for r0, c0, cw in _ffn_chunks(ts, f):
            du_chunk(r0, FFN_ROWS, c0, cw, d_ref[r0:r0 + FFN_ROWS, c0:c0 + cw].astype(F32))
        for _, c0, cw in _ffn_chunks(FFN_ROWS, f):
            d_next = dn_ref[:, c0:c0 + cw].astype(F32)[0:FFN_HALO]
            du_chunk(ts, FFN_HALO, c0, cw, jnp.where(more, d_next, 0.0))

        @pl.when(first)
        def _():
            dw_ref[...] = jnp.zeros_like(dw_ref)
            db_ref[...] = jnp.zeros_like(db_ref)

        du = dbuf[0:ts, :]
        db_ref[...] += jnp.sum(du, axis=0, keepdims=True)
        for k, (src, off) in enumerate(taps):
            dw_ref[k:k + 1, :] += jnp.sum(du * src[pl.ds(off, ts), :], axis=0, keepdims=True)

        dtaps = _tap_sources(dbuf, dmoved, bwd_offsets, ts)
        for r0, c0, cw in _ffn_chunks(ts, 2 * f):
            cols = slice(c0, c0 + cw)
            o_ref[r0:r0 + FFN_ROWS, cols] = _taps_sum(dtaps, w_ref, jnp.zeros((FFN_ROWS, cw), F32), r0, cols).astype(BF16)

    up_cur, up_prev = _seq_specs(dims, ts, 2 * f, FFN_HALO, 0)
    _, up_next = _seq_specs(dims, ts, 2 * f, FFN_HALO, 0, nxt=True)
    d_cur, d_next = _seq_specs(dims, ts, f, dact_halo, 0, nxt=True)
    full = lambda rows: pl.BlockSpec((rows, 2 * f), lambda b_, i: (0, 0))
    return pl.pallas_call(
        body, name=name, grid=(dims.batch_local, nst),
        in_specs=[d_cur, d_next, up_cur, up_prev, up_next, full(FFN_HALO), full(1)],
        out_specs=[pl.BlockSpec((ts, 2 * f), lambda b_, i: (b_ * nst + i, 0)), full(FFN_HALO), full(1)],
        out_shape=[jax.ShapeDtypeStruct((t, 2 * f), BF16), jax.ShapeDtypeStruct((FFN_HALO, 2 * f), F32),
                   jax.ShapeDtypeStruct((1, 2 * f), F32)],
        scratch_shapes=[pltpu.VMEM((ts + 2 * FFN_HALO, 2 * f), F32),
                        pltpu.VMEM((_moved_copies(fwd_offsets), ts + FFN_HALO, 2 * f), F32),
                        pltpu.VMEM((ts + FFN_HALO, 2 * f), F32),
                        pltpu.VMEM((_moved_copies(bwd_offsets), ts, 2 * f), F32)],
        compiler_params=_params("arbitrary", "arbitrary"),
    )(dact, dact, up, up, up, w, b)


def _ffn_act_fwd(up, w, b, dims, *, name, ts=128):
    t, f, kw = up.shape[0], dims.d_ff, dims.ffn_conv_width
    offsets = [FFN_HALO - (kw - 1) + k for k in range(kw)]

    def body(up_ref, h_ref, w_ref, b_ref, o_ref, buf, moved):
        buf[FFN_HALO:, :] = up_ref[...]
        buf[0:FFN_HALO, :] = jnp.where(pl.program_id(1) > 0, h_ref[...], 0.0)
        taps = _tap_sources(buf, moved, offsets, ts)
        for r0, c0, cw in _ffn_chunks(ts, f):
            vcols, gcols = slice(c0, c0 + cw), slice(f + c0, f + c0 + cw)
            uv = _taps_sum(taps, w_ref, jnp.broadcast_to(b_ref[:, vcols], (FFN_ROWS, cw)), r0, vcols)
            ug = _taps_sum(taps, w_ref, jnp.broadcast_to(b_ref[:, gcols], (FFN_ROWS, cw)), r0, gcols)
            o_ref[r0:r0 + FFN_ROWS, vcols] = (ug * _sigmoid(ug) * uv).astype(BF16)

    full = lambda rows: pl.BlockSpec((rows, 2 * f), lambda b_, i: (0, 0))
    return pl.pallas_call(
        body, name=name, grid=(dims.batch_local, dims.seq // ts),
        in_specs=[*_seq_specs(dims, ts, 2 * f, FFN_HALO, 0), full(FFN_HALO), full(1)],
        out_specs=pl.BlockSpec((ts, f), lambda b_, i: (b_ * (dims.seq // ts) + i, 0)),
        out_shape=jax.ShapeDtypeStruct((t, f), BF16),
        scratch_shapes=[pltpu.VMEM((FFN_HALO + ts, 2 * f), F32), pltpu.VMEM((_moved_copies(offsets), ts, 2 * f), F32)],
        compiler_params=_params("parallel", "parallel"),
    )(up, up, w, b)


def _ffn_act_bwd(dact, up, w, b, dims, *, name, ts=128):
    t, f, kw = up.shape[0], dims.d_ff, dims.ffn_conv_width
    offsets = [FFN_HALO - (kw - 1) + k for k in range(kw)]

    def body(d_ref, up_ref, h_ref, w_ref, b_ref, du_ref, dw_ref, db_ref, buf, moved):
        i = pl.program_id(1)
        first = jnp.logical_and(pl.program_id(0) == 0, i == 0)
        buf[FFN_HALO:, :] = up_ref[...]
        buf[0:FFN_HALO, :] = jnp.where(i > 0, h_ref[...], 0.0)
        taps = _tap_sources(buf, moved, offsets, ts)
        for r0, c0, cw in _ffn_chunks(ts, f):
            vcols, gcols = slice(c0, c0 + cw), slice(f + c0, f + c0 + cw)
            uv = _taps_sum(taps, w_ref, jnp.broadcast_to(b_ref[:, vcols], (FFN_ROWS, cw)), r0, vcols)
            ug = _taps_sum(taps, w_ref, jnp.broadcast_to(b_ref[:, gcols], (FFN_ROWS, cw)), r0, gcols)
            d = d_ref[r0:r0 + FFN_ROWS, vcols].astype(F32)
            sg = _sigmoid(ug)
            du_ref[r0:r0 + FFN_ROWS, vcols] = d * ug * sg
            du_ref[r0:r0 + FFN_ROWS, gcols] = d * uv * sg * (1.0 + ug * (1.0 - sg))

        @pl.when(first)
        def _():
            dw_ref[...] = jnp.zeros_like(dw_ref)
            db_ref[...] = jnp.zeros_like(db_ref)

        du = du_ref[...]
        db_ref[...] += jnp.sum(du, axis=0, keepdims=True)
        for k, (src, off) in enumerate(taps):
            dw_ref[k:k + 1, :] += jnp.sum(du * src[pl.ds(off, ts), :], axis=0, keepdims=True)

    nst = dims.seq // ts
    n_moved = _moved_copies(offsets)
    full = lambda rows: pl.BlockSpec((rows, 2 * f), lambda b_, i: (0, 0))
    return pl.pallas_call(
        body, name=name, grid=(dims.batch_local, nst),
        in_specs=[pl.BlockSpec((ts, f), lambda b_, i: (b_ * nst + i, 0)),
                  *_seq_specs(dims, ts, 2 * f, FFN_HALO, 0), full(FFN_HALO), full(1)],
        out_specs=[pl.BlockSpec((ts, 2 * f), lambda b_, i: (b_ * nst + i, 0)), full(FFN_HALO), full(1)],
        out_shape=[jax.ShapeDtypeStruct((t, 2 * f), F32), jax.ShapeDtypeStruct((FFN_HALO, 2 * f), F32),
                   jax.ShapeDtypeStruct((1, 2 * f), F32)],
        scratch_shapes=[pltpu.VMEM((FFN_HALO + ts, 2 * f), F32), pltpu.VMEM((n_moved, ts, 2 * f), F32)],
        compiler_params=_params("arbitrary", "arbitrary"),
    )(dact, up, up, w, b)


def _ffn_conv_bwd(du, w, dims, *, name, ts=128):
    t, f2 = du.shape
    kw = dims.ffn_conv_width
    nst = dims.seq // ts

    offsets = [(kw - 1) - k for k in range(kw)]

    def body(d_ref, dn_ref, w_ref, o_ref, buf, moved):
        buf[0:ts, :] = d_ref[...]
        buf[ts:, :] = jnp.where(pl.program_id(1) < nst - 1, dn_ref[...], 0.0)
        taps = _tap_sources(buf, moved, offsets, ts)
        for r0, c0, cw in _ffn_chunks(ts, f2):
            cols = slice(c0, c0 + cw)
            o_ref[r0:r0 + FFN_ROWS, cols] = _taps_sum(taps, w_ref, jnp.zeros((FFN_ROWS, cw), F32), r0, cols).astype(BF16)

    return pl.pallas_call(
        body, name=name, grid=(dims.batch_local, nst),
        in_specs=[*_seq_specs(dims, ts, f2, FFN_HALO, 0, nxt=True), pl.BlockSpec((FFN_HALO, f2), lambda b_, i: (0, 0))],
        out_specs=pl.BlockSpec((ts, f2), lambda b_, i: (b_ * nst + i, 0)),
        out_shape=jax.ShapeDtypeStruct((t, f2), BF16),
        scratch_shapes=[pltpu.VMEM((ts + FFN_HALO, f2), F32), pltpu.VMEM((_moved_copies(offsets), ts, f2), F32)],
        compiler_params=_params("parallel", "parallel"),
    )(du, du, w)


def _alibi_slope(h, n_heads):
    return 2.0 ** (-8.0 * (h + 1) / n_heads)


def _dot_nt(a, b):
    return lax.dot_general(a, b, (((1,), (1,)), ((), ())), preferred_element_type=F32)


def _dot_tn(a, b):
    return lax.dot_general(a, b, (((0,), (0,)), ((), ())), preferred_element_type=F32)


def _attn_view(x, dims, dil):
    return x.reshape(dims.batch_local, dims.seq // dil, dil * x.shape[-1])


def _attn_fwd_group(q, k, v, state, dims, dil, *, last, name):
    t, a = q.shape
    assert 2 * dims.head_dim == 128 and dims.n_heads % 2 == 0
    blk, hd = ATTN_BLOCK, dims.head_dim
    nb = dims.seq // dil // blk
    has_prev = nb > 1
    nkeys = 2 * blk if has_prev else blk

    def body(*refs):
        it = iter(refs)
        q_ref, kc_ref, vc_ref = next(it), next(it), next(it)
        kp_ref, vp_ref = (next(it), next(it)) if has_prev else (None, None)
        m_in, l_in, acc_in = (next(it), next(it), next(it)) if state is not None else (None, None, None)
        outs = list(it)
        iq = lax.broadcasted_iota(jnp.int32, (blk, nkeys), 0)
        jk = lax.broadcasted_iota(jnp.int32, (blk, nkeys), 1)
        if has_prev:
            steps = iq + blk - jk
            valid = (steps >= 0) & (steps <= blk) & ((jk >= blk) | (pl.program_id(2) > 0))
        else:
            steps = iq - jk
            valid = steps >= 0
        dist = steps.astype(F32) * float(dil)
        low = lax.broadcasted_iota(jnp.int32, (blk, 2 * hd), 1) < hd
        for hp in range(dims.n_heads // 2):
            sl = slice(2 * hd * hp, 2 * hd * (hp + 1))
            q2 = q_ref[:, sl]
            if has_prev:
                kcat = jnp.concatenate([kp_ref[:, sl], kc_ref[:, sl]], axis=0)
                vcat = jnp.concatenate([vp_ref[:, sl], vc_ref[:, sl]], axis=0)
            else:
                kcat, vcat = kc_ref[:, sl], vc_ref[:, sl]
            halves = []
            for half in range(2):
                col = 2 * hd * hp + hd * half
                qh = jnp.where(low if half == 0 else jnp.logical_not(low), q2, jnp.zeros_like(q2))
                sc = _dot_nt(qh, kcat) - _alibi_slope(2 * hp + half, dims.n_heads) * dist
                sc = jnp.where(valid, sc, MASKED_SCORE)
                row_max = jnp.max(sc, axis=-1, keepdims=True)
                if state is None:
                    m_new = row_max
                    p = jnp.exp(sc - m_new)
                    alpha = None
                    l_new = jnp.sum(p, axis=-1, keepdims=True)
                else:
                    m_old = m_in[:, col:col + 1]
                    m_new = jnp.maximum(m_old, row_max)
                    p = jnp.exp(sc - m_new)
                    alpha = jnp.exp(m_old - m_new)
                    l_new = alpha * l_in[:, col:col + 1] + jnp.sum(p, axis=-1, keepdims=True)
                pv = jnp.dot(p.astype(BF16), vcat, preferred_element_type=F32)
                halves.append((m_new, l_new, alpha, pv))
            (m_a, l_a, al_a, pv_a), (m_b, l_b, al_b, pv_b) = halves
            if state is None:
                acc = jnp.where(low, pv_a, pv_b)
            else:
                old = acc_in[:, sl]
                acc = jnp.where(low, al_a * old + pv_a, al_b * old + pv_b)
            m2 = jnp.where(low, m_a, m_b)
            l2 = jnp.where(low, l_a, l_b)
            if last:
                outs[0][:, sl] = (acc / l2).astype(BF16)
                outs[1][:, sl] = m2 + jnp.log(l2)
            else:
                outs[0][:, sl] = m2
                outs[1][:, sl] = l2
                outs[2][:, sl] = acc

    cur = pl.BlockSpec((None, blk, a), lambda b, r, i: (b, i, r))
    prev = pl.BlockSpec((None, blk, a), lambda b, r, i: (b, jnp.maximum(i - 1, 0), r))
    args, in_specs = [q, k, v], [cur, cur, cur]
    if has_prev:
        args += [k, v]
        in_specs += [prev, prev]
    if state is not None:
        args += list(state)
        in_specs += [cur] * 3
    shape = lambda dt: jax.ShapeDtypeStruct((dims.batch_local, dims.seq // dil, dil * a), dt)
    out_shape = [shape(BF16), shape(F32)] if last else [shape(F32)] * 3
    outs = pl.pallas_call(
        body, name=name, grid=(dims.batch_local, dil, nb),
        in_specs=in_specs, out_specs=[cur] * len(out_shape), out_shape=out_shape,
        compiler_params=_params("parallel", "parallel", "parallel"),
    )(*[_attn_view(x, dims, dil) for x in args])
    return tuple(o.reshape(t, a) for o in outs)


def _attn_delta(do, o, head_ones, dims, *, name, tr=512):
    t, a = o.shape
    tr = _pick(t, tr, 8)

    def body(do_ref, o_ref, e_ref, d_ref):
        prod = do_ref[...].astype(F32) * o_ref[...].astype(F32)
        d_ref[...] = _head_mean(prod, e_ref, dims.head_dim) * float(dims.head_dim)

    return pl.pallas_call(
        body, name=name, grid=(t // tr,),
        in_specs=[_row_spec(tr, a), _row_spec(tr, a), pl.BlockSpec((a, a), lambda i: (0, 0))],
        out_specs=_row_spec(tr, a), out_shape=jax.ShapeDtypeStruct((t, a), F32),
        compiler_params=_params("parallel"),
    )(do, o, head_ones)


def _attn_bwd_group(q, k, v, do, lse, delta, dims, dil, *, name):
    t, a = q.shape
    blk, hd = ATTN_BLOCK, dims.head_dim
    nb = dims.seq // dil // blk
    has_next = nb > 1

    def body(*refs):
        k_ref, v_ref, q_ref, do_ref, lse_ref, dl_ref = refs[:6]
        if has_next:
            qn_ref, don_ref, lsen_ref, dln_ref = refs[6:10]
            dq_ref, dk_ref, dv_ref, carry = refs[10:]
        else:
            dq_ref, dk_ref, dv_ref = refs[6:]
        j = pl.program_id(2)
        iq = lax.broadcasted_iota(jnp.int32, (blk, blk), 0)
        jk = lax.broadcasted_iota(jnp.int32, (blk, blk), 1)
        low = lax.broadcasted_iota(jnp.int32, (blk, 2 * hd), 1) < hd

        def pair(hp, qr, dor, lser, dlr, steps, valid):
            sl = slice(2 * hd * hp, 2 * hd * (hp + 1))
            q2, do2, k2, v2 = qr[:, sl], dor[:, sl], k_ref[:, sl], v_ref[:, sl]
            dist = steps.astype(F32) * float(dil)
            dq_h, dk2, dv2 = [], None, None
            for half in range(2):
                col = 2 * hd * hp + hd * half
                mask = low if half == 0 else jnp.logical_not(low)
                qh = jnp.where(mask, q2, jnp.zeros_like(q2))
                doh = jnp.where(mask, do2, jnp.zeros_like(do2))
                sc = _dot_nt(qh, k2) - _alibi_slope(2 * hp + half, dims.n_heads) * dist
                p = jnp.where(valid, jnp.exp(sc - lser[:, col:col + 1]), 0.0)
                ds = p * (_dot_nt(doh, v2) - dlr[:, col:col + 1])
                ds_b, p_b = ds.astype(BF16), p.astype(BF16)
                dq_h.append(jnp.dot(ds_b, k2, preferred_element_type=F32))
                dk_h, dv_h = _dot_tn(ds_b, qh), _dot_tn(p_b, doh)
                dk2 = dk_h if dk2 is None else dk2 + dk_h
                dv2 = dv_h if dv2 is None else dv2 + dv_h
            return sl, jnp.where(low, dq_h[0], dq_h[1]), dk2, dv2

        if has_next:
            @pl.when(j == 0)
            def _():
                carry[...] = jnp.zeros_like(carry)

        for hp in range(dims.n_heads // 2):
            sl, dq2, dk2, dv2 = pair(hp, q_ref, do_ref, lse_ref, dl_ref, iq - jk, iq >= jk)
            dq_ref[:, sl] = (carry[:, sl] + dq2) if has_next else dq2
            dk_ref[:, sl] = dk2
            dv_ref[:, sl] = dv2

        if has_next:
            @pl.when(j + 1 < nb)
            def _():
                for hp in range(dims.n_heads // 2):
                    sl, dq2, dk2, dv2 = pair(hp, qn_ref, don_ref, lsen_ref, dln_ref, iq - jk + blk, jk >= iq)
                    carry[:, sl] = dq2
                    dk_ref[:, sl] += dk2
                    dv_ref[:, sl] += dv2

    cur = pl.BlockSpec((None, blk, a), lambda b, r, j: (b, j, r))
    nxt = pl.BlockSpec((None, blk, a), lambda b, r, j: (b, jnp.minimum(j + 1, nb - 1), r))
    args, in_specs = [k, v, q, do, lse, delta], [cur] * 6
    if has_next:
        args += [q, do, lse, delta]
        in_specs += [nxt] * 4
    shape = jax.ShapeDtypeStruct((dims.batch_local, dims.seq // dil, dil * a), F32)
    outs = pl.pallas_call(
        body, name=name, grid=(dims.batch_local, dil, nb),
        in_specs=in_specs, out_specs=[cur] * 3, out_shape=[shape] * 3,
        scratch_shapes=[pltpu.VMEM((blk, a), F32)] if has_next else [],
        compiler_params=_params("parallel", "parallel", "arbitrary"),
    )(*[_attn_view(x, dims, dil) for x in args])
    return tuple(o.reshape(t, a) for o in outs)


LANES = 128
MASK_BIAS = 1e30
RESIDUE_DILATIONS = tuple(d for d in DILATIONS if d > 1)


def _rows_to_residues(value, out_ref, scr, d):
    rows, width = value.shape
    for c in range(width // LANES):
        cols = slice(LANES * c, LANES * (c + 1))
        scr[c] = value[:, cols]
        for r in range(d):
            out_ref[r, :, cols] = scr[c, pl.ds(r, rows // d, stride=d), :].astype(out_ref.dtype)


def _residues_to_rows(in_ref, scr, d):
    _, n, width = in_ref.shape
    slabs = []
    for c in range(width // LANES):
        cols = slice(LANES * c, LANES * (c + 1))
        for r in range(d):
            scr[c, pl.ds(r, n, stride=d), :] = in_ref[r, :, cols].astype(F32)
        slabs.append(scr[c])
    return slabs[0] if len(slabs) == 1 else jnp.concatenate(slabs, axis=1)


def _residue_shape(dims, d, width, dtype):
    return jax.ShapeDtypeStruct((dims.batch_local, d, dims.seq // d, width), dtype)


def _residue_spec(dims, d, tr, width):
    tiles = dims.seq // tr
    return pl.BlockSpec((None, d, tr // d, width), lambda i: (i // tiles, 0, i % tiles, 0))


def _head_sum_matrix(dims):
    a = dims.n_heads * dims.head_dim
    head = jnp.arange(a, dtype=jnp.int32) // dims.head_dim
    return (head[:, None] == jnp.arange(LANES, dtype=jnp.int32)[None, :]).astype(BF16)


def _two_pass_dot(v, m):
    hi = v.astype(BF16)
    lo = (v - hi.astype(F32)).astype(BF16)
    return jnp.dot(hi, m, preferred_element_type=F32) + jnp.dot(lo, m, preferred_element_type=F32)


def _qkv_layouts_fwd(z, gq, gk, head_ones, dims, *, name, tr=256):
    t = z.shape[0]
    a = dims.n_heads * dims.head_dim
    q_scale = dims.head_dim ** -0.5
    nres = len(RESIDUE_DILATIONS)

    def body(q_ref, k_ref, v_ref, gq_ref, gk_ref, sum_ref, spread_ref, *rest):
        outs, scr = rest[:-1], rest[-1]
        qv, kv = q_ref[...].astype(F32), k_ref[...].astype(F32)
        mean = lambda val: _two_pass_dot(_two_pass_dot(val, sum_ref[...]), spread_ref[...]) * (1.0 / dims.head_dim)
        rq = lax.rsqrt(mean(qv * qv) + RMS_EPS)
        rk = lax.rsqrt(mean(kv * kv) + RMS_EPS)
        values = (qv * rq * gq_ref[...] * q_scale, kv * rk * gk_ref[...], v_ref[...].astype(F32))
        for j, val in enumerate(values):
            outs[j][...] = val.astype(BF16)
            for g, d in enumerate(RESIDUE_DILATIONS):
                _rows_to_residues(val, outs[3 * (g + 1) + j], scr, d)

    out_specs = [_row_spec(tr, a)] * 3
    out_shape = [jax.ShapeDtypeStruct((t, a), BF16)] * 3
    for d in RESIDUE_DILATIONS:
        out_specs += [_residue_spec(dims, d, tr, a)] * 3
        out_shape += [_residue_shape(dims, d, a, BF16)] * 3
    outs = pl.pallas_call(
        body, name=name, grid=(t // tr,),
        in_specs=[_row_spec(tr, a, 2), _row_spec(tr, a, 3), _row_spec(tr, a, 4), _vec_spec(a), _vec_spec(a),
                  pl.BlockSpec((a, LANES), lambda i: (0, 0)), pl.BlockSpec((LANES, a), lambda i: (0, 0))],
        out_specs=out_specs, out_shape=out_shape,
        scratch_shapes=[pltpu.VMEM((a // LANES, tr, LANES), F32)],
        compiler_params=_params("parallel"),
    )(z, z, z, gq, gk, *head_ones)
    return {d: tuple(outs[3 * g:3 * g + 3]) for g, d in enumerate((1,) + RESIDUE_DILATIONS)}


def _attn_specs(dims, dil, width):
    blk = ATTN_BLOCK
    nb = dims.seq // dil // blk
    if dil == 1:
        grid = (dims.batch_local, nb)
        at = lambda f: pl.BlockSpec((blk, width), lambda b, i: (b * nb + f(i), 0))
    else:
        grid = (dims.batch_local, dil, nb)
        at = lambda f: pl.BlockSpec((None, None, blk, width), lambda b, r, i: (b, r, f(i), 0))
    return grid, at(lambda i: i), at(lambda i: jnp.maximum(i - 1, 0)), at(lambda i: jnp.minimum(i + 1, nb - 1))


def _head_slopes(n_heads):
    h = lax.broadcasted_iota(jnp.int32, (n_heads, 1, 1), 0).astype(F32)
    return jnp.exp((h + 1.0) * (-8.0 / n_heads * math.log(2.0)))


def _pair_masks(hd):
    low = lax.broadcasted_iota(jnp.int32, (1, 2 * hd), 1) < hd
    return low, jnp.logical_not(low)


def _attn_fwd(q, k, v, dims, dil, *, name):
    a = dims.n_heads * dims.head_dim
    heads, hd, blk = dims.n_heads, dims.head_dim, ATTN_BLOCK
    assert 2 * hd == LANES and heads % 2 == 0 and heads <= LANES
    nb = dims.seq // dil // blk
    has_prev = nb > 1
    nkeys = 2 * blk if has_prev else blk
    grid, cur, prev, _ = _attn_specs(dims, dil, a)
    _, cur_stat, _, _ = _attn_specs(dims, dil, LANES)

    def body(*refs):
        if has_prev:
            q_ref, kc_ref, vc_ref, kp_ref, vp_ref, o_ref, lse_ref, s_scr, p_scr = refs
        else:
            q_ref, kc_ref, vc_ref, o_ref, lse_ref, s_scr, p_scr = refs
        low, high = _pair_masks(hd)

        def keys(cur_ref, prev_ref, sl):
            return jnp.concatenate([prev_ref[:, sl], cur_ref[:, sl]], axis=0) if has_prev else cur_ref[:, sl]

        for hp in range(heads // 2):
            sl = slice(LANES * hp, LANES * (hp + 1))
            q2 = q_ref[:, sl]
            kcat = keys(kc_ref, kp_ref if has_prev else None, sl)
            s_scr[2 * hp] = _dot_nt(jnp.where(low, q2, jnp.zeros_like(q2)), kcat)
            s_scr[2 * hp + 1] = _dot_nt(jnp.where(high, q2, jnp.zeros_like(q2)), kcat)

        iq = lax.broadcasted_iota(jnp.int32, (blk, nkeys), 0)
        jk = lax.broadcasted_iota(jnp.int32, (blk, nkeys), 1)
        if has_prev:
            steps = iq + blk - jk
            valid = (steps >= 0) & (steps <= blk) & ((jk >= blk) | (pl.program_id(len(grid) - 1) > 0))
        else:
            steps = iq - jk
            valid = steps >= 0
        bias = jnp.where(valid, steps.astype(F32) * (-float(dil)), -MASK_BIAS)
        s = s_scr[...] + _head_slopes(heads) * bias[None]
        m = jnp.max(s, axis=-1, keepdims=True)
        p = jnp.exp(s - m)
        l = jnp.sum(p, axis=-1, keepdims=True)
        p_scr[...] = p.astype(BF16)
        inv = 1.0 / l
        lse = m + jnp.log(l)

        lane = lax.broadcasted_iota(jnp.int32, (blk, LANES), 1)
        stat = jnp.zeros((blk, LANES), F32)
        for hp in range(heads // 2):
            sl = slice(LANES * hp, LANES * (hp + 1))
            vcat = keys(vc_ref, vp_ref if has_prev else None, sl)
            pv_a = jnp.dot(p_scr[2 * hp], vcat, preferred_element_type=F32) * inv[2 * hp]
            pv_b = jnp.dot(p_scr[2 * hp + 1], vcat, preferred_element_type=F32) * inv[2 * hp + 1]
            o_ref[:, sl] = jnp.where(low, pv_a, pv_b)
            stat = jnp.where(lane == 2 * hp, lse[2 * hp], stat)
            stat = jnp.where(lane == 2 * hp + 1, lse[2 * hp + 1], stat)
        lse_ref[...] = stat

    lead = q.shape[:-2]
    rows = q.shape[-2]
    o, lse = pl.pallas_call(
        body, name=name, grid=grid,
        in_specs=[cur, cur, cur] + ([prev, prev] if has_prev else []),
        out_specs=[cur, cur_stat],
        out_shape=[jax.ShapeDtypeStruct(lead + (rows, a), F32), jax.ShapeDtypeStruct(lead + (rows, LANES), F32)],
        scratch_shapes=[pltpu.VMEM((heads, blk, nkeys), F32), pltpu.VMEM((heads, blk, nkeys), BF16)],
        compiler_params=_params(*["parallel"] * len(grid)),
    )(q, k, v, *([k, v] if has_prev else []))
    return o, lse


def _attn_combine(groups, head_spread, dims, *, name, tr=256):
    t = dims.tokens
    a = dims.n_heads * dims.head_dim
    dils = tuple(groups)

    def body(*refs):
        ins = refs[:2 * len(dils)]
        x_ref = refs[2 * len(dils)]
        o_ref = refs[2 * len(dils) + 1]
        lse_refs = refs[2 * len(dils) + 2:-2]
        scr, scr_stat = refs[-2], refs[-1]
        outs, stats = [], []
        for g, d in enumerate(dils):
            if d == 1:
                outs.append(ins[2 * g][...])
                stats.append(ins[2 * g + 1][...])
            else:
                outs.append(_residues_to_rows(ins[2 * g], scr, d))
                stats.append(_residues_to_rows(ins[2 * g + 1], scr_stat, d))
        top = functools.reduce(jnp.maximum, stats)
        weights = [jnp.exp(s - top) for s in stats]
        total = functools.reduce(jnp.add, weights)
        joint = top + jnp.log(total)
        inv = 1.0 / total
        acc = None
        for w, o in zip(weights, outs):
            term = _two_pass_dot(w * inv, x_ref[...]) * o
            acc = term if acc is None else acc + term
        o_ref[...] = acc.astype(BF16)
        for g, d in enumerate(dils):
            if d == 1:
                lse_refs[g][...] = joint
            else:
                _rows_to_residues(joint, lse_refs[g], scr_stat, d)

    in_specs, args, lse_specs, lse_shapes = [], [], [], []
    for d in dils:
        if d == 1:
            in_specs += [_row_spec(tr, a), _row_spec(tr, LANES)]
            lse_specs.append(_row_spec(tr, LANES))
            lse_shapes.append(jax.ShapeDtypeStruct((t, LANES), F32))
        else:
            in_specs += [_residue_spec(dims, d, tr, a), _residue_spec(dims, d, tr, LANES)]
            lse_specs.append(_residue_spec(dims, d, tr, LANES))
            lse_shapes.append(_residue_shape(dims, d, LANES, F32))
        args += list(groups[d])
    outs = pl.pallas_call(
        body, name=name, grid=(t // tr,),
        in_specs=in_specs + [pl.BlockSpec((LANES, a), lambda i: (0, 0))],
        out_specs=[_row_spec(tr, a)] + lse_specs,
        out_shape=[jax.ShapeDtypeStruct((t, a), BF16)] + lse_shapes,
        scratch_shapes=[pltpu.VMEM((a // LANES, tr, LANES), F32), pltpu.VMEM((1, tr, LANES), F32)],
        compiler_params=_params("parallel"),
    )(*args, head_spread)
    return outs[0], dict(zip(dils, outs[1:]))


def _attn_bwd_prep(do, o, head_sum, dims, *, name, tr=256):
    t, a = o.shape

    def body(do_ref, o_ref, e_ref, *rest):
        outs, scr, scr_stat = rest[:-2], rest[-2], rest[-1]
        dov = do_ref[...].astype(F32)
        delta = _two_pass_dot(dov * o_ref[...].astype(F32), e_ref[...])
        outs[0][...] = delta
        for g, d in enumerate(RESIDUE_DILATIONS):
            _rows_to_residues(dov, outs[1 + 2 * g], scr, d)
            _rows_to_residues(delta, outs[2 + 2 * g], scr_stat, d)

    out_specs, out_shape = [_row_spec(tr, LANES)], [jax.ShapeDtypeStruct((t, LANES), F32)]
    for d in RESIDUE_DILATIONS:
        out_specs += [_residue_spec(dims, d, tr, a), _residue_spec(dims, d, tr, LANES)]
        out_shape += [_residue_shape(dims, d, a, BF16), _residue_shape(dims, d, LANES, F32)]
    outs = pl.pallas_call(
        body, name=name, grid=(t // tr,),
        in_specs=[_row_spec(tr, a), _row_spec(tr, a), pl.BlockSpec((a, LANES), lambda i: (0, 0))],
        out_specs=out_specs, out_shape=out_shape,
        scratch_shapes=[pltpu.VMEM((a // LANES, tr, LANES), F32), pltpu.VMEM((1, tr, LANES), F32)],
        compiler_params=_params("parallel"),
    )(do, o, head_sum)
    dos, deltas = {1: do}, {1: outs[0]}
    for g, d in enumerate(RESIDUE_DILATIONS):
        dos[d], deltas[d] = outs[1 + 2 * g], outs[2 + 2 * g]
    return dos, deltas


def _attn_bwd(q, k, v, do, lse, delta, dims, dil, *, name):
    a = dims.n_heads * dims.head_dim
    heads, hd, blk = dims.n_heads, dims.head_dim, ATTN_BLOCK
    nb = dims.seq // dil // blk
    has_next = nb > 1
    nq = 2 * blk if has_next else blk
    grid, cur, _, nxt = _attn_specs(dims, dil, a)
    _, cur_stat, _, nxt_stat = _attn_specs(dims, dil, LANES)

    def body(*refs):
        k_ref, v_ref, q_ref, do_ref, lse_ref, dl_ref = refs[:6]
        if has_next:
            qn_ref, don_ref, lsen_ref, dln_ref = refs[6:10]
            dq_ref, dk_ref, dv_ref, s_scr, dp_scr, p_scr, ds_scr, carry = refs[10:]
        else:
            dq_ref, dk_ref, dv_ref, s_scr, dp_scr, p_scr, ds_scr = refs[6:]
        j = pl.program_id(len(grid) - 1)
        low, high = _pair_masks(hd)

        def stacked(ref, nref, sl):
            return jnp.concatenate([ref[:, sl], nref[:, sl]], axis=0) if has_next else ref[:, sl]

        def halves(x):
            return jnp.where(low, x, jnp.zeros_like(x)), jnp.where(high, x, jnp.zeros_like(x))

        for hp in range(heads // 2):
            sl = slice(LANES * hp, LANES * (hp + 1))
            k2, v2 = k_ref[:, sl], v_ref[:, sl]
            q_a, q_b = halves(stacked(q_ref, qn_ref if has_next else None, sl))
            do_a, do_b = halves(stacked(do_ref, don_ref if has_next else None, sl))
            s_scr[2 * hp], s_scr[2 * hp + 1] = _dot_nt(q_a, k2), _dot_nt(q_b, k2)
            dp_scr[2 * hp], dp_scr[2 * hp + 1] = _dot_nt(do_a, v2), _dot_nt(do_b, v2)

        rq = lax.broadcasted_iota(jnp.int32, (nq, blk), 0)
        jk = lax.broadcasted_iota(jnp.int32, (nq, blk), 1)
        if has_next:
            iq = jnp.where(rq < blk, rq, rq - blk)
            steps = jnp.where(rq < blk, iq - jk, iq - jk + blk)
            valid = ((rq < blk) & (iq >= jk)) | ((rq >= blk) & (jk >= iq) & (j + 1 < nb))
        else:
            steps, valid = rq - jk, rq >= jk
        bias = jnp.where(valid, steps.astype(F32) * (-float(dil)), -MASK_BIAS)
        lse_all = stacked(lse_ref, lsen_ref if has_next else None, slice(None))
        dl_all = stacked(dl_ref, dln_ref if has_next else None, slice(None))
        lse3 = jnp.stack([lse_all[:, h:h + 1] for h in range(heads)])
        dl3 = jnp.stack([dl_all[:, h:h + 1] for h in range(heads)])
        p = jnp.exp(s_scr[...] + _head_slopes(heads) * bias[None] - lse3)
        p_scr[...] = p.astype(BF16)
        ds_scr[...] = (p * (dp_scr[...] - dl3)).astype(BF16)

        if has_next:
            @pl.when(j == 0)
            def _():
                carry[...] = jnp.zeros_like(carry)

        for hp in range(heads // 2):
            sl = slice(LANES * hp, LANES * (hp + 1))
            k2 = k_ref[:, sl]
            q_a, q_b = halves(stacked(q_ref, qn_ref if has_next else None, sl))
            do_a, do_b = halves(stacked(do_ref, don_ref if has_next else None, sl))
            ds_a, ds_b = ds_scr[2 * hp], ds_scr[2 * hp + 1]
            dq2 = jnp.where(low, jnp.dot(ds_a, k2, preferred_element_type=F32),
                            jnp.dot(ds_b, k2, preferred_element_type=F32))
            dk_ref[:, sl] = _dot_tn(ds_a, q_a) + _dot_tn(ds_b, q_b)
            dv_ref[:, sl] = _dot_tn(p_scr[2 * hp], do_a) + _dot_tn(p_scr[2 * hp + 1], do_b)
            if has_next:
                dq_ref[:, sl] = carry[:, sl] + dq2[:blk]
                carry[:, sl] = dq2[blk:]
            else:
                dq_ref[:, sl] = dq2

    args, in_specs = [k, v, q, do, lse, delta], [cur] * 4 + [cur_stat] * 2
    if has_next:
        args += [q, do, lse, delta]
        in_specs += [nxt] * 2 + [nxt_stat] * 2
    shape = jax.ShapeDtypeStruct(q.shape, F32)
    scratch = [pltpu.VMEM((heads, nq, blk), F32)] * 2 + [pltpu.VMEM((heads, nq, blk), BF16)] * 2
    if has_next:
        scratch.append(pltpu.VMEM((blk, a), F32))
    return pl.pallas_call(
        body, name=name, grid=grid, in_specs=in_specs, out_specs=[cur] * 3, out_shape=[shape] * 3,
        scratch_shapes=scratch,
        compiler_params=_params(*["parallel"] * (len(grid) - 1), "arbitrary"),
    )(*args)


def _qkv_layouts_bwd(z, grads, gq, gk, head_ones, dims, *, name, tr=256):
    t = z.shape[0]
    a = dims.n_heads * dims.head_dim
    q_scale = dims.head_dim ** -0.5
    dils = tuple(grads)

    def body(q_ref, k_ref, *rest):
        d_refs = rest[:3 * len(dils)]
        gq_ref, gk_ref, sum_ref, spread_ref, dz_ref, dgq_ref, dgk_ref, scr = rest[3 * len(dils):]
        first = pl.program_id(0) == 0
        mean = lambda val: _two_pass_dot(_two_pass_dot(val, sum_ref[...]), spread_ref[...]) * (1.0 / dims.head_dim)

        def total(j):
            acc = None
            for g, d in enumerate(dils):
                ref = d_refs[3 * g + j]
                part = ref[...] if d == 1 else _residues_to_rows(ref, scr, d)
                acc = part if acc is None else acc + part
            return acc

        def norm_bwd(x_ref, dy, g_ref, scale, col, dg_ref):
            xv = x_ref[...].astype(F32)
            dy = dy * scale
            r = lax.rsqrt(mean(xv * xv) + RMS_EPS)
            gy = dy * g_ref[...]
            dx = r * gy - xv * (r * r * r) * mean(xv * gy)
            dz_ref[:, col * a:(col + 1) * a] = dx.astype(BF16)
            _accumulate(dg_ref, jnp.sum(dy * xv * r, axis=0, keepdims=True), first)

        norm_bwd(q_ref, total(0), gq_ref, q_scale, 0, dgq_ref)
        norm_bwd(k_ref, total(1), gk_ref, 1.0, 1, dgk_ref)
        dz_ref[:, 2 * a:3 * a] = total(2).astype(BF16)

    in_specs, args = [_row_spec(tr, a, 2), _row_spec(tr, a, 3)], [z, z]
    for d in dils:
        in_specs += [_row_spec(tr, a) if d == 1 else _residue_spec(dims, d, tr, a)] * 3
        args += list(grads[d])
    in_specs += [_vec_spec(a), _vec_spec(a), pl.BlockSpec((a, LANES), lambda i: (0, 0)),
                 pl.BlockSpec((LANES, a), lambda i: (0, 0))]
    return pl.pallas_call(
        body, name=name, grid=(t // tr,), in_specs=in_specs,
        out_specs=[_row_spec(tr, 3 * a), _vec_spec(a), _vec_spec(a)],
        out_shape=[jax.ShapeDtypeStruct((t, 3 * a), BF16)] + [jax.ShapeDtypeStruct((1, a), F32)] * 2,
        scratch_shapes=[pltpu.VMEM((a // LANES, tr, LANES), F32)],
        compiler_params=_params("arbitrary"),
    )(*args, gq, gk, *head_ones)


def _mix_fwd(ya, yb, z, gate_b, dims, *, name, tr=512):
    t, d = ya.shape
    tr = _pick(t, tr, 8)
    first_gate_col = z.shape[1] // d - 2

    def body(ya_ref, yb_ref, ga_ref, gb_ref, ba_ref, bb_ref, o_ref):
        g_a = _sigmoid(ga_ref[...].astype(F32) + ba_ref[...])
        g_b = _sigmoid(gb_ref[...].astype(F32) + bb_ref[...])
        o_ref[...] = (g_a * ya_ref[...] + g_b * yb_ref[...]).astype(BF16)

    return pl.pallas_call(
        body, name=name, grid=(t // tr,),
        in_specs=[_row_spec(tr, d), _row_spec(tr, d), _row_spec(tr, d, first_gate_col),
                  _row_spec(tr, d, first_gate_col + 1), _vec_spec(d, 0), _vec_spec(d, 1)],
        out_specs=_row_spec(tr, d), out_shape=jax.ShapeDtypeStruct((t, d), BF16),
        compiler_params=_params("parallel"),
    )(ya, yb, z, z, gate_b, gate_b)


def _mix_bwd(dmix, ya, yb, z, gate_b, dims, *, name, tr=512):
    t, d = ya.shape
    tr = _pick(t, tr, 8)
    first_gate_col = z.shape[1] // d - 2

    def body(dm_ref, ya_ref, yb_ref, ga_ref, gb_ref, ba_ref, bb_ref, dya_ref, dyb_ref, dz_ref, db_ref):
        dm = dm_ref[...].astype(F32)
        g_a = _sigmoid(ga_ref[...].astype(F32) + ba_ref[...])
        g_b = _sigmoid(gb_ref[...].astype(F32) + bb_ref[...])
        dya_ref[...] = (dm * g_a).astype(BF16)
        dyb_ref[...] = (dm * g_b).astype(BF16)
        dl_a = dm * ya_ref[...] * g_a * (1.0 - g_a)
        dl_b = dm * yb_ref[...] * g_b * (1.0 - g_b)
        dz_ref[:, 0:d] = dl_a.astype(BF16)
        dz_ref[:, d:2 * d] = dl_b.astype(BF16)
        first = pl.program_id(0) == 0
        sums = jnp.concatenate([jnp.sum(dl_a, axis=0, keepdims=True), jnp.sum(dl_b, axis=0, keepdims=True)], axis=1)
        _accumulate(db_ref, sums, first)

    return pl.pallas_call(
        body, name=name, grid=(t // tr,),
        in_specs=[_row_spec(tr, d), _row_spec(tr, d), _row_spec(tr, d), _row_spec(tr, d, first_gate_col),
                  _row_spec(tr, d, first_gate_col + 1), _vec_spec(d, 0), _vec_spec(d, 1)],
        out_specs=[_row_spec(tr, d), _row_spec(tr, d), _row_spec(tr, 2 * d), _vec_spec(2 * d)],
        out_shape=[jax.ShapeDtypeStruct((t, d), BF16)] * 2 + [jax.ShapeDtypeStruct((t, 2 * d), BF16),
                                                              jax.ShapeDtypeStruct((1, 2 * d), F32)],
        compiler_params=_params("arbitrary"),
    )(dmix, ya, yb, z, z, gate_b, gate_b)


def _loss_head(y, target, *, name, tr=512):
    t, d = y.shape
    tr = _pick(t, tr, 8)

    def body(y_ref, t_ref, dy_ref, dyb_ref, loss_ref):
        err = y_ref[...] - t_ref[...]
        dy = err * (1.0 / d)
        dy_ref[...] = dy
        dyb_ref[...] = dy.astype(BF16)
        part = jnp.sum(jnp.sum(err * err, axis=-1, keepdims=True), axis=0, keepdims=True) * (0.5 / d)
        _accumulate(loss_ref, jnp.broadcast_to(part, (8, 128)), pl.program_id(0) == 0)

    return pl.pallas_call(
        body, name=name, grid=(t // tr,),
        in_specs=[_row_spec(tr, d), _row_spec(tr, d)],
        out_specs=[_row_spec(tr, d), _row_spec(tr, d), pl.BlockSpec((8, 128), lambda i: (0, 0))],
        out_shape=[jax.ShapeDtypeStruct((t, d), F32), jax.ShapeDtypeStruct((t, d), BF16),
                   jax.ShapeDtypeStruct((8, 128), F32)],
        compiler_params=_params("arbitrary"),
    )(y, target)


def _adamw(w, grads, m, v, *, name, tr=256):
    r, c = w.shape
    tr = _pick(r, tr, 8)
    ng = len(grads)
    c1 = 1.0 - ADAM_B1 ** ADAM_STEP
    c2 = 1.0 - ADAM_B2 ** ADAM_STEP

    def body(*refs):
        w_ref, g_refs, m_ref, v_ref = refs[0], refs[1:1 + ng], refs[1 + ng], refs[2 + ng]
        g_out, d_out, m_out, v_out = refs[3 + ng:]
        g = g_refs[0][...]
        for extra in g_refs[1:]:
            g = g + extra[...]
        m_new = ADAM_B1 * m_ref[...] + (1.0 - ADAM_B1) * g
        v_new = ADAM_B2 * v_ref[...] + (1.0 - ADAM_B2) * (g * g)
        g_out[...] = g
        m_out[...] = m_new
        v_out[...] = v_new
        d_out[...] = -ADAM_LR * ((m_new / c1) / (jnp.sqrt(v_new / c2) + ADAM_EPS) + ADAM_WD * w_ref[...])

    spec = pl.BlockSpec((tr, c), lambda i: (i, 0))
    return pl.pallas_call(
        body, name=name, grid=(r // tr,),
        in_specs=[spec] * (3 + ng), out_specs=[spec] * 4, out_shape=[jax.ShapeDtypeStruct((r, c), F32)] * 4,
        compiler_params=_params("parallel"),
    )(w, *grads, m, v)


CHIP_PEERS = ((1, 0), (0, 1), (1, 1))


def _place():
    return lax.axis_index("x"), lax.axis_index("y"), lax.axis_index("c")


HBM = pl.BlockSpec(memory_space=pltpu.HBM)
SEM = pl.BlockSpec(memory_space=pltpu.SEMAPHORE)
IN_FLIGHT = pltpu.SideEffectType.DATAFLOW_SIDE_EFFECTING


def _in_hbm(a):
    return pltpu.with_memory_space_constraint(a, pltpu.HBM)


def _cast_to_lands(shards, dtypes, *, name):
    n = len(shards)

    def body(*refs):
        ins, outs, bufs, sems = refs[:n], refs[n:2 * n], refs[2 * n:3 * n], refs[3 * n]
        x, y, _ = _place()
        copies = []
        for a in range(n):
            bufs[a][...] = ins[a][...].astype(dtypes[a])
            cp = pltpu.make_async_copy(bufs[a], outs[a].at[2 * x + y], sems.at[a])
            cp.start()
            copies.append(cp)
        for cp in copies:
            cp.wait()

    return pl.pallas_call(
        body, name=name, in_specs=[pl.BlockSpec(memory_space=pltpu.VMEM)] * n, out_specs=[ANY] * n,
        out_shape=[jax.ShapeDtypeStruct((N_CHIPS,) + s.shape, dt) for s, dt in zip(shards, dtypes)],
        scratch_shapes=[pltpu.VMEM(s.shape, dt) for s, dt in zip(shards, dtypes)] + [pltpu.SemaphoreType.DMA((n,))],
        compiler_params=pltpu.CompilerParams(vmem_limit_bytes=V7X_VMEM_LIMIT_BYTES),
    )(*shards)


def _chip_copy(src, dst, send, recv, flip, place):
    x, y, c = place
    return pltpu.make_async_remote_copy(src_ref=src, dst_ref=dst, send_sem=send, recv_sem=recv,
                                        device_id=(x ^ flip[0], y ^ flip[1], c), device_id_type=MESH)


def _my_part(land, place, halved):
    block = land.at[2 * place[0] + place[1]]
    if not halved:
        return block
    rows = land.shape[1] // 2
    return block.at[pl.ds(pl.multiple_of(place[2] * rows, rows), rows)]


def _gather_start(lands, after, *, name, halved=()):
    n = len(lands)

    def body(*refs):
        ins, send, recv, token = refs[:n], refs[n + 1], refs[n + 2], refs[-1]
        place = _place()
        for a in range(n):
            part = _my_part(ins[a], place, a in halved)
            for p, flip in enumerate(CHIP_PEERS):
                k = 3 * a + p
                _chip_copy(part, part, send.at[k], recv.at[k], flip, place).start()
        token[...] = jnp.zeros_like(token)

    outs = pl.pallas_call(
        body, name=name, in_specs=[HBM] * n + [ANY],
        out_specs=(SEM, SEM, *[HBM] * n, pl.BlockSpec(memory_space=pltpu.VMEM)),
        out_shape=(pltpu.SemaphoreType.DMA((3 * n,)), pltpu.SemaphoreType.DMA((3 * n,)),
                   *[pltpu.HBM(l.shape, l.dtype) for l in lands], jax.ShapeDtypeStruct((8, 128), F32)),
        input_output_aliases={a: 2 + a for a in range(n)},
        compiler_params=pltpu.CompilerParams(has_side_effects=IN_FLIGHT),
    )(*[_in_hbm(l) for l in lands], after)
    return outs[0], outs[1], list(outs[2:2 + n]), outs[-1]


def _gather_wait(send, recv, lands, after, *, name, halved=()):
    n = len(lands)

    def body(*refs):
        ins, send_ref, recv_ref = refs[:n], refs[n], refs[n + 1]
        place = _place()
        for a in range(n):
            part = _my_part(ins[a], place, a in halved)
            for p, flip in enumerate(CHIP_PEERS):
                k = 3 * a + p
                cp = _chip_copy(part, part, send_ref.at[k], recv_ref.at[k], flip, place)
                cp.wait_send()
                cp.wait_recv()

    return pl.pallas_call(
        body, name=name, in_specs=[HBM] * n + [SEM, SEM, ANY], out_specs=[HBM] * n,
        out_shape=[pltpu.HBM(l.shape, l.dtype) for l in lands],
        input_output_aliases={a: a for a in range(n)},
        compiler_params=pltpu.CompilerParams(has_side_effects=IN_FLIGHT),
    )(*lands, send, recv, after)


def _forward_to_sibling(land, *, name):
    rows = land.shape[1] // 2

    def body(land_ref, out_ref, send, recv):
        x, y, c = _place()
        copies = []
        for p, (fx, fy) in enumerate(CHIP_PEERS):
            chip = 2 * (x ^ fx) + (y ^ fy)
            mine = pl.ds(pl.multiple_of(c * rows, rows), rows)
            theirs = pl.ds(pl.multiple_of((1 - c) * rows, rows), rows)
            out = pltpu.make_async_remote_copy(
                src_ref=land_ref.at[chip].at[mine], dst_ref=out_ref.at[chip].at[mine], send_sem=send.at[p],
                recv_sem=recv.at[p], device_id=(x, y, 1 - c), device_id_type=MESH)
            out.start()
            copies.append((out, pltpu.make_async_remote_copy(
                src_ref=land_ref.at[chip].at[theirs], dst_ref=out_ref.at[chip].at[theirs], send_sem=send.at[p],
                recv_sem=recv.at[p], device_id=(x, y, 1 - c), device_id_type=MESH)))
        for out, arriving in copies:
            out.wait_send()
            arriving.wait_recv()

    return pl.pallas_call(
        body, name=name, in_specs=[ANY], out_specs=ANY, out_shape=jax.ShapeDtypeStruct(land.shape, land.dtype),
        input_output_aliases={0: 0},
        scratch_shapes=[pltpu.SemaphoreType.DMA((3,)), pltpu.SemaphoreType.DMA((3,))],
    )(land)


def _scatter_start(grad, *, name):
    def body(g_ref, land_ref, send, recv, g_thru, land_thru, token):
        place = _place()
        for p, flip in enumerate(CHIP_PEERS):
            peer_chip = 2 * (place[0] ^ flip[0]) + (place[1] ^ flip[1])
            _chip_copy(g_ref.at[peer_chip], land_ref.at[p], send.at[p], recv.at[p], flip, place).start()
        token[...] = jnp.zeros_like(token)

    land = lax.empty((3,) + grad.shape[1:], grad.dtype)
    return pl.pallas_call(
        body, name=name, in_specs=[HBM, HBM],
        out_specs=(SEM, SEM, HBM, HBM, pl.BlockSpec(memory_space=pltpu.VMEM)),
        out_shape=(pltpu.SemaphoreType.DMA((3,)), pltpu.SemaphoreType.DMA((3,)), pltpu.HBM(grad.shape, grad.dtype),
                   pltpu.HBM(land.shape, land.dtype), jax.ShapeDtypeStruct((8, 128), F32)),
        input_output_aliases={0: 2, 1: 3},
        compiler_params=pltpu.CompilerParams(has_side_effects=IN_FLIGHT),
    )(_in_hbm(grad), _in_hbm(land))


def _scatter_wait(started, after, *, name):
    n = len(started)

    def body(*refs):
        grads, lands = refs[:n], refs[n:2 * n]
        sends, recvs = refs[2 * n:3 * n], refs[3 * n:4 * n]
        place = _place()
        for a in range(n):
            for p, flip in enumerate(CHIP_PEERS):
                cp = _chip_copy(grads[a].at[0], lands[a].at[p], sends[a].at[p], recvs[a].at[p], flip, place)
                cp.wait_send()
                cp.wait_recv()

    grads, lands = [s[2] for s in started], [s[3] for s in started]
    after = list(after) if isinstance(after, (list, tuple)) else [after]
    outs = pl.pallas_call(
        body, name=name, in_specs=[HBM] * (2 * n) + [SEM] * (2 * n) + [ANY] * len(after), out_specs=[HBM] * (2 * n),
        out_shape=[pltpu.HBM(a.shape, a.dtype) for a in grads + lands],
        input_output_aliases={a: a for a in range(2 * n)},
        compiler_params=pltpu.CompilerParams(has_side_effects=IN_FLIGHT),
    )(*grads, *lands, *[s[0] for s in started], *[s[1] for s in started], *after)
    return list(zip(outs[:n], outs[n:]))


def _sibling_copy(src, dst, send, recv, place):
    x, y, c = place
    return pltpu.make_async_remote_copy(src_ref=src, dst_ref=dst, send_sem=send, recv_sem=recv,
                                        device_id=(x, y, 1 - c), device_id_type=MESH)


def _swap_start(arrays, *, name):
    n = len(arrays)

    def body(*refs):
        ins, lands, send, recv, token = refs[:n], refs[n:2 * n], refs[2 * n], refs[2 * n + 1], refs[-1]
        place = _place()
        for a in range(n):
            _sibling_copy(ins[a], lands[a], send.at[a], recv.at[a], place).start()
        token[...] = jnp.zeros_like(token)

    both = [_in_hbm(a) for a in arrays] + [_in_hbm(lax.empty(a.shape, a.dtype)) for a in arrays]
    outs = pl.pallas_call(
        body, name=name, in_specs=[HBM] * (2 * n),
        out_specs=(SEM, SEM, *[HBM] * (2 * n), pl.BlockSpec(memory_space=pltpu.VMEM)),
        out_shape=(pltpu.SemaphoreType.DMA((n,)), pltpu.SemaphoreType.DMA((n,)),
                   *[pltpu.HBM(a.shape, a.dtype) for a in both], jax.ShapeDtypeStruct((8, 128), F32)),
        input_output_aliases={a: 2 + a for a in range(2 * n)},
        compiler_params=pltpu.CompilerParams(has_side_effects=IN_FLIGHT),
    )(*both)
    return outs[0], outs[1], list(outs[2:2 + n]), list(outs[2 + n:2 + 2 * n]), outs[-1]


def _swap_wait(started, after, *, name):
    send, recv, arrays, lands = started[:4]
    n = len(arrays)

    def body(*refs):
        ins, zones, send_ref, recv_ref = refs[:n], refs[n:2 * n], refs[2 * n], refs[2 * n + 1]
        place = _place()
        for a in range(n):
            cp = _sibling_copy(ins[a], zones[a], send_ref.at[a], recv_ref.at[a], place)
            cp.wait_send()
            cp.wait_recv()

    after = list(after) if isinstance(after, (list, tuple)) else [after]
    outs = pl.pallas_call(
        body, name=name, in_specs=[HBM] * (2 * n) + [SEM, SEM] + [ANY] * len(after), out_specs=[HBM] * (2 * n),
        out_shape=[pltpu.HBM(a.shape, a.dtype) for a in arrays + lands],
        input_output_aliases={a: a for a in range(2 * n)},
        compiler_params=pltpu.CompilerParams(has_side_effects=IN_FLIGHT),
    )(*arrays, *lands, send, recv, *after)
    return list(outs[:n]), list(outs[n:])


def _allreduce_start(packed, *, name):
    n_dev = 8

    def body(src_ref, land_ref, send, recv, src_thru, land_thru, token):
        x, y, c = _place()
        me = 4 * x + 2 * y + c
        for p in range(1, n_dev):
            pltpu.make_async_remote_copy(
                src_ref=src_ref, dst_ref=land_ref.at[me], send_sem=send.at[p - 1], recv_sem=recv.at[p - 1],
                device_id=(x ^ (p >> 2), y ^ ((p >> 1) & 1), c ^ (p & 1)), device_id_type=MESH).start()
        token[...] = jnp.zeros_like(token)

    land = lax.empty((n_dev,) + packed.shape, packed.dtype)
    return pl.pallas_call(
        body, name=name, in_specs=[HBM, HBM],
        out_specs=(SEM, SEM, HBM, HBM, pl.BlockSpec(memory_space=pltpu.VMEM)),
        out_shape=(pltpu.SemaphoreType.DMA((n_dev - 1,)), pltpu.SemaphoreType.DMA((n_dev - 1,)),
                   pltpu.HBM(packed.shape, packed.dtype), pltpu.HBM(land.shape, land.dtype),
                   jax.ShapeDtypeStruct((8, 128), F32)),
        input_output_aliases={0: 2, 1: 3},
        compiler_params=pltpu.CompilerParams(has_side_effects=IN_FLIGHT),
    )(_in_hbm(packed), _in_hbm(land))


def _allreduce_wait(started, after, *, name):
    send, recv, packed, land = started[:4]
    n_dev = 8

    def body(src_ref, land_ref, send_ref, recv_ref, *_):
        x, y, c = _place()
        for p in range(1, n_dev):
            cp = pltpu.make_async_remote_copy(
                src_ref=src_ref, dst_ref=land_ref.at[0], send_sem=send_ref.at[p - 1], recv_sem=recv_ref.at[p - 1],
                device_id=(x ^ (p >> 2), y ^ ((p >> 1) & 1), c ^ (p & 1)), device_id_type=MESH)
            cp.wait_send()
            cp.wait_recv()

    after = list(after) if isinstance(after, (list, tuple)) else [after]
    return pl.pallas_call(
        body, name=name, in_specs=[HBM, HBM, SEM, SEM] + [ANY] * len(after), out_specs=[HBM, HBM],
        out_shape=[pltpu.HBM(packed.shape, packed.dtype), pltpu.HBM(land.shape, land.dtype)],
        input_output_aliases={0: 0, 1: 1},
        compiler_params=pltpu.CompilerParams(has_side_effects=IN_FLIGHT),
    )(packed, land, send, recv, *after)


def _sum_devices(mine, land, *, name):
    n_dev = land.shape[0]

    def body(mine_ref, land_ref, out_ref):
        x, y, c = _place()
        me = 4 * x + 2 * y + c
        total = None
        for s in range(n_dev):
            part = jnp.where(me == s, mine_ref[...], land_ref[s])
            total = part if total is None else total + part
        out_ref[...] = total

    return pl.pallas_call(body, name=name, out_shape=jax.ShapeDtypeStruct(mine.shape, mine.dtype))(mine, land)


def _sum_received(grad, land, *, name, tr=256):
    _, r, c = grad.shape
    tr = _pick(r, tr, 8)

    def body(chip_ref, g_ref, l_ref, o_ref):
        o_ref[...] = ((g_ref[...] + l_ref[0].astype(F32)) + l_ref[1].astype(F32)) + l_ref[2].astype(F32)

    chip = (2 * lax.axis_index("x") + lax.axis_index("y")).astype(jnp.int32).reshape(1)
    return pl.pallas_call(
        body, name=name,
        grid_spec=pltpu.PrefetchScalarGridSpec(
            num_scalar_prefetch=1, grid=(r // tr,),
            in_specs=[pl.BlockSpec((None, tr, c), lambda i, chip_ref: (chip_ref[0], i, 0)),
                      pl.BlockSpec((3, tr, c), lambda i, chip_ref: (0, i, 0))],
            out_specs=pl.BlockSpec((tr, c), lambda i, chip_ref: (i, 0))),
        out_shape=jax.ShapeDtypeStruct((r, c), F32), compiler_params=_params("parallel"),
    )(chip, grad, land)


def _allreduce_small(packed, *, name, after=None):
    r, d = packed.shape
    n_dev = 8

    def body(src_ref, out_ref, buf, send, recv):
        x, y, c = _place()
        me = 4 * x + 2 * y + c
        started = []
        for p in range(1, n_dev):
            rc = pltpu.make_async_remote_copy(
                src_ref=src_ref, dst_ref=buf.at[me], send_sem=send.at[p - 1], recv_sem=recv.at[p - 1],
                device_id=(x ^ (p >> 2), y ^ ((p >> 1) & 1), c ^ (p & 1)), device_id_type=MESH)
            rc.start()
            started.append(rc)
        buf[me] = src_ref[...]
        for rc in started:
            rc.wait()
        total = buf[0]
        for s in range(1, n_dev):
            total = total + buf[s]
        out_ref[...] = total

    vmem = pl.BlockSpec(memory_space=pltpu.VMEM)
    body, more_specs, more_args = _ordered(body, 1, after)
    return pl.pallas_call(
        body, name=name, in_specs=[vmem] + more_specs, out_specs=vmem, out_shape=jax.ShapeDtypeStruct((r, d), F32),
        scratch_shapes=[pltpu.VMEM((n_dev, r, d), F32), pltpu.SemaphoreType.DMA((n_dev - 1,)),
                        pltpu.SemaphoreType.DMA((n_dev - 1,))],
    )(packed, *more_args)


def _packed_rows(size, d):
    return -(-size // (8 * d)) * 8


def _pack_rows(arrays, d):
    rows = []
    for arr in arrays:
        flat = arr.reshape(-1).astype(F32)
        n = _packed_rows(flat.shape[0], d)
        rows.append(jnp.pad(flat, (0, n * d - flat.shape[0])).reshape(n, d))
    return jnp.concatenate(rows, axis=0)


def _unpack_rows(packed, shapes, d):
    out, row = [], 0
    for shape in shapes:
        size = math.prod(shape)
        n = _packed_rows(size, d)
        out.append(packed[row:row + n].reshape(-1)[:size].reshape(shape))
        row += n
    return out


SMALL = ("norm1_g", "gate_b", "conv_b", "conv_norm_g", "q_norm_g", "k_norm_g", "norm2_g", "ffn_conv_b")
LARGE = ("w_in", "w_conv_out", "w_attn_out", "w_out", "w_up", "w_down")
WEIGHTS = ("norm1_g", "w_in", "gate_b", "conv_w", "conv_b", "conv_norm_g", "w_conv_out", "q_norm_g", "k_norm_g",
           "w_attn_out", "w_out", "norm2_g", "w_up", "ffn_conv_w", "ffn_conv_b", "w_down")


def _head_ones(dims):
    a = dims.n_heads * dims.head_dim
    head = jnp.arange(a, dtype=jnp.int32) // dims.head_dim
    return (head[:, None] == head[None, :]).astype(BF16)


def _after(vec, token):
    return vec if token is None else vec + token[0:1, 0:1]


def _local_step(dims, x, target, small, first_weights, other_weights, send_grad):
    d, f, heads = dims.d_model, dims.d_ff, dims.n_heads
    small = dict(small)
    row = lambda name: small[name].reshape(1, -1)
    head_sum = _head_sum_matrix(dims)
    head_spread = jnp.transpose(head_sum)
    ones = (head_sum, head_spread)
    gq = jnp.tile(row("q_norm_g"), (1, heads))
    gk = jnp.tile(row("k_norm_g"), (1, heads))
    one_shard = lambda w: w.reshape(1, -1, w.shape[-1])

    h = _rmsnorm_fwd(x, row("norm1_g"), name="norm1")
    full = first_weights(h)
    w_in = full["w_in"]
    conv_w = jnp.pad(full["conv_w"], ((0, CONV_HALO - dims.conv_width), (0, 0)))
    ffn_w = jnp.pad(full["ffn_conv_w"], ((0, FFN_HALO - dims.ffn_conv_width), (0, 0)))
    z = _mm_nn(h, w_in, out_dtype=BF16, after=full.get("token"), tm=2048, tn=1792, name="in_proj")
    a1, a3 = _conv_branch_fwd(z, conv_w, row("conv_b"), row("conv_norm_g"), dims, name="conv_branch")
    qkv = _qkv_layouts_fwd(z, gq, gk, ones, dims, name="qk_norm")
    per_group = {dil: _attn_fwd(*qkv[dil], dims, dil, name=f"attn_fwd_d{dil}") for dil in DILATIONS}
    o, lse = _attn_combine(per_group, head_spread, dims, name="attn_combine")
    full = other_weights(o)
    w_up = full["w_up"]
    w_co, w_ao, w_o, w_dn = (one_shard(full[k]) for k in ("w_conv_out", "w_attn_out", "w_out", "w_down"))
    ya = _mm_nn(a3, w_co, out_dtype=F32, name="conv_out_proj")
    yb = _mm_nn(o, w_ao, out_dtype=F32, name="attn_out_proj")
    mixed = _mix_fwd(ya, yb, z, row("gate_b"), dims, name="gate_mix")
    x1, h2 = _proj_residual_norm(mixed, w_o, x, row("norm2_g"), name="out_proj_norm2")
    up = _mm_nn(h2, w_up, out_dtype=F32, tm=2048, name="up_proj")
    act = _ffn_act_fwd(up, ffn_w, row("ffn_conv_b"), dims, name="ffn_act")
    dy, dy_b, loss = _proj_residual_loss(act, w_dn, x1, target, tm=512, name="down_proj_loss")

    grads = {}

    def large(name, g):
        grads[name], g_bf16 = g
        return send_grad(name, g_bf16)

    sent = large("w_down", _mm_tn(act, dy_b, n_shards=1, name="dw_down"))
    dact = _mm_nt(dy_b, w_dn, out_dtype=BF16, after=sent, name="d_act")
    dup, dfw, dfb = _ffn_bwd(dact, up, ffn_w, row("ffn_conv_b"), dims, name="ffn_bwd")
    grads["ffn_conv_w"], grads["ffn_conv_b"] = dfw[:dims.ffn_conv_width], dfb
    sent = large("w_up", _mm_tn(h2, dup, n_shards=N_CHIPS, name="dw_up"))
    dh2 = _mm_nt(dup, w_up, out_dtype=F32, after=sent, name="d_h2")
    dx1, dx1_b, grads["norm2_g"] = _rmsnorm_bwd(x1, row("norm2_g"), dh2, dy, want_bf16=True, name="norm2_bwd")
    sent = large("w_out", _mm_tn(mixed, dx1_b, n_shards=1, name="dw_out"))
    dmix = _mm_nt(dx1_b, w_o, out_dtype=F32, after=sent, name="d_mix")
    dya, dyb, dz_gate, grads["gate_b"] = _mix_bwd(dmix, ya, yb, z, row("gate_b"), dims, name="gate_mix_bwd")
    sent = large("w_conv_out", _mm_tn(a3, dya, n_shards=1, name="dw_conv_out"))
    da3 = _mm_nt(dya, w_co, out_dtype=F32, after=sent, name="d_conv_act")
    da1, grads["conv_norm_g"] = _conv_norm_bwd(da3, a1, row("conv_norm_g"), name="conv_norm_bwd")
    dz_glu, dcw, grads["conv_b"] = _conv_branch_bwd(da1, z, conv_w, dims, name="conv_branch_bwd")
    grads["conv_w"] = dcw[:dims.conv_width]
    sent = large("w_attn_out", _mm_tn(o, dyb, n_shards=1, name="dw_attn_out"))
    do = _mm_nt(dyb, w_ao, out_dtype=BF16, after=sent, name="d_attn")
    dos, deltas = _attn_bwd_prep(do, o, head_sum, dims, name="attn_bwd_prep")
    dqkv = {dil: _attn_bwd(*qkv[dil], dos[dil], lse[dil], deltas[dil], dims, dil, name=f"attn_bwd_d{dil}")
            for dil in DILATIONS}
    dz_qkv, dgq, dgk = _qkv_layouts_bwd(z, dqkv, gq, gk, ones, dims, name="qk_norm_bwd")
    grads["q_norm_g"] = dgq.reshape(heads, dims.head_dim).sum(axis=0)
    grads["k_norm_g"] = dgk.reshape(heads, dims.head_dim).sum(axis=0)
    dz = jnp.concatenate([dz_glu, dz_qkv, dz_gate], axis=1)
    sent = large("w_in", _mm_tn(h, dz, n_shards=N_CHIPS, name="dw_in"))
    dh = _mm_nt(dz, w_in, out_dtype=F32, after=sent, name="d_h")
    dx, grads["norm1_g"] = _rmsnorm_bwd(x, row("norm1_g"), dh, dx1, want_bf16=False, name="norm1_bwd")
    return loss, dx, grads


def _step(dims, x, target, w, m, v):
    d = dims.d_model
    t = dims.tokens
    sq = lambda a: a.reshape(a.shape[1:])
    w2, m2, v2 = ({k: sq(a) for k, a in grp.items()} for grp in (w, m, v))

    conv_pad = jnp.pad(w2["conv_w"], ((0, CONV_HALO - dims.conv_width), (0, 0)))
    ffn_pad = jnp.pad(w2["ffn_conv_w"], ((0, FFN_HALO - dims.ffn_conv_width), (0, 0)))
    gathered_names = LARGE + ("conv_w", "ffn_conv_w")
    lands = dict(zip(gathered_names, _cast_to_lands([w2[k] for k in LARGE] + [conv_pad, ffn_pad],
                                                   [BF16] * len(LARGE) + [F32, F32], name="cast_weights")))
    first_names = ("w_in", "conv_w", "ffn_conv_w")
    other_names = tuple(k for k in gathered_names if k not in first_names)
    first = _gather_start([lands[k] for k in first_names], x, halved=(0,), name="gather_start_first")
    other = []
    cols = lambda g, rows: jnp.moveaxis(g, 0, 1).reshape(g.shape[1], -1)[:rows]

    def first_weights(after):
        got = dict(zip(first_names, _gather_wait(*first[:3], after, halved=(0,), name="gather_wait_first")))
        got["w_in"] = _forward_to_sibling(got["w_in"], name="forward_w_in")
        other.extend(_gather_start([lands[k] for k in other_names], got["w_in"], name="gather_start_other"))
        got["conv_w"] = cols(got["conv_w"], dims.conv_width)
        got["ffn_conv_w"] = cols(got["ffn_conv_w"], dims.ffn_conv_width)
        got["token"] = other[3]
        return got

    def other_weights(after):
        return dict(zip(other_names, _gather_wait(*other[:3], after, name="gather_wait_other")))

    started = {}

    def send_grad(name, g):
        send, recv, g_thru, land, token = _scatter_start(g.reshape(N_CHIPS, -1, g.shape[-1]), name=f"scatter_start_{name}")
        started[name] = (send, recv, g_thru, land)
        return token

    small = {k: w2[k] for k in SMALL}
    small["norm1_g"] = _after(small["norm1_g"].reshape(1, -1), first[3])
    loss, dx, grads = _local_step(dims, x.reshape(t, d), target.reshape(t, d), small, first_weights, other_weights, send_grad)

    def my_sums(names, after, tag):
        arrived = _scatter_wait([started[k] for k in names], after, name=f"scatter_wait_{tag}")
        blocks = [grads[k].reshape(N_CHIPS, -1, grads[k].shape[-1]) for k in names]
        return [_sum_received(g, land, name=f"sum_{k}") for k, g, (_, land) in zip(names, blocks, arrived)]

    def updates(names, mine, theirs):
        return {k: _adamw(w2[k], [a, b], m2[k], v2[k], name=f"adamw_{k}") for k, a, b in zip(names, mine, theirs)}

    small_names = SMALL + ("conv_w", "ffn_conv_w")
    packed = _pack_rows([grads[k] for k in small_names] + [loss[0, 0]], d)
    reducing = _allreduce_start(packed, name="allreduce_start")
    others = [k for k in LARGE if k != "w_in"]
    mine_others = my_sums(others, [dx, reducing[4]], "others")
    swapping_others = _swap_start(mine_others, name="swap_start_others")
    mine_w_in = my_sums(["w_in"], swapping_others[4], "w_in")
    swapping_w_in = _swap_start(mine_w_in, name="swap_start_w_in")
    out = updates(others, *_swap_wait(swapping_others, swapping_w_in[4], name="swap_wait_others"))
    last_updates = [out[k][1] for k in others]
    reduced = _sum_devices(*_allreduce_wait(reducing, last_updates, name="allreduce_wait"), name="allreduce_sum")
    shapes = [grads[k].shape for k in small_names] + [()]
    *small_g, loss_total = _unpack_rows(reduced, shapes, d)
    small_g = dict(zip(small_names, small_g))
    chip = 2 * lax.axis_index("x") + lax.axis_index("y")
    for k in ("conv_w", "ffn_conv_w"):
        width = w2[k].shape[1]
        small_g[k] = lax.dynamic_slice_in_dim(small_g[k], chip * width, width, axis=1)

    small_shapes = [w2[k].shape for k in small_names]
    pack = lambda grp: _pack_rows([grp[k] for k in small_names], d)
    results = _adamw(pack(w2), [pack(small_g)], pack(m2), pack(v2), name="adamw_small")
    unpacked = [_unpack_rows(r, small_shapes, d) for r in results]
    out.update({k: tuple(u[i] for u in unpacked) for i, k in enumerate(small_names)})
    out.update(updates(["w_in"], *_swap_wait(swapping_w_in, results[1], name="swap_wait_w_in")))

    lead =lambda a: a.reshape((1,) + a.shape)
    ordered = [[lead(out[k][j].reshape(w2[k].shape)) for k in WEIGHTS] for j in range(4)]
    return (loss_total, dx.reshape(x.shape), *ordered[0], *ordered[1], *ordered[2], *ordered[3])


def kernel(x, norm1_g, w_in, gate_b, conv_w, conv_b, conv_norm_g, w_conv_out, q_norm_g, k_norm_g, w_attn_out, w_out, norm2_g, w_up, ffn_conv_w, ffn_conv_b, w_down, loss_target, m_norm1_g, m_w_in, m_gate_b, m_conv_w, m_conv_b, m_conv_norm_g, m_w_conv_out, m_q_norm_g, m_k_norm_g, m_w_attn_out, m_w_out, m_norm2_g, m_w_up, m_ffn_conv_w, m_ffn_conv_b, m_w_down, v_norm1_g, v_w_in, v_gate_b, v_conv_w, v_conv_b, v_conv_norm_g, v_w_conv_out, v_q_norm_g, v_k_norm_g, v_w_attn_out, v_w_out, v_norm2_g, v_w_up, v_ffn_conv_w, v_ffn_conv_b, v_w_down):
    w = dict(zip(WEIGHTS, (norm1_g, w_in, gate_b, conv_w, conv_b, conv_norm_g, w_conv_out, q_norm_g, k_norm_g,
                           w_attn_out, w_out, norm2_g, w_up, ffn_conv_w, ffn_conv_b, w_down)))
    m = dict(zip(WEIGHTS, (m_norm1_g, m_w_in, m_gate_b, m_conv_w, m_conv_b, m_conv_norm_g, m_w_conv_out, m_q_norm_g,
                           m_k_norm_g, m_w_attn_out, m_w_out, m_norm2_g, m_w_up, m_ffn_conv_w, m_ffn_conv_b, m_w_down)))
    v = dict(zip(WEIGHTS, (v_norm1_g, v_w_in, v_gate_b, v_conv_w, v_conv_b, v_conv_norm_g, v_w_conv_out, v_q_norm_g,
                           v_k_norm_g, v_w_attn_out, v_w_out, v_norm2_g, v_w_up, v_ffn_conv_w, v_ffn_conv_b, v_w_down)))
    dims = Dims(d_model=x.shape[-1], batch_local=x.shape[0], seq=x.shape[1], d_ff=w_down.shape[1] * N_CHIPS)
    return _step(dims, x, loss_target, w, m, v)
```

```python
import functools
import math
from typing import NamedTuple

import jax
import jax.numpy as jnp
from jax import lax
from jax.experimental import pallas as pl
from jax.experimental.pallas import tpu as pltpu

F32 = jnp.float32
BF16 = jnp.bfloat16

RMS_EPS = 1e-6
MASKED_SCORE = -1e30
ATTN_BLOCK = 128
DILATIONS = (1, 4, 16)
CONV_HALO = 32
FFN_HALO = 8
ADAM_LR, ADAM_B1, ADAM_B2, ADAM_EPS, ADAM_WD, ADAM_STEP = 0.001, 0.9, 0.999, 1e-08, 0.01, 10
V7X_VMEM_LIMIT_BYTES = 56 * 2 ** 20
N_CHIPS = 4
MESH = pl.DeviceIdType.MESH


class Dims(NamedTuple):
    d_model: int = 1024
    n_heads: int = 16
    head_dim: int = 64
    d_ff: int = 2816
    seq: int = 2048
    batch_local: int = 2
    conv_width: int = 31
    ffn_conv_width: int = 3

    @property
    def tokens(self):
        return self.seq * self.batch_local


def _params(*semantics):
    return pltpu.CompilerParams(dimension_semantics=semantics, vmem_limit_bytes=V7X_VMEM_LIMIT_BYTES)


ANY = pl.BlockSpec(memory_space=pl.ANY)


def _ordered(body, n_inputs, after):
    after = [] if after is None else list(after) if isinstance(after, (list, tuple)) else [after]
    if not after:
        return body, [], []

    def wrapped(*refs):
        return body(*refs[:n_inputs], *refs[n_inputs + len(after):])

    return wrapped, [ANY] * len(after), after


def _pick(n, target, mult=128):
    if n <= target:
        return n
    best = None
    for t in range(mult, target + 1, mult):
        if n % t == 0:
            best = t
    assert best is not None, (n, target, mult)
    return best


def _sigmoid(v):
    return 1.0 / (1.0 + jnp.exp(-v))


def _mm_nn(a, w, *, out_dtype, name, residual=None, after=None, tm=1024, tn=1408, tk=2816):
    m, k = a.shape
    nsh, k2, c = w.shape
    assert k == k2 and a.dtype == BF16 and w.dtype == BF16
    n = nsh * c
    tm, tn, tk = _pick(m, tm, 8), _pick(c, tn), _pick(k, tk)
    nk, cpn = k // tk, c // tn

    def body(*refs):
        if residual is None:
            a_ref, w_ref, o_ref, acc = refs
        else:
            a_ref, w_ref, r_ref, o_ref, acc = refs
        prod = jnp.dot(a_ref[...], w_ref[...], preferred_element_type=F32)

        def finish(total):
            if residual is not None:
                total = total + r_ref[...]
            o_ref[...] = total.astype(out_dtype)

        if nk == 1:
            finish(prod)
        else:
            kk = pl.program_id(2)

            @pl.when(kk == 0)
            def _():
                acc[...] = prod

            @pl.when(kk > 0)
            def _():
                acc[...] += prod

            @pl.when(kk == nk - 1)
            def _():
                finish(acc[...])

    in_specs = [pl.BlockSpec((tm, tk), lambda i, j, kk: (i, kk)),
                pl.BlockSpec((None, tk, tn), lambda i, j, kk: (j // cpn, kk, j % cpn))]
    args = [a, w]
    if residual is not None:
        in_specs.append(pl.BlockSpec((tm, tn), lambda i, j, kk: (i, j)))
        args.append(residual)
    body, more_specs, more_args = _ordered(body, len(args), after)
    return pl.pallas_call(
        body, name=name, grid=(m // tm, n // tn, nk),
        in_specs=in_specs + more_specs, out_specs=pl.BlockSpec((tm, tn), lambda i, j, kk: (i, j)),
        out_shape=jax.ShapeDtypeStruct((m, n), out_dtype),
        scratch_shapes=[pltpu.VMEM((tm, tn) if nk > 1 else (8, 128), F32)],
        compiler_params=_params("parallel", "parallel", "arbitrary"),
    )(*args, *more_args)


def _proj_residual_norm(a, w, residual, g, *, name, tm=1024):
    m, k = a.shape
    _, k2, n = w.shape
    assert w.shape[0] == 1 and k == k2 and a.dtype == BF16 and w.dtype == BF16
    tm = _pick(m, tm, 8)

    def body(a_ref, w_ref, r_ref, g_ref, y_ref, h_ref):
        y = r_ref[...] + jnp.dot(a_ref[...], w_ref[...], preferred_element_type=F32)
        y_ref[...] = y
        h_ref[...] = (y * lax.rsqrt(jnp.mean(y * y, axis=-1, keepdims=True) + RMS_EPS) * g_ref[...]).astype(BF16)

    rows = lambda width: pl.BlockSpec((tm, width), lambda i: (i, 0))
    return pl.pallas_call(
        body, name=name, grid=(m // tm,),
        in_specs=[rows(k), pl.BlockSpec((None, k, n), lambda i: (0, 0, 0)), rows(n), pl.BlockSpec((1, n), lambda i: (0, 0))],
        out_specs=[rows(n), rows(n)],
        out_shape=[jax.ShapeDtypeStruct((m, n), F32), jax.ShapeDtypeStruct((m, n), BF16)],
        compiler_params=_params("parallel"),
    )(a, w, residual, g)


def _proj_residual_loss(a, w, residual, target, *, name, tm=1024):
    m, k = a.shape
    _, k2, n = w.shape
    assert w.shape[0] == 1 and k == k2 and a.dtype == BF16 and w.dtype == BF16
    tm = _pick(m, tm, 8)

    def body(a_ref, w_ref, r_ref, t_ref, dy_ref, dyb_ref, loss_ref):
        err = r_ref[...] + jnp.dot(a_ref[...], w_ref[...], preferred_element_type=F32) - t_ref[...]
        dy = err * (1.0 / n)
        dy_ref[...] = dy
        dyb_ref[...] = dy.astype(BF16)
        part = jnp.sum(jnp.sum(err * err, axis=-1, keepdims=True), axis=0, keepdims=True) * (0.5 / n)
        _accumulate(loss_ref, jnp.broadcast_to(part, (8, 128)), pl.program_id(0) == 0)

    rows = lambda width: pl.BlockSpec((tm, width), lambda i: (i, 0))
    return pl.pallas_call(
        body, name=name, grid=(m // tm,),
        in_specs=[rows(k), pl.BlockSpec((None, k, n), lambda i: (0, 0, 0)), rows(n), rows(n)],
        out_specs=[rows(n), rows(n), pl.BlockSpec((8, 128), lambda i: (0, 0))],
        out_shape=[jax.ShapeDtypeStruct((m, n), F32), jax.ShapeDtypeStruct((m, n), BF16),
                   jax.ShapeDtypeStruct((8, 128), F32)],
        compiler_params=_params("arbitrary"),
    )(a, w, residual, target)


def _mm_nt(a, w, *, out_dtype, name, after=None, tm=1024, tn=1408, tk=1792):
    m, k = a.shape
    nsh, r, c = w.shape
    assert k == nsh * c and a.dtype == BF16 and w.dtype == BF16
    tm, tn, tk = _pick(m, tm, 8), _pick(r, tn), _pick(c, tk)
    nk, cpk = k // tk, c // tk

    def body(a_ref, w_ref, o_ref, acc):
        prod = lax.dot_general(a_ref[...], w_ref[...], (((1,), (1,)), ((), ())), preferred_element_type=F32)
        if nk == 1:
            o_ref[...] = prod.astype(out_dtype)
        else:
            kk = pl.program_id(2)

            @pl.when(kk == 0)
            def _():
                acc[...] = prod

            @pl.when(kk > 0)
            def _():
                acc[...] += prod

            @pl.when(kk == nk - 1)
            def _():
                o_ref[...] = acc[...].astype(out_dtype)

    body, more_specs, more_args = _ordered(body, 2, after)
    return pl.pallas_call(
        body, name=name, grid=(m // tm, r // tn, nk),
        in_specs=[pl.BlockSpec((tm, tk), lambda i, j, kk: (i, kk)),
                  pl.BlockSpec((None, tn, tk), lambda i, j, kk: (kk // cpk, j, kk % cpk))] + more_specs,
        out_specs=pl.BlockSpec((tm, tn), lambda i, j, kk: (i, j)),
        out_shape=jax.ShapeDtypeStruct((m, r), out_dtype),
        scratch_shapes=[pltpu.VMEM((tm, tn) if nk > 1 else (8, 128), F32)],
        compiler_params=_params("parallel", "parallel", "arbitrary"),
    )(a, w, *more_args)


MM_TN_VMEM_BYTES = 44 * 2 ** 20


def _mm_tn(a, b, *, n_shards, name, tm=1408, tn=1408):
    t, m = a.shape
    t2, n = b.shape
    assert t == t2 and a.dtype == BF16 and b.dtype == BF16
    c = n // n_shards
    tm, tn = _pick(m, tm), _pick(c, tn)
    fixed = 2 * tm * tn * 6
    if 4 * t * (tm + tn) + fixed <= MM_TN_VMEM_BYTES:
        tk = t
    else:
        tk = _pick(t, (MM_TN_VMEM_BYTES - fixed - 4 * tm * tn) // (4 * (tm + tn)), 8)
    nk, cpn = t // tk, c // tn

    def body(a_ref, b_ref, o_ref, ob_ref, acc):
        kk = pl.program_id(2)
        prod = lax.dot_general(a_ref[...], b_ref[...], (((0,), (0,)), ((), ())), preferred_element_type=F32)

        def finish(total):
            o_ref[...] = total
            ob_ref[...] = total.astype(BF16)

        if nk == 1:
            finish(prod)
        else:
            @pl.when(kk == 0)
            def _():
                acc[...] = prod

            @pl.when(kk > 0)
            def _():
                acc[...] += prod

            @pl.when(kk == nk - 1)
            def _():
                finish(acc[...])

    out_spec = pl.BlockSpec((None, tm, tn), lambda i, j, kk: (j // cpn, i, j % cpn))
    return pl.pallas_call(
        body, name=name, grid=(m // tm, n // tn, nk),
        in_specs=[pl.BlockSpec((tk, tm), lambda i, j, kk: (kk, i)),
                  pl.BlockSpec((tk, tn), lambda i, j, kk: (kk, j))],
        out_specs=[out_spec, out_spec],
        out_shape=[jax.ShapeDtypeStruct((n_shards, m, c), F32), jax.ShapeDtypeStruct((n_shards, m, c), BF16)],
        scratch_shapes=[pltpu.VMEM((tm, tn) if nk > 1 else (8, 128), F32)],
        compiler_params=_params("parallel", "parallel", "arbitrary"),
    )(a, b)


def _row_spec(tr, width, col=0):
    return pl.BlockSpec((tr, width), lambda i, col=col: (i, col))


def _vec_spec(width, col=0):
    return pl.BlockSpec((1, width), lambda i, col=col: (0, col))


def _accumulate(ref, value, first):
    @pl.when(first)
    def _():
        ref[...] = value

    @pl.when(jnp.logical_not(first))
    def _():
        ref[...] += value


def _rmsnorm_fwd(x, g, *, name, tr=512):
    t, d = x.shape
    tr = _pick(t, tr, 8)

    def body(x_ref, g_ref, o_ref):
        xv = x_ref[...]
        r = lax.rsqrt(jnp.mean(xv * xv, axis=-1, keepdims=True) + RMS_EPS)
        o_ref[...] = (xv * r * g_ref[...]).astype(BF16)

    return pl.pallas_call(
        body, name=name, grid=(t // tr,),
        in_specs=[_row_spec(tr, d), _vec_spec(d)], out_specs=_row_spec(tr, d),
        out_shape=jax.ShapeDtypeStruct((t, d), BF16), compiler_params=_params("parallel"),
    )(x, g)


def _rmsnorm_bwd(x, g, dy, dres, *, name, want_bf16, tr=512):
    t, d = x.shape
    tr = _pick(t, tr, 8)

    def body(x_ref, g_ref, dy_ref, dres_ref, *outs):
        dx_ref, dg_ref = outs[0], outs[-1]
        xv, dyv = x_ref[...], dy_ref[...].astype(F32)
        r = lax.rsqrt(jnp.mean(xv * xv, axis=-1, keepdims=True) + RMS_EPS)
        gy = dyv * g_ref[...]
        dx = dres_ref[...] + r * gy - xv * (r * r * r) * jnp.mean(xv * gy, axis=-1, keepdims=True)
        dx_ref[...] = dx
        if want_bf16:
            outs[1][...] = dx.astype(BF16)
        _accumulate(dg_ref, jnp.sum(dyv * xv * r, axis=0, keepdims=True), pl.program_id(0) == 0)

    out_shape = [jax.ShapeDtypeStruct((t, d), F32)]
    out_specs = [_row_spec(tr, d)]
    if want_bf16:
        out_shape.append(jax.ShapeDtypeStruct((t, d), BF16))
        out_specs.append(_row_spec(tr, d))
    out_shape.append(jax.ShapeDtypeStruct((1, d), F32))
    out_specs.append(_vec_spec(d))
    return pl.pallas_call(
        body, name=name, grid=(t // tr,),
        in_specs=[_row_spec(tr, d), _vec_spec(d), _row_spec(tr, d), _row_spec(tr, d)],
        out_specs=out_specs, out_shape=out_shape, compiler_params=_params("arbitrary"),
    )(x, g, dy, dres)


def _head_mean(v, ones_ref, head_dim):
    hi = v.astype(BF16)
    lo = (v - hi.astype(F32)).astype(BF16)
    e = ones_ref[...]
    total = jnp.dot(hi, e, preferred_element_type=F32) + jnp.dot(lo, e, preferred_element_type=F32)
    return total * (1.0 / head_dim)


def _qkv_fwd(z, gq, gk, head_ones, dims, *, name, tr=256):
    t = z.shape[0]
    a = dims.n_heads * dims.head_dim
    tr = _pick(t, tr, 8)
    q_scale = dims.head_dim ** -0.5

    def body(q_ref, k_ref, v_ref, gq_ref, gk_ref, e_ref, qo_ref, ko_ref, vo_ref):
        qv, kv = q_ref[...], k_ref[...]
        rq = lax.rsqrt(_head_mean(qv * qv, e_ref, dims.head_dim) + RMS_EPS)
        rk = lax.rsqrt(_head_mean(kv * kv, e_ref, dims.head_dim) + RMS_EPS)
        qo_ref[...] = (qv * rq * gq_ref[...] * q_scale).astype(BF16)
        ko_ref[...] = (kv * rk * gk_ref[...]).astype(BF16)
        vo_ref[...] = v_ref[...].astype(BF16)

    return pl.pallas_call(
        body, name=name, grid=(t // tr,),
        in_specs=[_row_spec(tr, a, 2), _row_spec(tr, a, 3), _row_spec(tr, a, 4), _vec_spec(a), _vec_spec(a),
                  pl.BlockSpec((a, a), lambda i: (0, 0))],
        out_specs=[_row_spec(tr, a)] * 3, out_shape=[jax.ShapeDtypeStruct((t, a), BF16)] * 3,
        compiler_params=_params("parallel"),
    )(z, z, z, gq, gk, head_ones)


def _qkv_bwd(z, dqs, dks, dvs, gq, gk, head_ones, dims, *, name, tr=256):
    t = z.shape[0]
    a = dims.n_heads * dims.head_dim
    tr = _pick(t, tr, 8)
    q_scale = dims.head_dim ** -0.5
    ng = len(dqs)

    def body(*refs):
        q_ref, k_ref = refs[:2]
        dq_refs, dk_refs, dv_refs = refs[2:2 + ng], refs[2 + ng:2 + 2 * ng], refs[2 + 2 * ng:2 + 3 * ng]
        gq_ref, gk_ref, e_ref = refs[2 + 3 * ng:5 + 3 * ng]
        dz_ref, dgq_ref, dgk_ref = refs[5 + 3 * ng:]
        first = pl.program_id(0) == 0

        def norm_bwd(x_ref, d_refs, g_ref, scale, col, dg_ref):
            xv = x_ref[...]
            dy = sum(r[...] for r in d_refs) * scale
            r = lax.rsqrt(_head_mean(xv * xv, e_ref, dims.head_dim) + RMS_EPS)
            gy = dy * g_ref[...]
            dx = r * gy - xv * (r * r * r) * _head_mean(xv * gy, e_ref, dims.head_dim)
            dz_ref[:, col * a:(col + 1) * a] = dx.astype(BF16)
            _accumulate(dg_ref, jnp.sum(dy * xv * r, axis=0, keepdims=True), first)

        norm_bwd(q_ref, dq_refs, gq_ref, q_scale, 0, dgq_ref)
        norm_bwd(k_ref, dk_refs, gk_ref, 1.0, 1, dgk_ref)
        dz_ref[:, 2 * a:3 * a] = sum(r[...] for r in dv_refs).astype(BF16)

    in_specs = ([_row_spec(tr, a, 2), _row_spec(tr, a, 3)] + [_row_spec(tr, a)] * (3 * ng)
                + [_vec_spec(a), _vec_spec(a), pl.BlockSpec((a, a), lambda i: (0, 0))])
    return pl.pallas_call(
        body, name=name, grid=(t // tr,), in_specs=in_specs,
        out_specs=[_row_spec(tr, 3 * a), _vec_spec(a), _vec_spec(a)],
        out_shape=[jax.ShapeDtypeStruct((t, 3 * a), BF16)] + [jax.ShapeDtypeStruct((1, a), F32)] * 2,
        compiler_params=_params("arbitrary"),
    )(z, z, *dqs, *dks, *dvs, gq, gk, head_ones)


CONV_ROWS = 16


def _seq_specs(dims, ts, width, halo, col, *, nxt=False):
    nst, per = dims.seq // ts, ts // halo
    last = dims.tokens // halo - 1
    cur = pl.BlockSpec((ts, width), lambda b, i: (b * nst + i, col))
    if nxt:
        edge = pl.BlockSpec((halo, width), lambda b, i: (jnp.minimum((b * nst + i + 1) * per, last), col))
    else:
        edge = pl.BlockSpec((halo, width), lambda b, i: (jnp.maximum((b * nst + i) * per - 1, 0), col))
    return cur, edge


SUBLANES = 8


def _shifted_copies(buf, shifted):
    rows = shifted.shape[1]
    for s in range(1, SUBLANES):
        shifted[s - 1] = buf[pl.ds(s, rows), :]


def _window(buf, shifted, start, size):
    a, s = divmod(start, SUBLANES)
    src = buf if s == 0 else shifted.at[s - 1]
    return src[pl.ds(SUBLANES * a, size), :]


def _conv_branch_fwd(z, w, b, g, dims, *, name, ts=128):
    t, c, kw = z.shape[0], dims.d_model, dims.conv_width
    base = CONV_HALO - (kw - 1)

    def body(av_ref, hv_ref, ag_ref, hg_ref, w_ref, b_ref, g_ref, a1_ref, a3_ref, buf, shifted):
        i = pl.program_id(1)
        buf[CONV_HALO:, :] = av_ref[...].astype(F32) * _sigmoid(ag_ref[...].astype(F32))
        buf[0:CONV_HALO, :] = jnp.where(i > 0, hv_ref[...].astype(F32) * _sigmoid(hg_ref[...].astype(F32)), 0.0)
        _shifted_copies(buf, shifted)
        for r0 in range(0, ts, CONV_ROWS):
            acc = jnp.broadcast_to(b_ref[...], (CONV_ROWS, c))
            for k in range(kw):
                acc = acc + w_ref[k:k + 1, :] * _window(buf, shifted, r0 + base + k, CONV_ROWS)
            a1_ref[r0:r0 + CONV_ROWS, :] = acc
            a2 = acc * lax.rsqrt(jnp.mean(acc * acc, axis=-1, keepdims=True) + RMS_EPS) * g_ref[...]
            a3_ref[r0:r0 + CONV_ROWS, :] = (a2 * _sigmoid(a2)).astype(BF16)

    vec = pl.BlockSpec((1, c), lambda b, i: (0, 0))
    out = pl.BlockSpec((ts, c), lambda b, i: (b * (dims.seq // ts) + i, 0))
    return pl.pallas_call(
        body, name=name, grid=(dims.batch_local, dims.seq // ts),
        in_specs=[*_seq_specs(dims, ts, c, CONV_HALO, 0), *_seq_specs(dims, ts, c, CONV_HALO, 1),
                  pl.BlockSpec((CONV_HALO, c), lambda b, i: (0, 0)), vec, vec],
        out_specs=[out, out],
        out_shape=[jax.ShapeDtypeStruct((t, c), F32), jax.ShapeDtypeStruct((t, c), BF16)],
        scratch_shapes=[pltpu.VMEM((CONV_HALO + ts, c), F32),
                        pltpu.VMEM((SUBLANES - 1, CONV_HALO + ts - SUBLANES, c), F32)],
        compiler_params=_params("parallel", "parallel"),
    )(z, z, z, z, w, b, g)


def _conv_norm_bwd(da3, a1, g, *, name, tr=256):
    t, c = a1.shape
    tr = _pick(t, tr, 8)

    def body(d_ref, a_ref, g_ref, o_ref, dg_ref):
        a1v, gv = a_ref[...], g_ref[...]
        r = lax.rsqrt(jnp.mean(a1v * a1v, axis=-1, keepdims=True) + RMS_EPS)
        a2 = a1v * r * gv
        sg = _sigmoid(a2)
        da2 = d_ref[...].astype(F32) * sg * (1.0 + a2 * (1.0 - sg))
        gy = da2 * gv
        o_ref[...] = r * gy - a1v * (r * r * r) * jnp.mean(a1v * gy, axis=-1, keepdims=True)
        _accumulate(dg_ref, jnp.sum(da2 * a1v * r, axis=0, keepdims=True), pl.program_id(0) == 0)

    return pl.pallas_call(
        body, name=name, grid=(t // tr,),
        in_specs=[_row_spec(tr, c), _row_spec(tr, c), _vec_spec(c)],
        out_specs=[_row_spec(tr, c), _vec_spec(c)],
        out_shape=[jax.ShapeDtypeStruct((t, c), F32), jax.ShapeDtypeStruct((1, c), F32)],
        compiler_params=_params("arbitrary"),
    )(da3, a1, g)


def _conv_branch_bwd(da1, z, w, rest_of_dz, dims, *, name, ts=128):
    t, c, kw = z.shape[0], dims.d_model, dims.conv_width
    nst = dims.seq // ts
    base = CONV_HALO - (kw - 1)
    n_rest = len(rest_of_dz)
    total = 2 * c + sum(r.shape[1] for r in rest_of_dz)

    def body(d_ref, dn_ref, av_ref, hv_ref, ag_ref, hg_ref, w_ref, *more):
        rest_refs = more[:n_rest]
        dz_ref, dw_ref, db_ref, abuf, dbuf, ashift, dshift = more[n_rest:]
        col = 2 * c
        for r in rest_refs:
            dz_ref[:, col:col + r.shape[1]] = r[...]
            col += r.shape[1]
        i = pl.program_id(1)
        first = jnp.logical_and(pl.program_id(0) == 0, i == 0)
        abuf[CONV_HALO:, :] = av_ref[...].astype(F32) * _sigmoid(ag_ref[...].astype(F32))
        abuf[0:CONV_HALO, :] = jnp.where(i > 0, hv_ref[...].astype(F32) * _sigmoid(hg_ref[...].astype(F32)), 0.0)
        d1 = d_ref[...]
        dbuf[0:ts, :] = d1
        dbuf[ts:, :] = jnp.where(i < nst - 1, dn_ref[...], 0.0)
        _shifted_copies(abuf, ashift)
        _shifted_copies(dbuf, dshift)

        @pl.when(first)
        def _():
            dw_ref[...] = jnp.zeros_like(dw_ref)
            db_ref[...] = jnp.zeros_like(db_ref)

        db_ref[...] += jnp.sum(d1, axis=0, keepdims=True)
        for k in range(kw):
            dw_ref[k:k + 1, :] += jnp.sum(d1 * _window(abuf, ashift, base + k, ts), axis=0, keepdims=True)
        for r0 in range(0, ts, CONV_ROWS):
            acc = jnp.zeros((CONV_ROWS, c), F32)
            for k in range(kw):
                acc = acc + w_ref[k:k + 1, :] * _window(dbuf, dshift, r0 + (kw - 1) - k, CONV_ROWS)
            av = av_ref[r0:r0 + CONV_ROWS, :].astype(F32)
            sg = _sigmoid(ag_ref[r0:r0 + CONV_ROWS, :].astype(F32))
            dz_ref[r0:r0 + CONV_ROWS, 0:c] = (acc * sg).astype(BF16)
            dz_ref[r0:r0 + CONV_ROWS, c:2 * c] = (acc * av * sg * (1.0 - sg)).astype(BF16)

    cur, nxt = _seq_specs(dims, ts, c, CONV_HALO, 0, nxt=True)
    return pl.pallas_call(
        body, name=name, grid=(dims.batch_local, nst),
        in_specs=[cur, nxt, *_seq_specs(dims, ts, c, CONV_HALO, 0), *_seq_specs(dims, ts, c, CONV_HALO, 1),
                  pl.BlockSpec((CONV_HALO, c), lambda b, i: (0, 0))]
        + [pl.BlockSpec((ts, r.shape[1]), lambda b, i: (b * nst + i, 0)) for r in rest_of_dz],
        out_specs=[pl.BlockSpec((ts, total), lambda b, i: (b * nst + i, 0)),
                   pl.BlockSpec((CONV_HALO, c), lambda b, i: (0, 0)), pl.BlockSpec((1, c), lambda b, i: (0, 0))],
        out_shape=[jax.ShapeDtypeStruct((t, total), BF16), jax.ShapeDtypeStruct((CONV_HALO, c), F32),
                   jax.ShapeDtypeStruct((1, c), F32)],
        scratch_shapes=[pltpu.VMEM((CONV_HALO + ts, c), F32)] * 2
        + [pltpu.VMEM((SUBLANES - 1, CONV_HALO + ts - SUBLANES, c), F32)] * 2,
        compiler_params=_params("arbitrary", "arbitrary"),
    )(da1, da1, z, z, z, z, w, *rest_of_dz)


FFN_ROWS = 16
FFN_COLS = 256


def _ffn_chunks(ts, f):
    cw = _pick(f, FFN_COLS)
    return [(r0, c0, cw) for r0 in range(0, ts, FFN_ROWS) for c0 in range(0, f, cw)]


def _tap_sources(buf, moved, offsets, rows):
    taps, used = [], 0
    for off in offsets:
        if off % SUBLANES:
            moved[used] = buf[pl.ds(off, rows), :]
            taps.append((moved.at[used], 0))
            used += 1
        else:
            taps.append((buf, off))
    return taps


def _moved_copies(offsets):
    return sum(1 for off in offsets if off % SUBLANES)


def _taps_sum(taps, w_ref, init, r0, cols):
    for k, (src, off) in enumerate(taps):
        init = init + w_ref[k:k + 1, cols] * src[pl.ds(off + r0, init.shape[0]), cols]
    return init


def _ffn_bwd(dact, up, w, b, dims, *, name, ts=128):
    t, f, kw = up.shape[0], dims.d_ff, dims.ffn_conv_width
    nst = dims.seq // ts
    fwd_offsets = [FFN_HALO - (kw - 1) + k for k in range(kw)]
    bwd_offsets = [(kw - 1) - k for k in range(kw)]
    dact_halo = 2 * FFN_HALO

    def body(d_ref, dn_ref, up_ref, hp_ref, hn_ref, w_ref, b_ref, o_ref, dw_ref, db_ref, buf, moved, dbuf, dmoved):
        i = pl.program_id(1)
        first = jnp.logical_and(pl.program_id(0) == 0, i == 0)
        more = i < nst - 1
        buf[0:FFN_HALO, :] = jnp.where(i > 0, hp_ref[...], 0.0)
        buf[FFN_HALO:FFN_HALO + ts, :] = up_ref[...]
        buf[FFN_HALO + ts:, :] = hn_ref[...]
        taps = _tap_sources(buf, moved, fwd_offsets, ts + FFN_HALO)

        def du_chunk(r0, rows, c0, cw, d):
            vcols, gcols = slice(c0, c0 + cw), slice(f + c0, f + c0 + cw)
            uv = _taps_sum(taps, w_ref, jnp.broadcast_to(b_ref[:, vcols], (rows, cw)), r0, vcols)
            ug = _taps_sum(taps, w_ref, jnp.broadcast_to(b_ref[:, gcols], (rows, cw)), r0, gcols)
            sg = _sigmoid(ug)
            dbuf[r0:r0 + rows, vcols] = d * ug * sg
            dbuf[r0:r0 + rows, gcols] = d * uv * sg * (1.0 + ug * (1.0 - sg))

        for r0, c0, cw in _ffn_chunks(ts, f):
            du_chunk(r0, FFN_ROWS, c0, cw, d_ref[r0:r0 + FFN_ROWS, c0:c0 + cw].astype(F32))
        for _, c0, cw in _ffn_chunks(FFN_ROWS, f):
            d_next = dn_ref[:, c0:c0 + cw].astype(F32)[0:FFN_HALO]
            du_chunk(ts, FFN_HALO, c0, cw, jnp.where(more, d_next, 0.0))

        @pl.when(first)
        def _():
            dw_ref[...] = jnp.zeros_like(dw_ref)
            db_ref[...] = jnp.zeros_like(db_ref)

        du = dbuf[0:ts, :]
        db_ref[...] += jnp.sum(du, axis=0, keepdims=True)
        for k, (src, off) in enumerate(taps):
            dw_ref[k:k + 1, :] += jnp.sum(du * src[pl.ds(off, ts), :], axis=0, keepdims=True)

        dtaps = _tap_sources(dbuf, dmoved, bwd_offsets, ts)
        for r0, c0, cw in _ffn_chunks(ts, 2 * f):
            cols = slice(c0, c0 + cw)
            o_ref[r0:r0 + FFN_ROWS, cols] = _taps_sum(dtaps, w_ref, jnp.zeros((FFN_ROWS, cw), F32), r0, cols).astype(BF16)

    up_cur, up_prev = _seq_specs(dims, ts, 2 * f, FFN_HALO, 0)
    _, up_next = _seq_specs(dims, ts, 2 * f, FFN_HALO, 0, nxt=True)
    d_cur, d_next = _seq_specs(dims, ts, f, dact_halo, 0, nxt=True)
    full = lambda rows: pl.BlockSpec((rows, 2 * f), lambda b_, i: (0, 0))
    return pl.pallas_call(
        body, name=name, grid=(dims.batch_local, nst),
        in_specs=[d_cur, d_next, up_cur, up_prev, up_next, full(FFN_HALO), full(1)],
        out_specs=[pl.BlockSpec((ts, 2 * f), lambda b_, i: (b_ * nst + i, 0)), full(FFN_HALO), full(1)],
        out_shape=[jax.ShapeDtypeStruct((t, 2 * f), BF16), jax.ShapeDtypeStruct((FFN_HALO, 2 * f), F32),
                   jax.ShapeDtypeStruct((1, 2 * f), F32)],
        scratch_shapes=[pltpu.VMEM((ts + 2 * FFN_HALO, 2 * f), F32),
                        pltpu.VMEM((_moved_copies(fwd_offsets), ts + FFN_HALO, 2 * f), F32),
                        pltpu.VMEM((ts + FFN_HALO, 2 * f), F32),
                        pltpu.VMEM((_moved_copies(bwd_offsets), ts, 2 * f), F32)],
        compiler_params=_params("arbitrary", "arbitrary"),
    )(dact, dact, up, up, up, w, b)


def _ffn_act_fwd(up, w, b, dims, *, name, ts=128):
    t, f, kw = up.shape[0], dims.d_ff, dims.ffn_conv_width
    offsets = [FFN_HALO - (kw - 1) + k for k in range(kw)]

    def body(up_ref, h_ref, w_ref, b_ref, o_ref, buf, moved):
        buf[FFN_HALO:, :] = up_ref[...]
        buf[0:FFN_HALO, :] = jnp.where(pl.program_id(1) > 0, h_ref[...], 0.0)
        taps = _tap_sources(buf, moved, offsets, ts)
        for r0, c0, cw in _ffn_chunks(ts, f):
            vcols, gcols = slice(c0, c0 + cw), slice(f + c0, f + c0 + cw)
            uv = _taps_sum(taps, w_ref, jnp.broadcast_to(b_ref[:, vcols], (FFN_ROWS, cw)), r0, vcols)
            ug = _taps_sum(taps, w_ref, jnp.broadcast_to(b_ref[:, gcols], (FFN_ROWS, cw)), r0, gcols)
            o_ref[r0:r0 + FFN_ROWS, vcols] = (ug * _sigmoid(ug) * uv).astype(BF16)

    full = lambda rows: pl.BlockSpec((rows, 2 * f), lambda b_, i: (0, 0))
    return pl.pallas_call(
        body, name=name, grid=(dims.batch_local, dims.seq // ts),
        in_specs=[*_seq_specs(dims, ts, 2 * f, FFN_HALO, 0), full(FFN_HALO), full(1)],
        out_specs=pl.BlockSpec((ts, f), lambda b_, i: (b_ * (dims.seq // ts) + i, 0)),
        out_shape=jax.ShapeDtypeStruct((t, f), BF16),
        scratch_shapes=[pltpu.VMEM((FFN_HALO + ts, 2 * f), F32), pltpu.VMEM((_moved_copies(offsets), ts, 2 * f), F32)],
        compiler_params=_params("parallel", "parallel"),
    )(up, up, w, b)


def _ffn_act_bwd(dact, up, w, b, dims, *, name, ts=128):
    t, f, kw = up.shape[0], dims.d_ff, dims.ffn_conv_width
    offsets = [FFN_HALO - (kw - 1) + k for k in range(kw)]

    def body(d_ref, up_ref, h_ref, w_ref, b_ref, du_ref, dw_ref, db_ref, buf, moved):
        i = pl.program_id(1)
        first = jnp.logical_and(pl.program_id(0) == 0, i == 0)
        buf[FFN_HALO:, :] = up_ref[...]
        buf[0:FFN_HALO, :] = jnp.where(i > 0, h_ref[...], 0.0)
        taps = _tap_sources(buf, moved, offsets, ts)
        for r0, c0, cw in _ffn_chunks(ts, f):
            vcols, gcols = slice(c0, c0 + cw), slice(f + c0, f + c0 + cw)
            uv = _taps_sum(taps, w_ref, jnp.broadcast_to(b_ref[:, vcols], (FFN_ROWS, cw)), r0, vcols)
            ug = _taps_sum(taps, w_ref, jnp.broadcast_to(b_ref[:, gcols], (FFN_ROWS, cw)), r0, gcols)
            d = d_ref[r0:r0 + FFN_ROWS, vcols].astype(F32)
            sg = _sigmoid(ug)
            du_ref[r0:r0 + FFN_ROWS, vcols] = d * ug * sg
            du_ref[r0:r0 + FFN_ROWS, gcols] = d * uv * sg * (1.0 + ug * (1.0 - sg))

        @pl.when(first)
        def _():
            dw_ref[...] = jnp.zeros_like(dw_ref)
            db_ref[...] = jnp.zeros_like(db_ref)

        du = du_ref[...]
        db_ref[...] += jnp.sum(du, axis=0, keepdims=True)
        for k, (src, off) in enumerate(taps):
            dw_ref[k:k + 1, :] += jnp.sum(du * src[pl.ds(off, ts), :], axis=0, keepdims=True)

    nst = dims.seq // ts
    n_moved = _moved_copies(offsets)
    full = lambda rows: pl.BlockSpec((rows, 2 * f), lambda b_, i: (0, 0))
    return pl.pallas_call(
        body, name=name, grid=(dims.batch_local, nst),
        in_specs=[pl.BlockSpec((ts, f), lambda b_, i: (b_ * nst + i, 0)),
                  *_seq_specs(dims, ts, 2 * f, FFN_HALO, 0), full(FFN_HALO), full(1)],
        out_specs=[pl.BlockSpec((ts, 2 * f), lambda b_, i: (b_ * nst + i, 0)), full(FFN_HALO), full(1)],
        out_shape=[jax.ShapeDtypeStruct((t, 2 * f), F32), jax.ShapeDtypeStruct((FFN_HALO, 2 * f), F32),
                   jax.ShapeDtypeStruct((1, 2 * f), F32)],
        scratch_shapes=[pltpu.VMEM((FFN_HALO + ts, 2 * f), F32), pltpu.VMEM((n_moved, ts, 2 * f), F32)],
        compiler_params=_params("arbitrary", "arbitrary"),
    )(dact, up, up, w, b)


def _ffn_conv_bwd(du, w, dims, *, name, ts=128):
    t, f2 = du.shape
    kw = dims.ffn_conv_width
    nst = dims.seq // ts

    offsets = [(kw - 1) - k for k in range(kw)]

    def body(d_ref, dn_ref, w_ref, o_ref, buf, moved):
        buf[0:ts, :] = d_ref[...]
        buf[ts:, :] = jnp.where(pl.program_id(1) < nst - 1, dn_ref[...], 0.0)
        taps = _tap_sources(buf, moved, offsets, ts)
        for r0, c0, cw in _ffn_chunks(ts, f2):
            cols = slice(c0, c0 + cw)
            o_ref[r0:r0 + FFN_ROWS, cols] = _taps_sum(taps, w_ref, jnp.zeros((FFN_ROWS, cw), F32), r0, cols).astype(BF16)

    return pl.pallas_call(
        body, name=name, grid=(dims.batch_local, nst),
        in_specs=[*_seq_specs(dims, ts, f2, FFN_HALO, 0, nxt=True), pl.BlockSpec((FFN_HALO, f2), lambda b_, i: (0, 0))],
        out_specs=pl.BlockSpec((ts, f2), lambda b_, i: (b_ * nst + i, 0)),
        out_shape=jax.ShapeDtypeStruct((t, f2), BF16),
        scratch_shapes=[pltpu.VMEM((ts + FFN_HALO, f2), F32), pltpu.VMEM((_moved_copies(offsets), ts, f2), F32)],
        compiler_params=_params("parallel", "parallel"),
    )(du, du, w)


def _alibi_slope(h, n_heads):
    return 2.0 ** (-8.0 * (h + 1) / n_heads)


def _dot_nt(a, b):
    return lax.dot_general(a, b, (((1,), (1,)), ((), ())), preferred_element_type=F32)


def _dot_tn(a, b):
    return lax.dot_general(a, b, (((0,), (0,)), ((), ())), preferred_element_type=F32)


def _attn_view(x, dims, dil):
    return x.reshape(dims.batch_local, dims.seq // dil, dil * x.shape[-1])


def _attn_fwd_group(q, k, v, state, dims, dil, *, last, name):
    t, a = q.shape
    assert 2 * dims.head_dim == 128 and dims.n_heads % 2 == 0
    blk, hd = ATTN_BLOCK, dims.head_dim
    nb = dims.seq // dil // blk
    has_prev = nb > 1
    nkeys = 2 * blk if has_prev else blk

    def body(*refs):
        it = iter(refs)
        q_ref, kc_ref, vc_ref = next(it), next(it), next(it)
        kp_ref, vp_ref = (next(it), next(it)) if has_prev else (None, None)
        m_in, l_in, acc_in = (next(it), next(it), next(it)) if state is not None else (None, None, None)
        outs = list(it)
        iq = lax.broadcasted_iota(jnp.int32, (blk, nkeys), 0)
        jk = lax.broadcasted_iota(jnp.int32, (blk, nkeys), 1)
        if has_prev:
            steps = iq + blk - jk
            valid = (steps >= 0) & (steps <= blk) & ((jk >= blk) | (pl.program_id(2) > 0))
        else:
            steps = iq - jk
            valid = steps >= 0
        dist = steps.astype(F32) * float(dil)
        low = lax.broadcasted_iota(jnp.int32, (blk, 2 * hd), 1) < hd
        for hp in range(dims.n_heads // 2):
            sl = slice(2 * hd * hp, 2 * hd * (hp + 1))
            q2 = q_ref[:, sl]
            if has_prev:
                kcat = jnp.concatenate([kp_ref[:, sl], kc_ref[:, sl]], axis=0)
                vcat = jnp.concatenate([vp_ref[:, sl], vc_ref[:, sl]], axis=0)
            else:
                kcat, vcat = kc_ref[:, sl], vc_ref[:, sl]
            halves = []
            for half in range(2):
                col = 2 * hd * hp + hd * half
                qh = jnp.where(low if half == 0 else jnp.logical_not(low), q2, jnp.zeros_like(q2))
                sc = _dot_nt(qh, kcat) - _alibi_slope(2 * hp + half, dims.n_heads) * dist
                sc = jnp.where(valid, sc, MASKED_SCORE)
                row_max = jnp.max(sc, axis=-1, keepdims=True)
                if state is None:
                    m_new = row_max
                    p = jnp.exp(sc - m_new)
                    alpha = None
                    l_new = jnp.sum(p, axis=-1, keepdims=True)
                else:
                    m_old = m_in[:, col:col + 1]
                    m_new = jnp.maximum(m_old, row_max)
                    p = jnp.exp(sc - m_new)
                    alpha = jnp.exp(m_old - m_new)
                    l_new = alpha * l_in[:, col:col + 1] + jnp.sum(p, axis=-1, keepdims=True)
                pv = jnp.dot(p.astype(BF16), vcat, preferred_element_type=F32)
                halves.append((m_new, l_new, alpha, pv))
            (m_a, l_a, al_a, pv_a), (m_b, l_b, al_b, pv_b) = halves
            if state is None:
                acc = jnp.where(low, pv_a, pv_b)
            else:
                old = acc_in[:, sl]
                acc = jnp.where(low, al_a * old + pv_a, al_b * old + pv_b)
            m2 = jnp.where(low, m_a, m_b)
            l2 = jnp.where(low, l_a, l_b)
            if last:
                outs[0][:, sl] = (acc / l2).astype(BF16)
                outs[1][:, sl] = m2 + jnp.log(l2)
            else:
                outs[0][:, sl] = m2
                outs[1][:, sl] = l2
                outs[2][:, sl] = acc

    cur = pl.BlockSpec((None, blk, a), lambda b, r, i: (b, i, r))
    prev = pl.BlockSpec((None, blk, a), lambda b, r, i: (b, jnp.maximum(i - 1, 0), r))
    args, in_specs = [q, k, v], [cur, cur, cur]
    if has_prev:
        args += [k, v]
        in_specs += [prev, prev]
    if state is not None:
        args += list(state)
        in_specs += [cur] * 3
    shape = lambda dt: jax.ShapeDtypeStruct((dims.batch_local, dims.seq // dil, dil * a), dt)
    out_shape = [shape(BF16), shape(F32)] if last else [shape(F32)] * 3
    outs = pl.pallas_call(
        body, name=name, grid=(dims.batch_local, dil, nb),
        in_specs=in_specs, out_specs=[cur] * len(out_shape), out_shape=out_shape,
        compiler_params=_params("parallel", "parallel", "parallel"),
    )(*[_attn_view(x, dims, dil) for x in args])
    return tuple(o.reshape(t, a) for o in outs)


def _attn_delta(do, o, head_ones, dims, *, name, tr=512):
    t, a = o.shape
    tr = _pick(t, tr, 8)

    def body(do_ref, o_ref, e_ref, d_ref):
        prod = do_ref[...].astype(F32) * o_ref[...].astype(F32)
        d_ref[...] = _head_mean(prod, e_ref, dims.head_dim) * float(dims.head_dim)

    return pl.pallas_call(
        body, name=name, grid=(t // tr,),
        in_specs=[_row_spec(tr, a), _row_spec(tr, a), pl.BlockSpec((a, a), lambda i: (0, 0))],
        out_specs=_row_spec(tr, a), out_shape=jax.ShapeDtypeStruct((t, a), F32),
        compiler_params=_params("parallel"),
    )(do, o, head_ones)


def _attn_bwd_group(q, k, v, do, lse, delta, dims, dil, *, name):
    t, a = q.shape
    blk, hd = ATTN_BLOCK, dims.head_dim
    nb = dims.seq // dil // blk
    has_next = nb > 1

    def body(*refs):
        k_ref, v_ref, q_ref, do_ref, lse_ref, dl_ref = refs[:6]
        if has_next:
            qn_ref, don_ref, lsen_ref, dln_ref = refs[6:10]
            dq_ref, dk_ref, dv_ref, carry = refs[10:]
        else:
            dq_ref, dk_ref, dv_ref = refs[6:]
        j = pl.program_id(2)
        iq = lax.broadcasted_iota(jnp.int32, (blk, blk), 0)
        jk = lax.broadcasted_iota(jnp.int32, (blk, blk), 1)
        low = lax.broadcasted_iota(jnp.int32, (blk, 2 * hd), 1) < hd

        def pair(hp, qr, dor, lser, dlr, steps, valid):
            sl = slice(2 * hd * hp, 2 * hd * (hp + 1))
            q2, do2, k2, v2 = qr[:, sl], dor[:, sl], k_ref[:, sl], v_ref[:, sl]
            dist = steps.astype(F32) * float(dil)
            dq_h, dk2, dv2 = [], None, None
            for half in range(2):
                col = 2 * hd * hp + hd * half
                mask = low if half == 0 else jnp.logical_not(low)
                qh = jnp.where(mask, q2, jnp.zeros_like(q2))
                doh = jnp.where(mask, do2, jnp.zeros_like(do2))
                sc = _dot_nt(qh, k2) - _alibi_slope(2 * hp + half, dims.n_heads) * dist
                p = jnp.where(valid, jnp.exp(sc - lser[:, col:col + 1]), 0.0)
                ds = p * (_dot_nt(doh, v2) - dlr[:, col:col + 1])
                ds_b, p_b = ds.astype(BF16), p.astype(BF16)
                dq_h.append(jnp.dot(ds_b, k2, preferred_element_type=F32))
                dk_h, dv_h = _dot_tn(ds_b, qh), _dot_tn(p_b, doh)
                dk2 = dk_h if dk2 is None else dk2 + dk_h
                dv2 = dv_h if dv2 is None else dv2 + dv_h
            return sl, jnp.where(low, dq_h[0], dq_h[1]), dk2, dv2

        if has_next:
            @pl.when(j == 0)
            def _():
                carry[...] = jnp.zeros_like(carry)

        for hp in range(dims.n_heads // 2):
            sl, dq2, dk2, dv2 = pair(hp, q_ref, do_ref, lse_ref, dl_ref, iq - jk, iq >= jk)
            dq_ref[:, sl] = (carry[:, sl] + dq2) if has_next else dq2
            dk_ref[:, sl] = dk2
            dv_ref[:, sl] = dv2

        if has_next:
            @pl.when(j + 1 < nb)
            def _():
                for hp in range(dims.n_heads // 2):
                    sl, dq2, dk2, dv2 = pair(hp, qn_ref, don_ref, lsen_ref, dln_ref, iq - jk + blk, jk >= iq)
                    carry[:, sl] = dq2
                    dk_ref[:, sl] += dk2
                    dv_ref[:, sl] += dv2

    cur = pl.BlockSpec((None, blk, a), lambda b, r, j: (b, j, r))
    nxt = pl.BlockSpec((None, blk, a), lambda b, r, j: (b, jnp.minimum(j + 1, nb - 1), r))
    args, in_specs = [k, v, q, do, lse, delta], [cur] * 6
    if has_next:
        args += [q, do, lse, delta]
        in_specs += [nxt] * 4
    shape = jax.ShapeDtypeStruct((dims.batch_local, dims.seq // dil, dil * a), F32)
    outs = pl.pallas_call(
        body, name=name, grid=(dims.batch_local, dil, nb),
        in_specs=in_specs, out_specs=[cur] * 3, out_shape=[shape] * 3,
        scratch_shapes=[pltpu.VMEM((blk, a), F32)] if has_next else [],
        compiler_params=_params("parallel", "parallel", "arbitrary"),
    )(*[_attn_view(x, dims, dil) for x in args])
    return tuple(o.reshape(t, a) for o in outs)


LANES = 128
MASK_BIAS = 1e30
RESIDUE_DILATIONS = tuple(d for d in DILATIONS if d > 1)


def _rows_to_residues(value, out_ref, scr, d):
    rows, width = value.shape
    for c in range(width // LANES):
        cols = slice(LANES * c, LANES * (c + 1))
        scr[c] = value[:, cols]
        for r in range(d):
            out_ref[r, :, cols] = scr[c, pl.ds(r, rows // d, stride=d), :].astype(out_ref.dtype)


def _residues_to_rows(in_ref, scr, d):
    _, n, width = in_ref.shape
    slabs = []
    for c in range(width // LANES):
        cols = slice(LANES * c, LANES * (c + 1))
        for r in range(d):
            scr[c, pl.ds(r, n, stride=d), :] = in_ref[r, :, cols].astype(F32)
        slabs.append(scr[c])
    return slabs[0] if len(slabs) == 1 else jnp.concatenate(slabs, axis=1)


def _residue_shape(dims, d, width, dtype):
    return jax.ShapeDtypeStruct((dims.batch_local, d, dims.seq // d, width), dtype)


def _residue_spec(dims, d, tr, width):
    tiles = dims.seq // tr
    return pl.BlockSpec((None, d, tr // d, width), lambda i: (i // tiles, 0, i % tiles, 0))


def _head_sum_matrix(dims):
    a = dims.n_heads * dims.head_dim
    head = jnp.arange(a, dtype=jnp.int32) // dims.head_dim
    return (head[:, None] == jnp.arange(LANES, dtype=jnp.int32)[None, :]).astype(BF16)


def _two_pass_dot(v, m):
    hi = v.astype(BF16)
    lo = (v - hi.astype(F32)).astype(BF16)
    return jnp.dot(hi, m, preferred_element_type=F32) + jnp.dot(lo, m, preferred_element_type=F32)


def _qkv_layouts_fwd(z, gq, gk, head_ones, dims, *, name, tr=256):
    t = z.shape[0]
    a = dims.n_heads * dims.head_dim
    q_scale = dims.head_dim ** -0.5
    nres = len(RESIDUE_DILATIONS)

    def body(q_ref, k_ref, v_ref, gq_ref, gk_ref, sum_ref, spread_ref, *rest):
        outs, scr = rest[:-1], rest[-1]
        qv, kv = q_ref[...].astype(F32), k_ref[...].astype(F32)
        mean = lambda val: _two_pass_dot(_two_pass_dot(val, sum_ref[...]), spread_ref[...]) * (1.0 / dims.head_dim)
        rq = lax.rsqrt(mean(qv * qv) + RMS_EPS)
        rk = lax.rsqrt(mean(kv * kv) + RMS_EPS)
        values = (qv * rq * gq_ref[...] * q_scale, kv * rk * gk_ref[...], v_ref[...].astype(F32))
        for j, val in enumerate(values):
            outs[j][...] = val.astype(BF16)
            for g, d in enumerate(RESIDUE_DILATIONS):
                _rows_to_residues(val, outs[3 * (g + 1) + j], scr, d)

    out_specs = [_row_spec(tr, a)] * 3
    out_shape = [jax.ShapeDtypeStruct((t, a), BF16)] * 3
    for d in RESIDUE_DILATIONS:
        out_specs += [_residue_spec(dims, d, tr, a)] * 3
        out_shape += [_residue_shape(dims, d, a, BF16)] * 3
    outs = pl.pallas_call(
        body, name=name, grid=(t // tr,),
        in_specs=[_row_spec(tr, a, 2), _row_spec(tr, a, 3), _row_spec(tr, a, 4), _vec_spec(a), _vec_spec(a),
                  pl.BlockSpec((a, LANES), lambda i: (0, 0)), pl.BlockSpec((LANES, a), lambda i: (0, 0))],
        out_specs=out_specs, out_shape=out_shape,
        scratch_shapes=[pltpu.VMEM((a // LANES, tr, LANES), F32)],
        compiler_params=_params("parallel"),
    )(z, z, z, gq, gk, *head_ones)
    return {d: tuple(outs[3 * g:3 * g + 3]) for g, d in enumerate((1,) + RESIDUE_DILATIONS)}


def _attn_specs(dims, dil, width):
    blk = ATTN_BLOCK
    nb = dims.seq // dil // blk
    if dil == 1:
        grid = (dims.batch_local, nb)
        at = lambda f: pl.BlockSpec((blk, width), lambda b, i: (b * nb + f(i), 0))
    else:
        grid = (dims.batch_local, dil, nb)
        at = lambda f: pl.BlockSpec((None, None, blk, width), lambda b, r, i: (b, r, f(i), 0))
    return grid, at(lambda i: i), at(lambda i: jnp.maximum(i - 1, 0)), at(lambda i: jnp.minimum(i + 1, nb - 1))


def _head_slopes(n_heads):
    h = lax.broadcasted_iota(jnp.int32, (n_heads, 1, 1), 0).astype(F32)
    return jnp.exp((h + 1.0) * (-8.0 / n_heads * math.log(2.0)))


def _pair_masks(hd):
    low = lax.broadcasted_iota(jnp.int32, (1, 2 * hd), 1) < hd
    return low, jnp.logical_not(low)


def _attn_fwd(q, k, v, dims, dil, *, name):
    a = dims.n_heads * dims.head_dim
    heads, hd, blk = dims.n_heads, dims.head_dim, ATTN_BLOCK
    assert 2 * hd == LANES and heads % 2 == 0 and heads <= LANES
    nb = dims.seq // dil // blk
    has_prev = nb > 1
    nkeys = 2 * blk if has_prev else blk
    grid, cur, prev, _ = _attn_specs(dims, dil, a)
    _, cur_stat, _, _ = _attn_specs(dims, dil, LANES)

    def body(*refs):
        if has_prev:
            q_ref, kc_ref, vc_ref, kp_ref, vp_ref, o_ref, lse_ref, s_scr, p_scr = refs
        else:
            q_ref, kc_ref, vc_ref, o_ref, lse_ref, s_scr, p_scr = refs
        low, high = _pair_masks(hd)

        def keys(cur_ref, prev_ref, sl):
            return jnp.concatenate([prev_ref[:, sl], cur_ref[:, sl]], axis=0) if has_prev else cur_ref[:, sl]

        for hp in range(heads // 2):
            sl = slice(LANES * hp, LANES * (hp + 1))
            q2 = q_ref[:, sl]
            kcat = keys(kc_ref, kp_ref if has_prev else None, sl)
            s_scr[2 * hp] = _dot_nt(jnp.where(low, q2, jnp.zeros_like(q2)), kcat)
            s_scr[2 * hp + 1] = _dot_nt(jnp.where(high, q2, jnp.zeros_like(q2)), kcat)

        iq = lax.broadcasted_iota(jnp.int32, (blk, nkeys), 0)
        jk = lax.broadcasted_iota(jnp.int32, (blk, nkeys), 1)
        if has_prev:
            steps = iq + blk - jk
            valid = (steps >= 0) & (steps <= blk) & ((jk >= blk) | (pl.program_id(len(grid) - 1) > 0))
        else:
            steps = iq - jk
            valid = steps >= 0
        bias = jnp.where(valid, steps.astype(F32) * (-float(dil)), -MASK_BIAS)
        s = s_scr[...] + _head_slopes(heads) * bias[None]
        m = jnp.max(s, axis=-1, keepdims=True)
        p = jnp.exp(s - m)
        l = jnp.sum(p, axis=-1, keepdims=True)
        p_scr[...] = p.astype(BF16)
        inv = 1.0 / l
        lse = m + jnp.log(l)

        lane = lax.broadcasted_iota(jnp.int32, (blk, LANES), 1)
        stat = jnp.zeros((blk, LANES), F32)
        for hp in range(heads // 2):
            sl = slice(LANES * hp, LANES * (hp + 1))
            vcat = keys(vc_ref, vp_ref if has_prev else None, sl)
            pv_a = jnp.dot(p_scr[2 * hp], vcat, preferred_element_type=F32) * inv[2 * hp]
            pv_b = jnp.dot(p_scr[2 * hp + 1], vcat, preferred_element_type=F32) * inv[2 * hp + 1]
            o_ref[:, sl] = jnp.where(low, pv_a, pv_b)
            stat = jnp.where(lane == 2 * hp, lse[2 * hp], stat)
            stat = jnp.where(lane == 2 * hp + 1, lse[2 * hp + 1], stat)
        lse_ref[...] = stat

    lead = q.shape[:-2]
    rows = q.shape[-2]
    o, lse = pl.pallas_call(
        body, name=name, grid=grid,
        in_specs=[cur, cur, cur] + ([prev, prev] if has_prev else []),
        out_specs=[cur, cur_stat],
        out_shape=[jax.ShapeDtypeStruct(lead + (rows, a), F32), jax.ShapeDtypeStruct(lead + (rows, LANES), F32)],
        scratch_shapes=[pltpu.VMEM((heads, blk, nkeys), F32), pltpu.VMEM((heads, blk, nkeys), BF16)],
        compiler_params=_params(*["parallel"] * len(grid)),
    )(q, k, v, *([k, v] if has_prev else []))
    return o, lse


def _attn_combine(groups, head_spread, dims, *, name, tr=256):
    t = dims.tokens
    a = dims.n_heads * dims.head_dim
    dils = tuple(groups)

    def body(*refs):
        ins = refs[:2 * len(dils)]
        x_ref = refs[2 * len(dils)]
        o_ref = refs[2 * len(dils) + 1]
        lse_refs = refs[2 * len(dils) + 2:-2]
        scr, scr_stat = refs[-2], refs[-1]
        outs, stats = [], []
        for g, d in enumerate(dils):
            if d == 1:
                outs.append(ins[2 * g][...])
                stats.append(ins[2 * g + 1][...])
            else:
                outs.append(_residues_to_rows(ins[2 * g], scr, d))
                stats.append(_residues_to_rows(ins[2 * g + 1], scr_stat, d))
        top = functools.reduce(jnp.maximum, stats)
        weights = [jnp.exp(s - top) for s in stats]
        total = functools.reduce(jnp.add, weights)
        joint = top + jnp.log(total)
        inv = 1.0 / total
        acc = None
        for w, o in zip(weights, outs):
            term = _two_pass_dot(w * inv, x_ref[...]) * o
            acc = term if acc is None else acc + term
        o_ref[...] = acc.astype(BF16)
        for g, d in enumerate(dils):
            if d == 1:
                lse_refs[g][...] = joint
            else:
                _rows_to_residues(joint, lse_refs[g], scr_stat, d)

    in_specs, args, lse_specs, lse_shapes = [], [], [], []
    for d in dils:
        if d == 1:
            in_specs += [_row_spec(tr, a), _row_spec(tr, LANES)]
            lse_specs.append(_row_spec(tr, LANES))
            lse_shapes.append(jax.ShapeDtypeStruct((t, LANES), F32))
        else:
            in_specs += [_residue_spec(dims, d, tr, a), _residue_spec(dims, d, tr, LANES)]
            lse_specs.append(_residue_spec(dims, d, tr, LANES))
            lse_shapes.append(_residue_shape(dims, d, LANES, F32))
        args += list(groups[d])
    outs = pl.pallas_call(
        body, name=name, grid=(t // tr,),
        in_specs=in_specs + [pl.BlockSpec((LANES, a), lambda i: (0, 0))],
        out_specs=[_row_spec(tr, a)] + lse_specs,
        out_shape=[jax.ShapeDtypeStruct((t, a), BF16)] + lse_shapes,
        scratch_shapes=[pltpu.VMEM((a // LANES, tr, LANES), F32), pltpu.VMEM((1, tr, LANES), F32)],
        compiler_params=_params("parallel"),
    )(*args, head_spread)
    return outs[0], dict(zip(dils, outs[1:]))


def _attn_bwd_prep(do, o, head_sum, dims, *, name, tr=256):
    t, a = o.shape

    def body(do_ref, o_ref, e_ref, *rest):
        outs, scr, scr_stat = rest[:-2], rest[-2], rest[-1]
        dov = do_ref[...].astype(F32)
        delta = _two_pass_dot(dov * o_ref[...].astype(F32), e_ref[...])
        outs[0][...] = delta
        for g, d in enumerate(RESIDUE_DILATIONS):
            _rows_to_residues(dov, outs[1 + 2 * g], scr, d)
            _rows_to_residues(delta, outs[2 + 2 * g], scr_stat, d)

    out_specs, out_shape = [_row_spec(tr, LANES)], [jax.ShapeDtypeStruct((t, LANES), F32)]
    for d in RESIDUE_DILATIONS:
        out_specs += [_residue_spec(dims, d, tr, a), _residue_spec(dims, d, tr, LANES)]
        out_shape += [_residue_shape(dims, d, a, BF16), _residue_shape(dims, d, LANES, F32)]
    outs = pl.pallas_call(
        body, name=name, grid=(t // tr,),
        in_specs=[_row_spec(tr, a), _row_spec(tr, a), pl.BlockSpec((a, LANES), lambda i: (0, 0))],
        out_specs=out_specs, out_shape=out_shape,
        scratch_shapes=[pltpu.VMEM((a // LANES, tr, LANES), F32), pltpu.VMEM((1, tr, LANES), F32)],
        compiler_params=_params("parallel"),
    )(do, o, head_sum)
    dos, deltas = {1: do}, {1: outs[0]}
    for g, d in enumerate(RESIDUE_DILATIONS):
        dos[d], deltas[d] = outs[1 + 2 * g], outs[2 + 2 * g]
    return dos, deltas


def _attn_bwd(q, k, v, do, lse, delta, dims, dil, *, name):
    a = dims.n_heads * dims.head_dim
    heads, hd, blk = dims.n_heads, dims.head_dim, ATTN_BLOCK
    nb = dims.seq // dil // blk
    has_next = nb > 1
    nq = 2 * blk if has_next else blk
    grid, cur, _, nxt = _attn_specs(dims, dil, a)
    _, cur_stat, _, nxt_stat = _attn_specs(dims, dil, LANES)

    def body(*refs):
        k_ref, v_ref, q_ref, do_ref, lse_ref, dl_ref = refs[:6]
        if has_next:
            qn_ref, don_ref, lsen_ref, dln_ref = refs[6:10]
            dq_ref, dk_ref, dv_ref, s_scr, dp_scr, p_scr, ds_scr, carry = refs[10:]
        else:
            dq_ref, dk_ref, dv_ref, s_scr, dp_scr, p_scr, ds_scr = refs[6:]
        j = pl.program_id(len(grid) - 1)
        low, high = _pair_masks(hd)

        def stacked(ref, nref, sl):
            return jnp.concatenate([ref[:, sl], nref[:, sl]], axis=0) if has_next else ref[:, sl]

        def halves(x):
            return jnp.where(low, x, jnp.zeros_like(x)), jnp.where(high, x, jnp.zeros_like(x))

        for hp in range(heads // 2):
            sl = slice(LANES * hp, LANES * (hp + 1))
            k2, v2 = k_ref[:, sl], v_ref[:, sl]
            q_a, q_b = halves(stacked(q_ref, qn_ref if has_next else None, sl))
            do_a, do_b = halves(stacked(do_ref, don_ref if has_next else None, sl))
            s_scr[2 * hp], s_scr[2 * hp + 1] = _dot_nt(q_a, k2), _dot_nt(q_b, k2)
            dp_scr[2 * hp], dp_scr[2 * hp + 1] = _dot_nt(do_a, v2), _dot_nt(do_b, v2)

        rq = lax.broadcasted_iota(jnp.int32, (nq, blk), 0)
        jk = lax.broadcasted_iota(jnp.int32, (nq, blk), 1)
        if has_next:
            iq = jnp.where(rq < blk, rq, rq - blk)
            steps = jnp.where(rq < blk, iq - jk, iq - jk + blk)
            valid = ((rq < blk) & (iq >= jk)) | ((rq >= blk) & (jk >= iq) & (j + 1 < nb))
        else:
            steps, valid = rq - jk, rq >= jk
        bias = jnp.where(valid, steps.astype(F32) * (-float(dil)), -MASK_BIAS)
        lse_all = stacked(lse_ref, lsen_ref if has_next else None, slice(None))
        dl_all = stacked(dl_ref, dln_ref if has_next else None, slice(None))
        lse3 = jnp.stack([lse_all[:, h:h + 1] for h in range(heads)])
        dl3 = jnp.stack([dl_all[:, h:h + 1] for h in range(heads)])
        p = jnp.exp(s_scr[...] + _head_slopes(heads) * bias[None] - lse3)
        p_scr[...] = p.astype(BF16)
        ds_scr[...] = (p * (dp_scr[...] - dl3)).astype(BF16)

        if has_next:
            @pl.when(j == 0)
            def _():
                carry[...] = jnp.zeros_like(carry)

        for hp in range(heads // 2):
            sl = slice(LANES * hp, LANES * (hp + 1))
            k2 = k_ref[:, sl]
            q_a, q_b = halves(stacked(q_ref, qn_ref if has_next else None, sl))
            do_a, do_b = halves(stacked(do_ref, don_ref if has_next else None, sl))
            ds_a, ds_b = ds_scr[2 * hp], ds_scr[2 * hp + 1]
            dq2 = jnp.where(low, jnp.dot(ds_a, k2, preferred_element_type=F32),
                            jnp.dot(ds_b, k2, preferred_element_type=F32))
            dk_ref[:, sl] = _dot_tn(ds_a, q_a) + _dot_tn(ds_b, q_b)
            dv_ref[:, sl] = _dot_tn(p_scr[2 * hp], do_a) + _dot_tn(p_scr[2 * hp + 1], do_b)
            if has_next:
                dq_ref[:, sl] = carry[:, sl] + dq2[:blk]
                carry[:, sl] = dq2[blk:]
            else:
                dq_ref[:, sl] = dq2

    args, in_specs = [k, v, q, do, lse, delta], [cur] * 4 + [cur_stat] * 2
    if has_next:
        args += [q, do, lse, delta]
        in_specs += [nxt] * 2 + [nxt_stat] * 2
    shape = jax.ShapeDtypeStruct(q.shape, F32)
    scratch = [pltpu.VMEM((heads, nq, blk), F32)] * 2 + [pltpu.VMEM((heads, nq, blk), BF16)] * 2
    if has_next:
        scratch.append(pltpu.VMEM((blk, a), F32))
    return pl.pallas_call(
        body, name=name, grid=grid, in_specs=in_specs, out_specs=[cur] * 3, out_shape=[shape] * 3,
        scratch_shapes=scratch,
        compiler_params=_params(*["parallel"] * (len(grid) - 1), "arbitrary"),
    )(*args)


def _qkv_layouts_bwd(z, grads, gq, gk, head_ones, dims, *, name, tr=256):
    t = z.shape[0]
    a = dims.n_heads * dims.head_dim
    q_scale = dims.head_dim ** -0.5
    dils = tuple(grads)

    def body(q_ref, k_ref, *rest):
        d_refs = rest[:3 * len(dils)]
        gq_ref, gk_ref, sum_ref, spread_ref, dz_ref, dgq_ref, dgk_ref, scr = rest[3 * len(dils):]
        first = pl.program_id(0) == 0
        mean = lambda val: _two_pass_dot(_two_pass_dot(val, sum_ref[...]), spread_ref[...]) * (1.0 / dims.head_dim)

        def total(j):
            acc = None
            for g, d in enumerate(dils):
                ref = d_refs[3 * g + j]
                part = ref[...] if d == 1 else _residues_to_rows(ref, scr, d)
                acc = part if acc is None else acc + part
            return acc

        def norm_bwd(x_ref, dy, g_ref, scale, col, dg_ref):
            xv = x_ref[...].astype(F32)
            dy = dy * scale
            r = lax.rsqrt(mean(xv * xv) + RMS_EPS)
            gy = dy * g_ref[...]
            dx = r * gy - xv * (r * r * r) * mean(xv * gy)
            dz_ref[:, col * a:(col + 1) * a] = dx.astype(BF16)
            _accumulate(dg_ref, jnp.sum(dy * xv * r, axis=0, keepdims=True), first)

        norm_bwd(q_ref, total(0), gq_ref, q_scale, 0, dgq_ref)
        norm_bwd(k_ref, total(1), gk_ref, 1.0, 1, dgk_ref)
        dz_ref[:, 2 * a:3 * a] = total(2).astype(BF16)

    in_specs, args = [_row_spec(tr, a, 2), _row_spec(tr, a, 3)], [z, z]
    for d in dils:
        in_specs += [_row_spec(tr, a) if d == 1 else _residue_spec(dims, d, tr, a)] * 3
        args += list(grads[d])
    in_specs += [_vec_spec(a), _vec_spec(a), pl.BlockSpec((a, LANES), lambda i: (0, 0)),
                 pl.BlockSpec((LANES, a), lambda i: (0, 0))]
    return pl.pallas_call(
        body, name=name, grid=(t // tr,), in_specs=in_specs,
        out_specs=[_row_spec(tr, 3 * a), _vec_spec(a), _vec_spec(a)],
        out_shape=[jax.ShapeDtypeStruct((t, 3 * a), BF16)] + [jax.ShapeDtypeStruct((1, a), F32)] * 2,
        scratch_shapes=[pltpu.VMEM((a // LANES, tr, LANES), F32)],
        compiler_params=_params("arbitrary"),
    )(*args, gq, gk, *head_ones)


def _mix_fwd(ya, yb, z, gate_b, dims, *, name, tr=512):
    t, d = ya.shape
    tr = _pick(t, tr, 8)
    first_gate_col = z.shape[1] // d - 2

    def body(ya_ref, yb_ref, ga_ref, gb_ref, ba_ref, bb_ref, o_ref):
        g_a = _sigmoid(ga_ref[...].astype(F32) + ba_ref[...])
        g_b = _sigmoid(gb_ref[...].astype(F32) + bb_ref[...])
        o_ref[...] = (g_a * ya_ref[...] + g_b * yb_ref[...]).astype(BF16)

    return pl.pallas_call(
        body, name=name, grid=(t // tr,),
        in_specs=[_row_spec(tr, d), _row_spec(tr, d), _row_spec(tr, d, first_gate_col),
                  _row_spec(tr, d, first_gate_col + 1), _vec_spec(d, 0), _vec_spec(d, 1)],
        out_specs=_row_spec(tr, d), out_shape=jax.ShapeDtypeStruct((t, d), BF16),
        compiler_params=_params("parallel"),
    )(ya, yb, z, z, gate_b, gate_b)


def _mix_bwd(dmix, ya, yb, z, gate_b, dims, *, name, tr=512):
    t, d = ya.shape
    tr = _pick(t, tr, 8)
    first_gate_col = z.shape[1] // d - 2

    def body(dm_ref, ya_ref, yb_ref, ga_ref, gb_ref, ba_ref, bb_ref, dya_ref, dyb_ref, dz_ref, db_ref):
        dm = dm_ref[...].astype(F32)
        g_a = _sigmoid(ga_ref[...].astype(F32) + ba_ref[...])
        g_b = _sigmoid(gb_ref[...].astype(F32) + bb_ref[...])
        dya_ref[...] = (dm * g_a).astype(BF16)
        dyb_ref[...] = (dm * g_b).astype(BF16)
        dl_a = dm * ya_ref[...] * g_a * (1.0 - g_a)
        dl_b = dm * yb_ref[...] * g_b * (1.0 - g_b)
        dz_ref[:, 0:d] = dl_a.astype(BF16)
        dz_ref[:, d:2 * d] = dl_b.astype(BF16)
        first = pl.program_id(0) == 0
        sums = jnp.concatenate([jnp.sum(dl_a, axis=0, keepdims=True), jnp.sum(dl_b, axis=0, keepdims=True)], axis=1)
        _accumulate(db_ref, sums, first)

    return pl.pallas_call(
        body, name=name, grid=(t // tr,),
        in_specs=[_row_spec(tr, d), _row_spec(tr, d), _row_spec(tr, d), _row_spec(tr, d, first_gate_col),
                  _row_spec(tr, d, first_gate_col + 1), _vec_spec(d, 0), _vec_spec(d, 1)],
        out_specs=[_row_spec(tr, d), _row_spec(tr, d), _row_spec(tr, 2 * d), _vec_spec(2 * d)],
        out_shape=[jax.ShapeDtypeStruct((t, d), BF16)] * 2 + [jax.ShapeDtypeStruct((t, 2 * d), BF16),
                                                              jax.ShapeDtypeStruct((1, 2 * d), F32)],
        compiler_params=_params("arbitrary"),
    )(dmix, ya, yb, z, z, gate_b, gate_b)


def _loss_head(y, target, *, name, tr=512):
    t, d = y.shape
    tr = _pick(t, tr, 8)

    def body(y_ref, t_ref, dy_ref, dyb_ref, loss_ref):
        err = y_ref[...] - t_ref[...]
        dy = err * (1.0 / d)
        dy_ref[...] = dy
        dyb_ref[...] = dy.astype(BF16)
        part = jnp.sum(jnp.sum(err * err, axis=-1, keepdims=True), axis=0, keepdims=True) * (0.5 / d)
        _accumulate(loss_ref, jnp.broadcast_to(part, (8, 128)), pl.program_id(0) == 0)

    return pl.pallas_call(
        body, name=name, grid=(t // tr,),
        in_specs=[_row_spec(tr, d), _row_spec(tr, d)],
        out_specs=[_row_spec(tr, d), _row_spec(tr, d), pl.BlockSpec((8, 128), lambda i: (0, 0))],
        out_shape=[jax.ShapeDtypeStruct((t, d), F32), jax.ShapeDtypeStruct((t, d), BF16),
                   jax.ShapeDtypeStruct((8, 128), F32)],
        compiler_params=_params("arbitrary"),
    )(y, target)


def _adamw(w, grads, m, v, *, name, tr=256):
    r, c = w.shape
    tr = _pick(r, tr, 8)
    ng = len(grads)
    c1 = 1.0 - ADAM_B1 ** ADAM_STEP
    c2 = 1.0 - ADAM_B2 ** ADAM_STEP

    def body(*refs):
        w_ref, g_refs, m_ref, v_ref = refs[0], refs[1:1 + ng], refs[1 + ng], refs[2 + ng]
        g_out, d_out, m_out, v_out = refs[3 + ng:]
        g = g_refs[0][...]
        for extra in g_refs[1:]:
            g = g + extra[...]
        m_new = ADAM_B1 * m_ref[...] + (1.0 - ADAM_B1) * g
        v_new = ADAM_B2 * v_ref[...] + (1.0 - ADAM_B2) * (g * g)
        g_out[...] = g
        m_out[...] = m_new
        v_out[...] = v_new
        d_out[...] = -ADAM_LR * ((m_new / c1) / (jnp.sqrt(v_new / c2) + ADAM_EPS) + ADAM_WD * w_ref[...])

    spec = pl.BlockSpec((tr, c), lambda i: (i, 0))
    return pl.pallas_call(
        body, name=name, grid=(r // tr,),
        in_specs=[spec] * (3 + ng), out_specs=[spec] * 4, out_shape=[jax.ShapeDtypeStruct((r, c), F32)] * 4,
        compiler_params=_params("parallel"),
    )(w, *grads, m, v)


CHIP_PEERS = ((1, 0), (0, 1), (1, 1))


def _place():
    return lax.axis_index("x"), lax.axis_index("y"), lax.axis_index("c")


HBM = pl.BlockSpec(memory_space=pltpu.HBM)
SEM = pl.BlockSpec(memory_space=pltpu.SEMAPHORE)
IN_FLIGHT = pltpu.SideEffectType.DATAFLOW_SIDE_EFFECTING


def _in_hbm(a):
    return pltpu.with_memory_space_constraint(a, pltpu.HBM)


def _cast_to_lands(shards, dtypes, *, name):
    n = len(shards)

    def body(*refs):
        ins, outs, bufs, sems = refs[:n], refs[n:2 * n], refs[2 * n:3 * n], refs[3 * n]
        x, y, _ = _place()
        copies = []
        for a in range(n):
            bufs[a][...] = ins[a][...].astype(dtypes[a])
            cp = pltpu.make_async_copy(bufs[a], outs[a].at[2 * x + y], sems.at[a])
            cp.start()
            copies.append(cp)
        for cp in copies:
            cp.wait()

    return pl.pallas_call(
        body, name=name, in_specs=[pl.BlockSpec(memory_space=pltpu.VMEM)] * n, out_specs=[ANY] * n,
        out_shape=[jax.ShapeDtypeStruct((N_CHIPS,) + s.shape, dt) for s, dt in zip(shards, dtypes)],
        scratch_shapes=[pltpu.VMEM(s.shape, dt) for s, dt in zip(shards, dtypes)] + [pltpu.SemaphoreType.DMA((n,))],
        compiler_params=pltpu.CompilerParams(vmem_limit_bytes=V7X_VMEM_LIMIT_BYTES),
    )(*shards)


def _chip_copy(src, dst, send, recv, flip, place):
    x, y, c = place
    return pltpu.make_async_remote_copy(src_ref=src, dst_ref=dst, send_sem=send, recv_sem=recv,
                                        device_id=(x ^ flip[0], y ^ flip[1], c), device_id_type=MESH)


def _my_part(land, place, halved):
    block = land.at[2 * place[0] + place[1]]
    if not halved:
        return block
    rows = land.shape[1] // 2
    return block.at[pl.ds(pl.multiple_of(place[2] * rows, rows), rows)]


def _gather_start(lands, after, *, name, halved=()):
    n = len(lands)

    def body(*refs):
        ins, send, recv, token = refs[:n], refs[n + 1], refs[n + 2], refs[-1]
        place = _place()
        for a in range(n):
            part = _my_part(ins[a], place, a in halved)
            for p, flip in enumerate(CHIP_PEERS):
                k = 3 * a + p
                _chip_copy(part, part, send.at[k], recv.at[k], flip, place).start()
        token[...] = jnp.zeros_like(token)

    outs = pl.pallas_call(
        body, name=name, in_specs=[HBM] * n + [ANY],
        out_specs=(SEM, SEM, *[HBM] * n, pl.BlockSpec(memory_space=pltpu.VMEM)),
        out_shape=(pltpu.SemaphoreType.DMA((3 * n,)), pltpu.SemaphoreType.DMA((3 * n,)),
                   *[pltpu.HBM(l.shape, l.dtype) for l in lands], jax.ShapeDtypeStruct((8, 128), F32)),
        input_output_aliases={a: 2 + a for a in range(n)},
        compiler_params=pltpu.CompilerParams(has_side_effects=IN_FLIGHT),
    )(*[_in_hbm(l) for l in lands], after)
    return outs[0], outs[1], list(outs[2:2 + n]), outs[-1]


def _gather_wait(send, recv, lands, after, *, name, halved=()):
    n = len(lands)

    def body(*refs):
        ins, send_ref, recv_ref = refs[:n], refs[n], refs[n + 1]
        place = _place()
        for a in range(n):
            part = _my_part(ins[a], place, a in halved)
            for p, flip in enumerate(CHIP_PEERS):
                k = 3 * a + p
                cp = _chip_copy(part, part, send_ref.at[k], recv_ref.at[k], flip, place)
                cp.wait_send()
                cp.wait_recv()

    return pl.pallas_call(
        body, name=name, in_specs=[HBM] * n + [SEM, SEM, ANY], out_specs=[HBM] * n,
        out_shape=[pltpu.HBM(l.shape, l.dtype) for l in lands],
        input_output_aliases={a: a for a in range(n)},
        compiler_params=pltpu.CompilerParams(has_side_effects=IN_FLIGHT),
    )(*lands, send, recv, after)


def _forward_to_sibling(land, *, name):
    rows = land.shape[1] // 2

    def body(land_ref, out_ref, send, recv):
        x, y, c = _place()
        copies = []
        for p, (fx, fy) in enumerate(CHIP_PEERS):
            chip = 2 * (x ^ fx) + (y ^ fy)
            mine = pl.ds(pl.multiple_of(c * rows, rows), rows)
            theirs = pl.ds(pl.multiple_of((1 - c) * rows, rows), rows)
            out = pltpu.make_async_remote_copy(
                src_ref=land_ref.at[chip].at[mine], dst_ref=out_ref.at[chip].at[mine], send_sem=send.at[p],
                recv_sem=recv.at[p], device_id=(x, y, 1 - c), device_id_type=MESH)
            out.start()
            copies.append((out, pltpu.make_async_remote_copy(
                src_ref=land_ref.at[chip].at[theirs], dst_ref=out_ref.at[chip].at[theirs], send_sem=send.at[p],
                recv_sem=recv.at[p], device_id=(x, y, 1 - c), device_id_type=MESH)))
        for out, arriving in copies:
            out.wait_send()
            arriving.wait_recv()

    return pl.pallas_call(
        body, name=name, in_specs=[ANY], out_specs=ANY, out_shape=jax.ShapeDtypeStruct(land.shape, land.dtype),
        input_output_aliases={0: 0},
        scratch_shapes=[pltpu.SemaphoreType.DMA((3,)), pltpu.SemaphoreType.DMA((3,))],
    )(land)


def _scatter_start(grad, *, name):
    def body(g_ref, land_ref, send, recv, g_thru, land_thru, token):
        place = _place()
        for p, flip in enumerate(CHIP_PEERS):
            peer_chip = 2 * (place[0] ^ flip[0]) + (place[1] ^ flip[1])
            _chip_copy(g_ref.at[peer_chip], land_ref.at[p], send.at[p], recv.at[p], flip, place).start()
        token[...] = jnp.zeros_like(token)

    land = lax.empty((3,) + grad.shape[1:], grad.dtype)
    return pl.pallas_call(
        body, name=name, in_specs=[HBM, HBM],
        out_specs=(SEM, SEM, HBM, HBM, pl.BlockSpec(memory_space=pltpu.VMEM)),
        out_shape=(pltpu.SemaphoreType.DMA((3,)), pltpu.SemaphoreType.DMA((3,)), pltpu.HBM(grad.shape, grad.dtype),
                   pltpu.HBM(land.shape, land.dtype), jax.ShapeDtypeStruct((8, 128), F32)),
        input_output_aliases={0: 2, 1: 3},
        compiler_params=pltpu.CompilerParams(has_side_effects=IN_FLIGHT),
    )(_in_hbm(grad), _in_hbm(land))


def _scatter_wait(started, after, *, name):
    n = len(started)

    def body(*refs):
        grads, lands = refs[:n], refs[n:2 * n]
        sends, recvs = refs[2 * n:3 * n], refs[3 * n:4 * n]
        place = _place()
        for a in range(n):
            for p, flip in enumerate(CHIP_PEERS):
                cp = _chip_copy(grads[a].at[0], lands[a].at[p], sends[a].at[p], recvs[a].at[p], flip, place)
                cp.wait_send()
                cp.wait_recv()

    grads, lands = [s[2] for s in started], [s[3] for s in started]
    after = list(after) if isinstance(after, (list, tuple)) else [after]
    outs = pl.pallas_call(
        body, name=name, in_specs=[HBM] * (2 * n) + [SEM] * (2 * n) + [ANY] * len(after), out_specs=[HBM] * (2 * n),
        out_shape=[pltpu.HBM(a.shape, a.dtype) for a in grads + lands],
        input_output_aliases={a: a for a in range(2 * n)},
        compiler_params=pltpu.CompilerParams(has_side_effects=IN_FLIGHT),
    )(*grads, *lands, *[s[0] for s in started], *[s[1] for s in started], *after)
    return list(zip(outs[:n], outs[n:]))


def _sibling_copy(src, dst, send, recv, place):
    x, y, c = place
    return pltpu.make_async_remote_copy(src_ref=src, dst_ref=dst, send_sem=send, recv_sem=recv,
                                        device_id=(x, y, 1 - c), device_id_type=MESH)


def _swap_start(arrays, *, name):
    n = len(arrays)

    def body(*refs):
        ins, lands, send, recv, token = refs[:n], refs[n:2 * n], refs[2 * n], refs[2 * n + 1], refs[-1]
        place = _place()
        for a in range(n):
            _sibling_copy(ins[a], lands[a], send.at[a], recv.at[a], place).start()
        token[...] = jnp.zeros_like(token)

    both = [_in_hbm(a) for a in arrays] + [_in_hbm(lax.empty(a.shape, a.dtype)) for a in arrays]
    outs = pl.pallas_call(
        body, name=name, in_specs=[HBM] * (2 * n),
        out_specs=(SEM, SEM, *[HBM] * (2 * n), pl.BlockSpec(memory_space=pltpu.VMEM)),
        out_shape=(pltpu.SemaphoreType.DMA((n,)), pltpu.SemaphoreType.DMA((n,)),
                   *[pltpu.HBM(a.shape, a.dtype) for a in both], jax.ShapeDtypeStruct((8, 128), F32)),
        input_output_aliases={a: 2 + a for a in range(2 * n)},
        compiler_params=pltpu.CompilerParams(has_side_effects=IN_FLIGHT),
    )(*both)
    return outs[0], outs[1], list(outs[2:2 + n]), list(outs[2 + n:2 + 2 * n]), outs[-1]


def _swap_wait(started, after, *, name):
    send, recv, arrays, lands = started[:4]
    n = len(arrays)

    def body(*refs):
        ins, zones, send_ref, recv_ref = refs[:n], refs[n:2 * n], refs[2 * n], refs[2 * n + 1]
        place = _place()
        for a in range(n):
            cp = _sibling_copy(ins[a], zones[a], send_ref.at[a], recv_ref.at[a], place)
            cp.wait_send()
            cp.wait_recv()

    after = list(after) if isinstance(after, (list, tuple)) else [after]
    outs = pl.pallas_call(
        body, name=name, in_specs=[HBM] * (2 * n) + [SEM, SEM] + [ANY] * len(after), out_specs=[HBM] * (2 * n),
        out_shape=[pltpu.HBM(a.shape, a.dtype) for a in arrays + lands],
        input_output_aliases={a: a for a in range(2 * n)},
        compiler_params=pltpu.CompilerParams(has_side_effects=IN_FLIGHT),
    )(*arrays, *lands, send, recv, *after)
    return list(outs[:n]), list(outs[n:])


def _allreduce_start(packed, *, name):
    n_dev = 8

    def body(src_ref, land_ref, send, recv, src_thru, land_thru, token):
        x, y, c = _place()
        me = 4 * x + 2 * y + c
        for p in range(1, n_dev):
            pltpu.make_async_remote_copy(
                src_ref=src_ref, dst_ref=land_ref.at[me], send_sem=send.at[p - 1], recv_sem=recv.at[p - 1],
                device_id=(x ^ (p >> 2), y ^ ((p >> 1) & 1), c ^ (p & 1)), device_id_type=MESH).start()
        token[...] = jnp.zeros_like(token)

    land = lax.empty((n_dev,) + packed.shape, packed.dtype)
    return pl.pallas_call(
        body, name=name, in_specs=[HBM, HBM],
        out_specs=(SEM, SEM, HBM, HBM, pl.BlockSpec(memory_space=pltpu.VMEM)),
        out_shape=(pltpu.SemaphoreType.DMA((n_dev - 1,)), pltpu.SemaphoreType.DMA((n_dev - 1,)),
                   pltpu.HBM(packed.shape, packed.dtype), pltpu.HBM(land.shape, land.dtype),
                   jax.ShapeDtypeStruct((8, 128), F32)),
        input_output_aliases={0: 2, 1: 3},
        compiler_params=pltpu.CompilerParams(has_side_effects=IN_FLIGHT),
    )(_in_hbm(packed), _in_hbm(land))


def _allreduce_wait(started, after, *, name):
    send, recv, packed, land = started[:4]
    n_dev = 8

    def body(src_ref, land_ref, send_ref, recv_ref, *_):
        x, y, c = _place()
        for p in range(1, n_dev):
            cp = pltpu.make_async_remote_copy(
                src_ref=src_ref, dst_ref=land_ref.at[0], send_sem=send_ref.at[p - 1], recv_sem=recv_ref.at[p - 1],
                device_id=(x ^ (p >> 2), y ^ ((p >> 1) & 1), c ^ (p & 1)), device_id_type=MESH)
            cp.wait_send()
            cp.wait_recv()

    after = list(after) if isinstance(after, (list, tuple)) else [after]
    return pl.pallas_call(
        body, name=name, in_specs=[HBM, HBM, SEM, SEM] + [ANY] * len(after), out_specs=[HBM, HBM],
        out_shape=[pltpu.HBM(packed.shape, packed.dtype), pltpu.HBM(land.shape, land.dtype)],
        input_output_aliases={0: 0, 1: 1},
        compiler_params=pltpu.CompilerParams(has_side_effects=IN_FLIGHT),
    )(packed, land, send, recv, *after)


def _sum_devices(mine, land, *, name):
    n_dev = land.shape[0]

    def body(mine_ref, land_ref, out_ref):
        x, y, c = _place()
        me = 4 * x + 2 * y + c
        total = None
        for s in range(n_dev):
            part = jnp.where(me == s, mine_ref[...], land_ref[s])
            total = part if total is None else total + part
        out_ref[...] = total

    return pl.pallas_call(body, name=name, out_shape=jax.ShapeDtypeStruct(mine.shape, mine.dtype))(mine, land)


def _sum_received(grad, land, *, name, tr=256):
    _, r, c = grad.shape
    tr = _pick(r, tr, 8)

    def body(chip_ref, g_ref, l_ref, o_ref):
        o_ref[...] = ((g_ref[...] + l_ref[0].astype(F32)) + l_ref[1].astype(F32)) + l_ref[2].astype(F32)

    chip = (2 * lax.axis_index("x") + lax.axis_index("y")).astype(jnp.int32).reshape(1)
    return pl.pallas_call(
        body, name=name,
        grid_spec=pltpu.PrefetchScalarGridSpec(
            num_scalar_prefetch=1, grid=(r // tr,),
            in_specs=[pl.BlockSpec((None, tr, c), lambda i, chip_ref: (chip_ref[0], i, 0)),
                      pl.BlockSpec((3, tr, c), lambda i, chip_ref: (0, i, 0))],
            out_specs=pl.BlockSpec((tr, c), lambda i, chip_ref: (i, 0))),
        out_shape=jax.ShapeDtypeStruct((r, c), F32), compiler_params=_params("parallel"),
    )(chip, grad, land)


def _allreduce_small(packed, *, name, after=None):
    r, d = packed.shape
    n_dev = 8

    def body(src_ref, out_ref, buf, send, recv):
        x, y, c = _place()
        me = 4 * x + 2 * y + c
        started = []
        for p in range(1, n_dev):
            rc = pltpu.make_async_remote_copy(
                src_ref=src_ref, dst_ref=buf.at[me], send_sem=send.at[p - 1], recv_sem=recv.at[p - 1],
                device_id=(x ^ (p >> 2), y ^ ((p >> 1) & 1), c ^ (p & 1)), device_id_type=MESH)
            rc.start()
            started.append(rc)
        buf[me] = src_ref[...]
        for rc in started:
            rc.wait()
        total = buf[0]
        for s in range(1, n_dev):
            total = total + buf[s]
        out_ref[...] = total

    vmem = pl.BlockSpec(memory_space=pltpu.VMEM)
    body, more_specs, more_args = _ordered(body, 1, after)
    return pl.pallas_call(
        body, name=name, in_specs=[vmem] + more_specs, out_specs=vmem, out_shape=jax.ShapeDtypeStruct((r, d), F32),
        scratch_shapes=[pltpu.VMEM((n_dev, r, d), F32), pltpu.SemaphoreType.DMA((n_dev - 1,)),
                        pltpu.SemaphoreType.DMA((n_dev - 1,))],
    )(packed, *more_args)


def _packed_rows(size, d):
    return -(-size // (8 * d)) * 8


def _pack_rows(arrays, d):
    rows = []
    for arr in arrays:
        flat = arr.reshape(-1).astype(F32)
        n = _packed_rows(flat.shape[0], d)
        rows.append(jnp.pad(flat, (0, n * d - flat.shape[0])).reshape(n, d))
    return jnp.concatenate(rows, axis=0)


def _unpack_rows(packed, shapes, d):
    out, row = [], 0
    for shape in shapes:
        size = math.prod(shape)
        n = _packed_rows(size, d)
        out.append(packed[row:row + n].reshape(-1)[:size].reshape(shape))
        row += n
    return out


SMALL = ("norm1_g", "gate_b", "conv_b", "conv_norm_g", "q_norm_g", "k_norm_g", "norm2_g", "ffn_conv_b")
LARGE = ("w_in", "w_conv_out", "w_attn_out", "w_out", "w_up", "w_down")
WEIGHTS = ("norm1_g", "w_in", "gate_b", "conv_w", "conv_b", "conv_norm_g", "w_conv_out", "q_norm_g", "k_norm_g",
           "w_attn_out", "w_out", "norm2_g", "w_up", "ffn_conv_w", "ffn_conv_b", "w_down")


def _head_ones(dims):
    a = dims.n_heads * dims.head_dim
    head = jnp.arange(a, dtype=jnp.int32) // dims.head_dim
    return (head[:, None] == head[None, :]).astype(BF16)


def _after(vec, token):
    return vec if token is None else vec + token[0:1, 0:1]


def _local_step(dims, x, target, small, first_weights, other_weights, send_grad):
    d, f, heads = dims.d_model, dims.d_ff, dims.n_heads
    small = dict(small)
    row = lambda name: small[name].reshape(1, -1)
    head_sum = _head_sum_matrix(dims)
    head_spread = jnp.transpose(head_sum)
    ones = (head_sum, head_spread)
    gq = jnp.tile(row("q_norm_g"), (1, heads))
    gk = jnp.tile(row("k_norm_g"), (1, heads))
    one_shard = lambda w: w.reshape(1, -1, w.shape[-1])

    h = _rmsnorm_fwd(x, row("norm1_g"), name="norm1")
    full = first_weights(h)
    w_in = full["w_in"]
    conv_w = jnp.pad(full["conv_w"], ((0, CONV_HALO - dims.conv_width), (0, 0)))
    ffn_w = jnp.pad(full["ffn_conv_w"], ((0, FFN_HALO - dims.ffn_conv_width), (0, 0)))
    z = _mm_nn(h, w_in, out_dtype=BF16, after=full.get("token"), tm=2048, tn=1792, name="in_proj")
    a1, a3 = _conv_branch_fwd(z, conv_w, row("conv_b"), row("conv_norm_g"), dims, name="conv_branch")
    qkv = _qkv_layouts_fwd(z, gq, gk, ones, dims, name="qk_norm")
    per_group = {dil: _attn_fwd(*qkv[dil], dims, dil, name=f"attn_fwd_d{dil}") for dil in DILATIONS}
    o, lse = _attn_combine(per_group, head_spread, dims, name="attn_combine")
    full = other_weights(o)
    w_up = full["w_up"]
    w_co, w_ao, w_o, w_dn = (one_shard(full[k]) for k in ("w_conv_out", "w_attn_out", "w_out", "w_down"))
    ya = _mm_nn(a3, w_co, out_dtype=F32, name="conv_out_proj")
    yb = _mm_nn(o, w_ao, out_dtype=F32, name="attn_out_proj")
    mixed = _mix_fwd(ya, yb, z, row("gate_b"), dims, name="gate_mix")
    x1, h2 = _proj_residual_norm(mixed, w_o, x, row("norm2_g"), name="out_proj_norm2")
    up = _mm_nn(h2, w_up, out_dtype=F32, tm=2048, name="up_proj")
    act = _ffn_act_fwd(up, ffn_w, row("ffn_conv_b"), dims, name="ffn_act")
    dy, dy_b, loss = _proj_residual_loss(act, w_dn, x1, target, tm=512, name="down_proj_loss")

    grads = {}

    def large(name, g):
        grads[name], g_bf16 = g
        return send_grad(name, g_bf16)

    sent = large("w_down", _mm_tn(act, dy_b, n_shards=1, name="dw_down"))
    dact = _mm_nt(dy_b, w_dn, out_dtype=BF16, after=sent, name="d_act")
    dup, dfw, dfb = _ffn_bwd(dact, up, ffn_w, row("ffn_conv_b"), dims, name="ffn_bwd")
    grads["ffn_conv_w"], grads["ffn_conv_b"] = dfw[:dims.ffn_conv_width], dfb
    sent = large("w_up", _mm_tn(h2, dup, n_shards=N_CHIPS, name="dw_up"))
    dh2 = _mm_nt(dup, w_up, out_dtype=F32, after=sent, name="d_h2")
    dx1, dx1_b, grads["norm2_g"] = _rmsnorm_bwd(x1, row("norm2_g"), dh2, dy, want_bf16=True, name="norm2_bwd")
    sent = large("w_out", _mm_tn(mixed, dx1_b, n_shards=1, name="dw_out"))
    dmix = _mm_nt(dx1_b, w_o, out_dtype=F32, after=sent, name="d_mix")
    dya, dyb, dz_gate, grads["gate_b"] = _mix_bwd(dmix, ya, yb, z, row("gate_b"), dims, name="gate_mix_bwd")
    sent = large("w_attn_out", _mm_tn(o, dyb, n_shards=1, name="dw_attn_out"))
    do = _mm_nt(dyb, w_ao, out_dtype=BF16, after=sent, name="d_attn")
    dos, deltas = _attn_bwd_prep(do, o, head_sum, dims, name="attn_bwd_prep")
    dqkv = {dil: _attn_bwd(*qkv[dil], dos[dil], lse[dil], deltas[dil], dims, dil, name=f"attn_bwd_d{dil}")
            for dil in DILATIONS}
    dz_qkv, dgq, dgk = _qkv_layouts_bwd(z, dqkv, gq, gk, ones, dims, name="qk_norm_bwd")
    grads["q_norm_g"] = dgq.reshape(heads, dims.head_dim).sum(axis=0)
    grads["k_norm_g"] = dgk.reshape(heads, dims.head_dim).sum(axis=0)
    sent = large("w_conv_out", _mm_tn(a3, dya, n_shards=1, name="dw_conv_out"))
    da3 = _mm_nt(dya, w_co, out_dtype=F32, after=sent, name="d_conv_act")
    da1, grads["conv_norm_g"] = _conv_norm_bwd(da3, a1, row("conv_norm_g"), name="conv_norm_bwd")
    dz, dcw, grads["conv_b"] = _conv_branch_bwd(da1, z, conv_w, [dz_qkv, dz_gate], dims, name="conv_branch_bwd")
    grads["conv_w"] = dcw[:dims.conv_width]
    sent = large("w_in", _mm_tn(h, dz, n_shards=N_CHIPS, name="dw_in"))
    dh = _mm_nt(dz, w_in, out_dtype=F32, after=sent, name="d_h")
    dx, grads["norm1_g"] = _rmsnorm_bwd(x, row("norm1_g"), dh, dx1, want_bf16=False, name="norm1_bwd")
    return loss, dx, grads


def _step(dims, x, target, w, m, v):
    d = dims.d_model
    t = dims.tokens
    sq = lambda a: a.reshape(a.shape[1:])
    w2, m2, v2 = ({k: sq(a) for k, a in grp.items()} for grp in (w, m, v))

    conv_pad = jnp.pad(w2["conv_w"], ((0, CONV_HALO - dims.conv_width), (0, 0)))
    ffn_pad = jnp.pad(w2["ffn_conv_w"], ((0, FFN_HALO - dims.ffn_conv_width), (0, 0)))
    gathered_names = LARGE + ("conv_w", "ffn_conv_w")
    lands = dict(zip(gathered_names, _cast_to_lands([w2[k] for k in LARGE] + [conv_pad, ffn_pad],
                                                   [BF16] * len(LARGE) + [F32, F32], name="cast_weights")))
    first_names = ("w_in", "conv_w", "ffn_conv_w")
    other_names = tuple(k for k in gathered_names if k not in first_names)
    first = _gather_start([lands[k] for k in first_names], x, halved=(0,), name="gather_start_first")
    other = []
    cols = lambda g, rows: jnp.moveaxis(g, 0, 1).reshape(g.shape[1], -1)[:rows]

    def first_weights(after):
        got = dict(zip(first_names, _gather_wait(*first[:3], after, halved=(0,), name="gather_wait_first")))
        got["w_in"] = _forward_to_sibling(got["w_in"], name="forward_w_in")
        other.extend(_gather_start([lands[k] for k in other_names], got["w_in"], name="gather_start_other"))
        got["conv_w"] = cols(got["conv_w"], dims.conv_width)
        got["ffn_conv_w"] = cols(got["ffn_conv_w"], dims.ffn_conv_width)
        got["token"] = other[3]
        return got

    def other_weights(after):
        return dict(zip(other_names, _gather_wait(*other[:3], after, name="gather_wait_other")))

    started = {}

    def send_grad(name, g):
        send, recv, g_thru, land, token = _scatter_start(g.reshape(N_CHIPS, -1, g.shape[-1]), name=f"scatter_start_{name}")
        started[name] = (send, recv, g_thru, land)
        return token

    small = {k: w2[k] for k in SMALL}
    small["norm1_g"] = _after(small["norm1_g"].reshape(1, -1), first[3])
    loss, dx, grads = _local_step(dims, x.reshape(t, d), target.reshape(t, d), small, first_weights, other_weights, send_grad)

    def my_sums(names, after, tag):
        arrived = _scatter_wait([started[k] for k in names], after, name=f"scatter_wait_{tag}")
        blocks = [grads[k].reshape(N_CHIPS, -1, grads[k].shape[-1]) for k in names]
        return [_sum_received(g, land, name=f"sum_{k}") for k, g, (_, land) in zip(names, blocks, arrived)]

    def updates(names, mine, theirs):
        return {k: _adamw(w2[k], [a, b], m2[k], v2[k], name=f"adamw_{k}") for k, a, b in zip(names, mine, theirs)}

    small_names = SMALL + ("conv_w", "ffn_conv_w")
    packed = _pack_rows([grads[k] for k in small_names] + [loss[0, 0]], d)
    reducing = _allreduce_start(packed, name="allreduce_start")
    others = [k for k in LARGE if k != "w_in"]
    mine_others = my_sums(others, [dx, reducing[4]], "others")
    swapping_others = _swap_start(mine_others, name="swap_start_others")
    mine_w_in = my_sums(["w_in"], swapping_others[4], "w_in")
    swapping_w_in = _swap_start(mine_w_in, name="swap_start_w_in")
    out = updates(others, *_swap_wait(swapping_others, swapping_w_in[4], name="swap_wait_others"))
    last_updates = [out[k][1] for k in others]
    reduced = _sum_devices(*_allreduce_wait(reducing, last_updates, name="allreduce_wait"), name="allreduce_sum")
    shapes = [grads[k].shape for k in small_names] + [()]
    *small_g, loss_total = _unpack_rows(reduced, shapes, d)
    small_g = dict(zip(small_names, small_g))
    chip = 2 * lax.axis_index("x") + lax.axis_index("y")
    for k in ("conv_w", "ffn_conv_w"):
        width = w2[k].shape[1]
        small_g[k] = lax.dynamic_slice_in_dim(small_g[k], chip * width, width, axis=1)

    small_shapes = [w2[k].shape for k in small_names]
    pack = lambda grp: _pack_rows([grp[k] for k in small_names], d)
    results = _adamw(pack(w2), [pack(small_g)], pack(m2), pack(v2), name="adamw_small")
    unpacked = [_unpack_rows(r, small_shapes, d) for r in results]
    out.update({k: tuple(u[i] for u in unpacked) for i, k in enumerate(small_names)})
    out.update(updates(["w_in"], *_swap_wait(swapping_w_in, results[1], name="swap_wait_w_in")))

    lead =lambda a: a.reshape((1,) + a.shape)
    ordered = [[lead(out[k][j].reshape(w2[k].shape)) for k in WEIGHTS] for j in range(4)]
    return (loss_total, dx.reshape(x.shape), *ordered[0], *ordered[1], *ordered[2], *ordered[3])


def kernel(x, norm1_g, w_in, gate_b, conv_w, conv_b, conv_norm_g, w_conv_out, q_norm_g, k_norm_g, w_attn_out, w_out, norm2_g, w_up, ffn_conv_w, ffn_conv_b, w_down, loss_target, m_norm1_g, m_w_in, m_gate_b, m_conv_w, m_conv_b, m_conv_norm_g, m_w_conv_out, m_q_norm_g, m_k_norm_g, m_w_attn_out, m_w_out, m_norm2_g, m_w_up, m_ffn_conv_w, m_ffn_conv_b, m_w_down, v_norm1_g, v_w_in, v_gate_b, v_conv_w, v_conv_b, v_conv_norm_g, v_w_conv_out, v_q_norm_g, v_k_norm_g, v_w_attn_out, v_w_out, v_norm2_g, v_w_up, v_ffn_conv_w, v_ffn_conv_b, v_w_down):
    w = dict(zip(WEIGHTS, (norm1_g, w_in, gate_b, conv_w, conv_b, conv_norm_g, w_conv_out, q_norm_g, k_norm_g,
                           w_attn_out, w_out, norm2_g, w_up, ffn_conv_w, ffn_conv_b, w_down)))
    m = dict(zip(WEIGHTS, (m_norm1_g, m_w_in, m_gate_b, m_conv_w, m_conv_b, m_conv_norm_g, m_w_conv_out, m_q_norm_g,
                           m_k_norm_g, m_w_attn_out, m_w_out, m_norm2_g, m_w_up, m_ffn_conv_w, m_ffn_conv_b, m_w_down)))
    v = dict(zip(WEIGHTS, (v_norm1_g, v_w_in, v_gate_b, v_conv_w, v_conv_b, v_conv_norm_g, v_w_conv_out, v_q_norm_g,
                           v_k_norm_g, v_w_attn_out, v_w_out, v_norm2_g, v_w_up, v_ffn_conv_w, v_ffn_conv_b, v_w_down)))
    dims = Dims(d_model=x.shape[-1], batch_local=x.shape[0], seq=x.shape[1], d_ff=w_down.shape[1] * N_CHIPS)
    return _step(dims, x, loss_target, w, m, v)
```

```python
import functools
import math
from typing import NamedTuple

import jax
import jax.numpy as jnp
from jax import lax
from jax.experimental import pallas as pl
from jax.experimental.pallas import tpu as pltpu

F32 = jnp.float32
BF16 = jnp.bfloat16

RMS_EPS = 1e-6
MASKED_SCORE = -1e30
ATTN_BLOCK = 128
DILATIONS = (1, 4, 16)
CONV_HALO = 32
FFN_HALO = 8
ADAM_LR, ADAM_B1, ADAM_B2, ADAM_EPS, ADAM_WD, ADAM_STEP = 0.001, 0.9, 0.999, 1e-08, 0.01, 10
V7X_VMEM_LIMIT_BYTES = 56 * 2 ** 20
N_CHIPS = 4
MESH = pl.DeviceIdType.MESH


class Dims(NamedTuple):
    d_model: int = 1024
    n_heads: int = 16
    head_dim: int = 64
    d_ff: int = 2816
    seq: int = 2048
    batch_local: int = 2
    conv_width: int = 31
    ffn_conv_width: int = 3

    @property
    def tokens(self):
        return self.seq * self.batch_local


def _params(*semantics):
    return pltpu.CompilerParams(dimension_semantics=semantics, vmem_limit_bytes=V7X_VMEM_LIMIT_BYTES)


ANY = pl.BlockSpec(memory_space=pl.ANY)


def _ordered(body, n_inputs, after):
    after = [] if after is None else list(after) if isinstance(after, (list, tuple)) else [after]
    if not after:
        return body, [], []

    def wrapped(*refs):
        return body(*refs[:n_inputs], *refs[n_inputs + len(after):])

    return wrapped, [ANY] * len(after), after


def _pick(n, target, mult=128):
    if n <= target:
        return n
    best = None
    for t in range(mult, target + 1, mult):
        if n % t == 0:
            best = t
    assert best is not None, (n, target, mult)
    return best


def _sigmoid(v):
    return 1.0 / (1.0 + jnp.exp(-v))


def _mm_nn(a, w, *, out_dtype, name, residual=None, after=None, tm=1024, tn=1408, tk=2816):
    m, k = a.shape
    nsh, k2, c = w.shape
    assert k == k2 and a.dtype == BF16 and w.dtype == BF16
    n = nsh * c
    tm, tn, tk = _pick(m, tm, 8), _pick(c, tn), _pick(k, tk)
    nk, cpn = k // tk, c // tn

    def body(*refs):
        if residual is None:
            a_ref, w_ref, o_ref, acc = refs
        else:
            a_ref, w_ref, r_ref, o_ref, acc = refs
        prod = jnp.dot(a_ref[...], w_ref[...], preferred_element_type=F32)

        def finish(total):
            if residual is not None:
                total = total + r_ref[...]
            o_ref[...] = total.astype(out_dtype)

        if nk == 1:
            finish(prod)
        else:
            kk = pl.program_id(2)

            @pl.when(kk == 0)
            def _():
                acc[...] = prod

            @pl.when(kk > 0)
            def _():
                acc[...] += prod

            @pl.when(kk == nk - 1)
            def _():
                finish(acc[...])

    in_specs = [pl.BlockSpec((tm, tk), lambda i, j, kk: (i, kk)),
                pl.BlockSpec((None, tk, tn), lambda i, j, kk: (j // cpn, kk, j % cpn))]
    args = [a, w]
    if residual is not None:
        in_specs.append(pl.BlockSpec((tm, tn), lambda i, j, kk: (i, j)))
        args.append(residual)
    body, more_specs, more_args = _ordered(body, len(args), after)
    return pl.pallas_call(
        body, name=name, grid=(m // tm, n // tn, nk),
        in_specs=in_specs + more_specs, out_specs=pl.BlockSpec((tm, tn), lambda i, j, kk: (i, j)),
        out_shape=jax.ShapeDtypeStruct((m, n), out_dtype),
        scratch_shapes=[pltpu.VMEM((tm, tn) if nk > 1 else (8, 128), F32)],
        compiler_params=_params("parallel", "parallel", "arbitrary"),
    )(*args, *more_args)


def _proj_residual_norm(a, w, residual, g, *, name, tm=1024):
    m, k = a.shape
    _, k2, n = w.shape
    assert w.shape[0] == 1 and k == k2 and a.dtype == BF16 and w.dtype == BF16
    tm = _pick(m, tm, 8)

    def body(a_ref, w_ref, r_ref, g_ref, y_ref, h_ref):
        y = r_ref[...] + jnp.dot(a_ref[...], w_ref[...], preferred_element_type=F32)
        y_ref[...] = y
        h_ref[...] = (y * lax.rsqrt(jnp.mean(y * y, axis=-1, keepdims=True) + RMS_EPS) * g_ref[...]).astype(BF16)

    rows = lambda width: pl.BlockSpec((tm, width), lambda i: (i, 0))
    return pl.pallas_call(
        body, name=name, grid=(m // tm,),
        in_specs=[rows(k), pl.BlockSpec((None, k, n), lambda i: (0, 0, 0)), rows(n), pl.BlockSpec((1, n), lambda i: (0, 0))],
        out_specs=[rows(n), rows(n)],
        out_shape=[jax.ShapeDtypeStruct((m, n), F32), jax.ShapeDtypeStruct((m, n), BF16)],
        compiler_params=_params("parallel"),
    )(a, w, residual, g)


def _proj_residual_loss(a, w, residual, target, *, name, tm=1024):
    m, k = a.shape
    _, k2, n = w.shape
    assert w.shape[0] == 1 and k == k2 and a.dtype == BF16 and w.dtype == BF16
    tm = _pick(m, tm, 8)

    def body(a_ref, w_ref, r_ref, t_ref, dy_ref, dyb_ref, loss_ref):
        err = r_ref[...] + jnp.dot(a_ref[...], w_ref[...], preferred_element_type=F32) - t_ref[...]
        dy = err * (1.0 / n)
        dy_ref[...] = dy
        dyb_ref[...] = dy.astype(BF16)
        part = jnp.sum(jnp.sum(err * err, axis=-1, keepdims=True), axis=0, keepdims=True) * (0.5 / n)
        _accumulate(loss_ref, jnp.broadcast_to(part, (8, 128)), pl.program_id(0) == 0)

    rows = lambda width: pl.BlockSpec((tm, width), lambda i: (i, 0))
    return pl.pallas_call(
        body, name=name, grid=(m // tm,),
        in_specs=[rows(k), pl.BlockSpec((None, k, n), lambda i: (0, 0, 0)), rows(n), rows(n)],
        out_specs=[rows(n), rows(n), pl.BlockSpec((8, 128), lambda i: (0, 0))],
        out_shape=[jax.ShapeDtypeStruct((m, n), F32), jax.ShapeDtypeStruct((m, n), BF16),
                   jax.ShapeDtypeStruct((8, 128), F32)],
        compiler_params=_params("arbitrary"),
    )(a, w, residual, target)


def _mm_nt(a, w, *, out_dtype, name, after=None, tm=1024, tn=1408, tk=1792):
    m, k = a.shape
    nsh, r, c = w.shape
    assert k == nsh * c and a.dtype == BF16 and w.dtype == BF16
    tm, tn, tk = _pick(m, tm, 8), _pick(r, tn), _pick(c, tk)
    nk, cpk = k // tk, c // tk

    def body(a_ref, w_ref, o_ref, acc):
        prod = lax.dot_general(a_ref[...], w_ref[...], (((1,), (1,)), ((), ())), preferred_element_type=F32)
        if nk == 1:
            o_ref[...] = prod.astype(out_dtype)
        else:
            kk = pl.program_id(2)

            @pl.when(kk == 0)
            def _():
                acc[...] = prod

            @pl.when(kk > 0)
            def _():
                acc[...] += prod

            @pl.when(kk == nk - 1)
            def _():
                o_ref[...] = acc[...].astype(out_dtype)

    body, more_specs, more_args = _ordered(body, 2, after)
    return pl.pallas_call(
        body, name=name, grid=(m // tm, r // tn, nk),
        in_specs=[pl.BlockSpec((tm, tk), lambda i, j, kk: (i, kk)),
                  pl.BlockSpec((None, tn, tk), lambda i, j, kk: (kk // cpk, j, kk % cpk))] + more_specs,
        out_specs=pl.BlockSpec((tm, tn), lambda i, j, kk: (i, j)),
        out_shape=jax.ShapeDtypeStruct((m, r), out_dtype),
        scratch_shapes=[pltpu.VMEM((tm, tn) if nk > 1 else (8, 128), F32)],
        compiler_params=_params("parallel", "parallel", "arbitrary"),
    )(a, w, *more_args)


MM_TN_VMEM_BYTES = 44 * 2 ** 20


def _mm_tn(a, b, *, n_shards, name, tm=1408, tn=1408):
    t, m = a.shape
    t2, n = b.shape
    assert t == t2 and a.dtype == BF16 and b.dtype == BF16
    c = n // n_shards
    tm, tn = _pick(m, tm), _pick(c, tn)
    if m // tm == 1 and n // tn == 1 and tn % (2 * LANES) == 0:
        tn //= 2
    fixed = 2 * tm * tn * 6
    if 4 * t * (tm + tn) + fixed <= MM_TN_VMEM_BYTES:
        tk = t
    else:
        tk = _pick(t, (MM_TN_VMEM_BYTES - fixed - 4 * tm * tn) // (4 * (tm + tn)), 8)
    nk, cpn = t // tk, c // tn

    def body(a_ref, b_ref, o_ref, ob_ref, acc):
        kk = pl.program_id(2)
        prod = lax.dot_general(a_ref[...], b_ref[...], (((0,), (0,)), ((), ())), preferred_element_type=F32)

        def finish(total):
            o_ref[...] = total
            ob_ref[...] = total.astype(BF16)

        if nk == 1:
            finish(prod)
        else:
            @pl.when(kk == 0)
            def _():
                acc[...] = prod

            @pl.when(kk > 0)
            def _():
                acc[...] += prod

            @pl.when(kk == nk - 1)
            def _():
                finish(acc[...])

    out_spec = pl.BlockSpec((None, tm, tn), lambda i, j, kk: (j // cpn, i, j % cpn))
    return pl.pallas_call(
        body, name=name, grid=(m // tm, n // tn, nk),
        in_specs=[pl.BlockSpec((tk, tm), lambda i, j, kk: (kk, i)),
                  pl.BlockSpec((tk, tn), lambda i, j, kk: (kk, j))],
        out_specs=[out_spec, out_spec],
        out_shape=[jax.ShapeDtypeStruct((n_shards, m, c), F32), jax.ShapeDtypeStruct((n_shards, m, c), BF16)],
        scratch_shapes=[pltpu.VMEM((tm, tn) if nk > 1 else (8, 128), F32)],
        compiler_params=_params("parallel", "parallel", "arbitrary"),
    )(a, b)


def _row_spec(tr, width, col=0):
    return pl.BlockSpec((tr, width), lambda i, col=col: (i, col))


def _vec_spec(width, col=0):
    return pl.BlockSpec((1, width), lambda i, col=col: (0, col))


def _accumulate(ref, value, first):
    @pl.when(first)
    def _():
        ref[...] = value

    @pl.when(jnp.logical_not(first))
    def _():
        ref[...] += value


def _rmsnorm_fwd(x, g, *, name, tr=512):
    t, d = x.shape
    tr = _pick(t, tr, 8)

    def body(x_ref, g_ref, o_ref):
        xv = x_ref[...]
        r = lax.rsqrt(jnp.mean(xv * xv, axis=-1, keepdims=True) + RMS_EPS)
        o_ref[...] = (xv * r * g_ref[...]).astype(BF16)

    return pl.pallas_call(
        body, name=name, grid=(t // tr,),
        in_specs=[_row_spec(tr, d), _vec_spec(d)], out_specs=_row_spec(tr, d),
        out_shape=jax.ShapeDtypeStruct((t, d), BF16), compiler_params=_params("parallel"),
    )(x, g)


def _rmsnorm_bwd(x, g, dy, dres, *, name, want_bf16, tr=512):
    t, d = x.shape
    tr = _pick(t, tr, 8)

    def body(x_ref, g_ref, dy_ref, dres_ref, *outs):
        dx_ref, dg_ref = outs[0], outs[-1]
        xv, dyv = x_ref[...], dy_ref[...].astype(F32)
        r = lax.rsqrt(jnp.mean(xv * xv, axis=-1, keepdims=True) + RMS_EPS)
        gy = dyv * g_ref[...]
        dx = dres_ref[...] + r * gy - xv * (r * r * r) * jnp.mean(xv * gy, axis=-1, keepdims=True)
        dx_ref[...] = dx
        if want_bf16:
            outs[1][...] = dx.astype(BF16)
        _accumulate(dg_ref, jnp.sum(dyv * xv * r, axis=0, keepdims=True), pl.program_id(0) == 0)

    out_shape = [jax.ShapeDtypeStruct((t, d), F32)]
    out_specs = [_row_spec(tr, d)]
    if want_bf16:
        out_shape.append(jax.ShapeDtypeStruct((t, d), BF16))
        out_specs.append(_row_spec(tr, d))
    out_shape.append(jax.ShapeDtypeStruct((1, d), F32))
    out_specs.append(_vec_spec(d))
    return pl.pallas_call(
        body, name=name, grid=(t // tr,),
        in_specs=[_row_spec(tr, d), _vec_spec(d), _row_spec(tr, d), _row_spec(tr, d)],
        out_specs=out_specs, out_shape=out_shape, compiler_params=_params("arbitrary"),
    )(x, g, dy, dres)


def _head_mean(v, ones_ref, head_dim):
    hi = v.astype(BF16)
    lo = (v - hi.astype(F32)).astype(BF16)
    e = ones_ref[...]
    total = jnp.dot(hi, e, preferred_element_type=F32) + jnp.dot(lo, e, preferred_element_type=F32)
    return total * (1.0 / head_dim)


def _qkv_fwd(z, gq, gk, head_ones, dims, *, name, tr=256):
    t = z.shape[0]
    a = dims.n_heads * dims.head_dim
    tr = _pick(t, tr, 8)
    q_scale = dims.head_dim ** -0.5

    def body(q_ref, k_ref, v_ref, gq_ref, gk_ref, e_ref, qo_ref, ko_ref, vo_ref):
        qv, kv = q_ref[...], k_ref[...]
        rq = lax.rsqrt(_head_mean(qv * qv, e_ref, dims.head_dim) + RMS_EPS)
        rk = lax.rsqrt(_head_mean(kv * kv, e_ref, dims.head_dim) + RMS_EPS)
        qo_ref[...] = (qv * rq * gq_ref[...] * q_scale).astype(BF16)
        ko_ref[...] = (kv * rk * gk_ref[...]).astype(BF16)
        vo_ref[...] = v_ref[...].astype(BF16)

    return pl.pallas_call(
        body, name=name, grid=(t // tr,),
        in_specs=[_row_spec(tr, a, 2), _row_spec(tr, a, 3), _row_spec(tr, a, 4), _vec_spec(a), _vec_spec(a),
                  pl.BlockSpec((a, a), lambda i: (0, 0))],
        out_specs=[_row_spec(tr, a)] * 3, out_shape=[jax.ShapeDtypeStruct((t, a), BF16)] * 3,
        compiler_params=_params("parallel"),
    )(z, z, z, gq, gk, head_ones)


def _qkv_bwd(z, dqs, dks, dvs, gq, gk, head_ones, dims, *, name, tr=256):
    t = z.shape[0]
    a = dims.n_heads * dims.head_dim
    tr = _pick(t, tr, 8)
    q_scale = dims.head_dim ** -0.5
    ng = len(dqs)

    def body(*refs):
        q_ref, k_ref = refs[:2]
        dq_refs, dk_refs, dv_refs = refs[2:2 + ng], refs[2 + ng:2 + 2 * ng], refs[2 + 2 * ng:2 + 3 * ng]
        gq_ref, gk_ref, e_ref = refs[2 + 3 * ng:5 + 3 * ng]
        dz_ref, dgq_ref, dgk_ref = refs[5 + 3 * ng:]
        first = pl.program_id(0) == 0

        def norm_bwd(x_ref, d_refs, g_ref, scale, col, dg_ref):
            xv = x_ref[...]
            dy = sum(r[...] for r in d_refs) * scale
            r = lax.rsqrt(_head_mean(xv * xv, e_ref, dims.head_dim) + RMS_EPS)
            gy = dy * g_ref[...]
            dx = r * gy - xv * (r * r * r) * _head_mean(xv * gy, e_ref, dims.head_dim)
            dz_ref[:, col * a:(col + 1) * a] = dx.astype(BF16)
            _accumulate(dg_ref, jnp.sum(dy * xv * r, axis=0, keepdims=True), first)

        norm_bwd(q_ref, dq_refs, gq_ref, q_scale, 0, dgq_ref)
        norm_bwd(k_ref, dk_refs, gk_ref, 1.0, 1, dgk_ref)
        dz_ref[:, 2 * a:3 * a] = sum(r[...] for r in dv_refs).astype(BF16)

    in_specs = ([_row_spec(tr, a, 2), _row_spec(tr, a, 3)] + [_row_spec(tr, a)] * (3 * ng)
                + [_vec_spec(a), _vec_spec(a), pl.BlockSpec((a, a), lambda i: (0, 0))])
    return pl.pallas_call(
        body, name=name, grid=(t // tr,), in_specs=in_specs,
        out_specs=[_row_spec(tr, 3 * a), _vec_spec(a), _vec_spec(a)],
        out_shape=[jax.ShapeDtypeStruct((t, 3 * a), BF16)] + [jax.ShapeDtypeStruct((1, a), F32)] * 2,
        compiler_params=_params("arbitrary"),
    )(z, z, *dqs, *dks, *dvs, gq, gk, head_ones)


CONV_ROWS = 16


def _seq_specs(dims, ts, width, halo, col, *, nxt=False):
    nst, per = dims.seq // ts, ts // halo
    last = dims.tokens // halo - 1
    cur = pl.BlockSpec((ts, width), lambda b, i: (b * nst + i, col))
    if nxt:
        edge = pl.BlockSpec((halo, width), lambda b, i: (jnp.minimum((b * nst + i + 1) * per, last), col))
    else:
        edge = pl.BlockSpec((halo, width), lambda b, i: (jnp.maximum((b * nst + i) * per - 1, 0), col))
    return cur, edge


SUBLANES = 8


def _shifted_copies(buf, shifted):
    rows = shifted.shape[1]
    for s in range(1, SUBLANES):
        shifted[s - 1] = buf[pl.ds(s, rows), :]


def _window(buf, shifted, start, size):
    a, s = divmod(start, SUBLANES)
    src = buf if s == 0 else shifted.at[s - 1]
    return src[pl.ds(SUBLANES * a, size), :]


def _conv_branch_fwd(z, w, b, g, dims, *, name, ts=128):
    t, c, kw = z.shape[0], dims.d_model, dims.conv_width
    base = CONV_HALO - (kw - 1)

    def body(av_ref, hv_ref, ag_ref, hg_ref, w_ref, b_ref, g_ref, a1_ref, a3_ref, buf, shifted):
        i = pl.program_id(1)
        buf[CONV_HALO:, :] = av_ref[...].astype(F32) * _sigmoid(ag_ref[...].astype(F32))
        buf[0:CONV_HALO, :] = jnp.where(i > 0, hv_ref[...].astype(F32) * _sigmoid(hg_ref[...].astype(F32)), 0.0)
        _shifted_copies(buf, shifted)
        for r0 in range(0, ts, CONV_ROWS):
            acc = jnp.broadcast_to(b_ref[...], (CONV_ROWS, c))
            for k in range(kw):
                acc = acc + w_ref[k:k + 1, :] * _window(buf, shifted, r0 + base + k, CONV_ROWS)
            a1_ref[r0:r0 + CONV_ROWS, :] = acc
            a2 = acc * lax.rsqrt(jnp.mean(acc * acc, axis=-1, keepdims=True) + RMS_EPS) * g_ref[...]
            a3_ref[r0:r0 + CONV_ROWS, :] = (a2 * _sigmoid(a2)).astype(BF16)

    vec = pl.BlockSpec((1, c), lambda b, i: (0, 0))
    out = pl.BlockSpec((ts, c), lambda b, i: (b * (dims.seq // ts) + i, 0))
    return pl.pallas_call(
        body, name=name, grid=(dims.batch_local, dims.seq // ts),
        in_specs=[*_seq_specs(dims, ts, c, CONV_HALO, 0), *_seq_specs(dims, ts, c, CONV_HALO, 1),
                  pl.BlockSpec((CONV_HALO, c), lambda b, i: (0, 0)), vec, vec],
        out_specs=[out, out],
        out_shape=[jax.ShapeDtypeStruct((t, c), F32), jax.ShapeDtypeStruct((t, c), BF16)],
        scratch_shapes=[pltpu.VMEM((CONV_HALO + ts, c), F32),
                        pltpu.VMEM((SUBLANES - 1, CONV_HALO + ts - SUBLANES, c), F32)],
        compiler_params=_params("parallel", "parallel"),
    )(z, z, z, z, w, b, g)


def _conv_norm_bwd(da3, a1, g, *, name, tr=256):
    t, c = a1.shape
    tr = _pick(t, tr, 8)

    def body(d_ref, a_ref, g_ref, o_ref, dg_ref):
        a1v, gv = a_ref[...], g_ref[...]
        r = lax.rsqrt(jnp.mean(a1v * a1v, axis=-1, keepdims=True) + RMS_EPS)
        a2 = a1v * r * gv
        sg = _sigmoid(a2)
        da2 = d_ref[...].astype(F32) * sg * (1.0 + a2 * (1.0 - sg))
        gy = da2 * gv
        o_ref[...] = r * gy - a1v * (r * r * r) * jnp.mean(a1v * gy, axis=-1, keepdims=True)
        _accumulate(dg_ref, jnp.sum(da2 * a1v * r, axis=0, keepdims=True), pl.program_id(0) == 0)

    return pl.pallas_call(
        body, name=name, grid=(t // tr,),
        in_specs=[_row_spec(tr, c), _row_spec(tr, c), _vec_spec(c)],
        out_specs=[_row_spec(tr, c), _vec_spec(c)],
        out_shape=[jax.ShapeDtypeStruct((t, c), F32), jax.ShapeDtypeStruct((1, c), F32)],
        compiler_params=_params("arbitrary"),
    )(da3, a1, g)


def _conv_branch_bwd(da1, z, w, rest_of_dz, dims, *, name, ts=128):
    t, c, kw = z.shape[0], dims.d_model, dims.conv_width
    nst = dims.seq // ts
    base = CONV_HALO - (kw - 1)
    n_rest = len(rest_of_dz)
    total = 2 * c + sum(r.shape[1] for r in rest_of_dz)

    def body(d_ref, dn_ref, av_ref, hv_ref, ag_ref, hg_ref, w_ref, *more):
        rest_refs = more[:n_rest]
        dz_ref, dw_ref, db_ref, abuf, dbuf, ashift, dshift = more[n_rest:]
        col = 2 * c
        for r in rest_refs:
            dz_ref[:, col:col + r.shape[1]] = r[...]
            col += r.shape[1]
        i = pl.program_id(1)
        first = jnp.logical_and(pl.program_id(0) == 0, i == 0)
        abuf[CONV_HALO:, :] = av_ref[...].astype(F32) * _sigmoid(ag_ref[...].astype(F32))
        abuf[0:CONV_HALO, :] = jnp.where(i > 0, hv_ref[...].astype(F32) * _sigmoid(hg_ref[...].astype(F32)), 0.0)
        d1 = d_ref[...]
        dbuf[0:ts, :] = d1
        dbuf[ts:, :] = jnp.where(i < nst - 1, dn_ref[...], 0.0)
        _shifted_copies(abuf, ashift)
        _shifted_copies(dbuf, dshift)

        @pl.when(first)
        def _():
            dw_ref[...] = jnp.zeros_like(dw_ref)
            db_ref[...] = jnp.zeros_like(db_ref)

        db_ref[...] += jnp.sum(d1, axis=0, keepdims=True)
        for k in range(kw):
            dw_ref[k:k + 1, :] += jnp.sum(d1 * _window(abuf, ashift, base + k, ts), axis=0, keepdims=True)
        for r0 in range(0, ts, CONV_ROWS):
            acc = jnp.zeros((CONV_ROWS, c), F32)
            for k in range(kw):
                acc = acc + w_ref[k:k + 1, :] * _window(dbuf, dshift, r0 + (kw - 1) - k, CONV_ROWS)
            av = av_ref[r0:r0 + CONV_ROWS, :].astype(F32)
            sg = _sigmoid(ag_ref[r0:r0 + CONV_ROWS, :].astype(F32))
            dz_ref[r0:r0 + CONV_ROWS, 0:c] = (acc * sg).astype(BF16)
            dz_ref[r0:r0 + CONV_ROWS, c:2 * c] = (acc * av * sg * (1.0 - sg)).astype(BF16)

    cur, nxt = _seq_specs(dims, ts, c, CONV_HALO, 0, nxt=True)
    return pl.pallas_call(
        body, name=name, grid=(dims.batch_local, nst),
        in_specs=[cur, nxt, *_seq_specs(dims, ts, c, CONV_HALO, 0), *_seq_specs(dims, ts, c, CONV_HALO, 1),
                  pl.BlockSpec((CONV_HALO, c), lambda b, i: (0, 0))]
        + [pl.BlockSpec((ts, r.shape[1]), lambda b, i: (b * nst + i, 0)) for r in rest_of_dz],
        out_specs=[pl.BlockSpec((ts, total), lambda b, i: (b * nst + i, 0)),
                   pl.BlockSpec((CONV_HALO, c), lambda b, i: (0, 0)), pl.BlockSpec((1, c), lambda b, i: (0, 0))],
        out_shape=[jax.ShapeDtypeStruct((t, total), BF16), jax.ShapeDtypeStruct((CONV_HALO, c), F32),
                   jax.ShapeDtypeStruct((1, c), F32)],
        scratch_shapes=[pltpu.VMEM((CONV_HALO + ts, c), F32)] * 2
        + [pltpu.VMEM((SUBLANES - 1, CONV_HALO + ts - SUBLANES, c), F32)] * 2,
        compiler_params=_params("arbitrary", "arbitrary"),
    )(da1, da1, z, z, z, z, w, *rest_of_dz)


FFN_ROWS = 16
FFN_COLS = 256


def _ffn_chunks(ts, f):
    cw = _pick(f, FFN_COLS)
    return [(r0, c0, cw) for r0 in range(0, ts, FFN_ROWS) for c0 in range(0, f, cw)]


def _tap_sources(buf, moved, offsets, rows):
    taps, used = [], 0
    for off in offsets:
        if off % SUBLANES:
            moved[used] = buf[pl.ds(off, rows), :]
            taps.append((moved.at[used], 0))
            used += 1
        else:
            taps.append((buf, off))
    return taps


def _moved_copies(offsets):
    return sum(1 for off in offsets if off % SUBLANES)


def _taps_sum(taps, w_ref, init, r0, cols):
    for k, (src, off) in enumerate(taps):
        init = init + w_ref[k:k + 1, cols] * src[pl.ds(off + r0, init.shape[0]), cols]
    return init


def _ffn_bwd(dact, up, w, b, dims, *, name, ts=128):
    t, f, kw = up.shape[0], dims.d_ff, dims.ffn_conv_width
    nst = dims.seq // ts
    fwd_offsets = [FFN_HALO - (kw - 1) + k for k in range(kw)]
    bwd_offsets = [(kw - 1) - k for k in range(kw)]
    dact_halo = 2 * FFN_HALO

    def body(d_ref, dn_ref, up_ref, hp_ref, hn_ref, w_ref, b_ref, o_ref, dw_ref, db_ref, buf, moved, dbuf, dmoved):
        i = pl.program_id(1)
        first = jnp.logical_and(pl.program_id(0) == 0, i == 0)
        more = i < nst - 1
        buf[0:FFN_HALO, :] = jnp.where(i > 0, hp_ref[...], 0.0)
        buf[FFN_HALO:FFN_HALO + ts, :] = up_ref[...]
        buf[FFN_HALO + ts:, :] = hn_ref[...]
        taps = _tap_sources(buf, moved, fwd_offsets, ts + FFN_HALO)

        def du_chunk(r0, rows, c0, cw, d):
            vcols, gcols = slice(c0, c0 + cw), slice(f + c0, f + c0 + cw)
            uv = _taps_sum(taps, w_ref, jnp.broadcast_to(b_ref[:, vcols], (rows, cw)), r0, vcols)
            ug = _taps_sum(taps, w_ref, jnp.broadcast_to(b_ref[:, gcols], (rows, cw)), r0, gcols)
            sg = _sigmoid(ug)
            dbuf[r0:r0 + rows, vcols] = d * ug * sg
            dbuf[r0:r0 + rows, gcols] = d * uv * sg * (1.0 + ug * (1.0 - sg))

        for r0, c0, cw in _ffn_chunks(ts, f):
            du_chunk(r0, FFN_ROWS, c0, cw, d_ref[r0:r0 + FFN_ROWS, c0:c0 + cw].astype(F32))
        for _, c0, cw in _ffn_chunks(FFN_ROWS, f):
            d_next = dn_ref[:, c0:c0 + cw].astype(F32)[0:FFN_HALO]
            du_chunk(ts, FFN_HALO, c0, cw, jnp.where(more, d_next, 0.0))

        @pl.when(first)
        def _():
            dw_ref[...] = jnp.zeros_like(dw_ref)
            db_ref[...] = jnp.zeros_like(db_ref)

        du = dbuf[0:ts, :]
        db_ref[...] += jnp.sum(du, axis=0, keepdims=True)
        for k, (src, off) in enumerate(taps):
            dw_ref[k:k + 1, :] += jnp.sum(du * src[pl.ds(off, ts), :], axis=0, keepdims=True)

        dtaps = _tap_sources(dbuf, dmoved, bwd_offsets, ts)
        for r0, c0, cw in _ffn_chunks(ts, 2 * f):
            cols = slice(c0, c0 + cw)
            o_ref[r0:r0 + FFN_ROWS, cols] = _taps_sum(dtaps, w_ref, jnp.zeros((FFN_ROWS, cw), F32), r0, cols).astype(BF16)

    up_cur, up_prev = _seq_specs(dims, ts, 2 * f, FFN_HALO, 0)
    _, up_next = _seq_specs(dims, ts, 2 * f, FFN_HALO, 0, nxt=True)
    d_cur, d_next = _seq_specs(dims, ts, f, dact_halo, 0, nxt=True)
    full = lambda rows: pl.BlockSpec((rows, 2 * f), lambda b_, i: (0, 0))
    return pl.pallas_call(
        body, name=name, grid=(dims.batch_local, nst),
        in_specs=[d_cur, d_next, up_cur, up_prev, up_next, full(FFN_HALO), full(1)],
        out_specs=[pl.BlockSpec((ts, 2 * f), lambda b_, i: (b_ * nst + i, 0)), full(FFN_HALO), full(1)],
        out_shape=[jax.ShapeDtypeStruct((t, 2 * f), BF16), jax.ShapeDtypeStruct((FFN_HALO, 2 * f), F32),
                   jax.ShapeDtypeStruct((1, 2 * f), F32)],
        scratch_shapes=[pltpu.VMEM((ts + 2 * FFN_HALO, 2 * f), F32),
                        pltpu.VMEM((_moved_copies(fwd_offsets), ts + FFN_HALO, 2 * f), F32),
                        pltpu.VMEM((ts + FFN_HALO, 2 * f), F32),
                        pltpu.VMEM((_moved_copies(bwd_offsets), ts, 2 * f), F32)],
        compiler_params=_params("arbitrary", "arbitrary"),
    )(dact, dact, up, up, up, w, b)


def _ffn_act_fwd(up, w, b, dims, *, name, ts=128):
    t, f, kw = up.shape[0], dims.d_ff, dims.ffn_conv_width
    offsets = [FFN_HALO - (kw - 1) + k for k in range(kw)]

    def body(up_ref, h_ref, w_ref, b_ref, o_ref, buf, moved):
        buf[FFN_HALO:, :] = up_ref[...]
        buf[0:FFN_HALO, :] = jnp.where(pl.program_id(1) > 0, h_ref[...], 0.0)
        taps = _tap_sources(buf, moved, offsets, ts)
        for r0, c0, cw in _ffn_chunks(ts, f):
            vcols, gcols = slice(c0, c0 + cw), slice(f + c0, f + c0 + cw)
            uv = _taps_sum(taps, w_ref, jnp.broadcast_to(b_ref[:, vcols], (FFN_ROWS, cw)), r0, vcols)
            ug = _taps_sum(taps, w_ref, jnp.broadcast_to(b_ref[:, gcols], (FFN_ROWS, cw)), r0, gcols)
            o_ref[r0:r0 + FFN_ROWS, vcols] = (ug * _sigmoid(ug) * uv).astype(BF16)

    full = lambda rows: pl.BlockSpec((rows, 2 * f), lambda b_, i: (0, 0))
    return pl.pallas_call(
        body, name=name, grid=(dims.batch_local, dims.seq // ts),
        in_specs=[*_seq_specs(dims, ts, 2 * f, FFN_HALO, 0), full(FFN_HALO), full(1)],
        out_specs=pl.BlockSpec((ts, f), lambda b_, i: (b_ * (dims.seq // ts) + i, 0)),
        out_shape=jax.ShapeDtypeStruct((t, f), BF16),
        scratch_shapes=[pltpu.VMEM((FFN_HALO + ts, 2 * f), F32), pltpu.VMEM((_moved_copies(offsets), ts, 2 * f), F32)],
        compiler_params=_params("parallel", "parallel"),
    )(up, up, w, b)


def _ffn_act_bwd(dact, up, w, b, dims, *, name, ts=128):
    t, f, kw = up.shape[0], dims.d_ff, dims.ffn_conv_width
    offsets = [FFN_HALO - (kw - 1) + k for k in range(kw)]

    def body(d_ref, up_ref, h_ref, w_ref, b_ref, du_ref, dw_ref, db_ref, buf, moved):
        i = pl.program_id(1)
        first = jnp.logical_and(pl.program_id(0) == 0, i == 0)
        buf[FFN_HALO:, :] = up_ref[...]
        buf[0:FFN_HALO, :] = jnp.where(i > 0, h_ref[...], 0.0)
        taps = _tap_sources(buf, moved, offsets, ts)
        for r0, c0, cw in _ffn_chunks(ts, f):
            vcols, gcols = slice(c0, c0 + cw), slice(f + c0, f + c0 + cw)
            uv = _taps_sum(taps, w_ref, jnp.broadcast_to(b_ref[:, vcols], (FFN_ROWS, cw)), r0, vcols)
            ug = _taps_sum(taps, w_ref, jnp.broadcast_to(b_ref[:, gcols], (FFN_ROWS, cw)), r0, gcols)
            d = d_ref[r0:r0 + FFN_ROWS, vcols].astype(F32)
            sg = _sigmoid(ug)
            du_ref[r0:r0 + FFN_ROWS, vcols] = d * ug * sg
            du_ref[r0:r0 + FFN_ROWS, gcols] = d * uv * sg * (1.0 + ug * (1.0 - sg))

        @pl.when(first)
        def _():
            dw_ref[...] = jnp.zeros_like(dw_ref)
            db_ref[...] = jnp.zeros_like(db_ref)

        du = du_ref[...]
        db_ref[...] += jnp.sum(du, axis=0, keepdims=True)
        for k, (src, off) in enumerate(taps):
            dw_ref[k:k + 1, :] += jnp.sum(du * src[pl.ds(off, ts), :], axis=0, keepdims=True)

    nst = dims.seq // ts
    n_moved = _moved_copies(offsets)
    full = lambda rows: pl.BlockSpec((rows, 2 * f), lambda b_, i: (0, 0))
    return pl.pallas_call(
        body, name=name, grid=(dims.batch_local, nst),
        in_specs=[pl.BlockSpec((ts, f), lambda b_, i: (b_ * nst + i, 0)),
                  *_seq_specs(dims, ts, 2 * f, FFN_HALO, 0), full(FFN_HALO), full(1)],
        out_specs=[pl.BlockSpec((ts, 2 * f), lambda b_, i: (b_ * nst + i, 0)), full(FFN_HALO), full(1)],
        out_shape=[jax.ShapeDtypeStruct((t, 2 * f), F32), jax.ShapeDtypeStruct((FFN_HALO, 2 * f), F32),
                   jax.ShapeDtypeStruct((1, 2 * f), F32)],
        scratch_shapes=[pltpu.VMEM((FFN_HALO + ts, 2 * f), F32), pltpu.VMEM((n_moved, ts, 2 * f), F32)],
        compiler_params=_params("arbitrary", "arbitrary"),
    )(dact, up, up, w, b)


def _ffn_conv_bwd(du, w, dims, *, name, ts=128):
    t, f2 = du.shape
    kw = dims.ffn_conv_width
    nst = dims.seq // ts

    offsets = [(kw - 1) - k for k in range(kw)]

    def body(d_ref, dn_ref, w_ref, o_ref, buf, moved):
        buf[0:ts, :] = d_ref[...]
        buf[ts:, :] = jnp.where(pl.program_id(1) < nst - 1, dn_ref[...], 0.0)
        taps = _tap_sources(buf, moved, offsets, ts)
        for r0, c0, cw in _ffn_chunks(ts, f2):
            cols = slice(c0, c0 + cw)
            o_ref[r0:r0 + FFN_ROWS, cols] = _taps_sum(taps, w_ref, jnp.zeros((FFN_ROWS, cw), F32), r0, cols).astype(BF16)

    return pl.pallas_call(
        body, name=name, grid=(dims.batch_local, nst),
        in_specs=[*_seq_specs(dims, ts, f2, FFN_HALO, 0, nxt=True), pl.BlockSpec((FFN_HALO, f2), lambda b_, i: (0, 0))],
        out_specs=pl.BlockSpec((ts, f2), lambda b_, i: (b_ * nst + i, 0)),
        out_shape=jax.ShapeDtypeStruct((t, f2), BF16),
        scratch_shapes=[pltpu.VMEM((ts + FFN_HALO, f2), F32), pltpu.VMEM((_moved_copies(offsets), ts, f2), F32)],
        compiler_params=_params("parallel", "parallel"),
    )(du, du, w)


def _alibi_slope(h, n_heads):
    return 2.0 ** (-8.0 * (h + 1) / n_heads)


def _dot_nt(a, b):
    return lax.dot_general(a, b, (((1,), (1,)), ((), ())), preferred_element_type=F32)


def _dot_tn(a, b):
    return lax.dot_general(a, b, (((0,), (0,)), ((), ())), preferred_element_type=F32)


def _attn_view(x, dims, dil):
    return x.reshape(dims.batch_local, dims.seq // dil, dil * x.shape[-1])


def _attn_fwd_group(q, k, v, state, dims, dil, *, last, name):
    t, a = q.shape
    assert 2 * dims.head_dim == 128 and dims.n_heads % 2 == 0
    blk, hd = ATTN_BLOCK, dims.head_dim
    nb = dims.seq // dil // blk
    has_prev = nb > 1
    nkeys = 2 * blk if has_prev else blk

    def body(*refs):
        it = iter(refs)
        q_ref, kc_ref, vc_ref = next(it), next(it), next(it)
        kp_ref, vp_ref = (next(it), next(it)) if has_prev else (None, None)
        m_in, l_in, acc_in = (next(it), next(it), next(it)) if state is not None else (None, None, None)
        outs = list(it)
        iq = lax.broadcasted_iota(jnp.int32, (blk, nkeys), 0)
        jk = lax.broadcasted_iota(jnp.int32, (blk, nkeys), 1)
        if has_prev:
            steps = iq + blk - jk
            valid = (steps >= 0) & (steps <= blk) & ((jk >= blk) | (pl.program_id(2) > 0))
        else:
            steps = iq - jk
            valid = steps >= 0
        dist = steps.astype(F32) * float(dil)
        low = lax.broadcasted_iota(jnp.int32, (blk, 2 * hd), 1) < hd
        for hp in range(dims.n_heads // 2):
            sl = slice(2 * hd * hp, 2 * hd * (hp + 1))
            q2 = q_ref[:, sl]
            if has_prev:
                kcat = jnp.concatenate([kp_ref[:, sl], kc_ref[:, sl]], axis=0)
                vcat = jnp.concatenate([vp_ref[:, sl], vc_ref[:, sl]], axis=0)
            else:
                kcat, vcat = kc_ref[:, sl], vc_ref[:, sl]
            halves = []
            for half in range(2):
                col = 2 * hd * hp + hd * half
                qh = jnp.where(low if half == 0 else jnp.logical_not(low), q2, jnp.zeros_like(q2))
                sc = _dot_nt(qh, kcat) - _alibi_slope(2 * hp + half, dims.n_heads) * dist
                sc = jnp.where(valid, sc, MASKED_SCORE)
                row_max = jnp.max(sc, axis=-1, keepdims=True)
                if state is None:
                    m_new = row_max
                    p = jnp.exp(sc - m_new)
                    alpha = None
                    l_new = jnp.sum(p, axis=-1, keepdims=True)
                else:
                    m_old = m_in[:, col:col + 1]
                    m_new = jnp.maximum(m_old, row_max)
                    p = jnp.exp(sc - m_new)
                    alpha = jnp.exp(m_old - m_new)
                    l_new = alpha * l_in[:, col:col + 1] + jnp.sum(p, axis=-1, keepdims=True)
                pv = jnp.dot(p.astype(BF16), vcat, preferred_element_type=F32)
                halves.append((m_new, l_new, alpha, pv))
            (m_a, l_a, al_a, pv_a), (m_b, l_b, al_b, pv_b) = halves
            if state is None:
                acc = jnp.where(low, pv_a, pv_b)
            else:
                old = acc_in[:, sl]
                acc = jnp.where(low, al_a * old + pv_a, al_b * old + pv_b)
            m2 = jnp.where(low, m_a, m_b)
            l2 = jnp.where(low, l_a, l_b)
            if last:
                outs[0][:, sl] = (acc / l2).astype(BF16)
                outs[1][:, sl] = m2 + jnp.log(l2)
            else:
                outs[0][:, sl] = m2
                outs[1][:, sl] = l2
                outs[2][:, sl] = acc

    cur = pl.BlockSpec((None, blk, a), lambda b, r, i: (b, i, r))
    prev = pl.BlockSpec((None, blk, a), lambda b, r, i: (b, jnp.maximum(i - 1, 0), r))
    args, in_specs = [q, k, v], [cur, cur, cur]
    if has_prev:
        args += [k, v]
        in_specs += [prev, prev]
    if state is not None:
        args += list(state)
        in_specs += [cur] * 3
    shape = lambda dt: jax.ShapeDtypeStruct((dims.batch_local, dims.seq // dil, dil * a), dt)
    out_shape = [shape(BF16), shape(F32)] if last else [shape(F32)] * 3
    outs = pl.pallas_call(
        body, name=name, grid=(dims.batch_local, dil, nb),
        in_specs=in_specs, out_specs=[cur] * len(out_shape), out_shape=out_shape,
        compiler_params=_params("parallel", "parallel", "parallel"),
    )(*[_attn_view(x, dims, dil) for x in args])
    return tuple(o.reshape(t, a) for o in outs)


def _attn_delta(do, o, head_ones, dims, *, name, tr=512):
    t, a = o.shape
    tr = _pick(t, tr, 8)

    def body(do_ref, o_ref, e_ref, d_ref):
        prod = do_ref[...].astype(F32) * o_ref[...].astype(F32)
        d_ref[...] = _head_mean(prod, e_ref, dims.head_dim) * float(dims.head_dim)

    return pl.pallas_call(
        body, name=name, grid=(t // tr,),
        in_specs=[_row_spec(tr, a), _row_spec(tr, a), pl.BlockSpec((a, a), lambda i: (0, 0))],
        out_specs=_row_spec(tr, a), out_shape=jax.ShapeDtypeStruct((t, a), F32),
        compiler_params=_params("parallel"),
    )(do, o, head_ones)


def _attn_bwd_group(q, k, v, do, lse, delta, dims, dil, *, name):
    t, a = q.shape
    blk, hd = ATTN_BLOCK, dims.head_dim
    nb = dims.seq // dil // blk
    has_next = nb > 1

    def body(*refs):
        k_ref, v_ref, q_ref, do_ref, lse_ref, dl_ref = refs[:6]
        if has_next:
            qn_ref, don_ref, lsen_ref, dln_ref = refs[6:10]
            dq_ref, dk_ref, dv_ref, carry = refs[10:]
        else:
            dq_ref, dk_ref, dv_ref = refs[6:]
        j = pl.program_id(2)
        iq = lax.broadcasted_iota(jnp.int32, (blk, blk), 0)
        jk = lax.broadcasted_iota(jnp.int32, (blk, blk), 1)
        low = lax.broadcasted_iota(jnp.int32, (blk, 2 * hd), 1) < hd

        def pair(hp, qr, dor, lser, dlr, steps, valid):
            sl = slice(2 * hd * hp, 2 * hd * (hp + 1))
            q2, do2, k2, v2 = qr[:, sl], dor[:, sl], k_ref[:, sl], v_ref[:, sl]
            dist = steps.astype(F32) * float(dil)
            dq_h, dk2, dv2 = [], None, None
            for half in range(2):
                col = 2 * hd * hp + hd * half
                mask = low if half == 0 else jnp.logical_not(low)
                qh = jnp.where(mask, q2, jnp.zeros_like(q2))
                doh = jnp.where(mask, do2, jnp.zeros_like(do2))
                sc = _dot_nt(qh, k2) - _alibi_slope(2 * hp + half, dims.n_heads) * dist
                p = jnp.where(valid, jnp.exp(sc - lser[:, col:col + 1]), 0.0)
                ds = p * (_dot_nt(doh, v2) - dlr[:, col:col + 1])
                ds_b, p_b = ds.astype(BF16), p.astype(BF16)
                dq_h.append(jnp.dot(ds_b, k2, preferred_element_type=F32))
                dk_h, dv_h = _dot_tn(ds_b, qh), _dot_tn(p_b, doh)
                dk2 = dk_h if dk2 is None else dk2 + dk_h
                dv2 = dv_h if dv2 is None else dv2 + dv_h
            return sl, jnp.where(low, dq_h[0], dq_h[1]), dk2, dv2

        if has_next:
            @pl.when(j == 0)
            def _():
                carry[...] = jnp.zeros_like(carry)

        for hp in range(dims.n_heads // 2):
            sl, dq2, dk2, dv2 = pair(hp, q_ref, do_ref, lse_ref, dl_ref, iq - jk, iq >= jk)
            dq_ref[:, sl] = (carry[:, sl] + dq2) if has_next else dq2
            dk_ref[:, sl] = dk2
            dv_ref[:, sl] = dv2

        if has_next:
            @pl.when(j + 1 < nb)
            def _():
                for hp in range(dims.n_heads // 2):
                    sl, dq2, dk2, dv2 = pair(hp, qn_ref, don_ref, lsen_ref, dln_ref, iq - jk + blk, jk >= iq)
                    carry[:, sl] = dq2
                    dk_ref[:, sl] += dk2
                    dv_ref[:, sl] += dv2

    cur = pl.BlockSpec((None, blk, a), lambda b, r, j: (b, j, r))
    nxt = pl.BlockSpec((None, blk, a), lambda b, r, j: (b, jnp.minimum(j + 1, nb - 1), r))
    args, in_specs = [k, v, q, do, lse, delta], [cur] * 6
    if has_next:
        args += [q, do, lse, delta]
        in_specs += [nxt] * 4
    shape = jax.ShapeDtypeStruct((dims.batch_local, dims.seq // dil, dil * a), F32)
    outs = pl.pallas_call(
        body, name=name, grid=(dims.batch_local, dil, nb),
        in_specs=in_specs, out_specs=[cur] * 3, out_shape=[shape] * 3,
        scratch_shapes=[pltpu.VMEM((blk, a), F32)] if has_next else [],
        compiler_params=_params("parallel", "parallel", "arbitrary"),
    )(*[_attn_view(x, dims, dil) for x in args])
    return tuple(o.reshape(t, a) for o in outs)


LANES = 128
MASK_BIAS = 1e30
RESIDUE_DILATIONS = tuple(d for d in DILATIONS if d > 1)


def _rows_to_residues(value, out_ref, scr, d):
    rows, width = value.shape
    for c in range(width // LANES):
        cols = slice(LANES * c, LANES * (c + 1))
        scr[c] = value[:, cols]
        for r in range(d):
            out_ref[r, :, cols] = scr[c, pl.ds(r, rows // d, stride=d), :].astype(out_ref.dtype)


def _residues_to_rows(in_ref, scr, d):
    _, n, width = in_ref.shape
    slabs = []
    for c in range(width // LANES):
        cols = slice(LANES * c, LANES * (c + 1))
        for r in range(d):
            scr[c, pl.ds(r, n, stride=d), :] = in_ref[r, :, cols].astype(F32)
        slabs.append(scr[c])
    return slabs[0] if len(slabs) == 1 else jnp.concatenate(slabs, axis=1)


def _residue_shape(dims, d, width, dtype):
    return jax.ShapeDtypeStruct((dims.batch_local, d, dims.seq // d, width), dtype)


def _residue_spec(dims, d, tr, width):
    tiles = dims.seq // tr
    return pl.BlockSpec((None, d, tr // d, width), lambda i: (i // tiles, 0, i % tiles, 0))


def _head_sum_matrix(dims):
    a = dims.n_heads * dims.head_dim
    head = jnp.arange(a, dtype=jnp.int32) // dims.head_dim
    return (head[:, None] == jnp.arange(LANES, dtype=jnp.int32)[None, :]).astype(BF16)


def _two_pass_dot(v, m):
    hi = v.astype(BF16)
    lo = (v - hi.astype(F32)).astype(BF16)
    return jnp.dot(hi, m, preferred_element_type=F32) + jnp.dot(lo, m, preferred_element_type=F32)


def _qkv_layouts_fwd(z, gq, gk, head_ones, dims, *, name, tr=256):
    t = z.shape[0]
    a = dims.n_heads * dims.head_dim
    q_scale = dims.head_dim ** -0.5
    nres = len(RESIDUE_DILATIONS)

    def body(q_ref, k_ref, v_ref, gq_ref, gk_ref, sum_ref, spread_ref, *rest):
        outs, scr = rest[:-1], rest[-1]
        qv, kv = q_ref[...].astype(F32), k_ref[...].astype(F32)
        mean = lambda val: _two_pass_dot(_two_pass_dot(val, sum_ref[...]), spread_ref[...]) * (1.0 / dims.head_dim)
        rq = lax.rsqrt(mean(qv * qv) + RMS_EPS)
        rk = lax.rsqrt(mean(kv * kv) + RMS_EPS)
        values = (qv * rq * gq_ref[...] * q_scale, kv * rk * gk_ref[...], v_ref[...].astype(F32))
        for j, val in enumerate(values):
            outs[j][...] = val.astype(BF16)
            for g, d in enumerate(RESIDUE_DILATIONS):
                _rows_to_residues(val, outs[3 * (g + 1) + j], scr, d)

    out_specs = [_row_spec(tr, a)] * 3
    out_shape = [jax.ShapeDtypeStruct((t, a), BF16)] * 3
    for d in RESIDUE_DILATIONS:
        out_specs += [_residue_spec(dims, d, tr, a)] * 3
        out_shape += [_residue_shape(dims, d, a, BF16)] * 3
    outs = pl.pallas_call(
        body, name=name, grid=(t // tr,),
        in_specs=[_row_spec(tr, a, 2), _row_spec(tr, a, 3), _row_spec(tr, a, 4), _vec_spec(a), _vec_spec(a),
                  pl.BlockSpec((a, LANES), lambda i: (0, 0)), pl.BlockSpec((LANES, a), lambda i: (0, 0))],
        out_specs=out_specs, out_shape=out_shape,
        scratch_shapes=[pltpu.VMEM((a // LANES, tr, LANES), F32)],
        compiler_params=_params("parallel"),
    )(z, z, z, gq, gk, *head_ones)
    return {d: tuple(outs[3 * g:3 * g + 3]) for g, d in enumerate((1,) + RESIDUE_DILATIONS)}


def _attn_specs(dims, dil, width):
    blk = ATTN_BLOCK
    nb = dims.seq // dil // blk
    if dil == 1:
        grid = (dims.batch_local, nb)
        at = lambda f: pl.BlockSpec((blk, width), lambda b, i: (b * nb + f(i), 0))
    else:
        grid = (dims.batch_local, dil, nb)
        at = lambda f: pl.BlockSpec((None, None, blk, width), lambda b, r, i: (b, r, f(i), 0))
    return grid, at(lambda i: i), at(lambda i: jnp.maximum(i - 1, 0)), at(lambda i: jnp.minimum(i + 1, nb - 1))


def _head_slopes(n_heads):
    h = lax.broadcasted_iota(jnp.int32, (n_heads, 1, 1), 0).astype(F32)
    return jnp.exp((h + 1.0) * (-8.0 / n_heads * math.log(2.0)))


def _pair_masks(hd):
    low = lax.broadcasted_iota(jnp.int32, (1, 2 * hd), 1) < hd
    return low, jnp.logical_not(low)


def _attn_fwd(q, k, v, dims, dil, *, name):
    a = dims.n_heads * dims.head_dim
    heads, hd, blk = dims.n_heads, dims.head_dim, ATTN_BLOCK
    assert 2 * hd == LANES and heads % 2 == 0 and heads <= LANES
    nb = dims.seq // dil // blk
    has_prev = nb > 1
    nkeys = 2 * blk if has_prev else blk
    grid, cur, prev, _ = _attn_specs(dims, dil, a)
    _, cur_stat, _, _ = _attn_specs(dims, dil, LANES)

    def body(*refs):
        if has_prev:
            q_ref, kc_ref, vc_ref, kp_ref, vp_ref, o_ref, lse_ref, s_scr, p_scr = refs
        else:
            q_ref, kc_ref, vc_ref, o_ref, lse_ref, s_scr, p_scr = refs
        low, high = _pair_masks(hd)

        def keys(cur_ref, prev_ref, sl):
            return jnp.concatenate([prev_ref[:, sl], cur_ref[:, sl]], axis=0) if has_prev else cur_ref[:, sl]

        for hp in range(heads // 2):
            sl = slice(LANES * hp, LANES * (hp + 1))
            q2 = q_ref[:, sl]
            kcat = keys(kc_ref, kp_ref if has_prev else None, sl)
            s_scr[2 * hp] = _dot_nt(jnp.where(low, q2, jnp.zeros_like(q2)), kcat)
            s_scr[2 * hp + 1] = _dot_nt(jnp.where(high, q2, jnp.zeros_like(q2)), kcat)

        iq = lax.broadcasted_iota(jnp.int32, (blk, nkeys), 0)
        jk = lax.broadcasted_iota(jnp.int32, (blk, nkeys), 1)
        if has_prev:
            steps = iq + blk - jk
            valid = (steps >= 0) & (steps <= blk) & ((jk >= blk) | (pl.program_id(len(grid) - 1) > 0))
        else:
            steps = iq - jk
            valid = steps >= 0
        bias = jnp.where(valid, steps.astype(F32) * (-float(dil)), -MASK_BIAS)
        s = s_scr[...] + _head_slopes(heads) * bias[None]
        m = jnp.max(s, axis=-1, keepdims=True)
        p = jnp.exp(s - m)
        l = jnp.sum(p, axis=-1, keepdims=True)
        p_scr[...] = p.astype(BF16)
        inv = 1.0 / l
        lse = m + jnp.log(l)

        lane = lax.broadcasted_iota(jnp.int32, (blk, LANES), 1)
        stat = jnp.zeros((blk, LANES), F32)
        for hp in range(heads // 2):
            sl = slice(LANES * hp, LANES * (hp + 1))
            vcat = keys(vc_ref, vp_ref if has_prev else None, sl)
            pv_a = jnp.dot(p_scr[2 * hp], vcat, preferred_element_type=F32) * inv[2 * hp]
            pv_b = jnp.dot(p_scr[2 * hp + 1], vcat, preferred_element_type=F32) * inv[2 * hp + 1]
            o_ref[:, sl] = jnp.where(low, pv_a, pv_b)
            stat = jnp.where(lane == 2 * hp, lse[2 * hp], stat)
            stat = jnp.where(lane == 2 * hp + 1, lse[2 * hp + 1], stat)
        lse_ref[...] = stat

    lead = q.shape[:-2]
    rows = q.shape[-2]
    o, lse = pl.pallas_call(
        body, name=name, grid=grid,
        in_specs=[cur, cur, cur] + ([prev, prev] if has_prev else []),
        out_specs=[cur, cur_stat],
        out_shape=[jax.ShapeDtypeStruct(lead + (rows, a), F32), jax.ShapeDtypeStruct(lead + (rows, LANES), F32)],
        scratch_shapes=[pltpu.VMEM((heads, blk, nkeys), F32), pltpu.VMEM((heads, blk, nkeys), BF16)],
        compiler_params=_params(*["parallel"] * len(grid)),
    )(q, k, v, *([k, v] if has_prev else []))
    return o, lse


def _attn_combine(groups, head_spread, dims, *, name, tr=256):
    t = dims.tokens
    a = dims.n_heads * dims.head_dim
    dils = tuple(groups)

    def body(*refs):
        ins = refs[:2 * len(dils)]
        x_ref = refs[2 * len(dils)]
        o_ref = refs[2 * len(dils) + 1]
        lse_refs = refs[2 * len(dils) + 2:-2]
        scr, scr_stat = refs[-2], refs[-1]
        outs, stats = [], []
        for g, d in enumerate(dils):
            if d == 1:
                outs.append(ins[2 * g][...])
                stats.append(ins[2 * g + 1][...])
            else:
                outs.append(_residues_to_rows(ins[2 * g], scr, d))
                stats.append(_residues_to_rows(ins[2 * g + 1], scr_stat, d))
        top = functools.reduce(jnp.maximum, stats)
        weights = [jnp.exp(s - top) for s in stats]
        total = functools.reduce(jnp.add, weights)
        joint = top + jnp.log(total)
        inv = 1.0 / total
        acc = None
        for w, o in zip(weights, outs):
            term = _two_pass_dot(w * inv, x_ref[...]) * o
            acc = term if acc is None else acc + term
        o_ref[...] = acc.astype(BF16)
        for g, d in enumerate(dils):
            if d == 1:
                lse_refs[g][...] = joint
            else:
                _rows_to_residues(joint, lse_refs[g], scr_stat, d)

    in_specs, args, lse_specs, lse_shapes = [], [], [], []
    for d in dils:
        if d == 1:
            in_specs += [_row_spec(tr, a), _row_spec(tr, LANES)]
            lse_specs.append(_row_spec(tr, LANES))
            lse_shapes.append(jax.ShapeDtypeStruct((t, LANES), F32))
        else:
            in_specs += [_residue_spec(dims, d, tr, a), _residue_spec(dims, d, tr, LANES)]
            lse_specs.append(_residue_spec(dims, d, tr, LANES))
            lse_shapes.append(_residue_shape(dims, d, LANES, F32))
        args += list(groups[d])
    outs = pl.pallas_call(
        body, name=name, grid=(t // tr,),
        in_specs=in_specs + [pl.BlockSpec((LANES, a), lambda i: (0, 0))],
        out_specs=[_row_spec(tr, a)] + lse_specs,
        out_shape=[jax.ShapeDtypeStruct((t, a), BF16)] + lse_shapes,
        scratch_shapes=[pltpu.VMEM((a // LANES, tr, LANES), F32), pltpu.VMEM((1, tr, LANES), F32)],
        compiler_params=_params("parallel"),
    )(*args, head_spread)
    return outs[0], dict(zip(dils, outs[1:]))


def _attn_bwd_prep(do, o, head_sum, dims, *, name, tr=256):
    t, a = o.shape

    def body(do_ref, o_ref, e_ref, *rest):
        outs, scr, scr_stat = rest[:-2], rest[-2], rest[-1]
        dov = do_ref[...].astype(F32)
        delta = _two_pass_dot(dov * o_ref[...].astype(F32), e_ref[...])
        outs[0][...] = delta
        for g, d in enumerate(RESIDUE_DILATIONS):
            _rows_to_residues(dov, outs[1 + 2 * g], scr, d)
            _rows_to_residues(delta, outs[2 + 2 * g], scr_stat, d)

    out_specs, out_shape = [_row_spec(tr, LANES)], [jax.ShapeDtypeStruct((t, LANES), F32)]
    for d in RESIDUE_DILATIONS:
        out_specs += [_residue_spec(dims, d, tr, a), _residue_spec(dims, d, tr, LANES)]
        out_shape += [_residue_shape(dims, d, a, BF16), _residue_shape(dims, d, LANES, F32)]
    outs = pl.pallas_call(
        body, name=name, grid=(t // tr,),
        in_specs=[_row_spec(tr, a), _row_spec(tr, a), pl.BlockSpec((a, LANES), lambda i: (0, 0))],
        out_specs=out_specs, out_shape=out_shape,
        scratch_shapes=[pltpu.VMEM((a // LANES, tr, LANES), F32), pltpu.VMEM((1, tr, LANES), F32)],
        compiler_params=_params("parallel"),
    )(do, o, head_sum)
    dos, deltas = {1: do}, {1: outs[0]}
    for g, d in enumerate(RESIDUE_DILATIONS):
        dos[d], deltas[d] = outs[1 + 2 * g], outs[2 + 2 * g]
    return dos, deltas


def _attn_bwd(q, k, v, do, lse, delta, dims, dil, *, name):
    a = dims.n_heads * dims.head_dim
    heads, hd, blk = dims.n_heads, dims.head_dim, ATTN_BLOCK
    nb = dims.seq // dil // blk
    has_next = nb > 1
    nq = 2 * blk if has_next else blk
    grid, cur, _, nxt = _attn_specs(dims, dil, a)
    _, cur_stat, _, nxt_stat = _attn_specs(dims, dil, LANES)

    def body(*refs):
        k_ref, v_ref, q_ref, do_ref, lse_ref, dl_ref = refs[:6]
        if has_next:
            qn_ref, don_ref, lsen_ref, dln_ref = refs[6:10]
            dq_ref, dk_ref, dv_ref, s_scr, dp_scr, p_scr, ds_scr, carry = refs[10:]
        else:
            dq_ref, dk_ref, dv_ref, s_scr, dp_scr, p_scr, ds_scr = refs[6:]
        j = pl.program_id(len(grid) - 1)
        low, high = _pair_masks(hd)

        def stacked(ref, nref, sl):
            return jnp.concatenate([ref[:, sl], nref[:, sl]], axis=0) if has_next else ref[:, sl]

        def halves(x):
            return jnp.where(low, x, jnp.zeros_like(x)), jnp.where(high, x, jnp.zeros_like(x))

        for hp in range(heads // 2):
            sl = slice(LANES * hp, LANES * (hp + 1))
            k2, v2 = k_ref[:, sl], v_ref[:, sl]
            q_a, q_b = halves(stacked(q_ref, qn_ref if has_next else None, sl))
            do_a, do_b = halves(stacked(do_ref, don_ref if has_next else None, sl))
            s_scr[2 * hp], s_scr[2 * hp + 1] = _dot_nt(q_a, k2), _dot_nt(q_b, k2)
            dp_scr[2 * hp], dp_scr[2 * hp + 1] = _dot_nt(do_a, v2), _dot_nt(do_b, v2)

        rq = lax.broadcasted_iota(jnp.int32, (nq, blk), 0)
        jk = lax.broadcasted_iota(jnp.int32, (nq, blk), 1)
        if has_next:
            iq = jnp.where(rq < blk, rq, rq - blk)
            steps = jnp.where(rq < blk, iq - jk, iq - jk + blk)
            valid = ((rq < blk) & (iq >= jk)) | ((rq >= blk) & (jk >= iq) & (j + 1 < nb))
        else:
            steps, valid = rq - jk, rq >= jk
        bias = jnp.where(valid, steps.astype(F32) * (-float(dil)), -MASK_BIAS)
        lse_all = stacked(lse_ref, lsen_ref if has_next else None, slice(None))
        dl_all = stacked(dl_ref, dln_ref if has_next else None, slice(None))
        lse3 = jnp.stack([lse_all[:, h:h + 1] for h in range(heads)])
        dl3 = jnp.stack([dl_all[:, h:h + 1] for h in range(heads)])
        p = jnp.exp(s_scr[...] + _head_slopes(heads) * bias[None] - lse3)
        p_scr[...] = p.astype(BF16)
        ds_scr[...] = (p * (dp_scr[...] - dl3)).astype(BF16)

        if has_next:
            @pl.when(j == 0)
            def _():
                carry[...] = jnp.zeros_like(carry)

        for hp in range(heads // 2):
            sl = slice(LANES * hp, LANES * (hp + 1))
            k2 = k_ref[:, sl]
            q_a, q_b = halves(stacked(q_ref, qn_ref if has_next else None, sl))
            do_a, do_b = halves(stacked(do_ref, don_ref if has_next else None, sl))
            ds_a, ds_b = ds_scr[2 * hp], ds_scr[2 * hp + 1]
            dq2 = jnp.where(low, jnp.dot(ds_a, k2, preferred_element_type=F32),
                            jnp.dot(ds_b, k2, preferred_element_type=F32))
            dk_ref[:, sl] = _dot_tn(ds_a, q_a) + _dot_tn(ds_b, q_b)
            dv_ref[:, sl] = _dot_tn(p_scr[2 * hp], do_a) + _dot_tn(p_scr[2 * hp + 1], do_b)
            if has_next:
                dq_ref[:, sl] = carry[:, sl] + dq2[:blk]
                carry[:, sl] = dq2[blk:]
            else:
                dq_ref[:, sl] = dq2

    args, in_specs = [k, v, q, do, lse, delta], [cur] * 4 + [cur_stat] * 2
    if has_next:
        args += [q, do, lse, delta]
        in_specs += [nxt] * 2 + [nxt_stat] * 2
    shape = jax.ShapeDtypeStruct(q.shape, F32)
    scratch = [pltpu.VMEM((heads, nq, blk), F32)] * 2 + [pltpu.VMEM((heads, nq, blk), BF16)] * 2
    if has_next:
        scratch.append(pltpu.VMEM((blk, a), F32))
    return pl.pallas_call(
        body, name=name, grid=grid, in_specs=in_specs, out_specs=[cur] * 3, out_shape=[shape] * 3,
        scratch_shapes=scratch,
        compiler_params=_params(*["parallel"] * (len(grid) - 1), "arbitrary"),
    )(*args)


def _qkv_layouts_bwd(z, grads, gq, gk, head_ones, dims, *, name, tr=256):
    t = z.shape[0]
    a = dims.n_heads * dims.head_dim
    q_scale = dims.head_dim ** -0.5
    dils = tuple(grads)

    def body(q_ref, k_ref, *rest):
        d_refs = rest[:3 * len(dils)]
        gq_ref, gk_ref, sum_ref, spread_ref, dz_ref, dgq_ref, dgk_ref, scr = rest[3 * len(dils):]
        first = pl.program_id(0) == 0
        mean = lambda val: _two_pass_dot(_two_pass_dot(val, sum_ref[...]), spread_ref[...]) * (1.0 / dims.head_dim)

        def total(j):
            acc = None
            for g, d in enumerate(dils):
                ref = d_refs[3 * g + j]
                part = ref[...] if d == 1 else _residues_to_rows(ref, scr, d)
                acc = part if acc is None else acc + part
            return acc

        def norm_bwd(x_ref, dy, g_ref, scale, col, dg_ref):
            xv = x_ref[...].astype(F32)
            dy = dy * scale
            r = lax.rsqrt(mean(xv * xv) + RMS_EPS)
            gy = dy * g_ref[...]
            dx = r * gy - xv * (r * r * r) * mean(xv * gy)
            dz_ref[:, col * a:(col + 1) * a] = dx.astype(BF16)
            _accumulate(dg_ref, jnp.sum(dy * xv * r, axis=0, keepdims=True), first)

        norm_bwd(q_ref, total(0), gq_ref, q_scale, 0, dgq_ref)
        norm_bwd(k_ref, total(1), gk_ref, 1.0, 1, dgk_ref)
        dz_ref[:, 2 * a:3 * a] = total(2).astype(BF16)

    in_specs, args = [_row_spec(tr, a, 2), _row_spec(tr, a, 3)], [z, z]
    for d in dils:
        in_specs += [_row_spec(tr, a) if d == 1 else _residue_spec(dims, d, tr, a)] * 3
        args += list(grads[d])
    in_specs += [_vec_spec(a), _vec_spec(a), pl.BlockSpec((a, LANES), lambda i: (0, 0)),
                 pl.BlockSpec((LANES, a), lambda i: (0, 0))]
    return pl.pallas_call(
        body, name=name, grid=(t // tr,), in_specs=in_specs,
        out_specs=[_row_spec(tr, 3 * a), _vec_spec(a), _vec_spec(a)],
        out_shape=[jax.ShapeDtypeStruct((t, 3 * a), BF16)] + [jax.ShapeDtypeStruct((1, a), F32)] * 2,
        scratch_shapes=[pltpu.VMEM((a // LANES, tr, LANES), F32)],
        compiler_params=_params("arbitrary"),
    )(*args, gq, gk, *head_ones)


def _mix_fwd(ya, yb, z, gate_b, dims, *, name, tr=512):
    t, d = ya.shape
    tr = _pick(t, tr, 8)
    first_gate_col = z.shape[1] // d - 2

    def body(ya_ref, yb_ref, ga_ref, gb_ref, ba_ref, bb_ref, o_ref):
        g_a = _sigmoid(ga_ref[...].astype(F32) + ba_ref[...])
        g_b = _sigmoid(gb_ref[...].astype(F32) + bb_ref[...])
        o_ref[...] = (g_a * ya_ref[...] + g_b * yb_ref[...]).astype(BF16)

    return pl.pallas_call(
        body, name=name, grid=(t // tr,),
        in_specs=[_row_spec(tr, d), _row_spec(tr, d), _row_spec(tr, d, first_gate_col),
                  _row_spec(tr, d, first_gate_col + 1), _vec_spec(d, 0), _vec_spec(d, 1)],
        out_specs=_row_spec(tr, d), out_shape=jax.ShapeDtypeStruct((t, d), BF16),
        compiler_params=_params("parallel"),
    )(ya, yb, z, z, gate_b, gate_b)


def _mix_bwd(dmix, ya, yb, z, gate_b, dims, *, name, tr=512):
    t, d = ya.shape
    tr = _pick(t, tr, 8)
    first_gate_col = z.shape[1] // d - 2

    def body(dm_ref, ya_ref, yb_ref, ga_ref, gb_ref, ba_ref, bb_ref, dya_ref, dyb_ref, dz_ref, db_ref):
        dm = dm_ref[...].astype(F32)
        g_a = _sigmoid(ga_ref[...].astype(F32) + ba_ref[...])
        g_b = _sigmoid(gb_ref[...].astype(F32) + bb_ref[...])
        dya_ref[...] = (dm * g_a).astype(BF16)
        dyb_ref[...] = (dm * g_b).astype(BF16)
        dl_a = dm * ya_ref[...] * g_a * (1.0 - g_a)
        dl_b = dm * yb_ref[...] * g_b * (1.0 - g_b)
        dz_ref[:, 0:d] = dl_a.astype(BF16)
        dz_ref[:, d:2 * d] = dl_b.astype(BF16)
        first = pl.program_id(0) == 0
        sums = jnp.concatenate([jnp.sum(dl_a, axis=0, keepdims=True), jnp.sum(dl_b, axis=0, keepdims=True)], axis=1)
        _accumulate(db_ref, sums, first)

    return pl.pallas_call(
        body, name=name, grid=(t // tr,),
        in_specs=[_row_spec(tr, d), _row_spec(tr, d), _row_spec(tr, d), _row_spec(tr, d, first_gate_col),
                  _row_spec(tr, d, first_gate_col + 1), _vec_spec(d, 0), _vec_spec(d, 1)],
        out_specs=[_row_spec(tr, d), _row_spec(tr, d), _row_spec(tr, 2 * d), _vec_spec(2 * d)],
        out_shape=[jax.ShapeDtypeStruct((t, d), BF16)] * 2 + [jax.ShapeDtypeStruct((t, 2 * d), BF16),
                                                              jax.ShapeDtypeStruct((1, 2 * d), F32)],
        compiler_params=_params("arbitrary"),
    )(dmix, ya, yb, z, z, gate_b, gate_b)


def _loss_head(y, target, *, name, tr=512):
    t, d = y.shape
    tr = _pick(t, tr, 8)

    def body(y_ref, t_ref, dy_ref, dyb_ref, loss_ref):
        err = y_ref[...] - t_ref[...]
        dy = err * (1.0 / d)
        dy_ref[...] = dy
        dyb_ref[...] = dy.astype(BF16)
        part = jnp.sum(jnp.sum(err * err, axis=-1, keepdims=True), axis=0, keepdims=True) * (0.5 / d)
        _accumulate(loss_ref, jnp.broadcast_to(part, (8, 128)), pl.program_id(0) == 0)

    return pl.pallas_call(
        body, name=name, grid=(t // tr,),
        in_specs=[_row_spec(tr, d), _row_spec(tr, d)],
        out_specs=[_row_spec(tr, d), _row_spec(tr, d), pl.BlockSpec((8, 128), lambda i: (0, 0))],
        out_shape=[jax.ShapeDtypeStruct((t, d), F32), jax.ShapeDtypeStruct((t, d), BF16),
                   jax.ShapeDtypeStruct((8, 128), F32)],
        compiler_params=_params("arbitrary"),
    )(y, target)


def _adamw(w, grads, m, v, *, name, tr=256):
    r, c = w.shape
    tr = _pick(r, tr, 8)
    ng = len(grads)
    c1 = 1.0 - ADAM_B1 ** ADAM_STEP
    c2 = 1.0 - ADAM_B2 ** ADAM_STEP

    def body(*refs):
        w_ref, g_refs, m_ref, v_ref = refs[0], refs[1:1 + ng], refs[1 + ng], refs[2 + ng]
        g_out, d_out, m_out, v_out = refs[3 + ng:]
        g = g_refs[0][...]
        for extra in g_refs[1:]:
            g = g + extra[...]
        m_new = ADAM_B1 * m_ref[...] + (1.0 - ADAM_B1) * g
        v_new = ADAM_B2 * v_ref[...] + (1.0 - ADAM_B2) * (g * g)
        g_out[...] = g
        m_out[...] = m_new
        v_out[...] = v_new
        d_out[...] = -ADAM_LR * ((m_new / c1) / (jnp.sqrt(v_new / c2) + ADAM_EPS) + ADAM_WD * w_ref[...])

    spec = pl.BlockSpec((tr, c), lambda i: (i, 0))
    return pl.pallas_call(
        body, name=name, grid=(r // tr,),
        in_specs=[spec] * (3 + ng), out_specs=[spec] * 4, out_shape=[jax.ShapeDtypeStruct((r, c), F32)] * 4,
        compiler_params=_params("parallel"),
    )(w, *grads, m, v)


CHIP_PEERS = ((1, 0), (0, 1), (1, 1))


def _place():
    return lax.axis_index("x"), lax.axis_index("y"), lax.axis_index("c")


HBM = pl.BlockSpec(memory_space=pltpu.HBM)
SEM = pl.BlockSpec(memory_space=pltpu.SEMAPHORE)
IN_FLIGHT = pltpu.SideEffectType.DATAFLOW_SIDE_EFFECTING


def _in_hbm(a):
    return pltpu.with_memory_space_constraint(a, pltpu.HBM)


def _cast_to_lands(shards, dtypes, *, name):
    n = len(shards)

    def body(*refs):
        ins, outs, bufs, sems = refs[:n], refs[n:2 * n], refs[2 * n:3 * n], refs[3 * n]
        x, y, _ = _place()
        copies = []
        for a in range(n):
            bufs[a][...] = ins[a][...].astype(dtypes[a])
            cp = pltpu.make_async_copy(bufs[a], outs[a].at[2 * x + y], sems.at[a])
            cp.start()
            copies.append(cp)
        for cp in copies:
            cp.wait()

    return pl.pallas_call(
        body, name=name, in_specs=[pl.BlockSpec(memory_space=pltpu.VMEM)] * n, out_specs=[ANY] * n,
        out_shape=[jax.ShapeDtypeStruct((N_CHIPS,) + s.shape, dt) for s, dt in zip(shards, dtypes)],
        scratch_shapes=[pltpu.VMEM(s.shape, dt) for s, dt in zip(shards, dtypes)] + [pltpu.SemaphoreType.DMA((n,))],
        compiler_params=pltpu.CompilerParams(vmem_limit_bytes=V7X_VMEM_LIMIT_BYTES),
    )(*shards)


def _chip_copy(src, dst, send, recv, flip, place):
    x, y, c = place
    return pltpu.make_async_remote_copy(src_ref=src, dst_ref=dst, send_sem=send, recv_sem=recv,
                                        device_id=(x ^ flip[0], y ^ flip[1], c), device_id_type=MESH)


def _my_part(land, place, halved):
    block = land.at[2 * place[0] + place[1]]
    if not halved:
        return block
    rows = land.shape[1] // 2
    return block.at[pl.ds(pl.multiple_of(place[2] * rows, rows), rows)]


def _gather_start(lands, after, *, name, halved=()):
    n = len(lands)

    def body(*refs):
        ins, send, recv, token = refs[:n], refs[n + 1], refs[n + 2], refs[-1]
        place = _place()
        for a in range(n):
            part = _my_part(ins[a], place, a in halved)
            for p, flip in enumerate(CHIP_PEERS):
                k = 3 * a + p
                _chip_copy(part, part, send.at[k], recv.at[k], flip, place).start()
        token[...] = jnp.zeros_like(token)

    outs = pl.pallas_call(
        body, name=name, in_specs=[HBM] * n + [ANY],
        out_specs=(SEM, SEM, *[HBM] * n, pl.BlockSpec(memory_space=pltpu.VMEM)),
        out_shape=(pltpu.SemaphoreType.DMA((3 * n,)), pltpu.SemaphoreType.DMA((3 * n,)),
                   *[pltpu.HBM(l.shape, l.dtype) for l in lands], jax.ShapeDtypeStruct((8, 128), F32)),
        input_output_aliases={a: 2 + a for a in range(n)},
        compiler_params=pltpu.CompilerParams(has_side_effects=IN_FLIGHT),
    )(*[_in_hbm(l) for l in lands], after)
    return outs[0], outs[1], list(outs[2:2 + n]), outs[-1]


def _gather_wait(send, recv, lands, after, *, name, halved=()):
    n = len(lands)

    def body(*refs):
        ins, send_ref, recv_ref = refs[:n], refs[n], refs[n + 1]
        place = _place()
        for a in range(n):
            part = _my_part(ins[a], place, a in halved)
            for p, flip in enumerate(CHIP_PEERS):
                k = 3 * a + p
                cp = _chip_copy(part, part, send_ref.at[k], recv_ref.at[k], flip, place)
                cp.wait_send()
                cp.wait_recv()

    return pl.pallas_call(
        body, name=name, in_specs=[HBM] * n + [SEM, SEM, ANY], out_specs=[HBM] * n,
        out_shape=[pltpu.HBM(l.shape, l.dtype) for l in lands],
        input_output_aliases={a: a for a in range(n)},
        compiler_params=pltpu.CompilerParams(has_side_effects=IN_FLIGHT),
    )(*lands, send, recv, after)


def _forward_to_sibling(land, *, name):
    rows = land.shape[1] // 2

    def body(land_ref, out_ref, send, recv):
        x, y, c = _place()
        copies = []
        for p, (fx, fy) in enumerate(CHIP_PEERS):
            chip = 2 * (x ^ fx) + (y ^ fy)
            mine = pl.ds(pl.multiple_of(c * rows, rows), rows)
            theirs = pl.ds(pl.multiple_of((1 - c) * rows, rows), rows)
            out = pltpu.make_async_remote_copy(
                src_ref=land_ref.at[chip].at[mine], dst_ref=out_ref.at[chip].at[mine], send_sem=send.at[p],
                recv_sem=recv.at[p], device_id=(x, y, 1 - c), device_id_type=MESH)
            out.start()
            copies.append((out, pltpu.make_async_remote_copy(
                src_ref=land_ref.at[chip].at[theirs], dst_ref=out_ref.at[chip].at[theirs], send_sem=send.at[p],
                recv_sem=recv.at[p], device_id=(x, y, 1 - c), device_id_type=MESH)))
        for out, arriving in copies:
            out.wait_send()
            arriving.wait_recv()

    return pl.pallas_call(
        body, name=name, in_specs=[ANY], out_specs=ANY, out_shape=jax.ShapeDtypeStruct(land.shape, land.dtype),
        input_output_aliases={0: 0},
        scratch_shapes=[pltpu.SemaphoreType.DMA((3,)), pltpu.SemaphoreType.DMA((3,))],
    )(land)


def _scatter_start(grad, *, name):
    def body(g_ref, land_ref, send, recv, g_thru, land_thru, token):
        place = _place()
        for p, flip in enumerate(CHIP_PEERS):
            peer_chip = 2 * (place[0] ^ flip[0]) + (place[1] ^ flip[1])
            _chip_copy(g_ref.at[peer_chip], land_ref.at[p], send.at[p], recv.at[p], flip, place).start()
        token[...] = jnp.zeros_like(token)

    land = lax.empty((3,) + grad.shape[1:], grad.dtype)
    return pl.pallas_call(
        body, name=name, in_specs=[HBM, HBM],
        out_specs=(SEM, SEM, HBM, HBM, pl.BlockSpec(memory_space=pltpu.VMEM)),
        out_shape=(pltpu.SemaphoreType.DMA((3,)), pltpu.SemaphoreType.DMA((3,)), pltpu.HBM(grad.shape, grad.dtype),
                   pltpu.HBM(land.shape, land.dtype), jax.ShapeDtypeStruct((8, 128), F32)),
        input_output_aliases={0: 2, 1: 3},
        compiler_params=pltpu.CompilerParams(has_side_effects=IN_FLIGHT),
    )(_in_hbm(grad), _in_hbm(land))


def _scatter_wait(started, after, *, name):
    n = len(started)

    def body(*refs):
        grads, lands = refs[:n], refs[n:2 * n]
        sends, recvs = refs[2 * n:3 * n], refs[3 * n:4 * n]
        place = _place()
        for a in range(n):
            for p, flip in enumerate(CHIP_PEERS):
                cp = _chip_copy(grads[a].at[0], lands[a].at[p], sends[a].at[p], recvs[a].at[p], flip, place)
                cp.wait_send()
                cp.wait_recv()

    grads, lands = [s[2] for s in started], [s[3] for s in started]
    after = list(after) if isinstance(after, (list, tuple)) else [after]
    outs = pl.pallas_call(
        body, name=name, in_specs=[HBM] * (2 * n) + [SEM] * (2 * n) + [ANY] * len(after), out_specs=[HBM] * (2 * n),
        out_shape=[pltpu.HBM(a.shape, a.dtype) for a in grads + lands],
        input_output_aliases={a: a for a in range(2 * n)},
        compiler_params=pltpu.CompilerParams(has_side_effects=IN_FLIGHT),
    )(*grads, *lands, *[s[0] for s in started], *[s[1] for s in started], *after)
    return list(zip(outs[:n], outs[n:]))


def _sibling_copy(src, dst, send, recv, place):
    x, y, c = place
    return pltpu.make_async_remote_copy(src_ref=src, dst_ref=dst, send_sem=send, recv_sem=recv,
                                        device_id=(x, y, 1 - c), device_id_type=MESH)


def _swap_start(arrays, *, name):
    n = len(arrays)

    def body(*refs):
        ins, lands, send, recv, token = refs[:n], refs[n:2 * n], refs[2 * n], refs[2 * n + 1], refs[-1]
        place = _place()
        for a in range(n):
            _sibling_copy(ins[a], lands[a], send.at[a], recv.at[a], place).start()
        token[...] = jnp.zeros_like(token)

    both = [_in_hbm(a) for a in arrays] + [_in_hbm(lax.empty(a.shape, a.dtype)) for a in arrays]
    outs = pl.pallas_call(
        body, name=name, in_specs=[HBM] * (2 * n),
        out_specs=(SEM, SEM, *[HBM] * (2 * n), pl.BlockSpec(memory_space=pltpu.VMEM)),
        out_shape=(pltpu.SemaphoreType.DMA((n,)), pltpu.SemaphoreType.DMA((n,)),
                   *[pltpu.HBM(a.shape, a.dtype) for a in both], jax.ShapeDtypeStruct((8, 128), F32)),
        input_output_aliases={a: 2 + a for a in range(2 * n)},
        compiler_params=pltpu.CompilerParams(has_side_effects=IN_FLIGHT),
    )(*both)
    return outs[0], outs[1], list(outs[2:2 + n]), list(outs[2 + n:2 + 2 * n]), outs[-1]


def _swap_wait(started, after, *, name):
    send, recv, arrays, lands = started[:4]
    n = len(arrays)

    def body(*refs):
        ins, zones, send_ref, recv_ref = refs[:n], refs[n:2 * n], refs[2 * n], refs[2 * n + 1]
        place = _place()
        for a in range(n):
            cp = _sibling_copy(ins[a], zones[a], send_ref.at[a], recv_ref.at[a], place)
            cp.wait_send()
            cp.wait_recv()

    after = list(after) if isinstance(after, (list, tuple)) else [after]
    outs = pl.pallas_call(
        body, name=name, in_specs=[HBM] * (2 * n) + [SEM, SEM] + [ANY] * len(after), out_specs=[HBM] * (2 * n),
        out_shape=[pltpu.HBM(a.shape, a.dtype) for a in arrays + lands],
        input_output_aliases={a: a for a in range(2 * n)},
        compiler_params=pltpu.CompilerParams(has_side_effects=IN_FLIGHT),
    )(*arrays, *lands, send, recv, *after)
    return list(outs[:n]), list(outs[n:])


def _allreduce_start(packed, *, name):
    n_dev = 8

    def body(src_ref, land_ref, send, recv, src_thru, land_thru, token):
        x, y, c = _place()
        me = 4 * x + 2 * y + c
        for p in range(1, n_dev):
            pltpu.make_async_remote_copy(
                src_ref=src_ref, dst_ref=land_ref.at[me], send_sem=send.at[p - 1], recv_sem=recv.at[p - 1],
                device_id=(x ^ (p >> 2), y ^ ((p >> 1) & 1), c ^ (p & 1)), device_id_type=MESH).start()
        token[...] = jnp.zeros_like(token)

    land = lax.empty((n_dev,) + packed.shape, packed.dtype)
    return pl.pallas_call(
        body, name=name, in_specs=[HBM, HBM],
        out_specs=(SEM, SEM, HBM, HBM, pl.BlockSpec(memory_space=pltpu.VMEM)),
        out_shape=(pltpu.SemaphoreType.DMA((n_dev - 1,)), pltpu.SemaphoreType.DMA((n_dev - 1,)),
                   pltpu.HBM(packed.shape, packed.dtype), pltpu.HBM(land.shape, land.dtype),
                   jax.ShapeDtypeStruct((8, 128), F32)),
        input_output_aliases={0: 2, 1: 3},
        compiler_params=pltpu.CompilerParams(has_side_effects=IN_FLIGHT),
    )(_in_hbm(packed), _in_hbm(land))


def _allreduce_wait(started, after, *, name):
    send, recv, packed, land = started[:4]
    n_dev = 8

    def body(src_ref, land_ref, send_ref, recv_ref, *_):
        x, y, c = _place()
        for p in range(1, n_dev):
            cp = pltpu.make_async_remote_copy(
                src_ref=src_ref, dst_ref=land_ref.at[0], send_sem=send_ref.at[p - 1], recv_sem=recv_ref.at[p - 1],
                device_id=(x ^ (p >> 2), y ^ ((p >> 1) & 1), c ^ (p & 1)), device_id_type=MESH)
            cp.wait_send()
            cp.wait_recv()

    after = list(after) if isinstance(after, (list, tuple)) else [after]
    return pl.pallas_call(
        body, name=name, in_specs=[HBM, HBM, SEM, SEM] + [ANY] * len(after), out_specs=[HBM, HBM],
        out_shape=[pltpu.HBM(packed.shape, packed.dtype), pltpu.HBM(land.shape, land.dtype)],
        input_output_aliases={0: 0, 1: 1},
        compiler_params=pltpu.CompilerParams(has_side_effects=IN_FLIGHT),
    )(packed, land, send, recv, *after)


def _sum_devices(mine, land, *, name):
    n_dev = land.shape[0]

    def body(mine_ref, land_ref, out_ref):
        x, y, c = _place()
        me = 4 * x + 2 * y + c
        total = None
        for s in range(n_dev):
            part = jnp.where(me == s, mine_ref[...], land_ref[s])
            total = part if total is None else total + part
        out_ref[...] = total

    return pl.pallas_call(body, name=name, out_shape=jax.ShapeDtypeStruct(mine.shape, mine.dtype))(mine, land)


def _sum_received(grad, land, *, name, tr=256):
    _, r, c = grad.shape
    tr = _pick(r, tr, 8)

    def body(chip_ref, g_ref, l_ref, o_ref):
        o_ref[...] = ((g_ref[...] + l_ref[0].astype(F32)) + l_ref[1].astype(F32)) + l_ref[2].astype(F32)

    chip = (2 * lax.axis_index("x") + lax.axis_index("y")).astype(jnp.int32).reshape(1)
    return pl.pallas_call(
        body, name=name,
        grid_spec=pltpu.PrefetchScalarGridSpec(
            num_scalar_prefetch=1, grid=(r // tr,),
            in_specs=[pl.BlockSpec((None, tr, c), lambda i, chip_ref: (chip_ref[0], i, 0)),
                      pl.BlockSpec((3, tr, c), lambda i, chip_ref: (0, i, 0))],
            out_specs=pl.BlockSpec((tr, c), lambda i, chip_ref: (i, 0))),
        out_shape=jax.ShapeDtypeStruct((r, c), F32), compiler_params=_params("parallel"),
    )(chip, grad, land)


def _allreduce_small(packed, *, name, after=None):
    r, d = packed.shape
    n_dev = 8

    def body(src_ref, out_ref, buf, send, recv):
        x, y, c = _place()
        me = 4 * x + 2 * y + c
        started = []
        for p in range(1, n_dev):
            rc = pltpu.make_async_remote_copy(
                src_ref=src_ref, dst_ref=buf.at[me], send_sem=send.at[p - 1], recv_sem=recv.at[p - 1],
                device_id=(x ^ (p >> 2), y ^ ((p >> 1) & 1), c ^ (p & 1)), device_id_type=MESH)
            rc.start()
            started.append(rc)
        buf[me] = src_ref[...]
        for rc in started:
            rc.wait()
        total = buf[0]
        for s in range(1, n_dev):
            total = total + buf[s]
        out_ref[...] = total

    vmem = pl.BlockSpec(memory_space=pltpu.VMEM)
    body, more_specs, more_args = _ordered(body, 1, after)
    return pl.pallas_call(
        body, name=name, in_specs=[vmem] + more_specs, out_specs=vmem, out_shape=jax.ShapeDtypeStruct((r, d), F32),
        scratch_shapes=[pltpu.VMEM((n_dev, r, d), F32), pltpu.SemaphoreType.DMA((n_dev - 1,)),
                        pltpu.SemaphoreType.DMA((n_dev - 1,))],
    )(packed, *more_args)


def _packed_rows(size, d):
    return -(-size // (8 * d)) * 8


def _pack_rows(arrays, d):
    rows = []
    for arr in arrays:
        flat = arr.reshape(-1).astype(F32)
        n = _packed_rows(flat.shape[0], d)
        rows.append(jnp.pad(flat, (0, n * d - flat.shape[0])).reshape(n, d))
    return jnp.concatenate(rows, axis=0)


def _unpack_rows(packed, shapes, d):
    out, row = [], 0
    for shape in shapes:
        size = math.prod(shape)
        n = _packed_rows(size, d)
        out.append(packed[row:row + n].reshape(-1)[:size].reshape(shape))
        row += n
    return out


SMALL = ("norm1_g", "gate_b", "conv_b", "conv_norm_g", "q_norm_g", "k_norm_g", "norm2_g", "ffn_conv_b")
LARGE = ("w_in", "w_conv_out", "w_attn_out", "w_out", "w_up", "w_down")
WEIGHTS = ("norm1_g", "w_in", "gate_b", "conv_w", "conv_b", "conv_norm_g", "w_conv_out", "q_norm_g", "k_norm_g",
           "w_attn_out", "w_out", "norm2_g", "w_up", "ffn_conv_w", "ffn_conv_b", "w_down")


def _head_ones(dims):
    a = dims.n_heads * dims.head_dim
    head = jnp.arange(a, dtype=jnp.int32) // dims.head_dim
    return (head[:, None] == head[None, :]).astype(BF16)


def _after(vec, token):
    return vec if token is None else vec + token[0:1, 0:1]


def _local_step(dims, x, target, small, first_weights, other_weights, send_grad):
    d, f, heads = dims.d_model, dims.d_ff, dims.n_heads
    small = dict(small)
    row = lambda name: small[name].reshape(1, -1)
    head_sum = _head_sum_matrix(dims)
    head_spread = jnp.transpose(head_sum)
    ones = (head_sum, head_spread)
    gq = jnp.tile(row("q_norm_g"), (1, heads))
    gk = jnp.tile(row("k_norm_g"), (1, heads))
    one_shard = lambda w: w.reshape(1, -1, w.shape[-1])

    h = _rmsnorm_fwd(x, row("norm1_g"), name="norm1")
    full = first_weights(h)
    w_in = full["w_in"]
    conv_w = jnp.pad(full["conv_w"], ((0, CONV_HALO - dims.conv_width), (0, 0)))
    ffn_w = jnp.pad(full["ffn_conv_w"], ((0, FFN_HALO - dims.ffn_conv_width), (0, 0)))
    z = _mm_nn(h, w_in, out_dtype=BF16, after=full.get("token"), tm=2048, tn=1792, name="in_proj")
    a1, a3 = _conv_branch_fwd(z, conv_w, row("conv_b"), row("conv_norm_g"), dims, name="conv_branch")
    qkv = _qkv_layouts_fwd(z, gq, gk, ones, dims, name="qk_norm")
    per_group = {dil: _attn_fwd(*qkv[dil], dims, dil, name=f"attn_fwd_d{dil}") for dil in DILATIONS}
    o, lse = _attn_combine(per_group, head_spread, dims, name="attn_combine")
    full = other_weights(o)
    w_up = full["w_up"]
    w_co, w_ao, w_o, w_dn = (one_shard(full[k]) for k in ("w_conv_out", "w_attn_out", "w_out", "w_down"))
    ya = _mm_nn(a3, w_co, out_dtype=F32, name="conv_out_proj")
    yb = _mm_nn(o, w_ao, out_dtype=F32, name="attn_out_proj")
    mixed = _mix_fwd(ya, yb, z, row("gate_b"), dims, name="gate_mix")
    x1, h2 = _proj_residual_norm(mixed, w_o, x, row("norm2_g"), name="out_proj_norm2")
    up = _mm_nn(h2, w_up, out_dtype=F32, tm=2048, name="up_proj")
    act = _ffn_act_fwd(up, ffn_w, row("ffn_conv_b"), dims, name="ffn_act")
    dy, dy_b, loss = _proj_residual_loss(act, w_dn, x1, target, tm=512, name="down_proj_loss")

    grads = {}

    def large(name, g):
        grads[name], g_bf16 = g
        return send_grad(name, g_bf16)

    sent = large("w_down", _mm_tn(act, dy_b, n_shards=1, name="dw_down"))
    dact = _mm_nt(dy_b, w_dn, out_dtype=BF16, after=sent, name="d_act")
    dup, dfw, dfb = _ffn_bwd(dact, up, ffn_w, row("ffn_conv_b"), dims, name="ffn_bwd")
    grads["ffn_conv_w"], grads["ffn_conv_b"] = dfw[:dims.ffn_conv_width], dfb
    sent = large("w_up", _mm_tn(h2, dup, n_shards=N_CHIPS, name="dw_up"))
    dh2 = _mm_nt(dup, w_up, out_dtype=F32, after=sent, name="d_h2")
    dx1, dx1_b, grads["norm2_g"] = _rmsnorm_bwd(x1, row("norm2_g"), dh2, dy, want_bf16=True, name="norm2_bwd")
    sent = large("w_out", _mm_tn(mixed, dx1_b, n_shards=1, name="dw_out"))
    dmix = _mm_nt(dx1_b, w_o, out_dtype=F32, after=sent, name="d_mix")
    dya, dyb, dz_gate, grads["gate_b"] = _mix_bwd(dmix, ya, yb, z, row("gate_b"), dims, name="gate_mix_bwd")
    sent = large("w_attn_out", _mm_tn(o, dyb, n_shards=1, name="dw_attn_out"))
    do = _mm_nt(dyb, w_ao, out_dtype=BF16, after=sent, name="d_attn")
    dos, deltas = _attn_bwd_prep(do, o, head_sum, dims, name="attn_bwd_prep")
    dqkv = {dil: _attn_bwd(*qkv[dil], dos[dil], lse[dil], deltas[dil], dims, dil, name=f"attn_bwd_d{dil}")
            for dil in DILATIONS}
    dz_qkv, dgq, dgk = _qkv_layouts_bwd(z, dqkv, gq, gk, ones, dims, name="qk_norm_bwd")
    grads["q_norm_g"] = dgq.reshape(heads, dims.head_dim).sum(axis=0)
    grads["k_norm_g"] = dgk.reshape(heads, dims.head_dim).sum(axis=0)
    sent = large("w_conv_out", _mm_tn(a3, dya, n_shards=1, name="dw_conv_out"))
    da3 = _mm_nt(dya, w_co, out_dtype=F32, after=sent, name="d_conv_act")
    da1, grads["conv_norm_g"] = _conv_norm_bwd(da3, a1, row("conv_norm_g"), name="conv_norm_bwd")
    dz, dcw, grads["conv_b"] = _conv_branch_bwd(da1, z, conv_w, [dz_qkv, dz_gate], dims, name="conv_branch_bwd")
    grads["conv_w"] = dcw[:dims.conv_width]
    sent = large("w_in", _mm_tn(h, dz, n_shards=N_CHIPS, name="dw_in"))
    dh = _mm_nt(dz, w_in, out_dtype=F32, after=sent, name="d_h")
    dx, grads["norm1_g"] = _rmsnorm_bwd(x, row("norm1_g"), dh, dx1, want_bf16=False, name="norm1_bwd")
    return loss, dx, grads


def _step(dims, x, target, w, m, v):
    d = dims.d_model
    t = dims.tokens
    sq = lambda a: a.reshape(a.shape[1:])
    w2, m2, v2 = ({k: sq(a) for k, a in grp.items()} for grp in (w, m, v))

    conv_pad = jnp.pad(w2["conv_w"], ((0, CONV_HALO - dims.conv_width), (0, 0)))
    ffn_pad = jnp.pad(w2["ffn_conv_w"], ((0, FFN_HALO - dims.ffn_conv_width), (0, 0)))
    gathered_names = LARGE + ("conv_w", "ffn_conv_w")
    lands = dict(zip(gathered_names, _cast_to_lands([w2[k] for k in LARGE] + [conv_pad, ffn_pad],
                                                   [BF16] * len(LARGE) + [F32, F32], name="cast_weights")))
    first_names = ("w_in", "conv_w", "ffn_conv_w")
    other_names = tuple(k for k in gathered_names if k not in first_names)
    first = _gather_start([lands[k] for k in first_names], x, halved=(0,), name="gather_start_first")
    other = []
    cols = lambda g, rows: jnp.moveaxis(g, 0, 1).reshape(g.shape[1], -1)[:rows]

    def first_weights(after):
        got = dict(zip(first_names, _gather_wait(*first[:3], after, halved=(0,), name="gather_wait_first")))
        got["w_in"] = _forward_to_sibling(got["w_in"], name="forward_w_in")
        other.extend(_gather_start([lands[k] for k in other_names], got["w_in"], name="gather_start_other"))
        got["conv_w"] = cols(got["conv_w"], dims.conv_width)
        got["ffn_conv_w"] = cols(got["ffn_conv_w"], dims.ffn_conv_width)
        got["token"] = other[3]
        return got

    def other_weights(after):
        return dict(zip(other_names, _gather_wait(*other[:3], after, name="gather_wait_other")))

    started = {}

    def send_grad(name, g):
        send, recv, g_thru, land, token = _scatter_start(g.reshape(N_CHIPS, -1, g.shape[-1]), name=f"scatter_start_{name}")
        started[name] = (send, recv, g_thru, land)
        return token

    small = {k: w2[k] for k in SMALL}
    small["norm1_g"] = _after(small["norm1_g"].reshape(1, -1), first[3])
    loss, dx, grads = _local_step(dims, x.reshape(t, d), target.reshape(t, d), small, first_weights, other_weights, send_grad)

    def my_sums(names, after, tag):
        arrived = _scatter_wait([started[k] for k in names], after, name=f"scatter_wait_{tag}")
        blocks = [grads[k].reshape(N_CHIPS, -1, grads[k].shape[-1]) for k in names]
        return [_sum_received(g, land, name=f"sum_{k}") for k, g, (_, land) in zip(names, blocks, arrived)]

    def updates(names, mine, theirs):
        return {k: _adamw(w2[k], [a, b], m2[k], v2[k], name=f"adamw_{k}") for k, a, b in zip(names, mine, theirs)}

    small_names = SMALL + ("conv_w", "ffn_conv_w")
    packed = _pack_rows([grads[k] for k in small_names] + [loss[0, 0]], d)
    reducing = _allreduce_start(packed, name="allreduce_start")
    others = [k for k in LARGE if k != "w_in"]
    mine_others = my_sums(others, [dx, reducing[4]], "others")
    swapping_others = _swap_start(mine_others, name="swap_start_others")
    mine_w_in = my_sums(["w_in"], swapping_others[4], "w_in")
    swapping_w_in = _swap_start(mine_w_in, name="swap_start_w_in")
    out = updates(others, *_swap_wait(swapping_others, swapping_w_in[4], name="swap_wait_others"))
    last_updates = [out[k][1] for k in others]
    reduced = _sum_devices(*_allreduce_wait(reducing, last_updates, name="allreduce_wait"), name="allreduce_sum")
    shapes = [grads[k].shape for k in small_names] + [()]
    *small_g, loss_total = _unpack_rows(reduced, shapes, d)
    small_g = dict(zip(small_names, small_g))
    chip = 2 * lax.axis_index("x") + lax.axis_index("y")
    for k in ("conv_w", "ffn_conv_w"):
        width = w2[k].shape[1]
        small_g[k] = lax.dynamic_slice_in_dim(small_g[k], chip * width, width, axis=1)

    small_shapes = [w2[k].shape for k in small_names]
    pack = lambda grp: _pack_rows([grp[k] for k in small_names], d)
    results = _adamw(pack(w2), [pack(small_g)], pack(m2), pack(v2), name="adamw_small")
    unpacked = [_unpack_rows(r, small_shapes, d) for r in results]
    out.update({k: tuple(u[i] for u in unpacked) for i, k in enumerate(small_names)})
    out.update(updates(["w_in"], *_swap_wait(swapping_w_in, results[1], name="swap_wait_w_in")))

    lead =lambda a: a.reshape((1,) + a.shape)
    ordered = [[lead(out[k][j].reshape(w2[k].shape)) for k in WEIGHTS] for j in range(4)]
    return (loss_total, dx.reshape(x.shape), *ordered[0], *ordered[1], *ordered[2], *ordered[3])


def kernel(x, norm1_g, w_in, gate_b, conv_w, conv_b, conv_norm_g, w_conv_out, q_norm_g, k_norm_g, w_attn_out, w_out, norm2_g, w_up, ffn_conv_w, ffn_conv_b, w_down, loss_target, m_norm1_g, m_w_in, m_gate_b, m_conv_w, m_conv_b, m_conv_norm_g, m_w_conv_out, m_q_norm_g, m_k_norm_g, m_w_attn_out, m_w_out, m_norm2_g, m_w_up, m_ffn_conv_w, m_ffn_conv_b, m_w_down, v_norm1_g, v_w_in, v_gate_b, v_conv_w, v_conv_b, v_conv_norm_g, v_w_conv_out, v_q_norm_g, v_k_norm_g, v_w_attn_out, v_w_out, v_norm2_g, v_w_up, v_ffn_conv_w, v_ffn_conv_b, v_w_down):
    w = dict(zip(WEIGHTS, (norm1_g, w_in, gate_b, conv_w, conv_b, conv_norm_g, w_conv_out, q_norm_g, k_norm_g,
                           w_attn_out, w_out, norm2_g, w_up, ffn_conv_w, ffn_conv_b, w_down)))
    m = dict(zip(WEIGHTS, (m_norm1_g, m_w_in, m_gate_b, m_conv_w, m_conv_b, m_conv_norm_g, m_w_conv_out, m_q_norm_g,
                           m_k_norm_g, m_w_attn_out, m_w_out, m_norm2_g, m_w_up, m_ffn_conv_w, m_ffn_conv_b, m_w_down)))
    v = dict(zip(WEIGHTS, (v_norm1_g, v_w_in, v_gate_b, v_conv_w, v_conv_b, v_conv_norm_g, v_w_conv_out, v_q_norm_g,
                           v_k_norm_g, v_w_attn_out, v_w_out, v_norm2_g, v_w_up, v_ffn_conv_w, v_ffn_conv_b, v_w_down)))
    dims = Dims(d_model=x.shape[-1], batch_local=x.shape[0], seq=x.shape[1], d_ff=w_down.shape[1] * N_CHIPS)
    return _step(dims, x, loss_target, w, m, v)
```

```python
import functools
import math
from typing import NamedTuple

import jax
import jax.numpy as jnp
from jax import lax
from jax.experimental import pallas as pl
from jax.experimental.pallas import tpu as pltpu

F32 = jnp.float32
BF16 = jnp.bfloat16

RMS_EPS = 1e-6
MASKED_SCORE = -1e30
ATTN_BLOCK = 128
DILATIONS = (1, 4, 16)
CONV_HALO = 32
FFN_HALO = 8
ADAM_LR, ADAM_B1, ADAM_B2, ADAM_EPS, ADAM_WD, ADAM_STEP = 0.001, 0.9, 0.999, 1e-08, 0.01, 10
V7X_VMEM_LIMIT_BYTES = 56 * 2 ** 20
N_CHIPS = 4
MESH = pl.DeviceIdType.MESH


class Dims(NamedTuple):
    d_model: int = 1024
    n_heads: int = 16
    head_dim: int = 64
    d_ff: int = 2816
    seq: int = 2048
    batch_local: int = 2
    conv_width: int = 31
    ffn_conv_width: int = 3

    @property
    def tokens(self):
        return self.seq * self.batch_local


def _params(*semantics):
    return pltpu.CompilerParams(dimension_semantics=semantics, vmem_limit_bytes=V7X_VMEM_LIMIT_BYTES)


ANY = pl.BlockSpec(memory_space=pl.ANY)


def _ordered(body, n_inputs, after):
    after = [] if after is None else list(after) if isinstance(after, (list, tuple)) else [after]
    if not after:
        return body, [], []

    def wrapped(*refs):
        return body(*refs[:n_inputs], *refs[n_inputs + len(after):])

    return wrapped, [ANY] * len(after), after


def _pick(n, target, mult=128):
    if n <= target:
        return n
    best = None
    for t in range(mult, target + 1, mult):
        if n % t == 0:
            best = t
    assert best is not None, (n, target, mult)
    return best


def _sigmoid(v):
    return 1.0 / (1.0 + jnp.exp(-v))


def _mm_nn(a, w, *, out_dtype, name, residual=None, after=None, tm=1024, tn=1408, tk=2816):
    m, k = a.shape
    nsh, k2, c = w.shape
    assert k == k2 and a.dtype == BF16 and w.dtype == BF16
    n = nsh * c
    tm, tn, tk = _pick(m, tm, 8), _pick(c, tn), _pick(k, tk)
    nk, cpn = k // tk, c // tn

    def body(*refs):
        if residual is None:
            a_ref, w_ref, o_ref, acc = refs
        else:
            a_ref, w_ref, r_ref, o_ref, acc = refs
        prod = jnp.dot(a_ref[...], w_ref[...], preferred_element_type=F32)

        def finish(total):
            if residual is not None:
                total = total + r_ref[...]
            o_ref[...] = total.astype(out_dtype)

        if nk == 1:
            finish(prod)
        else:
            kk = pl.program_id(2)

            @pl.when(kk == 0)
            def _():
                acc[...] = prod

            @pl.when(kk > 0)
            def _():
                acc[...] += prod

            @pl.when(kk == nk - 1)
            def _():
                finish(acc[...])

    in_specs = [pl.BlockSpec((tm, tk), lambda i, j, kk: (i, kk)),
                pl.BlockSpec((None, tk, tn), lambda i, j, kk: (j // cpn, kk, j % cpn))]
    args = [a, w]
    if residual is not None:
        in_specs.append(pl.BlockSpec((tm, tn), lambda i, j, kk: (i, j)))
        args.append(residual)
    body, more_specs, more_args = _ordered(body, len(args), after)
    return pl.pallas_call(
        body, name=name, grid=(m // tm, n // tn, nk),
        in_specs=in_specs + more_specs, out_specs=pl.BlockSpec((tm, tn), lambda i, j, kk: (i, j)),
        out_shape=jax.ShapeDtypeStruct((m, n), out_dtype),
        scratch_shapes=[pltpu.VMEM((tm, tn) if nk > 1 else (8, 128), F32)],
        compiler_params=_params("parallel", "parallel", "arbitrary"),
    )(*args, *more_args)


def _proj_residual_norm(a, w, residual, g, *, name, tm=1024):
    m, k = a.shape
    _, k2, n = w.shape
    assert w.shape[0] == 1 and k == k2 and a.dtype == BF16 and w.dtype == BF16
    tm = _pick(m, tm, 8)

    def body(a_ref, w_ref, r_ref, g_ref, y_ref, h_ref):
        y = r_ref[...] + jnp.dot(a_ref[...], w_ref[...], preferred_element_type=F32)
        y_ref[...] = y
        h_ref[...] = (y * lax.rsqrt(jnp.mean(y * y, axis=-1, keepdims=True) + RMS_EPS) * g_ref[...]).astype(BF16)

    rows = lambda width: pl.BlockSpec((tm, width), lambda i: (i, 0))
    return pl.pallas_call(
        body, name=name, grid=(m // tm,),
        in_specs=[rows(k), pl.BlockSpec((None, k, n), lambda i: (0, 0, 0)), rows(n), pl.BlockSpec((1, n), lambda i: (0, 0))],
        out_specs=[rows(n), rows(n)],
        out_shape=[jax.ShapeDtypeStruct((m, n), F32), jax.ShapeDtypeStruct((m, n), BF16)],
        compiler_params=_params("parallel"),
    )(a, w, residual, g)


def _proj_residual_loss(a, w, residual, target, *, name, tm=1024):
    m, k = a.shape
    _, k2, n = w.shape
    assert w.shape[0] == 1 and k == k2 and a.dtype == BF16 and w.dtype == BF16
    tm = _pick(m, tm, 8)

    def body(a_ref, w_ref, r_ref, t_ref, dy_ref, dyb_ref, loss_ref):
        err = r_ref[...] + jnp.dot(a_ref[...], w_ref[...], preferred_element_type=F32) - t_ref[...]
        dy = err * (1.0 / n)
        dy_ref[...] = dy
        dyb_ref[...] = dy.astype(BF16)
        part = jnp.sum(jnp.sum(err * err, axis=-1, keepdims=True), axis=0, keepdims=True) * (0.5 / n)
        _accumulate(loss_ref, jnp.broadcast_to(part, (8, 128)), pl.program_id(0) == 0)

    rows = lambda width: pl.BlockSpec((tm, width), lambda i: (i, 0))
    return pl.pallas_call(
        body, name=name, grid=(m // tm,),
        in_specs=[rows(k), pl.BlockSpec((None, k, n), lambda i: (0, 0, 0)), rows(n), rows(n)],
        out_specs=[rows(n), rows(n), pl.BlockSpec((8, 128), lambda i: (0, 0))],
        out_shape=[jax.ShapeDtypeStruct((m, n), F32), jax.ShapeDtypeStruct((m, n), BF16),
                   jax.ShapeDtypeStruct((8, 128), F32)],
        compiler_params=_params("arbitrary"),
    )(a, w, residual, target)


def _mm_nt(a, w, *, out_dtype, name, after=None, tm=1024, tn=1408, tk=1792):
    m, k = a.shape
    nsh, r, c = w.shape
    assert k == nsh * c and a.dtype == BF16 and w.dtype == BF16
    tm, tn, tk = _pick(m, tm, 8), _pick(r, tn), _pick(c, tk)
    nk, cpk = k // tk, c // tk

    def body(a_ref, w_ref, o_ref, acc):
        prod = lax.dot_general(a_ref[...], w_ref[...], (((1,), (1,)), ((), ())), preferred_element_type=F32)
        if nk == 1:
            o_ref[...] = prod.astype(out_dtype)
        else:
            kk = pl.program_id(2)

            @pl.when(kk == 0)
            def _():
                acc[...] = prod

            @pl.when(kk > 0)
            def _():
                acc[...] += prod

            @pl.when(kk == nk - 1)
            def _():
                o_ref[...] = acc[...].astype(out_dtype)

    body, more_specs, more_args = _ordered(body, 2, after)
    return pl.pallas_call(
        body, name=name, grid=(m // tm, r // tn, nk),
        in_specs=[pl.BlockSpec((tm, tk), lambda i, j, kk: (i, kk)),
                  pl.BlockSpec((None, tn, tk), lambda i, j, kk: (kk // cpk, j, kk % cpk))] + more_specs,
        out_specs=pl.BlockSpec((tm, tn), lambda i, j, kk: (i, j)),
        out_shape=jax.ShapeDtypeStruct((m, r), out_dtype),
        scratch_shapes=[pltpu.VMEM((tm, tn) if nk > 1 else (8, 128), F32)],
        compiler_params=_params("parallel", "parallel", "arbitrary"),
    )(a, w, *more_args)


MM_TN_VMEM_BYTES = 44 * 2 ** 20


def _mm_tn(a, b, *, n_shards, name, tm=1408, tn=1408):
    t, m = a.shape
    t2, n = b.shape
    assert t == t2 and a.dtype == BF16 and b.dtype == BF16
    c = n // n_shards
    tm, tn = _pick(m, tm), _pick(c, tn)
    if m // tm == 1 and n // tn == 1 and tn % (2 * LANES) == 0:
        tn //= 2
    fixed = 2 * tm * tn * 6
    if 4 * t * (tm + tn) + fixed <= MM_TN_VMEM_BYTES:
        tk = t
    else:
        tk = _pick(t, (MM_TN_VMEM_BYTES - fixed - 4 * tm * tn) // (4 * (tm + tn)), 8)
    nk, cpn = t // tk, c // tn

    def body(a_ref, b_ref, o_ref, ob_ref, acc):
        kk = pl.program_id(2)
        prod = lax.dot_general(a_ref[...], b_ref[...], (((0,), (0,)), ((), ())), preferred_element_type=F32)

        def finish(total):
            o_ref[...] = total
            ob_ref[...] = total.astype(BF16)

        if nk == 1:
            finish(prod)
        else:
            @pl.when(kk == 0)
            def _():
                acc[...] = prod

            @pl.when(kk > 0)
            def _():
                acc[...] += prod

            @pl.when(kk == nk - 1)
            def _():
                finish(acc[...])

    out_spec = pl.BlockSpec((None, tm, tn), lambda i, j, kk: (j // cpn, i, j % cpn))
    return pl.pallas_call(
        body, name=name, grid=(m // tm, n // tn, nk),
        in_specs=[pl.BlockSpec((tk, tm), lambda i, j, kk: (kk, i)),
                  pl.BlockSpec((tk, tn), lambda i, j, kk: (kk, j))],
        out_specs=[out_spec, out_spec],
        out_shape=[jax.ShapeDtypeStruct((n_shards, m, c), F32), jax.ShapeDtypeStruct((n_shards, m, c), BF16)],
        scratch_shapes=[pltpu.VMEM((tm, tn) if nk > 1 else (8, 128), F32)],
        compiler_params=_params("parallel", "parallel", "arbitrary"),
    )(a, b)


def _row_spec(tr, width, col=0):
    return pl.BlockSpec((tr, width), lambda i, col=col: (i, col))


def _vec_spec(width, col=0):
    return pl.BlockSpec((1, width), lambda i, col=col: (0, col))


def _accumulate(ref, value, first):
    @pl.when(first)
    def _():
        ref[...] = value

    @pl.when(jnp.logical_not(first))
    def _():
        ref[...] += value


def _rmsnorm_fwd(x, g, *, name, tr=512):
    t, d = x.shape
    tr = _pick(t, tr, 8)

    def body(x_ref, g_ref, o_ref):
        xv = x_ref[...]
        r = lax.rsqrt(jnp.mean(xv * xv, axis=-1, keepdims=True) + RMS_EPS)
        o_ref[...] = (xv * r * g_ref[...]).astype(BF16)

    return pl.pallas_call(
        body, name=name, grid=(t // tr,),
        in_specs=[_row_spec(tr, d), _vec_spec(d)], out_specs=_row_spec(tr, d),
        out_shape=jax.ShapeDtypeStruct((t, d), BF16), compiler_params=_params("parallel"),
    )(x, g)


def _rmsnorm_bwd(x, g, dy, dres, *, name, want_bf16, tr=512):
    t, d = x.shape
    tr = _pick(t, tr, 8)

    def body(x_ref, g_ref, dy_ref, dres_ref, *outs):
        dx_ref, dg_ref = outs[0], outs[-1]
        xv, dyv = x_ref[...], dy_ref[...].astype(F32)
        r = lax.rsqrt(jnp.mean(xv * xv, axis=-1, keepdims=True) + RMS_EPS)
        gy = dyv * g_ref[...]
        dx = dres_ref[...] + r * gy - xv * (r * r * r) * jnp.mean(xv * gy, axis=-1, keepdims=True)
        dx_ref[...] = dx
        if want_bf16:
            outs[1][...] = dx.astype(BF16)
        _accumulate(dg_ref, jnp.sum(dyv * xv * r, axis=0, keepdims=True), pl.program_id(0) == 0)

    out_shape = [jax.ShapeDtypeStruct((t, d), F32)]
    out_specs = [_row_spec(tr, d)]
    if want_bf16:
        out_shape.append(jax.ShapeDtypeStruct((t, d), BF16))
        out_specs.append(_row_spec(tr, d))
    out_shape.append(jax.ShapeDtypeStruct((1, d), F32))
    out_specs.append(_vec_spec(d))
    return pl.pallas_call(
        body, name=name, grid=(t // tr,),
        in_specs=[_row_spec(tr, d), _vec_spec(d), _row_spec(tr, d), _row_spec(tr, d)],
        out_specs=out_specs, out_shape=out_shape, compiler_params=_params("arbitrary"),
    )(x, g, dy, dres)


def _head_mean(v, ones_ref, head_dim):
    hi = v.astype(BF16)
    lo = (v - hi.astype(F32)).astype(BF16)
    e = ones_ref[...]
    total = jnp.dot(hi, e, preferred_element_type=F32) + jnp.dot(lo, e, preferred_element_type=F32)
    return total * (1.0 / head_dim)


def _qkv_fwd(z, gq, gk, head_ones, dims, *, name, tr=256):
    t = z.shape[0]
    a = dims.n_heads * dims.head_dim
    tr = _pick(t, tr, 8)
    q_scale = dims.head_dim ** -0.5

    def body(q_ref, k_ref, v_ref, gq_ref, gk_ref, e_ref, qo_ref, ko_ref, vo_ref):
        qv, kv = q_ref[...], k_ref[...]
        rq = lax.rsqrt(_head_mean(qv * qv, e_ref, dims.head_dim) + RMS_EPS)
        rk = lax.rsqrt(_head_mean(kv * kv, e_ref, dims.head_dim) + RMS_EPS)
        qo_ref[...] = (qv * rq * gq_ref[...] * q_scale).astype(BF16)
        ko_ref[...] = (kv * rk * gk_ref[...]).astype(BF16)
        vo_ref[...] = v_ref[...].astype(BF16)

    return pl.pallas_call(
        body, name=name, grid=(t // tr,),
        in_specs=[_row_spec(tr, a, 2), _row_spec(tr, a, 3), _row_spec(tr, a, 4), _vec_spec(a), _vec_spec(a),
                  pl.BlockSpec((a, a), lambda i: (0, 0))],
        out_specs=[_row_spec(tr, a)] * 3, out_shape=[jax.ShapeDtypeStruct((t, a), BF16)] * 3,
        compiler_params=_params("parallel"),
    )(z, z, z, gq, gk, head_ones)


def _qkv_bwd(z, dqs, dks, dvs, gq, gk, head_ones, dims, *, name, tr=256):
    t = z.shape[0]
    a = dims.n_heads * dims.head_dim
    tr = _pick(t, tr, 8)
    q_scale = dims.head_dim ** -0.5
    ng = len(dqs)

    def body(*refs):
        q_ref, k_ref = refs[:2]
        dq_refs, dk_refs, dv_refs = refs[2:2 + ng], refs[2 + ng:2 + 2 * ng], refs[2 + 2 * ng:2 + 3 * ng]
        gq_ref, gk_ref, e_ref = refs[2 + 3 * ng:5 + 3 * ng]
        dz_ref, dgq_ref, dgk_ref = refs[5 + 3 * ng:]
        first = pl.program_id(0) == 0

        def norm_bwd(x_ref, d_refs, g_ref, scale, col, dg_ref):
            xv = x_ref[...]
            dy = sum(r[...] for r in d_refs) * scale
            r = lax.rsqrt(_head_mean(xv * xv, e_ref, dims.head_dim) + RMS_EPS)
            gy = dy * g_ref[...]
            dx = r * gy - xv * (r * r * r) * _head_mean(xv * gy, e_ref, dims.head_dim)
            dz_ref[:, col * a:(col + 1) * a] = dx.astype(BF16)
            _accumulate(dg_ref, jnp.sum(dy * xv * r, axis=0, keepdims=True), first)

        norm_bwd(q_ref, dq_refs, gq_ref, q_scale, 0, dgq_ref)
        norm_bwd(k_ref, dk_refs, gk_ref, 1.0, 1, dgk_ref)
        dz_ref[:, 2 * a:3 * a] = sum(r[...] for r in dv_refs).astype(BF16)

    in_specs = ([_row_spec(tr, a, 2), _row_spec(tr, a, 3)] + [_row_spec(tr, a)] * (3 * ng)
                + [_vec_spec(a), _vec_spec(a), pl.BlockSpec((a, a), lambda i: (0, 0))])
    return pl.pallas_call(
        body, name=name, grid=(t // tr,), in_specs=in_specs,
        out_specs=[_row_spec(tr, 3 * a), _vec_spec(a), _vec_spec(a)],
        out_shape=[jax.ShapeDtypeStruct((t, 3 * a), BF16)] + [jax.ShapeDtypeStruct((1, a), F32)] * 2,
        compiler_params=_params("arbitrary"),
    )(z, z, *dqs, *dks, *dvs, gq, gk, head_ones)


CONV_ROWS = 16


def _seq_specs(dims, ts, width, halo, col, *, nxt=False):
    nst, per = dims.seq // ts, ts // halo
    last = dims.tokens // halo - 1
    cur = pl.BlockSpec((ts, width), lambda b, i: (b * nst + i, col))
    if nxt:
        edge = pl.BlockSpec((halo, width), lambda b, i: (jnp.minimum((b * nst + i + 1) * per, last), col))
    else:
        edge = pl.BlockSpec((halo, width), lambda b, i: (jnp.maximum((b * nst + i) * per - 1, 0), col))
    return cur, edge


SUBLANES = 8


def _shifted_copies(buf, shifted):
    rows = shifted.shape[1]
    for s in range(1, SUBLANES):
        shifted[s - 1] = buf[pl.ds(s, rows), :]


def _window(buf, shifted, start, size):
    a, s = divmod(start, SUBLANES)
    src = buf if s == 0 else shifted.at[s - 1]
    return src[pl.ds(SUBLANES * a, size), :]


def _conv_branch_fwd(z, w, b, g, dims, *, name, ts=128):
    t, c, kw = z.shape[0], dims.d_model, dims.conv_width
    base = CONV_HALO - (kw - 1)

    def body(av_ref, hv_ref, ag_ref, hg_ref, w_ref, b_ref, g_ref, a1_ref, a3_ref, buf, shifted):
        i = pl.program_id(1)
        buf[CONV_HALO:, :] = av_ref[...].astype(F32) * _sigmoid(ag_ref[...].astype(F32))
        buf[0:CONV_HALO, :] = jnp.where(i > 0, hv_ref[...].astype(F32) * _sigmoid(hg_ref[...].astype(F32)), 0.0)
        _shifted_copies(buf, shifted)
        for r0 in range(0, ts, CONV_ROWS):
            acc = jnp.broadcast_to(b_ref[...], (CONV_ROWS, c))
            for k in range(kw):
                acc = acc + w_ref[k:k + 1, :] * _window(buf, shifted, r0 + base + k, CONV_ROWS)
            a1_ref[r0:r0 + CONV_ROWS, :] = acc
            a2 = acc * lax.rsqrt(jnp.mean(acc * acc, axis=-1, keepdims=True) + RMS_EPS) * g_ref[...]
            a3_ref[r0:r0 + CONV_ROWS, :] = (a2 * _sigmoid(a2)).astype(BF16)

    vec = pl.BlockSpec((1, c), lambda b, i: (0, 0))
    out = pl.BlockSpec((ts, c), lambda b, i: (b * (dims.seq // ts) + i, 0))
    return pl.pallas_call(
        body, name=name, grid=(dims.batch_local, dims.seq // ts),
        in_specs=[*_seq_specs(dims, ts, c, CONV_HALO, 0), *_seq_specs(dims, ts, c, CONV_HALO, 1),
                  pl.BlockSpec((CONV_HALO, c), lambda b, i: (0, 0)), vec, vec],
        out_specs=[out, out],
        out_shape=[jax.ShapeDtypeStruct((t, c), F32), jax.ShapeDtypeStruct((t, c), BF16)],
        scratch_shapes=[pltpu.VMEM((CONV_HALO + ts, c), F32),
                        pltpu.VMEM((SUBLANES - 1, CONV_HALO + ts - SUBLANES, c), F32)],
        compiler_params=_params("parallel", "parallel"),
    )(z, z, z, z, w, b, g)


def _conv_norm_bwd(da3, a1, g, *, name, tr=256):
    t, c = a1.shape
    tr = _pick(t, tr, 8)

    def body(d_ref, a_ref, g_ref, o_ref, dg_ref):
        a1v, gv = a_ref[...], g_ref[...]
        r = lax.rsqrt(jnp.mean(a1v * a1v, axis=-1, keepdims=True) + RMS_EPS)
        a2 = a1v * r * gv
        sg = _sigmoid(a2)
        da2 = d_ref[...].astype(F32) * sg * (1.0 + a2 * (1.0 - sg))
        gy = da2 * gv
        o_ref[...] = r * gy - a1v * (r * r * r) * jnp.mean(a1v * gy, axis=-1, keepdims=True)
        _accumulate(dg_ref, jnp.sum(da2 * a1v * r, axis=0, keepdims=True), pl.program_id(0) == 0)

    return pl.pallas_call(
        body, name=name, grid=(t // tr,),
        in_specs=[_row_spec(tr, c), _row_spec(tr, c), _vec_spec(c)],
        out_specs=[_row_spec(tr, c), _vec_spec(c)],
        out_shape=[jax.ShapeDtypeStruct((t, c), F32), jax.ShapeDtypeStruct((1, c), F32)],
        compiler_params=_params("arbitrary"),
    )(da3, a1, g)


def _conv_branch_bwd(da1, z, w, rest_of_dz, dims, *, name, ts=128):
    t, c, kw = z.shape[0], dims.d_model, dims.conv_width
    nst = dims.seq // ts
    base = CONV_HALO - (kw - 1)
    n_rest = len(rest_of_dz)
    total = 2 * c + sum(r.shape[1] for r in rest_of_dz)

    def body(d_ref, dn_ref, av_ref, hv_ref, ag_ref, hg_ref, w_ref, *more):
        rest_refs = more[:n_rest]
        dz_ref, dw_ref, db_ref, abuf, dbuf, ashift, dshift = more[n_rest:]
        col = 2 * c
        for r in rest_refs:
            dz_ref[:, col:col + r.shape[1]] = r[...]
            col += r.shape[1]
        i = pl.program_id(1)
        first = jnp.logical_and(pl.program_id(0) == 0, i == 0)
        abuf[CONV_HALO:, :] = av_ref[...].astype(F32) * _sigmoid(ag_ref[...].astype(F32))
        abuf[0:CONV_HALO, :] = jnp.where(i > 0, hv_ref[...].astype(F32) * _sigmoid(hg_ref[...].astype(F32)), 0.0)
        d1 = d_ref[...]
        dbuf[0:ts, :] = d1
        dbuf[ts:, :] = jnp.where(i < nst - 1, dn_ref[...], 0.0)
        _shifted_copies(abuf, ashift)
        _shifted_copies(dbuf, dshift)

        @pl.when(first)
        def _():
            dw_ref[...] = jnp.zeros_like(dw_ref)
            db_ref[...] = jnp.zeros_like(db_ref)

        db_ref[...] += jnp.sum(d1, axis=0, keepdims=True)
        for k in range(kw):
            dw_ref[k:k + 1, :] += jnp.sum(d1 * _window(abuf, ashift, base + k, ts), axis=0, keepdims=True)
        for r0 in range(0, ts, CONV_ROWS):
            acc = jnp.zeros((CONV_ROWS, c), F32)
            for k in range(kw):
                acc = acc + w_ref[k:k + 1, :] * _window(dbuf, dshift, r0 + (kw - 1) - k, CONV_ROWS)
            av = av_ref[r0:r0 + CONV_ROWS, :].astype(F32)
            sg = _sigmoid(ag_ref[r0:r0 + CONV_ROWS, :].astype(F32))
            dz_ref[r0:r0 + CONV_ROWS, 0:c] = (acc * sg).astype(BF16)
            dz_ref[r0:r0 + CONV_ROWS, c:2 * c] = (acc * av * sg * (1.0 - sg)).astype(BF16)

    cur, nxt = _seq_specs(dims, ts, c, CONV_HALO, 0, nxt=True)
    return pl.pallas_call(
        body, name=name, grid=(dims.batch_local, nst),
        in_specs=[cur, nxt, *_seq_specs(dims, ts, c, CONV_HALO, 0), *_seq_specs(dims, ts, c, CONV_HALO, 1),
                  pl.BlockSpec((CONV_HALO, c), lambda b, i: (0, 0))]
        + [pl.BlockSpec((ts, r.shape[1]), lambda b, i: (b * nst + i, 0)) for r in rest_of_dz],
        out_specs=[pl.BlockSpec((ts, total), lambda b, i: (b * nst + i, 0)),
                   pl.BlockSpec((CONV_HALO, c), lambda b, i: (0, 0)), pl.BlockSpec((1, c), lambda b, i: (0, 0))],
        out_shape=[jax.ShapeDtypeStruct((t, total), BF16), jax.ShapeDtypeStruct((CONV_HALO, c), F32),
                   jax.ShapeDtypeStruct((1, c), F32)],
        scratch_shapes=[pltpu.VMEM((CONV_HALO + ts, c), F32)] * 2
        + [pltpu.VMEM((SUBLANES - 1, CONV_HALO + ts - SUBLANES, c), F32)] * 2,
        compiler_params=_params("arbitrary", "arbitrary"),
    )(da1, da1, z, z, z, z, w, *rest_of_dz)


FFN_ROWS = 16
FFN_COLS = 256


def _ffn_chunks(ts, f):
    cw = _pick(f, FFN_COLS)
    return [(r0, c0, cw) for r0 in range(0, ts, FFN_ROWS) for c0 in range(0, f, cw)]


def _tap_sources(buf, moved, offsets, rows):
    taps, used = [], 0
    for off in offsets:
        if off % SUBLANES:
            moved[used] = buf[pl.ds(off, rows), :]
            taps.append((moved.at[used], 0))
            used += 1
        else:
            taps.append((buf, off))
    return taps


def _moved_copies(offsets):
    return sum(1 for off in offsets if off % SUBLANES)


def _taps_sum(taps, w_ref, init, r0, cols):
    for k, (src, off) in enumerate(taps):
        init = init + w_ref[k:k + 1, cols] * src[pl.ds(off + r0, init.shape[0]), cols]
    return init


def _ffn_bwd(dact, up, w, b, dims, *, name, ts=128):
    t, f, kw = up.shape[0], dims.d_ff, dims.ffn_conv_width
    nst = dims.seq // ts
    fwd_offsets = [FFN_HALO - (kw - 1) + k for k in range(kw)]
    bwd_offsets = [(kw - 1) - k for k in range(kw)]
    dact_halo = 2 * FFN_HALO

    def body(d_ref, dn_ref, up_ref, hp_ref, hn_ref, w_ref, b_ref, o_ref, dw_ref, db_ref, buf, moved, dbuf, dmoved):
        i = pl.program_id(1)
        first = jnp.logical_and(pl.program_id(0) == 0, i == 0)
        more = i < nst - 1
        buf[0:FFN_HALO, :] = jnp.where(i > 0, hp_ref[...], 0.0)
        buf[FFN_HALO:FFN_HALO + ts, :] = up_ref[...]
        buf[FFN_HALO + ts:, :] = hn_ref[...]
        taps = _tap_sources(buf, moved, fwd_offsets, ts + FFN_HALO)

        def du_chunk(r0, rows, c0, cw, d):
            vcols, gcols = slice(c0, c0 + cw), slice(f + c0, f + c0 + cw)
            uv = _taps_sum(taps, w_ref, jnp.broadcast_to(b_ref[:, vcols], (rows, cw)), r0, vcols)
            ug = _taps_sum(taps, w_ref, jnp.broadcast_to(b_ref[:, gcols], (rows, cw)), r0, gcols)
            sg = _sigmoid(ug)
            dbuf[r0:r0 + rows, vcols] = d * ug * sg
            dbuf[r0:r0 + rows, gcols] = d * uv * sg * (1.0 + ug * (1.0 - sg))

        for r0, c0, cw in _ffn_chunks(ts, f):
            du_chunk(r0, FFN_ROWS, c0, cw, d_ref[r0:r0 + FFN_ROWS, c0:c0 + cw].astype(F32))
        for _, c0, cw in _ffn_chunks(FFN_ROWS, f):
            d_next = dn_ref[:, c0:c0 + cw].astype(F32)[0:FFN_HALO]
            du_chunk(ts, FFN_HALO, c0, cw, jnp.where(more, d_next, 0.0))

        @pl.when(first)
        def _():
            dw_ref[...] = jnp.zeros_like(dw_ref)
            db_ref[...] = jnp.zeros_like(db_ref)

        du = dbuf[0:ts, :]
        db_ref[...] += jnp.sum(du, axis=0, keepdims=True)
        for k, (src, off) in enumerate(taps):
            dw_ref[k:k + 1, :] += jnp.sum(du * src[pl.ds(off, ts), :], axis=0, keepdims=True)

        dtaps = _tap_sources(dbuf, dmoved, bwd_offsets, ts)
        for r0, c0, cw in _ffn_chunks(ts, 2 * f):
            cols = slice(c0, c0 + cw)
            o_ref[r0:r0 + FFN_ROWS, cols] = _taps_sum(dtaps, w_ref, jnp.zeros((FFN_ROWS, cw), F32), r0, cols).astype(BF16)

    up_cur, up_prev = _seq_specs(dims, ts, 2 * f, FFN_HALO, 0)
    _, up_next = _seq_specs(dims, ts, 2 * f, FFN_HALO, 0, nxt=True)
    d_cur, d_next = _seq_specs(dims, ts, f, dact_halo, 0, nxt=True)
    full = lambda rows: pl.BlockSpec((rows, 2 * f), lambda b_, i: (0, 0))
    return pl.pallas_call(
        body, name=name, grid=(dims.batch_local, nst),
        in_specs=[d_cur, d_next, up_cur, up_prev, up_next, full(FFN_HALO), full(1)],
        out_specs=[pl.BlockSpec((ts, 2 * f), lambda b_, i: (b_ * nst + i, 0)), full(FFN_HALO), full(1)],
        out_shape=[jax.ShapeDtypeStruct((t, 2 * f), BF16), jax.ShapeDtypeStruct((FFN_HALO, 2 * f), F32),
                   jax.ShapeDtypeStruct((1, 2 * f), F32)],
        scratch_shapes=[pltpu.VMEM((ts + 2 * FFN_HALO, 2 * f), F32),
                        pltpu.VMEM((_moved_copies(fwd_offsets), ts + FFN_HALO, 2 * f), F32),
                        pltpu.VMEM((ts + FFN_HALO, 2 * f), F32),
                        pltpu.VMEM((_moved_copies(bwd_offsets), ts, 2 * f), F32)],
        compiler_params=_params("arbitrary", "arbitrary"),
    )(dact, dact, up, up, up, w, b)


def _ffn_act_fwd(up, w, b, dims, *, name, ts=128):
    t, f, kw = up.shape[0], dims.d_ff, dims.ffn_conv_width
    offsets = [FFN_HALO - (kw - 1) + k for k in range(kw)]

    def body(up_ref, h_ref, w_ref, b_ref, o_ref, buf, moved):
        buf[FFN_HALO:, :] = up_ref[...]
        buf[0:FFN_HALO, :] = jnp.where(pl.program_id(1) > 0, h_ref[...], 0.0)
        taps = _tap_sources(buf, moved, offsets, ts)
        for r0, c0, cw in _ffn_chunks(ts, f):
            vcols, gcols = slice(c0, c0 + cw), slice(f + c0, f + c0 + cw)
            uv = _taps_sum(taps, w_ref, jnp.broadcast_to(b_ref[:, vcols], (FFN_ROWS, cw)), r0, vcols)
            ug = _taps_sum(taps, w_ref, jnp.broadcast_to(b_ref[:, gcols], (FFN_ROWS, cw)), r0, gcols)
            o_ref[r0:r0 + FFN_ROWS, vcols] = (ug * _sigmoid(ug) * uv).astype(BF16)

    full = lambda rows: pl.BlockSpec((rows, 2 * f), lambda b_, i: (0, 0))
    return pl.pallas_call(
        body, name=name, grid=(dims.batch_local, dims.seq // ts),
        in_specs=[*_seq_specs(dims, ts, 2 * f, FFN_HALO, 0), full(FFN_HALO), full(1)],
        out_specs=pl.BlockSpec((ts, f), lambda b_, i: (b_ * (dims.seq // ts) + i, 0)),
        out_shape=jax.ShapeDtypeStruct((t, f), BF16),
        scratch_shapes=[pltpu.VMEM((FFN_HALO + ts, 2 * f), F32), pltpu.VMEM((_moved_copies(offsets), ts, 2 * f), F32)],
        compiler_params=_params("parallel", "parallel"),
    )(up, up, w, b)


def _ffn_act_bwd(dact, up, w, b, dims, *, name, ts=128):
    t, f, kw = up.shape[0], dims.d_ff, dims.ffn_conv_width
    offsets = [FFN_HALO - (kw - 1) + k for k in range(kw)]

    def body(d_ref, up_ref, h_ref, w_ref, b_ref, du_ref, dw_ref, db_ref, buf, moved):
        i = pl.program_id(1)
        first = jnp.logical_and(pl.program_id(0) == 0, i == 0)
        buf[FFN_HALO:, :] = up_ref[...]
        buf[0:FFN_HALO, :] = jnp.where(i > 0, h_ref[...], 0.0)
        taps = _tap_sources(buf, moved, offsets, ts)
        for r0, c0, cw in _ffn_chunks(ts, f):
            vcols, gcols = slice(c0, c0 + cw), slice(f + c0, f + c0 + cw)
            uv = _taps_sum(taps, w_ref, jnp.broadcast_to(b_ref[:, vcols], (FFN_ROWS, cw)), r0, vcols)
            ug = _taps_sum(taps, w_ref, jnp.broadcast_to(b_ref[:, gcols], (FFN_ROWS, cw)), r0, gcols)
            d = d_ref[r0:r0 + FFN_ROWS, vcols].astype(F32)
            sg = _sigmoid(ug)
            du_ref[r0:r0 + FFN_ROWS, vcols] = d * ug * sg
            du_ref[r0:r0 + FFN_ROWS, gcols] = d * uv * sg * (1.0 + ug * (1.0 - sg))

        @pl.when(first)
        def _():
            dw_ref[...] = jnp.zeros_like(dw_ref)
            db_ref[...] = jnp.zeros_like(db_ref)

        du = du_ref[...]
        db_ref[...] += jnp.sum(du, axis=0, keepdims=True)
        for k, (src, off) in enumerate(taps):
            dw_ref[k:k + 1, :] += jnp.sum(du * src[pl.ds(off, ts), :], axis=0, keepdims=True)

    nst = dims.seq // ts
    n_moved = _moved_copies(offsets)
    full = lambda rows: pl.BlockSpec((rows, 2 * f), lambda b_, i: (0, 0))
    return pl.pallas_call(
        body, name=name, grid=(dims.batch_local, nst),
        in_specs=[pl.BlockSpec((ts, f), lambda b_, i: (b_ * nst + i, 0)),
                  *_seq_specs(dims, ts, 2 * f, FFN_HALO, 0), full(FFN_HALO), full(1)],
        out_specs=[pl.BlockSpec((ts, 2 * f), lambda b_, i: (b_ * nst + i, 0)), full(FFN_HALO), full(1)],
        out_shape=[jax.ShapeDtypeStruct((t, 2 * f), F32), jax.ShapeDtypeStruct((FFN_HALO, 2 * f), F32),
                   jax.ShapeDtypeStruct((1, 2 * f), F32)],
        scratch_shapes=[pltpu.VMEM((FFN_HALO + ts, 2 * f), F32), pltpu.VMEM((n_moved, ts, 2 * f), F32)],
        compiler_params=_params("arbitrary", "arbitrary"),
    )(dact, up, up, w, b)


def _ffn_conv_bwd(du, w, dims, *, name, ts=128):
    t, f2 = du.shape
    kw = dims.ffn_conv_width
    nst = dims.seq // ts

    offsets = [(kw - 1) - k for k in range(kw)]

    def body(d_ref, dn_ref, w_ref, o_ref, buf, moved):
        buf[0:ts, :] = d_ref[...]
        buf[ts:, :] = jnp.where(pl.program_id(1) < nst - 1, dn_ref[...], 0.0)
        taps = _tap_sources(buf, moved, offsets, ts)
        for r0, c0, cw in _ffn_chunks(ts, f2):
            cols = slice(c0, c0 + cw)
            o_ref[r0:r0 + FFN_ROWS, cols] = _taps_sum(taps, w_ref, jnp.zeros((FFN_ROWS, cw), F32), r0, cols).astype(BF16)

    return pl.pallas_call(
        body, name=name, grid=(dims.batch_local, nst),
        in_specs=[*_seq_specs(dims, ts, f2, FFN_HALO, 0, nxt=True), pl.BlockSpec((FFN_HALO, f2), lambda b_, i: (0, 0))],
        out_specs=pl.BlockSpec((ts, f2), lambda b_, i: (b_ * nst + i, 0)),
        out_shape=jax.ShapeDtypeStruct((t, f2), BF16),
        scratch_shapes=[pltpu.VMEM((ts + FFN_HALO, f2), F32), pltpu.VMEM((_moved_copies(offsets), ts, f2), F32)],
        compiler_params=_params("parallel", "parallel"),
    )(du, du, w)


def _alibi_slope(h, n_heads):
    return 2.0 ** (-8.0 * (h + 1) / n_heads)


def _dot_nt(a, b):
    return lax.dot_general(a, b, (((1,), (1,)), ((), ())), preferred_element_type=F32)


def _dot_tn(a, b):
    return lax.dot_general(a, b, (((0,), (0,)), ((), ())), preferred_element_type=F32)


def _attn_view(x, dims, dil):
    return x.reshape(dims.batch_local, dims.seq // dil, dil * x.shape[-1])


def _attn_fwd_group(q, k, v, state, dims, dil, *, last, name):
    t, a = q.shape
    assert 2 * dims.head_dim == 128 and dims.n_heads % 2 == 0
    blk, hd = ATTN_BLOCK, dims.head_dim
    nb = dims.seq // dil // blk
    has_prev = nb > 1
    nkeys = 2 * blk if has_prev else blk

    def body(*refs):
        it = iter(refs)
        q_ref, kc_ref, vc_ref = next(it), next(it), next(it)
        kp_ref, vp_ref = (next(it), next(it)) if has_prev else (None, None)
        m_in, l_in, acc_in = (next(it), next(it), next(it)) if state is not None else (None, None, None)
        outs = list(it)
        iq = lax.broadcasted_iota(jnp.int32, (blk, nkeys), 0)
        jk = lax.broadcasted_iota(jnp.int32, (blk, nkeys), 1)
        if has_prev:
            steps = iq + blk - jk
            valid = (steps >= 0) & (steps <= blk) & ((jk >= blk) | (pl.program_id(2) > 0))
        else:
            steps = iq - jk
            valid = steps >= 0
        dist = steps.astype(F32) * float(dil)
        low = lax.broadcasted_iota(jnp.int32, (blk, 2 * hd), 1) < hd
        for hp in range(dims.n_heads // 2):
            sl = slice(2 * hd * hp, 2 * hd * (hp + 1))
            q2 = q_ref[:, sl]
            if has_prev:
                kcat = jnp.concatenate([kp_ref[:, sl], kc_ref[:, sl]], axis=0)
                vcat = jnp.concatenate([vp_ref[:, sl], vc_ref[:, sl]], axis=0)
            else:
                kcat, vcat = kc_ref[:, sl], vc_ref[:, sl]
            halves = []
            for half in range(2):
                col = 2 * hd * hp + hd * half
                qh = jnp.where(low if half == 0 else jnp.logical_not(low), q2, jnp.zeros_like(q2))
                sc = _dot_nt(qh, kcat) - _alibi_slope(2 * hp + half, dims.n_heads) * dist
                sc = jnp.where(valid, sc, MASKED_SCORE)
                row_max = jnp.max(sc, axis=-1, keepdims=True)
                if state is None:
                    m_new = row_max
                    p = jnp.exp(sc - m_new)
                    alpha = None
                    l_new = jnp.sum(p, axis=-1, keepdims=True)
                else:
                    m_old = m_in[:, col:col + 1]
                    m_new = jnp.maximum(m_old, row_max)
                    p = jnp.exp(sc - m_new)
                    alpha = jnp.exp(m_old - m_new)
                    l_new = alpha * l_in[:, col:col + 1] + jnp.sum(p, axis=-1, keepdims=True)
                pv = jnp.dot(p.astype(BF16), vcat, preferred_element_type=F32)
                halves.append((m_new, l_new, alpha, pv))
            (m_a, l_a, al_a, pv_a), (m_b, l_b, al_b, pv_b) = halves
            if state is None:
                acc = jnp.where(low, pv_a, pv_b)
            else:
                old = acc_in[:, sl]
                acc = jnp.where(low, al_a * old + pv_a, al_b * old + pv_b)
            m2 = jnp.where(low, m_a, m_b)
            l2 = jnp.where(low, l_a, l_b)
            if last:
                outs[0][:, sl] = (acc / l2).astype(BF16)
                outs[1][:, sl] = m2 + jnp.log(l2)
            else:
                outs[0][:, sl] = m2
                outs[1][:, sl] = l2
                outs[2][:, sl] = acc

    cur = pl.BlockSpec((None, blk, a), lambda b, r, i: (b, i, r))
    prev = pl.BlockSpec((None, blk, a), lambda b, r, i: (b, jnp.maximum(i - 1, 0), r))
    args, in_specs = [q, k, v], [cur, cur, cur]
    if has_prev:
        args += [k, v]
        in_specs += [prev, prev]
    if state is not None:
        args += list(state)
        in_specs += [cur] * 3
    shape = lambda dt: jax.ShapeDtypeStruct((dims.batch_local, dims.seq // dil, dil * a), dt)
    out_shape = [shape(BF16), shape(F32)] if last else [shape(F32)] * 3
    outs = pl.pallas_call(
        body, name=name, grid=(dims.batch_local, dil, nb),
        in_specs=in_specs, out_specs=[cur] * len(out_shape), out_shape=out_shape,
        compiler_params=_params("parallel", "parallel", "parallel"),
    )(*[_attn_view(x, dims, dil) for x in args])
    return tuple(o.reshape(t, a) for o in outs)


def _attn_delta(do, o, head_ones, dims, *, name, tr=512):
    t, a = o.shape
    tr = _pick(t, tr, 8)

    def body(do_ref, o_ref, e_ref, d_ref):
        prod = do_ref[...].astype(F32) * o_ref[...].astype(F32)
        d_ref[...] = _head_mean(prod, e_ref, dims.head_dim) * float(dims.head_dim)

    return pl.pallas_call(
        body, name=name, grid=(t // tr,),
        in_specs=[_row_spec(tr, a), _row_spec(tr, a), pl.BlockSpec((a, a), lambda i: (0, 0))],
        out_specs=_row_spec(tr, a), out_shape=jax.ShapeDtypeStruct((t, a), F32),
        compiler_params=_params("parallel"),
    )(do, o, head_ones)


def _attn_bwd_group(q, k, v, do, lse, delta, dims, dil, *, name):
    t, a = q.shape
    blk, hd = ATTN_BLOCK, dims.head_dim
    nb = dims.seq // dil // blk
    has_next = nb > 1

    def body(*refs):
        k_ref, v_ref, q_ref, do_ref, lse_ref, dl_ref = refs[:6]
        if has_next:
            qn_ref, don_ref, lsen_ref, dln_ref = refs[6:10]
            dq_ref, dk_ref, dv_ref, carry = refs[10:]
        else:
            dq_ref, dk_ref, dv_ref = refs[6:]
        j = pl.program_id(2)
        iq = lax.broadcasted_iota(jnp.int32, (blk, blk), 0)
        jk = lax.broadcasted_iota(jnp.int32, (blk, blk), 1)
        low = lax.broadcasted_iota(jnp.int32, (blk, 2 * hd), 1) < hd

        def pair(hp, qr, dor, lser, dlr, steps, valid):
            sl = slice(2 * hd * hp, 2 * hd * (hp + 1))
            q2, do2, k2, v2 = qr[:, sl], dor[:, sl], k_ref[:, sl], v_ref[:, sl]
            dist = steps.astype(F32) * float(dil)
            dq_h, dk2, dv2 = [], None, None
            for half in range(2):
                col = 2 * hd * hp + hd * half
                mask = low if half == 0 else jnp.logical_not(low)
                qh = jnp.where(mask, q2, jnp.zeros_like(q2))
                doh = jnp.where(mask, do2, jnp.zeros_like(do2))
                sc = _dot_nt(qh, k2) - _alibi_slope(2 * hp + half, dims.n_heads) * dist
                p = jnp.where(valid, jnp.exp(sc - lser[:, col:col + 1]), 0.0)
                ds = p * (_dot_nt(doh, v2) - dlr[:, col:col + 1])
                ds_b, p_b = ds.astype(BF16), p.astype(BF16)
                dq_h.append(jnp.dot(ds_b, k2, preferred_element_type=F32))
                dk_h, dv_h = _dot_tn(ds_b, qh), _dot_tn(p_b, doh)
                dk2 = dk_h if dk2 is None else dk2 + dk_h
                dv2 = dv_h if dv2 is None else dv2 + dv_h
            return sl, jnp.where(low, dq_h[0], dq_h[1]), dk2, dv2

        if has_next:
            @pl.when(j == 0)
            def _():
                carry[...] = jnp.zeros_like(carry)

        for hp in range(dims.n_heads // 2):
            sl, dq2, dk2, dv2 = pair(hp, q_ref, do_ref, lse_ref, dl_ref, iq - jk, iq >= jk)
            dq_ref[:, sl] = (carry[:, sl] + dq2) if has_next else dq2
            dk_ref[:, sl] = dk2
            dv_ref[:, sl] = dv2

        if has_next:
            @pl.when(j + 1 < nb)
            def _():
                for hp in range(dims.n_heads // 2):
                    sl, dq2, dk2, dv2 = pair(hp, qn_ref, don_ref, lsen_ref, dln_ref, iq - jk + blk, jk >= iq)
                    carry[:, sl] = dq2
                    dk_ref[:, sl] += dk2
                    dv_ref[:, sl] += dv2

    cur = pl.BlockSpec((None, blk, a), lambda b, r, j: (b, j, r))
    nxt = pl.BlockSpec((None, blk, a), lambda b, r, j: (b, jnp.minimum(j + 1, nb - 1), r))
    args, in_specs = [k, v, q, do, lse, delta], [cur] * 6
    if has_next:
        args += [q, do, lse, delta]
        in_specs += [nxt] * 4
    shape = jax.ShapeDtypeStruct((dims.batch_local, dims.seq // dil, dil * a), F32)
    outs = pl.pallas_call(
        body, name=name, grid=(dims.batch_local, dil, nb),
        in_specs=in_specs, out_specs=[cur] * 3, out_shape=[shape] * 3,
        scratch_shapes=[pltpu.VMEM((blk, a), F32)] if has_next else [],
        compiler_params=_params("parallel", "parallel", "arbitrary"),
    )(*[_attn_view(x, dims, dil) for x in args])
    return tuple(o.reshape(t, a) for o in outs)


LANES = 128
MASK_BIAS = 1e30
RESIDUE_DILATIONS = tuple(d for d in DILATIONS if d > 1)


def _rows_to_residues(value, out_ref, scr, d):
    rows, width = value.shape
    for c in range(width // LANES):
        cols = slice(LANES * c, LANES * (c + 1))
        scr[c] = value[:, cols]
        for r in range(d):
            out_ref[r, :, cols] = scr[c, pl.ds(r, rows // d, stride=d), :].astype(out_ref.dtype)


def _residues_to_rows(in_ref, scr, d):
    _, n, width = in_ref.shape
    slabs = []
    for c in range(width // LANES):
        cols = slice(LANES * c, LANES * (c + 1))
        for r in range(d):
            scr[c, pl.ds(r, n, stride=d), :] = in_ref[r, :, cols].astype(F32)
        slabs.append(scr[c])
    return slabs[0] if len(slabs) == 1 else jnp.concatenate(slabs, axis=1)


def _residue_shape(dims, d, width, dtype):
    return jax.ShapeDtypeStruct((dims.batch_local, d, dims.seq // d, width), dtype)


def _residue_spec(dims, d, tr, width):
    tiles = dims.seq // tr
    return pl.BlockSpec((None, d, tr // d, width), lambda i: (i // tiles, 0, i % tiles, 0))


def _head_sum_matrix(dims):
    a = dims.n_heads * dims.head_dim
    head = jnp.arange(a, dtype=jnp.int32) // dims.head_dim
    return (head[:, None] == jnp.arange(LANES, dtype=jnp.int32)[None, :]).astype(BF16)


def _two_pass_dot(v, m):
    hi = v.astype(BF16)
    lo = (v - hi.astype(F32)).astype(BF16)
    return jnp.dot(hi, m, preferred_element_type=F32) + jnp.dot(lo, m, preferred_element_type=F32)


def _qkv_layouts_fwd(z, gq, gk, head_ones, dims, *, name, tr=256):
    t = z.shape[0]
    a = dims.n_heads * dims.head_dim
    q_scale = dims.head_dim ** -0.5
    nres = len(RESIDUE_DILATIONS)

    def body(q_ref, k_ref, v_ref, gq_ref, gk_ref, sum_ref, spread_ref, *rest):
        outs, scr = rest[:-1], rest[-1]
        qv, kv = q_ref[...].astype(F32), k_ref[...].astype(F32)
        mean = lambda val: _two_pass_dot(_two_pass_dot(val, sum_ref[...]), spread_ref[...]) * (1.0 / dims.head_dim)
        rq = lax.rsqrt(mean(qv * qv) + RMS_EPS)
        rk = lax.rsqrt(mean(kv * kv) + RMS_EPS)
        values = (qv * rq * gq_ref[...] * q_scale, kv * rk * gk_ref[...], v_ref[...].astype(F32))
        for j, val in enumerate(values):
            outs[j][...] = val.astype(BF16)
            for g, d in enumerate(RESIDUE_DILATIONS):
                _rows_to_residues(val, outs[3 * (g + 1) + j], scr, d)

    out_specs = [_row_spec(tr, a)] * 3
    out_shape = [jax.ShapeDtypeStruct((t, a), BF16)] * 3
    for d in RESIDUE_DILATIONS:
        out_specs += [_residue_spec(dims, d, tr, a)] * 3
        out_shape += [_residue_shape(dims, d, a, BF16)] * 3
    outs = pl.pallas_call(
        body, name=name, grid=(t // tr,),
        in_specs=[_row_spec(tr, a, 2), _row_spec(tr, a, 3), _row_spec(tr, a, 4), _vec_spec(a), _vec_spec(a),
                  pl.BlockSpec((a, LANES), lambda i: (0, 0)), pl.BlockSpec((LANES, a), lambda i: (0, 0))],
        out_specs=out_specs, out_shape=out_shape,
        scratch_shapes=[pltpu.VMEM((a // LANES, tr, LANES), F32)],
        compiler_params=_params("parallel"),
    )(z, z, z, gq, gk, *head_ones)
    return {d: tuple(outs[3 * g:3 * g + 3]) for g, d in enumerate((1,) + RESIDUE_DILATIONS)}


def _attn_specs(dims, dil, width):
    blk = ATTN_BLOCK
    nb = dims.seq // dil // blk
    if dil == 1:
        grid = (dims.batch_local, nb)
        at = lambda f: pl.BlockSpec((blk, width), lambda b, i: (b * nb + f(i), 0))
    else:
        grid = (dims.batch_local, dil, nb)
        at = lambda f: pl.BlockSpec((None, None, blk, width), lambda b, r, i: (b, r, f(i), 0))
    return grid, at(lambda i: i), at(lambda i: jnp.maximum(i - 1, 0)), at(lambda i: jnp.minimum(i + 1, nb - 1))


def _head_slopes(n_heads):
    h = lax.broadcasted_iota(jnp.int32, (n_heads, 1, 1), 0).astype(F32)
    return jnp.exp((h + 1.0) * (-8.0 / n_heads * math.log(2.0)))


def _pair_masks(hd):
    low = lax.broadcasted_iota(jnp.int32, (1, 2 * hd), 1) < hd
    return low, jnp.logical_not(low)


def _attn_fwd(q, k, v, dims, dil, *, name):
    a = dims.n_heads * dims.head_dim
    heads, hd, blk = dims.n_heads, dims.head_dim, ATTN_BLOCK
    assert 2 * hd == LANES and heads % 2 == 0 and heads <= LANES
    nb = dims.seq // dil // blk
    has_prev = nb > 1
    nkeys = 2 * blk if has_prev else blk
    grid, cur, prev, _ = _attn_specs(dims, dil, a)
    _, cur_stat, _, _ = _attn_specs(dims, dil, LANES)

    def body(*refs):
        if has_prev:
            q_ref, kc_ref, vc_ref, kp_ref, vp_ref, o_ref, lse_ref, s_scr, p_scr = refs
        else:
            q_ref, kc_ref, vc_ref, o_ref, lse_ref, s_scr, p_scr = refs
        low, high = _pair_masks(hd)

        def keys(cur_ref, prev_ref, sl):
            return jnp.concatenate([prev_ref[:, sl], cur_ref[:, sl]], axis=0) if has_prev else cur_ref[:, sl]

        for hp in range(heads // 2):
            sl = slice(LANES * hp, LANES * (hp + 1))
            q2 = q_ref[:, sl]
            kcat = keys(kc_ref, kp_ref if has_prev else None, sl)
            s_scr[2 * hp] = _dot_nt(jnp.where(low, q2, jnp.zeros_like(q2)), kcat)
            s_scr[2 * hp + 1] = _dot_nt(jnp.where(high, q2, jnp.zeros_like(q2)), kcat)

        iq = lax.broadcasted_iota(jnp.int32, (blk, nkeys), 0)
        jk = lax.broadcasted_iota(jnp.int32, (blk, nkeys), 1)
        if has_prev:
            steps = iq + blk - jk
            valid = (steps >= 0) & (steps <= blk) & ((jk >= blk) | (pl.program_id(len(grid) - 1) > 0))
        else:
            steps = iq - jk
            valid = steps >= 0
        bias = jnp.where(valid, steps.astype(F32) * (-float(dil)), -MASK_BIAS)
        s = s_scr[...] + _head_slopes(heads) * bias[None]
        m = jnp.max(s, axis=-1, keepdims=True)
        p = jnp.exp(s - m)
        l = jnp.sum(p, axis=-1, keepdims=True)
        p_scr[...] = p.astype(BF16)
        inv = 1.0 / l
        lse = m + jnp.log(l)

        lane = lax.broadcasted_iota(jnp.int32, (blk, LANES), 1)
        stat = jnp.zeros((blk, LANES), F32)
        for hp in range(heads // 2):
            sl = slice(LANES * hp, LANES * (hp + 1))
            vcat = keys(vc_ref, vp_ref if has_prev else None, sl)
            pv_a = jnp.dot(p_scr[2 * hp], vcat, preferred_element_type=F32) * inv[2 * hp]
            pv_b = jnp.dot(p_scr[2 * hp + 1], vcat, preferred_element_type=F32) * inv[2 * hp + 1]
            o_ref[:, sl] = jnp.where(low, pv_a, pv_b)
            stat = jnp.where(lane == 2 * hp, lse[2 * hp], stat)
            stat = jnp.where(lane == 2 * hp + 1, lse[2 * hp + 1], stat)
        lse_ref[...] = stat

    lead = q.shape[:-2]
    rows = q.shape[-2]
    o, lse = pl.pallas_call(
        body, name=name, grid=grid,
        in_specs=[cur, cur, cur] + ([prev, prev] if has_prev else []),
        out_specs=[cur, cur_stat],
        out_shape=[jax.ShapeDtypeStruct(lead + (rows, a), F32), jax.ShapeDtypeStruct(lead + (rows, LANES), F32)],
        scratch_shapes=[pltpu.VMEM((heads, blk, nkeys), F32), pltpu.VMEM((heads, blk, nkeys), BF16)],
        compiler_params=_params(*["parallel"] * len(grid)),
    )(q, k, v, *([k, v] if has_prev else []))
    return o, lse


def _attn_combine(groups, head_spread, dims, *, name, tr=256):
    t = dims.tokens
    a = dims.n_heads * dims.head_dim
    dils = tuple(groups)

    def body(*refs):
        ins = refs[:2 * len(dils)]
        x_ref = refs[2 * len(dils)]
        o_ref = refs[2 * len(dils) + 1]
        lse_refs = refs[2 * len(dils) + 2:-2]
        scr, scr_stat = refs[-2], refs[-1]
        outs, stats = [], []
        for g, d in enumerate(dils):
            if d == 1:
                outs.append(ins[2 * g][...])
                stats.append(ins[2 * g + 1][...])
            else:
                outs.append(_residues_to_rows(ins[2 * g], scr, d))
                stats.append(_residues_to_rows(ins[2 * g + 1], scr_stat, d))
        top = functools.reduce(jnp.maximum, stats)
        weights = [jnp.exp(s - top) for s in stats]
        total = functools.reduce(jnp.add, weights)
        joint = top + jnp.log(total)
        inv = 1.0 / total
        acc = None
        for w, o in zip(weights, outs):
            term = _two_pass_dot(w * inv, x_ref[...]) * o
            acc = term if acc is None else acc + term
        o_ref[...] = acc.astype(BF16)
        for g, d in enumerate(dils):
            if d == 1:
                lse_refs[g][...] = joint
            else:
                _rows_to_residues(joint, lse_refs[g], scr_stat, d)

    in_specs, args, lse_specs, lse_shapes = [], [], [], []
    for d in dils:
        if d == 1:
            in_specs += [_row_spec(tr, a), _row_spec(tr, LANES)]
            lse_specs.append(_row_spec(tr, LANES))
            lse_shapes.append(jax.ShapeDtypeStruct((t, LANES), F32))
        else:
            in_specs += [_residue_spec(dims, d, tr, a), _residue_spec(dims, d, tr, LANES)]
            lse_specs.append(_residue_spec(dims, d, tr, LANES))
            lse_shapes.append(_residue_shape(dims, d, LANES, F32))
        args += list(groups[d])
    outs = pl.pallas_call(
        body, name=name, grid=(t // tr,),
        in_specs=in_specs + [pl.BlockSpec((LANES, a), lambda i: (0, 0))],
        out_specs=[_row_spec(tr, a)] + lse_specs,
        out_shape=[jax.ShapeDtypeStruct((t, a), BF16)] + lse_shapes,
        scratch_shapes=[pltpu.VMEM((a // LANES, tr, LANES), F32), pltpu.VMEM((1, tr, LANES), F32)],
        compiler_params=_params("parallel"),
    )(*args, head_spread)
    return outs[0], dict(zip(dils, outs[1:]))


def _attn_bwd_prep(do, o, head_sum, dims, *, name, tr=256):
    t, a = o.shape

    def body(do_ref, o_ref, e_ref, *rest):
        outs, scr, scr_stat = rest[:-2], rest[-2], rest[-1]
        dov = do_ref[...].astype(F32)
        delta = _two_pass_dot(dov * o_ref[...].astype(F32), e_ref[...])
        outs[0][...] = delta
        for g, d in enumerate(RESIDUE_DILATIONS):
            _rows_to_residues(dov, outs[1 + 2 * g], scr, d)
            _rows_to_residues(delta, outs[2 + 2 * g], scr_stat, d)

    out_specs, out_shape = [_row_spec(tr, LANES)], [jax.ShapeDtypeStruct((t, LANES), F32)]
    for d in RESIDUE_DILATIONS:
        out_specs += [_residue_spec(dims, d, tr, a), _residue_spec(dims, d, tr, LANES)]
        out_shape += [_residue_shape(dims, d, a, BF16), _residue_shape(dims, d, LANES, F32)]
    outs = pl.pallas_call(
        body, name=name, grid=(t // tr,),
        in_specs=[_row_spec(tr, a), _row_spec(tr, a), pl.BlockSpec((a, LANES), lambda i: (0, 0))],
        out_specs=out_specs, out_shape=out_shape,
        scratch_shapes=[pltpu.VMEM((a // LANES, tr, LANES), F32), pltpu.VMEM((1, tr, LANES), F32)],
        compiler_params=_params("parallel"),
    )(do, o, head_sum)
    dos, deltas = {1: do}, {1: outs[0]}
    for g, d in enumerate(RESIDUE_DILATIONS):
        dos[d], deltas[d] = outs[1 + 2 * g], outs[2 + 2 * g]
    return dos, deltas


def _attn_bwd(q, k, v, do, lse, delta, dims, dil, *, name):
    a = dims.n_heads * dims.head_dim
    heads, hd, blk = dims.n_heads, dims.head_dim, ATTN_BLOCK
    nb = dims.seq // dil // blk
    has_next = nb > 1
    nq = 2 * blk if has_next else blk
    grid, cur, _, nxt = _attn_specs(dims, dil, a)
    _, cur_stat, _, nxt_stat = _attn_specs(dims, dil, LANES)

    def body(*refs):
        k_ref, v_ref, q_ref, do_ref, lse_ref, dl_ref = refs[:6]
        if has_next:
            qn_ref, don_ref, lsen_ref, dln_ref = refs[6:10]
            dq_ref, dk_ref, dv_ref, s_scr, dp_scr, p_scr, ds_scr, carry = refs[10:]
        else:
            dq_ref, dk_ref, dv_ref, s_scr, dp_scr, p_scr, ds_scr = refs[6:]
        j = pl.program_id(len(grid) - 1)
        low, high = _pair_masks(hd)

        def stacked(ref, nref, sl):
            return jnp.concatenate([ref[:, sl], nref[:, sl]], axis=0) if has_next else ref[:, sl]

        def halves(x):
            return jnp.where(low, x, jnp.zeros_like(x)), jnp.where(high, x, jnp.zeros_like(x))

        for hp in range(heads // 2):
            sl = slice(LANES * hp, LANES * (hp + 1))
            k2, v2 = k_ref[:, sl], v_ref[:, sl]
            q_a, q_b = halves(stacked(q_ref, qn_ref if has_next else None, sl))
            do_a, do_b = halves(stacked(do_ref, don_ref if has_next else None, sl))
            s_scr[2 * hp], s_scr[2 * hp + 1] = _dot_nt(q_a, k2), _dot_nt(q_b, k2)
            dp_scr[2 * hp], dp_scr[2 * hp + 1] = _dot_nt(do_a, v2), _dot_nt(do_b, v2)

        rq = lax.broadcasted_iota(jnp.int32, (nq, blk), 0)
        jk = lax.broadcasted_iota(jnp.int32, (nq, blk), 1)
        if has_next:
            iq = jnp.where(rq < blk, rq, rq - blk)
            steps = jnp.where(rq < blk, iq - jk, iq - jk + blk)
            valid = ((rq < blk) & (iq >= jk)) | ((rq >= blk) & (jk >= iq) & (j + 1 < nb))
        else:
            steps, valid = rq - jk, rq >= jk
        bias = jnp.where(valid, steps.astype(F32) * (-float(dil)), -MASK_BIAS)
        lse_all = stacked(lse_ref, lsen_ref if has_next else None, slice(None))
        dl_all = stacked(dl_ref, dln_ref if has_next else None, slice(None))
        lse3 = jnp.stack([lse_all[:, h:h + 1] for h in range(heads)])
        dl3 = jnp.stack([dl_all[:, h:h + 1] for h in range(heads)])
        p = jnp.exp(s_scr[...] + _head_slopes(heads) * bias[None] - lse3)
        p_scr[...] = p.astype(BF16)
        ds_scr[...] = (p * (dp_scr[...] - dl3)).astype(BF16)

        if has_next:
            @pl.when(j == 0)
            def _():
                carry[...] = jnp.zeros_like(carry)

        for hp in range(heads // 2):
            sl = slice(LANES * hp, LANES * (hp + 1))
            k2 = k_ref[:, sl]
            q_a, q_b = halves(stacked(q_ref, qn_ref if has_next else None, sl))
            do_a, do_b = halves(stacked(do_ref, don_ref if has_next else None, sl))
            ds_a, ds_b = ds_scr[2 * hp], ds_scr[2 * hp + 1]
            dq2 = jnp.where(low, jnp.dot(ds_a, k2, preferred_element_type=F32),
                            jnp.dot(ds_b, k2, preferred_element_type=F32))
            dk_ref[:, sl] = _dot_tn(ds_a, q_a) + _dot_tn(ds_b, q_b)
            dv_ref[:, sl] = _dot_tn(p_scr[2 * hp], do_a) + _dot_tn(p_scr[2 * hp + 1], do_b)
            if has_next:
                dq_ref[:, sl] = carry[:, sl] + dq2[:blk]
                carry[:, sl] = dq2[blk:]
            else:
                dq_ref[:, sl] = dq2

    args, in_specs = [k, v, q, do, lse, delta], [cur] * 4 + [cur_stat] * 2
    if has_next:
        args += [q, do, lse, delta]
        in_specs += [nxt] * 2 + [nxt_stat] * 2
    shape = jax.ShapeDtypeStruct(q.shape, F32)
    scratch = [pltpu.VMEM((heads, nq, blk), F32)] * 2 + [pltpu.VMEM((heads, nq, blk), BF16)] * 2
    if has_next:
        scratch.append(pltpu.VMEM((blk, a), F32))
    return pl.pallas_call(
        body, name=name, grid=grid, in_specs=in_specs, out_specs=[cur] * 3, out_shape=[shape] * 3,
        scratch_shapes=scratch,
        compiler_params=_params(*["parallel"] * (len(grid) - 1), "arbitrary"),
    )(*args)


def _qkv_layouts_bwd(z, grads, gq, gk, head_ones, dims, *, name, tr=256):
    t = z.shape[0]
    a = dims.n_heads * dims.head_dim
    q_scale = dims.head_dim ** -0.5
    dils = tuple(grads)

    def body(q_ref, k_ref, *rest):
        d_refs = rest[:3 * len(dils)]
        gq_ref, gk_ref, sum_ref, spread_ref, dz_ref, dgq_ref, dgk_ref, scr = rest[3 * len(dils):]
        first = pl.program_id(0) == 0
        mean = lambda val: _two_pass_dot(_two_pass_dot(val, sum_ref[...]), spread_ref[...]) * (1.0 / dims.head_dim)

        def total(j):
            acc = None
            for g, d in enumerate(dils):
                ref = d_refs[3 * g + j]
                part = ref[...] if d == 1 else _residues_to_rows(ref, scr, d)
                acc = part if acc is None else acc + part
            return acc

        def norm_bwd(x_ref, dy, g_ref, scale, col, dg_ref):
            xv = x_ref[...].astype(F32)
            dy = dy * scale
            r = lax.rsqrt(mean(xv * xv) + RMS_EPS)
            gy = dy * g_ref[...]
            dx = r * gy - xv * (r * r * r) * mean(xv * gy)
            dz_ref[:, col * a:(col + 1) * a] = dx.astype(BF16)
            _accumulate(dg_ref, jnp.sum(dy * xv * r, axis=0, keepdims=True), first)

        norm_bwd(q_ref, total(0), gq_ref, q_scale, 0, dgq_ref)
        norm_bwd(k_ref, total(1), gk_ref, 1.0, 1, dgk_ref)
        dz_ref[:, 2 * a:3 * a] = total(2).astype(BF16)

    in_specs, args = [_row_spec(tr, a, 2), _row_spec(tr, a, 3)], [z, z]
    for d in dils:
        in_specs += [_row_spec(tr, a) if d == 1 else _residue_spec(dims, d, tr, a)] * 3
        args += list(grads[d])
    in_specs += [_vec_spec(a), _vec_spec(a), pl.BlockSpec((a, LANES), lambda i: (0, 0)),
                 pl.BlockSpec((LANES, a), lambda i: (0, 0))]
    return pl.pallas_call(
        body, name=name, grid=(t // tr,), in_specs=in_specs,
        out_specs=[_row_spec(tr, 3 * a), _vec_spec(a), _vec_spec(a)],
        out_shape=[jax.ShapeDtypeStruct((t, 3 * a), BF16)] + [jax.ShapeDtypeStruct((1, a), F32)] * 2,
        scratch_shapes=[pltpu.VMEM((a // LANES, tr, LANES), F32)],
        compiler_params=_params("arbitrary"),
    )(*args, gq, gk, *head_ones)


def _mix_fwd(ya, yb, z, gate_b, dims, *, name, tr=512):
    t, d = ya.shape
    tr = _pick(t, tr, 8)
    first_gate_col = z.shape[1] // d - 2

    def body(ya_ref, yb_ref, ga_ref, gb_ref, ba_ref, bb_ref, o_ref):
        g_a = _sigmoid(ga_ref[...].astype(F32) + ba_ref[...])
        g_b = _sigmoid(gb_ref[...].astype(F32) + bb_ref[...])
        o_ref[...] = (g_a * ya_ref[...] + g_b * yb_ref[...]).astype(BF16)

    return pl.pallas_call(
        body, name=name, grid=(t // tr,),
        in_specs=[_row_spec(tr, d), _row_spec(tr, d), _row_spec(tr, d, first_gate_col),
                  _row_spec(tr, d, first_gate_col + 1), _vec_spec(d, 0), _vec_spec(d, 1)],
        out_specs=_row_spec(tr, d), out_shape=jax.ShapeDtypeStruct((t, d), BF16),
        compiler_params=_params("parallel"),
    )(ya, yb, z, z, gate_b, gate_b)


def _mix_bwd(dmix, ya, yb, z, gate_b, dims, *, name, tr=512):
    t, d = ya.shape
    tr = _pick(t, tr, 8)
    first_gate_col = z.shape[1] // d - 2

    def body(dm_ref, ya_ref, yb_ref, ga_ref, gb_ref, ba_ref, bb_ref, dya_ref, dyb_ref, dz_ref, db_ref):
        dm = dm_ref[...].astype(F32)
        g_a = _sigmoid(ga_ref[...].astype(F32) + ba_ref[...])
        g_b = _sigmoid(gb_ref[...].astype(F32) + bb_ref[...])
        dya_ref[...] = (dm * g_a).astype(BF16)
        dyb_ref[...] = (dm * g_b).astype(BF16)
        dl_a = dm * ya_ref[...] * g_a * (1.0 - g_a)
        dl_b = dm * yb_ref[...] * g_b * (1.0 - g_b)
        dz_ref[:, 0:d] = dl_a.astype(BF16)
        dz_ref[:, d:2 * d] = dl_b.astype(BF16)
        first = pl.program_id(0) == 0
        sums = jnp.concatenate([jnp.sum(dl_a, axis=0, keepdims=True), jnp.sum(dl_b, axis=0, keepdims=True)], axis=1)
        _accumulate(db_ref, sums, first)

    return pl.pallas_call(
        body, name=name, grid=(t // tr,),
        in_specs=[_row_spec(tr, d), _row_spec(tr, d), _row_spec(tr, d), _row_spec(tr, d, first_gate_col),
                  _row_spec(tr, d, first_gate_col + 1), _vec_spec(d, 0), _vec_spec(d, 1)],
        out_specs=[_row_spec(tr, d), _row_spec(tr, d), _row_spec(tr, 2 * d), _vec_spec(2 * d)],
        out_shape=[jax.ShapeDtypeStruct((t, d), BF16)] * 2 + [jax.ShapeDtypeStruct((t, 2 * d), BF16),
                                                              jax.ShapeDtypeStruct((1, 2 * d), F32)],
        compiler_params=_params("arbitrary"),
    )(dmix, ya, yb, z, z, gate_b, gate_b)


def _loss_head(y, target, *, name, tr=512):
    t, d = y.shape
    tr = _pick(t, tr, 8)

    def body(y_ref, t_ref, dy_ref, dyb_ref, loss_ref):
        err = y_ref[...] - t_ref[...]
        dy = err * (1.0 / d)
        dy_ref[...] = dy
        dyb_ref[...] = dy.astype(BF16)
        part = jnp.sum(jnp.sum(err * err, axis=-1, keepdims=True), axis=0, keepdims=True) * (0.5 / d)
        _accumulate(loss_ref, jnp.broadcast_to(part, (8, 128)), pl.program_id(0) == 0)

    return pl.pallas_call(
        body, name=name, grid=(t // tr,),
        in_specs=[_row_spec(tr, d), _row_spec(tr, d)],
        out_specs=[_row_spec(tr, d), _row_spec(tr, d), pl.BlockSpec((8, 128), lambda i: (0, 0))],
        out_shape=[jax.ShapeDtypeStruct((t, d), F32), jax.ShapeDtypeStruct((t, d), BF16),
                   jax.ShapeDtypeStruct((8, 128), F32)],
        compiler_params=_params("arbitrary"),
    )(y, target)


def _adamw(w, grads, m, v, *, name, tr=256):
    r, c = w.shape
    tr = _pick(r, tr, 8)
    ng = len(grads)
    c1 = 1.0 - ADAM_B1 ** ADAM_STEP
    c2 = 1.0 - ADAM_B2 ** ADAM_STEP

    def body(*refs):
        w_ref, g_refs, m_ref, v_ref = refs[0], refs[1:1 + ng], refs[1 + ng], refs[2 + ng]
        g_out, d_out, m_out, v_out = refs[3 + ng:]
        g = g_refs[0][...]
        for extra in g_refs[1:]:
            g = g + extra[...]
        m_new = ADAM_B1 * m_ref[...] + (1.0 - ADAM_B1) * g
        v_new = ADAM_B2 * v_ref[...] + (1.0 - ADAM_B2) * (g * g)
        g_out[...] = g
        m_out[...] = m_new
        v_out[...] = v_new
        d_out[...] = -ADAM_LR * ((m_new / c1) / (jnp.sqrt(v_new / c2) + ADAM_EPS) + ADAM_WD * w_ref[...])

    spec = pl.BlockSpec((tr, c), lambda i: (i, 0))
    return pl.pallas_call(
        body, name=name, grid=(r // tr,),
        in_specs=[spec] * (3 + ng), out_specs=[spec] * 4, out_shape=[jax.ShapeDtypeStruct((r, c), F32)] * 4,
        compiler_params=_params("parallel"),
    )(w, *grads, m, v)


CHIP_PEERS = ((1, 0), (0, 1), (1, 1))


def _place():
    return lax.axis_index("x"), lax.axis_index("y"), lax.axis_index("c")


HBM = pl.BlockSpec(memory_space=pltpu.HBM)
SEM = pl.BlockSpec(memory_space=pltpu.SEMAPHORE)
IN_FLIGHT = pltpu.SideEffectType.DATAFLOW_SIDE_EFFECTING


def _in_hbm(a):
    return pltpu.with_memory_space_constraint(a, pltpu.HBM)


def _cast_to_lands(shards, dtypes, *, name, after=None):
    n = len(shards)

    def body(*refs):
        ins, outs, bufs, sems = refs[:n], refs[n:2 * n], refs[2 * n:3 * n], refs[3 * n]
        x, y, _ = _place()
        copies = []
        for a in range(n):
            bufs[a][...] = ins[a][...].astype(dtypes[a])
            cp = pltpu.make_async_copy(bufs[a], outs[a].at[2 * x + y], sems.at[a])
            cp.start()
            copies.append(cp)
        for cp in copies:
            cp.wait()

    body, more_specs, more_args = _ordered(body, n, after)
    return pl.pallas_call(
        body, name=name, in_specs=[pl.BlockSpec(memory_space=pltpu.VMEM)] * n + more_specs, out_specs=[ANY] * n,
        out_shape=[jax.ShapeDtypeStruct((N_CHIPS,) + s.shape, dt) for s, dt in zip(shards, dtypes)],
        scratch_shapes=[pltpu.VMEM(s.shape, dt) for s, dt in zip(shards, dtypes)] + [pltpu.SemaphoreType.DMA((n,))],
        compiler_params=pltpu.CompilerParams(vmem_limit_bytes=V7X_VMEM_LIMIT_BYTES),
    )(*shards, *more_args)


def _chip_copy(src, dst, send, recv, flip, place):
    x, y, c = place
    return pltpu.make_async_remote_copy(src_ref=src, dst_ref=dst, send_sem=send, recv_sem=recv,
                                        device_id=(x ^ flip[0], y ^ flip[1], c), device_id_type=MESH)


def _my_part(land, place, halved):
    block = land.at[2 * place[0] + place[1]]
    if not halved:
        return block
    rows = land.shape[1] // 2
    return block.at[pl.ds(pl.multiple_of(place[2] * rows, rows), rows)]


def _gather_start(lands, after, *, name, halved=()):
    n = len(lands)

    def body(*refs):
        ins, send, recv, token = refs[:n], refs[n + 1], refs[n + 2], refs[-1]
        place = _place()
        for a in range(n):
            part = _my_part(ins[a], place, a in halved)
            for p, flip in enumerate(CHIP_PEERS):
                k = 3 * a + p
                _chip_copy(part, part, send.at[k], recv.at[k], flip, place).start()
        token[...] = jnp.zeros_like(token)

    outs = pl.pallas_call(
        body, name=name, in_specs=[HBM] * n + [ANY],
        out_specs=(SEM, SEM, *[HBM] * n, pl.BlockSpec(memory_space=pltpu.VMEM)),
        out_shape=(pltpu.SemaphoreType.DMA((3 * n,)), pltpu.SemaphoreType.DMA((3 * n,)),
                   *[pltpu.HBM(l.shape, l.dtype) for l in lands], jax.ShapeDtypeStruct((8, 128), F32)),
        input_output_aliases={a: 2 + a for a in range(n)},
        compiler_params=pltpu.CompilerParams(has_side_effects=IN_FLIGHT),
    )(*[_in_hbm(l) for l in lands], after)
    return outs[0], outs[1], list(outs[2:2 + n]), outs[-1]


def _gather_wait(send, recv, lands, after, *, name, halved=()):
    n = len(lands)

    def body(*refs):
        ins, send_ref, recv_ref = refs[:n], refs[n], refs[n + 1]
        place = _place()
        for a in range(n):
            part = _my_part(ins[a], place, a in halved)
            for p, flip in enumerate(CHIP_PEERS):
                k = 3 * a + p
                cp = _chip_copy(part, part, send_ref.at[k], recv_ref.at[k], flip, place)
                cp.wait_send()
                cp.wait_recv()

    after = list(after) if isinstance(after, (list, tuple)) else [after]
    return pl.pallas_call(
        body, name=name, in_specs=[HBM] * n + [SEM, SEM] + [ANY] * len(after), out_specs=[HBM] * n,
        out_shape=[pltpu.HBM(l.shape, l.dtype) for l in lands],
        input_output_aliases={a: a for a in range(n)},
        compiler_params=pltpu.CompilerParams(has_side_effects=IN_FLIGHT),
    )(*lands, send, recv, *after)


def _forward_to_sibling(land, *, name):
    rows = land.shape[1] // 2

    def body(land_ref, out_ref, send, recv):
        x, y, c = _place()
        copies = []
        for p, (fx, fy) in enumerate(CHIP_PEERS):
            chip = 2 * (x ^ fx) + (y ^ fy)
            mine = pl.ds(pl.multiple_of(c * rows, rows), rows)
            theirs = pl.ds(pl.multiple_of((1 - c) * rows, rows), rows)
            out = pltpu.make_async_remote_copy(
                src_ref=land_ref.at[chip].at[mine], dst_ref=out_ref.at[chip].at[mine], send_sem=send.at[p],
                recv_sem=recv.at[p], device_id=(x, y, 1 - c), device_id_type=MESH)
            out.start()
            copies.append((out, pltpu.make_async_remote_copy(
                src_ref=land_ref.at[chip].at[theirs], dst_ref=out_ref.at[chip].at[theirs], send_sem=send.at[p],
                recv_sem=recv.at[p], device_id=(x, y, 1 - c), device_id_type=MESH)))
        for out, arriving in copies:
            out.wait_send()
            arriving.wait_recv()

    return pl.pallas_call(
        body, name=name, in_specs=[ANY], out_specs=ANY, out_shape=jax.ShapeDtypeStruct(land.shape, land.dtype),
        input_output_aliases={0: 0},
        scratch_shapes=[pltpu.SemaphoreType.DMA((3,)), pltpu.SemaphoreType.DMA((3,))],
    )(land)


def _scatter_start(grad, *, name):
    def body(g_ref, land_ref, send, recv, g_thru, land_thru, token):
        place = _place()
        for p, flip in enumerate(CHIP_PEERS):
            peer_chip = 2 * (place[0] ^ flip[0]) + (place[1] ^ flip[1])
            _chip_copy(g_ref.at[peer_chip], land_ref.at[p], send.at[p], recv.at[p], flip, place).start()
        token[...] = jnp.zeros_like(token)

    land = lax.empty((3,) + grad.shape[1:], grad.dtype)
    return pl.pallas_call(
        body, name=name, in_specs=[HBM, HBM],
        out_specs=(SEM, SEM, HBM, HBM, pl.BlockSpec(memory_space=pltpu.VMEM)),
        out_shape=(pltpu.SemaphoreType.DMA((3,)), pltpu.SemaphoreType.DMA((3,)), pltpu.HBM(grad.shape, grad.dtype),
                   pltpu.HBM(land.shape, land.dtype), jax.ShapeDtypeStruct((8, 128), F32)),
        input_output_aliases={0: 2, 1: 3},
        compiler_params=pltpu.CompilerParams(has_side_effects=IN_FLIGHT),
    )(_in_hbm(grad), _in_hbm(land))


def _scatter_wait(started, after, *, name):
    n = len(started)

    def body(*refs):
        grads, lands = refs[:n], refs[n:2 * n]
        sends, recvs = refs[2 * n:3 * n], refs[3 * n:4 * n]
        place = _place()
        for a in range(n):
            for p, flip in enumerate(CHIP_PEERS):
                cp = _chip_copy(grads[a].at[0], lands[a].at[p], sends[a].at[p], recvs[a].at[p], flip, place)
                cp.wait_send()
                cp.wait_recv()

    grads, lands = [s[2] for s in started], [s[3] for s in started]
    after = list(after) if isinstance(after, (list, tuple)) else [after]
    outs = pl.pallas_call(
        body, name=name, in_specs=[HBM] * (2 * n) + [SEM] * (2 * n) + [ANY] * len(after), out_specs=[HBM] * (2 * n),
        out_shape=[pltpu.HBM(a.shape, a.dtype) for a in grads + lands],
        input_output_aliases={a: a for a in range(2 * n)},
        compiler_params=pltpu.CompilerParams(has_side_effects=IN_FLIGHT),
    )(*grads, *lands, *[s[0] for s in started], *[s[1] for s in started], *after)
    return list(zip(outs[:n], outs[n:]))


def _sibling_copy(src, dst, send, recv, place):
    x, y, c = place
    return pltpu.make_async_remote_copy(src_ref=src, dst_ref=dst, send_sem=send, recv_sem=recv,
                                        device_id=(x, y, 1 - c), device_id_type=MESH)


def _swap_start(arrays, *, name):
    n = len(arrays)

    def body(*refs):
        ins, lands, send, recv, token = refs[:n], refs[n:2 * n], refs[2 * n], refs[2 * n + 1], refs[-1]
        place = _place()
        for a in range(n):
            _sibling_copy(ins[a], lands[a], send.at[a], recv.at[a], place).start()
        token[...] = jnp.zeros_like(token)

    both = [_in_hbm(a) for a in arrays] + [_in_hbm(lax.empty(a.shape, a.dtype)) for a in arrays]
    outs = pl.pallas_call(
        body, name=name, in_specs=[HBM] * (2 * n),
        out_specs=(SEM, SEM, *[HBM] * (2 * n), pl.BlockSpec(memory_space=pltpu.VMEM)),
        out_shape=(pltpu.SemaphoreType.DMA((n,)), pltpu.SemaphoreType.DMA((n,)),
                   *[pltpu.HBM(a.shape, a.dtype) for a in both], jax.ShapeDtypeStruct((8, 128), F32)),
        input_output_aliases={a: 2 + a for a in range(2 * n)},
        compiler_params=pltpu.CompilerParams(has_side_effects=IN_FLIGHT),
    )(*both)
    return outs[0], outs[1], list(outs[2:2 + n]), list(outs[2 + n:2 + 2 * n]), outs[-1]


def _swap_wait(started, after, *, name):
    send, recv, arrays, lands = started[:4]
    n = len(arrays)

    def body(*refs):
        ins, zones, send_ref, recv_ref = refs[:n], refs[n:2 * n], refs[2 * n], refs[2 * n + 1]
        place = _place()
        for a in range(n):
            cp = _sibling_copy(ins[a], zones[a], send_ref.at[a], recv_ref.at[a], place)
            cp.wait_send()
            cp.wait_recv()

    after = list(after) if isinstance(after, (list, tuple)) else [after]
    outs = pl.pallas_call(
        body, name=name, in_specs=[HBM] * (2 * n) + [SEM, SEM] + [ANY] * len(after), out_specs=[HBM] * (2 * n),
        out_shape=[pltpu.HBM(a.shape, a.dtype) for a in arrays + lands],
        input_output_aliases={a: a for a in range(2 * n)},
        compiler_params=pltpu.CompilerParams(has_side_effects=IN_FLIGHT),
    )(*arrays, *lands, send, recv, *after)
    return list(outs[:n]), list(outs[n:])


def _allreduce_start(packed, *, name):
    n_dev = 8

    def body(src_ref, land_ref, send, recv, src_thru, land_thru, token):
        x, y, c = _place()
        me = 4 * x + 2 * y + c
        for p in range(1, n_dev):
            pltpu.make_async_remote_copy(
                src_ref=src_ref, dst_ref=land_ref.at[me], send_sem=send.at[p - 1], recv_sem=recv.at[p - 1],
                device_id=(x ^ (p >> 2), y ^ ((p >> 1) & 1), c ^ (p & 1)), device_id_type=MESH).start()
        token[...] = jnp.zeros_like(token)

    land = lax.empty((n_dev,) + packed.shape, packed.dtype)
    return pl.pallas_call(
        body, name=name, in_specs=[HBM, HBM],
        out_specs=(SEM, SEM, HBM, HBM, pl.BlockSpec(memory_space=pltpu.VMEM)),
        out_shape=(pltpu.SemaphoreType.DMA((n_dev - 1,)), pltpu.SemaphoreType.DMA((n_dev - 1,)),
                   pltpu.HBM(packed.shape, packed.dtype), pltpu.HBM(land.shape, land.dtype),
                   jax.ShapeDtypeStruct((8, 128), F32)),
        input_output_aliases={0: 2, 1: 3},
        compiler_params=pltpu.CompilerParams(has_side_effects=IN_FLIGHT),
    )(_in_hbm(packed), _in_hbm(land))


def _allreduce_wait(started, after, *, name):
    send, recv, packed, land = started[:4]
    n_dev = 8

    def body(src_ref, land_ref, send_ref, recv_ref, *_):
        x, y, c = _place()
        for p in range(1, n_dev):
            cp = pltpu.make_async_remote_copy(
                src_ref=src_ref, dst_ref=land_ref.at[0], send_sem=send_ref.at[p - 1], recv_sem=recv_ref.at[p - 1],
                device_id=(x ^ (p >> 2), y ^ ((p >> 1) & 1), c ^ (p & 1)), device_id_type=MESH)
            cp.wait_send()
            cp.wait_recv()

    after = list(after) if isinstance(after, (list, tuple)) else [after]
    return pl.pallas_call(
        body, name=name, in_specs=[HBM, HBM, SEM, SEM] + [ANY] * len(after), out_specs=[HBM, HBM],
        out_shape=[pltpu.HBM(packed.shape, packed.dtype), pltpu.HBM(land.shape, land.dtype)],
        input_output_aliases={0: 0, 1: 1},
        compiler_params=pltpu.CompilerParams(has_side_effects=IN_FLIGHT),
    )(packed, land, send, recv, *after)


def _sum_devices(mine, land, *, name):
    n_dev = land.shape[0]

    def body(mine_ref, land_ref, out_ref):
        x, y, c = _place()
        me = 4 * x + 2 * y + c
        total = None
        for s in range(n_dev):
            part = jnp.where(me == s, mine_ref[...], land_ref[s])
            total = part if total is None else total + part
        out_ref[...] = total

    return pl.pallas_call(body, name=name, out_shape=jax.ShapeDtypeStruct(mine.shape, mine.dtype))(mine, land)


def _sum_received(grad, land, *, name, tr=256):
    _, r, c = grad.shape
    tr = _pick(r, tr, 8)

    def body(chip_ref, g_ref, l_ref, o_ref):
        o_ref[...] = ((g_ref[...] + l_ref[0].astype(F32)) + l_ref[1].astype(F32)) + l_ref[2].astype(F32)

    chip = (2 * lax.axis_index("x") + lax.axis_index("y")).astype(jnp.int32).reshape(1)
    return pl.pallas_call(
        body, name=name,
        grid_spec=pltpu.PrefetchScalarGridSpec(
            num_scalar_prefetch=1, grid=(r // tr,),
            in_specs=[pl.BlockSpec((None, tr, c), lambda i, chip_ref: (chip_ref[0], i, 0)),
                      pl.BlockSpec((3, tr, c), lambda i, chip_ref: (0, i, 0))],
            out_specs=pl.BlockSpec((tr, c), lambda i, chip_ref: (i, 0))),
        out_shape=jax.ShapeDtypeStruct((r, c), F32), compiler_params=_params("parallel"),
    )(chip, grad, land)


def _allreduce_small(packed, *, name, after=None):
    r, d = packed.shape
    n_dev = 8

    def body(src_ref, out_ref, buf, send, recv):
        x, y, c = _place()
        me = 4 * x + 2 * y + c
        started = []
        for p in range(1, n_dev):
            rc = pltpu.make_async_remote_copy(
                src_ref=src_ref, dst_ref=buf.at[me], send_sem=send.at[p - 1], recv_sem=recv.at[p - 1],
                device_id=(x ^ (p >> 2), y ^ ((p >> 1) & 1), c ^ (p & 1)), device_id_type=MESH)
            rc.start()
            started.append(rc)
        buf[me] = src_ref[...]
        for rc in started:
            rc.wait()
        total = buf[0]
        for s in range(1, n_dev):
            total = total + buf[s]
        out_ref[...] = total

    vmem = pl.BlockSpec(memory_space=pltpu.VMEM)
    body, more_specs, more_args = _ordered(body, 1, after)
    return pl.pallas_call(
        body, name=name, in_specs=[vmem] + more_specs, out_specs=vmem, out_shape=jax.ShapeDtypeStruct((r, d), F32),
        scratch_shapes=[pltpu.VMEM((n_dev, r, d), F32), pltpu.SemaphoreType.DMA((n_dev - 1,)),
                        pltpu.SemaphoreType.DMA((n_dev - 1,))],
    )(packed, *more_args)


def _packed_rows(size, d):
    return -(-size // (8 * d)) * 8


def _pack_rows(arrays, d):
    rows = []
    for arr in arrays:
        flat = arr.reshape(-1).astype(F32)
        n = _packed_rows(flat.shape[0], d)
        rows.append(jnp.pad(flat, (0, n * d - flat.shape[0])).reshape(n, d))
    return jnp.concatenate(rows, axis=0)


def _unpack_rows(packed, shapes, d):
    out, row = [], 0
    for shape in shapes:
        size = math.prod(shape)
        n = _packed_rows(size, d)
        out.append(packed[row:row + n].reshape(-1)[:size].reshape(shape))
        row += n
    return out


SMALL = ("norm1_g", "gate_b", "conv_b", "conv_norm_g", "q_norm_g", "k_norm_g", "norm2_g", "ffn_conv_b")
LARGE = ("w_in", "w_conv_out", "w_attn_out", "w_out", "w_up", "w_down")
WEIGHTS = ("norm1_g", "w_in", "gate_b", "conv_w", "conv_b", "conv_norm_g", "w_conv_out", "q_norm_g", "k_norm_g",
           "w_attn_out", "w_out", "norm2_g", "w_up", "ffn_conv_w", "ffn_conv_b", "w_down")


def _head_ones(dims):
    a = dims.n_heads * dims.head_dim
    head = jnp.arange(a, dtype=jnp.int32) // dims.head_dim
    return (head[:, None] == head[None, :]).astype(BF16)


def _after(vec, token):
    return vec if token is None else vec + token[0:1, 0:1]


def _local_step(dims, x, target, small, first_weights, other_weights, send_grad):
    d, f, heads = dims.d_model, dims.d_ff, dims.n_heads
    small = dict(small)
    row = lambda name: small[name].reshape(1, -1)
    head_sum = _head_sum_matrix(dims)
    head_spread = jnp.transpose(head_sum)
    ones = (head_sum, head_spread)
    gq = jnp.tile(row("q_norm_g"), (1, heads))
    gk = jnp.tile(row("k_norm_g"), (1, heads))
    one_shard = lambda w: w.reshape(1, -1, w.shape[-1])

    h = _rmsnorm_fwd(x, row("norm1_g"), name="norm1")
    full = first_weights(h)
    w_in = full["w_in"]
    conv_w = jnp.pad(full["conv_w"], ((0, CONV_HALO - dims.conv_width), (0, 0)))
    ffn_w = jnp.pad(full["ffn_conv_w"], ((0, FFN_HALO - dims.ffn_conv_width), (0, 0)))
    z = _mm_nn(h, w_in, out_dtype=BF16, after=full.get("token"), tm=2048, tn=1792, name="in_proj")
    a1, a3 = _conv_branch_fwd(z, conv_w, row("conv_b"), row("conv_norm_g"), dims, name="conv_branch")
    qkv = _qkv_layouts_fwd(z, gq, gk, ones, dims, name="qk_norm")
    per_group = {dil: _attn_fwd(*qkv[dil], dims, dil, name=f"attn_fwd_d{dil}") for dil in DILATIONS}
    o, lse = _attn_combine(per_group, head_spread, dims, name="attn_combine")
    full = other_weights(o)
    w_up = full["w_up"]
    w_co, w_ao, w_o, w_dn = (one_shard(full[k]) for k in ("w_conv_out", "w_attn_out", "w_out", "w_down"))
    ya = _mm_nn(a3, w_co, out_dtype=F32, name="conv_out_proj")
    yb = _mm_nn(o, w_ao, out_dtype=F32, name="attn_out_proj")
    mixed = _mix_fwd(ya, yb, z, row("gate_b"), dims, name="gate_mix")
    x1, h2 = _proj_residual_norm(mixed, w_o, x, row("norm2_g"), name="out_proj_norm2")
    up = _mm_nn(h2, w_up, out_dtype=F32, tm=2048, name="up_proj")
    act = _ffn_act_fwd(up, ffn_w, row("ffn_conv_b"), dims, name="ffn_act")
    dy, dy_b, loss = _proj_residual_loss(act, w_dn, x1, target, tm=512, name="down_proj_loss")

    grads = {}

    def large(name, g):
        grads[name], g_bf16 = g
        return send_grad(name, g_bf16)

    sent = large("w_down", _mm_tn(act, dy_b, n_shards=1, name="dw_down"))
    dact = _mm_nt(dy_b, w_dn, out_dtype=BF16, after=sent, name="d_act")
    dup, dfw, dfb = _ffn_bwd(dact, up, ffn_w, row("ffn_conv_b"), dims, name="ffn_bwd")
    grads["ffn_conv_w"], grads["ffn_conv_b"] = dfw[:dims.ffn_conv_width], dfb
    sent = large("w_up", _mm_tn(h2, dup, n_shards=N_CHIPS, name="dw_up"))
    dh2 = _mm_nt(dup, w_up, out_dtype=F32, after=sent, name="d_h2")
    dx1, dx1_b, grads["norm2_g"] = _rmsnorm_bwd(x1, row("norm2_g"), dh2, dy, want_bf16=True, name="norm2_bwd")
    sent = large("w_out", _mm_tn(mixed, dx1_b, n_shards=1, name="dw_out"))
    dmix = _mm_nt(dx1_b, w_o, out_dtype=F32, after=sent, name="d_mix")
    dya, dyb, dz_gate, grads["gate_b"] = _mix_bwd(dmix, ya, yb, z, row("gate_b"), dims, name="gate_mix_bwd")
    sent = large("w_attn_out", _mm_tn(o, dyb, n_shards=1, name="dw_attn_out"))
    do = _mm_nt(dyb, w_ao, out_dtype=BF16, after=sent, name="d_attn")
    dos, deltas = _attn_bwd_prep(do, o, head_sum, dims, name="attn_bwd_prep")
    dqkv = {dil: _attn_bwd(*qkv[dil], dos[dil], lse[dil], deltas[dil], dims, dil, name=f"attn_bwd_d{dil}")
            for dil in DILATIONS}
    dz_qkv, dgq, dgk = _qkv_layouts_bwd(z, dqkv, gq, gk, ones, dims, name="qk_norm_bwd")
    grads["q_norm_g"] = dgq.reshape(heads, dims.head_dim).sum(axis=0)
    grads["k_norm_g"] = dgk.reshape(heads, dims.head_dim).sum(axis=0)
    sent = large("w_conv_out", _mm_tn(a3, dya, n_shards=1, name="dw_conv_out"))
    da3 = _mm_nt(dya, w_co, out_dtype=F32, after=sent, name="d_conv_act")
    da1, grads["conv_norm_g"] = _conv_norm_bwd(da3, a1, row("conv_norm_g"), name="conv_norm_bwd")
    dz, dcw, grads["conv_b"] = _conv_branch_bwd(da1, z, conv_w, [dz_qkv, dz_gate], dims, name="conv_branch_bwd")
    grads["conv_w"] = dcw[:dims.conv_width]
    sent = large("w_in", _mm_tn(h, dz, n_shards=N_CHIPS, name="dw_in"))
    dh = _mm_nt(dz, w_in, out_dtype=F32, after=sent, name="d_h")
    dx, grads["norm1_g"] = _rmsnorm_bwd(x, row("norm1_g"), dh, dx1, want_bf16=False, name="norm1_bwd")
    return loss, dx, grads


def _step(dims, x, target, w, m, v):
    d = dims.d_model
    t = dims.tokens
    sq = lambda a: a.reshape(a.shape[1:])
    w2, m2, v2 = ({k: sq(a) for k, a in grp.items()} for grp in (w, m, v))

    conv_pad = jnp.pad(w2["conv_w"], ((0, CONV_HALO - dims.conv_width), (0, 0)))
    ffn_pad = jnp.pad(w2["ffn_conv_w"], ((0, FFN_HALO - dims.ffn_conv_width), (0, 0)))
    first_names = ("w_in", "conv_w", "ffn_conv_w")
    other_names = tuple(k for k in LARGE if k not in first_names)
    lands = dict(zip(first_names, _cast_to_lands([w2["w_in"], conv_pad, ffn_pad], [BF16, F32, F32], name="cast_first")))
    first = _gather_start([lands[k] for k in first_names], x, halved=(0,), name="gather_start_first")
    lands.update(zip(other_names, _cast_to_lands([w2[k] for k in other_names], [BF16] * len(other_names),
                                                 after=first[3], name="cast_other")))
    other = []
    cols = lambda g, rows: jnp.moveaxis(g, 0, 1).reshape(g.shape[1], -1)[:rows]

    def first_weights(after):
        got = dict(zip(first_names, _gather_wait(*first[:3], [after] + [lands[k] for k in other_names], halved=(0,),
                                                 name="gather_wait_first")))
        got["w_in"] = _forward_to_sibling(got["w_in"], name="forward_w_in")
        other.extend(_gather_start([lands[k] for k in other_names], got["w_in"], name="gather_start_other"))
        got["conv_w"] = cols(got["conv_w"], dims.conv_width)
        got["ffn_conv_w"] = cols(got["ffn_conv_w"], dims.ffn_conv_width)
        got["token"] = other[3]
        return got

    def other_weights(after):
        return dict(zip(other_names, _gather_wait(*other[:3], after, name="gather_wait_other")))

    started = {}

    def send_grad(name, g):
        send, recv, g_thru, land, token = _scatter_start(g.reshape(N_CHIPS, -1, g.shape[-1]), name=f"scatter_start_{name}")
        started[name] = (send, recv, g_thru, land)
        return token

    small = {k: w2[k] for k in SMALL}
    small["norm1_g"] = _after(small["norm1_g"].reshape(1, -1), first[3])
    loss, dx, grads = _local_step(dims, x.reshape(t, d), target.reshape(t, d), small, first_weights, other_weights, send_grad)

    def my_sums(names, after, tag):
        arrived = _scatter_wait([started[k] for k in names], after, name=f"scatter_wait_{tag}")
        blocks = [grads[k].reshape(N_CHIPS, -1, grads[k].shape[-1]) for k in names]
        return [_sum_received(g, land, name=f"sum_{k}") for k, g, (_, land) in zip(names, blocks, arrived)]

    def updates(names, mine, theirs):
        return {k: _adamw(w2[k], [a, b], m2[k], v2[k], name=f"adamw_{k}") for k, a, b in zip(names, mine, theirs)}

    small_names = SMALL + ("conv_w", "ffn_conv_w")
    packed = _pack_rows([grads[k] for k in small_names] + [loss[0, 0]], d)
    reducing = _allreduce_start(packed, name="allreduce_start")
    others = [k for k in LARGE if k != "w_in"]
    mine_others = my_sums(others, [dx, reducing[4]], "others")
    swapping_others = _swap_start(mine_others, name="swap_start_others")
    mine_w_in = my_sums(["w_in"], swapping_others[4], "w_in")
    swapping_w_in = _swap_start(mine_w_in, name="swap_start_w_in")
    out = updates(others, *_swap_wait(swapping_others, swapping_w_in[4], name="swap_wait_others"))
    last_updates = [out[k][1] for k in others]
    reduced = _sum_devices(*_allreduce_wait(reducing, last_updates, name="allreduce_wait"), name="allreduce_sum")
    shapes = [grads[k].shape for k in small_names] + [()]
    *small_g, loss_total = _unpack_rows(reduced, shapes, d)
    small_g = dict(zip(small_names, small_g))
    chip = 2 * lax.axis_index("x") + lax.axis_index("y")
    for k in ("conv_w", "ffn_conv_w"):
        width = w2[k].shape[1]
        small_g[k] = lax.dynamic_slice_in_dim(small_g[k], chip * width, width, axis=1)

    small_shapes = [w2[k].shape for k in small_names]
    pack = lambda grp: _pack_rows([grp[k] for k in small_names], d)
    results = _adamw(pack(w2), [pack(small_g)], pack(m2), pack(v2), name="adamw_small")
    unpacked = [_unpack_rows(r, small_shapes, d) for r in results]
    out.update({k: tuple(u[i] for u in unpacked) for i, k in enumerate(small_names)})
    out.update(updates(["w_in"], *_swap_wait(swapping_w_in, results[1], name="swap_wait_w_in")))

    lead =lambda a: a.reshape((1,) + a.shape)
    ordered = [[lead(out[k][j].reshape(w2[k].shape)) for k in WEIGHTS] for j in range(4)]
    return (loss_total, dx.reshape(x.shape), *ordered[0], *ordered[1], *ordered[2], *ordered[3])


def kernel(x, norm1_g, w_in, gate_b, conv_w, conv_b, conv_norm_g, w_conv_out, q_norm_g, k_norm_g, w_attn_out, w_out, norm2_g, w_up, ffn_conv_w, ffn_conv_b, w_down, loss_target, m_norm1_g, m_w_in, m_gate_b, m_conv_w, m_conv_b, m_conv_norm_g, m_w_conv_out, m_q_norm_g, m_k_norm_g, m_w_attn_out, m_w_out, m_norm2_g, m_w_up, m_ffn_conv_w, m_ffn_conv_b, m_w_down, v_norm1_g, v_w_in, v_gate_b, v_conv_w, v_conv_b, v_conv_norm_g, v_w_conv_out, v_q_norm_g, v_k_norm_g, v_w_attn_out, v_w_out, v_norm2_g, v_w_up, v_ffn_conv_w, v_ffn_conv_b, v_w_down):
    w = dict(zip(WEIGHTS, (norm1_g, w_in, gate_b, conv_w, conv_b, conv_norm_g, w_conv_out, q_norm_g, k_norm_g,
                           w_attn_out, w_out, norm2_g, w_up, ffn_conv_w, ffn_conv_b, w_down)))
    m = dict(zip(WEIGHTS, (m_norm1_g, m_w_in, m_gate_b, m_conv_w, m_conv_b, m_conv_norm_g, m_w_conv_out, m_q_norm_g,
                           m_k_norm_g, m_w_attn_out, m_w_out, m_norm2_g, m_w_up, m_ffn_conv_w, m_ffn_conv_b, m_w_down)))
    v = dict(zip(WEIGHTS, (v_norm1_g, v_w_in, v_gate_b, v_conv_w, v_conv_b, v_conv_norm_g, v_w_conv_out, v_q_norm_g,
                           v_k_norm_g, v_w_attn_out, v_w_out, v_norm2_g, v_w_up, v_ffn_conv_w, v_ffn_conv_b, v_w_down)))
    dims = Dims(d_model=x.shape[-1], batch_local=x.shape[0], seq=x.shape[1], d_ff=w_down.shape[1] * N_CHIPS)
    return _step(dims, x, loss_target, w, m, v)
```

```python
import functools
import math
from typing import NamedTuple

import jax
import jax.numpy as jnp
from jax import lax
from jax.experimental import pallas as pl
from jax.experimental.pallas import tpu as pltpu

F32 = jnp.float32
BF16 = jnp.bfloat16

RMS_EPS = 1e-6
MASKED_SCORE = -1e30
ATTN_BLOCK = 128
DILATIONS = (1, 4, 16)
CONV_HALO = 32
FFN_HALO = 8
ADAM_LR, ADAM_B1, ADAM_B2, ADAM_EPS, ADAM_WD, ADAM_STEP = 0.001, 0.9, 0.999, 1e-08, 0.01, 10
V7X_VMEM_LIMIT_BYTES = 56 * 2 ** 20
N_CHIPS = 4
MESH = pl.DeviceIdType.MESH


class Dims(NamedTuple):
    d_model: int = 1024
    n_heads: int = 16
    head_dim: int = 64
    d_ff: int = 2816
    seq: int = 2048
    batch_local: int = 2
    conv_width: int = 31
    ffn_conv_width: int = 3

    @property
    def tokens(self):
        return self.seq * self.batch_local


def _params(*semantics):
    return pltpu.CompilerParams(dimension_semantics=semantics, vmem_limit_bytes=V7X_VMEM_LIMIT_BYTES)


ANY = pl.BlockSpec(memory_space=pl.ANY)


def _ordered(body, n_inputs, after):
    after = [] if after is None else list(after) if isinstance(after, (list, tuple)) else [after]
    if not after:
        return body, [], []

    def wrapped(*refs):
        return body(*refs[:n_inputs], *refs[n_inputs + len(after):])

    return wrapped, [ANY] * len(after), after


def _pick(n, target, mult=128):
    if n <= target:
        return n
    best = None
    for t in range(mult, target + 1, mult):
        if n % t == 0:
            best = t
    assert best is not None, (n, target, mult)
    return best


def _sigmoid(v):
    return 1.0 / (1.0 + jnp.exp(-v))


def _mm_nn(a, w, *, out_dtype, name, residual=None, after=None, tm=1024, tn=1408, tk=2816):
    m, k = a.shape
    nsh, k2, c = w.shape
    assert k == k2 and a.dtype == BF16 and w.dtype == BF16
    n = nsh * c
    tm, tn, tk = _pick(m, tm, 8), _pick(c, tn), _pick(k, tk)
    nk, cpn = k // tk, c // tn

    def body(*refs):
        if residual is None:
            a_ref, w_ref, o_ref, acc = refs
        else:
            a_ref, w_ref, r_ref, o_ref, acc = refs
        prod = jnp.dot(a_ref[...], w_ref[...], preferred_element_type=F32)

        def finish(total):
            if residual is not None:
                total = total + r_ref[...]
            o_ref[...] = total.astype(out_dtype)

        if nk == 1:
            finish(prod)
        else:
            kk = pl.program_id(2)

            @pl.when(kk == 0)
            def _():
                acc[...] = prod

            @pl.when(kk > 0)
            def _():
                acc[...] += prod

            @pl.when(kk == nk - 1)
            def _():
                finish(acc[...])

    in_specs = [pl.BlockSpec((tm, tk), lambda i, j, kk: (i, kk)),
                pl.BlockSpec((None, tk, tn), lambda i, j, kk: (j // cpn, kk, j % cpn))]
    args = [a, w]
    if residual is not None:
        in_specs.append(pl.BlockSpec((tm, tn), lambda i, j, kk: (i, j)))
        args.append(residual)
    body, more_specs, more_args = _ordered(body, len(args), after)
    return pl.pallas_call(
        body, name=name, grid=(m // tm, n // tn, nk),
        in_specs=in_specs + more_specs, out_specs=pl.BlockSpec((tm, tn), lambda i, j, kk: (i, j)),
        out_shape=jax.ShapeDtypeStruct((m, n), out_dtype),
        scratch_shapes=[pltpu.VMEM((tm, tn) if nk > 1 else (8, 128), F32)],
        compiler_params=_params("parallel", "parallel", "arbitrary"),
    )(*args, *more_args)


def _proj_residual_norm(a, w, residual, g, *, name, tm=1024):
    m, k = a.shape
    _, k2, n = w.shape
    assert w.shape[0] == 1 and k == k2 and a.dtype == BF16 and w.dtype == BF16
    tm = _pick(m, tm, 8)

    def body(a_ref, w_ref, r_ref, g_ref, y_ref, h_ref):
        y = r_ref[...] + jnp.dot(a_ref[...], w_ref[...], preferred_element_type=F32)
        y_ref[...] = y
        h_ref[...] = (y * lax.rsqrt(jnp.mean(y * y, axis=-1, keepdims=True) + RMS_EPS) * g_ref[...]).astype(BF16)

    rows = lambda width: pl.BlockSpec((tm, width), lambda i: (i, 0))
    return pl.pallas_call(
        body, name=name, grid=(m // tm,),
        in_specs=[rows(k), pl.BlockSpec((None, k, n), lambda i: (0, 0, 0)), rows(n), pl.BlockSpec((1, n), lambda i: (0, 0))],
        out_specs=[rows(n), rows(n)],
        out_shape=[jax.ShapeDtypeStruct((m, n), F32), jax.ShapeDtypeStruct((m, n), BF16)],
        compiler_params=_params("parallel"),
    )(a, w, residual, g)


def _proj_residual_loss(a, w, residual, target, *, name, tm=1024):
    m, k = a.shape
    _, k2, n = w.shape
    assert w.shape[0] == 1 and k == k2 and a.dtype == BF16 and w.dtype == BF16
    tm = _pick(m, tm, 8)

    def body(a_ref, w_ref, r_ref, t_ref, dy_ref, dyb_ref, loss_ref):
        err = r_ref[...] + jnp.dot(a_ref[...], w_ref[...], preferred_element_type=F32) - t_ref[...]
        dy = err * (1.0 / n)
        dy_ref[...] = dy
        dyb_ref[...] = dy.astype(BF16)
        part = jnp.sum(jnp.sum(err * err, axis=-1, keepdims=True), axis=0, keepdims=True) * (0.5 / n)
        _accumulate(loss_ref, jnp.broadcast_to(part, (8, 128)), pl.program_id(0) == 0)

    rows = lambda width: pl.BlockSpec((tm, width), lambda i: (i, 0))
    return pl.pallas_call(
        body, name=name, grid=(m // tm,),
        in_specs=[rows(k), pl.BlockSpec((None, k, n), lambda i: (0, 0, 0)), rows(n), rows(n)],
        out_specs=[rows(n), rows(n), pl.BlockSpec((8, 128), lambda i: (0, 0))],
        out_shape=[jax.ShapeDtypeStruct((m, n), F32), jax.ShapeDtypeStruct((m, n), BF16),
                   jax.ShapeDtypeStruct((8, 128), F32)],
        compiler_params=_params("arbitrary"),
    )(a, w, residual, target)


def _mm_nt(a, w, *, out_dtype, name, after=None, tm=1024, tn=1408, tk=1792):
    m, k = a.shape
    nsh, r, c = w.shape
    assert k == nsh * c and a.dtype == BF16 and w.dtype == BF16
    tm, tn, tk = _pick(m, tm, 8), _pick(r, tn), _pick(c, tk)
    nk, cpk = k // tk, c // tk

    def body(a_ref, w_ref, o_ref, acc):
        prod = lax.dot_general(a_ref[...], w_ref[...], (((1,), (1,)), ((), ())), preferred_element_type=F32)
        if nk == 1:
            o_ref[...] = prod.astype(out_dtype)
        else:
            kk = pl.program_id(2)

            @pl.when(kk == 0)
            def _():
                acc[...] = prod

            @pl.when(kk > 0)
            def _():
                acc[...] += prod

            @pl.when(kk == nk - 1)
            def _():
                o_ref[...] = acc[...].astype(out_dtype)

    body, more_specs, more_args = _ordered(body, 2, after)
    return pl.pallas_call(
        body, name=name, grid=(m // tm, r // tn, nk),
        in_specs=[pl.BlockSpec((tm, tk), lambda i, j, kk: (i, kk)),
                  pl.BlockSpec((None, tn, tk), lambda i, j, kk: (kk // cpk, j, kk % cpk))] + more_specs,
        out_specs=pl.BlockSpec((tm, tn), lambda i, j, kk: (i, j)),
        out_shape=jax.ShapeDtypeStruct((m, r), out_dtype),
        scratch_shapes=[pltpu.VMEM((tm, tn) if nk > 1 else (8, 128), F32)],
        compiler_params=_params("parallel", "parallel", "arbitrary"),
    )(a, w, *more_args)


MM_TN_VMEM_BYTES = 44 * 2 ** 20


def _mm_tn(a, b, *, n_shards, name, tm=1408, tn=1408):
    t, m = a.shape
    t2, n = b.shape
    assert t == t2 and a.dtype == BF16 and b.dtype == BF16
    c = n // n_shards
    tm, tn = _pick(m, tm), _pick(c, tn)
    if m // tm == 1 and n // tn == 1 and tn % (2 * LANES) == 0:
        tn //= 2
    fixed = 2 * tm * tn * 6
    if 4 * t * (tm + tn) + fixed <= MM_TN_VMEM_BYTES:
        tk = t
    else:
        tk = _pick(t, (MM_TN_VMEM_BYTES - fixed - 4 * tm * tn) // (4 * (tm + tn)), 8)
    nk, cpn = t // tk, c // tn

    def body(a_ref, b_ref, o_ref, ob_ref, acc):
        kk = pl.program_id(2)
        prod = lax.dot_general(a_ref[...], b_ref[...], (((0,), (0,)), ((), ())), preferred_element_type=F32)

        def finish(total):
            o_ref[...] = total
            ob_ref[...] = total.astype(BF16)

        if nk == 1:
            finish(prod)
        else:
            @pl.when(kk == 0)
            def _():
                acc[...] = prod

            @pl.when(kk > 0)
            def _():
                acc[...] += prod

            @pl.when(kk == nk - 1)
            def _():
                finish(acc[...])

    out_spec = pl.BlockSpec((None, tm, tn), lambda i, j, kk: (j // cpn, i, j % cpn))
    return pl.pallas_call(
        body, name=name, grid=(m // tm, n // tn, nk),
        in_specs=[pl.BlockSpec((tk, tm), lambda i, j, kk: (kk, i)),
                  pl.BlockSpec((tk, tn), lambda i, j, kk: (kk, j))],
        out_specs=[out_spec, out_spec],
        out_shape=[jax.ShapeDtypeStruct((n_shards, m, c), F32), jax.ShapeDtypeStruct((n_shards, m, c), BF16)],
        scratch_shapes=[pltpu.VMEM((tm, tn) if nk > 1 else (8, 128), F32)],
        compiler_params=_params("parallel", "parallel", "arbitrary"),
    )(a, b)


def _row_spec(tr, width, col=0):
    return pl.BlockSpec((tr, width), lambda i, col=col: (i, col))


def _vec_spec(width, col=0):
    return pl.BlockSpec((1, width), lambda i, col=col: (0, col))


def _accumulate(ref, value, first):
    @pl.when(first)
    def _():
        ref[...] = value

    @pl.when(jnp.logical_not(first))
    def _():
        ref[...] += value


def _rmsnorm_fwd(x, g, *, name, tr=512):
    t, d = x.shape
    tr = _pick(t, tr, 8)

    def body(x_ref, g_ref, o_ref):
        xv = x_ref[...]
        r = lax.rsqrt(jnp.mean(xv * xv, axis=-1, keepdims=True) + RMS_EPS)
        o_ref[...] = (xv * r * g_ref[...]).astype(BF16)

    return pl.pallas_call(
        body, name=name, grid=(t // tr,),
        in_specs=[_row_spec(tr, d), _vec_spec(d)], out_specs=_row_spec(tr, d),
        out_shape=jax.ShapeDtypeStruct((t, d), BF16), compiler_params=_params("parallel"),
    )(x, g)


def _rmsnorm_bwd(x, g, dy, dres, *, name, want_bf16, tr=512):
    t, d = x.shape
    tr = _pick(t, tr, 8)

    def body(x_ref, g_ref, dy_ref, dres_ref, *outs):
        dx_ref, dg_ref = outs[0], outs[-1]
        xv, dyv = x_ref[...], dy_ref[...].astype(F32)
        r = lax.rsqrt(jnp.mean(xv * xv, axis=-1, keepdims=True) + RMS_EPS)
        gy = dyv * g_ref[...]
        dx = dres_ref[...] + r * gy - xv * (r * r * r) * jnp.mean(xv * gy, axis=-1, keepdims=True)
        dx_ref[...] = dx
        if want_bf16:
            outs[1][...] = dx.astype(BF16)
        _accumulate(dg_ref, jnp.sum(dyv * xv * r, axis=0, keepdims=True), pl.program_id(0) == 0)

    out_shape = [jax.ShapeDtypeStruct((t, d), F32)]
    out_specs = [_row_spec(tr, d)]
    if want_bf16:
        out_shape.append(jax.ShapeDtypeStruct((t, d), BF16))
        out_specs.append(_row_spec(tr, d))
    out_shape.append(jax.ShapeDtypeStruct((1, d), F32))
    out_specs.append(_vec_spec(d))
    return pl.pallas_call(
        body, name=name, grid=(t // tr,),
        in_specs=[_row_spec(tr, d), _vec_spec(d), _row_spec(tr, d), _row_spec(tr, d)],
        out_specs=out_specs, out_shape=out_shape, compiler_params=_params("arbitrary"),
    )(x, g, dy, dres)


def _head_mean(v, ones_ref, head_dim):
    hi = v.astype(BF16)
    lo = (v - hi.astype(F32)).astype(BF16)
    e = ones_ref[...]
    total = jnp.dot(hi, e, preferred_element_type=F32) + jnp.dot(lo, e, preferred_element_type=F32)
    return total * (1.0 / head_dim)


def _qkv_fwd(z, gq, gk, head_ones, dims, *, name, tr=256):
    t = z.shape[0]
    a = dims.n_heads * dims.head_dim
    tr = _pick(t, tr, 8)
    q_scale = dims.head_dim ** -0.5

    def body(q_ref, k_ref, v_ref, gq_ref, gk_ref, e_ref, qo_ref, ko_ref, vo_ref):
        qv, kv = q_ref[...], k_ref[...]
        rq = lax.rsqrt(_head_mean(qv * qv, e_ref, dims.head_dim) + RMS_EPS)
        rk = lax.rsqrt(_head_mean(kv * kv, e_ref, dims.head_dim) + RMS_EPS)
        qo_ref[...] = (qv * rq * gq_ref[...] * q_scale).astype(BF16)
        ko_ref[...] = (kv * rk * gk_ref[...]).astype(BF16)
        vo_ref[...] = v_ref[...].astype(BF16)

    return pl.pallas_call(
        body, name=name, grid=(t // tr,),
        in_specs=[_row_spec(tr, a, 2), _row_spec(tr, a, 3), _row_spec(tr, a, 4), _vec_spec(a), _vec_spec(a),
                  pl.BlockSpec((a, a), lambda i: (0, 0))],
        out_specs=[_row_spec(tr, a)] * 3, out_shape=[jax.ShapeDtypeStruct((t, a), BF16)] * 3,
        compiler_params=_params("parallel"),
    )(z, z, z, gq, gk, head_ones)


def _qkv_bwd(z, dqs, dks, dvs, gq, gk, head_ones, dims, *, name, tr=256):
    t = z.shape[0]
    a = dims.n_heads * dims.head_dim
    tr = _pick(t, tr, 8)
    q_scale = dims.head_dim ** -0.5
    ng = len(dqs)

    def body(*refs):
        q_ref, k_ref = refs[:2]
        dq_refs, dk_refs, dv_refs = refs[2:2 + ng], refs[2 + ng:2 + 2 * ng], refs[2 + 2 * ng:2 + 3 * ng]
        gq_ref, gk_ref, e_ref = refs[2 + 3 * ng:5 + 3 * ng]
        dz_ref, dgq_ref, dgk_ref = refs[5 + 3 * ng:]
        first = pl.program_id(0) == 0

        def norm_bwd(x_ref, d_refs, g_ref, scale, col, dg_ref):
            xv = x_ref[...]
            dy = sum(r[...] for r in d_refs) * scale
            r = lax.rsqrt(_head_mean(xv * xv, e_ref, dims.head_dim) + RMS_EPS)
            gy = dy * g_ref[...]
            dx = r * gy - xv * (r * r * r) * _head_mean(xv * gy, e_ref, dims.head_dim)
            dz_ref[:, col * a:(col + 1) * a] = dx.astype(BF16)
            _accumulate(dg_ref, jnp.sum(dy * xv * r, axis=0, keepdims=True), first)

        norm_bwd(q_ref, dq_refs, gq_ref, q_scale, 0, dgq_ref)
        norm_bwd(k_ref, dk_refs, gk_ref, 1.0, 1, dgk_ref)
        dz_ref[:, 2 * a:3 * a] = sum(r[...] for r in dv_refs).astype(BF16)

    in_specs = ([_row_spec(tr, a, 2), _row_spec(tr, a, 3)] + [_row_spec(tr, a)] * (3 * ng)
                + [_vec_spec(a), _vec_spec(a), pl.BlockSpec((a, a), lambda i: (0, 0))])
    return pl.pallas_call(
        body, name=name, grid=(t // tr,), in_specs=in_specs,
        out_specs=[_row_spec(tr, 3 * a), _vec_spec(a), _vec_spec(a)],
        out_shape=[jax.ShapeDtypeStruct((t, 3 * a), BF16)] + [jax.ShapeDtypeStruct((1, a), F32)] * 2,
        compiler_params=_params("arbitrary"),
    )(z, z, *dqs, *dks, *dvs, gq, gk, head_ones)


CONV_ROWS = 16


def _seq_specs(dims, ts, width, halo, col, *, nxt=False):
    nst, per = dims.seq // ts, ts // halo
    last = dims.tokens // halo - 1
    cur = pl.BlockSpec((ts, width), lambda b, i: (b * nst + i, col))
    if nxt:
        edge = pl.BlockSpec((halo, width), lambda b, i: (jnp.minimum((b * nst + i + 1) * per, last), col))
    else:
        edge = pl.BlockSpec((halo, width), lambda b, i: (jnp.maximum((b * nst + i) * per - 1, 0), col))
    return cur, edge


SUBLANES = 8


def _shifted_copies(buf, shifted):
    rows = shifted.shape[1]
    for s in range(1, SUBLANES):
        shifted[s - 1] = buf[pl.ds(s, rows), :]


def _window(buf, shifted, start, size):
    a, s = divmod(start, SUBLANES)
    src = buf if s == 0 else shifted.at[s - 1]
    return src[pl.ds(SUBLANES * a, size), :]


def _conv_branch_fwd(z, w, b, g, dims, *, name, ts=128):
    t, c, kw = z.shape[0], dims.d_model, dims.conv_width
    base = CONV_HALO - (kw - 1)

    def body(av_ref, hv_ref, ag_ref, hg_ref, w_ref, b_ref, g_ref, a1_ref, a3_ref, buf, shifted):
        i = pl.program_id(1)
        buf[CONV_HALO:, :] = av_ref[...].astype(F32) * _sigmoid(ag_ref[...].astype(F32))
        buf[0:CONV_HALO, :] = jnp.where(i > 0, hv_ref[...].astype(F32) * _sigmoid(hg_ref[...].astype(F32)), 0.0)
        _shifted_copies(buf, shifted)
        for r0 in range(0, ts, CONV_ROWS):
            acc = jnp.broadcast_to(b_ref[...], (CONV_ROWS, c))
            for k in range(kw):
                acc = acc + w_ref[k:k + 1, :] * _window(buf, shifted, r0 + base + k, CONV_ROWS)
            a1_ref[r0:r0 + CONV_ROWS, :] = acc
            a2 = acc * lax.rsqrt(jnp.mean(acc * acc, axis=-1, keepdims=True) + RMS_EPS) * g_ref[...]
            a3_ref[r0:r0 + CONV_ROWS, :] = (a2 * _sigmoid(a2)).astype(BF16)

    vec = pl.BlockSpec((1, c), lambda b, i: (0, 0))
    out = pl.BlockSpec((ts, c), lambda b, i: (b * (dims.seq // ts) + i, 0))
    return pl.pallas_call(
        body, name=name, grid=(dims.batch_local, dims.seq // ts),
        in_specs=[*_seq_specs(dims, ts, c, CONV_HALO, 0), *_seq_specs(dims, ts, c, CONV_HALO, 1),
                  pl.BlockSpec((CONV_HALO, c), lambda b, i: (0, 0)), vec, vec],
        out_specs=[out, out],
        out_shape=[jax.ShapeDtypeStruct((t, c), F32), jax.ShapeDtypeStruct((t, c), BF16)],
        scratch_shapes=[pltpu.VMEM((CONV_HALO + ts, c), F32),
                        pltpu.VMEM((SUBLANES - 1, CONV_HALO + ts - SUBLANES, c), F32)],
        compiler_params=_params("parallel", "parallel"),
    )(z, z, z, z, w, b, g)


def _conv_norm_bwd(da3, a1, g, *, name, tr=256):
    t, c = a1.shape
    tr = _pick(t, tr, 8)

    def body(d_ref, a_ref, g_ref, o_ref, dg_ref):
        a1v, gv = a_ref[...], g_ref[...]
        r = lax.rsqrt(jnp.mean(a1v * a1v, axis=-1, keepdims=True) + RMS_EPS)
        a2 = a1v * r * gv
        sg = _sigmoid(a2)
        da2 = d_ref[...].astype(F32) * sg * (1.0 + a2 * (1.0 - sg))
        gy = da2 * gv
        o_ref[...] = r * gy - a1v * (r * r * r) * jnp.mean(a1v * gy, axis=-1, keepdims=True)
        _accumulate(dg_ref, jnp.sum(da2 * a1v * r, axis=0, keepdims=True), pl.program_id(0) == 0)

    return pl.pallas_call(
        body, name=name, grid=(t // tr,),
        in_specs=[_row_spec(tr, c), _row_spec(tr, c), _vec_spec(c)],
        out_specs=[_row_spec(tr, c), _vec_spec(c)],
        out_shape=[jax.ShapeDtypeStruct((t, c), F32), jax.ShapeDtypeStruct((1, c), F32)],
        compiler_params=_params("arbitrary"),
    )(da3, a1, g)


def _conv_branch_bwd(da1, z, w, rest_of_dz, dims, *, name, ts=128):
    t, c, kw = z.shape[0], dims.d_model, dims.conv_width
    nst = dims.seq // ts
    base = CONV_HALO - (kw - 1)
    n_rest = len(rest_of_dz)
    total = 2 * c + sum(r.shape[1] for r in rest_of_dz)

    def body(d_ref, dn_ref, av_ref, hv_ref, ag_ref, hg_ref, w_ref, *more):
        rest_refs = more[:n_rest]
        dz_ref, dw_ref, db_ref, abuf, dbuf, ashift, dshift = more[n_rest:]
        col = 2 * c
        for r in rest_refs:
            dz_ref[:, col:col + r.shape[1]] = r[...]
            col += r.shape[1]
        i = pl.program_id(1)
        first = jnp.logical_and(pl.program_id(0) == 0, i == 0)
        abuf[CONV_HALO:, :] = av_ref[...].astype(F32) * _sigmoid(ag_ref[...].astype(F32))
        abuf[0:CONV_HALO, :] = jnp.where(i > 0, hv_ref[...].astype(F32) * _sigmoid(hg_ref[...].astype(F32)), 0.0)
        d1 = d_ref[...]
        dbuf[0:ts, :] = d1
        dbuf[ts:, :] = jnp.where(i < nst - 1, dn_ref[...], 0.0)
        _shifted_copies(abuf, ashift)
        _shifted_copies(dbuf, dshift)

        @pl.when(first)
        def _():
            dw_ref[...] = jnp.zeros_like(dw_ref)
            db_ref[...] = jnp.zeros_like(db_ref)

        db_ref[...] += jnp.sum(d1, axis=0, keepdims=True)
        for k in range(kw):
            dw_ref[k:k + 1, :] += jnp.sum(d1 * _window(abuf, ashift, base + k, ts), axis=0, keepdims=True)
        for r0 in range(0, ts, CONV_ROWS):
            acc = jnp.zeros((CONV_ROWS, c), F32)
            for k in range(kw):
                acc = acc + w_ref[k:k + 1, :] * _window(dbuf, dshift, r0 + (kw - 1) - k, CONV_ROWS)
            av = av_ref[r0:r0 + CONV_ROWS, :].astype(F32)
            sg = _sigmoid(ag_ref[r0:r0 + CONV_ROWS, :].astype(F32))
            dz_ref[r0:r0 + CONV_ROWS, 0:c] = (acc * sg).astype(BF16)
            dz_ref[r0:r0 + CONV_ROWS, c:2 * c] = (acc * av * sg * (1.0 - sg)).astype(BF16)

    cur, nxt = _seq_specs(dims, ts, c, CONV_HALO, 0, nxt=True)
    return pl.pallas_call(
        body, name=name, grid=(dims.batch_local, nst),
        in_specs=[cur, nxt, *_seq_specs(dims, ts, c, CONV_HALO, 0), *_seq_specs(dims, ts, c, CONV_HALO, 1),
                  pl.BlockSpec((CONV_HALO, c), lambda b, i: (0, 0))]
        + [pl.BlockSpec((ts, r.shape[1]), lambda b, i: (b * nst + i, 0)) for r in rest_of_dz],
        out_specs=[pl.BlockSpec((ts, total), lambda b, i: (b * nst + i, 0)),
                   pl.BlockSpec((CONV_HALO, c), lambda b, i: (0, 0)), pl.BlockSpec((1, c), lambda b, i: (0, 0))],
        out_shape=[jax.ShapeDtypeStruct((t, total), BF16), jax.ShapeDtypeStruct((CONV_HALO, c), F32),
                   jax.ShapeDtypeStruct((1, c), F32)],
        scratch_shapes=[pltpu.VMEM((CONV_HALO + ts, c), F32)] * 2
        + [pltpu.VMEM((SUBLANES - 1, CONV_HALO + ts - SUBLANES, c), F32)] * 2,
        compiler_params=_params("arbitrary", "arbitrary"),
    )(da1, da1, z, z, z, z, w, *rest_of_dz)


FFN_ROWS = 16
FFN_COLS = 256


def _ffn_chunks(ts, f):
    cw = _pick(f, FFN_COLS)
    return [(r0, c0, cw) for r0 in range(0, ts, FFN_ROWS) for c0 in range(0, f, cw)]


def _tap_sources(buf, moved, offsets, rows):
    taps, used = [], 0
    for off in offsets:
        if off % SUBLANES:
            moved[used] = buf[pl.ds(off, rows), :]
            taps.append((moved.at[used], 0))
            used += 1
        else:
            taps.append((buf, off))
    return taps


def _moved_copies(offsets):
    return sum(1 for off in offsets if off % SUBLANES)


def _taps_sum(taps, w_ref, init, r0, cols):
    for k, (src, off) in enumerate(taps):
        init = init + w_ref[k:k + 1, cols] * src[pl.ds(off + r0, init.shape[0]), cols]
    return init


def _ffn_bwd(dact, up, w, b, dims, *, name, ts=128):
    t, f, kw = up.shape[0], dims.d_ff, dims.ffn_conv_width
    nst = dims.seq // ts
    fwd_offsets = [FFN_HALO - (kw - 1) + k for k in range(kw)]
    bwd_offsets = [(kw - 1) - k for k in range(kw)]
    dact_halo = 2 * FFN_HALO

    def body(d_ref, dn_ref, up_ref, hp_ref, hn_ref, w_ref, b_ref, o_ref, dw_ref, db_ref, buf, moved, dbuf, dmoved):
        i = pl.program_id(1)
        first = jnp.logical_and(pl.program_id(0) == 0, i == 0)
        more = i < nst - 1
        buf[0:FFN_HALO, :] = jnp.where(i > 0, hp_ref[...], 0.0)
        buf[FFN_HALO:FFN_HALO + ts, :] = up_ref[...]
        buf[FFN_HALO + ts:, :] = hn_ref[...]
        taps = _tap_sources(buf, moved, fwd_offsets, ts + FFN_HALO)

        def du_chunk(r0, rows, c0, cw, d):
            vcols, gcols = slice(c0, c0 + cw), slice(f + c0, f + c0 + cw)
            uv = _taps_sum(taps, w_ref, jnp.broadcast_to(b_ref[:, vcols], (rows, cw)), r0, vcols)
            ug = _taps_sum(taps, w_ref, jnp.broadcast_to(b_ref[:, gcols], (rows, cw)), r0, gcols)
            sg = _sigmoid(ug)
            dbuf[r0:r0 + rows, vcols] = d * ug * sg
            dbuf[r0:r0 + rows, gcols] = d * uv * sg * (1.0 + ug * (1.0 - sg))

        for r0, c0, cw in _ffn_chunks(ts, f):
            du_chunk(r0, FFN_ROWS, c0, cw, d_ref[r0:r0 + FFN_ROWS, c0:c0 + cw].astype(F32))
        for _, c0, cw in _ffn_chunks(FFN_ROWS, f):
            d_next = dn_ref[:, c0:c0 + cw].astype(F32)[0:FFN_HALO]
            du_chunk(ts, FFN_HALO, c0, cw, jnp.where(more, d_next, 0.0))

        @pl.when(first)
        def _():
            dw_ref[...] = jnp.zeros_like(dw_ref)
            db_ref[...] = jnp.zeros_like(db_ref)

        du = dbuf[0:ts, :]
        db_ref[...] += jnp.sum(du, axis=0, keepdims=True)
        for k, (src, off) in enumerate(taps):
            dw_ref[k:k + 1, :] += jnp.sum(du * src[pl.ds(off, ts), :], axis=0, keepdims=True)

        dtaps = _tap_sources(dbuf, dmoved, bwd_offsets, ts)
        for r0, c0, cw in _ffn_chunks(ts, 2 * f):
            cols = slice(c0, c0 + cw)
            o_ref[r0:r0 + FFN_ROWS, cols] = _taps_sum(dtaps, w_ref, jnp.zeros((FFN_ROWS, cw), F32), r0, cols).astype(BF16)

    up_cur, up_prev = _seq_specs(dims, ts, 2 * f, FFN_HALO, 0)
    _, up_next = _seq_specs(dims, ts, 2 * f, FFN_HALO, 0, nxt=True)
    d_cur, d_next = _seq_specs(dims, ts, f, dact_halo, 0, nxt=True)
    full = lambda rows: pl.BlockSpec((rows, 2 * f), lambda b_, i: (0, 0))
    return pl.pallas_call(
        body, name=name, grid=(dims.batch_local, nst),
        in_specs=[d_cur, d_next, up_cur, up_prev, up_next, full(FFN_HALO), full(1)],
        out_specs=[pl.BlockSpec((ts, 2 * f), lambda b_, i: (b_ * nst + i, 0)), full(FFN_HALO), full(1)],
        out_shape=[jax.ShapeDtypeStruct((t, 2 * f), BF16), jax.ShapeDtypeStruct((FFN_HALO, 2 * f), F32),
                   jax.ShapeDtypeStruct((1, 2 * f), F32)],
        scratch_shapes=[pltpu.VMEM((ts + 2 * FFN_HALO, 2 * f), F32),
                        pltpu.VMEM((_moved_copies(fwd_offsets), ts + FFN_HALO, 2 * f), F32),
                        pltpu.VMEM((ts + FFN_HALO, 2 * f), F32),
                        pltpu.VMEM((_moved_copies(bwd_offsets), ts, 2 * f), F32)],
        compiler_params=_params("arbitrary", "arbitrary"),
    )(dact, dact, up, up, up, w, b)


def _ffn_act_fwd(up, w, b, dims, *, name, ts=128):
    t, f, kw = up.shape[0], dims.d_ff, dims.ffn_conv_width
    offsets = [FFN_HALO - (kw - 1) + k for k in range(kw)]

    def body(up_ref, h_ref, w_ref, b_ref, o_ref, buf, moved):
        buf[FFN_HALO:, :] = up_ref[...]
        buf[0:FFN_HALO, :] = jnp.where(pl.program_id(1) > 0, h_ref[...], 0.0)
        taps = _tap_sources(buf, moved, offsets, ts)
        for r0, c0, cw in _ffn_chunks(ts, f):
            vcols, gcols = slice(c0, c0 + cw), slice(f + c0, f + c0 + cw)
            uv = _taps_sum(taps, w_ref, jnp.broadcast_to(b_ref[:, vcols], (FFN_ROWS, cw)), r0, vcols)
            ug = _taps_sum(taps, w_ref, jnp.broadcast_to(b_ref[:, gcols], (FFN_ROWS, cw)), r0, gcols)
            o_ref[r0:r0 + FFN_ROWS, vcols] = (ug * _sigmoid(ug) * uv).astype(BF16)

    full = lambda rows: pl.BlockSpec((rows, 2 * f), lambda b_, i: (0, 0))
    return pl.pallas_call(
        body, name=name, grid=(dims.batch_local, dims.seq // ts),
        in_specs=[*_seq_specs(dims, ts, 2 * f, FFN_HALO, 0), full(FFN_HALO), full(1)],
        out_specs=pl.BlockSpec((ts, f), lambda b_, i: (b_ * (dims.seq // ts) + i, 0)),
        out_shape=jax.ShapeDtypeStruct((t, f), BF16),
        scratch_shapes=[pltpu.VMEM((FFN_HALO + ts, 2 * f), F32), pltpu.VMEM((_moved_copies(offsets), ts, 2 * f), F32)],
        compiler_params=_params("parallel", "parallel"),
    )(up, up, w, b)


def _ffn_act_bwd(dact, up, w, b, dims, *, name, ts=128):
    t, f, kw = up.shape[0], dims.d_ff, dims.ffn_conv_width
    offsets = [FFN_HALO - (kw - 1) + k for k in range(kw)]

    def body(d_ref, up_ref, h_ref, w_ref, b_ref, du_ref, dw_ref, db_ref, buf, moved):
        i = pl.program_id(1)
        first = jnp.logical_and(pl.program_id(0) == 0, i == 0)
        buf[FFN_HALO:, :] = up_ref[...]
        buf[0:FFN_HALO, :] = jnp.where(i > 0, h_ref[...], 0.0)
        taps = _tap_sources(buf, moved, offsets, ts)
        for r0, c0, cw in _ffn_chunks(ts, f):
            vcols, gcols = slice(c0, c0 + cw), slice(f + c0, f + c0 + cw)
            uv = _taps_sum(taps, w_ref, jnp.broadcast_to(b_ref[:, vcols], (FFN_ROWS, cw)), r0, vcols)
            ug = _taps_sum(taps, w_ref, jnp.broadcast_to(b_ref[:, gcols], (FFN_ROWS, cw)), r0, gcols)
            d = d_ref[r0:r0 + FFN_ROWS, vcols].astype(F32)
            sg = _sigmoid(ug)
            du_ref[r0:r0 + FFN_ROWS, vcols] = d * ug * sg
            du_ref[r0:r0 + FFN_ROWS, gcols] = d * uv * sg * (1.0 + ug * (1.0 - sg))

        @pl.when(first)
        def _():
            dw_ref[...] = jnp.zeros_like(dw_ref)
            db_ref[...] = jnp.zeros_like(db_ref)

        du = du_ref[...]
        db_ref[...] += jnp.sum(du, axis=0, keepdims=True)
        for k, (src, off) in enumerate(taps):
            dw_ref[k:k + 1, :] += jnp.sum(du * src[pl.ds(off, ts), :], axis=0, keepdims=True)

    nst = dims.seq // ts
    n_moved = _moved_copies(offsets)
    full = lambda rows: pl.BlockSpec((rows, 2 * f), lambda b_, i: (0, 0))
    return pl.pallas_call(
        body, name=name, grid=(dims.batch_local, nst),
        in_specs=[pl.BlockSpec((ts, f), lambda b_, i: (b_ * nst + i, 0)),
                  *_seq_specs(dims, ts, 2 * f, FFN_HALO, 0), full(FFN_HALO), full(1)],
        out_specs=[pl.BlockSpec((ts, 2 * f), lambda b_, i: (b_ * nst + i, 0)), full(FFN_HALO), full(1)],
        out_shape=[jax.ShapeDtypeStruct((t, 2 * f), F32), jax.ShapeDtypeStruct((FFN_HALO, 2 * f), F32),
                   jax.ShapeDtypeStruct((1, 2 * f), F32)],
        scratch_shapes=[pltpu.VMEM((FFN_HALO + ts, 2 * f), F32), pltpu.VMEM((n_moved, ts, 2 * f), F32)],
        compiler_params=_params("arbitrary", "arbitrary"),
    )(dact, up, up, w, b)


def _ffn_conv_bwd(du, w, dims, *, name, ts=128):
    t, f2 = du.shape
    kw = dims.ffn_conv_width
    nst = dims.seq // ts

    offsets = [(kw - 1) - k for k in range(kw)]

    def body(d_ref, dn_ref, w_ref, o_ref, buf, moved):
        buf[0:ts, :] = d_ref[...]
        buf[ts:, :] = jnp.where(pl.program_id(1) < nst - 1, dn_ref[...], 0.0)
        taps = _tap_sources(buf, moved, offsets, ts)
        for r0, c0, cw in _ffn_chunks(ts, f2):
            cols = slice(c0, c0 + cw)
            o_ref[r0:r0 + FFN_ROWS, cols] = _taps_sum(taps, w_ref, jnp.zeros((FFN_ROWS, cw), F32), r0, cols).astype(BF16)

    return pl.pallas_call(
        body, name=name, grid=(dims.batch_local, nst),
        in_specs=[*_seq_specs(dims, ts, f2, FFN_HALO, 0, nxt=True), pl.BlockSpec((FFN_HALO, f2), lambda b_, i: (0, 0))],
        out_specs=pl.BlockSpec((ts, f2), lambda b_, i: (b_ * nst + i, 0)),
        out_shape=jax.ShapeDtypeStruct((t, f2), BF16),
        scratch_shapes=[pltpu.VMEM((ts + FFN_HALO, f2), F32), pltpu.VMEM((_moved_copies(offsets), ts, f2), F32)],
        compiler_params=_params("parallel", "parallel"),
    )(du, du, w)


def _alibi_slope(h, n_heads):
    return 2.0 ** (-8.0 * (h + 1) / n_heads)


def _dot_nt(a, b):
    return lax.dot_general(a, b, (((1,), (1,)), ((), ())), preferred_element_type=F32)


def _dot_tn(a, b):
    return lax.dot_general(a, b, (((0,), (0,)), ((), ())), preferred_element_type=F32)


def _attn_view(x, dims, dil):
    return x.reshape(dims.batch_local, dims.seq // dil, dil * x.shape[-1])


def _attn_fwd_group(q, k, v, state, dims, dil, *, last, name):
    t, a = q.shape
    assert 2 * dims.head_dim == 128 and dims.n_heads % 2 == 0
    blk, hd = ATTN_BLOCK, dims.head_dim
    nb = dims.seq // dil // blk
    has_prev = nb > 1
    nkeys = 2 * blk if has_prev else blk

    def body(*refs):
        it = iter(refs)
        q_ref, kc_ref, vc_ref = next(it), next(it), next(it)
        kp_ref, vp_ref = (next(it), next(it)) if has_prev else (None, None)
        m_in, l_in, acc_in = (next(it), next(it), next(it)) if state is not None else (None, None, None)
        outs = list(it)
        iq = lax.broadcasted_iota(jnp.int32, (blk, nkeys), 0)
        jk = lax.broadcasted_iota(jnp.int32, (blk, nkeys), 1)
        if has_prev:
            steps = iq + blk - jk
            valid = (steps >= 0) & (steps <= blk) & ((jk >= blk) | (pl.program_id(2) > 0))
        else:
            steps = iq - jk
            valid = steps >= 0
        dist = steps.astype(F32) * float(dil)
        low = lax.broadcasted_iota(jnp.int32, (blk, 2 * hd), 1) < hd
        for hp in range(dims.n_heads // 2):
            sl = slice(2 * hd * hp, 2 * hd * (hp + 1))
            q2 = q_ref[:, sl]
            if has_prev:
                kcat = jnp.concatenate([kp_ref[:, sl], kc_ref[:, sl]], axis=0)
                vcat = jnp.concatenate([vp_ref[:, sl], vc_ref[:, sl]], axis=0)
            else:
                kcat, vcat = kc_ref[:, sl], vc_ref[:, sl]
            halves = []
            for half in range(2):
                col = 2 * hd * hp + hd * half
                qh = jnp.where(low if half == 0 else jnp.logical_not(low), q2, jnp.zeros_like(q2))
                sc = _dot_nt(qh, kcat) - _alibi_slope(2 * hp + half, dims.n_heads) * dist
                sc = jnp.where(valid, sc, MASKED_SCORE)
                row_max = jnp.max(sc, axis=-1, keepdims=True)
                if state is None:
                    m_new = row_max
                    p = jnp.exp(sc - m_new)
                    alpha = None
                    l_new = jnp.sum(p, axis=-1, keepdims=True)
                else:
                    m_old = m_in[:, col:col + 1]
                    m_new = jnp.maximum(m_old, row_max)
                    p = jnp.exp(sc - m_new)
                    alpha = jnp.exp(m_old - m_new)
                    l_new = alpha * l_in[:, col:col + 1] + jnp.sum(p, axis=-1, keepdims=True)
                pv = jnp.dot(p.astype(BF16), vcat, preferred_element_type=F32)
                halves.append((m_new, l_new, alpha, pv))
            (m_a, l_a, al_a, pv_a), (m_b, l_b, al_b, pv_b) = halves
            if state is None:
                acc = jnp.where(low, pv_a, pv_b)
            else:
                old = acc_in[:, sl]
                acc = jnp.where(low, al_a * old + pv_a, al_b * old + pv_b)
            m2 = jnp.where(low, m_a, m_b)
            l2 = jnp.where(low, l_a, l_b)
            if last:
                outs[0][:, sl] = (acc / l2).astype(BF16)
                outs[1][:, sl] = m2 + jnp.log(l2)
            else:
                outs[0][:, sl] = m2
                outs[1][:, sl] = l2
                outs[2][:, sl] = acc

    cur = pl.BlockSpec((None, blk, a), lambda b, r, i: (b, i, r))
    prev = pl.BlockSpec((None, blk, a), lambda b, r, i: (b, jnp.maximum(i - 1, 0), r))
    args, in_specs = [q, k, v], [cur, cur, cur]
    if has_prev:
        args += [k, v]
        in_specs += [prev, prev]
    if state is not None:
        args += list(state)
        in_specs += [cur] * 3
    shape = lambda dt: jax.ShapeDtypeStruct((dims.batch_local, dims.seq // dil, dil * a), dt)
    out_shape = [shape(BF16), shape(F32)] if last else [shape(F32)] * 3
    outs = pl.pallas_call(
        body, name=name, grid=(dims.batch_local, dil, nb),
        in_specs=in_specs, out_specs=[cur] * len(out_shape), out_shape=out_shape,
        compiler_params=_params("parallel", "parallel", "parallel"),
    )(*[_attn_view(x, dims, dil) for x in args])
    return tuple(o.reshape(t, a) for o in outs)


def _attn_delta(do, o, head_ones, dims, *, name, tr=512):
    t, a = o.shape
    tr = _pick(t, tr, 8)

    def body(do_ref, o_ref, e_ref, d_ref):
        prod = do_ref[...].astype(F32) * o_ref[...].astype(F32)
        d_ref[...] = _head_mean(prod, e_ref, dims.head_dim) * float(dims.head_dim)

    return pl.pallas_call(
        body, name=name, grid=(t // tr,),
        in_specs=[_row_spec(tr, a), _row_spec(tr, a), pl.BlockSpec((a, a), lambda i: (0, 0))],
        out_specs=_row_spec(tr, a), out_shape=jax.ShapeDtypeStruct((t, a), F32),
        compiler_params=_params("parallel"),
    )(do, o, head_ones)


def _attn_bwd_group(q, k, v, do, lse, delta, dims, dil, *, name):
    t, a = q.shape
    blk, hd = ATTN_BLOCK, dims.head_dim
    nb = dims.seq // dil // blk
    has_next = nb > 1

    def body(*refs):
        k_ref, v_ref, q_ref, do_ref, lse_ref, dl_ref = refs[:6]
        if has_next:
            qn_ref, don_ref, lsen_ref, dln_ref = refs[6:10]
            dq_ref, dk_ref, dv_ref, carry = refs[10:]
        else:
            dq_ref, dk_ref, dv_ref = refs[6:]
        j = pl.program_id(2)
        iq = lax.broadcasted_iota(jnp.int32, (blk, blk), 0)
        jk = lax.broadcasted_iota(jnp.int32, (blk, blk), 1)
        low = lax.broadcasted_iota(jnp.int32, (blk, 2 * hd), 1) < hd

        def pair(hp, qr, dor, lser, dlr, steps, valid):
            sl = slice(2 * hd * hp, 2 * hd * (hp + 1))
            q2, do2, k2, v2 = qr[:, sl], dor[:, sl], k_ref[:, sl], v_ref[:, sl]
            dist = steps.astype(F32) * float(dil)
            dq_h, dk2, dv2 = [], None, None
            for half in range(2):
                col = 2 * hd * hp + hd * half
                mask = low if half == 0 else jnp.logical_not(low)
                qh = jnp.where(mask, q2, jnp.zeros_like(q2))
                doh = jnp.where(mask, do2, jnp.zeros_like(do2))
                sc = _dot_nt(qh, k2) - _alibi_slope(2 * hp + half, dims.n_heads) * dist
                p = jnp.where(valid, jnp.exp(sc - lser[:, col:col + 1]), 0.0)
                ds = p * (_dot_nt(doh, v2) - dlr[:, col:col + 1])
                ds_b, p_b = ds.astype(BF16), p.astype(BF16)
                dq_h.append(jnp.dot(ds_b, k2, preferred_element_type=F32))
                dk_h, dv_h = _dot_tn(ds_b, qh), _dot_tn(p_b, doh)
                dk2 = dk_h if dk2 is None else dk2 + dk_h
                dv2 = dv_h if dv2 is None else dv2 + dv_h
            return sl, jnp.where(low, dq_h[0], dq_h[1]), dk2, dv2

        if has_next:
            @pl.when(j == 0)
            def _():
                carry[...] = jnp.zeros_like(carry)

        for hp in range(dims.n_heads // 2):
            sl, dq2, dk2, dv2 = pair(hp, q_ref, do_ref, lse_ref, dl_ref, iq - jk, iq >= jk)
            dq_ref[:, sl] = (carry[:, sl] + dq2) if has_next else dq2
            dk_ref[:, sl] = dk2
            dv_ref[:, sl] = dv2

        if has_next:
            @pl.when(j + 1 < nb)
            def _():
                for hp in range(dims.n_heads // 2):
                    sl, dq2, dk2, dv2 = pair(hp, qn_ref, don_ref, lsen_ref, dln_ref, iq - jk + blk, jk >= iq)
                    carry[:, sl] = dq2
                    dk_ref[:, sl] += dk2
                    dv_ref[:, sl] += dv2

    cur = pl.BlockSpec((None, blk, a), lambda b, r, j: (b, j, r))
    nxt = pl.BlockSpec((None, blk, a), lambda b, r, j: (b, jnp.minimum(j + 1, nb - 1), r))
    args, in_specs = [k, v, q, do, lse, delta], [cur] * 6
    if has_next:
        args += [q, do, lse, delta]
        in_specs += [nxt] * 4
    shape = jax.ShapeDtypeStruct((dims.batch_local, dims.seq // dil, dil * a), F32)
    outs = pl.pallas_call(
        body, name=name, grid=(dims.batch_local, dil, nb),
        in_specs=in_specs, out_specs=[cur] * 3, out_shape=[shape] * 3,
        scratch_shapes=[pltpu.VMEM((blk, a), F32)] if has_next else [],
        compiler_params=_params("parallel", "parallel", "arbitrary"),
    )(*[_attn_view(x, dims, dil) for x in args])
    return tuple(o.reshape(t, a) for o in outs)


LANES = 128
MASK_BIAS = 1e30
RESIDUE_DILATIONS = tuple(d for d in DILATIONS if d > 1)


def _rows_to_residues(value, out_ref, scr, d):
    rows, width = value.shape
    for c in range(width // LANES):
        cols = slice(LANES * c, LANES * (c + 1))
        scr[c] = value[:, cols]
        for r in range(d):
            out_ref[r, :, cols] = scr[c, pl.ds(r, rows // d, stride=d), :].astype(out_ref.dtype)


def _residues_to_rows(in_ref, scr, d):
    _, n, width = in_ref.shape
    slabs = []
    for c in range(width // LANES):
        cols = slice(LANES * c, LANES * (c + 1))
        for r in range(d):
            scr[c, pl.ds(r, n, stride=d), :] = in_ref[r, :, cols].astype(F32)
        slabs.append(scr[c])
    return slabs[0] if len(slabs) == 1 else jnp.concatenate(slabs, axis=1)


def _residue_shape(dims, d, width, dtype):
    return jax.ShapeDtypeStruct((dims.batch_local, d, dims.seq // d, width), dtype)


def _residue_spec(dims, d, tr, width):
    tiles = dims.seq // tr
    return pl.BlockSpec((None, d, tr // d, width), lambda i: (i // tiles, 0, i % tiles, 0))


def _head_sum_matrix(dims):
    a = dims.n_heads * dims.head_dim
    head = jnp.arange(a, dtype=jnp.int32) // dims.head_dim
    return (head[:, None] == jnp.arange(LANES, dtype=jnp.int32)[None, :]).astype(BF16)


def _two_pass_dot(v, m):
    hi = v.astype(BF16)
    lo = (v - hi.astype(F32)).astype(BF16)
    return jnp.dot(hi, m, preferred_element_type=F32) + jnp.dot(lo, m, preferred_element_type=F32)


def _qkv_layouts_fwd(z, gq, gk, head_ones, dims, *, name, tr=256):
    t = z.shape[0]
    a = dims.n_heads * dims.head_dim
    q_scale = dims.head_dim ** -0.5
    nres = len(RESIDUE_DILATIONS)

    def body(q_ref, k_ref, v_ref, gq_ref, gk_ref, sum_ref, spread_ref, *rest):
        outs, scr = rest[:-1], rest[-1]
        qv, kv = q_ref[...].astype(F32), k_ref[...].astype(F32)
        mean = lambda val: _two_pass_dot(_two_pass_dot(val, sum_ref[...]), spread_ref[...]) * (1.0 / dims.head_dim)
        rq = lax.rsqrt(mean(qv * qv) + RMS_EPS)
        rk = lax.rsqrt(mean(kv * kv) + RMS_EPS)
        values = (qv * rq * gq_ref[...] * q_scale, kv * rk * gk_ref[...], v_ref[...].astype(F32))
        for j, val in enumerate(values):
            outs[j][...] = val.astype(BF16)
            for g, d in enumerate(RESIDUE_DILATIONS):
                _rows_to_residues(val, outs[3 * (g + 1) + j], scr, d)

    out_specs = [_row_spec(tr, a)] * 3
    out_shape = [jax.ShapeDtypeStruct((t, a), BF16)] * 3
    for d in RESIDUE_DILATIONS:
        out_specs += [_residue_spec(dims, d, tr, a)] * 3
        out_shape += [_residue_shape(dims, d, a, BF16)] * 3
    outs = pl.pallas_call(
        body, name=name, grid=(t // tr,),
        in_specs=[_row_spec(tr, a, 2), _row_spec(tr, a, 3), _row_spec(tr, a, 4), _vec_spec(a), _vec_spec(a),
                  pl.BlockSpec((a, LANES), lambda i: (0, 0)), pl.BlockSpec((LANES, a), lambda i: (0, 0))],
        out_specs=out_specs, out_shape=out_shape,
        scratch_shapes=[pltpu.VMEM((a // LANES, tr, LANES), F32)],
        compiler_params=_params("parallel"),
    )(z, z, z, gq, gk, *head_ones)
    return {d: tuple(outs[3 * g:3 * g + 3]) for g, d in enumerate((1,) + RESIDUE_DILATIONS)}


def _attn_specs(dims, dil, width):
    blk = ATTN_BLOCK
    nb = dims.seq // dil // blk
    if dil == 1:
        grid = (dims.batch_local, nb)
        at = lambda f: pl.BlockSpec((blk, width), lambda b, i: (b * nb + f(i), 0))
    else:
        grid = (dims.batch_local, dil, nb)
        at = lambda f: pl.BlockSpec((None, None, blk, width), lambda b, r, i: (b, r, f(i), 0))
    return grid, at(lambda i: i), at(lambda i: jnp.maximum(i - 1, 0)), at(lambda i: jnp.minimum(i + 1, nb - 1))


def _head_slopes(n_heads):
    h = lax.broadcasted_iota(jnp.int32, (n_heads, 1, 1), 0).astype(F32)
    return jnp.exp((h + 1.0) * (-8.0 / n_heads * math.log(2.0)))


def _pair_masks(hd):
    low = lax.broadcasted_iota(jnp.int32, (1, 2 * hd), 1) < hd
    return low, jnp.logical_not(low)


def _prob_spec(dims, dil, width, col, step=0):
    blk = ATTN_BLOCK
    nb = dims.seq // dil // blk
    at = lambda i: jnp.minimum(i + step, nb - 1)
    if dil == 1:
        return pl.BlockSpec((None, dims.n_heads, blk, width), lambda b, i: (b * nb + at(i), 0, 0, col))
    return pl.BlockSpec((None, None, None, dims.n_heads, blk, width), lambda b, r, i: (b, r, at(i), 0, 0, col))


def _attn_fwd(q, k, v, dims, dil, *, name):
    a = dims.n_heads * dims.head_dim
    heads, hd, blk = dims.n_heads, dims.head_dim, ATTN_BLOCK
    assert 2 * hd == LANES and heads % 2 == 0 and heads <= LANES
    nb = dims.seq // dil // blk
    has_prev = nb > 1
    nkeys = 2 * blk if has_prev else blk
    grid, cur, prev, _ = _attn_specs(dims, dil, a)
    _, cur_stat, _, _ = _attn_specs(dims, dil, LANES)

    def body(*refs):
        if has_prev:
            q_ref, kc_ref, vc_ref, kp_ref, vp_ref, o_ref, lse_ref, p_scr, s_scr = refs
        else:
            q_ref, kc_ref, vc_ref, o_ref, lse_ref, p_scr, s_scr = refs
        low, high = _pair_masks(hd)

        def keys(cur_ref, prev_ref, sl):
            return jnp.concatenate([prev_ref[:, sl], cur_ref[:, sl]], axis=0) if has_prev else cur_ref[:, sl]

        for hp in range(heads // 2):
            sl = slice(LANES * hp, LANES * (hp + 1))
            q2 = q_ref[:, sl]
            kcat = keys(kc_ref, kp_ref if has_prev else None, sl)
            s_scr[2 * hp] = _dot_nt(jnp.where(low, q2, jnp.zeros_like(q2)), kcat)
            s_scr[2 * hp + 1] = _dot_nt(jnp.where(high, q2, jnp.zeros_like(q2)), kcat)

        iq = lax.broadcasted_iota(jnp.int32, (blk, nkeys), 0)
        jk = lax.broadcasted_iota(jnp.int32, (blk, nkeys), 1)
        if has_prev:
            steps = iq + blk - jk
            valid = (steps >= 0) & (steps <= blk) & ((jk >= blk) | (pl.program_id(len(grid) - 1) > 0))
        else:
            steps = iq - jk
            valid = steps >= 0
        bias = jnp.where(valid, steps.astype(F32) * (-float(dil)), -MASK_BIAS)
        s = s_scr[...] + _head_slopes(heads) * bias[None]
        m = jnp.max(s, axis=-1, keepdims=True)
        p = jnp.exp(s - m)
        l = jnp.sum(p, axis=-1, keepdims=True)
        p_scr[...] = (p * (1.0 / l)).astype(BF16)
        lse = m + jnp.log(l)

        lane = lax.broadcasted_iota(jnp.int32, (blk, LANES), 1)
        stat = jnp.zeros((blk, LANES), F32)
        for hp in range(heads // 2):
            sl = slice(LANES * hp, LANES * (hp + 1))
            vcat = keys(vc_ref, vp_ref if has_prev else None, sl)
            pv_a = jnp.dot(p_scr[2 * hp], vcat, preferred_element_type=F32)
            pv_b = jnp.dot(p_scr[2 * hp + 1], vcat, preferred_element_type=F32)
            o_ref[:, sl] = jnp.where(low, pv_a, pv_b)
            stat = jnp.where(lane == 2 * hp, lse[2 * hp], stat)
            stat = jnp.where(lane == 2 * hp + 1, lse[2 * hp + 1], stat)
        lse_ref[...] = stat

    lead = q.shape[:-2]
    rows = q.shape[-2]
    return pl.pallas_call(
        body, name=name, grid=grid,
        in_specs=[cur, cur, cur] + ([prev, prev] if has_prev else []),
        out_specs=[cur, cur_stat, _prob_spec(dims, dil, nkeys, 0)],
        out_shape=[jax.ShapeDtypeStruct(lead + (rows, a), F32), jax.ShapeDtypeStruct(lead + (rows, LANES), F32),
                   jax.ShapeDtypeStruct(lead + (rows // blk, heads, blk, nkeys), BF16)],
        scratch_shapes=[pltpu.VMEM((heads, blk, nkeys), F32)],
        compiler_params=_params(*["parallel"] * len(grid)),
    )(q, k, v, *([k, v] if has_prev else []))


def _attn_combine(groups, head_spread, dims, *, name, tr=256):
    t = dims.tokens
    a = dims.n_heads * dims.head_dim
    dils = tuple(groups)

    def body(*refs):
        ins = refs[:2 * len(dils)]
        x_ref = refs[2 * len(dils)]
        o_ref = refs[2 * len(dils) + 1]
        lse_refs = refs[2 * len(dils) + 2:-2]
        scr, scr_stat = refs[-2], refs[-1]
        outs, stats = [], []
        for g, d in enumerate(dils):
            if d == 1:
                outs.append(ins[2 * g][...])
                stats.append(ins[2 * g + 1][...])
            else:
                outs.append(_residues_to_rows(ins[2 * g], scr, d))
                stats.append(_residues_to_rows(ins[2 * g + 1], scr_stat, d))
        top = functools.reduce(jnp.maximum, stats)
        weights = [jnp.exp(s - top) for s in stats]
        total = functools.reduce(jnp.add, weights)
        joint = top + jnp.log(total)
        inv = 1.0 / total
        acc = None
        for w, o in zip(weights, outs):
            term = _two_pass_dot(w * inv, x_ref[...]) * o
            acc = term if acc is None else acc + term
        o_ref[...] = acc.astype(BF16)
        for g, d in enumerate(dils):
            if d == 1:
                lse_refs[g][...] = joint
            else:
                _rows_to_residues(joint, lse_refs[g], scr_stat, d)

    in_specs, args, lse_specs, lse_shapes = [], [], [], []
    for d in dils:
        if d == 1:
            in_specs += [_row_spec(tr, a), _row_spec(tr, LANES)]
            lse_specs.append(_row_spec(tr, LANES))
            lse_shapes.append(jax.ShapeDtypeStruct((t, LANES), F32))
        else:
            in_specs += [_residue_spec(dims, d, tr, a), _residue_spec(dims, d, tr, LANES)]
            lse_specs.append(_residue_spec(dims, d, tr, LANES))
            lse_shapes.append(_residue_shape(dims, d, LANES, F32))
        args += list(groups[d])
    outs = pl.pallas_call(
        body, name=name, grid=(t // tr,),
        in_specs=in_specs + [pl.BlockSpec((LANES, a), lambda i: (0, 0))],
        out_specs=[_row_spec(tr, a)] + lse_specs,
        out_shape=[jax.ShapeDtypeStruct((t, a), BF16)] + lse_shapes,
        scratch_shapes=[pltpu.VMEM((a // LANES, tr, LANES), F32), pltpu.VMEM((1, tr, LANES), F32)],
        compiler_params=_params("parallel"),
    )(*args, head_spread)
    return outs[0], dict(zip(dils, outs[1:]))


def _attn_bwd_prep(do, o, head_sum, dims, *, name, tr=256):
    t, a = o.shape

    def body(do_ref, o_ref, e_ref, *rest):
        outs, scr, scr_stat = rest[:-2], rest[-2], rest[-1]
        dov = do_ref[...].astype(F32)
        delta = _two_pass_dot(dov * o_ref[...].astype(F32), e_ref[...])
        outs[0][...] = delta
        for g, d in enumerate(RESIDUE_DILATIONS):
            _rows_to_residues(dov, outs[1 + 2 * g], scr, d)
            _rows_to_residues(delta, outs[2 + 2 * g], scr_stat, d)

    out_specs, out_shape = [_row_spec(tr, LANES)], [jax.ShapeDtypeStruct((t, LANES), F32)]
    for d in RESIDUE_DILATIONS:
        out_specs += [_residue_spec(dims, d, tr, a), _residue_spec(dims, d, tr, LANES)]
        out_shape += [_residue_shape(dims, d, a, BF16), _residue_shape(dims, d, LANES, F32)]
    outs = pl.pallas_call(
        body, name=name, grid=(t // tr,),
        in_specs=[_row_spec(tr, a), _row_spec(tr, a), pl.BlockSpec((a, LANES), lambda i: (0, 0))],
        out_specs=out_specs, out_shape=out_shape,
        scratch_shapes=[pltpu.VMEM((a // LANES, tr, LANES), F32), pltpu.VMEM((1, tr, LANES), F32)],
        compiler_params=_params("parallel"),
    )(do, o, head_sum)
    dos, deltas = {1: do}, {1: outs[0]}
    for g, d in enumerate(RESIDUE_DILATIONS):
        dos[d], deltas[d] = outs[1 + 2 * g], outs[2 + 2 * g]
    return dos, deltas


def _attn_bwd(q, k, v, do, lse, delta, probs, group_lse, dims, dil, *, name):
    a = dims.n_heads * dims.head_dim
    heads, hd, blk = dims.n_heads, dims.head_dim, ATTN_BLOCK
    nb = dims.seq // dil // blk
    has_next = nb > 1
    nq = 2 * blk if has_next else blk
    grid, cur, _, nxt = _attn_specs(dims, dil, a)
    _, cur_stat, _, nxt_stat = _attn_specs(dims, dil, LANES)

    def body(*refs):
        k_ref, v_ref, q_ref, do_ref, lse_ref, gl_ref, dl_ref, pc_ref = refs[:8]
        if has_next:
            qn_ref, don_ref, lsen_ref, gln_ref, dln_ref, pn_ref = refs[8:14]
            dq_ref, dk_ref, dv_ref, dp_scr, p_scr, ds_scr, carry = refs[14:]
        else:
            dq_ref, dk_ref, dv_ref, dp_scr, p_scr, ds_scr = refs[8:]
        j = pl.program_id(len(grid) - 1)
        low, high = _pair_masks(hd)

        def stacked(ref, nref, sl):
            return jnp.concatenate([ref[:, sl], nref[:, sl]], axis=0) if has_next else ref[:, sl]

        def halves(x):
            return jnp.where(low, x, jnp.zeros_like(x)), jnp.where(high, x, jnp.zeros_like(x))

        for hp in range(heads // 2):
            sl = slice(LANES * hp, LANES * (hp + 1))
            v2 = v_ref[:, sl]
            do_a, do_b = halves(stacked(do_ref, don_ref if has_next else None, sl))
            dp_scr[2 * hp], dp_scr[2 * hp + 1] = _dot_nt(do_a, v2), _dot_nt(do_b, v2)

        lse_all = stacked(lse_ref, lsen_ref if has_next else None, slice(None))
        gl_all = stacked(gl_ref, gln_ref if has_next else None, slice(None))
        dl_all = stacked(dl_ref, dln_ref if has_next else None, slice(None))
        share = jnp.exp(gl_all - lse_all)
        share3 = jnp.stack([share[:, h:h + 1] for h in range(heads)])
        dl3 = jnp.stack([dl_all[:, h:h + 1] for h in range(heads)])
        kept = pc_ref[...].astype(F32)
        if has_next:
            kept = jnp.concatenate([kept, jnp.where(j + 1 < nb, pn_ref[...].astype(F32), 0.0)], axis=1)
        p = kept * share3
        p_scr[...] = p.astype(BF16)
        ds_scr[...] = (p * (dp_scr[...] - dl3)).astype(BF16)

        if has_next:
            @pl.when(j == 0)
            def _():
                carry[...] = jnp.zeros_like(carry)

        for hp in range(heads // 2):
            sl = slice(LANES * hp, LANES * (hp + 1))
            k2 = k_ref[:, sl]
            q_a, q_b = halves(stacked(q_ref, qn_ref if has_next else None, sl))
            do_a, do_b = halves(stacked(do_ref, don_ref if has_next else None, sl))
            ds_a, ds_b = ds_scr[2 * hp], ds_scr[2 * hp + 1]
            dq2 = jnp.where(low, jnp.dot(ds_a, k2, preferred_element_type=F32),
                            jnp.dot(ds_b, k2, preferred_element_type=F32))
            dk_ref[:, sl] = _dot_tn(ds_a, q_a) + _dot_tn(ds_b, q_b)
            dv_ref[:, sl] = _dot_tn(p_scr[2 * hp], do_a) + _dot_tn(p_scr[2 * hp + 1], do_b)
            if has_next:
                dq_ref[:, sl] = carry[:, sl] + dq2[:blk]
                carry[:, sl] = dq2[blk:]
            else:
                dq_ref[:, sl] = dq2

    args = [k, v, q, do, lse, group_lse, delta, probs]
    in_specs = [cur] * 4 + [cur_stat] * 3 + [_prob_spec(dims, dil, blk, 1 if has_next else 0)]
    if has_next:
        args += [q, do, lse, group_lse, delta, probs]
        in_specs += [nxt] * 2 + [nxt_stat] * 3 + [_prob_spec(dims, dil, blk, 0, step=1)]
    shape = jax.ShapeDtypeStruct(q.shape, F32)
    scratch = [pltpu.VMEM((heads, nq, blk), F32)] + [pltpu.VMEM((heads, nq, blk), BF16)] * 2
    if has_next:
        scratch.append(pltpu.VMEM((blk, a), F32))
    return pl.pallas_call(
        body, name=name, grid=grid, in_specs=in_specs, out_specs=[cur] * 3, out_shape=[shape] * 3,
        scratch_shapes=scratch,
        compiler_params=_params(*["parallel"] * (len(grid) - 1), "arbitrary"),
    )(*args)


def _qkv_layouts_bwd(z, grads, gq, gk, head_ones, dims, *, name, tr=256):
    t = z.shape[0]
    a = dims.n_heads * dims.head_dim
    q_scale = dims.head_dim ** -0.5
    dils = tuple(grads)

    def body(q_ref, k_ref, *rest):
        d_refs = rest[:3 * len(dils)]
        gq_ref, gk_ref, sum_ref, spread_ref, dz_ref, dgq_ref, dgk_ref, scr = rest[3 * len(dils):]
        first = pl.program_id(0) == 0
        mean = lambda val: _two_pass_dot(_two_pass_dot(val, sum_ref[...]), spread_ref[...]) * (1.0 / dims.head_dim)

        def total(j):
            acc = None
            for g, d in enumerate(dils):
                ref = d_refs[3 * g + j]
                part = ref[...] if d == 1 else _residues_to_rows(ref, scr, d)
                acc = part if acc is None else acc + part
            return acc

        def norm_bwd(x_ref, dy, g_ref, scale, col, dg_ref):
            xv = x_ref[...].astype(F32)
            dy = dy * scale
            r = lax.rsqrt(mean(xv * xv) + RMS_EPS)
            gy = dy * g_ref[...]
            dx = r * gy - xv * (r * r * r) * mean(xv * gy)
            dz_ref[:, col * a:(col + 1) * a] = dx.astype(BF16)
            _accumulate(dg_ref, jnp.sum(dy * xv * r, axis=0, keepdims=True), first)

        norm_bwd(q_ref, total(0), gq_ref, q_scale, 0, dgq_ref)
        norm_bwd(k_ref, total(1), gk_ref, 1.0, 1, dgk_ref)
        dz_ref[:, 2 * a:3 * a] = total(2).astype(BF16)

    in_specs, args = [_row_spec(tr, a, 2), _row_spec(tr, a, 3)], [z, z]
    for d in dils:
        in_specs += [_row_spec(tr, a) if d == 1 else _residue_spec(dims, d, tr, a)] * 3
        args += list(grads[d])
    in_specs += [_vec_spec(a), _vec_spec(a), pl.BlockSpec((a, LANES), lambda i: (0, 0)),
                 pl.BlockSpec((LANES, a), lambda i: (0, 0))]
    return pl.pallas_call(
        body, name=name, grid=(t // tr,), in_specs=in_specs,
        out_specs=[_row_spec(tr, 3 * a), _vec_spec(a), _vec_spec(a)],
        out_shape=[jax.ShapeDtypeStruct((t, 3 * a), BF16)] + [jax.ShapeDtypeStruct((1, a), F32)] * 2,
        scratch_shapes=[pltpu.VMEM((a // LANES, tr, LANES), F32)],
        compiler_params=_params("arbitrary"),
    )(*args, gq, gk, *head_ones)


def _mix_fwd(ya, yb, z, gate_b, dims, *, name, tr=512):
    t, d = ya.shape
    tr = _pick(t, tr, 8)
    first_gate_col = z.shape[1] // d - 2

    def body(ya_ref, yb_ref, ga_ref, gb_ref, ba_ref, bb_ref, o_ref):
        g_a = _sigmoid(ga_ref[...].astype(F32) + ba_ref[...])
        g_b = _sigmoid(gb_ref[...].astype(F32) + bb_ref[...])
        o_ref[...] = (g_a * ya_ref[...] + g_b * yb_ref[...]).astype(BF16)

    return pl.pallas_call(
        body, name=name, grid=(t // tr,),
        in_specs=[_row_spec(tr, d), _row_spec(tr, d), _row_spec(tr, d, first_gate_col),
                  _row_spec(tr, d, first_gate_col + 1), _vec_spec(d, 0), _vec_spec(d, 1)],
        out_specs=_row_spec(tr, d), out_shape=jax.ShapeDtypeStruct((t, d), BF16),
        compiler_params=_params("parallel"),
    )(ya, yb, z, z, gate_b, gate_b)


def _mix_bwd(dmix, ya, yb, z, gate_b, dims, *, name, tr=512):
    t, d = ya.shape
    tr = _pick(t, tr, 8)
    first_gate_col = z.shape[1] // d - 2

    def body(dm_ref, ya_ref, yb_ref, ga_ref, gb_ref, ba_ref, bb_ref, dya_ref, dyb_ref, dz_ref, db_ref):
        dm = dm_ref[...].astype(F32)
        g_a = _sigmoid(ga_ref[...].astype(F32) + ba_ref[...])
        g_b = _sigmoid(gb_ref[...].astype(F32) + bb_ref[...])
        dya_ref[...] = (dm * g_a).astype(BF16)
        dyb_ref[...] = (dm * g_b).astype(BF16)
        dl_a = dm * ya_ref[...] * g_a * (1.0 - g_a)
        dl_b = dm * yb_ref[...] * g_b * (1.0 - g_b)
        dz_ref[:, 0:d] = dl_a.astype(BF16)
        dz_ref[:, d:2 * d] = dl_b.astype(BF16)
        first = pl.program_id(0) == 0
        sums = jnp.concatenate([jnp.sum(dl_a, axis=0, keepdims=True), jnp.sum(dl_b, axis=0, keepdims=True)], axis=1)
        _accumulate(db_ref, sums, first)

    return pl.pallas_call(
        body, name=name, grid=(t // tr,),
        in_specs=[_row_spec(tr, d), _row_spec(tr, d), _row_spec(tr, d), _row_spec(tr, d, first_gate_col),
                  _row_spec(tr, d, first_gate_col + 1), _vec_spec(d, 0), _vec_spec(d, 1)],
        out_specs=[_row_spec(tr, d), _row_spec(tr, d), _row_spec(tr, 2 * d), _vec_spec(2 * d)],
        out_shape=[jax.ShapeDtypeStruct((t, d), BF16)] * 2 + [jax.ShapeDtypeStruct((t, 2 * d), BF16),
                                                              jax.ShapeDtypeStruct((1, 2 * d), F32)],
        compiler_params=_params("arbitrary"),
    )(dmix, ya, yb, z, z, gate_b, gate_b)


def _loss_head(y, target, *, name, tr=512):
    t, d = y.shape
    tr = _pick(t, tr, 8)

    def body(y_ref, t_ref, dy_ref, dyb_ref, loss_ref):
        err = y_ref[...] - t_ref[...]
        dy = err * (1.0 / d)
        dy_ref[...] = dy
        dyb_ref[...] = dy.astype(BF16)
        part = jnp.sum(jnp.sum(err * err, axis=-1, keepdims=True), axis=0, keepdims=True) * (0.5 / d)
        _accumulate(loss_ref, jnp.broadcast_to(part, (8, 128)), pl.program_id(0) == 0)

    return pl.pallas_call(
        body, name=name, grid=(t // tr,),
        in_specs=[_row_spec(tr, d), _row_spec(tr, d)],
        out_specs=[_row_spec(tr, d), _row_spec(tr, d), pl.BlockSpec((8, 128), lambda i: (0, 0))],
        out_shape=[jax.ShapeDtypeStruct((t, d), F32), jax.ShapeDtypeStruct((t, d), BF16),
                   jax.ShapeDtypeStruct((8, 128), F32)],
        compiler_params=_params("arbitrary"),
    )(y, target)


def _adamw(w, grads, m, v, *, name, tr=256):
    r, c = w.shape
    tr = _pick(r, tr, 8)
    ng = len(grads)
    c1 = 1.0 - ADAM_B1 ** ADAM_STEP
    c2 = 1.0 - ADAM_B2 ** ADAM_STEP

    def body(*refs):
        w_ref, g_refs, m_ref, v_ref = refs[0], refs[1:1 + ng], refs[1 + ng], refs[2 + ng]
        g_out, d_out, m_out, v_out = refs[3 + ng:]
        g = g_refs[0][...]
        for extra in g_refs[1:]:
            g = g + extra[...]
        m_new = ADAM_B1 * m_ref[...] + (1.0 - ADAM_B1) * g
        v_new = ADAM_B2 * v_ref[...] + (1.0 - ADAM_B2) * (g * g)
        g_out[...] = g
        m_out[...] = m_new
        v_out[...] = v_new
        d_out[...] = -ADAM_LR * ((m_new / c1) / (jnp.sqrt(v_new / c2) + ADAM_EPS) + ADAM_WD * w_ref[...])

    spec = pl.BlockSpec((tr, c), lambda i: (i, 0))
    return pl.pallas_call(
        body, name=name, grid=(r // tr,),
        in_specs=[spec] * (3 + ng), out_specs=[spec] * 4, out_shape=[jax.ShapeDtypeStruct((r, c), F32)] * 4,
        compiler_params=_params("parallel"),
    )(w, *grads, m, v)


CHIP_PEERS = ((1, 0), (0, 1), (1, 1))


def _place():
    return lax.axis_index("x"), lax.axis_index("y"), lax.axis_index("c")


HBM = pl.BlockSpec(memory_space=pltpu.HBM)
SEM = pl.BlockSpec(memory_space=pltpu.SEMAPHORE)
IN_FLIGHT = pltpu.SideEffectType.DATAFLOW_SIDE_EFFECTING


def _in_hbm(a):
    return pltpu.with_memory_space_constraint(a, pltpu.HBM)


def _cast_to_lands(shards, dtypes, *, name, after=None):
    n = len(shards)

    def body(*refs):
        ins, outs, bufs, sems = refs[:n], refs[n:2 * n], refs[2 * n:3 * n], refs[3 * n]
        x, y, _ = _place()
        copies = []
        for a in range(n):
            bufs[a][...] = ins[a][...].astype(dtypes[a])
            cp = pltpu.make_async_copy(bufs[a], outs[a].at[2 * x + y], sems.at[a])
            cp.start()
            copies.append(cp)
        for cp in copies:
            cp.wait()

    body, more_specs, more_args = _ordered(body, n, after)
    return pl.pallas_call(
        body, name=name, in_specs=[pl.BlockSpec(memory_space=pltpu.VMEM)] * n + more_specs, out_specs=[ANY] * n,
        out_shape=[jax.ShapeDtypeStruct((N_CHIPS,) + s.shape, dt) for s, dt in zip(shards, dtypes)],
        scratch_shapes=[pltpu.VMEM(s.shape, dt) for s, dt in zip(shards, dtypes)] + [pltpu.SemaphoreType.DMA((n,))],
        compiler_params=pltpu.CompilerParams(vmem_limit_bytes=V7X_VMEM_LIMIT_BYTES),
    )(*shards, *more_args)


def _chip_copy(src, dst, send, recv, flip, place):
    x, y, c = place
    return pltpu.make_async_remote_copy(src_ref=src, dst_ref=dst, send_sem=send, recv_sem=recv,
                                        device_id=(x ^ flip[0], y ^ flip[1], c), device_id_type=MESH)


def _my_part(land, place, halved):
    block = land.at[2 * place[0] + place[1]]
    if not halved:
        return block
    rows = land.shape[1] // 2
    return block.at[pl.ds(pl.multiple_of(place[2] * rows, rows), rows)]


def _gather_start(lands, after, *, name, halved=()):
    n = len(lands)

    def body(*refs):
        ins, send, recv, token = refs[:n], refs[n + 1], refs[n + 2], refs[-1]
        place = _place()
        for a in range(n):
            part = _my_part(ins[a], place, a in halved)
            for p, flip in enumerate(CHIP_PEERS):
                k = 3 * a + p
                _chip_copy(part, part, send.at[k], recv.at[k], flip, place).start()
        token[...] = jnp.zeros_like(token)

    outs = pl.pallas_call(
        body, name=name, in_specs=[HBM] * n + [ANY],
        out_specs=(SEM, SEM, *[HBM] * n, pl.BlockSpec(memory_space=pltpu.VMEM)),
        out_shape=(pltpu.SemaphoreType.DMA((3 * n,)), pltpu.SemaphoreType.DMA((3 * n,)),
                   *[pltpu.HBM(l.shape, l.dtype) for l in lands], jax.ShapeDtypeStruct((8, 128), F32)),
        input_output_aliases={a: 2 + a for a in range(n)},
        compiler_params=pltpu.CompilerParams(has_side_effects=IN_FLIGHT),
    )(*[_in_hbm(l) for l in lands], after)
    return outs[0], outs[1], list(outs[2:2 + n]), outs[-1]


def _gather_wait(send, recv, lands, after, *, name, halved=()):
    n = len(lands)

    def body(*refs):
        ins, send_ref, recv_ref = refs[:n], refs[n], refs[n + 1]
        place = _place()
        for a in range(n):
            part = _my_part(ins[a], place, a in halved)
            for p, flip in enumerate(CHIP_PEERS):
                k = 3 * a + p
                cp = _chip_copy(part, part, send_ref.at[k], recv_ref.at[k], flip, place)
                cp.wait_send()
                cp.wait_recv()

    after = list(after) if isinstance(after, (list, tuple)) else [after]
    return pl.pallas_call(
        body, name=name, in_specs=[HBM] * n + [SEM, SEM] + [ANY] * len(after), out_specs=[HBM] * n,
        out_shape=[pltpu.HBM(l.shape, l.dtype) for l in lands],
        input_output_aliases={a: a for a in range(n)},
        compiler_params=pltpu.CompilerParams(has_side_effects=IN_FLIGHT),
    )(*lands, send, recv, *after)


def _forward_to_sibling(land, *, name):
    rows = land.shape[1] // 2

    def body(land_ref, out_ref, send, recv):
        x, y, c = _place()
        copies = []
        for p, (fx, fy) in enumerate(CHIP_PEERS):
            chip = 2 * (x ^ fx) + (y ^ fy)
            mine = pl.ds(pl.multiple_of(c * rows, rows), rows)
            theirs = pl.ds(pl.multiple_of((1 - c) * rows, rows), rows)
            out = pltpu.make_async_remote_copy(
                src_ref=land_ref.at[chip].at[mine], dst_ref=out_ref.at[chip].at[mine], send_sem=send.at[p],
                recv_sem=recv.at[p], device_id=(x, y, 1 - c), device_id_type=MESH)
            out.start()
            copies.append((out, pltpu.make_async_remote_copy(
                src_ref=land_ref.at[chip].at[theirs], dst_ref=out_ref.at[chip].at[theirs], send_sem=send.at[p],
                recv_sem=recv.at[p], device_id=(x, y, 1 - c), device_id_type=MESH)))
        for out, arriving in copies:
            out.wait_send()
            arriving.wait_recv()

    return pl.pallas_call(
        body, name=name, in_specs=[ANY], out_specs=ANY, out_shape=jax.ShapeDtypeStruct(land.shape, land.dtype),
        input_output_aliases={0: 0},
        scratch_shapes=[pltpu.SemaphoreType.DMA((3,)), pltpu.SemaphoreType.DMA((3,))],
    )(land)


def _scatter_start(grad, *, name):
    def body(g_ref, land_ref, send, recv, g_thru, land_thru, token):
        place = _place()
        for p, flip in enumerate(CHIP_PEERS):
            peer_chip = 2 * (place[0] ^ flip[0]) + (place[1] ^ flip[1])
            _chip_copy(g_ref.at[peer_chip], land_ref.at[p], send.at[p], recv.at[p], flip, place).start()
        token[...] = jnp.zeros_like(token)

    land = lax.empty((3,) + grad.shape[1:], grad.dtype)
    return pl.pallas_call(
        body, name=name, in_specs=[HBM, HBM],
        out_specs=(SEM, SEM, HBM, HBM, pl.BlockSpec(memory_space=pltpu.VMEM)),
        out_shape=(pltpu.SemaphoreType.DMA((3,)), pltpu.SemaphoreType.DMA((3,)), pltpu.HBM(grad.shape, grad.dtype),
                   pltpu.HBM(land.shape, land.dtype), jax.ShapeDtypeStruct((8, 128), F32)),
        input_output_aliases={0: 2, 1: 3},
        compiler_params=pltpu.CompilerParams(has_side_effects=IN_FLIGHT),
    )(_in_hbm(grad), _in_hbm(land))


def _scatter_wait(started, after, *, name):
    n = len(started)

    def body(*refs):
        grads, lands = refs[:n], refs[n:2 * n]
        sends, recvs = refs[2 * n:3 * n], refs[3 * n:4 * n]
        place = _place()
        for a in range(n):
            for p, flip in enumerate(CHIP_PEERS):
                cp = _chip_copy(grads[a].at[0], lands[a].at[p], sends[a].at[p], recvs[a].at[p], flip, place)
                cp.wait_send()
                cp.wait_recv()

    grads, lands = [s[2] for s in started], [s[3] for s in started]
    after = list(after) if isinstance(after, (list, tuple)) else [after]
    outs = pl.pallas_call(
        body, name=name, in_specs=[HBM] * (2 * n) + [SEM] * (2 * n) + [ANY] * len(after), out_specs=[HBM] * (2 * n),
        out_shape=[pltpu.HBM(a.shape, a.dtype) for a in grads + lands],
        input_output_aliases={a: a for a in range(2 * n)},
        compiler_params=pltpu.CompilerParams(has_side_effects=IN_FLIGHT),
    )(*grads, *lands, *[s[0] for s in started], *[s[1] for s in started], *after)
    return list(zip(outs[:n], outs[n:]))


def _sibling_copy(src, dst, send, recv, place):
    x, y, c = place
    return pltpu.make_async_remote_copy(src_ref=src, dst_ref=dst, send_sem=send, recv_sem=recv,
                                        device_id=(x, y, 1 - c), device_id_type=MESH)


def _swap_start(arrays, *, name):
    n = len(arrays)

    def body(*refs):
        ins, lands, send, recv, token = refs[:n], refs[n:2 * n], refs[2 * n], refs[2 * n + 1], refs[-1]
        place = _place()
        for a in range(n):
            _sibling_copy(ins[a], lands[a], send.at[a], recv.at[a], place).start()
        token[...] = jnp.zeros_like(token)

    both = [_in_hbm(a) for a in arrays] + [_in_hbm(lax.empty(a.shape, a.dtype)) for a in arrays]
    outs = pl.pallas_call(
        body, name=name, in_specs=[HBM] * (2 * n),
        out_specs=(SEM, SEM, *[HBM] * (2 * n), pl.BlockSpec(memory_space=pltpu.VMEM)),
        out_shape=(pltpu.SemaphoreType.DMA((n,)), pltpu.SemaphoreType.DMA((n,)),
                   *[pltpu.HBM(a.shape, a.dtype) for a in both], jax.ShapeDtypeStruct((8, 128), F32)),
        input_output_aliases={a: 2 + a for a in range(2 * n)},
        compiler_params=pltpu.CompilerParams(has_side_effects=IN_FLIGHT),
    )(*both)
    return outs[0], outs[1], list(outs[2:2 + n]), list(outs[2 + n:2 + 2 * n]), outs[-1]


def _swap_wait(started, after, *, name):
    send, recv, arrays, lands = started[:4]
    n = len(arrays)

    def body(*refs):
        ins, zones, send_ref, recv_ref = refs[:n], refs[n:2 * n], refs[2 * n], refs[2 * n + 1]
        place = _place()
        for a in range(n):
            cp = _sibling_copy(ins[a], zones[a], send_ref.at[a], recv_ref.at[a], place)
            cp.wait_send()
            cp.wait_recv()

    after = list(after) if isinstance(after, (list, tuple)) else [after]
    outs = pl.pallas_call(
        body, name=name, in_specs=[HBM] * (2 * n) + [SEM, SEM] + [ANY] * len(after), out_specs=[HBM] * (2 * n),
        out_shape=[pltpu.HBM(a.shape, a.dtype) for a in arrays + lands],
        input_output_aliases={a: a for a in range(2 * n)},
        compiler_params=pltpu.CompilerParams(has_side_effects=IN_FLIGHT),
    )(*arrays, *lands, send, recv, *after)
    return list(outs[:n]), list(outs[n:])


def _allreduce_start(packed, *, name):
    n_dev = 8

    def body(src_ref, land_ref, send, recv, src_thru, land_thru, token):
        x, y, c = _place()
        me = 4 * x + 2 * y + c
        for p in range(1, n_dev):
            pltpu.make_async_remote_copy(
                src_ref=src_ref, dst_ref=land_ref.at[me], send_sem=send.at[p - 1], recv_sem=recv.at[p - 1],
                device_id=(x ^ (p >> 2), y ^ ((p >> 1) & 1), c ^ (p & 1)), device_id_type=MESH).start()
        token[...] = jnp.zeros_like(token)

    land = lax.empty((n_dev,) + packed.shape, packed.dtype)
    return pl.pallas_call(
        body, name=name, in_specs=[HBM, HBM],
        out_specs=(SEM, SEM, HBM, HBM, pl.BlockSpec(memory_space=pltpu.VMEM)),
        out_shape=(pltpu.SemaphoreType.DMA((n_dev - 1,)), pltpu.SemaphoreType.DMA((n_dev - 1,)),
                   pltpu.HBM(packed.shape, packed.dtype), pltpu.HBM(land.shape, land.dtype),
                   jax.ShapeDtypeStruct((8, 128), F32)),
        input_output_aliases={0: 2, 1: 3},
        compiler_params=pltpu.CompilerParams(has_side_effects=IN_FLIGHT),
    )(_in_hbm(packed), _in_hbm(land))


def _allreduce_wait(started, after, *, name):
    send, recv, packed, land = started[:4]
    n_dev = 8

    def body(src_ref, land_ref, send_ref, recv_ref, *_):
        x, y, c = _place()
        for p in range(1, n_dev):
            cp = pltpu.make_async_remote_copy(
                src_ref=src_ref, dst_ref=land_ref.at[0], send_sem=send_ref.at[p - 1], recv_sem=recv_ref.at[p - 1],
                device_id=(x ^ (p >> 2), y ^ ((p >> 1) & 1), c ^ (p & 1)), device_id_type=MESH)
            cp.wait_send()
            cp.wait_recv()

    after = list(after) if isinstance(after, (list, tuple)) else [after]
    return pl.pallas_call(
        body, name=name, in_specs=[HBM, HBM, SEM, SEM] + [ANY] * len(after), out_specs=[HBM, HBM],
        out_shape=[pltpu.HBM(packed.shape, packed.dtype), pltpu.HBM(land.shape, land.dtype)],
        input_output_aliases={0: 0, 1: 1},
        compiler_params=pltpu.CompilerParams(has_side_effects=IN_FLIGHT),
    )(packed, land, send, recv, *after)


def _sum_devices(mine, land, *, name):
    n_dev = land.shape[0]

    def body(mine_ref, land_ref, out_ref):
        x, y, c = _place()
        me = 4 * x + 2 * y + c
        total = None
        for s in range(n_dev):
            part = jnp.where(me == s, mine_ref[...], land_ref[s])
            total = part if total is None else total + part
        out_ref[...] = total

    return pl.pallas_call(body, name=name, out_shape=jax.ShapeDtypeStruct(mine.shape, mine.dtype))(mine, land)


def _sum_received(grad, land, *, name, tr=256):
    _, r, c = grad.shape
    tr = _pick(r, tr, 8)

    def body(chip_ref, g_ref, l_ref, o_ref):
        o_ref[...] = ((g_ref[...] + l_ref[0].astype(F32)) + l_ref[1].astype(F32)) + l_ref[2].astype(F32)

    chip = (2 * lax.axis_index("x") + lax.axis_index("y")).astype(jnp.int32).reshape(1)
    return pl.pallas_call(
        body, name=name,
        grid_spec=pltpu.PrefetchScalarGridSpec(
            num_scalar_prefetch=1, grid=(r // tr,),
            in_specs=[pl.BlockSpec((None, tr, c), lambda i, chip_ref: (chip_ref[0], i, 0)),
                      pl.BlockSpec((3, tr, c), lambda i, chip_ref: (0, i, 0))],
            out_specs=pl.BlockSpec((tr, c), lambda i, chip_ref: (i, 0))),
        out_shape=jax.ShapeDtypeStruct((r, c), F32), compiler_params=_params("parallel"),
    )(chip, grad, land)


def _allreduce_small(packed, *, name, after=None):
    r, d = packed.shape
    n_dev = 8

    def body(src_ref, out_ref, buf, send, recv):
        x, y, c = _place()
        me = 4 * x + 2 * y + c
        started = []
        for p in range(1, n_dev):
            rc = pltpu.make_async_remote_copy(
                src_ref=src_ref, dst_ref=buf.at[me], send_sem=send.at[p - 1], recv_sem=recv.at[p - 1],
                device_id=(x ^ (p >> 2), y ^ ((p >> 1) & 1), c ^ (p & 1)), device_id_type=MESH)
            rc.start()
            started.append(rc)
        buf[me] = src_ref[...]
        for rc in started:
            rc.wait()
        total = buf[0]
        for s in range(1, n_dev):
            total = total + buf[s]
        out_ref[...] = total

    vmem = pl.BlockSpec(memory_space=pltpu.VMEM)
    body, more_specs, more_args = _ordered(body, 1, after)
    return pl.pallas_call(
        body, name=name, in_specs=[vmem] + more_specs, out_specs=vmem, out_shape=jax.ShapeDtypeStruct((r, d), F32),
        scratch_shapes=[pltpu.VMEM((n_dev, r, d), F32), pltpu.SemaphoreType.DMA((n_dev - 1,)),
                        pltpu.SemaphoreType.DMA((n_dev - 1,))],
    )(packed, *more_args)


def _packed_rows(size, d):
    return -(-size // (8 * d)) * 8


def _pack_rows(arrays, d):
    rows = []
    for arr in arrays:
        flat = arr.reshape(-1).astype(F32)
        n = _packed_rows(flat.shape[0], d)
        rows.append(jnp.pad(flat, (0, n * d - flat.shape[0])).reshape(n, d))
    return jnp.concatenate(rows, axis=0)


def _unpack_rows(packed, shapes, d):
    out, row = [], 0
    for shape in shapes:
        size = math.prod(shape)
        n = _packed_rows(size, d)
        out.append(packed[row:row + n].reshape(-1)[:size].reshape(shape))
        row += n
    return out


SMALL = ("norm1_g", "gate_b", "conv_b", "conv_norm_g", "q_norm_g", "k_norm_g", "norm2_g", "ffn_conv_b")
LARGE = ("w_in", "w_conv_out", "w_attn_out", "w_out", "w_up", "w_down")
WEIGHTS = ("norm1_g", "w_in", "gate_b", "conv_w", "conv_b", "conv_norm_g", "w_conv_out", "q_norm_g", "k_norm_g",
           "w_attn_out", "w_out", "norm2_g", "w_up", "ffn_conv_w", "ffn_conv_b", "w_down")


def _head_ones(dims):
    a = dims.n_heads * dims.head_dim
    head = jnp.arange(a, dtype=jnp.int32) // dims.head_dim
    return (head[:, None] == head[None, :]).astype(BF16)


def _after(vec, token):
    return vec if token is None else vec + token[0:1, 0:1]


def _local_step(dims, x, target, small, first_weights, other_weights, send_grad):
    d, f, heads = dims.d_model, dims.d_ff, dims.n_heads
    small = dict(small)
    row = lambda name: small[name].reshape(1, -1)
    head_sum = _head_sum_matrix(dims)
    head_spread = jnp.transpose(head_sum)
    ones = (head_sum, head_spread)
    gq = jnp.tile(row("q_norm_g"), (1, heads))
    gk = jnp.tile(row("k_norm_g"), (1, heads))
    one_shard = lambda w: w.reshape(1, -1, w.shape[-1])

    h = _rmsnorm_fwd(x, row("norm1_g"), name="norm1")
    full = first_weights(h)
    w_in = full["w_in"]
    conv_w = jnp.pad(full["conv_w"], ((0, CONV_HALO - dims.conv_width), (0, 0)))
    ffn_w = jnp.pad(full["ffn_conv_w"], ((0, FFN_HALO - dims.ffn_conv_width), (0, 0)))
    z = _mm_nn(h, w_in, out_dtype=BF16, after=full.get("token"), tm=2048, tn=1792, name="in_proj")
    a1, a3 = _conv_branch_fwd(z, conv_w, row("conv_b"), row("conv_norm_g"), dims, name="conv_branch")
    qkv = _qkv_layouts_fwd(z, gq, gk, ones, dims, name="qk_norm")
    per_group = {dil: _attn_fwd(*qkv[dil], dims, dil, name=f"attn_fwd_d{dil}") for dil in DILATIONS}
    o, lse = _attn_combine({dil: g[:2] for dil, g in per_group.items()}, head_spread, dims, name="attn_combine")
    full = other_weights(o)
    w_up = full["w_up"]
    w_co, w_ao, w_o, w_dn = (one_shard(full[k]) for k in ("w_conv_out", "w_attn_out", "w_out", "w_down"))
    ya = _mm_nn(a3, w_co, out_dtype=F32, name="conv_out_proj")
    yb = _mm_nn(o, w_ao, out_dtype=F32, name="attn_out_proj")
    mixed = _mix_fwd(ya, yb, z, row("gate_b"), dims, name="gate_mix")
    x1, h2 = _proj_residual_norm(mixed, w_o, x, row("norm2_g"), name="out_proj_norm2")
    up = _mm_nn(h2, w_up, out_dtype=F32, tm=2048, name="up_proj")
    act = _ffn_act_fwd(up, ffn_w, row("ffn_conv_b"), dims, name="ffn_act")
    dy, dy_b, loss = _proj_residual_loss(act, w_dn, x1, target, tm=512, name="down_proj_loss")

    grads = {}

    def large(name, g):
        grads[name], g_bf16 = g
        return send_grad(name, g_bf16)

    sent = large("w_down", _mm_tn(act, dy_b, n_shards=1, name="dw_down"))
    dact = _mm_nt(dy_b, w_dn, out_dtype=BF16, after=sent, name="d_act")
    dup, dfw, dfb = _ffn_bwd(dact, up, ffn_w, row("ffn_conv_b"), dims, name="ffn_bwd")
    grads["ffn_conv_w"], grads["ffn_conv_b"] = dfw[:dims.ffn_conv_width], dfb
    sent = large("w_up", _mm_tn(h2, dup, n_shards=N_CHIPS, name="dw_up"))
    dh2 = _mm_nt(dup, w_up, out_dtype=F32, after=sent, name="d_h2")
    dx1, dx1_b, grads["norm2_g"] = _rmsnorm_bwd(x1, row("norm2_g"), dh2, dy, want_bf16=True, name="norm2_bwd")
    sent = large("w_out", _mm_tn(mixed, dx1_b, n_shards=1, name="dw_out"))
    dmix = _mm_nt(dx1_b, w_o, out_dtype=F32, after=sent, name="d_mix")
    dya, dyb, dz_gate, grads["gate_b"] = _mix_bwd(dmix, ya, yb, z, row("gate_b"), dims, name="gate_mix_bwd")
    sent = large("w_attn_out", _mm_tn(o, dyb, n_shards=1, name="dw_attn_out"))
    do = _mm_nt(dyb, w_ao, out_dtype=BF16, after=sent, name="d_attn")
    dos, deltas = _attn_bwd_prep(do, o, head_sum, dims, name="attn_bwd_prep")
    dqkv = {dil: _attn_bwd(*qkv[dil], dos[dil], lse[dil], deltas[dil], per_group[dil][2], per_group[dil][1], dims, dil,
                           name=f"attn_bwd_d{dil}") for dil in DILATIONS}
    dz_qkv, dgq, dgk = _qkv_layouts_bwd(z, dqkv, gq, gk, ones, dims, name="qk_norm_bwd")
    grads["q_norm_g"] = dgq.reshape(heads, dims.head_dim).sum(axis=0)
    grads["k_norm_g"] = dgk.reshape(heads, dims.head_dim).sum(axis=0)
    sent = large("w_conv_out", _mm_tn(a3, dya, n_shards=1, name="dw_conv_out"))
    da3 = _mm_nt(dya, w_co, out_dtype=F32, after=sent, name="d_conv_act")
    da1, grads["conv_norm_g"] = _conv_norm_bwd(da3, a1, row("conv_norm_g"), name="conv_norm_bwd")
    dz, dcw, grads["conv_b"] = _conv_branch_bwd(da1, z, conv_w, [dz_qkv, dz_gate], dims, name="conv_branch_bwd")
    grads["conv_w"] = dcw[:dims.conv_width]
    sent = large("w_in", _mm_tn(h, dz, n_shards=N_CHIPS, name="dw_in"))
    dh = _mm_nt(dz, w_in, out_dtype=F32, after=sent, name="d_h")
    dx, grads["norm1_g"] = _rmsnorm_bwd(x, row("norm1_g"), dh, dx1, want_bf16=False, name="norm1_bwd")
    return loss, dx, grads


def _step(dims, x, target, w, m, v):
    d = dims.d_model
    t = dims.tokens
    sq = lambda a: a.reshape(a.shape[1:])
    w2, m2, v2 = ({k: sq(a) for k, a in grp.items()} for grp in (w, m, v))

    conv_pad = jnp.pad(w2["conv_w"], ((0, CONV_HALO - dims.conv_width), (0, 0)))
    ffn_pad = jnp.pad(w2["ffn_conv_w"], ((0, FFN_HALO - dims.ffn_conv_width), (0, 0)))
    first_names = ("w_in", "conv_w", "ffn_conv_w")
    other_names = tuple(k for k in LARGE if k not in first_names)
    lands = dict(zip(first_names, _cast_to_lands([w2["w_in"], conv_pad, ffn_pad], [BF16, F32, F32], name="cast_first")))
    first = _gather_start([lands[k] for k in first_names], x, halved=(0,), name="gather_start_first")
    lands.update(zip(other_names, _cast_to_lands([w2[k] for k in other_names], [BF16] * len(other_names),
                                                 after=first[3], name="cast_other")))
    other = []
    cols = lambda g, rows: jnp.moveaxis(g, 0, 1).reshape(g.shape[1], -1)[:rows]

    def first_weights(after):
        got = dict(zip(first_names, _gather_wait(*first[:3], [after] + [lands[k] for k in other_names], halved=(0,),
                                                 name="gather_wait_first")))
        got["w_in"] = _forward_to_sibling(got["w_in"], name="forward_w_in")
        other.extend(_gather_start([lands[k] for k in other_names], got["w_in"], name="gather_start_other"))
        got["conv_w"] = cols(got["conv_w"], dims.conv_width)
        got["ffn_conv_w"] = cols(got["ffn_conv_w"], dims.ffn_conv_width)
        got["token"] = other[3]
        return got

    def other_weights(after):
        return dict(zip(other_names, _gather_wait(*other[:3], after, name="gather_wait_other")))

    started = {}

    def send_grad(name, g):
        send, recv, g_thru, land, token = _scatter_start(g.reshape(N_CHIPS, -1, g.shape[-1]), name=f"scatter_start_{name}")
        started[name] = (send, recv, g_thru, land)
        return token

    small = {k: w2[k] for k in SMALL}
    small["norm1_g"] = _after(small["norm1_g"].reshape(1, -1), first[3])
    loss, dx, grads = _local_step(dims, x.reshape(t, d), target.reshape(t, d), small, first_weights, other_weights, send_grad)

    def my_sums(names, after, tag):
        arrived = _scatter_wait([started[k] for k in names], after, name=f"scatter_wait_{tag}")
        blocks = [grads[k].reshape(N_CHIPS, -1, grads[k].shape[-1]) for k in names]
        return [_sum_received(g, land, name=f"sum_{k}") for k, g, (_, land) in zip(names, blocks, arrived)]

    def updates(names, mine, theirs):
        return {k: _adamw(w2[k], [a, b], m2[k], v2[k], name=f"adamw_{k}") for k, a, b in zip(names, mine, theirs)}

    small_names = SMALL + ("conv_w", "ffn_conv_w")
    packed = _pack_rows([grads[k] for k in small_names] + [loss[0, 0]], d)
    reducing = _allreduce_start(packed, name="allreduce_start")
    others = [k for k in LARGE if k != "w_in"]
    mine_others = my_sums(others, [dx, reducing[4]], "others")
    swapping_others = _swap_start(mine_others, name="swap_start_others")
    mine_w_in = my_sums(["w_in"], swapping_others[4], "w_in")
    swapping_w_in = _swap_start(mine_w_in, name="swap_start_w_in")
    out = updates(others, *_swap_wait(swapping_others, swapping_w_in[4], name="swap_wait_others"))
    last_updates = [out[k][1] for k in others]
    reduced = _sum_devices(*_allreduce_wait(reducing, last_updates, name="allreduce_wait"), name="allreduce_sum")
    shapes = [grads[k].shape for k in small_names] + [()]
    *small_g, loss_total = _unpack_rows(reduced, shapes, d)
    small_g = dict(zip(small_names, small_g))
    chip = 2 * lax.axis_index("x") + lax.axis_index("y")
    for k in ("conv_w", "ffn_conv_w"):
        width = w2[k].shape[1]
        small_g[k] = lax.dynamic_slice_in_dim(small_g[k], chip * width, width, axis=1)

    small_shapes = [w2[k].shape for k in small_names]
    pack = lambda grp: _pack_rows([grp[k] for k in small_names], d)
    results = _adamw(pack(w2), [pack(small_g)], pack(m2), pack(v2), name="adamw_small")
    unpacked = [_unpack_rows(r, small_shapes, d) for r in results]
    out.update({k: tuple(u[i] for u in unpacked) for i, k in enumerate(small_names)})
    out.update(updates(["w_in"], *_swap_wait(swapping_w_in, results[1], name="swap_wait_w_in")))

    lead =lambda a: a.reshape((1,) + a.shape)
    ordered = [[lead(out[k][j].reshape(w2[k].shape)) for k in WEIGHTS] for j in range(4)]
    return (loss_total, dx.reshape(x.shape), *ordered[0], *ordered[1], *ordered[2], *ordered[3])


def kernel(x, norm1_g, w_in, gate_b, conv_w, conv_b, conv_norm_g, w_conv_out, q_norm_g, k_norm_g, w_attn_out, w_out, norm2_g, w_up, ffn_conv_w, ffn_conv_b, w_down, loss_target, m_norm1_g, m_w_in, m_gate_b, m_conv_w, m_conv_b, m_conv_norm_g, m_w_conv_out, m_q_norm_g, m_k_norm_g, m_w_attn_out, m_w_out, m_norm2_g, m_w_up, m_ffn_conv_w, m_ffn_conv_b, m_w_down, v_norm1_g, v_w_in, v_gate_b, v_conv_w, v_conv_b, v_conv_norm_g, v_w_conv_out, v_q_norm_g, v_k_norm_g, v_w_attn_out, v_w_out, v_norm2_g, v_w_up, v_ffn_conv_w, v_ffn_conv_b, v_w_down):
    w = dict(zip(WEIGHTS, (norm1_g, w_in, gate_b, conv_w, conv_b, conv_norm_g, w_conv_out, q_norm_g, k_norm_g,
                           w_attn_out, w_out, norm2_g, w_up, ffn_conv_w, ffn_conv_b, w_down)))
    m = dict(zip(WEIGHTS, (m_norm1_g, m_w_in, m_gate_b, m_conv_w, m_conv_b, m_conv_norm_g, m_w_conv_out, m_q_norm_g,
                           m_k_norm_g, m_w_attn_out, m_w_out, m_norm2_g, m_w_up, m_ffn_conv_w, m_ffn_conv_b, m_w_down)))
    v = dict(zip(WEIGHTS, (v_norm1_g, v_w_in, v_gate_b, v_conv_w, v_conv_b, v_conv_norm_g, v_w_conv_out, v_q_norm_g,
                           v_k_norm_g, v_w_attn_out, v_w_out, v_norm2_g, v_w_up, v_ffn_conv_w, v_ffn_conv_b, v_w_down)))
    dims = Dims(d_model=x.shape[-1], batch_local=x.shape[0], seq=x.shape[1], d_ff=w_down.shape[1] * N_CHIPS)
    return _step(dims, x, loss_target, w, m, v)
```

```python
import functools
import math
from typing import NamedTuple

import jax
import jax.numpy as jnp
from jax import lax
from jax.experimental import pallas as pl
from jax.experimental.pallas import tpu as pltpu

F32 = jnp.float32
BF16 = jnp.bfloat16

RMS_EPS = 1e-6
MASKED_SCORE = -1e30
ATTN_BLOCK = 128
DILATIONS = (1, 4, 16)
CONV_HALO = 32
FFN_HALO = 8
ADAM_LR, ADAM_B1, ADAM_B2, ADAM_EPS, ADAM_WD, ADAM_STEP = 0.001, 0.9, 0.999, 1e-08, 0.01, 10
V7X_VMEM_LIMIT_BYTES = 56 * 2 ** 20
N_CHIPS = 4
MESH = pl.DeviceIdType.MESH


class Dims(NamedTuple):
    d_model: int = 1024
    n_heads: int = 16
    head_dim: int = 64
    d_ff: int = 2816
    seq: int = 2048
    batch_local: int = 2
    conv_width: int = 31
    ffn_conv_width: int = 3

    @property
    def tokens(self):
        return self.seq * self.batch_local


def _params(*semantics):
    return pltpu.CompilerParams(dimension_semantics=semantics, vmem_limit_bytes=V7X_VMEM_LIMIT_BYTES)


ANY = pl.BlockSpec(memory_space=pl.ANY)


def _ordered(body, n_inputs, after):
    after = [] if after is None else list(after) if isinstance(after, (list, tuple)) else [after]
    if not after:
        return body, [], []

    def wrapped(*refs):
        return body(*refs[:n_inputs], *refs[n_inputs + len(after):])

    return wrapped, [ANY] * len(after), after


def _pick(n, target, mult=128):
    if n <= target:
        return n
    best = None
    for t in range(mult, target + 1, mult):
        if n % t == 0:
            best = t
    assert best is not None, (n, target, mult)
    return best


def _sigmoid(v):
    return 1.0 / (1.0 + jnp.exp(-v))


def _mm_nn(a, w, *, out_dtype, name, residual=None, after=None, tm=1024, tn=1408, tk=2816):
    m, k = a.shape
    nsh, k2, c = w.shape
    assert k == k2 and a.dtype == BF16 and w.dtype == BF16
    n = nsh * c
    tm, tn, tk = _pick(m, tm, 8), _pick(c, tn), _pick(k, tk)
    nk, cpn = k // tk, c // tn

    def body(*refs):
        if residual is None:
            a_ref, w_ref, o_ref, acc = refs
        else:
            a_ref, w_ref, r_ref, o_ref, acc = refs
        prod = jnp.dot(a_ref[...], w_ref[...], preferred_element_type=F32)

        def finish(total):
            if residual is not None:
                total = total + r_ref[...]
            o_ref[...] = total.astype(out_dtype)

        if nk == 1:
            finish(prod)
        else:
            kk = pl.program_id(2)

            @pl.when(kk == 0)
            def _():
                acc[...] = prod

            @pl.when(kk > 0)
            def _():
                acc[...] += prod

            @pl.when(kk == nk - 1)
            def _():
                finish(acc[...])

    in_specs = [pl.BlockSpec((tm, tk), lambda i, j, kk: (i, kk)),
                pl.BlockSpec((None, tk, tn), lambda i, j, kk: (j // cpn, kk, j % cpn))]
    args = [a, w]
    if residual is not None:
        in_specs.append(pl.BlockSpec((tm, tn), lambda i, j, kk: (i, j)))
        args.append(residual)
    body, more_specs, more_args = _ordered(body, len(args), after)
    return pl.pallas_call(
        body, name=name, grid=(m // tm, n // tn, nk),
        in_specs=in_specs + more_specs, out_specs=pl.BlockSpec((tm, tn), lambda i, j, kk: (i, j)),
        out_shape=jax.ShapeDtypeStruct((m, n), out_dtype),
        scratch_shapes=[pltpu.VMEM((tm, tn) if nk > 1 else (8, 128), F32)],
        compiler_params=_params("parallel", "parallel", "arbitrary"),
    )(*args, *more_args)


def _proj_residual_norm(a, w, residual, g, *, name, tm=1024):
    m, k = a.shape
    _, k2, n = w.shape
    assert w.shape[0] == 1 and k == k2 and a.dtype == BF16 and w.dtype == BF16
    tm = _pick(m, tm, 8)

    def body(a_ref, w_ref, r_ref, g_ref, y_ref, h_ref):
        y = r_ref[...] + jnp.dot(a_ref[...], w_ref[...], preferred_element_type=F32)
        y_ref[...] = y
        h_ref[...] = (y * lax.rsqrt(jnp.mean(y * y, axis=-1, keepdims=True) + RMS_EPS) * g_ref[...]).astype(BF16)

    rows = lambda width: pl.BlockSpec((tm, width), lambda i: (i, 0))
    return pl.pallas_call(
        body, name=name, grid=(m // tm,),
        in_specs=[rows(k), pl.BlockSpec((None, k, n), lambda i: (0, 0, 0)), rows(n), pl.BlockSpec((1, n), lambda i: (0, 0))],
        out_specs=[rows(n), rows(n)],
        out_shape=[jax.ShapeDtypeStruct((m, n), F32), jax.ShapeDtypeStruct((m, n), BF16)],
        compiler_params=_params("parallel"),
    )(a, w, residual, g)


def _proj_residual_loss(a, w, residual, target, *, name, tm=1024):
    m, k = a.shape
    _, k2, n = w.shape
    assert w.shape[0] == 1 and k == k2 and a.dtype == BF16 and w.dtype == BF16
    tm = _pick(m, tm, 8)

    def body(a_ref, w_ref, r_ref, t_ref, dy_ref, dyb_ref, loss_ref):
        err = r_ref[...] + jnp.dot(a_ref[...], w_ref[...], preferred_element_type=F32) - t_ref[...]
        dy = err * (1.0 / n)
        dy_ref[...] = dy
        dyb_ref[...] = dy.astype(BF16)
        part = jnp.sum(jnp.sum(err * err, axis=-1, keepdims=True), axis=0, keepdims=True) * (0.5 / n)
        _accumulate(loss_ref, jnp.broadcast_to(part, (8, 128)), pl.program_id(0) == 0)

    rows = lambda width: pl.BlockSpec((tm, width), lambda i: (i, 0))
    return pl.pallas_call(
        body, name=name, grid=(m // tm,),
        in_specs=[rows(k), pl.BlockSpec((None, k, n), lambda i: (0, 0, 0)), rows(n), rows(n)],
        out_specs=[rows(n), rows(n), pl.BlockSpec((8, 128), lambda i: (0, 0))],
        out_shape=[jax.ShapeDtypeStruct((m, n), F32), jax.ShapeDtypeStruct((m, n), BF16),
                   jax.ShapeDtypeStruct((8, 128), F32)],
        compiler_params=_params("arbitrary"),
    )(a, w, residual, target)


def _mm_nt(a, w, *, out_dtype, name, after=None, tm=1024, tn=1408, tk=1792):
    m, k = a.shape
    nsh, r, c = w.shape
    assert k == nsh * c and a.dtype == BF16 and w.dtype == BF16
    tm, tn, tk = _pick(m, tm, 8), _pick(r, tn), _pick(c, tk)
    nk, cpk = k // tk, c // tk

    def body(a_ref, w_ref, o_ref, acc):
        prod = lax.dot_general(a_ref[...], w_ref[...], (((1,), (1,)), ((), ())), preferred_element_type=F32)
        if nk == 1:
            o_ref[...] = prod.astype(out_dtype)
        else:
            kk = pl.program_id(2)

            @pl.when(kk == 0)
            def _():
                acc[...] = prod

            @pl.when(kk > 0)
            def _():
                acc[...] += prod

            @pl.when(kk == nk - 1)
            def _():
                o_ref[...] = acc[...].astype(out_dtype)

    body, more_specs, more_args = _ordered(body, 2, after)
    return pl.pallas_call(
        body, name=name, grid=(m // tm, r // tn, nk),
        in_specs=[pl.BlockSpec((tm, tk), lambda i, j, kk: (i, kk)),
                  pl.BlockSpec((None, tn, tk), lambda i, j, kk: (kk // cpk, j, kk % cpk))] + more_specs,
        out_specs=pl.BlockSpec((tm, tn), lambda i, j, kk: (i, j)),
        out_shape=jax.ShapeDtypeStruct((m, r), out_dtype),
        scratch_shapes=[pltpu.VMEM((tm, tn) if nk > 1 else (8, 128), F32)],
        compiler_params=_params("parallel", "parallel", "arbitrary"),
    )(a, w, *more_args)


MM_TN_VMEM_BYTES = 44 * 2 ** 20


def _mm_tn(a, b, *, n_shards, name, tm=1408, tn=1408):
    t, m = a.shape
    t2, n = b.shape
    assert t == t2 and a.dtype == BF16 and b.dtype == BF16
    c = n // n_shards
    tm, tn = _pick(m, tm), _pick(c, tn)
    if m // tm == 1 and n // tn == 1 and tn % (2 * LANES) == 0:
        tn //= 2
    fixed = 2 * tm * tn * 6
    if 4 * t * (tm + tn) + fixed <= MM_TN_VMEM_BYTES:
        tk = t
    else:
        tk = _pick(t, (MM_TN_VMEM_BYTES - fixed - 4 * tm * tn) // (4 * (tm + tn)), 8)
    nk, cpn = t // tk, c // tn

    def body(a_ref, b_ref, o_ref, ob_ref, acc):
        kk = pl.program_id(2)
        prod = lax.dot_general(a_ref[...], b_ref[...], (((0,), (0,)), ((), ())), preferred_element_type=F32)

        def finish(total):
            o_ref[...] = total
            ob_ref[...] = total.astype(BF16)

        if nk == 1:
            finish(prod)
        else:
            @pl.when(kk == 0)
            def _():
                acc[...] = prod

            @pl.when(kk > 0)
            def _():
                acc[...] += prod

            @pl.when(kk == nk - 1)
            def _():
                finish(acc[...])

    out_spec = pl.BlockSpec((None, tm, tn), lambda i, j, kk: (j // cpn, i, j % cpn))
    return pl.pallas_call(
        body, name=name, grid=(m // tm, n // tn, nk),
        in_specs=[pl.BlockSpec((tk, tm), lambda i, j, kk: (kk, i)),
                  pl.BlockSpec((tk, tn), lambda i, j, kk: (kk, j))],
        out_specs=[out_spec, out_spec],
        out_shape=[jax.ShapeDtypeStruct((n_shards, m, c), F32), jax.ShapeDtypeStruct((n_shards, m, c), BF16)],
        scratch_shapes=[pltpu.VMEM((tm, tn) if nk > 1 else (8, 128), F32)],
        compiler_params=_params("parallel", "parallel", "arbitrary"),
    )(a, b)


def _row_spec(tr, width, col=0):
    return pl.BlockSpec((tr, width), lambda i, col=col: (i, col))


def _vec_spec(width, col=0):
    return pl.BlockSpec((1, width), lambda i, col=col: (0, col))


def _accumulate(ref, value, first):
    @pl.when(first)
    def _():
        ref[...] = value

    @pl.when(jnp.logical_not(first))
    def _():
        ref[...] += value


def _rmsnorm_fwd(x, g, *, name, tr=512):
    t, d = x.shape
    tr = _pick(t, tr, 8)

    def body(x_ref, g_ref, o_ref):
        xv = x_ref[...]
        r = lax.rsqrt(jnp.mean(xv * xv, axis=-1, keepdims=True) + RMS_EPS)
        o_ref[...] = (xv * r * g_ref[...]).astype(BF16)

    return pl.pallas_call(
        body, name=name, grid=(t // tr,),
        in_specs=[_row_spec(tr, d), _vec_spec(d)], out_specs=_row_spec(tr, d),
        out_shape=jax.ShapeDtypeStruct((t, d), BF16), compiler_params=_params("parallel"),
    )(x, g)


def _rmsnorm_bwd(x, g, dy, dres, *, name, want_bf16, tr=512):
    t, d = x.shape
    tr = _pick(t, tr, 8)

    def body(x_ref, g_ref, dy_ref, dres_ref, *outs):
        dx_ref, dg_ref = outs[0], outs[-1]
        xv, dyv = x_ref[...], dy_ref[...].astype(F32)
        r = lax.rsqrt(jnp.mean(xv * xv, axis=-1, keepdims=True) + RMS_EPS)
        gy = dyv * g_ref[...]
        dx = dres_ref[...] + r * gy - xv * (r * r * r) * jnp.mean(xv * gy, axis=-1, keepdims=True)
        dx_ref[...] = dx
        if want_bf16:
            outs[1][...] = dx.astype(BF16)
        _accumulate(dg_ref, jnp.sum(dyv * xv * r, axis=0, keepdims=True), pl.program_id(0) == 0)

    out_shape = [jax.ShapeDtypeStruct((t, d), F32)]
    out_specs = [_row_spec(tr, d)]
    if want_bf16:
        out_shape.append(jax.ShapeDtypeStruct((t, d), BF16))
        out_specs.append(_row_spec(tr, d))
    out_shape.append(jax.ShapeDtypeStruct((1, d), F32))
    out_specs.append(_vec_spec(d))
    return pl.pallas_call(
        body, name=name, grid=(t // tr,),
        in_specs=[_row_spec(tr, d), _vec_spec(d), _row_spec(tr, d), _row_spec(tr, d)],
        out_specs=out_specs, out_shape=out_shape, compiler_params=_params("arbitrary"),
    )(x, g, dy, dres)


def _head_mean(v, ones_ref, head_dim):
    hi = v.astype(BF16)
    lo = (v - hi.astype(F32)).astype(BF16)
    e = ones_ref[...]
    total = jnp.dot(hi, e, preferred_element_type=F32) + jnp.dot(lo, e, preferred_element_type=F32)
    return total * (1.0 / head_dim)


def _qkv_fwd(z, gq, gk, head_ones, dims, *, name, tr=256):
    t = z.shape[0]
    a = dims.n_heads * dims.head_dim
    tr = _pick(t, tr, 8)
    q_scale = dims.head_dim ** -0.5

    def body(q_ref, k_ref, v_ref, gq_ref, gk_ref, e_ref, qo_ref, ko_ref, vo_ref):
        qv, kv = q_ref[...], k_ref[...]
        rq = lax.rsqrt(_head_mean(qv * qv, e_ref, dims.head_dim) + RMS_EPS)
        rk = lax.rsqrt(_head_mean(kv * kv, e_ref, dims.head_dim) + RMS_EPS)
        qo_ref[...] = (qv * rq * gq_ref[...] * q_scale).astype(BF16)
        ko_ref[...] = (kv * rk * gk_ref[...]).astype(BF16)
        vo_ref[...] = v_ref[...].astype(BF16)

    return pl.pallas_call(
        body, name=name, grid=(t // tr,),
        in_specs=[_row_spec(tr, a, 2), _row_spec(tr, a, 3), _row_spec(tr, a, 4), _vec_spec(a), _vec_spec(a),
                  pl.BlockSpec((a, a), lambda i: (0, 0))],
        out_specs=[_row_spec(tr, a)] * 3, out_shape=[jax.ShapeDtypeStruct((t, a), BF16)] * 3,
        compiler_params=_params("parallel"),
    )(z, z, z, gq, gk, head_ones)


def _qkv_bwd(z, dqs, dks, dvs, gq, gk, head_ones, dims, *, name, tr=256):
    t = z.shape[0]
    a = dims.n_heads * dims.head_dim
    tr = _pick(t, tr, 8)
    q_scale = dims.head_dim ** -0.5
    ng = len(dqs)

    def body(*refs):
        q_ref, k_ref = refs[:2]
        dq_refs, dk_refs, dv_refs = refs[2:2 + ng], refs[2 + ng:2 + 2 * ng], refs[2 + 2 * ng:2 + 3 * ng]
        gq_ref, gk_ref, e_ref = refs[2 + 3 * ng:5 + 3 * ng]
        dz_ref, dgq_ref, dgk_ref = refs[5 + 3 * ng:]
        first = pl.program_id(0) == 0

        def norm_bwd(x_ref, d_refs, g_ref, scale, col, dg_ref):
            xv = x_ref[...]
            dy = sum(r[...] for r in d_refs) * scale
            r = lax.rsqrt(_head_mean(xv * xv, e_ref, dims.head_dim) + RMS_EPS)
            gy = dy * g_ref[...]
            dx = r * gy - xv * (r * r * r) * _head_mean(xv * gy, e_ref, dims.head_dim)
            dz_ref[:, col * a:(col + 1) * a] = dx.astype(BF16)
            _accumulate(dg_ref, jnp.sum(dy * xv * r, axis=0, keepdims=True), first)

        norm_bwd(q_ref, dq_refs, gq_ref, q_scale, 0, dgq_ref)
        norm_bwd(k_ref, dk_refs, gk_ref, 1.0, 1, dgk_ref)
        dz_ref[:, 2 * a:3 * a] = sum(r[...] for r in dv_refs).astype(BF16)

    in_specs = ([_row_spec(tr, a, 2), _row_spec(tr, a, 3)] + [_row_spec(tr, a)] * (3 * ng)
                + [_vec_spec(a), _vec_spec(a), pl.BlockSpec((a, a), lambda i: (0, 0))])
    return pl.pallas_call(
        body, name=name, grid=(t // tr,), in_specs=in_specs,
        out_specs=[_row_spec(tr, 3 * a), _vec_spec(a), _vec_spec(a)],
        out_shape=[jax.ShapeDtypeStruct((t, 3 * a), BF16)] + [jax.ShapeDtypeStruct((1, a), F32)] * 2,
        compiler_params=_params("arbitrary"),
    )(z, z, *dqs, *dks, *dvs, gq, gk, head_ones)


CONV_ROWS = 16


def _seq_specs(dims, ts, width, halo, col, *, nxt=False):
    nst, per = dims.seq // ts, ts // halo
    last = dims.tokens // halo - 1
    cur = pl.BlockSpec((ts, width), lambda b, i: (b * nst + i, col))
    if nxt:
        edge = pl.BlockSpec((halo, width), lambda b, i: (jnp.minimum((b * nst + i + 1) * per, last), col))
    else:
        edge = pl.BlockSpec((halo, width), lambda b, i: (jnp.maximum((b * nst + i) * per - 1, 0), col))
    return cur, edge


SUBLANES = 8


def _shifted_copies(buf, shifted):
    rows = shifted.shape[1]
    for s in range(1, SUBLANES):
        shifted[s - 1] = buf[pl.ds(s, rows), :]


def _window(buf, shifted, start, size):
    a, s = divmod(start, SUBLANES)
    src = buf if s == 0 else shifted.at[s - 1]
    return src[pl.ds(SUBLANES * a, size), :]


def _conv_branch_fwd(z, w, b, g, dims, *, name, ts=128):
    t, c, kw = z.shape[0], dims.d_model, dims.conv_width
    base = CONV_HALO - (kw - 1)

    def body(av_ref, hv_ref, ag_ref, hg_ref, w_ref, b_ref, g_ref, a1_ref, a3_ref, buf, shifted):
        i = pl.program_id(1)
        buf[CONV_HALO:, :] = av_ref[...].astype(F32) * _sigmoid(ag_ref[...].astype(F32))
        buf[0:CONV_HALO, :] = jnp.where(i > 0, hv_ref[...].astype(F32) * _sigmoid(hg_ref[...].astype(F32)), 0.0)
        _shifted_copies(buf, shifted)
        for r0 in range(0, ts, CONV_ROWS):
            acc = jnp.broadcast_to(b_ref[...], (CONV_ROWS, c))
            for k in range(kw):
                acc = acc + w_ref[k:k + 1, :] * _window(buf, shifted, r0 + base + k, CONV_ROWS)
            a1_ref[r0:r0 + CONV_ROWS, :] = acc
            a2 = acc * lax.rsqrt(jnp.mean(acc * acc, axis=-1, keepdims=True) + RMS_EPS) * g_ref[...]
            a3_ref[r0:r0 + CONV_ROWS, :] = (a2 * _sigmoid(a2)).astype(BF16)

    vec = pl.BlockSpec((1, c), lambda b, i: (0, 0))
    out = pl.BlockSpec((ts, c), lambda b, i: (b * (dims.seq // ts) + i, 0))
    return pl.pallas_call(
        body, name=name, grid=(dims.batch_local, dims.seq // ts),
        in_specs=[*_seq_specs(dims, ts, c, CONV_HALO, 0), *_seq_specs(dims, ts, c, CONV_HALO, 1),
                  pl.BlockSpec((CONV_HALO, c), lambda b, i: (0, 0)), vec, vec],
        out_specs=[out, out],
        out_shape=[jax.ShapeDtypeStruct((t, c), F32), jax.ShapeDtypeStruct((t, c), BF16)],
        scratch_shapes=[pltpu.VMEM((CONV_HALO + ts, c), F32),
                        pltpu.VMEM((SUBLANES - 1, CONV_HALO + ts - SUBLANES, c), F32)],
        compiler_params=_params("parallel", "parallel"),
    )(z, z, z, z, w, b, g)


def _conv_norm_bwd(da3, a1, g, *, name, tr=256):
    t, c = a1.shape
    tr = _pick(t, tr, 8)

    def body(d_ref, a_ref, g_ref, o_ref, dg_ref):
        a1v, gv = a_ref[...], g_ref[...]
        r = lax.rsqrt(jnp.mean(a1v * a1v, axis=-1, keepdims=True) + RMS_EPS)
        a2 = a1v * r * gv
        sg = _sigmoid(a2)
        da2 = d_ref[...].astype(F32) * sg * (1.0 + a2 * (1.0 - sg))
        gy = da2 * gv
        o_ref[...] = r * gy - a1v * (r * r * r) * jnp.mean(a1v * gy, axis=-1, keepdims=True)
        _accumulate(dg_ref, jnp.sum(da2 * a1v * r, axis=0, keepdims=True), pl.program_id(0) == 0)

    return pl.pallas_call(
        body, name=name, grid=(t // tr,),
        in_specs=[_row_spec(tr, c), _row_spec(tr, c), _vec_spec(c)],
        out_specs=[_row_spec(tr, c), _vec_spec(c)],
        out_shape=[jax.ShapeDtypeStruct((t, c), F32), jax.ShapeDtypeStruct((1, c), F32)],
        compiler_params=_params("arbitrary"),
    )(da3, a1, g)


def _conv_branch_bwd(da1, z, w, rest_of_dz, dims, *, name, ts=128):
    t, c, kw = z.shape[0], dims.d_model, dims.conv_width
    nst = dims.seq // ts
    base = CONV_HALO - (kw - 1)
    n_rest = len(rest_of_dz)
    total = 2 * c + sum(r.shape[1] for r in rest_of_dz)

    def body(d_ref, dn_ref, av_ref, hv_ref, ag_ref, hg_ref, w_ref, *more):
        rest_refs = more[:n_rest]
        dz_ref, dw_ref, db_ref, abuf, dbuf, ashift, dshift = more[n_rest:]
        col = 2 * c
        for r in rest_refs:
            dz_ref[:, col:col + r.shape[1]] = r[...]
            col += r.shape[1]
        i = pl.program_id(1)
        first = jnp.logical_and(pl.program_id(0) == 0, i == 0)
        abuf[CONV_HALO:, :] = av_ref[...].astype(F32) * _sigmoid(ag_ref[...].astype(F32))
        abuf[0:CONV_HALO, :] = jnp.where(i > 0, hv_ref[...].astype(F32) * _sigmoid(hg_ref[...].astype(F32)), 0.0)
        d1 = d_ref[...]
        dbuf[0:ts, :] = d1
        dbuf[ts:, :] = jnp.where(i < nst - 1, dn_ref[...], 0.0)
        _shifted_copies(abuf, ashift)
        _shifted_copies(dbuf, dshift)

        @pl.when(first)
        def _():
            dw_ref[...] = jnp.zeros_like(dw_ref)
            db_ref[...] = jnp.zeros_like(db_ref)

        db_ref[...] += jnp.sum(d1, axis=0, keepdims=True)
        for k in range(kw):
            dw_ref[k:k + 1, :] += jnp.sum(d1 * _window(abuf, ashift, base + k, ts), axis=0, keepdims=True)
        for r0 in range(0, ts, CONV_ROWS):
            acc = jnp.zeros((CONV_ROWS, c), F32)
            for k in range(kw):
                acc = acc + w_ref[k:k + 1, :] * _window(dbuf, dshift, r0 + (kw - 1) - k, CONV_ROWS)
            av = av_ref[r0:r0 + CONV_ROWS, :].astype(F32)
            sg = _sigmoid(ag_ref[r0:r0 + CONV_ROWS, :].astype(F32))
            dz_ref[r0:r0 + CONV_ROWS, 0:c] = (acc * sg).astype(BF16)
            dz_ref[r0:r0 + CONV_ROWS, c:2 * c] = (acc * av * sg * (1.0 - sg)).astype(BF16)

    cur, nxt = _seq_specs(dims, ts, c, CONV_HALO, 0, nxt=True)
    return pl.pallas_call(
        body, name=name, grid=(dims.batch_local, nst),
        in_specs=[cur, nxt, *_seq_specs(dims, ts, c, CONV_HALO, 0), *_seq_specs(dims, ts, c, CONV_HALO, 1),
                  pl.BlockSpec((CONV_HALO, c), lambda b, i: (0, 0))]
        + [pl.BlockSpec((ts, r.shape[1]), lambda b, i: (b * nst + i, 0)) for r in rest_of_dz],
        out_specs=[pl.BlockSpec((ts, total), lambda b, i: (b * nst + i, 0)),
                   pl.BlockSpec((CONV_HALO, c), lambda b, i: (0, 0)), pl.BlockSpec((1, c), lambda b, i: (0, 0))],
        out_shape=[jax.ShapeDtypeStruct((t, total), BF16), jax.ShapeDtypeStruct((CONV_HALO, c), F32),
                   jax.ShapeDtypeStruct((1, c), F32)],
        scratch_shapes=[pltpu.VMEM((CONV_HALO + ts, c), F32)] * 2
        + [pltpu.VMEM((SUBLANES - 1, CONV_HALO + ts - SUBLANES, c), F32)] * 2,
        compiler_params=_params("arbitrary", "arbitrary"),
    )(da1, da1, z, z, z, z, w, *rest_of_dz)


FFN_ROWS = 16
FFN_COLS = 256


def _ffn_chunks(ts, f):
    cw = _pick(f, FFN_COLS)
    return [(r0, c0, cw) for r0 in range(0, ts, FFN_ROWS) for c0 in range(0, f, cw)]


def _tap_sources(buf, moved, offsets, rows):
    taps, used = [], 0
    for off in offsets:
        if off % SUBLANES:
            moved[used] = buf[pl.ds(off, rows), :]
            taps.append((moved.at[used], 0))
            used += 1
        else:
            taps.append((buf, off))
    return taps


def _moved_copies(offsets):
    return sum(1 for off in offsets if off % SUBLANES)


def _taps_sum(taps, w_ref, init, r0, cols):
    for k, (src, off) in enumerate(taps):
        init = init + w_ref[k:k + 1, cols] * src[pl.ds(off + r0, init.shape[0]), cols]
    return init


def _ffn_bwd(dact, up, w, b, dims, *, name, ts=128):
    t, f, kw = up.shape[0], dims.d_ff, dims.ffn_conv_width
    nst = dims.seq // ts
    fwd_offsets = [FFN_HALO - (kw - 1) + k for k in range(kw)]
    bwd_offsets = [(kw - 1) - k for k in range(kw)]
    dact_halo = 2 * FFN_HALO

    def body(d_ref, dn_ref, up_ref, hp_ref, hn_ref, w_ref, b_ref, o_ref, dw_ref, db_ref, buf, moved, dbuf, dmoved):
        i = pl.program_id(1)
        first = jnp.logical_and(pl.program_id(0) == 0, i == 0)
        more = i < nst - 1
        buf[0:FFN_HALO, :] = jnp.where(i > 0, hp_ref[...], 0.0)
        buf[FFN_HALO:FFN_HALO + ts, :] = up_ref[...]
        buf[FFN_HALO + ts:, :] = hn_ref[...]
        taps = _tap_sources(buf, moved, fwd_offsets, ts + FFN_HALO)

        def du_chunk(r0, rows, c0, cw, d):
            vcols, gcols = slice(c0, c0 + cw), slice(f + c0, f + c0 + cw)
            uv = _taps_sum(taps, w_ref, jnp.broadcast_to(b_ref[:, vcols], (rows, cw)), r0, vcols)
            ug = _taps_sum(taps, w_ref, jnp.broadcast_to(b_ref[:, gcols], (rows, cw)), r0, gcols)
            sg = _sigmoid(ug)
            dbuf[r0:r0 + rows, vcols] = d * ug * sg
            dbuf[r0:r0 + rows, gcols] = d * uv * sg * (1.0 + ug * (1.0 - sg))

        for r0, c0, cw in _ffn_chunks(ts, f):
            du_chunk(r0, FFN_ROWS, c0, cw, d_ref[r0:r0 + FFN_ROWS, c0:c0 + cw].astype(F32))
        for _, c0, cw in _ffn_chunks(FFN_ROWS, f):
            d_next = dn_ref[:, c0:c0 + cw].astype(F32)[0:FFN_HALO]
            du_chunk(ts, FFN_HALO, c0, cw, jnp.where(more, d_next, 0.0))

        @pl.when(first)
        def _():
            dw_ref[...] = jnp.zeros_like(dw_ref)
            db_ref[...] = jnp.zeros_like(db_ref)

        du = dbuf[0:ts, :]
        db_ref[...] += jnp.sum(du, axis=0, keepdims=True)
        for k, (src, off) in enumerate(taps):
            dw_ref[k:k + 1, :] += jnp.sum(du * src[pl.ds(off, ts), :], axis=0, keepdims=True)

        dtaps = _tap_sources(dbuf, dmoved, bwd_offsets, ts)
        for r0, c0, cw in _ffn_chunks(ts, 2 * f):
            cols = slice(c0, c0 + cw)
            o_ref[r0:r0 + FFN_ROWS, cols] = _taps_sum(dtaps, w_ref, jnp.zeros((FFN_ROWS, cw), F32), r0, cols).astype(BF16)

    up_cur, up_prev = _seq_specs(dims, ts, 2 * f, FFN_HALO, 0)
    _, up_next = _seq_specs(dims, ts, 2 * f, FFN_HALO, 0, nxt=True)
    d_cur, d_next = _seq_specs(dims, ts, f, dact_halo, 0, nxt=True)
    full = lambda rows: pl.BlockSpec((rows, 2 * f), lambda b_, i: (0, 0))
    return pl.pallas_call(
        body, name=name, grid=(dims.batch_local, nst),
        in_specs=[d_cur, d_next, up_cur, up_prev, up_next, full(FFN_HALO), full(1)],
        out_specs=[pl.BlockSpec((ts, 2 * f), lambda b_, i: (b_ * nst + i, 0)), full(FFN_HALO), full(1)],
        out_shape=[jax.ShapeDtypeStruct((t, 2 * f), BF16), jax.ShapeDtypeStruct((FFN_HALO, 2 * f), F32),
                   jax.ShapeDtypeStruct((1, 2 * f), F32)],
        scratch_shapes=[pltpu.VMEM((ts + 2 * FFN_HALO, 2 * f), F32),
                        pltpu.VMEM((_moved_copies(fwd_offsets), ts + FFN_HALO, 2 * f), F32),
                        pltpu.VMEM((ts + FFN_HALO, 2 * f), F32),
                        pltpu.VMEM((_moved_copies(bwd_offsets), ts, 2 * f), F32)],
        compiler_params=_params("arbitrary", "arbitrary"),
    )(dact, dact, up, up, up, w, b)


def _ffn_act_fwd(up, w, b, dims, *, name, ts=128):
    t, f, kw = up.shape[0], dims.d_ff, dims.ffn_conv_width
    offsets = [FFN_HALO - (kw - 1) + k for k in range(kw)]

    def body(up_ref, h_ref, w_ref, b_ref, o_ref, buf, moved):
        buf[FFN_HALO:, :] = up_ref[...]
        buf[0:FFN_HALO, :] = jnp.where(pl.program_id(1) > 0, h_ref[...], 0.0)
        taps = _tap_sources(buf, moved, offsets, ts)
        for r0, c0, cw in _ffn_chunks(ts, f):
            vcols, gcols = slice(c0, c0 + cw), slice(f + c0, f + c0 + cw)
            uv = _taps_sum(taps, w_ref, jnp.broadcast_to(b_ref[:, vcols], (FFN_ROWS, cw)), r0, vcols)
            ug = _taps_sum(taps, w_ref, jnp.broadcast_to(b_ref[:, gcols], (FFN_ROWS, cw)), r0, gcols)
            o_ref[r0:r0 + FFN_ROWS, vcols] = (ug * _sigmoid(ug) * uv).astype(BF16)

    full = lambda rows: pl.BlockSpec((rows, 2 * f), lambda b_, i: (0, 0))
    return pl.pallas_call(
        body, name=name, grid=(dims.batch_local, dims.seq // ts),
        in_specs=[*_seq_specs(dims, ts, 2 * f, FFN_HALO, 0), full(FFN_HALO), full(1)],
        out_specs=pl.BlockSpec((ts, f), lambda b_, i: (b_ * (dims.seq // ts) + i, 0)),
        out_shape=jax.ShapeDtypeStruct((t, f), BF16),
        scratch_shapes=[pltpu.VMEM((FFN_HALO + ts, 2 * f), F32), pltpu.VMEM((_moved_copies(offsets), ts, 2 * f), F32)],
        compiler_params=_params("parallel", "parallel"),
    )(up, up, w, b)


def _ffn_act_bwd(dact, up, w, b, dims, *, name, ts=128):
    t, f, kw = up.shape[0], dims.d_ff, dims.ffn_conv_width
    offsets = [FFN_HALO - (kw - 1) + k for k in range(kw)]

    def body(d_ref, up_ref, h_ref, w_ref, b_ref, du_ref, dw_ref, db_ref, buf, moved):
        i = pl.program_id(1)
        first = jnp.logical_and(pl.program_id(0) == 0, i == 0)
        buf[FFN_HALO:, :] = up_ref[...]
        buf[0:FFN_HALO, :] = jnp.where(i > 0, h_ref[...], 0.0)
        taps = _tap_sources(buf, moved, offsets, ts)
        for r0, c0, cw in _ffn_chunks(ts, f):
            vcols, gcols = slice(c0, c0 + cw), slice(f + c0, f + c0 + cw)
            uv = _taps_sum(taps, w_ref, jnp.broadcast_to(b_ref[:, vcols], (FFN_ROWS, cw)), r0, vcols)
            ug = _taps_sum(taps, w_ref, jnp.broadcast_to(b_ref[:, gcols], (FFN_ROWS, cw)), r0, gcols)
            d = d_ref[r0:r0 + FFN_ROWS, vcols].astype(F32)
            sg = _sigmoid(ug)
            du_ref[r0:r0 + FFN_ROWS, vcols] = d * ug * sg
            du_ref[r0:r0 + FFN_ROWS, gcols] = d * uv * sg * (1.0 + ug * (1.0 - sg))

        @pl.when(first)
        def _():
            dw_ref[...] = jnp.zeros_like(dw_ref)
            db_ref[...] = jnp.zeros_like(db_ref)

        du = du_ref[...]
        db_ref[...] += jnp.sum(du, axis=0, keepdims=True)
        for k, (src, off) in enumerate(taps):
            dw_ref[k:k + 1, :] += jnp.sum(du * src[pl.ds(off, ts), :], axis=0, keepdims=True)

    nst = dims.seq // ts
    n_moved = _moved_copies(offsets)
    full = lambda rows: pl.BlockSpec((rows, 2 * f), lambda b_, i: (0, 0))
    return pl.pallas_call(
        body, name=name, grid=(dims.batch_local, nst),
        in_specs=[pl.BlockSpec((ts, f), lambda b_, i: (b_ * nst + i, 0)),
                  *_seq_specs(dims, ts, 2 * f, FFN_HALO, 0), full(FFN_HALO), full(1)],
        out_specs=[pl.BlockSpec((ts, 2 * f), lambda b_, i: (b_ * nst + i, 0)), full(FFN_HALO), full(1)],
        out_shape=[jax.ShapeDtypeStruct((t, 2 * f), F32), jax.ShapeDtypeStruct((FFN_HALO, 2 * f), F32),
                   jax.ShapeDtypeStruct((1, 2 * f), F32)],
        scratch_shapes=[pltpu.VMEM((FFN_HALO + ts, 2 * f), F32), pltpu.VMEM((n_moved, ts, 2 * f), F32)],
        compiler_params=_params("arbitrary", "arbitrary"),
    )(dact, up, up, w, b)


def _ffn_conv_bwd(du, w, dims, *, name, ts=128):
    t, f2 = du.shape
    kw = dims.ffn_conv_width
    nst = dims.seq // ts

    offsets = [(kw - 1) - k for k in range(kw)]

    def body(d_ref, dn_ref, w_ref, o_ref, buf, moved):
        buf[0:ts, :] = d_ref[...]
        buf[ts:, :] = jnp.where(pl.program_id(1) < nst - 1, dn_ref[...], 0.0)
        taps = _tap_sources(buf, moved, offsets, ts)
        for r0, c0, cw in _ffn_chunks(ts, f2):
            cols = slice(c0, c0 + cw)
            o_ref[r0:r0 + FFN_ROWS, cols] = _taps_sum(taps, w_ref, jnp.zeros((FFN_ROWS, cw), F32), r0, cols).astype(BF16)

    return pl.pallas_call(
        body, name=name, grid=(dims.batch_local, nst),
        in_specs=[*_seq_specs(dims, ts, f2, FFN_HALO, 0, nxt=True), pl.BlockSpec((FFN_HALO, f2), lambda b_, i: (0, 0))],
        out_specs=pl.BlockSpec((ts, f2), lambda b_, i: (b_ * nst + i, 0)),
        out_shape=jax.ShapeDtypeStruct((t, f2), BF16),
        scratch_shapes=[pltpu.VMEM((ts + FFN_HALO, f2), F32), pltpu.VMEM((_moved_copies(offsets), ts, f2), F32)],
        compiler_params=_params("parallel", "parallel"),
    )(du, du, w)


def _alibi_slope(h, n_heads):
    return 2.0 ** (-8.0 * (h + 1) / n_heads)


def _dot_nt(a, b):
    return lax.dot_general(a, b, (((1,), (1,)), ((), ())), preferred_element_type=F32)


def _dot_tn(a, b):
    return lax.dot_general(a, b, (((0,), (0,)), ((), ())), preferred_element_type=F32)


def _attn_view(x, dims, dil):
    return x.reshape(dims.batch_local, dims.seq // dil, dil * x.shape[-1])


def _attn_fwd_group(q, k, v, state, dims, dil, *, last, name):
    t, a = q.shape
    assert 2 * dims.head_dim == 128 and dims.n_heads % 2 == 0
    blk, hd = ATTN_BLOCK, dims.head_dim
    nb = dims.seq // dil // blk
    has_prev = nb > 1
    nkeys = 2 * blk if has_prev else blk

    def body(*refs):
        it = iter(refs)
        q_ref, kc_ref, vc_ref = next(it), next(it), next(it)
        kp_ref, vp_ref = (next(it), next(it)) if has_prev else (None, None)
        m_in, l_in, acc_in = (next(it), next(it), next(it)) if state is not None else (None, None, None)
        outs = list(it)
        iq = lax.broadcasted_iota(jnp.int32, (blk, nkeys), 0)
        jk = lax.broadcasted_iota(jnp.int32, (blk, nkeys), 1)
        if has_prev:
            steps = iq + blk - jk
            valid = (steps >= 0) & (steps <= blk) & ((jk >= blk) | (pl.program_id(2) > 0))
        else:
            steps = iq - jk
            valid = steps >= 0
        dist = steps.astype(F32) * float(dil)
        low = lax.broadcasted_iota(jnp.int32, (blk, 2 * hd), 1) < hd
        for hp in range(dims.n_heads // 2):
            sl = slice(2 * hd * hp, 2 * hd * (hp + 1))
            q2 = q_ref[:, sl]
            if has_prev:
                kcat = jnp.concatenate([kp_ref[:, sl], kc_ref[:, sl]], axis=0)
                vcat = jnp.concatenate([vp_ref[:, sl], vc_ref[:, sl]], axis=0)
            else:
                kcat, vcat = kc_ref[:, sl], vc_ref[:, sl]
            halves = []
            for half in range(2):
                col = 2 * hd * hp + hd * half
                qh = jnp.where(low if half == 0 else jnp.logical_not(low), q2, jnp.zeros_like(q2))
                sc = _dot_nt(qh, kcat) - _alibi_slope(2 * hp + half, dims.n_heads) * dist
                sc = jnp.where(valid, sc, MASKED_SCORE)
                row_max = jnp.max(sc, axis=-1, keepdims=True)
                if state is None:
                    m_new = row_max
                    p = jnp.exp(sc - m_new)
                    alpha = None
                    l_new = jnp.sum(p, axis=-1, keepdims=True)
                else:
                    m_old = m_in[:, col:col + 1]
                    m_new = jnp.maximum(m_old, row_max)
                    p = jnp.exp(sc - m_new)
                    alpha = jnp.exp(m_old - m_new)
                    l_new = alpha * l_in[:, col:col + 1] + jnp.sum(p, axis=-1, keepdims=True)
                pv = jnp.dot(p.astype(BF16), vcat, preferred_element_type=F32)
                halves.append((m_new, l_new, alpha, pv))
            (m_a, l_a, al_a, pv_a), (m_b, l_b, al_b, pv_b) = halves
            if state is None:
                acc = jnp.where(low, pv_a, pv_b)
            else:
                old = acc_in[:, sl]
                acc = jnp.where(low, al_a * old + pv_a, al_b * old + pv_b)
            m2 = jnp.where(low, m_a, m_b)
            l2 = jnp.where(low, l_a, l_b)
            if last:
                outs[0][:, sl] = (acc / l2).astype(BF16)
                outs[1][:, sl] = m2 + jnp.log(l2)
            else:
                outs[0][:, sl] = m2
                outs[1][:, sl] = l2
                outs[2][:, sl] = acc

    cur = pl.BlockSpec((None, blk, a), lambda b, r, i: (b, i, r))
    prev = pl.BlockSpec((None, blk, a), lambda b, r, i: (b, jnp.maximum(i - 1, 0), r))
    args, in_specs = [q, k, v], [cur, cur, cur]
    if has_prev:
        args += [k, v]
        in_specs += [prev, prev]
    if state is not None:
        args += list(state)
        in_specs += [cur] * 3
    shape = lambda dt: jax.ShapeDtypeStruct((dims.batch_local, dims.seq // dil, dil * a), dt)
    out_shape = [shape(BF16), shape(F32)] if last else [shape(F32)] * 3
    outs = pl.pallas_call(
        body, name=name, grid=(dims.batch_local, dil, nb),
        in_specs=in_specs, out_specs=[cur] * len(out_shape), out_shape=out_shape,
        compiler_params=_params("parallel", "parallel", "parallel"),
    )(*[_attn_view(x, dims, dil) for x in args])
    return tuple(o.reshape(t, a) for o in outs)


def _attn_delta(do, o, head_ones, dims, *, name, tr=512):
    t, a = o.shape
    tr = _pick(t, tr, 8)

    def body(do_ref, o_ref, e_ref, d_ref):
        prod = do_ref[...].astype(F32) * o_ref[...].astype(F32)
        d_ref[...] = _head_mean(prod, e_ref, dims.head_dim) * float(dims.head_dim)

    return pl.pallas_call(
        body, name=name, grid=(t // tr,),
        in_specs=[_row_spec(tr, a), _row_spec(tr, a), pl.BlockSpec((a, a), lambda i: (0, 0))],
        out_specs=_row_spec(tr, a), out_shape=jax.ShapeDtypeStruct((t, a), F32),
        compiler_params=_params("parallel"),
    )(do, o, head_ones)


def _attn_bwd_group(q, k, v, do, lse, delta, dims, dil, *, name):
    t, a = q.shape
    blk, hd = ATTN_BLOCK, dims.head_dim
    nb = dims.seq // dil // blk
    has_next = nb > 1

    def body(*refs):
        k_ref, v_ref, q_ref, do_ref, lse_ref, dl_ref = refs[:6]
        if has_next:
            qn_ref, don_ref, lsen_ref, dln_ref = refs[6:10]
            dq_ref, dk_ref, dv_ref, carry = refs[10:]
        else:
            dq_ref, dk_ref, dv_ref = refs[6:]
        j = pl.program_id(2)
        iq = lax.broadcasted_iota(jnp.int32, (blk, blk), 0)
        jk = lax.broadcasted_iota(jnp.int32, (blk, blk), 1)
        low = lax.broadcasted_iota(jnp.int32, (blk, 2 * hd), 1) < hd

        def pair(hp, qr, dor, lser, dlr, steps, valid):
            sl = slice(2 * hd * hp, 2 * hd * (hp + 1))
            q2, do2, k2, v2 = qr[:, sl], dor[:, sl], k_ref[:, sl], v_ref[:, sl]
            dist = steps.astype(F32) * float(dil)
            dq_h, dk2, dv2 = [], None, None
            for half in range(2):
                col = 2 * hd * hp + hd * half
                mask = low if half == 0 else jnp.logical_not(low)
                qh = jnp.where(mask, q2, jnp.zeros_like(q2))
                doh = jnp.where(mask, do2, jnp.zeros_like(do2))
                sc = _dot_nt(qh, k2) - _alibi_slope(2 * hp + half, dims.n_heads) * dist
                p = jnp.where(valid, jnp.exp(sc - lser[:, col:col + 1]), 0.0)
                ds = p * (_dot_nt(doh, v2) - dlr[:, col:col + 1])
                ds_b, p_b = ds.astype(BF16), p.astype(BF16)
                dq_h.append(jnp.dot(ds_b, k2, preferred_element_type=F32))
                dk_h, dv_h = _dot_tn(ds_b, qh), _dot_tn(p_b, doh)
                dk2 = dk_h if dk2 is None else dk2 + dk_h
                dv2 = dv_h if dv2 is None else dv2 + dv_h
            return sl, jnp.where(low, dq_h[0], dq_h[1]), dk2, dv2

        if has_next:
            @pl.when(j == 0)
            def _():
                carry[...] = jnp.zeros_like(carry)

        for hp in range(dims.n_heads // 2):
            sl, dq2, dk2, dv2 = pair(hp, q_ref, do_ref, lse_ref, dl_ref, iq - jk, iq >= jk)
            dq_ref[:, sl] = (carry[:, sl] + dq2) if has_next else dq2
            dk_ref[:, sl] = dk2
            dv_ref[:, sl] = dv2

        if has_next:
            @pl.when(j + 1 < nb)
            def _():
                for hp in range(dims.n_heads // 2):
                    sl, dq2, dk2, dv2 = pair(hp, qn_ref, don_ref, lsen_ref, dln_ref, iq - jk + blk, jk >= iq)
                    carry[:, sl] = dq2
                    dk_ref[:, sl] += dk2
                    dv_ref[:, sl] += dv2

    cur = pl.BlockSpec((None, blk, a), lambda b, r, j: (b, j, r))
    nxt = pl.BlockSpec((None, blk, a), lambda b, r, j: (b, jnp.minimum(j + 1, nb - 1), r))
    args, in_specs = [k, v, q, do, lse, delta], [cur] * 6
    if has_next:
        args += [q, do, lse, delta]
        in_specs += [nxt] * 4
    shape = jax.ShapeDtypeStruct((dims.batch_local, dims.seq // dil, dil * a), F32)
    outs = pl.pallas_call(
        body, name=name, grid=(dims.batch_local, dil, nb),
        in_specs=in_specs, out_specs=[cur] * 3, out_shape=[shape] * 3,
        scratch_shapes=[pltpu.VMEM((blk, a), F32)] if has_next else [],
        compiler_params=_params("parallel", "parallel", "arbitrary"),
    )(*[_attn_view(x, dims, dil) for x in args])
    return tuple(o.reshape(t, a) for o in outs)


LANES = 128
MASK_BIAS = 1e30
RESIDUE_DILATIONS = tuple(d for d in DILATIONS if d > 1)


def _rows_to_residues(value, out_ref, scr, d):
    rows, width = value.shape
    for c in range(width // LANES):
        cols = slice(LANES * c, LANES * (c + 1))
        scr[c] = value[:, cols]
        for r in range(d):
            out_ref[r, :, cols] = scr[c, pl.ds(r, rows // d, stride=d), :].astype(out_ref.dtype)


def _residues_to_rows(in_ref, scr, d):
    _, n, width = in_ref.shape
    slabs = []
    for c in range(width // LANES):
        cols = slice(LANES * c, LANES * (c + 1))
        for r in range(d):
            scr[c, pl.ds(r, n, stride=d), :] = in_ref[r, :, cols].astype(F32)
        slabs.append(scr[c])
    return slabs[0] if len(slabs) == 1 else jnp.concatenate(slabs, axis=1)


def _residue_shape(dims, d, width, dtype):
    return jax.ShapeDtypeStruct((dims.batch_local, d, dims.seq // d, width), dtype)


def _residue_spec(dims, d, tr, width):
    tiles = dims.seq // tr
    return pl.BlockSpec((None, d, tr // d, width), lambda i: (i // tiles, 0, i % tiles, 0))


def _head_sum_matrix(dims):
    a = dims.n_heads * dims.head_dim
    head = jnp.arange(a, dtype=jnp.int32) // dims.head_dim
    return (head[:, None] == jnp.arange(LANES, dtype=jnp.int32)[None, :]).astype(BF16)


def _two_pass_dot(v, m):
    hi = v.astype(BF16)
    lo = (v - hi.astype(F32)).astype(BF16)
    return jnp.dot(hi, m, preferred_element_type=F32) + jnp.dot(lo, m, preferred_element_type=F32)


def _qkv_layouts_fwd(z, gq, gk, head_ones, dims, *, name, tr=256):
    t = z.shape[0]
    a = dims.n_heads * dims.head_dim
    q_scale = dims.head_dim ** -0.5
    nres = len(RESIDUE_DILATIONS)

    def body(q_ref, k_ref, v_ref, gq_ref, gk_ref, sum_ref, spread_ref, *rest):
        outs, scr = rest[:-1], rest[-1]
        qv, kv = q_ref[...].astype(F32), k_ref[...].astype(F32)
        mean = lambda val: _two_pass_dot(_two_pass_dot(val, sum_ref[...]), spread_ref[...]) * (1.0 / dims.head_dim)
        rq = lax.rsqrt(mean(qv * qv) + RMS_EPS)
        rk = lax.rsqrt(mean(kv * kv) + RMS_EPS)
        values = (qv * rq * gq_ref[...] * q_scale, kv * rk * gk_ref[...], v_ref[...].astype(F32))
        for j, val in enumerate(values):
            outs[j][...] = val.astype(BF16)
            for g, d in enumerate(RESIDUE_DILATIONS):
                _rows_to_residues(val, outs[3 * (g + 1) + j], scr, d)

    out_specs = [_row_spec(tr, a)] * 3
    out_shape = [jax.ShapeDtypeStruct((t, a), BF16)] * 3
    for d in RESIDUE_DILATIONS:
        out_specs += [_residue_spec(dims, d, tr, a)] * 3
        out_shape += [_residue_shape(dims, d, a, BF16)] * 3
    outs = pl.pallas_call(
        body, name=name, grid=(t // tr,),
        in_specs=[_row_spec(tr, a, 2), _row_spec(tr, a, 3), _row_spec(tr, a, 4), _vec_spec(a), _vec_spec(a),
                  pl.BlockSpec((a, LANES), lambda i: (0, 0)), pl.BlockSpec((LANES, a), lambda i: (0, 0))],
        out_specs=out_specs, out_shape=out_shape,
        scratch_shapes=[pltpu.VMEM((a // LANES, tr, LANES), F32)],
        compiler_params=_params("parallel"),
    )(z, z, z, gq, gk, *head_ones)
    return {d: tuple(outs[3 * g:3 * g + 3]) for g, d in enumerate((1,) + RESIDUE_DILATIONS)}


def _attn_specs(dims, dil, width):
    blk = ATTN_BLOCK
    nb = dims.seq // dil // blk
    if dil == 1:
        grid = (dims.batch_local, nb)
        at = lambda f: pl.BlockSpec((blk, width), lambda b, i: (b * nb + f(i), 0))
    else:
        grid = (dims.batch_local, dil, nb)
        at = lambda f: pl.BlockSpec((None, None, blk, width), lambda b, r, i: (b, r, f(i), 0))
    return grid, at(lambda i: i), at(lambda i: jnp.maximum(i - 1, 0)), at(lambda i: jnp.minimum(i + 1, nb - 1))


def _head_slopes(n_heads):
    h = lax.broadcasted_iota(jnp.int32, (n_heads, 1, 1), 0).astype(F32)
    return jnp.exp((h + 1.0) * (-8.0 / n_heads * math.log(2.0)))


def _pair_masks(hd):
    low = lax.broadcasted_iota(jnp.int32, (1, 2 * hd), 1) < hd
    return low, jnp.logical_not(low)


def _attn_fwd(q, k, v, dims, dil, *, name):
    a = dims.n_heads * dims.head_dim
    heads, hd, blk = dims.n_heads, dims.head_dim, ATTN_BLOCK
    assert 2 * hd == LANES and heads % 2 == 0 and heads <= LANES
    nb = dims.seq // dil // blk
    has_prev = nb > 1
    nkeys = 2 * blk if has_prev else blk
    grid, cur, prev, _ = _attn_specs(dims, dil, a)
    _, cur_stat, _, _ = _attn_specs(dims, dil, LANES)

    def body(*refs):
        if has_prev:
            q_ref, kc_ref, vc_ref, kp_ref, vp_ref, o_ref, lse_ref, s_scr, p_scr = refs
        else:
            q_ref, kc_ref, vc_ref, o_ref, lse_ref, s_scr, p_scr = refs
        low, high = _pair_masks(hd)

        def keys(cur_ref, prev_ref, sl):
            return jnp.concatenate([prev_ref[:, sl], cur_ref[:, sl]], axis=0) if has_prev else cur_ref[:, sl]

        for hp in range(heads // 2):
            sl = slice(LANES * hp, LANES * (hp + 1))
            q2 = q_ref[:, sl]
            kcat = keys(kc_ref, kp_ref if has_prev else None, sl)
            s_scr[2 * hp] = _dot_nt(jnp.where(low, q2, jnp.zeros_like(q2)), kcat)
            s_scr[2 * hp + 1] = _dot_nt(jnp.where(high, q2, jnp.zeros_like(q2)), kcat)

        iq = lax.broadcasted_iota(jnp.int32, (blk, nkeys), 0)
        jk = lax.broadcasted_iota(jnp.int32, (blk, nkeys), 1)
        if has_prev:
            steps = iq + blk - jk
            valid = (steps >= 0) & (steps <= blk) & ((jk >= blk) | (pl.program_id(len(grid) - 1) > 0))
        else:
            steps = iq - jk
            valid = steps >= 0
        bias = jnp.where(valid, steps.astype(F32) * (-float(dil)), -MASK_BIAS)
        s = s_scr[...] + _head_slopes(heads) * bias[None]
        m = jnp.max(s, axis=-1, keepdims=True)
        p = jnp.exp(s - m)
        l = jnp.sum(p, axis=-1, keepdims=True)
        p_scr[...] = p.astype(BF16)
        inv = 1.0 / l
        lse = m + jnp.log(l)

        lane = lax.broadcasted_iota(jnp.int32, (blk, LANES), 1)
        stat = jnp.zeros((blk, LANES), F32)
        for hp in range(heads // 2):
            sl = slice(LANES * hp, LANES * (hp + 1))
            vcat = keys(vc_ref, vp_ref if has_prev else None, sl)
            pv_a = jnp.dot(p_scr[2 * hp], vcat, preferred_element_type=F32) * inv[2 * hp]
            pv_b = jnp.dot(p_scr[2 * hp + 1], vcat, preferred_element_type=F32) * inv[2 * hp + 1]
            o_ref[:, sl] = jnp.where(low, pv_a, pv_b)
            stat = jnp.where(lane == 2 * hp, lse[2 * hp], stat)
            stat = jnp.where(lane == 2 * hp + 1, lse[2 * hp + 1], stat)
        lse_ref[...] = stat

    lead = q.shape[:-2]
    rows = q.shape[-2]
    o, lse = pl.pallas_call(
        body, name=name, grid=grid,
        in_specs=[cur, cur, cur] + ([prev, prev] if has_prev else []),
        out_specs=[cur, cur_stat],
        out_shape=[jax.ShapeDtypeStruct(lead + (rows, a), F32), jax.ShapeDtypeStruct(lead + (rows, LANES), F32)],
        scratch_shapes=[pltpu.VMEM((heads, blk, nkeys), F32), pltpu.VMEM((heads, blk, nkeys), BF16)],
        compiler_params=_params(*["parallel"] * len(grid)),
    )(q, k, v, *([k, v] if has_prev else []))
    return o, lse


def _attn_combine(groups, head_spread, dims, *, name, tr=256):
    t = dims.tokens
    a = dims.n_heads * dims.head_dim
    dils = tuple(groups)

    def body(*refs):
        ins = refs[:2 * len(dils)]
        x_ref = refs[2 * len(dils)]
        o_ref = refs[2 * len(dils) + 1]
        lse_refs = refs[2 * len(dils) + 2:-2]
        scr, scr_stat = refs[-2], refs[-1]
        outs, stats = [], []
        for g, d in enumerate(dils):
            if d == 1:
                outs.append(ins[2 * g][...])
                stats.append(ins[2 * g + 1][...])
            else:
                outs.append(_residues_to_rows(ins[2 * g], scr, d))
                stats.append(_residues_to_rows(ins[2 * g + 1], scr_stat, d))
        top = functools.reduce(jnp.maximum, stats)
        weights = [jnp.exp(s - top) for s in stats]
        total = functools.reduce(jnp.add, weights)
        joint = top + jnp.log(total)
        inv = 1.0 / total
        acc = None
        for w, o in zip(weights, outs):
            term = _two_pass_dot(w * inv, x_ref[...]) * o
            acc = term if acc is None else acc + term
        o_ref[...] = acc.astype(BF16)
        for g, d in enumerate(dils):
            if d == 1:
                lse_refs[g][...] = joint
            else:
                _rows_to_residues(joint, lse_refs[g], scr_stat, d)

    in_specs, args, lse_specs, lse_shapes = [], [], [], []
    for d in dils:
        if d == 1:
            in_specs += [_row_spec(tr, a), _row_spec(tr, LANES)]
            lse_specs.append(_row_spec(tr, LANES))
            lse_shapes.append(jax.ShapeDtypeStruct((t, LANES), F32))
        else:
            in_specs += [_residue_spec(dims, d, tr, a), _residue_spec(dims, d, tr, LANES)]
            lse_specs.append(_residue_spec(dims, d, tr, LANES))
            lse_shapes.append(_residue_shape(dims, d, LANES, F32))
        args += list(groups[d])
    outs = pl.pallas_call(
        body, name=name, grid=(t // tr,),
        in_specs=in_specs + [pl.BlockSpec((LANES, a), lambda i: (0, 0))],
        out_specs=[_row_spec(tr, a)] + lse_specs,
        out_shape=[jax.ShapeDtypeStruct((t, a), BF16)] + lse_shapes,
        scratch_shapes=[pltpu.VMEM((a // LANES, tr, LANES), F32), pltpu.VMEM((1, tr, LANES), F32)],
        compiler_params=_params("parallel"),
    )(*args, head_spread)
    return outs[0], dict(zip(dils, outs[1:]))


def _attn_bwd_prep(do, o, head_sum, dims, *, name, tr=256):
    t, a = o.shape

    def body(do_ref, o_ref, e_ref, *rest):
        outs, scr, scr_stat = rest[:-2], rest[-2], rest[-1]
        dov = do_ref[...].astype(F32)
        delta = _two_pass_dot(dov * o_ref[...].astype(F32), e_ref[...])
        outs[0][...] = delta
        for g, d in enumerate(RESIDUE_DILATIONS):
            _rows_to_residues(dov, outs[1 + 2 * g], scr, d)
            _rows_to_residues(delta, outs[2 + 2 * g], scr_stat, d)

    out_specs, out_shape = [_row_spec(tr, LANES)], [jax.ShapeDtypeStruct((t, LANES), F32)]
    for d in RESIDUE_DILATIONS:
        out_specs += [_residue_spec(dims, d, tr, a), _residue_spec(dims, d, tr, LANES)]
        out_shape += [_residue_shape(dims, d, a, BF16), _residue_shape(dims, d, LANES, F32)]
    outs = pl.pallas_call(
        body, name=name, grid=(t // tr,),
        in_specs=[_row_spec(tr, a), _row_spec(tr, a), pl.BlockSpec((a, LANES), lambda i: (0, 0))],
        out_specs=out_specs, out_shape=out_shape,
        scratch_shapes=[pltpu.VMEM((a // LANES, tr, LANES), F32), pltpu.VMEM((1, tr, LANES), F32)],
        compiler_params=_params("parallel"),
    )(do, o, head_sum)
    dos, deltas = {1: do}, {1: outs[0]}
    for g, d in enumerate(RESIDUE_DILATIONS):
        dos[d], deltas[d] = outs[1 + 2 * g], outs[2 + 2 * g]
    return dos, deltas


def _attn_bwd(q, k, v, do, lse, delta, dims, dil, *, name):
    a = dims.n_heads * dims.head_dim
    heads, hd, blk = dims.n_heads, dims.head_dim, ATTN_BLOCK
    nb = dims.seq // dil // blk
    has_next = nb > 1
    nq = 2 * blk if has_next else blk
    grid, cur, _, nxt = _attn_specs(dims, dil, a)
    _, cur_stat, _, nxt_stat = _attn_specs(dims, dil, LANES)

    def body(*refs):
        k_ref, v_ref, q_ref, do_ref, lse_ref, dl_ref = refs[:6]
        if has_next:
            qn_ref, don_ref, lsen_ref, dln_ref = refs[6:10]
            dq_ref, dk_ref, dv_ref, s_scr, dp_scr, p_scr, ds_scr, carry = refs[10:]
        else:
            dq_ref, dk_ref, dv_ref, s_scr, dp_scr, p_scr, ds_scr = refs[6:]
        j = pl.program_id(len(grid) - 1)
        low, high = _pair_masks(hd)

        def stacked(ref, nref, sl):
            return jnp.concatenate([ref[:, sl], nref[:, sl]], axis=0) if has_next else ref[:, sl]

        def halves(x):
            return jnp.where(low, x, jnp.zeros_like(x)), jnp.where(high, x, jnp.zeros_like(x))

        for hp in range(heads // 2):
            sl = slice(LANES * hp, LANES * (hp + 1))
            k2, v2 = k_ref[:, sl], v_ref[:, sl]
            q_a, q_b = halves(stacked(q_ref, qn_ref if has_next else None, sl))
            do_a, do_b = halves(stacked(do_ref, don_ref if has_next else None, sl))
            s_scr[2 * hp], s_scr[2 * hp + 1] = _dot_nt(q_a, k2), _dot_nt(q_b, k2)
            dp_scr[2 * hp], dp_scr[2 * hp + 1] = _dot_nt(do_a, v2), _dot_nt(do_b, v2)

        rq = lax.broadcasted_iota(jnp.int32, (nq, blk), 0)
        jk = lax.broadcasted_iota(jnp.int32, (nq, blk), 1)
        if has_next:
            iq = jnp.where(rq < blk, rq, rq - blk)
            steps = jnp.where(rq < blk, iq - jk, iq - jk + blk)
            valid = ((rq < blk) & (iq >= jk)) | ((rq >= blk) & (jk >= iq) & (j + 1 < nb))
        else:
            steps, valid = rq - jk, rq >= jk
        bias = jnp.where(valid, steps.astype(F32) * (-float(dil)), -MASK_BIAS)
        lse_all = stacked(lse_ref, lsen_ref if has_next else None, slice(None))
        dl_all = stacked(dl_ref, dln_ref if has_next else None, slice(None))
        lse3 = jnp.stack([lse_all[:, h:h + 1] for h in range(heads)])
        dl3 = jnp.stack([dl_all[:, h:h + 1] for h in range(heads)])
        p = jnp.exp(s_scr[...] + _head_slopes(heads) * bias[None] - lse3)
        p_scr[...] = p.astype(BF16)
        ds_scr[...] = (p * (dp_scr[...] - dl3)).astype(BF16)

        if has_next:
            @pl.when(j == 0)
            def _():
                carry[...] = jnp.zeros_like(carry)

        for hp in range(heads // 2):
            sl = slice(LANES * hp, LANES * (hp + 1))
            k2 = k_ref[:, sl]
            q_a, q_b = halves(stacked(q_ref, qn_ref if has_next else None, sl))
            do_a, do_b = halves(stacked(do_ref, don_ref if has_next else None, sl))
            ds_a, ds_b = ds_scr[2 * hp], ds_scr[2 * hp + 1]
            dq2 = jnp.where(low, jnp.dot(ds_a, k2, preferred_element_type=F32),
                            jnp.dot(ds_b, k2, preferred_element_type=F32))
            dk_ref[:, sl] = _dot_tn(ds_a, q_a) + _dot_tn(ds_b, q_b)
            dv_ref[:, sl] = _dot_tn(p_scr[2 * hp], do_a) + _dot_tn(p_scr[2 * hp + 1], do_b)
            if has_next:
                dq_ref[:, sl] = carry[:, sl] + dq2[:blk]
                carry[:, sl] = dq2[blk:]
            else:
                dq_ref[:, sl] = dq2

    args, in_specs = [k, v, q, do, lse, delta], [cur] * 4 + [cur_stat] * 2
    if has_next:
        args += [q, do, lse, delta]
        in_specs += [nxt] * 2 + [nxt_stat] * 2
    shape = jax.ShapeDtypeStruct(q.shape, F32)
    scratch = [pltpu.VMEM((heads, nq, blk), F32)] * 2 + [pltpu.VMEM((heads, nq, blk), BF16)] * 2
    if has_next:
        scratch.append(pltpu.VMEM((blk, a), F32))
    return pl.pallas_call(
        body, name=name, grid=grid, in_specs=in_specs, out_specs=[cur] * 3, out_shape=[shape] * 3,
        scratch_shapes=scratch,
        compiler_params=_params(*["parallel"] * (len(grid) - 1), "arbitrary"),
    )(*args)


def _attn_bwd_keys_major(q, k, v, do, lse, delta, dims, dil, *, name):
    a = dims.n_heads * dims.head_dim
    heads, hd, blk = dims.n_heads, dims.head_dim, ATTN_BLOCK
    nb = dims.seq // dil // blk
    has_next = nb > 1
    nq = 2 * blk if has_next else blk
    grid, cur, _, nxt = _attn_specs(dims, dil, a)
    _, cur_stat, _, nxt_stat = _attn_specs(dims, dil, LANES)

    def body(*refs):
        k_ref, v_ref, q_ref, do_ref, lse_ref, dl_ref = refs[:6]
        if has_next:
            qn_ref, don_ref, lsen_ref, dln_ref = refs[6:10]
            dq_ref, dk_ref, dv_ref, q_st, do_st, s_scr, dp_scr, p_scr, ds_scr, carry = refs[10:]
        else:
            dq_ref, dk_ref, dv_ref, q_st, do_st, s_scr, dp_scr, p_scr, ds_scr = refs[6:]
        j = pl.program_id(len(grid) - 1)
        low, high = _pair_masks(hd)
        q_st[0:blk, :] = q_ref[...]
        do_st[0:blk, :] = do_ref[...]
        if has_next:
            q_st[blk:, :] = qn_ref[...]
            do_st[blk:, :] = don_ref[...]
            lse_all = jnp.concatenate([lse_ref[...], lsen_ref[...]], axis=0)
            dl_all = jnp.concatenate([dl_ref[...], dln_ref[...]], axis=0)
        else:
            lse_all, dl_all = lse_ref[...], dl_ref[...]
        lse_t, dl_t = jnp.transpose(lse_all), jnp.transpose(dl_all)
        lse3 = jnp.stack([lse_t[h:h + 1, :] for h in range(heads)])
        dl3 = jnp.stack([dl_t[h:h + 1, :] for h in range(heads)])

        def halves(x):
            return jnp.where(low, x, jnp.zeros_like(x)), jnp.where(high, x, jnp.zeros_like(x))

        for hp in range(heads // 2):
            sl = slice(LANES * hp, LANES * (hp + 1))
            k2, v2 = k_ref[:, sl], v_ref[:, sl]
            q_a, q_b = halves(q_st[:, sl])
            do_a, do_b = halves(do_st[:, sl])
            s_scr[2 * hp], s_scr[2 * hp + 1] = _dot_nt(k2, q_a), _dot_nt(k2, q_b)
            dp_scr[2 * hp], dp_scr[2 * hp + 1] = _dot_nt(v2, do_a), _dot_nt(v2, do_b)

        jk = lax.broadcasted_iota(jnp.int32, (blk, nq), 0)
        rq = lax.broadcasted_iota(jnp.int32, (blk, nq), 1)
        if has_next:
            iq = jnp.where(rq < blk, rq, rq - blk)
            steps = jnp.where(rq < blk, iq - jk, iq - jk + blk)
            valid = ((rq < blk) & (iq >= jk)) | ((rq >= blk) & (jk >= iq) & (j + 1 < nb))
        else:
            steps, valid = rq - jk, rq >= jk
        bias = jnp.where(valid, steps.astype(F32) * (-float(dil)), -MASK_BIAS)
        p = jnp.exp(s_scr[...] + _head_slopes(heads) * bias[None] - lse3)
        p_scr[...] = p.astype(BF16)
        ds_scr[...] = (p * (dp_scr[...] - dl3)).astype(BF16)

        if has_next:
            @pl.when(j == 0)
            def _():
                carry[...] = jnp.zeros_like(carry)

        for hp in range(heads // 2):
            sl = slice(LANES * hp, LANES * (hp + 1))
            k2 = k_ref[:, sl]
            q_a, q_b = halves(q_st[:, sl])
            do_a, do_b = halves(do_st[:, sl])
            ds_a, ds_b = ds_scr[2 * hp], ds_scr[2 * hp + 1]
            dk_ref[:, sl] = (jnp.dot(ds_a, q_a, preferred_element_type=F32)
                             + jnp.dot(ds_b, q_b, preferred_element_type=F32))
            dv_ref[:, sl] = (jnp.dot(p_scr[2 * hp], do_a, preferred_element_type=F32)
                             + jnp.dot(p_scr[2 * hp + 1], do_b, preferred_element_type=F32))
            dq2 = jnp.where(low, _dot_tn(ds_a, k2), _dot_tn(ds_b, k2))
            if has_next:
                dq_ref[:, sl] = carry[:, sl] + dq2[:blk]
                carry[:, sl] = dq2[blk:]
            else:
                dq_ref[:, sl] = dq2

    args, in_specs = [k, v, q, do, lse, delta], [cur] * 4 + [cur_stat] * 2
    if has_next:
        args += [q, do, lse, delta]
        in_specs += [nxt] * 2 + [nxt_stat] * 2
    shape = jax.ShapeDtypeStruct(q.shape, F32)
    scratch = ([pltpu.VMEM((nq, a), BF16)] * 2 + [pltpu.VMEM((heads, blk, nq), F32)] * 2
               + [pltpu.VMEM((heads, blk, nq), BF16)] * 2)
    if has_next:
        scratch.append(pltpu.VMEM((blk, a), F32))
    return pl.pallas_call(
        body, name=name, grid=grid, in_specs=in_specs, out_specs=[cur] * 3, out_shape=[shape] * 3,
        scratch_shapes=scratch,
        compiler_params=_params(*["parallel"] * (len(grid) - 1), "arbitrary"),
    )(*args)


def _qkv_layouts_bwd(z, grads, gq, gk, head_ones, dims, *, name, tr=256):
    t = z.shape[0]
    a = dims.n_heads * dims.head_dim
    q_scale = dims.head_dim ** -0.5
    dils = tuple(grads)

    def body(q_ref, k_ref, *rest):
        d_refs = rest[:3 * len(dils)]
        gq_ref, gk_ref, sum_ref, spread_ref, dz_ref, dgq_ref, dgk_ref, scr = rest[3 * len(dils):]
        first = pl.program_id(0) == 0
        mean = lambda val: _two_pass_dot(_two_pass_dot(val, sum_ref[...]), spread_ref[...]) * (1.0 / dims.head_dim)

        def total(j):
            acc = None
            for g, d in enumerate(dils):
                ref = d_refs[3 * g + j]
                part = ref[...] if d == 1 else _residues_to_rows(ref, scr, d)
                acc = part if acc is None else acc + part
            return acc

        def norm_bwd(x_ref, dy, g_ref, scale, col, dg_ref):
            xv = x_ref[...].astype(F32)
            dy = dy * scale
            r = lax.rsqrt(mean(xv * xv) + RMS_EPS)
            gy = dy * g_ref[...]
            dx = r * gy - xv * (r * r * r) * mean(xv * gy)
            dz_ref[:, col * a:(col + 1) * a] = dx.astype(BF16)
            _accumulate(dg_ref, jnp.sum(dy * xv * r, axis=0, keepdims=True), first)

        norm_bwd(q_ref, total(0), gq_ref, q_scale, 0, dgq_ref)
        norm_bwd(k_ref, total(1), gk_ref, 1.0, 1, dgk_ref)
        dz_ref[:, 2 * a:3 * a] = total(2).astype(BF16)

    in_specs, args = [_row_spec(tr, a, 2), _row_spec(tr, a, 3)], [z, z]
    for d in dils:
        in_specs += [_row_spec(tr, a) if d == 1 else _residue_spec(dims, d, tr, a)] * 3
        args += list(grads[d])
    in_specs += [_vec_spec(a), _vec_spec(a), pl.BlockSpec((a, LANES), lambda i: (0, 0)),
                 pl.BlockSpec((LANES, a), lambda i: (0, 0))]
    return pl.pallas_call(
        body, name=name, grid=(t // tr,), in_specs=in_specs,
        out_specs=[_row_spec(tr, 3 * a), _vec_spec(a), _vec_spec(a)],
        out_shape=[jax.ShapeDtypeStruct((t, 3 * a), BF16)] + [jax.ShapeDtypeStruct((1, a), F32)] * 2,
        scratch_shapes=[pltpu.VMEM((a // LANES, tr, LANES), F32)],
        compiler_params=_params("arbitrary"),
    )(*args, gq, gk, *head_ones)


def _mix_fwd(ya, yb, z, gate_b, dims, *, name, tr=512):
    t, d = ya.shape
    tr = _pick(t, tr, 8)
    first_gate_col = z.shape[1] // d - 2

    def body(ya_ref, yb_ref, ga_ref, gb_ref, ba_ref, bb_ref, o_ref):
        g_a = _sigmoid(ga_ref[...].astype(F32) + ba_ref[...])
        g_b = _sigmoid(gb_ref[...].astype(F32) + bb_ref[...])
        o_ref[...] = (g_a * ya_ref[...] + g_b * yb_ref[...]).astype(BF16)

    return pl.pallas_call(
        body, name=name, grid=(t // tr,),
        in_specs=[_row_spec(tr, d), _row_spec(tr, d), _row_spec(tr, d, first_gate_col),
                  _row_spec(tr, d, first_gate_col + 1), _vec_spec(d, 0), _vec_spec(d, 1)],
        out_specs=_row_spec(tr, d), out_shape=jax.ShapeDtypeStruct((t, d), BF16),
        compiler_params=_params("parallel"),
    )(ya, yb, z, z, gate_b, gate_b)


def _mix_bwd(dmix, ya, yb, z, gate_b, dims, *, name, tr=512):
    t, d = ya.shape
    tr = _pick(t, tr, 8)
    first_gate_col = z.shape[1] // d - 2

    def body(dm_ref, ya_ref, yb_ref, ga_ref, gb_ref, ba_ref, bb_ref, dya_ref, dyb_ref, dz_ref, db_ref):
        dm = dm_ref[...].astype(F32)
        g_a = _sigmoid(ga_ref[...].astype(F32) + ba_ref[...])
        g_b = _sigmoid(gb_ref[...].astype(F32) + bb_ref[...])
        dya_ref[...] = (dm * g_a).astype(BF16)
        dyb_ref[...] = (dm * g_b).astype(BF16)
        dl_a = dm * ya_ref[...] * g_a * (1.0 - g_a)
        dl_b = dm * yb_ref[...] * g_b * (1.0 - g_b)
        dz_ref[:, 0:d] = dl_a.astype(BF16)
        dz_ref[:, d:2 * d] = dl_b.astype(BF16)
        first = pl.program_id(0) == 0
        sums = jnp.concatenate([jnp.sum(dl_a, axis=0, keepdims=True), jnp.sum(dl_b, axis=0, keepdims=True)], axis=1)
        _accumulate(db_ref, sums, first)

    return pl.pallas_call(
        body, name=name, grid=(t // tr,),
        in_specs=[_row_spec(tr, d), _row_spec(tr, d), _row_spec(tr, d), _row_spec(tr, d, first_gate_col),
                  _row_spec(tr, d, first_gate_col + 1), _vec_spec(d, 0), _vec_spec(d, 1)],
        out_specs=[_row_spec(tr, d), _row_spec(tr, d), _row_spec(tr, 2 * d), _vec_spec(2 * d)],
        out_shape=[jax.ShapeDtypeStruct((t, d), BF16)] * 2 + [jax.ShapeDtypeStruct((t, 2 * d), BF16),
                                                              jax.ShapeDtypeStruct((1, 2 * d), F32)],
        compiler_params=_params("arbitrary"),
    )(dmix, ya, yb, z, z, gate_b, gate_b)


def _loss_head(y, target, *, name, tr=512):
    t, d = y.shape
    tr = _pick(t, tr, 8)

    def body(y_ref, t_ref, dy_ref, dyb_ref, loss_ref):
        err = y_ref[...] - t_ref[...]
        dy = err * (1.0 / d)
        dy_ref[...] = dy
        dyb_ref[...] = dy.astype(BF16)
        part = jnp.sum(jnp.sum(err * err, axis=-1, keepdims=True), axis=0, keepdims=True) * (0.5 / d)
        _accumulate(loss_ref, jnp.broadcast_to(part, (8, 128)), pl.program_id(0) == 0)

    return pl.pallas_call(
        body, name=name, grid=(t // tr,),
        in_specs=[_row_spec(tr, d), _row_spec(tr, d)],
        out_specs=[_row_spec(tr, d), _row_spec(tr, d), pl.BlockSpec((8, 128), lambda i: (0, 0))],
        out_shape=[jax.ShapeDtypeStruct((t, d), F32), jax.ShapeDtypeStruct((t, d), BF16),
                   jax.ShapeDtypeStruct((8, 128), F32)],
        compiler_params=_params("arbitrary"),
    )(y, target)


def _adamw(w, grads, m, v, *, name, tr=256):
    r, c = w.shape
    tr = _pick(r, tr, 8)
    ng = len(grads)
    c1 = 1.0 - ADAM_B1 ** ADAM_STEP
    c2 = 1.0 - ADAM_B2 ** ADAM_STEP

    def body(*refs):
        w_ref, g_refs, m_ref, v_ref = refs[0], refs[1:1 + ng], refs[1 + ng], refs[2 + ng]
        g_out, d_out, m_out, v_out = refs[3 + ng:]
        g = g_refs[0][...]
        for extra in g_refs[1:]:
            g = g + extra[...]
        m_new = ADAM_B1 * m_ref[...] + (1.0 - ADAM_B1) * g
        v_new = ADAM_B2 * v_ref[...] + (1.0 - ADAM_B2) * (g * g)
        g_out[...] = g
        m_out[...] = m_new
        v_out[...] = v_new
        d_out[...] = -ADAM_LR * ((m_new / c1) / (jnp.sqrt(v_new / c2) + ADAM_EPS) + ADAM_WD * w_ref[...])

    spec = pl.BlockSpec((tr, c), lambda i: (i, 0))
    return pl.pallas_call(
        body, name=name, grid=(r // tr,),
        in_specs=[spec] * (3 + ng), out_specs=[spec] * 4, out_shape=[jax.ShapeDtypeStruct((r, c), F32)] * 4,
        compiler_params=_params("parallel"),
    )(w, *grads, m, v)


CHIP_PEERS = ((1, 0), (0, 1), (1, 1))


def _place():
    return lax.axis_index("x"), lax.axis_index("y"), lax.axis_index("c")


HBM = pl.BlockSpec(memory_space=pltpu.HBM)
SEM = pl.BlockSpec(memory_space=pltpu.SEMAPHORE)
IN_FLIGHT = pltpu.SideEffectType.DATAFLOW_SIDE_EFFECTING


def _in_hbm(a):
    return pltpu.with_memory_space_constraint(a, pltpu.HBM)


def _cast_to_lands(shards, dtypes, *, name, after=None):
    n = len(shards)

    def body(*refs):
        ins, outs, bufs, sems = refs[:n], refs[n:2 * n], refs[2 * n:3 * n], refs[3 * n]
        x, y, _ = _place()
        copies = []
        for a in range(n):
            bufs[a][...] = ins[a][...].astype(dtypes[a])
            cp = pltpu.make_async_copy(bufs[a], outs[a].at[2 * x + y], sems.at[a])
            cp.start()
            copies.append(cp)
        for cp in copies:
            cp.wait()

    body, more_specs, more_args = _ordered(body, n, after)
    return pl.pallas_call(
        body, name=name, in_specs=[pl.BlockSpec(memory_space=pltpu.VMEM)] * n + more_specs, out_specs=[ANY] * n,
        out_shape=[jax.ShapeDtypeStruct((N_CHIPS,) + s.shape, dt) for s, dt in zip(shards, dtypes)],
        scratch_shapes=[pltpu.VMEM(s.shape, dt) for s, dt in zip(shards, dtypes)] + [pltpu.SemaphoreType.DMA((n,))],
        compiler_params=pltpu.CompilerParams(vmem_limit_bytes=V7X_VMEM_LIMIT_BYTES),
    )(*shards, *more_args)


def _chip_copy(src, dst, send, recv, flip, place):
    x, y, c = place
    return pltpu.make_async_remote_copy(src_ref=src, dst_ref=dst, send_sem=send, recv_sem=recv,
                                        device_id=(x ^ flip[0], y ^ flip[1], c), device_id_type=MESH)


def _my_part(land, place, halved):
    block = land.at[2 * place[0] + place[1]]
    if not halved:
        return block
    rows = land.shape[1] // 2
    return block.at[pl.ds(pl.multiple_of(place[2] * rows, rows), rows)]


def _gather_start(lands, after, *, name, halved=()):
    n = len(lands)

    def body(*refs):
        ins, send, recv, token = refs[:n], refs[n + 1], refs[n + 2], refs[-1]
        place = _place()
        for a in range(n):
            part = _my_part(ins[a], place, a in halved)
            for p, flip in enumerate(CHIP_PEERS):
                k = 3 * a + p
                _chip_copy(part, part, send.at[k], recv.at[k], flip, place).start()
        token[...] = jnp.zeros_like(token)

    outs = pl.pallas_call(
        body, name=name, in_specs=[HBM] * n + [ANY],
        out_specs=(SEM, SEM, *[HBM] * n, pl.BlockSpec(memory_space=pltpu.VMEM)),
        out_shape=(pltpu.SemaphoreType.DMA((3 * n,)), pltpu.SemaphoreType.DMA((3 * n,)),
                   *[pltpu.HBM(l.shape, l.dtype) for l in lands], jax.ShapeDtypeStruct((8, 128), F32)),
        input_output_aliases={a: 2 + a for a in range(n)},
        compiler_params=pltpu.CompilerParams(has_side_effects=IN_FLIGHT),
    )(*[_in_hbm(l) for l in lands], after)
    return outs[0], outs[1], list(outs[2:2 + n]), outs[-1]


def _gather_wait(send, recv, lands, after, *, name, halved=()):
    n = len(lands)

    def body(*refs):
        ins, send_ref, recv_ref = refs[:n], refs[n], refs[n + 1]
        place = _place()
        for a in range(n):
            part = _my_part(ins[a], place, a in halved)
            for p, flip in enumerate(CHIP_PEERS):
                k = 3 * a + p
                cp = _chip_copy(part, part, send_ref.at[k], recv_ref.at[k], flip, place)
                cp.wait_send()
                cp.wait_recv()

    after = list(after) if isinstance(after, (list, tuple)) else [after]
    return pl.pallas_call(
        body, name=name, in_specs=[HBM] * n + [SEM, SEM] + [ANY] * len(after), out_specs=[HBM] * n,
        out_shape=[pltpu.HBM(l.shape, l.dtype) for l in lands],
        input_output_aliases={a: a for a in range(n)},
        compiler_params=pltpu.CompilerParams(has_side_effects=IN_FLIGHT),
    )(*lands, send, recv, *after)


def _forward_to_sibling(land, *, name):
    rows = land.shape[1] // 2

    def body(land_ref, out_ref, send, recv):
        x, y, c = _place()
        copies = []
        for p, (fx, fy) in enumerate(CHIP_PEERS):
            chip = 2 * (x ^ fx) + (y ^ fy)
            mine = pl.ds(pl.multiple_of(c * rows, rows), rows)
            theirs = pl.ds(pl.multiple_of((1 - c) * rows, rows), rows)
            out = pltpu.make_async_remote_copy(
                src_ref=land_ref.at[chip].at[mine], dst_ref=out_ref.at[chip].at[mine], send_sem=send.at[p],
                recv_sem=recv.at[p], device_id=(x, y, 1 - c), device_id_type=MESH)
            out.start()
            copies.append((out, pltpu.make_async_remote_copy(
                src_ref=land_ref.at[chip].at[theirs], dst_ref=out_ref.at[chip].at[theirs], send_sem=send.at[p],
                recv_sem=recv.at[p], device_id=(x, y, 1 - c), device_id_type=MESH)))
        for out, arriving in copies:
            out.wait_send()
            arriving.wait_recv()

    return pl.pallas_call(
        body, name=name, in_specs=[ANY], out_specs=ANY, out_shape=jax.ShapeDtypeStruct(land.shape, land.dtype),
        input_output_aliases={0: 0},
        scratch_shapes=[pltpu.SemaphoreType.DMA((3,)), pltpu.SemaphoreType.DMA((3,))],
    )(land)


def _scatter_start(grad, *, name):
    def body(g_ref, land_ref, send, recv, g_thru, land_thru, token):
        place = _place()
        for p, flip in enumerate(CHIP_PEERS):
            peer_chip = 2 * (place[0] ^ flip[0]) + (place[1] ^ flip[1])
            _chip_copy(g_ref.at[peer_chip], land_ref.at[p], send.at[p], recv.at[p], flip, place).start()
        token[...] = jnp.zeros_like(token)

    land = lax.empty((3,) + grad.shape[1:], grad.dtype)
    return pl.pallas_call(
        body, name=name, in_specs=[HBM, HBM],
        out_specs=(SEM, SEM, HBM, HBM, pl.BlockSpec(memory_space=pltpu.VMEM)),
        out_shape=(pltpu.SemaphoreType.DMA((3,)), pltpu.SemaphoreType.DMA((3,)), pltpu.HBM(grad.shape, grad.dtype),
                   pltpu.HBM(land.shape, land.dtype), jax.ShapeDtypeStruct((8, 128), F32)),
        input_output_aliases={0: 2, 1: 3},
        compiler_params=pltpu.CompilerParams(has_side_effects=IN_FLIGHT),
    )(_in_hbm(grad), _in_hbm(land))


def _scatter_wait(started, after, *, name):
    n = len(started)

    def body(*refs):
        grads, lands = refs[:n], refs[n:2 * n]
        sends, recvs = refs[2 * n:3 * n], refs[3 * n:4 * n]
        place = _place()
        for a in range(n):
            for p, flip in enumerate(CHIP_PEERS):
                cp = _chip_copy(grads[a].at[0], lands[a].at[p], sends[a].at[p], recvs[a].at[p], flip, place)
                cp.wait_send()
                cp.wait_recv()

    grads, lands = [s[2] for s in started], [s[3] for s in started]
    after = list(after) if isinstance(after, (list, tuple)) else [after]
    outs = pl.pallas_call(
        body, name=name, in_specs=[HBM] * (2 * n) + [SEM] * (2 * n) + [ANY] * len(after), out_specs=[HBM] * (2 * n),
        out_shape=[pltpu.HBM(a.shape, a.dtype) for a in grads + lands],
        input_output_aliases={a: a for a in range(2 * n)},
        compiler_params=pltpu.CompilerParams(has_side_effects=IN_FLIGHT),
    )(*grads, *lands, *[s[0] for s in started], *[s[1] for s in started], *after)
    return list(zip(outs[:n], outs[n:]))


def _sibling_copy(src, dst, send, recv, place):
    x, y, c = place
    return pltpu.make_async_remote_copy(src_ref=src, dst_ref=dst, send_sem=send, recv_sem=recv,
                                        device_id=(x, y, 1 - c), device_id_type=MESH)


def _swap_start(arrays, *, name):
    n = len(arrays)

    def body(*refs):
        ins, lands, send, recv, token = refs[:n], refs[n:2 * n], refs[2 * n], refs[2 * n + 1], refs[-1]
        place = _place()
        for a in range(n):
            _sibling_copy(ins[a], lands[a], send.at[a], recv.at[a], place).start()
        token[...] = jnp.zeros_like(token)

    both = [_in_hbm(a) for a in arrays] + [_in_hbm(lax.empty(a.shape, a.dtype)) for a in arrays]
    outs = pl.pallas_call(
        body, name=name, in_specs=[HBM] * (2 * n),
        out_specs=(SEM, SEM, *[HBM] * (2 * n), pl.BlockSpec(memory_space=pltpu.VMEM)),
        out_shape=(pltpu.SemaphoreType.DMA((n,)), pltpu.SemaphoreType.DMA((n,)),
                   *[pltpu.HBM(a.shape, a.dtype) for a in both], jax.ShapeDtypeStruct((8, 128), F32)),
        input_output_aliases={a: 2 + a for a in range(2 * n)},
        compiler_params=pltpu.CompilerParams(has_side_effects=IN_FLIGHT),
    )(*both)
    return outs[0], outs[1], list(outs[2:2 + n]), list(outs[2 + n:2 + 2 * n]), outs[-1]


def _swap_wait(started, after, *, name):
    send, recv, arrays, lands = started[:4]
    n = len(arrays)

    def body(*refs):
        ins, zones, send_ref, recv_ref = refs[:n], refs[n:2 * n], refs[2 * n], refs[2 * n + 1]
        place = _place()
        for a in range(n):
            cp = _sibling_copy(ins[a], zones[a], send_ref.at[a], recv_ref.at[a], place)
            cp.wait_send()
            cp.wait_recv()

    after = list(after) if isinstance(after, (list, tuple)) else [after]
    outs = pl.pallas_call(
        body, name=name, in_specs=[HBM] * (2 * n) + [SEM, SEM] + [ANY] * len(after), out_specs=[HBM] * (2 * n),
        out_shape=[pltpu.HBM(a.shape, a.dtype) for a in arrays + lands],
        input_output_aliases={a: a for a in range(2 * n)},
        compiler_params=pltpu.CompilerParams(has_side_effects=IN_FLIGHT),
    )(*arrays, *lands, send, recv, *after)
    return list(outs[:n]), list(outs[n:])


def _allreduce_start(packed, *, name):
    n_dev = 8

    def body(src_ref, land_ref, send, recv, src_thru, land_thru, token):
        x, y, c = _place()
        me = 4 * x + 2 * y + c
        for p in range(1, n_dev):
            pltpu.make_async_remote_copy(
                src_ref=src_ref, dst_ref=land_ref.at[me], send_sem=send.at[p - 1], recv_sem=recv.at[p - 1],
                device_id=(x ^ (p >> 2), y ^ ((p >> 1) & 1), c ^ (p & 1)), device_id_type=MESH).start()
        token[...] = jnp.zeros_like(token)

    land = lax.empty((n_dev,) + packed.shape, packed.dtype)
    return pl.pallas_call(
        body, name=name, in_specs=[HBM, HBM],
        out_specs=(SEM, SEM, HBM, HBM, pl.BlockSpec(memory_space=pltpu.VMEM)),
        out_shape=(pltpu.SemaphoreType.DMA((n_dev - 1,)), pltpu.SemaphoreType.DMA((n_dev - 1,)),
                   pltpu.HBM(packed.shape, packed.dtype), pltpu.HBM(land.shape, land.dtype),
                   jax.ShapeDtypeStruct((8, 128), F32)),
        input_output_aliases={0: 2, 1: 3},
        compiler_params=pltpu.CompilerParams(has_side_effects=IN_FLIGHT),
    )(_in_hbm(packed), _in_hbm(land))


def _allreduce_wait(started, after, *, name):
    send, recv, packed, land = started[:4]
    n_dev = 8

    def body(src_ref, land_ref, send_ref, recv_ref, *_):
        x, y, c = _place()
        for p in range(1, n_dev):
            cp = pltpu.make_async_remote_copy(
                src_ref=src_ref, dst_ref=land_ref.at[0], send_sem=send_ref.at[p - 1], recv_sem=recv_ref.at[p - 1],
                device_id=(x ^ (p >> 2), y ^ ((p >> 1) & 1), c ^ (p & 1)), device_id_type=MESH)
            cp.wait_send()
            cp.wait_recv()

    after = list(after) if isinstance(after, (list, tuple)) else [after]
    return pl.pallas_call(
        body, name=name, in_specs=[HBM, HBM, SEM, SEM] + [ANY] * len(after), out_specs=[HBM, HBM],
        out_shape=[pltpu.HBM(packed.shape, packed.dtype), pltpu.HBM(land.shape, land.dtype)],
        input_output_aliases={0: 0, 1: 1},
        compiler_params=pltpu.CompilerParams(has_side_effects=IN_FLIGHT),
    )(packed, land, send, recv, *after)


def _sum_devices(mine, land, *, name):
    n_dev = land.shape[0]

    def body(mine_ref, land_ref, out_ref):
        x, y, c = _place()
        me = 4 * x + 2 * y + c
        total = None
        for s in range(n_dev):
            part = jnp.where(me == s, mine_ref[...], land_ref[s])
            total = part if total is None else total + part
        out_ref[...] = total

    return pl.pallas_call(body, name=name, out_shape=jax.ShapeDtypeStruct(mine.shape, mine.dtype))(mine, land)


def _sum_received(grad, land, *, name, tr=256):
    _, r, c = grad.shape
    tr = _pick(r, tr, 8)

    def body(chip_ref, g_ref, l_ref, o_ref):
        o_ref[...] = ((g_ref[...] + l_ref[0].astype(F32)) + l_ref[1].astype(F32)) + l_ref[2].astype(F32)

    chip = (2 * lax.axis_index("x") + lax.axis_index("y")).astype(jnp.int32).reshape(1)
    return pl.pallas_call(
        body, name=name,
        grid_spec=pltpu.PrefetchScalarGridSpec(
            num_scalar_prefetch=1, grid=(r // tr,),
            in_specs=[pl.BlockSpec((None, tr, c), lambda i, chip_ref: (chip_ref[0], i, 0)),
                      pl.BlockSpec((3, tr, c), lambda i, chip_ref: (0, i, 0))],
            out_specs=pl.BlockSpec((tr, c), lambda i, chip_ref: (i, 0))),
        out_shape=jax.ShapeDtypeStruct((r, c), F32), compiler_params=_params("parallel"),
    )(chip, grad, land)


def _allreduce_small(packed, *, name, after=None):
    r, d = packed.shape
    n_dev = 8

    def body(src_ref, out_ref, buf, send, recv):
        x, y, c = _place()
        me = 4 * x + 2 * y + c
        started = []
        for p in range(1, n_dev):
            rc = pltpu.make_async_remote_copy(
                src_ref=src_ref, dst_ref=buf.at[me], send_sem=send.at[p - 1], recv_sem=recv.at[p - 1],
                device_id=(x ^ (p >> 2), y ^ ((p >> 1) & 1), c ^ (p & 1)), device_id_type=MESH)
            rc.start()
            started.append(rc)
        buf[me] = src_ref[...]
        for rc in started:
            rc.wait()
        total = buf[0]
        for s in range(1, n_dev):
            total = total + buf[s]
        out_ref[...] = total

    vmem = pl.BlockSpec(memory_space=pltpu.VMEM)
    body, more_specs, more_args = _ordered(body, 1, after)
    return pl.pallas_call(
        body, name=name, in_specs=[vmem] + more_specs, out_specs=vmem, out_shape=jax.ShapeDtypeStruct((r, d), F32),
        scratch_shapes=[pltpu.VMEM((n_dev, r, d), F32), pltpu.SemaphoreType.DMA((n_dev - 1,)),
                        pltpu.SemaphoreType.DMA((n_dev - 1,))],
    )(packed, *more_args)


def _packed_rows(size, d):
    return -(-size // (8 * d)) * 8


def _pack_rows(arrays, d):
    rows = []
    for arr in arrays:
        flat = arr.reshape(-1).astype(F32)
        n = _packed_rows(flat.shape[0], d)
        rows.append(jnp.pad(flat, (0, n * d - flat.shape[0])).reshape(n, d))
    return jnp.concatenate(rows, axis=0)


def _unpack_rows(packed, shapes, d):
    out, row = [], 0
    for shape in shapes:
        size = math.prod(shape)
        n = _packed_rows(size, d)
        out.append(packed[row:row + n].reshape(-1)[:size].reshape(shape))
        row += n
    return out


SMALL = ("norm1_g", "gate_b", "conv_b", "conv_norm_g", "q_norm_g", "k_norm_g", "norm2_g", "ffn_conv_b")
LARGE = ("w_in", "w_conv_out", "w_attn_out", "w_out", "w_up", "w_down")
WEIGHTS = ("norm1_g", "w_in", "gate_b", "conv_w", "conv_b", "conv_norm_g", "w_conv_out", "q_norm_g", "k_norm_g",
           "w_attn_out", "w_out", "norm2_g", "w_up", "ffn_conv_w", "ffn_conv_b", "w_down")


def _head_ones(dims):
    a = dims.n_heads * dims.head_dim
    head = jnp.arange(a, dtype=jnp.int32) // dims.head_dim
    return (head[:, None] == head[None, :]).astype(BF16)


def _after(vec, token):
    return vec if token is None else vec + token[0:1, 0:1]


def _local_step(dims, x, target, small, first_weights, other_weights, send_grad):
    d, f, heads = dims.d_model, dims.d_ff, dims.n_heads
    small = dict(small)
    row = lambda name: small[name].reshape(1, -1)
    head_sum = _head_sum_matrix(dims)
    head_spread = jnp.transpose(head_sum)
    ones = (head_sum, head_spread)
    gq = jnp.tile(row("q_norm_g"), (1, heads))
    gk = jnp.tile(row("k_norm_g"), (1, heads))
    one_shard = lambda w: w.reshape(1, -1, w.shape[-1])

    h = _rmsnorm_fwd(x, row("norm1_g"), name="norm1")
    full = first_weights(h)
    w_in = full["w_in"]
    conv_w = jnp.pad(full["conv_w"], ((0, CONV_HALO - dims.conv_width), (0, 0)))
    ffn_w = jnp.pad(full["ffn_conv_w"], ((0, FFN_HALO - dims.ffn_conv_width), (0, 0)))
    z = _mm_nn(h, w_in, out_dtype=BF16, after=full.get("token"), tm=2048, tn=1792, name="in_proj")
    a1, a3 = _conv_branch_fwd(z, conv_w, row("conv_b"), row("conv_norm_g"), dims, name="conv_branch")
    qkv = _qkv_layouts_fwd(z, gq, gk, ones, dims, name="qk_norm")
    per_group = {dil: _attn_fwd(*qkv[dil], dims, dil, name=f"attn_fwd_d{dil}") for dil in DILATIONS}
    o, lse = _attn_combine(per_group, head_spread, dims, name="attn_combine")
    full = other_weights(o)
    w_up = full["w_up"]
    w_co, w_ao, w_o, w_dn = (one_shard(full[k]) for k in ("w_conv_out", "w_attn_out", "w_out", "w_down"))
    ya = _mm_nn(a3, w_co, out_dtype=F32, name="conv_out_proj")
    yb = _mm_nn(o, w_ao, out_dtype=F32, name="attn_out_proj")
    mixed = _mix_fwd(ya, yb, z, row("gate_b"), dims, name="gate_mix")
    x1, h2 = _proj_residual_norm(mixed, w_o, x, row("norm2_g"), name="out_proj_norm2")
    up = _mm_nn(h2, w_up, out_dtype=F32, tm=2048, name="up_proj")
    act = _ffn_act_fwd(up, ffn_w, row("ffn_conv_b"), dims, name="ffn_act")
    dy, dy_b, loss = _proj_residual_loss(act, w_dn, x1, target, tm=512, name="down_proj_loss")

    grads = {}

    def large(name, g):
        grads[name], g_bf16 = g
        return send_grad(name, g_bf16)

    sent = large("w_down", _mm_tn(act, dy_b, n_shards=1, name="dw_down"))
    dact = _mm_nt(dy_b, w_dn, out_dtype=BF16, after=sent, name="d_act")
    dup, dfw, dfb = _ffn_bwd(dact, up, ffn_w, row("ffn_conv_b"), dims, name="ffn_bwd")
    grads["ffn_conv_w"], grads["ffn_conv_b"] = dfw[:dims.ffn_conv_width], dfb
    sent = large("w_up", _mm_tn(h2, dup, n_shards=N_CHIPS, name="dw_up"))
    dh2 = _mm_nt(dup, w_up, out_dtype=F32, after=sent, name="d_h2")
    dx1, dx1_b, grads["norm2_g"] = _rmsnorm_bwd(x1, row("norm2_g"), dh2, dy, want_bf16=True, name="norm2_bwd")
    sent = large("w_out", _mm_tn(mixed, dx1_b, n_shards=1, name="dw_out"))
    dmix = _mm_nt(dx1_b, w_o, out_dtype=F32, after=sent, name="d_mix")
    dya, dyb, dz_gate, grads["gate_b"] = _mix_bwd(dmix, ya, yb, z, row("gate_b"), dims, name="gate_mix_bwd")
    sent = large("w_attn_out", _mm_tn(o, dyb, n_shards=1, name="dw_attn_out"))
    do = _mm_nt(dyb, w_ao, out_dtype=BF16, after=sent, name="d_attn")
    dos, deltas = _attn_bwd_prep(do, o, head_sum, dims, name="attn_bwd_prep")
    dqkv = {dil: _attn_bwd_keys_major(*qkv[dil], dos[dil], lse[dil], deltas[dil], dims, dil, name=f"attn_bwd_d{dil}")
            for dil in DILATIONS}
    dz_qkv, dgq, dgk = _qkv_layouts_bwd(z, dqkv, gq, gk, ones, dims, name="qk_norm_bwd")
    grads["q_norm_g"] = dgq.reshape(heads, dims.head_dim).sum(axis=0)
    grads["k_norm_g"] = dgk.reshape(heads, dims.head_dim).sum(axis=0)
    sent = large("w_conv_out", _mm_tn(a3, dya, n_shards=1, name="dw_conv_out"))
    da3 = _mm_nt(dya, w_co, out_dtype=F32, after=sent, name="d_conv_act")
    da1, grads["conv_norm_g"] = _conv_norm_bwd(da3, a1, row("conv_norm_g"), name="conv_norm_bwd")
    dz, dcw, grads["conv_b"] = _conv_branch_bwd(da1, z, conv_w, [dz_qkv, dz_gate], dims, name="conv_branch_bwd")
    grads["conv_w"] = dcw[:dims.conv_width]
    sent = large("w_in", _mm_tn(h, dz, n_shards=N_CHIPS, name="dw_in"))
    dh = _mm_nt(dz, w_in, out_dtype=F32, after=sent, name="d_h")
    dx, grads["norm1_g"] = _rmsnorm_bwd(x, row("norm1_g"), dh, dx1, want_bf16=False, name="norm1_bwd")
    return loss, dx, grads


def _step(dims, x, target, w, m, v):
    d = dims.d_model
    t = dims.tokens
    sq = lambda a: a.reshape(a.shape[1:])
    w2, m2, v2 = ({k: sq(a) for k, a in grp.items()} for grp in (w, m, v))

    conv_pad = jnp.pad(w2["conv_w"], ((0, CONV_HALO - dims.conv_width), (0, 0)))
    ffn_pad = jnp.pad(w2["ffn_conv_w"], ((0, FFN_HALO - dims.ffn_conv_width), (0, 0)))
    first_names = ("w_in", "conv_w", "ffn_conv_w")
    other_names = tuple(k for k in LARGE if k not in first_names)
    lands = dict(zip(first_names, _cast_to_lands([w2["w_in"], conv_pad, ffn_pad], [BF16, F32, F32], name="cast_first")))
    first = _gather_start([lands[k] for k in first_names], x, halved=(0,), name="gather_start_first")
    lands.update(zip(other_names, _cast_to_lands([w2[k] for k in other_names], [BF16] * len(other_names),
                                                 after=first[3], name="cast_other")))
    other = []
    cols = lambda g, rows: jnp.moveaxis(g, 0, 1).reshape(g.shape[1], -1)[:rows]

    def first_weights(after):
        got = dict(zip(first_names, _gather_wait(*first[:3], [after] + [lands[k] for k in other_names], halved=(0,),
                                                 name="gather_wait_first")))
        got["w_in"] = _forward_to_sibling(got["w_in"], name="forward_w_in")
        other.extend(_gather_start([lands[k] for k in other_names], got["w_in"], name="gather_start_other"))
        got["conv_w"] = cols(got["conv_w"], dims.conv_width)
        got["ffn_conv_w"] = cols(got["ffn_conv_w"], dims.ffn_conv_width)
        got["token"] = other[3]
        return got

    def other_weights(after):
        return dict(zip(other_names, _gather_wait(*other[:3], after, name="gather_wait_other")))

    started = {}

    def send_grad(name, g):
        send, recv, g_thru, land, token = _scatter_start(g.reshape(N_CHIPS, -1, g.shape[-1]), name=f"scatter_start_{name}")
        started[name] = (send, recv, g_thru, land)
        return token

    small = {k: w2[k] for k in SMALL}
    small["norm1_g"] = _after(small["norm1_g"].reshape(1, -1), first[3])
    loss, dx, grads = _local_step(dims, x.reshape(t, d), target.reshape(t, d), small, first_weights, other_weights, send_grad)

    def my_sums(names, after, tag):
        arrived = _scatter_wait([started[k] for k in names], after, name=f"scatter_wait_{tag}")
        blocks = [grads[k].reshape(N_CHIPS, -1, grads[k].shape[-1]) for k in names]
        return [_sum_received(g, land, name=f"sum_{k}") for k, g, (_, land) in zip(names, blocks, arrived)]

    def updates(names, mine, theirs):
        return {k: _adamw(w2[k], [a, b], m2[k], v2[k], name=f"adamw_{k}") for k, a, b in zip(names, mine, theirs)}

    small_names = SMALL + ("conv_w", "ffn_conv_w")
    packed = _pack_rows([grads[k] for k in small_names] + [loss[0, 0]], d)
    reducing = _allreduce_start(packed, name="allreduce_start")
    others = [k for k in LARGE if k != "w_in"]
    mine_others = my_sums(others, [dx, reducing[4]], "others")
    swapping_others = _swap_start(mine_others, name="swap_start_others")
    mine_w_in = my_sums(["w_in"], swapping_others[4], "w_in")
    swapping_w_in = _swap_start(mine_w_in, name="swap_start_w_in")
    out = updates(others, *_swap_wait(swapping_others, swapping_w_in[4], name="swap_wait_others"))
    last_updates = [out[k][1] for k in others]
    reduced = _sum_devices(*_allreduce_wait(reducing, last_updates, name="allreduce_wait"), name="allreduce_sum")
    shapes = [grads[k].shape for k in small_names] + [()]
    *small_g, loss_total = _unpack_rows(reduced, shapes, d)
    small_g = dict(zip(small_names, small_g))
    chip = 2 * lax.axis_index("x") + lax.axis_index("y")
    for k in ("conv_w", "ffn_conv_w"):
        width = w2[k].shape[1]
        small_g[k] = lax.dynamic_slice_in_dim(small_g[k], chip * width, width, axis=1)

    small_shapes = [w2[k].shape for k in small_names]
    pack = lambda grp: _pack_rows([grp[k] for k in small_names], d)
    results = _adamw(pack(w2), [pack(small_g)], pack(m2), pack(v2), name="adamw_small")
    unpacked = [_unpack_rows(r, small_shapes, d) for r in results]
    out.update({k: tuple(u[i] for u in unpacked) for i, k in enumerate(small_names)})
    out.update(updates(["w_in"], *_swap_wait(swapping_w_in, results[1], name="swap_wait_w_in")))

    lead =lambda a: a.reshape((1,) + a.shape)
    ordered = [[lead(out[k][j].reshape(w2[k].shape)) for k in WEIGHTS] for j in range(4)]
    return (loss_total, dx.reshape(x.shape), *ordered[0], *ordered[1], *ordered[2], *ordered[3])


def kernel(x, norm1_g, w_in, gate_b, conv_w, conv_b, conv_norm_g, w_conv_out, q_norm_g, k_norm_g, w_attn_out, w_out, norm2_g, w_up, ffn_conv_w, ffn_conv_b, w_down, loss_target, m_norm1_g, m_w_in, m_gate_b, m_conv_w, m_conv_b, m_conv_norm_g, m_w_conv_out, m_q_norm_g, m_k_norm_g, m_w_attn_out, m_w_out, m_norm2_g, m_w_up, m_ffn_conv_w, m_ffn_conv_b, m_w_down, v_norm1_g, v_w_in, v_gate_b, v_conv_w, v_conv_b, v_conv_norm_g, v_w_conv_out, v_q_norm_g, v_k_norm_g, v_w_attn_out, v_w_out, v_norm2_g, v_w_up, v_ffn_conv_w, v_ffn_conv_b, v_w_down):
    w = dict(zip(WEIGHTS, (norm1_g, w_in, gate_b, conv_w, conv_b, conv_norm_g, w_conv_out, q_norm_g, k_norm_g,
                           w_attn_out, w_out, norm2_g, w_up, ffn_conv_w, ffn_conv_b, w_down)))
    m = dict(zip(WEIGHTS, (m_norm1_g, m_w_in, m_gate_b, m_conv_w, m_conv_b, m_conv_norm_g, m_w_conv_out, m_q_norm_g,
                           m_k_norm_g, m_w_attn_out, m_w_out, m_norm2_g, m_w_up, m_ffn_conv_w, m_ffn_conv_b, m_w_down)))
    v = dict(zip(WEIGHTS, (v_norm1_g, v_w_in, v_gate_b, v_conv_w, v_conv_b, v_conv_norm_g, v_w_conv_out, v_q_norm_g,
                           v_k_norm_g, v_w_attn_out, v_w_out, v_norm2_g, v_w_up, v_ffn_conv_w, v_ffn_conv_b, v_w_down)))
    dims = Dims(d_model=x.shape[-1], batch_local=x.shape[0], seq=x.shape[1], d_ff=w_down.shape[1] * N_CHIPS)
    return _step(dims, x, loss_target, w, m, v)
```

```python
import functools
import math
from typing import NamedTuple

import jax
import jax.numpy as jnp
from jax import lax
from jax.experimental import pallas as pl
from jax.experimental.pallas import tpu as pltpu

F32 = jnp.float32
BF16 = jnp.bfloat16

RMS_EPS = 1e-6
MASKED_SCORE = -1e30
ATTN_BLOCK = 128
DILATIONS = (1, 4, 16)
CONV_HALO = 32
FFN_HALO = 8
ADAM_LR, ADAM_B1, ADAM_B2, ADAM_EPS, ADAM_WD, ADAM_STEP = 0.001, 0.9, 0.999, 1e-08, 0.01, 10
V7X_VMEM_LIMIT_BYTES = 56 * 2 ** 20
N_CHIPS = 4
MESH = pl.DeviceIdType.MESH


class Dims(NamedTuple):
    d_model: int = 1024
    n_heads: int = 16
    head_dim: int = 64
    d_ff: int = 2816
    seq: int = 2048
    batch_local: int = 2
    conv_width: int = 31
    ffn_conv_width: int = 3

    @property
    def tokens(self):
        return self.seq * self.batch_local


def _params(*semantics):
    return pltpu.CompilerParams(dimension_semantics=semantics, vmem_limit_bytes=V7X_VMEM_LIMIT_BYTES)


ANY = pl.BlockSpec(memory_space=pl.ANY)


def _ordered(body, n_inputs, after):
    after = [] if after is None else list(after) if isinstance(after, (list, tuple)) else [after]
    if not after:
        return body, [], []

    def wrapped(*refs):
        return body(*refs[:n_inputs], *refs[n_inputs + len(after):])

    return wrapped, [ANY] * len(after), after


def _pick(n, target, mult=128):
    if n <= target:
        return n
    best = None
    for t in range(mult, target + 1, mult):
        if n % t == 0:
            best = t
    assert best is not None, (n, target, mult)
    return best


def _sigmoid(v):
    return 1.0 / (1.0 + jnp.exp(-v))


def _mm_nn(a, w, *, out_dtype, name, residual=None, after=None, tm=1024, tn=1408, tk=2816):
    m, k = a.shape
    nsh, k2, c = w.shape
    assert k == k2 and a.dtype == BF16 and w.dtype == BF16
    n = nsh * c
    tm, tn, tk = _pick(m, tm, 8), _pick(c, tn), _pick(k, tk)
    nk, cpn = k // tk, c // tn

    def body(*refs):
        if residual is None:
            a_ref, w_ref, o_ref, acc = refs
        else:
            a_ref, w_ref, r_ref, o_ref, acc = refs
        prod = jnp.dot(a_ref[...], w_ref[...], preferred_element_type=F32)

        def finish(total):
            if residual is not None:
                total = total + r_ref[...]
            o_ref[...] = total.astype(out_dtype)

        if nk == 1:
            finish(prod)
        else:
            kk = pl.program_id(2)

            @pl.when(kk == 0)
            def _():
                acc[...] = prod

            @pl.when(kk > 0)
            def _():
                acc[...] += prod

            @pl.when(kk == nk - 1)
            def _():
                finish(acc[...])

    in_specs = [pl.BlockSpec((tm, tk), lambda i, j, kk: (i, kk)),
                pl.BlockSpec((None, tk, tn), lambda i, j, kk: (j // cpn, kk, j % cpn))]
    args = [a, w]
    if residual is not None:
        in_specs.append(pl.BlockSpec((tm, tn), lambda i, j, kk: (i, j)))
        args.append(residual)
    body, more_specs, more_args = _ordered(body, len(args), after)
    return pl.pallas_call(
        body, name=name, grid=(m // tm, n // tn, nk),
        in_specs=in_specs + more_specs, out_specs=pl.BlockSpec((tm, tn), lambda i, j, kk: (i, j)),
        out_shape=jax.ShapeDtypeStruct((m, n), out_dtype),
        scratch_shapes=[pltpu.VMEM((tm, tn) if nk > 1 else (8, 128), F32)],
        compiler_params=_params("parallel", "parallel", "arbitrary"),
    )(*args, *more_args)


def _proj_residual_norm(a, w, residual, g, *, name, tm=1024):
    m, k = a.shape
    _, k2, n = w.shape
    assert w.shape[0] == 1 and k == k2 and a.dtype == BF16 and w.dtype == BF16
    tm = _pick(m, tm, 8)

    def body(a_ref, w_ref, r_ref, g_ref, y_ref, h_ref):
        y = r_ref[...] + jnp.dot(a_ref[...], w_ref[...], preferred_element_type=F32)
        y_ref[...] = y
        h_ref[...] = (y * lax.rsqrt(jnp.mean(y * y, axis=-1, keepdims=True) + RMS_EPS) * g_ref[...]).astype(BF16)

    rows = lambda width: pl.BlockSpec((tm, width), lambda i: (i, 0))
    return pl.pallas_call(
        body, name=name, grid=(m // tm,),
        in_specs=[rows(k), pl.BlockSpec((None, k, n), lambda i: (0, 0, 0)), rows(n), pl.BlockSpec((1, n), lambda i: (0, 0))],
        out_specs=[rows(n), rows(n)],
        out_shape=[jax.ShapeDtypeStruct((m, n), F32), jax.ShapeDtypeStruct((m, n), BF16)],
        compiler_params=_params("parallel"),
    )(a, w, residual, g)


def _proj_residual_loss(a, w, residual, target, *, name, tm=1024):
    m, k = a.shape
    _, k2, n = w.shape
    assert w.shape[0] == 1 and k == k2 and a.dtype == BF16 and w.dtype == BF16
    tm = _pick(m, tm, 8)

    def body(a_ref, w_ref, r_ref, t_ref, dy_ref, dyb_ref, loss_ref):
        err = r_ref[...] + jnp.dot(a_ref[...], w_ref[...], preferred_element_type=F32) - t_ref[...]
        dy = err * (1.0 / n)
        dy_ref[...] = dy
        dyb_ref[...] = dy.astype(BF16)
        part = jnp.sum(jnp.sum(err * err, axis=-1, keepdims=True), axis=0, keepdims=True) * (0.5 / n)
        _accumulate(loss_ref, jnp.broadcast_to(part, (8, 128)), pl.program_id(0) == 0)

    rows = lambda width: pl.BlockSpec((tm, width), lambda i: (i, 0))
    return pl.pallas_call(
        body, name=name, grid=(m // tm,),
        in_specs=[rows(k), pl.BlockSpec((None, k, n), lambda i: (0, 0, 0)), rows(n), rows(n)],
        out_specs=[rows(n), rows(n), pl.BlockSpec((8, 128), lambda i: (0, 0))],
        out_shape=[jax.ShapeDtypeStruct((m, n), F32), jax.ShapeDtypeStruct((m, n), BF16),
                   jax.ShapeDtypeStruct((8, 128), F32)],
        compiler_params=_params("arbitrary"),
    )(a, w, residual, target)


def _mm_nt(a, w, *, out_dtype, name, after=None, tm=1024, tn=1408, tk=1792):
    m, k = a.shape
    nsh, r, c = w.shape
    assert k == nsh * c and a.dtype == BF16 and w.dtype == BF16
    tm, tn, tk = _pick(m, tm, 8), _pick(r, tn), _pick(c, tk)
    nk, cpk = k // tk, c // tk

    def body(a_ref, w_ref, o_ref, acc):
        prod = lax.dot_general(a_ref[...], w_ref[...], (((1,), (1,)), ((), ())), preferred_element_type=F32)
        if nk == 1:
            o_ref[...] = prod.astype(out_dtype)
        else:
            kk = pl.program_id(2)

            @pl.when(kk == 0)
            def _():
                acc[...] = prod

            @pl.when(kk > 0)
            def _():
                acc[...] += prod

            @pl.when(kk == nk - 1)
            def _():
                o_ref[...] = acc[...].astype(out_dtype)

    body, more_specs, more_args = _ordered(body, 2, after)
    return pl.pallas_call(
        body, name=name, grid=(m // tm, r // tn, nk),
        in_specs=[pl.BlockSpec((tm, tk), lambda i, j, kk: (i, kk)),
                  pl.BlockSpec((None, tn, tk), lambda i, j, kk: (kk // cpk, j, kk % cpk))] + more_specs,
        out_specs=pl.BlockSpec((tm, tn), lambda i, j, kk: (i, j)),
        out_shape=jax.ShapeDtypeStruct((m, r), out_dtype),
        scratch_shapes=[pltpu.VMEM((tm, tn) if nk > 1 else (8, 128), F32)],
        compiler_params=_params("parallel", "parallel", "arbitrary"),
    )(a, w, *more_args)


MM_TN_VMEM_BYTES = 44 * 2 ** 20


def _mm_tn(a, b, *, n_shards, name, tm=1408, tn=1408):
    t, m = a.shape
    t2, n = b.shape
    assert t == t2 and a.dtype == BF16 and b.dtype == BF16
    c = n // n_shards
    tm, tn = _pick(m, tm), _pick(c, tn)
    if m // tm == 1 and n // tn == 1 and tn % (2 * LANES) == 0:
        tn //= 2
    fixed = 2 * tm * tn * 6
    if 4 * t * (tm + tn) + fixed <= MM_TN_VMEM_BYTES:
        tk = t
    else:
        tk = _pick(t, (MM_TN_VMEM_BYTES - fixed - 4 * tm * tn) // (4 * (tm + tn)), 8)
    nk, cpn = t // tk, c // tn

    def body(a_ref, b_ref, o_ref, ob_ref, acc):
        kk = pl.program_id(2)
        prod = lax.dot_general(a_ref[...], b_ref[...], (((0,), (0,)), ((), ())), preferred_element_type=F32)

        def finish(total):
            o_ref[...] = total
            ob_ref[...] = total.astype(BF16)

        if nk == 1:
            finish(prod)
        else:
            @pl.when(kk == 0)
            def _():
                acc[...] = prod

            @pl.when(kk > 0)
            def _():
                acc[...] += prod

            @pl.when(kk == nk - 1)
            def _():
                finish(acc[...])

    out_spec = pl.BlockSpec((None, tm, tn), lambda i, j, kk: (j // cpn, i, j % cpn))
    return pl.pallas_call(
        body, name=name, grid=(m // tm, n // tn, nk),
        in_specs=[pl.BlockSpec((tk, tm), lambda i, j, kk: (kk, i)),
                  pl.BlockSpec((tk, tn), lambda i, j, kk: (kk, j))],
        out_specs=[out_spec, out_spec],
        out_shape=[jax.ShapeDtypeStruct((n_shards, m, c), F32), jax.ShapeDtypeStruct((n_shards, m, c), BF16)],
        scratch_shapes=[pltpu.VMEM((tm, tn) if nk > 1 else (8, 128), F32)],
        compiler_params=_params("parallel", "parallel", "arbitrary"),
    )(a, b)


def _row_spec(tr, width, col=0):
    return pl.BlockSpec((tr, width), lambda i, col=col: (i, col))


def _vec_spec(width, col=0):
    return pl.BlockSpec((1, width), lambda i, col=col: (0, col))


def _accumulate(ref, value, first):
    @pl.when(first)
    def _():
        ref[...] = value

    @pl.when(jnp.logical_not(first))
    def _():
        ref[...] += value


def _rmsnorm_fwd(x, g, *, name, tr=512):
    t, d = x.shape
    tr = _pick(t, tr, 8)

    def body(x_ref, g_ref, o_ref):
        xv = x_ref[...]
        r = lax.rsqrt(jnp.mean(xv * xv, axis=-1, keepdims=True) + RMS_EPS)
        o_ref[...] = (xv * r * g_ref[...]).astype(BF16)

    return pl.pallas_call(
        body, name=name, grid=(t // tr,),
        in_specs=[_row_spec(tr, d), _vec_spec(d)], out_specs=_row_spec(tr, d),
        out_shape=jax.ShapeDtypeStruct((t, d), BF16), compiler_params=_params("parallel"),
    )(x, g)


def _rmsnorm_bwd(x, g, dy, dres, *, name, want_bf16, tr=512):
    t, d = x.shape
    tr = _pick(t, tr, 8)

    def body(x_ref, g_ref, dy_ref, dres_ref, *outs):
        dx_ref, dg_ref = outs[0], outs[-1]
        xv, dyv = x_ref[...], dy_ref[...].astype(F32)
        r = lax.rsqrt(jnp.mean(xv * xv, axis=-1, keepdims=True) + RMS_EPS)
        gy = dyv * g_ref[...]
        dx = dres_ref[...] + r * gy - xv * (r * r * r) * jnp.mean(xv * gy, axis=-1, keepdims=True)
        dx_ref[...] = dx
        if want_bf16:
            outs[1][...] = dx.astype(BF16)
        _accumulate(dg_ref, jnp.sum(dyv * xv * r, axis=0, keepdims=True), pl.program_id(0) == 0)

    out_shape = [jax.ShapeDtypeStruct((t, d), F32)]
    out_specs = [_row_spec(tr, d)]
    if want_bf16:
        out_shape.append(jax.ShapeDtypeStruct((t, d), BF16))
        out_specs.append(_row_spec(tr, d))
    out_shape.append(jax.ShapeDtypeStruct((1, d), F32))
    out_specs.append(_vec_spec(d))
    return pl.pallas_call(
        body, name=name, grid=(t // tr,),
        in_specs=[_row_spec(tr, d), _vec_spec(d), _row_spec(tr, d), _row_spec(tr, d)],
        out_specs=out_specs, out_shape=out_shape, compiler_params=_params("arbitrary"),
    )(x, g, dy, dres)


def _head_mean(v, ones_ref, head_dim):
    hi = v.astype(BF16)
    lo = (v - hi.astype(F32)).astype(BF16)
    e = ones_ref[...]
    total = jnp.dot(hi, e, preferred_element_type=F32) + jnp.dot(lo, e, preferred_element_type=F32)
    return total * (1.0 / head_dim)


def _qkv_fwd(z, gq, gk, head_ones, dims, *, name, tr=256):
    t = z.shape[0]
    a = dims.n_heads * dims.head_dim
    tr = _pick(t, tr, 8)
    q_scale = dims.head_dim ** -0.5

    def body(q_ref, k_ref, v_ref, gq_ref, gk_ref, e_ref, qo_ref, ko_ref, vo_ref):
        qv, kv = q_ref[...], k_ref[...]
        rq = lax.rsqrt(_head_mean(qv * qv, e_ref, dims.head_dim) + RMS_EPS)
        rk = lax.rsqrt(_head_mean(kv * kv, e_ref, dims.head_dim) + RMS_EPS)
        qo_ref[...] = (qv * rq * gq_ref[...] * q_scale).astype(BF16)
        ko_ref[...] = (kv * rk * gk_ref[...]).astype(BF16)
        vo_ref[...] = v_ref[...].astype(BF16)

    return pl.pallas_call(
        body, name=name, grid=(t // tr,),
        in_specs=[_row_spec(tr, a, 2), _row_spec(tr, a, 3), _row_spec(tr, a, 4), _vec_spec(a), _vec_spec(a),
                  pl.BlockSpec((a, a), lambda i: (0, 0))],
        out_specs=[_row_spec(tr, a)] * 3, out_shape=[jax.ShapeDtypeStruct((t, a), BF16)] * 3,
        compiler_params=_params("parallel"),
    )(z, z, z, gq, gk, head_ones)


def _qkv_bwd(z, dqs, dks, dvs, gq, gk, head_ones, dims, *, name, tr=256):
    t = z.shape[0]
    a = dims.n_heads * dims.head_dim
    tr = _pick(t, tr, 8)
    q_scale = dims.head_dim ** -0.5
    ng = len(dqs)

    def body(*refs):
        q_ref, k_ref = refs[:2]
        dq_refs, dk_refs, dv_refs = refs[2:2 + ng], refs[2 + ng:2 + 2 * ng], refs[2 + 2 * ng:2 + 3 * ng]
        gq_ref, gk_ref, e_ref = refs[2 + 3 * ng:5 + 3 * ng]
        dz_ref, dgq_ref, dgk_ref = refs[5 + 3 * ng:]
        first = pl.program_id(0) == 0

        def norm_bwd(x_ref, d_refs, g_ref, scale, col, dg_ref):
            xv = x_ref[...]
            dy = sum(r[...] for r in d_refs) * scale
            r = lax.rsqrt(_head_mean(xv * xv, e_ref, dims.head_dim) + RMS_EPS)
            gy = dy * g_ref[...]
            dx = r * gy - xv * (r * r * r) * _head_mean(xv * gy, e_ref, dims.head_dim)
            dz_ref[:, col * a:(col + 1) * a] = dx.astype(BF16)
            _accumulate(dg_ref, jnp.sum(dy * xv * r, axis=0, keepdims=True), first)

        norm_bwd(q_ref, dq_refs, gq_ref, q_scale, 0, dgq_ref)
        norm_bwd(k_ref, dk_refs, gk_ref, 1.0, 1, dgk_ref)
        dz_ref[:, 2 * a:3 * a] = sum(r[...] for r in dv_refs).astype(BF16)

    in_specs = ([_row_spec(tr, a, 2), _row_spec(tr, a, 3)] + [_row_spec(tr, a)] * (3 * ng)
                + [_vec_spec(a), _vec_spec(a), pl.BlockSpec((a, a), lambda i: (0, 0))])
    return pl.pallas_call(
        body, name=name, grid=(t // tr,), in_specs=in_specs,
        out_specs=[_row_spec(tr, 3 * a), _vec_spec(a), _vec_spec(a)],
        out_shape=[jax.ShapeDtypeStruct((t, 3 * a), BF16)] + [jax.ShapeDtypeStruct((1, a), F32)] * 2,
        compiler_params=_params("arbitrary"),
    )(z, z, *dqs, *dks, *dvs, gq, gk, head_ones)


CONV_ROWS = 16


def _seq_specs(dims, ts, width, halo, col, *, nxt=False):
    nst, per = dims.seq // ts, ts // halo
    last = dims.tokens // halo - 1
    cur = pl.BlockSpec((ts, width), lambda b, i: (b * nst + i, col))
    if nxt:
        edge = pl.BlockSpec((halo, width), lambda b, i: (jnp.minimum((b * nst + i + 1) * per, last), col))
    else:
        edge = pl.BlockSpec((halo, width), lambda b, i: (jnp.maximum((b * nst + i) * per - 1, 0), col))
    return cur, edge


SUBLANES = 8


def _shifted_copies(buf, shifted):
    rows = shifted.shape[1]
    for s in range(1, SUBLANES):
        shifted[s - 1] = buf[pl.ds(s, rows), :]


def _window(buf, shifted, start, size):
    a, s = divmod(start, SUBLANES)
    src = buf if s == 0 else shifted.at[s - 1]
    return src[pl.ds(SUBLANES * a, size), :]


def _conv_branch_fwd(z, w, b, g, dims, *, name, ts=128):
    t, c, kw = z.shape[0], dims.d_model, dims.conv_width
    base = CONV_HALO - (kw - 1)

    def body(av_ref, hv_ref, ag_ref, hg_ref, w_ref, b_ref, g_ref, a1_ref, a3_ref, buf, shifted):
        i = pl.program_id(1)
        buf[CONV_HALO:, :] = av_ref[...].astype(F32) * _sigmoid(ag_ref[...].astype(F32))
        buf[0:CONV_HALO, :] = jnp.where(i > 0, hv_ref[...].astype(F32) * _sigmoid(hg_ref[...].astype(F32)), 0.0)
        _shifted_copies(buf, shifted)
        for r0 in range(0, ts, CONV_ROWS):
            acc = jnp.broadcast_to(b_ref[...], (CONV_ROWS, c))
            for k in range(kw):
                acc = acc + w_ref[k:k + 1, :] * _window(buf, shifted, r0 + base + k, CONV_ROWS)
            a1_ref[r0:r0 + CONV_ROWS, :] = acc
            a2 = acc * lax.rsqrt(jnp.mean(acc * acc, axis=-1, keepdims=True) + RMS_EPS) * g_ref[...]
            a3_ref[r0:r0 + CONV_ROWS, :] = (a2 * _sigmoid(a2)).astype(BF16)

    vec = pl.BlockSpec((1, c), lambda b, i: (0, 0))
    out = pl.BlockSpec((ts, c), lambda b, i: (b * (dims.seq // ts) + i, 0))
    return pl.pallas_call(
        body, name=name, grid=(dims.batch_local, dims.seq // ts),
        in_specs=[*_seq_specs(dims, ts, c, CONV_HALO, 0), *_seq_specs(dims, ts, c, CONV_HALO, 1),
                  pl.BlockSpec((CONV_HALO, c), lambda b, i: (0, 0)), vec, vec],
        out_specs=[out, out],
        out_shape=[jax.ShapeDtypeStruct((t, c), F32), jax.ShapeDtypeStruct((t, c), BF16)],
        scratch_shapes=[pltpu.VMEM((CONV_HALO + ts, c), F32),
                        pltpu.VMEM((SUBLANES - 1, CONV_HALO + ts - SUBLANES, c), F32)],
        compiler_params=_params("parallel", "parallel"),
    )(z, z, z, z, w, b, g)


def _conv_norm_bwd(da3, a1, g, *, name, tr=256):
    t, c = a1.shape
    tr = _pick(t, tr, 8)

    def body(d_ref, a_ref, g_ref, o_ref, dg_ref):
        a1v, gv = a_ref[...], g_ref[...]
        r = lax.rsqrt(jnp.mean(a1v * a1v, axis=-1, keepdims=True) + RMS_EPS)
        a2 = a1v * r * gv
        sg = _sigmoid(a2)
        da2 = d_ref[...].astype(F32) * sg * (1.0 + a2 * (1.0 - sg))
        gy = da2 * gv
        o_ref[...] = r * gy - a1v * (r * r * r) * jnp.mean(a1v * gy, axis=-1, keepdims=True)
        _accumulate(dg_ref, jnp.sum(da2 * a1v * r, axis=0, keepdims=True), pl.program_id(0) == 0)

    return pl.pallas_call(
        body, name=name, grid=(t // tr,),
        in_specs=[_row_spec(tr, c), _row_spec(tr, c), _vec_spec(c)],
        out_specs=[_row_spec(tr, c), _vec_spec(c)],
        out_shape=[jax.ShapeDtypeStruct((t, c), F32), jax.ShapeDtypeStruct((1, c), F32)],
        compiler_params=_params("arbitrary"),
    )(da3, a1, g)


def _conv_branch_bwd(da1, z, w, rest_of_dz, dims, *, name, ts=128):
    t, c, kw = z.shape[0], dims.d_model, dims.conv_width
    nst = dims.seq // ts
    base = CONV_HALO - (kw - 1)
    n_rest = len(rest_of_dz)
    total = 2 * c + sum(r.shape[1] for r in rest_of_dz)

    def body(d_ref, dn_ref, av_ref, hv_ref, ag_ref, hg_ref, w_ref, *more):
        rest_refs = more[:n_rest]
        dz_ref, dw_ref, db_ref, abuf, dbuf, ashift, dshift = more[n_rest:]
        col = 2 * c
        for r in rest_refs:
            dz_ref[:, col:col + r.shape[1]] = r[...]
            col += r.shape[1]
        i = pl.program_id(1)
        first = jnp.logical_and(pl.program_id(0) == 0, i == 0)
        abuf[CONV_HALO:, :] = av_ref[...].astype(F32) * _sigmoid(ag_ref[...].astype(F32))
        abuf[0:CONV_HALO, :] = jnp.where(i > 0, hv_ref[...].astype(F32) * _sigmoid(hg_ref[...].astype(F32)), 0.0)
        d1 = d_ref[...]
        dbuf[0:ts, :] = d1
        dbuf[ts:, :] = jnp.where(i < nst - 1, dn_ref[...], 0.0)
        _shifted_copies(abuf, ashift)
        _shifted_copies(dbuf, dshift)

        @pl.when(first)
        def _():
            dw_ref[...] = jnp.zeros_like(dw_ref)
            db_ref[...] = jnp.zeros_like(db_ref)

        db_ref[...] += jnp.sum(d1, axis=0, keepdims=True)
        for k in range(kw):
            dw_ref[k:k + 1, :] += jnp.sum(d1 * _window(abuf, ashift, base + k, ts), axis=0, keepdims=True)
        for r0 in range(0, ts, CONV_ROWS):
            acc = jnp.zeros((CONV_ROWS, c), F32)
            for k in range(kw):
                acc = acc + w_ref[k:k + 1, :] * _window(dbuf, dshift, r0 + (kw - 1) - k, CONV_ROWS)
            av = av_ref[r0:r0 + CONV_ROWS, :].astype(F32)
            sg = _sigmoid(ag_ref[r0:r0 + CONV_ROWS, :].astype(F32))
            dz_ref[r0:r0 + CONV_ROWS, 0:c] = (acc * sg).astype(BF16)
            dz_ref[r0:r0 + CONV_ROWS, c:2 * c] = (acc * av * sg * (1.0 - sg)).astype(BF16)

    cur, nxt = _seq_specs(dims, ts, c, CONV_HALO, 0, nxt=True)
    return pl.pallas_call(
        body, name=name, grid=(dims.batch_local, nst),
        in_specs=[cur, nxt, *_seq_specs(dims, ts, c, CONV_HALO, 0), *_seq_specs(dims, ts, c, CONV_HALO, 1),
                  pl.BlockSpec((CONV_HALO, c), lambda b, i: (0, 0))]
        + [pl.BlockSpec((ts, r.shape[1]), lambda b, i: (b * nst + i, 0)) for r in rest_of_dz],
        out_specs=[pl.BlockSpec((ts, total), lambda b, i: (b * nst + i, 0)),
                   pl.BlockSpec((CONV_HALO, c), lambda b, i: (0, 0)), pl.BlockSpec((1, c), lambda b, i: (0, 0))],
        out_shape=[jax.ShapeDtypeStruct((t, total), BF16), jax.ShapeDtypeStruct((CONV_HALO, c), F32),
                   jax.ShapeDtypeStruct((1, c), F32)],
        scratch_shapes=[pltpu.VMEM((CONV_HALO + ts, c), F32)] * 2
        + [pltpu.VMEM((SUBLANES - 1, CONV_HALO + ts - SUBLANES, c), F32)] * 2,
        compiler_params=_params("arbitrary", "arbitrary"),
    )(da1, da1, z, z, z, z, w, *rest_of_dz)


FFN_ROWS = 16
FFN_COLS = 256


def _ffn_chunks(ts, f):
    cw = _pick(f, FFN_COLS)
    return [(r0, c0, cw) for r0 in range(0, ts, FFN_ROWS) for c0 in range(0, f, cw)]


def _tap_sources(buf, moved, offsets, rows):
    taps, used = [], 0
    for off in offsets:
        if off % SUBLANES:
            moved[used] = buf[pl.ds(off, rows), :]
            taps.append((moved.at[used], 0))
            used += 1
        else:
            taps.append((buf, off))
    return taps


def _moved_copies(offsets):
    return sum(1 for off in offsets if off % SUBLANES)


def _taps_sum(taps, w_ref, init, r0, cols):
    for k, (src, off) in enumerate(taps):
        init = init + w_ref[k:k + 1, cols] * src[pl.ds(off + r0, init.shape[0]), cols]
    return init


def _ffn_bwd(dact, up, w, b, dims, *, name, ts=128):
    t, f, kw = up.shape[0], dims.d_ff, dims.ffn_conv_width
    nst = dims.seq // ts
    fwd_offsets = [FFN_HALO - (kw - 1) + k for k in range(kw)]
    bwd_offsets = [(kw - 1) - k for k in range(kw)]
    dact_halo = 2 * FFN_HALO

    def body(d_ref, dn_ref, up_ref, hp_ref, hn_ref, w_ref, b_ref, o_ref, dw_ref, db_ref, buf, moved, dbuf, dmoved):
        i = pl.program_id(1)
        first = jnp.logical_and(pl.program_id(0) == 0, i == 0)
        more = i < nst - 1
        buf[0:FFN_HALO, :] = jnp.where(i > 0, hp_ref[...], 0.0)
        buf[FFN_HALO:FFN_HALO + ts, :] = up_ref[...]
        buf[FFN_HALO + ts:, :] = hn_ref[...]
        taps = _tap_sources(buf, moved, fwd_offsets, ts + FFN_HALO)

        def du_chunk(r0, rows, c0, cw, d):
            vcols, gcols = slice(c0, c0 + cw), slice(f + c0, f + c0 + cw)
            uv = _taps_sum(taps, w_ref, jnp.broadcast_to(b_ref[:, vcols], (rows, cw)), r0, vcols)
            ug = _taps_sum(taps, w_ref, jnp.broadcast_to(b_ref[:, gcols], (rows, cw)), r0, gcols)
            sg = _sigmoid(ug)
            dbuf[r0:r0 + rows, vcols] = d * ug * sg
            dbuf[r0:r0 + rows, gcols] = d * uv * sg * (1.0 + ug * (1.0 - sg))

        for r0, c0, cw in _ffn_chunks(ts, f):
            du_chunk(r0, FFN_ROWS, c0, cw, d_ref[r0:r0 + FFN_ROWS, c0:c0 + cw].astype(F32))
        for _, c0, cw in _ffn_chunks(FFN_ROWS, f):
            d_next = dn_ref[:, c0:c0 + cw].astype(F32)[0:FFN_HALO]
            du_chunk(ts, FFN_HALO, c0, cw, jnp.where(more, d_next, 0.0))

        @pl.when(first)
        def _():
            dw_ref[...] = jnp.zeros_like(dw_ref)
            db_ref[...] = jnp.zeros_like(db_ref)

        du = dbuf[0:ts, :]
        db_ref[...] += jnp.sum(du, axis=0, keepdims=True)
        for k, (src, off) in enumerate(taps):
            dw_ref[k:k + 1, :] += jnp.sum(du * src[pl.ds(off, ts), :], axis=0, keepdims=True)

        dtaps = _tap_sources(dbuf, dmoved, bwd_offsets, ts)
        for r0, c0, cw in _ffn_chunks(ts, 2 * f):
            cols = slice(c0, c0 + cw)
            o_ref[r0:r0 + FFN_ROWS, cols] = _taps_sum(dtaps, w_ref, jnp.zeros((FFN_ROWS, cw), F32), r0, cols).astype(BF16)

    up_cur, up_prev = _seq_specs(dims, ts, 2 * f, FFN_HALO, 0)
    _, up_next = _seq_specs(dims, ts, 2 * f, FFN_HALO, 0, nxt=True)
    d_cur, d_next = _seq_specs(dims, ts, f, dact_halo, 0, nxt=True)
    full = lambda rows: pl.BlockSpec((rows, 2 * f), lambda b_, i: (0, 0))
    return pl.pallas_call(
        body, name=name, grid=(dims.batch_local, nst),
        in_specs=[d_cur, d_next, up_cur, up_prev, up_next, full(FFN_HALO), full(1)],
        out_specs=[pl.BlockSpec((ts, 2 * f), lambda b_, i: (b_ * nst + i, 0)), full(FFN_HALO), full(1)],
        out_shape=[jax.ShapeDtypeStruct((t, 2 * f), BF16), jax.ShapeDtypeStruct((FFN_HALO, 2 * f), F32),
                   jax.ShapeDtypeStruct((1, 2 * f), F32)],
        scratch_shapes=[pltpu.VMEM((ts + 2 * FFN_HALO, 2 * f), F32),
                        pltpu.VMEM((_moved_copies(fwd_offsets), ts + FFN_HALO, 2 * f), F32),
                        pltpu.VMEM((ts + FFN_HALO, 2 * f), F32),
                        pltpu.VMEM((_moved_copies(bwd_offsets), ts, 2 * f), F32)],
        compiler_params=_params("arbitrary", "arbitrary"),
    )(dact, dact, up, up, up, w, b)


def _ffn_act_fwd(up, w, b, dims, *, name, ts=128):
    t, f, kw = up.shape[0], dims.d_ff, dims.ffn_conv_width
    offsets = [FFN_HALO - (kw - 1) + k for k in range(kw)]

    def body(up_ref, h_ref, w_ref, b_ref, o_ref, buf, moved):
        buf[FFN_HALO:, :] = up_ref[...]
        buf[0:FFN_HALO, :] = jnp.where(pl.program_id(1) > 0, h_ref[...], 0.0)
        taps = _tap_sources(buf, moved, offsets, ts)
        for r0, c0, cw in _ffn_chunks(ts, f):
            vcols, gcols = slice(c0, c0 + cw), slice(f + c0, f + c0 + cw)
            uv = _taps_sum(taps, w_ref, jnp.broadcast_to(b_ref[:, vcols], (FFN_ROWS, cw)), r0, vcols)
            ug = _taps_sum(taps, w_ref, jnp.broadcast_to(b_ref[:, gcols], (FFN_ROWS, cw)), r0, gcols)
            o_ref[r0:r0 + FFN_ROWS, vcols] = (ug * _sigmoid(ug) * uv).astype(BF16)

    full = lambda rows: pl.BlockSpec((rows, 2 * f), lambda b_, i: (0, 0))
    return pl.pallas_call(
        body, name=name, grid=(dims.batch_local, dims.seq // ts),
        in_specs=[*_seq_specs(dims, ts, 2 * f, FFN_HALO, 0), full(FFN_HALO), full(1)],
        out_specs=pl.BlockSpec((ts, f), lambda b_, i: (b_ * (dims.seq // ts) + i, 0)),
        out_shape=jax.ShapeDtypeStruct((t, f), BF16),
        scratch_shapes=[pltpu.VMEM((FFN_HALO + ts, 2 * f), F32), pltpu.VMEM((_moved_copies(offsets), ts, 2 * f), F32)],
        compiler_params=_params("parallel", "parallel"),
    )(up, up, w, b)


def _ffn_act_bwd(dact, up, w, b, dims, *, name, ts=128):
    t, f, kw = up.shape[0], dims.d_ff, dims.ffn_conv_width
    offsets = [FFN_HALO - (kw - 1) + k for k in range(kw)]

    def body(d_ref, up_ref, h_ref, w_ref, b_ref, du_ref, dw_ref, db_ref, buf, moved):
        i = pl.program_id(1)
        first = jnp.logical_and(pl.program_id(0) == 0, i == 0)
        buf[FFN_HALO:, :] = up_ref[...]
        buf[0:FFN_HALO, :] = jnp.where(i > 0, h_ref[...], 0.0)
        taps = _tap_sources(buf, moved, offsets, ts)
        for r0, c0, cw in _ffn_chunks(ts, f):
            vcols, gcols = slice(c0, c0 + cw), slice(f + c0, f + c0 + cw)
            uv = _taps_sum(taps, w_ref, jnp.broadcast_to(b_ref[:, vcols], (FFN_ROWS, cw)), r0, vcols)
            ug = _taps_sum(taps, w_ref, jnp.broadcast_to(b_ref[:, gcols], (FFN_ROWS, cw)), r0, gcols)
            d = d_ref[r0:r0 + FFN_ROWS, vcols].astype(F32)
            sg = _sigmoid(ug)
            du_ref[r0:r0 + FFN_ROWS, vcols] = d * ug * sg
            du_ref[r0:r0 + FFN_ROWS, gcols] = d * uv * sg * (1.0 + ug * (1.0 - sg))

        @pl.when(first)
        def _():
            dw_ref[...] = jnp.zeros_like(dw_ref)
            db_ref[...] = jnp.zeros_like(db_ref)

        du = du_ref[...]
        db_ref[...] += jnp.sum(du, axis=0, keepdims=True)
        for k, (src, off) in enumerate(taps):
            dw_ref[k:k + 1, :] += jnp.sum(du * src[pl.ds(off, ts), :], axis=0, keepdims=True)

    nst = dims.seq // ts
    n_moved = _moved_copies(offsets)
    full = lambda rows: pl.BlockSpec((rows, 2 * f), lambda b_, i: (0, 0))
    return pl.pallas_call(
        body, name=name, grid=(dims.batch_local, nst),
        in_specs=[pl.BlockSpec((ts, f), lambda b_, i: (b_ * nst + i, 0)),
                  *_seq_specs(dims, ts, 2 * f, FFN_HALO, 0), full(FFN_HALO), full(1)],
        out_specs=[pl.BlockSpec((ts, 2 * f), lambda b_, i: (b_ * nst + i, 0)), full(FFN_HALO), full(1)],
        out_shape=[jax.ShapeDtypeStruct((t, 2 * f), F32), jax.ShapeDtypeStruct((FFN_HALO, 2 * f), F32),
                   jax.ShapeDtypeStruct((1, 2 * f), F32)],
        scratch_shapes=[pltpu.VMEM((FFN_HALO + ts, 2 * f), F32), pltpu.VMEM((n_moved, ts, 2 * f), F32)],
        compiler_params=_params("arbitrary", "arbitrary"),
    )(dact, up, up, w, b)


def _ffn_conv_bwd(du, w, dims, *, name, ts=128):
    t, f2 = du.shape
    kw = dims.ffn_conv_width
    nst = dims.seq // ts

    offsets = [(kw - 1) - k for k in range(kw)]

    def body(d_ref, dn_ref, w_ref, o_ref, buf, moved):
        buf[0:ts, :] = d_ref[...]
        buf[ts:, :] = jnp.where(pl.program_id(1) < nst - 1, dn_ref[...], 0.0)
        taps = _tap_sources(buf, moved, offsets, ts)
        for r0, c0, cw in _ffn_chunks(ts, f2):
            cols = slice(c0, c0 + cw)
            o_ref[r0:r0 + FFN_ROWS, cols] = _taps_sum(taps, w_ref, jnp.zeros((FFN_ROWS, cw), F32), r0, cols).astype(BF16)

    return pl.pallas_call(
        body, name=name, grid=(dims.batch_local, nst),
        in_specs=[*_seq_specs(dims, ts, f2, FFN_HALO, 0, nxt=True), pl.BlockSpec((FFN_HALO, f2), lambda b_, i: (0, 0))],
        out_specs=pl.BlockSpec((ts, f2), lambda b_, i: (b_ * nst + i, 0)),
        out_shape=jax.ShapeDtypeStruct((t, f2), BF16),
        scratch_shapes=[pltpu.VMEM((ts + FFN_HALO, f2), F32), pltpu.VMEM((_moved_copies(offsets), ts, f2), F32)],
        compiler_params=_params("parallel", "parallel"),
    )(du, du, w)


def _alibi_slope(h, n_heads):
    return 2.0 ** (-8.0 * (h + 1) / n_heads)


def _dot_nt(a, b):
    return lax.dot_general(a, b, (((1,), (1,)), ((), ())), preferred_element_type=F32)


def _dot_tn(a, b):
    return lax.dot_general(a, b, (((0,), (0,)), ((), ())), preferred_element_type=F32)


def _attn_view(x, dims, dil):
    return x.reshape(dims.batch_local, dims.seq // dil, dil * x.shape[-1])


def _attn_fwd_group(q, k, v, state, dims, dil, *, last, name):
    t, a = q.shape
    assert 2 * dims.head_dim == 128 and dims.n_heads % 2 == 0
    blk, hd = ATTN_BLOCK, dims.head_dim
    nb = dims.seq // dil // blk
    has_prev = nb > 1
    nkeys = 2 * blk if has_prev else blk

    def body(*refs):
        it = iter(refs)
        q_ref, kc_ref, vc_ref = next(it), next(it), next(it)
        kp_ref, vp_ref = (next(it), next(it)) if has_prev else (None, None)
        m_in, l_in, acc_in = (next(it), next(it), next(it)) if state is not None else (None, None, None)
        outs = list(it)
        iq = lax.broadcasted_iota(jnp.int32, (blk, nkeys), 0)
        jk = lax.broadcasted_iota(jnp.int32, (blk, nkeys), 1)
        if has_prev:
            steps = iq + blk - jk
            valid = (steps >= 0) & (steps <= blk) & ((jk >= blk) | (pl.program_id(2) > 0))
        else:
            steps = iq - jk
            valid = steps >= 0
        dist = steps.astype(F32) * float(dil)
        low = lax.broadcasted_iota(jnp.int32, (blk, 2 * hd), 1) < hd
        for hp in range(dims.n_heads // 2):
            sl = slice(2 * hd * hp, 2 * hd * (hp + 1))
            q2 = q_ref[:, sl]
            if has_prev:
                kcat = jnp.concatenate([kp_ref[:, sl], kc_ref[:, sl]], axis=0)
                vcat = jnp.concatenate([vp_ref[:, sl], vc_ref[:, sl]], axis=0)
            else:
                kcat, vcat = kc_ref[:, sl], vc_ref[:, sl]
            halves = []
            for half in range(2):
                col = 2 * hd * hp + hd * half
                qh = jnp.where(low if half == 0 else jnp.logical_not(low), q2, jnp.zeros_like(q2))
                sc = _dot_nt(qh, kcat) - _alibi_slope(2 * hp + half, dims.n_heads) * dist
                sc = jnp.where(valid, sc, MASKED_SCORE)
                row_max = jnp.max(sc, axis=-1, keepdims=True)
                if state is None:
                    m_new = row_max
                    p = jnp.exp(sc - m_new)
                    alpha = None
                    l_new = jnp.sum(p, axis=-1, keepdims=True)
                else:
                    m_old = m_in[:, col:col + 1]
                    m_new = jnp.maximum(m_old, row_max)
                    p = jnp.exp(sc - m_new)
                    alpha = jnp.exp(m_old - m_new)
                    l_new = alpha * l_in[:, col:col + 1] + jnp.sum(p, axis=-1, keepdims=True)
                pv = jnp.dot(p.astype(BF16), vcat, preferred_element_type=F32)
                halves.append((m_new, l_new, alpha, pv))
            (m_a, l_a, al_a, pv_a), (m_b, l_b, al_b, pv_b) = halves
            if state is None:
                acc = jnp.where(low, pv_a, pv_b)
            else:
                old = acc_in[:, sl]
                acc = jnp.where(low, al_a * old + pv_a, al_b * old + pv_b)
            m2 = jnp.where(low, m_a, m_b)
            l2 = jnp.where(low, l_a, l_b)
            if last:
                outs[0][:, sl] = (acc / l2).astype(BF16)
                outs[1][:, sl] = m2 + jnp.log(l2)
            else:
                outs[0][:, sl] = m2
                outs[1][:, sl] = l2
                outs[2][:, sl] = acc

    cur = pl.BlockSpec((None, blk, a), lambda b, r, i: (b, i, r))
    prev = pl.BlockSpec((None, blk, a), lambda b, r, i: (b, jnp.maximum(i - 1, 0), r))
    args, in_specs = [q, k, v], [cur, cur, cur]
    if has_prev:
        args += [k, v]
        in_specs += [prev, prev]
    if state is not None:
        args += list(state)
        in_specs += [cur] * 3
    shape = lambda dt: jax.ShapeDtypeStruct((dims.batch_local, dims.seq // dil, dil * a), dt)
    out_shape = [shape(BF16), shape(F32)] if last else [shape(F32)] * 3
    outs = pl.pallas_call(
        body, name=name, grid=(dims.batch_local, dil, nb),
        in_specs=in_specs, out_specs=[cur] * len(out_shape), out_shape=out_shape,
        compiler_params=_params("parallel", "parallel", "parallel"),
    )(*[_attn_view(x, dims, dil) for x in args])
    return tuple(o.reshape(t, a) for o in outs)


def _attn_delta(do, o, head_ones, dims, *, name, tr=512):
    t, a = o.shape
    tr = _pick(t, tr, 8)

    def body(do_ref, o_ref, e_ref, d_ref):
        prod = do_ref[...].astype(F32) * o_ref[...].astype(F32)
        d_ref[...] = _head_mean(prod, e_ref, dims.head_dim) * float(dims.head_dim)

    return pl.pallas_call(
        body, name=name, grid=(t // tr,),
        in_specs=[_row_spec(tr, a), _row_spec(tr, a), pl.BlockSpec((a, a), lambda i: (0, 0))],
        out_specs=_row_spec(tr, a), out_shape=jax.ShapeDtypeStruct((t, a), F32),
        compiler_params=_params("parallel"),
    )(do, o, head_ones)


def _attn_bwd_group(q, k, v, do, lse, delta, dims, dil, *, name):
    t, a = q.shape
    blk, hd = ATTN_BLOCK, dims.head_dim
    nb = dims.seq // dil // blk
    has_next = nb > 1

    def body(*refs):
        k_ref, v_ref, q_ref, do_ref, lse_ref, dl_ref = refs[:6]
        if has_next:
            qn_ref, don_ref, lsen_ref, dln_ref = refs[6:10]
            dq_ref, dk_ref, dv_ref, carry = refs[10:]
        else:
            dq_ref, dk_ref, dv_ref = refs[6:]
        j = pl.program_id(2)
        iq = lax.broadcasted_iota(jnp.int32, (blk, blk), 0)
        jk = lax.broadcasted_iota(jnp.int32, (blk, blk), 1)
        low = lax.broadcasted_iota(jnp.int32, (blk, 2 * hd), 1) < hd

        def pair(hp, qr, dor, lser, dlr, steps, valid):
            sl = slice(2 * hd * hp, 2 * hd * (hp + 1))
            q2, do2, k2, v2 = qr[:, sl], dor[:, sl], k_ref[:, sl], v_ref[:, sl]
            dist = steps.astype(F32) * float(dil)
            dq_h, dk2, dv2 = [], None, None
            for half in range(2):
                col = 2 * hd * hp + hd * half
                mask = low if half == 0 else jnp.logical_not(low)
                qh = jnp.where(mask, q2, jnp.zeros_like(q2))
                doh = jnp.where(mask, do2, jnp.zeros_like(do2))
                sc = _dot_nt(qh, k2) - _alibi_slope(2 * hp + half, dims.n_heads) * dist
                p = jnp.where(valid, jnp.exp(sc - lser[:, col:col + 1]), 0.0)
                ds = p * (_dot_nt(doh, v2) - dlr[:, col:col + 1])
                ds_b, p_b = ds.astype(BF16), p.astype(BF16)
                dq_h.append(jnp.dot(ds_b, k2, preferred_element_type=F32))
                dk_h, dv_h = _dot_tn(ds_b, qh), _dot_tn(p_b, doh)
                dk2 = dk_h if dk2 is None else dk2 + dk_h
                dv2 = dv_h if dv2 is None else dv2 + dv_h
            return sl, jnp.where(low, dq_h[0], dq_h[1]), dk2, dv2

        if has_next:
            @pl.when(j == 0)
            def _():
                carry[...] = jnp.zeros_like(carry)

        for hp in range(dims.n_heads // 2):
            sl, dq2, dk2, dv2 = pair(hp, q_ref, do_ref, lse_ref, dl_ref, iq - jk, iq >= jk)
            dq_ref[:, sl] = (carry[:, sl] + dq2) if has_next else dq2
            dk_ref[:, sl] = dk2
            dv_ref[:, sl] = dv2

        if has_next:
            @pl.when(j + 1 < nb)
            def _():
                for hp in range(dims.n_heads // 2):
                    sl, dq2, dk2, dv2 = pair(hp, qn_ref, don_ref, lsen_ref, dln_ref, iq - jk + blk, jk >= iq)
                    carry[:, sl] = dq2
                    dk_ref[:, sl] += dk2
                    dv_ref[:, sl] += dv2

    cur = pl.BlockSpec((None, blk, a), lambda b, r, j: (b, j, r))
    nxt = pl.BlockSpec((None, blk, a), lambda b, r, j: (b, jnp.minimum(j + 1, nb - 1), r))
    args, in_specs = [k, v, q, do, lse, delta], [cur] * 6
    if has_next:
        args += [q, do, lse, delta]
        in_specs += [nxt] * 4
    shape = jax.ShapeDtypeStruct((dims.batch_local, dims.seq // dil, dil * a), F32)
    outs = pl.pallas_call(
        body, name=name, grid=(dims.batch_local, dil, nb),
        in_specs=in_specs, out_specs=[cur] * 3, out_shape=[shape] * 3,
        scratch_shapes=[pltpu.VMEM((blk, a), F32)] if has_next else [],
        compiler_params=_params("parallel", "parallel", "arbitrary"),
    )(*[_attn_view(x, dims, dil) for x in args])
    return tuple(o.reshape(t, a) for o in outs)


LANES = 128
MASK_BIAS = 1e30
RESIDUE_DILATIONS = tuple(d for d in DILATIONS if d > 1)


def _rows_to_residues(value, out_ref, scr, d):
    rows, width = value.shape
    for c in range(width // LANES):
        cols = slice(LANES * c, LANES * (c + 1))
        scr[c] = value[:, cols]
        for r in range(d):
            out_ref[r, :, cols] = scr[c, pl.ds(r, rows // d, stride=d), :].astype(out_ref.dtype)


def _residues_to_rows(in_ref, scr, d):
    _, n, width = in_ref.shape
    slabs = []
    for c in range(width // LANES):
        cols = slice(LANES * c, LANES * (c + 1))
        for r in range(d):
            scr[c, pl.ds(r, n, stride=d), :] = in_ref[r, :, cols].astype(F32)
        slabs.append(scr[c])
    return slabs[0] if len(slabs) == 1 else jnp.concatenate(slabs, axis=1)


def _residue_shape(dims, d, width, dtype):
    return jax.ShapeDtypeStruct((dims.batch_local, d, dims.seq // d, width), dtype)


def _residue_spec(dims, d, tr, width):
    tiles = dims.seq // tr
    return pl.BlockSpec((None, d, tr // d, width), lambda i: (i // tiles, 0, i % tiles, 0))


def _head_sum_matrix(dims):
    a = dims.n_heads * dims.head_dim
    head = jnp.arange(a, dtype=jnp.int32) // dims.head_dim
    return (head[:, None] == jnp.arange(LANES, dtype=jnp.int32)[None, :]).astype(BF16)


def _two_pass_dot(v, m):
    hi = v.astype(BF16)
    lo = (v - hi.astype(F32)).astype(BF16)
    return jnp.dot(hi, m, preferred_element_type=F32) + jnp.dot(lo, m, preferred_element_type=F32)


def _qkv_layouts_fwd(z, gq, gk, head_ones, dims, *, name, tr=256):
    t = z.shape[0]
    a = dims.n_heads * dims.head_dim
    q_scale = dims.head_dim ** -0.5
    nres = len(RESIDUE_DILATIONS)

    def body(q_ref, k_ref, v_ref, gq_ref, gk_ref, sum_ref, spread_ref, *rest):
        outs, scr = rest[:-1], rest[-1]
        qv, kv = q_ref[...].astype(F32), k_ref[...].astype(F32)
        mean = lambda val: _two_pass_dot(_two_pass_dot(val, sum_ref[...]), spread_ref[...]) * (1.0 / dims.head_dim)
        rq = lax.rsqrt(mean(qv * qv) + RMS_EPS)
        rk = lax.rsqrt(mean(kv * kv) + RMS_EPS)
        values = (qv * rq * gq_ref[...] * q_scale, kv * rk * gk_ref[...], v_ref[...].astype(F32))
        for j, val in enumerate(values):
            outs[j][...] = val.astype(BF16)
            for g, d in enumerate(RESIDUE_DILATIONS):
                _rows_to_residues(val, outs[3 * (g + 1) + j], scr, d)

    out_specs = [_row_spec(tr, a)] * 3
    out_shape = [jax.ShapeDtypeStruct((t, a), BF16)] * 3
    for d in RESIDUE_DILATIONS:
        out_specs += [_residue_spec(dims, d, tr, a)] * 3
        out_shape += [_residue_shape(dims, d, a, BF16)] * 3
    outs = pl.pallas_call(
        body, name=name, grid=(t // tr,),
        in_specs=[_row_spec(tr, a, 2), _row_spec(tr, a, 3), _row_spec(tr, a, 4), _vec_spec(a), _vec_spec(a),
                  pl.BlockSpec((a, LANES), lambda i: (0, 0)), pl.BlockSpec((LANES, a), lambda i: (0, 0))],
        out_specs=out_specs, out_shape=out_shape,
        scratch_shapes=[pltpu.VMEM((a // LANES, tr, LANES), F32)],
        compiler_params=_params("parallel"),
    )(z, z, z, gq, gk, *head_ones)
    return {d: tuple(outs[3 * g:3 * g + 3]) for g, d in enumerate((1,) + RESIDUE_DILATIONS)}


def _attn_specs(dims, dil, width):
    blk = ATTN_BLOCK
    nb = dims.seq // dil // blk
    if dil == 1:
        grid = (dims.batch_local, nb)
        at = lambda f: pl.BlockSpec((blk, width), lambda b, i: (b * nb + f(i), 0))
    else:
        grid = (dims.batch_local, dil, nb)
        at = lambda f: pl.BlockSpec((None, None, blk, width), lambda b, r, i: (b, r, f(i), 0))
    return grid, at(lambda i: i), at(lambda i: jnp.maximum(i - 1, 0)), at(lambda i: jnp.minimum(i + 1, nb - 1))


def _head_slopes(n_heads):
    h = lax.broadcasted_iota(jnp.int32, (n_heads, 1, 1), 0).astype(F32)
    return jnp.exp((h + 1.0) * (-8.0 / n_heads * math.log(2.0)))


def _pair_masks(hd):
    low = lax.broadcasted_iota(jnp.int32, (1, 2 * hd), 1) < hd
    return low, jnp.logical_not(low)


def _attn_fwd(q, k, v, dims, dil, *, name):
    a = dims.n_heads * dims.head_dim
    heads, hd, blk = dims.n_heads, dims.head_dim, ATTN_BLOCK
    assert 2 * hd == LANES and heads % 2 == 0 and heads <= LANES
    nb = dims.seq // dil // blk
    has_prev = nb > 1
    nkeys = 2 * blk if has_prev else blk
    grid, cur, prev, _ = _attn_specs(dims, dil, a)
    _, cur_stat, _, _ = _attn_specs(dims, dil, LANES)

    def body(*refs):
        if has_prev:
            q_ref, kc_ref, vc_ref, kp_ref, vp_ref, o_ref, lse_ref, s_scr, p_scr, k_st, v_st = refs
            k_st[0:blk, :], k_st[blk:, :] = kp_ref[...], kc_ref[...]
            v_st[0:blk, :], v_st[blk:, :] = vp_ref[...], vc_ref[...]
        else:
            q_ref, k_st, v_st, o_ref, lse_ref, s_scr, p_scr = refs
        low, high = _pair_masks(hd)

        for hp in range(heads // 2):
            sl = slice(LANES * hp, LANES * (hp + 1))
            q2 = q_ref[:, sl]
            kcat = k_st[:, sl]
            s_scr[2 * hp] = _dot_nt(jnp.where(low, q2, jnp.zeros_like(q2)), kcat)
            s_scr[2 * hp + 1] = _dot_nt(jnp.where(high, q2, jnp.zeros_like(q2)), kcat)

        iq = lax.broadcasted_iota(jnp.int32, (blk, nkeys), 0)
        jk = lax.broadcasted_iota(jnp.int32, (blk, nkeys), 1)
        if has_prev:
            steps = iq + blk - jk
            valid = (steps >= 0) & (steps <= blk) & ((jk >= blk) | (pl.program_id(len(grid) - 1) > 0))
        else:
            steps = iq - jk
            valid = steps >= 0
        bias = jnp.where(valid, steps.astype(F32) * (-float(dil)), -MASK_BIAS)
        s = s_scr[...] + _head_slopes(heads) * bias[None]
        m = jnp.max(s, axis=-1, keepdims=True)
        p = jnp.exp(s - m)
        l = jnp.sum(p, axis=-1, keepdims=True)
        p_scr[...] = p.astype(BF16)
        inv = 1.0 / l
        lse = m + jnp.log(l)

        lane = lax.broadcasted_iota(jnp.int32, (blk, LANES), 1)
        stat = jnp.zeros((blk, LANES), F32)
        for hp in range(heads // 2):
            sl = slice(LANES * hp, LANES * (hp + 1))
            vcat = v_st[:, sl]
            pv_a = jnp.dot(p_scr[2 * hp], vcat, preferred_element_type=F32) * inv[2 * hp]
            pv_b = jnp.dot(p_scr[2 * hp + 1], vcat, preferred_element_type=F32) * inv[2 * hp + 1]
            o_ref[:, sl] = jnp.where(low, pv_a, pv_b)
            stat = jnp.where(lane == 2 * hp, lse[2 * hp], stat)
            stat = jnp.where(lane == 2 * hp + 1, lse[2 * hp + 1], stat)
        lse_ref[...] = stat

    lead = q.shape[:-2]
    rows = q.shape[-2]
    o, lse = pl.pallas_call(
        body, name=name, grid=grid,
        in_specs=[cur, cur, cur] + ([prev, prev] if has_prev else []),
        out_specs=[cur, cur_stat],
        out_shape=[jax.ShapeDtypeStruct(lead + (rows, a), F32), jax.ShapeDtypeStruct(lead + (rows, LANES), F32)],
        scratch_shapes=[pltpu.VMEM((heads, blk, nkeys), F32), pltpu.VMEM((heads, blk, nkeys), BF16)]
        + ([pltpu.VMEM((nkeys, a), BF16)] * 2 if has_prev else []),
        compiler_params=_params(*["parallel"] * len(grid)),
    )(q, k, v, *([k, v] if has_prev else []))
    return o, lse


def _attn_fwd_keys_major(q, k, v, dims, dil, *, name):
    a = dims.n_heads * dims.head_dim
    heads, hd, blk = dims.n_heads, dims.head_dim, ATTN_BLOCK
    assert 2 * hd == LANES and heads % 2 == 0 and heads <= LANES
    nb = dims.seq // dil // blk
    has_prev = nb > 1
    nkeys = 2 * blk if has_prev else blk
    grid, cur, prev, _ = _attn_specs(dims, dil, a)
    _, cur_stat, _, _ = _attn_specs(dims, dil, LANES)

    def body(*refs):
        if has_prev:
            q_ref, kc_ref, vc_ref, kp_ref, vp_ref, o_ref, lse_ref, s_scr, p_scr, k_st, v_st = refs
            k_st[0:blk, :], k_st[blk:, :] = kp_ref[...], kc_ref[...]
            v_st[0:blk, :], v_st[blk:, :] = vp_ref[...], vc_ref[...]
        else:
            q_ref, k_st, v_st, o_ref, lse_ref, s_scr, p_scr = refs
        low, high = _pair_masks(hd)
        for hp in range(heads // 2):
            sl = slice(LANES * hp, LANES * (hp + 1))
            q2, kcat = q_ref[:, sl], k_st[:, sl]
            s_scr[2 * hp] = _dot_nt(kcat, jnp.where(low, q2, jnp.zeros_like(q2)))
            s_scr[2 * hp + 1] = _dot_nt(kcat, jnp.where(high, q2, jnp.zeros_like(q2)))

        jk = lax.broadcasted_iota(jnp.int32, (nkeys, blk), 0)
        iq = lax.broadcasted_iota(jnp.int32, (nkeys, blk), 1)
        if has_prev:
            steps = iq + blk - jk
            valid = (steps >= 0) & (steps <= blk) & ((jk >= blk) | (pl.program_id(len(grid) - 1) > 0))
        else:
            steps = iq - jk
            valid = steps >= 0
        bias = jnp.where(valid, steps.astype(F32) * (-float(dil)), -MASK_BIAS)
        s = s_scr[...] + _head_slopes(heads) * bias[None]
        m = jnp.max(s, axis=1, keepdims=True)
        p = jnp.exp(s - m)
        l = jnp.sum(p, axis=1, keepdims=True)
        p_scr[...] = (p * (1.0 / l)).astype(BF16)
        lse = m + jnp.log(l)

        for hp in range(heads // 2):
            sl = slice(LANES * hp, LANES * (hp + 1))
            vcat = v_st[:, sl]
            o_ref[:, sl] = jnp.where(low, _dot_tn(p_scr[2 * hp], vcat), _dot_tn(p_scr[2 * hp + 1], vcat))
        row = lax.broadcasted_iota(jnp.int32, (LANES, blk), 0)
        by_head = jnp.zeros((LANES, blk), F32)
        for h in range(heads):
            by_head = jnp.where(row == h, lse[h], by_head)
        lse_ref[...] = jnp.transpose(by_head)

    lead = q.shape[:-2]
    rows = q.shape[-2]
    scratch = [pltpu.VMEM((heads, nkeys, blk), F32), pltpu.VMEM((heads, nkeys, blk), BF16)]
    if has_prev:
        scratch += [pltpu.VMEM((nkeys, a), BF16)] * 2
    o, lse = pl.pallas_call(
        body, name=name, grid=grid,
        in_specs=[cur, cur, cur] + ([prev, prev] if has_prev else []),
        out_specs=[cur, cur_stat],
        out_shape=[jax.ShapeDtypeStruct(lead + (rows, a), F32), jax.ShapeDtypeStruct(lead + (rows, LANES), F32)],
        scratch_shapes=scratch,
        compiler_params=_params(*["parallel"] * len(grid)),
    )(q, k, v, *([k, v] if has_prev else []))
    return o, lse


def _attn_combine(groups, head_spread, dims, *, name, tr=256):
    t = dims.tokens
    a = dims.n_heads * dims.head_dim
    dils = tuple(groups)

    def body(*refs):
        ins = refs[:2 * len(dils)]
        x_ref = refs[2 * len(dils)]
        o_ref = refs[2 * len(dils) + 1]
        lse_refs = refs[2 * len(dils) + 2:-2]
        scr, scr_stat = refs[-2], refs[-1]
        outs, stats = [], []
        for g, d in enumerate(dils):
            if d == 1:
                outs.append(ins[2 * g][...])
                stats.append(ins[2 * g + 1][...])
            else:
                outs.append(_residues_to_rows(ins[2 * g], scr, d))
                stats.append(_residues_to_rows(ins[2 * g + 1], scr_stat, d))
        top = functools.reduce(jnp.maximum, stats)
        weights = [jnp.exp(s - top) for s in stats]
        total = functools.reduce(jnp.add, weights)
        joint = top + jnp.log(total)
        inv = 1.0 / total
        acc = None
        for w, o in zip(weights, outs):
            term = _two_pass_dot(w * inv, x_ref[...]) * o
            acc = term if acc is None else acc + term
        o_ref[...] = acc.astype(BF16)
        for g, d in enumerate(dils):
            if d == 1:
                lse_refs[g][...] = joint
            else:
                _rows_to_residues(joint, lse_refs[g], scr_stat, d)

    in_specs, args, lse_specs, lse_shapes = [], [], [], []
    for d in dils:
        if d == 1:
            in_specs += [_row_spec(tr, a), _row_spec(tr, LANES)]
            lse_specs.append(_row_spec(tr, LANES))
            lse_shapes.append(jax.ShapeDtypeStruct((t, LANES), F32))
        else:
            in_specs += [_residue_spec(dims, d, tr, a), _residue_spec(dims, d, tr, LANES)]
            lse_specs.append(_residue_spec(dims, d, tr, LANES))
            lse_shapes.append(_residue_shape(dims, d, LANES, F32))
        args += list(groups[d])
    outs = pl.pallas_call(
        body, name=name, grid=(t // tr,),
        in_specs=in_specs + [pl.BlockSpec((LANES, a), lambda i: (0, 0))],
        out_specs=[_row_spec(tr, a)] + lse_specs,
        out_shape=[jax.ShapeDtypeStruct((t, a), BF16)] + lse_shapes,
        scratch_shapes=[pltpu.VMEM((a // LANES, tr, LANES), F32), pltpu.VMEM((1, tr, LANES), F32)],
        compiler_params=_params("parallel"),
    )(*args, head_spread)
    return outs[0], dict(zip(dils, outs[1:]))


def _attn_bwd_prep(do, o, head_sum, dims, *, name, tr=256):
    t, a = o.shape

    def body(do_ref, o_ref, e_ref, *rest):
        outs, scr, scr_stat = rest[:-2], rest[-2], rest[-1]
        dov = do_ref[...].astype(F32)
        delta = _two_pass_dot(dov * o_ref[...].astype(F32), e_ref[...])
        outs[0][...] = delta
        for g, d in enumerate(RESIDUE_DILATIONS):
            _rows_to_residues(dov, outs[1 + 2 * g], scr, d)
            _rows_to_residues(delta, outs[2 + 2 * g], scr_stat, d)

    out_specs, out_shape = [_row_spec(tr, LANES)], [jax.ShapeDtypeStruct((t, LANES), F32)]
    for d in RESIDUE_DILATIONS:
        out_specs += [_residue_spec(dims, d, tr, a), _residue_spec(dims, d, tr, LANES)]
        out_shape += [_residue_shape(dims, d, a, BF16), _residue_shape(dims, d, LANES, F32)]
    outs = pl.pallas_call(
        body, name=name, grid=(t // tr,),
        in_specs=[_row_spec(tr, a), _row_spec(tr, a), pl.BlockSpec((a, LANES), lambda i: (0, 0))],
        out_specs=out_specs, out_shape=out_shape,
        scratch_shapes=[pltpu.VMEM((a // LANES, tr, LANES), F32), pltpu.VMEM((1, tr, LANES), F32)],
        compiler_params=_params("parallel"),
    )(do, o, head_sum)
    dos, deltas = {1: do}, {1: outs[0]}
    for g, d in enumerate(RESIDUE_DILATIONS):
        dos[d], deltas[d] = outs[1 + 2 * g], outs[2 + 2 * g]
    return dos, deltas


def _attn_bwd(q, k, v, do, lse, delta, dims, dil, *, name):
    a = dims.n_heads * dims.head_dim
    heads, hd, blk = dims.n_heads, dims.head_dim, ATTN_BLOCK
    nb = dims.seq // dil // blk
    has_next = nb > 1
    nq = 2 * blk if has_next else blk
    grid, cur, _, nxt = _attn_specs(dims, dil, a)
    _, cur_stat, _, nxt_stat = _attn_specs(dims, dil, LANES)

    def body(*refs):
        k_ref, v_ref, q_ref, do_ref, lse_ref, dl_ref = refs[:6]
        if has_next:
            qn_ref, don_ref, lsen_ref, dln_ref = refs[6:10]
            dq_ref, dk_ref, dv_ref, s_scr, dp_scr, p_scr, ds_scr, carry = refs[10:]
        else:
            dq_ref, dk_ref, dv_ref, s_scr, dp_scr, p_scr, ds_scr = refs[6:]
        j = pl.program_id(len(grid) - 1)
        low, high = _pair_masks(hd)

        def stacked(ref, nref, sl):
            return jnp.concatenate([ref[:, sl], nref[:, sl]], axis=0) if has_next else ref[:, sl]

        def halves(x):
            return jnp.where(low, x, jnp.zeros_like(x)), jnp.where(high, x, jnp.zeros_like(x))

        for hp in range(heads // 2):
            sl = slice(LANES * hp, LANES * (hp + 1))
            k2, v2 = k_ref[:, sl], v_ref[:, sl]
            q_a, q_b = halves(stacked(q_ref, qn_ref if has_next else None, sl))
            do_a, do_b = halves(stacked(do_ref, don_ref if has_next else None, sl))
            s_scr[2 * hp], s_scr[2 * hp + 1] = _dot_nt(q_a, k2), _dot_nt(q_b, k2)
            dp_scr[2 * hp], dp_scr[2 * hp + 1] = _dot_nt(do_a, v2), _dot_nt(do_b, v2)

        rq = lax.broadcasted_iota(jnp.int32, (nq, blk), 0)
        jk = lax.broadcasted_iota(jnp.int32, (nq, blk), 1)
        if has_next:
            iq = jnp.where(rq < blk, rq, rq - blk)
            steps = jnp.where(rq < blk, iq - jk, iq - jk + blk)
            valid = ((rq < blk) & (iq >= jk)) | ((rq >= blk) & (jk >= iq) & (j + 1 < nb))
        else:
            steps, valid = rq - jk, rq >= jk
        bias = jnp.where(valid, steps.astype(F32) * (-float(dil)), -MASK_BIAS)
        lse_all = stacked(lse_ref, lsen_ref if has_next else None, slice(None))
        dl_all = stacked(dl_ref, dln_ref if has_next else None, slice(None))
        lse3 = jnp.stack([lse_all[:, h:h + 1] for h in range(heads)])
        dl3 = jnp.stack([dl_all[:, h:h + 1] for h in range(heads)])
        p = jnp.exp(s_scr[...] + _head_slopes(heads) * bias[None] - lse3)
        p_scr[...] = p.astype(BF16)
        ds_scr[...] = (p * (dp_scr[...] - dl3)).astype(BF16)

        if has_next:
            @pl.when(j == 0)
            def _():
                carry[...] = jnp.zeros_like(carry)

        for hp in range(heads // 2):
            sl = slice(LANES * hp, LANES * (hp + 1))
            k2 = k_ref[:, sl]
            q_a, q_b = halves(stacked(q_ref, qn_ref if has_next else None, sl))
            do_a, do_b = halves(stacked(do_ref, don_ref if has_next else None, sl))
            ds_a, ds_b = ds_scr[2 * hp], ds_scr[2 * hp + 1]
            dq2 = jnp.where(low, jnp.dot(ds_a, k2, preferred_element_type=F32),
                            jnp.dot(ds_b, k2, preferred_element_type=F32))
            dk_ref[:, sl] = _dot_tn(ds_a, q_a) + _dot_tn(ds_b, q_b)
            dv_ref[:, sl] = _dot_tn(p_scr[2 * hp], do_a) + _dot_tn(p_scr[2 * hp + 1], do_b)
            if has_next:
                dq_ref[:, sl] = carry[:, sl] + dq2[:blk]
                carry[:, sl] = dq2[blk:]
            else:
                dq_ref[:, sl] = dq2

    args, in_specs = [k, v, q, do, lse, delta], [cur] * 4 + [cur_stat] * 2
    if has_next:
        args += [q, do, lse, delta]
        in_specs += [nxt] * 2 + [nxt_stat] * 2
    shape = jax.ShapeDtypeStruct(q.shape, F32)
    scratch = [pltpu.VMEM((heads, nq, blk), F32)] * 2 + [pltpu.VMEM((heads, nq, blk), BF16)] * 2
    if has_next:
        scratch.append(pltpu.VMEM((blk, a), F32))
    return pl.pallas_call(
        body, name=name, grid=grid, in_specs=in_specs, out_specs=[cur] * 3, out_shape=[shape] * 3,
        scratch_shapes=scratch,
        compiler_params=_params(*["parallel"] * (len(grid) - 1), "arbitrary"),
    )(*args)


def _attn_bwd_keys_major(q, k, v, do, lse, delta, dims, dil, *, name):
    a = dims.n_heads * dims.head_dim
    heads, hd, blk = dims.n_heads, dims.head_dim, ATTN_BLOCK
    nb = dims.seq // dil // blk
    has_next = nb > 1
    nq = 2 * blk if has_next else blk
    grid, cur, _, nxt = _attn_specs(dims, dil, a)
    _, cur_stat, _, nxt_stat = _attn_specs(dims, dil, LANES)

    def body(*refs):
        k_ref, v_ref, q_ref, do_ref, lse_ref, dl_ref = refs[:6]
        if has_next:
            qn_ref, don_ref, lsen_ref, dln_ref = refs[6:10]
            dq_ref, dk_ref, dv_ref, q_st, do_st, s_scr, dp_scr, p_scr, ds_scr, carry = refs[10:]
        else:
            dq_ref, dk_ref, dv_ref, q_st, do_st, s_scr, dp_scr, p_scr, ds_scr = refs[6:]
        j = pl.program_id(len(grid) - 1)
        low, high = _pair_masks(hd)
        q_st[0:blk, :] = q_ref[...]
        do_st[0:blk, :] = do_ref[...]
        if has_next:
            q_st[blk:, :] = qn_ref[...]
            do_st[blk:, :] = don_ref[...]
            lse_all = jnp.concatenate([lse_ref[...], lsen_ref[...]], axis=0)
            dl_all = jnp.concatenate([dl_ref[...], dln_ref[...]], axis=0)
        else:
            lse_all, dl_all = lse_ref[...], dl_ref[...]
        lse_t, dl_t = jnp.transpose(lse_all), jnp.transpose(dl_all)
        lse3 = jnp.stack([lse_t[h:h + 1, :] for h in range(heads)])
        dl3 = jnp.stack([dl_t[h:h + 1, :] for h in range(heads)])

        def halves(x):
            return jnp.where(low, x, jnp.zeros_like(x)), jnp.where(high, x, jnp.zeros_like(x))

        for hp in range(heads // 2):
            sl = slice(LANES * hp, LANES * (hp + 1))
            k2, v2 = k_ref[:, sl], v_ref[:, sl]
            q_a, q_b = halves(q_st[:, sl])
            do_a, do_b = halves(do_st[:, sl])
            s_scr[2 * hp], s_scr[2 * hp + 1] = _dot_nt(k2, q_a), _dot_nt(k2, q_b)
            dp_scr[2 * hp], dp_scr[2 * hp + 1] = _dot_nt(v2, do_a), _dot_nt(v2, do_b)

        jk = lax.broadcasted_iota(jnp.int32, (blk, nq), 0)
        rq = lax.broadcasted_iota(jnp.int32, (blk, nq), 1)
        if has_next:
            iq = jnp.where(rq < blk, rq, rq - blk)
            steps = jnp.where(rq < blk, iq - jk, iq - jk + blk)
            valid = ((rq < blk) & (iq >= jk)) | ((rq >= blk) & (jk >= iq) & (j + 1 < nb))
        else:
            steps, valid = rq - jk, rq >= jk
        bias = jnp.where(valid, steps.astype(F32) * (-float(dil)), -MASK_BIAS)
        p = jnp.exp(s_scr[...] + _head_slopes(heads) * bias[None] - lse3)
        p_scr[...] = p.astype(BF16)
        ds_scr[...] = (p * (dp_scr[...] - dl3)).astype(BF16)

        if has_next:
            @pl.when(j == 0)
            def _():
                carry[...] = jnp.zeros_like(carry)

        for hp in range(heads // 2):
            sl = slice(LANES * hp, LANES * (hp + 1))
            k2 = k_ref[:, sl]
            q_a, q_b = halves(q_st[:, sl])
            do_a, do_b = halves(do_st[:, sl])
            ds_a, ds_b = ds_scr[2 * hp], ds_scr[2 * hp + 1]
            dk_ref[:, sl] = (jnp.dot(ds_a, q_a, preferred_element_type=F32)
                             + jnp.dot(ds_b, q_b, preferred_element_type=F32))
            dv_ref[:, sl] = (jnp.dot(p_scr[2 * hp], do_a, preferred_element_type=F32)
                             + jnp.dot(p_scr[2 * hp + 1], do_b, preferred_element_type=F32))
            dq2 = jnp.where(low, _dot_tn(ds_a, k2), _dot_tn(ds_b, k2))
            if has_next:
                dq_ref[:, sl] = carry[:, sl] + dq2[:blk]
                carry[:, sl] = dq2[blk:]
            else:
                dq_ref[:, sl] = dq2

    args, in_specs = [k, v, q, do, lse, delta], [cur] * 4 + [cur_stat] * 2
    if has_next:
        args += [q, do, lse, delta]
        in_specs += [nxt] * 2 + [nxt_stat] * 2
    shape = jax.ShapeDtypeStruct(q.shape, F32)
    scratch = ([pltpu.VMEM((nq, a), BF16)] * 2 + [pltpu.VMEM((heads, blk, nq), F32)] * 2
               + [pltpu.VMEM((heads, blk, nq), BF16)] * 2)
    if has_next:
        scratch.append(pltpu.VMEM((blk, a), F32))
    return pl.pallas_call(
        body, name=name, grid=grid, in_specs=in_specs, out_specs=[cur] * 3, out_shape=[shape] * 3,
        scratch_shapes=scratch,
        compiler_params=_params(*["parallel"] * (len(grid) - 1), "arbitrary"),
    )(*args)


def _qkv_layouts_bwd(z, grads, gq, gk, head_ones, dims, *, name, tr=256):
    t = z.shape[0]
    a = dims.n_heads * dims.head_dim
    q_scale = dims.head_dim ** -0.5
    dils = tuple(grads)

    def body(q_ref, k_ref, *rest):
        d_refs = rest[:3 * len(dils)]
        gq_ref, gk_ref, sum_ref, spread_ref, dz_ref, dgq_ref, dgk_ref, scr = rest[3 * len(dils):]
        first = pl.program_id(0) == 0
        mean = lambda val: _two_pass_dot(_two_pass_dot(val, sum_ref[...]), spread_ref[...]) * (1.0 / dims.head_dim)

        def total(j):
            acc = None
            for g, d in enumerate(dils):
                ref = d_refs[3 * g + j]
                part = ref[...] if d == 1 else _residues_to_rows(ref, scr, d)
                acc = part if acc is None else acc + part
            return acc

        def norm_bwd(x_ref, dy, g_ref, scale, col, dg_ref):
            xv = x_ref[...].astype(F32)
            dy = dy * scale
            r = lax.rsqrt(mean(xv * xv) + RMS_EPS)
            gy = dy * g_ref[...]
            dx = r * gy - xv * (r * r * r) * mean(xv * gy)
            dz_ref[:, col * a:(col + 1) * a] = dx.astype(BF16)
            _accumulate(dg_ref, jnp.sum(dy * xv * r, axis=0, keepdims=True), first)

        norm_bwd(q_ref, total(0), gq_ref, q_scale, 0, dgq_ref)
        norm_bwd(k_ref, total(1), gk_ref, 1.0, 1, dgk_ref)
        dz_ref[:, 2 * a:3 * a] = total(2).astype(BF16)

    in_specs, args = [_row_spec(tr, a, 2), _row_spec(tr, a, 3)], [z, z]
    for d in dils:
        in_specs += [_row_spec(tr, a) if d == 1 else _residue_spec(dims, d, tr, a)] * 3
        args += list(grads[d])
    in_specs += [_vec_spec(a), _vec_spec(a), pl.BlockSpec((a, LANES), lambda i: (0, 0)),
                 pl.BlockSpec((LANES, a), lambda i: (0, 0))]
    return pl.pallas_call(
        body, name=name, grid=(t // tr,), in_specs=in_specs,
        out_specs=[_row_spec(tr, 3 * a), _vec_spec(a), _vec_spec(a)],
        out_shape=[jax.ShapeDtypeStruct((t, 3 * a), BF16)] + [jax.ShapeDtypeStruct((1, a), F32)] * 2,
        scratch_shapes=[pltpu.VMEM((a // LANES, tr, LANES), F32)],
        compiler_params=_params("arbitrary"),
    )(*args, gq, gk, *head_ones)


def _mix_fwd(ya, yb, z, gate_b, dims, *, name, tr=512):
    t, d = ya.shape
    tr = _pick(t, tr, 8)
    first_gate_col = z.shape[1] // d - 2

    def body(ya_ref, yb_ref, ga_ref, gb_ref, ba_ref, bb_ref, o_ref):
        g_a = _sigmoid(ga_ref[...].astype(F32) + ba_ref[...])
        g_b = _sigmoid(gb_ref[...].astype(F32) + bb_ref[...])
        o_ref[...] = (g_a * ya_ref[...] + g_b * yb_ref[...]).astype(BF16)

    return pl.pallas_call(
        body, name=name, grid=(t // tr,),
        in_specs=[_row_spec(tr, d), _row_spec(tr, d), _row_spec(tr, d, first_gate_col),
                  _row_spec(tr, d, first_gate_col + 1), _vec_spec(d, 0), _vec_spec(d, 1)],
        out_specs=_row_spec(tr, d), out_shape=jax.ShapeDtypeStruct((t, d), BF16),
        compiler_params=_params("parallel"),
    )(ya, yb, z, z, gate_b, gate_b)


def _mix_bwd(dmix, ya, yb, z, gate_b, dims, *, name, tr=512):
    t, d = ya.shape
    tr = _pick(t, tr, 8)
    first_gate_col = z.shape[1] // d - 2

    def body(dm_ref, ya_ref, yb_ref, ga_ref, gb_ref, ba_ref, bb_ref, dya_ref, dyb_ref, dz_ref, db_ref):
        dm = dm_ref[...].astype(F32)
        g_a = _sigmoid(ga_ref[...].astype(F32) + ba_ref[...])
        g_b = _sigmoid(gb_ref[...].astype(F32) + bb_ref[...])
        dya_ref[...] = (dm * g_a).astype(BF16)
        dyb_ref[...] = (dm * g_b).astype(BF16)
        dl_a = dm * ya_ref[...] * g_a * (1.0 - g_a)
        dl_b = dm * yb_ref[...] * g_b * (1.0 - g_b)
        dz_ref[:, 0:d] = dl_a.astype(BF16)
        dz_ref[:, d:2 * d] = dl_b.astype(BF16)
        first = pl.program_id(0) == 0
        sums = jnp.concatenate([jnp.sum(dl_a, axis=0, keepdims=True), jnp.sum(dl_b, axis=0, keepdims=True)], axis=1)
        _accumulate(db_ref, sums, first)

    return pl.pallas_call(
        body, name=name, grid=(t // tr,),
        in_specs=[_row_spec(tr, d), _row_spec(tr, d), _row_spec(tr, d), _row_spec(tr, d, first_gate_col),
                  _row_spec(tr, d, first_gate_col + 1), _vec_spec(d, 0), _vec_spec(d, 1)],
        out_specs=[_row_spec(tr, d), _row_spec(tr, d), _row_spec(tr, 2 * d), _vec_spec(2 * d)],
        out_shape=[jax.ShapeDtypeStruct((t, d), BF16)] * 2 + [jax.ShapeDtypeStruct((t, 2 * d), BF16),
                                                              jax.ShapeDtypeStruct((1, 2 * d), F32)],
        compiler_params=_params("arbitrary"),
    )(dmix, ya, yb, z, z, gate_b, gate_b)


def _loss_head(y, target, *, name, tr=512):
    t, d = y.shape
    tr = _pick(t, tr, 8)

    def body(y_ref, t_ref, dy_ref, dyb_ref, loss_ref):
        err = y_ref[...] - t_ref[...]
        dy = err * (1.0 / d)
        dy_ref[...] = dy
        dyb_ref[...] = dy.astype(BF16)
        part = jnp.sum(jnp.sum(err * err, axis=-1, keepdims=True), axis=0, keepdims=True) * (0.5 / d)
        _accumulate(loss_ref, jnp.broadcast_to(part, (8, 128)), pl.program_id(0) == 0)

    return pl.pallas_call(
        body, name=name, grid=(t // tr,),
        in_specs=[_row_spec(tr, d), _row_spec(tr, d)],
        out_specs=[_row_spec(tr, d), _row_spec(tr, d), pl.BlockSpec((8, 128), lambda i: (0, 0))],
        out_shape=[jax.ShapeDtypeStruct((t, d), F32), jax.ShapeDtypeStruct((t, d), BF16),
                   jax.ShapeDtypeStruct((8, 128), F32)],
        compiler_params=_params("arbitrary"),
    )(y, target)


def _adamw(w, grads, m, v, *, name, tr=256):
    r, c = w.shape
    tr = _pick(r, tr, 8)
    ng = len(grads)
    c1 = 1.0 - ADAM_B1 ** ADAM_STEP
    c2 = 1.0 - ADAM_B2 ** ADAM_STEP

    def body(*refs):
        w_ref, g_refs, m_ref, v_ref = refs[0], refs[1:1 + ng], refs[1 + ng], refs[2 + ng]
        g_out, d_out, m_out, v_out = refs[3 + ng:]
        g = g_refs[0][...]
        for extra in g_refs[1:]:
            g = g + extra[...]
        m_new = ADAM_B1 * m_ref[...] + (1.0 - ADAM_B1) * g
        v_new = ADAM_B2 * v_ref[...] + (1.0 - ADAM_B2) * (g * g)
        g_out[...] = g
        m_out[...] = m_new
        v_out[...] = v_new
        d_out[...] = -ADAM_LR * ((m_new / c1) / (jnp.sqrt(v_new / c2) + ADAM_EPS) + ADAM_WD * w_ref[...])

    spec = pl.BlockSpec((tr, c), lambda i: (i, 0))
    return pl.pallas_call(
        body, name=name, grid=(r // tr,),
        in_specs=[spec] * (3 + ng), out_specs=[spec] * 4, out_shape=[jax.ShapeDtypeStruct((r, c), F32)] * 4,
        compiler_params=_params("parallel"),
    )(w, *grads, m, v)


CHIP_PEERS = ((1, 0), (0, 1), (1, 1))


def _place():
    return lax.axis_index("x"), lax.axis_index("y"), lax.axis_index("c")


HBM = pl.BlockSpec(memory_space=pltpu.HBM)
SEM = pl.BlockSpec(memory_space=pltpu.SEMAPHORE)
IN_FLIGHT = pltpu.SideEffectType.DATAFLOW_SIDE_EFFECTING


def _in_hbm(a):
    return pltpu.with_memory_space_constraint(a, pltpu.HBM)


def _cast_to_lands(shards, dtypes, *, name, after=None):
    n = len(shards)

    def body(*refs):
        ins, outs, bufs, sems = refs[:n], refs[n:2 * n], refs[2 * n:3 * n], refs[3 * n]
        x, y, _ = _place()
        copies = []
        for a in range(n):
            bufs[a][...] = ins[a][...].astype(dtypes[a])
            cp = pltpu.make_async_copy(bufs[a], outs[a].at[2 * x + y], sems.at[a])
            cp.start()
            copies.append(cp)
        for cp in copies:
            cp.wait()

    body, more_specs, more_args = _ordered(body, n, after)
    return pl.pallas_call(
        body, name=name, in_specs=[pl.BlockSpec(memory_space=pltpu.VMEM)] * n + more_specs, out_specs=[ANY] * n,
        out_shape=[jax.ShapeDtypeStruct((N_CHIPS,) + s.shape, dt) for s, dt in zip(shards, dtypes)],
        scratch_shapes=[pltpu.VMEM(s.shape, dt) for s, dt in zip(shards, dtypes)] + [pltpu.SemaphoreType.DMA((n,))],
        compiler_params=pltpu.CompilerParams(vmem_limit_bytes=V7X_VMEM_LIMIT_BYTES),
    )(*shards, *more_args)


def _chip_copy(src, dst, send, recv, flip, place):
    x, y, c = place
    return pltpu.make_async_remote_copy(src_ref=src, dst_ref=dst, send_sem=send, recv_sem=recv,
                                        device_id=(x ^ flip[0], y ^ flip[1], c), device_id_type=MESH)


def _my_part(land, place, halved):
    block = land.at[2 * place[0] + place[1]]
    if not halved:
        return block
    rows = land.shape[1] // 2
    return block.at[pl.ds(pl.multiple_of(place[2] * rows, rows), rows)]


def _gather_start(lands, after, *, name, halved=()):
    n = len(lands)

    def body(*refs):
        ins, send, recv, token = refs[:n], refs[n + 1], refs[n + 2], refs[-1]
        place = _place()
        for a in range(n):
            part = _my_part(ins[a], place, a in halved)
            for p, flip in enumerate(CHIP_PEERS):
                k = 3 * a + p
                _chip_copy(part, part, send.at[k], recv.at[k], flip, place).start()
        token[...] = jnp.zeros_like(token)

    outs = pl.pallas_call(
        body, name=name, in_specs=[HBM] * n + [ANY],
        out_specs=(SEM, SEM, *[HBM] * n, pl.BlockSpec(memory_space=pltpu.VMEM)),
        out_shape=(pltpu.SemaphoreType.DMA((3 * n,)), pltpu.SemaphoreType.DMA((3 * n,)),
                   *[pltpu.HBM(l.shape, l.dtype) for l in lands], jax.ShapeDtypeStruct((8, 128), F32)),
        input_output_aliases={a: 2 + a for a in range(n)},
        compiler_params=pltpu.CompilerParams(has_side_effects=IN_FLIGHT),
    )(*[_in_hbm(l) for l in lands], after)
    return outs[0], outs[1], list(outs[2:2 + n]), outs[-1]


def _gather_wait(send, recv, lands, after, *, name, halved=()):
    n = len(lands)

    def body(*refs):
        ins, send_ref, recv_ref = refs[:n], refs[n], refs[n + 1]
        place = _place()
        for a in range(n):
            part = _my_part(ins[a], place, a in halved)
            for p, flip in enumerate(CHIP_PEERS):
                k = 3 * a + p
                cp = _chip_copy(part, part, send_ref.at[k], recv_ref.at[k], flip, place)
                cp.wait_send()
                cp.wait_recv()

    after = list(after) if isinstance(after, (list, tuple)) else [after]
    return pl.pallas_call(
        body, name=name, in_specs=[HBM] * n + [SEM, SEM] + [ANY] * len(after), out_specs=[HBM] * n,
        out_shape=[pltpu.HBM(l.shape, l.dtype) for l in lands],
        input_output_aliases={a: a for a in range(n)},
        compiler_params=pltpu.CompilerParams(has_side_effects=IN_FLIGHT),
    )(*lands, send, recv, *after)


def _forward_to_sibling(land, *, name):
    rows = land.shape[1] // 2

    def body(land_ref, out_ref, send, recv):
        x, y, c = _place()
        copies = []
        for p, (fx, fy) in enumerate(CHIP_PEERS):
            chip = 2 * (x ^ fx) + (y ^ fy)
            mine = pl.ds(pl.multiple_of(c * rows, rows), rows)
            theirs = pl.ds(pl.multiple_of((1 - c) * rows, rows), rows)
            out = pltpu.make_async_remote_copy(
                src_ref=land_ref.at[chip].at[mine], dst_ref=out_ref.at[chip].at[mine], send_sem=send.at[p],
                recv_sem=recv.at[p], device_id=(x, y, 1 - c), device_id_type=MESH)
            out.start()
            copies.append((out, pltpu.make_async_remote_copy(
                src_ref=land_ref.at[chip].at[theirs], dst_ref=out_ref.at[chip].at[theirs], send_sem=send.at[p],
                recv_sem=recv.at[p], device_id=(x, y, 1 - c), device_id_type=MESH)))
        for out, arriving in copies:
            out.wait_send()
            arriving.wait_recv()

    return pl.pallas_call(
        body, name=name, in_specs=[ANY], out_specs=ANY, out_shape=jax.ShapeDtypeStruct(land.shape, land.dtype),
        input_output_aliases={0: 0},
        scratch_shapes=[pltpu.SemaphoreType.DMA((3,)), pltpu.SemaphoreType.DMA((3,))],
    )(land)


def _scatter_start(grad, *, name):
    def body(g_ref, land_ref, send, recv, g_thru, land_thru, token):
        place = _place()
        for p, flip in enumerate(CHIP_PEERS):
            peer_chip = 2 * (place[0] ^ flip[0]) + (place[1] ^ flip[1])
            _chip_copy(g_ref.at[peer_chip], land_ref.at[p], send.at[p], recv.at[p], flip, place).start()
        token[...] = jnp.zeros_like(token)

    land = lax.empty((3,) + grad.shape[1:], grad.dtype)
    return pl.pallas_call(
        body, name=name, in_specs=[HBM, HBM],
        out_specs=(SEM, SEM, HBM, HBM, pl.BlockSpec(memory_space=pltpu.VMEM)),
        out_shape=(pltpu.SemaphoreType.DMA((3,)), pltpu.SemaphoreType.DMA((3,)), pltpu.HBM(grad.shape, grad.dtype),
                   pltpu.HBM(land.shape, land.dtype), jax.ShapeDtypeStruct((8, 128), F32)),
        input_output_aliases={0: 2, 1: 3},
        compiler_params=pltpu.CompilerParams(has_side_effects=IN_FLIGHT),
    )(_in_hbm(grad), _in_hbm(land))


def _scatter_wait(started, after, *, name):
    n = len(started)

    def body(*refs):
        grads, lands = refs[:n], refs[n:2 * n]
        sends, recvs = refs[2 * n:3 * n], refs[3 * n:4 * n]
        place = _place()
        for a in range(n):
            for p, flip in enumerate(CHIP_PEERS):
                cp = _chip_copy(grads[a].at[0], lands[a].at[p], sends[a].at[p], recvs[a].at[p], flip, place)
                cp.wait_send()
                cp.wait_recv()

    grads, lands = [s[2] for s in started], [s[3] for s in started]
    after = list(after) if isinstance(after, (list, tuple)) else [after]
    outs = pl.pallas_call(
        body, name=name, in_specs=[HBM] * (2 * n) + [SEM] * (2 * n) + [ANY] * len(after), out_specs=[HBM] * (2 * n),
        out_shape=[pltpu.HBM(a.shape, a.dtype) for a in grads + lands],
        input_output_aliases={a: a for a in range(2 * n)},
        compiler_params=pltpu.CompilerParams(has_side_effects=IN_FLIGHT),
    )(*grads, *lands, *[s[0] for s in started], *[s[1] for s in started], *after)
    return list(zip(outs[:n], outs[n:]))


def _sibling_copy(src, dst, send, recv, place):
    x, y, c = place
    return pltpu.make_async_remote_copy(src_ref=src, dst_ref=dst, send_sem=send, recv_sem=recv,
                                        device_id=(x, y, 1 - c), device_id_type=MESH)


def _swap_start(arrays, *, name):
    n = len(arrays)

    def body(*refs):
        ins, lands, send, recv, token = refs[:n], refs[n:2 * n], refs[2 * n], refs[2 * n + 1], refs[-1]
        place = _place()
        for a in range(n):
            _sibling_copy(ins[a], lands[a], send.at[a], recv.at[a], place).start()
        token[...] = jnp.zeros_like(token)

    both = [_in_hbm(a) for a in arrays] + [_in_hbm(lax.empty(a.shape, a.dtype)) for a in arrays]
    outs = pl.pallas_call(
        body, name=name, in_specs=[HBM] * (2 * n),
        out_specs=(SEM, SEM, *[HBM] * (2 * n), pl.BlockSpec(memory_space=pltpu.VMEM)),
        out_shape=(pltpu.SemaphoreType.DMA((n,)), pltpu.SemaphoreType.DMA((n,)),
                   *[pltpu.HBM(a.shape, a.dtype) for a in both], jax.ShapeDtypeStruct((8, 128), F32)),
        input_output_aliases={a: 2 + a for a in range(2 * n)},
        compiler_params=pltpu.CompilerParams(has_side_effects=IN_FLIGHT),
    )(*both)
    return outs[0], outs[1], list(outs[2:2 + n]), list(outs[2 + n:2 + 2 * n]), outs[-1]


def _swap_wait(started, after, *, name):
    send, recv, arrays, lands = started[:4]
    n = len(arrays)

    def body(*refs):
        ins, zones, send_ref, recv_ref = refs[:n], refs[n:2 * n], refs[2 * n], refs[2 * n + 1]
        place = _place()
        for a in range(n):
            cp = _sibling_copy(ins[a], zones[a], send_ref.at[a], recv_ref.at[a], place)
            cp.wait_send()
            cp.wait_recv()

    after = list(after) if isinstance(after, (list, tuple)) else [after]
    outs = pl.pallas_call(
        body, name=name, in_specs=[HBM] * (2 * n) + [SEM, SEM] + [ANY] * len(after), out_specs=[HBM] * (2 * n),
        out_shape=[pltpu.HBM(a.shape, a.dtype) for a in arrays + lands],
        input_output_aliases={a: a for a in range(2 * n)},
        compiler_params=pltpu.CompilerParams(has_side_effects=IN_FLIGHT),
    )(*arrays, *lands, send, recv, *after)
    return list(outs[:n]), list(outs[n:])


def _allreduce_start(packed, *, name):
    n_dev = 8

    def body(src_ref, land_ref, send, recv, src_thru, land_thru, token):
        x, y, c = _place()
        me = 4 * x + 2 * y + c
        for p in range(1, n_dev):
            pltpu.make_async_remote_copy(
                src_ref=src_ref, dst_ref=land_ref.at[me], send_sem=send.at[p - 1], recv_sem=recv.at[p - 1],
                device_id=(x ^ (p >> 2), y ^ ((p >> 1) & 1), c ^ (p & 1)), device_id_type=MESH).start()
        token[...] = jnp.zeros_like(token)

    land = lax.empty((n_dev,) + packed.shape, packed.dtype)
    return pl.pallas_call(
        body, name=name, in_specs=[HBM, HBM],
        out_specs=(SEM, SEM, HBM, HBM, pl.BlockSpec(memory_space=pltpu.VMEM)),
        out_shape=(pltpu.SemaphoreType.DMA((n_dev - 1,)), pltpu.SemaphoreType.DMA((n_dev - 1,)),
                   pltpu.HBM(packed.shape, packed.dtype), pltpu.HBM(land.shape, land.dtype),
                   jax.ShapeDtypeStruct((8, 128), F32)),
        input_output_aliases={0: 2, 1: 3},
        compiler_params=pltpu.CompilerParams(has_side_effects=IN_FLIGHT),
    )(_in_hbm(packed), _in_hbm(land))


def _allreduce_wait(started, after, *, name):
    send, recv, packed, land = started[:4]
    n_dev = 8

    def body(src_ref, land_ref, send_ref, recv_ref, *_):
        x, y, c = _place()
        for p in range(1, n_dev):
            cp = pltpu.make_async_remote_copy(
                src_ref=src_ref, dst_ref=land_ref.at[0], send_sem=send_ref.at[p - 1], recv_sem=recv_ref.at[p - 1],
                device_id=(x ^ (p >> 2), y ^ ((p >> 1) & 1), c ^ (p & 1)), device_id_type=MESH)
            cp.wait_send()
            cp.wait_recv()

    after = list(after) if isinstance(after, (list, tuple)) else [after]
    return pl.pallas_call(
        body, name=name, in_specs=[HBM, HBM, SEM, SEM] + [ANY] * len(after), out_specs=[HBM, HBM],
        out_shape=[pltpu.HBM(packed.shape, packed.dtype), pltpu.HBM(land.shape, land.dtype)],
        input_output_aliases={0: 0, 1: 1},
        compiler_params=pltpu.CompilerParams(has_side_effects=IN_FLIGHT),
    )(packed, land, send, recv, *after)


def _sum_devices(mine, land, *, name):
    n_dev = land.shape[0]

    def body(mine_ref, land_ref, out_ref):
        x, y, c = _place()
        me = 4 * x + 2 * y + c
        total = None
        for s in range(n_dev):
            part = jnp.where(me == s, mine_ref[...], land_ref[s])
            total = part if total is None else total + part
        out_ref[...] = total

    return pl.pallas_call(body, name=name, out_shape=jax.ShapeDtypeStruct(mine.shape, mine.dtype))(mine, land)


def _sum_received(grad, land, *, name, tr=256):
    _, r, c = grad.shape
    tr = _pick(r, tr, 8)

    def body(chip_ref, g_ref, l_ref, o_ref):
        o_ref[...] = ((g_ref[...] + l_ref[0].astype(F32)) + l_ref[1].astype(F32)) + l_ref[2].astype(F32)

    chip = (2 * lax.axis_index("x") + lax.axis_index("y")).astype(jnp.int32).reshape(1)
    return pl.pallas_call(
        body, name=name,
        grid_spec=pltpu.PrefetchScalarGridSpec(
            num_scalar_prefetch=1, grid=(r // tr,),
            in_specs=[pl.BlockSpec((None, tr, c), lambda i, chip_ref: (chip_ref[0], i, 0)),
                      pl.BlockSpec((3, tr, c), lambda i, chip_ref: (0, i, 0))],
            out_specs=pl.BlockSpec((tr, c), lambda i, chip_ref: (i, 0))),
        out_shape=jax.ShapeDtypeStruct((r, c), F32), compiler_params=_params("parallel"),
    )(chip, grad, land)


def _allreduce_small(packed, *, name, after=None):
    r, d = packed.shape
    n_dev = 8

    def body(src_ref, out_ref, buf, send, recv):
        x, y, c = _place()
        me = 4 * x + 2 * y + c
        started = []
        for p in range(1, n_dev):
            rc = pltpu.make_async_remote_copy(
                src_ref=src_ref, dst_ref=buf.at[me], send_sem=send.at[p - 1], recv_sem=recv.at[p - 1],
                device_id=(x ^ (p >> 2), y ^ ((p >> 1) & 1), c ^ (p & 1)), device_id_type=MESH)
            rc.start()
            started.append(rc)
        buf[me] = src_ref[...]
        for rc in started:
            rc.wait()
        total = buf[0]
        for s in range(1, n_dev):
            total = total + buf[s]
        out_ref[...] = total

    vmem = pl.BlockSpec(memory_space=pltpu.VMEM)
    body, more_specs, more_args = _ordered(body, 1, after)
    return pl.pallas_call(
        body, name=name, in_specs=[vmem] + more_specs, out_specs=vmem, out_shape=jax.ShapeDtypeStruct((r, d), F32),
        scratch_shapes=[pltpu.VMEM((n_dev, r, d), F32), pltpu.SemaphoreType.DMA((n_dev - 1,)),
                        pltpu.SemaphoreType.DMA((n_dev - 1,))],
    )(packed, *more_args)


def _packed_rows(size, d):
    return -(-size // (8 * d)) * 8


def _pack_rows(arrays, d):
    rows = []
    for arr in arrays:
        flat = arr.reshape(-1).astype(F32)
        n = _packed_rows(flat.shape[0], d)
        rows.append(jnp.pad(flat, (0, n * d - flat.shape[0])).reshape(n, d))
    return jnp.concatenate(rows, axis=0)


def _unpack_rows(packed, shapes, d):
    out, row = [], 0
    for shape in shapes:
        size = math.prod(shape)
        n = _packed_rows(size, d)
        out.append(packed[row:row + n].reshape(-1)[:size].reshape(shape))
        row += n
    return out


SMALL = ("norm1_g", "gate_b", "conv_b", "conv_norm_g", "q_norm_g", "k_norm_g", "norm2_g", "ffn_conv_b")
LARGE = ("w_in", "w_conv_out", "w_attn_out", "w_out", "w_up", "w_down")
WEIGHTS = ("norm1_g", "w_in", "gate_b", "conv_w", "conv_b", "conv_norm_g", "w_conv_out", "q_norm_g", "k_norm_g",
           "w_attn_out", "w_out", "norm2_g", "w_up", "ffn_conv_w", "ffn_conv_b", "w_down")


def _head_ones(dims):
    a = dims.n_heads * dims.head_dim
    head = jnp.arange(a, dtype=jnp.int32) // dims.head_dim
    return (head[:, None] == head[None, :]).astype(BF16)


def _after(vec, token):
    return vec if token is None else vec + token[0:1, 0:1]


def _local_step(dims, x, target, small, first_weights, other_weights, send_grad):
    d, f, heads = dims.d_model, dims.d_ff, dims.n_heads
    small = dict(small)
    row = lambda name: small[name].reshape(1, -1)
    head_sum = _head_sum_matrix(dims)
    head_spread = jnp.transpose(head_sum)
    ones = (head_sum, head_spread)
    gq = jnp.tile(row("q_norm_g"), (1, heads))
    gk = jnp.tile(row("k_norm_g"), (1, heads))
    one_shard = lambda w: w.reshape(1, -1, w.shape[-1])

    h = _rmsnorm_fwd(x, row("norm1_g"), name="norm1")
    full = first_weights(h)
    w_in = full["w_in"]
    conv_w = jnp.pad(full["conv_w"], ((0, CONV_HALO - dims.conv_width), (0, 0)))
    ffn_w = jnp.pad(full["ffn_conv_w"], ((0, FFN_HALO - dims.ffn_conv_width), (0, 0)))
    z = _mm_nn(h, w_in, out_dtype=BF16, after=full.get("token"), tm=2048, tn=1792, name="in_proj")
    a1, a3 = _conv_branch_fwd(z, conv_w, row("conv_b"), row("conv_norm_g"), dims, name="conv_branch")
    qkv = _qkv_layouts_fwd(z, gq, gk, ones, dims, name="qk_norm")
    per_group = {dil: _attn_fwd(*qkv[dil], dims, dil, name=f"attn_fwd_d{dil}") for dil in DILATIONS}
    o, lse = _attn_combine(per_group, head_spread, dims, name="attn_combine")
    full = other_weights(o)
    w_up = full["w_up"]
    w_co, w_ao, w_o, w_dn = (one_shard(full[k]) for k in ("w_conv_out", "w_attn_out", "w_out", "w_down"))
    ya = _mm_nn(a3, w_co, out_dtype=F32, name="conv_out_proj")
    yb = _mm_nn(o, w_ao, out_dtype=F32, name="attn_out_proj")
    mixed = _mix_fwd(ya, yb, z, row("gate_b"), dims, name="gate_mix")
    x1, h2 = _proj_residual_norm(mixed, w_o, x, row("norm2_g"), name="out_proj_norm2")
    up = _mm_nn(h2, w_up, out_dtype=F32, tm=2048, name="up_proj")
    act = _ffn_act_fwd(up, ffn_w, row("ffn_conv_b"), dims, name="ffn_act")
    dy, dy_b, loss = _proj_residual_loss(act, w_dn, x1, target, tm=512, name="down_proj_loss")

    grads = {}

    def large(name, g):
        grads[name], g_bf16 = g
        return send_grad(name, g_bf16)

    sent = large("w_down", _mm_tn(act, dy_b, n_shards=1, name="dw_down"))
    dact = _mm_nt(dy_b, w_dn, out_dtype=BF16, after=sent, name="d_act")
    dup, dfw, dfb = _ffn_bwd(dact, up, ffn_w, row("ffn_conv_b"), dims, name="ffn_bwd")
    grads["ffn_conv_w"], grads["ffn_conv_b"] = dfw[:dims.ffn_conv_width], dfb
    sent = large("w_up", _mm_tn(h2, dup, n_shards=N_CHIPS, name="dw_up"))
    dh2 = _mm_nt(dup, w_up, out_dtype=F32, after=sent, name="d_h2")
    dx1, dx1_b, grads["norm2_g"] = _rmsnorm_bwd(x1, row("norm2_g"), dh2, dy, want_bf16=True, name="norm2_bwd")
    sent = large("w_out", _mm_tn(mixed, dx1_b, n_shards=1, name="dw_out"))
    dmix = _mm_nt(dx1_b, w_o, out_dtype=F32, after=sent, name="d_mix")
    dya, dyb, dz_gate, grads["gate_b"] = _mix_bwd(dmix, ya, yb, z, row("gate_b"), dims, name="gate_mix_bwd")
    sent = large("w_attn_out", _mm_tn(o, dyb, n_shards=1, name="dw_attn_out"))
    do = _mm_nt(dyb, w_ao, out_dtype=BF16, after=sent, name="d_attn")
    dos, deltas = _attn_bwd_prep(do, o, head_sum, dims, name="attn_bwd_prep")
    dqkv = {dil: _attn_bwd_keys_major(*qkv[dil], dos[dil], lse[dil], deltas[dil], dims, dil, name=f"attn_bwd_d{dil}")
            for dil in DILATIONS}
    dz_qkv, dgq, dgk = _qkv_layouts_bwd(z, dqkv, gq, gk, ones, dims, name="qk_norm_bwd")
    grads["q_norm_g"] = dgq.reshape(heads, dims.head_dim).sum(axis=0)
    grads["k_norm_g"] = dgk.reshape(heads, dims.head_dim).sum(axis=0)
    sent = large("w_conv_out", _mm_tn(a3, dya, n_shards=1, name="dw_conv_out"))
    da3 = _mm_nt(dya, w_co, out_dtype=F32, after=sent, name="d_conv_act")
    da1, grads["conv_norm_g"] = _conv_norm_bwd(da3, a1, row("conv_norm_g"), name="conv_norm_bwd")
    dz, dcw, grads["conv_b"] = _conv_branch_bwd(da1, z, conv_w, [dz_qkv, dz_gate], dims, name="conv_branch_bwd")
    grads["conv_w"] = dcw[:dims.conv_width]
    sent = large("w_in", _mm_tn(h, dz, n_shards=N_CHIPS, name="dw_in"))
    dh = _mm_nt(dz, w_in, out_dtype=F32, after=sent, name="d_h")
    dx, grads["norm1_g"] = _rmsnorm_bwd(x, row("norm1_g"), dh, dx1, want_bf16=False, name="norm1_bwd")
    return loss, dx, grads


def _step(dims, x, target, w, m, v):
    d = dims.d_model
    t = dims.tokens
    sq = lambda a: a.reshape(a.shape[1:])
    w2, m2, v2 = ({k: sq(a) for k, a in grp.items()} for grp in (w, m, v))

    conv_pad = jnp.pad(w2["conv_w"], ((0, CONV_HALO - dims.conv_width), (0, 0)))
    ffn_pad = jnp.pad(w2["ffn_conv_w"], ((0, FFN_HALO - dims.ffn_conv_width), (0, 0)))
    first_names = ("w_in", "conv_w", "ffn_conv_w")
    other_names = tuple(k for k in LARGE if k not in first_names)
    lands = dict(zip(first_names, _cast_to_lands([w2["w_in"], conv_pad, ffn_pad], [BF16, F32, F32], name="cast_first")))
    first = _gather_start([lands[k] for k in first_names], x, halved=(0,), name="gather_start_first")
    lands.update(zip(other_names, _cast_to_lands([w2[k] for k in other_names], [BF16] * len(other_names),
                                                 after=first[3], name="cast_other")))
    other = []
    cols = lambda g, rows: jnp.moveaxis(g, 0, 1).reshape(g.shape[1], -1)[:rows]

    def first_weights(after):
        got = dict(zip(first_names, _gather_wait(*first[:3], [after] + [lands[k] for k in other_names], halved=(0,),
                                                 name="gather_wait_first")))
        got["w_in"] = _forward_to_sibling(got["w_in"], name="forward_w_in")
        other.extend(_gather_start([lands[k] for k in other_names], got["w_in"], name="gather_start_other"))
        got["conv_w"] = cols(got["conv_w"], dims.conv_width)
        got["ffn_conv_w"] = cols(got["ffn_conv_w"], dims.ffn_conv_width)
        got["token"] = other[3]
        return got

    def other_weights(after):
        return dict(zip(other_names, _gather_wait(*other[:3], after, name="gather_wait_other")))

    started = {}

    def send_grad(name, g):
        send, recv, g_thru, land, token = _scatter_start(g.reshape(N_CHIPS, -1, g.shape[-1]), name=f"scatter_start_{name}")
        started[name] = (send, recv, g_thru, land)
        return token

    small = {k: w2[k] for k in SMALL}
    small["norm1_g"] = _after(small["norm1_g"].reshape(1, -1), first[3])
    loss, dx, grads = _local_step(dims, x.reshape(t, d), target.reshape(t, d), small, first_weights, other_weights, send_grad)

    def my_sums(names, after, tag):
        arrived = _scatter_wait([started[k] for k in names], after, name=f"scatter_wait_{tag}")
        blocks = [grads[k].reshape(N_CHIPS, -1, grads[k].shape[-1]) for k in names]
        return [_sum_received(g, land, name=f"sum_{k}") for k, g, (_, land) in zip(names, blocks, arrived)]

    def updates(names, mine, theirs):
        return {k: _adamw(w2[k], [a, b], m2[k], v2[k], name=f"adamw_{k}") for k, a, b in zip(names, mine, theirs)}

    small_names = SMALL + ("conv_w", "ffn_conv_w")
    packed = _pack_rows([grads[k] for k in small_names] + [loss[0, 0]], d)
    reducing = _allreduce_start(packed, name="allreduce_start")
    others = [k for k in LARGE if k != "w_in"]
    mine_others = my_sums(others, [dx, reducing[4]], "others")
    swapping_others = _swap_start(mine_others, name="swap_start_others")
    mine_w_in = my_sums(["w_in"], swapping_others[4], "w_in")
    swapping_w_in = _swap_start(mine_w_in, name="swap_start_w_in")
    out = updates(others, *_swap_wait(swapping_others, swapping_w_in[4], name="swap_wait_others"))
    last_updates = [out[k][1] for k in others]
    reduced = _sum_devices(*_allreduce_wait(reducing, last_updates, name="allreduce_wait"), name="allreduce_sum")
    shapes = [grads[k].shape for k in small_names] + [()]
    *small_g, loss_total = _unpack_rows(reduced, shapes, d)
    small_g = dict(zip(small_names, small_g))
    chip = 2 * lax.axis_index("x") + lax.axis_index("y")
    for k in ("conv_w", "ffn_conv_w"):
        width = w2[k].shape[1]
        small_g[k] = lax.dynamic_slice_in_dim(small_g[k], chip * width, width, axis=1)

    small_shapes = [w2[k].shape for k in small_names]
    pack = lambda grp: _pack_rows([grp[k] for k in small_names], d)
    results = _adamw(pack(w2), [pack(small_g)], pack(m2), pack(v2), name="adamw_small")
    unpacked = [_unpack_rows(r, small_shapes, d) for r in results]
    out.update({k: tuple(u[i] for u in unpacked) for i, k in enumerate(small_names)})
    out.update(updates(["w_in"], *_swap_wait(swapping_w_in, results[1], name="swap_wait_w_in")))

    lead =lambda a: a.reshape((1,) + a.shape)
    ordered = [[lead(out[k][j].reshape(w2[k].shape)) for k in WEIGHTS] for j in range(4)]
    return (loss_total, dx.reshape(x.shape), *ordered[0], *ordered[1], *ordered[2], *ordered[3])


def kernel(x, norm1_g, w_in, gate_b, conv_w, conv_b, conv_norm_g, w_conv_out, q_norm_g, k_norm_g, w_attn_out, w_out, norm2_g, w_up, ffn_conv_w, ffn_conv_b, w_down, loss_target, m_norm1_g, m_w_in, m_gate_b, m_conv_w, m_conv_b, m_conv_norm_g, m_w_conv_out, m_q_norm_g, m_k_norm_g, m_w_attn_out, m_w_out, m_norm2_g, m_w_up, m_ffn_conv_w, m_ffn_conv_b, m_w_down, v_norm1_g, v_w_in, v_gate_b, v_conv_w, v_conv_b, v_conv_norm_g, v_w_conv_out, v_q_norm_g, v_k_norm_g, v_w_attn_out, v_w_out, v_norm2_g, v_w_up, v_ffn_conv_w, v_ffn_conv_b, v_w_down):
    w = dict(zip(WEIGHTS, (norm1_g, w_in, gate_b, conv_w, conv_b, conv_norm_g, w_conv_out, q_norm_g, k_norm_g,
                           w_attn_out, w_out, norm2_g, w_up, ffn_conv_w, ffn_conv_b, w_down)))
    m = dict(zip(WEIGHTS, (m_norm1_g, m_w_in, m_gate_b, m_conv_w, m_conv_b, m_conv_norm_g, m_w_conv_out, m_q_norm_g,
                           m_k_norm_g, m_w_attn_out, m_w_out, m_norm2_g, m_w_up, m_ffn_conv_w, m_ffn_conv_b, m_w_down)))
    v = dict(zip(WEIGHTS, (v_norm1_g, v_w_in, v_gate_b, v_conv_w, v_conv_b, v_conv_norm_g, v_w_conv_out, v_q_norm_g,
                           v_k_norm_g, v_w_attn_out, v_w_out, v_norm2_g, v_w_up, v_ffn_conv_w, v_ffn_conv_b, v_w_down)))
    dims = Dims(d_model=x.shape[-1], batch_local=x.shape[0], seq=x.shape[1], d_ff=w_down.shape[1] * N_CHIPS)
    return _step(dims, x, loss_target, w, m, v)
```

```python
import functools
import math
from typing import NamedTuple

import jax
import jax.numpy as jnp
from jax import lax
from jax.experimental import pallas as pl
from jax.experimental.pallas import tpu as pltpu

F32 = jnp.float32
BF16 = jnp.bfloat16

RMS_EPS = 1e-6
MASKED_SCORE = -1e30
ATTN_BLOCK = 128
DILATIONS = (1, 4, 16)
CONV_HALO = 32
FFN_HALO = 8
ADAM_LR, ADAM_B1, ADAM_B2, ADAM_EPS, ADAM_WD, ADAM_STEP = 0.001, 0.9, 0.999, 1e-08, 0.01, 10
V7X_VMEM_LIMIT_BYTES = 56 * 2 ** 20
N_CHIPS = 4
MESH = pl.DeviceIdType.MESH


class Dims(NamedTuple):
    d_model: int = 1024
    n_heads: int = 16
    head_dim: int = 64
    d_ff: int = 2816
    seq: int = 2048
    batch_local: int = 2
    conv_width: int = 31
    ffn_conv_width: int = 3

    @property
    def tokens(self):
        return self.seq * self.batch_local


def _params(*semantics):
    return pltpu.CompilerParams(dimension_semantics=semantics, vmem_limit_bytes=V7X_VMEM_LIMIT_BYTES)


ANY = pl.BlockSpec(memory_space=pl.ANY)


def _ordered(body, n_inputs, after):
    after = [] if after is None else list(after) if isinstance(after, (list, tuple)) else [after]
    if not after:
        return body, [], []

    def wrapped(*refs):
        return body(*refs[:n_inputs], *refs[n_inputs + len(after):])

    return wrapped, [ANY] * len(after), after


def _pick(n, target, mult=128):
    if n <= target:
        return n
    best = None
    for t in range(mult, target + 1, mult):
        if n % t == 0:
            best = t
    assert best is not None, (n, target, mult)
    return best


def _sigmoid(v):
    return 1.0 / (1.0 + jnp.exp(-v))


def _mm_nn(a, w, *, out_dtype, name, residual=None, after=None, tm=1024, tn=1408, tk=2816):
    m, k = a.shape
    nsh, k2, c = w.shape
    assert k == k2 and a.dtype == BF16 and w.dtype == BF16
    n = nsh * c
    tm, tn, tk = _pick(m, tm, 8), _pick(c, tn), _pick(k, tk)
    nk, cpn = k // tk, c // tn

    def body(*refs):
        if residual is None:
            a_ref, w_ref, o_ref, acc = refs
        else:
            a_ref, w_ref, r_ref, o_ref, acc = refs
        prod = jnp.dot(a_ref[...], w_ref[...], preferred_element_type=F32)

        def finish(total):
            if residual is not None:
                total = total + r_ref[...]
            o_ref[...] = total.astype(out_dtype)

        if nk == 1:
            finish(prod)
        else:
            kk = pl.program_id(2)

            @pl.when(kk == 0)
            def _():
                acc[...] = prod

            @pl.when(kk > 0)
            def _():
                acc[...] += prod

            @pl.when(kk == nk - 1)
            def _():
                finish(acc[...])

    in_specs = [pl.BlockSpec((tm, tk), lambda i, j, kk: (i, kk)),
                pl.BlockSpec((None, tk, tn), lambda i, j, kk: (j // cpn, kk, j % cpn))]
    args = [a, w]
    if residual is not None:
        in_specs.append(pl.BlockSpec((tm, tn), lambda i, j, kk: (i, j)))
        args.append(residual)
    body, more_specs, more_args = _ordered(body, len(args), after)
    return pl.pallas_call(
        body, name=name, grid=(m // tm, n // tn, nk),
        in_specs=in_specs + more_specs, out_specs=pl.BlockSpec((tm, tn), lambda i, j, kk: (i, j)),
        out_shape=jax.ShapeDtypeStruct((m, n), out_dtype),
        scratch_shapes=[pltpu.VMEM((tm, tn) if nk > 1 else (8, 128), F32)],
        compiler_params=_params("parallel", "parallel", "arbitrary"),
    )(*args, *more_args)


def _proj_residual_norm(a, w, residual, g, *, name, tm=1024):
    m, k = a.shape
    _, k2, n = w.shape
    assert w.shape[0] == 1 and k == k2 and a.dtype == BF16 and w.dtype == BF16
    tm = _pick(m, tm, 8)

    def body(a_ref, w_ref, r_ref, g_ref, y_ref, h_ref):
        y = r_ref[...] + jnp.dot(a_ref[...], w_ref[...], preferred_element_type=F32)
        y_ref[...] = y
        h_ref[...] = (y * lax.rsqrt(jnp.mean(y * y, axis=-1, keepdims=True) + RMS_EPS) * g_ref[...]).astype(BF16)

    rows = lambda width: pl.BlockSpec((tm, width), lambda i: (i, 0))
    return pl.pallas_call(
        body, name=name, grid=(m // tm,),
        in_specs=[rows(k), pl.BlockSpec((None, k, n), lambda i: (0, 0, 0)), rows(n), pl.BlockSpec((1, n), lambda i: (0, 0))],
        out_specs=[rows(n), rows(n)],
        out_shape=[jax.ShapeDtypeStruct((m, n), F32), jax.ShapeDtypeStruct((m, n), BF16)],
        compiler_params=_params("parallel"),
    )(a, w, residual, g)


def _proj_residual_loss(a, w, residual, target, *, name, tm=1024):
    m, k = a.shape
    _, k2, n = w.shape
    assert w.shape[0] == 1 and k == k2 and a.dtype == BF16 and w.dtype == BF16
    tm = _pick(m, tm, 8)

    def body(a_ref, w_ref, r_ref, t_ref, dy_ref, dyb_ref, loss_ref):
        err = r_ref[...] + jnp.dot(a_ref[...], w_ref[...], preferred_element_type=F32) - t_ref[...]
        dy = err * (1.0 / n)
        dy_ref[...] = dy
        dyb_ref[...] = dy.astype(BF16)
        part = jnp.sum(jnp.sum(err * err, axis=-1, keepdims=True), axis=0, keepdims=True) * (0.5 / n)
        _accumulate(loss_ref, jnp.broadcast_to(part, (8, 128)), pl.program_id(0) == 0)

    rows = lambda width: pl.BlockSpec((tm, width), lambda i: (i, 0))
    return pl.pallas_call(
        body, name=name, grid=(m // tm,),
        in_specs=[rows(k), pl.BlockSpec((None, k, n), lambda i: (0, 0, 0)), rows(n), rows(n)],
        out_specs=[rows(n), rows(n), pl.BlockSpec((8, 128), lambda i: (0, 0))],
        out_shape=[jax.ShapeDtypeStruct((m, n), F32), jax.ShapeDtypeStruct((m, n), BF16),
                   jax.ShapeDtypeStruct((8, 128), F32)],
        compiler_params=_params("arbitrary"),
    )(a, w, residual, target)


def _mm_nt(a, w, *, out_dtype, name, after=None, tm=1024, tn=1408, tk=1792):
    m, k = a.shape
    nsh, r, c = w.shape
    assert k == nsh * c and a.dtype == BF16 and w.dtype == BF16
    tm, tn, tk = _pick(m, tm, 8), _pick(r, tn), _pick(c, tk)
    nk, cpk = k // tk, c // tk

    def body(a_ref, w_ref, o_ref, acc):
        prod = lax.dot_general(a_ref[...], w_ref[...], (((1,), (1,)), ((), ())), preferred_element_type=F32)
        if nk == 1:
            o_ref[...] = prod.astype(out_dtype)
        else:
            kk = pl.program_id(2)

            @pl.when(kk == 0)
            def _():
                acc[...] = prod

            @pl.when(kk > 0)
            def _():
                acc[...] += prod

            @pl.when(kk == nk - 1)
            def _():
                o_ref[...] = acc[...].astype(out_dtype)

    body, more_specs, more_args = _ordered(body, 2, after)
    return pl.pallas_call(
        body, name=name, grid=(m // tm, r // tn, nk),
        in_specs=[pl.BlockSpec((tm, tk), lambda i, j, kk: (i, kk)),
                  pl.BlockSpec((None, tn, tk), lambda i, j, kk: (kk // cpk, j, kk % cpk))] + more_specs,
        out_specs=pl.BlockSpec((tm, tn), lambda i, j, kk: (i, j)),
        out_shape=jax.ShapeDtypeStruct((m, r), out_dtype),
        scratch_shapes=[pltpu.VMEM((tm, tn) if nk > 1 else (8, 128), F32)],
        compiler_params=_params("parallel", "parallel", "arbitrary"),
    )(a, w, *more_args)


MM_TN_VMEM_BYTES = 44 * 2 ** 20


def _mm_tn(a, b, *, n_shards, name, tm=1408, tn=1408):
    t, m = a.shape
    t2, n = b.shape
    assert t == t2 and a.dtype == BF16 and b.dtype == BF16
    c = n // n_shards
    tm, tn = _pick(m, tm), _pick(c, tn)
    if m // tm == 1 and n // tn == 1 and tn % (2 * LANES) == 0:
        tn //= 2
    fixed = 2 * tm * tn * 6
    if 4 * t * (tm + tn) + fixed <= MM_TN_VMEM_BYTES:
        tk = t
    else:
        tk = _pick(t, (MM_TN_VMEM_BYTES - fixed - 4 * tm * tn) // (4 * (tm + tn)), 8)
    nk, cpn = t // tk, c // tn

    def body(a_ref, b_ref, o_ref, ob_ref, acc):
        kk = pl.program_id(2)
        prod = lax.dot_general(a_ref[...], b_ref[...], (((0,), (0,)), ((), ())), preferred_element_type=F32)

        def finish(total):
            o_ref[...] = total
            ob_ref[...] = total.astype(BF16)

        if nk == 1:
            finish(prod)
        else:
            @pl.when(kk == 0)
            def _():
                acc[...] = prod

            @pl.when(kk > 0)
            def _():
                acc[...] += prod

            @pl.when(kk == nk - 1)
            def _():
                finish(acc[...])

    out_spec = pl.BlockSpec((None, tm, tn), lambda i, j, kk: (j // cpn, i, j % cpn))
    return pl.pallas_call(
        body, name=name, grid=(m // tm, n // tn, nk),
        in_specs=[pl.BlockSpec((tk, tm), lambda i, j, kk: (kk, i)),
                  pl.BlockSpec((tk, tn), lambda i, j, kk: (kk, j))],
        out_specs=[out_spec, out_spec],
        out_shape=[jax.ShapeDtypeStruct((n_shards, m, c), F32), jax.ShapeDtypeStruct((n_shards, m, c), BF16)],
        scratch_shapes=[pltpu.VMEM((tm, tn) if nk > 1 else (8, 128), F32)],
        compiler_params=_params("parallel", "parallel", "arbitrary"),
    )(a, b)


def _row_spec(tr, width, col=0):
    return pl.BlockSpec((tr, width), lambda i, col=col: (i, col))


def _vec_spec(width, col=0):
    return pl.BlockSpec((1, width), lambda i, col=col: (0, col))


def _accumulate(ref, value, first):
    @pl.when(first)
    def _():
        ref[...] = value

    @pl.when(jnp.logical_not(first))
    def _():
        ref[...] += value


def _rmsnorm_fwd(x, g, *, name, tr=512):
    t, d = x.shape
    tr = _pick(t, tr, 8)

    def body(x_ref, g_ref, o_ref):
        xv = x_ref[...]
        r = lax.rsqrt(jnp.mean(xv * xv, axis=-1, keepdims=True) + RMS_EPS)
        o_ref[...] = (xv * r * g_ref[...]).astype(BF16)

    return pl.pallas_call(
        body, name=name, grid=(t // tr,),
        in_specs=[_row_spec(tr, d), _vec_spec(d)], out_specs=_row_spec(tr, d),
        out_shape=jax.ShapeDtypeStruct((t, d), BF16), compiler_params=_params("parallel"),
    )(x, g)


def _rmsnorm_bwd(x, g, dy, dres, *, name, want_bf16, tr=512):
    t, d = x.shape
    tr = _pick(t, tr, 8)

    def body(x_ref, g_ref, dy_ref, dres_ref, *outs):
        dx_ref, dg_ref = outs[0], outs[-1]
        xv, dyv = x_ref[...], dy_ref[...].astype(F32)
        r = lax.rsqrt(jnp.mean(xv * xv, axis=-1, keepdims=True) + RMS_EPS)
        gy = dyv * g_ref[...]
        dx = dres_ref[...] + r * gy - xv * (r * r * r) * jnp.mean(xv * gy, axis=-1, keepdims=True)
        dx_ref[...] = dx
        if want_bf16:
            outs[1][...] = dx.astype(BF16)
        _accumulate(dg_ref, jnp.sum(dyv * xv * r, axis=0, keepdims=True), pl.program_id(0) == 0)

    out_shape = [jax.ShapeDtypeStruct((t, d), F32)]
    out_specs = [_row_spec(tr, d)]
    if want_bf16:
        out_shape.append(jax.ShapeDtypeStruct((t, d), BF16))
        out_specs.append(_row_spec(tr, d))
    out_shape.append(jax.ShapeDtypeStruct((1, d), F32))
    out_specs.append(_vec_spec(d))
    return pl.pallas_call(
        body, name=name, grid=(t // tr,),
        in_specs=[_row_spec(tr, d), _vec_spec(d), _row_spec(tr, d), _row_spec(tr, d)],
        out_specs=out_specs, out_shape=out_shape, compiler_params=_params("arbitrary"),
    )(x, g, dy, dres)


def _head_mean(v, ones_ref, head_dim):
    hi = v.astype(BF16)
    lo = (v - hi.astype(F32)).astype(BF16)
    e = ones_ref[...]
    total = jnp.dot(hi, e, preferred_element_type=F32) + jnp.dot(lo, e, preferred_element_type=F32)
    return total * (1.0 / head_dim)


def _qkv_fwd(z, gq, gk, head_ones, dims, *, name, tr=256):
    t = z.shape[0]
    a = dims.n_heads * dims.head_dim
    tr = _pick(t, tr, 8)
    q_scale = dims.head_dim ** -0.5

    def body(q_ref, k_ref, v_ref, gq_ref, gk_ref, e_ref, qo_ref, ko_ref, vo_ref):
        qv, kv = q_ref[...], k_ref[...]
        rq = lax.rsqrt(_head_mean(qv * qv, e_ref, dims.head_dim) + RMS_EPS)
        rk = lax.rsqrt(_head_mean(kv * kv, e_ref, dims.head_dim) + RMS_EPS)
        qo_ref[...] = (qv * rq * gq_ref[...] * q_scale).astype(BF16)
        ko_ref[...] = (kv * rk * gk_ref[...]).astype(BF16)
        vo_ref[...] = v_ref[...].astype(BF16)

    return pl.pallas_call(
        body, name=name, grid=(t // tr,),
        in_specs=[_row_spec(tr, a, 2), _row_spec(tr, a, 3), _row_spec(tr, a, 4), _vec_spec(a), _vec_spec(a),
                  pl.BlockSpec((a, a), lambda i: (0, 0))],
        out_specs=[_row_spec(tr, a)] * 3, out_shape=[jax.ShapeDtypeStruct((t, a), BF16)] * 3,
        compiler_params=_params("parallel"),
    )(z, z, z, gq, gk, head_ones)


def _qkv_bwd(z, dqs, dks, dvs, gq, gk, head_ones, dims, *, name, tr=256):
    t = z.shape[0]
    a = dims.n_heads * dims.head_dim
    tr = _pick(t, tr, 8)
    q_scale = dims.head_dim ** -0.5
    ng = len(dqs)

    def body(*refs):
        q_ref, k_ref = refs[:2]
        dq_refs, dk_refs, dv_refs = refs[2:2 + ng], refs[2 + ng:2 + 2 * ng], refs[2 + 2 * ng:2 + 3 * ng]
        gq_ref, gk_ref, e_ref = refs[2 + 3 * ng:5 + 3 * ng]
        dz_ref, dgq_ref, dgk_ref = refs[5 + 3 * ng:]
        first = pl.program_id(0) == 0

        def norm_bwd(x_ref, d_refs, g_ref, scale, col, dg_ref):
            xv = x_ref[...]
            dy = sum(r[...] for r in d_refs) * scale
            r = lax.rsqrt(_head_mean(xv * xv, e_ref, dims.head_dim) + RMS_EPS)
            gy = dy * g_ref[...]
            dx = r * gy - xv * (r * r * r) * _head_mean(xv * gy, e_ref, dims.head_dim)
            dz_ref[:, col * a:(col + 1) * a] = dx.astype(BF16)
            _accumulate(dg_ref, jnp.sum(dy * xv * r, axis=0, keepdims=True), first)

        norm_bwd(q_ref, dq_refs, gq_ref, q_scale, 0, dgq_ref)
        norm_bwd(k_ref, dk_refs, gk_ref, 1.0, 1, dgk_ref)
        dz_ref[:, 2 * a:3 * a] = sum(r[...] for r in dv_refs).astype(BF16)

    in_specs = ([_row_spec(tr, a, 2), _row_spec(tr, a, 3)] + [_row_spec(tr, a)] * (3 * ng)
                + [_vec_spec(a), _vec_spec(a), pl.BlockSpec((a, a), lambda i: (0, 0))])
    return pl.pallas_call(
        body, name=name, grid=(t // tr,), in_specs=in_specs,
        out_specs=[_row_spec(tr, 3 * a), _vec_spec(a), _vec_spec(a)],
        out_shape=[jax.ShapeDtypeStruct((t, 3 * a), BF16)] + [jax.ShapeDtypeStruct((1, a), F32)] * 2,
        compiler_params=_params("arbitrary"),
    )(z, z, *dqs, *dks, *dvs, gq, gk, head_ones)


CONV_ROWS = 16


def _seq_specs(dims, ts, width, halo, col, *, nxt=False):
    nst, per = dims.seq // ts, ts // halo
    last = dims.tokens // halo - 1
    cur = pl.BlockSpec((ts, width), lambda b, i: (b * nst + i, col))
    if nxt:
        edge = pl.BlockSpec((halo, width), lambda b, i: (jnp.minimum((b * nst + i + 1) * per, last), col))
    else:
        edge = pl.BlockSpec((halo, width), lambda b, i: (jnp.maximum((b * nst + i) * per - 1, 0), col))
    return cur, edge


SUBLANES = 8


def _shifted_copies(buf, shifted):
    rows = shifted.shape[1]
    for s in range(1, SUBLANES):
        shifted[s - 1] = buf[pl.ds(s, rows), :]


def _window(buf, shifted, start, size):
    a, s = divmod(start, SUBLANES)
    src = buf if s == 0 else shifted.at[s - 1]
    return src[pl.ds(SUBLANES * a, size), :]


def _conv_branch_fwd(z, w, b, g, dims, *, name, ts=128):
    t, c, kw = z.shape[0], dims.d_model, dims.conv_width
    base = CONV_HALO - (kw - 1)

    def body(av_ref, hv_ref, ag_ref, hg_ref, w_ref, b_ref, g_ref, a1_ref, a3_ref, buf, shifted):
        i = pl.program_id(1)
        buf[CONV_HALO:, :] = av_ref[...].astype(F32) * _sigmoid(ag_ref[...].astype(F32))
        buf[0:CONV_HALO, :] = jnp.where(i > 0, hv_ref[...].astype(F32) * _sigmoid(hg_ref[...].astype(F32)), 0.0)
        _shifted_copies(buf, shifted)
        for r0 in range(0, ts, CONV_ROWS):
            acc = jnp.broadcast_to(b_ref[...], (CONV_ROWS, c))
            for k in range(kw):
                acc = acc + w_ref[k:k + 1, :] * _window(buf, shifted, r0 + base + k, CONV_ROWS)
            a1_ref[r0:r0 + CONV_ROWS, :] = acc
            a2 = acc * lax.rsqrt(jnp.mean(acc * acc, axis=-1, keepdims=True) + RMS_EPS) * g_ref[...]
            a3_ref[r0:r0 + CONV_ROWS, :] = (a2 * _sigmoid(a2)).astype(BF16)

    vec = pl.BlockSpec((1, c), lambda b, i: (0, 0))
    out = pl.BlockSpec((ts, c), lambda b, i: (b * (dims.seq // ts) + i, 0))
    return pl.pallas_call(
        body, name=name, grid=(dims.batch_local, dims.seq // ts),
        in_specs=[*_seq_specs(dims, ts, c, CONV_HALO, 0), *_seq_specs(dims, ts, c, CONV_HALO, 1),
                  pl.BlockSpec((CONV_HALO, c), lambda b, i: (0, 0)), vec, vec],
        out_specs=[out, out],
        out_shape=[jax.ShapeDtypeStruct((t, c), F32), jax.ShapeDtypeStruct((t, c), BF16)],
        scratch_shapes=[pltpu.VMEM((CONV_HALO + ts, c), F32),
                        pltpu.VMEM((SUBLANES - 1, CONV_HALO + ts - SUBLANES, c), F32)],
        compiler_params=_params("parallel", "parallel"),
    )(z, z, z, z, w, b, g)


def _conv_norm_bwd(da3, a1, g, *, name, tr=256):
    t, c = a1.shape
    tr = _pick(t, tr, 8)

    def body(d_ref, a_ref, g_ref, o_ref, dg_ref):
        a1v, gv = a_ref[...], g_ref[...]
        r = lax.rsqrt(jnp.mean(a1v * a1v, axis=-1, keepdims=True) + RMS_EPS)
        a2 = a1v * r * gv
        sg = _sigmoid(a2)
        da2 = d_ref[...].astype(F32) * sg * (1.0 + a2 * (1.0 - sg))
        gy = da2 * gv
        o_ref[...] = r * gy - a1v * (r * r * r) * jnp.mean(a1v * gy, axis=-1, keepdims=True)
        _accumulate(dg_ref, jnp.sum(da2 * a1v * r, axis=0, keepdims=True), pl.program_id(0) == 0)

    return pl.pallas_call(
        body, name=name, grid=(t // tr,),
        in_specs=[_row_spec(tr, c), _row_spec(tr, c), _vec_spec(c)],
        out_specs=[_row_spec(tr, c), _vec_spec(c)],
        out_shape=[jax.ShapeDtypeStruct((t, c), F32), jax.ShapeDtypeStruct((1, c), F32)],
        compiler_params=_params("arbitrary"),
    )(da3, a1, g)


def _conv_branch_bwd(da1, z, w, rest_of_dz, dims, *, name, ts=128):
    t, c, kw = z.shape[0], dims.d_model, dims.conv_width
    nst = dims.seq // ts
    base = CONV_HALO - (kw - 1)
    n_rest = len(rest_of_dz)
    total = 2 * c + sum(r.shape[1] for r in rest_of_dz)

    def body(d_ref, dn_ref, av_ref, hv_ref, ag_ref, hg_ref, w_ref, *more):
        rest_refs = more[:n_rest]
        dz_ref, dw_ref, db_ref, abuf, dbuf, ashift, dshift = more[n_rest:]
        col = 2 * c
        for r in rest_refs:
            dz_ref[:, col:col + r.shape[1]] = r[...]
            col += r.shape[1]
        i = pl.program_id(1)
        first = jnp.logical_and(pl.program_id(0) == 0, i == 0)
        abuf[CONV_HALO:, :] = av_ref[...].astype(F32) * _sigmoid(ag_ref[...].astype(F32))
        abuf[0:CONV_HALO, :] = jnp.where(i > 0, hv_ref[...].astype(F32) * _sigmoid(hg_ref[...].astype(F32)), 0.0)
        d1 = d_ref[...]
        dbuf[0:ts, :] = d1
        dbuf[ts:, :] = jnp.where(i < nst - 1, dn_ref[...], 0.0)
        _shifted_copies(abuf, ashift)
        _shifted_copies(dbuf, dshift)

        @pl.when(first)
        def _():
            dw_ref[...] = jnp.zeros_like(dw_ref)
            db_ref[...] = jnp.zeros_like(db_ref)

        db_ref[...] += jnp.sum(d1, axis=0, keepdims=True)
        for k in range(kw):
            dw_ref[k:k + 1, :] += jnp.sum(d1 * _window(abuf, ashift, base + k, ts), axis=0, keepdims=True)
        for r0 in range(0, ts, CONV_ROWS):
            acc = jnp.zeros((CONV_ROWS, c), F32)
            for k in range(kw):
                acc = acc + w_ref[k:k + 1, :] * _window(dbuf, dshift, r0 + (kw - 1) - k, CONV_ROWS)
            av = av_ref[r0:r0 + CONV_ROWS, :].astype(F32)
            sg = _sigmoid(ag_ref[r0:r0 + CONV_ROWS, :].astype(F32))
            dz_ref[r0:r0 + CONV_ROWS, 0:c] = (acc * sg).astype(BF16)
            dz_ref[r0:r0 + CONV_ROWS, c:2 * c] = (acc * av * sg * (1.0 - sg)).astype(BF16)

    cur, nxt = _seq_specs(dims, ts, c, CONV_HALO, 0, nxt=True)
    return pl.pallas_call(
        body, name=name, grid=(dims.batch_local, nst),
        in_specs=[cur, nxt, *_seq_specs(dims, ts, c, CONV_HALO, 0), *_seq_specs(dims, ts, c, CONV_HALO, 1),
                  pl.BlockSpec((CONV_HALO, c), lambda b, i: (0, 0))]
        + [pl.BlockSpec((ts, r.shape[1]), lambda b, i: (b * nst + i, 0)) for r in rest_of_dz],
        out_specs=[pl.BlockSpec((ts, total), lambda b, i: (b * nst + i, 0)),
                   pl.BlockSpec((CONV_HALO, c), lambda b, i: (0, 0)), pl.BlockSpec((1, c), lambda b, i: (0, 0))],
        out_shape=[jax.ShapeDtypeStruct((t, total), BF16), jax.ShapeDtypeStruct((CONV_HALO, c), F32),
                   jax.ShapeDtypeStruct((1, c), F32)],
        scratch_shapes=[pltpu.VMEM((CONV_HALO + ts, c), F32)] * 2
        + [pltpu.VMEM((SUBLANES - 1, CONV_HALO + ts - SUBLANES, c), F32)] * 2,
        compiler_params=_params("arbitrary", "arbitrary"),
    )(da1, da1, z, z, z, z, w, *rest_of_dz)


FFN_ROWS = 16
FFN_COLS = 256


def _ffn_chunks(ts, f):
    cw = _pick(f, FFN_COLS)
    return [(r0, c0, cw) for r0 in range(0, ts, FFN_ROWS) for c0 in range(0, f, cw)]


def _tap_sources(buf, moved, offsets, rows):
    taps, used = [], 0
    for off in offsets:
        if off % SUBLANES:
            moved[used] = buf[pl.ds(off, rows), :]
            taps.append((moved.at[used], 0))
            used += 1
        else:
            taps.append((buf, off))
    return taps


def _moved_copies(offsets):
    return sum(1 for off in offsets if off % SUBLANES)


def _taps_sum(taps, w_ref, init, r0, cols):
    for k, (src, off) in enumerate(taps):
        init = init + w_ref[k:k + 1, cols] * src[pl.ds(off + r0, init.shape[0]), cols]
    return init


def _ffn_bwd(dact, up, w, b, dims, *, name, ts=128):
    t, f, kw = up.shape[0], dims.d_ff, dims.ffn_conv_width
    nst = dims.seq // ts
    fwd_offsets = [FFN_HALO - (kw - 1) + k for k in range(kw)]
    bwd_offsets = [(kw - 1) - k for k in range(kw)]
    dact_halo = 2 * FFN_HALO

    def body(d_ref, dn_ref, up_ref, hp_ref, hn_ref, w_ref, b_ref, o_ref, dw_ref, db_ref, buf, moved, dbuf, dmoved):
        i = pl.program_id(1)
        first = jnp.logical_and(pl.program_id(0) == 0, i == 0)
        more = i < nst - 1
        buf[0:FFN_HALO, :] = jnp.where(i > 0, hp_ref[...], 0.0)
        buf[FFN_HALO:FFN_HALO + ts, :] = up_ref[...]
        buf[FFN_HALO + ts:, :] = hn_ref[...]
        taps = _tap_sources(buf, moved, fwd_offsets, ts + FFN_HALO)

        def du_chunk(r0, rows, c0, cw, d):
            vcols, gcols = slice(c0, c0 + cw), slice(f + c0, f + c0 + cw)
            uv = _taps_sum(taps, w_ref, jnp.broadcast_to(b_ref[:, vcols], (rows, cw)), r0, vcols)
            ug = _taps_sum(taps, w_ref, jnp.broadcast_to(b_ref[:, gcols], (rows, cw)), r0, gcols)
            sg = _sigmoid(ug)
            dbuf[r0:r0 + rows, vcols] = d * ug * sg
            dbuf[r0:r0 + rows, gcols] = d * uv * sg * (1.0 + ug * (1.0 - sg))

        for r0, c0, cw in _ffn_chunks(ts, f):
            du_chunk(r0, FFN_ROWS, c0, cw, d_ref[r0:r0 + FFN_ROWS, c0:c0 + cw].astype(F32))
        for _, c0, cw in _ffn_chunks(FFN_ROWS, f):
            d_next = dn_ref[:, c0:c0 + cw].astype(F32)[0:FFN_HALO]
            du_chunk(ts, FFN_HALO, c0, cw, jnp.where(more, d_next, 0.0))

        @pl.when(first)
        def _():
            dw_ref[...] = jnp.zeros_like(dw_ref)
            db_ref[...] = jnp.zeros_like(db_ref)

        du = dbuf[0:ts, :]
        db_ref[...] += jnp.sum(du, axis=0, keepdims=True)
        for k, (src, off) in enumerate(taps):
            dw_ref[k:k + 1, :] += jnp.sum(du * src[pl.ds(off, ts), :], axis=0, keepdims=True)

        dtaps = _tap_sources(dbuf, dmoved, bwd_offsets, ts)
        for r0, c0, cw in _ffn_chunks(ts, 2 * f):
            cols = slice(c0, c0 + cw)
            o_ref[r0:r0 + FFN_ROWS, cols] = _taps_sum(dtaps, w_ref, jnp.zeros((FFN_ROWS, cw), F32), r0, cols).astype(BF16)

    up_cur, up_prev = _seq_specs(dims, ts, 2 * f, FFN_HALO, 0)
    _, up_next = _seq_specs(dims, ts, 2 * f, FFN_HALO, 0, nxt=True)
    d_cur, d_next = _seq_specs(dims, ts, f, dact_halo, 0, nxt=True)
    full = lambda rows: pl.BlockSpec((rows, 2 * f), lambda b_, i: (0, 0))
    return pl.pallas_call(
        body, name=name, grid=(dims.batch_local, nst),
        in_specs=[d_cur, d_next, up_cur, up_prev, up_next, full(FFN_HALO), full(1)],
        out_specs=[pl.BlockSpec((ts, 2 * f), lambda b_, i: (b_ * nst + i, 0)), full(FFN_HALO), full(1)],
        out_shape=[jax.ShapeDtypeStruct((t, 2 * f), BF16), jax.ShapeDtypeStruct((FFN_HALO, 2 * f), F32),
                   jax.ShapeDtypeStruct((1, 2 * f), F32)],
        scratch_shapes=[pltpu.VMEM((ts + 2 * FFN_HALO, 2 * f), F32),
                        pltpu.VMEM((_moved_copies(fwd_offsets), ts + FFN_HALO, 2 * f), F32),
                        pltpu.VMEM((ts + FFN_HALO, 2 * f), F32),
                        pltpu.VMEM((_moved_copies(bwd_offsets), ts, 2 * f), F32)],
        compiler_params=_params("arbitrary", "arbitrary"),
    )(dact, dact, up, up, up, w, b)


def _ffn_act_fwd(up, w, b, dims, *, name, ts=128):
    t, f, kw = up.shape[0], dims.d_ff, dims.ffn_conv_width
    offsets = [FFN_HALO - (kw - 1) + k for k in range(kw)]

    def body(up_ref, h_ref, w_ref, b_ref, o_ref, buf, moved):
        buf[FFN_HALO:, :] = up_ref[...]
        buf[0:FFN_HALO, :] = jnp.where(pl.program_id(1) > 0, h_ref[...], 0.0)
        taps = _tap_sources(buf, moved, offsets, ts)
        for r0, c0, cw in _ffn_chunks(ts, f):
            vcols, gcols = slice(c0, c0 + cw), slice(f + c0, f + c0 + cw)
            uv = _taps_sum(taps, w_ref, jnp.broadcast_to(b_ref[:, vcols], (FFN_ROWS, cw)), r0, vcols)
            ug = _taps_sum(taps, w_ref, jnp.broadcast_to(b_ref[:, gcols], (FFN_ROWS, cw)), r0, gcols)
            o_ref[r0:r0 + FFN_ROWS, vcols] = (ug * _sigmoid(ug) * uv).astype(BF16)

    full = lambda rows: pl.BlockSpec((rows, 2 * f), lambda b_, i: (0, 0))
    return pl.pallas_call(
        body, name=name, grid=(dims.batch_local, dims.seq // ts),
        in_specs=[*_seq_specs(dims, ts, 2 * f, FFN_HALO, 0), full(FFN_HALO), full(1)],
        out_specs=pl.BlockSpec((ts, f), lambda b_, i: (b_ * (dims.seq // ts) + i, 0)),
        out_shape=jax.ShapeDtypeStruct((t, f), BF16),
        scratch_shapes=[pltpu.VMEM((FFN_HALO + ts, 2 * f), F32), pltpu.VMEM((_moved_copies(offsets), ts, 2 * f), F32)],
        compiler_params=_params("parallel", "parallel"),
    )(up, up, w, b)


def _ffn_act_bwd(dact, up, w, b, dims, *, name, ts=128):
    t, f, kw = up.shape[0], dims.d_ff, dims.ffn_conv_width
    offsets = [FFN_HALO - (kw - 1) + k for k in range(kw)]

    def body(d_ref, up_ref, h_ref, w_ref, b_ref, du_ref, dw_ref, db_ref, buf, moved):
        i = pl.program_id(1)
        first = jnp.logical_and(pl.program_id(0) == 0, i == 0)
        buf[FFN_HALO:, :] = up_ref[...]
        buf[0:FFN_HALO, :] = jnp.where(i > 0, h_ref[...], 0.0)
        taps = _tap_sources(buf, moved, offsets, ts)
        for r0, c0, cw in _ffn_chunks(ts, f):
            vcols, gcols = slice(c0, c0 + cw), slice(f + c0, f + c0 + cw)
            uv = _taps_sum(taps, w_ref, jnp.broadcast_to(b_ref[:, vcols], (FFN_ROWS, cw)), r0, vcols)
            ug = _taps_sum(taps, w_ref, jnp.broadcast_to(b_ref[:, gcols], (FFN_ROWS, cw)), r0, gcols)
            d = d_ref[r0:r0 + FFN_ROWS, vcols].astype(F32)
            sg = _sigmoid(ug)
            du_ref[r0:r0 + FFN_ROWS, vcols] = d * ug * sg
            du_ref[r0:r0 + FFN_ROWS, gcols] = d * uv * sg * (1.0 + ug * (1.0 - sg))

        @pl.when(first)
        def _():
            dw_ref[...] = jnp.zeros_like(dw_ref)
            db_ref[...] = jnp.zeros_like(db_ref)

        du = du_ref[...]
        db_ref[...] += jnp.sum(du, axis=0, keepdims=True)
        for k, (src, off) in enumerate(taps):
            dw_ref[k:k + 1, :] += jnp.sum(du * src[pl.ds(off, ts), :], axis=0, keepdims=True)

    nst = dims.seq // ts
    n_moved = _moved_copies(offsets)
    full = lambda rows: pl.BlockSpec((rows, 2 * f), lambda b_, i: (0, 0))
    return pl.pallas_call(
        body, name=name, grid=(dims.batch_local, nst),
        in_specs=[pl.BlockSpec((ts, f), lambda b_, i: (b_ * nst + i, 0)),
                  *_seq_specs(dims, ts, 2 * f, FFN_HALO, 0), full(FFN_HALO), full(1)],
        out_specs=[pl.BlockSpec((ts, 2 * f), lambda b_, i: (b_ * nst + i, 0)), full(FFN_HALO), full(1)],
        out_shape=[jax.ShapeDtypeStruct((t, 2 * f), F32), jax.ShapeDtypeStruct((FFN_HALO, 2 * f), F32),
                   jax.ShapeDtypeStruct((1, 2 * f), F32)],
        scratch_shapes=[pltpu.VMEM((FFN_HALO + ts, 2 * f), F32), pltpu.VMEM((n_moved, ts, 2 * f), F32)],
        compiler_params=_params("arbitrary", "arbitrary"),
    )(dact, up, up, w, b)


def _ffn_conv_bwd(du, w, dims, *, name, ts=128):
    t, f2 = du.shape
    kw = dims.ffn_conv_width
    nst = dims.seq // ts

    offsets = [(kw - 1) - k for k in range(kw)]

    def body(d_ref, dn_ref, w_ref, o_ref, buf, moved):
        buf[0:ts, :] = d_ref[...]
        buf[ts:, :] = jnp.where(pl.program_id(1) < nst - 1, dn_ref[...], 0.0)
        taps = _tap_sources(buf, moved, offsets, ts)
        for r0, c0, cw in _ffn_chunks(ts, f2):
            cols = slice(c0, c0 + cw)
            o_ref[r0:r0 + FFN_ROWS, cols] = _taps_sum(taps, w_ref, jnp.zeros((FFN_ROWS, cw), F32), r0, cols).astype(BF16)

    return pl.pallas_call(
        body, name=name, grid=(dims.batch_local, nst),
        in_specs=[*_seq_specs(dims, ts, f2, FFN_HALO, 0, nxt=True), pl.BlockSpec((FFN_HALO, f2), lambda b_, i: (0, 0))],
        out_specs=pl.BlockSpec((ts, f2), lambda b_, i: (b_ * nst + i, 0)),
        out_shape=jax.ShapeDtypeStruct((t, f2), BF16),
        scratch_shapes=[pltpu.VMEM((ts + FFN_HALO, f2), F32), pltpu.VMEM((_moved_copies(offsets), ts, f2), F32)],
        compiler_params=_params("parallel", "parallel"),
    )(du, du, w)


def _alibi_slope(h, n_heads):
    return 2.0 ** (-8.0 * (h + 1) / n_heads)


def _dot_nt(a, b):
    return lax.dot_general(a, b, (((1,), (1,)), ((), ())), preferred_element_type=F32)


def _dot_tn(a, b):
    return lax.dot_general(a, b, (((0,), (0,)), ((), ())), preferred_element_type=F32)


def _attn_view(x, dims, dil):
    return x.reshape(dims.batch_local, dims.seq // dil, dil * x.shape[-1])


def _attn_fwd_group(q, k, v, state, dims, dil, *, last, name):
    t, a = q.shape
    assert 2 * dims.head_dim == 128 and dims.n_heads % 2 == 0
    blk, hd = ATTN_BLOCK, dims.head_dim
    nb = dims.seq // dil // blk
    has_prev = nb > 1
    nkeys = 2 * blk if has_prev else blk

    def body(*refs):
        it = iter(refs)
        q_ref, kc_ref, vc_ref = next(it), next(it), next(it)
        kp_ref, vp_ref = (next(it), next(it)) if has_prev else (None, None)
        m_in, l_in, acc_in = (next(it), next(it), next(it)) if state is not None else (None, None, None)
        outs = list(it)
        iq = lax.broadcasted_iota(jnp.int32, (blk, nkeys), 0)
        jk = lax.broadcasted_iota(jnp.int32, (blk, nkeys), 1)
        if has_prev:
            steps = iq + blk - jk
            valid = (steps >= 0) & (steps <= blk) & ((jk >= blk) | (pl.program_id(2) > 0))
        else:
            steps = iq - jk
            valid = steps >= 0
        dist = steps.astype(F32) * float(dil)
        low = lax.broadcasted_iota(jnp.int32, (blk, 2 * hd), 1) < hd
        for hp in range(dims.n_heads // 2):
            sl = slice(2 * hd * hp, 2 * hd * (hp + 1))
            q2 = q_ref[:, sl]
            if has_prev:
                kcat = jnp.concatenate([kp_ref[:, sl], kc_ref[:, sl]], axis=0)
                vcat = jnp.concatenate([vp_ref[:, sl], vc_ref[:, sl]], axis=0)
            else:
                kcat, vcat = kc_ref[:, sl], vc_ref[:, sl]
            halves = []
            for half in range(2):
                col = 2 * hd * hp + hd * half
                qh = jnp.where(low if half == 0 else jnp.logical_not(low), q2, jnp.zeros_like(q2))
                sc = _dot_nt(qh, kcat) - _alibi_slope(2 * hp + half, dims.n_heads) * dist
                sc = jnp.where(valid, sc, MASKED_SCORE)
                row_max = jnp.max(sc, axis=-1, keepdims=True)
                if state is None:
                    m_new = row_max
                    p = jnp.exp(sc - m_new)
                    alpha = None
                    l_new = jnp.sum(p, axis=-1, keepdims=True)
                else:
                    m_old = m_in[:, col:col + 1]
                    m_new = jnp.maximum(m_old, row_max)
                    p = jnp.exp(sc - m_new)
                    alpha = jnp.exp(m_old - m_new)
                    l_new = alpha * l_in[:, col:col + 1] + jnp.sum(p, axis=-1, keepdims=True)
                pv = jnp.dot(p.astype(BF16), vcat, preferred_element_type=F32)
                halves.append((m_new, l_new, alpha, pv))
            (m_a, l_a, al_a, pv_a), (m_b, l_b, al_b, pv_b) = halves
            if state is None:
                acc = jnp.where(low, pv_a, pv_b)
            else:
                old = acc_in[:, sl]
                acc = jnp.where(low, al_a * old + pv_a, al_b * old + pv_b)
            m2 = jnp.where(low, m_a, m_b)
            l2 = jnp.where(low, l_a, l_b)
            if last:
                outs[0][:, sl] = (acc / l2).astype(BF16)
                outs[1][:, sl] = m2 + jnp.log(l2)
            else:
                outs[0][:, sl] = m2
                outs[1][:, sl] = l2
                outs[2][:, sl] = acc

    cur = pl.BlockSpec((None, blk, a), lambda b, r, i: (b, i, r))
    prev = pl.BlockSpec((None, blk, a), lambda b, r, i: (b, jnp.maximum(i - 1, 0), r))
    args, in_specs = [q, k, v], [cur, cur, cur]
    if has_prev:
        args += [k, v]
        in_specs += [prev, prev]
    if state is not None:
        args += list(state)
        in_specs += [cur] * 3
    shape = lambda dt: jax.ShapeDtypeStruct((dims.batch_local, dims.seq // dil, dil * a), dt)
    out_shape = [shape(BF16), shape(F32)] if last else [shape(F32)] * 3
    outs = pl.pallas_call(
        body, name=name, grid=(dims.batch_local, dil, nb),
        in_specs=in_specs, out_specs=[cur] * len(out_shape), out_shape=out_shape,
        compiler_params=_params("parallel", "parallel", "parallel"),
    )(*[_attn_view(x, dims, dil) for x in args])
    return tuple(o.reshape(t, a) for o in outs)


def _attn_delta(do, o, head_ones, dims, *, name, tr=512):
    t, a = o.shape
    tr = _pick(t, tr, 8)

    def body(do_ref, o_ref, e_ref, d_ref):
        prod = do_ref[...].astype(F32) * o_ref[...].astype(F32)
        d_ref[...] = _head_mean(prod, e_ref, dims.head_dim) * float(dims.head_dim)

    return pl.pallas_call(
        body, name=name, grid=(t // tr,),
        in_specs=[_row_spec(tr, a), _row_spec(tr, a), pl.BlockSpec((a, a), lambda i: (0, 0))],
        out_specs=_row_spec(tr, a), out_shape=jax.ShapeDtypeStruct((t, a), F32),
        compiler_params=_params("parallel"),
    )(do, o, head_ones)


def _attn_bwd_group(q, k, v, do, lse, delta, dims, dil, *, name):
    t, a = q.shape
    blk, hd = ATTN_BLOCK, dims.head_dim
    nb = dims.seq // dil // blk
    has_next = nb > 1

    def body(*refs):
        k_ref, v_ref, q_ref, do_ref, lse_ref, dl_ref = refs[:6]
        if has_next:
            qn_ref, don_ref, lsen_ref, dln_ref = refs[6:10]
            dq_ref, dk_ref, dv_ref, carry = refs[10:]
        else:
            dq_ref, dk_ref, dv_ref = refs[6:]
        j = pl.program_id(2)
        iq = lax.broadcasted_iota(jnp.int32, (blk, blk), 0)
        jk = lax.broadcasted_iota(jnp.int32, (blk, blk), 1)
        low = lax.broadcasted_iota(jnp.int32, (blk, 2 * hd), 1) < hd

        def pair(hp, qr, dor, lser, dlr, steps, valid):
            sl = slice(2 * hd * hp, 2 * hd * (hp + 1))
            q2, do2, k2, v2 = qr[:, sl], dor[:, sl], k_ref[:, sl], v_ref[:, sl]
            dist = steps.astype(F32) * float(dil)
            dq_h, dk2, dv2 = [], None, None
            for half in range(2):
                col = 2 * hd * hp + hd * half
                mask = low if half == 0 else jnp.logical_not(low)
                qh = jnp.where(mask, q2, jnp.zeros_like(q2))
                doh = jnp.where(mask, do2, jnp.zeros_like(do2))
                sc = _dot_nt(qh, k2) - _alibi_slope(2 * hp + half, dims.n_heads) * dist
                p = jnp.where(valid, jnp.exp(sc - lser[:, col:col + 1]), 0.0)
                ds = p * (_dot_nt(doh, v2) - dlr[:, col:col + 1])
                ds_b, p_b = ds.astype(BF16), p.astype(BF16)
                dq_h.append(jnp.dot(ds_b, k2, preferred_element_type=F32))
                dk_h, dv_h = _dot_tn(ds_b, qh), _dot_tn(p_b, doh)
                dk2 = dk_h if dk2 is None else dk2 + dk_h
                dv2 = dv_h if dv2 is None else dv2 + dv_h
            return sl, jnp.where(low, dq_h[0], dq_h[1]), dk2, dv2

        if has_next:
            @pl.when(j == 0)
            def _():
                carry[...] = jnp.zeros_like(carry)

        for hp in range(dims.n_heads // 2):
            sl, dq2, dk2, dv2 = pair(hp, q_ref, do_ref, lse_ref, dl_ref, iq - jk, iq >= jk)
            dq_ref[:, sl] = (carry[:, sl] + dq2) if has_next else dq2
            dk_ref[:, sl] = dk2
            dv_ref[:, sl] = dv2

        if has_next:
            @pl.when(j + 1 < nb)
            def _():
                for hp in range(dims.n_heads // 2):
                    sl, dq2, dk2, dv2 = pair(hp, qn_ref, don_ref, lsen_ref, dln_ref, iq - jk + blk, jk >= iq)
                    carry[:, sl] = dq2
                    dk_ref[:, sl] += dk2
                    dv_ref[:, sl] += dv2

    cur = pl.BlockSpec((None, blk, a), lambda b, r, j: (b, j, r))
    nxt = pl.BlockSpec((None, blk, a), lambda b, r, j: (b, jnp.minimum(j + 1, nb - 1), r))
    args, in_specs = [k, v, q, do, lse, delta], [cur] * 6
    if has_next:
        args += [q, do, lse, delta]
        in_specs += [nxt] * 4
    shape = jax.ShapeDtypeStruct((dims.batch_local, dims.seq // dil, dil * a), F32)
    outs = pl.pallas_call(
        body, name=name, grid=(dims.batch_local, dil, nb),
        in_specs=in_specs, out_specs=[cur] * 3, out_shape=[shape] * 3,
        scratch_shapes=[pltpu.VMEM((blk, a), F32)] if has_next else [],
        compiler_params=_params("parallel", "parallel", "arbitrary"),
    )(*[_attn_view(x, dims, dil) for x in args])
    return tuple(o.reshape(t, a) for o in outs)


LANES = 128
MASK_BIAS = 1e30
RESIDUE_DILATIONS = tuple(d for d in DILATIONS if d > 1)


def _rows_to_residues(value, out_ref, scr, d):
    rows, width = value.shape
    for c in range(width // LANES):
        cols = slice(LANES * c, LANES * (c + 1))
        scr[c] = value[:, cols]
        for r in range(d):
            out_ref[r, :, cols] = scr[c, pl.ds(r, rows // d, stride=d), :].astype(out_ref.dtype)


def _residues_to_rows(in_ref, scr, d):
    _, n, width = in_ref.shape
    slabs = []
    for c in range(width // LANES):
        cols = slice(LANES * c, LANES * (c + 1))
        for r in range(d):
            scr[c, pl.ds(r, n, stride=d), :] = in_ref[r, :, cols].astype(F32)
        slabs.append(scr[c])
    return slabs[0] if len(slabs) == 1 else jnp.concatenate(slabs, axis=1)


def _residue_shape(dims, d, width, dtype):
    return jax.ShapeDtypeStruct((dims.batch_local, d, dims.seq // d, width), dtype)


def _residue_spec(dims, d, tr, width):
    tiles = dims.seq // tr
    return pl.BlockSpec((None, d, tr // d, width), lambda i: (i // tiles, 0, i % tiles, 0))


def _head_sum_matrix(dims):
    a = dims.n_heads * dims.head_dim
    head = jnp.arange(a, dtype=jnp.int32) // dims.head_dim
    return (head[:, None] == jnp.arange(LANES, dtype=jnp.int32)[None, :]).astype(BF16)


def _two_pass_dot(v, m):
    hi = v.astype(BF16)
    lo = (v - hi.astype(F32)).astype(BF16)
    return jnp.dot(hi, m, preferred_element_type=F32) + jnp.dot(lo, m, preferred_element_type=F32)


def _residue_permutations(tr):
    out = []
    for d in RESIDUE_DILATIONS:
        dst = jnp.arange(tr, dtype=jnp.int32)
        src = d * (dst % (tr // d)) + dst // (tr // d)
        out.append((src[:, None] == jnp.arange(tr, dtype=jnp.int32)[None, :]).astype(BF16))
    return out


def _bf16_rows_to_residues(value, out_ref, perm_ref, d):
    n = value.shape[0] // d
    moved = jnp.dot(perm_ref[...], value, preferred_element_type=F32).astype(out_ref.dtype)
    for r in range(d):
        out_ref[r] = moved[r * n:(r + 1) * n]


def _bf16_residues_to_rows(in_ref, back_ref):
    d = in_ref.shape[0]
    stacked = jnp.concatenate([in_ref[r] for r in range(d)], axis=0)
    return jnp.dot(back_ref[...], stacked, preferred_element_type=F32)


def _qkv_layouts_fwd(z, gq, gk, head_ones, dims, *, name, tr=256):
    t = z.shape[0]
    a = dims.n_heads * dims.head_dim
    q_scale = dims.head_dim ** -0.5
    nres = len(RESIDUE_DILATIONS)

    def body(q_ref, k_ref, v_ref, gq_ref, gk_ref, sum_ref, spread_ref, *rest):
        perm_refs, outs = rest[:nres], rest[nres:]
        qv, kv = q_ref[...].astype(F32), k_ref[...].astype(F32)
        mean = lambda val: _two_pass_dot(_two_pass_dot(val, sum_ref[...]), spread_ref[...]) * (1.0 / dims.head_dim)
        rq = lax.rsqrt(mean(qv * qv) + RMS_EPS)
        rk = lax.rsqrt(mean(kv * kv) + RMS_EPS)
        values = ((qv * rq * gq_ref[...] * q_scale).astype(BF16), (kv * rk * gk_ref[...]).astype(BF16), v_ref[...])
        for j, val in enumerate(values):
            outs[j][...] = val
            for g, d in enumerate(RESIDUE_DILATIONS):
                _bf16_rows_to_residues(val, outs[3 * (g + 1) + j], perm_refs[g], d)

    out_specs = [_row_spec(tr, a)] * 3
    out_shape = [jax.ShapeDtypeStruct((t, a), BF16)] * 3
    for d in RESIDUE_DILATIONS:
        out_specs += [_residue_spec(dims, d, tr, a)] * 3
        out_shape += [_residue_shape(dims, d, a, BF16)] * 3
    outs = pl.pallas_call(
        body, name=name, grid=(t // tr,),
        in_specs=[_row_spec(tr, a, 2), _row_spec(tr, a, 3), _row_spec(tr, a, 4), _vec_spec(a), _vec_spec(a),
                  pl.BlockSpec((a, LANES), lambda i: (0, 0)), pl.BlockSpec((LANES, a), lambda i: (0, 0))]
        + [pl.BlockSpec((tr, tr), lambda i: (0, 0))] * nres,
        out_specs=out_specs, out_shape=out_shape,
        compiler_params=_params("parallel"),
    )(z, z, z, gq, gk, *head_ones, *_residue_permutations(tr))
    return {d: tuple(outs[3 * g:3 * g + 3]) for g, d in enumerate((1,) + RESIDUE_DILATIONS)}


def _attn_specs(dims, dil, width):
    blk = ATTN_BLOCK
    nb = dims.seq // dil // blk
    if dil == 1:
        grid = (dims.batch_local, nb)
        at = lambda f: pl.BlockSpec((blk, width), lambda b, i: (b * nb + f(i), 0))
    else:
        grid = (dims.batch_local, dil, nb)
        at = lambda f: pl.BlockSpec((None, None, blk, width), lambda b, r, i: (b, r, f(i), 0))
    return grid, at(lambda i: i), at(lambda i: jnp.maximum(i - 1, 0)), at(lambda i: jnp.minimum(i + 1, nb - 1))


def _head_slopes(n_heads):
    h = lax.broadcasted_iota(jnp.int32, (n_heads, 1, 1), 0).astype(F32)
    return jnp.exp((h + 1.0) * (-8.0 / n_heads * math.log(2.0)))


def _pair_masks(hd):
    low = lax.broadcasted_iota(jnp.int32, (1, 2 * hd), 1) < hd
    return low, jnp.logical_not(low)


def _attn_fwd(q, k, v, dims, dil, *, name):
    a = dims.n_heads * dims.head_dim
    heads, hd, blk = dims.n_heads, dims.head_dim, ATTN_BLOCK
    assert 2 * hd == LANES and heads % 2 == 0 and heads <= LANES
    nb = dims.seq // dil // blk
    has_prev = nb > 1
    nkeys = 2 * blk if has_prev else blk
    grid, cur, prev, _ = _attn_specs(dims, dil, a)
    _, cur_stat, _, _ = _attn_specs(dims, dil, LANES)

    def body(*refs):
        if has_prev:
            q_ref, kc_ref, vc_ref, kp_ref, vp_ref, o_ref, lse_ref, s_scr, p_scr, k_st, v_st = refs
            k_st[0:blk, :], k_st[blk:, :] = kp_ref[...], kc_ref[...]
            v_st[0:blk, :], v_st[blk:, :] = vp_ref[...], vc_ref[...]
        else:
            q_ref, k_st, v_st, o_ref, lse_ref, s_scr, p_scr = refs
        low, high = _pair_masks(hd)

        for hp in range(heads // 2):
            sl = slice(LANES * hp, LANES * (hp + 1))
            q2 = q_ref[:, sl]
            kcat = k_st[:, sl]
            s_scr[2 * hp] = _dot_nt(jnp.where(low, q2, jnp.zeros_like(q2)), kcat)
            s_scr[2 * hp + 1] = _dot_nt(jnp.where(high, q2, jnp.zeros_like(q2)), kcat)

        iq = lax.broadcasted_iota(jnp.int32, (blk, nkeys), 0)
        jk = lax.broadcasted_iota(jnp.int32, (blk, nkeys), 1)
        if has_prev:
            steps = iq + blk - jk
            valid = (steps >= 0) & (steps <= blk) & ((jk >= blk) | (pl.program_id(len(grid) - 1) > 0))
        else:
            steps = iq - jk
            valid = steps >= 0
        bias = jnp.where(valid, steps.astype(F32) * (-float(dil)), -MASK_BIAS)
        s = s_scr[...] + _head_slopes(heads) * bias[None]
        m = jnp.max(s, axis=-1, keepdims=True)
        p = jnp.exp(s - m)
        l = jnp.sum(p, axis=-1, keepdims=True)
        p_scr[...] = p.astype(BF16)
        inv = 1.0 / l
        lse = m + jnp.log(l)

        lane = lax.broadcasted_iota(jnp.int32, (blk, LANES), 1)
        stat = jnp.zeros((blk, LANES), F32)
        for hp in range(heads // 2):
            sl = slice(LANES * hp, LANES * (hp + 1))
            vcat = v_st[:, sl]
            pv_a = jnp.dot(p_scr[2 * hp], vcat, preferred_element_type=F32) * inv[2 * hp]
            pv_b = jnp.dot(p_scr[2 * hp + 1], vcat, preferred_element_type=F32) * inv[2 * hp + 1]
            o_ref[:, sl] = jnp.where(low, pv_a, pv_b).astype(BF16)
            stat = jnp.where(lane == 2 * hp, lse[2 * hp], stat)
            stat = jnp.where(lane == 2 * hp + 1, lse[2 * hp + 1], stat)
        lse_ref[...] = stat

    lead = q.shape[:-2]
    rows = q.shape[-2]
    o, lse = pl.pallas_call(
        body, name=name, grid=grid,
        in_specs=[cur, cur, cur] + ([prev, prev] if has_prev else []),
        out_specs=[cur, cur_stat],
        out_shape=[jax.ShapeDtypeStruct(lead + (rows, a), BF16), jax.ShapeDtypeStruct(lead + (rows, LANES), F32)],
        scratch_shapes=[pltpu.VMEM((heads, blk, nkeys), F32), pltpu.VMEM((heads, blk, nkeys), BF16)]
        + ([pltpu.VMEM((nkeys, a), BF16)] * 2 if has_prev else []),
        compiler_params=_params(*["parallel"] * len(grid)),
    )(q, k, v, *([k, v] if has_prev else []))
    return o, lse


def _attn_fwd_keys_major(q, k, v, dims, dil, *, name):
    a = dims.n_heads * dims.head_dim
    heads, hd, blk = dims.n_heads, dims.head_dim, ATTN_BLOCK
    assert 2 * hd == LANES and heads % 2 == 0 and heads <= LANES
    nb = dims.seq // dil // blk
    has_prev = nb > 1
    nkeys = 2 * blk if has_prev else blk
    grid, cur, prev, _ = _attn_specs(dims, dil, a)
    _, cur_stat, _, _ = _attn_specs(dims, dil, LANES)

    def body(*refs):
        if has_prev:
            q_ref, kc_ref, vc_ref, kp_ref, vp_ref, o_ref, lse_ref, s_scr, p_scr, k_st, v_st = refs
            k_st[0:blk, :], k_st[blk:, :] = kp_ref[...], kc_ref[...]
            v_st[0:blk, :], v_st[blk:, :] = vp_ref[...], vc_ref[...]
        else:
            q_ref, k_st, v_st, o_ref, lse_ref, s_scr, p_scr = refs
        low, high = _pair_masks(hd)
        for hp in range(heads // 2):
            sl = slice(LANES * hp, LANES * (hp + 1))
            q2, kcat = q_ref[:, sl], k_st[:, sl]
            s_scr[2 * hp] = _dot_nt(kcat, jnp.where(low, q2, jnp.zeros_like(q2)))
            s_scr[2 * hp + 1] = _dot_nt(kcat, jnp.where(high, q2, jnp.zeros_like(q2)))

        jk = lax.broadcasted_iota(jnp.int32, (nkeys, blk), 0)
        iq = lax.broadcasted_iota(jnp.int32, (nkeys, blk), 1)
        if has_prev:
            steps = iq + blk - jk
            valid = (steps >= 0) & (steps <= blk) & ((jk >= blk) | (pl.program_id(len(grid) - 1) > 0))
        else:
            steps = iq - jk
            valid = steps >= 0
        bias = jnp.where(valid, steps.astype(F32) * (-float(dil)), -MASK_BIAS)
        s = s_scr[...] + _head_slopes(heads) * bias[None]
        m = jnp.max(s, axis=1, keepdims=True)
        p = jnp.exp(s - m)
        l = jnp.sum(p, axis=1, keepdims=True)
        p_scr[...] = (p * (1.0 / l)).astype(BF16)
        lse = m + jnp.log(l)

        for hp in range(heads // 2):
            sl = slice(LANES * hp, LANES * (hp + 1))
            vcat = v_st[:, sl]
            o_ref[:, sl] = jnp.where(low, _dot_tn(p_scr[2 * hp], vcat), _dot_tn(p_scr[2 * hp + 1], vcat))
        row = lax.broadcasted_iota(jnp.int32, (LANES, blk), 0)
        by_head = jnp.zeros((LANES, blk), F32)
        for h in range(heads):
            by_head = jnp.where(row == h, lse[h], by_head)
        lse_ref[...] = jnp.transpose(by_head)

    lead = q.shape[:-2]
    rows = q.shape[-2]
    scratch = [pltpu.VMEM((heads, nkeys, blk), F32), pltpu.VMEM((heads, nkeys, blk), BF16)]
    if has_prev:
        scratch += [pltpu.VMEM((nkeys, a), BF16)] * 2
    o, lse = pl.pallas_call(
        body, name=name, grid=grid,
        in_specs=[cur, cur, cur] + ([prev, prev] if has_prev else []),
        out_specs=[cur, cur_stat],
        out_shape=[jax.ShapeDtypeStruct(lead + (rows, a), F32), jax.ShapeDtypeStruct(lead + (rows, LANES), F32)],
        scratch_shapes=scratch,
        compiler_params=_params(*["parallel"] * len(grid)),
    )(q, k, v, *([k, v] if has_prev else []))
    return o, lse


def _attn_combine(groups, head_spread, dims, *, name, tr=256):
    t = dims.tokens
    a = dims.n_heads * dims.head_dim
    dils = tuple(groups)

    nres = len(RESIDUE_DILATIONS)

    def body(*refs):
        ins = refs[:2 * len(dils)]
        x_ref = refs[2 * len(dils)]
        back_refs = dict(zip(RESIDUE_DILATIONS, refs[2 * len(dils) + 1:2 * len(dils) + 1 + nres]))
        o_ref = refs[2 * len(dils) + 1 + nres]
        lse_refs = refs[2 * len(dils) + 2 + nres:-1]
        scr_stat = refs[-1]
        outs, stats = [], []
        for g, d in enumerate(dils):
            if d == 1:
                outs.append(ins[2 * g][...].astype(F32))
                stats.append(ins[2 * g + 1][...])
            else:
                outs.append(_bf16_residues_to_rows(ins[2 * g], back_refs[d]))
                stats.append(_residues_to_rows(ins[2 * g + 1], scr_stat, d))
        top = functools.reduce(jnp.maximum, stats)
        weights = [jnp.exp(s - top) for s in stats]
        total = functools.reduce(jnp.add, weights)
        joint = top + jnp.log(total)
        inv = 1.0 / total
        acc = None
        for w, o in zip(weights, outs):
            term = _two_pass_dot(w * inv, x_ref[...]) * o
            acc = term if acc is None else acc + term
        o_ref[...] = acc.astype(BF16)
        for g, d in enumerate(dils):
            if d == 1:
                lse_refs[g][...] = joint
            else:
                _rows_to_residues(joint, lse_refs[g], scr_stat, d)

    in_specs, args, lse_specs, lse_shapes = [], [], [], []
    for d in dils:
        if d == 1:
            in_specs += [_row_spec(tr, a), _row_spec(tr, LANES)]
            lse_specs.append(_row_spec(tr, LANES))
            lse_shapes.append(jax.ShapeDtypeStruct((t, LANES), F32))
        else:
            in_specs += [_residue_spec(dims, d, tr, a), _residue_spec(dims, d, tr, LANES)]
            lse_specs.append(_residue_spec(dims, d, tr, LANES))
            lse_shapes.append(_residue_shape(dims, d, LANES, F32))
        args += list(groups[d])
    outs = pl.pallas_call(
        body, name=name, grid=(t // tr,),
        in_specs=in_specs + [pl.BlockSpec((LANES, a), lambda i: (0, 0))] + [pl.BlockSpec((tr, tr), lambda i: (0, 0))] * nres,
        out_specs=[_row_spec(tr, a)] + lse_specs,
        out_shape=[jax.ShapeDtypeStruct((t, a), BF16)] + lse_shapes,
        scratch_shapes=[pltpu.VMEM((1, tr, LANES), F32)],
        compiler_params=_params("parallel"),
    )(*args, head_spread, *[jnp.transpose(p) for p in _residue_permutations(tr)])
    return outs[0], dict(zip(dils, outs[1:]))


def _attn_bwd_prep(do, o, head_sum, dims, *, name, tr=256):
    t, a = o.shape
    nres = len(RESIDUE_DILATIONS)

    def body(do_ref, o_ref, e_ref, *rest):
        perm_refs, outs, scr_stat = rest[:nres], rest[nres:-1], rest[-1]
        delta = _two_pass_dot(do_ref[...].astype(F32) * o_ref[...].astype(F32), e_ref[...])
        outs[0][...] = delta
        for g, d in enumerate(RESIDUE_DILATIONS):
            _bf16_rows_to_residues(do_ref[...], outs[1 + 2 * g], perm_refs[g], d)
            _rows_to_residues(delta, outs[2 + 2 * g], scr_stat, d)

    out_specs, out_shape = [_row_spec(tr, LANES)], [jax.ShapeDtypeStruct((t, LANES), F32)]
    for d in RESIDUE_DILATIONS:
        out_specs += [_residue_spec(dims, d, tr, a), _residue_spec(dims, d, tr, LANES)]
        out_shape += [_residue_shape(dims, d, a, BF16), _residue_shape(dims, d, LANES, F32)]
    outs = pl.pallas_call(
        body, name=name, grid=(t // tr,),
        in_specs=[_row_spec(tr, a), _row_spec(tr, a), pl.BlockSpec((a, LANES), lambda i: (0, 0))]
        + [pl.BlockSpec((tr, tr), lambda i: (0, 0))] * nres,
        out_specs=out_specs, out_shape=out_shape,
        scratch_shapes=[pltpu.VMEM((1, tr, LANES), F32)],
        compiler_params=_params("parallel"),
    )(do, o, head_sum, *_residue_permutations(tr))
    dos, deltas = {1: do}, {1: outs[0]}
    for g, d in enumerate(RESIDUE_DILATIONS):
        dos[d], deltas[d] = outs[1 + 2 * g], outs[2 + 2 * g]
    return dos, deltas


def _attn_bwd(q, k, v, do, lse, delta, dims, dil, *, name):
    a = dims.n_heads * dims.head_dim
    heads, hd, blk = dims.n_heads, dims.head_dim, ATTN_BLOCK
    nb = dims.seq // dil // blk
    has_next = nb > 1
    nq = 2 * blk if has_next else blk
    grid, cur, _, nxt = _attn_specs(dims, dil, a)
    _, cur_stat, _, nxt_stat = _attn_specs(dims, dil, LANES)

    def body(*refs):
        k_ref, v_ref, q_ref, do_ref, lse_ref, dl_ref = refs[:6]
        if has_next:
            qn_ref, don_ref, lsen_ref, dln_ref = refs[6:10]
            dq_ref, dk_ref, dv_ref, s_scr, dp_scr, p_scr, ds_scr, carry = refs[10:]
        else:
            dq_ref, dk_ref, dv_ref, s_scr, dp_scr, p_scr, ds_scr = refs[6:]
        j = pl.program_id(len(grid) - 1)
        low, high = _pair_masks(hd)

        def stacked(ref, nref, sl):
            return jnp.concatenate([ref[:, sl], nref[:, sl]], axis=0) if has_next else ref[:, sl]

        def halves(x):
            return jnp.where(low, x, jnp.zeros_like(x)), jnp.where(high, x, jnp.zeros_like(x))

        for hp in range(heads // 2):
            sl = slice(LANES * hp, LANES * (hp + 1))
            k2, v2 = k_ref[:, sl], v_ref[:, sl]
            q_a, q_b = halves(stacked(q_ref, qn_ref if has_next else None, sl))
            do_a, do_b = halves(stacked(do_ref, don_ref if has_next else None, sl))
            s_scr[2 * hp], s_scr[2 * hp + 1] = _dot_nt(q_a, k2), _dot_nt(q_b, k2)
            dp_scr[2 * hp], dp_scr[2 * hp + 1] = _dot_nt(do_a, v2), _dot_nt(do_b, v2)

        rq = lax.broadcasted_iota(jnp.int32, (nq, blk), 0)
        jk = lax.broadcasted_iota(jnp.int32, (nq, blk), 1)
        if has_next:
            iq = jnp.where(rq < blk, rq, rq - blk)
            steps = jnp.where(rq < blk, iq - jk, iq - jk + blk)
            valid = ((rq < blk) & (iq >= jk)) | ((rq >= blk) & (jk >= iq) & (j + 1 < nb))
        else:
            steps, valid = rq - jk, rq >= jk
        bias = jnp.where(valid, steps.astype(F32) * (-float(dil)), -MASK_BIAS)
        lse_all = stacked(lse_ref, lsen_ref if has_next else None, slice(None))
        dl_all = stacked(dl_ref, dln_ref if has_next else None, slice(None))
        lse3 = jnp.stack([lse_all[:, h:h + 1] for h in range(heads)])
        dl3 = jnp.stack([dl_all[:, h:h + 1] for h in range(heads)])
        p = jnp.exp(s_scr[...] + _head_slopes(heads) * bias[None] - lse3)
        p_scr[...] = p.astype(BF16)
        ds_scr[...] = (p * (dp_scr[...] - dl3)).astype(BF16)

        if has_next:
            @pl.when(j == 0)
            def _():
                carry[...] = jnp.zeros_like(carry)

        for hp in range(heads // 2):
            sl = slice(LANES * hp, LANES * (hp + 1))
            k2 = k_ref[:, sl]
            q_a, q_b = halves(stacked(q_ref, qn_ref if has_next else None, sl))
            do_a, do_b = halves(stacked(do_ref, don_ref if has_next else None, sl))
            ds_a, ds_b = ds_scr[2 * hp], ds_scr[2 * hp + 1]
            dq2 = jnp.where(low, jnp.dot(ds_a, k2, preferred_element_type=F32),
                            jnp.dot(ds_b, k2, preferred_element_type=F32))
            dk_ref[:, sl] = _dot_tn(ds_a, q_a) + _dot_tn(ds_b, q_b)
            dv_ref[:, sl] = _dot_tn(p_scr[2 * hp], do_a) + _dot_tn(p_scr[2 * hp + 1], do_b)
            if has_next:
                dq_ref[:, sl] = carry[:, sl] + dq2[:blk]
                carry[:, sl] = dq2[blk:]
            else:
                dq_ref[:, sl] = dq2

    args, in_specs = [k, v, q, do, lse, delta], [cur] * 4 + [cur_stat] * 2
    if has_next:
        args += [q, do, lse, delta]
        in_specs += [nxt] * 2 + [nxt_stat] * 2
    shape = jax.ShapeDtypeStruct(q.shape, F32)
    scratch = [pltpu.VMEM((heads, nq, blk), F32)] * 2 + [pltpu.VMEM((heads, nq, blk), BF16)] * 2
    if has_next:
        scratch.append(pltpu.VMEM((blk, a), F32))
    return pl.pallas_call(
        body, name=name, grid=grid, in_specs=in_specs, out_specs=[cur] * 3, out_shape=[shape] * 3,
        scratch_shapes=scratch,
        compiler_params=_params(*["parallel"] * (len(grid) - 1), "arbitrary"),
    )(*args)


def _attn_bwd_keys_major(q, k, v, do, lse, delta, dims, dil, *, name):
    a = dims.n_heads * dims.head_dim
    heads, hd, blk = dims.n_heads, dims.head_dim, ATTN_BLOCK
    nb = dims.seq // dil // blk
    has_next = nb > 1
    nq = 2 * blk if has_next else blk
    grid, cur, _, nxt = _attn_specs(dims, dil, a)
    _, cur_stat, _, nxt_stat = _attn_specs(dims, dil, LANES)

    def body(*refs):
        k_ref, v_ref, q_ref, do_ref, lse_ref, dl_ref = refs[:6]
        if has_next:
            qn_ref, don_ref, lsen_ref, dln_ref = refs[6:10]
            dq_ref, dk_ref, dv_ref, q_st, do_st, s_scr, dp_scr, p_scr, ds_scr, carry = refs[10:]
        else:
            dq_ref, dk_ref, dv_ref, q_st, do_st, s_scr, dp_scr, p_scr, ds_scr = refs[6:]
        j = pl.program_id(len(grid) - 1)
        low, high = _pair_masks(hd)
        q_st[0:blk, :] = q_ref[...]
        do_st[0:blk, :] = do_ref[...]
        if has_next:
            q_st[blk:, :] = qn_ref[...]
            do_st[blk:, :] = don_ref[...]
            lse_all = jnp.concatenate([lse_ref[...], lsen_ref[...]], axis=0)
            dl_all = jnp.concatenate([dl_ref[...], dln_ref[...]], axis=0)
        else:
            lse_all, dl_all = lse_ref[...], dl_ref[...]
        lse_t, dl_t = jnp.transpose(lse_all), jnp.transpose(dl_all)
        lse3 = jnp.stack([lse_t[h:h + 1, :] for h in range(heads)])
        dl3 = jnp.stack([dl_t[h:h + 1, :] for h in range(heads)])

        def halves(x):
            return jnp.where(low, x, jnp.zeros_like(x)), jnp.where(high, x, jnp.zeros_like(x))

        for hp in range(heads // 2):
            sl = slice(LANES * hp, LANES * (hp + 1))
            k2, v2 = k_ref[:, sl], v_ref[:, sl]
            q_a, q_b = halves(q_st[:, sl])
            do_a, do_b = halves(do_st[:, sl])
            s_scr[2 * hp], s_scr[2 * hp + 1] = _dot_nt(k2, q_a), _dot_nt(k2, q_b)
            dp_scr[2 * hp], dp_scr[2 * hp + 1] = _dot_nt(v2, do_a), _dot_nt(v2, do_b)

        jk = lax.broadcasted_iota(jnp.int32, (blk, nq), 0)
        rq = lax.broadcasted_iota(jnp.int32, (blk, nq), 1)
        if has_next:
            iq = jnp.where(rq < blk, rq, rq - blk)
            steps = jnp.where(rq < blk, iq - jk, iq - jk + blk)
            valid = ((rq < blk) & (iq >= jk)) | ((rq >= blk) & (jk >= iq) & (j + 1 < nb))
        else:
            steps, valid = rq - jk, rq >= jk
        bias = jnp.where(valid, steps.astype(F32) * (-float(dil)), -MASK_BIAS)
        p = jnp.exp(s_scr[...] + _head_slopes(heads) * bias[None] - lse3)
        p_scr[...] = p.astype(BF16)
        ds_scr[...] = (p * (dp_scr[...] - dl3)).astype(BF16)

        if has_next:
            @pl.when(j == 0)
            def _():
                carry[...] = jnp.zeros_like(carry)

        for hp in range(heads // 2):
            sl = slice(LANES * hp, LANES * (hp + 1))
            k2 = k_ref[:, sl]
            q_a, q_b = halves(q_st[:, sl])
            do_a, do_b = halves(do_st[:, sl])
            ds_a, ds_b = ds_scr[2 * hp], ds_scr[2 * hp + 1]
            dk_ref[:, sl] = (jnp.dot(ds_a, q_a, preferred_element_type=F32)
                             + jnp.dot(ds_b, q_b, preferred_element_type=F32)).astype(BF16)
            dv_ref[:, sl] = (jnp.dot(p_scr[2 * hp], do_a, preferred_element_type=F32)
                             + jnp.dot(p_scr[2 * hp + 1], do_b, preferred_element_type=F32)).astype(BF16)
            dq2 = jnp.where(low, _dot_tn(ds_a, k2), _dot_tn(ds_b, k2))
            if has_next:
                dq_ref[:, sl] = (carry[:, sl] + dq2[:blk]).astype(BF16)
                carry[:, sl] = dq2[blk:]
            else:
                dq_ref[:, sl] = dq2.astype(BF16)

    args, in_specs = [k, v, q, do, lse, delta], [cur] * 4 + [cur_stat] * 2
    if has_next:
        args += [q, do, lse, delta]
        in_specs += [nxt] * 2 + [nxt_stat] * 2
    shape = jax.ShapeDtypeStruct(q.shape, BF16)
    scratch = ([pltpu.VMEM((nq, a), BF16)] * 2 + [pltpu.VMEM((heads, blk, nq), F32)] * 2
               + [pltpu.VMEM((heads, blk, nq), BF16)] * 2)
    if has_next:
        scratch.append(pltpu.VMEM((blk, a), F32))
    return pl.pallas_call(
        body, name=name, grid=grid, in_specs=in_specs, out_specs=[cur] * 3, out_shape=[shape] * 3,
        scratch_shapes=scratch,
        compiler_params=_params(*["parallel"] * (len(grid) - 1), "arbitrary"),
    )(*args)


def _qkv_layouts_bwd(z, grads, gq, gk, head_ones, dims, *, name, tr=256):
    t = z.shape[0]
    a = dims.n_heads * dims.head_dim
    q_scale = dims.head_dim ** -0.5
    dils = tuple(grads)
    nres = len(RESIDUE_DILATIONS)

    def body(q_ref, k_ref, *rest):
        d_refs = rest[:3 * len(dils)]
        gq_ref, gk_ref, sum_ref, spread_ref = rest[3 * len(dils):3 * len(dils) + 4]
        back_refs = dict(zip(RESIDUE_DILATIONS, rest[3 * len(dils) + 4:3 * len(dils) + 4 + nres]))
        dz_ref, dgq_ref, dgk_ref = rest[3 * len(dils) + 4 + nres:]
        first = pl.program_id(0) == 0
        mean = lambda val: _two_pass_dot(_two_pass_dot(val, sum_ref[...]), spread_ref[...]) * (1.0 / dims.head_dim)

        def total(j):
            acc = None
            for g, d in enumerate(dils):
                ref = d_refs[3 * g + j]
                part = ref[...].astype(F32) if d == 1 else _bf16_residues_to_rows(ref, back_refs[d])
                acc = part if acc is None else acc + part
            return acc

        def norm_bwd(x_ref, dy, g_ref, scale, col, dg_ref):
            xv = x_ref[...].astype(F32)
            dy = dy * scale
            r = lax.rsqrt(mean(xv * xv) + RMS_EPS)
            gy = dy * g_ref[...]
            dx = r * gy - xv * (r * r * r) * mean(xv * gy)
            dz_ref[:, col * a:(col + 1) * a] = dx.astype(BF16)
            _accumulate(dg_ref, jnp.sum(dy * xv * r, axis=0, keepdims=True), first)

        norm_bwd(q_ref, total(0), gq_ref, q_scale, 0, dgq_ref)
        norm_bwd(k_ref, total(1), gk_ref, 1.0, 1, dgk_ref)
        dz_ref[:, 2 * a:3 * a] = total(2).astype(BF16)

    in_specs, args = [_row_spec(tr, a, 2), _row_spec(tr, a, 3)], [z, z]
    for d in dils:
        in_specs += [_row_spec(tr, a) if d == 1 else _residue_spec(dims, d, tr, a)] * 3
        args += list(grads[d])
    in_specs += [_vec_spec(a), _vec_spec(a), pl.BlockSpec((a, LANES), lambda i: (0, 0)),
                 pl.BlockSpec((LANES, a), lambda i: (0, 0))] + [pl.BlockSpec((tr, tr), lambda i: (0, 0))] * nres
    return pl.pallas_call(
        body, name=name, grid=(t // tr,), in_specs=in_specs,
        out_specs=[_row_spec(tr, 3 * a), _vec_spec(a), _vec_spec(a)],
        out_shape=[jax.ShapeDtypeStruct((t, 3 * a), BF16)] + [jax.ShapeDtypeStruct((1, a), F32)] * 2,
        compiler_params=_params("arbitrary"),
    )(*args, gq, gk, *head_ones, *[jnp.transpose(p) for p in _residue_permutations(tr)])


def _mix_fwd(ya, yb, z, gate_b, dims, *, name, tr=512):
    t, d = ya.shape
    tr = _pick(t, tr, 8)
    first_gate_col = z.shape[1] // d - 2

    def body(ya_ref, yb_ref, ga_ref, gb_ref, ba_ref, bb_ref, o_ref):
        g_a = _sigmoid(ga_ref[...].astype(F32) + ba_ref[...])
        g_b = _sigmoid(gb_ref[...].astype(F32) + bb_ref[...])
        o_ref[...] = (g_a * ya_ref[...] + g_b * yb_ref[...]).astype(BF16)

    return pl.pallas_call(
        body, name=name, grid=(t // tr,),
        in_specs=[_row_spec(tr, d), _row_spec(tr, d), _row_spec(tr, d, first_gate_col),
                  _row_spec(tr, d, first_gate_col + 1), _vec_spec(d, 0), _vec_spec(d, 1)],
        out_specs=_row_spec(tr, d), out_shape=jax.ShapeDtypeStruct((t, d), BF16),
        compiler_params=_params("parallel"),
    )(ya, yb, z, z, gate_b, gate_b)


def _mix_bwd(dmix, ya, yb, z, gate_b, dims, *, name, tr=512):
    t, d = ya.shape
    tr = _pick(t, tr, 8)
    first_gate_col = z.shape[1] // d - 2

    def body(dm_ref, ya_ref, yb_ref, ga_ref, gb_ref, ba_ref, bb_ref, dya_ref, dyb_ref, dz_ref, db_ref):
        dm = dm_ref[...].astype(F32)
        g_a = _sigmoid(ga_ref[...].astype(F32) + ba_ref[...])
        g_b = _sigmoid(gb_ref[...].astype(F32) + bb_ref[...])
        dya_ref[...] = (dm * g_a).astype(BF16)
        dyb_ref[...] = (dm * g_b).astype(BF16)
        dl_a = dm * ya_ref[...] * g_a * (1.0 - g_a)
        dl_b = dm * yb_ref[...] * g_b * (1.0 - g_b)
        dz_ref[:, 0:d] = dl_a.astype(BF16)
        dz_ref[:, d:2 * d] = dl_b.astype(BF16)
        first = pl.program_id(0) == 0
        sums = jnp.concatenate([jnp.sum(dl_a, axis=0, keepdims=True), jnp.sum(dl_b, axis=0, keepdims=True)], axis=1)
        _accumulate(db_ref, sums, first)

    return pl.pallas_call(
        body, name=name, grid=(t // tr,),
        in_specs=[_row_spec(tr, d), _row_spec(tr, d), _row_spec(tr, d), _row_spec(tr, d, first_gate_col),
                  _row_spec(tr, d, first_gate_col + 1), _vec_spec(d, 0), _vec_spec(d, 1)],
        out_specs=[_row_spec(tr, d), _row_spec(tr, d), _row_spec(tr, 2 * d), _vec_spec(2 * d)],
        out_shape=[jax.ShapeDtypeStruct((t, d), BF16)] * 2 + [jax.ShapeDtypeStruct((t, 2 * d), BF16),
                                                              jax.ShapeDtypeStruct((1, 2 * d), F32)],
        compiler_params=_params("arbitrary"),
    )(dmix, ya, yb, z, z, gate_b, gate_b)


def _loss_head(y, target, *, name, tr=512):
    t, d = y.shape
    tr = _pick(t, tr, 8)

    def body(y_ref, t_ref, dy_ref, dyb_ref, loss_ref):
        err = y_ref[...] - t_ref[...]
        dy = err * (1.0 / d)
        dy_ref[...] = dy
        dyb_ref[...] = dy.astype(BF16)
        part = jnp.sum(jnp.sum(err * err, axis=-1, keepdims=True), axis=0, keepdims=True) * (0.5 / d)
        _accumulate(loss_ref, jnp.broadcast_to(part, (8, 128)), pl.program_id(0) == 0)

    return pl.pallas_call(
        body, name=name, grid=(t // tr,),
        in_specs=[_row_spec(tr, d), _row_spec(tr, d)],
        out_specs=[_row_spec(tr, d), _row_spec(tr, d), pl.BlockSpec((8, 128), lambda i: (0, 0))],
        out_shape=[jax.ShapeDtypeStruct((t, d), F32), jax.ShapeDtypeStruct((t, d), BF16),
                   jax.ShapeDtypeStruct((8, 128), F32)],
        compiler_params=_params("arbitrary"),
    )(y, target)


def _adamw(w, grads, m, v, *, name, tr=256):
    r, c = w.shape
    tr = _pick(r, tr, 8)
    ng = len(grads)
    c1 = 1.0 - ADAM_B1 ** ADAM_STEP
    c2 = 1.0 - ADAM_B2 ** ADAM_STEP

    def body(*refs):
        w_ref, g_refs, m_ref, v_ref = refs[0], refs[1:1 + ng], refs[1 + ng], refs[2 + ng]
        g_out, d_out, m_out, v_out = refs[3 + ng:]
        g = g_refs[0][...]
        for extra in g_refs[1:]:
            g = g + extra[...]
        m_new = ADAM_B1 * m_ref[...] + (1.0 - ADAM_B1) * g
        v_new = ADAM_B2 * v_ref[...] + (1.0 - ADAM_B2) * (g * g)
        g_out[...] = g
        m_out[...] = m_new
        v_out[...] = v_new
        d_out[...] = -ADAM_LR * ((m_new / c1) / (jnp.sqrt(v_new / c2) + ADAM_EPS) + ADAM_WD * w_ref[...])

    spec = pl.BlockSpec((tr, c), lambda i: (i, 0))
    return pl.pallas_call(
        body, name=name, grid=(r // tr,),
        in_specs=[spec] * (3 + ng), out_specs=[spec] * 4, out_shape=[jax.ShapeDtypeStruct((r, c), F32)] * 4,
        compiler_params=_params("parallel"),
    )(w, *grads, m, v)


CHIP_PEERS = ((1, 0), (0, 1), (1, 1))


def _place():
    return lax.axis_index("x"), lax.axis_index("y"), lax.axis_index("c")


HBM = pl.BlockSpec(memory_space=pltpu.HBM)
SEM = pl.BlockSpec(memory_space=pltpu.SEMAPHORE)
IN_FLIGHT = pltpu.SideEffectType.DATAFLOW_SIDE_EFFECTING


def _in_hbm(a):
    return pltpu.with_memory_space_constraint(a, pltpu.HBM)


def _cast_to_lands(shards, dtypes, *, name, after=None):
    n = len(shards)

    def body(*refs):
        ins, outs, bufs, sems = refs[:n], refs[n:2 * n], refs[2 * n:3 * n], refs[3 * n]
        x, y, _ = _place()
        copies = []
        for a in range(n):
            bufs[a][...] = ins[a][...].astype(dtypes[a])
            cp = pltpu.make_async_copy(bufs[a], outs[a].at[2 * x + y], sems.at[a])
            cp.start()
            copies.append(cp)
        for cp in copies:
            cp.wait()

    body, more_specs, more_args = _ordered(body, n, after)
    return pl.pallas_call(
        body, name=name, in_specs=[pl.BlockSpec(memory_space=pltpu.VMEM)] * n + more_specs, out_specs=[ANY] * n,
        out_shape=[jax.ShapeDtypeStruct((N_CHIPS,) + s.shape, dt) for s, dt in zip(shards, dtypes)],
        scratch_shapes=[pltpu.VMEM(s.shape, dt) for s, dt in zip(shards, dtypes)] + [pltpu.SemaphoreType.DMA((n,))],
        compiler_params=pltpu.CompilerParams(vmem_limit_bytes=V7X_VMEM_LIMIT_BYTES),
    )(*shards, *more_args)


def _chip_copy(src, dst, send, recv, flip, place):
    x, y, c = place
    return pltpu.make_async_remote_copy(src_ref=src, dst_ref=dst, send_sem=send, recv_sem=recv,
                                        device_id=(x ^ flip[0], y ^ flip[1], c), device_id_type=MESH)


def _my_part(land, place, halved):
    block = land.at[2 * place[0] + place[1]]
    if not halved:
        return block
    rows = land.shape[1] // 2
    return block.at[pl.ds(pl.multiple_of(place[2] * rows, rows), rows)]


def _gather_start(lands, after, *, name, halved=()):
    n = len(lands)

    def body(*refs):
        ins, send, recv, token = refs[:n], refs[n + 1], refs[n + 2], refs[-1]
        place = _place()
        for a in range(n):
            part = _my_part(ins[a], place, a in halved)
            for p, flip in enumerate(CHIP_PEERS):
                k = 3 * a + p
                _chip_copy(part, part, send.at[k], recv.at[k], flip, place).start()
        token[...] = jnp.zeros_like(token)

    outs = pl.pallas_call(
        body, name=name, in_specs=[HBM] * n + [ANY],
        out_specs=(SEM, SEM, *[HBM] * n, pl.BlockSpec(memory_space=pltpu.VMEM)),
        out_shape=(pltpu.SemaphoreType.DMA((3 * n,)), pltpu.SemaphoreType.DMA((3 * n,)),
                   *[pltpu.HBM(l.shape, l.dtype) for l in lands], jax.ShapeDtypeStruct((8, 128), F32)),
        input_output_aliases={a: 2 + a for a in range(n)},
        compiler_params=pltpu.CompilerParams(has_side_effects=IN_FLIGHT),
    )(*[_in_hbm(l) for l in lands], after)
    return outs[0], outs[1], list(outs[2:2 + n]), outs[-1]


def _gather_wait(send, recv, lands, after, *, name, halved=()):
    n = len(lands)

    def body(*refs):
        ins, send_ref, recv_ref = refs[:n], refs[n], refs[n + 1]
        place = _place()
        for a in range(n):
            part = _my_part(ins[a], place, a in halved)
            for p, flip in enumerate(CHIP_PEERS):
                k = 3 * a + p
                cp = _chip_copy(part, part, send_ref.at[k], recv_ref.at[k], flip, place)
                cp.wait_send()
                cp.wait_recv()

    after = list(after) if isinstance(after, (list, tuple)) else [after]
    return pl.pallas_call(
        body, name=name, in_specs=[HBM] * n + [SEM, SEM] + [ANY] * len(after), out_specs=[HBM] * n,
        out_shape=[pltpu.HBM(l.shape, l.dtype) for l in lands],
        input_output_aliases={a: a for a in range(n)},
        compiler_params=pltpu.CompilerParams(has_side_effects=IN_FLIGHT),
    )(*lands, send, recv, *after)


def _forward_to_sibling(land, *, name):
    rows = land.shape[1] // 2

    def body(land_ref, out_ref, send, recv):
        x, y, c = _place()
        copies = []
        for p, (fx, fy) in enumerate(CHIP_PEERS):
            chip = 2 * (x ^ fx) + (y ^ fy)
            mine = pl.ds(pl.multiple_of(c * rows, rows), rows)
            theirs = pl.ds(pl.multiple_of((1 - c) * rows, rows), rows)
            out = pltpu.make_async_remote_copy(
                src_ref=land_ref.at[chip].at[mine], dst_ref=out_ref.at[chip].at[mine], send_sem=send.at[p],
                recv_sem=recv.at[p], device_id=(x, y, 1 - c), device_id_type=MESH)
            out.start()
            copies.append((out, pltpu.make_async_remote_copy(
                src_ref=land_ref.at[chip].at[theirs], dst_ref=out_ref.at[chip].at[theirs], send_sem=send.at[p],
                recv_sem=recv.at[p], device_id=(x, y, 1 - c), device_id_type=MESH)))
        for out, arriving in copies:
            out.wait_send()
            arriving.wait_recv()

    return pl.pallas_call(
        body, name=name, in_specs=[ANY], out_specs=ANY, out_shape=jax.ShapeDtypeStruct(land.shape, land.dtype),
        input_output_aliases={0: 0},
        scratch_shapes=[pltpu.SemaphoreType.DMA((3,)), pltpu.SemaphoreType.DMA((3,))],
    )(land)


def _scatter_start(grad, *, name):
    def body(g_ref, land_ref, send, recv, g_thru, land_thru, token):
        place = _place()
        for p, flip in enumerate(CHIP_PEERS):
            peer_chip = 2 * (place[0] ^ flip[0]) + (place[1] ^ flip[1])
            _chip_copy(g_ref.at[peer_chip], land_ref.at[p], send.at[p], recv.at[p], flip, place).start()
        token[...] = jnp.zeros_like(token)

    land = lax.empty((3,) + grad.shape[1:], grad.dtype)
    return pl.pallas_call(
        body, name=name, in_specs=[HBM, HBM],
        out_specs=(SEM, SEM, HBM, HBM, pl.BlockSpec(memory_space=pltpu.VMEM)),
        out_shape=(pltpu.SemaphoreType.DMA((3,)), pltpu.SemaphoreType.DMA((3,)), pltpu.HBM(grad.shape, grad.dtype),
                   pltpu.HBM(land.shape, land.dtype), jax.ShapeDtypeStruct((8, 128), F32)),
        input_output_aliases={0: 2, 1: 3},
        compiler_params=pltpu.CompilerParams(has_side_effects=IN_FLIGHT),
    )(_in_hbm(grad), _in_hbm(land))


def _scatter_wait(started, after, *, name):
    n = len(started)

    def body(*refs):
        grads, lands = refs[:n], refs[n:2 * n]
        sends, recvs = refs[2 * n:3 * n], refs[3 * n:4 * n]
        place = _place()
        for a in range(n):
            for p, flip in enumerate(CHIP_PEERS):
                cp = _chip_copy(grads[a].at[0], lands[a].at[p], sends[a].at[p], recvs[a].at[p], flip, place)
                cp.wait_send()
                cp.wait_recv()

    grads, lands = [s[2] for s in started], [s[3] for s in started]
    after = list(after) if isinstance(after, (list, tuple)) else [after]
    outs = pl.pallas_call(
        body, name=name, in_specs=[HBM] * (2 * n) + [SEM] * (2 * n) + [ANY] * len(after), out_specs=[HBM] * (2 * n),
        out_shape=[pltpu.HBM(a.shape, a.dtype) for a in grads + lands],
        input_output_aliases={a: a for a in range(2 * n)},
        compiler_params=pltpu.CompilerParams(has_side_effects=IN_FLIGHT),
    )(*grads, *lands, *[s[0] for s in started], *[s[1] for s in started], *after)
    return list(zip(outs[:n], outs[n:]))


def _sibling_copy(src, dst, send, recv, place):
    x, y, c = place
    return pltpu.make_async_remote_copy(src_ref=src, dst_ref=dst, send_sem=send, recv_sem=recv,
                                        device_id=(x, y, 1 - c), device_id_type=MESH)


def _swap_start(arrays, *, name):
    n = len(arrays)

    def body(*refs):
        ins, lands, send, recv, token = refs[:n], refs[n:2 * n], refs[2 * n], refs[2 * n + 1], refs[-1]
        place = _place()
        for a in range(n):
            _sibling_copy(ins[a], lands[a], send.at[a], recv.at[a], place).start()
        token[...] = jnp.zeros_like(token)

    both = [_in_hbm(a) for a in arrays] + [_in_hbm(lax.empty(a.shape, a.dtype)) for a in arrays]
    outs = pl.pallas_call(
        body, name=name, in_specs=[HBM] * (2 * n),
        out_specs=(SEM, SEM, *[HBM] * (2 * n), pl.BlockSpec(memory_space=pltpu.VMEM)),
        out_shape=(pltpu.SemaphoreType.DMA((n,)), pltpu.SemaphoreType.DMA((n,)),
                   *[pltpu.HBM(a.shape, a.dtype) for a in both], jax.ShapeDtypeStruct((8, 128), F32)),
        input_output_aliases={a: 2 + a for a in range(2 * n)},
        compiler_params=pltpu.CompilerParams(has_side_effects=IN_FLIGHT),
    )(*both)
    return outs[0], outs[1], list(outs[2:2 + n]), list(outs[2 + n:2 + 2 * n]), outs[-1]


def _swap_wait(started, after, *, name):
    send, recv, arrays, lands = started[:4]
    n = len(arrays)

    def body(*refs):
        ins, zones, send_ref, recv_ref = refs[:n], refs[n:2 * n], refs[2 * n], refs[2 * n + 1]
        place = _place()
        for a in range(n):
            cp = _sibling_copy(ins[a], zones[a], send_ref.at[a], recv_ref.at[a], place)
            cp.wait_send()
            cp.wait_recv()

    after = list(after) if isinstance(after, (list, tuple)) else [after]
    outs = pl.pallas_call(
        body, name=name, in_specs=[HBM] * (2 * n) + [SEM, SEM] + [ANY] * len(after), out_specs=[HBM] * (2 * n),
        out_shape=[pltpu.HBM(a.shape, a.dtype) for a in arrays + lands],
        input_output_aliases={a: a for a in range(2 * n)},
        compiler_params=pltpu.CompilerParams(has_side_effects=IN_FLIGHT),
    )(*arrays, *lands, send, recv, *after)
    return list(outs[:n]), list(outs[n:])


def _allreduce_start(packed, *, name):
    n_dev = 8

    def body(src_ref, land_ref, send, recv, src_thru, land_thru, token):
        x, y, c = _place()
        me = 4 * x + 2 * y + c
        for p in range(1, n_dev):
            pltpu.make_async_remote_copy(
                src_ref=src_ref, dst_ref=land_ref.at[me], send_sem=send.at[p - 1], recv_sem=recv.at[p - 1],
                device_id=(x ^ (p >> 2), y ^ ((p >> 1) & 1), c ^ (p & 1)), device_id_type=MESH).start()
        token[...] = jnp.zeros_like(token)

    land = lax.empty((n_dev,) + packed.shape, packed.dtype)
    return pl.pallas_call(
        body, name=name, in_specs=[HBM, HBM],
        out_specs=(SEM, SEM, HBM, HBM, pl.BlockSpec(memory_space=pltpu.VMEM)),
        out_shape=(pltpu.SemaphoreType.DMA((n_dev - 1,)), pltpu.SemaphoreType.DMA((n_dev - 1,)),
                   pltpu.HBM(packed.shape, packed.dtype), pltpu.HBM(land.shape, land.dtype),
                   jax.ShapeDtypeStruct((8, 128), F32)),
        input_output_aliases={0: 2, 1: 3},
        compiler_params=pltpu.CompilerParams(has_side_effects=IN_FLIGHT),
    )(_in_hbm(packed), _in_hbm(land))


def _allreduce_wait(started, after, *, name):
    send, recv, packed, land = started[:4]
    n_dev = 8

    def body(src_ref, land_ref, send_ref, recv_ref, *_):
        x, y, c = _place()
        for p in range(1, n_dev):
            cp = pltpu.make_async_remote_copy(
                src_ref=src_ref, dst_ref=land_ref.at[0], send_sem=send_ref.at[p - 1], recv_sem=recv_ref.at[p - 1],
                device_id=(x ^ (p >> 2), y ^ ((p >> 1) & 1), c ^ (p & 1)), device_id_type=MESH)
            cp.wait_send()
            cp.wait_recv()

    after = list(after) if isinstance(after, (list, tuple)) else [after]
    return pl.pallas_call(
        body, name=name, in_specs=[HBM, HBM, SEM, SEM] + [ANY] * len(after), out_specs=[HBM, HBM],
        out_shape=[pltpu.HBM(packed.shape, packed.dtype), pltpu.HBM(land.shape, land.dtype)],
        input_output_aliases={0: 0, 1: 1},
        compiler_params=pltpu.CompilerParams(has_side_effects=IN_FLIGHT),
    )(packed, land, send, recv, *after)


def _sum_devices(mine, land, *, name):
    n_dev = land.shape[0]

    def body(mine_ref, land_ref, out_ref):
        x, y, c = _place()
        me = 4 * x + 2 * y + c
        total = None
        for s in range(n_dev):
            part = jnp.where(me == s, mine_ref[...], land_ref[s])
            total = part if total is None else total + part
        out_ref[...] = total

    return pl.pallas_call(body, name=name, out_shape=jax.ShapeDtypeStruct(mine.shape, mine.dtype))(mine, land)


def _sum_received(grad, land, *, name, tr=256):
    _, r, c = grad.shape
    tr = _pick(r, tr, 8)

    def body(chip_ref, g_ref, l_ref, o_ref):
        o_ref[...] = ((g_ref[...] + l_ref[0].astype(F32)) + l_ref[1].astype(F32)) + l_ref[2].astype(F32)

    chip = (2 * lax.axis_index("x") + lax.axis_index("y")).astype(jnp.int32).reshape(1)
    return pl.pallas_call(
        body, name=name,
        grid_spec=pltpu.PrefetchScalarGridSpec(
            num_scalar_prefetch=1, grid=(r // tr,),
            in_specs=[pl.BlockSpec((None, tr, c), lambda i, chip_ref: (chip_ref[0], i, 0)),
                      pl.BlockSpec((3, tr, c), lambda i, chip_ref: (0, i, 0))],
            out_specs=pl.BlockSpec((tr, c), lambda i, chip_ref: (i, 0))),
        out_shape=jax.ShapeDtypeStruct((r, c), F32), compiler_params=_params("parallel"),
    )(chip, grad, land)


def _allreduce_small(packed, *, name, after=None):
    r, d = packed.shape
    n_dev = 8

    def body(src_ref, out_ref, buf, send, recv):
        x, y, c = _place()
        me = 4 * x + 2 * y + c
        started = []
        for p in range(1, n_dev):
            rc = pltpu.make_async_remote_copy(
                src_ref=src_ref, dst_ref=buf.at[me], send_sem=send.at[p - 1], recv_sem=recv.at[p - 1],
                device_id=(x ^ (p >> 2), y ^ ((p >> 1) & 1), c ^ (p & 1)), device_id_type=MESH)
            rc.start()
            started.append(rc)
        buf[me] = src_ref[...]
        for rc in started:
            rc.wait()
        total = buf[0]
        for s in range(1, n_dev):
            total = total + buf[s]
        out_ref[...] = total

    vmem = pl.BlockSpec(memory_space=pltpu.VMEM)
    body, more_specs, more_args = _ordered(body, 1, after)
    return pl.pallas_call(
        body, name=name, in_specs=[vmem] + more_specs, out_specs=vmem, out_shape=jax.ShapeDtypeStruct((r, d), F32),
        scratch_shapes=[pltpu.VMEM((n_dev, r, d), F32), pltpu.SemaphoreType.DMA((n_dev - 1,)),
                        pltpu.SemaphoreType.DMA((n_dev - 1,))],
    )(packed, *more_args)


def _packed_rows(size, d):
    return -(-size // (8 * d)) * 8


def _pack_rows(arrays, d):
    rows = []
    for arr in arrays:
        flat = arr.reshape(-1).astype(F32)
        n = _packed_rows(flat.shape[0], d)
        rows.append(jnp.pad(flat, (0, n * d - flat.shape[0])).reshape(n, d))
    return jnp.concatenate(rows, axis=0)


def _unpack_rows(packed, shapes, d):
    out, row = [], 0
    for shape in shapes:
        size = math.prod(shape)
        n = _packed_rows(size, d)
        out.append(packed[row:row + n].reshape(-1)[:size].reshape(shape))
        row += n
    return out


SMALL = ("norm1_g", "gate_b", "conv_b", "conv_norm_g", "q_norm_g", "k_norm_g", "norm2_g", "ffn_conv_b")
LARGE = ("w_in", "w_conv_out", "w_attn_out", "w_out", "w_up", "w_down")
WEIGHTS = ("norm1_g", "w_in", "gate_b", "conv_w", "conv_b", "conv_norm_g", "w_conv_out", "q_norm_g", "k_norm_g",
           "w_attn_out", "w_out", "norm2_g", "w_up", "ffn_conv_w", "ffn_conv_b", "w_down")


def _head_ones(dims):
    a = dims.n_heads * dims.head_dim
    head = jnp.arange(a, dtype=jnp.int32) // dims.head_dim
    return (head[:, None] == head[None, :]).astype(BF16)


def _after(vec, token):
    return vec if token is None else vec + token[0:1, 0:1]


def _local_step(dims, x, target, small, first_weights, other_weights, send_grad):
    d, f, heads = dims.d_model, dims.d_ff, dims.n_heads
    small = dict(small)
    row = lambda name: small[name].reshape(1, -1)
    head_sum = _head_sum_matrix(dims)
    head_spread = jnp.transpose(head_sum)
    ones = (head_sum, head_spread)
    gq = jnp.tile(row("q_norm_g"), (1, heads))
    gk = jnp.tile(row("k_norm_g"), (1, heads))
    one_shard = lambda w: w.reshape(1, -1, w.shape[-1])

    h = _rmsnorm_fwd(x, row("norm1_g"), name="norm1")
    full = first_weights(h)
    w_in = full["w_in"]
    conv_w = jnp.pad(full["conv_w"], ((0, CONV_HALO - dims.conv_width), (0, 0)))
    ffn_w = jnp.pad(full["ffn_conv_w"], ((0, FFN_HALO - dims.ffn_conv_width), (0, 0)))
    z = _mm_nn(h, w_in, out_dtype=BF16, after=full.get("token"), tm=2048, tn=1792, name="in_proj")
    a1, a3 = _conv_branch_fwd(z, conv_w, row("conv_b"), row("conv_norm_g"), dims, name="conv_branch")
    qkv = _qkv_layouts_fwd(z, gq, gk, ones, dims, name="qk_norm")
    per_group = {dil: _attn_fwd(*qkv[dil], dims, dil, name=f"attn_fwd_d{dil}") for dil in DILATIONS}
    o, lse = _attn_combine(per_group, head_spread, dims, name="attn_combine")
    full = other_weights(o)
    w_up = full["w_up"]
    w_co, w_ao, w_o, w_dn = (one_shard(full[k]) for k in ("w_conv_out", "w_attn_out", "w_out", "w_down"))
    ya = _mm_nn(a3, w_co, out_dtype=F32, name="conv_out_proj")
    yb = _mm_nn(o, w_ao, out_dtype=F32, name="attn_out_proj")
    mixed = _mix_fwd(ya, yb, z, row("gate_b"), dims, name="gate_mix")
    x1, h2 = _proj_residual_norm(mixed, w_o, x, row("norm2_g"), name="out_proj_norm2")
    up = _mm_nn(h2, w_up, out_dtype=F32, tm=2048, name="up_proj")
    act = _ffn_act_fwd(up, ffn_w, row("ffn_conv_b"), dims, name="ffn_act")
    dy, dy_b, loss = _proj_residual_loss(act, w_dn, x1, target, tm=512, name="down_proj_loss")

    grads = {}

    def large(name, g):
        grads[name], g_bf16 = g
        return send_grad(name, g_bf16)

    sent = large("w_down", _mm_tn(act, dy_b, n_shards=1, name="dw_down"))
    dact = _mm_nt(dy_b, w_dn, out_dtype=BF16, after=sent, name="d_act")
    dup, dfw, dfb = _ffn_bwd(dact, up, ffn_w, row("ffn_conv_b"), dims, name="ffn_bwd")
    grads["ffn_conv_w"], grads["ffn_conv_b"] = dfw[:dims.ffn_conv_width], dfb
    sent = large("w_up", _mm_tn(h2, dup, n_shards=N_CHIPS, name="dw_up"))
    dh2 = _mm_nt(dup, w_up, out_dtype=F32, after=sent, name="d_h2")
    dx1, dx1_b, grads["norm2_g"] = _rmsnorm_bwd(x1, row("norm2_g"), dh2, dy, want_bf16=True, name="norm2_bwd")
    sent = large("w_out", _mm_tn(mixed, dx1_b, n_shards=1, name="dw_out"))
    dmix = _mm_nt(dx1_b, w_o, out_dtype=F32, after=sent, name="d_mix")
    dya, dyb, dz_gate, grads["gate_b"] = _mix_bwd(dmix, ya, yb, z, row("gate_b"), dims, name="gate_mix_bwd")
    sent = large("w_attn_out", _mm_tn(o, dyb, n_shards=1, name="dw_attn_out"))
    do = _mm_nt(dyb, w_ao, out_dtype=BF16, after=sent, name="d_attn")
    dos, deltas = _attn_bwd_prep(do, o, head_sum, dims, name="attn_bwd_prep")
    dqkv = {dil: _attn_bwd_keys_major(*qkv[dil], dos[dil], lse[dil], deltas[dil], dims, dil, name=f"attn_bwd_d{dil}")
            for dil in DILATIONS}
    dz_qkv, dgq, dgk = _qkv_layouts_bwd(z, dqkv, gq, gk, ones, dims, name="qk_norm_bwd")
    grads["q_norm_g"] = dgq.reshape(heads, dims.head_dim).sum(axis=0)
    grads["k_norm_g"] = dgk.reshape(heads, dims.head_dim).sum(axis=0)
    sent = large("w_conv_out", _mm_tn(a3, dya, n_shards=1, name="dw_conv_out"))
    da3 = _mm_nt(dya, w_co, out_dtype=F32, after=sent, name="d_conv_act")
    da1, grads["conv_norm_g"] = _conv_norm_bwd(da3, a1, row("conv_norm_g"), name="conv_norm_bwd")
    dz, dcw, grads["conv_b"] = _conv_branch_bwd(da1, z, conv_w, [dz_qkv, dz_gate], dims, name="conv_branch_bwd")
    grads["conv_w"] = dcw[:dims.conv_width]
    sent = large("w_in", _mm_tn(h, dz, n_shards=N_CHIPS, name="dw_in"))
    dh = _mm_nt(dz, w_in, out_dtype=F32, after=sent, name="d_h")
    dx, grads["norm1_g"] = _rmsnorm_bwd(x, row("norm1_g"), dh, dx1, want_bf16=False, name="norm1_bwd")
    return loss, dx, grads


def _step(dims, x, target, w, m, v):
    d = dims.d_model
    t = dims.tokens
    sq = lambda a: a.reshape(a.shape[1:])
    w2, m2, v2 = ({k: sq(a) for k, a in grp.items()} for grp in (w, m, v))

    conv_pad = jnp.pad(w2["conv_w"], ((0, CONV_HALO - dims.conv_width), (0, 0)))
    ffn_pad = jnp.pad(w2["ffn_conv_w"], ((0, FFN_HALO - dims.ffn_conv_width), (0, 0)))
    first_names = ("w_in", "conv_w", "ffn_conv_w")
    other_names = tuple(k for k in LARGE if k not in first_names)
    lands = dict(zip(first_names, _cast_to_lands([w2["w_in"], conv_pad, ffn_pad], [BF16, F32, F32], name="cast_first")))
    first = _gather_start([lands[k] for k in first_names], x, halved=(0,), name="gather_start_first")
    lands.update(zip(other_names, _cast_to_lands([w2[k] for k in other_names], [BF16] * len(other_names),
                                                 after=first[3], name="cast_other")))
    other = []
    cols = lambda g, rows: jnp.moveaxis(g, 0, 1).reshape(g.shape[1], -1)[:rows]

    def first_weights(after):
        got = dict(zip(first_names, _gather_wait(*first[:3], [after] + [lands[k] for k in other_names], halved=(0,),
                                                 name="gather_wait_first")))
        got["w_in"] = _forward_to_sibling(got["w_in"], name="forward_w_in")
        other.extend(_gather_start([lands[k] for k in other_names], got["w_in"], name="gather_start_other"))
        got["conv_w"] = cols(got["conv_w"], dims.conv_width)
        got["ffn_conv_w"] = cols(got["ffn_conv_w"], dims.ffn_conv_width)
        got["token"] = other[3]
        return got

    def other_weights(after):
        return dict(zip(other_names, _gather_wait(*other[:3], after, name="gather_wait_other")))

    started = {}

    def send_grad(name, g):
        send, recv, g_thru, land, token = _scatter_start(g.reshape(N_CHIPS, -1, g.shape[-1]), name=f"scatter_start_{name}")
        started[name] = (send, recv, g_thru, land)
        return token

    small = {k: w2[k] for k in SMALL}
    small["norm1_g"] = _after(small["norm1_g"].reshape(1, -1), first[3])
    loss, dx, grads = _local_step(dims, x.reshape(t, d), target.reshape(t, d), small, first_weights, other_weights, send_grad)

    def my_sums(names, after, tag):
        arrived = _scatter_wait([started[k] for k in names], after, name=f"scatter_wait_{tag}")
        blocks = [grads[k].reshape(N_CHIPS, -1, grads[k].shape[-1]) for k in names]
        return [_sum_received(g, land, name=f"sum_{k}") for k, g, (_, land) in zip(names, blocks, arrived)]

    def updates(names, mine, theirs):
        return {k: _adamw(w2[k], [a, b], m2[k], v2[k], name=f"adamw_{k}") for k, a, b in zip(names, mine, theirs)}

    small_names = SMALL + ("conv_w", "ffn_conv_w")
    packed = _pack_rows([grads[k] for k in small_names] + [loss[0, 0]], d)
    reducing = _allreduce_start(packed, name="allreduce_start")
    others = [k for k in LARGE if k != "w_in"]
    mine_others = my_sums(others, [dx, reducing[4]], "others")
    swapping_others = _swap_start(mine_others, name="swap_start_others")
    mine_w_in = my_sums(["w_in"], swapping_others[4], "w_in")
    swapping_w_in = _swap_start(mine_w_in, name="swap_start_w_in")
    out = updates(others, *_swap_wait(swapping_others, swapping_w_in[4], name="swap_wait_others"))
    last_updates = [out[k][1] for k in others]
    reduced = _sum_devices(*_allreduce_wait(reducing, last_updates, name="allreduce_wait"), name="allreduce_sum")
    shapes = [grads[k].shape for k in small_names] + [()]
    *small_g, loss_total = _unpack_rows(reduced, shapes, d)
    small_g = dict(zip(small_names, small_g))
    chip = 2 * lax.axis_index("x") + lax.axis_index("y")
    for k in ("conv_w", "ffn_conv_w"):
        width = w2[k].shape[1]
        small_g[k] = lax.dynamic_slice_in_dim(small_g[k], chip * width, width, axis=1)

    small_shapes = [w2[k].shape for k in small_names]
    pack = lambda grp: _pack_rows([grp[k] for k in small_names], d)
    results = _adamw(pack(w2), [pack(small_g)], pack(m2), pack(v2), name="adamw_small")
    unpacked = [_unpack_rows(r, small_shapes, d) for r in results]
    out.update({k: tuple(u[i] for u in unpacked) for i, k in enumerate(small_names)})
    out.update(updates(["w_in"], *_swap_wait(swapping_w_in, results[1], name="swap_wait_w_in")))

    lead =lambda a: a.reshape((1,) + a.shape)
    ordered = [[lead(out[k][j].reshape(w2[k].shape)) for k in WEIGHTS] for j in range(4)]
    return (loss_total, dx.reshape(x.shape), *ordered[0], *ordered[1], *ordered[2], *ordered[3])


def kernel(x, norm1_g, w_in, gate_b, conv_w, conv_b, conv_norm_g, w_conv_out, q_norm_g, k_norm_g, w_attn_out, w_out, norm2_g, w_up, ffn_conv_w, ffn_conv_b, w_down, loss_target, m_norm1_g, m_w_in, m_gate_b, m_conv_w, m_conv_b, m_conv_norm_g, m_w_conv_out, m_q_norm_g, m_k_norm_g, m_w_attn_out, m_w_out, m_norm2_g, m_w_up, m_ffn_conv_w, m_ffn_conv_b, m_w_down, v_norm1_g, v_w_in, v_gate_b, v_conv_w, v_conv_b, v_conv_norm_g, v_w_conv_out, v_q_norm_g, v_k_norm_g, v_w_attn_out, v_w_out, v_norm2_g, v_w_up, v_ffn_conv_w, v_ffn_conv_b, v_w_down):
    w = dict(zip(WEIGHTS, (norm1_g, w_in, gate_b, conv_w, conv_b, conv_norm_g, w_conv_out, q_norm_g, k_norm_g,
                           w_attn_out, w_out, norm2_g, w_up, ffn_conv_w, ffn_conv_b, w_down)))
    m = dict(zip(WEIGHTS, (m_norm1_g, m_w_in, m_gate_b, m_conv_w, m_conv_b, m_conv_norm_g, m_w_conv_out, m_q_norm_g,
                           m_k_norm_g, m_w_attn_out, m_w_out, m_norm2_g, m_w_up, m_ffn_conv_w, m_ffn_conv_b, m_w_down)))
    v = dict(zip(WEIGHTS, (v_norm1_g, v_w_in, v_gate_b, v_conv_w, v_conv_b, v_conv_norm_g, v_w_conv_out, v_q_norm_g,
                           v_k_norm_g, v_w_attn_out, v_w_out, v_norm2_g, v_w_up, v_ffn_conv_w, v_ffn_conv_b, v_w_down)))
    dims = Dims(d_model=x.shape[-1], batch_local=x.shape[0], seq=x.shape[1], d_ff=w_down.shape[1] * N_CHIPS)
    return _step(dims, x, loss_target, w, m, v)
```

```python
import functools
import math
from typing import NamedTuple

import jax
import jax.numpy as jnp
from jax import lax
from jax.experimental import pallas as pl
from jax.experimental.pallas import tpu as pltpu

F32 = jnp.float32
BF16 = jnp.bfloat16

RMS_EPS = 1e-6
ATTN_BLOCK = 128
DILATIONS = (1, 4, 16)
CONV_HALO = 32
FFN_HALO = 8
ADAM_LR, ADAM_B1, ADAM_B2, ADAM_EPS, ADAM_WD, ADAM_STEP = 0.001, 0.9, 0.999, 1e-08, 0.01, 10
V7X_VMEM_LIMIT_BYTES = 56 * 2 ** 20
N_CHIPS = 4
MESH = pl.DeviceIdType.MESH


class Dims(NamedTuple):
    d_model: int = 1024
    n_heads: int = 16
    head_dim: int = 64
    d_ff: int = 2816
    seq: int = 2048
    batch_local: int = 2
    conv_width: int = 31
    ffn_conv_width: int = 3

    @property
    def tokens(self):
        return self.seq * self.batch_local


def _params(*semantics):
    return pltpu.CompilerParams(dimension_semantics=semantics, vmem_limit_bytes=V7X_VMEM_LIMIT_BYTES)


ANY = pl.BlockSpec(memory_space=pl.ANY)


def _ordered(body, n_inputs, after):
    after = [] if after is None else list(after) if isinstance(after, (list, tuple)) else [after]
    if not after:
        return body, [], []

    def wrapped(*refs):
        return body(*refs[:n_inputs], *refs[n_inputs + len(after):])

    return wrapped, [ANY] * len(after), after


def _pick(n, target, mult=128):
    if n <= target:
        return n
    best = None
    for t in range(mult, target + 1, mult):
        if n % t == 0:
            best = t
    assert best is not None, (n, target, mult)
    return best


def _sigmoid(v):
    return 1.0 / (1.0 + jnp.exp(-v))


def _mm_nn(a, w, *, out_dtype, name, residual=None, after=None, tm=1024, tn=1408, tk=2816):
    m, k = a.shape
    nsh, k2, c = w.shape
    assert k == k2 and a.dtype == BF16 and w.dtype == BF16
    n = nsh * c
    tm, tn, tk = _pick(m, tm, 8), _pick(c, tn), _pick(k, tk)
    nk, cpn = k // tk, c // tn

    def body(*refs):
        if residual is None:
            a_ref, w_ref, o_ref, acc = refs
        else:
            a_ref, w_ref, r_ref, o_ref, acc = refs
        prod = jnp.dot(a_ref[...], w_ref[...], preferred_element_type=F32)

        def finish(total):
            if residual is not None:
                total = total + r_ref[...]
            o_ref[...] = total.astype(out_dtype)

        if nk == 1:
            finish(prod)
        else:
            kk = pl.program_id(2)

            @pl.when(kk == 0)
            def _():
                acc[...] = prod

            @pl.when(kk > 0)
            def _():
                acc[...] += prod

            @pl.when(kk == nk - 1)
            def _():
                finish(acc[...])

    in_specs = [pl.BlockSpec((tm, tk), lambda i, j, kk: (i, kk)),
                pl.BlockSpec((None, tk, tn), lambda i, j, kk: (j // cpn, kk, j % cpn))]
    args = [a, w]
    if residual is not None:
        in_specs.append(pl.BlockSpec((tm, tn), lambda i, j, kk: (i, j)))
        args.append(residual)
    body, more_specs, more_args = _ordered(body, len(args), after)
    return pl.pallas_call(
        body, name=name, grid=(m // tm, n // tn, nk),
        in_specs=in_specs + more_specs, out_specs=pl.BlockSpec((tm, tn), lambda i, j, kk: (i, j)),
        out_shape=jax.ShapeDtypeStruct((m, n), out_dtype),
        scratch_shapes=[pltpu.VMEM((tm, tn) if nk > 1 else (8, 128), F32)],
        compiler_params=_params("parallel", "parallel", "arbitrary"),
    )(*args, *more_args)


def _proj_residual_norm(a, w, residual, g, *, name, tm=1024):
    m, k = a.shape
    _, k2, n = w.shape
    assert w.shape[0] == 1 and k == k2 and a.dtype == BF16 and w.dtype == BF16
    tm = _pick(m, tm, 8)

    def body(a_ref, w_ref, r_ref, g_ref, y_ref, h_ref):
        y = r_ref[...] + jnp.dot(a_ref[...], w_ref[...], preferred_element_type=F32)
        y_ref[...] = y
        h_ref[...] = (y * lax.rsqrt(jnp.mean(y * y, axis=-1, keepdims=True) + RMS_EPS) * g_ref[...]).astype(BF16)

    rows = lambda width: pl.BlockSpec((tm, width), lambda i: (i, 0))
    return pl.pallas_call(
        body, name=name, grid=(m // tm,),
        in_specs=[rows(k), pl.BlockSpec((None, k, n), lambda i: (0, 0, 0)), rows(n), pl.BlockSpec((1, n), lambda i: (0, 0))],
        out_specs=[rows(n), rows(n)],
        out_shape=[jax.ShapeDtypeStruct((m, n), F32), jax.ShapeDtypeStruct((m, n), BF16)],
        compiler_params=_params("parallel"),
    )(a, w, residual, g)


def _proj_residual_loss(a, w, residual, target, *, name, tm=1024):
    m, k = a.shape
    _, k2, n = w.shape
    assert w.shape[0] == 1 and k == k2 and a.dtype == BF16 and w.dtype == BF16
    tm = _pick(m, tm, 8)

    def body(a_ref, w_ref, r_ref, t_ref, dy_ref, dyb_ref, loss_ref):
        err = r_ref[...] + jnp.dot(a_ref[...], w_ref[...], preferred_element_type=F32) - t_ref[...]
        dy = err * (1.0 / n)
        dy_ref[...] = dy
        dyb_ref[...] = dy.astype(BF16)
        part = jnp.sum(jnp.sum(err * err, axis=-1, keepdims=True), axis=0, keepdims=True) * (0.5 / n)
        _accumulate(loss_ref, jnp.broadcast_to(part, (8, 128)), pl.program_id(0) == 0)

    rows = lambda width: pl.BlockSpec((tm, width), lambda i: (i, 0))
    return pl.pallas_call(
        body, name=name, grid=(m // tm,),
        in_specs=[rows(k), pl.BlockSpec((None, k, n), lambda i: (0, 0, 0)), rows(n), rows(n)],
        out_specs=[rows(n), rows(n), pl.BlockSpec((8, 128), lambda i: (0, 0))],
        out_shape=[jax.ShapeDtypeStruct((m, n), F32), jax.ShapeDtypeStruct((m, n), BF16),
                   jax.ShapeDtypeStruct((8, 128), F32)],
        compiler_params=_params("arbitrary"),
    )(a, w, residual, target)


def _mm_nt(a, w, *, out_dtype, name, after=None, tm=1024, tn=1408, tk=1792):
    m, k = a.shape
    nsh, r, c = w.shape
    assert k == nsh * c and a.dtype == BF16 and w.dtype == BF16
    tm, tn, tk = _pick(m, tm, 8), _pick(r, tn), _pick(c, tk)
    nk, cpk = k // tk, c // tk

    def body(a_ref, w_ref, o_ref, acc):
        prod = lax.dot_general(a_ref[...], w_ref[...], (((1,), (1,)), ((), ())), preferred_element_type=F32)
        if nk == 1:
            o_ref[...] = prod.astype(out_dtype)
        else:
            kk = pl.program_id(2)

            @pl.when(kk == 0)
            def _():
                acc[...] = prod

            @pl.when(kk > 0)
            def _():
                acc[...] += prod

            @pl.when(kk == nk - 1)
            def _():
                o_ref[...] = acc[...].astype(out_dtype)

    body, more_specs, more_args = _ordered(body, 2, after)
    return pl.pallas_call(
        body, name=name, grid=(m // tm, r // tn, nk),
        in_specs=[pl.BlockSpec((tm, tk), lambda i, j, kk: (i, kk)),
                  pl.BlockSpec((None, tn, tk), lambda i, j, kk: (kk // cpk, j, kk % cpk))] + more_specs,
        out_specs=pl.BlockSpec((tm, tn), lambda i, j, kk: (i, j)),
        out_shape=jax.ShapeDtypeStruct((m, r), out_dtype),
        scratch_shapes=[pltpu.VMEM((tm, tn) if nk > 1 else (8, 128), F32)],
        compiler_params=_params("parallel", "parallel", "arbitrary"),
    )(a, w, *more_args)


MM_TN_VMEM_BYTES = 44 * 2 ** 20


def _mm_tn(a, b, *, n_shards, name, tm=1408, tn=1408):
    t, m = a.shape
    t2, n = b.shape
    assert t == t2 and a.dtype == BF16 and b.dtype == BF16
    c = n // n_shards
    tm, tn = _pick(m, tm), _pick(c, tn)
    if m // tm == 1 and n // tn == 1 and tn % (2 * LANES) == 0:
        tn //= 2
    fixed = 2 * tm * tn * 6
    if 4 * t * (tm + tn) + fixed <= MM_TN_VMEM_BYTES:
        tk = t
    else:
        tk = _pick(t, (MM_TN_VMEM_BYTES - fixed - 4 * tm * tn) // (4 * (tm + tn)), 8)
    nk, cpn = t // tk, c // tn

    def body(a_ref, b_ref, o_ref, ob_ref, acc):
        kk = pl.program_id(2)
        prod = lax.dot_general(a_ref[...], b_ref[...], (((0,), (0,)), ((), ())), preferred_element_type=F32)

        def finish(total):
            o_ref[...] = total
            ob_ref[...] = total.astype(BF16)

        if nk == 1:
            finish(prod)
        else:
            @pl.when(kk == 0)
            def _():
                acc[...] = prod

            @pl.when(kk > 0)
            def _():
                acc[...] += prod

            @pl.when(kk == nk - 1)
            def _():
                finish(acc[...])

    out_spec = pl.BlockSpec((None, tm, tn), lambda i, j, kk: (j // cpn, i, j % cpn))
    return pl.pallas_call(
        body, name=name, grid=(m // tm, n // tn, nk),
        in_specs=[pl.BlockSpec((tk, tm), lambda i, j, kk: (kk, i)),
                  pl.BlockSpec((tk, tn), lambda i, j, kk: (kk, j))],
        out_specs=[out_spec, out_spec],
        out_shape=[jax.ShapeDtypeStruct((n_shards, m, c), F32), jax.ShapeDtypeStruct((n_shards, m, c), BF16)],
        scratch_shapes=[pltpu.VMEM((tm, tn) if nk > 1 else (8, 128), F32)],
        compiler_params=_params("parallel", "parallel", "arbitrary"),
    )(a, b)


def _row_spec(tr, width, col=0):
    return pl.BlockSpec((tr, width), lambda i, col=col: (i, col))


def _vec_spec(width, col=0):
    return pl.BlockSpec((1, width), lambda i, col=col: (0, col))


def _accumulate(ref, value, first):
    @pl.when(first)
    def _():
        ref[...] = value

    @pl.when(jnp.logical_not(first))
    def _():
        ref[...] += value


def _rmsnorm_fwd(x, g, *, name, tr=512):
    t, d = x.shape
    tr = _pick(t, tr, 8)

    def body(x_ref, g_ref, o_ref):
        xv = x_ref[...]
        r = lax.rsqrt(jnp.mean(xv * xv, axis=-1, keepdims=True) + RMS_EPS)
        o_ref[...] = (xv * r * g_ref[...]).astype(BF16)

    return pl.pallas_call(
        body, name=name, grid=(t // tr,),
        in_specs=[_row_spec(tr, d), _vec_spec(d)], out_specs=_row_spec(tr, d),
        out_shape=jax.ShapeDtypeStruct((t, d), BF16), compiler_params=_params("parallel"),
    )(x, g)


def _rmsnorm_bwd(x, g, dy, dres, *, name, want_bf16, tr=512):
    t, d = x.shape
    tr = _pick(t, tr, 8)

    def body(x_ref, g_ref, dy_ref, dres_ref, *outs):
        dx_ref, dg_ref = outs[0], outs[-1]
        xv, dyv = x_ref[...], dy_ref[...].astype(F32)
        r = lax.rsqrt(jnp.mean(xv * xv, axis=-1, keepdims=True) + RMS_EPS)
        gy = dyv * g_ref[...]
        dx = dres_ref[...] + r * gy - xv * (r * r * r) * jnp.mean(xv * gy, axis=-1, keepdims=True)
        dx_ref[...] = dx
        if want_bf16:
            outs[1][...] = dx.astype(BF16)
        _accumulate(dg_ref, jnp.sum(dyv * xv * r, axis=0, keepdims=True), pl.program_id(0) == 0)

    out_shape = [jax.ShapeDtypeStruct((t, d), F32)]
    out_specs = [_row_spec(tr, d)]
    if want_bf16:
        out_shape.append(jax.ShapeDtypeStruct((t, d), BF16))
        out_specs.append(_row_spec(tr, d))
    out_shape.append(jax.ShapeDtypeStruct((1, d), F32))
    out_specs.append(_vec_spec(d))
    return pl.pallas_call(
        body, name=name, grid=(t // tr,),
        in_specs=[_row_spec(tr, d), _vec_spec(d), _row_spec(tr, d), _row_spec(tr, d)],
        out_specs=out_specs, out_shape=out_shape, compiler_params=_params("arbitrary"),
    )(x, g, dy, dres)


CONV_ROWS = 16


def _seq_specs(dims, ts, width, halo, col, *, nxt=False):
    nst, per = dims.seq // ts, ts // halo
    last = dims.tokens // halo - 1
    cur = pl.BlockSpec((ts, width), lambda b, i: (b * nst + i, col))
    if nxt:
        edge = pl.BlockSpec((halo, width), lambda b, i: (jnp.minimum((b * nst + i + 1) * per, last), col))
    else:
        edge = pl.BlockSpec((halo, width), lambda b, i: (jnp.maximum((b * nst + i) * per - 1, 0), col))
    return cur, edge


SUBLANES = 8


def _shifted_copies(buf, shifted):
    rows = shifted.shape[1]
    for s in range(1, SUBLANES):
        shifted[s - 1] = buf[pl.ds(s, rows), :]


def _window(buf, shifted, start, size):
    a, s = divmod(start, SUBLANES)
    src = buf if s == 0 else shifted.at[s - 1]
    return src[pl.ds(SUBLANES * a, size), :]


def _conv_branch_fwd(z, w, b, g, dims, *, name, ts=128):
    t, c, kw = z.shape[0], dims.d_model, dims.conv_width
    base = CONV_HALO - (kw - 1)

    def body(av_ref, hv_ref, ag_ref, hg_ref, w_ref, b_ref, g_ref, a1_ref, a3_ref, buf, shifted):
        i = pl.program_id(1)
        buf[CONV_HALO:, :] = av_ref[...].astype(F32) * _sigmoid(ag_ref[...].astype(F32))
        buf[0:CONV_HALO, :] = jnp.where(i > 0, hv_ref[...].astype(F32) * _sigmoid(hg_ref[...].astype(F32)), 0.0)
        _shifted_copies(buf, shifted)
        for r0 in range(0, ts, CONV_ROWS):
            acc = jnp.broadcast_to(b_ref[...], (CONV_ROWS, c))
            for k in range(kw):
                acc = acc + w_ref[k:k + 1, :] * _window(buf, shifted, r0 + base + k, CONV_ROWS)
            a1_ref[r0:r0 + CONV_ROWS, :] = acc
            a2 = acc * lax.rsqrt(jnp.mean(acc * acc, axis=-1, keepdims=True) + RMS_EPS) * g_ref[...]
            a3_ref[r0:r0 + CONV_ROWS, :] = (a2 * _sigmoid(a2)).astype(BF16)

    vec = pl.BlockSpec((1, c), lambda b, i: (0, 0))
    out = pl.BlockSpec((ts, c), lambda b, i: (b * (dims.seq // ts) + i, 0))
    return pl.pallas_call(
        body, name=name, grid=(dims.batch_local, dims.seq // ts),
        in_specs=[*_seq_specs(dims, ts, c, CONV_HALO, 0), *_seq_specs(dims, ts, c, CONV_HALO, 1),
                  pl.BlockSpec((CONV_HALO, c), lambda b, i: (0, 0)), vec, vec],
        out_specs=[out, out],
        out_shape=[jax.ShapeDtypeStruct((t, c), F32), jax.ShapeDtypeStruct((t, c), BF16)],
        scratch_shapes=[pltpu.VMEM((CONV_HALO + ts, c), F32),
                        pltpu.VMEM((SUBLANES - 1, CONV_HALO + ts - SUBLANES, c), F32)],
        compiler_params=_params("parallel", "parallel"),
    )(z, z, z, z, w, b, g)


def _conv_norm_bwd(da3, a1, g, *, name, tr=256):
    t, c = a1.shape
    tr = _pick(t, tr, 8)

    def body(d_ref, a_ref, g_ref, o_ref, dg_ref):
        a1v, gv = a_ref[...], g_ref[...]
        r = lax.rsqrt(jnp.mean(a1v * a1v, axis=-1, keepdims=True) + RMS_EPS)
        a2 = a1v * r * gv
        sg = _sigmoid(a2)
        da2 = d_ref[...].astype(F32) * sg * (1.0 + a2 * (1.0 - sg))
        gy = da2 * gv
        o_ref[...] = r * gy - a1v * (r * r * r) * jnp.mean(a1v * gy, axis=-1, keepdims=True)
        _accumulate(dg_ref, jnp.sum(da2 * a1v * r, axis=0, keepdims=True), pl.program_id(0) == 0)

    return pl.pallas_call(
        body, name=name, grid=(t // tr,),
        in_specs=[_row_spec(tr, c), _row_spec(tr, c), _vec_spec(c)],
        out_specs=[_row_spec(tr, c), _vec_spec(c)],
        out_shape=[jax.ShapeDtypeStruct((t, c), F32), jax.ShapeDtypeStruct((1, c), F32)],
        compiler_params=_params("arbitrary"),
    )(da3, a1, g)


def _conv_branch_bwd(da1, z, w, rest_of_dz, dims, *, name, ts=128):
    t, c, kw = z.shape[0], dims.d_model, dims.conv_width
    nst = dims.seq // ts
    base = CONV_HALO - (kw - 1)
    n_rest = len(rest_of_dz)
    total = 2 * c + sum(r.shape[1] for r in rest_of_dz)

    def body(d_ref, dn_ref, av_ref, hv_ref, ag_ref, hg_ref, w_ref, *more):
        rest_refs = more[:n_rest]
        dz_ref, dw_ref, db_ref, abuf, dbuf, ashift, dshift = more[n_rest:]
        col = 2 * c
        for r in rest_refs:
            dz_ref[:, col:col + r.shape[1]] = r[...]
            col += r.shape[1]
        i = pl.program_id(1)
        first = jnp.logical_and(pl.program_id(0) == 0, i == 0)
        abuf[CONV_HALO:, :] = av_ref[...].astype(F32) * _sigmoid(ag_ref[...].astype(F32))
        abuf[0:CONV_HALO, :] = jnp.where(i > 0, hv_ref[...].astype(F32) * _sigmoid(hg_ref[...].astype(F32)), 0.0)
        d1 = d_ref[...]
        dbuf[0:ts, :] = d1
        dbuf[ts:, :] = jnp.where(i < nst - 1, dn_ref[...], 0.0)
        _shifted_copies(abuf, ashift)
        _shifted_copies(dbuf, dshift)

        @pl.when(first)
        def _():
            dw_ref[...] = jnp.zeros_like(dw_ref)
            db_ref[...] = jnp.zeros_like(db_ref)

        db_ref[...] += jnp.sum(d1, axis=0, keepdims=True)
        for k in range(kw):
            dw_ref[k:k + 1, :] += jnp.sum(d1 * _window(abuf, ashift, base + k, ts), axis=0, keepdims=True)
        for r0 in range(0, ts, CONV_ROWS):
            acc = jnp.zeros((CONV_ROWS, c), F32)
            for k in range(kw):
                acc = acc + w_ref[k:k + 1, :] * _window(dbuf, dshift, r0 + (kw - 1) - k, CONV_ROWS)
            av = av_ref[r0:r0 + CONV_ROWS, :].astype(F32)
            sg = _sigmoid(ag_ref[r0:r0 + CONV_ROWS, :].astype(F32))
            dz_ref[r0:r0 + CONV_ROWS, 0:c] = (acc * sg).astype(BF16)
            dz_ref[r0:r0 + CONV_ROWS, c:2 * c] = (acc * av * sg * (1.0 - sg)).astype(BF16)

    cur, nxt = _seq_specs(dims, ts, c, CONV_HALO, 0, nxt=True)
    return pl.pallas_call(
        body, name=name, grid=(dims.batch_local, nst),
        in_specs=[cur, nxt, *_seq_specs(dims, ts, c, CONV_HALO, 0), *_seq_specs(dims, ts, c, CONV_HALO, 1),
                  pl.BlockSpec((CONV_HALO, c), lambda b, i: (0, 0))]
        + [pl.BlockSpec((ts, r.shape[1]), lambda b, i: (b * nst + i, 0)) for r in rest_of_dz],
        out_specs=[pl.BlockSpec((ts, total), lambda b, i: (b * nst + i, 0)),
                   pl.BlockSpec((CONV_HALO, c), lambda b, i: (0, 0)), pl.BlockSpec((1, c), lambda b, i: (0, 0))],
        out_shape=[jax.ShapeDtypeStruct((t, total), BF16), jax.ShapeDtypeStruct((CONV_HALO, c), F32),
                   jax.ShapeDtypeStruct((1, c), F32)],
        scratch_shapes=[pltpu.VMEM((CONV_HALO + ts, c), F32)] * 2
        + [pltpu.VMEM((SUBLANES - 1, CONV_HALO + ts - SUBLANES, c), F32)] * 2,
        compiler_params=_params("arbitrary", "arbitrary"),
    )(da1, da1, z, z, z, z, w, *rest_of_dz)


FFN_ROWS = 16
FFN_COLS = 256


def _ffn_chunks(ts, f):
    cw = _pick(f, FFN_COLS)
    return [(r0, c0, cw) for r0 in range(0, ts, FFN_ROWS) for c0 in range(0, f, cw)]


def _tap_sources(buf, moved, offsets, rows):
    taps, used = [], 0
    for off in offsets:
        if off % SUBLANES:
            moved[used] = buf[pl.ds(off, rows), :]
            taps.append((moved.at[used], 0))
            used += 1
        else:
            taps.append((buf, off))
    return taps


def _moved_copies(offsets):
    return sum(1 for off in offsets if off % SUBLANES)


def _taps_sum(taps, w_ref, init, r0, cols):
    for k, (src, off) in enumerate(taps):
        init = init + w_ref[k:k + 1, cols] * src[pl.ds(off + r0, init.shape[0]), cols]
    return init


def _ffn_bwd(dact, up, w, b, dims, *, name, ts=128):
    t, f, kw = up.shape[0], dims.d_ff, dims.ffn_conv_width
    nst = dims.seq // ts
    fwd_offsets = [FFN_HALO - (kw - 1) + k for k in range(kw)]
    bwd_offsets = [(kw - 1) - k for k in range(kw)]
    dact_halo = 2 * FFN_HALO

    def body(d_ref, dn_ref, up_ref, hp_ref, hn_ref, w_ref, b_ref, o_ref, dw_ref, db_ref, buf, moved, dbuf, dmoved):
        i = pl.program_id(1)
        first = jnp.logical_and(pl.program_id(0) == 0, i == 0)
        more = i < nst - 1
        buf[0:FFN_HALO, :] = jnp.where(i > 0, hp_ref[...], 0.0)
        buf[FFN_HALO:FFN_HALO + ts, :] = up_ref[...]
        buf[FFN_HALO + ts:, :] = hn_ref[...]
        taps = _tap_sources(buf, moved, fwd_offsets, ts + FFN_HALO)

        def du_chunk(r0, rows, c0, cw, d):
            vcols, gcols = slice(c0, c0 + cw), slice(f + c0, f + c0 + cw)
            uv = _taps_sum(taps, w_ref, jnp.broadcast_to(b_ref[:, vcols], (rows, cw)), r0, vcols)
            ug = _taps_sum(taps, w_ref, jnp.broadcast_to(b_ref[:, gcols], (rows, cw)), r0, gcols)
            sg = _sigmoid(ug)
            dbuf[r0:r0 + rows, vcols] = d * ug * sg
            dbuf[r0:r0 + rows, gcols] = d * uv * sg * (1.0 + ug * (1.0 - sg))

        for r0, c0, cw in _ffn_chunks(ts, f):
            du_chunk(r0, FFN_ROWS, c0, cw, d_ref[r0:r0 + FFN_ROWS, c0:c0 + cw].astype(F32))
        for _, c0, cw in _ffn_chunks(FFN_ROWS, f):
            d_next = dn_ref[:, c0:c0 + cw].astype(F32)[0:FFN_HALO]
            du_chunk(ts, FFN_HALO, c0, cw, jnp.where(more, d_next, 0.0))

        @pl.when(first)
        def _():
            dw_ref[...] = jnp.zeros_like(dw_ref)
            db_ref[...] = jnp.zeros_like(db_ref)

        du = dbuf[0:ts, :]
        db_ref[...] += jnp.sum(du, axis=0, keepdims=True)
        for k, (src, off) in enumerate(taps):
            dw_ref[k:k + 1, :] += jnp.sum(du * src[pl.ds(off, ts), :], axis=0, keepdims=True)

        dtaps = _tap_sources(dbuf, dmoved, bwd_offsets, ts)
        for r0, c0, cw in _ffn_chunks(ts, 2 * f):
            cols = slice(c0, c0 + cw)
            o_ref[r0:r0 + FFN_ROWS, cols] = _taps_sum(dtaps, w_ref, jnp.zeros((FFN_ROWS, cw), F32), r0, cols).astype(BF16)

    up_cur, up_prev = _seq_specs(dims, ts, 2 * f, FFN_HALO, 0)
    _, up_next = _seq_specs(dims, ts, 2 * f, FFN_HALO, 0, nxt=True)
    d_cur, d_next = _seq_specs(dims, ts, f, dact_halo, 0, nxt=True)
    full = lambda rows: pl.BlockSpec((rows, 2 * f), lambda b_, i: (0, 0))
    return pl.pallas_call(
        body, name=name, grid=(dims.batch_local, nst),
        in_specs=[d_cur, d_next, up_cur, up_prev, up_next, full(FFN_HALO), full(1)],
        out_specs=[pl.BlockSpec((ts, 2 * f), lambda b_, i: (b_ * nst + i, 0)), full(FFN_HALO), full(1)],
        out_shape=[jax.ShapeDtypeStruct((t, 2 * f), BF16), jax.ShapeDtypeStruct((FFN_HALO, 2 * f), F32),
                   jax.ShapeDtypeStruct((1, 2 * f), F32)],
        scratch_shapes=[pltpu.VMEM((ts + 2 * FFN_HALO, 2 * f), F32),
                        pltpu.VMEM((_moved_copies(fwd_offsets), ts + FFN_HALO, 2 * f), F32),
                        pltpu.VMEM((ts + FFN_HALO, 2 * f), F32),
                        pltpu.VMEM((_moved_copies(bwd_offsets), ts, 2 * f), F32)],
        compiler_params=_params("arbitrary", "arbitrary"),
    )(dact, dact, up, up, up, w, b)


def _ffn_act_fwd(up, w, b, dims, *, name, ts=128):
    t, f, kw = up.shape[0], dims.d_ff, dims.ffn_conv_width
    offsets = [FFN_HALO - (kw - 1) + k for k in range(kw)]

    def body(up_ref, h_ref, w_ref, b_ref, o_ref, buf, moved):
        buf[FFN_HALO:, :] = up_ref[...]
        buf[0:FFN_HALO, :] = jnp.where(pl.program_id(1) > 0, h_ref[...], 0.0)
        taps = _tap_sources(buf, moved, offsets, ts)
        for r0, c0, cw in _ffn_chunks(ts, f):
            vcols, gcols = slice(c0, c0 + cw), slice(f + c0, f + c0 + cw)
            uv = _taps_sum(taps, w_ref, jnp.broadcast_to(b_ref[:, vcols], (FFN_ROWS, cw)), r0, vcols)
            ug = _taps_sum(taps, w_ref, jnp.broadcast_to(b_ref[:, gcols], (FFN_ROWS, cw)), r0, gcols)
            o_ref[r0:r0 + FFN_ROWS, vcols] = (ug * _sigmoid(ug) * uv).astype(BF16)

    full = lambda rows: pl.BlockSpec((rows, 2 * f), lambda b_, i: (0, 0))
    return pl.pallas_call(
        body, name=name, grid=(dims.batch_local, dims.seq // ts),
        in_specs=[*_seq_specs(dims, ts, 2 * f, FFN_HALO, 0), full(FFN_HALO), full(1)],
        out_specs=pl.BlockSpec((ts, f), lambda b_, i: (b_ * (dims.seq // ts) + i, 0)),
        out_shape=jax.ShapeDtypeStruct((t, f), BF16),
        scratch_shapes=[pltpu.VMEM((FFN_HALO + ts, 2 * f), F32), pltpu.VMEM((_moved_copies(offsets), ts, 2 * f), F32)],
        compiler_params=_params("parallel", "parallel"),
    )(up, up, w, b)


def _dot_nt(a, b):
    return lax.dot_general(a, b, (((1,), (1,)), ((), ())), preferred_element_type=F32)


def _dot_tn(a, b):
    return lax.dot_general(a, b, (((0,), (0,)), ((), ())), preferred_element_type=F32)


LANES = 128
MASK_BIAS = 1e30
RESIDUE_DILATIONS = tuple(d for d in DILATIONS if d > 1)


def _rows_to_residues(value, out_ref, scr, d):
    rows, width = value.shape
    for c in range(width // LANES):
        cols = slice(LANES * c, LANES * (c + 1))
        scr[c] = value[:, cols]
        for r in range(d):
            out_ref[r, :, cols] = scr[c, pl.ds(r, rows // d, stride=d), :].astype(out_ref.dtype)


def _residues_to_rows(in_ref, scr, d):
    _, n, width = in_ref.shape
    slabs = []
    for c in range(width // LANES):
        cols = slice(LANES * c, LANES * (c + 1))
        for r in range(d):
            scr[c, pl.ds(r, n, stride=d), :] = in_ref[r, :, cols].astype(F32)
        slabs.append(scr[c])
    return slabs[0] if len(slabs) == 1 else jnp.concatenate(slabs, axis=1)


def _residue_shape(dims, d, width, dtype):
    return jax.ShapeDtypeStruct((dims.batch_local, d, dims.seq // d, width), dtype)


def _residue_spec(dims, d, tr, width):
    tiles = dims.seq // tr
    return pl.BlockSpec((None, d, tr // d, width), lambda i: (i // tiles, 0, i % tiles, 0))


def _head_sum_matrix(dims):
    a = dims.n_heads * dims.head_dim
    head = jnp.arange(a, dtype=jnp.int32) // dims.head_dim
    return (head[:, None] == jnp.arange(LANES, dtype=jnp.int32)[None, :]).astype(BF16)


def _two_pass_dot(v, m):
    hi = v.astype(BF16)
    lo = (v - hi.astype(F32)).astype(BF16)
    return jnp.dot(hi, m, preferred_element_type=F32) + jnp.dot(lo, m, preferred_element_type=F32)


def _residue_permutations(tr):
    out = []
    for d in RESIDUE_DILATIONS:
        dst = jnp.arange(tr, dtype=jnp.int32)
        src = d * (dst % (tr // d)) + dst // (tr // d)
        out.append((src[:, None] == jnp.arange(tr, dtype=jnp.int32)[None, :]).astype(BF16))
    return out


def _bf16_rows_to_residues(value, out_ref, perm_ref, d):
    n = value.shape[0] // d
    moved = jnp.dot(perm_ref[...], value, preferred_element_type=F32).astype(out_ref.dtype)
    for r in range(d):
        out_ref[r] = moved[r * n:(r + 1) * n]


def _bf16_residues_to_rows(in_ref, back_ref):
    d = in_ref.shape[0]
    stacked = jnp.concatenate([in_ref[r] for r in range(d)], axis=0)
    return jnp.dot(back_ref[...], stacked, preferred_element_type=F32)


def _qkv_layouts_fwd(z, gq, gk, head_ones, dims, *, name, tr=256):
    t = z.shape[0]
    a = dims.n_heads * dims.head_dim
    q_scale = dims.head_dim ** -0.5
    nres = len(RESIDUE_DILATIONS)

    def body(q_ref, k_ref, v_ref, gq_ref, gk_ref, sum_ref, spread_ref, *rest):
        perm_refs, outs = rest[:nres], rest[nres:]
        qv, kv = q_ref[...].astype(F32), k_ref[...].astype(F32)
        mean = lambda val: _two_pass_dot(_two_pass_dot(val, sum_ref[...]), spread_ref[...]) * (1.0 / dims.head_dim)
        rq = lax.rsqrt(mean(qv * qv) + RMS_EPS)
        rk = lax.rsqrt(mean(kv * kv) + RMS_EPS)
        values = ((qv * rq * gq_ref[...] * q_scale).astype(BF16), (kv * rk * gk_ref[...]).astype(BF16), v_ref[...])
        for j, val in enumerate(values):
            outs[j][...] = val
            for g, d in enumerate(RESIDUE_DILATIONS):
                _bf16_rows_to_residues(val, outs[3 * (g + 1) + j], perm_refs[g], d)

    out_specs = [_row_spec(tr, a)] * 3
    out_shape = [jax.ShapeDtypeStruct((t, a), BF16)] * 3
    for d in RESIDUE_DILATIONS:
        out_specs += [_residue_spec(dims, d, tr, a)] * 3
        out_shape += [_residue_shape(dims, d, a, BF16)] * 3
    outs = pl.pallas_call(
        body, name=name, grid=(t // tr,),
        in_specs=[_row_spec(tr, a, 2), _row_spec(tr, a, 3), _row_spec(tr, a, 4), _vec_spec(a), _vec_spec(a),
                  pl.BlockSpec((a, LANES), lambda i: (0, 0)), pl.BlockSpec((LANES, a), lambda i: (0, 0))]
        + [pl.BlockSpec((tr, tr), lambda i: (0, 0))] * nres,
        out_specs=out_specs, out_shape=out_shape,
        compiler_params=_params("parallel"),
    )(z, z, z, gq, gk, *head_ones, *_residue_permutations(tr))
    return {d: tuple(outs[3 * g:3 * g + 3]) for g, d in enumerate((1,) + RESIDUE_DILATIONS)}


ATTN_RESIDUES_PER_STEP = 4


def _attn_residues(dims, dil):
    one_block = dims.seq // dil == ATTN_BLOCK
    return math.gcd(dil, ATTN_RESIDUES_PER_STEP) if dil > 1 and one_block else 1


def _per_residue(body, rs):
    if rs == 1:
        return body

    def stepped(*refs):
        for r in range(rs):
            body(*[ref.at[r] for ref in refs])

    return stepped


def _attn_specs(dims, dil, width):
    blk = ATTN_BLOCK
    nb = dims.seq // dil // blk
    rs = _attn_residues(dims, dil)
    if dil == 1:
        grid = (dims.batch_local, nb)
        at = lambda f: pl.BlockSpec((blk, width), lambda b, i: (b * nb + f(i), 0))
    elif rs > 1:
        grid = (dims.batch_local, dil // rs)
        at = lambda f: pl.BlockSpec((None, rs, blk, width), lambda b, r: (b, r, 0, 0))
    else:
        grid = (dims.batch_local, dil, nb)
        at = lambda f: pl.BlockSpec((None, None, blk, width), lambda b, r, i: (b, r, f(i), 0))
    return grid, at(lambda i: i), at(lambda i: jnp.maximum(i - 1, 0)), at(lambda i: jnp.minimum(i + 1, nb - 1))


def _head_slopes(n_heads):
    h = lax.broadcasted_iota(jnp.int32, (n_heads, 1, 1), 0).astype(F32)
    return jnp.exp((h + 1.0) * (-8.0 / n_heads * math.log(2.0)))


def _pair_masks(hd):
    low = lax.broadcasted_iota(jnp.int32, (1, 2 * hd), 1) < hd
    return low, jnp.logical_not(low)


def _attn_fwd(q, k, v, dims, dil, *, name):
    a = dims.n_heads * dims.head_dim
    heads, hd, blk = dims.n_heads, dims.head_dim, ATTN_BLOCK
    assert 2 * hd == LANES and heads % 2 == 0 and heads <= LANES
    nb = dims.seq // dil // blk
    has_prev = nb > 1
    nkeys = 2 * blk if has_prev else blk
    grid, cur, prev, _ = _attn_specs(dims, dil, a)
    _, cur_stat, _, _ = _attn_specs(dims, dil, LANES)

    def body(*refs):
        if has_prev:
            q_ref, kc_ref, vc_ref, kp_ref, vp_ref, o_ref, lse_ref, s_scr, p_scr, k_st, v_st = refs
            k_st[0:blk, :], k_st[blk:, :] = kp_ref[...], kc_ref[...]
            v_st[0:blk, :], v_st[blk:, :] = vp_ref[...], vc_ref[...]
        else:
            q_ref, k_st, v_st, o_ref, lse_ref, s_scr, p_scr = refs
        low, high = _pair_masks(hd)

        for hp in range(heads // 2):
            sl = slice(LANES * hp, LANES * (hp + 1))
            q2 = q_ref[:, sl]
            kcat = k_st[:, sl]
            s_scr[2 * hp] = _dot_nt(jnp.where(low, q2, jnp.zeros_like(q2)), kcat)
            s_scr[2 * hp + 1] = _dot_nt(jnp.where(high, q2, jnp.zeros_like(q2)), kcat)

        iq = lax.broadcasted_iota(jnp.int32, (blk, nkeys), 0)
        jk = lax.broadcasted_iota(jnp.int32, (blk, nkeys), 1)
        if has_prev:
            steps = iq + blk - jk
            valid = (steps >= 0) & (steps <= blk) & ((jk >= blk) | (pl.program_id(len(grid) - 1) > 0))
        else:
            steps = iq - jk
            valid = steps >= 0
        bias = jnp.where(valid, steps.astype(F32) * (-float(dil)), -MASK_BIAS)
        s = s_scr[...] + _head_slopes(heads) * bias[None]
        m = jnp.max(s, axis=-1, keepdims=True)
        p = jnp.exp(s - m)
        l = jnp.sum(p, axis=-1, keepdims=True)
        p_scr[...] = p.astype(BF16)
        inv = 1.0 / l
        lse = m + jnp.log(l)

        lane = lax.broadcasted_iota(jnp.int32, (blk, LANES), 1)
        stat = jnp.zeros((blk, LANES), F32)
        for hp in range(heads // 2):
            sl = slice(LANES * hp, LANES * (hp + 1))
            vcat = v_st[:, sl]
            pv_a = jnp.dot(p_scr[2 * hp], vcat, preferred_element_type=F32) * inv[2 * hp]
            pv_b = jnp.dot(p_scr[2 * hp + 1], vcat, preferred_element_type=F32) * inv[2 * hp + 1]
            o_ref[:, sl] = jnp.where(low, pv_a, pv_b).astype(BF16)
            stat = jnp.where(lane == 2 * hp, lse[2 * hp], stat)
            stat = jnp.where(lane == 2 * hp + 1, lse[2 * hp + 1], stat)
        lse_ref[...] = stat

    lead = q.shape[:-2]
    rows = q.shape[-2]
    rs = _attn_residues(dims, dil)
    per_step = lambda shape: shape if rs == 1 else (rs,) + shape
    o, lse = pl.pallas_call(
        _per_residue(body, rs), name=name, grid=grid,
        in_specs=[cur, cur, cur] + ([prev, prev] if has_prev else []),
        out_specs=[cur, cur_stat],
        out_shape=[jax.ShapeDtypeStruct(lead + (rows, a), BF16), jax.ShapeDtypeStruct(lead + (rows, LANES), F32)],
        scratch_shapes=[pltpu.VMEM(per_step((heads, blk, nkeys)), F32), pltpu.VMEM(per_step((heads, blk, nkeys)), BF16)]
        + ([pltpu.VMEM((nkeys, a), BF16)] * 2 if has_prev else []),
        compiler_params=_params(*["parallel"] * len(grid)),
    )(q, k, v, *([k, v] if has_prev else []))
    return o, lse


def _attn_combine(groups, head_spread, dims, *, name, tr=256):
    t = dims.tokens
    a = dims.n_heads * dims.head_dim
    dils = tuple(groups)

    nres = len(RESIDUE_DILATIONS)

    def body(*refs):
        ins = refs[:2 * len(dils)]
        x_ref = refs[2 * len(dils)]
        back_refs = dict(zip(RESIDUE_DILATIONS, refs[2 * len(dils) + 1:2 * len(dils) + 1 + nres]))
        o_ref = refs[2 * len(dils) + 1 + nres]
        lse_refs = refs[2 * len(dils) + 2 + nres:-1]
        scr_stat = refs[-1]
        outs, stats = [], []
        for g, d in enumerate(dils):
            if d == 1:
                outs.append(ins[2 * g][...].astype(F32))
                stats.append(ins[2 * g + 1][...])
            else:
                outs.append(_bf16_residues_to_rows(ins[2 * g], back_refs[d]))
                stats.append(_residues_to_rows(ins[2 * g + 1], scr_stat, d))
        top = functools.reduce(jnp.maximum, stats)
        weights = [jnp.exp(s - top) for s in stats]
        total = functools.reduce(jnp.add, weights)
        joint = top + jnp.log(total)
        inv = 1.0 / total
        acc = None
        for w, o in zip(weights, outs):
            term = _two_pass_dot(w * inv, x_ref[...]) * o
            acc = term if acc is None else acc + term
        o_ref[...] = acc.astype(BF16)
        for g, d in enumerate(dils):
            if d == 1:
                lse_refs[g][...] = joint
            else:
                _rows_to_residues(joint, lse_refs[g], scr_stat, d)

    in_specs, args, lse_specs, lse_shapes = [], [], [], []
    for d in dils:
        if d == 1:
            in_specs += [_row_spec(tr, a), _row_spec(tr, LANES)]
            lse_specs.append(_row_spec(tr, LANES))
            lse_shapes.append(jax.ShapeDtypeStruct((t, LANES), F32))
        else:
            in_specs += [_residue_spec(dims, d, tr, a), _residue_spec(dims, d, tr, LANES)]
            lse_specs.append(_residue_spec(dims, d, tr, LANES))
            lse_shapes.append(_residue_shape(dims, d, LANES, F32))
        args += list(groups[d])
    outs = pl.pallas_call(
        body, name=name, grid=(t // tr,),
        in_specs=in_specs + [pl.BlockSpec((LANES, a), lambda i: (0, 0))] + [pl.BlockSpec((tr, tr), lambda i: (0, 0))] * nres,
        out_specs=[_row_spec(tr, a)] + lse_specs,
        out_shape=[jax.ShapeDtypeStruct((t, a), BF16)] + lse_shapes,
        scratch_shapes=[pltpu.VMEM((1, tr, LANES), F32)],
        compiler_params=_params("parallel"),
    )(*args, head_spread, *[jnp.transpose(p) for p in _residue_permutations(tr)])
    return outs[0], dict(zip(dils, outs[1:]))


def _attn_bwd_prep(do, o, head_sum, dims, *, name, tr=256):
    t, a = o.shape
    nres = len(RESIDUE_DILATIONS)

    def body(do_ref, o_ref, e_ref, *rest):
        perm_refs, outs, scr_stat = rest[:nres], rest[nres:-1], rest[-1]
        delta = _two_pass_dot(do_ref[...].astype(F32) * o_ref[...].astype(F32), e_ref[...])
        outs[0][...] = delta
        for g, d in enumerate(RESIDUE_DILATIONS):
            _bf16_rows_to_residues(do_ref[...], outs[1 + 2 * g], perm_refs[g], d)
            _rows_to_residues(delta, outs[2 + 2 * g], scr_stat, d)

    out_specs, out_shape = [_row_spec(tr, LANES)], [jax.ShapeDtypeStruct((t, LANES), F32)]
    for d in RESIDUE_DILATIONS:
        out_specs += [_residue_spec(dims, d, tr, a), _residue_spec(dims, d, tr, LANES)]
        out_shape += [_residue_shape(dims, d, a, BF16), _residue_shape(dims, d, LANES, F32)]
    outs = pl.pallas_call(
        body, name=name, grid=(t // tr,),
        in_specs=[_row_spec(tr, a), _row_spec(tr, a), pl.BlockSpec((a, LANES), lambda i: (0, 0))]
        + [pl.BlockSpec((tr, tr), lambda i: (0, 0))] * nres,
        out_specs=out_specs, out_shape=out_shape,
        scratch_shapes=[pltpu.VMEM((1, tr, LANES), F32)],
        compiler_params=_params("parallel"),
    )(do, o, head_sum, *_residue_permutations(tr))
    dos, deltas = {1: do}, {1: outs[0]}
    for g, d in enumerate(RESIDUE_DILATIONS):
        dos[d], deltas[d] = outs[1 + 2 * g], outs[2 + 2 * g]
    return dos, deltas


def _attn_bwd(q, k, v, do, lse, delta, dims, dil, *, name):
    a = dims.n_heads * dims.head_dim
    heads, hd, blk = dims.n_heads, dims.head_dim, ATTN_BLOCK
    nb = dims.seq // dil // blk
    has_next = nb > 1
    nq = 2 * blk if has_next else blk
    grid, cur, _, nxt = _attn_specs(dims, dil, a)
    _, cur_stat, _, nxt_stat = _attn_specs(dims, dil, LANES)

    def body(*refs):
        k_ref, v_ref, q_ref, do_ref, lse_ref, dl_ref = refs[:6]
        if has_next:
            qn_ref, don_ref, lsen_ref, dln_ref = refs[6:10]
            dq_ref, dk_ref, dv_ref, q_st, do_st, s_scr, dp_scr, p_scr, ds_scr, carry = refs[10:]
        else:
            dq_ref, dk_ref, dv_ref, q_st, do_st, s_scr, dp_scr, p_scr, ds_scr = refs[6:]
        j = pl.program_id(len(grid) - 1)
        low, high = _pair_masks(hd)
        q_st[0:blk, :] = q_ref[...]
        do_st[0:blk, :] = do_ref[...]
        if has_next:
            q_st[blk:, :] = qn_ref[...]
            do_st[blk:, :] = don_ref[...]
            lse_all = jnp.concatenate([lse_ref[...], lsen_ref[...]], axis=0)
            dl_all = jnp.concatenate([dl_ref[...], dln_ref[...]], axis=0)
        else:
            lse_all, dl_all = lse_ref[...], dl_ref[...]
        lse_t, dl_t = jnp.transpose(lse_all), jnp.transpose(dl_all)
        lse3 = jnp.stack([lse_t[h:h + 1, :] for h in range(heads)])
        dl3 = jnp.stack([dl_t[h:h + 1, :] for h in range(heads)])

        def halves(x):
            return jnp.where(low, x, jnp.zeros_like(x)), jnp.where(high, x, jnp.zeros_like(x))

        for hp in range(heads // 2):
            sl = slice(LANES * hp, LANES * (hp + 1))
            k2, v2 = k_ref[:, sl], v_ref[:, sl]
            q_a, q_b = halves(q_st[:, sl])
            do_a, do_b = halves(do_st[:, sl])
            s_scr[2 * hp], s_scr[2 * hp + 1] = _dot_nt(k2, q_a), _dot_nt(k2, q_b)
            dp_scr[2 * hp], dp_scr[2 * hp + 1] = _dot_nt(v2, do_a), _dot_nt(v2, do_b)

        jk = lax.broadcasted_iota(jnp.int32, (blk, nq), 0)
        rq = lax.broadcasted_iota(jnp.int32, (blk, nq), 1)
        if has_next:
            iq = jnp.where(rq < blk, rq, rq - blk)
            steps = jnp.where(rq < blk, iq - jk, iq - jk + blk)
            valid = ((rq < blk) & (iq >= jk)) | ((rq >= blk) & (jk >= iq) & (j + 1 < nb))
        else:
            steps, valid = rq - jk, rq >= jk
        bias = jnp.where(valid, steps.astype(F32) * (-float(dil)), -MASK_BIAS)
        p = jnp.exp(s_scr[...] + _head_slopes(heads) * bias[None] - lse3)
        p_scr[...] = p.astype(BF16)
        ds_scr[...] = (p * (dp_scr[...] - dl3)).astype(BF16)

        if has_next:
            @pl.when(j == 0)
            def _():
                carry[...] = jnp.zeros_like(carry)

        for hp in range(heads // 2):
            sl = slice(LANES * hp, LANES * (hp + 1))
            k2 = k_ref[:, sl]
            q_a, q_b = halves(q_st[:, sl])
            do_a, do_b = halves(do_st[:, sl])
            ds_a, ds_b = ds_scr[2 * hp], ds_scr[2 * hp + 1]
            dk_ref[:, sl] = (jnp.dot(ds_a, q_a, preferred_element_type=F32)
                             + jnp.dot(ds_b, q_b, preferred_element_type=F32)).astype(BF16)
            dv_ref[:, sl] = (jnp.dot(p_scr[2 * hp], do_a, preferred_element_type=F32)
                             + jnp.dot(p_scr[2 * hp + 1], do_b, preferred_element_type=F32)).astype(BF16)
            dq2 = jnp.where(low, _dot_tn(ds_a, k2), _dot_tn(ds_b, k2))
            if has_next:
                dq_ref[:, sl] = (carry[:, sl] + dq2[:blk]).astype(BF16)
                carry[:, sl] = dq2[blk:]
            else:
                dq_ref[:, sl] = dq2.astype(BF16)

    args, in_specs = [k, v, q, do, lse, delta], [cur] * 4 + [cur_stat] * 2
    if has_next:
        args += [q, do, lse, delta]
        in_specs += [nxt] * 2 + [nxt_stat] * 2
    shape = jax.ShapeDtypeStruct(q.shape, BF16)
    rs = _attn_residues(dims, dil)
    per_step = lambda dims_: dims_ if rs == 1 else (rs,) + dims_
    scratch = ([pltpu.VMEM(per_step((nq, a)), BF16)] * 2 + [pltpu.VMEM(per_step((heads, blk, nq)), F32)] * 2
               + [pltpu.VMEM(per_step((heads, blk, nq)), BF16)] * 2)
    if has_next:
        scratch.append(pltpu.VMEM((blk, a), F32))
    return pl.pallas_call(
        _per_residue(body, rs), name=name, grid=grid, in_specs=in_specs, out_specs=[cur] * 3, out_shape=[shape] * 3,
        scratch_shapes=scratch,
        compiler_params=_params(*["parallel"] * (len(grid) - 1), "arbitrary"),
    )(*args)


def _qkv_layouts_bwd(z, grads, gq, gk, head_ones, dims, *, name, tr=256):
    t = z.shape[0]
    a = dims.n_heads * dims.head_dim
    q_scale = dims.head_dim ** -0.5
    dils = tuple(grads)
    nres = len(RESIDUE_DILATIONS)

    def body(q_ref, k_ref, *rest):
        d_refs = rest[:3 * len(dils)]
        gq_ref, gk_ref, sum_ref, spread_ref = rest[3 * len(dils):3 * len(dils) + 4]
        back_refs = dict(zip(RESIDUE_DILATIONS, rest[3 * len(dils) + 4:3 * len(dils) + 4 + nres]))
        dz_ref, dgq_ref, dgk_ref = rest[3 * len(dils) + 4 + nres:]
        first = pl.program_id(0) == 0
        mean = lambda val: _two_pass_dot(_two_pass_dot(val, sum_ref[...]), spread_ref[...]) * (1.0 / dims.head_dim)

        def total(j):
            acc = None
            for g, d in enumerate(dils):
                ref = d_refs[3 * g + j]
                part = ref[...].astype(F32) if d == 1 else _bf16_residues_to_rows(ref, back_refs[d])
                acc = part if acc is None else acc + part
            return acc

        def norm_bwd(x_ref, dy, g_ref, scale, col, dg_ref):
            xv = x_ref[...].astype(F32)
            dy = dy * scale
            r = lax.rsqrt(mean(xv * xv) + RMS_EPS)
            gy = dy * g_ref[...]
            dx = r * gy - xv * (r * r * r) * mean(xv * gy)
            dz_ref[:, col * a:(col + 1) * a] = dx.astype(BF16)
            _accumulate(dg_ref, jnp.sum(dy * xv * r, axis=0, keepdims=True), first)

        norm_bwd(q_ref, total(0), gq_ref, q_scale, 0, dgq_ref)
        norm_bwd(k_ref, total(1), gk_ref, 1.0, 1, dgk_ref)
        dz_ref[:, 2 * a:3 * a] = total(2).astype(BF16)

    in_specs, args = [_row_spec(tr, a, 2), _row_spec(tr, a, 3)], [z, z]
    for d in dils:
        in_specs += [_row_spec(tr, a) if d == 1 else _residue_spec(dims, d, tr, a)] * 3
        args += list(grads[d])
    in_specs += [_vec_spec(a), _vec_spec(a), pl.BlockSpec((a, LANES), lambda i: (0, 0)),
                 pl.BlockSpec((LANES, a), lambda i: (0, 0))] + [pl.BlockSpec((tr, tr), lambda i: (0, 0))] * nres
    return pl.pallas_call(
        body, name=name, grid=(t // tr,), in_specs=in_specs,
        out_specs=[_row_spec(tr, 3 * a), _vec_spec(a), _vec_spec(a)],
        out_shape=[jax.ShapeDtypeStruct((t, 3 * a), BF16)] + [jax.ShapeDtypeStruct((1, a), F32)] * 2,
        compiler_params=_params("arbitrary"),
    )(*args, gq, gk, *head_ones, *[jnp.transpose(p) for p in _residue_permutations(tr)])


def _mix_fwd(ya, yb, z, gate_b, dims, *, name, tr=512):
    t, d = ya.shape
    tr = _pick(t, tr, 8)
    first_gate_col = z.shape[1] // d - 2

    def body(ya_ref, yb_ref, ga_ref, gb_ref, ba_ref, bb_ref, o_ref):
        g_a = _sigmoid(ga_ref[...].astype(F32) + ba_ref[...])
        g_b = _sigmoid(gb_ref[...].astype(F32) + bb_ref[...])
        o_ref[...] = (g_a * ya_ref[...] + g_b * yb_ref[...]).astype(BF16)

    return pl.pallas_call(
        body, name=name, grid=(t // tr,),
        in_specs=[_row_spec(tr, d), _row_spec(tr, d), _row_spec(tr, d, first_gate_col),
                  _row_spec(tr, d, first_gate_col + 1), _vec_spec(d, 0), _vec_spec(d, 1)],
        out_specs=_row_spec(tr, d), out_shape=jax.ShapeDtypeStruct((t, d), BF16),
        compiler_params=_params("parallel"),
    )(ya, yb, z, z, gate_b, gate_b)


def _mix_bwd(dmix, ya, yb, z, gate_b, dims, *, name, tr=512):
    t, d = ya.shape
    tr = _pick(t, tr, 8)
    first_gate_col = z.shape[1] // d - 2

    def body(dm_ref, ya_ref, yb_ref, ga_ref, gb_ref, ba_ref, bb_ref, dya_ref, dyb_ref, dz_ref, db_ref):
        dm = dm_ref[...].astype(F32)
        g_a = _sigmoid(ga_ref[...].astype(F32) + ba_ref[...])
        g_b = _sigmoid(gb_ref[...].astype(F32) + bb_ref[...])
        dya_ref[...] = (dm * g_a).astype(BF16)
        dyb_ref[...] = (dm * g_b).astype(BF16)
        dl_a = dm * ya_ref[...] * g_a * (1.0 - g_a)
        dl_b = dm * yb_ref[...] * g_b * (1.0 - g_b)
        dz_ref[:, 0:d] = dl_a.astype(BF16)
        dz_ref[:, d:2 * d] = dl_b.astype(BF16)
        first = pl.program_id(0) == 0
        sums = jnp.concatenate([jnp.sum(dl_a, axis=0, keepdims=True), jnp.sum(dl_b, axis=0, keepdims=True)], axis=1)
        _accumulate(db_ref, sums, first)

    return pl.pallas_call(
        body, name=name, grid=(t // tr,),
        in_specs=[_row_spec(tr, d), _row_spec(tr, d), _row_spec(tr, d), _row_spec(tr, d, first_gate_col),
                  _row_spec(tr, d, first_gate_col + 1), _vec_spec(d, 0), _vec_spec(d, 1)],
        out_specs=[_row_spec(tr, d), _row_spec(tr, d), _row_spec(tr, 2 * d), _vec_spec(2 * d)],
        out_shape=[jax.ShapeDtypeStruct((t, d), BF16)] * 2 + [jax.ShapeDtypeStruct((t, 2 * d), BF16),
                                                              jax.ShapeDtypeStruct((1, 2 * d), F32)],
        compiler_params=_params("arbitrary"),
    )(dmix, ya, yb, z, z, gate_b, gate_b)


def _adamw(w, grads, m, v, *, name, tr=256):
    r, c = w.shape
    tr = _pick(r, tr, 8)
    ng = len(grads)
    c1 = 1.0 - ADAM_B1 ** ADAM_STEP
    c2 = 1.0 - ADAM_B2 ** ADAM_STEP

    def body(*refs):
        w_ref, g_refs, m_ref, v_ref = refs[0], refs[1:1 + ng], refs[1 + ng], refs[2 + ng]
        g_out, d_out, m_out, v_out = refs[3 + ng:]
        g = g_refs[0][...]
        for extra in g_refs[1:]:
            g = g + extra[...]
        m_new = ADAM_B1 * m_ref[...] + (1.0 - ADAM_B1) * g
        v_new = ADAM_B2 * v_ref[...] + (1.0 - ADAM_B2) * (g * g)
        g_out[...] = g
        m_out[...] = m_new
        v_out[...] = v_new
        d_out[...] = -ADAM_LR * ((m_new / c1) / (jnp.sqrt(v_new / c2) + ADAM_EPS) + ADAM_WD * w_ref[...])

    spec = pl.BlockSpec((tr, c), lambda i: (i, 0))
    return pl.pallas_call(
        body, name=name, grid=(r // tr,),
        in_specs=[spec] * (3 + ng), out_specs=[spec] * 4, out_shape=[jax.ShapeDtypeStruct((r, c), F32)] * 4,
        compiler_params=_params("parallel"),
    )(w, *grads, m, v)


CHIP_PEERS = ((1, 0), (0, 1), (1, 1))


def _place():
    return lax.axis_index("x"), lax.axis_index("y"), lax.axis_index("c")


HBM = pl.BlockSpec(memory_space=pltpu.HBM)
SEM = pl.BlockSpec(memory_space=pltpu.SEMAPHORE)
IN_FLIGHT = pltpu.SideEffectType.DATAFLOW_SIDE_EFFECTING


def _in_hbm(a):
    return pltpu.with_memory_space_constraint(a, pltpu.HBM)


def _cast_to_lands(shards, dtypes, *, name, after=None):
    n = len(shards)

    def body(*refs):
        ins, outs, bufs, sems = refs[:n], refs[n:2 * n], refs[2 * n:3 * n], refs[3 * n]
        x, y, _ = _place()
        copies = []
        for a in range(n):
            bufs[a][...] = ins[a][...].astype(dtypes[a])
            cp = pltpu.make_async_copy(bufs[a], outs[a].at[2 * x + y], sems.at[a])
            cp.start()
            copies.append(cp)
        for cp in copies:
            cp.wait()

    body, more_specs, more_args = _ordered(body, n, after)
    return pl.pallas_call(
        body, name=name, in_specs=[pl.BlockSpec(memory_space=pltpu.VMEM)] * n + more_specs, out_specs=[ANY] * n,
        out_shape=[jax.ShapeDtypeStruct((N_CHIPS,) + s.shape, dt) for s, dt in zip(shards, dtypes)],
        scratch_shapes=[pltpu.VMEM(s.shape, dt) for s, dt in zip(shards, dtypes)] + [pltpu.SemaphoreType.DMA((n,))],
        compiler_params=pltpu.CompilerParams(vmem_limit_bytes=V7X_VMEM_LIMIT_BYTES),
    )(*shards, *more_args)


def _chip_copy(src, dst, send, recv, flip, place):
    x, y, c = place
    return pltpu.make_async_remote_copy(src_ref=src, dst_ref=dst, send_sem=send, recv_sem=recv,
                                        device_id=(x ^ flip[0], y ^ flip[1], c), device_id_type=MESH)


def _my_part(land, place, halved):
    block = land.at[2 * place[0] + place[1]]
    if not halved:
        return block
    rows = land.shape[1] // 2
    return block.at[pl.ds(pl.multiple_of(place[2] * rows, rows), rows)]


def _gather_start(lands, after, *, name, halved=()):
    n = len(lands)

    def body(*refs):
        ins, send, recv, token = refs[:n], refs[n + 1], refs[n + 2], refs[-1]
        place = _place()
        for a in range(n):
            part = _my_part(ins[a], place, a in halved)
            for p, flip in enumerate(CHIP_PEERS):
                k = 3 * a + p
                _chip_copy(part, part, send.at[k], recv.at[k], flip, place).start()
        token[...] = jnp.zeros_like(token)

    outs = pl.pallas_call(
        body, name=name, in_specs=[HBM] * n + [ANY],
        out_specs=(SEM, SEM, *[HBM] * n, pl.BlockSpec(memory_space=pltpu.VMEM)),
        out_shape=(pltpu.SemaphoreType.DMA((3 * n,)), pltpu.SemaphoreType.DMA((3 * n,)),
                   *[pltpu.HBM(l.shape, l.dtype) for l in lands], jax.ShapeDtypeStruct((8, 128), F32)),
        input_output_aliases={a: 2 + a for a in range(n)},
        compiler_params=pltpu.CompilerParams(has_side_effects=IN_FLIGHT),
    )(*[_in_hbm(l) for l in lands], after)
    return outs[0], outs[1], list(outs[2:2 + n]), outs[-1]


def _gather_wait(send, recv, lands, after, *, name, halved=()):
    n = len(lands)

    def body(*refs):
        ins, send_ref, recv_ref = refs[:n], refs[n], refs[n + 1]
        place = _place()
        for a in range(n):
            part = _my_part(ins[a], place, a in halved)
            for p, flip in enumerate(CHIP_PEERS):
                k = 3 * a + p
                cp = _chip_copy(part, part, send_ref.at[k], recv_ref.at[k], flip, place)
                cp.wait_send()
                cp.wait_recv()

    after = list(after) if isinstance(after, (list, tuple)) else [after]
    return pl.pallas_call(
        body, name=name, in_specs=[HBM] * n + [SEM, SEM] + [ANY] * len(after), out_specs=[HBM] * n,
        out_shape=[pltpu.HBM(l.shape, l.dtype) for l in lands],
        input_output_aliases={a: a for a in range(n)},
        compiler_params=pltpu.CompilerParams(has_side_effects=IN_FLIGHT),
    )(*lands, send, recv, *after)


def _forward_to_sibling(land, *, name):
    rows = land.shape[1] // 2

    def body(land_ref, out_ref, send, recv):
        x, y, c = _place()
        copies = []
        for p, (fx, fy) in enumerate(CHIP_PEERS):
            chip = 2 * (x ^ fx) + (y ^ fy)
            mine = pl.ds(pl.multiple_of(c * rows, rows), rows)
            theirs = pl.ds(pl.multiple_of((1 - c) * rows, rows), rows)
            out = pltpu.make_async_remote_copy(
                src_ref=land_ref.at[chip].at[mine], dst_ref=out_ref.at[chip].at[mine], send_sem=send.at[p],
                recv_sem=recv.at[p], device_id=(x, y, 1 - c), device_id_type=MESH)
            out.start()
            copies.append((out, pltpu.make_async_remote_copy(
                src_ref=land_ref.at[chip].at[theirs], dst_ref=out_ref.at[chip].at[theirs], send_sem=send.at[p],
                recv_sem=recv.at[p], device_id=(x, y, 1 - c), device_id_type=MESH)))
        for out, arriving in copies:
            out.wait_send()
            arriving.wait_recv()

    return pl.pallas_call(
        body, name=name, in_specs=[ANY], out_specs=ANY, out_shape=jax.ShapeDtypeStruct(land.shape, land.dtype),
        input_output_aliases={0: 0},
        scratch_shapes=[pltpu.SemaphoreType.DMA((3,)), pltpu.SemaphoreType.DMA((3,))],
    )(land)


def _scatter_start(grad, *, name):
    def body(g_ref, land_ref, send, recv, g_thru, land_thru, token):
        place = _place()
        for p, flip in enumerate(CHIP_PEERS):
            peer_chip = 2 * (place[0] ^ flip[0]) + (place[1] ^ flip[1])
            _chip_copy(g_ref.at[peer_chip], land_ref.at[p], send.at[p], recv.at[p], flip, place).start()
        token[...] = jnp.zeros_like(token)

    land = lax.empty((3,) + grad.shape[1:], grad.dtype)
    return pl.pallas_call(
        body, name=name, in_specs=[HBM, HBM],
        out_specs=(SEM, SEM, HBM, HBM, pl.BlockSpec(memory_space=pltpu.VMEM)),
        out_shape=(pltpu.SemaphoreType.DMA((3,)), pltpu.SemaphoreType.DMA((3,)), pltpu.HBM(grad.shape, grad.dtype),
                   pltpu.HBM(land.shape, land.dtype), jax.ShapeDtypeStruct((8, 128), F32)),
        input_output_aliases={0: 2, 1: 3},
        compiler_params=pltpu.CompilerParams(has_side_effects=IN_FLIGHT),
    )(_in_hbm(grad), _in_hbm(land))


def _scatter_wait(started, after, *, name):
    n = len(started)

    def body(*refs):
        grads, lands = refs[:n], refs[n:2 * n]
        sends, recvs = refs[2 * n:3 * n], refs[3 * n:4 * n]
        place = _place()
        for a in range(n):
            for p, flip in enumerate(CHIP_PEERS):
                cp = _chip_copy(grads[a].at[0], lands[a].at[p], sends[a].at[p], recvs[a].at[p], flip, place)
                cp.wait_send()
                cp.wait_recv()

    grads, lands = [s[2] for s in started], [s[3] for s in started]
    after = list(after) if isinstance(after, (list, tuple)) else [after]
    outs = pl.pallas_call(
        body, name=name, in_specs=[HBM] * (2 * n) + [SEM] * (2 * n) + [ANY] * len(after), out_specs=[HBM] * (2 * n),
        out_shape=[pltpu.HBM(a.shape, a.dtype) for a in grads + lands],
        input_output_aliases={a: a for a in range(2 * n)},
        compiler_params=pltpu.CompilerParams(has_side_effects=IN_FLIGHT),
    )(*grads, *lands, *[s[0] for s in started], *[s[1] for s in started], *after)
    return list(zip(outs[:n], outs[n:]))


def _sibling_copy(src, dst, send, recv, place):
    x, y, c = place
    return pltpu.make_async_remote_copy(src_ref=src, dst_ref=dst, send_sem=send, recv_sem=recv,
                                        device_id=(x, y, 1 - c), device_id_type=MESH)


def _swap_start(arrays, *, name):
    n = len(arrays)

    def body(*refs):
        ins, lands, send, recv, token = refs[:n], refs[n:2 * n], refs[2 * n], refs[2 * n + 1], refs[-1]
        place = _place()
        for a in range(n):
            _sibling_copy(ins[a], lands[a], send.at[a], recv.at[a], place).start()
        token[...] = jnp.zeros_like(token)

    both = [_in_hbm(a) for a in arrays] + [_in_hbm(lax.empty(a.shape, a.dtype)) for a in arrays]
    outs = pl.pallas_call(
        body, name=name, in_specs=[HBM] * (2 * n),
        out_specs=(SEM, SEM, *[HBM] * (2 * n), pl.BlockSpec(memory_space=pltpu.VMEM)),
        out_shape=(pltpu.SemaphoreType.DMA((n,)), pltpu.SemaphoreType.DMA((n,)),
                   *[pltpu.HBM(a.shape, a.dtype) for a in both], jax.ShapeDtypeStruct((8, 128), F32)),
        input_output_aliases={a: 2 + a for a in range(2 * n)},
        compiler_params=pltpu.CompilerParams(has_side_effects=IN_FLIGHT),
    )(*both)
    return outs[0], outs[1], list(outs[2:2 + n]), list(outs[2 + n:2 + 2 * n]), outs[-1]


def _swap_wait(started, after, *, name):
    send, recv, arrays, lands = started[:4]
    n = len(arrays)

    def body(*refs):
        ins, zones, send_ref, recv_ref = refs[:n], refs[n:2 * n], refs[2 * n], refs[2 * n + 1]
        place = _place()
        for a in range(n):
            cp = _sibling_copy(ins[a], zones[a], send_ref.at[a], recv_ref.at[a], place)
            cp.wait_send()
            cp.wait_recv()

    after = list(after) if isinstance(after, (list, tuple)) else [after]
    outs = pl.pallas_call(
        body, name=name, in_specs=[HBM] * (2 * n) + [SEM, SEM] + [ANY] * len(after), out_specs=[HBM] * (2 * n),
        out_shape=[pltpu.HBM(a.shape, a.dtype) for a in arrays + lands],
        input_output_aliases={a: a for a in range(2 * n)},
        compiler_params=pltpu.CompilerParams(has_side_effects=IN_FLIGHT),
    )(*arrays, *lands, send, recv, *after)
    return list(outs[:n]), list(outs[n:])


def _allreduce_start(packed, *, name):
    n_dev = 8

    def body(src_ref, land_ref, send, recv, src_thru, land_thru, token):
        x, y, c = _place()
        me = 4 * x + 2 * y + c
        for p in range(1, n_dev):
            pltpu.make_async_remote_copy(
                src_ref=src_ref, dst_ref=land_ref.at[me], send_sem=send.at[p - 1], recv_sem=recv.at[p - 1],
                device_id=(x ^ (p >> 2), y ^ ((p >> 1) & 1), c ^ (p & 1)), device_id_type=MESH).start()
        token[...] = jnp.zeros_like(token)

    land = lax.empty((n_dev,) + packed.shape, packed.dtype)
    return pl.pallas_call(
        body, name=name, in_specs=[HBM, HBM],
        out_specs=(SEM, SEM, HBM, HBM, pl.BlockSpec(memory_space=pltpu.VMEM)),
        out_shape=(pltpu.SemaphoreType.DMA((n_dev - 1,)), pltpu.SemaphoreType.DMA((n_dev - 1,)),
                   pltpu.HBM(packed.shape, packed.dtype), pltpu.HBM(land.shape, land.dtype),
                   jax.ShapeDtypeStruct((8, 128), F32)),
        input_output_aliases={0: 2, 1: 3},
        compiler_params=pltpu.CompilerParams(has_side_effects=IN_FLIGHT),
    )(_in_hbm(packed), _in_hbm(land))


def _allreduce_wait(started, after, *, name):
    send, recv, packed, land = started[:4]
    n_dev = 8

    def body(src_ref, land_ref, send_ref, recv_ref, *_):
        x, y, c = _place()
        for p in range(1, n_dev):
            cp = pltpu.make_async_remote_copy(
                src_ref=src_ref, dst_ref=land_ref.at[0], send_sem=send_ref.at[p - 1], recv_sem=recv_ref.at[p - 1],
                device_id=(x ^ (p >> 2), y ^ ((p >> 1) & 1), c ^ (p & 1)), device_id_type=MESH)
            cp.wait_send()
            cp.wait_recv()

    after = list(after) if isinstance(after, (list, tuple)) else [after]
    return pl.pallas_call(
        body, name=name, in_specs=[HBM, HBM, SEM, SEM] + [ANY] * len(after), out_specs=[HBM, HBM],
        out_shape=[pltpu.HBM(packed.shape, packed.dtype), pltpu.HBM(land.shape, land.dtype)],
        input_output_aliases={0: 0, 1: 1},
        compiler_params=pltpu.CompilerParams(has_side_effects=IN_FLIGHT),
    )(packed, land, send, recv, *after)


def _sum_devices(mine, land, *, name):
    n_dev = land.shape[0]

    def body(mine_ref, land_ref, out_ref):
        x, y, c = _place()
        me = 4 * x + 2 * y + c
        total = None
        for s in range(n_dev):
            part = jnp.where(me == s, mine_ref[...], land_ref[s])
            total = part if total is None else total + part
        out_ref[...] = total

    return pl.pallas_call(body, name=name, out_shape=jax.ShapeDtypeStruct(mine.shape, mine.dtype))(mine, land)


def _sum_received(grad, land, *, name, tr=256):
    _, r, c = grad.shape
    tr = _pick(r, tr, 8)

    def body(chip_ref, g_ref, l_ref, o_ref):
        o_ref[...] = ((g_ref[...] + l_ref[0].astype(F32)) + l_ref[1].astype(F32)) + l_ref[2].astype(F32)

    chip = (2 * lax.axis_index("x") + lax.axis_index("y")).astype(jnp.int32).reshape(1)
    return pl.pallas_call(
        body, name=name,
        grid_spec=pltpu.PrefetchScalarGridSpec(
            num_scalar_prefetch=1, grid=(r // tr,),
            in_specs=[pl.BlockSpec((None, tr, c), lambda i, chip_ref: (chip_ref[0], i, 0)),
                      pl.BlockSpec((3, tr, c), lambda i, chip_ref: (0, i, 0))],
            out_specs=pl.BlockSpec((tr, c), lambda i, chip_ref: (i, 0))),
        out_shape=jax.ShapeDtypeStruct((r, c), F32), compiler_params=_params("parallel"),
    )(chip, grad, land)


def _packed_rows(size, d):
    return -(-size // (8 * d)) * 8


def _pack_rows(arrays, d):
    rows = []
    for arr in arrays:
        flat = arr.reshape(-1).astype(F32)
        n = _packed_rows(flat.shape[0], d)
        rows.append(jnp.pad(flat, (0, n * d - flat.shape[0])).reshape(n, d))
    return jnp.concatenate(rows, axis=0)


def _unpack_rows(packed, shapes, d):
    out, row = [], 0
    for shape in shapes:
        size = math.prod(shape)
        n = _packed_rows(size, d)
        out.append(packed[row:row + n].reshape(-1)[:size].reshape(shape))
        row += n
    return out


SMALL = ("norm1_g", "gate_b", "conv_b", "conv_norm_g", "q_norm_g", "k_norm_g", "norm2_g", "ffn_conv_b")
LARGE = ("w_in", "w_conv_out", "w_attn_out", "w_out", "w_up", "w_down")
WEIGHTS = ("norm1_g", "w_in", "gate_b", "conv_w", "conv_b", "conv_norm_g", "w_conv_out", "q_norm_g", "k_norm_g",
           "w_attn_out", "w_out", "norm2_g", "w_up", "ffn_conv_w", "ffn_conv_b", "w_down")


def _after(vec, token):
    return vec if token is None else vec + token[0:1, 0:1]


def _local_step(dims, x, target, small, first_weights, other_weights, send_grad):
    d, f, heads = dims.d_model, dims.d_ff, dims.n_heads
    small = dict(small)
    row = lambda name: small[name].reshape(1, -1)
    head_sum = _head_sum_matrix(dims)
    head_spread = jnp.transpose(head_sum)
    ones = (head_sum, head_spread)
    gq = jnp.tile(row("q_norm_g"), (1, heads))
    gk = jnp.tile(row("k_norm_g"), (1, heads))
    one_shard = lambda w: w.reshape(1, -1, w.shape[-1])

    h = _rmsnorm_fwd(x, row("norm1_g"), name="norm1")
    full = first_weights(h)
    w_in = full["w_in"]
    conv_w = jnp.pad(full["conv_w"], ((0, CONV_HALO - dims.conv_width), (0, 0)))
    ffn_w = jnp.pad(full["ffn_conv_w"], ((0, FFN_HALO - dims.ffn_conv_width), (0, 0)))
    z = _mm_nn(h, w_in, out_dtype=BF16, after=full.get("token"), tm=2048, tn=1792, name="in_proj")
    a1, a3 = _conv_branch_fwd(z, conv_w, row("conv_b"), row("conv_norm_g"), dims, name="conv_branch")
    qkv = _qkv_layouts_fwd(z, gq, gk, ones, dims, name="qk_norm")
    per_group = {dil: _attn_fwd(*qkv[dil], dims, dil, name=f"attn_fwd_d{dil}") for dil in DILATIONS}
    o, lse = _attn_combine(per_group, head_spread, dims, name="attn_combine")
    full = other_weights(o)
    w_up = full["w_up"]
    w_co, w_ao, w_o, w_dn = (one_shard(full[k]) for k in ("w_conv_out", "w_attn_out", "w_out", "w_down"))
    ya = _mm_nn(a3, w_co, out_dtype=F32, name="conv_out_proj")
    yb = _mm_nn(o, w_ao, out_dtype=F32, name="attn_out_proj")
    mixed = _mix_fwd(ya, yb, z, row("gate_b"), dims, name="gate_mix")
    x1, h2 = _proj_residual_norm(mixed, w_o, x, row("norm2_g"), name="out_proj_norm2")
    up = _mm_nn(h2, w_up, out_dtype=F32, tm=2048, name="up_proj")
    act = _ffn_act_fwd(up, ffn_w, row("ffn_conv_b"), dims, name="ffn_act")
    dy, dy_b, loss = _proj_residual_loss(act, w_dn, x1, target, tm=512, name="down_proj_loss")

    grads = {}

    def large(name, g):
        grads[name], g_bf16 = g
        return send_grad(name, g_bf16)

    sent = large("w_down", _mm_tn(act, dy_b, n_shards=1, name="dw_down"))
    dact = _mm_nt(dy_b, w_dn, out_dtype=BF16, after=sent, name="d_act")
    dup, dfw, dfb = _ffn_bwd(dact, up, ffn_w, row("ffn_conv_b"), dims, name="ffn_bwd")
    grads["ffn_conv_w"], grads["ffn_conv_b"] = dfw[:dims.ffn_conv_width], dfb
    sent = large("w_up", _mm_tn(h2, dup, n_shards=N_CHIPS, name="dw_up"))
    dh2 = _mm_nt(dup, w_up, out_dtype=F32, after=sent, name="d_h2")
    dx1, dx1_b, grads["norm2_g"] = _rmsnorm_bwd(x1, row("norm2_g"), dh2, dy, want_bf16=True, name="norm2_bwd")
    sent = large("w_out", _mm_tn(mixed, dx1_b, n_shards=1, name="dw_out"))
    dmix = _mm_nt(dx1_b, w_o, out_dtype=F32, after=sent, name="d_mix")
    dya, dyb, dz_gate, grads["gate_b"] = _mix_bwd(dmix, ya, yb, z, row("gate_b"), dims, name="gate_mix_bwd")
    sent = large("w_attn_out", _mm_tn(o, dyb, n_shards=1, name="dw_attn_out"))
    do = _mm_nt(dyb, w_ao, out_dtype=BF16, after=sent, name="d_attn")
    dos, deltas = _attn_bwd_prep(do, o, head_sum, dims, name="attn_bwd_prep")
    dqkv = {dil: _attn_bwd(*qkv[dil], dos[dil], lse[dil], deltas[dil], dims, dil, name=f"attn_bwd_d{dil}")
            for dil in DILATIONS}
    dz_qkv, dgq, dgk = _qkv_layouts_bwd(z, dqkv, gq, gk, ones, dims, name="qk_norm_bwd")
    grads["q_norm_g"] = dgq.reshape(heads, dims.head_dim).sum(axis=0)
    grads["k_norm_g"] = dgk.reshape(heads, dims.head_dim).sum(axis=0)
    sent = large("w_conv_out", _mm_tn(a3, dya, n_shards=1, name="dw_conv_out"))
    da3 = _mm_nt(dya, w_co, out_dtype=F32, after=sent, name="d_conv_act")
    da1, grads["conv_norm_g"] = _conv_norm_bwd(da3, a1, row("conv_norm_g"), name="conv_norm_bwd")
    dz, dcw, grads["conv_b"] = _conv_branch_bwd(da1, z, conv_w, [dz_qkv, dz_gate], dims, name="conv_branch_bwd")
    grads["conv_w"] = dcw[:dims.conv_width]
    sent = large("w_in", _mm_tn(h, dz, n_shards=N_CHIPS, name="dw_in"))
    dh = _mm_nt(dz, w_in, out_dtype=F32, after=sent, name="d_h")
    dx, grads["norm1_g"] = _rmsnorm_bwd(x, row("norm1_g"), dh, dx1, want_bf16=False, name="norm1_bwd")
    return loss, dx, grads


def _step(dims, x, target, w, m, v):
    d = dims.d_model
    t = dims.tokens
    sq = lambda a: a.reshape(a.shape[1:])
    w2, m2, v2 = ({k: sq(a) for k, a in grp.items()} for grp in (w, m, v))

    conv_pad = jnp.pad(w2["conv_w"], ((0, CONV_HALO - dims.conv_width), (0, 0)))
    ffn_pad = jnp.pad(w2["ffn_conv_w"], ((0, FFN_HALO - dims.ffn_conv_width), (0, 0)))
    first_names = ("w_in", "conv_w", "ffn_conv_w")
    other_names = tuple(k for k in LARGE if k not in first_names)
    lands = dict(zip(first_names, _cast_to_lands([w2["w_in"], conv_pad, ffn_pad], [BF16, F32, F32], name="cast_first")))
    first = _gather_start([lands[k] for k in first_names], x, halved=(0,), name="gather_start_first")
    lands.update(zip(other_names, _cast_to_lands([w2[k] for k in other_names], [BF16] * len(other_names),
                                                 after=first[3], name="cast_other")))
    other = []
    cols = lambda g, rows: jnp.moveaxis(g, 0, 1).reshape(g.shape[1], -1)[:rows]

    def first_weights(after):
        got = dict(zip(first_names, _gather_wait(*first[:3], [after] + [lands[k] for k in other_names], halved=(0,),
                                                 name="gather_wait_first")))
        got["w_in"] = _forward_to_sibling(got["w_in"], name="forward_w_in")
        other.extend(_gather_start([lands[k] for k in other_names], got["w_in"], name="gather_start_other"))
        got["conv_w"] = cols(got["conv_w"], dims.conv_width)
        got["ffn_conv_w"] = cols(got["ffn_conv_w"], dims.ffn_conv_width)
        got["token"] = other[3]
        return got

    def other_weights(after):
        return dict(zip(other_names, _gather_wait(*other[:3], after, name="gather_wait_other")))

    started = {}

    def send_grad(name, g):
        send, recv, g_thru, land, token = _scatter_start(g.reshape(N_CHIPS, -1, g.shape[-1]), name=f"scatter_start_{name}")
        started[name] = (send, recv, g_thru, land)
        return token

    small = {k: w2[k] for k in SMALL}
    small["norm1_g"] = _after(small["norm1_g"].reshape(1, -1), first[3])
    loss, dx, grads = _local_step(dims, x.reshape(t, d), target.reshape(t, d), small, first_weights, other_weights, send_grad)

    def my_sums(names, after, tag):
        arrived = _scatter_wait([started[k] for k in names], after, name=f"scatter_wait_{tag}")
        blocks = [grads[k].reshape(N_CHIPS, -1, grads[k].shape[-1]) for k in names]
        return [_sum_received(g, land, name=f"sum_{k}") for k, g, (_, land) in zip(names, blocks, arrived)]

    def updates(names, mine, theirs):
        return {k: _adamw(w2[k], [a, b], m2[k], v2[k], name=f"adamw_{k}") for k, a, b in zip(names, mine, theirs)}

    small_names = SMALL + ("conv_w", "ffn_conv_w")
    packed = _pack_rows([grads[k] for k in small_names] + [loss[0, 0]], d)
    reducing = _allreduce_start(packed, name="allreduce_start")
    others = [k for k in LARGE if k != "w_in"]
    mine_others = my_sums(others, [dx, reducing[4]], "others")
    swapping_others = _swap_start(mine_others, name="swap_start_others")
    mine_w_in = my_sums(["w_in"], swapping_others[4], "w_in")
    swapping_w_in = _swap_start(mine_w_in, name="swap_start_w_in")
    out = updates(others, *_swap_wait(swapping_others, swapping_w_in[4], name="swap_wait_others"))
    last_updates = [out[k][1] for k in others]
    reduced = _sum_devices(*_allreduce_wait(reducing, last_updates, name="allreduce_wait"), name="allreduce_sum")
    shapes = [grads[k].shape for k in small_names] + [()]
    *small_g, loss_total = _unpack_rows(reduced, shapes, d)
    small_g = dict(zip(small_names, small_g))
    chip = 2 * lax.axis_index("x") + lax.axis_index("y")
    for k in ("conv_w", "ffn_conv_w"):
        width = w2[k].shape[1]
        small_g[k] = lax.dynamic_slice_in_dim(small_g[k], chip * width, width, axis=1)

    small_shapes = [w2[k].shape for k in small_names]
    pack = lambda grp: _pack_rows([grp[k] for k in small_names], d)
    results = _adamw(pack(w2), [pack(small_g)], pack(m2), pack(v2), name="adamw_small")
    unpacked = [_unpack_rows(r, small_shapes, d) for r in results]
    out.update({k: tuple(u[i] for u in unpacked) for i, k in enumerate(small_names)})
    out.update(updates(["w_in"], *_swap_wait(swapping_w_in, results[1], name="swap_wait_w_in")))

    lead = lambda a: a.reshape((1,) + a.shape)
    ordered = [[lead(out[k][j].reshape(w2[k].shape)) for k in WEIGHTS] for j in range(4)]
    return (loss_total, dx.reshape(x.shape), *ordered[0], *ordered[1], *ordered[2], *ordered[3])


def kernel(x, norm1_g, w_in, gate_b, conv_w, conv_b, conv_norm_g, w_conv_out, q_norm_g, k_norm_g, w_attn_out, w_out, norm2_g, w_up, ffn_conv_w, ffn_conv_b, w_down, loss_target, m_norm1_g, m_w_in, m_gate_b, m_conv_w, m_conv_b, m_conv_norm_g, m_w_conv_out, m_q_norm_g, m_k_norm_g, m_w_attn_out, m_w_out, m_norm2_g, m_w_up, m_ffn_conv_w, m_ffn_conv_b, m_w_down, v_norm1_g, v_w_in, v_gate_b, v_conv_w, v_conv_b, v_conv_norm_g, v_w_conv_out, v_q_norm_g, v_k_norm_g, v_w_attn_out, v_w_out, v_norm2_g, v_w_up, v_ffn_conv_w, v_ffn_conv_b, v_w_down):
    w = dict(zip(WEIGHTS, (norm1_g, w_in, gate_b, conv_w, conv_b, conv_norm_g, w_conv_out, q_norm_g, k_norm_g,
                           w_attn_out, w_out, norm2_g, w_up, ffn_conv_w, ffn_conv_b, w_down)))
    m = dict(zip(WEIGHTS, (m_norm1_g, m_w_in, m_gate_b, m_conv_w, m_conv_b, m_conv_norm_g, m_w_conv_out, m_q_norm_g,
                           m_k_norm_g, m_w_attn_out, m_w_out, m_norm2_g, m_w_up, m_ffn_conv_w, m_ffn_conv_b, m_w_down)))
    v = dict(zip(WEIGHTS, (v_norm1_g, v_w_in, v_gate_b, v_conv_w, v_conv_b, v_conv_norm_g, v_w_conv_out, v_q_norm_g,
                           v_k_norm_g, v_w_attn_out, v_w_out, v_norm2_g, v_w_up, v_ffn_conv_w, v_ffn_conv_b, v_w_down)))
    dims = Dims(d_model=x.shape[-1], batch_local=x.shape[0], seq=x.shape[1], d_ff=w_down.shape[1] * N_CHIPS)
    return _step(dims, x, loss_target, w, m, v)
```

```python
import functools
import math
from typing import NamedTuple

import jax
import jax.numpy as jnp
from jax import lax
from jax.experimental import pallas as pl
from jax.experimental.pallas import tpu as pltpu

F32 = jnp.float32
BF16 = jnp.bfloat16

RMS_EPS = 1e-6
ATTN_BLOCK = 128
DILATIONS = (1, 4, 16)
CONV_HALO = 32
FFN_HALO = 8
ADAM_LR, ADAM_B1, ADAM_B2, ADAM_EPS, ADAM_WD, ADAM_STEP = 0.001, 0.9, 0.999, 1e-08, 0.01, 10
V7X_VMEM_LIMIT_BYTES = 56 * 2 ** 20
N_CHIPS = 4
MESH = pl.DeviceIdType.MESH


class Dims(NamedTuple):
    d_model: int = 1024
    n_heads: int = 16
    head_dim: int = 64
    d_ff: int = 2816
    seq: int = 2048
    batch_local: int = 2
    conv_width: int = 31
    ffn_conv_width: int = 3

    @property
    def tokens(self):
        return self.seq * self.batch_local


def _params(*semantics):
    return pltpu.CompilerParams(dimension_semantics=semantics, vmem_limit_bytes=V7X_VMEM_LIMIT_BYTES)


ANY = pl.BlockSpec(memory_space=pl.ANY)


def _ordered(body, n_inputs, after):
    after = [] if after is None else list(after) if isinstance(after, (list, tuple)) else [after]
    if not after:
        return body, [], []

    def wrapped(*refs):
        return body(*refs[:n_inputs], *refs[n_inputs + len(after):])

    return wrapped, [ANY] * len(after), after


def _pick(n, target, mult=128):
    if n <= target:
        return n
    best = None
    for t in range(mult, target + 1, mult):
        if n % t == 0:
            best = t
    assert best is not None, (n, target, mult)
    return best


def _sigmoid(v):
    return 1.0 / (1.0 + jnp.exp(-v))


def _mm_nn(a, w, *, out_dtype, name, residual=None, after=None, tm=1024, tn=1408, tk=2816):
    m, k = a.shape
    nsh, k2, c = w.shape
    assert k == k2 and a.dtype == BF16 and w.dtype == BF16
    n = nsh * c
    tm, tn, tk = _pick(m, tm, 8), _pick(c, tn), _pick(k, tk)
    nk, cpn = k // tk, c // tn

    def body(*refs):
        if residual is None:
            a_ref, w_ref, o_ref, acc = refs
        else:
            a_ref, w_ref, r_ref, o_ref, acc = refs
        prod = jnp.dot(a_ref[...], w_ref[...], preferred_element_type=F32)

        def finish(total):
            if residual is not None:
                total = total + r_ref[...]
            o_ref[...] = total.astype(out_dtype)

        if nk == 1:
            finish(prod)
        else:
            kk = pl.program_id(2)

            @pl.when(kk == 0)
            def _():
                acc[...] = prod

            @pl.when(kk > 0)
            def _():
                acc[...] += prod

            @pl.when(kk == nk - 1)
            def _():
                finish(acc[...])

    in_specs = [pl.BlockSpec((tm, tk), lambda i, j, kk: (i, kk)),
                pl.BlockSpec((None, tk, tn), lambda i, j, kk: (j // cpn, kk, j % cpn))]
    args = [a, w]
    if residual is not None:
        in_specs.append(pl.BlockSpec((tm, tn), lambda i, j, kk: (i, j)))
        args.append(residual)
    body, more_specs, more_args = _ordered(body, len(args), after)
    return pl.pallas_call(
        body, name=name, grid=(m // tm, n // tn, nk),
        in_specs=in_specs + more_specs, out_specs=pl.BlockSpec((tm, tn), lambda i, j, kk: (i, j)),
        out_shape=jax.ShapeDtypeStruct((m, n), out_dtype),
        scratch_shapes=[pltpu.VMEM((tm, tn) if nk > 1 else (8, 128), F32)],
        compiler_params=_params("parallel", "parallel", "arbitrary"),
    )(*args, *more_args)


def _proj_residual_norm(a, w, residual, g, *, name, tm=1024):
    m, k = a.shape
    _, k2, n = w.shape
    assert w.shape[0] == 1 and k == k2 and a.dtype == BF16 and w.dtype == BF16
    tm = _pick(m, tm, 8)

    def body(a_ref, w_ref, r_ref, g_ref, y_ref, h_ref):
        y = r_ref[...] + jnp.dot(a_ref[...], w_ref[...], preferred_element_type=F32)
        y_ref[...] = y
        h_ref[...] = (y * lax.rsqrt(jnp.mean(y * y, axis=-1, keepdims=True) + RMS_EPS) * g_ref[...]).astype(BF16)

    rows = lambda width: pl.BlockSpec((tm, width), lambda i: (i, 0))
    return pl.pallas_call(
        body, name=name, grid=(m // tm,),
        in_specs=[rows(k), pl.BlockSpec((None, k, n), lambda i: (0, 0, 0)), rows(n), pl.BlockSpec((1, n), lambda i: (0, 0))],
        out_specs=[rows(n), rows(n)],
        out_shape=[jax.ShapeDtypeStruct((m, n), F32), jax.ShapeDtypeStruct((m, n), BF16)],
        compiler_params=_params("parallel"),
    )(a, w, residual, g)


def _proj_residual_loss(a, w, residual, target, *, name, tm=1024):
    m, k = a.shape
    _, k2, n = w.shape
    assert w.shape[0] == 1 and k == k2 and a.dtype == BF16 and w.dtype == BF16
    tm = _pick(m, tm, 8)

    def body(a_ref, w_ref, r_ref, t_ref, dy_ref, dyb_ref, loss_ref):
        err = r_ref[...] + jnp.dot(a_ref[...], w_ref[...], preferred_element_type=F32) - t_ref[...]
        dy = err * (1.0 / n)
        dy_ref[...] = dy
        dyb_ref[...] = dy.astype(BF16)
        part = jnp.sum(jnp.sum(err * err, axis=-1, keepdims=True), axis=0, keepdims=True) * (0.5 / n)
        _accumulate(loss_ref, jnp.broadcast_to(part, (8, 128)), pl.program_id(0) == 0)

    rows = lambda width: pl.BlockSpec((tm, width), lambda i: (i, 0))
    return pl.pallas_call(
        body, name=name, grid=(m // tm,),
        in_specs=[rows(k), pl.BlockSpec((None, k, n), lambda i: (0, 0, 0)), rows(n), rows(n)],
        out_specs=[rows(n), rows(n), pl.BlockSpec((8, 128), lambda i: (0, 0))],
        out_shape=[jax.ShapeDtypeStruct((m, n), F32), jax.ShapeDtypeStruct((m, n), BF16),
                   jax.ShapeDtypeStruct((8, 128), F32)],
        compiler_params=_params("arbitrary"),
    )(a, w, residual, target)


def _mm_nt(a, w, *, out_dtype, name, after=None, tm=1024, tn=1408, tk=1792):
    m, k = a.shape
    nsh, r, c = w.shape
    assert k == nsh * c and a.dtype == BF16 and w.dtype == BF16
    tm, tn, tk = _pick(m, tm, 8), _pick(r, tn), _pick(c, tk)
    nk, cpk = k // tk, c // tk

    def body(a_ref, w_ref, o_ref, acc):
        prod = lax.dot_general(a_ref[...], w_ref[...], (((1,), (1,)), ((), ())), preferred_element_type=F32)
        if nk == 1:
            o_ref[...] = prod.astype(out_dtype)
        else:
            kk = pl.program_id(2)

            @pl.when(kk == 0)
            def _():
                acc[...] = prod

            @pl.when(kk > 0)
            def _():
                acc[...] += prod

            @pl.when(kk == nk - 1)
            def _():
                o_ref[...] = acc[...].astype(out_dtype)

    body, more_specs, more_args = _ordered(body, 2, after)
    return pl.pallas_call(
        body, name=name, grid=(m // tm, r // tn, nk),
        in_specs=[pl.BlockSpec((tm, tk), lambda i, j, kk: (i, kk)),
                  pl.BlockSpec((None, tn, tk), lambda i, j, kk: (kk // cpk, j, kk % cpk))] + more_specs,
        out_specs=pl.BlockSpec((tm, tn), lambda i, j, kk: (i, j)),
        out_shape=jax.ShapeDtypeStruct((m, r), out_dtype),
        scratch_shapes=[pltpu.VMEM((tm, tn) if nk > 1 else (8, 128), F32)],
        compiler_params=_params("parallel", "parallel", "arbitrary"),
    )(a, w, *more_args)


MM_TN_VMEM_BYTES = 44 * 2 ** 20


def _mm_tn(a, b, *, n_shards, name, tm=1408, tn=1408):
    t, m = a.shape
    t2, n = b.shape
    assert t == t2 and a.dtype == BF16 and b.dtype == BF16
    c = n // n_shards
    tm, tn = _pick(m, tm), _pick(c, tn)
    if m // tm == 1 and n // tn == 1 and tn % (2 * LANES) == 0:
        tn //= 2
    fixed = 2 * tm * tn * 6
    if 4 * t * (tm + tn) + fixed <= MM_TN_VMEM_BYTES:
        tk = t
    else:
        tk = _pick(t, (MM_TN_VMEM_BYTES - fixed - 4 * tm * tn) // (4 * (tm + tn)), 8)
    nk, cpn = t // tk, c // tn

    def body(a_ref, b_ref, o_ref, ob_ref, acc):
        kk = pl.program_id(2)
        prod = lax.dot_general(a_ref[...], b_ref[...], (((0,), (0,)), ((), ())), preferred_element_type=F32)

        def finish(total):
            o_ref[...] = total
            ob_ref[...] = total.astype(BF16)

        if nk == 1:
            finish(prod)
        else:
            @pl.when(kk == 0)
            def _():
                acc[...] = prod

            @pl.when(kk > 0)
            def _():
                acc[...] += prod

            @pl.when(kk == nk - 1)
            def _():
                finish(acc[...])

    out_spec = pl.BlockSpec((None, tm, tn), lambda i, j, kk: (j // cpn, i, j % cpn))
    return pl.pallas_call(
        body, name=name, grid=(m // tm, n // tn, nk),
        in_specs=[pl.BlockSpec((tk, tm), lambda i, j, kk: (kk, i)),
                  pl.BlockSpec((tk, tn), lambda i, j, kk: (kk, j))],
        out_specs=[out_spec, out_spec],
        out_shape=[jax.ShapeDtypeStruct((n_shards, m, c), F32), jax.ShapeDtypeStruct((n_shards, m, c), BF16)],
        scratch_shapes=[pltpu.VMEM((tm, tn) if nk > 1 else (8, 128), F32)],
        compiler_params=_params("parallel", "parallel", "arbitrary"),
    )(a, b)


def _row_spec(tr, width, col=0):
    return pl.BlockSpec((tr, width), lambda i, col=col: (i, col))


def _vec_spec(width, col=0):
    return pl.BlockSpec((1, width), lambda i, col=col: (0, col))


def _accumulate(ref, value, first):
    @pl.when(first)
    def _():
        ref[...] = value

    @pl.when(jnp.logical_not(first))
    def _():
        ref[...] += value


def _rmsnorm_fwd(x, g, *, name, tr=512):
    t, d = x.shape
    tr = _pick(t, tr, 8)

    def body(x_ref, g_ref, o_ref):
        xv = x_ref[...]
        r = lax.rsqrt(jnp.mean(xv * xv, axis=-1, keepdims=True) + RMS_EPS)
        o_ref[...] = (xv * r * g_ref[...]).astype(BF16)

    return pl.pallas_call(
        body, name=name, grid=(t // tr,),
        in_specs=[_row_spec(tr, d), _vec_spec(d)], out_specs=_row_spec(tr, d),
        out_shape=jax.ShapeDtypeStruct((t, d), BF16), compiler_params=_params("parallel"),
    )(x, g)


def _rmsnorm_bwd(x, g, dy, dres, *, name, want_bf16, tr=512):
    t, d = x.shape
    tr = _pick(t, tr, 8)

    def body(x_ref, g_ref, dy_ref, dres_ref, *outs):
        dx_ref, dg_ref = outs[0], outs[-1]
        xv, dyv = x_ref[...], dy_ref[...].astype(F32)
        r = lax.rsqrt(jnp.mean(xv * xv, axis=-1, keepdims=True) + RMS_EPS)
        gy = dyv * g_ref[...]
        dx = dres_ref[...] + r * gy - xv * (r * r * r) * jnp.mean(xv * gy, axis=-1, keepdims=True)
        dx_ref[...] = dx
        if want_bf16:
            outs[1][...] = dx.astype(BF16)
        _accumulate(dg_ref, jnp.sum(dyv * xv * r, axis=0, keepdims=True), pl.program_id(0) == 0)

    out_shape = [jax.ShapeDtypeStruct((t, d), F32)]
    out_specs = [_row_spec(tr, d)]
    if want_bf16:
        out_shape.append(jax.ShapeDtypeStruct((t, d), BF16))
        out_specs.append(_row_spec(tr, d))
    out_shape.append(jax.ShapeDtypeStruct((1, d), F32))
    out_specs.append(_vec_spec(d))
    return pl.pallas_call(
        body, name=name, grid=(t // tr,),
        in_specs=[_row_spec(tr, d), _vec_spec(d), _row_spec(tr, d), _row_spec(tr, d)],
        out_specs=out_specs, out_shape=out_shape, compiler_params=_params("arbitrary"),
    )(x, g, dy, dres)


CONV_ROWS = 16


def _seq_specs(dims, ts, width, halo, col, *, nxt=False):
    nst, per = dims.seq // ts, ts // halo
    last = dims.tokens // halo - 1
    cur = pl.BlockSpec((ts, width), lambda b, i: (b * nst + i, col))
    if nxt:
        edge = pl.BlockSpec((halo, width), lambda b, i: (jnp.minimum((b * nst + i + 1) * per, last), col))
    else:
        edge = pl.BlockSpec((halo, width), lambda b, i: (jnp.maximum((b * nst + i) * per - 1, 0), col))
    return cur, edge


SUBLANES = 8


def _shifted_copies(buf, shifted):
    rows = shifted.shape[1]
    for s in range(1, SUBLANES):
        shifted[s - 1] = buf[pl.ds(s, rows), :]


def _window(buf, shifted, start, size):
    a, s = divmod(start, SUBLANES)
    src = buf if s == 0 else shifted.at[s - 1]
    return src[pl.ds(SUBLANES * a, size), :]


def _conv_branch_fwd(z, w, b, g, dims, *, name, ts=128):
    t, c, kw = z.shape[0], dims.d_model, dims.conv_width
    base = CONV_HALO - (kw - 1)

    def body(av_ref, hv_ref, ag_ref, hg_ref, w_ref, b_ref, g_ref, a1_ref, a3_ref, buf, shifted):
        i = pl.program_id(1)
        buf[CONV_HALO:, :] = av_ref[...].astype(F32) * _sigmoid(ag_ref[...].astype(F32))
        buf[0:CONV_HALO, :] = jnp.where(i > 0, hv_ref[...].astype(F32) * _sigmoid(hg_ref[...].astype(F32)), 0.0)
        _shifted_copies(buf, shifted)
        for r0 in range(0, ts, CONV_ROWS):
            acc = jnp.broadcast_to(b_ref[...], (CONV_ROWS, c))
            for k in range(kw):
                acc = acc + w_ref[k:k + 1, :] * _window(buf, shifted, r0 + base + k, CONV_ROWS)
            a1_ref[r0:r0 + CONV_ROWS, :] = acc
            a2 = acc * lax.rsqrt(jnp.mean(acc * acc, axis=-1, keepdims=True) + RMS_EPS) * g_ref[...]
            a3_ref[r0:r0 + CONV_ROWS, :] = (a2 * _sigmoid(a2)).astype(BF16)

    vec = pl.BlockSpec((1, c), lambda b, i: (0, 0))
    out = pl.BlockSpec((ts, c), lambda b, i: (b * (dims.seq // ts) + i, 0))
    return pl.pallas_call(
        body, name=name, grid=(dims.batch_local, dims.seq // ts),
        in_specs=[*_seq_specs(dims, ts, c, CONV_HALO, 0), *_seq_specs(dims, ts, c, CONV_HALO, 1),
                  pl.BlockSpec((CONV_HALO, c), lambda b, i: (0, 0)), vec, vec],
        out_specs=[out, out],
        out_shape=[jax.ShapeDtypeStruct((t, c), F32), jax.ShapeDtypeStruct((t, c), BF16)],
        scratch_shapes=[pltpu.VMEM((CONV_HALO + ts, c), F32),
                        pltpu.VMEM((SUBLANES - 1, CONV_HALO + ts - SUBLANES, c), F32)],
        compiler_params=_params("parallel", "parallel"),
    )(z, z, z, z, w, b, g)


def _conv_norm_bwd(da3, a1, g, *, name, tr=256):
    t, c = a1.shape
    tr = _pick(t, tr, 8)

    def body(d_ref, a_ref, g_ref, o_ref, dg_ref):
        a1v, gv = a_ref[...], g_ref[...]
        r = lax.rsqrt(jnp.mean(a1v * a1v, axis=-1, keepdims=True) + RMS_EPS)
        a2 = a1v * r * gv
        sg = _sigmoid(a2)
        da2 = d_ref[...].astype(F32) * sg * (1.0 + a2 * (1.0 - sg))
        gy = da2 * gv
        o_ref[...] = r * gy - a1v * (r * r * r) * jnp.mean(a1v * gy, axis=-1, keepdims=True)
        _accumulate(dg_ref, jnp.sum(da2 * a1v * r, axis=0, keepdims=True), pl.program_id(0) == 0)

    return pl.pallas_call(
        body, name=name, grid=(t // tr,),
        in_specs=[_row_spec(tr, c), _row_spec(tr, c), _vec_spec(c)],
        out_specs=[_row_spec(tr, c), _vec_spec(c)],
        out_shape=[jax.ShapeDtypeStruct((t, c), F32), jax.ShapeDtypeStruct((1, c), F32)],
        compiler_params=_params("arbitrary"),
    )(da3, a1, g)


def _conv_branch_bwd(da1, z, w, rest_of_dz, dims, *, name, ts=128):
    t, c, kw = z.shape[0], dims.d_model, dims.conv_width
    nst = dims.seq // ts
    base = CONV_HALO - (kw - 1)
    n_rest = len(rest_of_dz)
    total = 2 * c + sum(r.shape[1] for r in rest_of_dz)

    def body(d_ref, dn_ref, av_ref, hv_ref, ag_ref, hg_ref, w_ref, *more):
        rest_refs = more[:n_rest]
        dz_ref, dw_ref, db_ref, abuf, dbuf, ashift, dshift = more[n_rest:]
        col = 2 * c
        for r in rest_refs:
            dz_ref[:, col:col + r.shape[1]] = r[...]
            col += r.shape[1]
        i = pl.program_id(1)
        first = jnp.logical_and(pl.program_id(0) == 0, i == 0)
        abuf[CONV_HALO:, :] = av_ref[...].astype(F32) * _sigmoid(ag_ref[...].astype(F32))
        abuf[0:CONV_HALO, :] = jnp.where(i > 0, hv_ref[...].astype(F32) * _sigmoid(hg_ref[...].astype(F32)), 0.0)
        d1 = d_ref[...]
        dbuf[0:ts, :] = d1
        dbuf[ts:, :] = jnp.where(i < nst - 1, dn_ref[...], 0.0)
        _shifted_copies(abuf, ashift)
        _shifted_copies(dbuf, dshift)

        @pl.when(first)
        def _():
            dw_ref[...] = jnp.zeros_like(dw_ref)
            db_ref[...] = jnp.zeros_like(db_ref)

        db_ref[...] += jnp.sum(d1, axis=0, keepdims=True)
        for k in range(kw):
            dw_ref[k:k + 1, :] += jnp.sum(d1 * _window(abuf, ashift, base + k, ts), axis=0, keepdims=True)
        for r0 in range(0, ts, CONV_ROWS):
            acc = jnp.zeros((CONV_ROWS, c), F32)
            for k in range(kw):
                acc = acc + w_ref[k:k + 1, :] * _window(dbuf, dshift, r0 + (kw - 1) - k, CONV_ROWS)
            av = av_ref[r0:r0 + CONV_ROWS, :].astype(F32)
            sg = _sigmoid(ag_ref[r0:r0 + CONV_ROWS, :].astype(F32))
            dz_ref[r0:r0 + CONV_ROWS, 0:c] = (acc * sg).astype(BF16)
            dz_ref[r0:r0 + CONV_ROWS, c:2 * c] = (acc * av * sg * (1.0 - sg)).astype(BF16)

    cur, nxt = _seq_specs(dims, ts, c, CONV_HALO, 0, nxt=True)
    return pl.pallas_call(
        body, name=name, grid=(dims.batch_local, nst),
        in_specs=[cur, nxt, *_seq_specs(dims, ts, c, CONV_HALO, 0), *_seq_specs(dims, ts, c, CONV_HALO, 1),
                  pl.BlockSpec((CONV_HALO, c), lambda b, i: (0, 0))]
        + [pl.BlockSpec((ts, r.shape[1]), lambda b, i: (b * nst + i, 0)) for r in rest_of_dz],
        out_specs=[pl.BlockSpec((ts, total), lambda b, i: (b * nst + i, 0)),
                   pl.BlockSpec((CONV_HALO, c), lambda b, i: (0, 0)), pl.BlockSpec((1, c), lambda b, i: (0, 0))],
        out_shape=[jax.ShapeDtypeStruct((t, total), BF16), jax.ShapeDtypeStruct((CONV_HALO, c), F32),
                   jax.ShapeDtypeStruct((1, c), F32)],
        scratch_shapes=[pltpu.VMEM((CONV_HALO + ts, c), F32)] * 2
        + [pltpu.VMEM((SUBLANES - 1, CONV_HALO + ts - SUBLANES, c), F32)] * 2,
        compiler_params=_params("arbitrary", "arbitrary"),
    )(da1, da1, z, z, z, z, w, *rest_of_dz)


FFN_ROWS = 16
FFN_COLS = 256


def _ffn_chunks(ts, f):
    cw = _pick(f, FFN_COLS)
    return [(r0, c0, cw) for r0 in range(0, ts, FFN_ROWS) for c0 in range(0, f, cw)]


def _tap_sources(buf, moved, offsets, rows):
    taps, used = [], 0
    for off in offsets:
        if off % SUBLANES:
            moved[used] = buf[pl.ds(off, rows), :]
            taps.append((moved.at[used], 0))
            used += 1
        else:
            taps.append((buf, off))
    return taps


def _moved_copies(offsets):
    return sum(1 for off in offsets if off % SUBLANES)


def _taps_sum(taps, w_ref, init, r0, cols):
    for k, (src, off) in enumerate(taps):
        init = init + w_ref[k:k + 1, cols] * src[pl.ds(off + r0, init.shape[0]), cols]
    return init


def _ffn_bwd(dact, up, w, b, dims, *, name, ts=128):
    t, f, kw = up.shape[0], dims.d_ff, dims.ffn_conv_width
    nst = dims.seq // ts
    fwd_offsets = [FFN_HALO - (kw - 1) + k for k in range(kw)]
    bwd_offsets = [(kw - 1) - k for k in range(kw)]
    dact_halo = 2 * FFN_HALO

    def body(d_ref, dn_ref, up_ref, hp_ref, hn_ref, w_ref, b_ref, o_ref, dw_ref, db_ref, buf, moved, dbuf, dmoved):
        i = pl.program_id(1)
        first = jnp.logical_and(pl.program_id(0) == 0, i == 0)
        more = i < nst - 1
        buf[0:FFN_HALO, :] = jnp.where(i > 0, hp_ref[...], 0.0)
        buf[FFN_HALO:FFN_HALO + ts, :] = up_ref[...]
        buf[FFN_HALO + ts:, :] = hn_ref[...]
        taps = _tap_sources(buf, moved, fwd_offsets, ts + FFN_HALO)

        def du_chunk(r0, rows, c0, cw, d):
            vcols, gcols = slice(c0, c0 + cw), slice(f + c0, f + c0 + cw)
            uv = _taps_sum(taps, w_ref, jnp.broadcast_to(b_ref[:, vcols], (rows, cw)), r0, vcols)
            ug = _taps_sum(taps, w_ref, jnp.broadcast_to(b_ref[:, gcols], (rows, cw)), r0, gcols)
            sg = _sigmoid(ug)
            dbuf[r0:r0 + rows, vcols] = d * ug * sg
            dbuf[r0:r0 + rows, gcols] = d * uv * sg * (1.0 + ug * (1.0 - sg))

        for r0, c0, cw in _ffn_chunks(ts, f):
            du_chunk(r0, FFN_ROWS, c0, cw, d_ref[r0:r0 + FFN_ROWS, c0:c0 + cw].astype(F32))
        for _, c0, cw in _ffn_chunks(FFN_ROWS, f):
            d_next = dn_ref[:, c0:c0 + cw].astype(F32)[0:FFN_HALO]
            du_chunk(ts, FFN_HALO, c0, cw, jnp.where(more, d_next, 0.0))

        @pl.when(first)
        def _():
            dw_ref[...] = jnp.zeros_like(dw_ref)
            db_ref[...] = jnp.zeros_like(db_ref)

        du = dbuf[0:ts, :]
        db_ref[...] += jnp.sum(du, axis=0, keepdims=True)
        for k, (src, off) in enumerate(taps):
            dw_ref[k:k + 1, :] += jnp.sum(du * src[pl.ds(off, ts), :], axis=0, keepdims=True)

        dtaps = _tap_sources(dbuf, dmoved, bwd_offsets, ts)
        for r0, c0, cw in _ffn_chunks(ts, 2 * f):
            cols = slice(c0, c0 + cw)
            o_ref[r0:r0 + FFN_ROWS, cols] = _taps_sum(dtaps, w_ref, jnp.zeros((FFN_ROWS, cw), F32), r0, cols).astype(BF16)

    up_cur, up_prev = _seq_specs(dims, ts, 2 * f, FFN_HALO, 0)
    _, up_next = _seq_specs(dims, ts, 2 * f, FFN_HALO, 0, nxt=True)
    d_cur, d_next = _seq_specs(dims, ts, f, dact_halo, 0, nxt=True)
    full = lambda rows: pl.BlockSpec((rows, 2 * f), lambda b_, i: (0, 0))
    return pl.pallas_call(
        body, name=name, grid=(dims.batch_local, nst),
        in_specs=[d_cur, d_next, up_cur, up_prev, up_next, full(FFN_HALO), full(1)],
        out_specs=[pl.BlockSpec((ts, 2 * f), lambda b_, i: (b_ * nst + i, 0)), full(FFN_HALO), full(1)],
        out_shape=[jax.ShapeDtypeStruct((t, 2 * f), BF16), jax.ShapeDtypeStruct((FFN_HALO, 2 * f), F32),
                   jax.ShapeDtypeStruct((1, 2 * f), F32)],
        scratch_shapes=[pltpu.VMEM((ts + 2 * FFN_HALO, 2 * f), F32),
                        pltpu.VMEM((_moved_copies(fwd_offsets), ts + FFN_HALO, 2 * f), F32),
                        pltpu.VMEM((ts + FFN_HALO, 2 * f), F32),
                        pltpu.VMEM((_moved_copies(bwd_offsets), ts, 2 * f), F32)],
        compiler_params=_params("arbitrary", "arbitrary"),
    )(dact, dact, up, up, up, w, b)


def _ffn_act_fwd(up, w, b, dims, *, name, ts=128):
    t, f, kw = up.shape[0], dims.d_ff, dims.ffn_conv_width
    offsets = [FFN_HALO - (kw - 1) + k for k in range(kw)]

    def body(up_ref, h_ref, w_ref, b_ref, o_ref, buf, moved):
        buf[FFN_HALO:, :] = up_ref[...]
        buf[0:FFN_HALO, :] = jnp.where(pl.program_id(1) > 0, h_ref[...], 0.0)
        taps = _tap_sources(buf, moved, offsets, ts)
        for r0, c0, cw in _ffn_chunks(ts, f):
            vcols, gcols = slice(c0, c0 + cw), slice(f + c0, f + c0 + cw)
            uv = _taps_sum(taps, w_ref, jnp.broadcast_to(b_ref[:, vcols], (FFN_ROWS, cw)), r0, vcols)
            ug = _taps_sum(taps, w_ref, jnp.broadcast_to(b_ref[:, gcols], (FFN_ROWS, cw)), r0, gcols)
            o_ref[r0:r0 + FFN_ROWS, vcols] = (ug * _sigmoid(ug) * uv).astype(BF16)

    full = lambda rows: pl.BlockSpec((rows, 2 * f), lambda b_, i: (0, 0))
    return pl.pallas_call(
        body, name=name, grid=(dims.batch_local, dims.seq // ts),
        in_specs=[*_seq_specs(dims, ts, 2 * f, FFN_HALO, 0), full(FFN_HALO), full(1)],
        out_specs=pl.BlockSpec((ts, f), lambda b_, i: (b_ * (dims.seq // ts) + i, 0)),
        out_shape=jax.ShapeDtypeStruct((t, f), BF16),
        scratch_shapes=[pltpu.VMEM((FFN_HALO + ts, 2 * f), F32), pltpu.VMEM((_moved_copies(offsets), ts, 2 * f), F32)],
        compiler_params=_params("parallel", "parallel"),
    )(up, up, w, b)


def _dot_nt(a, b):
    return lax.dot_general(a, b, (((1,), (1,)), ((), ())), preferred_element_type=F32)


def _dot_tn(a, b):
    return lax.dot_general(a, b, (((0,), (0,)), ((), ())), preferred_element_type=F32)


LANES = 128
MASK_BIAS = 1e30
RESIDUE_DILATIONS = tuple(d for d in DILATIONS if d > 1)


def _rows_to_residues(value, out_ref, scr, d):
    rows, width = value.shape
    for c in range(width // LANES):
        cols = slice(LANES * c, LANES * (c + 1))
        scr[c] = value[:, cols]
        for r in range(d):
            out_ref[r, :, cols] = scr[c, pl.ds(r, rows // d, stride=d), :].astype(out_ref.dtype)


def _residues_to_rows(in_ref, scr, d):
    _, n, width = in_ref.shape
    slabs = []
    for c in range(width // LANES):
        cols = slice(LANES * c, LANES * (c + 1))
        for r in range(d):
            scr[c, pl.ds(r, n, stride=d), :] = in_ref[r, :, cols].astype(F32)
        slabs.append(scr[c])
    return slabs[0] if len(slabs) == 1 else jnp.concatenate(slabs, axis=1)


def _residue_shape(dims, d, width, dtype):
    return jax.ShapeDtypeStruct((dims.batch_local, d, dims.seq // d, width), dtype)


def _residue_spec(dims, d, tr, width):
    tiles = dims.seq // tr
    return pl.BlockSpec((None, d, tr // d, width), lambda i: (i // tiles, 0, i % tiles, 0))


def _head_sum_matrix(dims):
    a = dims.n_heads * dims.head_dim
    head = jnp.arange(a, dtype=jnp.int32) // dims.head_dim
    return (head[:, None] == jnp.arange(LANES, dtype=jnp.int32)[None, :]).astype(BF16)


def _two_pass_dot(v, m):
    hi = v.astype(BF16)
    lo = (v - hi.astype(F32)).astype(BF16)
    return jnp.dot(hi, m, preferred_element_type=F32) + jnp.dot(lo, m, preferred_element_type=F32)


def _residue_permutations(tr):
    out = []
    for d in RESIDUE_DILATIONS:
        dst = jnp.arange(tr, dtype=jnp.int32)
        src = d * (dst % (tr // d)) + dst // (tr // d)
        out.append((src[:, None] == jnp.arange(tr, dtype=jnp.int32)[None, :]).astype(BF16))
    return out


def _bf16_rows_to_residues(value, out_ref, perm_ref, d):
    n = value.shape[0] // d
    moved = jnp.dot(perm_ref[...], value, preferred_element_type=F32).astype(out_ref.dtype)
    for r in range(d):
        out_ref[r] = moved[r * n:(r + 1) * n]


def _bf16_residues_to_rows(in_ref, back_ref):
    d = in_ref.shape[0]
    stacked = jnp.concatenate([in_ref[r] for r in range(d)], axis=0)
    return jnp.dot(back_ref[...], stacked, preferred_element_type=F32)


def _qkv_layouts_fwd(z, gq, gk, head_ones, dims, *, name, tr=256):
    t = z.shape[0]
    a = dims.n_heads * dims.head_dim
    q_scale = dims.head_dim ** -0.5
    nres = len(RESIDUE_DILATIONS)

    def body(q_ref, k_ref, v_ref, gq_ref, gk_ref, sum_ref, spread_ref, *rest):
        perm_refs, outs = rest[:nres], rest[nres:]
        qv, kv = q_ref[...].astype(F32), k_ref[...].astype(F32)
        mean = lambda val: _two_pass_dot(_two_pass_dot(val, sum_ref[...]), spread_ref[...]) * (1.0 / dims.head_dim)
        rq = lax.rsqrt(mean(qv * qv) + RMS_EPS)
        rk = lax.rsqrt(mean(kv * kv) + RMS_EPS)
        values = ((qv * rq * gq_ref[...] * q_scale).astype(BF16), (kv * rk * gk_ref[...]).astype(BF16), v_ref[...])
        for j, val in enumerate(values):
            outs[j][...] = val
            for g, d in enumerate(RESIDUE_DILATIONS):
                _bf16_rows_to_residues(val, outs[3 * (g + 1) + j], perm_refs[g], d)

    out_specs = [_row_spec(tr, a)] * 3
    out_shape = [jax.ShapeDtypeStruct((t, a), BF16)] * 3
    for d in RESIDUE_DILATIONS:
        out_specs += [_residue_spec(dims, d, tr, a)] * 3
        out_shape += [_residue_shape(dims, d, a, BF16)] * 3
    outs = pl.pallas_call(
        body, name=name, grid=(t // tr,),
        in_specs=[_row_spec(tr, a, 2), _row_spec(tr, a, 3), _row_spec(tr, a, 4), _vec_spec(a), _vec_spec(a),
                  pl.BlockSpec((a, LANES), lambda i: (0, 0)), pl.BlockSpec((LANES, a), lambda i: (0, 0))]
        + [pl.BlockSpec((tr, tr), lambda i: (0, 0))] * nres,
        out_specs=out_specs, out_shape=out_shape,
        compiler_params=_params("parallel"),
    )(z, z, z, gq, gk, *head_ones, *_residue_permutations(tr))
    return {d: tuple(outs[3 * g:3 * g + 3]) for g, d in enumerate((1,) + RESIDUE_DILATIONS)}


ATTN_RESIDUES_PER_STEP = 4
ATTN_RESIDUES_PER_STEP_WINDOWED = 2


def _attn_residues(dims, dil):
    if dil == 1:
        return 1
    one_block = dims.seq // dil == ATTN_BLOCK
    return math.gcd(dil, ATTN_RESIDUES_PER_STEP if one_block else ATTN_RESIDUES_PER_STEP_WINDOWED)


def _per_residue(body, rs):
    if rs == 1:
        return body

    def stepped(*refs):
        for r in range(rs):
            body(*[ref.at[r] for ref in refs])

    return stepped


def _attn_specs(dims, dil, width):
    blk = ATTN_BLOCK
    nb = dims.seq // dil // blk
    rs = _attn_residues(dims, dil)
    if dil == 1:
        grid = (dims.batch_local, nb)
        at = lambda f: pl.BlockSpec((blk, width), lambda b, i: (b * nb + f(i), 0))
    elif rs > 1 and nb == 1:
        grid = (dims.batch_local, dil // rs)
        at = lambda f: pl.BlockSpec((None, rs, blk, width), lambda b, r: (b, r, 0, 0))
    elif rs > 1:
        grid = (dims.batch_local, dil // rs, nb)
        at = lambda f: pl.BlockSpec((None, rs, blk, width), lambda b, r, i: (b, r, f(i), 0))
    else:
        grid = (dims.batch_local, dil, nb)
        at = lambda f: pl.BlockSpec((None, None, blk, width), lambda b, r, i: (b, r, f(i), 0))
    return grid, at(lambda i: i), at(lambda i: jnp.maximum(i - 1, 0)), at(lambda i: jnp.minimum(i + 1, nb - 1))


def _head_slopes(n_heads):
    h = lax.broadcasted_iota(jnp.int32, (n_heads, 1, 1), 0).astype(F32)
    return jnp.exp((h + 1.0) * (-8.0 / n_heads * math.log(2.0)))


def _pair_masks(hd):
    low = lax.broadcasted_iota(jnp.int32, (1, 2 * hd), 1) < hd
    return low, jnp.logical_not(low)


def _attn_fwd(q, k, v, dims, dil, *, name):
    a = dims.n_heads * dims.head_dim
    heads, hd, blk = dims.n_heads, dims.head_dim, ATTN_BLOCK
    assert 2 * hd == LANES and heads % 2 == 0 and heads <= LANES
    nb = dims.seq // dil // blk
    has_prev = nb > 1
    nkeys = 2 * blk if has_prev else blk
    grid, cur, prev, _ = _attn_specs(dims, dil, a)
    _, cur_stat, _, _ = _attn_specs(dims, dil, LANES)

    def body(*refs):
        if has_prev:
            q_ref, kc_ref, vc_ref, kp_ref, vp_ref, o_ref, lse_ref, s_scr, p_scr, k_st, v_st = refs
            k_st[0:blk, :], k_st[blk:, :] = kp_ref[...], kc_ref[...]
            v_st[0:blk, :], v_st[blk:, :] = vp_ref[...], vc_ref[...]
        else:
            q_ref, k_st, v_st, o_ref, lse_ref, s_scr, p_scr = refs
        low, high = _pair_masks(hd)

        for hp in range(heads // 2):
            sl = slice(LANES * hp, LANES * (hp + 1))
            q2 = q_ref[:, sl]
            kcat = k_st[:, sl]
            s_scr[2 * hp] = _dot_nt(jnp.where(low, q2, jnp.zeros_like(q2)), kcat)
            s_scr[2 * hp + 1] = _dot_nt(jnp.where(high, q2, jnp.zeros_like(q2)), kcat)

        iq = lax.broadcasted_iota(jnp.int32, (blk, nkeys), 0)
        jk = lax.broadcasted_iota(jnp.int32, (blk, nkeys), 1)
        if has_prev:
            steps = iq + blk - jk
            valid = (steps >= 0) & (steps <= blk) & ((jk >= blk) | (pl.program_id(len(grid) - 1) > 0))
        else:
            steps = iq - jk
            valid = steps >= 0
        bias = jnp.where(valid, steps.astype(F32) * (-float(dil)), -MASK_BIAS)
        s = s_scr[...] + _head_slopes(heads) * bias[None]
        m = jnp.max(s, axis=-1, keepdims=True)
        p = jnp.exp(s - m)
        l = jnp.sum(p, axis=-1, keepdims=True)
        p_scr[...] = p.astype(BF16)
        inv = 1.0 / l
        lse = m + jnp.log(l)

        lane = lax.broadcasted_iota(jnp.int32, (blk, LANES), 1)
        stat = jnp.zeros((blk, LANES), F32)
        for hp in range(heads // 2):
            sl = slice(LANES * hp, LANES * (hp + 1))
            vcat = v_st[:, sl]
            pv_a = jnp.dot(p_scr[2 * hp], vcat, preferred_element_type=F32) * inv[2 * hp]
            pv_b = jnp.dot(p_scr[2 * hp + 1], vcat, preferred_element_type=F32) * inv[2 * hp + 1]
            o_ref[:, sl] = jnp.where(low, pv_a, pv_b).astype(BF16)
            stat = jnp.where(lane == 2 * hp, lse[2 * hp], stat)
            stat = jnp.where(lane == 2 * hp + 1, lse[2 * hp + 1], stat)
        lse_ref[...] = stat

    lead = q.shape[:-2]
    rows = q.shape[-2]
    rs = _attn_residues(dims, dil)
    per_step = lambda shape: shape if rs == 1 else (rs,) + shape
    o, lse = pl.pallas_call(
        _per_residue(body, rs), name=name, grid=grid,
        in_specs=[cur, cur, cur] + ([prev, prev] if has_prev else []),
        out_specs=[cur, cur_stat],
        out_shape=[jax.ShapeDtypeStruct(lead + (rows, a), BF16), jax.ShapeDtypeStruct(lead + (rows, LANES), F32)],
        scratch_shapes=[pltpu.VMEM(per_step((heads, blk, nkeys)), F32), pltpu.VMEM(per_step((heads, blk, nkeys)), BF16)]
        + ([pltpu.VMEM(per_step((nkeys, a)), BF16)] * 2 if has_prev else []),
        compiler_params=_params(*["parallel"] * len(grid)),
    )(q, k, v, *([k, v] if has_prev else []))
    return o, lse


def _attn_combine(groups, head_spread, dims, *, name, tr=256):
    t = dims.tokens
    a = dims.n_heads * dims.head_dim
    dils = tuple(groups)

    nres = len(RESIDUE_DILATIONS)

    def body(*refs):
        ins = refs[:2 * len(dils)]
        x_ref = refs[2 * len(dils)]
        back_refs = dict(zip(RESIDUE_DILATIONS, refs[2 * len(dils) + 1:2 * len(dils) + 1 + nres]))
        o_ref = refs[2 * len(dils) + 1 + nres]
        lse_refs = refs[2 * len(dils) + 2 + nres:-1]
        scr_stat = refs[-1]
        outs, stats = [], []
        for g, d in enumerate(dils):
            if d == 1:
                outs.append(ins[2 * g][...].astype(F32))
                stats.append(ins[2 * g + 1][...])
            else:
                outs.append(_bf16_residues_to_rows(ins[2 * g], back_refs[d]))
                stats.append(_residues_to_rows(ins[2 * g + 1], scr_stat, d))
        top = functools.reduce(jnp.maximum, stats)
        weights = [jnp.exp(s - top) for s in stats]
        total = functools.reduce(jnp.add, weights)
        joint = top + jnp.log(total)
        inv = 1.0 / total
        acc = None
        for w, o in zip(weights, outs):
            term = _two_pass_dot(w * inv, x_ref[...]) * o
            acc = term if acc is None else acc + term
        o_ref[...] = acc.astype(BF16)
        for g, d in enumerate(dils):
            if d == 1:
                lse_refs[g][...] = joint
            else:
                _rows_to_residues(joint, lse_refs[g], scr_stat, d)

    in_specs, args, lse_specs, lse_shapes = [], [], [], []
    for d in dils:
        if d == 1:
            in_specs += [_row_spec(tr, a), _row_spec(tr, LANES)]
            lse_specs.append(_row_spec(tr, LANES))
            lse_shapes.append(jax.ShapeDtypeStruct((t, LANES), F32))
        else:
            in_specs += [_residue_spec(dims, d, tr, a), _residue_spec(dims, d, tr, LANES)]
            lse_specs.append(_residue_spec(dims, d, tr, LANES))
            lse_shapes.append(_residue_shape(dims, d, LANES, F32))
        args += list(groups[d])
    outs = pl.pallas_call(
        body, name=name, grid=(t // tr,),
        in_specs=in_specs + [pl.BlockSpec((LANES, a), lambda i: (0, 0))] + [pl.BlockSpec((tr, tr), lambda i: (0, 0))] * nres,
        out_specs=[_row_spec(tr, a)] + lse_specs,
        out_shape=[jax.ShapeDtypeStruct((t, a), BF16)] + lse_shapes,
        scratch_shapes=[pltpu.VMEM((1, tr, LANES), F32)],
        compiler_params=_params("parallel"),
    )(*args, head_spread, *[jnp.transpose(p) for p in _residue_permutations(tr)])
    return outs[0], dict(zip(dils, outs[1:]))


def _attn_bwd_prep(do, o, head_sum, dims, *, name, tr=256):
    t, a = o.shape
    nres = len(RESIDUE_DILATIONS)

    def body(do_ref, o_ref, e_ref, *rest):
        perm_refs, outs, scr_stat = rest[:nres], rest[nres:-1], rest[-1]
        delta = _two_pass_dot(do_ref[...].astype(F32) * o_ref[...].astype(F32), e_ref[...])
        outs[0][...] = delta
        for g, d in enumerate(RESIDUE_DILATIONS):
            _bf16_rows_to_residues(do_ref[...], outs[1 + 2 * g], perm_refs[g], d)
            _rows_to_residues(delta, outs[2 + 2 * g], scr_stat, d)

    out_specs, out_shape = [_row_spec(tr, LANES)], [jax.ShapeDtypeStruct((t, LANES), F32)]
    for d in RESIDUE_DILATIONS:
        out_specs += [_residue_spec(dims, d, tr, a), _residue_spec(dims, d, tr, LANES)]
        out_shape += [_residue_shape(dims, d, a, BF16), _residue_shape(dims, d, LANES, F32)]
    outs = pl.pallas_call(
        body, name=name, grid=(t // tr,),
        in_specs=[_row_spec(tr, a), _row_spec(tr, a), pl.BlockSpec((a, LANES), lambda i: (0, 0))]
        + [pl.BlockSpec((tr, tr), lambda i: (0, 0))] * nres,
        out_specs=out_specs, out_shape=out_shape,
        scratch_shapes=[pltpu.VMEM((1, tr, LANES), F32)],
        compiler_params=_params("parallel"),
    )(do, o, head_sum, *_residue_permutations(tr))
    dos, deltas = {1: do}, {1: outs[0]}
    for g, d in enumerate(RESIDUE_DILATIONS):
        dos[d], deltas[d] = outs[1 + 2 * g], outs[2 + 2 * g]
    return dos, deltas


def _attn_bwd(q, k, v, do, lse, delta, dims, dil, *, name):
    a = dims.n_heads * dims.head_dim
    heads, hd, blk = dims.n_heads, dims.head_dim, ATTN_BLOCK
    nb = dims.seq // dil // blk
    has_next = nb > 1
    nq = 2 * blk if has_next else blk
    grid, cur, _, nxt = _attn_specs(dims, dil, a)
    _, cur_stat, _, nxt_stat = _attn_specs(dims, dil, LANES)

    def body(*refs):
        k_ref, v_ref, q_ref, do_ref, lse_ref, dl_ref = refs[:6]
        if has_next:
            qn_ref, don_ref, lsen_ref, dln_ref = refs[6:10]
            dq_ref, dk_ref, dv_ref, q_st, do_st, s_scr, dp_scr, p_scr, ds_scr, carry = refs[10:]
        else:
            dq_ref, dk_ref, dv_ref, q_st, do_st, s_scr, dp_scr, p_scr, ds_scr = refs[6:]
        j = pl.program_id(len(grid) - 1)
        low, high = _pair_masks(hd)
        q_st[0:blk, :] = q_ref[...]
        do_st[0:blk, :] = do_ref[...]
        if has_next:
            q_st[blk:, :] = qn_ref[...]
            do_st[blk:, :] = don_ref[...]
            lse_all = jnp.concatenate([lse_ref[...], lsen_ref[...]], axis=0)
            dl_all = jnp.concatenate([dl_ref[...], dln_ref[...]], axis=0)
        else:
            lse_all, dl_all = lse_ref[...], dl_ref[...]
        lse_t, dl_t = jnp.transpose(lse_all), jnp.transpose(dl_all)
        lse3 = jnp.stack([lse_t[h:h + 1, :] for h in range(heads)])
        dl3 = jnp.stack([dl_t[h:h + 1, :] for h in range(heads)])

        def halves(x):
            return jnp.where(low, x, jnp.zeros_like(x)), jnp.where(high, x, jnp.zeros_like(x))

        for hp in range(heads // 2):
            sl = slice(LANES * hp, LANES * (hp + 1))
            k2, v2 = k_ref[:, sl], v_ref[:, sl]
            q_a, q_b = halves(q_st[:, sl])
            do_a, do_b = halves(do_st[:, sl])
            s_scr[2 * hp], s_scr[2 * hp + 1] = _dot_nt(k2, q_a), _dot_nt(k2, q_b)
            dp_scr[2 * hp], dp_scr[2 * hp + 1] = _dot_nt(v2, do_a), _dot_nt(v2, do_b)

        jk = lax.broadcasted_iota(jnp.int32, (blk, nq), 0)
        rq = lax.broadcasted_iota(jnp.int32, (blk, nq), 1)
        if has_next:
            iq = jnp.where(rq < blk, rq, rq - blk)
            steps = jnp.where(rq < blk, iq - jk, iq - jk + blk)
            valid = ((rq < blk) & (iq >= jk)) | ((rq >= blk) & (jk >= iq) & (j + 1 < nb))
        else:
            steps, valid = rq - jk, rq >= jk
        bias = jnp.where(valid, steps.astype(F32) * (-float(dil)), -MASK_BIAS)
        p = jnp.exp(s_scr[...] + _head_slopes(heads) * bias[None] - lse3)
        p_scr[...] = p.astype(BF16)
        ds_scr[...] = (p * (dp_scr[...] - dl3)).astype(BF16)

        if has_next:
            @pl.when(j == 0)
            def _():
                carry[...] = jnp.zeros_like(carry)

        for hp in range(heads // 2):
            sl = slice(LANES * hp, LANES * (hp + 1))
            k2 = k_ref[:, sl]
            q_a, q_b = halves(q_st[:, sl])
            do_a, do_b = halves(do_st[:, sl])
            ds_a, ds_b = ds_scr[2 * hp], ds_scr[2 * hp + 1]
            dk_ref[:, sl] = (jnp.dot(ds_a, q_a, preferred_element_type=F32)
                             + jnp.dot(ds_b, q_b, preferred_element_type=F32)).astype(BF16)
            dv_ref[:, sl] = (jnp.dot(p_scr[2 * hp], do_a, preferred_element_type=F32)
                             + jnp.dot(p_scr[2 * hp + 1], do_b, preferred_element_type=F32)).astype(BF16)
            dq2 = jnp.where(low, _dot_tn(ds_a, k2), _dot_tn(ds_b, k2))
            if has_next:
                dq_ref[:, sl] = (carry[:, sl] + dq2[:blk]).astype(BF16)
                carry[:, sl] = dq2[blk:]
            else:
                dq_ref[:, sl] = dq2.astype(BF16)

    args, in_specs = [k, v, q, do, lse, delta], [cur] * 4 + [cur_stat] * 2
    if has_next:
        args += [q, do, lse, delta]
        in_specs += [nxt] * 2 + [nxt_stat] * 2
    shape = jax.ShapeDtypeStruct(q.shape, BF16)
    rs = _attn_residues(dims, dil)
    per_step = lambda dims_: dims_ if rs == 1 else (rs,) + dims_
    scratch = ([pltpu.VMEM(per_step((nq, a)), BF16)] * 2 + [pltpu.VMEM(per_step((heads, blk, nq)), F32)] * 2
               + [pltpu.VMEM(per_step((heads, blk, nq)), BF16)] * 2)
    if has_next:
        scratch.append(pltpu.VMEM(per_step((blk, a)), F32))
    return pl.pallas_call(
        _per_residue(body, rs), name=name, grid=grid, in_specs=in_specs, out_specs=[cur] * 3, out_shape=[shape] * 3,
        scratch_shapes=scratch,
        compiler_params=_params(*["parallel"] * (len(grid) - 1), "arbitrary"),
    )(*args)


def _qkv_layouts_bwd(z, grads, gq, gk, head_ones, dims, *, name, tr=256):
    t = z.shape[0]
    a = dims.n_heads * dims.head_dim
    q_scale = dims.head_dim ** -0.5
    dils = tuple(grads)
    nres = len(RESIDUE_DILATIONS)

    def body(q_ref, k_ref, *rest):
        d_refs = rest[:3 * len(dils)]
        gq_ref, gk_ref, sum_ref, spread_ref = rest[3 * len(dils):3 * len(dils) + 4]
        back_refs = dict(zip(RESIDUE_DILATIONS, rest[3 * len(dils) + 4:3 * len(dils) + 4 + nres]))
        dz_ref, dgq_ref, dgk_ref = rest[3 * len(dils) + 4 + nres:]
        first = pl.program_id(0) == 0
        mean = lambda val: _two_pass_dot(_two_pass_dot(val, sum_ref[...]), spread_ref[...]) * (1.0 / dims.head_dim)

        def total(j):
            acc = None
            for g, d in enumerate(dils):
                ref = d_refs[3 * g + j]
                part = ref[...].astype(F32) if d == 1 else _bf16_residues_to_rows(ref, back_refs[d])
                acc = part if acc is None else acc + part
            return acc

        def norm_bwd(x_ref, dy, g_ref, scale, col, dg_ref):
            xv = x_ref[...].astype(F32)
            dy = dy * scale
            r = lax.rsqrt(mean(xv * xv) + RMS_EPS)
            gy = dy * g_ref[...]
            dx = r * gy - xv * (r * r * r) * mean(xv * gy)
            dz_ref[:, col * a:(col + 1) * a] = dx.astype(BF16)
            _accumulate(dg_ref, jnp.sum(dy * xv * r, axis=0, keepdims=True), first)

        norm_bwd(q_ref, total(0), gq_ref, q_scale, 0, dgq_ref)
        norm_bwd(k_ref, total(1), gk_ref, 1.0, 1, dgk_ref)
        dz_ref[:, 2 * a:3 * a] = total(2).astype(BF16)

    in_specs, args = [_row_spec(tr, a, 2), _row_spec(tr, a, 3)], [z, z]
    for d in dils:
        in_specs += [_row_spec(tr, a) if d == 1 else _residue_spec(dims, d, tr, a)] * 3
        args += list(grads[d])
    in_specs += [_vec_spec(a), _vec_spec(a), pl.BlockSpec((a, LANES), lambda i: (0, 0)),
                 pl.BlockSpec((LANES, a), lambda i: (0, 0))] + [pl.BlockSpec((tr, tr), lambda i: (0, 0))] * nres
    return pl.pallas_call(
        body, name=name, grid=(t // tr,), in_specs=in_specs,
        out_specs=[_row_spec(tr, 3 * a), _vec_spec(a), _vec_spec(a)],
        out_shape=[jax.ShapeDtypeStruct((t, 3 * a), BF16)] + [jax.ShapeDtypeStruct((1, a), F32)] * 2,
        compiler_params=_params("arbitrary"),
    )(*args, gq, gk, *head_ones, *[jnp.transpose(p) for p in _residue_permutations(tr)])


def _mix_fwd(ya, yb, z, gate_b, dims, *, name, tr=512):
    t, d = ya.shape
    tr = _pick(t, tr, 8)
    first_gate_col = z.shape[1] // d - 2

    def body(ya_ref, yb_ref, ga_ref, gb_ref, ba_ref, bb_ref, o_ref):
        g_a = _sigmoid(ga_ref[...].astype(F32) + ba_ref[...])
        g_b = _sigmoid(gb_ref[...].astype(F32) + bb_ref[...])
        o_ref[...] = (g_a * ya_ref[...] + g_b * yb_ref[...]).astype(BF16)

    return pl.pallas_call(
        body, name=name, grid=(t // tr,),
        in_specs=[_row_spec(tr, d), _row_spec(tr, d), _row_spec(tr, d, first_gate_col),
                  _row_spec(tr, d, first_gate_col + 1), _vec_spec(d, 0), _vec_spec(d, 1)],
        out_specs=_row_spec(tr, d), out_shape=jax.ShapeDtypeStruct((t, d), BF16),
        compiler_params=_params("parallel"),
    )(ya, yb, z, z, gate_b, gate_b)


def _mix_bwd(dmix, ya, yb, z, gate_b, dims, *, name, tr=512):
    t, d = ya.shape
    tr = _pick(t, tr, 8)
    first_gate_col = z.shape[1] // d - 2

    def body(dm_ref, ya_ref, yb_ref, ga_ref, gb_ref, ba_ref, bb_ref, dya_ref, dyb_ref, dz_ref, db_ref):
        dm = dm_ref[...].astype(F32)
        g_a = _sigmoid(ga_ref[...].astype(F32) + ba_ref[...])
        g_b = _sigmoid(gb_ref[...].astype(F32) + bb_ref[...])
        dya_ref[...] = (dm * g_a).astype(BF16)
        dyb_ref[...] = (dm * g_b).astype(BF16)
        dl_a = dm * ya_ref[...] * g_a * (1.0 - g_a)
        dl_b = dm * yb_ref[...] * g_b * (1.0 - g_b)
        dz_ref[:, 0:d] = dl_a.astype(BF16)
        dz_ref[:, d:2 * d] = dl_b.astype(BF16)
        first = pl.program_id(0) == 0
        sums = jnp.concatenate([jnp.sum(dl_a, axis=0, keepdims=True), jnp.sum(dl_b, axis=0, keepdims=True)], axis=1)
        _accumulate(db_ref, sums, first)

    return pl.pallas_call(
        body, name=name, grid=(t // tr,),
        in_specs=[_row_spec(tr, d), _row_spec(tr, d), _row_spec(tr, d), _row_spec(tr, d, first_gate_col),
                  _row_spec(tr, d, first_gate_col + 1), _vec_spec(d, 0), _vec_spec(d, 1)],
        out_specs=[_row_spec(tr, d), _row_spec(tr, d), _row_spec(tr, 2 * d), _vec_spec(2 * d)],
        out_shape=[jax.ShapeDtypeStruct((t, d), BF16)] * 2 + [jax.ShapeDtypeStruct((t, 2 * d), BF16),
                                                              jax.ShapeDtypeStruct((1, 2 * d), F32)],
        compiler_params=_params("arbitrary"),
    )(dmix, ya, yb, z, z, gate_b, gate_b)


def _adamw(w, grads, m, v, *, name, tr=256):
    r, c = w.shape
    tr = _pick(r, tr, 8)
    ng = len(grads)
    c1 = 1.0 - ADAM_B1 ** ADAM_STEP
    c2 = 1.0 - ADAM_B2 ** ADAM_STEP

    def body(*refs):
        w_ref, g_refs, m_ref, v_ref = refs[0], refs[1:1 + ng], refs[1 + ng], refs[2 + ng]
        g_out, d_out, m_out, v_out = refs[3 + ng:]
        g = g_refs[0][...]
        for extra in g_refs[1:]:
            g = g + extra[...]
        m_new = ADAM_B1 * m_ref[...] + (1.0 - ADAM_B1) * g
        v_new = ADAM_B2 * v_ref[...] + (1.0 - ADAM_B2) * (g * g)
        g_out[...] = g
        m_out[...] = m_new
        v_out[...] = v_new
        d_out[...] = -ADAM_LR * ((m_new / c1) / (jnp.sqrt(v_new / c2) + ADAM_EPS) + ADAM_WD * w_ref[...])

    spec = pl.BlockSpec((tr, c), lambda i: (i, 0))
    return pl.pallas_call(
        body, name=name, grid=(r // tr,),
        in_specs=[spec] * (3 + ng), out_specs=[spec] * 4, out_shape=[jax.ShapeDtypeStruct((r, c), F32)] * 4,
        compiler_params=_params("parallel"),
    )(w, *grads, m, v)


CHIP_PEERS = ((1, 0), (0, 1), (1, 1))


def _place():
    return lax.axis_index("x"), lax.axis_index("y"), lax.axis_index("c")


HBM = pl.BlockSpec(memory_space=pltpu.HBM)
SEM = pl.BlockSpec(memory_space=pltpu.SEMAPHORE)
IN_FLIGHT = pltpu.SideEffectType.DATAFLOW_SIDE_EFFECTING


def _in_hbm(a):
    return pltpu.with_memory_space_constraint(a, pltpu.HBM)


def _cast_to_lands(shards, dtypes, *, name, after=None):
    n = len(shards)

    def body(*refs):
        ins, outs, bufs, sems = refs[:n], refs[n:2 * n], refs[2 * n:3 * n], refs[3 * n]
        x, y, _ = _place()
        copies = []
        for a in range(n):
            bufs[a][...] = ins[a][...].astype(dtypes[a])
            cp = pltpu.make_async_copy(bufs[a], outs[a].at[2 * x + y], sems.at[a])
            cp.start()
            copies.append(cp)
        for cp in copies:
            cp.wait()

    body, more_specs, more_args = _ordered(body, n, after)
    return pl.pallas_call(
        body, name=name, in_specs=[pl.BlockSpec(memory_space=pltpu.VMEM)] * n + more_specs, out_specs=[ANY] * n,
        out_shape=[jax.ShapeDtypeStruct((N_CHIPS,) + s.shape, dt) for s, dt in zip(shards, dtypes)],
        scratch_shapes=[pltpu.VMEM(s.shape, dt) for s, dt in zip(shards, dtypes)] + [pltpu.SemaphoreType.DMA((n,))],
        compiler_params=pltpu.CompilerParams(vmem_limit_bytes=V7X_VMEM_LIMIT_BYTES),
    )(*shards, *more_args)


def _chip_copy(src, dst, send, recv, flip, place):
    x, y, c = place
    return pltpu.make_async_remote_copy(src_ref=src, dst_ref=dst, send_sem=send, recv_sem=recv,
                                        device_id=(x ^ flip[0], y ^ flip[1], c), device_id_type=MESH)


def _my_part(land, place, halved):
    block = land.at[2 * place[0] + place[1]]
    if not halved:
        return block
    rows = land.shape[1] // 2
    return block.at[pl.ds(pl.multiple_of(place[2] * rows, rows), rows)]


def _gather_start(lands, after, *, name, halved=()):
    n = len(lands)

    def body(*refs):
        ins, send, recv, token = refs[:n], refs[n + 1], refs[n + 2], refs[-1]
        place = _place()
        for a in range(n):
            part = _my_part(ins[a], place, a in halved)
            for p, flip in enumerate(CHIP_PEERS):
                k = 3 * a + p
                _chip_copy(part, part, send.at[k], recv.at[k], flip, place).start()
        token[...] = jnp.zeros_like(token)

    outs = pl.pallas_call(
        body, name=name, in_specs=[HBM] * n + [ANY],
        out_specs=(SEM, SEM, *[HBM] * n, pl.BlockSpec(memory_space=pltpu.VMEM)),
        out_shape=(pltpu.SemaphoreType.DMA((3 * n,)), pltpu.SemaphoreType.DMA((3 * n,)),
                   *[pltpu.HBM(l.shape, l.dtype) for l in lands], jax.ShapeDtypeStruct((8, 128), F32)),
        input_output_aliases={a: 2 + a for a in range(n)},
        compiler_params=pltpu.CompilerParams(has_side_effects=IN_FLIGHT),
    )(*[_in_hbm(l) for l in lands], after)
    return outs[0], outs[1], list(outs[2:2 + n]), outs[-1]


def _gather_wait(send, recv, lands, after, *, name, halved=()):
    n = len(lands)

    def body(*refs):
        ins, send_ref, recv_ref = refs[:n], refs[n], refs[n + 1]
        place = _place()
        for a in range(n):
            part = _my_part(ins[a], place, a in halved)
            for p, flip in enumerate(CHIP_PEERS):
                k = 3 * a + p
                cp = _chip_copy(part, part, send_ref.at[k], recv_ref.at[k], flip, place)
                cp.wait_send()
                cp.wait_recv()

    after = list(after) if isinstance(after, (list, tuple)) else [after]
    return pl.pallas_call(
        body, name=name, in_specs=[HBM] * n + [SEM, SEM] + [ANY] * len(after), out_specs=[HBM] * n,
        out_shape=[pltpu.HBM(l.shape, l.dtype) for l in lands],
        input_output_aliases={a: a for a in range(n)},
        compiler_params=pltpu.CompilerParams(has_side_effects=IN_FLIGHT),
    )(*lands, send, recv, *after)


def _forward_to_sibling(land, *, name):
    rows = land.shape[1] // 2

    def body(land_ref, out_ref, send, recv):
        x, y, c = _place()
        copies = []
        for p, (fx, fy) in enumerate(CHIP_PEERS):
            chip = 2 * (x ^ fx) + (y ^ fy)
            mine = pl.ds(pl.multiple_of(c * rows, rows), rows)
            theirs = pl.ds(pl.multiple_of((1 - c) * rows, rows), rows)
            out = pltpu.make_async_remote_copy(
                src_ref=land_ref.at[chip].at[mine], dst_ref=out_ref.at[chip].at[mine], send_sem=send.at[p],
                recv_sem=recv.at[p], device_id=(x, y, 1 - c), device_id_type=MESH)
            out.start()
            copies.append((out, pltpu.make_async_remote_copy(
                src_ref=land_ref.at[chip].at[theirs], dst_ref=out_ref.at[chip].at[theirs], send_sem=send.at[p],
                recv_sem=recv.at[p], device_id=(x, y, 1 - c), device_id_type=MESH)))
        for out, arriving in copies:
            out.wait_send()
            arriving.wait_recv()

    return pl.pallas_call(
        body, name=name, in_specs=[ANY], out_specs=ANY, out_shape=jax.ShapeDtypeStruct(land.shape, land.dtype),
        input_output_aliases={0: 0},
        scratch_shapes=[pltpu.SemaphoreType.DMA((3,)), pltpu.SemaphoreType.DMA((3,))],
    )(land)


def _scatter_start(grad, *, name):
    def body(g_ref, land_ref, send, recv, g_thru, land_thru, token):
        place = _place()
        for p, flip in enumerate(CHIP_PEERS):
            peer_chip = 2 * (place[0] ^ flip[0]) + (place[1] ^ flip[1])
            _chip_copy(g_ref.at[peer_chip], land_ref.at[p], send.at[p], recv.at[p], flip, place).start()
        token[...] = jnp.zeros_like(token)

    land = lax.empty((3,) + grad.shape[1:], grad.dtype)
    return pl.pallas_call(
        body, name=name, in_specs=[HBM, HBM],
        out_specs=(SEM, SEM, HBM, HBM, pl.BlockSpec(memory_space=pltpu.VMEM)),
        out_shape=(pltpu.SemaphoreType.DMA((3,)), pltpu.SemaphoreType.DMA((3,)), pltpu.HBM(grad.shape, grad.dtype),
                   pltpu.HBM(land.shape, land.dtype), jax.ShapeDtypeStruct((8, 128), F32)),
        input_output_aliases={0: 2, 1: 3},
        compiler_params=pltpu.CompilerParams(has_side_effects=IN_FLIGHT),
    )(_in_hbm(grad), _in_hbm(land))


def _scatter_wait(started, after, *, name):
    n = len(started)

    def body(*refs):
        grads, lands = refs[:n], refs[n:2 * n]
        sends, recvs = refs[2 * n:3 * n], refs[3 * n:4 * n]
        place = _place()
        for a in range(n):
            for p, flip in enumerate(CHIP_PEERS):
                cp = _chip_copy(grads[a].at[0], lands[a].at[p], sends[a].at[p], recvs[a].at[p], flip, place)
                cp.wait_send()
                cp.wait_recv()

    grads, lands = [s[2] for s in started], [s[3] for s in started]
    after = list(after) if isinstance(after, (list, tuple)) else [after]
    outs = pl.pallas_call(
        body, name=name, in_specs=[HBM] * (2 * n) + [SEM] * (2 * n) + [ANY] * len(after), out_specs=[HBM] * (2 * n),
        out_shape=[pltpu.HBM(a.shape, a.dtype) for a in grads + lands],
        input_output_aliases={a: a for a in range(2 * n)},
        compiler_params=pltpu.CompilerParams(has_side_effects=IN_FLIGHT),
    )(*grads, *lands, *[s[0] for s in started], *[s[1] for s in started], *after)
    return list(zip(outs[:n], outs[n:]))


def _sibling_copy(src, dst, send, recv, place):
    x, y, c = place
    return pltpu.make_async_remote_copy(src_ref=src, dst_ref=dst, send_sem=send, recv_sem=recv,
                                        device_id=(x, y, 1 - c), device_id_type=MESH)


def _swap_start(arrays, *, name):
    n = len(arrays)

    def body(*refs):
        ins, lands, send, recv, token = refs[:n], refs[n:2 * n], refs[2 * n], refs[2 * n + 1], refs[-1]
        place = _place()
        for a in range(n):
            _sibling_copy(ins[a], lands[a], send.at[a], recv.at[a], place).start()
        token[...] = jnp.zeros_like(token)

    both = [_in_hbm(a) for a in arrays] + [_in_hbm(lax.empty(a.shape, a.dtype)) for a in arrays]
    outs = pl.pallas_call(
        body, name=name, in_specs=[HBM] * (2 * n),
        out_specs=(SEM, SEM, *[HBM] * (2 * n), pl.BlockSpec(memory_space=pltpu.VMEM)),
        out_shape=(pltpu.SemaphoreType.DMA((n,)), pltpu.SemaphoreType.DMA((n,)),
                   *[pltpu.HBM(a.shape, a.dtype) for a in both], jax.ShapeDtypeStruct((8, 128), F32)),
        input_output_aliases={a: 2 + a for a in range(2 * n)},
        compiler_params=pltpu.CompilerParams(has_side_effects=IN_FLIGHT),
    )(*both)
    return outs[0], outs[1], list(outs[2:2 + n]), list(outs[2 + n:2 + 2 * n]), outs[-1]


def _swap_wait(started, after, *, name):
    send, recv, arrays, lands = started[:4]
    n = len(arrays)

    def body(*refs):
        ins, zones, send_ref, recv_ref = refs[:n], refs[n:2 * n], refs[2 * n], refs[2 * n + 1]
        place = _place()
        for a in range(n):
            cp = _sibling_copy(ins[a], zones[a], send_ref.at[a], recv_ref.at[a], place)
            cp.wait_send()
            cp.wait_recv()

    after = list(after) if isinstance(after, (list, tuple)) else [after]
    outs = pl.pallas_call(
        body, name=name, in_specs=[HBM] * (2 * n) + [SEM, SEM] + [ANY] * len(after), out_specs=[HBM] * (2 * n),
        out_shape=[pltpu.HBM(a.shape, a.dtype) for a in arrays + lands],
        input_output_aliases={a: a for a in range(2 * n)},
        compiler_params=pltpu.CompilerParams(has_side_effects=IN_FLIGHT),
    )(*arrays, *lands, send, recv, *after)
    return list(outs[:n]), list(outs[n:])


def _allreduce_start(packed, *, name):
    n_dev = 8

    def body(src_ref, land_ref, send, recv, src_thru, land_thru, token):
        x, y, c = _place()
        me = 4 * x + 2 * y + c
        for p in range(1, n_dev):
            pltpu.make_async_remote_copy(
                src_ref=src_ref, dst_ref=land_ref.at[me], send_sem=send.at[p - 1], recv_sem=recv.at[p - 1],
                device_id=(x ^ (p >> 2), y ^ ((p >> 1) & 1), c ^ (p & 1)), device_id_type=MESH).start()
        token[...] = jnp.zeros_like(token)

    land = lax.empty((n_dev,) + packed.shape, packed.dtype)
    return pl.pallas_call(
        body, name=name, in_specs=[HBM, HBM],
        out_specs=(SEM, SEM, HBM, HBM, pl.BlockSpec(memory_space=pltpu.VMEM)),
        out_shape=(pltpu.SemaphoreType.DMA((n_dev - 1,)), pltpu.SemaphoreType.DMA((n_dev - 1,)),
                   pltpu.HBM(packed.shape, packed.dtype), pltpu.HBM(land.shape, land.dtype),
                   jax.ShapeDtypeStruct((8, 128), F32)),
        input_output_aliases={0: 2, 1: 3},
        compiler_params=pltpu.CompilerParams(has_side_effects=IN_FLIGHT),
    )(_in_hbm(packed), _in_hbm(land))


def _allreduce_wait(started, after, *, name):
    send, recv, packed, land = started[:4]
    n_dev = 8

    def body(src_ref, land_ref, send_ref, recv_ref, *_):
        x, y, c = _place()
        for p in range(1, n_dev):
            cp = pltpu.make_async_remote_copy(
                src_ref=src_ref, dst_ref=land_ref.at[0], send_sem=send_ref.at[p - 1], recv_sem=recv_ref.at[p - 1],
                device_id=(x ^ (p >> 2), y ^ ((p >> 1) & 1), c ^ (p & 1)), device_id_type=MESH)
            cp.wait_send()
            cp.wait_recv()

    after = list(after) if isinstance(after, (list, tuple)) else [after]
    return pl.pallas_call(
        body, name=name, in_specs=[HBM, HBM, SEM, SEM] + [ANY] * len(after), out_specs=[HBM, HBM],
        out_shape=[pltpu.HBM(packed.shape, packed.dtype), pltpu.HBM(land.shape, land.dtype)],
        input_output_aliases={0: 0, 1: 1},
        compiler_params=pltpu.CompilerParams(has_side_effects=IN_FLIGHT),
    )(packed, land, send, recv, *after)


def _sum_devices(mine, land, *, name):
    n_dev = land.shape[0]

    def body(mine_ref, land_ref, out_ref):
        x, y, c = _place()
        me = 4 * x + 2 * y + c
        total = None
        for s in range(n_dev):
            part = jnp.where(me == s, mine_ref[...], land_ref[s])
            total = part if total is None else total + part
        out_ref[...] = total

    return pl.pallas_call(body, name=name, out_shape=jax.ShapeDtypeStruct(mine.shape, mine.dtype))(mine, land)


def _sum_received(grad, land, *, name, tr=256):
    _, r, c = grad.shape
    tr = _pick(r, tr, 8)

    def body(chip_ref, g_ref, l_ref, o_ref):
        o_ref[...] = ((g_ref[...] + l_ref[0].astype(F32)) + l_ref[1].astype(F32)) + l_ref[2].astype(F32)

    chip = (2 * lax.axis_index("x") + lax.axis_index("y")).astype(jnp.int32).reshape(1)
    return pl.pallas_call(
        body, name=name,
        grid_spec=pltpu.PrefetchScalarGridSpec(
            num_scalar_prefetch=1, grid=(r // tr,),
            in_specs=[pl.BlockSpec((None, tr, c), lambda i, chip_ref: (chip_ref[0], i, 0)),
                      pl.BlockSpec((3, tr, c), lambda i, chip_ref: (0, i, 0))],
            out_specs=pl.BlockSpec((tr, c), lambda i, chip_ref: (i, 0))),
        out_shape=jax.ShapeDtypeStruct((r, c), F32), compiler_params=_params("parallel"),
    )(chip, grad, land)


def _packed_rows(size, d):
    return -(-size // (8 * d)) * 8


def _pack_rows(arrays, d):
    rows = []
    for arr in arrays:
        flat = arr.reshape(-1).astype(F32)
        n = _packed_rows(flat.shape[0], d)
        rows.append(jnp.pad(flat, (0, n * d - flat.shape[0])).reshape(n, d))
    return jnp.concatenate(rows, axis=0)


def _unpack_rows(packed, shapes, d):
    out, row = [], 0
    for shape in shapes:
        size = math.prod(shape)
        n = _packed_rows(size, d)
        out.append(packed[row:row + n].reshape(-1)[:size].reshape(shape))
        row += n
    return out


SMALL = ("norm1_g", "gate_b", "conv_b", "conv_norm_g", "q_norm_g", "k_norm_g", "norm2_g", "ffn_conv_b")
LARGE = ("w_in", "w_conv_out", "w_attn_out", "w_out", "w_up", "w_down")
WEIGHTS = ("norm1_g", "w_in", "gate_b", "conv_w", "conv_b", "conv_norm_g", "w_conv_out", "q_norm_g", "k_norm_g",
           "w_attn_out", "w_out", "norm2_g", "w_up", "ffn_conv_w", "ffn_conv_b", "w_down")


def _after(vec, token):
    return vec if token is None else vec + token[0:1, 0:1]


def _local_step(dims, x, target, small, first_weights, other_weights, send_grad):
    d, f, heads = dims.d_model, dims.d_ff, dims.n_heads
    small = dict(small)
    row = lambda name: small[name].reshape(1, -1)
    head_sum = _head_sum_matrix(dims)
    head_spread = jnp.transpose(head_sum)
    ones = (head_sum, head_spread)
    gq = jnp.tile(row("q_norm_g"), (1, heads))
    gk = jnp.tile(row("k_norm_g"), (1, heads))
    one_shard = lambda w: w.reshape(1, -1, w.shape[-1])

    h = _rmsnorm_fwd(x, row("norm1_g"), name="norm1")
    full = first_weights(h)
    w_in = full["w_in"]
    conv_w = jnp.pad(full["conv_w"], ((0, CONV_HALO - dims.conv_width), (0, 0)))
    ffn_w = jnp.pad(full["ffn_conv_w"], ((0, FFN_HALO - dims.ffn_conv_width), (0, 0)))
    z = _mm_nn(h, w_in, out_dtype=BF16, after=full.get("token"), tm=2048, tn=1792, name="in_proj")
    a1, a3 = _conv_branch_fwd(z, conv_w, row("conv_b"), row("conv_norm_g"), dims, name="conv_branch")
    qkv = _qkv_layouts_fwd(z, gq, gk, ones, dims, name="qk_norm")
    per_group = {dil: _attn_fwd(*qkv[dil], dims, dil, name=f"attn_fwd_d{dil}") for dil in DILATIONS}
    o, lse = _attn_combine(per_group, head_spread, dims, name="attn_combine")
    full = other_weights(o)
    w_up = full["w_up"]
    w_co, w_ao, w_o, w_dn = (one_shard(full[k]) for k in ("w_conv_out", "w_attn_out", "w_out", "w_down"))
    ya = _mm_nn(a3, w_co, out_dtype=F32, name="conv_out_proj")
    yb = _mm_nn(o, w_ao, out_dtype=F32, name="attn_out_proj")
    mixed = _mix_fwd(ya, yb, z, row("gate_b"), dims, name="gate_mix")
    x1, h2 = _proj_residual_norm(mixed, w_o, x, row("norm2_g"), name="out_proj_norm2")
    up = _mm_nn(h2, w_up, out_dtype=F32, tm=2048, name="up_proj")
    act = _ffn_act_fwd(up, ffn_w, row("ffn_conv_b"), dims, name="ffn_act")
    dy, dy_b, loss = _proj_residual_loss(act, w_dn, x1, target, tm=512, name="down_proj_loss")

    grads = {}

    def large(name, g):
        grads[name], g_bf16 = g
        return send_grad(name, g_bf16)

    sent = large("w_down", _mm_tn(act, dy_b, n_shards=1, name="dw_down"))
    dact = _mm_nt(dy_b, w_dn, out_dtype=BF16, after=sent, name="d_act")
    dup, dfw, dfb = _ffn_bwd(dact, up, ffn_w, row("ffn_conv_b"), dims, name="ffn_bwd")
    grads["ffn_conv_w"], grads["ffn_conv_b"] = dfw[:dims.ffn_conv_width], dfb
    sent = large("w_up", _mm_tn(h2, dup, n_shards=N_CHIPS, name="dw_up"))
    dh2 = _mm_nt(dup, w_up, out_dtype=F32, after=sent, name="d_h2")
    dx1, dx1_b, grads["norm2_g"] = _rmsnorm_bwd(x1, row("norm2_g"), dh2, dy, want_bf16=True, name="norm2_bwd")
    sent = large("w_out", _mm_tn(mixed, dx1_b, n_shards=1, name="dw_out"))
    dmix = _mm_nt(dx1_b, w_o, out_dtype=F32, after=sent, name="d_mix")
    dya, dyb, dz_gate, grads["gate_b"] = _mix_bwd(dmix, ya, yb, z, row("gate_b"), dims, name="gate_mix_bwd")
    sent = large("w_attn_out", _mm_tn(o, dyb, n_shards=1, name="dw_attn_out"))
    do = _mm_nt(dyb, w_ao, out_dtype=BF16, after=sent, name="d_attn")
    dos, deltas = _attn_bwd_prep(do, o, head_sum, dims, name="attn_bwd_prep")
    dqkv = {dil: _attn_bwd(*qkv[dil], dos[dil], lse[dil], deltas[dil], dims, dil, name=f"attn_bwd_d{dil}")
            for dil in DILATIONS}
    dz_qkv, dgq, dgk = _qkv_layouts_bwd(z, dqkv, gq, gk, ones, dims, name="qk_norm_bwd")
    grads["q_norm_g"] = dgq.reshape(heads, dims.head_dim).sum(axis=0)
    grads["k_norm_g"] = dgk.reshape(heads, dims.head_dim).sum(axis=0)
    sent = large("w_conv_out", _mm_tn(a3, dya, n_shards=1, name="dw_conv_out"))
    da3 = _mm_nt(dya, w_co, out_dtype=F32, after=sent, name="d_conv_act")
    da1, grads["conv_norm_g"] = _conv_norm_bwd(da3, a1, row("conv_norm_g"), name="conv_norm_bwd")
    dz, dcw, grads["conv_b"] = _conv_branch_bwd(da1, z, conv_w, [dz_qkv, dz_gate], dims, name="conv_branch_bwd")
    grads["conv_w"] = dcw[:dims.conv_width]
    sent = large("w_in", _mm_tn(h, dz, n_shards=N_CHIPS, name="dw_in"))
    dh = _mm_nt(dz, w_in, out_dtype=F32, after=sent, name="d_h")
    dx, grads["norm1_g"] = _rmsnorm_bwd(x, row("norm1_g"), dh, dx1, want_bf16=False, name="norm1_bwd")
    return loss, dx, grads


def _step(dims, x, target, w, m, v):
    d = dims.d_model
    t = dims.tokens
    sq = lambda a: a.reshape(a.shape[1:])
    w2, m2, v2 = ({k: sq(a) for k, a in grp.items()} for grp in (w, m, v))

    conv_pad = jnp.pad(w2["conv_w"], ((0, CONV_HALO - dims.conv_width), (0, 0)))
    ffn_pad = jnp.pad(w2["ffn_conv_w"], ((0, FFN_HALO - dims.ffn_conv_width), (0, 0)))
    first_names = ("w_in", "conv_w", "ffn_conv_w")
    other_names = tuple(k for k in LARGE if k not in first_names)
    lands = dict(zip(first_names, _cast_to_lands([w2["w_in"], conv_pad, ffn_pad], [BF16, F32, F32], name="cast_first")))
    first = _gather_start([lands[k] for k in first_names], x, halved=(0,), name="gather_start_first")
    lands.update(zip(other_names, _cast_to_lands([w2[k] for k in other_names], [BF16] * len(other_names),
                                                 after=first[3], name="cast_other")))
    other = []
    cols = lambda g, rows: jnp.moveaxis(g, 0, 1).reshape(g.shape[1], -1)[:rows]

    def first_weights(after):
        got = dict(zip(first_names, _gather_wait(*first[:3], [after] + [lands[k] for k in other_names], halved=(0,),
                                                 name="gather_wait_first")))
        got["w_in"] = _forward_to_sibling(got["w_in"], name="forward_w_in")
        other.extend(_gather_start([lands[k] for k in other_names], got["w_in"], name="gather_start_other"))
        got["conv_w"] = cols(got["conv_w"], dims.conv_width)
        got["ffn_conv_w"] = cols(got["ffn_conv_w"], dims.ffn_conv_width)
        got["token"] = other[3]
        return got

    def other_weights(after):
        return dict(zip(other_names, _gather_wait(*other[:3], after, name="gather_wait_other")))

    started = {}

    def send_grad(name, g):
        send, recv, g_thru, land, token = _scatter_start(g.reshape(N_CHIPS, -1, g.shape[-1]), name=f"scatter_start_{name}")
        started[name] = (send, recv, g_thru, land)
        return token

    small = {k: w2[k] for k in SMALL}
    small["norm1_g"] = _after(small["norm1_g"].reshape(1, -1), first[3])
    loss, dx, grads = _local_step(dims, x.reshape(t, d), target.reshape(t, d), small, first_weights, other_weights, send_grad)

    def my_sums(names, after, tag):
        arrived = _scatter_wait([started[k] for k in names], after, name=f"scatter_wait_{tag}")
        blocks = [grads[k].reshape(N_CHIPS, -1, grads[k].shape[-1]) for k in names]
        return [_sum_received(g, land, name=f"sum_{k}") for k, g, (_, land) in zip(names, blocks, arrived)]

    def updates(names, mine, theirs):
        return {k: _adamw(w2[k], [a, b], m2[k], v2[k], name=f"adamw_{k}") for k, a, b in zip(names, mine, theirs)}

    small_names = SMALL + ("conv_w", "ffn_conv_w")
    packed = _pack_rows([grads[k] for k in small_names] + [loss[0, 0]], d)
    reducing = _allreduce_start(packed, name="allreduce_start")
    others = [k for k in LARGE if k != "w_in"]
    mine_others = my_sums(others, [dx, reducing[4]], "others")
    swapping_others = _swap_start(mine_others, name="swap_start_others")
    mine_w_in = my_sums(["w_in"], swapping_others[4], "w_in")
    swapping_w_in = _swap_start(mine_w_in, name="swap_start_w_in")
    out = updates(others, *_swap_wait(swapping_others, swapping_w_in[4], name="swap_wait_others"))
    last_updates = [out[k][1] for k in others]
    reduced = _sum_devices(*_allreduce_wait(reducing, last_updates, name="allreduce_wait"), name="allreduce_sum")
    shapes = [grads[k].shape for k in small_names] + [()]
    *small_g, loss_total = _unpack_rows(reduced, shapes, d)
    small_g = dict(zip(small_names, small_g))
    chip = 2 * lax.axis_index("x") + lax.axis_index("y")
    for k in ("conv_w", "ffn_conv_w"):
        width = w2[k].shape[1]
        small_g[k] = lax.dynamic_slice_in_dim(small_g[k], chip * width, width, axis=1)

    small_shapes = [w2[k].shape for k in small_names]
    pack = lambda grp: _pack_rows([grp[k] for k in small_names], d)
    results = _adamw(pack(w2), [pack(small_g)], pack(m2), pack(v2), name="adamw_small")
    unpacked = [_unpack_rows(r, small_shapes, d) for r in results]
    out.update({k: tuple(u[i] for u in unpacked) for i, k in enumerate(small_names)})
    out.update(updates(["w_in"], *_swap_wait(swapping_w_in, results[1], name="swap_wait_w_in")))

    lead = lambda a: a.reshape((1,) + a.shape)
    ordered = [[lead(out[k][j].reshape(w2[k].shape)) for k in WEIGHTS] for j in range(4)]
    return (loss_total, dx.reshape(x.shape), *ordered[0], *ordered[1], *ordered[2], *ordered[3])


def kernel(x, norm1_g, w_in, gate_b, conv_w, conv_b, conv_norm_g, w_conv_out, q_norm_g, k_norm_g, w_attn_out, w_out, norm2_g, w_up, ffn_conv_w, ffn_conv_b, w_down, loss_target, m_norm1_g, m_w_in, m_gate_b, m_conv_w, m_conv_b, m_conv_norm_g, m_w_conv_out, m_q_norm_g, m_k_norm_g, m_w_attn_out, m_w_out, m_norm2_g, m_w_up, m_ffn_conv_w, m_ffn_conv_b, m_w_down, v_norm1_g, v_w_in, v_gate_b, v_conv_w, v_conv_b, v_conv_norm_g, v_w_conv_out, v_q_norm_g, v_k_norm_g, v_w_attn_out, v_w_out, v_norm2_g, v_w_up, v_ffn_conv_w, v_ffn_conv_b, v_w_down):
    w = dict(zip(WEIGHTS, (norm1_g, w_in, gate_b, conv_w, conv_b, conv_norm_g, w_conv_out, q_norm_g, k_norm_g,
                           w_attn_out, w_out, norm2_g, w_up, ffn_conv_w, ffn_conv_b, w_down)))
    m = dict(zip(WEIGHTS, (m_norm1_g, m_w_in, m_gate_b, m_conv_w, m_conv_b, m_conv_norm_g, m_w_conv_out, m_q_norm_g,
                           m_k_norm_g, m_w_attn_out, m_w_out, m_norm2_g, m_w_up, m_ffn_conv_w, m_ffn_conv_b, m_w_down)))
    v = dict(zip(WEIGHTS, (v_norm1_g, v_w_in, v_gate_b, v_conv_w, v_conv_b, v_conv_norm_g, v_w_conv_out, v_q_norm_g,
                           v_k_norm_g, v_w_attn_out, v_w_out, v_norm2_g, v_w_up, v_ffn_conv_w, v_ffn_conv_b, v_w_down)))
    dims = Dims(d_model=x.shape[-1], batch_local=x.shape[0], seq=x.shape[1], d_ff=w_down.shape[1] * N_CHIPS)
    return _step(dims, x, loss_target, w, m, v)
```

```python
import functools
import math
from typing import NamedTuple

import jax
import jax.numpy as jnp
from jax import lax
from jax.experimental import pallas as pl
from jax.experimental.pallas import tpu as pltpu

F32 = jnp.float32
BF16 = jnp.bfloat16

RMS_EPS = 1e-6
ATTN_BLOCK = 128
DILATIONS = (1, 4, 16)
CONV_HALO = 32
FFN_HALO = 8
ADAM_LR, ADAM_B1, ADAM_B2, ADAM_EPS, ADAM_WD, ADAM_STEP = 0.001, 0.9, 0.999, 1e-08, 0.01, 10
V7X_VMEM_LIMIT_BYTES = 56 * 2 ** 20
N_CHIPS = 4
MESH = pl.DeviceIdType.MESH


class Dims(NamedTuple):
    d_model: int = 1024
    n_heads: int = 16
    head_dim: int = 64
    d_ff: int = 2816
    seq: int = 2048
    batch_local: int = 2
    conv_width: int = 31
    ffn_conv_width: int = 3

    @property
    def tokens(self):
        return self.seq * self.batch_local


def _params(*semantics):
    return pltpu.CompilerParams(dimension_semantics=semantics, vmem_limit_bytes=V7X_VMEM_LIMIT_BYTES)


ANY = pl.BlockSpec(memory_space=pl.ANY)


def _ordered(body, n_inputs, after):
    after = [] if after is None else list(after) if isinstance(after, (list, tuple)) else [after]
    if not after:
        return body, [], []

    def wrapped(*refs):
        return body(*refs[:n_inputs], *refs[n_inputs + len(after):])

    return wrapped, [ANY] * len(after), after


def _pick(n, target, mult=128):
    if n <= target:
        return n
    best = None
    for t in range(mult, target + 1, mult):
        if n % t == 0:
            best = t
    assert best is not None, (n, target, mult)
    return best


def _sigmoid(v):
    return 1.0 / (1.0 + jnp.exp(-v))


def _mm_nn(a, w, *, out_dtype, name, residual=None, after=None, tm=1024, tn=1408, tk=2816):
    m, k = a.shape
    nsh, k2, c = w.shape
    assert k == k2 and a.dtype == BF16 and w.dtype == BF16
    n = nsh * c
    tm, tn, tk = _pick(m, tm, 8), _pick(c, tn), _pick(k, tk)
    nk, cpn = k // tk, c // tn

    def body(*refs):
        if residual is None:
            a_ref, w_ref, o_ref, acc = refs
        else:
            a_ref, w_ref, r_ref, o_ref, acc = refs
        prod = jnp.dot(a_ref[...], w_ref[...], preferred_element_type=F32)

        def finish(total):
            if residual is not None:
                total = total + r_ref[...]
            o_ref[...] = total.astype(out_dtype)

        if nk == 1:
            finish(prod)
        else:
            kk = pl.program_id(2)

            @pl.when(kk == 0)
            def _():
                acc[...] = prod

            @pl.when(kk > 0)
            def _():
                acc[...] += prod

            @pl.when(kk == nk - 1)
            def _():
                finish(acc[...])

    in_specs = [pl.BlockSpec((tm, tk), lambda i, j, kk: (i, kk)),
                pl.BlockSpec((None, tk, tn), lambda i, j, kk: (j // cpn, kk, j % cpn))]
    args = [a, w]
    if residual is not None:
        in_specs.append(pl.BlockSpec((tm, tn), lambda i, j, kk: (i, j)))
        args.append(residual)
    body, more_specs, more_args = _ordered(body, len(args), after)
    return pl.pallas_call(
        body, name=name, grid=(m // tm, n // tn, nk),
        in_specs=in_specs + more_specs, out_specs=pl.BlockSpec((tm, tn), lambda i, j, kk: (i, j)),
        out_shape=jax.ShapeDtypeStruct((m, n), out_dtype),
        scratch_shapes=[pltpu.VMEM((tm, tn) if nk > 1 else (8, 128), F32)],
        compiler_params=_params("parallel", "parallel", "arbitrary"),
    )(*args, *more_args)


def _proj_residual_norm(a, w, residual, g, *, name, tm=1024):
    m, k = a.shape
    _, k2, n = w.shape
    assert w.shape[0] == 1 and k == k2 and a.dtype == BF16 and w.dtype == BF16
    tm = _pick(m, tm, 8)

    def body(a_ref, w_ref, r_ref, g_ref, y_ref, h_ref):
        y = r_ref[...] + jnp.dot(a_ref[...], w_ref[...], preferred_element_type=F32)
        y_ref[...] = y
        h_ref[...] = (y * lax.rsqrt(jnp.mean(y * y, axis=-1, keepdims=True) + RMS_EPS) * g_ref[...]).astype(BF16)

    rows = lambda width: pl.BlockSpec((tm, width), lambda i: (i, 0))
    return pl.pallas_call(
        body, name=name, grid=(m // tm,),
        in_specs=[rows(k), pl.BlockSpec((None, k, n), lambda i: (0, 0, 0)), rows(n), pl.BlockSpec((1, n), lambda i: (0, 0))],
        out_specs=[rows(n), rows(n)],
        out_shape=[jax.ShapeDtypeStruct((m, n), F32), jax.ShapeDtypeStruct((m, n), BF16)],
        compiler_params=_params("parallel"),
    )(a, w, residual, g)


def _proj_residual_loss(a, w, residual, target, *, name, tm=1024):
    m, k = a.shape
    _, k2, n = w.shape
    assert w.shape[0] == 1 and k == k2 and a.dtype == BF16 and w.dtype == BF16
    tm = _pick(m, tm, 8)

    def body(a_ref, w_ref, r_ref, t_ref, dy_ref, dyb_ref, loss_ref):
        err = r_ref[...] + jnp.dot(a_ref[...], w_ref[...], preferred_element_type=F32) - t_ref[...]
        dy = err * (1.0 / n)
        dy_ref[...] = dy
        dyb_ref[...] = dy.astype(BF16)
        part = jnp.sum(jnp.sum(err * err, axis=-1, keepdims=True), axis=0, keepdims=True) * (0.5 / n)
        _accumulate(loss_ref, jnp.broadcast_to(part, (8, 128)), pl.program_id(0) == 0)

    rows = lambda width: pl.BlockSpec((tm, width), lambda i: (i, 0))
    return pl.pallas_call(
        body, name=name, grid=(m // tm,),
        in_specs=[rows(k), pl.BlockSpec((None, k, n), lambda i: (0, 0, 0)), rows(n), rows(n)],
        out_specs=[rows(n), rows(n), pl.BlockSpec((8, 128), lambda i: (0, 0))],
        out_shape=[jax.ShapeDtypeStruct((m, n), F32), jax.ShapeDtypeStruct((m, n), BF16),
                   jax.ShapeDtypeStruct((8, 128), F32)],
        compiler_params=_params("arbitrary"),
    )(a, w, residual, target)


def _mm_nt(a, w, *, out_dtype, name, after=None, tm=1024, tn=1408, tk=1792):
    m, k = a.shape
    nsh, r, c = w.shape
    assert k == nsh * c and a.dtype == BF16 and w.dtype == BF16
    tm, tn, tk = _pick(m, tm, 8), _pick(r, tn), _pick(c, tk)
    nk, cpk = k // tk, c // tk

    def body(a_ref, w_ref, o_ref, acc):
        prod = lax.dot_general(a_ref[...], w_ref[...], (((1,), (1,)), ((), ())), preferred_element_type=F32)
        if nk == 1:
            o_ref[...] = prod.astype(out_dtype)
        else:
            kk = pl.program_id(2)

            @pl.when(kk == 0)
            def _():
                acc[...] = prod

            @pl.when(kk > 0)
            def _():
                acc[...] += prod

            @pl.when(kk == nk - 1)
            def _():
                o_ref[...] = acc[...].astype(out_dtype)

    body, more_specs, more_args = _ordered(body, 2, after)
    return pl.pallas_call(
        body, name=name, grid=(m // tm, r // tn, nk),
        in_specs=[pl.BlockSpec((tm, tk), lambda i, j, kk: (i, kk)),
                  pl.BlockSpec((None, tn, tk), lambda i, j, kk: (kk // cpk, j, kk % cpk))] + more_specs,
        out_specs=pl.BlockSpec((tm, tn), lambda i, j, kk: (i, j)),
        out_shape=jax.ShapeDtypeStruct((m, r), out_dtype),
        scratch_shapes=[pltpu.VMEM((tm, tn) if nk > 1 else (8, 128), F32)],
        compiler_params=_params("parallel", "parallel", "arbitrary"),
    )(a, w, *more_args)


MM_TN_VMEM_BYTES = 44 * 2 ** 20


def _mm_tn(a, b, *, n_shards, name, tm=1408, tn=1408):
    t, m = a.shape
    t2, n = b.shape
    assert t == t2 and a.dtype == BF16 and b.dtype == BF16
    c = n // n_shards
    tm, tn = _pick(m, tm), _pick(c, tn)
    if m // tm == 1 and n // tn == 1 and tn % (2 * LANES) == 0:
        tn //= 2
    fixed = 2 * tm * tn * 6
    if 4 * t * (tm + tn) + fixed <= MM_TN_VMEM_BYTES:
        tk = t
    else:
        tk = _pick(t, (MM_TN_VMEM_BYTES - fixed - 4 * tm * tn) // (4 * (tm + tn)), 8)
    nk, cpn = t // tk, c // tn

    def body(a_ref, b_ref, o_ref, ob_ref, acc):
        kk = pl.program_id(2)
        prod = lax.dot_general(a_ref[...], b_ref[...], (((0,), (0,)), ((), ())), preferred_element_type=F32)

        def finish(total):
            o_ref[...] = total
            ob_ref[...] = total.astype(BF16)

        if nk == 1:
            finish(prod)
        else:
            @pl.when(kk == 0)
            def _():
                acc[...] = prod

            @pl.when(kk > 0)
            def _():
                acc[...] += prod

            @pl.when(kk == nk - 1)
            def _():
                finish(acc[...])

    out_spec = pl.BlockSpec((None, tm, tn), lambda i, j, kk: (j // cpn, i, j % cpn))
    return pl.pallas_call(
        body, name=name, grid=(m // tm, n // tn, nk),
        in_specs=[pl.BlockSpec((tk, tm), lambda i, j, kk: (kk, i)),
                  pl.BlockSpec((tk, tn), lambda i, j, kk: (kk, j))],
        out_specs=[out_spec, out_spec],
        out_shape=[jax.ShapeDtypeStruct((n_shards, m, c), F32), jax.ShapeDtypeStruct((n_shards, m, c), BF16)],
        scratch_shapes=[pltpu.VMEM((tm, tn) if nk > 1 else (8, 128), F32)],
        compiler_params=_params("parallel", "parallel", "arbitrary"),
    )(a, b)


def _row_spec(tr, width, col=0):
    return pl.BlockSpec((tr, width), lambda i, col=col: (i, col))


def _vec_spec(width, col=0):
    return pl.BlockSpec((1, width), lambda i, col=col: (0, col))


def _accumulate(ref, value, first):
    @pl.when(first)
    def _():
        ref[...] = value

    @pl.when(jnp.logical_not(first))
    def _():
        ref[...] += value


def _rmsnorm_fwd(x, g, *, name, tr=512):
    t, d = x.shape
    tr = _pick(t, tr, 8)

    def body(x_ref, g_ref, o_ref):
        xv = x_ref[...]
        r = lax.rsqrt(jnp.mean(xv * xv, axis=-1, keepdims=True) + RMS_EPS)
        o_ref[...] = (xv * r * g_ref[...]).astype(BF16)

    return pl.pallas_call(
        body, name=name, grid=(t // tr,),
        in_specs=[_row_spec(tr, d), _vec_spec(d)], out_specs=_row_spec(tr, d),
        out_shape=jax.ShapeDtypeStruct((t, d), BF16), compiler_params=_params("parallel"),
    )(x, g)


def _rmsnorm_bwd(x, g, dy, dres, *, name, want_bf16, tr=512):
    t, d = x.shape
    tr = _pick(t, tr, 8)

    def body(x_ref, g_ref, dy_ref, dres_ref, *outs):
        dx_ref, dg_ref = outs[0], outs[-1]
        xv, dyv = x_ref[...], dy_ref[...].astype(F32)
        r = lax.rsqrt(jnp.mean(xv * xv, axis=-1, keepdims=True) + RMS_EPS)
        gy = dyv * g_ref[...]
        dx = dres_ref[...] + r * gy - xv * (r * r * r) * jnp.mean(xv * gy, axis=-1, keepdims=True)
        dx_ref[...] = dx
        if want_bf16:
            outs[1][...] = dx.astype(BF16)
        _accumulate(dg_ref, jnp.sum(dyv * xv * r, axis=0, keepdims=True), pl.program_id(0) == 0)

    out_shape = [jax.ShapeDtypeStruct((t, d), F32)]
    out_specs = [_row_spec(tr, d)]
    if want_bf16:
        out_shape.append(jax.ShapeDtypeStruct((t, d), BF16))
        out_specs.append(_row_spec(tr, d))
    out_shape.append(jax.ShapeDtypeStruct((1, d), F32))
    out_specs.append(_vec_spec(d))
    return pl.pallas_call(
        body, name=name, grid=(t // tr,),
        in_specs=[_row_spec(tr, d), _vec_spec(d), _row_spec(tr, d), _row_spec(tr, d)],
        out_specs=out_specs, out_shape=out_shape, compiler_params=_params("arbitrary"),
    )(x, g, dy, dres)


CONV_ROWS = 16


def _seq_specs(dims, ts, width, halo, col, *, nxt=False):
    nst, per = dims.seq // ts, ts // halo
    last = dims.tokens // halo - 1
    cur = pl.BlockSpec((ts, width), lambda b, i: (b * nst + i, col))
    if nxt:
        edge = pl.BlockSpec((halo, width), lambda b, i: (jnp.minimum((b * nst + i + 1) * per, last), col))
    else:
        edge = pl.BlockSpec((halo, width), lambda b, i: (jnp.maximum((b * nst + i) * per - 1, 0), col))
    return cur, edge


SUBLANES = 8


def _shifted_copies(buf, shifted):
    rows = shifted.shape[1]
    for s in range(1, SUBLANES):
        shifted[s - 1] = buf[pl.ds(s, rows), :]


def _window(buf, shifted, start, size):
    a, s = divmod(start, SUBLANES)
    src = buf if s == 0 else shifted.at[s - 1]
    return src[pl.ds(SUBLANES * a, size), :]


def _conv_branch_fwd(z, w, b, g, dims, *, name, ts=128):
    t, c, kw = z.shape[0], dims.d_model, dims.conv_width
    base = CONV_HALO - (kw - 1)

    def body(av_ref, hv_ref, ag_ref, hg_ref, w_ref, b_ref, g_ref, a1_ref, a3_ref, buf, shifted):
        i = pl.program_id(1)
        buf[CONV_HALO:, :] = av_ref[...].astype(F32) * _sigmoid(ag_ref[...].astype(F32))
        buf[0:CONV_HALO, :] = jnp.where(i > 0, hv_ref[...].astype(F32) * _sigmoid(hg_ref[...].astype(F32)), 0.0)
        _shifted_copies(buf, shifted)
        for r0 in range(0, ts, CONV_ROWS):
            acc = jnp.broadcast_to(b_ref[...], (CONV_ROWS, c))
            for k in range(kw):
                acc = acc + w_ref[k:k + 1, :] * _window(buf, shifted, r0 + base + k, CONV_ROWS)
            a1_ref[r0:r0 + CONV_ROWS, :] = acc
            a2 = acc * lax.rsqrt(jnp.mean(acc * acc, axis=-1, keepdims=True) + RMS_EPS) * g_ref[...]
            a3_ref[r0:r0 + CONV_ROWS, :] = (a2 * _sigmoid(a2)).astype(BF16)

    vec = pl.BlockSpec((1, c), lambda b, i: (0, 0))
    out = pl.BlockSpec((ts, c), lambda b, i: (b * (dims.seq // ts) + i, 0))
    return pl.pallas_call(
        body, name=name, grid=(dims.batch_local, dims.seq // ts),
        in_specs=[*_seq_specs(dims, ts, c, CONV_HALO, 0), *_seq_specs(dims, ts, c, CONV_HALO, 1),
                  pl.BlockSpec((CONV_HALO, c), lambda b, i: (0, 0)), vec, vec],
        out_specs=[out, out],
        out_shape=[jax.ShapeDtypeStruct((t, c), F32), jax.ShapeDtypeStruct((t, c), BF16)],
        scratch_shapes=[pltpu.VMEM((CONV_HALO + ts, c), F32),
                        pltpu.VMEM((SUBLANES - 1, CONV_HALO + ts - SUBLANES, c), F32)],
        compiler_params=_params("parallel", "parallel"),
    )(z, z, z, z, w, b, g)


def _conv_norm_bwd(da3, a1, g, *, name, tr=256):
    t, c = a1.shape
    tr = _pick(t, tr, 8)

    def body(d_ref, a_ref, g_ref, o_ref, dg_ref):
        a1v, gv = a_ref[...], g_ref[...]
        r = lax.rsqrt(jnp.mean(a1v * a1v, axis=-1, keepdims=True) + RMS_EPS)
        a2 = a1v * r * gv
        sg = _sigmoid(a2)
        da2 = d_ref[...].astype(F32) * sg * (1.0 + a2 * (1.0 - sg))
        gy = da2 * gv
        o_ref[...] = r * gy - a1v * (r * r * r) * jnp.mean(a1v * gy, axis=-1, keepdims=True)
        _accumulate(dg_ref, jnp.sum(da2 * a1v * r, axis=0, keepdims=True), pl.program_id(0) == 0)

    return pl.pallas_call(
        body, name=name, grid=(t // tr,),
        in_specs=[_row_spec(tr, c), _row_spec(tr, c), _vec_spec(c)],
        out_specs=[_row_spec(tr, c), _vec_spec(c)],
        out_shape=[jax.ShapeDtypeStruct((t, c), F32), jax.ShapeDtypeStruct((1, c), F32)],
        compiler_params=_params("arbitrary"),
    )(da3, a1, g)


def _conv_branch_bwd(da1, z, w, rest_of_dz, dims, *, name, ts=128):
    t, c, kw = z.shape[0], dims.d_model, dims.conv_width
    nst = dims.seq // ts
    base = CONV_HALO - (kw - 1)
    n_rest = len(rest_of_dz)
    total = 2 * c + sum(r.shape[1] for r in rest_of_dz)

    def body(d_ref, dn_ref, av_ref, hv_ref, ag_ref, hg_ref, w_ref, *more):
        rest_refs = more[:n_rest]
        dz_ref, dw_ref, db_ref, abuf, dbuf, ashift, dshift = more[n_rest:]
        col = 2 * c
        for r in rest_refs:
            dz_ref[:, col:col + r.shape[1]] = r[...]
            col += r.shape[1]
        i = pl.program_id(1)
        first = jnp.logical_and(pl.program_id(0) == 0, i == 0)
        abuf[CONV_HALO:, :] = av_ref[...].astype(F32) * _sigmoid(ag_ref[...].astype(F32))
        abuf[0:CONV_HALO, :] = jnp.where(i > 0, hv_ref[...].astype(F32) * _sigmoid(hg_ref[...].astype(F32)), 0.0)
        d1 = d_ref[...]
        dbuf[0:ts, :] = d1
        dbuf[ts:, :] = jnp.where(i < nst - 1, dn_ref[...], 0.0)
        _shifted_copies(abuf, ashift)
        _shifted_copies(dbuf, dshift)

        @pl.when(first)
        def _():
            dw_ref[...] = jnp.zeros_like(dw_ref)
            db_ref[...] = jnp.zeros_like(db_ref)

        db_ref[...] += jnp.sum(d1, axis=0, keepdims=True)
        for k in range(kw):
            dw_ref[k:k + 1, :] += jnp.sum(d1 * _window(abuf, ashift, base + k, ts), axis=0, keepdims=True)
        for r0 in range(0, ts, CONV_ROWS):
            acc = jnp.zeros((CONV_ROWS, c), F32)
            for k in range(kw):
                acc = acc + w_ref[k:k + 1, :] * _window(dbuf, dshift, r0 + (kw - 1) - k, CONV_ROWS)
            av = av_ref[r0:r0 + CONV_ROWS, :].astype(F32)
            sg = _sigmoid(ag_ref[r0:r0 + CONV_ROWS, :].astype(F32))
            dz_ref[r0:r0 + CONV_ROWS, 0:c] = (acc * sg).astype(BF16)
            dz_ref[r0:r0 + CONV_ROWS, c:2 * c] = (acc * av * sg * (1.0 - sg)).astype(BF16)

    cur, nxt = _seq_specs(dims, ts, c, CONV_HALO, 0, nxt=True)
    return pl.pallas_call(
        body, name=name, grid=(dims.batch_local, nst),
        in_specs=[cur, nxt, *_seq_specs(dims, ts, c, CONV_HALO, 0), *_seq_specs(dims, ts, c, CONV_HALO, 1),
                  pl.BlockSpec((CONV_HALO, c), lambda b, i: (0, 0))]
        + [pl.BlockSpec((ts, r.shape[1]), lambda b, i: (b * nst + i, 0)) for r in rest_of_dz],
        out_specs=[pl.BlockSpec((ts, total), lambda b, i: (b * nst + i, 0)),
                   pl.BlockSpec((CONV_HALO, c), lambda b, i: (0, 0)), pl.BlockSpec((1, c), lambda b, i: (0, 0))],
        out_shape=[jax.ShapeDtypeStruct((t, total), BF16), jax.ShapeDtypeStruct((CONV_HALO, c), F32),
                   jax.ShapeDtypeStruct((1, c), F32)],
        scratch_shapes=[pltpu.VMEM((CONV_HALO + ts, c), F32)] * 2
        + [pltpu.VMEM((SUBLANES - 1, CONV_HALO + ts - SUBLANES, c), F32)] * 2,
        compiler_params=_params("arbitrary", "arbitrary"),
    )(da1, da1, z, z, z, z, w, *rest_of_dz)


FFN_ROWS = 16
FFN_COLS = 256


def _ffn_chunks(ts, f):
    cw = _pick(f, FFN_COLS)
    return [(r0, c0, cw) for r0 in range(0, ts, FFN_ROWS) for c0 in range(0, f, cw)]


def _tap_sources(buf, moved, offsets, rows):
    taps, used = [], 0
    for off in offsets:
        if off % SUBLANES:
            moved[used] = buf[pl.ds(off, rows), :]
            taps.append((moved.at[used], 0))
            used += 1
        else:
            taps.append((buf, off))
    return taps


def _moved_copies(offsets):
    return sum(1 for off in offsets if off % SUBLANES)


def _taps_sum(taps, w_ref, init, r0, cols):
    for k, (src, off) in enumerate(taps):
        init = init + w_ref[k:k + 1, cols] * src[pl.ds(off + r0, init.shape[0]), cols]
    return init


def _ffn_bwd(dact, up, w, b, dims, *, name, ts=128):
    t, f, kw = up.shape[0], dims.d_ff, dims.ffn_conv_width
    nst = dims.seq // ts
    fwd_offsets = [FFN_HALO - (kw - 1) + k for k in range(kw)]
    bwd_offsets = [(kw - 1) - k for k in range(kw)]
    dact_halo = 2 * FFN_HALO

    def body(d_ref, dn_ref, up_ref, hp_ref, hn_ref, w_ref, b_ref, o_ref, dw_ref, db_ref, buf, moved, dbuf, dmoved):
        i = pl.program_id(1)
        first = jnp.logical_and(pl.program_id(0) == 0, i == 0)
        more = i < nst - 1
        buf[0:FFN_HALO, :] = jnp.where(i > 0, hp_ref[...], 0.0)
        buf[FFN_HALO:FFN_HALO + ts, :] = up_ref[...]
        buf[FFN_HALO + ts:, :] = hn_ref[...]
        taps = _tap_sources(buf, moved, fwd_offsets, ts + FFN_HALO)

        def du_chunk(r0, rows, c0, cw, d):
            vcols, gcols = slice(c0, c0 + cw), slice(f + c0, f + c0 + cw)
            uv = _taps_sum(taps, w_ref, jnp.broadcast_to(b_ref[:, vcols], (rows, cw)), r0, vcols)
            ug = _taps_sum(taps, w_ref, jnp.broadcast_to(b_ref[:, gcols], (rows, cw)), r0, gcols)
            sg = _sigmoid(ug)
            dbuf[r0:r0 + rows, vcols] = d * ug * sg
            dbuf[r0:r0 + rows, gcols] = d * uv * sg * (1.0 + ug * (1.0 - sg))

        for r0, c0, cw in _ffn_chunks(ts, f):
            du_chunk(r0, FFN_ROWS, c0, cw, d_ref[r0:r0 + FFN_ROWS, c0:c0 + cw].astype(F32))
        for _, c0, cw in _ffn_chunks(FFN_ROWS, f):
            d_next = dn_ref[:, c0:c0 + cw].astype(F32)[0:FFN_HALO]
            du_chunk(ts, FFN_HALO, c0, cw, jnp.where(more, d_next, 0.0))

        @pl.when(first)
        def _():
            dw_ref[...] = jnp.zeros_like(dw_ref)
            db_ref[...] = jnp.zeros_like(db_ref)

        du = dbuf[0:ts, :]
        db_ref[...] += jnp.sum(du, axis=0, keepdims=True)
        for k, (src, off) in enumerate(taps):
            dw_ref[k:k + 1, :] += jnp.sum(du * src[pl.ds(off, ts), :], axis=0, keepdims=True)

        dtaps = _tap_sources(dbuf, dmoved, bwd_offsets, ts)
        for r0, c0, cw in _ffn_chunks(ts, 2 * f):
            cols = slice(c0, c0 + cw)
            o_ref[r0:r0 + FFN_ROWS, cols] = _taps_sum(dtaps, w_ref, jnp.zeros((FFN_ROWS, cw), F32), r0, cols).astype(BF16)

    up_cur, up_prev = _seq_specs(dims, ts, 2 * f, FFN_HALO, 0)
    _, up_next = _seq_specs(dims, ts, 2 * f, FFN_HALO, 0, nxt=True)
    d_cur, d_next = _seq_specs(dims, ts, f, dact_halo, 0, nxt=True)
    full = lambda rows: pl.BlockSpec((rows, 2 * f), lambda b_, i: (0, 0))
    return pl.pallas_call(
        body, name=name, grid=(dims.batch_local, nst),
        in_specs=[d_cur, d_next, up_cur, up_prev, up_next, full(FFN_HALO), full(1)],
        out_specs=[pl.BlockSpec((ts, 2 * f), lambda b_, i: (b_ * nst + i, 0)), full(FFN_HALO), full(1)],
        out_shape=[jax.ShapeDtypeStruct((t, 2 * f), BF16), jax.ShapeDtypeStruct((FFN_HALO, 2 * f), F32),
                   jax.ShapeDtypeStruct((1, 2 * f), F32)],
        scratch_shapes=[pltpu.VMEM((ts + 2 * FFN_HALO, 2 * f), F32),
                        pltpu.VMEM((_moved_copies(fwd_offsets), ts + FFN_HALO, 2 * f), F32),
                        pltpu.VMEM((ts + FFN_HALO, 2 * f), F32),
                        pltpu.VMEM((_moved_copies(bwd_offsets), ts, 2 * f), F32)],
        compiler_params=_params("arbitrary", "arbitrary"),
    )(dact, dact, up, up, up, w, b)


def _ffn_act_fwd(up, w, b, dims, *, name, ts=128):
    t, f, kw = up.shape[0], dims.d_ff, dims.ffn_conv_width
    offsets = [FFN_HALO - (kw - 1) + k for k in range(kw)]

    def body(up_ref, h_ref, w_ref, b_ref, o_ref, buf, moved):
        buf[FFN_HALO:, :] = up_ref[...]
        buf[0:FFN_HALO, :] = jnp.where(pl.program_id(1) > 0, h_ref[...], 0.0)
        taps = _tap_sources(buf, moved, offsets, ts)
        for r0, c0, cw in _ffn_chunks(ts, f):
            vcols, gcols = slice(c0, c0 + cw), slice(f + c0, f + c0 + cw)
            uv = _taps_sum(taps, w_ref, jnp.broadcast_to(b_ref[:, vcols], (FFN_ROWS, cw)), r0, vcols)
            ug = _taps_sum(taps, w_ref, jnp.broadcast_to(b_ref[:, gcols], (FFN_ROWS, cw)), r0, gcols)
            o_ref[r0:r0 + FFN_ROWS, vcols] = (ug * _sigmoid(ug) * uv).astype(BF16)

    full = lambda rows: pl.BlockSpec((rows, 2 * f), lambda b_, i: (0, 0))
    return pl.pallas_call(
        body, name=name, grid=(dims.batch_local, dims.seq // ts),
        in_specs=[*_seq_specs(dims, ts, 2 * f, FFN_HALO, 0), full(FFN_HALO), full(1)],
        out_specs=pl.BlockSpec((ts, f), lambda b_, i: (b_ * (dims.seq // ts) + i, 0)),
        out_shape=jax.ShapeDtypeStruct((t, f), BF16),
        scratch_shapes=[pltpu.VMEM((FFN_HALO + ts, 2 * f), F32), pltpu.VMEM((_moved_copies(offsets), ts, 2 * f), F32)],
        compiler_params=_params("parallel", "parallel"),
    )(up, up, w, b)


def _dot_nt(a, b):
    return lax.dot_general(a, b, (((1,), (1,)), ((), ())), preferred_element_type=F32)


def _dot_tn(a, b):
    return lax.dot_general(a, b, (((0,), (0,)), ((), ())), preferred_element_type=F32)


LANES = 128
MASK_BIAS = 1e30
RESIDUE_DILATIONS = tuple(d for d in DILATIONS if d > 1)


def _rows_to_residues(value, out_ref, scr, d):
    rows, width = value.shape
    for c in range(width // LANES):
        cols = slice(LANES * c, LANES * (c + 1))
        scr[c] = value[:, cols]
        for r in range(d):
            out_ref[r, :, cols] = scr[c, pl.ds(r, rows // d, stride=d), :].astype(out_ref.dtype)


def _residues_to_rows(in_ref, scr, d):
    _, n, width = in_ref.shape
    slabs = []
    for c in range(width // LANES):
        cols = slice(LANES * c, LANES * (c + 1))
        for r in range(d):
            scr[c, pl.ds(r, n, stride=d), :] = in_ref[r, :, cols].astype(F32)
        slabs.append(scr[c])
    return slabs[0] if len(slabs) == 1 else jnp.concatenate(slabs, axis=1)


def _residue_shape(dims, d, width, dtype):
    return jax.ShapeDtypeStruct((dims.batch_local, d, dims.seq // d, width), dtype)


def _residue_spec(dims, d, tr, width):
    tiles = dims.seq // tr
    return pl.BlockSpec((None, d, tr // d, width), lambda i: (i // tiles, 0, i % tiles, 0))


def _head_sum_matrix(dims):
    a = dims.n_heads * dims.head_dim
    head = jnp.arange(a, dtype=jnp.int32) // dims.head_dim
    return (head[:, None] == jnp.arange(LANES, dtype=jnp.int32)[None, :]).astype(BF16)


def _two_pass_dot(v, m):
    hi = v.astype(BF16)
    lo = (v - hi.astype(F32)).astype(BF16)
    return jnp.dot(hi, m, preferred_element_type=F32) + jnp.dot(lo, m, preferred_element_type=F32)


def _residue_permutations(tr):
    out = []
    for d in RESIDUE_DILATIONS:
        dst = jnp.arange(tr, dtype=jnp.int32)
        src = d * (dst % (tr // d)) + dst // (tr // d)
        out.append((src[:, None] == jnp.arange(tr, dtype=jnp.int32)[None, :]).astype(BF16))
    return out


def _bf16_rows_to_residues(value, out_ref, perm_ref, d):
    n = value.shape[0] // d
    moved = jnp.dot(perm_ref[...], value, preferred_element_type=F32).astype(out_ref.dtype)
    for r in range(d):
        out_ref[r] = moved[r * n:(r + 1) * n]


def _bf16_residues_to_rows(in_ref, back_ref):
    d = in_ref.shape[0]
    stacked = jnp.concatenate([in_ref[r] for r in range(d)], axis=0)
    return jnp.dot(back_ref[...], stacked, preferred_element_type=F32)


def _qkv_layouts_fwd(z, gq, gk, head_ones, dims, *, name, tr=256):
    t = z.shape[0]
    a = dims.n_heads * dims.head_dim
    q_scale = dims.head_dim ** -0.5
    nres = len(RESIDUE_DILATIONS)

    def body(q_ref, k_ref, v_ref, gq_ref, gk_ref, sum_ref, spread_ref, *rest):
        perm_refs, outs = rest[:nres], rest[nres:]
        qv, kv = q_ref[...].astype(F32), k_ref[...].astype(F32)
        mean = lambda val: _two_pass_dot(_two_pass_dot(val, sum_ref[...]), spread_ref[...]) * (1.0 / dims.head_dim)
        rq = lax.rsqrt(mean(qv * qv) + RMS_EPS)
        rk = lax.rsqrt(mean(kv * kv) + RMS_EPS)
        values = ((qv * rq * gq_ref[...] * q_scale).astype(BF16), (kv * rk * gk_ref[...]).astype(BF16), v_ref[...])
        for j, val in enumerate(values):
            outs[j][...] = val
            for g, d in enumerate(RESIDUE_DILATIONS):
                _bf16_rows_to_residues(val, outs[3 * (g + 1) + j], perm_refs[g], d)

    out_specs = [_row_spec(tr, a)] * 3
    out_shape = [jax.ShapeDtypeStruct((t, a), BF16)] * 3
    for d in RESIDUE_DILATIONS:
        out_specs += [_residue_spec(dims, d, tr, a)] * 3
        out_shape += [_residue_shape(dims, d, a, BF16)] * 3
    outs = pl.pallas_call(
        body, name=name, grid=(t // tr,),
        in_specs=[_row_spec(tr, a, 2), _row_spec(tr, a, 3), _row_spec(tr, a, 4), _vec_spec(a), _vec_spec(a),
                  pl.BlockSpec((a, LANES), lambda i: (0, 0)), pl.BlockSpec((LANES, a), lambda i: (0, 0))]
        + [pl.BlockSpec((tr, tr), lambda i: (0, 0))] * nres,
        out_specs=out_specs, out_shape=out_shape,
        compiler_params=_params("parallel"),
    )(z, z, z, gq, gk, *head_ones, *_residue_permutations(tr))
    return {d: tuple(outs[3 * g:3 * g + 3]) for g, d in enumerate((1,) + RESIDUE_DILATIONS)}


ATTN_RESIDUES_PER_STEP = 4
ATTN_RESIDUES_PER_STEP_WINDOWED = 2


def _attn_groups(dims, dil):
    return (dims.batch_local, dil) if dil > 1 else (1, dims.batch_local)


def _attn_array(x, dims, dil):
    return x if dil > 1 else x.reshape(1, dims.batch_local, dims.seq, x.shape[-1])


def _attn_residues(dims, dil):
    one_block = dims.seq // dil == ATTN_BLOCK
    return math.gcd(_attn_groups(dims, dil)[1], ATTN_RESIDUES_PER_STEP if one_block else ATTN_RESIDUES_PER_STEP_WINDOWED)


def _per_residue(body, rs):
    if rs == 1:
        return body

    def stepped(*refs):
        for r in range(rs):
            body(*[ref.at[r] for ref in refs])

    return stepped


def _attn_specs(dims, dil, width):
    blk = ATTN_BLOCK
    nb = dims.seq // dil // blk
    rs = _attn_residues(dims, dil)
    lead, groups = _attn_groups(dims, dil)
    if rs > 1 and nb == 1:
        grid = (lead, groups // rs)
        at = lambda f: pl.BlockSpec((None, rs, blk, width), lambda b, r: (b, r, 0, 0))
    elif rs > 1:
        grid = (lead, groups // rs, nb)
        at = lambda f: pl.BlockSpec((None, rs, blk, width), lambda b, r, i: (b, r, f(i), 0))
    else:
        grid = (lead, groups, nb)
        at = lambda f: pl.BlockSpec((None, None, blk, width), lambda b, r, i: (b, r, f(i), 0))
    return grid, at(lambda i: i), at(lambda i: jnp.maximum(i - 1, 0)), at(lambda i: jnp.minimum(i + 1, nb - 1))


def _head_slopes(n_heads):
    h = lax.broadcasted_iota(jnp.int32, (n_heads, 1, 1), 0).astype(F32)
    return jnp.exp((h + 1.0) * (-8.0 / n_heads * math.log(2.0)))


def _pair_masks(hd):
    low = lax.broadcasted_iota(jnp.int32, (1, 2 * hd), 1) < hd
    return low, jnp.logical_not(low)


def _attn_fwd(q, k, v, dims, dil, *, name):
    a = dims.n_heads * dims.head_dim
    heads, hd, blk = dims.n_heads, dims.head_dim, ATTN_BLOCK
    assert 2 * hd == LANES and heads % 2 == 0 and heads <= LANES
    nb = dims.seq // dil // blk
    has_prev = nb > 1
    nkeys = 2 * blk if has_prev else blk
    grid, cur, prev, _ = _attn_specs(dims, dil, a)
    _, cur_stat, _, _ = _attn_specs(dims, dil, LANES)

    def body(*refs):
        if has_prev:
            q_ref, kc_ref, vc_ref, kp_ref, vp_ref, o_ref, lse_ref, s_scr, p_scr, k_st, v_st = refs
            k_st[0:blk, :], k_st[blk:, :] = kp_ref[...], kc_ref[...]
            v_st[0:blk, :], v_st[blk:, :] = vp_ref[...], vc_ref[...]
        else:
            q_ref, k_st, v_st, o_ref, lse_ref, s_scr, p_scr = refs
        low, high = _pair_masks(hd)

        for hp in range(heads // 2):
            sl = slice(LANES * hp, LANES * (hp + 1))
            q2 = q_ref[:, sl]
            kcat = k_st[:, sl]
            s_scr[2 * hp] = _dot_nt(jnp.where(low, q2, jnp.zeros_like(q2)), kcat)
            s_scr[2 * hp + 1] = _dot_nt(jnp.where(high, q2, jnp.zeros_like(q2)), kcat)

        iq = lax.broadcasted_iota(jnp.int32, (blk, nkeys), 0)
        jk = lax.broadcasted_iota(jnp.int32, (blk, nkeys), 1)
        if has_prev:
            steps = iq + blk - jk
            valid = (steps >= 0) & (steps <= blk) & ((jk >= blk) | (pl.program_id(len(grid) - 1) > 0))
        else:
            steps = iq - jk
            valid = steps >= 0
        bias = jnp.where(valid, steps.astype(F32) * (-float(dil)), -MASK_BIAS)
        s = s_scr[...] + _head_slopes(heads) * bias[None]
        m = jnp.max(s, axis=-1, keepdims=True)
        p = jnp.exp(s - m)
        l = jnp.sum(p, axis=-1, keepdims=True)
        p_scr[...] = p.astype(BF16)
        inv = 1.0 / l
        lse = m + jnp.log(l)

        lane = lax.broadcasted_iota(jnp.int32, (blk, LANES), 1)
        stat = jnp.zeros((blk, LANES), F32)
        for hp in range(heads // 2):
            sl = slice(LANES * hp, LANES * (hp + 1))
            vcat = v_st[:, sl]
            pv_a = jnp.dot(p_scr[2 * hp], vcat, preferred_element_type=F32) * inv[2 * hp]
            pv_b = jnp.dot(p_scr[2 * hp + 1], vcat, preferred_element_type=F32) * inv[2 * hp + 1]
            o_ref[:, sl] = jnp.where(low, pv_a, pv_b).astype(BF16)
            stat = jnp.where(lane == 2 * hp, lse[2 * hp], stat)
            stat = jnp.where(lane == 2 * hp + 1, lse[2 * hp + 1], stat)
        lse_ref[...] = stat

    q4, k4, v4 = (_attn_array(x, dims, dil) for x in (q, k, v))
    rs = _attn_residues(dims, dil)
    per_step = lambda shape: shape if rs == 1 else (rs,) + shape
    o, lse = pl.pallas_call(
        _per_residue(body, rs), name=name, grid=grid,
        in_specs=[cur, cur, cur] + ([prev, prev] if has_prev else []),
        out_specs=[cur, cur_stat],
        out_shape=[jax.ShapeDtypeStruct(q4.shape, BF16), jax.ShapeDtypeStruct(q4.shape[:-1] + (LANES,), F32)],
        scratch_shapes=[pltpu.VMEM(per_step((heads, blk, nkeys)), F32), pltpu.VMEM(per_step((heads, blk, nkeys)), BF16)]
        + ([pltpu.VMEM(per_step((nkeys, a)), BF16)] * 2 if has_prev else []),
        compiler_params=_params(*["parallel"] * len(grid)),
    )(q4, k4, v4, *([k4, v4] if has_prev else []))
    return o.reshape(q.shape), lse.reshape(q.shape[:-1] + (LANES,))


def _attn_combine(groups, head_spread, dims, *, name, tr=256):
    t = dims.tokens
    a = dims.n_heads * dims.head_dim
    dils = tuple(groups)

    nres = len(RESIDUE_DILATIONS)

    def body(*refs):
        ins = refs[:2 * len(dils)]
        x_ref = refs[2 * len(dils)]
        back_refs = dict(zip(RESIDUE_DILATIONS, refs[2 * len(dils) + 1:2 * len(dils) + 1 + nres]))
        o_ref = refs[2 * len(dils) + 1 + nres]
        lse_refs = refs[2 * len(dils) + 2 + nres:-1]
        scr_stat = refs[-1]
        outs, stats = [], []
        for g, d in enumerate(dils):
            if d == 1:
                outs.append(ins[2 * g][...].astype(F32))
                stats.append(ins[2 * g + 1][...])
            else:
                outs.append(_bf16_residues_to_rows(ins[2 * g], back_refs[d]))
                stats.append(_residues_to_rows(ins[2 * g + 1], scr_stat, d))
        top = functools.reduce(jnp.maximum, stats)
        weights = [jnp.exp(s - top) for s in stats]
        total = functools.reduce(jnp.add, weights)
        joint = top + jnp.log(total)
        inv = 1.0 / total
        acc = None
        for w, o in zip(weights, outs):
            term = _two_pass_dot(w * inv, x_ref[...]) * o
            acc = term if acc is None else acc + term
        o_ref[...] = acc.astype(BF16)
        for g, d in enumerate(dils):
            if d == 1:
                lse_refs[g][...] = joint
            else:
                _rows_to_residues(joint, lse_refs[g], scr_stat, d)

    in_specs, args, lse_specs, lse_shapes = [], [], [], []
    for d in dils:
        if d == 1:
            in_specs += [_row_spec(tr, a), _row_spec(tr, LANES)]
            lse_specs.append(_row_spec(tr, LANES))
            lse_shapes.append(jax.ShapeDtypeStruct((t, LANES), F32))
        else:
            in_specs += [_residue_spec(dims, d, tr, a), _residue_spec(dims, d, tr, LANES)]
            lse_specs.append(_residue_spec(dims, d, tr, LANES))
            lse_shapes.append(_residue_shape(dims, d, LANES, F32))
        args += list(groups[d])
    outs = pl.pallas_call(
        body, name=name, grid=(t // tr,),
        in_specs=in_specs + [pl.BlockSpec((LANES, a), lambda i: (0, 0))] + [pl.BlockSpec((tr, tr), lambda i: (0, 0))] * nres,
        out_specs=[_row_spec(tr, a)] + lse_specs,
        out_shape=[jax.ShapeDtypeStruct((t, a), BF16)] + lse_shapes,
        scratch_shapes=[pltpu.VMEM((1, tr, LANES), F32)],
        compiler_params=_params("parallel"),
    )(*args, head_spread, *[jnp.transpose(p) for p in _residue_permutations(tr)])
    return outs[0], dict(zip(dils, outs[1:]))


def _attn_bwd_prep(do, o, head_sum, dims, *, name, tr=256):
    t, a = o.shape
    nres = len(RESIDUE_DILATIONS)

    def body(do_ref, o_ref, e_ref, *rest):
        perm_refs, outs, scr_stat = rest[:nres], rest[nres:-1], rest[-1]
        delta = _two_pass_dot(do_ref[...].astype(F32) * o_ref[...].astype(F32), e_ref[...])
        outs[0][...] = delta
        for g, d in enumerate(RESIDUE_DILATIONS):
            _bf16_rows_to_residues(do_ref[...], outs[1 + 2 * g], perm_refs[g], d)
            _rows_to_residues(delta, outs[2 + 2 * g], scr_stat, d)

    out_specs, out_shape = [_row_spec(tr, LANES)], [jax.ShapeDtypeStruct((t, LANES), F32)]
    for d in RESIDUE_DILATIONS:
        out_specs += [_residue_spec(dims, d, tr, a), _residue_spec(dims, d, tr, LANES)]
        out_shape += [_residue_shape(dims, d, a, BF16), _residue_shape(dims, d, LANES, F32)]
    outs = pl.pallas_call(
        body, name=name, grid=(t // tr,),
        in_specs=[_row_spec(tr, a), _row_spec(tr, a), pl.BlockSpec((a, LANES), lambda i: (0, 0))]
        + [pl.BlockSpec((tr, tr), lambda i: (0, 0))] * nres,
        out_specs=out_specs, out_shape=out_shape,
        scratch_shapes=[pltpu.VMEM((1, tr, LANES), F32)],
        compiler_params=_params("parallel"),
    )(do, o, head_sum, *_residue_permutations(tr))
    dos, deltas = {1: do}, {1: outs[0]}
    for g, d in enumerate(RESIDUE_DILATIONS):
        dos[d], deltas[d] = outs[1 + 2 * g], outs[2 + 2 * g]
    return dos, deltas


def _attn_bwd(q, k, v, do, lse, delta, dims, dil, *, name):
    a = dims.n_heads * dims.head_dim
    heads, hd, blk = dims.n_heads, dims.head_dim, ATTN_BLOCK
    nb = dims.seq // dil // blk
    has_next = nb > 1
    nq = 2 * blk if has_next else blk
    grid, cur, _, nxt = _attn_specs(dims, dil, a)
    _, cur_stat, _, nxt_stat = _attn_specs(dims, dil, LANES)

    def body(*refs):
        k_ref, v_ref, q_ref, do_ref, lse_ref, dl_ref = refs[:6]
        if has_next:
            qn_ref, don_ref, lsen_ref, dln_ref = refs[6:10]
            dq_ref, dk_ref, dv_ref, q_st, do_st, s_scr, dp_scr, p_scr, ds_scr, carry = refs[10:]
        else:
            dq_ref, dk_ref, dv_ref, q_st, do_st, s_scr, dp_scr, p_scr, ds_scr = refs[6:]
        j = pl.program_id(len(grid) - 1)
        low, high = _pair_masks(hd)
        q_st[0:blk, :] = q_ref[...]
        do_st[0:blk, :] = do_ref[...]
        if has_next:
            q_st[blk:, :] = qn_ref[...]
            do_st[blk:, :] = don_ref[...]
            lse_all = jnp.concatenate([lse_ref[...], lsen_ref[...]], axis=0)
            dl_all = jnp.concatenate([dl_ref[...], dln_ref[...]], axis=0)
        else:
            lse_all, dl_all = lse_ref[...], dl_ref[...]
        lse_t, dl_t = jnp.transpose(lse_all), jnp.transpose(dl_all)
        lse3 = jnp.stack([lse_t[h:h + 1, :] for h in range(heads)])
        dl3 = jnp.stack([dl_t[h:h + 1, :] for h in range(heads)])

        def halves(x):
            return jnp.where(low, x, jnp.zeros_like(x)), jnp.where(high, x, jnp.zeros_like(x))

        for hp in range(heads // 2):
            sl = slice(LANES * hp, LANES * (hp + 1))
            k2, v2 = k_ref[:, sl], v_ref[:, sl]
            q_a, q_b = halves(q_st[:, sl])
            do_a, do_b = halves(do_st[:, sl])
            s_scr[2 * hp], s_scr[2 * hp + 1] = _dot_nt(k2, q_a), _dot_nt(k2, q_b)
            dp_scr[2 * hp], dp_scr[2 * hp + 1] = _dot_nt(v2, do_a), _dot_nt(v2, do_b)

        jk = lax.broadcasted_iota(jnp.int32, (blk, nq), 0)
        rq = lax.broadcasted_iota(jnp.int32, (blk, nq), 1)
        if has_next:
            iq = jnp.where(rq < blk, rq, rq - blk)
            steps = jnp.where(rq < blk, iq - jk, iq - jk + blk)
            valid = ((rq < blk) & (iq >= jk)) | ((rq >= blk) & (jk >= iq) & (j + 1 < nb))
        else:
            steps, valid = rq - jk, rq >= jk
        bias = jnp.where(valid, steps.astype(F32) * (-float(dil)), -MASK_BIAS)
        p = jnp.exp(s_scr[...] + _head_slopes(heads) * bias[None] - lse3)
        p_scr[...] = p.astype(BF16)
        ds_scr[...] = (p * (dp_scr[...] - dl3)).astype(BF16)

        if has_next:
            @pl.when(j == 0)
            def _():
                carry[...] = jnp.zeros_like(carry)

        for hp in range(heads // 2):
            sl = slice(LANES * hp, LANES * (hp + 1))
            k2 = k_ref[:, sl]
            q_a, q_b = halves(q_st[:, sl])
            do_a, do_b = halves(do_st[:, sl])
            ds_a, ds_b = ds_scr[2 * hp], ds_scr[2 * hp + 1]
            dk_ref[:, sl] = (jnp.dot(ds_a, q_a, preferred_element_type=F32)
                             + jnp.dot(ds_b, q_b, preferred_element_type=F32)).astype(BF16)
            dv_ref[:, sl] = (jnp.dot(p_scr[2 * hp], do_a, preferred_element_type=F32)
                             + jnp.dot(p_scr[2 * hp + 1], do_b, preferred_element_type=F32)).astype(BF16)
            dq2 = jnp.where(low, _dot_tn(ds_a, k2), _dot_tn(ds_b, k2))
            if has_next:
                dq_ref[:, sl] = (carry[:, sl] + dq2[:blk]).astype(BF16)
                carry[:, sl] = dq2[blk:]
            else:
                dq_ref[:, sl] = dq2.astype(BF16)

    args, in_specs = [_attn_array(x, dims, dil) for x in (k, v, q, do, lse, delta)], [cur] * 4 + [cur_stat] * 2
    if has_next:
        args += [args[2], args[3], args[4], args[5]]
        in_specs += [nxt] * 2 + [nxt_stat] * 2
    shape = jax.ShapeDtypeStruct(args[2].shape, BF16)
    rs = _attn_residues(dims, dil)
    per_step = lambda dims_: dims_ if rs == 1 else (rs,) + dims_
    scratch = ([pltpu.VMEM(per_step((nq, a)), BF16)] * 2 + [pltpu.VMEM(per_step((heads, blk, nq)), F32)] * 2
               + [pltpu.VMEM(per_step((heads, blk, nq)), BF16)] * 2)
    if has_next:
        scratch.append(pltpu.VMEM(per_step((blk, a)), F32))
    grads = pl.pallas_call(
        _per_residue(body, rs), name=name, grid=grid, in_specs=in_specs, out_specs=[cur] * 3, out_shape=[shape] * 3,
        scratch_shapes=scratch,
        compiler_params=_params(*["parallel"] * (len(grid) - 1), "arbitrary"),
    )(*args)
    return tuple(g.reshape(q.shape) for g in grads)


def _qkv_layouts_bwd(z, grads, gq, gk, head_ones, dims, *, name, tr=256):
    t = z.shape[0]
    a = dims.n_heads * dims.head_dim
    q_scale = dims.head_dim ** -0.5
    dils = tuple(grads)
    nres = len(RESIDUE_DILATIONS)

    def body(q_ref, k_ref, *rest):
        d_refs = rest[:3 * len(dils)]
        gq_ref, gk_ref, sum_ref, spread_ref = rest[3 * len(dils):3 * len(dils) + 4]
        back_refs = dict(zip(RESIDUE_DILATIONS, rest[3 * len(dils) + 4:3 * len(dils) + 4 + nres]))
        dz_ref, dgq_ref, dgk_ref = rest[3 * len(dils) + 4 + nres:]
        first = pl.program_id(0) == 0
        mean = lambda val: _two_pass_dot(_two_pass_dot(val, sum_ref[...]), spread_ref[...]) * (1.0 / dims.head_dim)

        def total(j):
            acc = None
            for g, d in enumerate(dils):
                ref = d_refs[3 * g + j]
                part = ref[...].astype(F32) if d == 1 else _bf16_residues_to_rows(ref, back_refs[d])
                acc = part if acc is None else acc + part
            return acc

        def norm_bwd(x_ref, dy, g_ref, scale, col, dg_ref):
            xv = x_ref[...].astype(F32)
            dy = dy * scale
            r = lax.rsqrt(mean(xv * xv) + RMS_EPS)
            gy = dy * g_ref[...]
            dx = r * gy - xv * (r * r * r) * mean(xv * gy)
            dz_ref[:, col * a:(col + 1) * a] = dx.astype(BF16)
            _accumulate(dg_ref, jnp.sum(dy * xv * r, axis=0, keepdims=True), first)

        norm_bwd(q_ref, total(0), gq_ref, q_scale, 0, dgq_ref)
        norm_bwd(k_ref, total(1), gk_ref, 1.0, 1, dgk_ref)
        dz_ref[:, 2 * a:3 * a] = total(2).astype(BF16)

    in_specs, args = [_row_spec(tr, a, 2), _row_spec(tr, a, 3)], [z, z]
    for d in dils:
        in_specs += [_row_spec(tr, a) if d == 1 else _residue_spec(dims, d, tr, a)] * 3
        args += list(grads[d])
    in_specs += [_vec_spec(a), _vec_spec(a), pl.BlockSpec((a, LANES), lambda i: (0, 0)),
                 pl.BlockSpec((LANES, a), lambda i: (0, 0))] + [pl.BlockSpec((tr, tr), lambda i: (0, 0))] * nres
    return pl.pallas_call(
        body, name=name, grid=(t // tr,), in_specs=in_specs,
        out_specs=[_row_spec(tr, 3 * a), _vec_spec(a), _vec_spec(a)],
        out_shape=[jax.ShapeDtypeStruct((t, 3 * a), BF16)] + [jax.ShapeDtypeStruct((1, a), F32)] * 2,
        compiler_params=_params("arbitrary"),
    )(*args, gq, gk, *head_ones, *[jnp.transpose(p) for p in _residue_permutations(tr)])


def _mix_fwd(ya, yb, z, gate_b, dims, *, name, tr=512):
    t, d = ya.shape
    tr = _pick(t, tr, 8)
    first_gate_col = z.shape[1] // d - 2

    def body(ya_ref, yb_ref, ga_ref, gb_ref, ba_ref, bb_ref, o_ref):
        g_a = _sigmoid(ga_ref[...].astype(F32) + ba_ref[...])
        g_b = _sigmoid(gb_ref[...].astype(F32) + bb_ref[...])
        o_ref[...] = (g_a * ya_ref[...] + g_b * yb_ref[...]).astype(BF16)

    return pl.pallas_call(
        body, name=name, grid=(t // tr,),
        in_specs=[_row_spec(tr, d), _row_spec(tr, d), _row_spec(tr, d, first_gate_col),
                  _row_spec(tr, d, first_gate_col + 1), _vec_spec(d, 0), _vec_spec(d, 1)],
        out_specs=_row_spec(tr, d), out_shape=jax.ShapeDtypeStruct((t, d), BF16),
        compiler_params=_params("parallel"),
    )(ya, yb, z, z, gate_b, gate_b)


def _mix_bwd(dmix, ya, yb, z, gate_b, dims, *, name, tr=512):
    t, d = ya.shape
    tr = _pick(t, tr, 8)
    first_gate_col = z.shape[1] // d - 2

    def body(dm_ref, ya_ref, yb_ref, ga_ref, gb_ref, ba_ref, bb_ref, dya_ref, dyb_ref, dz_ref, db_ref):
        dm = dm_ref[...].astype(F32)
        g_a = _sigmoid(ga_ref[...].astype(F32) + ba_ref[...])
        g_b = _sigmoid(gb_ref[...].astype(F32) + bb_ref[...])
        dya_ref[...] = (dm * g_a).astype(BF16)
        dyb_ref[...] = (dm * g_b).astype(BF16)
        dl_a = dm * ya_ref[...] * g_a * (1.0 - g_a)
        dl_b = dm * yb_ref[...] * g_b * (1.0 - g_b)
        dz_ref[:, 0:d] = dl_a.astype(BF16)
        dz_ref[:, d:2 * d] = dl_b.astype(BF16)
        first = pl.program_id(0) == 0
        sums = jnp.concatenate([jnp.sum(dl_a, axis=0, keepdims=True), jnp.sum(dl_b, axis=0, keepdims=True)], axis=1)
        _accumulate(db_ref, sums, first)

    return pl.pallas_call(
        body, name=name, grid=(t // tr,),
        in_specs=[_row_spec(tr, d), _row_spec(tr, d), _row_spec(tr, d), _row_spec(tr, d, first_gate_col),
                  _row_spec(tr, d, first_gate_col + 1), _vec_spec(d, 0), _vec_spec(d, 1)],
        out_specs=[_row_spec(tr, d), _row_spec(tr, d), _row_spec(tr, 2 * d), _vec_spec(2 * d)],
        out_shape=[jax.ShapeDtypeStruct((t, d), BF16)] * 2 + [jax.ShapeDtypeStruct((t, 2 * d), BF16),
                                                              jax.ShapeDtypeStruct((1, 2 * d), F32)],
        compiler_params=_params("arbitrary"),
    )(dmix, ya, yb, z, z, gate_b, gate_b)


def _adamw(w, grads, m, v, *, name, tr=256):
    r, c = w.shape
    tr = _pick(r, tr, 8)
    ng = len(grads)
    c1 = 1.0 - ADAM_B1 ** ADAM_STEP
    c2 = 1.0 - ADAM_B2 ** ADAM_STEP

    def body(*refs):
        w_ref, g_refs, m_ref, v_ref = refs[0], refs[1:1 + ng], refs[1 + ng], refs[2 + ng]
        g_out, d_out, m_out, v_out = refs[3 + ng:]
        g = g_refs[0][...]
        for extra in g_refs[1:]:
            g = g + extra[...]
        m_new = ADAM_B1 * m_ref[...] + (1.0 - ADAM_B1) * g
        v_new = ADAM_B2 * v_ref[...] + (1.0 - ADAM_B2) * (g * g)
        g_out[...] = g
        m_out[...] = m_new
        v_out[...] = v_new
        d_out[...] = -ADAM_LR * ((m_new / c1) / (jnp.sqrt(v_new / c2) + ADAM_EPS) + ADAM_WD * w_ref[...])

    spec = pl.BlockSpec((tr, c), lambda i: (i, 0))
    return pl.pallas_call(
        body, name=name, grid=(r // tr,),
        in_specs=[spec] * (3 + ng), out_specs=[spec] * 4, out_shape=[jax.ShapeDtypeStruct((r, c), F32)] * 4,
        compiler_params=_params("parallel"),
    )(w, *grads, m, v)


CHIP_PEERS = ((1, 0), (0, 1), (1, 1))


def _place():
    return lax.axis_index("x"), lax.axis_index("y"), lax.axis_index("c")


HBM = pl.BlockSpec(memory_space=pltpu.HBM)
SEM = pl.BlockSpec(memory_space=pltpu.SEMAPHORE)
IN_FLIGHT = pltpu.SideEffectType.DATAFLOW_SIDE_EFFECTING


def _in_hbm(a):
    return pltpu.with_memory_space_constraint(a, pltpu.HBM)


def _cast_to_lands(shards, dtypes, *, name, after=None):
    n = len(shards)

    def body(*refs):
        ins, outs, bufs, sems = refs[:n], refs[n:2 * n], refs[2 * n:3 * n], refs[3 * n]
        x, y, _ = _place()
        copies = []
        for a in range(n):
            bufs[a][...] = ins[a][...].astype(dtypes[a])
            cp = pltpu.make_async_copy(bufs[a], outs[a].at[2 * x + y], sems.at[a])
            cp.start()
            copies.append(cp)
        for cp in copies:
            cp.wait()

    body, more_specs, more_args = _ordered(body, n, after)
    return pl.pallas_call(
        body, name=name, in_specs=[pl.BlockSpec(memory_space=pltpu.VMEM)] * n + more_specs, out_specs=[ANY] * n,
        out_shape=[jax.ShapeDtypeStruct((N_CHIPS,) + s.shape, dt) for s, dt in zip(shards, dtypes)],
        scratch_shapes=[pltpu.VMEM(s.shape, dt) for s, dt in zip(shards, dtypes)] + [pltpu.SemaphoreType.DMA((n,))],
        compiler_params=pltpu.CompilerParams(vmem_limit_bytes=V7X_VMEM_LIMIT_BYTES),
    )(*shards, *more_args)


def _chip_copy(src, dst, send, recv, flip, place):
    x, y, c = place
    return pltpu.make_async_remote_copy(src_ref=src, dst_ref=dst, send_sem=send, recv_sem=recv,
                                        device_id=(x ^ flip[0], y ^ flip[1], c), device_id_type=MESH)


def _my_part(land, place, halved):
    block = land.at[2 * place[0] + place[1]]
    if not halved:
        return block
    rows = land.shape[1] // 2
    return block.at[pl.ds(pl.multiple_of(place[2] * rows, rows), rows)]


def _gather_start(lands, after, *, name, halved=()):
    n = len(lands)

    def body(*refs):
        ins, send, recv, token = refs[:n], refs[n + 1], refs[n + 2], refs[-1]
        place = _place()
        for a in range(n):
            part = _my_part(ins[a], place, a in halved)
            for p, flip in enumerate(CHIP_PEERS):
                k = 3 * a + p
                _chip_copy(part, part, send.at[k], recv.at[k], flip, place).start()
        token[...] = jnp.zeros_like(token)

    outs = pl.pallas_call(
        body, name=name, in_specs=[HBM] * n + [ANY],
        out_specs=(SEM, SEM, *[HBM] * n, pl.BlockSpec(memory_space=pltpu.VMEM)),
        out_shape=(pltpu.SemaphoreType.DMA((3 * n,)), pltpu.SemaphoreType.DMA((3 * n,)),
                   *[pltpu.HBM(l.shape, l.dtype) for l in lands], jax.ShapeDtypeStruct((8, 128), F32)),
        input_output_aliases={a: 2 + a for a in range(n)},
        compiler_params=pltpu.CompilerParams(has_side_effects=IN_FLIGHT),
    )(*[_in_hbm(l) for l in lands], after)
    return outs[0], outs[1], list(outs[2:2 + n]), outs[-1]


def _gather_wait(send, recv, lands, after, *, name, halved=()):
    n = len(lands)

    def body(*refs):
        ins, send_ref, recv_ref = refs[:n], refs[n], refs[n + 1]
        place = _place()
        for a in range(n):
            part = _my_part(ins[a], place, a in halved)
            for p, flip in enumerate(CHIP_PEERS):
                k = 3 * a + p
                cp = _chip_copy(part, part, send_ref.at[k], recv_ref.at[k], flip, place)
                cp.wait_send()
                cp.wait_recv()

    after = list(after) if isinstance(after, (list, tuple)) else [after]
    return pl.pallas_call(
        body, name=name, in_specs=[HBM] * n + [SEM, SEM] + [ANY] * len(after), out_specs=[HBM] * n,
        out_shape=[pltpu.HBM(l.shape, l.dtype) for l in lands],
        input_output_aliases={a: a for a in range(n)},
        compiler_params=pltpu.CompilerParams(has_side_effects=IN_FLIGHT),
    )(*lands, send, recv, *after)


def _forward_to_sibling(land, *, name):
    rows = land.shape[1] // 2

    def body(land_ref, out_ref, send, recv):
        x, y, c = _place()
        copies = []
        for p, (fx, fy) in enumerate(CHIP_PEERS):
            chip = 2 * (x ^ fx) + (y ^ fy)
            mine = pl.ds(pl.multiple_of(c * rows, rows), rows)
            theirs = pl.ds(pl.multiple_of((1 - c) * rows, rows), rows)
            out = pltpu.make_async_remote_copy(
                src_ref=land_ref.at[chip].at[mine], dst_ref=out_ref.at[chip].at[mine], send_sem=send.at[p],
                recv_sem=recv.at[p], device_id=(x, y, 1 - c), device_id_type=MESH)
            out.start()
            copies.append((out, pltpu.make_async_remote_copy(
                src_ref=land_ref.at[chip].at[theirs], dst_ref=out_ref.at[chip].at[theirs], send_sem=send.at[p],
                recv_sem=recv.at[p], device_id=(x, y, 1 - c), device_id_type=MESH)))
        for out, arriving in copies:
            out.wait_send()
            arriving.wait_recv()

    return pl.pallas_call(
        body, name=name, in_specs=[ANY], out_specs=ANY, out_shape=jax.ShapeDtypeStruct(land.shape, land.dtype),
        input_output_aliases={0: 0},
        scratch_shapes=[pltpu.SemaphoreType.DMA((3,)), pltpu.SemaphoreType.DMA((3,))],
    )(land)


def _scatter_start(grad, *, name):
    def body(g_ref, land_ref, send, recv, g_thru, land_thru, token):
        place = _place()
        for p, flip in enumerate(CHIP_PEERS):
            peer_chip = 2 * (place[0] ^ flip[0]) + (place[1] ^ flip[1])
            _chip_copy(g_ref.at[peer_chip], land_ref.at[p], send.at[p], recv.at[p], flip, place).start()
        token[...] = jnp.zeros_like(token)

    land = lax.empty((3,) + grad.shape[1:], grad.dtype)
    return pl.pallas_call(
        body, name=name, in_specs=[HBM, HBM],
        out_specs=(SEM, SEM, HBM, HBM, pl.BlockSpec(memory_space=pltpu.VMEM)),
        out_shape=(pltpu.SemaphoreType.DMA((3,)), pltpu.SemaphoreType.DMA((3,)), pltpu.HBM(grad.shape, grad.dtype),
                   pltpu.HBM(land.shape, land.dtype), jax.ShapeDtypeStruct((8, 128), F32)),
        input_output_aliases={0: 2, 1: 3},
        compiler_params=pltpu.CompilerParams(has_side_effects=IN_FLIGHT),
    )(_in_hbm(grad), _in_hbm(land))


def _scatter_wait(started, after, *, name):
    n = len(started)

    def body(*refs):
        grads, lands = refs[:n], refs[n:2 * n]
        sends, recvs = refs[2 * n:3 * n], refs[3 * n:4 * n]
        place = _place()
        for a in range(n):
            for p, flip in enumerate(CHIP_PEERS):
                cp = _chip_copy(grads[a].at[0], lands[a].at[p], sends[a].at[p], recvs[a].at[p], flip, place)
                cp.wait_send()
                cp.wait_recv()

    grads, lands = [s[2] for s in started], [s[3] for s in started]
    after = list(after) if isinstance(after, (list, tuple)) else [after]
    outs = pl.pallas_call(
        body, name=name, in_specs=[HBM] * (2 * n) + [SEM] * (2 * n) + [ANY] * len(after), out_specs=[HBM] * (2 * n),
        out_shape=[pltpu.HBM(a.shape, a.dtype) for a in grads + lands],
        input_output_aliases={a: a for a in range(2 * n)},
        compiler_params=pltpu.CompilerParams(has_side_effects=IN_FLIGHT),
    )(*grads, *lands, *[s[0] for s in started], *[s[1] for s in started], *after)
    return list(zip(outs[:n], outs[n:]))


def _sibling_copy(src, dst, send, recv, place):
    x, y, c = place
    return pltpu.make_async_remote_copy(src_ref=src, dst_ref=dst, send_sem=send, recv_sem=recv,
                                        device_id=(x, y, 1 - c), device_id_type=MESH)


def _swap_start(arrays, *, name):
    n = len(arrays)

    def body(*refs):
        ins, lands, send, recv, token = refs[:n], refs[n:2 * n], refs[2 * n], refs[2 * n + 1], refs[-1]
        place = _place()
        for a in range(n):
            _sibling_copy(ins[a], lands[a], send.at[a], recv.at[a], place).start()
        token[...] = jnp.zeros_like(token)

    both = [_in_hbm(a) for a in arrays] + [_in_hbm(lax.empty(a.shape, a.dtype)) for a in arrays]
    outs = pl.pallas_call(
        body, name=name, in_specs=[HBM] * (2 * n),
        out_specs=(SEM, SEM, *[HBM] * (2 * n), pl.BlockSpec(memory_space=pltpu.VMEM)),
        out_shape=(pltpu.SemaphoreType.DMA((n,)), pltpu.SemaphoreType.DMA((n,)),
                   *[pltpu.HBM(a.shape, a.dtype) for a in both], jax.ShapeDtypeStruct((8, 128), F32)),
        input_output_aliases={a: 2 + a for a in range(2 * n)},
        compiler_params=pltpu.CompilerParams(has_side_effects=IN_FLIGHT),
    )(*both)
    return outs[0], outs[1], list(outs[2:2 + n]), list(outs[2 + n:2 + 2 * n]), outs[-1]


def _swap_wait(started, after, *, name):
    send, recv, arrays, lands = started[:4]
    n = len(arrays)

    def body(*refs):
        ins, zones, send_ref, recv_ref = refs[:n], refs[n:2 * n], refs[2 * n], refs[2 * n + 1]
        place = _place()
        for a in range(n):
            cp = _sibling_copy(ins[a], zones[a], send_ref.at[a], recv_ref.at[a], place)
            cp.wait_send()
            cp.wait_recv()

    after = list(after) if isinstance(after, (list, tuple)) else [after]
    outs = pl.pallas_call(
        body, name=name, in_specs=[HBM] * (2 * n) + [SEM, SEM] + [ANY] * len(after), out_specs=[HBM] * (2 * n),
        out_shape=[pltpu.HBM(a.shape, a.dtype) for a in arrays + lands],
        input_output_aliases={a: a for a in range(2 * n)},
        compiler_params=pltpu.CompilerParams(has_side_effects=IN_FLIGHT),
    )(*arrays, *lands, send, recv, *after)
    return list(outs[:n]), list(outs[n:])


def _allreduce_start(packed, *, name):
    n_dev = 8

    def body(src_ref, land_ref, send, recv, src_thru, land_thru, token):
        x, y, c = _place()
        me = 4 * x + 2 * y + c
        for p in range(1, n_dev):
            pltpu.make_async_remote_copy(
                src_ref=src_ref, dst_ref=land_ref.at[me], send_sem=send.at[p - 1], recv_sem=recv.at[p - 1],
                device_id=(x ^ (p >> 2), y ^ ((p >> 1) & 1), c ^ (p & 1)), device_id_type=MESH).start()
        token[...] = jnp.zeros_like(token)

    land = lax.empty((n_dev,) + packed.shape, packed.dtype)
    return pl.pallas_call(
        body, name=name, in_specs=[HBM, HBM],
        out_specs=(SEM, SEM, HBM, HBM, pl.BlockSpec(memory_space=pltpu.VMEM)),
        out_shape=(pltpu.SemaphoreType.DMA((n_dev - 1,)), pltpu.SemaphoreType.DMA((n_dev - 1,)),
                   pltpu.HBM(packed.shape, packed.dtype), pltpu.HBM(land.shape, land.dtype),
                   jax.ShapeDtypeStruct((8, 128), F32)),
        input_output_aliases={0: 2, 1: 3},
        compiler_params=pltpu.CompilerParams(has_side_effects=IN_FLIGHT),
    )(_in_hbm(packed), _in_hbm(land))


def _allreduce_wait(started, after, *, name):
    send, recv, packed, land = started[:4]
    n_dev = 8

    def body(src_ref, land_ref, send_ref, recv_ref, *_):
        x, y, c = _place()
        for p in range(1, n_dev):
            cp = pltpu.make_async_remote_copy(
                src_ref=src_ref, dst_ref=land_ref.at[0], send_sem=send_ref.at[p - 1], recv_sem=recv_ref.at[p - 1],
                device_id=(x ^ (p >> 2), y ^ ((p >> 1) & 1), c ^ (p & 1)), device_id_type=MESH)
            cp.wait_send()
            cp.wait_recv()

    after = list(after) if isinstance(after, (list, tuple)) else [after]
    return pl.pallas_call(
        body, name=name, in_specs=[HBM, HBM, SEM, SEM] + [ANY] * len(after), out_specs=[HBM, HBM],
        out_shape=[pltpu.HBM(packed.shape, packed.dtype), pltpu.HBM(land.shape, land.dtype)],
        input_output_aliases={0: 0, 1: 1},
        compiler_params=pltpu.CompilerParams(has_side_effects=IN_FLIGHT),
    )(packed, land, send, recv, *after)


def _sum_devices(mine, land, *, name):
    n_dev = land.shape[0]

    def body(mine_ref, land_ref, out_ref):
        x, y, c = _place()
        me = 4 * x + 2 * y + c
        total = None
        for s in range(n_dev):
            part = jnp.where(me == s, mine_ref[...], land_ref[s])
            total = part if total is None else total + part
        out_ref[...] = total

    return pl.pallas_call(body, name=name, out_shape=jax.ShapeDtypeStruct(mine.shape, mine.dtype))(mine, land)


def _sum_received(grad, land, *, name, tr=256):
    _, r, c = grad.shape
    tr = _pick(r, tr, 8)

    def body(chip_ref, g_ref, l_ref, o_ref):
        o_ref[...] = ((g_ref[...] + l_ref[0].astype(F32)) + l_ref[1].astype(F32)) + l_ref[2].astype(F32)

    chip = (2 * lax.axis_index("x") + lax.axis_index("y")).astype(jnp.int32).reshape(1)
    return pl.pallas_call(
        body, name=name,
        grid_spec=pltpu.PrefetchScalarGridSpec(
            num_scalar_prefetch=1, grid=(r // tr,),
            in_specs=[pl.BlockSpec((None, tr, c), lambda i, chip_ref: (chip_ref[0], i, 0)),
                      pl.BlockSpec((3, tr, c), lambda i, chip_ref: (0, i, 0))],
            out_specs=pl.BlockSpec((tr, c), lambda i, chip_ref: (i, 0))),
        out_shape=jax.ShapeDtypeStruct((r, c), F32), compiler_params=_params("parallel"),
    )(chip, grad, land)


def _packed_rows(size, d):
    return -(-size // (8 * d)) * 8


def _pack_rows(arrays, d):
    rows = []
    for arr in arrays:
        flat = arr.reshape(-1).astype(F32)
        n = _packed_rows(flat.shape[0], d)
        rows.append(jnp.pad(flat, (0, n * d - flat.shape[0])).reshape(n, d))
    return jnp.concatenate(rows, axis=0)


def _unpack_rows(packed, shapes, d):
    out, row = [], 0
    for shape in shapes:
        size = math.prod(shape)
        n = _packed_rows(size, d)
        out.append(packed[row:row + n].reshape(-1)[:size].reshape(shape))
        row += n
    return out


SMALL = ("norm1_g", "gate_b", "conv_b", "conv_norm_g", "q_norm_g", "k_norm_g", "norm2_g", "ffn_conv_b")
LARGE = ("w_in", "w_conv_out", "w_attn_out", "w_out", "w_up", "w_down")
WEIGHTS = ("norm1_g", "w_in", "gate_b", "conv_w", "conv_b", "conv_norm_g", "w_conv_out", "q_norm_g", "k_norm_g",
           "w_attn_out", "w_out", "norm2_g", "w_up", "ffn_conv_w", "ffn_conv_b", "w_down")


def _after(vec, token):
    return vec if token is None else vec + token[0:1, 0:1]


def _local_step(dims, x, target, small, first_weights, other_weights, send_grad):
    d, f, heads = dims.d_model, dims.d_ff, dims.n_heads
    small = dict(small)
    row = lambda name: small[name].reshape(1, -1)
    head_sum = _head_sum_matrix(dims)
    head_spread = jnp.transpose(head_sum)
    ones = (head_sum, head_spread)
    gq = jnp.tile(row("q_norm_g"), (1, heads))
    gk = jnp.tile(row("k_norm_g"), (1, heads))
    one_shard = lambda w: w.reshape(1, -1, w.shape[-1])

    h = _rmsnorm_fwd(x, row("norm1_g"), name="norm1")
    full = first_weights(h)
    w_in = full["w_in"]
    conv_w = jnp.pad(full["conv_w"], ((0, CONV_HALO - dims.conv_width), (0, 0)))
    ffn_w = jnp.pad(full["ffn_conv_w"], ((0, FFN_HALO - dims.ffn_conv_width), (0, 0)))
    z = _mm_nn(h, w_in, out_dtype=BF16, after=full.get("token"), tm=2048, tn=1792, name="in_proj")
    a1, a3 = _conv_branch_fwd(z, conv_w, row("conv_b"), row("conv_norm_g"), dims, name="conv_branch")
    qkv = _qkv_layouts_fwd(z, gq, gk, ones, dims, name="qk_norm")
    per_group = {dil: _attn_fwd(*qkv[dil], dims, dil, name=f"attn_fwd_d{dil}") for dil in DILATIONS}
    o, lse = _attn_combine(per_group, head_spread, dims, name="attn_combine")
    full = other_weights(o)
    w_up = full["w_up"]
    w_co, w_ao, w_o, w_dn = (one_shard(full[k]) for k in ("w_conv_out", "w_attn_out", "w_out", "w_down"))
    ya = _mm_nn(a3, w_co, out_dtype=F32, name="conv_out_proj")
    yb = _mm_nn(o, w_ao, out_dtype=F32, name="attn_out_proj")
    mixed = _mix_fwd(ya, yb, z, row("gate_b"), dims, name="gate_mix")
    x1, h2 = _proj_residual_norm(mixed, w_o, x, row("norm2_g"), name="out_proj_norm2")
    up = _mm_nn(h2, w_up, out_dtype=F32, tm=2048, name="up_proj")
    act = _ffn_act_fwd(up, ffn_w, row("ffn_conv_b"), dims, name="ffn_act")
    dy, dy_b, loss = _proj_residual_loss(act, w_dn, x1, target, tm=512, name="down_proj_loss")

    grads = {}

    def large(name, g):
        grads[name], g_bf16 = g
        return send_grad(name, g_bf16)

    sent = large("w_down", _mm_tn(act, dy_b, n_shards=1, name="dw_down"))
    dact = _mm_nt(dy_b, w_dn, out_dtype=BF16, after=sent, name="d_act")
    dup, dfw, dfb = _ffn_bwd(dact, up, ffn_w, row("ffn_conv_b"), dims, name="ffn_bwd")
    grads["ffn_conv_w"], grads["ffn_conv_b"] = dfw[:dims.ffn_conv_width], dfb
    sent = large("w_up", _mm_tn(h2, dup, n_shards=N_CHIPS, name="dw_up"))
    dh2 = _mm_nt(dup, w_up, out_dtype=F32, after=sent, name="d_h2")
    dx1, dx1_b, grads["norm2_g"] = _rmsnorm_bwd(x1, row("norm2_g"), dh2, dy, want_bf16=True, name="norm2_bwd")
    sent = large("w_out", _mm_tn(mixed, dx1_b, n_shards=1, name="dw_out"))
    dmix = _mm_nt(dx1_b, w_o, out_dtype=F32, after=sent, name="d_mix")
    dya, dyb, dz_gate, grads["gate_b"] = _mix_bwd(dmix, ya, yb, z, row("gate_b"), dims, name="gate_mix_bwd")
    sent = large("w_attn_out", _mm_tn(o, dyb, n_shards=1, name="dw_attn_out"))
    do = _mm_nt(dyb, w_ao, out_dtype=BF16, after=sent, name="d_attn")
    dos, deltas = _attn_bwd_prep(do, o, head_sum, dims, name="attn_bwd_prep")
    dqkv = {dil: _attn_bwd(*qkv[dil], dos[dil], lse[dil], deltas[dil], dims, dil, name=f"attn_bwd_d{dil}")
            for dil in DILATIONS}
    dz_qkv, dgq, dgk = _qkv_layouts_bwd(z, dqkv, gq, gk, ones, dims, name="qk_norm_bwd")
    grads["q_norm_g"] = dgq.reshape(heads, dims.head_dim).sum(axis=0)
    grads["k_norm_g"] = dgk.reshape(heads, dims.head_dim).sum(axis=0)
    sent = large("w_conv_out", _mm_tn(a3, dya, n_shards=1, name="dw_conv_out"))
    da3 = _mm_nt(dya, w_co, out_dtype=F32, after=sent, name="d_conv_act")
    da1, grads["conv_norm_g"] = _conv_norm_bwd(da3, a1, row("conv_norm_g"), name="conv_norm_bwd")
    dz, dcw, grads["conv_b"] = _conv_branch_bwd(da1, z, conv_w, [dz_qkv, dz_gate], dims, name="conv_branch_bwd")
    grads["conv_w"] = dcw[:dims.conv_width]
    sent = large("w_in", _mm_tn(h, dz, n_shards=N_CHIPS, name="dw_in"))
    dh = _mm_nt(dz, w_in, out_dtype=F32, after=sent, name="d_h")
    dx, grads["norm1_g"] = _rmsnorm_bwd(x, row("norm1_g"), dh, dx1, want_bf16=False, name="norm1_bwd")
    return loss, dx, grads


def _step(dims, x, target, w, m, v):
    d = dims.d_model
    t = dims.tokens
    sq = lambda a: a.reshape(a.shape[1:])
    w2, m2, v2 = ({k: sq(a) for k, a in grp.items()} for grp in (w, m, v))

    conv_pad = jnp.pad(w2["conv_w"], ((0, CONV_HALO - dims.conv_width), (0, 0)))
    ffn_pad = jnp.pad(w2["ffn_conv_w"], ((0, FFN_HALO - dims.ffn_conv_width), (0, 0)))
    first_names = ("w_in", "conv_w", "ffn_conv_w")
    other_names = tuple(k for k in LARGE if k not in first_names)
    lands = dict(zip(first_names, _cast_to_lands([w2["w_in"], conv_pad, ffn_pad], [BF16, F32, F32], name="cast_first")))
    first = _gather_start([lands[k] for k in first_names], x, halved=(0,), name="gather_start_first")
    lands.update(zip(other_names, _cast_to_lands([w2[k] for k in other_names], [BF16] * len(other_names),
                                                 after=first[3], name="cast_other")))
    other = []
    cols = lambda g, rows: jnp.moveaxis(g, 0, 1).reshape(g.shape[1], -1)[:rows]

    def first_weights(after):
        got = dict(zip(first_names, _gather_wait(*first[:3], [after] + [lands[k] for k in other_names], halved=(0,),
                                                 name="gather_wait_first")))
        got["w_in"] = _forward_to_sibling(got["w_in"], name="forward_w_in")
        other.extend(_gather_start([lands[k] for k in other_names], got["w_in"], name="gather_start_other"))
        got["conv_w"] = cols(got["conv_w"], dims.conv_width)
        got["ffn_conv_w"] = cols(got["ffn_conv_w"], dims.ffn_conv_width)
        got["token"] = other[3]
        return got

    def other_weights(after):
        return dict(zip(other_names, _gather_wait(*other[:3], after, name="gather_wait_other")))

    started = {}

    def send_grad(name, g):
        send, recv, g_thru, land, token = _scatter_start(g.reshape(N_CHIPS, -1, g.shape[-1]), name=f"scatter_start_{name}")
        started[name] = (send, recv, g_thru, land)
        return token

    small = {k: w2[k] for k in SMALL}
    small["norm1_g"] = _after(small["norm1_g"].reshape(1, -1), first[3])
    loss, dx, grads = _local_step(dims, x.reshape(t, d), target.reshape(t, d), small, first_weights, other_weights, send_grad)

    def my_sums(names, after, tag):
        arrived = _scatter_wait([started[k] for k in names], after, name=f"scatter_wait_{tag}")
        blocks = [grads[k].reshape(N_CHIPS, -1, grads[k].shape[-1]) for k in names]
        return [_sum_received(g, land, name=f"sum_{k}") for k, g, (_, land) in zip(names, blocks, arrived)]

    def updates(names, mine, theirs):
        return {k: _adamw(w2[k], [a, b], m2[k], v2[k], name=f"adamw_{k}") for k, a, b in zip(names, mine, theirs)}

    small_names = SMALL + ("conv_w", "ffn_conv_w")
    packed = _pack_rows([grads[k] for k in small_names] + [loss[0, 0]], d)
    reducing = _allreduce_start(packed, name="allreduce_start")
    others = [k for k in LARGE if k != "w_in"]
    mine_others = my_sums(others, [dx, reducing[4]], "others")
    swapping_others = _swap_start(mine_others, name="swap_start_others")
    mine_w_in = my_sums(["w_in"], swapping_others[4], "w_in")
    swapping_w_in = _swap_start(mine_w_in, name="swap_start_w_in")
    out = updates(others, *_swap_wait(swapping_others, swapping_w_in[4], name="swap_wait_others"))
    last_updates = [out[k][1] for k in others]
    reduced = _sum_devices(*_allreduce_wait(reducing, last_updates, name="allreduce_wait"), name="allreduce_sum")
    shapes = [grads[k].shape for k in small_names] + [()]
    *small_g, loss_total = _unpack_rows(reduced, shapes, d)
    small_g = dict(zip(small_names, small_g))
    chip = 2 * lax.axis_index("x") + lax.axis_index("y")
    for k in ("conv_w", "ffn_conv_w"):
        width = w2[k].shape[1]
        small_g[k] = lax.dynamic_slice_in_dim(small_g[k], chip * width, width, axis=1)

    small_shapes = [w2[k].shape for k in small_names]
    pack = lambda grp: _pack_rows([grp[k] for k in small_names], d)
    results = _adamw(pack(w2), [pack(small_g)], pack(m2), pack(v2), name="adamw_small")
    unpacked = [_unpack_rows(r, small_shapes, d) for r in results]
    out.update({k: tuple(u[i] for u in unpacked) for i, k in enumerate(small_names)})
    out.update(updates(["w_in"], *_swap_wait(swapping_w_in, results[1], name="swap_wait_w_in")))

    lead = lambda a: a.reshape((1,) + a.shape)
    ordered = [[lead(out[k][j].reshape(w2[k].shape)) for k in WEIGHTS] for j in range(4)]
    return (loss_total, dx.reshape(x.shape), *ordered[0], *ordered[1], *ordered[2], *ordered[3])


def kernel(x, norm1_g, w_in, gate_b, conv_w, conv_b, conv_norm_g, w_conv_out, q_norm_g, k_norm_g, w_attn_out, w_out, norm2_g, w_up, ffn_conv_w, ffn_conv_b, w_down, loss_target, m_norm1_g, m_w_in, m_gate_b, m_conv_w, m_conv_b, m_conv_norm_g, m_w_conv_out, m_q_norm_g, m_k_norm_g, m_w_attn_out, m_w_out, m_norm2_g, m_w_up, m_ffn_conv_w, m_ffn_conv_b, m_w_down, v_norm1_g, v_w_in, v_gate_b, v_conv_w, v_conv_b, v_conv_norm_g, v_w_conv_out, v_q_norm_g, v_k_norm_g, v_w_attn_out, v_w_out, v_norm2_g, v_w_up, v_ffn_conv_w, v_ffn_conv_b, v_w_down):
    w = dict(zip(WEIGHTS, (norm1_g, w_in, gate_b, conv_w, conv_b, conv_norm_g, w_conv_out, q_norm_g, k_norm_g,
                           w_attn_out, w_out, norm2_g, w_up, ffn_conv_w, ffn_conv_b, w_down)))
    m = dict(zip(WEIGHTS, (m_norm1_g, m_w_in, m_gate_b, m_conv_w, m_conv_b, m_conv_norm_g, m_w_conv_out, m_q_norm_g,
                           m_k_norm_g, m_w_attn_out, m_w_out, m_norm2_g, m_w_up, m_ffn_conv_w, m_ffn_conv_b, m_w_down)))
    v = dict(zip(WEIGHTS, (v_norm1_g, v_w_in, v_gate_b, v_conv_w, v_conv_b, v_conv_norm_g, v_w_conv_out, v_q_norm_g,
                           v_k_norm_g, v_w_attn_out, v_w_out, v_norm2_g, v_w_up, v_ffn_conv_w, v_ffn_conv_b, v_w_down)))
    dims = Dims(d_model=x.shape[-1], batch_local=x.shape[0], seq=x.shape[1], d_ff=w_down.shape[1] * N_CHIPS)
    return _step(dims, x, loss_target, w, m, v)
```

```python
import functools
import math
from typing import NamedTuple

import jax
import jax.numpy as jnp
from jax import lax
from jax.experimental import pallas as pl
from jax.experimental.pallas import tpu as pltpu

F32 = jnp.float32
BF16 = jnp.bfloat16

RMS_EPS = 1e-6
ATTN_BLOCK = 128
DILATIONS = (1, 4, 16)
CONV_HALO = 32
FFN_HALO = 8
ADAM_LR, ADAM_B1, ADAM_B2, ADAM_EPS, ADAM_WD, ADAM_STEP = 0.001, 0.9, 0.999, 1e-08, 0.01, 10
V7X_VMEM_LIMIT_BYTES = 56 * 2 ** 20
N_CHIPS = 4
MESH = pl.DeviceIdType.MESH


class Dims(NamedTuple):
    d_model: int = 1024
    n_heads: int = 16
    head_dim: int = 64
    d_ff: int = 2816
    seq: int = 2048
    batch_local: int = 2
    conv_width: int = 31
    ffn_conv_width: int = 3

    @property
    def tokens(self):
        return self.seq * self.batch_local


def _params(*semantics):
    return pltpu.CompilerParams(dimension_semantics=semantics, vmem_limit_bytes=V7X_VMEM_LIMIT_BYTES)


ANY = pl.BlockSpec(memory_space=pl.ANY)


def _ordered(body, n_inputs, after):
    after = [] if after is None else list(after) if isinstance(after, (list, tuple)) else [after]
    if not after:
        return body, [], []

    def wrapped(*refs):
        return body(*refs[:n_inputs], *refs[n_inputs + len(after):])

    return wrapped, [ANY] * len(after), after


def _pick(n, target, mult=128):
    if n <= target:
        return n
    best = None
    for t in range(mult, target + 1, mult):
        if n % t == 0:
            best = t
    assert best is not None, (n, target, mult)
    return best


def _sigmoid(v):
    return 1.0 / (1.0 + jnp.exp(-v))


def _mm_nn(a, w, *, out_dtype, name, residual=None, after=None, tm=1024, tn=1408, tk=2816):
    m, k = a.shape
    nsh, k2, c = w.shape
    assert k == k2 and a.dtype == BF16 and w.dtype == BF16
    n = nsh * c
    tm, tn, tk = _pick(m, tm, 8), _pick(c, tn), _pick(k, tk)
    nk, cpn = k // tk, c // tn

    def body(*refs):
        if residual is None:
            a_ref, w_ref, o_ref, acc = refs
        else:
            a_ref, w_ref, r_ref, o_ref, acc = refs
        prod = jnp.dot(a_ref[...], w_ref[...], preferred_element_type=F32)

        def finish(total):
            if residual is not None:
                total = total + r_ref[...]
            o_ref[...] = total.astype(out_dtype)

        if nk == 1:
            finish(prod)
        else:
            kk = pl.program_id(2)

            @pl.when(kk == 0)
            def _():
                acc[...] = prod

            @pl.when(kk > 0)
            def _():
                acc[...] += prod

            @pl.when(kk == nk - 1)
            def _():
                finish(acc[...])

    in_specs = [pl.BlockSpec((tm, tk), lambda i, j, kk: (i, kk)),
                pl.BlockSpec((None, tk, tn), lambda i, j, kk: (j // cpn, kk, j % cpn))]
    args = [a, w]
    if residual is not None:
        in_specs.append(pl.BlockSpec((tm, tn), lambda i, j, kk: (i, j)))
        args.append(residual)
    body, more_specs, more_args = _ordered(body, len(args), after)
    return pl.pallas_call(
        body, name=name, grid=(m // tm, n // tn, nk),
        in_specs=in_specs + more_specs, out_specs=pl.BlockSpec((tm, tn), lambda i, j, kk: (i, j)),
        out_shape=jax.ShapeDtypeStruct((m, n), out_dtype),
        scratch_shapes=[pltpu.VMEM((tm, tn) if nk > 1 else (8, 128), F32)],
        compiler_params=_params("parallel", "parallel", "arbitrary"),
    )(*args, *more_args)


def _proj_residual_norm(a, w, residual, g, *, name, tm=1024):
    m, k = a.shape
    _, k2, n = w.shape
    assert w.shape[0] == 1 and k == k2 and a.dtype == BF16 and w.dtype == BF16
    tm = _pick(m, tm, 8)

    def body(a_ref, w_ref, r_ref, g_ref, y_ref, h_ref):
        y = r_ref[...] + jnp.dot(a_ref[...], w_ref[...], preferred_element_type=F32)
        y_ref[...] = y
        h_ref[...] = (y * lax.rsqrt(jnp.mean(y * y, axis=-1, keepdims=True) + RMS_EPS) * g_ref[...]).astype(BF16)

    rows = lambda width: pl.BlockSpec((tm, width), lambda i: (i, 0))
    return pl.pallas_call(
        body, name=name, grid=(m // tm,),
        in_specs=[rows(k), pl.BlockSpec((None, k, n), lambda i: (0, 0, 0)), rows(n), pl.BlockSpec((1, n), lambda i: (0, 0))],
        out_specs=[rows(n), rows(n)],
        out_shape=[jax.ShapeDtypeStruct((m, n), F32), jax.ShapeDtypeStruct((m, n), BF16)],
        compiler_params=_params("parallel"),
    )(a, w, residual, g)


def _proj_residual_loss(a, w, residual, target, *, name, tm=1024):
    m, k = a.shape
    _, k2, n = w.shape
    assert w.shape[0] == 1 and k == k2 and a.dtype == BF16 and w.dtype == BF16
    tm = _pick(m, tm, 8)

    def body(a_ref, w_ref, r_ref, t_ref, dy_ref, dyb_ref, loss_ref):
        err = r_ref[...] + jnp.dot(a_ref[...], w_ref[...], preferred_element_type=F32) - t_ref[...]
        dy = err * (1.0 / n)
        dy_ref[...] = dy
        dyb_ref[...] = dy.astype(BF16)
        part = jnp.sum(jnp.sum(err * err, axis=-1, keepdims=True), axis=0, keepdims=True) * (0.5 / n)
        _accumulate(loss_ref, jnp.broadcast_to(part, (8, 128)), pl.program_id(0) == 0)

    rows = lambda width: pl.BlockSpec((tm, width), lambda i: (i, 0))
    return pl.pallas_call(
        body, name=name, grid=(m // tm,),
        in_specs=[rows(k), pl.BlockSpec((None, k, n), lambda i: (0, 0, 0)), rows(n), rows(n)],
        out_specs=[rows(n), rows(n), pl.BlockSpec((8, 128), lambda i: (0, 0))],
        out_shape=[jax.ShapeDtypeStruct((m, n), F32), jax.ShapeDtypeStruct((m, n), BF16),
                   jax.ShapeDtypeStruct((8, 128), F32)],
        compiler_params=_params("arbitrary"),
    )(a, w, residual, target)


def _mm_nt(a, w, *, out_dtype, name, after=None, tm=1024, tn=1408, tk=1792):
    m, k = a.shape
    nsh, r, c = w.shape
    assert k == nsh * c and a.dtype == BF16 and w.dtype == BF16
    tm, tn, tk = _pick(m, tm, 8), _pick(r, tn), _pick(c, tk)
    nk, cpk = k // tk, c // tk

    def body(a_ref, w_ref, o_ref, acc):
        prod = lax.dot_general(a_ref[...], w_ref[...], (((1,), (1,)), ((), ())), preferred_element_type=F32)
        if nk == 1:
            o_ref[...] = prod.astype(out_dtype)
        else:
            kk = pl.program_id(2)

            @pl.when(kk == 0)
            def _():
                acc[...] = prod

            @pl.when(kk > 0)
            def _():
                acc[...] += prod

            @pl.when(kk == nk - 1)
            def _():
                o_ref[...] = acc[...].astype(out_dtype)

    body, more_specs, more_args = _ordered(body, 2, after)
    return pl.pallas_call(
        body, name=name, grid=(m // tm, r // tn, nk),
        in_specs=[pl.BlockSpec((tm, tk), lambda i, j, kk: (i, kk)),
                  pl.BlockSpec((None, tn, tk), lambda i, j, kk: (kk // cpk, j, kk % cpk))] + more_specs,
        out_specs=pl.BlockSpec((tm, tn), lambda i, j, kk: (i, j)),
        out_shape=jax.ShapeDtypeStruct((m, r), out_dtype),
        scratch_shapes=[pltpu.VMEM((tm, tn) if nk > 1 else (8, 128), F32)],
        compiler_params=_params("parallel", "parallel", "arbitrary"),
    )(a, w, *more_args)


NORM_BWD_ROWS = 256


def _mm_nt_rmsnorm_bwd(a, w, x, g, dres, *, name, want_bf16, after=None, tm=1024, tk=1792):
    m, k = a.shape
    nsh, r, c = w.shape
    assert k == nsh * c and a.dtype == BF16 and w.dtype == BF16 and x.shape == (m, r)
    tm, tk = _pick(m, tm, 8), _pick(c, tk)
    nk, cpk = k // tk, c // tk
    rows = _pick(tm, NORM_BWD_ROWS, 8)

    def body(a_ref, w_ref, x_ref, g_ref, dres_ref, *rest):
        outs, acc = rest[:-1], rest[-1]
        dx_ref, dg_ref = outs[0], outs[-1]
        kk = pl.program_id(1)
        prod = lax.dot_general(a_ref[...], w_ref[...], (((1,), (1,)), ((), ())), preferred_element_type=F32)

        @pl.when(kk == 0)
        def _():
            acc[...] = prod

        @pl.when(kk > 0)
        def _():
            acc[...] += prod

        @pl.when(kk == nk - 1)
        def _():
            dg = jnp.zeros((1, r), F32)
            for r0 in range(0, tm, rows):
                part = slice(r0, r0 + rows)
                xv, dyv = x_ref[part, :], acc[part, :]
                inv = lax.rsqrt(jnp.mean(xv * xv, axis=-1, keepdims=True) + RMS_EPS)
                gy = dyv * g_ref[...]
                dx = dres_ref[part, :] + inv * gy - xv * (inv * inv * inv) * jnp.mean(xv * gy, axis=-1, keepdims=True)
                dx_ref[part, :] = dx
                if want_bf16:
                    outs[1][part, :] = dx.astype(BF16)
                dg = dg + jnp.sum(dyv * xv * inv, axis=0, keepdims=True)
            _accumulate(dg_ref, dg, pl.program_id(0) == 0)

    whole = lambda: pl.BlockSpec((tm, r), lambda i, kk: (i, 0))
    vec = pl.BlockSpec((1, r), lambda i, kk: (0, 0))
    out_shape, out_specs = [jax.ShapeDtypeStruct((m, r), F32)], [whole()]
    if want_bf16:
        out_shape.append(jax.ShapeDtypeStruct((m, r), BF16))
        out_specs.append(whole())
    out_shape.append(jax.ShapeDtypeStruct((1, r), F32))
    out_specs.append(vec)
    body, more_specs, more_args = _ordered(body, 5, after)
    return pl.pallas_call(
        body, name=name, grid=(m // tm, nk),
        in_specs=[pl.BlockSpec((tm, tk), lambda i, kk: (i, kk)),
                  pl.BlockSpec((None, r, tk), lambda i, kk: (kk // cpk, 0, kk % cpk)), whole(), vec, whole()] + more_specs,
        out_specs=out_specs, out_shape=out_shape,
        scratch_shapes=[pltpu.VMEM((tm, r), F32)],
        compiler_params=_params("arbitrary", "arbitrary"),
    )(a, w, x, g, dres, *more_args)


MM_TN_VMEM_BYTES = 44 * 2 ** 20


def _mm_tn(a, b, *, n_shards, name, tm=1408, tn=1408):
    t, m = a.shape
    t2, n = b.shape
    assert t == t2 and a.dtype == BF16 and b.dtype == BF16
    c = n // n_shards
    tm, tn = _pick(m, tm), _pick(c, tn)
    if m // tm == 1 and n // tn == 1 and tn % (2 * LANES) == 0:
        tn //= 2
    fixed = 2 * tm * tn * 6
    if 4 * t * (tm + tn) + fixed <= MM_TN_VMEM_BYTES:
        tk = t
    else:
        tk = _pick(t, (MM_TN_VMEM_BYTES - fixed - 4 * tm * tn) // (4 * (tm + tn)), 8)
    nk, cpn = t // tk, c // tn

    def body(a_ref, b_ref, o_ref, ob_ref, acc):
        kk = pl.program_id(2)
        prod = lax.dot_general(a_ref[...], b_ref[...], (((0,), (0,)), ((), ())), preferred_element_type=F32)

        def finish(total):
            o_ref[...] = total
            ob_ref[...] = total.astype(BF16)

        if nk == 1:
            finish(prod)
        else:
            @pl.when(kk == 0)
            def _():
                acc[...] = prod

            @pl.when(kk > 0)
            def _():
                acc[...] += prod

            @pl.when(kk == nk - 1)
            def _():
                finish(acc[...])

    out_spec = pl.BlockSpec((None, tm, tn), lambda i, j, kk: (j // cpn, i, j % cpn))
    return pl.pallas_call(
        body, name=name, grid=(m // tm, n // tn, nk),
        in_specs=[pl.BlockSpec((tk, tm), lambda i, j, kk: (kk, i)),
                  pl.BlockSpec((tk, tn), lambda i, j, kk: (kk, j))],
        out_specs=[out_spec, out_spec],
        out_shape=[jax.ShapeDtypeStruct((n_shards, m, c), F32), jax.ShapeDtypeStruct((n_shards, m, c), BF16)],
        scratch_shapes=[pltpu.VMEM((tm, tn) if nk > 1 else (8, 128), F32)],
        compiler_params=_params("parallel", "parallel", "arbitrary"),
    )(a, b)


def _row_spec(tr, width, col=0):
    return pl.BlockSpec((tr, width), lambda i, col=col: (i, col))


def _vec_spec(width, col=0):
    return pl.BlockSpec((1, width), lambda i, col=col: (0, col))


def _accumulate(ref, value, first):
    @pl.when(first)
    def _():
        ref[...] = value

    @pl.when(jnp.logical_not(first))
    def _():
        ref[...] += value


def _rmsnorm_fwd(x, g, *, name, tr=512):
    t, d = x.shape
    tr = _pick(t, tr, 8)

    def body(x_ref, g_ref, o_ref):
        xv = x_ref[...]
        r = lax.rsqrt(jnp.mean(xv * xv, axis=-1, keepdims=True) + RMS_EPS)
        o_ref[...] = (xv * r * g_ref[...]).astype(BF16)

    return pl.pallas_call(
        body, name=name, grid=(t // tr,),
        in_specs=[_row_spec(tr, d), _vec_spec(d)], out_specs=_row_spec(tr, d),
        out_shape=jax.ShapeDtypeStruct((t, d), BF16), compiler_params=_params("parallel"),
    )(x, g)


def _rmsnorm_bwd(x, g, dy, dres, *, name, want_bf16, tr=512):
    t, d = x.shape
    tr = _pick(t, tr, 8)

    def body(x_ref, g_ref, dy_ref, dres_ref, *outs):
        dx_ref, dg_ref = outs[0], outs[-1]
        xv, dyv = x_ref[...], dy_ref[...].astype(F32)
        r = lax.rsqrt(jnp.mean(xv * xv, axis=-1, keepdims=True) + RMS_EPS)
        gy = dyv * g_ref[...]
        dx = dres_ref[...] + r * gy - xv * (r * r * r) * jnp.mean(xv * gy, axis=-1, keepdims=True)
        dx_ref[...] = dx
        if want_bf16:
            outs[1][...] = dx.astype(BF16)
        _accumulate(dg_ref, jnp.sum(dyv * xv * r, axis=0, keepdims=True), pl.program_id(0) == 0)

    out_shape = [jax.ShapeDtypeStruct((t, d), F32)]
    out_specs = [_row_spec(tr, d)]
    if want_bf16:
        out_shape.append(jax.ShapeDtypeStruct((t, d), BF16))
        out_specs.append(_row_spec(tr, d))
    out_shape.append(jax.ShapeDtypeStruct((1, d), F32))
    out_specs.append(_vec_spec(d))
    return pl.pallas_call(
        body, name=name, grid=(t // tr,),
        in_specs=[_row_spec(tr, d), _vec_spec(d), _row_spec(tr, d), _row_spec(tr, d)],
        out_specs=out_specs, out_shape=out_shape, compiler_params=_params("arbitrary"),
    )(x, g, dy, dres)


CONV_ROWS = 16


def _seq_specs(dims, ts, width, halo, col, *, nxt=False):
    nst, per = dims.seq // ts, ts // halo
    last = dims.tokens // halo - 1
    cur = pl.BlockSpec((ts, width), lambda b, i: (b * nst + i, col))
    if nxt:
        edge = pl.BlockSpec((halo, width), lambda b, i: (jnp.minimum((b * nst + i + 1) * per, last), col))
    else:
        edge = pl.BlockSpec((halo, width), lambda b, i: (jnp.maximum((b * nst + i) * per - 1, 0), col))
    return cur, edge


SUBLANES = 8


def _shifted_copies(buf, shifted):
    rows = shifted.shape[1]
    for s in range(1, SUBLANES):
        shifted[s - 1] = buf[pl.ds(s, rows), :]


def _window(buf, shifted, start, size):
    a, s = divmod(start, SUBLANES)
    src = buf if s == 0 else shifted.at[s - 1]
    return src[pl.ds(SUBLANES * a, size), :]


def _conv_branch_fwd(z, w, b, g, dims, *, name, ts=128):
    t, c, kw = z.shape[0], dims.d_model, dims.conv_width
    base = CONV_HALO - (kw - 1)

    def body(av_ref, hv_ref, ag_ref, hg_ref, w_ref, b_ref, g_ref, a1_ref, a3_ref, buf, shifted):
        i = pl.program_id(1)
        buf[CONV_HALO:, :] = av_ref[...].astype(F32) * _sigmoid(ag_ref[...].astype(F32))
        buf[0:CONV_HALO, :] = jnp.where(i > 0, hv_ref[...].astype(F32) * _sigmoid(hg_ref[...].astype(F32)), 0.0)
        _shifted_copies(buf, shifted)
        for r0 in range(0, ts, CONV_ROWS):
            acc = jnp.broadcast_to(b_ref[...], (CONV_ROWS, c))
            for k in range(kw):
                acc = acc + w_ref[k:k + 1, :] * _window(buf, shifted, r0 + base + k, CONV_ROWS)
            a1_ref[r0:r0 + CONV_ROWS, :] = acc
            a2 = acc * lax.rsqrt(jnp.mean(acc * acc, axis=-1, keepdims=True) + RMS_EPS) * g_ref[...]
            a3_ref[r0:r0 + CONV_ROWS, :] = (a2 * _sigmoid(a2)).astype(BF16)

    vec = pl.BlockSpec((1, c), lambda b, i: (0, 0))
    out = pl.BlockSpec((ts, c), lambda b, i: (b * (dims.seq // ts) + i, 0))
    return pl.pallas_call(
        body, name=name, grid=(dims.batch_local, dims.seq // ts),
        in_specs=[*_seq_specs(dims, ts, c, CONV_HALO, 0), *_seq_specs(dims, ts, c, CONV_HALO, 1),
                  pl.BlockSpec((CONV_HALO, c), lambda b, i: (0, 0)), vec, vec],
        out_specs=[out, out],
        out_shape=[jax.ShapeDtypeStruct((t, c), F32), jax.ShapeDtypeStruct((t, c), BF16)],
        scratch_shapes=[pltpu.VMEM((CONV_HALO + ts, c), F32),
                        pltpu.VMEM((SUBLANES - 1, CONV_HALO + ts - SUBLANES, c), F32)],
        compiler_params=_params("parallel", "parallel"),
    )(z, z, z, z, w, b, g)


def _conv_norm_bwd(da3, a1, g, *, name, tr=256):
    t, c = a1.shape
    tr = _pick(t, tr, 8)

    def body(d_ref, a_ref, g_ref, o_ref, dg_ref):
        a1v, gv = a_ref[...], g_ref[...]
        r = lax.rsqrt(jnp.mean(a1v * a1v, axis=-1, keepdims=True) + RMS_EPS)
        a2 = a1v * r * gv
        sg = _sigmoid(a2)
        da2 = d_ref[...].astype(F32) * sg * (1.0 + a2 * (1.0 - sg))
        gy = da2 * gv
        o_ref[...] = r * gy - a1v * (r * r * r) * jnp.mean(a1v * gy, axis=-1, keepdims=True)
        _accumulate(dg_ref, jnp.sum(da2 * a1v * r, axis=0, keepdims=True), pl.program_id(0) == 0)

    return pl.pallas_call(
        body, name=name, grid=(t // tr,),
        in_specs=[_row_spec(tr, c), _row_spec(tr, c), _vec_spec(c)],
        out_specs=[_row_spec(tr, c), _vec_spec(c)],
        out_shape=[jax.ShapeDtypeStruct((t, c), F32), jax.ShapeDtypeStruct((1, c), F32)],
        compiler_params=_params("arbitrary"),
    )(da3, a1, g)


def _conv_branch_bwd(da1, z, w, rest_of_dz, dims, *, name, ts=128):
    t, c, kw = z.shape[0], dims.d_model, dims.conv_width
    nst = dims.seq // ts
    base = CONV_HALO - (kw - 1)
    n_rest = len(rest_of_dz)
    total = 2 * c + sum(r.shape[1] for r in rest_of_dz)

    def body(d_ref, dn_ref, av_ref, hv_ref, ag_ref, hg_ref, w_ref, *more):
        rest_refs = more[:n_rest]
        dz_ref, dw_ref, db_ref, abuf, dbuf, ashift, dshift = more[n_rest:]
        col = 2 * c
        for r in rest_refs:
            dz_ref[:, col:col + r.shape[1]] = r[...]
            col += r.shape[1]
        i = pl.program_id(1)
        first = jnp.logical_and(pl.program_id(0) == 0, i == 0)
        abuf[CONV_HALO:, :] = av_ref[...].astype(F32) * _sigmoid(ag_ref[...].astype(F32))
        abuf[0:CONV_HALO, :] = jnp.where(i > 0, hv_ref[...].astype(F32) * _sigmoid(hg_ref[...].astype(F32)), 0.0)
        d1 = d_ref[...]
        dbuf[0:ts, :] = d1
        dbuf[ts:, :] = jnp.where(i < nst - 1, dn_ref[...], 0.0)
        _shifted_copies(abuf, ashift)
        _shifted_copies(dbuf, dshift)

        @pl.when(first)
        def _():
            dw_ref[...] = jnp.zeros_like(dw_ref)
            db_ref[...] = jnp.zeros_like(db_ref)

        db_ref[...] += jnp.sum(d1, axis=0, keepdims=True)
        for k in range(kw):
            dw_ref[k:k + 1, :] += jnp.sum(d1 * _window(abuf, ashift, base + k, ts), axis=0, keepdims=True)
        for r0 in range(0, ts, CONV_ROWS):
            acc = jnp.zeros((CONV_ROWS, c), F32)
            for k in range(kw):
                acc = acc + w_ref[k:k + 1, :] * _window(dbuf, dshift, r0 + (kw - 1) - k, CONV_ROWS)
            av = av_ref[r0:r0 + CONV_ROWS, :].astype(F32)
            sg = _sigmoid(ag_ref[r0:r0 + CONV_ROWS, :].astype(F32))
            dz_ref[r0:r0 + CONV_ROWS, 0:c] = (acc * sg).astype(BF16)
            dz_ref[r0:r0 + CONV_ROWS, c:2 * c] = (acc * av * sg * (1.0 - sg)).astype(BF16)

    cur, nxt = _seq_specs(dims, ts, c, CONV_HALO, 0, nxt=True)
    return pl.pallas_call(
        body, name=name, grid=(dims.batch_local, nst),
        in_specs=[cur, nxt, *_seq_specs(dims, ts, c, CONV_HALO, 0), *_seq_specs(dims, ts, c, CONV_HALO, 1),
                  pl.BlockSpec((CONV_HALO, c), lambda b, i: (0, 0))]
        + [pl.BlockSpec((ts, r.shape[1]), lambda b, i: (b * nst + i, 0)) for r in rest_of_dz],
        out_specs=[pl.BlockSpec((ts, total), lambda b, i: (b * nst + i, 0)),
                   pl.BlockSpec((CONV_HALO, c), lambda b, i: (0, 0)), pl.BlockSpec((1, c), lambda b, i: (0, 0))],
        out_shape=[jax.ShapeDtypeStruct((t, total), BF16), jax.ShapeDtypeStruct((CONV_HALO, c), F32),
                   jax.ShapeDtypeStruct((1, c), F32)],
        scratch_shapes=[pltpu.VMEM((CONV_HALO + ts, c), F32)] * 2
        + [pltpu.VMEM((SUBLANES - 1, CONV_HALO + ts - SUBLANES, c), F32)] * 2,
        compiler_params=_params("arbitrary", "arbitrary"),
    )(da1, da1, z, z, z, z, w, *rest_of_dz)


FFN_ROWS = 16
FFN_COLS = 256


def _ffn_chunks(ts, f):
    cw = _pick(f, FFN_COLS)
    return [(r0, c0, cw) for r0 in range(0, ts, FFN_ROWS) for c0 in range(0, f, cw)]


def _tap_sources(buf, moved, offsets, rows):
    taps, used = [], 0
    for off in offsets:
        if off % SUBLANES:
            moved[used] = buf[pl.ds(off, rows), :]
            taps.append((moved.at[used], 0))
            used += 1
        else:
            taps.append((buf, off))
    return taps


def _moved_copies(offsets):
    return sum(1 for off in offsets if off % SUBLANES)


def _taps_sum(taps, w_ref, init, r0, cols):
    for k, (src, off) in enumerate(taps):
        init = init + w_ref[k:k + 1, cols] * src[pl.ds(off + r0, init.shape[0]), cols]
    return init


def _ffn_bwd(dact, up, w, b, dims, *, name, ts=128):
    t, f, kw = up.shape[0], dims.d_ff, dims.ffn_conv_width
    nst = dims.seq // ts
    fwd_offsets = [FFN_HALO - (kw - 1) + k for k in range(kw)]
    bwd_offsets = [(kw - 1) - k for k in range(kw)]
    dact_halo = 2 * FFN_HALO

    def body(d_ref, dn_ref, up_ref, hp_ref, hn_ref, w_ref, b_ref, o_ref, dw_ref, db_ref, buf, moved, dbuf, dmoved):
        i = pl.program_id(1)
        first = jnp.logical_and(pl.program_id(0) == 0, i == 0)
        more = i < nst - 1
        buf[0:FFN_HALO, :] = jnp.where(i > 0, hp_ref[...], 0.0)
        buf[FFN_HALO:FFN_HALO + ts, :] = up_ref[...]
        buf[FFN_HALO + ts:, :] = hn_ref[...]
        taps = _tap_sources(buf, moved, fwd_offsets, ts + FFN_HALO)

        def du_chunk(r0, rows, c0, cw, d):
            vcols, gcols = slice(c0, c0 + cw), slice(f + c0, f + c0 + cw)
            uv = _taps_sum(taps, w_ref, jnp.broadcast_to(b_ref[:, vcols], (rows, cw)), r0, vcols)
            ug = _taps_sum(taps, w_ref, jnp.broadcast_to(b_ref[:, gcols], (rows, cw)), r0, gcols)
            sg = _sigmoid(ug)
            dbuf[r0:r0 + rows, vcols] = d * ug * sg
            dbuf[r0:r0 + rows, gcols] = d * uv * sg * (1.0 + ug * (1.0 - sg))

        for r0, c0, cw in _ffn_chunks(ts, f):
            du_chunk(r0, FFN_ROWS, c0, cw, d_ref[r0:r0 + FFN_ROWS, c0:c0 + cw].astype(F32))
        for _, c0, cw in _ffn_chunks(FFN_ROWS, f):
            d_next = dn_ref[:, c0:c0 + cw].astype(F32)[0:FFN_HALO]
            du_chunk(ts, FFN_HALO, c0, cw, jnp.where(more, d_next, 0.0))

        @pl.when(first)
        def _():
            dw_ref[...] = jnp.zeros_like(dw_ref)
            db_ref[...] = jnp.zeros_like(db_ref)

        du = dbuf[0:ts, :]
        db_ref[...] += jnp.sum(du, axis=0, keepdims=True)
        for k, (src, off) in enumerate(taps):
            dw_ref[k:k + 1, :] += jnp.sum(du * src[pl.ds(off, ts), :], axis=0, keepdims=True)

        dtaps = _tap_sources(dbuf, dmoved, bwd_offsets, ts)
        for r0, c0, cw in _ffn_chunks(ts, 2 * f):
            cols = slice(c0, c0 + cw)
            o_ref[r0:r0 + FFN_ROWS, cols] = _taps_sum(dtaps, w_ref, jnp.zeros((FFN_ROWS, cw), F32), r0, cols).astype(BF16)

    up_cur, up_prev = _seq_specs(dims, ts, 2 * f, FFN_HALO, 0)
    _, up_next = _seq_specs(dims, ts, 2 * f, FFN_HALO, 0, nxt=True)
    d_cur, d_next = _seq_specs(dims, ts, f, dact_halo, 0, nxt=True)
    full = lambda rows: pl.BlockSpec((rows, 2 * f), lambda b_, i: (0, 0))
    return pl.pallas_call(
        body, name=name, grid=(dims.batch_local, nst),
        in_specs=[d_cur, d_next, up_cur, up_prev, up_next, full(FFN_HALO), full(1)],
        out_specs=[pl.BlockSpec((ts, 2 * f), lambda b_, i: (b_ * nst + i, 0)), full(FFN_HALO), full(1)],
        out_shape=[jax.ShapeDtypeStruct((t, 2 * f), BF16), jax.ShapeDtypeStruct((FFN_HALO, 2 * f), F32),
                   jax.ShapeDtypeStruct((1, 2 * f), F32)],
        scratch_shapes=[pltpu.VMEM((ts + 2 * FFN_HALO, 2 * f), F32),
                        pltpu.VMEM((_moved_copies(fwd_offsets), ts + FFN_HALO, 2 * f), F32),
                        pltpu.VMEM((ts + FFN_HALO, 2 * f), F32),
                        pltpu.VMEM((_moved_copies(bwd_offsets), ts, 2 * f), F32)],
        compiler_params=_params("arbitrary", "arbitrary"),
    )(dact, dact, up, up, up, w, b)


def _ffn_act_fwd(up, w, b, dims, *, name, ts=128):
    t, f, kw = up.shape[0], dims.d_ff, dims.ffn_conv_width
    offsets = [FFN_HALO - (kw - 1) + k for k in range(kw)]

    def body(up_ref, h_ref, w_ref, b_ref, o_ref, buf, moved):
        buf[FFN_HALO:, :] = up_ref[...]
        buf[0:FFN_HALO, :] = jnp.where(pl.program_id(1) > 0, h_ref[...], 0.0)
        taps = _tap_sources(buf, moved, offsets, ts)
        for r0, c0, cw in _ffn_chunks(ts, f):
            vcols, gcols = slice(c0, c0 + cw), slice(f + c0, f + c0 + cw)
            uv = _taps_sum(taps, w_ref, jnp.broadcast_to(b_ref[:, vcols], (FFN_ROWS, cw)), r0, vcols)
            ug = _taps_sum(taps, w_ref, jnp.broadcast_to(b_ref[:, gcols], (FFN_ROWS, cw)), r0, gcols)
            o_ref[r0:r0 + FFN_ROWS, vcols] = (ug * _sigmoid(ug) * uv).astype(BF16)

    full = lambda rows: pl.BlockSpec((rows, 2 * f), lambda b_, i: (0, 0))
    return pl.pallas_call(
        body, name=name, grid=(dims.batch_local, dims.seq // ts),
        in_specs=[*_seq_specs(dims, ts, 2 * f, FFN_HALO, 0), full(FFN_HALO), full(1)],
        out_specs=pl.BlockSpec((ts, f), lambda b_, i: (b_ * (dims.seq // ts) + i, 0)),
        out_shape=jax.ShapeDtypeStruct((t, f), BF16),
        scratch_shapes=[pltpu.VMEM((FFN_HALO + ts, 2 * f), F32), pltpu.VMEM((_moved_copies(offsets), ts, 2 * f), F32)],
        compiler_params=_params("parallel", "parallel"),
    )(up, up, w, b)


def _dot_nt(a, b):
    return lax.dot_general(a, b, (((1,), (1,)), ((), ())), preferred_element_type=F32)


def _dot_tn(a, b):
    return lax.dot_general(a, b, (((0,), (0,)), ((), ())), preferred_element_type=F32)


LANES = 128
MASK_BIAS = 1e30
RESIDUE_DILATIONS = tuple(d for d in DILATIONS if d > 1)


def _rows_to_residues(value, out_ref, scr, d):
    rows, width = value.shape
    for c in range(width // LANES):
        cols = slice(LANES * c, LANES * (c + 1))
        scr[c] = value[:, cols]
        for r in range(d):
            out_ref[r, :, cols] = scr[c, pl.ds(r, rows // d, stride=d), :].astype(out_ref.dtype)


def _residues_to_rows(in_ref, scr, d):
    _, n, width = in_ref.shape
    slabs = []
    for c in range(width // LANES):
        cols = slice(LANES * c, LANES * (c + 1))
        for r in range(d):
            scr[c, pl.ds(r, n, stride=d), :] = in_ref[r, :, cols].astype(F32)
        slabs.append(scr[c])
    return slabs[0] if len(slabs) == 1 else jnp.concatenate(slabs, axis=1)


def _residue_shape(dims, d, width, dtype):
    return jax.ShapeDtypeStruct((dims.batch_local, d, dims.seq // d, width), dtype)


def _residue_spec(dims, d, tr, width):
    tiles = dims.seq // tr
    return pl.BlockSpec((None, d, tr // d, width), lambda i: (i // tiles, 0, i % tiles, 0))


def _head_sum_matrix(dims):
    a = dims.n_heads * dims.head_dim
    head = jnp.arange(a, dtype=jnp.int32) // dims.head_dim
    return (head[:, None] == jnp.arange(LANES, dtype=jnp.int32)[None, :]).astype(BF16)


def _two_pass_dot(v, m):
    hi = v.astype(BF16)
    lo = (v - hi.astype(F32)).astype(BF16)
    return jnp.dot(hi, m, preferred_element_type=F32) + jnp.dot(lo, m, preferred_element_type=F32)


def _residue_permutations(tr):
    out = []
    for d in RESIDUE_DILATIONS:
        dst = jnp.arange(tr, dtype=jnp.int32)
        src = d * (dst % (tr // d)) + dst // (tr // d)
        out.append((src[:, None] == jnp.arange(tr, dtype=jnp.int32)[None, :]).astype(BF16))
    return out


def _bf16_rows_to_residues(value, out_ref, perm_ref, d):
    n = value.shape[0] // d
    moved = jnp.dot(perm_ref[...], value, preferred_element_type=F32).astype(out_ref.dtype)
    for r in range(d):
        out_ref[r] = moved[r * n:(r + 1) * n]


def _bf16_residues_to_rows(in_ref, back_ref):
    d = in_ref.shape[0]
    stacked = jnp.concatenate([in_ref[r] for r in range(d)], axis=0)
    return jnp.dot(back_ref[...], stacked, preferred_element_type=F32)


def _qkv_layouts_fwd(z, gq, gk, head_ones, dims, *, name, tr=256):
    t = z.shape[0]
    a = dims.n_heads * dims.head_dim
    q_scale = dims.head_dim ** -0.5
    nres = len(RESIDUE_DILATIONS)

    def body(q_ref, k_ref, v_ref, gq_ref, gk_ref, sum_ref, spread_ref, *rest):
        perm_refs, outs = rest[:nres], rest[nres:]
        qv, kv = q_ref[...].astype(F32), k_ref[...].astype(F32)
        mean = lambda val: _two_pass_dot(_two_pass_dot(val, sum_ref[...]), spread_ref[...]) * (1.0 / dims.head_dim)
        rq = lax.rsqrt(mean(qv * qv) + RMS_EPS)
        rk = lax.rsqrt(mean(kv * kv) + RMS_EPS)
        values = ((qv * rq * gq_ref[...] * q_scale).astype(BF16), (kv * rk * gk_ref[...]).astype(BF16), v_ref[...])
        for j, val in enumerate(values):
            outs[j][...] = val
            for g, d in enumerate(RESIDUE_DILATIONS):
                _bf16_rows_to_residues(val, outs[3 * (g + 1) + j], perm_refs[g], d)

    out_specs = [_row_spec(tr, a)] * 3
    out_shape = [jax.ShapeDtypeStruct((t, a), BF16)] * 3
    for d in RESIDUE_DILATIONS:
        out_specs += [_residue_spec(dims, d, tr, a)] * 3
        out_shape += [_residue_shape(dims, d, a, BF16)] * 3
    outs = pl.pallas_call(
        body, name=name, grid=(t // tr,),
        in_specs=[_row_spec(tr, a, 2), _row_spec(tr, a, 3), _row_spec(tr, a, 4), _vec_spec(a), _vec_spec(a),
                  pl.BlockSpec((a, LANES), lambda i: (0, 0)), pl.BlockSpec((LANES, a), lambda i: (0, 0))]
        + [pl.BlockSpec((tr, tr), lambda i: (0, 0))] * nres,
        out_specs=out_specs, out_shape=out_shape,
        compiler_params=_params("parallel"),
    )(z, z, z, gq, gk, *head_ones, *_residue_permutations(tr))
    return {d: tuple(outs[3 * g:3 * g + 3]) for g, d in enumerate((1,) + RESIDUE_DILATIONS)}


ATTN_RESIDUES_PER_STEP = 4
ATTN_RESIDUES_PER_STEP_WINDOWED = 2


def _attn_groups(dims, dil):
    return (dims.batch_local, dil) if dil > 1 else (1, dims.batch_local)


def _attn_array(x, dims, dil):
    return x if dil > 1 else x.reshape(1, dims.batch_local, dims.seq, x.shape[-1])


def _attn_residues(dims, dil):
    one_block = dims.seq // dil == ATTN_BLOCK
    return math.gcd(_attn_groups(dims, dil)[1], ATTN_RESIDUES_PER_STEP if one_block else ATTN_RESIDUES_PER_STEP_WINDOWED)


def _per_residue(body, rs):
    if rs == 1:
        return body

    def stepped(*refs):
        for r in range(rs):
            body(*[ref.at[r] for ref in refs])

    return stepped


def _attn_specs(dims, dil, width):
    blk = ATTN_BLOCK
    nb = dims.seq // dil // blk
    rs = _attn_residues(dims, dil)
    lead, groups = _attn_groups(dims, dil)
    if rs > 1 and nb == 1:
        grid = (lead, groups // rs)
        at = lambda f: pl.BlockSpec((None, rs, blk, width), lambda b, r: (b, r, 0, 0))
    elif rs > 1:
        grid = (lead, groups // rs, nb)
        at = lambda f: pl.BlockSpec((None, rs, blk, width), lambda b, r, i: (b, r, f(i), 0))
    else:
        grid = (lead, groups, nb)
        at = lambda f: pl.BlockSpec((None, None, blk, width), lambda b, r, i: (b, r, f(i), 0))
    return grid, at(lambda i: i), at(lambda i: jnp.maximum(i - 1, 0)), at(lambda i: jnp.minimum(i + 1, nb - 1))


def _head_slopes(n_heads):
    h = lax.broadcasted_iota(jnp.int32, (n_heads, 1, 1), 0).astype(F32)
    return jnp.exp((h + 1.0) * (-8.0 / n_heads * math.log(2.0)))


def _pair_masks(hd):
    low = lax.broadcasted_iota(jnp.int32, (1, 2 * hd), 1) < hd
    return low, jnp.logical_not(low)


def _attn_fwd(q, k, v, dims, dil, *, name):
    a = dims.n_heads * dims.head_dim
    heads, hd, blk = dims.n_heads, dims.head_dim, ATTN_BLOCK
    assert 2 * hd == LANES and heads % 2 == 0 and heads <= LANES
    nb = dims.seq // dil // blk
    has_prev = nb > 1
    nkeys = 2 * blk if has_prev else blk
    grid, cur, prev, _ = _attn_specs(dims, dil, a)
    _, cur_stat, _, _ = _attn_specs(dims, dil, LANES)

    def body(*refs):
        if has_prev:
            q_ref, kc_ref, vc_ref, kp_ref, vp_ref, o_ref, lse_ref, s_scr, p_scr, k_st, v_st = refs
            k_st[0:blk, :], k_st[blk:, :] = kp_ref[...], kc_ref[...]
            v_st[0:blk, :], v_st[blk:, :] = vp_ref[...], vc_ref[...]
        else:
            q_ref, k_st, v_st, o_ref, lse_ref, s_scr, p_scr = refs
        low, high = _pair_masks(hd)

        for hp in range(heads // 2):
            sl = slice(LANES * hp, LANES * (hp + 1))
            q2 = q_ref[:, sl]
            kcat = k_st[:, sl]
            s_scr[2 * hp] = _dot_nt(jnp.where(low, q2, jnp.zeros_like(q2)), kcat)
            s_scr[2 * hp + 1] = _dot_nt(jnp.where(high, q2, jnp.zeros_like(q2)), kcat)

        iq = lax.broadcasted_iota(jnp.int32, (blk, nkeys), 0)
        jk = lax.broadcasted_iota(jnp.int32, (blk, nkeys), 1)
        if has_prev:
            steps = iq + blk - jk
            valid = (steps >= 0) & (steps <= blk) & ((jk >= blk) | (pl.program_id(len(grid) - 1) > 0))
        else:
            steps = iq - jk
            valid = steps >= 0
        bias = jnp.where(valid, steps.astype(F32) * (-float(dil)), -MASK_BIAS)
        s = s_scr[...] + _head_slopes(heads) * bias[None]
        m = jnp.max(s, axis=-1, keepdims=True)
        p = jnp.exp(s - m)
        l = jnp.sum(p, axis=-1, keepdims=True)
        p_scr[...] = p.astype(BF16)
        inv = 1.0 / l
        lse = m + jnp.log(l)

        lane = lax.broadcasted_iota(jnp.int32, (blk, LANES), 1)
        stat = jnp.zeros((blk, LANES), F32)
        for hp in range(heads // 2):
            sl = slice(LANES * hp, LANES * (hp + 1))
            vcat = v_st[:, sl]
            pv_a = jnp.dot(p_scr[2 * hp], vcat, preferred_element_type=F32) * inv[2 * hp]
            pv_b = jnp.dot(p_scr[2 * hp + 1], vcat, preferred_element_type=F32) * inv[2 * hp + 1]
            o_ref[:, sl] = jnp.where(low, pv_a, pv_b).astype(BF16)
            stat = jnp.where(lane == 2 * hp, lse[2 * hp], stat)
            stat = jnp.where(lane == 2 * hp + 1, lse[2 * hp + 1], stat)
        lse_ref[...] = stat

    q4, k4, v4 = (_attn_array(x, dims, dil) for x in (q, k, v))
    rs = _attn_residues(dims, dil)
    per_step = lambda shape: shape if rs == 1 else (rs,) + shape
    o, lse = pl.pallas_call(
        _per_residue(body, rs), name=name, grid=grid,
        in_specs=[cur, cur, cur] + ([prev, prev] if has_prev else []),
        out_specs=[cur, cur_stat],
        out_shape=[jax.ShapeDtypeStruct(q4.shape, BF16), jax.ShapeDtypeStruct(q4.shape[:-1] + (LANES,), F32)],
        scratch_shapes=[pltpu.VMEM(per_step((heads, blk, nkeys)), F32), pltpu.VMEM(per_step((heads, blk, nkeys)), BF16)]
        + ([pltpu.VMEM(per_step((nkeys, a)), BF16)] * 2 if has_prev else []),
        compiler_params=_params(*["parallel"] * len(grid)),
    )(q4, k4, v4, *([k4, v4] if has_prev else []))
    return o.reshape(q.shape), lse.reshape(q.shape[:-1] + (LANES,))


def _attn_combine(groups, head_spread, dims, *, name, tr=256):
    t = dims.tokens
    a = dims.n_heads * dims.head_dim
    dils = tuple(groups)

    nres = len(RESIDUE_DILATIONS)

    def body(*refs):
        ins = refs[:2 * len(dils)]
        x_ref = refs[2 * len(dils)]
        back_refs = dict(zip(RESIDUE_DILATIONS, refs[2 * len(dils) + 1:2 * len(dils) + 1 + nres]))
        o_ref = refs[2 * len(dils) + 1 + nres]
        lse_refs = refs[2 * len(dils) + 2 + nres:-1]
        scr_stat = refs[-1]
        outs, stats = [], []
        for g, d in enumerate(dils):
            if d == 1:
                outs.append(ins[2 * g][...].astype(F32))
                stats.append(ins[2 * g + 1][...])
            else:
                outs.append(_bf16_residues_to_rows(ins[2 * g], back_refs[d]))
                stats.append(_residues_to_rows(ins[2 * g + 1], scr_stat, d))
        top = functools.reduce(jnp.maximum, stats)
        weights = [jnp.exp(s - top) for s in stats]
        total = functools.reduce(jnp.add, weights)
        joint = top + jnp.log(total)
        inv = 1.0 / total
        acc = None
        for w, o in zip(weights, outs):
            term = _two_pass_dot(w * inv, x_ref[...]) * o
            acc = term if acc is None else acc + term
        o_ref[...] = acc.astype(BF16)
        for g, d in enumerate(dils):
            if d == 1:
                lse_refs[g][...] = joint
            else:
                _rows_to_residues(joint, lse_refs[g], scr_stat, d)

    in_specs, args, lse_specs, lse_shapes = [], [], [], []
    for d in dils:
        if d == 1:
            in_specs += [_row_spec(tr, a), _row_spec(tr, LANES)]
            lse_specs.append(_row_spec(tr, LANES))
            lse_shapes.append(jax.ShapeDtypeStruct((t, LANES), F32))
        else:
            in_specs += [_residue_spec(dims, d, tr, a), _residue_spec(dims, d, tr, LANES)]
            lse_specs.append(_residue_spec(dims, d, tr, LANES))
            lse_shapes.append(_residue_shape(dims, d, LANES, F32))
        args += list(groups[d])
    outs = pl.pallas_call(
        body, name=name, grid=(t // tr,),
        in_specs=in_specs + [pl.BlockSpec((LANES, a), lambda i: (0, 0))] + [pl.BlockSpec((tr, tr), lambda i: (0, 0))] * nres,
        out_specs=[_row_spec(tr, a)] + lse_specs,
        out_shape=[jax.ShapeDtypeStruct((t, a), BF16)] + lse_shapes,
        scratch_shapes=[pltpu.VMEM((1, tr, LANES), F32)],
        compiler_params=_params("parallel"),
    )(*args, head_spread, *[jnp.transpose(p) for p in _residue_permutations(tr)])
    return outs[0], dict(zip(dils, outs[1:]))


def _attn_bwd_prep(do, o, head_sum, dims, *, name, tr=256):
    t, a = o.shape
    nres = len(RESIDUE_DILATIONS)

    def body(do_ref, o_ref, e_ref, *rest):
        perm_refs, outs, scr_stat = rest[:nres], rest[nres:-1], rest[-1]
        delta = _two_pass_dot(do_ref[...].astype(F32) * o_ref[...].astype(F32), e_ref[...])
        outs[0][...] = delta
        for g, d in enumerate(RESIDUE_DILATIONS):
            _bf16_rows_to_residues(do_ref[...], outs[1 + 2 * g], perm_refs[g], d)
            _rows_to_residues(delta, outs[2 + 2 * g], scr_stat, d)

    out_specs, out_shape = [_row_spec(tr, LANES)], [jax.ShapeDtypeStruct((t, LANES), F32)]
    for d in RESIDUE_DILATIONS:
        out_specs += [_residue_spec(dims, d, tr, a), _residue_spec(dims, d, tr, LANES)]
        out_shape += [_residue_shape(dims, d, a, BF16), _residue_shape(dims, d, LANES, F32)]
    outs = pl.pallas_call(
        body, name=name, grid=(t // tr,),
        in_specs=[_row_spec(tr, a), _row_spec(tr, a), pl.BlockSpec((a, LANES), lambda i: (0, 0))]
        + [pl.BlockSpec((tr, tr), lambda i: (0, 0))] * nres,
        out_specs=out_specs, out_shape=out_shape,
        scratch_shapes=[pltpu.VMEM((1, tr, LANES), F32)],
        compiler_params=_params("parallel"),
    )(do, o, head_sum, *_residue_permutations(tr))
    dos, deltas = {1: do}, {1: outs[0]}
    for g, d in enumerate(RESIDUE_DILATIONS):
        dos[d], deltas[d] = outs[1 + 2 * g], outs[2 + 2 * g]
    return dos, deltas


def _attn_bwd(q, k, v, do, lse, delta, dims, dil, *, name):
    a = dims.n_heads * dims.head_dim
    heads, hd, blk = dims.n_heads, dims.head_dim, ATTN_BLOCK
    nb = dims.seq // dil // blk
    has_next = nb > 1
    nq = 2 * blk if has_next else blk
    grid, cur, _, nxt = _attn_specs(dims, dil, a)
    _, cur_stat, _, nxt_stat = _attn_specs(dims, dil, LANES)

    def body(*refs):
        k_ref, v_ref, q_ref, do_ref, lse_ref, dl_ref = refs[:6]
        if has_next:
            qn_ref, don_ref, lsen_ref, dln_ref = refs[6:10]
            dq_ref, dk_ref, dv_ref, q_st, do_st, s_scr, dp_scr, p_scr, ds_scr, carry = refs[10:]
        else:
            dq_ref, dk_ref, dv_ref, q_st, do_st, s_scr, dp_scr, p_scr, ds_scr = refs[6:]
        j = pl.program_id(len(grid) - 1)
        low, high = _pair_masks(hd)
        q_st[0:blk, :] = q_ref[...]
        do_st[0:blk, :] = do_ref[...]
        if has_next:
            q_st[blk:, :] = qn_ref[...]
            do_st[blk:, :] = don_ref[...]
            lse_all = jnp.concatenate([lse_ref[...], lsen_ref[...]], axis=0)
            dl_all = jnp.concatenate([dl_ref[...], dln_ref[...]], axis=0)
        else:
            lse_all, dl_all = lse_ref[...], dl_ref[...]
        lse_t, dl_t = jnp.transpose(lse_all), jnp.transpose(dl_all)
        lse3 = jnp.stack([lse_t[h:h + 1, :] for h in range(heads)])
        dl3 = jnp.stack([dl_t[h:h + 1, :] for h in range(heads)])

        def halves(x):
            return jnp.where(low, x, jnp.zeros_like(x)), jnp.where(high, x, jnp.zeros_like(x))

        for hp in range(heads // 2):
            sl = slice(LANES * hp, LANES * (hp + 1))
            k2, v2 = k_ref[:, sl], v_ref[:, sl]
            q_a, q_b = halves(q_st[:, sl])
            do_a, do_b = halves(do_st[:, sl])
            s_scr[2 * hp], s_scr[2 * hp + 1] = _dot_nt(k2, q_a), _dot_nt(k2, q_b)
            dp_scr[2 * hp], dp_scr[2 * hp + 1] = _dot_nt(v2, do_a), _dot_nt(v2, do_b)

        jk = lax.broadcasted_iota(jnp.int32, (blk, nq), 0)
        rq = lax.broadcasted_iota(jnp.int32, (blk, nq), 1)
        if has_next:
            iq = jnp.where(rq < blk, rq, rq - blk)
            steps = jnp.where(rq < blk, iq - jk, iq - jk + blk)
            valid = ((rq < blk) & (iq >= jk)) | ((rq >= blk) & (jk >= iq) & (j + 1 < nb))
        else:
            steps, valid = rq - jk, rq >= jk
        bias = jnp.where(valid, steps.astype(F32) * (-float(dil)), -MASK_BIAS)
        p = jnp.exp(s_scr[...] + _head_slopes(heads) * bias[None] - lse3)
        p_scr[...] = p.astype(BF16)
        ds_scr[...] = (p * (dp_scr[...] - dl3)).astype(BF16)

        if has_next:
            @pl.when(j == 0)
            def _():
                carry[...] = jnp.zeros_like(carry)

        for hp in range(heads // 2):
            sl = slice(LANES * hp, LANES * (hp + 1))
            k2 = k_ref[:, sl]
            q_a, q_b = halves(q_st[:, sl])
            do_a, do_b = halves(do_st[:, sl])
            ds_a, ds_b = ds_scr[2 * hp], ds_scr[2 * hp + 1]
            dk_ref[:, sl] = (jnp.dot(ds_a, q_a, preferred_element_type=F32)
                             + jnp.dot(ds_b, q_b, preferred_element_type=F32)).astype(BF16)
            dv_ref[:, sl] = (jnp.dot(p_scr[2 * hp], do_a, preferred_element_type=F32)
                             + jnp.dot(p_scr[2 * hp + 1], do_b, preferred_element_type=F32)).astype(BF16)
            dq2 = jnp.where(low, _dot_tn(ds_a, k2), _dot_tn(ds_b, k2))
            if has_next:
                dq_ref[:, sl] = (carry[:, sl] + dq2[:blk]).astype(BF16)
                carry[:, sl] = dq2[blk:]
            else:
                dq_ref[:, sl] = dq2.astype(BF16)

    args, in_specs = [_attn_array(x, dims, dil) for x in (k, v, q, do, lse, delta)], [cur] * 4 + [cur_stat] * 2
    if has_next:
        args += [args[2], args[3], args[4], args[5]]
        in_specs += [nxt] * 2 + [nxt_stat] * 2
    shape = jax.ShapeDtypeStruct(args[2].shape, BF16)
    rs = _attn_residues(dims, dil)
    per_step = lambda dims_: dims_ if rs == 1 else (rs,) + dims_
    scratch = ([pltpu.VMEM(per_step((nq, a)), BF16)] * 2 + [pltpu.VMEM(per_step((heads, blk, nq)), F32)] * 2
               + [pltpu.VMEM(per_step((heads, blk, nq)), BF16)] * 2)
    if has_next:
        scratch.append(pltpu.VMEM(per_step((blk, a)), F32))
    grads = pl.pallas_call(
        _per_residue(body, rs), name=name, grid=grid, in_specs=in_specs, out_specs=[cur] * 3, out_shape=[shape] * 3,
        scratch_shapes=scratch,
        compiler_params=_params(*["parallel"] * (len(grid) - 1), "arbitrary"),
    )(*args)
    return tuple(g.reshape(q.shape) for g in grads)


def _qkv_layouts_bwd(z, grads, gq, gk, head_ones, dims, *, name, tr=256):
    t = z.shape[0]
    a = dims.n_heads * dims.head_dim
    q_scale = dims.head_dim ** -0.5
    dils = tuple(grads)
    nres = len(RESIDUE_DILATIONS)

    def body(q_ref, k_ref, *rest):
        d_refs = rest[:3 * len(dils)]
        gq_ref, gk_ref, sum_ref, spread_ref = rest[3 * len(dils):3 * len(dils) + 4]
        back_refs = dict(zip(RESIDUE_DILATIONS, rest[3 * len(dils) + 4:3 * len(dils) + 4 + nres]))
        dz_ref, dgq_ref, dgk_ref = rest[3 * len(dils) + 4 + nres:]
        first = pl.program_id(0) == 0
        mean = lambda val: _two_pass_dot(_two_pass_dot(val, sum_ref[...]), spread_ref[...]) * (1.0 / dims.head_dim)

        def total(j):
            acc = None
            for g, d in enumerate(dils):
                ref = d_refs[3 * g + j]
                part = ref[...].astype(F32) if d == 1 else _bf16_residues_to_rows(ref, back_refs[d])
                acc = part if acc is None else acc + part
            return acc

        def norm_bwd(x_ref, dy, g_ref, scale, col, dg_ref):
            xv = x_ref[...].astype(F32)
            dy = dy * scale
            r = lax.rsqrt(mean(xv * xv) + RMS_EPS)
            gy = dy * g_ref[...]
            dx = r * gy - xv * (r * r * r) * mean(xv * gy)
            dz_ref[:, col * a:(col + 1) * a] = dx.astype(BF16)
            _accumulate(dg_ref, jnp.sum(dy * xv * r, axis=0, keepdims=True), first)

        norm_bwd(q_ref, total(0), gq_ref, q_scale, 0, dgq_ref)
        norm_bwd(k_ref, total(1), gk_ref, 1.0, 1, dgk_ref)
        dz_ref[:, 2 * a:3 * a] = total(2).astype(BF16)

    in_specs, args = [_row_spec(tr, a, 2), _row_spec(tr, a, 3)], [z, z]
    for d in dils:
        in_specs += [_row_spec(tr, a) if d == 1 else _residue_spec(dims, d, tr, a)] * 3
        args += list(grads[d])
    in_specs += [_vec_spec(a), _vec_spec(a), pl.BlockSpec((a, LANES), lambda i: (0, 0)),
                 pl.BlockSpec((LANES, a), lambda i: (0, 0))] + [pl.BlockSpec((tr, tr), lambda i: (0, 0))] * nres
    return pl.pallas_call(
        body, name=name, grid=(t // tr,), in_specs=in_specs,
        out_specs=[_row_spec(tr, 3 * a), _vec_spec(a), _vec_spec(a)],
        out_shape=[jax.ShapeDtypeStruct((t, 3 * a), BF16)] + [jax.ShapeDtypeStruct((1, a), F32)] * 2,
        compiler_params=_params("arbitrary"),
    )(*args, gq, gk, *head_ones, *[jnp.transpose(p) for p in _residue_permutations(tr)])


def _mix_fwd(ya, yb, z, gate_b, dims, *, name, tr=512):
    t, d = ya.shape
    tr = _pick(t, tr, 8)
    first_gate_col = z.shape[1] // d - 2

    def body(ya_ref, yb_ref, ga_ref, gb_ref, ba_ref, bb_ref, o_ref):
        g_a = _sigmoid(ga_ref[...].astype(F32) + ba_ref[...])
        g_b = _sigmoid(gb_ref[...].astype(F32) + bb_ref[...])
        o_ref[...] = (g_a * ya_ref[...] + g_b * yb_ref[...]).astype(BF16)

    return pl.pallas_call(
        body, name=name, grid=(t // tr,),
        in_specs=[_row_spec(tr, d), _row_spec(tr, d), _row_spec(tr, d, first_gate_col),
                  _row_spec(tr, d, first_gate_col + 1), _vec_spec(d, 0), _vec_spec(d, 1)],
        out_specs=_row_spec(tr, d), out_shape=jax.ShapeDtypeStruct((t, d), BF16),
        compiler_params=_params("parallel"),
    )(ya, yb, z, z, gate_b, gate_b)


def _mix_bwd(dmix, ya, yb, z, gate_b, dims, *, name, tr=512):
    t, d = ya.shape
    tr = _pick(t, tr, 8)
    first_gate_col = z.shape[1] // d - 2

    def body(dm_ref, ya_ref, yb_ref, ga_ref, gb_ref, ba_ref, bb_ref, dya_ref, dyb_ref, dz_ref, db_ref):
        dm = dm_ref[...].astype(F32)
        g_a = _sigmoid(ga_ref[...].astype(F32) + ba_ref[...])
        g_b = _sigmoid(gb_ref[...].astype(F32) + bb_ref[...])
        dya_ref[...] = (dm * g_a).astype(BF16)
        dyb_ref[...] = (dm * g_b).astype(BF16)
        dl_a = dm * ya_ref[...] * g_a * (1.0 - g_a)
        dl_b = dm * yb_ref[...] * g_b * (1.0 - g_b)
        dz_ref[:, 0:d] = dl_a.astype(BF16)
        dz_ref[:, d:2 * d] = dl_b.astype(BF16)
        first = pl.program_id(0) == 0
        sums = jnp.concatenate([jnp.sum(dl_a, axis=0, keepdims=True), jnp.sum(dl_b, axis=0, keepdims=True)], axis=1)
        _accumulate(db_ref, sums, first)

    return pl.pallas_call(
        body, name=name, grid=(t // tr,),
        in_specs=[_row_spec(tr, d), _row_spec(tr, d), _row_spec(tr, d), _row_spec(tr, d, first_gate_col),
                  _row_spec(tr, d, first_gate_col + 1), _vec_spec(d, 0), _vec_spec(d, 1)],
        out_specs=[_row_spec(tr, d), _row_spec(tr, d), _row_spec(tr, 2 * d), _vec_spec(2 * d)],
        out_shape=[jax.ShapeDtypeStruct((t, d), BF16)] * 2 + [jax.ShapeDtypeStruct((t, 2 * d), BF16),
                                                              jax.ShapeDtypeStruct((1, 2 * d), F32)],
        compiler_params=_params("arbitrary"),
    )(dmix, ya, yb, z, z, gate_b, gate_b)


def _adamw(w, grads, m, v, *, name, tr=256):
    r, c = w.shape
    tr = _pick(r, tr, 8)
    ng = len(grads)
    c1 = 1.0 - ADAM_B1 ** ADAM_STEP
    c2 = 1.0 - ADAM_B2 ** ADAM_STEP

    def body(*refs):
        w_ref, g_refs, m_ref, v_ref = refs[0], refs[1:1 + ng], refs[1 + ng], refs[2 + ng]
        g_out, d_out, m_out, v_out = refs[3 + ng:]
        g = g_refs[0][...]
        for extra in g_refs[1:]:
            g = g + extra[...]
        m_new = ADAM_B1 * m_ref[...] + (1.0 - ADAM_B1) * g
        v_new = ADAM_B2 * v_ref[...] + (1.0 - ADAM_B2) * (g * g)
        g_out[...] = g
        m_out[...] = m_new
        v_out[...] = v_new
        d_out[...] = -ADAM_LR * ((m_new / c1) / (jnp.sqrt(v_new / c2) + ADAM_EPS) + ADAM_WD * w_ref[...])

    spec = pl.BlockSpec((tr, c), lambda i: (i, 0))
    return pl.pallas_call(
        body, name=name, grid=(r // tr,),
        in_specs=[spec] * (3 + ng), out_specs=[spec] * 4, out_shape=[jax.ShapeDtypeStruct((r, c), F32)] * 4,
        compiler_params=_params("parallel"),
    )(w, *grads, m, v)


CHIP_PEERS = ((1, 0), (0, 1), (1, 1))


def _place():
    return lax.axis_index("x"), lax.axis_index("y"), lax.axis_index("c")


HBM = pl.BlockSpec(memory_space=pltpu.HBM)
SEM = pl.BlockSpec(memory_space=pltpu.SEMAPHORE)
IN_FLIGHT = pltpu.SideEffectType.DATAFLOW_SIDE_EFFECTING


def _in_hbm(a):
    return pltpu.with_memory_space_constraint(a, pltpu.HBM)


def _cast_to_lands(shards, dtypes, *, name, after=None):
    n = len(shards)

    def body(*refs):
        ins, outs, bufs, sems = refs[:n], refs[n:2 * n], refs[2 * n:3 * n], refs[3 * n]
        x, y, _ = _place()
        copies = []
        for a in range(n):
            bufs[a][...] = ins[a][...].astype(dtypes[a])
            cp = pltpu.make_async_copy(bufs[a], outs[a].at[2 * x + y], sems.at[a])
            cp.start()
            copies.append(cp)
        for cp in copies:
            cp.wait()

    body, more_specs, more_args = _ordered(body, n, after)
    return pl.pallas_call(
        body, name=name, in_specs=[pl.BlockSpec(memory_space=pltpu.VMEM)] * n + more_specs, out_specs=[ANY] * n,
        out_shape=[jax.ShapeDtypeStruct((N_CHIPS,) + s.shape, dt) for s, dt in zip(shards, dtypes)],
        scratch_shapes=[pltpu.VMEM(s.shape, dt) for s, dt in zip(shards, dtypes)] + [pltpu.SemaphoreType.DMA((n,))],
        compiler_params=pltpu.CompilerParams(vmem_limit_bytes=V7X_VMEM_LIMIT_BYTES),
    )(*shards, *more_args)


def _chip_copy(src, dst, send, recv, flip, place):
    x, y, c = place
    return pltpu.make_async_remote_copy(src_ref=src, dst_ref=dst, send_sem=send, recv_sem=recv,
                                        device_id=(x ^ flip[0], y ^ flip[1], c), device_id_type=MESH)


def _my_part(land, place, halved):
    block = land.at[2 * place[0] + place[1]]
    if not halved:
        return block
    rows = land.shape[1] // 2
    return block.at[pl.ds(pl.multiple_of(place[2] * rows, rows), rows)]


def _gather_start(lands, after, *, name, halved=()):
    n = len(lands)

    def body(*refs):
        ins, send, recv, token = refs[:n], refs[n + 1], refs[n + 2], refs[-1]
        place = _place()
        for a in range(n):
            part = _my_part(ins[a], place, a in halved)
            for p, flip in enumerate(CHIP_PEERS):
                k = 3 * a + p
                _chip_copy(part, part, send.at[k], recv.at[k], flip, place).start()
        token[...] = jnp.zeros_like(token)

    outs = pl.pallas_call(
        body, name=name, in_specs=[HBM] * n + [ANY],
        out_specs=(SEM, SEM, *[HBM] * n, pl.BlockSpec(memory_space=pltpu.VMEM)),
        out_shape=(pltpu.SemaphoreType.DMA((3 * n,)), pltpu.SemaphoreType.DMA((3 * n,)),
                   *[pltpu.HBM(l.shape, l.dtype) for l in lands], jax.ShapeDtypeStruct((8, 128), F32)),
        input_output_aliases={a: 2 + a for a in range(n)},
        compiler_params=pltpu.CompilerParams(has_side_effects=IN_FLIGHT),
    )(*[_in_hbm(l) for l in lands], after)
    return outs[0], outs[1], list(outs[2:2 + n]), outs[-1]


def _gather_wait(send, recv, lands, after, *, name, halved=()):
    n = len(lands)

    def body(*refs):
        ins, send_ref, recv_ref = refs[:n], refs[n], refs[n + 1]
        place = _place()
        for a in range(n):
            part = _my_part(ins[a], place, a in halved)
            for p, flip in enumerate(CHIP_PEERS):
                k = 3 * a + p
                cp = _chip_copy(part, part, send_ref.at[k], recv_ref.at[k], flip, place)
                cp.wait_send()
                cp.wait_recv()

    after = list(after) if isinstance(after, (list, tuple)) else [after]
    return pl.pallas_call(
        body, name=name, in_specs=[HBM] * n + [SEM, SEM] + [ANY] * len(after), out_specs=[HBM] * n,
        out_shape=[pltpu.HBM(l.shape, l.dtype) for l in lands],
        input_output_aliases={a: a for a in range(n)},
        compiler_params=pltpu.CompilerParams(has_side_effects=IN_FLIGHT),
    )(*lands, send, recv, *after)


def _forward_to_sibling(land, *, name):
    rows = land.shape[1] // 2

    def body(land_ref, out_ref, send, recv):
        x, y, c = _place()
        copies = []
        for p, (fx, fy) in enumerate(CHIP_PEERS):
            chip = 2 * (x ^ fx) + (y ^ fy)
            mine = pl.ds(pl.multiple_of(c * rows, rows), rows)
            theirs = pl.ds(pl.multiple_of((1 - c) * rows, rows), rows)
            out = pltpu.make_async_remote_copy(
                src_ref=land_ref.at[chip].at[mine], dst_ref=out_ref.at[chip].at[mine], send_sem=send.at[p],
                recv_sem=recv.at[p], device_id=(x, y, 1 - c), device_id_type=MESH)
            out.start()
            copies.append((out, pltpu.make_async_remote_copy(
                src_ref=land_ref.at[chip].at[theirs], dst_ref=out_ref.at[chip].at[theirs], send_sem=send.at[p],
                recv_sem=recv.at[p], device_id=(x, y, 1 - c), device_id_type=MESH)))
        for out, arriving in copies:
            out.wait_send()
            arriving.wait_recv()

    return pl.pallas_call(
        body, name=name, in_specs=[ANY], out_specs=ANY, out_shape=jax.ShapeDtypeStruct(land.shape, land.dtype),
        input_output_aliases={0: 0},
        scratch_shapes=[pltpu.SemaphoreType.DMA((3,)), pltpu.SemaphoreType.DMA((3,))],
    )(land)


def _scatter_start(grad, *, name):
    def body(g_ref, land_ref, send, recv, g_thru, land_thru, token):
        place = _place()
        for p, flip in enumerate(CHIP_PEERS):
            peer_chip = 2 * (place[0] ^ flip[0]) + (place[1] ^ flip[1])
            _chip_copy(g_ref.at[peer_chip], land_ref.at[p], send.at[p], recv.at[p], flip, place).start()
        token[...] = jnp.zeros_like(token)

    land = lax.empty((3,) + grad.shape[1:], grad.dtype)
    return pl.pallas_call(
        body, name=name, in_specs=[HBM, HBM],
        out_specs=(SEM, SEM, HBM, HBM, pl.BlockSpec(memory_space=pltpu.VMEM)),
        out_shape=(pltpu.SemaphoreType.DMA((3,)), pltpu.SemaphoreType.DMA((3,)), pltpu.HBM(grad.shape, grad.dtype),
                   pltpu.HBM(land.shape, land.dtype), jax.ShapeDtypeStruct((8, 128), F32)),
        input_output_aliases={0: 2, 1: 3},
        compiler_params=pltpu.CompilerParams(has_side_effects=IN_FLIGHT),
    )(_in_hbm(grad), _in_hbm(land))


def _scatter_wait(started, after, *, name):
    n = len(started)

    def body(*refs):
        grads, lands = refs[:n], refs[n:2 * n]
        sends, recvs = refs[2 * n:3 * n], refs[3 * n:4 * n]
        place = _place()
        for a in range(n):
            for p, flip in enumerate(CHIP_PEERS):
                cp = _chip_copy(grads[a].at[0], lands[a].at[p], sends[a].at[p], recvs[a].at[p], flip, place)
                cp.wait_send()
                cp.wait_recv()

    grads, lands = [s[2] for s in started], [s[3] for s in started]
    after = list(after) if isinstance(after, (list, tuple)) else [after]
    outs = pl.pallas_call(
        body, name=name, in_specs=[HBM] * (2 * n) + [SEM] * (2 * n) + [ANY] * len(after), out_specs=[HBM] * (2 * n),
        out_shape=[pltpu.HBM(a.shape, a.dtype) for a in grads + lands],
        input_output_aliases={a: a for a in range(2 * n)},
        compiler_params=pltpu.CompilerParams(has_side_effects=IN_FLIGHT),
    )(*grads, *lands, *[s[0] for s in started], *[s[1] for s in started], *after)
    return list(zip(outs[:n], outs[n:]))


def _sibling_copy(src, dst, send, recv, place):
    x, y, c = place
    return pltpu.make_async_remote_copy(src_ref=src, dst_ref=dst, send_sem=send, recv_sem=recv,
                                        device_id=(x, y, 1 - c), device_id_type=MESH)


def _swap_start(arrays, *, name):
    n = len(arrays)

    def body(*refs):
        ins, lands, send, recv, token = refs[:n], refs[n:2 * n], refs[2 * n], refs[2 * n + 1], refs[-1]
        place = _place()
        for a in range(n):
            _sibling_copy(ins[a], lands[a], send.at[a], recv.at[a], place).start()
        token[...] = jnp.zeros_like(token)

    both = [_in_hbm(a) for a in arrays] + [_in_hbm(lax.empty(a.shape, a.dtype)) for a in arrays]
    outs = pl.pallas_call(
        body, name=name, in_specs=[HBM] * (2 * n),
        out_specs=(SEM, SEM, *[HBM] * (2 * n), pl.BlockSpec(memory_space=pltpu.VMEM)),
        out_shape=(pltpu.SemaphoreType.DMA((n,)), pltpu.SemaphoreType.DMA((n,)),
                   *[pltpu.HBM(a.shape, a.dtype) for a in both], jax.ShapeDtypeStruct((8, 128), F32)),
        input_output_aliases={a: 2 + a for a in range(2 * n)},
        compiler_params=pltpu.CompilerParams(has_side_effects=IN_FLIGHT),
    )(*both)
    return outs[0], outs[1], list(outs[2:2 + n]), list(outs[2 + n:2 + 2 * n]), outs[-1]


def _swap_wait(started, after, *, name):
    send, recv, arrays, lands = started[:4]
    n = len(arrays)

    def body(*refs):
        ins, zones, send_ref, recv_ref = refs[:n], refs[n:2 * n], refs[2 * n], refs[2 * n + 1]
        place = _place()
        for a in range(n):
            cp = _sibling_copy(ins[a], zones[a], send_ref.at[a], recv_ref.at[a], place)
            cp.wait_send()
            cp.wait_recv()

    after = list(after) if isinstance(after, (list, tuple)) else [after]
    outs = pl.pallas_call(
        body, name=name, in_specs=[HBM] * (2 * n) + [SEM, SEM] + [ANY] * len(after), out_specs=[HBM] * (2 * n),
        out_shape=[pltpu.HBM(a.shape, a.dtype) for a in arrays + lands],
        input_output_aliases={a: a for a in range(2 * n)},
        compiler_params=pltpu.CompilerParams(has_side_effects=IN_FLIGHT),
    )(*arrays, *lands, send, recv, *after)
    return list(outs[:n]), list(outs[n:])


def _allreduce_start(packed, *, name):
    n_dev = 8

    def body(src_ref, land_ref, send, recv, src_thru, land_thru, token):
        x, y, c = _place()
        me = 4 * x + 2 * y + c
        for p in range(1, n_dev):
            pltpu.make_async_remote_copy(
                src_ref=src_ref, dst_ref=land_ref.at[me], send_sem=send.at[p - 1], recv_sem=recv.at[p - 1],
                device_id=(x ^ (p >> 2), y ^ ((p >> 1) & 1), c ^ (p & 1)), device_id_type=MESH).start()
        token[...] = jnp.zeros_like(token)

    land = lax.empty((n_dev,) + packed.shape, packed.dtype)
    return pl.pallas_call(
        body, name=name, in_specs=[HBM, HBM],
        out_specs=(SEM, SEM, HBM, HBM, pl.BlockSpec(memory_space=pltpu.VMEM)),
        out_shape=(pltpu.SemaphoreType.DMA((n_dev - 1,)), pltpu.SemaphoreType.DMA((n_dev - 1,)),
                   pltpu.HBM(packed.shape, packed.dtype), pltpu.HBM(land.shape, land.dtype),
                   jax.ShapeDtypeStruct((8, 128), F32)),
        input_output_aliases={0: 2, 1: 3},
        compiler_params=pltpu.CompilerParams(has_side_effects=IN_FLIGHT),
    )(_in_hbm(packed), _in_hbm(land))


def _allreduce_wait(started, after, *, name):
    send, recv, packed, land = started[:4]
    n_dev = 8

    def body(src_ref, land_ref, send_ref, recv_ref, *_):
        x, y, c = _place()
        for p in range(1, n_dev):
            cp = pltpu.make_async_remote_copy(
                src_ref=src_ref, dst_ref=land_ref.at[0], send_sem=send_ref.at[p - 1], recv_sem=recv_ref.at[p - 1],
                device_id=(x ^ (p >> 2), y ^ ((p >> 1) & 1), c ^ (p & 1)), device_id_type=MESH)
            cp.wait_send()
            cp.wait_recv()

    after = list(after) if isinstance(after, (list, tuple)) else [after]
    return pl.pallas_call(
        body, name=name, in_specs=[HBM, HBM, SEM, SEM] + [ANY] * len(after), out_specs=[HBM, HBM],
        out_shape=[pltpu.HBM(packed.shape, packed.dtype), pltpu.HBM(land.shape, land.dtype)],
        input_output_aliases={0: 0, 1: 1},
        compiler_params=pltpu.CompilerParams(has_side_effects=IN_FLIGHT),
    )(packed, land, send, recv, *after)


def _sum_devices(mine, land, *, name):
    n_dev = land.shape[0]

    def body(mine_ref, land_ref, out_ref):
        x, y, c = _place()
        me = 4 * x + 2 * y + c
        total = None
        for s in range(n_dev):
            part = jnp.where(me == s, mine_ref[...], land_ref[s])
            total = part if total is None else total + part
        out_ref[...] = total

    return pl.pallas_call(body, name=name, out_shape=jax.ShapeDtypeStruct(mine.shape, mine.dtype))(mine, land)


def _sum_received(grad, land, *, name, tr=256):
    _, r, c = grad.shape
    tr = _pick(r, tr, 8)

    def body(chip_ref, g_ref, l_ref, o_ref):
        o_ref[...] = ((g_ref[...] + l_ref[0].astype(F32)) + l_ref[1].astype(F32)) + l_ref[2].astype(F32)

    chip = (2 * lax.axis_index("x") + lax.axis_index("y")).astype(jnp.int32).reshape(1)
    return pl.pallas_call(
        body, name=name,
        grid_spec=pltpu.PrefetchScalarGridSpec(
            num_scalar_prefetch=1, grid=(r // tr,),
            in_specs=[pl.BlockSpec((None, tr, c), lambda i, chip_ref: (chip_ref[0], i, 0)),
                      pl.BlockSpec((3, tr, c), lambda i, chip_ref: (0, i, 0))],
            out_specs=pl.BlockSpec((tr, c), lambda i, chip_ref: (i, 0))),
        out_shape=jax.ShapeDtypeStruct((r, c), F32), compiler_params=_params("parallel"),
    )(chip, grad, land)


def _packed_rows(size, d):
    return -(-size // (8 * d)) * 8


def _pack_rows(arrays, d):
    rows = []
    for arr in arrays:
        flat = arr.reshape(-1).astype(F32)
        n = _packed_rows(flat.shape[0], d)
        rows.append(jnp.pad(flat, (0, n * d - flat.shape[0])).reshape(n, d))
    return jnp.concatenate(rows, axis=0)


def _unpack_rows(packed, shapes, d):
    out, row = [], 0
    for shape in shapes:
        size = math.prod(shape)
        n = _packed_rows(size, d)
        out.append(packed[row:row + n].reshape(-1)[:size].reshape(shape))
        row += n
    return out


SMALL = ("norm1_g", "gate_b", "conv_b", "conv_norm_g", "q_norm_g", "k_norm_g", "norm2_g", "ffn_conv_b")
LARGE = ("w_in", "w_conv_out", "w_attn_out", "w_out", "w_up", "w_down")
WEIGHTS = ("norm1_g", "w_in", "gate_b", "conv_w", "conv_b", "conv_norm_g", "w_conv_out", "q_norm_g", "k_norm_g",
           "w_attn_out", "w_out", "norm2_g", "w_up", "ffn_conv_w", "ffn_conv_b", "w_down")


def _after(vec, token):
    return vec if token is None else vec + token[0:1, 0:1]


def _local_step(dims, x, target, small, first_weights, other_weights, send_grad):
    d, f, heads = dims.d_model, dims.d_ff, dims.n_heads
    small = dict(small)
    row = lambda name: small[name].reshape(1, -1)
    head_sum = _head_sum_matrix(dims)
    head_spread = jnp.transpose(head_sum)
    ones = (head_sum, head_spread)
    gq = jnp.tile(row("q_norm_g"), (1, heads))
    gk = jnp.tile(row("k_norm_g"), (1, heads))
    one_shard = lambda w: w.reshape(1, -1, w.shape[-1])

    h = _rmsnorm_fwd(x, row("norm1_g"), name="norm1")
    full = first_weights(h)
    w_in = full["w_in"]
    conv_w = jnp.pad(full["conv_w"], ((0, CONV_HALO - dims.conv_width), (0, 0)))
    ffn_w = jnp.pad(full["ffn_conv_w"], ((0, FFN_HALO - dims.ffn_conv_width), (0, 0)))
    z = _mm_nn(h, w_in, out_dtype=BF16, after=full.get("token"), tm=2048, tn=1792, name="in_proj")
    a1, a3 = _conv_branch_fwd(z, conv_w, row("conv_b"), row("conv_norm_g"), dims, name="conv_branch")
    qkv = _qkv_layouts_fwd(z, gq, gk, ones, dims, name="qk_norm")
    per_group = {dil: _attn_fwd(*qkv[dil], dims, dil, name=f"attn_fwd_d{dil}") for dil in DILATIONS}
    o, lse = _attn_combine(per_group, head_spread, dims, name="attn_combine")
    full = other_weights(o)
    w_up = full["w_up"]
    w_co, w_ao, w_o, w_dn = (one_shard(full[k]) for k in ("w_conv_out", "w_attn_out", "w_out", "w_down"))
    ya = _mm_nn(a3, w_co, out_dtype=F32, name="conv_out_proj")
    yb = _mm_nn(o, w_ao, out_dtype=F32, name="attn_out_proj")
    mixed = _mix_fwd(ya, yb, z, row("gate_b"), dims, name="gate_mix")
    x1, h2 = _proj_residual_norm(mixed, w_o, x, row("norm2_g"), name="out_proj_norm2")
    up = _mm_nn(h2, w_up, out_dtype=F32, tm=2048, name="up_proj")
    act = _ffn_act_fwd(up, ffn_w, row("ffn_conv_b"), dims, name="ffn_act")
    dy, dy_b, loss = _proj_residual_loss(act, w_dn, x1, target, tm=512, name="down_proj_loss")

    grads = {}

    def large(name, g):
        grads[name], g_bf16 = g
        return send_grad(name, g_bf16)

    sent = large("w_down", _mm_tn(act, dy_b, n_shards=1, name="dw_down"))
    dact = _mm_nt(dy_b, w_dn, out_dtype=BF16, after=sent, name="d_act")
    dup, dfw, dfb = _ffn_bwd(dact, up, ffn_w, row("ffn_conv_b"), dims, name="ffn_bwd")
    grads["ffn_conv_w"], grads["ffn_conv_b"] = dfw[:dims.ffn_conv_width], dfb
    sent = large("w_up", _mm_tn(h2, dup, n_shards=N_CHIPS, name="dw_up"))
    dx1, dx1_b, grads["norm2_g"] = _mm_nt_rmsnorm_bwd(dup, w_up, x1, row("norm2_g"), dy, want_bf16=True, after=sent,
                                                     name="d_h2_norm2_bwd")
    sent = large("w_out", _mm_tn(mixed, dx1_b, n_shards=1, name="dw_out"))
    dmix = _mm_nt(dx1_b, w_o, out_dtype=F32, after=sent, name="d_mix")
    dya, dyb, dz_gate, grads["gate_b"] = _mix_bwd(dmix, ya, yb, z, row("gate_b"), dims, name="gate_mix_bwd")
    sent = large("w_attn_out", _mm_tn(o, dyb, n_shards=1, name="dw_attn_out"))
    do = _mm_nt(dyb, w_ao, out_dtype=BF16, after=sent, name="d_attn")
    dos, deltas = _attn_bwd_prep(do, o, head_sum, dims, name="attn_bwd_prep")
    dqkv = {dil: _attn_bwd(*qkv[dil], dos[dil], lse[dil], deltas[dil], dims, dil, name=f"attn_bwd_d{dil}")
            for dil in DILATIONS}
    dz_qkv, dgq, dgk = _qkv_layouts_bwd(z, dqkv, gq, gk, ones, dims, name="qk_norm_bwd")
    grads["q_norm_g"] = dgq.reshape(heads, dims.head_dim).sum(axis=0)
    grads["k_norm_g"] = dgk.reshape(heads, dims.head_dim).sum(axis=0)
    sent = large("w_conv_out", _mm_tn(a3, dya, n_shards=1, name="dw_conv_out"))
    da3 = _mm_nt(dya, w_co, out_dtype=F32, after=sent, name="d_conv_act")
    da1, grads["conv_norm_g"] = _conv_norm_bwd(da3, a1, row("conv_norm_g"), name="conv_norm_bwd")
    dz, dcw, grads["conv_b"] = _conv_branch_bwd(da1, z, conv_w, [dz_qkv, dz_gate], dims, name="conv_branch_bwd")
    grads["conv_w"] = dcw[:dims.conv_width]
    sent = large("w_in", _mm_tn(h, dz, n_shards=N_CHIPS, name="dw_in"))
    dx, grads["norm1_g"] = _mm_nt_rmsnorm_bwd(dz, w_in, x, row("norm1_g"), dx1, want_bf16=False, after=sent,
                                              name="d_h_norm1_bwd")
    return loss, dx, grads


def _step(dims, x, target, w, m, v):
    d = dims.d_model
    t = dims.tokens
    sq = lambda a: a.reshape(a.shape[1:])
    w2, m2, v2 = ({k: sq(a) for k, a in grp.items()} for grp in (w, m, v))

    conv_pad = jnp.pad(w2["conv_w"], ((0, CONV_HALO - dims.conv_width), (0, 0)))
    ffn_pad = jnp.pad(w2["ffn_conv_w"], ((0, FFN_HALO - dims.ffn_conv_width), (0, 0)))
    first_names = ("w_in", "conv_w", "ffn_conv_w")
    other_names = tuple(k for k in LARGE if k not in first_names)
    lands = dict(zip(first_names, _cast_to_lands([w2["w_in"], conv_pad, ffn_pad], [BF16, F32, F32], name="cast_first")))
    first = _gather_start([lands[k] for k in first_names], x, halved=(0,), name="gather_start_first")
    lands.update(zip(other_names, _cast_to_lands([w2[k] for k in other_names], [BF16] * len(other_names),
                                                 after=first[3], name="cast_other")))
    other = []
    cols = lambda g, rows: jnp.moveaxis(g, 0, 1).reshape(g.shape[1], -1)[:rows]

    def first_weights(after):
        got = dict(zip(first_names, _gather_wait(*first[:3], [after] + [lands[k] for k in other_names], halved=(0,),
                                                 name="gather_wait_first")))
        got["w_in"] = _forward_to_sibling(got["w_in"], name="forward_w_in")
        other.extend(_gather_start([lands[k] for k in other_names], got["w_in"], name="gather_start_other"))
        got["conv_w"] = cols(got["conv_w"], dims.conv_width)
        got["ffn_conv_w"] = cols(got["ffn_conv_w"], dims.ffn_conv_width)
        got["token"] = other[3]
        return got

    def other_weights(after):
        return dict(zip(other_names, _gather_wait(*other[:3], after, name="gather_wait_other")))

    started = {}

    def send_grad(name, g):
        send, recv, g_thru, land, token = _scatter_start(g.reshape(N_CHIPS, -1, g.shape[-1]), name=f"scatter_start_{name}")
        started[name] = (send, recv, g_thru, land)
        return token

    small = {k: w2[k] for k in SMALL}
    small["norm1_g"] = _after(small["norm1_g"].reshape(1, -1), first[3])
    loss, dx, grads = _local_step(dims, x.reshape(t, d), target.reshape(t, d), small, first_weights, other_weights, send_grad)

    def my_sums(names, after, tag):
        arrived = _scatter_wait([started[k] for k in names], after, name=f"scatter_wait_{tag}")
        blocks = [grads[k].reshape(N_CHIPS, -1, grads[k].shape[-1]) for k in names]
        return [_sum_received(g, land, name=f"sum_{k}") for k, g, (_, land) in zip(names, blocks, arrived)]

    def updates(names, mine, theirs):
        return {k: _adamw(w2[k], [a, b], m2[k], v2[k], name=f"adamw_{k}") for k, a, b in zip(names, mine, theirs)}

    small_names = SMALL + ("conv_w", "ffn_conv_w")
    packed = _pack_rows([grads[k] for k in small_names] + [loss[0, 0]], d)
    reducing = _allreduce_start(packed, name="allreduce_start")
    others = [k for k in LARGE if k != "w_in"]
    mine_others = my_sums(others, [dx, reducing[4]], "others")
    swapping_others = _swap_start(mine_others, name="swap_start_others")
    mine_w_in = my_sums(["w_in"], swapping_others[4], "w_in")
    swapping_w_in = _swap_start(mine_w_in, name="swap_start_w_in")
    out = updates(others, *_swap_wait(swapping_others, swapping_w_in[4], name="swap_wait_others"))
    last_updates = [out[k][1] for k in others]
    reduced = _sum_devices(*_allreduce_wait(reducing, last_updates, name="allreduce_wait"), name="allreduce_sum")
    shapes = [grads[k].shape for k in small_names] + [()]
    *small_g, loss_total = _unpack_rows(reduced, shapes, d)
    small_g = dict(zip(small_names, small_g))
    chip = 2 * lax.axis_index("x") + lax.axis_index("y")
    for k in ("conv_w", "ffn_conv_w"):
        width = w2[k].shape[1]
        small_g[k] = lax.dynamic_slice_in_dim(small_g[k], chip * width, width, axis=1)

    small_shapes = [w2[k].shape for k in small_names]
    pack = lambda grp: _pack_rows([grp[k] for k in small_names], d)
    results = _adamw(pack(w2), [pack(small_g)], pack(m2), pack(v2), name="adamw_small")
    unpacked = [_unpack_rows(r, small_shapes, d) for r in results]
    out.update({k: tuple(u[i] for u in unpacked) for i, k in enumerate(small_names)})
    out.update(updates(["w_in"], *_swap_wait(swapping_w_in, results[1], name="swap_wait_w_in")))

    lead = lambda a: a.reshape((1,) + a.shape)
    ordered = [[lead(out[k][j].reshape(w2[k].shape)) for k in WEIGHTS] for j in range(4)]
    return (loss_total, dx.reshape(x.shape), *ordered[0], *ordered[1], *ordered[2], *ordered[3])


def kernel(x, norm1_g, w_in, gate_b, conv_w, conv_b, conv_norm_g, w_conv_out, q_norm_g, k_norm_g, w_attn_out, w_out, norm2_g, w_up, ffn_conv_w, ffn_conv_b, w_down, loss_target, m_norm1_g, m_w_in, m_gate_b, m_conv_w, m_conv_b, m_conv_norm_g, m_w_conv_out, m_q_norm_g, m_k_norm_g, m_w_attn_out, m_w_out, m_norm2_g, m_w_up, m_ffn_conv_w, m_ffn_conv_b, m_w_down, v_norm1_g, v_w_in, v_gate_b, v_conv_w, v_conv_b, v_conv_norm_g, v_w_conv_out, v_q_norm_g, v_k_norm_g, v_w_attn_out, v_w_out, v_norm2_g, v_w_up, v_ffn_conv_w, v_ffn_conv_b, v_w_down):
    w = dict(zip(WEIGHTS, (norm1_g, w_in, gate_b, conv_w, conv_b, conv_norm_g, w_conv_out, q_norm_g, k_norm_g,
                           w_attn_out, w_out, norm2_g, w_up, ffn_conv_w, ffn_conv_b, w_down)))
    m = dict(zip(WEIGHTS, (m_norm1_g, m_w_in, m_gate_b, m_conv_w, m_conv_b, m_conv_norm_g, m_w_conv_out, m_q_norm_g,
                           m_k_norm_g, m_w_attn_out, m_w_out, m_norm2_g, m_w_up, m_ffn_conv_w, m_ffn_conv_b, m_w_down)))
    v = dict(zip(WEIGHTS, (v_norm1_g, v_w_in, v_gate_b, v_conv_w, v_conv_b, v_conv_norm_g, v_w_conv_out, v_q_norm_g,
                           v_k_norm_g, v_w_attn_out, v_w_out, v_norm2_g, v_w_up, v_ffn_conv_w, v_ffn_conv_b, v_w_down)))
    dims = Dims(d_model=x.shape[-1], batch_local=x.shape[0], seq=x.shape[1], d_ff=w_down.shape[1] * N_CHIPS)
    return _step(dims, x, loss_target, w, m, v)
```

```python
import functools
import math
from typing import NamedTuple

import jax
import jax.numpy as jnp
from jax import lax
from jax.experimental import pallas as pl
from jax.experimental.pallas import tpu as pltpu

F32 = jnp.float32
BF16 = jnp.bfloat16

RMS_EPS = 1e-6
ATTN_BLOCK = 128
DILATIONS = (1, 4, 16)
CONV_HALO = 32
FFN_HALO = 8
ADAM_LR, ADAM_B1, ADAM_B2, ADAM_EPS, ADAM_WD, ADAM_STEP = 0.001, 0.9, 0.999, 1e-08, 0.01, 10
V7X_VMEM_LIMIT_BYTES = 56 * 2 ** 20
N_CHIPS = 4
MESH = pl.DeviceIdType.MESH


class Dims(NamedTuple):
    d_model: int = 1024
    n_heads: int = 16
    head_dim: int = 64
    d_ff: int = 2816
    seq: int = 2048
    batch_local: int = 2
    conv_width: int = 31
    ffn_conv_width: int = 3

    @property
    def tokens(self):
        return self.seq * self.batch_local


def _params(*semantics):
    return pltpu.CompilerParams(dimension_semantics=semantics, vmem_limit_bytes=V7X_VMEM_LIMIT_BYTES)


ANY = pl.BlockSpec(memory_space=pl.ANY)


def _ordered(body, n_inputs, after):
    after = [] if after is None else list(after) if isinstance(after, (list, tuple)) else [after]
    if not after:
        return body, [], []

    def wrapped(*refs):
        return body(*refs[:n_inputs], *refs[n_inputs + len(after):])

    return wrapped, [ANY] * len(after), after


def _pick(n, target, mult=128):
    if n <= target:
        return n
    best = None
    for t in range(mult, target + 1, mult):
        if n % t == 0:
            best = t
    assert best is not None, (n, target, mult)
    return best


def _sigmoid(v):
    return 1.0 / (1.0 + jnp.exp(-v))


def _mm_nn(a, w, *, out_dtype, name, residual=None, after=None, tm=1024, tn=1408, tk=2816):
    m, k = a.shape
    nsh, k2, c = w.shape
    assert k == k2 and a.dtype == BF16 and w.dtype == BF16
    n = nsh * c
    tm, tn, tk = _pick(m, tm, 8), _pick(c, tn), _pick(k, tk)
    nk, cpn = k // tk, c // tn

    def body(*refs):
        if residual is None:
            a_ref, w_ref, o_ref, acc = refs
        else:
            a_ref, w_ref, r_ref, o_ref, acc = refs
        prod = jnp.dot(a_ref[...], w_ref[...], preferred_element_type=F32)

        def finish(total):
            if residual is not None:
                total = total + r_ref[...]
            o_ref[...] = total.astype(out_dtype)

        if nk == 1:
            finish(prod)
        else:
            kk = pl.program_id(2)

            @pl.when(kk == 0)
            def _():
                acc[...] = prod

            @pl.when(kk > 0)
            def _():
                acc[...] += prod

            @pl.when(kk == nk - 1)
            def _():
                finish(acc[...])

    in_specs = [pl.BlockSpec((tm, tk), lambda i, j, kk: (i, kk)),
                pl.BlockSpec((None, tk, tn), lambda i, j, kk: (j // cpn, kk, j % cpn))]
    args = [a, w]
    if residual is not None:
        in_specs.append(pl.BlockSpec((tm, tn), lambda i, j, kk: (i, j)))
        args.append(residual)
    body, more_specs, more_args = _ordered(body, len(args), after)
    return pl.pallas_call(
        body, name=name, grid=(m // tm, n // tn, nk),
        in_specs=in_specs + more_specs, out_specs=pl.BlockSpec((tm, tn), lambda i, j, kk: (i, j)),
        out_shape=jax.ShapeDtypeStruct((m, n), out_dtype),
        scratch_shapes=[pltpu.VMEM((tm, tn) if nk > 1 else (8, 128), F32)],
        compiler_params=_params("parallel", "parallel", "arbitrary"),
    )(*args, *more_args)


def _proj_residual_norm(a, w, residual, g, *, name, tm=1024):
    m, k = a.shape
    _, k2, n = w.shape
    assert w.shape[0] == 1 and k == k2 and a.dtype == BF16 and w.dtype == BF16
    tm = _pick(m, tm, 8)

    def body(a_ref, w_ref, r_ref, g_ref, y_ref, h_ref):
        y = r_ref[...] + jnp.dot(a_ref[...], w_ref[...], preferred_element_type=F32)
        y_ref[...] = y
        h_ref[...] = (y * lax.rsqrt(jnp.mean(y * y, axis=-1, keepdims=True) + RMS_EPS) * g_ref[...]).astype(BF16)

    rows = lambda width: pl.BlockSpec((tm, width), lambda i: (i, 0))
    return pl.pallas_call(
        body, name=name, grid=(m // tm,),
        in_specs=[rows(k), pl.BlockSpec((None, k, n), lambda i: (0, 0, 0)), rows(n), pl.BlockSpec((1, n), lambda i: (0, 0))],
        out_specs=[rows(n), rows(n)],
        out_shape=[jax.ShapeDtypeStruct((m, n), F32), jax.ShapeDtypeStruct((m, n), BF16)],
        compiler_params=_params("parallel"),
    )(a, w, residual, g)


def _proj_residual_loss(a, w, residual, target, *, name, tm=1024):
    m, k = a.shape
    _, k2, n = w.shape
    assert w.shape[0] == 1 and k == k2 and a.dtype == BF16 and w.dtype == BF16
    tm = _pick(m, tm, 8)

    def body(a_ref, w_ref, r_ref, t_ref, dy_ref, dyb_ref, loss_ref):
        err = r_ref[...] + jnp.dot(a_ref[...], w_ref[...], preferred_element_type=F32) - t_ref[...]
        dy = err * (1.0 / n)
        dy_ref[...] = dy
        dyb_ref[...] = dy.astype(BF16)
        part = jnp.sum(jnp.sum(err * err, axis=-1, keepdims=True), axis=0, keepdims=True) * (0.5 / n)
        _accumulate(loss_ref, jnp.broadcast_to(part, (8, 128)), pl.program_id(0) == 0)

    rows = lambda width: pl.BlockSpec((tm, width), lambda i: (i, 0))
    return pl.pallas_call(
        body, name=name, grid=(m // tm,),
        in_specs=[rows(k), pl.BlockSpec((None, k, n), lambda i: (0, 0, 0)), rows(n), rows(n)],
        out_specs=[rows(n), rows(n), pl.BlockSpec((8, 128), lambda i: (0, 0))],
        out_shape=[jax.ShapeDtypeStruct((m, n), F32), jax.ShapeDtypeStruct((m, n), BF16),
                   jax.ShapeDtypeStruct((8, 128), F32)],
        compiler_params=_params("arbitrary"),
    )(a, w, residual, target)


def _mm_nt(a, w, *, out_dtype, name, after=None, tm=1024, tn=1408, tk=1792):
    m, k = a.shape
    nsh, r, c = w.shape
    assert k == nsh * c and a.dtype == BF16 and w.dtype == BF16
    tm, tn, tk = _pick(m, tm, 8), _pick(r, tn), _pick(c, tk)
    nk, cpk = k // tk, c // tk

    def body(a_ref, w_ref, o_ref, acc):
        prod = lax.dot_general(a_ref[...], w_ref[...], (((1,), (1,)), ((), ())), preferred_element_type=F32)
        if nk == 1:
            o_ref[...] = prod.astype(out_dtype)
        else:
            kk = pl.program_id(2)

            @pl.when(kk == 0)
            def _():
                acc[...] = prod

            @pl.when(kk > 0)
            def _():
                acc[...] += prod

            @pl.when(kk == nk - 1)
            def _():
                o_ref[...] = acc[...].astype(out_dtype)

    body, more_specs, more_args = _ordered(body, 2, after)
    return pl.pallas_call(
        body, name=name, grid=(m // tm, r // tn, nk),
        in_specs=[pl.BlockSpec((tm, tk), lambda i, j, kk: (i, kk)),
                  pl.BlockSpec((None, tn, tk), lambda i, j, kk: (kk // cpk, j, kk % cpk))] + more_specs,
        out_specs=pl.BlockSpec((tm, tn), lambda i, j, kk: (i, j)),
        out_shape=jax.ShapeDtypeStruct((m, r), out_dtype),
        scratch_shapes=[pltpu.VMEM((tm, tn) if nk > 1 else (8, 128), F32)],
        compiler_params=_params("parallel", "parallel", "arbitrary"),
    )(a, w, *more_args)


NORM_BWD_ROWS = 256


def _mm_nt_rmsnorm_bwd(a, w, x, g, dres, *, name, want_bf16, silu=False, after=None, tm=1024, tk=1792):
    m, k = a.shape
    nsh, r, c = w.shape
    assert k == nsh * c and a.dtype == BF16 and w.dtype == BF16 and x.shape == (m, r)
    tm, tk = _pick(m, tm, 8), _pick(c, tk)
    nk, cpk = k // tk, c // tk
    rows = _pick(tm, NORM_BWD_ROWS, 8)
    n_in = 4 if dres is None else 5

    def body(a_ref, w_ref, x_ref, g_ref, *rest):
        dres_ref = None if dres is None else rest[0]
        outs, acc = rest[n_in - 4:-1], rest[-1]
        dx_ref, dg_ref = outs[0], outs[-1]
        kk = pl.program_id(1)
        prod = lax.dot_general(a_ref[...], w_ref[...], (((1,), (1,)), ((), ())), preferred_element_type=F32)

        @pl.when(kk == 0)
        def _():
            acc[...] = prod

        @pl.when(kk > 0)
        def _():
            acc[...] += prod

        @pl.when(kk == nk - 1)
        def _():
            dg = jnp.zeros((1, r), F32)
            for r0 in range(0, tm, rows):
                part = slice(r0, r0 + rows)
                xv, dyv = x_ref[part, :], acc[part, :]
                inv = lax.rsqrt(jnp.mean(xv * xv, axis=-1, keepdims=True) + RMS_EPS)
                if silu:
                    y = xv * inv * g_ref[...]
                    sg = _sigmoid(y)
                    dyv = dyv * sg * (1.0 + y * (1.0 - sg))
                gy = dyv * g_ref[...]
                dx = inv * gy - xv * (inv * inv * inv) * jnp.mean(xv * gy, axis=-1, keepdims=True)
                if dres is not None:
                    dx = dx + dres_ref[part, :]
                dx_ref[part, :] = dx
                if want_bf16:
                    outs[1][part, :] = dx.astype(BF16)
                dg = dg + jnp.sum(dyv * xv * inv, axis=0, keepdims=True)
            _accumulate(dg_ref, dg, pl.program_id(0) == 0)

    whole = lambda: pl.BlockSpec((tm, r), lambda i, kk: (i, 0))
    vec = pl.BlockSpec((1, r), lambda i, kk: (0, 0))
    out_shape, out_specs = [jax.ShapeDtypeStruct((m, r), F32)], [whole()]
    if want_bf16:
        out_shape.append(jax.ShapeDtypeStruct((m, r), BF16))
        out_specs.append(whole())
    out_shape.append(jax.ShapeDtypeStruct((1, r), F32))
    out_specs.append(vec)
    body, more_specs, more_args = _ordered(body, n_in, after)
    residual_specs, residual_args = ([], []) if dres is None else ([whole()], [dres])
    return pl.pallas_call(
        body, name=name, grid=(m // tm, nk),
        in_specs=[pl.BlockSpec((tm, tk), lambda i, kk: (i, kk)),
                  pl.BlockSpec((None, r, tk), lambda i, kk: (kk // cpk, 0, kk % cpk)), whole(), vec]
        + residual_specs + more_specs,
        out_specs=out_specs, out_shape=out_shape,
        scratch_shapes=[pltpu.VMEM((tm, r), F32)],
        compiler_params=_params("arbitrary", "arbitrary"),
    )(a, w, x, g, *residual_args, *more_args)


MM_TN_VMEM_BYTES = 44 * 2 ** 20


def _mm_tn(a, b, *, n_shards, name, tm=1408, tn=1408):
    t, m = a.shape
    t2, n = b.shape
    assert t == t2 and a.dtype == BF16 and b.dtype == BF16
    c = n // n_shards
    tm, tn = _pick(m, tm), _pick(c, tn)
    if m // tm == 1 and n // tn == 1 and tn % (2 * LANES) == 0:
        tn //= 2
    fixed = 2 * tm * tn * 6
    if 4 * t * (tm + tn) + fixed <= MM_TN_VMEM_BYTES:
        tk = t
    else:
        tk = _pick(t, (MM_TN_VMEM_BYTES - fixed - 4 * tm * tn) // (4 * (tm + tn)), 8)
    nk, cpn = t // tk, c // tn

    def body(a_ref, b_ref, o_ref, ob_ref, acc):
        kk = pl.program_id(2)
        prod = lax.dot_general(a_ref[...], b_ref[...], (((0,), (0,)), ((), ())), preferred_element_type=F32)

        def finish(total):
            o_ref[...] = total
            ob_ref[...] = total.astype(BF16)

        if nk == 1:
            finish(prod)
        else:
            @pl.when(kk == 0)
            def _():
                acc[...] = prod

            @pl.when(kk > 0)
            def _():
                acc[...] += prod

            @pl.when(kk == nk - 1)
            def _():
                finish(acc[...])

    out_spec = pl.BlockSpec((None, tm, tn), lambda i, j, kk: (j // cpn, i, j % cpn))
    return pl.pallas_call(
        body, name=name, grid=(m // tm, n // tn, nk),
        in_specs=[pl.BlockSpec((tk, tm), lambda i, j, kk: (kk, i)),
                  pl.BlockSpec((tk, tn), lambda i, j, kk: (kk, j))],
        out_specs=[out_spec, out_spec],
        out_shape=[jax.ShapeDtypeStruct((n_shards, m, c), F32), jax.ShapeDtypeStruct((n_shards, m, c), BF16)],
        scratch_shapes=[pltpu.VMEM((tm, tn) if nk > 1 else (8, 128), F32)],
        compiler_params=_params("parallel", "parallel", "arbitrary"),
    )(a, b)


def _row_spec(tr, width, col=0):
    return pl.BlockSpec((tr, width), lambda i, col=col: (i, col))


def _vec_spec(width, col=0):
    return pl.BlockSpec((1, width), lambda i, col=col: (0, col))


def _accumulate(ref, value, first):
    @pl.when(first)
    def _():
        ref[...] = value

    @pl.when(jnp.logical_not(first))
    def _():
        ref[...] += value


def _rmsnorm_fwd(x, g, *, name, tr=512):
    t, d = x.shape
    tr = _pick(t, tr, 8)

    def body(x_ref, g_ref, o_ref):
        xv = x_ref[...]
        r = lax.rsqrt(jnp.mean(xv * xv, axis=-1, keepdims=True) + RMS_EPS)
        o_ref[...] = (xv * r * g_ref[...]).astype(BF16)

    return pl.pallas_call(
        body, name=name, grid=(t // tr,),
        in_specs=[_row_spec(tr, d), _vec_spec(d)], out_specs=_row_spec(tr, d),
        out_shape=jax.ShapeDtypeStruct((t, d), BF16), compiler_params=_params("parallel"),
    )(x, g)


CONV_ROWS = 16


def _seq_specs(dims, ts, width, halo, col, *, nxt=False):
    nst, per = dims.seq // ts, ts // halo
    last = dims.tokens // halo - 1
    cur = pl.BlockSpec((ts, width), lambda b, i: (b * nst + i, col))
    if nxt:
        edge = pl.BlockSpec((halo, width), lambda b, i: (jnp.minimum((b * nst + i + 1) * per, last), col))
    else:
        edge = pl.BlockSpec((halo, width), lambda b, i: (jnp.maximum((b * nst + i) * per - 1, 0), col))
    return cur, edge


SUBLANES = 8


def _shifted_copies(buf, shifted):
    rows = shifted.shape[1]
    for s in range(1, SUBLANES):
        shifted[s - 1] = buf[pl.ds(s, rows), :]


def _window(buf, shifted, start, size):
    a, s = divmod(start, SUBLANES)
    src = buf if s == 0 else shifted.at[s - 1]
    return src[pl.ds(SUBLANES * a, size), :]


def _conv_branch_fwd(z, w, b, g, dims, *, name, ts=128):
    t, c, kw = z.shape[0], dims.d_model, dims.conv_width
    base = CONV_HALO - (kw - 1)

    def body(av_ref, hv_ref, ag_ref, hg_ref, w_ref, b_ref, g_ref, a1_ref, a3_ref, buf, shifted):
        i = pl.program_id(1)
        buf[CONV_HALO:, :] = av_ref[...].astype(F32) * _sigmoid(ag_ref[...].astype(F32))
        buf[0:CONV_HALO, :] = jnp.where(i > 0, hv_ref[...].astype(F32) * _sigmoid(hg_ref[...].astype(F32)), 0.0)
        _shifted_copies(buf, shifted)
        for r0 in range(0, ts, CONV_ROWS):
            acc = jnp.broadcast_to(b_ref[...], (CONV_ROWS, c))
            for k in range(kw):
                acc = acc + w_ref[k:k + 1, :] * _window(buf, shifted, r0 + base + k, CONV_ROWS)
            a1_ref[r0:r0 + CONV_ROWS, :] = acc
            a2 = acc * lax.rsqrt(jnp.mean(acc * acc, axis=-1, keepdims=True) + RMS_EPS) * g_ref[...]
            a3_ref[r0:r0 + CONV_ROWS, :] = (a2 * _sigmoid(a2)).astype(BF16)

    vec = pl.BlockSpec((1, c), lambda b, i: (0, 0))
    out = pl.BlockSpec((ts, c), lambda b, i: (b * (dims.seq // ts) + i, 0))
    return pl.pallas_call(
        body, name=name, grid=(dims.batch_local, dims.seq // ts),
        in_specs=[*_seq_specs(dims, ts, c, CONV_HALO, 0), *_seq_specs(dims, ts, c, CONV_HALO, 1),
                  pl.BlockSpec((CONV_HALO, c), lambda b, i: (0, 0)), vec, vec],
        out_specs=[out, out],
        out_shape=[jax.ShapeDtypeStruct((t, c), F32), jax.ShapeDtypeStruct((t, c), BF16)],
        scratch_shapes=[pltpu.VMEM((CONV_HALO + ts, c), F32),
                        pltpu.VMEM((SUBLANES - 1, CONV_HALO + ts - SUBLANES, c), F32)],
        compiler_params=_params("parallel", "parallel"),
    )(z, z, z, z, w, b, g)


def _conv_branch_bwd(da1, z, w, rest_of_dz, dims, *, name, ts=128):
    t, c, kw = z.shape[0], dims.d_model, dims.conv_width
    nst = dims.seq // ts
    base = CONV_HALO - (kw - 1)
    n_rest = len(rest_of_dz)
    total = 2 * c + sum(r.shape[1] for r in rest_of_dz)

    def body(d_ref, dn_ref, av_ref, hv_ref, ag_ref, hg_ref, w_ref, *more):
        rest_refs = more[:n_rest]
        dz_ref, dw_ref, db_ref, abuf, dbuf, ashift, dshift = more[n_rest:]
        col = 2 * c
        for r in rest_refs:
            dz_ref[:, col:col + r.shape[1]] = r[...]
            col += r.shape[1]
        i = pl.program_id(1)
        first = jnp.logical_and(pl.program_id(0) == 0, i == 0)
        abuf[CONV_HALO:, :] = av_ref[...].astype(F32) * _sigmoid(ag_ref[...].astype(F32))
        abuf[0:CONV_HALO, :] = jnp.where(i > 0, hv_ref[...].astype(F32) * _sigmoid(hg_ref[...].astype(F32)), 0.0)
        d1 = d_ref[...]
        dbuf[0:ts, :] = d1
        dbuf[ts:, :] = jnp.where(i < nst - 1, dn_ref[...], 0.0)
        _shifted_copies(abuf, ashift)
        _shifted_copies(dbuf, dshift)

        @pl.when(first)
        def _():
            dw_ref[...] = jnp.zeros_like(dw_ref)
            db_ref[...] = jnp.zeros_like(db_ref)

        db_ref[...] += jnp.sum(d1, axis=0, keepdims=True)
        for k in range(kw):
            dw_ref[k:k + 1, :] += jnp.sum(d1 * _window(abuf, ashift, base + k, ts), axis=0, keepdims=True)
        for r0 in range(0, ts, CONV_ROWS):
            acc = jnp.zeros((CONV_ROWS, c), F32)
            for k in range(kw):
                acc = acc + w_ref[k:k + 1, :] * _window(dbuf, dshift, r0 + (kw - 1) - k, CONV_ROWS)
            av = av_ref[r0:r0 + CONV_ROWS, :].astype(F32)
            sg = _sigmoid(ag_ref[r0:r0 + CONV_ROWS, :].astype(F32))
            dz_ref[r0:r0 + CONV_ROWS, 0:c] = (acc * sg).astype(BF16)
            dz_ref[r0:r0 + CONV_ROWS, c:2 * c] = (acc * av * sg * (1.0 - sg)).astype(BF16)

    cur, nxt = _seq_specs(dims, ts, c, CONV_HALO, 0, nxt=True)
    return pl.pallas_call(
        body, name=name, grid=(dims.batch_local, nst),
        in_specs=[cur, nxt, *_seq_specs(dims, ts, c, CONV_HALO, 0), *_seq_specs(dims, ts, c, CONV_HALO, 1),
                  pl.BlockSpec((CONV_HALO, c), lambda b, i: (0, 0))]
        + [pl.BlockSpec((ts, r.shape[1]), lambda b, i: (b * nst + i, 0)) for r in rest_of_dz],
        out_specs=[pl.BlockSpec((ts, total), lambda b, i: (b * nst + i, 0)),
                   pl.BlockSpec((CONV_HALO, c), lambda b, i: (0, 0)), pl.BlockSpec((1, c), lambda b, i: (0, 0))],
        out_shape=[jax.ShapeDtypeStruct((t, total), BF16), jax.ShapeDtypeStruct((CONV_HALO, c), F32),
                   jax.ShapeDtypeStruct((1, c), F32)],
        scratch_shapes=[pltpu.VMEM((CONV_HALO + ts, c), F32)] * 2
        + [pltpu.VMEM((SUBLANES - 1, CONV_HALO + ts - SUBLANES, c), F32)] * 2,
        compiler_params=_params("arbitrary", "arbitrary"),
    )(da1, da1, z, z, z, z, w, *rest_of_dz)


FFN_ROWS = 16
FFN_COLS = 256


def _ffn_chunks(ts, f):
    cw = _pick(f, FFN_COLS)
    return [(r0, c0, cw) for r0 in range(0, ts, FFN_ROWS) for c0 in range(0, f, cw)]


def _tap_sources(buf, moved, offsets, rows):
    taps, used = [], 0
    for off in offsets:
        if off % SUBLANES:
            moved[used] = buf[pl.ds(off, rows), :]
            taps.append((moved.at[used], 0))
            used += 1
        else:
            taps.append((buf, off))
    return taps


def _moved_copies(offsets):
    return sum(1 for off in offsets if off % SUBLANES)


def _taps_sum(taps, w_ref, init, r0, cols):
    for k, (src, off) in enumerate(taps):
        init = init + w_ref[k:k + 1, cols] * src[pl.ds(off + r0, init.shape[0]), cols]
    return init


def _ffn_bwd(dact, up, w, b, dims, *, name, ts=128):
    t, f, kw = up.shape[0], dims.d_ff, dims.ffn_conv_width
    nst = dims.seq // ts
    fwd_offsets = [FFN_HALO - (kw - 1) + k for k in range(kw)]
    bwd_offsets = [(kw - 1) - k for k in range(kw)]
    dact_halo = 2 * FFN_HALO

    def body(d_ref, dn_ref, up_ref, hp_ref, hn_ref, w_ref, b_ref, o_ref, dw_ref, db_ref, buf, moved, dbuf, dmoved):
        i = pl.program_id(1)
        first = jnp.logical_and(pl.program_id(0) == 0, i == 0)
        more = i < nst - 1
        buf[0:FFN_HALO, :] = jnp.where(i > 0, hp_ref[...], 0.0)
        buf[FFN_HALO:FFN_HALO + ts, :] = up_ref[...]
        buf[FFN_HALO + ts:, :] = hn_ref[...]
        taps = _tap_sources(buf, moved, fwd_offsets, ts + FFN_HALO)

        def du_chunk(r0, rows, c0, cw, d):
            vcols, gcols = slice(c0, c0 + cw), slice(f + c0, f + c0 + cw)
            uv = _taps_sum(taps, w_ref, jnp.broadcast_to(b_ref[:, vcols], (rows, cw)), r0, vcols)
            ug = _taps_sum(taps, w_ref, jnp.broadcast_to(b_ref[:, gcols], (rows, cw)), r0, gcols)
            sg = _sigmoid(ug)
            dbuf[r0:r0 + rows, vcols] = d * ug * sg
            dbuf[r0:r0 + rows, gcols] = d * uv * sg * (1.0 + ug * (1.0 - sg))

        for r0, c0, cw in _ffn_chunks(ts, f):
            du_chunk(r0, FFN_ROWS, c0, cw, d_ref[r0:r0 + FFN_ROWS, c0:c0 + cw].astype(F32))
        for _, c0, cw in _ffn_chunks(FFN_ROWS, f):
            d_next = dn_ref[:, c0:c0 + cw].astype(F32)[0:FFN_HALO]
            du_chunk(ts, FFN_HALO, c0, cw, jnp.where(more, d_next, 0.0))

        @pl.when(first)
        def _():
            dw_ref[...] = jnp.zeros_like(dw_ref)
            db_ref[...] = jnp.zeros_like(db_ref)

        du = dbuf[0:ts, :]
        db_ref[...] += jnp.sum(du, axis=0, keepdims=True)
        for k, (src, off) in enumerate(taps):
            dw_ref[k:k + 1, :] += jnp.sum(du * src[pl.ds(off, ts), :], axis=0, keepdims=True)

        dtaps = _tap_sources(dbuf, dmoved, bwd_offsets, ts)
        for r0, c0, cw in _ffn_chunks(ts, 2 * f):
            cols = slice(c0, c0 + cw)
            o_ref[r0:r0 + FFN_ROWS, cols] = _taps_sum(dtaps, w_ref, jnp.zeros((FFN_ROWS, cw), F32), r0, cols).astype(BF16)

    up_cur, up_prev = _seq_specs(dims, ts, 2 * f, FFN_HALO, 0)
    _, up_next = _seq_specs(dims, ts, 2 * f, FFN_HALO, 0, nxt=True)
    d_cur, d_next = _seq_specs(dims, ts, f, dact_halo, 0, nxt=True)
    full = lambda rows: pl.BlockSpec((rows, 2 * f), lambda b_, i: (0, 0))
    return pl.pallas_call(
        body, name=name, grid=(dims.batch_local, nst),
        in_specs=[d_cur, d_next, up_cur, up_prev, up_next, full(FFN_HALO), full(1)],
        out_specs=[pl.BlockSpec((ts, 2 * f), lambda b_, i: (b_ * nst + i, 0)), full(FFN_HALO), full(1)],
        out_shape=[jax.ShapeDtypeStruct((t, 2 * f), BF16), jax.ShapeDtypeStruct((FFN_HALO, 2 * f), F32),
                   jax.ShapeDtypeStruct((1, 2 * f), F32)],
        scratch_shapes=[pltpu.VMEM((ts + 2 * FFN_HALO, 2 * f), F32),
                        pltpu.VMEM((_moved_copies(fwd_offsets), ts + FFN_HALO, 2 * f), F32),
                        pltpu.VMEM((ts + FFN_HALO, 2 * f), F32),
                        pltpu.VMEM((_moved_copies(bwd_offsets), ts, 2 * f), F32)],
        compiler_params=_params("arbitrary", "arbitrary"),
    )(dact, dact, up, up, up, w, b)


def _ffn_act_fwd(up, w, b, dims, *, name, ts=128):
    t, f, kw = up.shape[0], dims.d_ff, dims.ffn_conv_width
    offsets = [FFN_HALO - (kw - 1) + k for k in range(kw)]

    def body(up_ref, h_ref, w_ref, b_ref, o_ref, buf, moved):
        buf[FFN_HALO:, :] = up_ref[...]
        buf[0:FFN_HALO, :] = jnp.where(pl.program_id(1) > 0, h_ref[...], 0.0)
        taps = _tap_sources(buf, moved, offsets, ts)
        for r0, c0, cw in _ffn_chunks(ts, f):
            vcols, gcols = slice(c0, c0 + cw), slice(f + c0, f + c0 + cw)
            uv = _taps_sum(taps, w_ref, jnp.broadcast_to(b_ref[:, vcols], (FFN_ROWS, cw)), r0, vcols)
            ug = _taps_sum(taps, w_ref, jnp.broadcast_to(b_ref[:, gcols], (FFN_ROWS, cw)), r0, gcols)
            o_ref[r0:r0 + FFN_ROWS, vcols] = (ug * _sigmoid(ug) * uv).astype(BF16)

    full = lambda rows: pl.BlockSpec((rows, 2 * f), lambda b_, i: (0, 0))
    return pl.pallas_call(
        body, name=name, grid=(dims.batch_local, dims.seq // ts),
        in_specs=[*_seq_specs(dims, ts, 2 * f, FFN_HALO, 0), full(FFN_HALO), full(1)],
        out_specs=pl.BlockSpec((ts, f), lambda b_, i: (b_ * (dims.seq // ts) + i, 0)),
        out_shape=jax.ShapeDtypeStruct((t, f), BF16),
        scratch_shapes=[pltpu.VMEM((FFN_HALO + ts, 2 * f), F32), pltpu.VMEM((_moved_copies(offsets), ts, 2 * f), F32)],
        compiler_params=_params("parallel", "parallel"),
    )(up, up, w, b)


def _dot_nt(a, b):
    return lax.dot_general(a, b, (((1,), (1,)), ((), ())), preferred_element_type=F32)


def _dot_tn(a, b):
    return lax.dot_general(a, b, (((0,), (0,)), ((), ())), preferred_element_type=F32)


LANES = 128
MASK_BIAS = 1e30
RESIDUE_DILATIONS = tuple(d for d in DILATIONS if d > 1)


def _rows_to_residues(value, out_ref, scr, d):
    rows, width = value.shape
    for c in range(width // LANES):
        cols = slice(LANES * c, LANES * (c + 1))
        scr[c] = value[:, cols]
        for r in range(d):
            out_ref[r, :, cols] = scr[c, pl.ds(r, rows // d, stride=d), :].astype(out_ref.dtype)


def _residues_to_rows(in_ref, scr, d):
    _, n, width = in_ref.shape
    slabs = []
    for c in range(width // LANES):
        cols = slice(LANES * c, LANES * (c + 1))
        for r in range(d):
            scr[c, pl.ds(r, n, stride=d), :] = in_ref[r, :, cols].astype(F32)
        slabs.append(scr[c])
    return slabs[0] if len(slabs) == 1 else jnp.concatenate(slabs, axis=1)


def _residue_shape(dims, d, width, dtype):
    return jax.ShapeDtypeStruct((dims.batch_local, d, dims.seq // d, width), dtype)


def _residue_spec(dims, d, tr, width):
    tiles = dims.seq // tr
    return pl.BlockSpec((None, d, tr // d, width), lambda i: (i // tiles, 0, i % tiles, 0))


def _head_sum_matrix(dims):
    a = dims.n_heads * dims.head_dim
    head = jnp.arange(a, dtype=jnp.int32) // dims.head_dim
    return (head[:, None] == jnp.arange(LANES, dtype=jnp.int32)[None, :]).astype(BF16)


def _two_pass_dot(v, m):
    hi = v.astype(BF16)
    lo = (v - hi.astype(F32)).astype(BF16)
    return jnp.dot(hi, m, preferred_element_type=F32) + jnp.dot(lo, m, preferred_element_type=F32)


def _residue_permutations(tr):
    out = []
    for d in RESIDUE_DILATIONS:
        dst = jnp.arange(tr, dtype=jnp.int32)
        src = d * (dst % (tr // d)) + dst // (tr // d)
        out.append((src[:, None] == jnp.arange(tr, dtype=jnp.int32)[None, :]).astype(BF16))
    return out


def _bf16_rows_to_residues(value, out_ref, perm_ref, d):
    n = value.shape[0] // d
    moved = jnp.dot(perm_ref[...], value, preferred_element_type=F32).astype(out_ref.dtype)
    for r in range(d):
        out_ref[r] = moved[r * n:(r + 1) * n]


def _bf16_residues_to_rows(in_ref, back_ref):
    d = in_ref.shape[0]
    stacked = jnp.concatenate([in_ref[r] for r in range(d)], axis=0)
    return jnp.dot(back_ref[...], stacked, preferred_element_type=F32)


def _qkv_layouts_fwd(z, gq, gk, head_ones, dims, *, name, tr=256):
    t = z.shape[0]
    a = dims.n_heads * dims.head_dim
    q_scale = dims.head_dim ** -0.5
    nres = len(RESIDUE_DILATIONS)

    def body(q_ref, k_ref, v_ref, gq_ref, gk_ref, sum_ref, spread_ref, *rest):
        perm_refs, outs = rest[:nres], rest[nres:]
        qv, kv = q_ref[...].astype(F32), k_ref[...].astype(F32)
        mean = lambda val: _two_pass_dot(_two_pass_dot(val, sum_ref[...]), spread_ref[...]) * (1.0 / dims.head_dim)
        rq = lax.rsqrt(mean(qv * qv) + RMS_EPS)
        rk = lax.rsqrt(mean(kv * kv) + RMS_EPS)
        values = ((qv * rq * gq_ref[...] * q_scale).astype(BF16), (kv * rk * gk_ref[...]).astype(BF16), v_ref[...])
        for j, val in enumerate(values):
            outs[j][...] = val
            for g, d in enumerate(RESIDUE_DILATIONS):
                _bf16_rows_to_residues(val, outs[3 * (g + 1) + j], perm_refs[g], d)

    out_specs = [_row_spec(tr, a)] * 3
    out_shape = [jax.ShapeDtypeStruct((t, a), BF16)] * 3
    for d in RESIDUE_DILATIONS:
        out_specs += [_residue_spec(dims, d, tr, a)] * 3
        out_shape += [_residue_shape(dims, d, a, BF16)] * 3
    outs = pl.pallas_call(
        body, name=name, grid=(t // tr,),
        in_specs=[_row_spec(tr, a, 2), _row_spec(tr, a, 3), _row_spec(tr, a, 4), _vec_spec(a), _vec_spec(a),
                  pl.BlockSpec((a, LANES), lambda i: (0, 0)), pl.BlockSpec((LANES, a), lambda i: (0, 0))]
        + [pl.BlockSpec((tr, tr), lambda i: (0, 0))] * nres,
        out_specs=out_specs, out_shape=out_shape,
        compiler_params=_params("parallel"),
    )(z, z, z, gq, gk, *head_ones, *_residue_permutations(tr))
    return {d: tuple(outs[3 * g:3 * g + 3]) for g, d in enumerate((1,) + RESIDUE_DILATIONS)}


ATTN_RESIDUES_PER_STEP = 4
ATTN_RESIDUES_PER_STEP_WINDOWED = 2


def _attn_groups(dims, dil):
    return (dims.batch_local, dil) if dil > 1 else (1, dims.batch_local)


def _attn_array(x, dims, dil):
    return x if dil > 1 else x.reshape(1, dims.batch_local, dims.seq, x.shape[-1])


def _attn_residues(dims, dil):
    one_block = dims.seq // dil == ATTN_BLOCK
    return math.gcd(_attn_groups(dims, dil)[1], ATTN_RESIDUES_PER_STEP if one_block else ATTN_RESIDUES_PER_STEP_WINDOWED)


def _per_residue(body, rs):
    if rs == 1:
        return body

    def stepped(*refs):
        for r in range(rs):
            body(*[ref.at[r] for ref in refs])

    return stepped


def _attn_specs(dims, dil, width):
    blk = ATTN_BLOCK
    nb = dims.seq // dil // blk
    rs = _attn_residues(dims, dil)
    lead, groups = _attn_groups(dims, dil)
    if rs > 1 and nb == 1:
        grid = (lead, groups // rs)
        at = lambda f: pl.BlockSpec((None, rs, blk, width), lambda b, r: (b, r, 0, 0))
    elif rs > 1:
        grid = (lead, groups // rs, nb)
        at = lambda f: pl.BlockSpec((None, rs, blk, width), lambda b, r, i: (b, r, f(i), 0))
    else:
        grid = (lead, groups, nb)
        at = lambda f: pl.BlockSpec((None, None, blk, width), lambda b, r, i: (b, r, f(i), 0))
    return grid, at(lambda i: i), at(lambda i: jnp.maximum(i - 1, 0)), at(lambda i: jnp.minimum(i + 1, nb - 1))


def _head_slopes(n_heads):
    h = lax.broadcasted_iota(jnp.int32, (n_heads, 1, 1), 0).astype(F32)
    return jnp.exp((h + 1.0) * (-8.0 / n_heads * math.log(2.0)))


def _pair_masks(hd):
    low = lax.broadcasted_iota(jnp.int32, (1, 2 * hd), 1) < hd
    return low, jnp.logical_not(low)


def _attn_fwd(q, k, v, dims, dil, *, name):
    a = dims.n_heads * dims.head_dim
    heads, hd, blk = dims.n_heads, dims.head_dim, ATTN_BLOCK
    assert 2 * hd == LANES and heads % 2 == 0 and heads <= LANES
    nb = dims.seq // dil // blk
    has_prev = nb > 1
    nkeys = 2 * blk if has_prev else blk
    grid, cur, prev, _ = _attn_specs(dims, dil, a)
    _, cur_stat, _, _ = _attn_specs(dims, dil, LANES)

    def body(*refs):
        if has_prev:
            q_ref, kc_ref, vc_ref, kp_ref, vp_ref, o_ref, lse_ref, s_scr, p_scr, k_st, v_st = refs
            k_st[0:blk, :], k_st[blk:, :] = kp_ref[...], kc_ref[...]
            v_st[0:blk, :], v_st[blk:, :] = vp_ref[...], vc_ref[...]
        else:
            q_ref, k_st, v_st, o_ref, lse_ref, s_scr, p_scr = refs
        low, high = _pair_masks(hd)

        for hp in range(heads // 2):
            sl = slice(LANES * hp, LANES * (hp + 1))
            q2 = q_ref[:, sl]
            kcat = k_st[:, sl]
            s_scr[2 * hp] = _dot_nt(jnp.where(low, q2, jnp.zeros_like(q2)), kcat)
            s_scr[2 * hp + 1] = _dot_nt(jnp.where(high, q2, jnp.zeros_like(q2)), kcat)

        iq = lax.broadcasted_iota(jnp.int32, (blk, nkeys), 0)
        jk = lax.broadcasted_iota(jnp.int32, (blk, nkeys), 1)
        if has_prev:
            steps = iq + blk - jk
            valid = (steps >= 0) & (steps <= blk) & ((jk >= blk) | (pl.program_id(len(grid) - 1) > 0))
        else:
            steps = iq - jk
            valid = steps >= 0
        bias = jnp.where(valid, steps.astype(F32) * (-float(dil)), -MASK_BIAS)
        s = s_scr[...] + _head_slopes(heads) * bias[None]
        m = jnp.max(s, axis=-1, keepdims=True)
        p = jnp.exp(s - m)
        l = jnp.sum(p, axis=-1, keepdims=True)
        p_scr[...] = p.astype(BF16)
        inv = 1.0 / l
        lse = m + jnp.log(l)

        lane = lax.broadcasted_iota(jnp.int32, (blk, LANES), 1)
        stat = jnp.zeros((blk, LANES), F32)
        for hp in range(heads // 2):
            sl = slice(LANES * hp, LANES * (hp + 1))
            vcat = v_st[:, sl]
            pv_a = jnp.dot(p_scr[2 * hp], vcat, preferred_element_type=F32) * inv[2 * hp]
            pv_b = jnp.dot(p_scr[2 * hp + 1], vcat, preferred_element_type=F32) * inv[2 * hp + 1]
            o_ref[:, sl] = jnp.where(low, pv_a, pv_b).astype(BF16)
            stat = jnp.where(lane == 2 * hp, lse[2 * hp], stat)
            stat = jnp.where(lane == 2 * hp + 1, lse[2 * hp + 1], stat)
        lse_ref[...] = stat

    q4, k4, v4 = (_attn_array(x, dims, dil) for x in (q, k, v))
    rs = _attn_residues(dims, dil)
    per_step = lambda shape: shape if rs == 1 else (rs,) + shape
    o, lse = pl.pallas_call(
        _per_residue(body, rs), name=name, grid=grid,
        in_specs=[cur, cur, cur] + ([prev, prev] if has_prev else []),
        out_specs=[cur, cur_stat],
        out_shape=[jax.ShapeDtypeStruct(q4.shape, BF16), jax.ShapeDtypeStruct(q4.shape[:-1] + (LANES,), F32)],
        scratch_shapes=[pltpu.VMEM(per_step((heads, blk, nkeys)), F32), pltpu.VMEM(per_step((heads, blk, nkeys)), BF16)]
        + ([pltpu.VMEM(per_step((nkeys, a)), BF16)] * 2 if has_prev else []),
        compiler_params=_params(*["parallel"] * len(grid)),
    )(q4, k4, v4, *([k4, v4] if has_prev else []))
    return o.reshape(q.shape), lse.reshape(q.shape[:-1] + (LANES,))


def _attn_combine(groups, head_spread, dims, *, name, tr=256):
    t = dims.tokens
    a = dims.n_heads * dims.head_dim
    dils = tuple(groups)

    nres = len(RESIDUE_DILATIONS)

    def body(*refs):
        ins = refs[:2 * len(dils)]
        x_ref = refs[2 * len(dils)]
        back_refs = dict(zip(RESIDUE_DILATIONS, refs[2 * len(dils) + 1:2 * len(dils) + 1 + nres]))
        o_ref = refs[2 * len(dils) + 1 + nres]
        lse_refs = refs[2 * len(dils) + 2 + nres:-1]
        scr_stat = refs[-1]
        outs, stats = [], []
        for g, d in enumerate(dils):
            if d == 1:
                outs.append(ins[2 * g][...].astype(F32))
                stats.append(ins[2 * g + 1][...])
            else:
                outs.append(_bf16_residues_to_rows(ins[2 * g], back_refs[d]))
                stats.append(_residues_to_rows(ins[2 * g + 1], scr_stat, d))
        top = functools.reduce(jnp.maximum, stats)
        weights = [jnp.exp(s - top) for s in stats]
        total = functools.reduce(jnp.add, weights)
        joint = top + jnp.log(total)
        inv = 1.0 / total
        acc = None
        for w, o in zip(weights, outs):
            term = _two_pass_dot(w * inv, x_ref[...]) * o
            acc = term if acc is None else acc + term
        o_ref[...] = acc.astype(BF16)
        for g, d in enumerate(dils):
            if d == 1:
                lse_refs[g][...] = joint
            else:
                _rows_to_residues(joint, lse_refs[g], scr_stat, d)

    in_specs, args, lse_specs, lse_shapes = [], [], [], []
    for d in dils:
        if d == 1:
            in_specs += [_row_spec(tr, a), _row_spec(tr, LANES)]
            lse_specs.append(_row_spec(tr, LANES))
            lse_shapes.append(jax.ShapeDtypeStruct((t, LANES), F32))
        else:
            in_specs += [_residue_spec(dims, d, tr, a), _residue_spec(dims, d, tr, LANES)]
            lse_specs.append(_residue_spec(dims, d, tr, LANES))
            lse_shapes.append(_residue_shape(dims, d, LANES, F32))
        args += list(groups[d])
    outs = pl.pallas_call(
        body, name=name, grid=(t // tr,),
        in_specs=in_specs + [pl.BlockSpec((LANES, a), lambda i: (0, 0))] + [pl.BlockSpec((tr, tr), lambda i: (0, 0))] * nres,
        out_specs=[_row_spec(tr, a)] + lse_specs,
        out_shape=[jax.ShapeDtypeStruct((t, a), BF16)] + lse_shapes,
        scratch_shapes=[pltpu.VMEM((1, tr, LANES), F32)],
        compiler_params=_params("parallel"),
    )(*args, head_spread, *[jnp.transpose(p) for p in _residue_permutations(tr)])
    return outs[0], dict(zip(dils, outs[1:]))


def _attn_bwd_prep(do, o, head_sum, dims, *, name, tr=256):
    t, a = o.shape
    nres = len(RESIDUE_DILATIONS)

    def body(do_ref, o_ref, e_ref, *rest):
        perm_refs, outs, scr_stat = rest[:nres], rest[nres:-1], rest[-1]
        delta = _two_pass_dot(do_ref[...].astype(F32) * o_ref[...].astype(F32), e_ref[...])
        outs[0][...] = delta
        for g, d in enumerate(RESIDUE_DILATIONS):
            _bf16_rows_to_residues(do_ref[...], outs[1 + 2 * g], perm_refs[g], d)
            _rows_to_residues(delta, outs[2 + 2 * g], scr_stat, d)

    out_specs, out_shape = [_row_spec(tr, LANES)], [jax.ShapeDtypeStruct((t, LANES), F32)]
    for d in RESIDUE_DILATIONS:
        out_specs += [_residue_spec(dims, d, tr, a), _residue_spec(dims, d, tr, LANES)]
        out_shape += [_residue_shape(dims, d, a, BF16), _residue_shape(dims, d, LANES, F32)]
    outs = pl.pallas_call(
        body, name=name, grid=(t // tr,),
        in_specs=[_row_spec(tr, a), _row_spec(tr, a), pl.BlockSpec((a, LANES), lambda i: (0, 0))]
        + [pl.BlockSpec((tr, tr), lambda i: (0, 0))] * nres,
        out_specs=out_specs, out_shape=out_shape,
        scratch_shapes=[pltpu.VMEM((1, tr, LANES), F32)],
        compiler_params=_params("parallel"),
    )(do, o, head_sum, *_residue_permutations(tr))
    dos, deltas = {1: do}, {1: outs[0]}
    for g, d in enumerate(RESIDUE_DILATIONS):
        dos[d], deltas[d] = outs[1 + 2 * g], outs[2 + 2 * g]
    return dos, deltas


def _attn_bwd(q, k, v, do, lse, delta, dims, dil, *, name):
    a = dims.n_heads * dims.head_dim
    heads, hd, blk = dims.n_heads, dims.head_dim, ATTN_BLOCK
    nb = dims.seq // dil // blk
    has_next = nb > 1
    nq = 2 * blk if has_next else blk
    grid, cur, _, nxt = _attn_specs(dims, dil, a)
    _, cur_stat, _, nxt_stat = _attn_specs(dims, dil, LANES)

    def body(*refs):
        k_ref, v_ref, q_ref, do_ref, lse_ref, dl_ref = refs[:6]
        if has_next:
            qn_ref, don_ref, lsen_ref, dln_ref = refs[6:10]
            dq_ref, dk_ref, dv_ref, q_st, do_st, s_scr, dp_scr, p_scr, ds_scr, carry = refs[10:]
        else:
            dq_ref, dk_ref, dv_ref, q_st, do_st, s_scr, dp_scr, p_scr, ds_scr = refs[6:]
        j = pl.program_id(len(grid) - 1)
        low, high = _pair_masks(hd)
        q_st[0:blk, :] = q_ref[...]
        do_st[0:blk, :] = do_ref[...]
        if has_next:
            q_st[blk:, :] = qn_ref[...]
            do_st[blk:, :] = don_ref[...]
            lse_all = jnp.concatenate([lse_ref[...], lsen_ref[...]], axis=0)
            dl_all = jnp.concatenate([dl_ref[...], dln_ref[...]], axis=0)
        else:
            lse_all, dl_all = lse_ref[...], dl_ref[...]
        lse_t, dl_t = jnp.transpose(lse_all), jnp.transpose(dl_all)
        lse3 = jnp.stack([lse_t[h:h + 1, :] for h in range(heads)])
        dl3 = jnp.stack([dl_t[h:h + 1, :] for h in range(heads)])

        def halves(x):
            return jnp.where(low, x, jnp.zeros_like(x)), jnp.where(high, x, jnp.zeros_like(x))

        for hp in range(heads // 2):
            sl = slice(LANES * hp, LANES * (hp + 1))
            k2, v2 = k_ref[:, sl], v_ref[:, sl]
            q_a, q_b = halves(q_st[:, sl])
            do_a, do_b = halves(do_st[:, sl])
            s_scr[2 * hp], s_scr[2 * hp + 1] = _dot_nt(k2, q_a), _dot_nt(k2, q_b)
            dp_scr[2 * hp], dp_scr[2 * hp + 1] = _dot_nt(v2, do_a), _dot_nt(v2, do_b)

        jk = lax.broadcasted_iota(jnp.int32, (blk, nq), 0)
        rq = lax.broadcasted_iota(jnp.int32, (blk, nq), 1)
        if has_next:
            iq = jnp.where(rq < blk, rq, rq - blk)
            steps = jnp.where(rq < blk, iq - jk, iq - jk + blk)
            valid = ((rq < blk) & (iq >= jk)) | ((rq >= blk) & (jk >= iq) & (j + 1 < nb))
        else:
            steps, valid = rq - jk, rq >= jk
        bias = jnp.where(valid, steps.astype(F32) * (-float(dil)), -MASK_BIAS)
        p = jnp.exp(s_scr[...] + _head_slopes(heads) * bias[None] - lse3)
        p_scr[...] = p.astype(BF16)
        ds_scr[...] = (p * (dp_scr[...] - dl3)).astype(BF16)

        if has_next:
            @pl.when(j == 0)
            def _():
                carry[...] = jnp.zeros_like(carry)

        for hp in range(heads // 2):
            sl = slice(LANES * hp, LANES * (hp + 1))
            k2 = k_ref[:, sl]
            q_a, q_b = halves(q_st[:, sl])
            do_a, do_b = halves(do_st[:, sl])
            ds_a, ds_b = ds_scr[2 * hp], ds_scr[2 * hp + 1]
            dk_ref[:, sl] = (jnp.dot(ds_a, q_a, preferred_element_type=F32)
                             + jnp.dot(ds_b, q_b, preferred_element_type=F32)).astype(BF16)
            dv_ref[:, sl] = (jnp.dot(p_scr[2 * hp], do_a, preferred_element_type=F32)
                             + jnp.dot(p_scr[2 * hp + 1], do_b, preferred_element_type=F32)).astype(BF16)
            dq2 = jnp.where(low, _dot_tn(ds_a, k2), _dot_tn(ds_b, k2))
            if has_next:
                dq_ref[:, sl] = (carry[:, sl] + dq2[:blk]).astype(BF16)
                carry[:, sl] = dq2[blk:]
            else:
                dq_ref[:, sl] = dq2.astype(BF16)

    args, in_specs = [_attn_array(x, dims, dil) for x in (k, v, q, do, lse, delta)], [cur] * 4 + [cur_stat] * 2
    if has_next:
        args += [args[2], args[3], args[4], args[5]]
        in_specs += [nxt] * 2 + [nxt_stat] * 2
    shape = jax.ShapeDtypeStruct(args[2].shape, BF16)
    rs = _attn_residues(dims, dil)
    per_step = lambda dims_: dims_ if rs == 1 else (rs,) + dims_
    scratch = ([pltpu.VMEM(per_step((nq, a)), BF16)] * 2 + [pltpu.VMEM(per_step((heads, blk, nq)), F32)] * 2
               + [pltpu.VMEM(per_step((heads, blk, nq)), BF16)] * 2)
    if has_next:
        scratch.append(pltpu.VMEM(per_step((blk, a)), F32))
    grads = pl.pallas_call(
        _per_residue(body, rs), name=name, grid=grid, in_specs=in_specs, out_specs=[cur] * 3, out_shape=[shape] * 3,
        scratch_shapes=scratch,
        compiler_params=_params(*["parallel"] * (len(grid) - 1), "arbitrary"),
    )(*args)
    return tuple(g.reshape(q.shape) for g in grads)


def _qkv_layouts_bwd(z, grads, gq, gk, head_ones, dims, *, name, tr=256):
    t = z.shape[0]
    a = dims.n_heads * dims.head_dim
    q_scale = dims.head_dim ** -0.5
    dils = tuple(grads)
    nres = len(RESIDUE_DILATIONS)

    def body(q_ref, k_ref, *rest):
        d_refs = rest[:3 * len(dils)]
        gq_ref, gk_ref, sum_ref, spread_ref = rest[3 * len(dils):3 * len(dils) + 4]
        back_refs = dict(zip(RESIDUE_DILATIONS, rest[3 * len(dils) + 4:3 * len(dils) + 4 + nres]))
        dz_ref, dgq_ref, dgk_ref = rest[3 * len(dils) + 4 + nres:]
        first = pl.program_id(0) == 0
        mean = lambda val: _two_pass_dot(_two_pass_dot(val, sum_ref[...]), spread_ref[...]) * (1.0 / dims.head_dim)

        def total(j):
            acc = None
            for g, d in enumerate(dils):
                ref = d_refs[3 * g + j]
                part = ref[...].astype(F32) if d == 1 else _bf16_residues_to_rows(ref, back_refs[d])
                acc = part if acc is None else acc + part
            return acc

        def norm_bwd(x_ref, dy, g_ref, scale, col, dg_ref):
            xv = x_ref[...].astype(F32)
            dy = dy * scale
            r = lax.rsqrt(mean(xv * xv) + RMS_EPS)
            gy = dy * g_ref[...]
            dx = r * gy - xv * (r * r * r) * mean(xv * gy)
            dz_ref[:, col * a:(col + 1) * a] = dx.astype(BF16)
            _accumulate(dg_ref, jnp.sum(dy * xv * r, axis=0, keepdims=True), first)

        norm_bwd(q_ref, total(0), gq_ref, q_scale, 0, dgq_ref)
        norm_bwd(k_ref, total(1), gk_ref, 1.0, 1, dgk_ref)
        dz_ref[:, 2 * a:3 * a] = total(2).astype(BF16)

    in_specs, args = [_row_spec(tr, a, 2), _row_spec(tr, a, 3)], [z, z]
    for d in dils:
        in_specs += [_row_spec(tr, a) if d == 1 else _residue_spec(dims, d, tr, a)] * 3
        args += list(grads[d])
    in_specs += [_vec_spec(a), _vec_spec(a), pl.BlockSpec((a, LANES), lambda i: (0, 0)),
                 pl.BlockSpec((LANES, a), lambda i: (0, 0))] + [pl.BlockSpec((tr, tr), lambda i: (0, 0))] * nres
    return pl.pallas_call(
        body, name=name, grid=(t // tr,), in_specs=in_specs,
        out_specs=[_row_spec(tr, 3 * a), _vec_spec(a), _vec_spec(a)],
        out_shape=[jax.ShapeDtypeStruct((t, 3 * a), BF16)] + [jax.ShapeDtypeStruct((1, a), F32)] * 2,
        compiler_params=_params("arbitrary"),
    )(*args, gq, gk, *head_ones, *[jnp.transpose(p) for p in _residue_permutations(tr)])


def _mix_fwd(ya, yb, z, gate_b, dims, *, name, tr=512):
    t, d = ya.shape
    tr = _pick(t, tr, 8)
    first_gate_col = z.shape[1] // d - 2

    def body(ya_ref, yb_ref, ga_ref, gb_ref, ba_ref, bb_ref, o_ref):
        g_a = _sigmoid(ga_ref[...].astype(F32) + ba_ref[...])
        g_b = _sigmoid(gb_ref[...].astype(F32) + bb_ref[...])
        o_ref[...] = (g_a * ya_ref[...] + g_b * yb_ref[...]).astype(BF16)

    return pl.pallas_call(
        body, name=name, grid=(t // tr,),
        in_specs=[_row_spec(tr, d), _row_spec(tr, d), _row_spec(tr, d, first_gate_col),
                  _row_spec(tr, d, first_gate_col + 1), _vec_spec(d, 0), _vec_spec(d, 1)],
        out_specs=_row_spec(tr, d), out_shape=jax.ShapeDtypeStruct((t, d), BF16),
        compiler_params=_params("parallel"),
    )(ya, yb, z, z, gate_b, gate_b)


def _mix_bwd(dx, w, ya, yb, z, gate_b, dims, *, name, after=None, tr=512):
    t, d = ya.shape
    tr = _pick(t, tr, 8)
    first_gate_col = z.shape[1] // d - 2

    def body(dx_ref, w_ref, ya_ref, yb_ref, ga_ref, gb_ref, ba_ref, bb_ref, dya_ref, dyb_ref, dz_ref, db_ref):
        dm = _dot_nt(dx_ref[...], w_ref[...])
        g_a = _sigmoid(ga_ref[...].astype(F32) + ba_ref[...])
        g_b = _sigmoid(gb_ref[...].astype(F32) + bb_ref[...])
        dya_ref[...] = (dm * g_a).astype(BF16)
        dyb_ref[...] = (dm * g_b).astype(BF16)
        dl_a = dm * ya_ref[...] * g_a * (1.0 - g_a)
        dl_b = dm * yb_ref[...] * g_b * (1.0 - g_b)
        dz_ref[:, 0:d] = dl_a.astype(BF16)
        dz_ref[:, d:2 * d] = dl_b.astype(BF16)
        first = pl.program_id(0) == 0
        sums = jnp.concatenate([jnp.sum(dl_a, axis=0, keepdims=True), jnp.sum(dl_b, axis=0, keepdims=True)], axis=1)
        _accumulate(db_ref, sums, first)

    body, more_specs, more_args = _ordered(body, 8, after)
    return pl.pallas_call(
        body, name=name, grid=(t // tr,),
        in_specs=[_row_spec(tr, d), pl.BlockSpec((None, d, d), lambda i: (0, 0, 0)), _row_spec(tr, d), _row_spec(tr, d),
                  _row_spec(tr, d, first_gate_col), _row_spec(tr, d, first_gate_col + 1), _vec_spec(d, 0),
                  _vec_spec(d, 1)] + more_specs,
        out_specs=[_row_spec(tr, d), _row_spec(tr, d), _row_spec(tr, 2 * d), _vec_spec(2 * d)],
        out_shape=[jax.ShapeDtypeStruct((t, d), BF16)] * 2 + [jax.ShapeDtypeStruct((t, 2 * d), BF16),
                                                              jax.ShapeDtypeStruct((1, 2 * d), F32)],
        compiler_params=_params("arbitrary"),
    )(dx, w, ya, yb, z, z, gate_b, gate_b, *more_args)


def _adamw(w, grads, m, v, *, name, tr=256):
    r, c = w.shape
    tr = _pick(r, tr, 8)
    ng = len(grads)
    c1 = 1.0 - ADAM_B1 ** ADAM_STEP
    c2 = 1.0 - ADAM_B2 ** ADAM_STEP

    def body(*refs):
        w_ref, g_refs, m_ref, v_ref = refs[0], refs[1:1 + ng], refs[1 + ng], refs[2 + ng]
        g_out, d_out, m_out, v_out = refs[3 + ng:]
        g = g_refs[0][...]
        for extra in g_refs[1:]:
            g = g + extra[...]
        m_new = ADAM_B1 * m_ref[...] + (1.0 - ADAM_B1) * g
        v_new = ADAM_B2 * v_ref[...] + (1.0 - ADAM_B2) * (g * g)
        g_out[...] = g
        m_out[...] = m_new
        v_out[...] = v_new
        d_out[...] = -ADAM_LR * ((m_new / c1) / (jnp.sqrt(v_new / c2) + ADAM_EPS) + ADAM_WD * w_ref[...])

    spec = pl.BlockSpec((tr, c), lambda i: (i, 0))
    return pl.pallas_call(
        body, name=name, grid=(r // tr,),
        in_specs=[spec] * (3 + ng), out_specs=[spec] * 4, out_shape=[jax.ShapeDtypeStruct((r, c), F32)] * 4,
        compiler_params=_params("parallel"),
    )(w, *grads, m, v)


CHIP_PEERS = ((1, 0), (0, 1), (1, 1))


def _place():
    return lax.axis_index("x"), lax.axis_index("y"), lax.axis_index("c")


HBM = pl.BlockSpec(memory_space=pltpu.HBM)
SEM = pl.BlockSpec(memory_space=pltpu.SEMAPHORE)
IN_FLIGHT = pltpu.SideEffectType.DATAFLOW_SIDE_EFFECTING


def _in_hbm(a):
    return pltpu.with_memory_space_constraint(a, pltpu.HBM)


def _cast_to_lands(shards, dtypes, *, name, after=None):
    n = len(shards)

    def body(*refs):
        ins, outs, bufs, sems = refs[:n], refs[n:2 * n], refs[2 * n:3 * n], refs[3 * n]
        x, y, _ = _place()
        copies = []
        for a in range(n):
            bufs[a][...] = ins[a][...].astype(dtypes[a])
            cp = pltpu.make_async_copy(bufs[a], outs[a].at[2 * x + y], sems.at[a])
            cp.start()
            copies.append(cp)
        for cp in copies:
            cp.wait()

    body, more_specs, more_args = _ordered(body, n, after)
    return pl.pallas_call(
        body, name=name, in_specs=[pl.BlockSpec(memory_space=pltpu.VMEM)] * n + more_specs, out_specs=[ANY] * n,
        out_shape=[jax.ShapeDtypeStruct((N_CHIPS,) + s.shape, dt) for s, dt in zip(shards, dtypes)],
        scratch_shapes=[pltpu.VMEM(s.shape, dt) for s, dt in zip(shards, dtypes)] + [pltpu.SemaphoreType.DMA((n,))],
        compiler_params=pltpu.CompilerParams(vmem_limit_bytes=V7X_VMEM_LIMIT_BYTES),
    )(*shards, *more_args)


def _chip_copy(src, dst, send, recv, flip, place):
    x, y, c = place
    return pltpu.make_async_remote_copy(src_ref=src, dst_ref=dst, send_sem=send, recv_sem=recv,
                                        device_id=(x ^ flip[0], y ^ flip[1], c), device_id_type=MESH)


def _my_part(land, place, halved):
    block = land.at[2 * place[0] + place[1]]
    if not halved:
        return block
    rows = land.shape[1] // 2
    return block.at[pl.ds(pl.multiple_of(place[2] * rows, rows), rows)]


def _gather_start(lands, after, *, name, halved=()):
    n = len(lands)

    def body(*refs):
        ins, send, recv, token = refs[:n], refs[n + 1], refs[n + 2], refs[-1]
        place = _place()
        for a in range(n):
            part = _my_part(ins[a], place, a in halved)
            for p, flip in enumerate(CHIP_PEERS):
                k = 3 * a + p
                _chip_copy(part, part, send.at[k], recv.at[k], flip, place).start()
        token[...] = jnp.zeros_like(token)

    outs = pl.pallas_call(
        body, name=name, in_specs=[HBM] * n + [ANY],
        out_specs=(SEM, SEM, *[HBM] * n, pl.BlockSpec(memory_space=pltpu.VMEM)),
        out_shape=(pltpu.SemaphoreType.DMA((3 * n,)), pltpu.SemaphoreType.DMA((3 * n,)),
                   *[pltpu.HBM(l.shape, l.dtype) for l in lands], jax.ShapeDtypeStruct((8, 128), F32)),
        input_output_aliases={a: 2 + a for a in range(n)},
        compiler_params=pltpu.CompilerParams(has_side_effects=IN_FLIGHT),
    )(*[_in_hbm(l) for l in lands], after)
    return outs[0], outs[1], list(outs[2:2 + n]), outs[-1]


def _gather_wait(send, recv, lands, after, *, name, halved=()):
    n = len(lands)

    def body(*refs):
        ins, send_ref, recv_ref = refs[:n], refs[n], refs[n + 1]
        place = _place()
        for a in range(n):
            part = _my_part(ins[a], place, a in halved)
            for p, flip in enumerate(CHIP_PEERS):
                k = 3 * a + p
                cp = _chip_copy(part, part, send_ref.at[k], recv_ref.at[k], flip, place)
                cp.wait_send()
                cp.wait_recv()

    after = list(after) if isinstance(after, (list, tuple)) else [after]
    return pl.pallas_call(
        body, name=name, in_specs=[HBM] * n + [SEM, SEM] + [ANY] * len(after), out_specs=[HBM] * n,
        out_shape=[pltpu.HBM(l.shape, l.dtype) for l in lands],
        input_output_aliases={a: a for a in range(n)},
        compiler_params=pltpu.CompilerParams(has_side_effects=IN_FLIGHT),
    )(*lands, send, recv, *after)


def _forward_to_sibling(land, *, name):
    rows = land.shape[1] // 2

    def body(land_ref, out_ref, send, recv):
        x, y, c = _place()
        copies = []
        for p, (fx, fy) in enumerate(CHIP_PEERS):
            chip = 2 * (x ^ fx) + (y ^ fy)
            mine = pl.ds(pl.multiple_of(c * rows, rows), rows)
            theirs = pl.ds(pl.multiple_of((1 - c) * rows, rows), rows)
            out = pltpu.make_async_remote_copy(
                src_ref=land_ref.at[chip].at[mine], dst_ref=out_ref.at[chip].at[mine], send_sem=send.at[p],
                recv_sem=recv.at[p], device_id=(x, y, 1 - c), device_id_type=MESH)
            out.start()
            copies.append((out, pltpu.make_async_remote_copy(
                src_ref=land_ref.at[chip].at[theirs], dst_ref=out_ref.at[chip].at[theirs], send_sem=send.at[p],
                recv_sem=recv.at[p], device_id=(x, y, 1 - c), device_id_type=MESH)))
        for out, arriving in copies:
            out.wait_send()
            arriving.wait_recv()

    return pl.pallas_call(
        body, name=name, in_specs=[ANY], out_specs=ANY, out_shape=jax.ShapeDtypeStruct(land.shape, land.dtype),
        input_output_aliases={0: 0},
        scratch_shapes=[pltpu.SemaphoreType.DMA((3,)), pltpu.SemaphoreType.DMA((3,))],
    )(land)


def _scatter_start(grad, *, name):
    def body(g_ref, land_ref, send, recv, g_thru, land_thru, token):
        place = _place()
        for p, flip in enumerate(CHIP_PEERS):
            peer_chip = 2 * (place[0] ^ flip[0]) + (place[1] ^ flip[1])
            _chip_copy(g_ref.at[peer_chip], land_ref.at[p], send.at[p], recv.at[p], flip, place).start()
        token[...] = jnp.zeros_like(token)

    land = lax.empty((3,) + grad.shape[1:], grad.dtype)
    return pl.pallas_call(
        body, name=name, in_specs=[HBM, HBM],
        out_specs=(SEM, SEM, HBM, HBM, pl.BlockSpec(memory_space=pltpu.VMEM)),
        out_shape=(pltpu.SemaphoreType.DMA((3,)), pltpu.SemaphoreType.DMA((3,)), pltpu.HBM(grad.shape, grad.dtype),
                   pltpu.HBM(land.shape, land.dtype), jax.ShapeDtypeStruct((8, 128), F32)),
        input_output_aliases={0: 2, 1: 3},
        compiler_params=pltpu.CompilerParams(has_side_effects=IN_FLIGHT),
    )(_in_hbm(grad), _in_hbm(land))


def _scatter_wait(started, after, *, name):
    n = len(started)

    def body(*refs):
        grads, lands = refs[:n], refs[n:2 * n]
        sends, recvs = refs[2 * n:3 * n], refs[3 * n:4 * n]
        place = _place()
        for a in range(n):
            for p, flip in enumerate(CHIP_PEERS):
                cp = _chip_copy(grads[a].at[0], lands[a].at[p], sends[a].at[p], recvs[a].at[p], flip, place)
                cp.wait_send()
                cp.wait_recv()

    grads, lands = [s[2] for s in started], [s[3] for s in started]
    after = list(after) if isinstance(after, (list, tuple)) else [after]
    outs = pl.pallas_call(
        body, name=name, in_specs=[HBM] * (2 * n) + [SEM] * (2 * n) + [ANY] * len(after), out_specs=[HBM] * (2 * n),
        out_shape=[pltpu.HBM(a.shape, a.dtype) for a in grads + lands],
        input_output_aliases={a: a for a in range(2 * n)},
        compiler_params=pltpu.CompilerParams(has_side_effects=IN_FLIGHT),
    )(*grads, *lands, *[s[0] for s in started], *[s[1] for s in started], *after)
    return list(zip(outs[:n], outs[n:]))


def _sibling_copy(src, dst, send, recv, place):
    x, y, c = place
    return pltpu.make_async_remote_copy(src_ref=src, dst_ref=dst, send_sem=send, recv_sem=recv,
                                        device_id=(x, y, 1 - c), device_id_type=MESH)


def _swap_start(arrays, *, name):
    n = len(arrays)

    def body(*refs):
        ins, lands, send, recv, token = refs[:n], refs[n:2 * n], refs[2 * n], refs[2 * n + 1], refs[-1]
        place = _place()
        for a in range(n):
            _sibling_copy(ins[a], lands[a], send.at[a], recv.at[a], place).start()
        token[...] = jnp.zeros_like(token)

    both = [_in_hbm(a) for a in arrays] + [_in_hbm(lax.empty(a.shape, a.dtype)) for a in arrays]
    outs = pl.pallas_call(
        body, name=name, in_specs=[HBM] * (2 * n),
        out_specs=(SEM, SEM, *[HBM] * (2 * n), pl.BlockSpec(memory_space=pltpu.VMEM)),
        out_shape=(pltpu.SemaphoreType.DMA((n,)), pltpu.SemaphoreType.DMA((n,)),
                   *[pltpu.HBM(a.shape, a.dtype) for a in both], jax.ShapeDtypeStruct((8, 128), F32)),
        input_output_aliases={a: 2 + a for a in range(2 * n)},
        compiler_params=pltpu.CompilerParams(has_side_effects=IN_FLIGHT),
    )(*both)
    return outs[0], outs[1], list(outs[2:2 + n]), list(outs[2 + n:2 + 2 * n]), outs[-1]


def _swap_wait(started, after, *, name):
    send, recv, arrays, lands = started[:4]
    n = len(arrays)

    def body(*refs):
        ins, zones, send_ref, recv_ref = refs[:n], refs[n:2 * n], refs[2 * n], refs[2 * n + 1]
        place = _place()
        for a in range(n):
            cp = _sibling_copy(ins[a], zones[a], send_ref.at[a], recv_ref.at[a], place)
            cp.wait_send()
            cp.wait_recv()

    after = list(after) if isinstance(after, (list, tuple)) else [after]
    outs = pl.pallas_call(
        body, name=name, in_specs=[HBM] * (2 * n) + [SEM, SEM] + [ANY] * len(after), out_specs=[HBM] * (2 * n),
        out_shape=[pltpu.HBM(a.shape, a.dtype) for a in arrays + lands],
        input_output_aliases={a: a for a in range(2 * n)},
        compiler_params=pltpu.CompilerParams(has_side_effects=IN_FLIGHT),
    )(*arrays, *lands, send, recv, *after)
    return list(outs[:n]), list(outs[n:])


def _allreduce_start(packed, *, name):
    n_dev = 8

    def body(src_ref, land_ref, send, recv, src_thru, land_thru, token):
        x, y, c = _place()
        me = 4 * x + 2 * y + c
        for p in range(1, n_dev):
            pltpu.make_async_remote_copy(
                src_ref=src_ref, dst_ref=land_ref.at[me], send_sem=send.at[p - 1], recv_sem=recv.at[p - 1],
                device_id=(x ^ (p >> 2), y ^ ((p >> 1) & 1), c ^ (p & 1)), device_id_type=MESH).start()
        token[...] = jnp.zeros_like(token)

    land = lax.empty((n_dev,) + packed.shape, packed.dtype)
    return pl.pallas_call(
        body, name=name, in_specs=[HBM, HBM],
        out_specs=(SEM, SEM, HBM, HBM, pl.BlockSpec(memory_space=pltpu.VMEM)),
        out_shape=(pltpu.SemaphoreType.DMA((n_dev - 1,)), pltpu.SemaphoreType.DMA((n_dev - 1,)),
                   pltpu.HBM(packed.shape, packed.dtype), pltpu.HBM(land.shape, land.dtype),
                   jax.ShapeDtypeStruct((8, 128), F32)),
        input_output_aliases={0: 2, 1: 3},
        compiler_params=pltpu.CompilerParams(has_side_effects=IN_FLIGHT),
    )(_in_hbm(packed), _in_hbm(land))


def _allreduce_wait(started, after, *, name):
    send, recv, packed, land = started[:4]
    n_dev = 8

    def body(src_ref, land_ref, send_ref, recv_ref, *_):
        x, y, c = _place()
        for p in range(1, n_dev):
            cp = pltpu.make_async_remote_copy(
                src_ref=src_ref, dst_ref=land_ref.at[0], send_sem=send_ref.at[p - 1], recv_sem=recv_ref.at[p - 1],
                device_id=(x ^ (p >> 2), y ^ ((p >> 1) & 1), c ^ (p & 1)), device_id_type=MESH)
            cp.wait_send()
            cp.wait_recv()

    after = list(after) if isinstance(after, (list, tuple)) else [after]
    return pl.pallas_call(
        body, name=name, in_specs=[HBM, HBM, SEM, SEM] + [ANY] * len(after), out_specs=[HBM, HBM],
        out_shape=[pltpu.HBM(packed.shape, packed.dtype), pltpu.HBM(land.shape, land.dtype)],
        input_output_aliases={0: 0, 1: 1},
        compiler_params=pltpu.CompilerParams(has_side_effects=IN_FLIGHT),
    )(packed, land, send, recv, *after)


def _sum_devices(mine, land, *, name):
    n_dev = land.shape[0]

    def body(mine_ref, land_ref, out_ref):
        x, y, c = _place()
        me = 4 * x + 2 * y + c
        total = None
        for s in range(n_dev):
            part = jnp.where(me == s, mine_ref[...], land_ref[s])
            total = part if total is None else total + part
        out_ref[...] = total

    return pl.pallas_call(body, name=name, out_shape=jax.ShapeDtypeStruct(mine.shape, mine.dtype))(mine, land)


def _sum_received(grad, land, *, name, tr=256):
    _, r, c = grad.shape
    tr = _pick(r, tr, 8)

    def body(chip_ref, g_ref, l_ref, o_ref):
        o_ref[...] = ((g_ref[...] + l_ref[0].astype(F32)) + l_ref[1].astype(F32)) + l_ref[2].astype(F32)

    chip = (2 * lax.axis_index("x") + lax.axis_index("y")).astype(jnp.int32).reshape(1)
    return pl.pallas_call(
        body, name=name,
        grid_spec=pltpu.PrefetchScalarGridSpec(
            num_scalar_prefetch=1, grid=(r // tr,),
            in_specs=[pl.BlockSpec((None, tr, c), lambda i, chip_ref: (chip_ref[0], i, 0)),
                      pl.BlockSpec((3, tr, c), lambda i, chip_ref: (0, i, 0))],
            out_specs=pl.BlockSpec((tr, c), lambda i, chip_ref: (i, 0))),
        out_shape=jax.ShapeDtypeStruct((r, c), F32), compiler_params=_params("parallel"),
    )(chip, grad, land)


def _packed_rows(size, d):
    return -(-size // (8 * d)) * 8


def _pack_rows(arrays, d):
    rows = []
    for arr in arrays:
        flat = arr.reshape(-1).astype(F32)
        n = _packed_rows(flat.shape[0], d)
        rows.append(jnp.pad(flat, (0, n * d - flat.shape[0])).reshape(n, d))
    return jnp.concatenate(rows, axis=0)


def _unpack_rows(packed, shapes, d):
    out, row = [], 0
    for shape in shapes:
        size = math.prod(shape)
        n = _packed_rows(size, d)
        out.append(packed[row:row + n].reshape(-1)[:size].reshape(shape))
        row += n
    return out


SMALL = ("norm1_g", "gate_b", "conv_b", "conv_norm_g", "q_norm_g", "k_norm_g", "norm2_g", "ffn_conv_b")
LARGE = ("w_in", "w_conv_out", "w_attn_out", "w_out", "w_up", "w_down")
WEIGHTS = ("norm1_g", "w_in", "gate_b", "conv_w", "conv_b", "conv_norm_g", "w_conv_out", "q_norm_g", "k_norm_g",
           "w_attn_out", "w_out", "norm2_g", "w_up", "ffn_conv_w", "ffn_conv_b", "w_down")


def _after(vec, token):
    return vec if token is None else vec + token[0:1, 0:1]


def _local_step(dims, x, target, small, first_weights, other_weights, send_grad):
    d, f, heads = dims.d_model, dims.d_ff, dims.n_heads
    small = dict(small)
    row = lambda name: small[name].reshape(1, -1)
    head_sum = _head_sum_matrix(dims)
    head_spread = jnp.transpose(head_sum)
    ones = (head_sum, head_spread)
    gq = jnp.tile(row("q_norm_g"), (1, heads))
    gk = jnp.tile(row("k_norm_g"), (1, heads))
    one_shard = lambda w: w.reshape(1, -1, w.shape[-1])

    h = _rmsnorm_fwd(x, row("norm1_g"), name="norm1")
    full = first_weights(h)
    w_in = full["w_in"]
    conv_w = jnp.pad(full["conv_w"], ((0, CONV_HALO - dims.conv_width), (0, 0)))
    ffn_w = jnp.pad(full["ffn_conv_w"], ((0, FFN_HALO - dims.ffn_conv_width), (0, 0)))
    z = _mm_nn(h, w_in, out_dtype=BF16, after=full.get("token"), tm=2048, tn=1792, name="in_proj")
    a1, a3 = _conv_branch_fwd(z, conv_w, row("conv_b"), row("conv_norm_g"), dims, name="conv_branch")
    qkv = _qkv_layouts_fwd(z, gq, gk, ones, dims, name="qk_norm")
    per_group = {dil: _attn_fwd(*qkv[dil], dims, dil, name=f"attn_fwd_d{dil}") for dil in DILATIONS}
    o, lse = _attn_combine(per_group, head_spread, dims, name="attn_combine")
    full = other_weights(o)
    w_up = full["w_up"]
    w_co, w_ao, w_o, w_dn = (one_shard(full[k]) for k in ("w_conv_out", "w_attn_out", "w_out", "w_down"))
    ya = _mm_nn(a3, w_co, out_dtype=F32, name="conv_out_proj")
    yb = _mm_nn(o, w_ao, out_dtype=F32, name="attn_out_proj")
    mixed = _mix_fwd(ya, yb, z, row("gate_b"), dims, name="gate_mix")
    x1, h2 = _proj_residual_norm(mixed, w_o, x, row("norm2_g"), name="out_proj_norm2")
    up = _mm_nn(h2, w_up, out_dtype=F32, tm=2048, name="up_proj")
    act = _ffn_act_fwd(up, ffn_w, row("ffn_conv_b"), dims, name="ffn_act")
    dy, dy_b, loss = _proj_residual_loss(act, w_dn, x1, target, tm=512, name="down_proj_loss")

    grads = {}

    def large(name, g):
        grads[name], g_bf16 = g
        return send_grad(name, g_bf16)

    sent = large("w_down", _mm_tn(act, dy_b, n_shards=1, name="dw_down"))
    dact = _mm_nt(dy_b, w_dn, out_dtype=BF16, after=sent, name="d_act")
    dup, dfw, dfb = _ffn_bwd(dact, up, ffn_w, row("ffn_conv_b"), dims, name="ffn_bwd")
    grads["ffn_conv_w"], grads["ffn_conv_b"] = dfw[:dims.ffn_conv_width], dfb
    sent = large("w_up", _mm_tn(h2, dup, n_shards=N_CHIPS, name="dw_up"))
    dx1, dx1_b, grads["norm2_g"] = _mm_nt_rmsnorm_bwd(dup, w_up, x1, row("norm2_g"), dy, want_bf16=True, after=sent,
                                                     name="d_h2_norm2_bwd")
    sent = large("w_out", _mm_tn(mixed, dx1_b, n_shards=1, name="dw_out"))
    dya, dyb, dz_gate, grads["gate_b"] = _mix_bwd(dx1_b, w_o, ya, yb, z, row("gate_b"), dims, after=sent,
                                                  name="d_mix_gate_mix_bwd")
    sent = large("w_attn_out", _mm_tn(o, dyb, n_shards=1, name="dw_attn_out"))
    do = _mm_nt(dyb, w_ao, out_dtype=BF16, after=sent, name="d_attn")
    dos, deltas = _attn_bwd_prep(do, o, head_sum, dims, name="attn_bwd_prep")
    dqkv = {dil: _attn_bwd(*qkv[dil], dos[dil], lse[dil], deltas[dil], dims, dil, name=f"attn_bwd_d{dil}")
            for dil in DILATIONS}
    dz_qkv, dgq, dgk = _qkv_layouts_bwd(z, dqkv, gq, gk, ones, dims, name="qk_norm_bwd")
    grads["q_norm_g"] = dgq.reshape(heads, dims.head_dim).sum(axis=0)
    grads["k_norm_g"] = dgk.reshape(heads, dims.head_dim).sum(axis=0)
    sent = large("w_conv_out", _mm_tn(a3, dya, n_shards=1, name="dw_conv_out"))
    da1, grads["conv_norm_g"] = _mm_nt_rmsnorm_bwd(dya, w_co, a1, row("conv_norm_g"), None, want_bf16=False, silu=True,
                                                   after=sent, name="d_conv_act_norm_bwd")
    dz, dcw, grads["conv_b"] = _conv_branch_bwd(da1, z, conv_w, [dz_qkv, dz_gate], dims, name="conv_branch_bwd")
    grads["conv_w"] = dcw[:dims.conv_width]
    sent = large("w_in", _mm_tn(h, dz, n_shards=N_CHIPS, name="dw_in"))
    dx, grads["norm1_g"] = _mm_nt_rmsnorm_bwd(dz, w_in, x, row("norm1_g"), dx1, want_bf16=False, after=sent,
                                              name="d_h_norm1_bwd")
    return loss, dx, grads


def _step(dims, x, target, w, m, v):
    d = dims.d_model
    t = dims.tokens
    sq = lambda a: a.reshape(a.shape[1:])
    w2, m2, v2 = ({k: sq(a) for k, a in grp.items()} for grp in (w, m, v))

    conv_pad = jnp.pad(w2["conv_w"], ((0, CONV_HALO - dims.conv_width), (0, 0)))
    ffn_pad = jnp.pad(w2["ffn_conv_w"], ((0, FFN_HALO - dims.ffn_conv_width), (0, 0)))
    first_names = ("w_in", "conv_w", "ffn_conv_w")
    other_names = tuple(k for k in LARGE if k not in first_names)
    lands = dict(zip(first_names, _cast_to_lands([w2["w_in"], conv_pad, ffn_pad], [BF16, F32, F32], name="cast_first")))
    first = _gather_start([lands[k] for k in first_names], x, halved=(0,), name="gather_start_first")
    lands.update(zip(other_names, _cast_to_lands([w2[k] for k in other_names], [BF16] * len(other_names),
                                                 after=first[3], name="cast_other")))
    other = []
    cols = lambda g, rows: jnp.moveaxis(g, 0, 1).reshape(g.shape[1], -1)[:rows]

    def first_weights(after):
        got = dict(zip(first_names, _gather_wait(*first[:3], [after] + [lands[k] for k in other_names], halved=(0,),
                                                 name="gather_wait_first")))
        got["w_in"] = _forward_to_sibling(got["w_in"], name="forward_w_in")
        other.extend(_gather_start([lands[k] for k in other_names], got["w_in"], name="gather_start_other"))
        got["conv_w"] = cols(got["conv_w"], dims.conv_width)
        got["ffn_conv_w"] = cols(got["ffn_conv_w"], dims.ffn_conv_width)
        got["token"] = other[3]
        return got

    def other_weights(after):
        return dict(zip(other_names, _gather_wait(*other[:3], after, name="gather_wait_other")))

    started = {}

    def send_grad(name, g):
        send, recv, g_thru, land, token = _scatter_start(g.reshape(N_CHIPS, -1, g.shape[-1]), name=f"scatter_start_{name}")
        started[name] = (send, recv, g_thru, land)
        return token

    small = {k: w2[k] for k in SMALL}
    small["norm1_g"] = _after(small["norm1_g"].reshape(1, -1), first[3])
    loss, dx, grads = _local_step(dims, x.reshape(t, d), target.reshape(t, d), small, first_weights, other_weights, send_grad)

    def my_sums(names, after, tag):
        arrived = _scatter_wait([started[k] for k in names], after, name=f"scatter_wait_{tag}")
        blocks = [grads[k].reshape(N_CHIPS, -1, grads[k].shape[-1]) for k in names]
        return [_sum_received(g, land, name=f"sum_{k}") for k, g, (_, land) in zip(names, blocks, arrived)]

    def updates(names, mine, theirs):
        return {k: _adamw(w2[k], [a, b], m2[k], v2[k], name=f"adamw_{k}") for k, a, b in zip(names, mine, theirs)}

    small_names = SMALL + ("conv_w", "ffn_conv_w")
    packed = _pack_rows([grads[k] for k in small_names] + [loss[0, 0]], d)
    reducing = _allreduce_start(packed, name="allreduce_start")
    others = [k for k in LARGE if k != "w_in"]
    mine_others = my_sums(others, [dx, reducing[4]], "others")
    swapping_others = _swap_start(mine_others, name="swap_start_others")
    mine_w_in = my_sums(["w_in"], swapping_others[4], "w_in")
    swapping_w_in = _swap_start(mine_w_in, name="swap_start_w_in")
    out = updates(others, *_swap_wait(swapping_others, swapping_w_in[4], name="swap_wait_others"))
    last_updates = [out[k][1] for k in others]
    reduced = _sum_devices(*_allreduce_wait(reducing, last_updates, name="allreduce_wait"), name="allreduce_sum")
    shapes = [grads[k].shape for k in small_names] + [()]
    *small_g, loss_total = _unpack_rows(reduced, shapes, d)
    small_g = dict(zip(small_names, small_g))
    chip = 2 * lax.axis_index("x") + lax.axis_index("y")
    for k in ("conv_w", "ffn_conv_w"):
        width = w2[k].shape[1]
        small_g[k] = lax.dynamic_slice_in_dim(small_g[k], chip * width, width, axis=1)

    small_shapes = [w2[k].shape for k in small_names]
    pack = lambda grp: _pack_rows([grp[k] for k in small_names], d)
    results = _adamw(pack(w2), [pack(small_g)], pack(m2), pack(v2), name="adamw_small")
    unpacked = [_unpack_rows(r, small_shapes, d) for r in results]
    out.update({k: tuple(u[i] for u in unpacked) for i, k in enumerate(small_names)})
    out.update(updates(["w_in"], *_swap_wait(swapping_w_in, results[1], name="swap_wait_w_in")))

    lead = lambda a: a.reshape((1,) + a.shape)
    ordered = [[lead(out[k][j].reshape(w2[k].shape)) for k in WEIGHTS] for j in range(4)]
    return (loss_total, dx.reshape(x.shape), *ordered[0], *ordered[1], *ordered[2], *ordered[3])


def kernel(x, norm1_g, w_in, gate_b, conv_w, conv_b, conv_norm_g, w_conv_out, q_norm_g, k_norm_g, w_attn_out, w_out, norm2_g, w_up, ffn_conv_w, ffn_conv_b, w_down, loss_target, m_norm1_g, m_w_in, m_gate_b, m_conv_w, m_conv_b, m_conv_norm_g, m_w_conv_out, m_q_norm_g, m_k_norm_g, m_w_attn_out, m_w_out, m_norm2_g, m_w_up, m_ffn_conv_w, m_ffn_conv_b, m_w_down, v_norm1_g, v_w_in, v_gate_b, v_conv_w, v_conv_b, v_conv_norm_g, v_w_conv_out, v_q_norm_g, v_k_norm_g, v_w_attn_out, v_w_out, v_norm2_g, v_w_up, v_ffn_conv_w, v_ffn_conv_b, v_w_down):
    w = dict(zip(WEIGHTS, (norm1_g, w_in, gate_b, conv_w, conv_b, conv_norm_g, w_conv_out, q_norm_g, k_norm_g,
                           w_attn_out, w_out, norm2_g, w_up, ffn_conv_w, ffn_conv_b, w_down)))
    m = dict(zip(WEIGHTS, (m_norm1_g, m_w_in, m_gate_b, m_conv_w, m_conv_b, m_conv_norm_g, m_w_conv_out, m_q_norm_g,
                           m_k_norm_g, m_w_attn_out, m_w_out, m_norm2_g, m_w_up, m_ffn_conv_w, m_ffn_conv_b, m_w_down)))
    v = dict(zip(WEIGHTS, (v_norm1_g, v_w_in, v_gate_b, v_conv_w, v_conv_b, v_conv_norm_g, v_w_conv_out, v_q_norm_g,
                           v_k_norm_g, v_w_attn_out, v_w_out, v_norm2_g, v_w_up, v_ffn_conv_w, v_ffn_conv_b, v_w_down)))
    dims = Dims(d_model=x.shape[-1], batch_local=x.shape[0], seq=x.shape[1], d_ff=w_down.shape[1] * N_CHIPS)
    return _step(dims, x, loss_target, w, m, v)
```

```python
import functools
import math
from typing import NamedTuple

import jax
import jax.numpy as jnp
from jax import lax
from jax.experimental import pallas as pl
from jax.experimental.pallas import tpu as pltpu

F32 = jnp.float32
BF16 = jnp.bfloat16

RMS_EPS = 1e-6
ATTN_BLOCK = 128
DILATIONS = (1, 4, 16)
CONV_HALO = 32
FFN_HALO = 8
ADAM_LR, ADAM_B1, ADAM_B2, ADAM_EPS, ADAM_WD, ADAM_STEP = 0.001, 0.9, 0.999, 1e-08, 0.01, 10
V7X_VMEM_LIMIT_BYTES = 56 * 2 ** 20
N_CHIPS = 4
MESH = pl.DeviceIdType.MESH


class Dims(NamedTuple):
    d_model: int = 1024
    n_heads: int = 16
    head_dim: int = 64
    d_ff: int = 2816
    seq: int = 2048
    batch_local: int = 2
    conv_width: int = 31
    ffn_conv_width: int = 3

    @property
    def tokens(self):
        return self.seq * self.batch_local


def _params(*semantics):
    return pltpu.CompilerParams(dimension_semantics=semantics, vmem_limit_bytes=V7X_VMEM_LIMIT_BYTES)


ANY = pl.BlockSpec(memory_space=pl.ANY)


def _ordered(body, n_inputs, after):
    after = [] if after is None else list(after) if isinstance(after, (list, tuple)) else [after]
    if not after:
        return body, [], []

    def wrapped(*refs):
        return body(*refs[:n_inputs], *refs[n_inputs + len(after):])

    return wrapped, [ANY] * len(after), after


def _pick(n, target, mult=128):
    if n <= target:
        return n
    best = None
    for t in range(mult, target + 1, mult):
        if n % t == 0:
            best = t
    assert best is not None, (n, target, mult)
    return best


def _sigmoid(v):
    return 1.0 / (1.0 + jnp.exp(-v))


def _mm_nn(a, w, *, out_dtype, name, residual=None, after=None, tm=1024, tn=1408, tk=2816):
    m, k = a.shape
    nsh, k2, c = w.shape
    assert k == k2 and a.dtype == BF16 and w.dtype == BF16
    n = nsh * c
    tm, tn, tk = _pick(m, tm, 8), _pick(c, tn), _pick(k, tk)
    nk, cpn = k // tk, c // tn

    def body(*refs):
        if residual is None:
            a_ref, w_ref, o_ref, acc = refs
        else:
            a_ref, w_ref, r_ref, o_ref, acc = refs
        prod = jnp.dot(a_ref[...], w_ref[...], preferred_element_type=F32)

        def finish(total):
            if residual is not None:
                total = total + r_ref[...]
            o_ref[...] = total.astype(out_dtype)

        if nk == 1:
            finish(prod)
        else:
            kk = pl.program_id(2)

            @pl.when(kk == 0)
            def _():
                acc[...] = prod

            @pl.when(kk > 0)
            def _():
                acc[...] += prod

            @pl.when(kk == nk - 1)
            def _():
                finish(acc[...])

    in_specs = [pl.BlockSpec((tm, tk), lambda i, j, kk: (i, kk)),
                pl.BlockSpec((None, tk, tn), lambda i, j, kk: (j // cpn, kk, j % cpn))]
    args = [a, w]
    if residual is not None:
        in_specs.append(pl.BlockSpec((tm, tn), lambda i, j, kk: (i, j)))
        args.append(residual)
    body, more_specs, more_args = _ordered(body, len(args), after)
    return pl.pallas_call(
        body, name=name, grid=(m // tm, n // tn, nk),
        in_specs=in_specs + more_specs, out_specs=pl.BlockSpec((tm, tn), lambda i, j, kk: (i, j)),
        out_shape=jax.ShapeDtypeStruct((m, n), out_dtype),
        scratch_shapes=[pltpu.VMEM((tm, tn) if nk > 1 else (8, 128), F32)],
        compiler_params=_params("parallel", "parallel", "arbitrary"),
    )(*args, *more_args)


def _proj_residual_loss(a, w, residual, target, *, name, tm=1024):
    m, k = a.shape
    _, k2, n = w.shape
    assert w.shape[0] == 1 and k == k2 and a.dtype == BF16 and w.dtype == BF16
    tm = _pick(m, tm, 8)

    def body(a_ref, w_ref, r_ref, t_ref, dy_ref, dyb_ref, loss_ref):
        err = r_ref[...] + jnp.dot(a_ref[...], w_ref[...], preferred_element_type=F32) - t_ref[...]
        dy = err * (1.0 / n)
        dy_ref[...] = dy
        dyb_ref[...] = dy.astype(BF16)
        part = jnp.sum(jnp.sum(err * err, axis=-1, keepdims=True), axis=0, keepdims=True) * (0.5 / n)
        _accumulate(loss_ref, jnp.broadcast_to(part, (8, 128)), pl.program_id(0) == 0)

    rows = lambda width: pl.BlockSpec((tm, width), lambda i: (i, 0))
    return pl.pallas_call(
        body, name=name, grid=(m // tm,),
        in_specs=[rows(k), pl.BlockSpec((None, k, n), lambda i: (0, 0, 0)), rows(n), rows(n)],
        out_specs=[rows(n), rows(n), pl.BlockSpec((8, 128), lambda i: (0, 0))],
        out_shape=[jax.ShapeDtypeStruct((m, n), F32), jax.ShapeDtypeStruct((m, n), BF16),
                   jax.ShapeDtypeStruct((8, 128), F32)],
        compiler_params=_params("arbitrary"),
    )(a, w, residual, target)


def _mm_nt(a, w, *, out_dtype, name, after=None, tm=1024, tn=1408, tk=1792):
    m, k = a.shape
    nsh, r, c = w.shape
    assert k == nsh * c and a.dtype == BF16 and w.dtype == BF16
    tm, tn, tk = _pick(m, tm, 8), _pick(r, tn), _pick(c, tk)
    nk, cpk = k // tk, c // tk

    def body(a_ref, w_ref, o_ref, acc):
        prod = lax.dot_general(a_ref[...], w_ref[...], (((1,), (1,)), ((), ())), preferred_element_type=F32)
        if nk == 1:
            o_ref[...] = prod.astype(out_dtype)
        else:
            kk = pl.program_id(2)

            @pl.when(kk == 0)
            def _():
                acc[...] = prod

            @pl.when(kk > 0)
            def _():
                acc[...] += prod

            @pl.when(kk == nk - 1)
            def _():
                o_ref[...] = acc[...].astype(out_dtype)

    body, more_specs, more_args = _ordered(body, 2, after)
    return pl.pallas_call(
        body, name=name, grid=(m // tm, r // tn, nk),
        in_specs=[pl.BlockSpec((tm, tk), lambda i, j, kk: (i, kk)),
                  pl.BlockSpec((None, tn, tk), lambda i, j, kk: (kk // cpk, j, kk % cpk))] + more_specs,
        out_specs=pl.BlockSpec((tm, tn), lambda i, j, kk: (i, j)),
        out_shape=jax.ShapeDtypeStruct((m, r), out_dtype),
        scratch_shapes=[pltpu.VMEM((tm, tn) if nk > 1 else (8, 128), F32)],
        compiler_params=_params("parallel", "parallel", "arbitrary"),
    )(a, w, *more_args)


NORM_BWD_ROWS = 256


def _mm_nt_rmsnorm_bwd(a, w, x, g, dres, *, name, want_bf16, silu=False, after=None, tm=1024, tk=1792):
    m, k = a.shape
    nsh, r, c = w.shape
    assert k == nsh * c and a.dtype == BF16 and w.dtype == BF16 and x.shape == (m, r)
    tm, tk = _pick(m, tm, 8), _pick(c, tk)
    nk, cpk = k // tk, c // tk
    rows = _pick(tm, NORM_BWD_ROWS, 8)
    n_in = 4 if dres is None else 5

    def body(a_ref, w_ref, x_ref, g_ref, *rest):
        dres_ref = None if dres is None else rest[0]
        outs, acc = rest[n_in - 4:-1], rest[-1]
        dx_ref, dg_ref = outs[0], outs[-1]
        kk = pl.program_id(1)
        prod = lax.dot_general(a_ref[...], w_ref[...], (((1,), (1,)), ((), ())), preferred_element_type=F32)

        @pl.when(kk == 0)
        def _():
            acc[...] = prod

        @pl.when(kk > 0)
        def _():
            acc[...] += prod

        @pl.when(kk == nk - 1)
        def _():
            dg = jnp.zeros((1, r), F32)
            for r0 in range(0, tm, rows):
                part = slice(r0, r0 + rows)
                xv, dyv = x_ref[part, :], acc[part, :]
                inv = lax.rsqrt(jnp.mean(xv * xv, axis=-1, keepdims=True) + RMS_EPS)
                if silu:
                    y = xv * inv * g_ref[...]
                    sg = _sigmoid(y)
                    dyv = dyv * sg * (1.0 + y * (1.0 - sg))
                gy = dyv * g_ref[...]
                dx = inv * gy - xv * (inv * inv * inv) * jnp.mean(xv * gy, axis=-1, keepdims=True)
                if dres is not None:
                    dx = dx + dres_ref[part, :]
                dx_ref[part, :] = dx
                if want_bf16:
                    outs[1][part, :] = dx.astype(BF16)
                dg = dg + jnp.sum(dyv * xv * inv, axis=0, keepdims=True)
            _accumulate(dg_ref, dg, pl.program_id(0) == 0)

    whole = lambda: pl.BlockSpec((tm, r), lambda i, kk: (i, 0))
    vec = pl.BlockSpec((1, r), lambda i, kk: (0, 0))
    out_shape, out_specs = [jax.ShapeDtypeStruct((m, r), F32)], [whole()]
    if want_bf16:
        out_shape.append(jax.ShapeDtypeStruct((m, r), BF16))
        out_specs.append(whole())
    out_shape.append(jax.ShapeDtypeStruct((1, r), F32))
    out_specs.append(vec)
    body, more_specs, more_args = _ordered(body, n_in, after)
    residual_specs, residual_args = ([], []) if dres is None else ([whole()], [dres])
    return pl.pallas_call(
        body, name=name, grid=(m // tm, nk),
        in_specs=[pl.BlockSpec((tm, tk), lambda i, kk: (i, kk)),
                  pl.BlockSpec((None, r, tk), lambda i, kk: (kk // cpk, 0, kk % cpk)), whole(), vec]
        + residual_specs + more_specs,
        out_specs=out_specs, out_shape=out_shape,
        scratch_shapes=[pltpu.VMEM((tm, r), F32)],
        compiler_params=_params("arbitrary", "arbitrary"),
    )(a, w, x, g, *residual_args, *more_args)


MM_TN_VMEM_BYTES = 44 * 2 ** 20


def _mm_tn(a, b, *, n_shards, name, tm=1408, tn=1408):
    t, m = a.shape
    t2, n = b.shape
    assert t == t2 and a.dtype == BF16 and b.dtype == BF16
    c = n // n_shards
    tm, tn = _pick(m, tm), _pick(c, tn)
    if m // tm == 1 and n // tn == 1 and tn % (2 * LANES) == 0:
        tn //= 2
    fixed = 2 * tm * tn * 6
    if 4 * t * (tm + tn) + fixed <= MM_TN_VMEM_BYTES:
        tk = t
    else:
        tk = _pick(t, (MM_TN_VMEM_BYTES - fixed - 4 * tm * tn) // (4 * (tm + tn)), 8)
    nk, cpn = t // tk, c // tn

    def body(a_ref, b_ref, o_ref, ob_ref, acc):
        kk = pl.program_id(2)
        prod = lax.dot_general(a_ref[...], b_ref[...], (((0,), (0,)), ((), ())), preferred_element_type=F32)

        def finish(total):
            o_ref[...] = total
            ob_ref[...] = total.astype(BF16)

        if nk == 1:
            finish(prod)
        else:
            @pl.when(kk == 0)
            def _():
                acc[...] = prod

            @pl.when(kk > 0)
            def _():
                acc[...] += prod

            @pl.when(kk == nk - 1)
            def _():
                finish(acc[...])

    out_spec = pl.BlockSpec((None, tm, tn), lambda i, j, kk: (j // cpn, i, j % cpn))
    return pl.pallas_call(
        body, name=name, grid=(m // tm, n // tn, nk),
        in_specs=[pl.BlockSpec((tk, tm), lambda i, j, kk: (kk, i)),
                  pl.BlockSpec((tk, tn), lambda i, j, kk: (kk, j))],
        out_specs=[out_spec, out_spec],
        out_shape=[jax.ShapeDtypeStruct((n_shards, m, c), F32), jax.ShapeDtypeStruct((n_shards, m, c), BF16)],
        scratch_shapes=[pltpu.VMEM((tm, tn) if nk > 1 else (8, 128), F32)],
        compiler_params=_params("parallel", "parallel", "arbitrary"),
    )(a, b)


def _row_spec(tr, width, col=0):
    return pl.BlockSpec((tr, width), lambda i, col=col: (i, col))


def _vec_spec(width, col=0):
    return pl.BlockSpec((1, width), lambda i, col=col: (0, col))


def _accumulate(ref, value, first):
    @pl.when(first)
    def _():
        ref[...] = value

    @pl.when(jnp.logical_not(first))
    def _():
        ref[...] += value


def _rmsnorm_fwd(x, g, *, name, tr=512):
    t, d = x.shape
    tr = _pick(t, tr, 8)

    def body(x_ref, g_ref, o_ref):
        xv = x_ref[...]
        r = lax.rsqrt(jnp.mean(xv * xv, axis=-1, keepdims=True) + RMS_EPS)
        o_ref[...] = (xv * r * g_ref[...]).astype(BF16)

    return pl.pallas_call(
        body, name=name, grid=(t // tr,),
        in_specs=[_row_spec(tr, d), _vec_spec(d)], out_specs=_row_spec(tr, d),
        out_shape=jax.ShapeDtypeStruct((t, d), BF16), compiler_params=_params("parallel"),
    )(x, g)


CONV_ROWS = 16


def _seq_specs(dims, ts, width, halo, col, *, nxt=False):
    nst, per = dims.seq // ts, ts // halo
    last = dims.tokens // halo - 1
    cur = pl.BlockSpec((ts, width), lambda b, i: (b * nst + i, col))
    if nxt:
        edge = pl.BlockSpec((halo, width), lambda b, i: (jnp.minimum((b * nst + i + 1) * per, last), col))
    else:
        edge = pl.BlockSpec((halo, width), lambda b, i: (jnp.maximum((b * nst + i) * per - 1, 0), col))
    return cur, edge


SUBLANES = 8


def _shifted_copies(buf, shifted):
    rows = shifted.shape[1]
    for s in range(1, SUBLANES):
        shifted[s - 1] = buf[pl.ds(s, rows), :]


def _window(buf, shifted, start, size):
    a, s = divmod(start, SUBLANES)
    src = buf if s == 0 else shifted.at[s - 1]
    return src[pl.ds(SUBLANES * a, size), :]


def _conv_branch_fwd(z, w, b, g, dims, *, name, ts=128):
    t, c, kw = z.shape[0], dims.d_model, dims.conv_width
    base = CONV_HALO - (kw - 1)

    def body(av_ref, hv_ref, ag_ref, hg_ref, w_ref, b_ref, g_ref, a1_ref, a3_ref, buf, shifted):
        i = pl.program_id(1)
        buf[CONV_HALO:, :] = av_ref[...].astype(F32) * _sigmoid(ag_ref[...].astype(F32))
        buf[0:CONV_HALO, :] = jnp.where(i > 0, hv_ref[...].astype(F32) * _sigmoid(hg_ref[...].astype(F32)), 0.0)
        _shifted_copies(buf, shifted)
        for r0 in range(0, ts, CONV_ROWS):
            acc = jnp.broadcast_to(b_ref[...], (CONV_ROWS, c))
            for k in range(kw):
                acc = acc + w_ref[k:k + 1, :] * _window(buf, shifted, r0 + base + k, CONV_ROWS)
            a1_ref[r0:r0 + CONV_ROWS, :] = acc
            a2 = acc * lax.rsqrt(jnp.mean(acc * acc, axis=-1, keepdims=True) + RMS_EPS) * g_ref[...]
            a3_ref[r0:r0 + CONV_ROWS, :] = (a2 * _sigmoid(a2)).astype(BF16)

    vec = pl.BlockSpec((1, c), lambda b, i: (0, 0))
    out = pl.BlockSpec((ts, c), lambda b, i: (b * (dims.seq // ts) + i, 0))
    return pl.pallas_call(
        body, name=name, grid=(dims.batch_local, dims.seq // ts),
        in_specs=[*_seq_specs(dims, ts, c, CONV_HALO, 0), *_seq_specs(dims, ts, c, CONV_HALO, 1),
                  pl.BlockSpec((CONV_HALO, c), lambda b, i: (0, 0)), vec, vec],
        out_specs=[out, out],
        out_shape=[jax.ShapeDtypeStruct((t, c), F32), jax.ShapeDtypeStruct((t, c), BF16)],
        scratch_shapes=[pltpu.VMEM((CONV_HALO + ts, c), F32),
                        pltpu.VMEM((SUBLANES - 1, CONV_HALO + ts - SUBLANES, c), F32)],
        compiler_params=_params("parallel", "parallel"),
    )(z, z, z, z, w, b, g)


def _conv_branch_bwd(da1, z, w, rest_of_dz, dims, *, name, ts=128):
    t, c, kw = z.shape[0], dims.d_model, dims.conv_width
    nst = dims.seq // ts
    base = CONV_HALO - (kw - 1)
    n_rest = len(rest_of_dz)
    total = 2 * c + sum(r.shape[1] for r in rest_of_dz)

    def body(d_ref, dn_ref, av_ref, hv_ref, ag_ref, hg_ref, w_ref, *more):
        rest_refs = more[:n_rest]
        dz_ref, dw_ref, db_ref, abuf, dbuf, ashift, dshift = more[n_rest:]
        col = 2 * c
        for r in rest_refs:
            dz_ref[:, col:col + r.shape[1]] = r[...]
            col += r.shape[1]
        i = pl.program_id(1)
        first = jnp.logical_and(pl.program_id(0) == 0, i == 0)
        abuf[CONV_HALO:, :] = av_ref[...].astype(F32) * _sigmoid(ag_ref[...].astype(F32))
        abuf[0:CONV_HALO, :] = jnp.where(i > 0, hv_ref[...].astype(F32) * _sigmoid(hg_ref[...].astype(F32)), 0.0)
        d1 = d_ref[...]
        dbuf[0:ts, :] = d1
        dbuf[ts:, :] = jnp.where(i < nst - 1, dn_ref[...], 0.0)
        _shifted_copies(abuf, ashift)
        _shifted_copies(dbuf, dshift)

        @pl.when(first)
        def _():
            dw_ref[...] = jnp.zeros_like(dw_ref)
            db_ref[...] = jnp.zeros_like(db_ref)

        db_ref[...] += jnp.sum(d1, axis=0, keepdims=True)
        for k in range(kw):
            dw_ref[k:k + 1, :] += jnp.sum(d1 * _window(abuf, ashift, base + k, ts), axis=0, keepdims=True)
        for r0 in range(0, ts, CONV_ROWS):
            acc = jnp.zeros((CONV_ROWS, c), F32)
            for k in range(kw):
                acc = acc + w_ref[k:k + 1, :] * _window(dbuf, dshift, r0 + (kw - 1) - k, CONV_ROWS)
            av = av_ref[r0:r0 + CONV_ROWS, :].astype(F32)
            sg = _sigmoid(ag_ref[r0:r0 + CONV_ROWS, :].astype(F32))
            dz_ref[r0:r0 + CONV_ROWS, 0:c] = (acc * sg).astype(BF16)
            dz_ref[r0:r0 + CONV_ROWS, c:2 * c] = (acc * av * sg * (1.0 - sg)).astype(BF16)

    cur, nxt = _seq_specs(dims, ts, c, CONV_HALO, 0, nxt=True)
    return pl.pallas_call(
        body, name=name, grid=(dims.batch_local, nst),
        in_specs=[cur, nxt, *_seq_specs(dims, ts, c, CONV_HALO, 0), *_seq_specs(dims, ts, c, CONV_HALO, 1),
                  pl.BlockSpec((CONV_HALO, c), lambda b, i: (0, 0))]
        + [pl.BlockSpec((ts, r.shape[1]), lambda b, i: (b * nst + i, 0)) for r in rest_of_dz],
        out_specs=[pl.BlockSpec((ts, total), lambda b, i: (b * nst + i, 0)),
                   pl.BlockSpec((CONV_HALO, c), lambda b, i: (0, 0)), pl.BlockSpec((1, c), lambda b, i: (0, 0))],
        out_shape=[jax.ShapeDtypeStruct((t, total), BF16), jax.ShapeDtypeStruct((CONV_HALO, c), F32),
                   jax.ShapeDtypeStruct((1, c), F32)],
        scratch_shapes=[pltpu.VMEM((CONV_HALO + ts, c), F32)] * 2
        + [pltpu.VMEM((SUBLANES - 1, CONV_HALO + ts - SUBLANES, c), F32)] * 2,
        compiler_params=_params("arbitrary", "arbitrary"),
    )(da1, da1, z, z, z, z, w, *rest_of_dz)


FFN_ROWS = 16
FFN_COLS = 256


def _ffn_chunks(ts, f):
    cw = _pick(f, FFN_COLS)
    return [(r0, c0, cw) for r0 in range(0, ts, FFN_ROWS) for c0 in range(0, f, cw)]


def _tap_sources(buf, moved, offsets, rows):
    taps, used = [], 0
    for off in offsets:
        if off % SUBLANES:
            moved[used] = buf[pl.ds(off, rows), :]
            taps.append((moved.at[used], 0))
            used += 1
        else:
            taps.append((buf, off))
    return taps


def _moved_copies(offsets):
    return sum(1 for off in offsets if off % SUBLANES)


def _taps_sum(taps, w_ref, init, r0, cols):
    for k, (src, off) in enumerate(taps):
        init = init + w_ref[k:k + 1, cols] * src[pl.ds(off + r0, init.shape[0]), cols]
    return init


def _ffn_bwd(dact, up, w, b, dims, *, name, ts=128):
    t, f, kw = up.shape[0], dims.d_ff, dims.ffn_conv_width
    nst = dims.seq // ts
    fwd_offsets = [FFN_HALO - (kw - 1) + k for k in range(kw)]
    bwd_offsets = [(kw - 1) - k for k in range(kw)]
    dact_halo = 2 * FFN_HALO

    def body(d_ref, dn_ref, up_ref, hp_ref, hn_ref, w_ref, b_ref, o_ref, dw_ref, db_ref, buf, moved, dbuf, dmoved):
        i = pl.program_id(1)
        first = jnp.logical_and(pl.program_id(0) == 0, i == 0)
        more = i < nst - 1
        buf[0:FFN_HALO, :] = jnp.where(i > 0, hp_ref[...], 0.0)
        buf[FFN_HALO:FFN_HALO + ts, :] = up_ref[...]
        buf[FFN_HALO + ts:, :] = hn_ref[...]
        taps = _tap_sources(buf, moved, fwd_offsets, ts + FFN_HALO)

        def du_chunk(r0, rows, c0, cw, d):
            vcols, gcols = slice(c0, c0 + cw), slice(f + c0, f + c0 + cw)
            uv = _taps_sum(taps, w_ref, jnp.broadcast_to(b_ref[:, vcols], (rows, cw)), r0, vcols)
            ug = _taps_sum(taps, w_ref, jnp.broadcast_to(b_ref[:, gcols], (rows, cw)), r0, gcols)
            sg = _sigmoid(ug)
            dbuf[r0:r0 + rows, vcols] = d * ug * sg
            dbuf[r0:r0 + rows, gcols] = d * uv * sg * (1.0 + ug * (1.0 - sg))

        for r0, c0, cw in _ffn_chunks(ts, f):
            du_chunk(r0, FFN_ROWS, c0, cw, d_ref[r0:r0 + FFN_ROWS, c0:c0 + cw].astype(F32))
        for _, c0, cw in _ffn_chunks(FFN_ROWS, f):
            d_next = dn_ref[:, c0:c0 + cw].astype(F32)[0:FFN_HALO]
            du_chunk(ts, FFN_HALO, c0, cw, jnp.where(more, d_next, 0.0))

        @pl.when(first)
        def _():
            dw_ref[...] = jnp.zeros_like(dw_ref)
            db_ref[...] = jnp.zeros_like(db_ref)

        du = dbuf[0:ts, :]
        db_ref[...] += jnp.sum(du, axis=0, keepdims=True)
        for k, (src, off) in enumerate(taps):
            dw_ref[k:k + 1, :] += jnp.sum(du * src[pl.ds(off, ts), :], axis=0, keepdims=True)

        dtaps = _tap_sources(dbuf, dmoved, bwd_offsets, ts)
        for r0, c0, cw in _ffn_chunks(ts, 2 * f):
            cols = slice(c0, c0 + cw)
            o_ref[r0:r0 + FFN_ROWS, cols] = _taps_sum(dtaps, w_ref, jnp.zeros((FFN_ROWS, cw), F32), r0, cols).astype(BF16)

    up_cur, up_prev = _seq_specs(dims, ts, 2 * f, FFN_HALO, 0)
    _, up_next = _seq_specs(dims, ts, 2 * f, FFN_HALO, 0, nxt=True)
    d_cur, d_next = _seq_specs(dims, ts, f, dact_halo, 0, nxt=True)
    full = lambda rows: pl.BlockSpec((rows, 2 * f), lambda b_, i: (0, 0))
    return pl.pallas_call(
        body, name=name, grid=(dims.batch_local, nst),
        in_specs=[d_cur, d_next, up_cur, up_prev, up_next, full(FFN_HALO), full(1)],
        out_specs=[pl.BlockSpec((ts, 2 * f), lambda b_, i: (b_ * nst + i, 0)), full(FFN_HALO), full(1)],
        out_shape=[jax.ShapeDtypeStruct((t, 2 * f), BF16), jax.ShapeDtypeStruct((FFN_HALO, 2 * f), F32),
                   jax.ShapeDtypeStruct((1, 2 * f), F32)],
        scratch_shapes=[pltpu.VMEM((ts + 2 * FFN_HALO, 2 * f), F32),
                        pltpu.VMEM((_moved_copies(fwd_offsets), ts + FFN_HALO, 2 * f), F32),
                        pltpu.VMEM((ts + FFN_HALO, 2 * f), F32),
                        pltpu.VMEM((_moved_copies(bwd_offsets), ts, 2 * f), F32)],
        compiler_params=_params("arbitrary", "arbitrary"),
    )(dact, dact, up, up, up, w, b)


def _ffn_act_fwd(up, w, b, dims, *, name, ts=128):
    t, f, kw = up.shape[0], dims.d_ff, dims.ffn_conv_width
    offsets = [FFN_HALO - (kw - 1) + k for k in range(kw)]

    def body(up_ref, h_ref, w_ref, b_ref, o_ref, buf, moved):
        buf[FFN_HALO:, :] = up_ref[...]
        buf[0:FFN_HALO, :] = jnp.where(pl.program_id(1) > 0, h_ref[...], 0.0)
        taps = _tap_sources(buf, moved, offsets, ts)
        for r0, c0, cw in _ffn_chunks(ts, f):
            vcols, gcols = slice(c0, c0 + cw), slice(f + c0, f + c0 + cw)
            uv = _taps_sum(taps, w_ref, jnp.broadcast_to(b_ref[:, vcols], (FFN_ROWS, cw)), r0, vcols)
            ug = _taps_sum(taps, w_ref, jnp.broadcast_to(b_ref[:, gcols], (FFN_ROWS, cw)), r0, gcols)
            o_ref[r0:r0 + FFN_ROWS, vcols] = (ug * _sigmoid(ug) * uv).astype(BF16)

    full = lambda rows: pl.BlockSpec((rows, 2 * f), lambda b_, i: (0, 0))
    return pl.pallas_call(
        body, name=name, grid=(dims.batch_local, dims.seq // ts),
        in_specs=[*_seq_specs(dims, ts, 2 * f, FFN_HALO, 0), full(FFN_HALO), full(1)],
        out_specs=pl.BlockSpec((ts, f), lambda b_, i: (b_ * (dims.seq // ts) + i, 0)),
        out_shape=jax.ShapeDtypeStruct((t, f), BF16),
        scratch_shapes=[pltpu.VMEM((FFN_HALO + ts, 2 * f), F32), pltpu.VMEM((_moved_copies(offsets), ts, 2 * f), F32)],
        compiler_params=_params("parallel", "parallel"),
    )(up, up, w, b)


def _dot_nt(a, b):
    return lax.dot_general(a, b, (((1,), (1,)), ((), ())), preferred_element_type=F32)


def _dot_tn(a, b):
    return lax.dot_general(a, b, (((0,), (0,)), ((), ())), preferred_element_type=F32)


LANES = 128
MASK_BIAS = 1e30
RESIDUE_DILATIONS = tuple(d for d in DILATIONS if d > 1)


def _rows_to_residues(value, out_ref, scr, d):
    rows, width = value.shape
    for c in range(width // LANES):
        cols = slice(LANES * c, LANES * (c + 1))
        scr[c] = value[:, cols]
        for r in range(d):
            out_ref[r, :, cols] = scr[c, pl.ds(r, rows // d, stride=d), :].astype(out_ref.dtype)


def _residues_to_rows(in_ref, scr, d):
    _, n, width = in_ref.shape
    slabs = []
    for c in range(width // LANES):
        cols = slice(LANES * c, LANES * (c + 1))
        for r in range(d):
            scr[c, pl.ds(r, n, stride=d), :] = in_ref[r, :, cols].astype(F32)
        slabs.append(scr[c])
    return slabs[0] if len(slabs) == 1 else jnp.concatenate(slabs, axis=1)


def _residue_shape(dims, d, width, dtype):
    return jax.ShapeDtypeStruct((dims.batch_local, d, dims.seq // d, width), dtype)


def _residue_spec(dims, d, tr, width):
    tiles = dims.seq // tr
    return pl.BlockSpec((None, d, tr // d, width), lambda i: (i // tiles, 0, i % tiles, 0))


def _head_sum_matrix(dims):
    a = dims.n_heads * dims.head_dim
    head = jnp.arange(a, dtype=jnp.int32) // dims.head_dim
    return (head[:, None] == jnp.arange(LANES, dtype=jnp.int32)[None, :]).astype(BF16)


def _two_pass_dot(v, m):
    hi = v.astype(BF16)
    lo = (v - hi.astype(F32)).astype(BF16)
    return jnp.dot(hi, m, preferred_element_type=F32) + jnp.dot(lo, m, preferred_element_type=F32)


def _residue_permutations(tr):
    out = []
    for d in RESIDUE_DILATIONS:
        dst = jnp.arange(tr, dtype=jnp.int32)
        src = d * (dst % (tr // d)) + dst // (tr // d)
        out.append((src[:, None] == jnp.arange(tr, dtype=jnp.int32)[None, :]).astype(BF16))
    return out


def _bf16_rows_to_residues(value, out_ref, perm_ref, d):
    n = value.shape[0] // d
    moved = jnp.dot(perm_ref[...], value, preferred_element_type=F32).astype(out_ref.dtype)
    for r in range(d):
        out_ref[r] = moved[r * n:(r + 1) * n]


def _bf16_residues_to_rows(in_ref, back_ref):
    d = in_ref.shape[0]
    stacked = jnp.concatenate([in_ref[r] for r in range(d)], axis=0)
    return jnp.dot(back_ref[...], stacked, preferred_element_type=F32)


def _qkv_layouts_fwd(z, gq, gk, head_ones, dims, *, name, tr=256):
    t = z.shape[0]
    a = dims.n_heads * dims.head_dim
    q_scale = dims.head_dim ** -0.5
    nres = len(RESIDUE_DILATIONS)

    def body(q_ref, k_ref, v_ref, gq_ref, gk_ref, sum_ref, spread_ref, *rest):
        perm_refs, outs = rest[:nres], rest[nres:]
        qv, kv = q_ref[...].astype(F32), k_ref[...].astype(F32)
        mean = lambda val: _two_pass_dot(_two_pass_dot(val, sum_ref[...]), spread_ref[...]) * (1.0 / dims.head_dim)
        rq = lax.rsqrt(mean(qv * qv) + RMS_EPS)
        rk = lax.rsqrt(mean(kv * kv) + RMS_EPS)
        values = ((qv * rq * gq_ref[...] * q_scale).astype(BF16), (kv * rk * gk_ref[...]).astype(BF16), v_ref[...])
        for j, val in enumerate(values):
            outs[j][...] = val
            for g, d in enumerate(RESIDUE_DILATIONS):
                _bf16_rows_to_residues(val, outs[3 * (g + 1) + j], perm_refs[g], d)

    out_specs = [_row_spec(tr, a)] * 3
    out_shape = [jax.ShapeDtypeStruct((t, a), BF16)] * 3
    for d in RESIDUE_DILATIONS:
        out_specs += [_residue_spec(dims, d, tr, a)] * 3
        out_shape += [_residue_shape(dims, d, a, BF16)] * 3
    outs = pl.pallas_call(
        body, name=name, grid=(t // tr,),
        in_specs=[_row_spec(tr, a, 2), _row_spec(tr, a, 3), _row_spec(tr, a, 4), _vec_spec(a), _vec_spec(a),
                  pl.BlockSpec((a, LANES), lambda i: (0, 0)), pl.BlockSpec((LANES, a), lambda i: (0, 0))]
        + [pl.BlockSpec((tr, tr), lambda i: (0, 0))] * nres,
        out_specs=out_specs, out_shape=out_shape,
        compiler_params=_params("parallel"),
    )(z, z, z, gq, gk, *head_ones, *_residue_permutations(tr))
    return {d: tuple(outs[3 * g:3 * g + 3]) for g, d in enumerate((1,) + RESIDUE_DILATIONS)}


ATTN_RESIDUES_PER_STEP = 4
ATTN_RESIDUES_PER_STEP_WINDOWED = 2


def _attn_groups(dims, dil):
    return (dims.batch_local, dil) if dil > 1 else (1, dims.batch_local)


def _attn_array(x, dims, dil):
    return x if dil > 1 else x.reshape(1, dims.batch_local, dims.seq, x.shape[-1])


def _attn_residues(dims, dil):
    one_block = dims.seq // dil == ATTN_BLOCK
    return math.gcd(_attn_groups(dims, dil)[1], ATTN_RESIDUES_PER_STEP if one_block else ATTN_RESIDUES_PER_STEP_WINDOWED)


def _per_residue(body, rs):
    if rs == 1:
        return body

    def stepped(*refs):
        for r in range(rs):
            body(*[ref.at[r] for ref in refs])

    return stepped


def _attn_specs(dims, dil, width):
    blk = ATTN_BLOCK
    nb = dims.seq // dil // blk
    rs = _attn_residues(dims, dil)
    lead, groups = _attn_groups(dims, dil)
    if rs > 1 and nb == 1:
        grid = (lead, groups // rs)
        at = lambda f: pl.BlockSpec((None, rs, blk, width), lambda b, r: (b, r, 0, 0))
    elif rs > 1:
        grid = (lead, groups // rs, nb)
        at = lambda f: pl.BlockSpec((None, rs, blk, width), lambda b, r, i: (b, r, f(i), 0))
    else:
        grid = (lead, groups, nb)
        at = lambda f: pl.BlockSpec((None, None, blk, width), lambda b, r, i: (b, r, f(i), 0))
    return grid, at(lambda i: i), at(lambda i: jnp.maximum(i - 1, 0)), at(lambda i: jnp.minimum(i + 1, nb - 1))


def _head_slopes(n_heads):
    h = lax.broadcasted_iota(jnp.int32, (n_heads, 1, 1), 0).astype(F32)
    return jnp.exp((h + 1.0) * (-8.0 / n_heads * math.log(2.0)))


def _pair_masks(hd):
    low = lax.broadcasted_iota(jnp.int32, (1, 2 * hd), 1) < hd
    return low, jnp.logical_not(low)


def _attn_fwd(q, k, v, dims, dil, *, name):
    a = dims.n_heads * dims.head_dim
    heads, hd, blk = dims.n_heads, dims.head_dim, ATTN_BLOCK
    assert 2 * hd == LANES and heads % 2 == 0 and heads <= LANES
    nb = dims.seq // dil // blk
    has_prev = nb > 1
    nkeys = 2 * blk if has_prev else blk
    grid, cur, prev, _ = _attn_specs(dims, dil, a)
    _, cur_stat, _, _ = _attn_specs(dims, dil, LANES)

    def body(*refs):
        if has_prev:
            q_ref, kc_ref, vc_ref, kp_ref, vp_ref, o_ref, lse_ref, s_scr, p_scr, k_st, v_st = refs
            k_st[0:blk, :], k_st[blk:, :] = kp_ref[...], kc_ref[...]
            v_st[0:blk, :], v_st[blk:, :] = vp_ref[...], vc_ref[...]
        else:
            q_ref, k_st, v_st, o_ref, lse_ref, s_scr, p_scr = refs
        low, high = _pair_masks(hd)

        for hp in range(heads // 2):
            sl = slice(LANES * hp, LANES * (hp + 1))
            q2 = q_ref[:, sl]
            kcat = k_st[:, sl]
            s_scr[2 * hp] = _dot_nt(jnp.where(low, q2, jnp.zeros_like(q2)), kcat)
            s_scr[2 * hp + 1] = _dot_nt(jnp.where(high, q2, jnp.zeros_like(q2)), kcat)

        iq = lax.broadcasted_iota(jnp.int32, (blk, nkeys), 0)
        jk = lax.broadcasted_iota(jnp.int32, (blk, nkeys), 1)
        if has_prev:
            steps = iq + blk - jk
            valid = (steps >= 0) & (steps <= blk) & ((jk >= blk) | (pl.program_id(len(grid) - 1) > 0))
        else:
            steps = iq - jk
            valid = steps >= 0
        bias = jnp.where(valid, steps.astype(F32) * (-float(dil)), -MASK_BIAS)
        s = s_scr[...] + _head_slopes(heads) * bias[None]
        m = jnp.max(s, axis=-1, keepdims=True)
        p = jnp.exp(s - m)
        l = jnp.sum(p, axis=-1, keepdims=True)
        p_scr[...] = p.astype(BF16)
        inv = 1.0 / l
        lse = m + jnp.log(l)

        lane = lax.broadcasted_iota(jnp.int32, (blk, LANES), 1)
        stat = jnp.zeros((blk, LANES), F32)
        for hp in range(heads // 2):
            sl = slice(LANES * hp, LANES * (hp + 1))
            vcat = v_st[:, sl]
            pv_a = jnp.dot(p_scr[2 * hp], vcat, preferred_element_type=F32) * inv[2 * hp]
            pv_b = jnp.dot(p_scr[2 * hp + 1], vcat, preferred_element_type=F32) * inv[2 * hp + 1]
            o_ref[:, sl] = jnp.where(low, pv_a, pv_b).astype(BF16)
            stat = jnp.where(lane == 2 * hp, lse[2 * hp], stat)
            stat = jnp.where(lane == 2 * hp + 1, lse[2 * hp + 1], stat)
        lse_ref[...] = stat

    q4, k4, v4 = (_attn_array(x, dims, dil) for x in (q, k, v))
    rs = _attn_residues(dims, dil)
    per_step = lambda shape: shape if rs == 1 else (rs,) + shape
    o, lse = pl.pallas_call(
        _per_residue(body, rs), name=name, grid=grid,
        in_specs=[cur, cur, cur] + ([prev, prev] if has_prev else []),
        out_specs=[cur, cur_stat],
        out_shape=[jax.ShapeDtypeStruct(q4.shape, BF16), jax.ShapeDtypeStruct(q4.shape[:-1] + (LANES,), F32)],
        scratch_shapes=[pltpu.VMEM(per_step((heads, blk, nkeys)), F32), pltpu.VMEM(per_step((heads, blk, nkeys)), BF16)]
        + ([pltpu.VMEM(per_step((nkeys, a)), BF16)] * 2 if has_prev else []),
        compiler_params=_params(*["parallel"] * len(grid)),
    )(q4, k4, v4, *([k4, v4] if has_prev else []))
    return o.reshape(q.shape), lse.reshape(q.shape[:-1] + (LANES,))


def _attn_combine(groups, head_spread, dims, *, name, tr=256):
    t = dims.tokens
    a = dims.n_heads * dims.head_dim
    dils = tuple(groups)

    nres = len(RESIDUE_DILATIONS)

    def body(*refs):
        ins = refs[:2 * len(dils)]
        x_ref = refs[2 * len(dils)]
        back_refs = dict(zip(RESIDUE_DILATIONS, refs[2 * len(dils) + 1:2 * len(dils) + 1 + nres]))
        o_ref = refs[2 * len(dils) + 1 + nres]
        lse_refs = refs[2 * len(dils) + 2 + nres:-1]
        scr_stat = refs[-1]
        outs, stats = [], []
        for g, d in enumerate(dils):
            if d == 1:
                outs.append(ins[2 * g][...].astype(F32))
                stats.append(ins[2 * g + 1][...])
            else:
                outs.append(_bf16_residues_to_rows(ins[2 * g], back_refs[d]))
                stats.append(_residues_to_rows(ins[2 * g + 1], scr_stat, d))
        top = functools.reduce(jnp.maximum, stats)
        weights = [jnp.exp(s - top) for s in stats]
        total = functools.reduce(jnp.add, weights)
        joint = top + jnp.log(total)
        inv = 1.0 / total
        acc = None
        for w, o in zip(weights, outs):
            term = _two_pass_dot(w * inv, x_ref[...]) * o
            acc = term if acc is None else acc + term
        o_ref[...] = acc.astype(BF16)
        for g, d in enumerate(dils):
            if d == 1:
                lse_refs[g][...] = joint
            else:
                _rows_to_residues(joint, lse_refs[g], scr_stat, d)

    in_specs, args, lse_specs, lse_shapes = [], [], [], []
    for d in dils:
        if d == 1:
            in_specs += [_row_spec(tr, a), _row_spec(tr, LANES)]
            lse_specs.append(_row_spec(tr, LANES))
            lse_shapes.append(jax.ShapeDtypeStruct((t, LANES), F32))
        else:
            in_specs += [_residue_spec(dims, d, tr, a), _residue_spec(dims, d, tr, LANES)]
            lse_specs.append(_residue_spec(dims, d, tr, LANES))
            lse_shapes.append(_residue_shape(dims, d, LANES, F32))
        args += list(groups[d])
    outs = pl.pallas_call(
        body, name=name, grid=(t // tr,),
        in_specs=in_specs + [pl.BlockSpec((LANES, a), lambda i: (0, 0))] + [pl.BlockSpec((tr, tr), lambda i: (0, 0))] * nres,
        out_specs=[_row_spec(tr, a)] + lse_specs,
        out_shape=[jax.ShapeDtypeStruct((t, a), BF16)] + lse_shapes,
        scratch_shapes=[pltpu.VMEM((1, tr, LANES), F32)],
        compiler_params=_params("parallel"),
    )(*args, head_spread, *[jnp.transpose(p) for p in _residue_permutations(tr)])
    return outs[0], dict(zip(dils, outs[1:]))


def _attn_bwd_prep(dy, w, o, head_sum, dims, *, name, after=None, tr=256):
    t, a = o.shape
    nres = len(RESIDUE_DILATIONS)

    def body(dy_ref, w_ref, o_ref, e_ref, *rest):
        perm_refs, outs, scr_stat = rest[:nres], rest[nres:-1], rest[-1]
        do = _dot_nt(dy_ref[...], w_ref[...]).astype(BF16)
        outs[0][...] = do
        delta = _two_pass_dot(do.astype(F32) * o_ref[...].astype(F32), e_ref[...])
        outs[1][...] = delta
        for g, d in enumerate(RESIDUE_DILATIONS):
            _bf16_rows_to_residues(do, outs[2 + 2 * g], perm_refs[g], d)
            _rows_to_residues(delta, outs[3 + 2 * g], scr_stat, d)

    out_specs = [_row_spec(tr, a), _row_spec(tr, LANES)]
    out_shape = [jax.ShapeDtypeStruct((t, a), BF16), jax.ShapeDtypeStruct((t, LANES), F32)]
    for d in RESIDUE_DILATIONS:
        out_specs += [_residue_spec(dims, d, tr, a), _residue_spec(dims, d, tr, LANES)]
        out_shape += [_residue_shape(dims, d, a, BF16), _residue_shape(dims, d, LANES, F32)]
    n_in = 4 + nres
    body, more_specs, more_args = _ordered(body, n_in, after)
    outs = pl.pallas_call(
        body, name=name, grid=(t // tr,),
        in_specs=[_row_spec(tr, dy.shape[1]), pl.BlockSpec((None,) + w.shape[1:], lambda i: (0, 0, 0)), _row_spec(tr, a),
                  pl.BlockSpec((a, LANES), lambda i: (0, 0))] + [pl.BlockSpec((tr, tr), lambda i: (0, 0))] * nres + more_specs,
        out_specs=out_specs, out_shape=out_shape,
        scratch_shapes=[pltpu.VMEM((1, tr, LANES), F32)],
        compiler_params=_params("parallel"),
    )(dy, w, o, head_sum, *_residue_permutations(tr), *more_args)
    dos, deltas = {1: outs[0]}, {1: outs[1]}
    for g, d in enumerate(RESIDUE_DILATIONS):
        dos[d], deltas[d] = outs[2 + 2 * g], outs[3 + 2 * g]
    return dos, deltas


def _attn_bwd(q, k, v, do, lse, delta, dims, dil, *, name):
    a = dims.n_heads * dims.head_dim
    heads, hd, blk = dims.n_heads, dims.head_dim, ATTN_BLOCK
    nb = dims.seq // dil // blk
    has_next = nb > 1
    nq = 2 * blk if has_next else blk
    grid, cur, _, nxt = _attn_specs(dims, dil, a)
    _, cur_stat, _, nxt_stat = _attn_specs(dims, dil, LANES)

    def body(*refs):
        k_ref, v_ref, q_ref, do_ref, lse_ref, dl_ref = refs[:6]
        if has_next:
            qn_ref, don_ref, lsen_ref, dln_ref = refs[6:10]
            dq_ref, dk_ref, dv_ref, q_st, do_st, s_scr, dp_scr, p_scr, ds_scr, carry = refs[10:]
        else:
            dq_ref, dk_ref, dv_ref, q_st, do_st, s_scr, dp_scr, p_scr, ds_scr = refs[6:]
        j = pl.program_id(len(grid) - 1)
        low, high = _pair_masks(hd)
        q_st[0:blk, :] = q_ref[...]
        do_st[0:blk, :] = do_ref[...]
        if has_next:
            q_st[blk:, :] = qn_ref[...]
            do_st[blk:, :] = don_ref[...]
            lse_all = jnp.concatenate([lse_ref[...], lsen_ref[...]], axis=0)
            dl_all = jnp.concatenate([dl_ref[...], dln_ref[...]], axis=0)
        else:
            lse_all, dl_all = lse_ref[...], dl_ref[...]
        lse_t, dl_t = jnp.transpose(lse_all), jnp.transpose(dl_all)
        lse3 = jnp.stack([lse_t[h:h + 1, :] for h in range(heads)])
        dl3 = jnp.stack([dl_t[h:h + 1, :] for h in range(heads)])

        def halves(x):
            return jnp.where(low, x, jnp.zeros_like(x)), jnp.where(high, x, jnp.zeros_like(x))

        for hp in range(heads // 2):
            sl = slice(LANES * hp, LANES * (hp + 1))
            k2, v2 = k_ref[:, sl], v_ref[:, sl]
            q_a, q_b = halves(q_st[:, sl])
            do_a, do_b = halves(do_st[:, sl])
            s_scr[2 * hp], s_scr[2 * hp + 1] = _dot_nt(k2, q_a), _dot_nt(k2, q_b)
            dp_scr[2 * hp], dp_scr[2 * hp + 1] = _dot_nt(v2, do_a), _dot_nt(v2, do_b)

        jk = lax.broadcasted_iota(jnp.int32, (blk, nq), 0)
        rq = lax.broadcasted_iota(jnp.int32, (blk, nq), 1)
        if has_next:
            iq = jnp.where(rq < blk, rq, rq - blk)
            steps = jnp.where(rq < blk, iq - jk, iq - jk + blk)
            valid = ((rq < blk) & (iq >= jk)) | ((rq >= blk) & (jk >= iq) & (j + 1 < nb))
        else:
            steps, valid = rq - jk, rq >= jk
        bias = jnp.where(valid, steps.astype(F32) * (-float(dil)), -MASK_BIAS)
        p = jnp.exp(s_scr[...] + _head_slopes(heads) * bias[None] - lse3)
        p_scr[...] = p.astype(BF16)
        ds_scr[...] = (p * (dp_scr[...] - dl3)).astype(BF16)

        if has_next:
            @pl.when(j == 0)
            def _():
                carry[...] = jnp.zeros_like(carry)

        for hp in range(heads // 2):
            sl = slice(LANES * hp, LANES * (hp + 1))
            k2 = k_ref[:, sl]
            q_a, q_b = halves(q_st[:, sl])
            do_a, do_b = halves(do_st[:, sl])
            ds_a, ds_b = ds_scr[2 * hp], ds_scr[2 * hp + 1]
            dk_ref[:, sl] = (jnp.dot(ds_a, q_a, preferred_element_type=F32)
                             + jnp.dot(ds_b, q_b, preferred_element_type=F32)).astype(BF16)
            dv_ref[:, sl] = (jnp.dot(p_scr[2 * hp], do_a, preferred_element_type=F32)
                             + jnp.dot(p_scr[2 * hp + 1], do_b, preferred_element_type=F32)).astype(BF16)
            dq2 = jnp.where(low, _dot_tn(ds_a, k2), _dot_tn(ds_b, k2))
            if has_next:
                dq_ref[:, sl] = (carry[:, sl] + dq2[:blk]).astype(BF16)
                carry[:, sl] = dq2[blk:]
            else:
                dq_ref[:, sl] = dq2.astype(BF16)

    args, in_specs = [_attn_array(x, dims, dil) for x in (k, v, q, do, lse, delta)], [cur] * 4 + [cur_stat] * 2
    if has_next:
        args += [args[2], args[3], args[4], args[5]]
        in_specs += [nxt] * 2 + [nxt_stat] * 2
    shape = jax.ShapeDtypeStruct(args[2].shape, BF16)
    rs = _attn_residues(dims, dil)
    per_step = lambda dims_: dims_ if rs == 1 else (rs,) + dims_
    scratch = ([pltpu.VMEM(per_step((nq, a)), BF16)] * 2 + [pltpu.VMEM(per_step((heads, blk, nq)), F32)] * 2
               + [pltpu.VMEM(per_step((heads, blk, nq)), BF16)] * 2)
    if has_next:
        scratch.append(pltpu.VMEM(per_step((blk, a)), F32))
    grads = pl.pallas_call(
        _per_residue(body, rs), name=name, grid=grid, in_specs=in_specs, out_specs=[cur] * 3, out_shape=[shape] * 3,
        scratch_shapes=scratch,
        compiler_params=_params(*["parallel"] * (len(grid) - 1), "arbitrary"),
    )(*args)
    return tuple(g.reshape(q.shape) for g in grads)


def _qkv_layouts_bwd(z, grads, gq, gk, head_ones, dims, *, name, tr=256):
    t = z.shape[0]
    a = dims.n_heads * dims.head_dim
    q_scale = dims.head_dim ** -0.5
    dils = tuple(grads)
    nres = len(RESIDUE_DILATIONS)

    def body(q_ref, k_ref, *rest):
        d_refs = rest[:3 * len(dils)]
        gq_ref, gk_ref, sum_ref, spread_ref = rest[3 * len(dils):3 * len(dils) + 4]
        back_refs = dict(zip(RESIDUE_DILATIONS, rest[3 * len(dils) + 4:3 * len(dils) + 4 + nres]))
        dz_ref, dgq_ref, dgk_ref = rest[3 * len(dils) + 4 + nres:]
        first = pl.program_id(0) == 0
        mean = lambda val: _two_pass_dot(_two_pass_dot(val, sum_ref[...]), spread_ref[...]) * (1.0 / dims.head_dim)

        def total(j):
            acc = None
            for g, d in enumerate(dils):
                ref = d_refs[3 * g + j]
                part = ref[...].astype(F32) if d == 1 else _bf16_residues_to_rows(ref, back_refs[d])
                acc = part if acc is None else acc + part
            return acc

        def norm_bwd(x_ref, dy, g_ref, scale, col, dg_ref):
            xv = x_ref[...].astype(F32)
            dy = dy * scale
            r = lax.rsqrt(mean(xv * xv) + RMS_EPS)
            gy = dy * g_ref[...]
            dx = r * gy - xv * (r * r * r) * mean(xv * gy)
            dz_ref[:, col * a:(col + 1) * a] = dx.astype(BF16)
            _accumulate(dg_ref, jnp.sum(dy * xv * r, axis=0, keepdims=True), first)

        norm_bwd(q_ref, total(0), gq_ref, q_scale, 0, dgq_ref)
        norm_bwd(k_ref, total(1), gk_ref, 1.0, 1, dgk_ref)
        dz_ref[:, 2 * a:3 * a] = total(2).astype(BF16)

    in_specs, args = [_row_spec(tr, a, 2), _row_spec(tr, a, 3)], [z, z]
    for d in dils:
        in_specs += [_row_spec(tr, a) if d == 1 else _residue_spec(dims, d, tr, a)] * 3
        args += list(grads[d])
    in_specs += [_vec_spec(a), _vec_spec(a), pl.BlockSpec((a, LANES), lambda i: (0, 0)),
                 pl.BlockSpec((LANES, a), lambda i: (0, 0))] + [pl.BlockSpec((tr, tr), lambda i: (0, 0))] * nres
    return pl.pallas_call(
        body, name=name, grid=(t // tr,), in_specs=in_specs,
        out_specs=[_row_spec(tr, 3 * a), _vec_spec(a), _vec_spec(a)],
        out_shape=[jax.ShapeDtypeStruct((t, 3 * a), BF16)] + [jax.ShapeDtypeStruct((1, a), F32)] * 2,
        compiler_params=_params("arbitrary"),
    )(*args, gq, gk, *head_ones, *[jnp.transpose(p) for p in _residue_permutations(tr)])


def _mix_fwd(a3, o, w_a, w_b, w_out, z, gate_b, x, g2, dims, *, name, tr=512):
    t, d = x.shape
    tr = _pick(t, tr, 8)
    first_gate_col = z.shape[1] // d - 2

    def body(a_ref, o_ref, wa_ref, wb_ref, wo_ref, ga_ref, gb_ref, ba_ref, bb_ref, x_ref, g2_ref,
             ya_ref, yb_ref, mix_ref, x1_ref, h2_ref):
        ya = jnp.dot(a_ref[...], wa_ref[...], preferred_element_type=F32)
        yb = jnp.dot(o_ref[...], wb_ref[...], preferred_element_type=F32)
        ya_ref[...] = ya
        yb_ref[...] = yb
        g_a = _sigmoid(ga_ref[...].astype(F32) + ba_ref[...])
        g_b = _sigmoid(gb_ref[...].astype(F32) + bb_ref[...])
        mixed = (g_a * ya + g_b * yb).astype(BF16)
        mix_ref[...] = mixed
        x1 = x_ref[...] + jnp.dot(mixed, wo_ref[...], preferred_element_type=F32)
        x1_ref[...] = x1
        h2_ref[...] = (x1 * lax.rsqrt(jnp.mean(x1 * x1, axis=-1, keepdims=True) + RMS_EPS) * g2_ref[...]).astype(BF16)

    weight = pl.BlockSpec((None, d, d), lambda i: (0, 0, 0))
    rows = _row_spec(tr, d)
    return pl.pallas_call(
        body, name=name, grid=(t // tr,),
        in_specs=[rows, rows, weight, weight, weight, _row_spec(tr, d, first_gate_col),
                  _row_spec(tr, d, first_gate_col + 1), _vec_spec(d, 0), _vec_spec(d, 1), rows, _vec_spec(d)],
        out_specs=[rows] * 5,
        out_shape=[jax.ShapeDtypeStruct((t, d), dt) for dt in (F32, F32, BF16, F32, BF16)],
        compiler_params=_params("parallel"),
    )(a3, o, w_a, w_b, w_out, z, z, gate_b, gate_b, x, g2)


def _mix_bwd(dx, w, ya, yb, z, gate_b, dims, *, name, after=None, tr=512):
    t, d = ya.shape
    tr = _pick(t, tr, 8)
    first_gate_col = z.shape[1] // d - 2

    def body(dx_ref, w_ref, ya_ref, yb_ref, ga_ref, gb_ref, ba_ref, bb_ref, dya_ref, dyb_ref, dz_ref, db_ref):
        dm = _dot_nt(dx_ref[...], w_ref[...])
        g_a = _sigmoid(ga_ref[...].astype(F32) + ba_ref[...])
        g_b = _sigmoid(gb_ref[...].astype(F32) + bb_ref[...])
        dya_ref[...] = (dm * g_a).astype(BF16)
        dyb_ref[...] = (dm * g_b).astype(BF16)
        dl_a = dm * ya_ref[...] * g_a * (1.0 - g_a)
        dl_b = dm * yb_ref[...] * g_b * (1.0 - g_b)
        dz_ref[:, 0:d] = dl_a.astype(BF16)
        dz_ref[:, d:2 * d] = dl_b.astype(BF16)
        first = pl.program_id(0) == 0
        sums = jnp.concatenate([jnp.sum(dl_a, axis=0, keepdims=True), jnp.sum(dl_b, axis=0, keepdims=True)], axis=1)
        _accumulate(db_ref, sums, first)

    body, more_specs, more_args = _ordered(body, 8, after)
    return pl.pallas_call(
        body, name=name, grid=(t // tr,),
        in_specs=[_row_spec(tr, d), pl.BlockSpec((None, d, d), lambda i: (0, 0, 0)), _row_spec(tr, d), _row_spec(tr, d),
                  _row_spec(tr, d, first_gate_col), _row_spec(tr, d, first_gate_col + 1), _vec_spec(d, 0),
                  _vec_spec(d, 1)] + more_specs,
        out_specs=[_row_spec(tr, d), _row_spec(tr, d), _row_spec(tr, 2 * d), _vec_spec(2 * d)],
        out_shape=[jax.ShapeDtypeStruct((t, d), BF16)] * 2 + [jax.ShapeDtypeStruct((t, 2 * d), BF16),
                                                              jax.ShapeDtypeStruct((1, 2 * d), F32)],
        compiler_params=_params("arbitrary"),
    )(dx, w, ya, yb, z, z, gate_b, gate_b, *more_args)


def _adamw(w, grads, m, v, *, name, tr=256):
    r, c = w.shape
    tr = _pick(r, tr, 8)
    ng = len(grads)
    c1 = 1.0 - ADAM_B1 ** ADAM_STEP
    c2 = 1.0 - ADAM_B2 ** ADAM_STEP

    def body(*refs):
        w_ref, g_refs, m_ref, v_ref = refs[0], refs[1:1 + ng], refs[1 + ng], refs[2 + ng]
        g_out, d_out, m_out, v_out = refs[3 + ng:]
        g = g_refs[0][...]
        for extra in g_refs[1:]:
            g = g + extra[...]
        m_new = ADAM_B1 * m_ref[...] + (1.0 - ADAM_B1) * g
        v_new = ADAM_B2 * v_ref[...] + (1.0 - ADAM_B2) * (g * g)
        g_out[...] = g
        m_out[...] = m_new
        v_out[...] = v_new
        d_out[...] = -ADAM_LR * ((m_new / c1) / (jnp.sqrt(v_new / c2) + ADAM_EPS) + ADAM_WD * w_ref[...])

    spec = pl.BlockSpec((tr, c), lambda i: (i, 0))
    return pl.pallas_call(
        body, name=name, grid=(r // tr,),
        in_specs=[spec] * (3 + ng), out_specs=[spec] * 4, out_shape=[jax.ShapeDtypeStruct((r, c), F32)] * 4,
        compiler_params=_params("parallel"),
    )(w, *grads, m, v)


CHIP_PEERS = ((1, 0), (0, 1), (1, 1))


def _place():
    return lax.axis_index("x"), lax.axis_index("y"), lax.axis_index("c")


HBM = pl.BlockSpec(memory_space=pltpu.HBM)
SEM = pl.BlockSpec(memory_space=pltpu.SEMAPHORE)
IN_FLIGHT = pltpu.SideEffectType.DATAFLOW_SIDE_EFFECTING


def _in_hbm(a):
    return pltpu.with_memory_space_constraint(a, pltpu.HBM)


def _cast_to_lands(shards, dtypes, *, name, after=None):
    n = len(shards)

    def body(*refs):
        ins, outs, bufs, sems = refs[:n], refs[n:2 * n], refs[2 * n:3 * n], refs[3 * n]
        x, y, _ = _place()
        copies = []
        for a in range(n):
            bufs[a][...] = ins[a][...].astype(dtypes[a])
            cp = pltpu.make_async_copy(bufs[a], outs[a].at[2 * x + y], sems.at[a])
            cp.start()
            copies.append(cp)
        for cp in copies:
            cp.wait()

    body, more_specs, more_args = _ordered(body, n, after)
    return pl.pallas_call(
        body, name=name, in_specs=[pl.BlockSpec(memory_space=pltpu.VMEM)] * n + more_specs, out_specs=[ANY] * n,
        out_shape=[jax.ShapeDtypeStruct((N_CHIPS,) + s.shape, dt) for s, dt in zip(shards, dtypes)],
        scratch_shapes=[pltpu.VMEM(s.shape, dt) for s, dt in zip(shards, dtypes)] + [pltpu.SemaphoreType.DMA((n,))],
        compiler_params=pltpu.CompilerParams(vmem_limit_bytes=V7X_VMEM_LIMIT_BYTES),
    )(*shards, *more_args)


def _chip_copy(src, dst, send, recv, flip, place):
    x, y, c = place
    return pltpu.make_async_remote_copy(src_ref=src, dst_ref=dst, send_sem=send, recv_sem=recv,
                                        device_id=(x ^ flip[0], y ^ flip[1], c), device_id_type=MESH)


def _my_part(land, place, halved):
    block = land.at[2 * place[0] + place[1]]
    if not halved:
        return block
    rows = land.shape[1] // 2
    return block.at[pl.ds(pl.multiple_of(place[2] * rows, rows), rows)]


def _gather_start(lands, after, *, name, halved=()):
    n = len(lands)

    def body(*refs):
        ins, send, recv, token = refs[:n], refs[n + 1], refs[n + 2], refs[-1]
        place = _place()
        for a in range(n):
            part = _my_part(ins[a], place, a in halved)
            for p, flip in enumerate(CHIP_PEERS):
                k = 3 * a + p
                _chip_copy(part, part, send.at[k], recv.at[k], flip, place).start()
        token[...] = jnp.zeros_like(token)

    outs = pl.pallas_call(
        body, name=name, in_specs=[HBM] * n + [ANY],
        out_specs=(SEM, SEM, *[HBM] * n, pl.BlockSpec(memory_space=pltpu.VMEM)),
        out_shape=(pltpu.SemaphoreType.DMA((3 * n,)), pltpu.SemaphoreType.DMA((3 * n,)),
                   *[pltpu.HBM(l.shape, l.dtype) for l in lands], jax.ShapeDtypeStruct((8, 128), F32)),
        input_output_aliases={a: 2 + a for a in range(n)},
        compiler_params=pltpu.CompilerParams(has_side_effects=IN_FLIGHT),
    )(*[_in_hbm(l) for l in lands], after)
    return outs[0], outs[1], list(outs[2:2 + n]), outs[-1]


def _gather_wait(send, recv, lands, after, *, name, halved=()):
    n = len(lands)

    def body(*refs):
        ins, send_ref, recv_ref = refs[:n], refs[n], refs[n + 1]
        place = _place()
        for a in range(n):
            part = _my_part(ins[a], place, a in halved)
            for p, flip in enumerate(CHIP_PEERS):
                k = 3 * a + p
                cp = _chip_copy(part, part, send_ref.at[k], recv_ref.at[k], flip, place)
                cp.wait_send()
                cp.wait_recv()

    after = list(after) if isinstance(after, (list, tuple)) else [after]
    return pl.pallas_call(
        body, name=name, in_specs=[HBM] * n + [SEM, SEM] + [ANY] * len(after), out_specs=[HBM] * n,
        out_shape=[pltpu.HBM(l.shape, l.dtype) for l in lands],
        input_output_aliases={a: a for a in range(n)},
        compiler_params=pltpu.CompilerParams(has_side_effects=IN_FLIGHT),
    )(*lands, send, recv, *after)


def _forward_to_sibling(land, *, name):
    rows = land.shape[1] // 2

    def body(land_ref, out_ref, send, recv):
        x, y, c = _place()
        copies = []
        for p, (fx, fy) in enumerate(CHIP_PEERS):
            chip = 2 * (x ^ fx) + (y ^ fy)
            mine = pl.ds(pl.multiple_of(c * rows, rows), rows)
            theirs = pl.ds(pl.multiple_of((1 - c) * rows, rows), rows)
            out = pltpu.make_async_remote_copy(
                src_ref=land_ref.at[chip].at[mine], dst_ref=out_ref.at[chip].at[mine], send_sem=send.at[p],
                recv_sem=recv.at[p], device_id=(x, y, 1 - c), device_id_type=MESH)
            out.start()
            copies.append((out, pltpu.make_async_remote_copy(
                src_ref=land_ref.at[chip].at[theirs], dst_ref=out_ref.at[chip].at[theirs], send_sem=send.at[p],
                recv_sem=recv.at[p], device_id=(x, y, 1 - c), device_id_type=MESH)))
        for out, arriving in copies:
            out.wait_send()
            arriving.wait_recv()

    return pl.pallas_call(
        body, name=name, in_specs=[ANY], out_specs=ANY, out_shape=jax.ShapeDtypeStruct(land.shape, land.dtype),
        input_output_aliases={0: 0},
        scratch_shapes=[pltpu.SemaphoreType.DMA((3,)), pltpu.SemaphoreType.DMA((3,))],
    )(land)


def _scatter_start(grad, *, name):
    def body(g_ref, land_ref, send, recv, g_thru, land_thru, token):
        place = _place()
        for p, flip in enumerate(CHIP_PEERS):
            peer_chip = 2 * (place[0] ^ flip[0]) + (place[1] ^ flip[1])
            _chip_copy(g_ref.at[peer_chip], land_ref.at[p], send.at[p], recv.at[p], flip, place).start()
        token[...] = jnp.zeros_like(token)

    land = lax.empty((3,) + grad.shape[1:], grad.dtype)
    return pl.pallas_call(
        body, name=name, in_specs=[HBM, HBM],
        out_specs=(SEM, SEM, HBM, HBM, pl.BlockSpec(memory_space=pltpu.VMEM)),
        out_shape=(pltpu.SemaphoreType.DMA((3,)), pltpu.SemaphoreType.DMA((3,)), pltpu.HBM(grad.shape, grad.dtype),
                   pltpu.HBM(land.shape, land.dtype), jax.ShapeDtypeStruct((8, 128), F32)),
        input_output_aliases={0: 2, 1: 3},
        compiler_params=pltpu.CompilerParams(has_side_effects=IN_FLIGHT),
    )(_in_hbm(grad), _in_hbm(land))


def _scatter_wait(started, after, *, name):
    n = len(started)

    def body(*refs):
        grads, lands = refs[:n], refs[n:2 * n]
        sends, recvs = refs[2 * n:3 * n], refs[3 * n:4 * n]
        place = _place()
        for a in range(n):
            for p, flip in enumerate(CHIP_PEERS):
                cp = _chip_copy(grads[a].at[0], lands[a].at[p], sends[a].at[p], recvs[a].at[p], flip, place)
                cp.wait_send()
                cp.wait_recv()

    grads, lands = [s[2] for s in started], [s[3] for s in started]
    after = list(after) if isinstance(after, (list, tuple)) else [after]
    outs = pl.pallas_call(
        body, name=name, in_specs=[HBM] * (2 * n) + [SEM] * (2 * n) + [ANY] * len(after), out_specs=[HBM] * (2 * n),
        out_shape=[pltpu.HBM(a.shape, a.dtype) for a in grads + lands],
        input_output_aliases={a: a for a in range(2 * n)},
        compiler_params=pltpu.CompilerParams(has_side_effects=IN_FLIGHT),
    )(*grads, *lands, *[s[0] for s in started], *[s[1] for s in started], *after)
    return list(zip(outs[:n], outs[n:]))


def _sibling_copy(src, dst, send, recv, place):
    x, y, c = place
    return pltpu.make_async_remote_copy(src_ref=src, dst_ref=dst, send_sem=send, recv_sem=recv,
                                        device_id=(x, y, 1 - c), device_id_type=MESH)


def _swap_start(arrays, *, name):
    n = len(arrays)

    def body(*refs):
        ins, lands, send, recv, token = refs[:n], refs[n:2 * n], refs[2 * n], refs[2 * n + 1], refs[-1]
        place = _place()
        for a in range(n):
            _sibling_copy(ins[a], lands[a], send.at[a], recv.at[a], place).start()
        token[...] = jnp.zeros_like(token)

    both = [_in_hbm(a) for a in arrays] + [_in_hbm(lax.empty(a.shape, a.dtype)) for a in arrays]
    outs = pl.pallas_call(
        body, name=name, in_specs=[HBM] * (2 * n),
        out_specs=(SEM, SEM, *[HBM] * (2 * n), pl.BlockSpec(memory_space=pltpu.VMEM)),
        out_shape=(pltpu.SemaphoreType.DMA((n,)), pltpu.SemaphoreType.DMA((n,)),
                   *[pltpu.HBM(a.shape, a.dtype) for a in both], jax.ShapeDtypeStruct((8, 128), F32)),
        input_output_aliases={a: 2 + a for a in range(2 * n)},
        compiler_params=pltpu.CompilerParams(has_side_effects=IN_FLIGHT),
    )(*both)
    return outs[0], outs[1], list(outs[2:2 + n]), list(outs[2 + n:2 + 2 * n]), outs[-1]


def _swap_wait(started, after, *, name):
    send, recv, arrays, lands = started[:4]
    n = len(arrays)

    def body(*refs):
        ins, zones, send_ref, recv_ref = refs[:n], refs[n:2 * n], refs[2 * n], refs[2 * n + 1]
        place = _place()
        for a in range(n):
            cp = _sibling_copy(ins[a], zones[a], send_ref.at[a], recv_ref.at[a], place)
            cp.wait_send()
            cp.wait_recv()

    after = list(after) if isinstance(after, (list, tuple)) else [after]
    outs = pl.pallas_call(
        body, name=name, in_specs=[HBM] * (2 * n) + [SEM, SEM] + [ANY] * len(after), out_specs=[HBM] * (2 * n),
        out_shape=[pltpu.HBM(a.shape, a.dtype) for a in arrays + lands],
        input_output_aliases={a: a for a in range(2 * n)},
        compiler_params=pltpu.CompilerParams(has_side_effects=IN_FLIGHT),
    )(*arrays, *lands, send, recv, *after)
    return list(outs[:n]), list(outs[n:])


def _allreduce_start(packed, *, name):
    n_dev = 8

    def body(src_ref, land_ref, send, recv, src_thru, land_thru, token):
        x, y, c = _place()
        me = 4 * x + 2 * y + c
        for p in range(1, n_dev):
            pltpu.make_async_remote_copy(
                src_ref=src_ref, dst_ref=land_ref.at[me], send_sem=send.at[p - 1], recv_sem=recv.at[p - 1],
                device_id=(x ^ (p >> 2), y ^ ((p >> 1) & 1), c ^ (p & 1)), device_id_type=MESH).start()
        token[...] = jnp.zeros_like(token)

    land = lax.empty((n_dev,) + packed.shape, packed.dtype)
    return pl.pallas_call(
        body, name=name, in_specs=[HBM, HBM],
        out_specs=(SEM, SEM, HBM, HBM, pl.BlockSpec(memory_space=pltpu.VMEM)),
        out_shape=(pltpu.SemaphoreType.DMA((n_dev - 1,)), pltpu.SemaphoreType.DMA((n_dev - 1,)),
                   pltpu.HBM(packed.shape, packed.dtype), pltpu.HBM(land.shape, land.dtype),
                   jax.ShapeDtypeStruct((8, 128), F32)),
        input_output_aliases={0: 2, 1: 3},
        compiler_params=pltpu.CompilerParams(has_side_effects=IN_FLIGHT),
    )(_in_hbm(packed), _in_hbm(land))


def _allreduce_wait(started, after, *, name):
    send, recv, packed, land = started[:4]
    n_dev = 8

    def body(src_ref, land_ref, send_ref, recv_ref, *_):
        x, y, c = _place()
        for p in range(1, n_dev):
            cp = pltpu.make_async_remote_copy(
                src_ref=src_ref, dst_ref=land_ref.at[0], send_sem=send_ref.at[p - 1], recv_sem=recv_ref.at[p - 1],
                device_id=(x ^ (p >> 2), y ^ ((p >> 1) & 1), c ^ (p & 1)), device_id_type=MESH)
            cp.wait_send()
            cp.wait_recv()

    after = list(after) if isinstance(after, (list, tuple)) else [after]
    return pl.pallas_call(
        body, name=name, in_specs=[HBM, HBM, SEM, SEM] + [ANY] * len(after), out_specs=[HBM, HBM],
        out_shape=[pltpu.HBM(packed.shape, packed.dtype), pltpu.HBM(land.shape, land.dtype)],
        input_output_aliases={0: 0, 1: 1},
        compiler_params=pltpu.CompilerParams(has_side_effects=IN_FLIGHT),
    )(packed, land, send, recv, *after)


def _sum_devices(mine, land, *, name):
    n_dev = land.shape[0]

    def body(mine_ref, land_ref, out_ref):
        x, y, c = _place()
        me = 4 * x + 2 * y + c
        total = None
        for s in range(n_dev):
            part = jnp.where(me == s, mine_ref[...], land_ref[s])
            total = part if total is None else total + part
        out_ref[...] = total

    return pl.pallas_call(body, name=name, out_shape=jax.ShapeDtypeStruct(mine.shape, mine.dtype))(mine, land)


def _sum_received(grad, land, *, name, tr=256):
    _, r, c = grad.shape
    tr = _pick(r, tr, 8)

    def body(chip_ref, g_ref, l_ref, o_ref):
        o_ref[...] = ((g_ref[...] + l_ref[0].astype(F32)) + l_ref[1].astype(F32)) + l_ref[2].astype(F32)

    chip = (2 * lax.axis_index("x") + lax.axis_index("y")).astype(jnp.int32).reshape(1)
    return pl.pallas_call(
        body, name=name,
        grid_spec=pltpu.PrefetchScalarGridSpec(
            num_scalar_prefetch=1, grid=(r // tr,),
            in_specs=[pl.BlockSpec((None, tr, c), lambda i, chip_ref: (chip_ref[0], i, 0)),
                      pl.BlockSpec((3, tr, c), lambda i, chip_ref: (0, i, 0))],
            out_specs=pl.BlockSpec((tr, c), lambda i, chip_ref: (i, 0))),
        out_shape=jax.ShapeDtypeStruct((r, c), F32), compiler_params=_params("parallel"),
    )(chip, grad, land)


def _packed_rows(size, d):
    return -(-size // (8 * d)) * 8


def _pack_rows(arrays, d):
    rows = []
    for arr in arrays:
        flat = arr.reshape(-1).astype(F32)
        n = _packed_rows(flat.shape[0], d)
        rows.append(jnp.pad(flat, (0, n * d - flat.shape[0])).reshape(n, d))
    return jnp.concatenate(rows, axis=0)


def _unpack_rows(packed, shapes, d):
    out, row = [], 0
    for shape in shapes:
        size = math.prod(shape)
        n = _packed_rows(size, d)
        out.append(packed[row:row + n].reshape(-1)[:size].reshape(shape))
        row += n
    return out


SMALL = ("norm1_g", "gate_b", "conv_b", "conv_norm_g", "q_norm_g", "k_norm_g", "norm2_g", "ffn_conv_b")
LARGE = ("w_in", "w_conv_out", "w_attn_out", "w_out", "w_up", "w_down")
WEIGHTS = ("norm1_g", "w_in", "gate_b", "conv_w", "conv_b", "conv_norm_g", "w_conv_out", "q_norm_g", "k_norm_g",
           "w_attn_out", "w_out", "norm2_g", "w_up", "ffn_conv_w", "ffn_conv_b", "w_down")


def _after(vec, token):
    return vec if token is None else vec + token[0:1, 0:1]


def _local_step(dims, x, target, small, first_weights, other_weights, send_grad):
    d, f, heads = dims.d_model, dims.d_ff, dims.n_heads
    small = dict(small)
    row = lambda name: small[name].reshape(1, -1)
    head_sum = _head_sum_matrix(dims)
    head_spread = jnp.transpose(head_sum)
    ones = (head_sum, head_spread)
    gq = jnp.tile(row("q_norm_g"), (1, heads))
    gk = jnp.tile(row("k_norm_g"), (1, heads))
    one_shard = lambda w: w.reshape(1, -1, w.shape[-1])

    h = _rmsnorm_fwd(x, row("norm1_g"), name="norm1")
    full = first_weights(h)
    w_in = full["w_in"]
    conv_w = jnp.pad(full["conv_w"], ((0, CONV_HALO - dims.conv_width), (0, 0)))
    ffn_w = jnp.pad(full["ffn_conv_w"], ((0, FFN_HALO - dims.ffn_conv_width), (0, 0)))
    z = _mm_nn(h, w_in, out_dtype=BF16, after=full.get("token"), tm=2048, tn=1792, name="in_proj")
    a1, a3 = _conv_branch_fwd(z, conv_w, row("conv_b"), row("conv_norm_g"), dims, name="conv_branch")
    qkv = _qkv_layouts_fwd(z, gq, gk, ones, dims, name="qk_norm")
    per_group = {dil: _attn_fwd(*qkv[dil], dims, dil, name=f"attn_fwd_d{dil}") for dil in DILATIONS}
    o, lse = _attn_combine(per_group, head_spread, dims, name="attn_combine")
    full = other_weights(o)
    w_up = full["w_up"]
    w_co, w_ao, w_o, w_dn = (one_shard(full[k]) for k in ("w_conv_out", "w_attn_out", "w_out", "w_down"))
    ya, yb, mixed, x1, h2 = _mix_fwd(a3, o, w_co, w_ao, w_o, z, row("gate_b"), x, row("norm2_g"), dims,
                                     name="branch_projs_mix_out_proj_norm2")
    up = _mm_nn(h2, w_up, out_dtype=F32, tm=2048, name="up_proj")
    act = _ffn_act_fwd(up, ffn_w, row("ffn_conv_b"), dims, name="ffn_act")
    dy, dy_b, loss = _proj_residual_loss(act, w_dn, x1, target, tm=512, name="down_proj_loss")

    grads = {}

    def large(name, g):
        grads[name], g_bf16 = g
        return send_grad(name, g_bf16)

    sent = large("w_down", _mm_tn(act, dy_b, n_shards=1, name="dw_down"))
    dact = _mm_nt(dy_b, w_dn, out_dtype=BF16, after=sent, name="d_act")
    dup, dfw, dfb = _ffn_bwd(dact, up, ffn_w, row("ffn_conv_b"), dims, name="ffn_bwd")
    grads["ffn_conv_w"], grads["ffn_conv_b"] = dfw[:dims.ffn_conv_width], dfb
    sent = large("w_up", _mm_tn(h2, dup, n_shards=N_CHIPS, name="dw_up"))
    dx1, dx1_b, grads["norm2_g"] = _mm_nt_rmsnorm_bwd(dup, w_up, x1, row("norm2_g"), dy, want_bf16=True, after=sent,
                                                     name="d_h2_norm2_bwd")
    sent = large("w_out", _mm_tn(mixed, dx1_b, n_shards=1, name="dw_out"))
    dya, dyb, dz_gate, grads["gate_b"] = _mix_bwd(dx1_b, w_o, ya, yb, z, row("gate_b"), dims, after=sent,
                                                  name="d_mix_gate_mix_bwd")
    sent = large("w_attn_out", _mm_tn(o, dyb, n_shards=1, name="dw_attn_out"))
    dos, deltas = _attn_bwd_prep(dyb, w_ao, o, head_sum, dims, after=sent, name="d_attn_bwd_prep")
    dqkv = {dil: _attn_bwd(*qkv[dil], dos[dil], lse[dil], deltas[dil], dims, dil, name=f"attn_bwd_d{dil}")
            for dil in DILATIONS}
    dz_qkv, dgq, dgk = _qkv_layouts_bwd(z, dqkv, gq, gk, ones, dims, name="qk_norm_bwd")
    grads["q_norm_g"] = dgq.reshape(heads, dims.head_dim).sum(axis=0)
    grads["k_norm_g"] = dgk.reshape(heads, dims.head_dim).sum(axis=0)
    sent = large("w_conv_out", _mm_tn(a3, dya, n_shards=1, name="dw_conv_out"))
    da1, grads["conv_norm_g"] = _mm_nt_rmsnorm_bwd(dya, w_co, a1, row("conv_norm_g"), None, want_bf16=False, silu=True,
                                                   after=sent, name="d_conv_act_norm_bwd")
    dz, dcw, grads["conv_b"] = _conv_branch_bwd(da1, z, conv_w, [dz_qkv, dz_gate], dims, name="conv_branch_bwd")
    grads["conv_w"] = dcw[:dims.conv_width]
    sent = large("w_in", _mm_tn(h, dz, n_shards=N_CHIPS, name="dw_in"))
    dx, grads["norm1_g"] = _mm_nt_rmsnorm_bwd(dz, w_in, x, row("norm1_g"), dx1, want_bf16=False, after=sent,
                                              name="d_h_norm1_bwd")
    return loss, dx, grads


def _step(dims, x, target, w, m, v):
    d = dims.d_model
    t = dims.tokens
    sq = lambda a: a.reshape(a.shape[1:])
    w2, m2, v2 = ({k: sq(a) for k, a in grp.items()} for grp in (w, m, v))

    conv_pad = jnp.pad(w2["conv_w"], ((0, CONV_HALO - dims.conv_width), (0, 0)))
    ffn_pad = jnp.pad(w2["ffn_conv_w"], ((0, FFN_HALO - dims.ffn_conv_width), (0, 0)))
    first_names = ("w_in", "conv_w", "ffn_conv_w")
    other_names = tuple(k for k in LARGE if k not in first_names)
    lands = dict(zip(first_names, _cast_to_lands([w2["w_in"], conv_pad, ffn_pad], [BF16, F32, F32], name="cast_first")))
    first = _gather_start([lands[k] for k in first_names], x, halved=(0,), name="gather_start_first")
    lands.update(zip(other_names, _cast_to_lands([w2[k] for k in other_names], [BF16] * len(other_names),
                                                 after=first[3], name="cast_other")))
    other = []
    cols = lambda g, rows: jnp.moveaxis(g, 0, 1).reshape(g.shape[1], -1)[:rows]

    def first_weights(after):
        got = dict(zip(first_names, _gather_wait(*first[:3], [after] + [lands[k] for k in other_names], halved=(0,),
                                                 name="gather_wait_first")))
        got["w_in"] = _forward_to_sibling(got["w_in"], name="forward_w_in")
        other.extend(_gather_start([lands[k] for k in other_names], got["w_in"], name="gather_start_other"))
        got["conv_w"] = cols(got["conv_w"], dims.conv_width)
        got["ffn_conv_w"] = cols(got["ffn_conv_w"], dims.ffn_conv_width)
        got["token"] = other[3]
        return got

    def other_weights(after):
        return dict(zip(other_names, _gather_wait(*other[:3], after, name="gather_wait_other")))

    started = {}

    def send_grad(name, g):
        send, recv, g_thru, land, token = _scatter_start(g.reshape(N_CHIPS, -1, g.shape[-1]), name=f"scatter_start_{name}")
        started[name] = (send, recv, g_thru, land)
        return token

    small = {k: w2[k] for k in SMALL}
    small["norm1_g"] = _after(small["norm1_g"].reshape(1, -1), first[3])
    loss, dx, grads = _local_step(dims, x.reshape(t, d), target.reshape(t, d), small, first_weights, other_weights, send_grad)

    def my_sums(names, after, tag):
        arrived = _scatter_wait([started[k] for k in names], after, name=f"scatter_wait_{tag}")
        blocks = [grads[k].reshape(N_CHIPS, -1, grads[k].shape[-1]) for k in names]
        return [_sum_received(g, land, name=f"sum_{k}") for k, g, (_, land) in zip(names, blocks, arrived)]

    def updates(names, mine, theirs):
        return {k: _adamw(w2[k], [a, b], m2[k], v2[k], name=f"adamw_{k}") for k, a, b in zip(names, mine, theirs)}

    small_names = SMALL + ("conv_w", "ffn_conv_w")
    packed = _pack_rows([grads[k] for k in small_names] + [loss[0, 0]], d)
    reducing = _allreduce_start(packed, name="allreduce_start")
    others = [k for k in LARGE if k != "w_in"]
    mine_others = my_sums(others, [dx, reducing[4]], "others")
    swapping_others = _swap_start(mine_others, name="swap_start_others")
    mine_w_in = my_sums(["w_in"], swapping_others[4], "w_in")
    swapping_w_in = _swap_start(mine_w_in, name="swap_start_w_in")
    out = updates(others, *_swap_wait(swapping_others, swapping_w_in[4], name="swap_wait_others"))
    last_updates = [out[k][1] for k in others]
    reduced = _sum_devices(*_allreduce_wait(reducing, last_updates, name="allreduce_wait"), name="allreduce_sum")
    shapes = [grads[k].shape for k in small_names] + [()]
    *small_g, loss_total = _unpack_rows(reduced, shapes, d)
    small_g = dict(zip(small_names, small_g))
    chip = 2 * lax.axis_index("x") + lax.axis_index("y")
    for k in ("conv_w", "ffn_conv_w"):
        width = w2[k].shape[1]
        small_g[k] = lax.dynamic_slice_in_dim(small_g[k], chip * width, width, axis=1)

    small_shapes = [w2[k].shape for k in small_names]
    pack = lambda grp: _pack_rows([grp[k] for k in small_names], d)
    results = _adamw(pack(w2), [pack(small_g)], pack(m2), pack(v2), name="adamw_small")
    unpacked = [_unpack_rows(r, small_shapes, d) for r in results]
    out.update({k: tuple(u[i] for u in unpacked) for i, k in enumerate(small_names)})
    out.update(updates(["w_in"], *_swap_wait(swapping_w_in, results[1], name="swap_wait_w_in")))

    lead = lambda a: a.reshape((1,) + a.shape)
    ordered = [[lead(out[k][j].reshape(w2[k].shape)) for k in WEIGHTS] for j in range(4)]
    return (loss_total, dx.reshape(x.shape), *ordered[0], *ordered[1], *ordered[2], *ordered[3])


def kernel(x, norm1_g, w_in, gate_b, conv_w, conv_b, conv_norm_g, w_conv_out, q_norm_g, k_norm_g, w_attn_out, w_out, norm2_g, w_up, ffn_conv_w, ffn_conv_b, w_down, loss_target, m_norm1_g, m_w_in, m_gate_b, m_conv_w, m_conv_b, m_conv_norm_g, m_w_conv_out, m_q_norm_g, m_k_norm_g, m_w_attn_out, m_w_out, m_norm2_g, m_w_up, m_ffn_conv_w, m_ffn_conv_b, m_w_down, v_norm1_g, v_w_in, v_gate_b, v_conv_w, v_conv_b, v_conv_norm_g, v_w_conv_out, v_q_norm_g, v_k_norm_g, v_w_attn_out, v_w_out, v_norm2_g, v_w_up, v_ffn_conv_w, v_ffn_conv_b, v_w_down):
    w = dict(zip(WEIGHTS, (norm1_g, w_in, gate_b, conv_w, conv_b, conv_norm_g, w_conv_out, q_norm_g, k_norm_g,
                           w_attn_out, w_out, norm2_g, w_up, ffn_conv_w, ffn_conv_b, w_down)))
    m = dict(zip(WEIGHTS, (m_norm1_g, m_w_in, m_gate_b, m_conv_w, m_conv_b, m_conv_norm_g, m_w_conv_out, m_q_norm_g,
                           m_k_norm_g, m_w_attn_out, m_w_out, m_norm2_g, m_w_up, m_ffn_conv_w, m_ffn_conv_b, m_w_down)))
    v = dict(zip(WEIGHTS, (v_norm1_g, v_w_in, v_gate_b, v_conv_w, v_conv_b, v_conv_norm_g, v_w_conv_out, v_q_norm_g,
                           v_k_norm_g, v_w_attn_out, v_w_out, v_norm2_g, v_w_up, v_ffn_conv_w, v_ffn_conv_b, v_w_down)))
    dims = Dims(d_model=x.shape[-1], batch_local=x.shape[0], seq=x.shape[1], d_ff=w_down.shape[1] * N_CHIPS)
    return _step(dims, x, loss_target, w, m, v)
```

```python
import functools
import math
from typing import NamedTuple

import jax
import jax.numpy as jnp
from jax import lax
from jax.experimental import pallas as pl
from jax.experimental.pallas import tpu as pltpu

F32 = jnp.float32
BF16 = jnp.bfloat16

RMS_EPS = 1e-6
ATTN_BLOCK = 128
DILATIONS = (1, 4, 16)
CONV_HALO = 32
FFN_HALO = 8
ADAM_LR, ADAM_B1, ADAM_B2, ADAM_EPS, ADAM_WD, ADAM_STEP = 0.001, 0.9, 0.999, 1e-08, 0.01, 10
V7X_VMEM_LIMIT_BYTES = 56 * 2 ** 20
N_CHIPS = 4
MESH = pl.DeviceIdType.MESH


class Dims(NamedTuple):
    d_model: int = 1024
    n_heads: int = 16
    head_dim: int = 64
    d_ff: int = 2816
    seq: int = 2048
    batch_local: int = 2
    conv_width: int = 31
    ffn_conv_width: int = 3

    @property
    def tokens(self):
        return self.seq * self.batch_local


def _params(*semantics):
    return pltpu.CompilerParams(dimension_semantics=semantics, vmem_limit_bytes=V7X_VMEM_LIMIT_BYTES)


ANY = pl.BlockSpec(memory_space=pl.ANY)


def _ordered(body, n_inputs, after):
    after = [] if after is None else list(after) if isinstance(after, (list, tuple)) else [after]
    if not after:
        return body, [], []

    def wrapped(*refs):
        return body(*refs[:n_inputs], *refs[n_inputs + len(after):])

    return wrapped, [ANY] * len(after), after


def _pick(n, target, mult=128):
    if n <= target:
        return n
    best = None
    for t in range(mult, target + 1, mult):
        if n % t == 0:
            best = t
    assert best is not None, (n, target, mult)
    return best


def _sigmoid(v):
    return 1.0 / (1.0 + jnp.exp(-v))


def _mm_nn(a, w, *, out_dtype, name, residual=None, after=None, tm=1024, tn=1408, tk=2816):
    m, k = a.shape
    nsh, k2, c = w.shape
    assert k == k2 and a.dtype == BF16 and w.dtype == BF16
    n = nsh * c
    tm, tn, tk = _pick(m, tm, 8), _pick(c, tn), _pick(k, tk)
    nk, cpn = k // tk, c // tn

    def body(*refs):
        if residual is None:
            a_ref, w_ref, o_ref, acc = refs
        else:
            a_ref, w_ref, r_ref, o_ref, acc = refs
        prod = jnp.dot(a_ref[...], w_ref[...], preferred_element_type=F32)

        def finish(total):
            if residual is not None:
                total = total + r_ref[...]
            o_ref[...] = total.astype(out_dtype)

        if nk == 1:
            finish(prod)
        else:
            kk = pl.program_id(2)

            @pl.when(kk == 0)
            def _():
                acc[...] = prod

            @pl.when(kk > 0)
            def _():
                acc[...] += prod

            @pl.when(kk == nk - 1)
            def _():
                finish(acc[...])

    in_specs = [pl.BlockSpec((tm, tk), lambda i, j, kk: (i, kk)),
                pl.BlockSpec((None, tk, tn), lambda i, j, kk: (j // cpn, kk, j % cpn))]
    args = [a, w]
    if residual is not None:
        in_specs.append(pl.BlockSpec((tm, tn), lambda i, j, kk: (i, j)))
        args.append(residual)
    body, more_specs, more_args = _ordered(body, len(args), after)
    return pl.pallas_call(
        body, name=name, grid=(m // tm, n // tn, nk),
        in_specs=in_specs + more_specs, out_specs=pl.BlockSpec((tm, tn), lambda i, j, kk: (i, j)),
        out_shape=jax.ShapeDtypeStruct((m, n), out_dtype),
        scratch_shapes=[pltpu.VMEM((tm, tn) if nk > 1 else (8, 128), F32)],
        compiler_params=_params("parallel", "parallel", "arbitrary"),
    )(*args, *more_args)


def _proj_residual_loss(a, w, residual, target, *, name, tm=1024):
    m, k = a.shape
    _, k2, n = w.shape
    assert w.shape[0] == 1 and k == k2 and a.dtype == BF16 and w.dtype == BF16
    tm = _pick(m, tm, 8)

    def body(a_ref, w_ref, r_ref, t_ref, dy_ref, dyb_ref, loss_ref):
        err = r_ref[...] + jnp.dot(a_ref[...], w_ref[...], preferred_element_type=F32) - t_ref[...]
        dy = err * (1.0 / n)
        dy_ref[...] = dy
        dyb_ref[...] = dy.astype(BF16)
        part = jnp.sum(jnp.sum(err * err, axis=-1, keepdims=True), axis=0, keepdims=True) * (0.5 / n)
        _accumulate(loss_ref, jnp.broadcast_to(part, (8, 128)), pl.program_id(0) == 0)

    rows = lambda width: pl.BlockSpec((tm, width), lambda i: (i, 0))
    return pl.pallas_call(
        body, name=name, grid=(m // tm,),
        in_specs=[rows(k), pl.BlockSpec((None, k, n), lambda i: (0, 0, 0)), rows(n), rows(n)],
        out_specs=[rows(n), rows(n), pl.BlockSpec((8, 128), lambda i: (0, 0))],
        out_shape=[jax.ShapeDtypeStruct((m, n), F32), jax.ShapeDtypeStruct((m, n), BF16),
                   jax.ShapeDtypeStruct((8, 128), F32)],
        compiler_params=_params("arbitrary"),
    )(a, w, residual, target)


NORM_BWD_ROWS = 256


def _mm_nt_rmsnorm_bwd(a, w, x, g, dres, *, name, want_bf16, silu=False, after=None, tm=1024, tk=1792):
    m, k = a.shape
    nsh, r, c = w.shape
    assert k == nsh * c and a.dtype == BF16 and w.dtype == BF16 and x.shape == (m, r)
    tm, tk = _pick(m, tm, 8), _pick(c, tk)
    nk, cpk = k // tk, c // tk
    rows = _pick(tm, NORM_BWD_ROWS, 8)
    n_in = 4 if dres is None else 5

    def body(a_ref, w_ref, x_ref, g_ref, *rest):
        dres_ref = None if dres is None else rest[0]
        outs, acc = rest[n_in - 4:-1], rest[-1]
        dx_ref, dg_ref = outs[0], outs[-1]
        kk = pl.program_id(1)
        prod = lax.dot_general(a_ref[...], w_ref[...], (((1,), (1,)), ((), ())), preferred_element_type=F32)

        @pl.when(kk == 0)
        def _():
            acc[...] = prod

        @pl.when(kk > 0)
        def _():
            acc[...] += prod

        @pl.when(kk == nk - 1)
        def _():
            dg = jnp.zeros((1, r), F32)
            for r0 in range(0, tm, rows):
                part = slice(r0, r0 + rows)
                xv, dyv = x_ref[part, :], acc[part, :]
                inv = lax.rsqrt(jnp.mean(xv * xv, axis=-1, keepdims=True) + RMS_EPS)
                if silu:
                    y = xv * inv * g_ref[...]
                    sg = _sigmoid(y)
                    dyv = dyv * sg * (1.0 + y * (1.0 - sg))
                gy = dyv * g_ref[...]
                dx = inv * gy - xv * (inv * inv * inv) * jnp.mean(xv * gy, axis=-1, keepdims=True)
                if dres is not None:
                    dx = dx + dres_ref[part, :]
                dx_ref[part, :] = dx
                if want_bf16:
                    outs[1][part, :] = dx.astype(BF16)
                dg = dg + jnp.sum(dyv * xv * inv, axis=0, keepdims=True)
            _accumulate(dg_ref, dg, pl.program_id(0) == 0)

    whole = lambda: pl.BlockSpec((tm, r), lambda i, kk: (i, 0))
    vec = pl.BlockSpec((1, r), lambda i, kk: (0, 0))
    out_shape, out_specs = [jax.ShapeDtypeStruct((m, r), F32)], [whole()]
    if want_bf16:
        out_shape.append(jax.ShapeDtypeStruct((m, r), BF16))
        out_specs.append(whole())
    out_shape.append(jax.ShapeDtypeStruct((1, r), F32))
    out_specs.append(vec)
    body, more_specs, more_args = _ordered(body, n_in, after)
    residual_specs, residual_args = ([], []) if dres is None else ([whole()], [dres])
    return pl.pallas_call(
        body, name=name, grid=(m // tm, nk),
        in_specs=[pl.BlockSpec((tm, tk), lambda i, kk: (i, kk)),
                  pl.BlockSpec((None, r, tk), lambda i, kk: (kk // cpk, 0, kk % cpk)), whole(), vec]
        + residual_specs + more_specs,
        out_specs=out_specs, out_shape=out_shape,
        scratch_shapes=[pltpu.VMEM((tm, r), F32)],
        compiler_params=_params("arbitrary", "arbitrary"),
    )(a, w, x, g, *residual_args, *more_args)


MM_TN_VMEM_BYTES = 44 * 2 ** 20


def _mm_tn(a, b, *, n_shards, name, tm=1408, tn=1408):
    t, m = a.shape
    t2, n = b.shape
    assert t == t2 and a.dtype == BF16 and b.dtype == BF16
    c = n // n_shards
    tm, tn = _pick(m, tm), _pick(c, tn)
    if m // tm == 1 and n // tn == 1 and tn % (2 * LANES) == 0:
        tn //= 2
    fixed = 2 * tm * tn * 6
    if 4 * t * (tm + tn) + fixed <= MM_TN_VMEM_BYTES:
        tk = t
    else:
        tk = _pick(t, (MM_TN_VMEM_BYTES - fixed - 4 * tm * tn) // (4 * (tm + tn)), 8)
    nk, cpn = t // tk, c // tn

    def body(a_ref, b_ref, o_ref, ob_ref, acc):
        kk = pl.program_id(2)
        prod = lax.dot_general(a_ref[...], b_ref[...], (((0,), (0,)), ((), ())), preferred_element_type=F32)

        def finish(total):
            o_ref[...] = total
            ob_ref[...] = total.astype(BF16)

        if nk == 1:
            finish(prod)
        else:
            @pl.when(kk == 0)
            def _():
                acc[...] = prod

            @pl.when(kk > 0)
            def _():
                acc[...] += prod

            @pl.when(kk == nk - 1)
            def _():
                finish(acc[...])

    out_spec = pl.BlockSpec((None, tm, tn), lambda i, j, kk: (j // cpn, i, j % cpn))
    return pl.pallas_call(
        body, name=name, grid=(m // tm, n // tn, nk),
        in_specs=[pl.BlockSpec((tk, tm), lambda i, j, kk: (kk, i)),
                  pl.BlockSpec((tk, tn), lambda i, j, kk: (kk, j))],
        out_specs=[out_spec, out_spec],
        out_shape=[jax.ShapeDtypeStruct((n_shards, m, c), F32), jax.ShapeDtypeStruct((n_shards, m, c), BF16)],
        scratch_shapes=[pltpu.VMEM((tm, tn) if nk > 1 else (8, 128), F32)],
        compiler_params=_params("parallel", "parallel", "arbitrary"),
    )(a, b)


def _row_spec(tr, width, col=0):
    return pl.BlockSpec((tr, width), lambda i, col=col: (i, col))


def _vec_spec(width, col=0):
    return pl.BlockSpec((1, width), lambda i, col=col: (0, col))


def _accumulate(ref, value, first):
    @pl.when(first)
    def _():
        ref[...] = value

    @pl.when(jnp.logical_not(first))
    def _():
        ref[...] += value


def _rmsnorm_fwd(x, g, *, name, tr=512):
    t, d = x.shape
    tr = _pick(t, tr, 8)

    def body(x_ref, g_ref, o_ref):
        xv = x_ref[...]
        r = lax.rsqrt(jnp.mean(xv * xv, axis=-1, keepdims=True) + RMS_EPS)
        o_ref[...] = (xv * r * g_ref[...]).astype(BF16)

    return pl.pallas_call(
        body, name=name, grid=(t // tr,),
        in_specs=[_row_spec(tr, d), _vec_spec(d)], out_specs=_row_spec(tr, d),
        out_shape=jax.ShapeDtypeStruct((t, d), BF16), compiler_params=_params("parallel"),
    )(x, g)


CONV_ROWS = 16


def _seq_specs(dims, ts, width, halo, col, *, nxt=False):
    nst, per = dims.seq // ts, ts // halo
    last = dims.tokens // halo - 1
    cur = pl.BlockSpec((ts, width), lambda b, i: (b * nst + i, col))
    if nxt:
        edge = pl.BlockSpec((halo, width), lambda b, i: (jnp.minimum((b * nst + i + 1) * per, last), col))
    else:
        edge = pl.BlockSpec((halo, width), lambda b, i: (jnp.maximum((b * nst + i) * per - 1, 0), col))
    return cur, edge


SUBLANES = 8


def _shifted_copies(buf, shifted):
    rows = shifted.shape[1]
    for s in range(1, SUBLANES):
        shifted[s - 1] = buf[pl.ds(s, rows), :]


def _window(buf, shifted, start, size):
    a, s = divmod(start, SUBLANES)
    src = buf if s == 0 else shifted.at[s - 1]
    return src[pl.ds(SUBLANES * a, size), :]


def _conv_branch_fwd(z, w, b, g, dims, *, name, ts=128):
    t, c, kw = z.shape[0], dims.d_model, dims.conv_width
    base = CONV_HALO - (kw - 1)

    def body(av_ref, hv_ref, ag_ref, hg_ref, w_ref, b_ref, g_ref, a1_ref, a3_ref, buf, shifted):
        i = pl.program_id(1)
        buf[CONV_HALO:, :] = av_ref[...].astype(F32) * _sigmoid(ag_ref[...].astype(F32))
        buf[0:CONV_HALO, :] = jnp.where(i > 0, hv_ref[...].astype(F32) * _sigmoid(hg_ref[...].astype(F32)), 0.0)
        _shifted_copies(buf, shifted)
        for r0 in range(0, ts, CONV_ROWS):
            acc = jnp.broadcast_to(b_ref[...], (CONV_ROWS, c))
            for k in range(kw):
                acc = acc + w_ref[k:k + 1, :] * _window(buf, shifted, r0 + base + k, CONV_ROWS)
            a1_ref[r0:r0 + CONV_ROWS, :] = acc
            a2 = acc * lax.rsqrt(jnp.mean(acc * acc, axis=-1, keepdims=True) + RMS_EPS) * g_ref[...]
            a3_ref[r0:r0 + CONV_ROWS, :] = (a2 * _sigmoid(a2)).astype(BF16)

    vec = pl.BlockSpec((1, c), lambda b, i: (0, 0))
    out = pl.BlockSpec((ts, c), lambda b, i: (b * (dims.seq // ts) + i, 0))
    return pl.pallas_call(
        body, name=name, grid=(dims.batch_local, dims.seq // ts),
        in_specs=[*_seq_specs(dims, ts, c, CONV_HALO, 0), *_seq_specs(dims, ts, c, CONV_HALO, 1),
                  pl.BlockSpec((CONV_HALO, c), lambda b, i: (0, 0)), vec, vec],
        out_specs=[out, out],
        out_shape=[jax.ShapeDtypeStruct((t, c), F32), jax.ShapeDtypeStruct((t, c), BF16)],
        scratch_shapes=[pltpu.VMEM((CONV_HALO + ts, c), F32),
                        pltpu.VMEM((SUBLANES - 1, CONV_HALO + ts - SUBLANES, c), F32)],
        compiler_params=_params("parallel", "parallel"),
    )(z, z, z, z, w, b, g)


def _conv_branch_bwd(da1, z, w, rest_of_dz, dims, *, name, ts=128):
    t, c, kw = z.shape[0], dims.d_model, dims.conv_width
    nst = dims.seq // ts
    base = CONV_HALO - (kw - 1)
    n_rest = len(rest_of_dz)
    total = 2 * c + sum(r.shape[1] for r in rest_of_dz)

    def body(d_ref, dn_ref, av_ref, hv_ref, ag_ref, hg_ref, w_ref, *more):
        rest_refs = more[:n_rest]
        dz_ref, dw_ref, db_ref, abuf, dbuf, ashift, dshift = more[n_rest:]
        col = 2 * c
        for r in rest_refs:
            dz_ref[:, col:col + r.shape[1]] = r[...]
            col += r.shape[1]
        i = pl.program_id(1)
        first = jnp.logical_and(pl.program_id(0) == 0, i == 0)
        abuf[CONV_HALO:, :] = av_ref[...].astype(F32) * _sigmoid(ag_ref[...].astype(F32))
        abuf[0:CONV_HALO, :] = jnp.where(i > 0, hv_ref[...].astype(F32) * _sigmoid(hg_ref[...].astype(F32)), 0.0)
        d1 = d_ref[...]
        dbuf[0:ts, :] = d1
        dbuf[ts:, :] = jnp.where(i < nst - 1, dn_ref[...], 0.0)
        _shifted_copies(abuf, ashift)
        _shifted_copies(dbuf, dshift)

        @pl.when(first)
        def _():
            dw_ref[...] = jnp.zeros_like(dw_ref)
            db_ref[...] = jnp.zeros_like(db_ref)

        db_ref[...] += jnp.sum(d1, axis=0, keepdims=True)
        for k in range(kw):
            dw_ref[k:k + 1, :] += jnp.sum(d1 * _window(abuf, ashift, base + k, ts), axis=0, keepdims=True)
        for r0 in range(0, ts, CONV_ROWS):
            acc = jnp.zeros((CONV_ROWS, c), F32)
            for k in range(kw):
                acc = acc + w_ref[k:k + 1, :] * _window(dbuf, dshift, r0 + (kw - 1) - k, CONV_ROWS)
            av = av_ref[r0:r0 + CONV_ROWS, :].astype(F32)
            sg = _sigmoid(ag_ref[r0:r0 + CONV_ROWS, :].astype(F32))
            dz_ref[r0:r0 + CONV_ROWS, 0:c] = (acc * sg).astype(BF16)
            dz_ref[r0:r0 + CONV_ROWS, c:2 * c] = (acc * av * sg * (1.0 - sg)).astype(BF16)

    cur, nxt = _seq_specs(dims, ts, c, CONV_HALO, 0, nxt=True)
    return pl.pallas_call(
        body, name=name, grid=(dims.batch_local, nst),
        in_specs=[cur, nxt, *_seq_specs(dims, ts, c, CONV_HALO, 0), *_seq_specs(dims, ts, c, CONV_HALO, 1),
                  pl.BlockSpec((CONV_HALO, c), lambda b, i: (0, 0))]
        + [pl.BlockSpec((ts, r.shape[1]), lambda b, i: (b * nst + i, 0)) for r in rest_of_dz],
        out_specs=[pl.BlockSpec((ts, total), lambda b, i: (b * nst + i, 0)),
                   pl.BlockSpec((CONV_HALO, c), lambda b, i: (0, 0)), pl.BlockSpec((1, c), lambda b, i: (0, 0))],
        out_shape=[jax.ShapeDtypeStruct((t, total), BF16), jax.ShapeDtypeStruct((CONV_HALO, c), F32),
                   jax.ShapeDtypeStruct((1, c), F32)],
        scratch_shapes=[pltpu.VMEM((CONV_HALO + ts, c), F32)] * 2
        + [pltpu.VMEM((SUBLANES - 1, CONV_HALO + ts - SUBLANES, c), F32)] * 2,
        compiler_params=_params("arbitrary", "arbitrary"),
    )(da1, da1, z, z, z, z, w, *rest_of_dz)


FFN_ROWS = 16
FFN_COLS = 256


def _ffn_chunks(ts, f):
    cw = _pick(f, FFN_COLS)
    return [(r0, c0, cw) for r0 in range(0, ts, FFN_ROWS) for c0 in range(0, f, cw)]


def _tap_sources(buf, moved, offsets, rows):
    taps, used = [], 0
    for off in offsets:
        if off % SUBLANES:
            moved[used] = buf[pl.ds(off, rows), :]
            taps.append((moved.at[used], 0))
            used += 1
        else:
            taps.append((buf, off))
    return taps


def _moved_copies(offsets):
    return sum(1 for off in offsets if off % SUBLANES)


def _taps_sum(taps, w_ref, init, r0, cols):
    for k, (src, off) in enumerate(taps):
        init = init + w_ref[k:k + 1, cols] * src[pl.ds(off + r0, init.shape[0]), cols]
    return init


def _ffn_bwd(dy, w_down, up, w, b, dims, *, name, after=None, ts=128):
    t, f, kw = up.shape[0], dims.d_ff, dims.ffn_conv_width
    nst = dims.seq // ts
    fwd_offsets = [FFN_HALO - (kw - 1) + k for k in range(kw)]
    bwd_offsets = [(kw - 1) - k for k in range(kw)]
    dy_halo = 2 * FFN_HALO

    def body(dy_ref, dyn_ref, wd_ref, up_ref, hp_ref, hn_ref, w_ref, b_ref, o_ref, dw_ref, db_ref,
             d_ref, dn_ref, buf, moved, dbuf, dmoved):
        i = pl.program_id(1)
        first = jnp.logical_and(pl.program_id(0) == 0, i == 0)
        more = i < nst - 1
        d_ref[...] = _dot_nt(dy_ref[...], wd_ref[...])
        dn_ref[...] = _dot_nt(dyn_ref[...], wd_ref[...])
        buf[0:FFN_HALO, :] = jnp.where(i > 0, hp_ref[...], 0.0)
        buf[FFN_HALO:FFN_HALO + ts, :] = up_ref[...]
        buf[FFN_HALO + ts:, :] = hn_ref[...]
        taps = _tap_sources(buf, moved, fwd_offsets, ts + FFN_HALO)

        def du_chunk(r0, rows, c0, cw, d):
            vcols, gcols = slice(c0, c0 + cw), slice(f + c0, f + c0 + cw)
            uv = _taps_sum(taps, w_ref, jnp.broadcast_to(b_ref[:, vcols], (rows, cw)), r0, vcols)
            ug = _taps_sum(taps, w_ref, jnp.broadcast_to(b_ref[:, gcols], (rows, cw)), r0, gcols)
            sg = _sigmoid(ug)
            dbuf[r0:r0 + rows, vcols] = d * ug * sg
            dbuf[r0:r0 + rows, gcols] = d * uv * sg * (1.0 + ug * (1.0 - sg))

        for r0, c0, cw in _ffn_chunks(ts, f):
            du_chunk(r0, FFN_ROWS, c0, cw, d_ref[r0:r0 + FFN_ROWS, c0:c0 + cw].astype(F32))
        for _, c0, cw in _ffn_chunks(FFN_ROWS, f):
            d_next = dn_ref[:, c0:c0 + cw].astype(F32)[0:FFN_HALO]
            du_chunk(ts, FFN_HALO, c0, cw, jnp.where(more, d_next, 0.0))

        @pl.when(first)
        def _():
            dw_ref[...] = jnp.zeros_like(dw_ref)
            db_ref[...] = jnp.zeros_like(db_ref)

        du = dbuf[0:ts, :]
        db_ref[...] += jnp.sum(du, axis=0, keepdims=True)
        for k, (src, off) in enumerate(taps):
            dw_ref[k:k + 1, :] += jnp.sum(du * src[pl.ds(off, ts), :], axis=0, keepdims=True)

        dtaps = _tap_sources(dbuf, dmoved, bwd_offsets, ts)
        for r0, c0, cw in _ffn_chunks(ts, 2 * f):
            cols = slice(c0, c0 + cw)
            o_ref[r0:r0 + FFN_ROWS, cols] = _taps_sum(dtaps, w_ref, jnp.zeros((FFN_ROWS, cw), F32), r0, cols).astype(BF16)

    d_model = dy.shape[1]
    up_cur, up_prev = _seq_specs(dims, ts, 2 * f, FFN_HALO, 0)
    _, up_next = _seq_specs(dims, ts, 2 * f, FFN_HALO, 0, nxt=True)
    dy_cur, dy_next = _seq_specs(dims, ts, d_model, dy_halo, 0, nxt=True)
    full = lambda rows: pl.BlockSpec((rows, 2 * f), lambda b_, i: (0, 0))
    body, more_specs, more_args = _ordered(body, 8, after)
    return pl.pallas_call(
        body, name=name, grid=(dims.batch_local, nst),
        in_specs=[dy_cur, dy_next, pl.BlockSpec((None, f, d_model), lambda b_, i: (0, 0, 0)), up_cur, up_prev, up_next,
                  full(FFN_HALO), full(1)] + more_specs,
        out_specs=[pl.BlockSpec((ts, 2 * f), lambda b_, i: (b_ * nst + i, 0)), full(FFN_HALO), full(1)],
        out_shape=[jax.ShapeDtypeStruct((t, 2 * f), BF16), jax.ShapeDtypeStruct((FFN_HALO, 2 * f), F32),
                   jax.ShapeDtypeStruct((1, 2 * f), F32)],
        scratch_shapes=[pltpu.VMEM((ts, f), F32), pltpu.VMEM((dy_halo, f), F32),
                        pltpu.VMEM((ts + 2 * FFN_HALO, 2 * f), F32),
                        pltpu.VMEM((_moved_copies(fwd_offsets), ts + FFN_HALO, 2 * f), F32),
                        pltpu.VMEM((ts + FFN_HALO, 2 * f), F32),
                        pltpu.VMEM((_moved_copies(bwd_offsets), ts, 2 * f), F32)],
        compiler_params=_params("arbitrary", "arbitrary"),
    )(dy, dy, w_down, up, up, up, w, b, *more_args)


def _ffn_act_fwd(up, w, b, dims, *, name, ts=128):
    t, f, kw = up.shape[0], dims.d_ff, dims.ffn_conv_width
    offsets = [FFN_HALO - (kw - 1) + k for k in range(kw)]

    def body(up_ref, h_ref, w_ref, b_ref, o_ref, buf, moved):
        buf[FFN_HALO:, :] = up_ref[...]
        buf[0:FFN_HALO, :] = jnp.where(pl.program_id(1) > 0, h_ref[...], 0.0)
        taps = _tap_sources(buf, moved, offsets, ts)
        for r0, c0, cw in _ffn_chunks(ts, f):
            vcols, gcols = slice(c0, c0 + cw), slice(f + c0, f + c0 + cw)
            uv = _taps_sum(taps, w_ref, jnp.broadcast_to(b_ref[:, vcols], (FFN_ROWS, cw)), r0, vcols)
            ug = _taps_sum(taps, w_ref, jnp.broadcast_to(b_ref[:, gcols], (FFN_ROWS, cw)), r0, gcols)
            o_ref[r0:r0 + FFN_ROWS, vcols] = (ug * _sigmoid(ug) * uv).astype(BF16)

    full = lambda rows: pl.BlockSpec((rows, 2 * f), lambda b_, i: (0, 0))
    return pl.pallas_call(
        body, name=name, grid=(dims.batch_local, dims.seq // ts),
        in_specs=[*_seq_specs(dims, ts, 2 * f, FFN_HALO, 0), full(FFN_HALO), full(1)],
        out_specs=pl.BlockSpec((ts, f), lambda b_, i: (b_ * (dims.seq // ts) + i, 0)),
        out_shape=jax.ShapeDtypeStruct((t, f), BF16),
        scratch_shapes=[pltpu.VMEM((FFN_HALO + ts, 2 * f), F32), pltpu.VMEM((_moved_copies(offsets), ts, 2 * f), F32)],
        compiler_params=_params("parallel", "parallel"),
    )(up, up, w, b)


def _dot_nt(a, b):
    return lax.dot_general(a, b, (((1,), (1,)), ((), ())), preferred_element_type=F32)


def _dot_tn(a, b):
    return lax.dot_general(a, b, (((0,), (0,)), ((), ())), preferred_element_type=F32)


LANES = 128
MASK_BIAS = 1e30
RESIDUE_DILATIONS = tuple(d for d in DILATIONS if d > 1)


def _rows_to_residues(value, out_ref, scr, d):
    rows, width = value.shape
    for c in range(width // LANES):
        cols = slice(LANES * c, LANES * (c + 1))
        scr[c] = value[:, cols]
        for r in range(d):
            out_ref[r, :, cols] = scr[c, pl.ds(r, rows // d, stride=d), :].astype(out_ref.dtype)


def _residues_to_rows(in_ref, scr, d):
    _, n, width = in_ref.shape
    slabs = []
    for c in range(width // LANES):
        cols = slice(LANES * c, LANES * (c + 1))
        for r in range(d):
            scr[c, pl.ds(r, n, stride=d), :] = in_ref[r, :, cols].astype(F32)
        slabs.append(scr[c])
    return slabs[0] if len(slabs) == 1 else jnp.concatenate(slabs, axis=1)


def _residue_shape(dims, d, width, dtype):
    return jax.ShapeDtypeStruct((dims.batch_local, d, dims.seq // d, width), dtype)


def _residue_spec(dims, d, tr, width):
    tiles = dims.seq // tr
    return pl.BlockSpec((None, d, tr // d, width), lambda i: (i // tiles, 0, i % tiles, 0))


def _head_sum_matrix(dims):
    a = dims.n_heads * dims.head_dim
    head = jnp.arange(a, dtype=jnp.int32) // dims.head_dim
    return (head[:, None] == jnp.arange(LANES, dtype=jnp.int32)[None, :]).astype(BF16)


def _two_pass_dot(v, m):
    hi = v.astype(BF16)
    lo = (v - hi.astype(F32)).astype(BF16)
    return jnp.dot(hi, m, preferred_element_type=F32) + jnp.dot(lo, m, preferred_element_type=F32)


def _residue_permutations(tr):
    out = []
    for d in RESIDUE_DILATIONS:
        dst = jnp.arange(tr, dtype=jnp.int32)
        src = d * (dst % (tr // d)) + dst // (tr // d)
        out.append((src[:, None] == jnp.arange(tr, dtype=jnp.int32)[None, :]).astype(BF16))
    return out


def _bf16_rows_to_residues(value, out_ref, perm_ref, d):
    n = value.shape[0] // d
    moved = jnp.dot(perm_ref[...], value, preferred_element_type=F32).astype(out_ref.dtype)
    for r in range(d):
        out_ref[r] = moved[r * n:(r + 1) * n]


def _bf16_residues_to_rows(in_ref, back_ref):
    d = in_ref.shape[0]
    stacked = jnp.concatenate([in_ref[r] for r in range(d)], axis=0)
    return jnp.dot(back_ref[...], stacked, preferred_element_type=F32)


def _qkv_layouts_fwd(z, gq, gk, head_ones, dims, *, name, tr=256):
    t = z.shape[0]
    a = dims.n_heads * dims.head_dim
    q_scale = dims.head_dim ** -0.5
    nres = len(RESIDUE_DILATIONS)

    def body(q_ref, k_ref, v_ref, gq_ref, gk_ref, sum_ref, spread_ref, *rest):
        perm_refs, outs = rest[:nres], rest[nres:]
        qv, kv = q_ref[...].astype(F32), k_ref[...].astype(F32)
        mean = lambda val: _two_pass_dot(_two_pass_dot(val, sum_ref[...]), spread_ref[...]) * (1.0 / dims.head_dim)
        rq = lax.rsqrt(mean(qv * qv) + RMS_EPS)
        rk = lax.rsqrt(mean(kv * kv) + RMS_EPS)
        values = ((qv * rq * gq_ref[...] * q_scale).astype(BF16), (kv * rk * gk_ref[...]).astype(BF16), v_ref[...])
        for j, val in enumerate(values):
            outs[j][...] = val
            for g, d in enumerate(RESIDUE_DILATIONS):
                _bf16_rows_to_residues(val, outs[3 * (g + 1) + j], perm_refs[g], d)

    out_specs = [_row_spec(tr, a)] * 3
    out_shape = [jax.ShapeDtypeStruct((t, a), BF16)] * 3
    for d in RESIDUE_DILATIONS:
        out_specs += [_residue_spec(dims, d, tr, a)] * 3
        out_shape += [_residue_shape(dims, d, a, BF16)] * 3
    outs = pl.pallas_call(
        body, name=name, grid=(t // tr,),
        in_specs=[_row_spec(tr, a, 2), _row_spec(tr, a, 3), _row_spec(tr, a, 4), _vec_spec(a), _vec_spec(a),
                  pl.BlockSpec((a, LANES), lambda i: (0, 0)), pl.BlockSpec((LANES, a), lambda i: (0, 0))]
        + [pl.BlockSpec((tr, tr), lambda i: (0, 0))] * nres,
        out_specs=out_specs, out_shape=out_shape,
        compiler_params=_params("parallel"),
    )(z, z, z, gq, gk, *head_ones, *_residue_permutations(tr))
    return {d: tuple(outs[3 * g:3 * g + 3]) for g, d in enumerate((1,) + RESIDUE_DILATIONS)}


ATTN_RESIDUES_PER_STEP = 4
ATTN_RESIDUES_PER_STEP_WINDOWED = 2


def _attn_groups(dims, dil):
    return (dims.batch_local, dil) if dil > 1 else (1, dims.batch_local)


def _attn_array(x, dims, dil):
    return x if dil > 1 else x.reshape(1, dims.batch_local, dims.seq, x.shape[-1])


def _attn_residues(dims, dil):
    one_block = dims.seq // dil == ATTN_BLOCK
    return math.gcd(_attn_groups(dims, dil)[1], ATTN_RESIDUES_PER_STEP if one_block else ATTN_RESIDUES_PER_STEP_WINDOWED)


def _per_residue(body, rs):
    if rs == 1:
        return body

    def stepped(*refs):
        for r in range(rs):
            body(*[ref.at[r] for ref in refs])

    return stepped


def _attn_specs(dims, dil, width):
    blk = ATTN_BLOCK
    nb = dims.seq // dil // blk
    rs = _attn_residues(dims, dil)
    lead, groups = _attn_groups(dims, dil)
    if rs > 1 and nb == 1:
        grid = (lead, groups // rs)
        at = lambda f: pl.BlockSpec((None, rs, blk, width), lambda b, r: (b, r, 0, 0))
    elif rs > 1:
        grid = (lead, groups // rs, nb)
        at = lambda f: pl.BlockSpec((None, rs, blk, width), lambda b, r, i: (b, r, f(i), 0))
    else:
        grid = (lead, groups, nb)
        at = lambda f: pl.BlockSpec((None, None, blk, width), lambda b, r, i: (b, r, f(i), 0))
    return grid, at(lambda i: i), at(lambda i: jnp.maximum(i - 1, 0)), at(lambda i: jnp.minimum(i + 1, nb - 1))


def _head_slopes(n_heads):
    h = lax.broadcasted_iota(jnp.int32, (n_heads, 1, 1), 0).astype(F32)
    return jnp.exp((h + 1.0) * (-8.0 / n_heads * math.log(2.0)))


def _pair_masks(hd):
    low = lax.broadcasted_iota(jnp.int32, (1, 2 * hd), 1) < hd
    return low, jnp.logical_not(low)


def _attn_fwd(q, k, v, dims, dil, *, name):
    a = dims.n_heads * dims.head_dim
    heads, hd, blk = dims.n_heads, dims.head_dim, ATTN_BLOCK
    assert 2 * hd == LANES and heads % 2 == 0 and heads <= LANES
    nb = dims.seq // dil // blk
    has_prev = nb > 1
    nkeys = 2 * blk if has_prev else blk
    grid, cur, prev, _ = _attn_specs(dims, dil, a)
    _, cur_stat, _, _ = _attn_specs(dims, dil, LANES)

    def body(*refs):
        if has_prev:
            q_ref, kc_ref, vc_ref, kp_ref, vp_ref, o_ref, lse_ref, s_scr, p_scr, k_st, v_st = refs
            k_st[0:blk, :], k_st[blk:, :] = kp_ref[...], kc_ref[...]
            v_st[0:blk, :], v_st[blk:, :] = vp_ref[...], vc_ref[...]
        else:
            q_ref, k_st, v_st, o_ref, lse_ref, s_scr, p_scr = refs
        low, high = _pair_masks(hd)

        for hp in range(heads // 2):
            sl = slice(LANES * hp, LANES * (hp + 1))
            q2 = q_ref[:, sl]
            kcat = k_st[:, sl]
            s_scr[2 * hp] = _dot_nt(jnp.where(low, q2, jnp.zeros_like(q2)), kcat)
            s_scr[2 * hp + 1] = _dot_nt(jnp.where(high, q2, jnp.zeros_like(q2)), kcat)

        iq = lax.broadcasted_iota(jnp.int32, (blk, nkeys), 0)
        jk = lax.broadcasted_iota(jnp.int32, (blk, nkeys), 1)
        if has_prev:
            steps = iq + blk - jk
            valid = (steps >= 0) & (steps <= blk) & ((jk >= blk) | (pl.program_id(len(grid) - 1) > 0))
        else:
            steps = iq - jk
            valid = steps >= 0
        bias = jnp.where(valid, steps.astype(F32) * (-float(dil)), -MASK_BIAS)
        s = s_scr[...] + _head_slopes(heads) * bias[None]
        m = jnp.max(s, axis=-1, keepdims=True)
        p = jnp.exp(s - m)
        l = jnp.sum(p, axis=-1, keepdims=True)
        p_scr[...] = p.astype(BF16)
        inv = 1.0 / l
        lse = m + jnp.log(l)

        lane = lax.broadcasted_iota(jnp.int32, (blk, LANES), 1)
        stat = jnp.zeros((blk, LANES), F32)
        for hp in range(heads // 2):
            sl = slice(LANES * hp, LANES * (hp + 1))
            vcat = v_st[:, sl]
            pv_a = jnp.dot(p_scr[2 * hp], vcat, preferred_element_type=F32) * inv[2 * hp]
            pv_b = jnp.dot(p_scr[2 * hp + 1], vcat, preferred_element_type=F32) * inv[2 * hp + 1]
            o_ref[:, sl] = jnp.where(low, pv_a, pv_b).astype(BF16)
            stat = jnp.where(lane == 2 * hp, lse[2 * hp], stat)
            stat = jnp.where(lane == 2 * hp + 1, lse[2 * hp + 1], stat)
        lse_ref[...] = stat

    q4, k4, v4 = (_attn_array(x, dims, dil) for x in (q, k, v))
    rs = _attn_residues(dims, dil)
    per_step = lambda shape: shape if rs == 1 else (rs,) + shape
    o, lse = pl.pallas_call(
        _per_residue(body, rs), name=name, grid=grid,
        in_specs=[cur, cur, cur] + ([prev, prev] if has_prev else []),
        out_specs=[cur, cur_stat],
        out_shape=[jax.ShapeDtypeStruct(q4.shape, BF16), jax.ShapeDtypeStruct(q4.shape[:-1] + (LANES,), F32)],
        scratch_shapes=[pltpu.VMEM(per_step((heads, blk, nkeys)), F32), pltpu.VMEM(per_step((heads, blk, nkeys)), BF16)]
        + ([pltpu.VMEM(per_step((nkeys, a)), BF16)] * 2 if has_prev else []),
        compiler_params=_params(*["parallel"] * len(grid)),
    )(q4, k4, v4, *([k4, v4] if has_prev else []))
    return o.reshape(q.shape), lse.reshape(q.shape[:-1] + (LANES,))


def _attn_combine(groups, head_spread, dims, *, name, tr=256):
    t = dims.tokens
    a = dims.n_heads * dims.head_dim
    dils = tuple(groups)

    nres = len(RESIDUE_DILATIONS)

    def body(*refs):
        ins = refs[:2 * len(dils)]
        x_ref = refs[2 * len(dils)]
        back_refs = dict(zip(RESIDUE_DILATIONS, refs[2 * len(dils) + 1:2 * len(dils) + 1 + nres]))
        o_ref = refs[2 * len(dils) + 1 + nres]
        lse_refs = refs[2 * len(dils) + 2 + nres:-1]
        scr_stat = refs[-1]
        outs, stats = [], []
        for g, d in enumerate(dils):
            if d == 1:
                outs.append(ins[2 * g][...].astype(F32))
                stats.append(ins[2 * g + 1][...])
            else:
                outs.append(_bf16_residues_to_rows(ins[2 * g], back_refs[d]))
                stats.append(_residues_to_rows(ins[2 * g + 1], scr_stat, d))
        top = functools.reduce(jnp.maximum, stats)
        weights = [jnp.exp(s - top) for s in stats]
        total = functools.reduce(jnp.add, weights)
        joint = top + jnp.log(total)
        inv = 1.0 / total
        acc = None
        for w, o in zip(weights, outs):
            term = _two_pass_dot(w * inv, x_ref[...]) * o
            acc = term if acc is None else acc + term
        o_ref[...] = acc.astype(BF16)
        for g, d in enumerate(dils):
            if d == 1:
                lse_refs[g][...] = joint
            else:
                _rows_to_residues(joint, lse_refs[g], scr_stat, d)

    in_specs, args, lse_specs, lse_shapes = [], [], [], []
    for d in dils:
        if d == 1:
            in_specs += [_row_spec(tr, a), _row_spec(tr, LANES)]
            lse_specs.append(_row_spec(tr, LANES))
            lse_shapes.append(jax.ShapeDtypeStruct((t, LANES), F32))
        else:
            in_specs += [_residue_spec(dims, d, tr, a), _residue_spec(dims, d, tr, LANES)]
            lse_specs.append(_residue_spec(dims, d, tr, LANES))
            lse_shapes.append(_residue_shape(dims, d, LANES, F32))
        args += list(groups[d])
    outs = pl.pallas_call(
        body, name=name, grid=(t // tr,),
        in_specs=in_specs + [pl.BlockSpec((LANES, a), lambda i: (0, 0))] + [pl.BlockSpec((tr, tr), lambda i: (0, 0))] * nres,
        out_specs=[_row_spec(tr, a)] + lse_specs,
        out_shape=[jax.ShapeDtypeStruct((t, a), BF16)] + lse_shapes,
        scratch_shapes=[pltpu.VMEM((1, tr, LANES), F32)],
        compiler_params=_params("parallel"),
    )(*args, head_spread, *[jnp.transpose(p) for p in _residue_permutations(tr)])
    return outs[0], dict(zip(dils, outs[1:]))


def _attn_bwd_prep(dy, w, o, head_sum, dims, *, name, after=None, tr=256):
    t, a = o.shape
    nres = len(RESIDUE_DILATIONS)

    def body(dy_ref, w_ref, o_ref, e_ref, *rest):
        perm_refs, outs, scr_stat = rest[:nres], rest[nres:-1], rest[-1]
        do = _dot_nt(dy_ref[...], w_ref[...]).astype(BF16)
        outs[0][...] = do
        delta = _two_pass_dot(do.astype(F32) * o_ref[...].astype(F32), e_ref[...])
        outs[1][...] = delta
        for g, d in enumerate(RESIDUE_DILATIONS):
            _bf16_rows_to_residues(do, outs[2 + 2 * g], perm_refs[g], d)
            _rows_to_residues(delta, outs[3 + 2 * g], scr_stat, d)

    out_specs = [_row_spec(tr, a), _row_spec(tr, LANES)]
    out_shape = [jax.ShapeDtypeStruct((t, a), BF16), jax.ShapeDtypeStruct((t, LANES), F32)]
    for d in RESIDUE_DILATIONS:
        out_specs += [_residue_spec(dims, d, tr, a), _residue_spec(dims, d, tr, LANES)]
        out_shape += [_residue_shape(dims, d, a, BF16), _residue_shape(dims, d, LANES, F32)]
    n_in = 4 + nres
    body, more_specs, more_args = _ordered(body, n_in, after)
    outs = pl.pallas_call(
        body, name=name, grid=(t // tr,),
        in_specs=[_row_spec(tr, dy.shape[1]), pl.BlockSpec((None,) + w.shape[1:], lambda i: (0, 0, 0)), _row_spec(tr, a),
                  pl.BlockSpec((a, LANES), lambda i: (0, 0))] + [pl.BlockSpec((tr, tr), lambda i: (0, 0))] * nres + more_specs,
        out_specs=out_specs, out_shape=out_shape,
        scratch_shapes=[pltpu.VMEM((1, tr, LANES), F32)],
        compiler_params=_params("parallel"),
    )(dy, w, o, head_sum, *_residue_permutations(tr), *more_args)
    dos, deltas = {1: outs[0]}, {1: outs[1]}
    for g, d in enumerate(RESIDUE_DILATIONS):
        dos[d], deltas[d] = outs[2 + 2 * g], outs[3 + 2 * g]
    return dos, deltas


def _attn_bwd(q, k, v, do, lse, delta, dims, dil, *, name):
    a = dims.n_heads * dims.head_dim
    heads, hd, blk = dims.n_heads, dims.head_dim, ATTN_BLOCK
    nb = dims.seq // dil // blk
    has_next = nb > 1
    nq = 2 * blk if has_next else blk
    grid, cur, _, nxt = _attn_specs(dims, dil, a)
    _, cur_stat, _, nxt_stat = _attn_specs(dims, dil, LANES)

    def body(*refs):
        k_ref, v_ref, q_ref, do_ref, lse_ref, dl_ref = refs[:6]
        if has_next:
            qn_ref, don_ref, lsen_ref, dln_ref = refs[6:10]
            dq_ref, dk_ref, dv_ref, q_st, do_st, s_scr, dp_scr, p_scr, ds_scr, carry = refs[10:]
        else:
            dq_ref, dk_ref, dv_ref, q_st, do_st, s_scr, dp_scr, p_scr, ds_scr = refs[6:]
        j = pl.program_id(len(grid) - 1)
        low, high = _pair_masks(hd)
        q_st[0:blk, :] = q_ref[...]
        do_st[0:blk, :] = do_ref[...]
        if has_next:
            q_st[blk:, :] = qn_ref[...]
            do_st[blk:, :] = don_ref[...]
            lse_all = jnp.concatenate([lse_ref[...], lsen_ref[...]], axis=0)
            dl_all = jnp.concatenate([dl_ref[...], dln_ref[...]], axis=0)
        else:
            lse_all, dl_all = lse_ref[...], dl_ref[...]
        lse_t, dl_t = jnp.transpose(lse_all), jnp.transpose(dl_all)
        lse3 = jnp.stack([lse_t[h:h + 1, :] for h in range(heads)])
        dl3 = jnp.stack([dl_t[h:h + 1, :] for h in range(heads)])

        def halves(x):
            return jnp.where(low, x, jnp.zeros_like(x)), jnp.where(high, x, jnp.zeros_like(x))

        for hp in range(heads // 2):
            sl = slice(LANES * hp, LANES * (hp + 1))
            k2, v2 = k_ref[:, sl], v_ref[:, sl]
            q_a, q_b = halves(q_st[:, sl])
            do_a, do_b = halves(do_st[:, sl])
            s_scr[2 * hp], s_scr[2 * hp + 1] = _dot_nt(k2, q_a), _dot_nt(k2, q_b)
            dp_scr[2 * hp], dp_scr[2 * hp + 1] = _dot_nt(v2, do_a), _dot_nt(v2, do_b)

        jk = lax.broadcasted_iota(jnp.int32, (blk, nq), 0)
        rq = lax.broadcasted_iota(jnp.int32, (blk, nq), 1)
        if has_next:
            iq = jnp.where(rq < blk, rq, rq - blk)
            steps = jnp.where(rq < blk, iq - jk, iq - jk + blk)
            valid = ((rq < blk) & (iq >= jk)) | ((rq >= blk) & (jk >= iq) & (j + 1 < nb))
        else:
            steps, valid = rq - jk, rq >= jk
        bias = jnp.where(valid, steps.astype(F32) * (-float(dil)), -MASK_BIAS)
        p = jnp.exp(s_scr[...] + _head_slopes(heads) * bias[None] - lse3)
        p_scr[...] = p.astype(BF16)
        ds_scr[...] = (p * (dp_scr[...] - dl3)).astype(BF16)

        if has_next:
            @pl.when(j == 0)
            def _():
                carry[...] = jnp.zeros_like(carry)

        for hp in range(heads // 2):
            sl = slice(LANES * hp, LANES * (hp + 1))
            k2 = k_ref[:, sl]
            q_a, q_b = halves(q_st[:, sl])
            do_a, do_b = halves(do_st[:, sl])
            ds_a, ds_b = ds_scr[2 * hp], ds_scr[2 * hp + 1]
            dk_ref[:, sl] = (jnp.dot(ds_a, q_a, preferred_element_type=F32)
                             + jnp.dot(ds_b, q_b, preferred_element_type=F32)).astype(BF16)
            dv_ref[:, sl] = (jnp.dot(p_scr[2 * hp], do_a, preferred_element_type=F32)
                             + jnp.dot(p_scr[2 * hp + 1], do_b, preferred_element_type=F32)).astype(BF16)
            dq2 = jnp.where(low, _dot_tn(ds_a, k2), _dot_tn(ds_b, k2))
            if has_next:
                dq_ref[:, sl] = (carry[:, sl] + dq2[:blk]).astype(BF16)
                carry[:, sl] = dq2[blk:]
            else:
                dq_ref[:, sl] = dq2.astype(BF16)

    args, in_specs = [_attn_array(x, dims, dil) for x in (k, v, q, do, lse, delta)], [cur] * 4 + [cur_stat] * 2
    if has_next:
        args += [args[2], args[3], args[4], args[5]]
        in_specs += [nxt] * 2 + [nxt_stat] * 2
    shape = jax.ShapeDtypeStruct(args[2].shape, BF16)
    rs = _attn_residues(dims, dil)
    per_step = lambda dims_: dims_ if rs == 1 else (rs,) + dims_
    scratch = ([pltpu.VMEM(per_step((nq, a)), BF16)] * 2 + [pltpu.VMEM(per_step((heads, blk, nq)), F32)] * 2
               + [pltpu.VMEM(per_step((heads, blk, nq)), BF16)] * 2)
    if has_next:
        scratch.append(pltpu.VMEM(per_step((blk, a)), F32))
    grads = pl.pallas_call(
        _per_residue(body, rs), name=name, grid=grid, in_specs=in_specs, out_specs=[cur] * 3, out_shape=[shape] * 3,
        scratch_shapes=scratch,
        compiler_params=_params(*["parallel"] * (len(grid) - 1), "arbitrary"),
    )(*args)
    return tuple(g.reshape(q.shape) for g in grads)


def _qkv_layouts_bwd(z, grads, gq, gk, head_ones, dims, *, name, tr=256):
    t = z.shape[0]
    a = dims.n_heads * dims.head_dim
    q_scale = dims.head_dim ** -0.5
    dils = tuple(grads)
    nres = len(RESIDUE_DILATIONS)

    def body(q_ref, k_ref, *rest):
        d_refs = rest[:3 * len(dils)]
        gq_ref, gk_ref, sum_ref, spread_ref = rest[3 * len(dils):3 * len(dils) + 4]
        back_refs = dict(zip(RESIDUE_DILATIONS, rest[3 * len(dils) + 4:3 * len(dils) + 4 + nres]))
        dz_ref, dgq_ref, dgk_ref = rest[3 * len(dils) + 4 + nres:]
        first = pl.program_id(0) == 0
        mean = lambda val: _two_pass_dot(_two_pass_dot(val, sum_ref[...]), spread_ref[...]) * (1.0 / dims.head_dim)

        def total(j):
            acc = None
            for g, d in enumerate(dils):
                ref = d_refs[3 * g + j]
                part = ref[...].astype(F32) if d == 1 else _bf16_residues_to_rows(ref, back_refs[d])
                acc = part if acc is None else acc + part
            return acc

        def norm_bwd(x_ref, dy, g_ref, scale, col, dg_ref):
            xv = x_ref[...].astype(F32)
            dy = dy * scale
            r = lax.rsqrt(mean(xv * xv) + RMS_EPS)
            gy = dy * g_ref[...]
            dx = r * gy - xv * (r * r * r) * mean(xv * gy)
            dz_ref[:, col * a:(col + 1) * a] = dx.astype(BF16)
            _accumulate(dg_ref, jnp.sum(dy * xv * r, axis=0, keepdims=True), first)

        norm_bwd(q_ref, total(0), gq_ref, q_scale, 0, dgq_ref)
        norm_bwd(k_ref, total(1), gk_ref, 1.0, 1, dgk_ref)
        dz_ref[:, 2 * a:3 * a] = total(2).astype(BF16)

    in_specs, args = [_row_spec(tr, a, 2), _row_spec(tr, a, 3)], [z, z]
    for d in dils:
        in_specs += [_row_spec(tr, a) if d == 1 else _residue_spec(dims, d, tr, a)] * 3
        args += list(grads[d])
    in_specs += [_vec_spec(a), _vec_spec(a), pl.BlockSpec((a, LANES), lambda i: (0, 0)),
                 pl.BlockSpec((LANES, a), lambda i: (0, 0))] + [pl.BlockSpec((tr, tr), lambda i: (0, 0))] * nres
    return pl.pallas_call(
        body, name=name, grid=(t // tr,), in_specs=in_specs,
        out_specs=[_row_spec(tr, 3 * a), _vec_spec(a), _vec_spec(a)],
        out_shape=[jax.ShapeDtypeStruct((t, 3 * a), BF16)] + [jax.ShapeDtypeStruct((1, a), F32)] * 2,
        compiler_params=_params("arbitrary"),
    )(*args, gq, gk, *head_ones, *[jnp.transpose(p) for p in _residue_permutations(tr)])


def _mix_fwd(a3, o, w_a, w_b, w_out, z, gate_b, x, g2, dims, *, name, tr=512):
    t, d = x.shape
    tr = _pick(t, tr, 8)
    first_gate_col = z.shape[1] // d - 2

    def body(a_ref, o_ref, wa_ref, wb_ref, wo_ref, ga_ref, gb_ref, ba_ref, bb_ref, x_ref, g2_ref,
             ya_ref, yb_ref, mix_ref, x1_ref, h2_ref):
        ya = jnp.dot(a_ref[...], wa_ref[...], preferred_element_type=F32)
        yb = jnp.dot(o_ref[...], wb_ref[...], preferred_element_type=F32)
        ya_ref[...] = ya
        yb_ref[...] = yb
        g_a = _sigmoid(ga_ref[...].astype(F32) + ba_ref[...])
        g_b = _sigmoid(gb_ref[...].astype(F32) + bb_ref[...])
        mixed = (g_a * ya + g_b * yb).astype(BF16)
        mix_ref[...] = mixed
        x1 = x_ref[...] + jnp.dot(mixed, wo_ref[...], preferred_element_type=F32)
        x1_ref[...] = x1
        h2_ref[...] = (x1 * lax.rsqrt(jnp.mean(x1 * x1, axis=-1, keepdims=True) + RMS_EPS) * g2_ref[...]).astype(BF16)

    weight = pl.BlockSpec((None, d, d), lambda i: (0, 0, 0))
    rows = _row_spec(tr, d)
    return pl.pallas_call(
        body, name=name, grid=(t // tr,),
        in_specs=[rows, rows, weight, weight, weight, _row_spec(tr, d, first_gate_col),
                  _row_spec(tr, d, first_gate_col + 1), _vec_spec(d, 0), _vec_spec(d, 1), rows, _vec_spec(d)],
        out_specs=[rows] * 5,
        out_shape=[jax.ShapeDtypeStruct((t, d), dt) for dt in (F32, F32, BF16, F32, BF16)],
        compiler_params=_params("parallel"),
    )(a3, o, w_a, w_b, w_out, z, z, gate_b, gate_b, x, g2)


def _mix_bwd(dx, w, ya, yb, z, gate_b, dims, *, name, after=None, tr=512):
    t, d = ya.shape
    tr = _pick(t, tr, 8)
    first_gate_col = z.shape[1] // d - 2

    def body(dx_ref, w_ref, ya_ref, yb_ref, ga_ref, gb_ref, ba_ref, bb_ref, dya_ref, dyb_ref, dz_ref, db_ref):
        dm = _dot_nt(dx_ref[...], w_ref[...])
        g_a = _sigmoid(ga_ref[...].astype(F32) + ba_ref[...])
        g_b = _sigmoid(gb_ref[...].astype(F32) + bb_ref[...])
        dya_ref[...] = (dm * g_a).astype(BF16)
        dyb_ref[...] = (dm * g_b).astype(BF16)
        dl_a = dm * ya_ref[...] * g_a * (1.0 - g_a)
        dl_b = dm * yb_ref[...] * g_b * (1.0 - g_b)
        dz_ref[:, 0:d] = dl_a.astype(BF16)
        dz_ref[:, d:2 * d] = dl_b.astype(BF16)
        first = pl.program_id(0) == 0
        sums = jnp.concatenate([jnp.sum(dl_a, axis=0, keepdims=True), jnp.sum(dl_b, axis=0, keepdims=True)], axis=1)
        _accumulate(db_ref, sums, first)

    body, more_specs, more_args = _ordered(body, 8, after)
    return pl.pallas_call(
        body, name=name, grid=(t // tr,),
        in_specs=[_row_spec(tr, d), pl.BlockSpec((None, d, d), lambda i: (0, 0, 0)), _row_spec(tr, d), _row_spec(tr, d),
                  _row_spec(tr, d, first_gate_col), _row_spec(tr, d, first_gate_col + 1), _vec_spec(d, 0),
                  _vec_spec(d, 1)] + more_specs,
        out_specs=[_row_spec(tr, d), _row_spec(tr, d), _row_spec(tr, 2 * d), _vec_spec(2 * d)],
        out_shape=[jax.ShapeDtypeStruct((t, d), BF16)] * 2 + [jax.ShapeDtypeStruct((t, 2 * d), BF16),
                                                              jax.ShapeDtypeStruct((1, 2 * d), F32)],
        compiler_params=_params("arbitrary"),
    )(dx, w, ya, yb, z, z, gate_b, gate_b, *more_args)


def _adamw(w, grads, m, v, *, name, tr=256):
    r, c = w.shape
    tr = _pick(r, tr, 8)
    ng = len(grads)
    c1 = 1.0 - ADAM_B1 ** ADAM_STEP
    c2 = 1.0 - ADAM_B2 ** ADAM_STEP

    def body(*refs):
        w_ref, g_refs, m_ref, v_ref = refs[0], refs[1:1 + ng], refs[1 + ng], refs[2 + ng]
        g_out, d_out, m_out, v_out = refs[3 + ng:]
        g = g_refs[0][...]
        for extra in g_refs[1:]:
            g = g + extra[...]
        m_new = ADAM_B1 * m_ref[...] + (1.0 - ADAM_B1) * g
        v_new = ADAM_B2 * v_ref[...] + (1.0 - ADAM_B2) * (g * g)
        g_out[...] = g
        m_out[...] = m_new
        v_out[...] = v_new
        d_out[...] = -ADAM_LR * ((m_new / c1) / (jnp.sqrt(v_new / c2) + ADAM_EPS) + ADAM_WD * w_ref[...])

    spec = pl.BlockSpec((tr, c), lambda i: (i, 0))
    return pl.pallas_call(
        body, name=name, grid=(r // tr,),
        in_specs=[spec] * (3 + ng), out_specs=[spec] * 4, out_shape=[jax.ShapeDtypeStruct((r, c), F32)] * 4,
        compiler_params=_params("parallel"),
    )(w, *grads, m, v)


CHIP_PEERS = ((1, 0), (0, 1), (1, 1))


def _place():
    return lax.axis_index("x"), lax.axis_index("y"), lax.axis_index("c")


HBM = pl.BlockSpec(memory_space=pltpu.HBM)
SEM = pl.BlockSpec(memory_space=pltpu.SEMAPHORE)
IN_FLIGHT = pltpu.SideEffectType.DATAFLOW_SIDE_EFFECTING


def _in_hbm(a):
    return pltpu.with_memory_space_constraint(a, pltpu.HBM)


def _cast_to_lands(shards, dtypes, *, name, after=None):
    n = len(shards)

    def body(*refs):
        ins, outs, bufs, sems = refs[:n], refs[n:2 * n], refs[2 * n:3 * n], refs[3 * n]
        x, y, _ = _place()
        copies = []
        for a in range(n):
            bufs[a][...] = ins[a][...].astype(dtypes[a])
            cp = pltpu.make_async_copy(bufs[a], outs[a].at[2 * x + y], sems.at[a])
            cp.start()
            copies.append(cp)
        for cp in copies:
            cp.wait()

    body, more_specs, more_args = _ordered(body, n, after)
    return pl.pallas_call(
        body, name=name, in_specs=[pl.BlockSpec(memory_space=pltpu.VMEM)] * n + more_specs, out_specs=[ANY] * n,
        out_shape=[jax.ShapeDtypeStruct((N_CHIPS,) + s.shape, dt) for s, dt in zip(shards, dtypes)],
        scratch_shapes=[pltpu.VMEM(s.shape, dt) for s, dt in zip(shards, dtypes)] + [pltpu.SemaphoreType.DMA((n,))],
        compiler_params=pltpu.CompilerParams(vmem_limit_bytes=V7X_VMEM_LIMIT_BYTES),
    )(*shards, *more_args)


def _chip_copy(src, dst, send, recv, flip, place):
    x, y, c = place
    return pltpu.make_async_remote_copy(src_ref=src, dst_ref=dst, send_sem=send, recv_sem=recv,
                                        device_id=(x ^ flip[0], y ^ flip[1], c), device_id_type=MESH)


def _my_part(land, place, halved):
    block = land.at[2 * place[0] + place[1]]
    if not halved:
        return block
    rows = land.shape[1] // 2
    return block.at[pl.ds(pl.multiple_of(place[2] * rows, rows), rows)]


def _gather_start(lands, after, *, name, halved=()):
    n = len(lands)

    def body(*refs):
        ins, send, recv, token = refs[:n], refs[n + 1], refs[n + 2], refs[-1]
        place = _place()
        for a in range(n):
            part = _my_part(ins[a], place, a in halved)
            for p, flip in enumerate(CHIP_PEERS):
                k = 3 * a + p
                _chip_copy(part, part, send.at[k], recv.at[k], flip, place).start()
        token[...] = jnp.zeros_like(token)

    outs = pl.pallas_call(
        body, name=name, in_specs=[HBM] * n + [ANY],
        out_specs=(SEM, SEM, *[HBM] * n, pl.BlockSpec(memory_space=pltpu.VMEM)),
        out_shape=(pltpu.SemaphoreType.DMA((3 * n,)), pltpu.SemaphoreType.DMA((3 * n,)),
                   *[pltpu.HBM(l.shape, l.dtype) for l in lands], jax.ShapeDtypeStruct((8, 128), F32)),
        input_output_aliases={a: 2 + a for a in range(n)},
        compiler_params=pltpu.CompilerParams(has_side_effects=IN_FLIGHT),
    )(*[_in_hbm(l) for l in lands], after)
    return outs[0], outs[1], list(outs[2:2 + n]), outs[-1]


def _gather_wait(send, recv, lands, after, *, name, halved=()):
    n = len(lands)

    def body(*refs):
        ins, send_ref, recv_ref = refs[:n], refs[n], refs[n + 1]
        place = _place()
        for a in range(n):
            part = _my_part(ins[a], place, a in halved)
            for p, flip in enumerate(CHIP_PEERS):
                k = 3 * a + p
                cp = _chip_copy(part, part, send_ref.at[k], recv_ref.at[k], flip, place)
                cp.wait_send()
                cp.wait_recv()

    after = list(after) if isinstance(after, (list, tuple)) else [after]
    return pl.pallas_call(
        body, name=name, in_specs=[HBM] * n + [SEM, SEM] + [ANY] * len(after), out_specs=[HBM] * n,
        out_shape=[pltpu.HBM(l.shape, l.dtype) for l in lands],
        input_output_aliases={a: a for a in range(n)},
        compiler_params=pltpu.CompilerParams(has_side_effects=IN_FLIGHT),
    )(*lands, send, recv, *after)


def _forward_to_sibling(land, *, name):
    rows = land.shape[1] // 2

    def body(land_ref, out_ref, send, recv):
        x, y, c = _place()
        copies = []
        for p, (fx, fy) in enumerate(CHIP_PEERS):
            chip = 2 * (x ^ fx) + (y ^ fy)
            mine = pl.ds(pl.multiple_of(c * rows, rows), rows)
            theirs = pl.ds(pl.multiple_of((1 - c) * rows, rows), rows)
            out = pltpu.make_async_remote_copy(
                src_ref=land_ref.at[chip].at[mine], dst_ref=out_ref.at[chip].at[mine], send_sem=send.at[p],
                recv_sem=recv.at[p], device_id=(x, y, 1 - c), device_id_type=MESH)
            out.start()
            copies.append((out, pltpu.make_async_remote_copy(
                src_ref=land_ref.at[chip].at[theirs], dst_ref=out_ref.at[chip].at[theirs], send_sem=send.at[p],
                recv_sem=recv.at[p], device_id=(x, y, 1 - c), device_id_type=MESH)))
        for out, arriving in copies:
            out.wait_send()
            arriving.wait_recv()

    return pl.pallas_call(
        body, name=name, in_specs=[ANY], out_specs=ANY, out_shape=jax.ShapeDtypeStruct(land.shape, land.dtype),
        input_output_aliases={0: 0},
        scratch_shapes=[pltpu.SemaphoreType.DMA((3,)), pltpu.SemaphoreType.DMA((3,))],
    )(land)


def _scatter_start(grad, *, name):
    def body(g_ref, land_ref, send, recv, g_thru, land_thru, token):
        place = _place()
        for p, flip in enumerate(CHIP_PEERS):
            peer_chip = 2 * (place[0] ^ flip[0]) + (place[1] ^ flip[1])
            _chip_copy(g_ref.at[peer_chip], land_ref.at[p], send.at[p], recv.at[p], flip, place).start()
        token[...] = jnp.zeros_like(token)

    land = lax.empty((3,) + grad.shape[1:], grad.dtype)
    return pl.pallas_call(
        body, name=name, in_specs=[HBM, HBM],
        out_specs=(SEM, SEM, HBM, HBM, pl.BlockSpec(memory_space=pltpu.VMEM)),
        out_shape=(pltpu.SemaphoreType.DMA((3,)), pltpu.SemaphoreType.DMA((3,)), pltpu.HBM(grad.shape, grad.dtype),
                   pltpu.HBM(land.shape, land.dtype), jax.ShapeDtypeStruct((8, 128), F32)),
        input_output_aliases={0: 2, 1: 3},
        compiler_params=pltpu.CompilerParams(has_side_effects=IN_FLIGHT),
    )(_in_hbm(grad), _in_hbm(land))


def _scatter_wait(started, after, *, name):
    n = len(started)

    def body(*refs):
        grads, lands = refs[:n], refs[n:2 * n]
        sends, recvs = refs[2 * n:3 * n], refs[3 * n:4 * n]
        place = _place()
        for a in range(n):
            for p, flip in enumerate(CHIP_PEERS):
                cp = _chip_copy(grads[a].at[0], lands[a].at[p], sends[a].at[p], recvs[a].at[p], flip, place)
                cp.wait_send()
                cp.wait_recv()

    grads, lands = [s[2] for s in started], [s[3] for s in started]
    after = list(after) if isinstance(after, (list, tuple)) else [after]
    outs = pl.pallas_call(
        body, name=name, in_specs=[HBM] * (2 * n) + [SEM] * (2 * n) + [ANY] * len(after), out_specs=[HBM] * (2 * n),
        out_shape=[pltpu.HBM(a.shape, a.dtype) for a in grads + lands],
        input_output_aliases={a: a for a in range(2 * n)},
        compiler_params=pltpu.CompilerParams(has_side_effects=IN_FLIGHT),
    )(*grads, *lands, *[s[0] for s in started], *[s[1] for s in started], *after)
    return list(zip(outs[:n], outs[n:]))


def _sibling_copy(src, dst, send, recv, place):
    x, y, c = place
    return pltpu.make_async_remote_copy(src_ref=src, dst_ref=dst, send_sem=send, recv_sem=recv,
                                        device_id=(x, y, 1 - c), device_id_type=MESH)


def _swap_start(arrays, *, name):
    n = len(arrays)

    def body(*refs):
        ins, lands, send, recv, token = refs[:n], refs[n:2 * n], refs[2 * n], refs[2 * n + 1], refs[-1]
        place = _place()
        for a in range(n):
            _sibling_copy(ins[a], lands[a], send.at[a], recv.at[a], place).start()
        token[...] = jnp.zeros_like(token)

    both = [_in_hbm(a) for a in arrays] + [_in_hbm(lax.empty(a.shape, a.dtype)) for a in arrays]
    outs = pl.pallas_call(
        body, name=name, in_specs=[HBM] * (2 * n),
        out_specs=(SEM, SEM, *[HBM] * (2 * n), pl.BlockSpec(memory_space=pltpu.VMEM)),
        out_shape=(pltpu.SemaphoreType.DMA((n,)), pltpu.SemaphoreType.DMA((n,)),
                   *[pltpu.HBM(a.shape, a.dtype) for a in both], jax.ShapeDtypeStruct((8, 128), F32)),
        input_output_aliases={a: 2 + a for a in range(2 * n)},
        compiler_params=pltpu.CompilerParams(has_side_effects=IN_FLIGHT),
    )(*both)
    return outs[0], outs[1], list(outs[2:2 + n]), list(outs[2 + n:2 + 2 * n]), outs[-1]


def _swap_wait(started, after, *, name):
    send, recv, arrays, lands = started[:4]
    n = len(arrays)

    def body(*refs):
        ins, zones, send_ref, recv_ref = refs[:n], refs[n:2 * n], refs[2 * n], refs[2 * n + 1]
        place = _place()
        for a in range(n):
            cp = _sibling_copy(ins[a], zones[a], send_ref.at[a], recv_ref.at[a], place)
            cp.wait_send()
            cp.wait_recv()

    after = list(after) if isinstance(after, (list, tuple)) else [after]
    outs = pl.pallas_call(
        body, name=name, in_specs=[HBM] * (2 * n) + [SEM, SEM] + [ANY] * len(after), out_specs=[HBM] * (2 * n),
        out_shape=[pltpu.HBM(a.shape, a.dtype) for a in arrays + lands],
        input_output_aliases={a: a for a in range(2 * n)},
        compiler_params=pltpu.CompilerParams(has_side_effects=IN_FLIGHT),
    )(*arrays, *lands, send, recv, *after)
    return list(outs[:n]), list(outs[n:])


def _allreduce_start(packed, *, name):
    n_dev = 8

    def body(src_ref, land_ref, send, recv, src_thru, land_thru, token):
        x, y, c = _place()
        me = 4 * x + 2 * y + c
        for p in range(1, n_dev):
            pltpu.make_async_remote_copy(
                src_ref=src_ref, dst_ref=land_ref.at[me], send_sem=send.at[p - 1], recv_sem=recv.at[p - 1],
                device_id=(x ^ (p >> 2), y ^ ((p >> 1) & 1), c ^ (p & 1)), device_id_type=MESH).start()
        token[...] = jnp.zeros_like(token)

    land = lax.empty((n_dev,) + packed.shape, packed.dtype)
    return pl.pallas_call(
        body, name=name, in_specs=[HBM, HBM],
        out_specs=(SEM, SEM, HBM, HBM, pl.BlockSpec(memory_space=pltpu.VMEM)),
        out_shape=(pltpu.SemaphoreType.DMA((n_dev - 1,)), pltpu.SemaphoreType.DMA((n_dev - 1,)),
                   pltpu.HBM(packed.shape, packed.dtype), pltpu.HBM(land.shape, land.dtype),
                   jax.ShapeDtypeStruct((8, 128), F32)),
        input_output_aliases={0: 2, 1: 3},
        compiler_params=pltpu.CompilerParams(has_side_effects=IN_FLIGHT),
    )(_in_hbm(packed), _in_hbm(land))


def _allreduce_wait(started, after, *, name):
    send, recv, packed, land = started[:4]
    n_dev = 8

    def body(src_ref, land_ref, send_ref, recv_ref, *_):
        x, y, c = _place()
        for p in range(1, n_dev):
            cp = pltpu.make_async_remote_copy(
                src_ref=src_ref, dst_ref=land_ref.at[0], send_sem=send_ref.at[p - 1], recv_sem=recv_ref.at[p - 1],
                device_id=(x ^ (p >> 2), y ^ ((p >> 1) & 1), c ^ (p & 1)), device_id_type=MESH)
            cp.wait_send()
            cp.wait_recv()

    after = list(after) if isinstance(after, (list, tuple)) else [after]
    return pl.pallas_call(
        body, name=name, in_specs=[HBM, HBM, SEM, SEM] + [ANY] * len(after), out_specs=[HBM, HBM],
        out_shape=[pltpu.HBM(packed.shape, packed.dtype), pltpu.HBM(land.shape, land.dtype)],
        input_output_aliases={0: 0, 1: 1},
        compiler_params=pltpu.CompilerParams(has_side_effects=IN_FLIGHT),
    )(packed, land, send, recv, *after)


def _sum_devices(mine, land, *, name):
    n_dev = land.shape[0]

    def body(mine_ref, land_ref, out_ref):
        x, y, c = _place()
        me = 4 * x + 2 * y + c
        total = None
        for s in range(n_dev):
            part = jnp.where(me == s, mine_ref[...], land_ref[s])
            total = part if total is None else total + part
        out_ref[...] = total

    return pl.pallas_call(body, name=name, out_shape=jax.ShapeDtypeStruct(mine.shape, mine.dtype))(mine, land)


def _sum_received(grad, land, *, name, tr=256):
    _, r, c = grad.shape
    tr = _pick(r, tr, 8)

    def body(chip_ref, g_ref, l_ref, o_ref):
        o_ref[...] = ((g_ref[...] + l_ref[0].astype(F32)) + l_ref[1].astype(F32)) + l_ref[2].astype(F32)

    chip = (2 * lax.axis_index("x") + lax.axis_index("y")).astype(jnp.int32).reshape(1)
    return pl.pallas_call(
        body, name=name,
        grid_spec=pltpu.PrefetchScalarGridSpec(
            num_scalar_prefetch=1, grid=(r // tr,),
            in_specs=[pl.BlockSpec((None, tr, c), lambda i, chip_ref: (chip_ref[0], i, 0)),
                      pl.BlockSpec((3, tr, c), lambda i, chip_ref: (0, i, 0))],
            out_specs=pl.BlockSpec((tr, c), lambda i, chip_ref: (i, 0))),
        out_shape=jax.ShapeDtypeStruct((r, c), F32), compiler_params=_params("parallel"),
    )(chip, grad, land)


def _packed_rows(size, d):
    return -(-size // (8 * d)) * 8


def _pack_rows(arrays, d):
    rows = []
    for arr in arrays:
        flat = arr.reshape(-1).astype(F32)
        n = _packed_rows(flat.shape[0], d)
        rows.append(jnp.pad(flat, (0, n * d - flat.shape[0])).reshape(n, d))
    return jnp.concatenate(rows, axis=0)


def _unpack_rows(packed, shapes, d):
    out, row = [], 0
    for shape in shapes:
        size = math.prod(shape)
        n = _packed_rows(size, d)
        out.append(packed[row:row + n].reshape(-1)[:size].reshape(shape))
        row += n
    return out


SMALL = ("norm1_g", "gate_b", "conv_b", "conv_norm_g", "q_norm_g", "k_norm_g", "norm2_g", "ffn_conv_b")
LARGE = ("w_in", "w_conv_out", "w_attn_out", "w_out", "w_up", "w_down")
WEIGHTS = ("norm1_g", "w_in", "gate_b", "conv_w", "conv_b", "conv_norm_g", "w_conv_out", "q_norm_g", "k_norm_g",
           "w_attn_out", "w_out", "norm2_g", "w_up", "ffn_conv_w", "ffn_conv_b", "w_down")


def _after(vec, token):
    return vec if token is None else vec + token[0:1, 0:1]


def _local_step(dims, x, target, small, first_weights, other_weights, send_grad):
    d, f, heads = dims.d_model, dims.d_ff, dims.n_heads
    small = dict(small)
    row = lambda name: small[name].reshape(1, -1)
    head_sum = _head_sum_matrix(dims)
    head_spread = jnp.transpose(head_sum)
    ones = (head_sum, head_spread)
    gq = jnp.tile(row("q_norm_g"), (1, heads))
    gk = jnp.tile(row("k_norm_g"), (1, heads))
    one_shard = lambda w: w.reshape(1, -1, w.shape[-1])

    h = _rmsnorm_fwd(x, row("norm1_g"), name="norm1")
    full = first_weights(h)
    w_in = full["w_in"]
    conv_w = jnp.pad(full["conv_w"], ((0, CONV_HALO - dims.conv_width), (0, 0)))
    ffn_w = jnp.pad(full["ffn_conv_w"], ((0, FFN_HALO - dims.ffn_conv_width), (0, 0)))
    z = _mm_nn(h, w_in, out_dtype=BF16, after=full.get("token"), tm=2048, tn=1792, name="in_proj")
    a1, a3 = _conv_branch_fwd(z, conv_w, row("conv_b"), row("conv_norm_g"), dims, name="conv_branch")
    qkv = _qkv_layouts_fwd(z, gq, gk, ones, dims, name="qk_norm")
    per_group = {dil: _attn_fwd(*qkv[dil], dims, dil, name=f"attn_fwd_d{dil}") for dil in DILATIONS}
    o, lse = _attn_combine(per_group, head_spread, dims, name="attn_combine")
    full = other_weights(o)
    w_up = full["w_up"]
    w_co, w_ao, w_o, w_dn = (one_shard(full[k]) for k in ("w_conv_out", "w_attn_out", "w_out", "w_down"))
    ya, yb, mixed, x1, h2 = _mix_fwd(a3, o, w_co, w_ao, w_o, z, row("gate_b"), x, row("norm2_g"), dims,
                                     name="branch_projs_mix_out_proj_norm2")
    up = _mm_nn(h2, w_up, out_dtype=F32, tm=2048, name="up_proj")
    act = _ffn_act_fwd(up, ffn_w, row("ffn_conv_b"), dims, name="ffn_act")
    dy, dy_b, loss = _proj_residual_loss(act, w_dn, x1, target, tm=512, name="down_proj_loss")

    grads = {}

    def large(name, g):
        grads[name], g_bf16 = g
        return send_grad(name, g_bf16)

    sent = large("w_down", _mm_tn(act, dy_b, n_shards=1, name="dw_down"))
    dup, dfw, dfb = _ffn_bwd(dy_b, w_dn, up, ffn_w, row("ffn_conv_b"), dims, after=sent, name="d_act_ffn_bwd")
    grads["ffn_conv_w"], grads["ffn_conv_b"] = dfw[:dims.ffn_conv_width], dfb
    sent = large("w_up", _mm_tn(h2, dup, n_shards=N_CHIPS, name="dw_up"))
    dx1, dx1_b, grads["norm2_g"] = _mm_nt_rmsnorm_bwd(dup, w_up, x1, row("norm2_g"), dy, want_bf16=True, after=sent,
                                                     name="d_h2_norm2_bwd")
    sent = large("w_out", _mm_tn(mixed, dx1_b, n_shards=1, name="dw_out"))
    dya, dyb, dz_gate, grads["gate_b"] = _mix_bwd(dx1_b, w_o, ya, yb, z, row("gate_b"), dims, after=sent,
                                                  name="d_mix_gate_mix_bwd")
    sent = large("w_attn_out", _mm_tn(o, dyb, n_shards=1, name="dw_attn_out"))
    dos, deltas = _attn_bwd_prep(dyb, w_ao, o, head_sum, dims, after=sent, name="d_attn_bwd_prep")
    dqkv = {dil: _attn_bwd(*qkv[dil], dos[dil], lse[dil], deltas[dil], dims, dil, name=f"attn_bwd_d{dil}")
            for dil in DILATIONS}
    dz_qkv, dgq, dgk = _qkv_layouts_bwd(z, dqkv, gq, gk, ones, dims, name="qk_norm_bwd")
    grads["q_norm_g"] = dgq.reshape(heads, dims.head_dim).sum(axis=0)
    grads["k_norm_g"] = dgk.reshape(heads, dims.head_dim).sum(axis=0)
    sent = large("w_conv_out", _mm_tn(a3, dya, n_shards=1, name="dw_conv_out"))
    da1, grads["conv_norm_g"] = _mm_nt_rmsnorm_bwd(dya, w_co, a1, row("conv_norm_g"), None, want_bf16=False, silu=True,
                                                   after=sent, name="d_conv_act_norm_bwd")
    dz, dcw, grads["conv_b"] = _conv_branch_bwd(da1, z, conv_w, [dz_qkv, dz_gate], dims, name="conv_branch_bwd")
    grads["conv_w"] = dcw[:dims.conv_width]
    sent = large("w_in", _mm_tn(h, dz, n_shards=N_CHIPS, name="dw_in"))
    dx, grads["norm1_g"] = _mm_nt_rmsnorm_bwd(dz, w_in, x, row("norm1_g"), dx1, want_bf16=False, after=sent,
                                              name="d_h_norm1_bwd")
    return loss, dx, grads


def _step(dims, x, target, w, m, v):
    d = dims.d_model
    t = dims.tokens
    sq = lambda a: a.reshape(a.shape[1:])
    w2, m2, v2 = ({k: sq(a) for k, a in grp.items()} for grp in (w, m, v))

    conv_pad = jnp.pad(w2["conv_w"], ((0, CONV_HALO - dims.conv_width), (0, 0)))
    ffn_pad = jnp.pad(w2["ffn_conv_w"], ((0, FFN_HALO - dims.ffn_conv_width), (0, 0)))
    first_names = ("w_in", "conv_w", "ffn_conv_w")
    other_names = tuple(k for k in LARGE if k not in first_names)
    lands = dict(zip(first_names, _cast_to_lands([w2["w_in"], conv_pad, ffn_pad], [BF16, F32, F32], name="cast_first")))
    first = _gather_start([lands[k] for k in first_names], x, halved=(0,), name="gather_start_first")
    lands.update(zip(other_names, _cast_to_lands([w2[k] for k in other_names], [BF16] * len(other_names),
                                                 after=first[3], name="cast_other")))
    other = []
    cols = lambda g, rows: jnp.moveaxis(g, 0, 1).reshape(g.shape[1], -1)[:rows]

    def first_weights(after):
        got = dict(zip(first_names, _gather_wait(*first[:3], [after] + [lands[k] for k in other_names], halved=(0,),
                                                 name="gather_wait_first")))
        got["w_in"] = _forward_to_sibling(got["w_in"], name="forward_w_in")
        other.extend(_gather_start([lands[k] for k in other_names], got["w_in"], name="gather_start_other"))
        got["conv_w"] = cols(got["conv_w"], dims.conv_width)
        got["ffn_conv_w"] = cols(got["ffn_conv_w"], dims.ffn_conv_width)
        got["token"] = other[3]
        return got

    def other_weights(after):
        return dict(zip(other_names, _gather_wait(*other[:3], after, name="gather_wait_other")))

    started = {}

    def send_grad(name, g):
        send, recv, g_thru, land, token = _scatter_start(g.reshape(N_CHIPS, -1, g.shape[-1]), name=f"scatter_start_{name}")
        started[name] = (send, recv, g_thru, land)
        return token

    small = {k: w2[k] for k in SMALL}
    small["norm1_g"] = _after(small["norm1_g"].reshape(1, -1), first[3])
    loss, dx, grads = _local_step(dims, x.reshape(t, d), target.reshape(t, d), small, first_weights, other_weights, send_grad)

    def my_sums(names, after, tag):
        arrived = _scatter_wait([started[k] for k in names], after, name=f"scatter_wait_{tag}")
        blocks = [grads[k].reshape(N_CHIPS, -1, grads[k].shape[-1]) for k in names]
        return [_sum_received(g, land, name=f"sum_{k}") for k, g, (_, land) in zip(names, blocks, arrived)]

    def updates(names, mine, theirs):
        return {k: _adamw(w2[k], [a, b], m2[k], v2[k], name=f"adamw_{k}") for k, a, b in zip(names, mine, theirs)}

    small_names = SMALL + ("conv_w", "ffn_conv_w")
    packed = _pack_rows([grads[k] for k in small_names] + [loss[0, 0]], d)
    reducing = _allreduce_start(packed, name="allreduce_start")
    others = [k for k in LARGE if k != "w_in"]
    mine_others = my_sums(others, [dx, reducing[4]], "others")
    swapping_others = _swap_start(mine_others, name="swap_start_others")
    mine_w_in = my_sums(["w_in"], swapping_others[4], "w_in")
    swapping_w_in = _swap_start(mine_w_in, name="swap_start_w_in")
    out = updates(others, *_swap_wait(swapping_others, swapping_w_in[4], name="swap_wait_others"))
    last_updates = [out[k][1] for k in others]
    reduced = _sum_devices(*_allreduce_wait(reducing, last_updates, name="allreduce_wait"), name="allreduce_sum")
    shapes = [grads[k].shape for k in small_names] + [()]
    *small_g, loss_total = _unpack_rows(reduced, shapes, d)
    small_g = dict(zip(small_names, small_g))
    chip = 2 * lax.axis_index("x") + lax.axis_index("y")
    for k in ("conv_w", "ffn_conv_w"):
        width = w2[k].shape[1]
        small_g[k] = lax.dynamic_slice_in_dim(small_g[k], chip * width, width, axis=1)

    small_shapes = [w2[k].shape for k in small_names]
    pack = lambda grp: _pack_rows([grp[k] for k in small_names], d)
    results = _adamw(pack(w2), [pack(small_g)], pack(m2), pack(v2), name="adamw_small")
    unpacked = [_unpack_rows(r, small_shapes, d) for r in results]
    out.update({k: tuple(u[i] for u in unpacked) for i, k in enumerate(small_names)})
    out.update(updates(["w_in"], *_swap_wait(swapping_w_in, results[1], name="swap_wait_w_in")))

    lead = lambda a: a.reshape((1,) + a.shape)
    ordered = [[lead(out[k][j].reshape(w2[k].shape)) for k in WEIGHTS] for j in range(4)]
    return (loss_total, dx.reshape(x.shape), *ordered[0], *ordered[1], *ordered[2], *ordered[3])


def kernel(x, norm1_g, w_in, gate_b, conv_w, conv_b, conv_norm_g, w_conv_out, q_norm_g, k_norm_g, w_attn_out, w_out, norm2_g, w_up, ffn_conv_w, ffn_conv_b, w_down, loss_target, m_norm1_g, m_w_in, m_gate_b, m_conv_w, m_conv_b, m_conv_norm_g, m_w_conv_out, m_q_norm_g, m_k_norm_g, m_w_attn_out, m_w_out, m_norm2_g, m_w_up, m_ffn_conv_w, m_ffn_conv_b, m_w_down, v_norm1_g, v_w_in, v_gate_b, v_conv_w, v_conv_b, v_conv_norm_g, v_w_conv_out, v_q_norm_g, v_k_norm_g, v_w_attn_out, v_w_out, v_norm2_g, v_w_up, v_ffn_conv_w, v_ffn_conv_b, v_w_down):
    w = dict(zip(WEIGHTS, (norm1_g, w_in, gate_b, conv_w, conv_b, conv_norm_g, w_conv_out, q_norm_g, k_norm_g,
                           w_attn_out, w_out, norm2_g, w_up, ffn_conv_w, ffn_conv_b, w_down)))
    m = dict(zip(WEIGHTS, (m_norm1_g, m_w_in, m_gate_b, m_conv_w, m_conv_b, m_conv_norm_g, m_w_conv_out, m_q_norm_g,
                           m_k_norm_g, m_w_attn_out, m_w_out, m_norm2_g, m_w_up, m_ffn_conv_w, m_ffn_conv_b, m_w_down)))
    v = dict(zip(WEIGHTS, (v_norm1_g, v_w_in, v_gate_b, v_conv_w, v_conv_b, v_conv_norm_g, v_w_conv_out, v_q_norm_g,
                           v_k_norm_g, v_w_attn_out, v_w_out, v_norm2_g, v_w_up, v_ffn_conv_w, v_ffn_conv_b, v_w_down)))
    dims = Dims(d_model=x.shape[-1], batch_local=x.shape[0], seq=x.shape[1], d_ff=w_down.shape[1] * N_CHIPS)
    return _step(dims, x, loss_target, w, m, v)
```

```python
import functools
import math
from typing import NamedTuple

import jax
import jax.numpy as jnp
from jax import lax
from jax.experimental import pallas as pl
from jax.experimental.pallas import tpu as pltpu

F32 = jnp.float32
BF16 = jnp.bfloat16

RMS_EPS = 1e-6
ATTN_BLOCK = 128
DILATIONS = (1, 4, 16)
CONV_HALO = 32
FFN_HALO = 8
ADAM_LR, ADAM_B1, ADAM_B2, ADAM_EPS, ADAM_WD, ADAM_STEP = 0.001, 0.9, 0.999, 1e-08, 0.01, 10
V7X_VMEM_LIMIT_BYTES = 56 * 2 ** 20
N_CHIPS = 4
MESH = pl.DeviceIdType.MESH


class Dims(NamedTuple):
    d_model: int = 1024
    n_heads: int = 16
    head_dim: int = 64
    d_ff: int = 2816
    seq: int = 2048
    batch_local: int = 2
    conv_width: int = 31
    ffn_conv_width: int = 3

    @property
    def tokens(self):
        return self.seq * self.batch_local


def _params(*semantics):
    return pltpu.CompilerParams(dimension_semantics=semantics, vmem_limit_bytes=V7X_VMEM_LIMIT_BYTES)


ANY = pl.BlockSpec(memory_space=pl.ANY)


def _ordered(body, n_inputs, after):
    after = [] if after is None else list(after) if isinstance(after, (list, tuple)) else [after]
    if not after:
        return body, [], []

    def wrapped(*refs):
        return body(*refs[:n_inputs], *refs[n_inputs + len(after):])

    return wrapped, [ANY] * len(after), after


def _pick(n, target, mult=128):
    if n <= target:
        return n
    best = None
    for t in range(mult, target + 1, mult):
        if n % t == 0:
            best = t
    assert best is not None, (n, target, mult)
    return best


def _sigmoid(v):
    return 1.0 / (1.0 + jnp.exp(-v))


def _mm_nn(a, w, *, out_dtype, name, residual=None, after=None, tm=1024, tn=1408, tk=2816):
    m, k = a.shape
    nsh, k2, c = w.shape
    assert k == k2 and a.dtype == BF16 and w.dtype == BF16
    n = nsh * c
    tm, tn, tk = _pick(m, tm, 8), _pick(c, tn), _pick(k, tk)
    nk, cpn = k // tk, c // tn

    def body(*refs):
        if residual is None:
            a_ref, w_ref, o_ref, acc = refs
        else:
            a_ref, w_ref, r_ref, o_ref, acc = refs
        prod = jnp.dot(a_ref[...], w_ref[...], preferred_element_type=F32)

        def finish(total):
            if residual is not None:
                total = total + r_ref[...]
            o_ref[...] = total.astype(out_dtype)

        if nk == 1:
            finish(prod)
        else:
            kk = pl.program_id(2)

            @pl.when(kk == 0)
            def _():
                acc[...] = prod

            @pl.when(kk > 0)
            def _():
                acc[...] += prod

            @pl.when(kk == nk - 1)
            def _():
                finish(acc[...])

    in_specs = [pl.BlockSpec((tm, tk), lambda i, j, kk: (i, kk)),
                pl.BlockSpec((None, tk, tn), lambda i, j, kk: (j // cpn, kk, j % cpn))]
    args = [a, w]
    if residual is not None:
        in_specs.append(pl.BlockSpec((tm, tn), lambda i, j, kk: (i, j)))
        args.append(residual)
    body, more_specs, more_args = _ordered(body, len(args), after)
    return pl.pallas_call(
        body, name=name, grid=(m // tm, n // tn, nk),
        in_specs=in_specs + more_specs, out_specs=pl.BlockSpec((tm, tn), lambda i, j, kk: (i, j)),
        out_shape=jax.ShapeDtypeStruct((m, n), out_dtype),
        scratch_shapes=[pltpu.VMEM((tm, tn) if nk > 1 else (8, 128), F32)],
        compiler_params=_params("parallel", "parallel", "arbitrary"),
    )(*args, *more_args)


def _proj_residual_loss(a, w, residual, target, *, name, tm=1024):
    m, k = a.shape
    _, k2, n = w.shape
    assert w.shape[0] == 1 and k == k2 and a.dtype == BF16 and w.dtype == BF16
    tm = _pick(m, tm, 8)

    def body(a_ref, w_ref, r_ref, t_ref, dy_ref, dyb_ref, loss_ref):
        err = r_ref[...] + jnp.dot(a_ref[...], w_ref[...], preferred_element_type=F32) - t_ref[...]
        dy = err * (1.0 / n)
        dy_ref[...] = dy
        dyb_ref[...] = dy.astype(BF16)
        part = jnp.sum(jnp.sum(err * err, axis=-1, keepdims=True), axis=0, keepdims=True) * (0.5 / n)
        _accumulate(loss_ref, jnp.broadcast_to(part, (8, 128)), pl.program_id(0) == 0)

    rows = lambda width: pl.BlockSpec((tm, width), lambda i: (i, 0))
    return pl.pallas_call(
        body, name=name, grid=(m // tm,),
        in_specs=[rows(k), pl.BlockSpec((None, k, n), lambda i: (0, 0, 0)), rows(n), rows(n)],
        out_specs=[rows(n), rows(n), pl.BlockSpec((8, 128), lambda i: (0, 0))],
        out_shape=[jax.ShapeDtypeStruct((m, n), F32), jax.ShapeDtypeStruct((m, n), BF16),
                   jax.ShapeDtypeStruct((8, 128), F32)],
        compiler_params=_params("arbitrary"),
    )(a, w, residual, target)


def _mm_nt(a, w, *, out_dtype, name, after=None, tm=1024, tn=1408, tk=1792):
    m, k = a.shape
    nsh, r, c = w.shape
    assert k == nsh * c and a.dtype == BF16 and w.dtype == BF16
    tm, tn, tk = _pick(m, tm, 8), _pick(r, tn), _pick(c, tk)
    nk, cpk = k // tk, c // tk

    def body(a_ref, w_ref, o_ref, acc):
        prod = lax.dot_general(a_ref[...], w_ref[...], (((1,), (1,)), ((), ())), preferred_element_type=F32)
        if nk == 1:
            o_ref[...] = prod.astype(out_dtype)
        else:
            kk = pl.program_id(2)

            @pl.when(kk == 0)
            def _():
                acc[...] = prod

            @pl.when(kk > 0)
            def _():
                acc[...] += prod

            @pl.when(kk == nk - 1)
            def _():
                o_ref[...] = acc[...].astype(out_dtype)

    body, more_specs, more_args = _ordered(body, 2, after)
    return pl.pallas_call(
        body, name=name, grid=(m // tm, r // tn, nk),
        in_specs=[pl.BlockSpec((tm, tk), lambda i, j, kk: (i, kk)),
                  pl.BlockSpec((None, tn, tk), lambda i, j, kk: (kk // cpk, j, kk % cpk))] + more_specs,
        out_specs=pl.BlockSpec((tm, tn), lambda i, j, kk: (i, j)),
        out_shape=jax.ShapeDtypeStruct((m, r), out_dtype),
        scratch_shapes=[pltpu.VMEM((tm, tn) if nk > 1 else (8, 128), F32)],
        compiler_params=_params("parallel", "parallel", "arbitrary"),
    )(a, w, *more_args)


NORM_BWD_ROWS = 256


def _mm_nt_rmsnorm_bwd(a, w, x, g, dres, *, name, want_bf16, silu=False, after=None, tm=1024, tk=1792):
    m, k = a.shape
    nsh, r, c = w.shape
    assert k == nsh * c and a.dtype == BF16 and w.dtype == BF16 and x.shape == (m, r)
    tm, tk = _pick(m, tm, 8), _pick(c, tk)
    nk, cpk = k // tk, c // tk
    rows = _pick(tm, NORM_BWD_ROWS, 8)
    n_in = 4 if dres is None else 5

    def body(a_ref, w_ref, x_ref, g_ref, *rest):
        dres_ref = None if dres is None else rest[0]
        outs, acc = rest[n_in - 4:-1], rest[-1]
        dx_ref, dg_ref = outs[0], outs[-1]
        kk = pl.program_id(1)
        prod = lax.dot_general(a_ref[...], w_ref[...], (((1,), (1,)), ((), ())), preferred_element_type=F32)

        @pl.when(kk == 0)
        def _():
            acc[...] = prod

        @pl.when(kk > 0)
        def _():
            acc[...] += prod

        @pl.when(kk == nk - 1)
        def _():
            dg = jnp.zeros((1, r), F32)
            for r0 in range(0, tm, rows):
                part = slice(r0, r0 + rows)
                xv, dyv = x_ref[part, :], acc[part, :]
                inv = lax.rsqrt(jnp.mean(xv * xv, axis=-1, keepdims=True) + RMS_EPS)
                if silu:
                    y = xv * inv * g_ref[...]
                    sg = _sigmoid(y)
                    dyv = dyv * sg * (1.0 + y * (1.0 - sg))
                gy = dyv * g_ref[...]
                dx = inv * gy - xv * (inv * inv * inv) * jnp.mean(xv * gy, axis=-1, keepdims=True)
                if dres is not None:
                    dx = dx + dres_ref[part, :]
                dx_ref[part, :] = dx
                if want_bf16:
                    outs[1][part, :] = dx.astype(BF16)
                dg = dg + jnp.sum(dyv * xv * inv, axis=0, keepdims=True)
            _accumulate(dg_ref, dg, pl.program_id(0) == 0)

    whole = lambda: pl.BlockSpec((tm, r), lambda i, kk: (i, 0))
    vec = pl.BlockSpec((1, r), lambda i, kk: (0, 0))
    out_shape, out_specs = [jax.ShapeDtypeStruct((m, r), F32)], [whole()]
    if want_bf16:
        out_shape.append(jax.ShapeDtypeStruct((m, r), BF16))
        out_specs.append(whole())
    out_shape.append(jax.ShapeDtypeStruct((1, r), F32))
    out_specs.append(vec)
    body, more_specs, more_args = _ordered(body, n_in, after)
    residual_specs, residual_args = ([], []) if dres is None else ([whole()], [dres])
    return pl.pallas_call(
        body, name=name, grid=(m // tm, nk),
        in_specs=[pl.BlockSpec((tm, tk), lambda i, kk: (i, kk)),
                  pl.BlockSpec((None, r, tk), lambda i, kk: (kk // cpk, 0, kk % cpk)), whole(), vec]
        + residual_specs + more_specs,
        out_specs=out_specs, out_shape=out_shape,
        scratch_shapes=[pltpu.VMEM((tm, r), F32)],
        compiler_params=_params("arbitrary", "arbitrary"),
    )(a, w, x, g, *residual_args, *more_args)


MM_TN_VMEM_BYTES = 44 * 2 ** 20


def _mm_tn(a, b, *, n_shards, name, tm=1408, tn=1408):
    t, m = a.shape
    t2, n = b.shape
    assert t == t2 and a.dtype == BF16 and b.dtype == BF16
    c = n // n_shards
    tm, tn = _pick(m, tm), _pick(c, tn)
    if m // tm == 1 and n // tn == 1 and tn % (2 * LANES) == 0:
        tn //= 2
    fixed = 2 * tm * tn * 6
    if 4 * t * (tm + tn) + fixed <= MM_TN_VMEM_BYTES:
        tk = t
    else:
        tk = _pick(t, (MM_TN_VMEM_BYTES - fixed - 4 * tm * tn) // (4 * (tm + tn)), 8)
    nk, cpn = t // tk, c // tn

    def body(a_ref, b_ref, o_ref, ob_ref, acc):
        kk = pl.program_id(2)
        prod = lax.dot_general(a_ref[...], b_ref[...], (((0,), (0,)), ((), ())), preferred_element_type=F32)

        def finish(total):
            o_ref[...] = total
            ob_ref[...] = total.astype(BF16)

        if nk == 1:
            finish(prod)
        else:
            @pl.when(kk == 0)
            def _():
                acc[...] = prod

            @pl.when(kk > 0)
            def _():
                acc[...] += prod

            @pl.when(kk == nk - 1)
            def _():
                finish(acc[...])

    out_spec = pl.BlockSpec((None, tm, tn), lambda i, j, kk: (j // cpn, i, j % cpn))
    return pl.pallas_call(
        body, name=name, grid=(m // tm, n // tn, nk),
        in_specs=[pl.BlockSpec((tk, tm), lambda i, j, kk: (kk, i)),
                  pl.BlockSpec((tk, tn), lambda i, j, kk: (kk, j))],
        out_specs=[out_spec, out_spec],
        out_shape=[jax.ShapeDtypeStruct((n_shards, m, c), F32), jax.ShapeDtypeStruct((n_shards, m, c), BF16)],
        scratch_shapes=[pltpu.VMEM((tm, tn) if nk > 1 else (8, 128), F32)],
        compiler_params=_params("parallel", "parallel", "arbitrary"),
    )(a, b)


def _row_spec(tr, width, col=0):
    return pl.BlockSpec((tr, width), lambda i, col=col: (i, col))


def _vec_spec(width, col=0):
    return pl.BlockSpec((1, width), lambda i, col=col: (0, col))


def _accumulate(ref, value, first):
    @pl.when(first)
    def _():
        ref[...] = value

    @pl.when(jnp.logical_not(first))
    def _():
        ref[...] += value


def _rmsnorm_fwd(x, g, *, name, tr=512):
    t, d = x.shape
    tr = _pick(t, tr, 8)

    def body(x_ref, g_ref, o_ref):
        xv = x_ref[...]
        r = lax.rsqrt(jnp.mean(xv * xv, axis=-1, keepdims=True) + RMS_EPS)
        o_ref[...] = (xv * r * g_ref[...]).astype(BF16)

    return pl.pallas_call(
        body, name=name, grid=(t // tr,),
        in_specs=[_row_spec(tr, d), _vec_spec(d)], out_specs=_row_spec(tr, d),
        out_shape=jax.ShapeDtypeStruct((t, d), BF16), compiler_params=_params("parallel"),
    )(x, g)


CONV_ROWS = 16


def _seq_specs(dims, ts, width, halo, col, *, nxt=False):
    nst, per = dims.seq // ts, ts // halo
    last = dims.tokens // halo - 1
    cur = pl.BlockSpec((ts, width), lambda b, i: (b * nst + i, col))
    if nxt:
        edge = pl.BlockSpec((halo, width), lambda b, i: (jnp.minimum((b * nst + i + 1) * per, last), col))
    else:
        edge = pl.BlockSpec((halo, width), lambda b, i: (jnp.maximum((b * nst + i) * per - 1, 0), col))
    return cur, edge


SUBLANES = 8


def _shifted_copies(buf, shifted):
    rows = shifted.shape[1]
    for s in range(1, SUBLANES):
        shifted[s - 1] = buf[pl.ds(s, rows), :]


def _window(buf, shifted, start, size):
    a, s = divmod(start, SUBLANES)
    src = buf if s == 0 else shifted.at[s - 1]
    return src[pl.ds(SUBLANES * a, size), :]


def _conv_branch_fwd(z, w, b, g, dims, *, name, ts=128):
    t, c, kw = z.shape[0], dims.d_model, dims.conv_width
    base = CONV_HALO - (kw - 1)

    def body(av_ref, hv_ref, ag_ref, hg_ref, w_ref, b_ref, g_ref, a1_ref, a3_ref, buf, shifted):
        i = pl.program_id(1)
        buf[CONV_HALO:, :] = av_ref[...].astype(F32) * _sigmoid(ag_ref[...].astype(F32))
        buf[0:CONV_HALO, :] = jnp.where(i > 0, hv_ref[...].astype(F32) * _sigmoid(hg_ref[...].astype(F32)), 0.0)
        _shifted_copies(buf, shifted)
        for r0 in range(0, ts, CONV_ROWS):
            acc = jnp.broadcast_to(b_ref[...], (CONV_ROWS, c))
            for k in range(kw):
                acc = acc + w_ref[k:k + 1, :] * _window(buf, shifted, r0 + base + k, CONV_ROWS)
            a1_ref[r0:r0 + CONV_ROWS, :] = acc
            a2 = acc * lax.rsqrt(jnp.mean(acc * acc, axis=-1, keepdims=True) + RMS_EPS) * g_ref[...]
            a3_ref[r0:r0 + CONV_ROWS, :] = (a2 * _sigmoid(a2)).astype(BF16)

    vec = pl.BlockSpec((1, c), lambda b, i: (0, 0))
    out = pl.BlockSpec((ts, c), lambda b, i: (b * (dims.seq // ts) + i, 0))
    return pl.pallas_call(
        body, name=name, grid=(dims.batch_local, dims.seq // ts),
        in_specs=[*_seq_specs(dims, ts, c, CONV_HALO, 0), *_seq_specs(dims, ts, c, CONV_HALO, 1),
                  pl.BlockSpec((CONV_HALO, c), lambda b, i: (0, 0)), vec, vec],
        out_specs=[out, out],
        out_shape=[jax.ShapeDtypeStruct((t, c), F32), jax.ShapeDtypeStruct((t, c), BF16)],
        scratch_shapes=[pltpu.VMEM((CONV_HALO + ts, c), F32),
                        pltpu.VMEM((SUBLANES - 1, CONV_HALO + ts - SUBLANES, c), F32)],
        compiler_params=_params("parallel", "parallel"),
    )(z, z, z, z, w, b, g)


def _conv_branch_bwd(da1, z, w, rest_of_dz, dims, *, name, ts=128):
    t, c, kw = z.shape[0], dims.d_model, dims.conv_width
    nst = dims.seq // ts
    base = CONV_HALO - (kw - 1)
    n_rest = len(rest_of_dz)
    total = 2 * c + sum(r.shape[1] for r in rest_of_dz)

    def body(d_ref, dn_ref, av_ref, hv_ref, ag_ref, hg_ref, w_ref, *more):
        rest_refs = more[:n_rest]
        dz_ref, dw_ref, db_ref, abuf, dbuf, ashift, dshift = more[n_rest:]
        col = 2 * c
        for r in rest_refs:
            dz_ref[:, col:col + r.shape[1]] = r[...]
            col += r.shape[1]
        i = pl.program_id(1)
        first = jnp.logical_and(pl.program_id(0) == 0, i == 0)
        abuf[CONV_HALO:, :] = av_ref[...].astype(F32) * _sigmoid(ag_ref[...].astype(F32))
        abuf[0:CONV_HALO, :] = jnp.where(i > 0, hv_ref[...].astype(F32) * _sigmoid(hg_ref[...].astype(F32)), 0.0)
        d1 = d_ref[...]
        dbuf[0:ts, :] = d1
        dbuf[ts:, :] = jnp.where(i < nst - 1, dn_ref[...], 0.0)
        _shifted_copies(abuf, ashift)
        _shifted_copies(dbuf, dshift)

        @pl.when(first)
        def _():
            dw_ref[...] = jnp.zeros_like(dw_ref)
            db_ref[...] = jnp.zeros_like(db_ref)

        db_ref[...] += jnp.sum(d1, axis=0, keepdims=True)
        for k in range(kw):
            dw_ref[k:k + 1, :] += jnp.sum(d1 * _window(abuf, ashift, base + k, ts), axis=0, keepdims=True)
        for r0 in range(0, ts, CONV_ROWS):
            acc = jnp.zeros((CONV_ROWS, c), F32)
            for k in range(kw):
                acc = acc + w_ref[k:k + 1, :] * _window(dbuf, dshift, r0 + (kw - 1) - k, CONV_ROWS)
            av = av_ref[r0:r0 + CONV_ROWS, :].astype(F32)
            sg = _sigmoid(ag_ref[r0:r0 + CONV_ROWS, :].astype(F32))
            dz_ref[r0:r0 + CONV_ROWS, 0:c] = (acc * sg).astype(BF16)
            dz_ref[r0:r0 + CONV_ROWS, c:2 * c] = (acc * av * sg * (1.0 - sg)).astype(BF16)

    cur, nxt = _seq_specs(dims, ts, c, CONV_HALO, 0, nxt=True)
    return pl.pallas_call(
        body, name=name, grid=(dims.batch_local, nst),
        in_specs=[cur, nxt, *_seq_specs(dims, ts, c, CONV_HALO, 0), *_seq_specs(dims, ts, c, CONV_HALO, 1),
                  pl.BlockSpec((CONV_HALO, c), lambda b, i: (0, 0))]
        + [pl.BlockSpec((ts, r.shape[1]), lambda b, i: (b * nst + i, 0)) for r in rest_of_dz],
        out_specs=[pl.BlockSpec((ts, total), lambda b, i: (b * nst + i, 0)),
                   pl.BlockSpec((CONV_HALO, c), lambda b, i: (0, 0)), pl.BlockSpec((1, c), lambda b, i: (0, 0))],
        out_shape=[jax.ShapeDtypeStruct((t, total), BF16), jax.ShapeDtypeStruct((CONV_HALO, c), F32),
                   jax.ShapeDtypeStruct((1, c), F32)],
        scratch_shapes=[pltpu.VMEM((CONV_HALO + ts, c), F32)] * 2
        + [pltpu.VMEM((SUBLANES - 1, CONV_HALO + ts - SUBLANES, c), F32)] * 2,
        compiler_params=_params("arbitrary", "arbitrary"),
    )(da1, da1, z, z, z, z, w, *rest_of_dz)


FFN_ROWS = 16
FFN_COLS = 256


def _ffn_chunks(ts, f):
    cw = _pick(f, FFN_COLS)
    return [(r0, c0, cw) for r0 in range(0, ts, FFN_ROWS) for c0 in range(0, f, cw)]


def _tap_sources(buf, moved, offsets, rows):
    taps, used = [], 0
    for off in offsets:
        if off % SUBLANES:
            moved[used] = buf[pl.ds(off, rows), :]
            taps.append((moved.at[used], 0))
            used += 1
        else:
            taps.append((buf, off))
    return taps


def _moved_copies(offsets):
    return sum(1 for off in offsets if off % SUBLANES)


def _taps_sum(taps, w_ref, init, r0, cols):
    for k, (src, off) in enumerate(taps):
        init = init + w_ref[k:k + 1, cols] * src[pl.ds(off + r0, init.shape[0]), cols]
    return init


def _ffn_bwd(dact, up, w, b, dims, *, name, ts=128):
    t, f, kw = up.shape[0], dims.d_ff, dims.ffn_conv_width
    nst = dims.seq // ts
    fwd_offsets = [FFN_HALO - (kw - 1) + k for k in range(kw)]
    bwd_offsets = [(kw - 1) - k for k in range(kw)]
    dact_halo = 2 * FFN_HALO

    def body(d_ref, dn_ref, up_ref, hp_ref, hn_ref, w_ref, b_ref, o_ref, dw_ref, db_ref, buf, moved, dbuf, dmoved):
        i = pl.program_id(1)
        first = jnp.logical_and(pl.program_id(0) == 0, i == 0)
        more = i < nst - 1
        buf[0:FFN_HALO, :] = jnp.where(i > 0, hp_ref[...], 0.0)
        buf[FFN_HALO:FFN_HALO + ts, :] = up_ref[...]
        buf[FFN_HALO + ts:, :] = hn_ref[...]
        taps = _tap_sources(buf, moved, fwd_offsets, ts + FFN_HALO)

        def du_chunk(r0, rows, c0, cw, d):
            vcols, gcols = slice(c0, c0 + cw), slice(f + c0, f + c0 + cw)
            uv = _taps_sum(taps, w_ref, jnp.broadcast_to(b_ref[:, vcols], (rows, cw)), r0, vcols)
            ug = _taps_sum(taps, w_ref, jnp.broadcast_to(b_ref[:, gcols], (rows, cw)), r0, gcols)
            sg = _sigmoid(ug)
            dbuf[r0:r0 + rows, vcols] = d * ug * sg
            dbuf[r0:r0 + rows, gcols] = d * uv * sg * (1.0 + ug * (1.0 - sg))

        for r0, c0, cw in _ffn_chunks(ts, f):
            du_chunk(r0, FFN_ROWS, c0, cw, d_ref[r0:r0 + FFN_ROWS, c0:c0 + cw].astype(F32))
        for _, c0, cw in _ffn_chunks(FFN_ROWS, f):
            d_next = dn_ref[:, c0:c0 + cw].astype(F32)[0:FFN_HALO]
            du_chunk(ts, FFN_HALO, c0, cw, jnp.where(more, d_next, 0.0))

        @pl.when(first)
        def _():
            dw_ref[...] = jnp.zeros_like(dw_ref)
            db_ref[...] = jnp.zeros_like(db_ref)

        du = dbuf[0:ts, :]
        db_ref[...] += jnp.sum(du, axis=0, keepdims=True)
        for k, (src, off) in enumerate(taps):
            dw_ref[k:k + 1, :] += jnp.sum(du * src[pl.ds(off, ts), :], axis=0, keepdims=True)

        dtaps = _tap_sources(dbuf, dmoved, bwd_offsets, ts)
        for r0, c0, cw in _ffn_chunks(ts, 2 * f):
            cols = slice(c0, c0 + cw)
            o_ref[r0:r0 + FFN_ROWS, cols] = _taps_sum(dtaps, w_ref, jnp.zeros((FFN_ROWS, cw), F32), r0, cols).astype(BF16)

    up_cur, up_prev = _seq_specs(dims, ts, 2 * f, FFN_HALO, 0)
    _, up_next = _seq_specs(dims, ts, 2 * f, FFN_HALO, 0, nxt=True)
    d_cur, d_next = _seq_specs(dims, ts, f, dact_halo, 0, nxt=True)
    full = lambda rows: pl.BlockSpec((rows, 2 * f), lambda b_, i: (0, 0))
    return pl.pallas_call(
        body, name=name, grid=(dims.batch_local, nst),
        in_specs=[d_cur, d_next, up_cur, up_prev, up_next, full(FFN_HALO), full(1)],
        out_specs=[pl.BlockSpec((ts, 2 * f), lambda b_, i: (b_ * nst + i, 0)), full(FFN_HALO), full(1)],
        out_shape=[jax.ShapeDtypeStruct((t, 2 * f), BF16), jax.ShapeDtypeStruct((FFN_HALO, 2 * f), F32),
                   jax.ShapeDtypeStruct((1, 2 * f), F32)],
        scratch_shapes=[pltpu.VMEM((ts + 2 * FFN_HALO, 2 * f), F32),
                        pltpu.VMEM((_moved_copies(fwd_offsets), ts + FFN_HALO, 2 * f), F32),
                        pltpu.VMEM((ts + FFN_HALO, 2 * f), F32),
                        pltpu.VMEM((_moved_copies(bwd_offsets), ts, 2 * f), F32)],
        compiler_params=_params("arbitrary", "arbitrary"),
    )(dact, dact, up, up, up, w, b)


def _ffn_act_fwd(up, w, b, dims, *, name, ts=128):
    t, f, kw = up.shape[0], dims.d_ff, dims.ffn_conv_width
    offsets = [FFN_HALO - (kw - 1) + k for k in range(kw)]

    def body(up_ref, h_ref, w_ref, b_ref, o_ref, buf, moved):
        buf[FFN_HALO:, :] = up_ref[...]
        buf[0:FFN_HALO, :] = jnp.where(pl.program_id(1) > 0, h_ref[...], 0.0)
        taps = _tap_sources(buf, moved, offsets, ts)
        for r0, c0, cw in _ffn_chunks(ts, f):
            vcols, gcols = slice(c0, c0 + cw), slice(f + c0, f + c0 + cw)
            uv = _taps_sum(taps, w_ref, jnp.broadcast_to(b_ref[:, vcols], (FFN_ROWS, cw)), r0, vcols)
            ug = _taps_sum(taps, w_ref, jnp.broadcast_to(b_ref[:, gcols], (FFN_ROWS, cw)), r0, gcols)
            o_ref[r0:r0 + FFN_ROWS, vcols] = (ug * _sigmoid(ug) * uv).astype(BF16)

    full = lambda rows: pl.BlockSpec((rows, 2 * f), lambda b_, i: (0, 0))
    return pl.pallas_call(
        body, name=name, grid=(dims.batch_local, dims.seq // ts),
        in_specs=[*_seq_specs(dims, ts, 2 * f, FFN_HALO, 0), full(FFN_HALO), full(1)],
        out_specs=pl.BlockSpec((ts, f), lambda b_, i: (b_ * (dims.seq // ts) + i, 0)),
        out_shape=jax.ShapeDtypeStruct((t, f), BF16),
        scratch_shapes=[pltpu.VMEM((FFN_HALO + ts, 2 * f), F32), pltpu.VMEM((_moved_copies(offsets), ts, 2 * f), F32)],
        compiler_params=_params("parallel", "parallel"),
    )(up, up, w, b)


def _dot_nt(a, b):
    return lax.dot_general(a, b, (((1,), (1,)), ((), ())), preferred_element_type=F32)


def _dot_tn(a, b):
    return lax.dot_general(a, b, (((0,), (0,)), ((), ())), preferred_element_type=F32)


LANES = 128
MASK_BIAS = 1e30
RESIDUE_DILATIONS = tuple(d for d in DILATIONS if d > 1)


def _rows_to_residues(value, out_ref, scr, d):
    rows, width = value.shape
    for c in range(width // LANES):
        cols = slice(LANES * c, LANES * (c + 1))
        scr[c] = value[:, cols]
        for r in range(d):
            out_ref[r, :, cols] = scr[c, pl.ds(r, rows // d, stride=d), :].astype(out_ref.dtype)


def _residues_to_rows(in_ref, scr, d):
    _, n, width = in_ref.shape
    slabs = []
    for c in range(width // LANES):
        cols = slice(LANES * c, LANES * (c + 1))
        for r in range(d):
            scr[c, pl.ds(r, n, stride=d), :] = in_ref[r, :, cols].astype(F32)
        slabs.append(scr[c])
    return slabs[0] if len(slabs) == 1 else jnp.concatenate(slabs, axis=1)


def _residue_shape(dims, d, width, dtype):
    return jax.ShapeDtypeStruct((dims.batch_local, d, dims.seq // d, width), dtype)


def _residue_spec(dims, d, tr, width):
    tiles = dims.seq // tr
    return pl.BlockSpec((None, d, tr // d, width), lambda i: (i // tiles, 0, i % tiles, 0))


def _head_sum_matrix(dims):
    a = dims.n_heads * dims.head_dim
    head = jnp.arange(a, dtype=jnp.int32) // dims.head_dim
    return (head[:, None] == jnp.arange(LANES, dtype=jnp.int32)[None, :]).astype(BF16)


def _two_pass_dot(v, m):
    hi = v.astype(BF16)
    lo = (v - hi.astype(F32)).astype(BF16)
    return jnp.dot(hi, m, preferred_element_type=F32) + jnp.dot(lo, m, preferred_element_type=F32)


def _residue_permutations(tr):
    out = []
    for d in RESIDUE_DILATIONS:
        dst = jnp.arange(tr, dtype=jnp.int32)
        src = d * (dst % (tr // d)) + dst // (tr // d)
        out.append((src[:, None] == jnp.arange(tr, dtype=jnp.int32)[None, :]).astype(BF16))
    return out


def _bf16_rows_to_residues(value, out_ref, perm_ref, d):
    n = value.shape[0] // d
    moved = jnp.dot(perm_ref[...], value, preferred_element_type=F32).astype(out_ref.dtype)
    for r in range(d):
        out_ref[r] = moved[r * n:(r + 1) * n]


def _bf16_residues_to_rows(in_ref, back_ref):
    d = in_ref.shape[0]
    stacked = jnp.concatenate([in_ref[r] for r in range(d)], axis=0)
    return jnp.dot(back_ref[...], stacked, preferred_element_type=F32)


def _qkv_layouts_fwd(z, gq, gk, head_ones, dims, *, name, tr=256):
    t = z.shape[0]
    a = dims.n_heads * dims.head_dim
    q_scale = dims.head_dim ** -0.5
    nres = len(RESIDUE_DILATIONS)

    def body(q_ref, k_ref, v_ref, gq_ref, gk_ref, sum_ref, spread_ref, *rest):
        perm_refs, outs = rest[:nres], rest[nres:]
        qv, kv = q_ref[...].astype(F32), k_ref[...].astype(F32)
        mean = lambda val: _two_pass_dot(_two_pass_dot(val, sum_ref[...]), spread_ref[...]) * (1.0 / dims.head_dim)
        rq = lax.rsqrt(mean(qv * qv) + RMS_EPS)
        rk = lax.rsqrt(mean(kv * kv) + RMS_EPS)
        values = ((qv * rq * gq_ref[...] * q_scale).astype(BF16), (kv * rk * gk_ref[...]).astype(BF16), v_ref[...])
        for j, val in enumerate(values):
            outs[j][...] = val
            for g, d in enumerate(RESIDUE_DILATIONS):
                _bf16_rows_to_residues(val, outs[3 * (g + 1) + j], perm_refs[g], d)

    out_specs = [_row_spec(tr, a)] * 3
    out_shape = [jax.ShapeDtypeStruct((t, a), BF16)] * 3
    for d in RESIDUE_DILATIONS:
        out_specs += [_residue_spec(dims, d, tr, a)] * 3
        out_shape += [_residue_shape(dims, d, a, BF16)] * 3
    outs = pl.pallas_call(
        body, name=name, grid=(t // tr,),
        in_specs=[_row_spec(tr, a, 2), _row_spec(tr, a, 3), _row_spec(tr, a, 4), _vec_spec(a), _vec_spec(a),
                  pl.BlockSpec((a, LANES), lambda i: (0, 0)), pl.BlockSpec((LANES, a), lambda i: (0, 0))]
        + [pl.BlockSpec((tr, tr), lambda i: (0, 0))] * nres,
        out_specs=out_specs, out_shape=out_shape,
        compiler_params=_params("parallel"),
    )(z, z, z, gq, gk, *head_ones, *_residue_permutations(tr))
    return {d: tuple(outs[3 * g:3 * g + 3]) for g, d in enumerate((1,) + RESIDUE_DILATIONS)}


ATTN_RESIDUES_PER_STEP = 4
ATTN_RESIDUES_PER_STEP_WINDOWED = 2


def _attn_groups(dims, dil):
    return (dims.batch_local, dil) if dil > 1 else (1, dims.batch_local)


def _attn_array(x, dims, dil):
    return x if dil > 1 else x.reshape(1, dims.batch_local, dims.seq, x.shape[-1])


def _attn_residues(dims, dil):
    one_block = dims.seq // dil == ATTN_BLOCK
    return math.gcd(_attn_groups(dims, dil)[1], ATTN_RESIDUES_PER_STEP if one_block else ATTN_RESIDUES_PER_STEP_WINDOWED)


def _per_residue(body, rs):
    if rs == 1:
        return body

    def stepped(*refs):
        for r in range(rs):
            body(*[ref.at[r] for ref in refs])

    return stepped


def _attn_specs(dims, dil, width):
    blk = ATTN_BLOCK
    nb = dims.seq // dil // blk
    rs = _attn_residues(dims, dil)
    lead, groups = _attn_groups(dims, dil)
    if rs > 1 and nb == 1:
        grid = (lead, groups // rs)
        at = lambda f: pl.BlockSpec((None, rs, blk, width), lambda b, r: (b, r, 0, 0))
    elif rs > 1:
        grid = (lead, groups // rs, nb)
        at = lambda f: pl.BlockSpec((None, rs, blk, width), lambda b, r, i: (b, r, f(i), 0))
    else:
        grid = (lead, groups, nb)
        at = lambda f: pl.BlockSpec((None, None, blk, width), lambda b, r, i: (b, r, f(i), 0))
    return grid, at(lambda i: i), at(lambda i: jnp.maximum(i - 1, 0)), at(lambda i: jnp.minimum(i + 1, nb - 1))


def _head_slopes(n_heads):
    h = lax.broadcasted_iota(jnp.int32, (n_heads, 1, 1), 0).astype(F32)
    return jnp.exp((h + 1.0) * (-8.0 / n_heads * math.log(2.0)))


def _pair_masks(hd):
    low = lax.broadcasted_iota(jnp.int32, (1, 2 * hd), 1) < hd
    return low, jnp.logical_not(low)


def _attn_fwd(q, k, v, dims, dil, *, name):
    a = dims.n_heads * dims.head_dim
    heads, hd, blk = dims.n_heads, dims.head_dim, ATTN_BLOCK
    assert 2 * hd == LANES and heads % 2 == 0 and heads <= LANES
    nb = dims.seq // dil // blk
    has_prev = nb > 1
    nkeys = 2 * blk if has_prev else blk
    grid, cur, prev, _ = _attn_specs(dims, dil, a)
    _, cur_stat, _, _ = _attn_specs(dims, dil, LANES)

    def body(*refs):
        if has_prev:
            q_ref, kc_ref, vc_ref, kp_ref, vp_ref, o_ref, lse_ref, s_scr, p_scr, k_st, v_st = refs
            k_st[0:blk, :], k_st[blk:, :] = kp_ref[...], kc_ref[...]
            v_st[0:blk, :], v_st[blk:, :] = vp_ref[...], vc_ref[...]
        else:
            q_ref, k_st, v_st, o_ref, lse_ref, s_scr, p_scr = refs
        low, high = _pair_masks(hd)

        for hp in range(heads // 2):
            sl = slice(LANES * hp, LANES * (hp + 1))
            q2 = q_ref[:, sl]
            kcat = k_st[:, sl]
            s_scr[2 * hp] = _dot_nt(jnp.where(low, q2, jnp.zeros_like(q2)), kcat)
            s_scr[2 * hp + 1] = _dot_nt(jnp.where(high, q2, jnp.zeros_like(q2)), kcat)

        iq = lax.broadcasted_iota(jnp.int32, (blk, nkeys), 0)
        jk = lax.broadcasted_iota(jnp.int32, (blk, nkeys), 1)
        if has_prev:
            steps = iq + blk - jk
            valid = (steps >= 0) & (steps <= blk) & ((jk >= blk) | (pl.program_id(len(grid) - 1) > 0))
        else:
            steps = iq - jk
            valid = steps >= 0
        bias = jnp.where(valid, steps.astype(F32) * (-float(dil)), -MASK_BIAS)
        s = s_scr[...] + _head_slopes(heads) * bias[None]
        m = jnp.max(s, axis=-1, keepdims=True)
        p = jnp.exp(s - m)
        l = jnp.sum(p, axis=-1, keepdims=True)
        p_scr[...] = p.astype(BF16)
        inv = 1.0 / l
        lse = m + jnp.log(l)

        lane = lax.broadcasted_iota(jnp.int32, (blk, LANES), 1)
        stat = jnp.zeros((blk, LANES), F32)
        for hp in range(heads // 2):
            sl = slice(LANES * hp, LANES * (hp + 1))
            vcat = v_st[:, sl]
            pv_a = jnp.dot(p_scr[2 * hp], vcat, preferred_element_type=F32) * inv[2 * hp]
            pv_b = jnp.dot(p_scr[2 * hp + 1], vcat, preferred_element_type=F32) * inv[2 * hp + 1]
            o_ref[:, sl] = jnp.where(low, pv_a, pv_b).astype(BF16)
            stat = jnp.where(lane == 2 * hp, lse[2 * hp], stat)
            stat = jnp.where(lane == 2 * hp + 1, lse[2 * hp + 1], stat)
        lse_ref[...] = stat

    q4, k4, v4 = (_attn_array(x, dims, dil) for x in (q, k, v))
    rs = _attn_residues(dims, dil)
    per_step = lambda shape: shape if rs == 1 else (rs,) + shape
    o, lse = pl.pallas_call(
        _per_residue(body, rs), name=name, grid=grid,
        in_specs=[cur, cur, cur] + ([prev, prev] if has_prev else []),
        out_specs=[cur, cur_stat],
        out_shape=[jax.ShapeDtypeStruct(q4.shape, BF16), jax.ShapeDtypeStruct(q4.shape[:-1] + (LANES,), F32)],
        scratch_shapes=[pltpu.VMEM(per_step((heads, blk, nkeys)), F32), pltpu.VMEM(per_step((heads, blk, nkeys)), BF16)]
        + ([pltpu.VMEM(per_step((nkeys, a)), BF16)] * 2 if has_prev else []),
        compiler_params=_params(*["parallel"] * len(grid)),
    )(q4, k4, v4, *([k4, v4] if has_prev else []))
    return o.reshape(q.shape), lse.reshape(q.shape[:-1] + (LANES,))


def _attn_combine(groups, head_spread, dims, *, name, tr=256):
    t = dims.tokens
    a = dims.n_heads * dims.head_dim
    dils = tuple(groups)

    nres = len(RESIDUE_DILATIONS)

    def body(*refs):
        ins = refs[:2 * len(dils)]
        x_ref = refs[2 * len(dils)]
        back_refs = dict(zip(RESIDUE_DILATIONS, refs[2 * len(dils) + 1:2 * len(dils) + 1 + nres]))
        o_ref = refs[2 * len(dils) + 1 + nres]
        lse_refs = refs[2 * len(dils) + 2 + nres:-1]
        scr_stat = refs[-1]
        outs, stats = [], []
        for g, d in enumerate(dils):
            if d == 1:
                outs.append(ins[2 * g][...].astype(F32))
                stats.append(ins[2 * g + 1][...])
            else:
                outs.append(_bf16_residues_to_rows(ins[2 * g], back_refs[d]))
                stats.append(_residues_to_rows(ins[2 * g + 1], scr_stat, d))
        top = functools.reduce(jnp.maximum, stats)
        weights = [jnp.exp(s - top) for s in stats]
        total = functools.reduce(jnp.add, weights)
        joint = top + jnp.log(total)
        inv = 1.0 / total
        acc = None
        for w, o in zip(weights, outs):
            term = _two_pass_dot(w * inv, x_ref[...]) * o
            acc = term if acc is None else acc + term
        o_ref[...] = acc.astype(BF16)
        for g, d in enumerate(dils):
            if d == 1:
                lse_refs[g][...] = joint
            else:
                _rows_to_residues(joint, lse_refs[g], scr_stat, d)

    in_specs, args, lse_specs, lse_shapes = [], [], [], []
    for d in dils:
        if d == 1:
            in_specs += [_row_spec(tr, a), _row_spec(tr, LANES)]
            lse_specs.append(_row_spec(tr, LANES))
            lse_shapes.append(jax.ShapeDtypeStruct((t, LANES), F32))
        else:
            in_specs += [_residue_spec(dims, d, tr, a), _residue_spec(dims, d, tr, LANES)]
            lse_specs.append(_residue_spec(dims, d, tr, LANES))
            lse_shapes.append(_residue_shape(dims, d, LANES, F32))
        args += list(groups[d])
    outs = pl.pallas_call(
        body, name=name, grid=(t // tr,),
        in_specs=in_specs + [pl.BlockSpec((LANES, a), lambda i: (0, 0))] + [pl.BlockSpec((tr, tr), lambda i: (0, 0))] * nres,
        out_specs=[_row_spec(tr, a)] + lse_specs,
        out_shape=[jax.ShapeDtypeStruct((t, a), BF16)] + lse_shapes,
        scratch_shapes=[pltpu.VMEM((1, tr, LANES), F32)],
        compiler_params=_params("parallel"),
    )(*args, head_spread, *[jnp.transpose(p) for p in _residue_permutations(tr)])
    return outs[0], dict(zip(dils, outs[1:]))


def _attn_bwd_prep(dy, w, o, head_sum, dims, *, name, after=None, tr=256):
    t, a = o.shape
    nres = len(RESIDUE_DILATIONS)

    def body(dy_ref, w_ref, o_ref, e_ref, *rest):
        perm_refs, outs, scr_stat = rest[:nres], rest[nres:-1], rest[-1]
        do = _dot_nt(dy_ref[...], w_ref[...]).astype(BF16)
        outs[0][...] = do
        delta = _two_pass_dot(do.astype(F32) * o_ref[...].astype(F32), e_ref[...])
        outs[1][...] = delta
        for g, d in enumerate(RESIDUE_DILATIONS):
            _bf16_rows_to_residues(do, outs[2 + 2 * g], perm_refs[g], d)
            _rows_to_residues(delta, outs[3 + 2 * g], scr_stat, d)

    out_specs = [_row_spec(tr, a), _row_spec(tr, LANES)]
    out_shape = [jax.ShapeDtypeStruct((t, a), BF16), jax.ShapeDtypeStruct((t, LANES), F32)]
    for d in RESIDUE_DILATIONS:
        out_specs += [_residue_spec(dims, d, tr, a), _residue_spec(dims, d, tr, LANES)]
        out_shape += [_residue_shape(dims, d, a, BF16), _residue_shape(dims, d, LANES, F32)]
    n_in = 4 + nres
    body, more_specs, more_args = _ordered(body, n_in, after)
    outs = pl.pallas_call(
        body, name=name, grid=(t // tr,),
        in_specs=[_row_spec(tr, dy.shape[1]), pl.BlockSpec((None,) + w.shape[1:], lambda i: (0, 0, 0)), _row_spec(tr, a),
                  pl.BlockSpec((a, LANES), lambda i: (0, 0))] + [pl.BlockSpec((tr, tr), lambda i: (0, 0))] * nres + more_specs,
        out_specs=out_specs, out_shape=out_shape,
        scratch_shapes=[pltpu.VMEM((1, tr, LANES), F32)],
        compiler_params=_params("parallel"),
    )(dy, w, o, head_sum, *_residue_permutations(tr), *more_args)
    dos, deltas = {1: outs[0]}, {1: outs[1]}
    for g, d in enumerate(RESIDUE_DILATIONS):
        dos[d], deltas[d] = outs[2 + 2 * g], outs[3 + 2 * g]
    return dos, deltas


def _attn_bwd(q, k, v, do, lse, delta, dims, dil, *, name):
    a = dims.n_heads * dims.head_dim
    heads, hd, blk = dims.n_heads, dims.head_dim, ATTN_BLOCK
    nb = dims.seq // dil // blk
    has_next = nb > 1
    nq = 2 * blk if has_next else blk
    grid, cur, _, nxt = _attn_specs(dims, dil, a)
    _, cur_stat, _, nxt_stat = _attn_specs(dims, dil, LANES)

    def body(*refs):
        k_ref, v_ref, q_ref, do_ref, lse_ref, dl_ref = refs[:6]
        if has_next:
            qn_ref, don_ref, lsen_ref, dln_ref = refs[6:10]
            dq_ref, dk_ref, dv_ref, q_st, do_st, s_scr, dp_scr, p_scr, ds_scr, carry = refs[10:]
        else:
            dq_ref, dk_ref, dv_ref, q_st, do_st, s_scr, dp_scr, p_scr, ds_scr = refs[6:]
        j = pl.program_id(len(grid) - 1)
        low, high = _pair_masks(hd)
        q_st[0:blk, :] = q_ref[...]
        do_st[0:blk, :] = do_ref[...]
        if has_next:
            q_st[blk:, :] = qn_ref[...]
            do_st[blk:, :] = don_ref[...]
            lse_all = jnp.concatenate([lse_ref[...], lsen_ref[...]], axis=0)
            dl_all = jnp.concatenate([dl_ref[...], dln_ref[...]], axis=0)
        else:
            lse_all, dl_all = lse_ref[...], dl_ref[...]
        lse_t, dl_t = jnp.transpose(lse_all), jnp.transpose(dl_all)
        lse3 = jnp.stack([lse_t[h:h + 1, :] for h in range(heads)])
        dl3 = jnp.stack([dl_t[h:h + 1, :] for h in range(heads)])

        def halves(x):
            return jnp.where(low, x, jnp.zeros_like(x)), jnp.where(high, x, jnp.zeros_like(x))

        for hp in range(heads // 2):
            sl = slice(LANES * hp, LANES * (hp + 1))
            k2, v2 = k_ref[:, sl], v_ref[:, sl]
            q_a, q_b = halves(q_st[:, sl])
            do_a, do_b = halves(do_st[:, sl])
            s_scr[2 * hp], s_scr[2 * hp + 1] = _dot_nt(k2, q_a), _dot_nt(k2, q_b)
            dp_scr[2 * hp], dp_scr[2 * hp + 1] = _dot_nt(v2, do_a), _dot_nt(v2, do_b)

        jk = lax.broadcasted_iota(jnp.int32, (blk, nq), 0)
        rq = lax.broadcasted_iota(jnp.int32, (blk, nq), 1)
        if has_next:
            iq = jnp.where(rq < blk, rq, rq - blk)
            steps = jnp.where(rq < blk, iq - jk, iq - jk + blk)
            valid = ((rq < blk) & (iq >= jk)) | ((rq >= blk) & (jk >= iq) & (j + 1 < nb))
        else:
            steps, valid = rq - jk, rq >= jk
        bias = jnp.where(valid, steps.astype(F32) * (-float(dil)), -MASK_BIAS)
        p = jnp.exp(s_scr[...] + _head_slopes(heads) * bias[None] - lse3)
        p_scr[...] = p.astype(BF16)
        ds_scr[...] = (p * (dp_scr[...] - dl3)).astype(BF16)

        if has_next:
            @pl.when(j == 0)
            def _():
                carry[...] = jnp.zeros_like(carry)

        for hp in range(heads // 2):
            sl = slice(LANES * hp, LANES * (hp + 1))
            k2 = k_ref[:, sl]
            q_a, q_b = halves(q_st[:, sl])
            do_a, do_b = halves(do_st[:, sl])
            ds_a, ds_b = ds_scr[2 * hp], ds_scr[2 * hp + 1]
            dk_ref[:, sl] = (jnp.dot(ds_a, q_a, preferred_element_type=F32)
                             + jnp.dot(ds_b, q_b, preferred_element_type=F32)).astype(BF16)
            dv_ref[:, sl] = (jnp.dot(p_scr[2 * hp], do_a, preferred_element_type=F32)
                             + jnp.dot(p_scr[2 * hp + 1], do_b, preferred_element_type=F32)).astype(BF16)
            dq2 = jnp.where(low, _dot_tn(ds_a, k2), _dot_tn(ds_b, k2))
            if has_next:
                dq_ref[:, sl] = (carry[:, sl] + dq2[:blk]).astype(BF16)
                carry[:, sl] = dq2[blk:]
            else:
                dq_ref[:, sl] = dq2.astype(BF16)

    args, in_specs = [_attn_array(x, dims, dil) for x in (k, v, q, do, lse, delta)], [cur] * 4 + [cur_stat] * 2
    if has_next:
        args += [args[2], args[3], args[4], args[5]]
        in_specs += [nxt] * 2 + [nxt_stat] * 2
    shape = jax.ShapeDtypeStruct(args[2].shape, BF16)
    rs = _attn_residues(dims, dil)
    per_step = lambda dims_: dims_ if rs == 1 else (rs,) + dims_
    scratch = ([pltpu.VMEM(per_step((nq, a)), BF16)] * 2 + [pltpu.VMEM(per_step((heads, blk, nq)), F32)] * 2
               + [pltpu.VMEM(per_step((heads, blk, nq)), BF16)] * 2)
    if has_next:
        scratch.append(pltpu.VMEM(per_step((blk, a)), F32))
    grads = pl.pallas_call(
        _per_residue(body, rs), name=name, grid=grid, in_specs=in_specs, out_specs=[cur] * 3, out_shape=[shape] * 3,
        scratch_shapes=scratch,
        compiler_params=_params(*["parallel"] * (len(grid) - 1), "arbitrary"),
    )(*args)
    return tuple(g.reshape(q.shape) for g in grads)


def _qkv_layouts_bwd(z, grads, gq, gk, head_ones, dims, *, name, tr=256):
    t = z.shape[0]
    a = dims.n_heads * dims.head_dim
    q_scale = dims.head_dim ** -0.5
    dils = tuple(grads)
    nres = len(RESIDUE_DILATIONS)

    def body(q_ref, k_ref, *rest):
        d_refs = rest[:3 * len(dils)]
        gq_ref, gk_ref, sum_ref, spread_ref = rest[3 * len(dils):3 * len(dils) + 4]
        back_refs = dict(zip(RESIDUE_DILATIONS, rest[3 * len(dils) + 4:3 * len(dils) + 4 + nres]))
        dz_ref, dgq_ref, dgk_ref = rest[3 * len(dils) + 4 + nres:]
        first = pl.program_id(0) == 0
        mean = lambda val: _two_pass_dot(_two_pass_dot(val, sum_ref[...]), spread_ref[...]) * (1.0 / dims.head_dim)

        def total(j):
            acc = None
            for g, d in enumerate(dils):
                ref = d_refs[3 * g + j]
                part = ref[...].astype(F32) if d == 1 else _bf16_residues_to_rows(ref, back_refs[d])
                acc = part if acc is None else acc + part
            return acc

        def norm_bwd(x_ref, dy, g_ref, scale, col, dg_ref):
            xv = x_ref[...].astype(F32)
            dy = dy * scale
            r = lax.rsqrt(mean(xv * xv) + RMS_EPS)
            gy = dy * g_ref[...]
            dx = r * gy - xv * (r * r * r) * mean(xv * gy)
            dz_ref[:, col * a:(col + 1) * a] = dx.astype(BF16)
            _accumulate(dg_ref, jnp.sum(dy * xv * r, axis=0, keepdims=True), first)

        norm_bwd(q_ref, total(0), gq_ref, q_scale, 0, dgq_ref)
        norm_bwd(k_ref, total(1), gk_ref, 1.0, 1, dgk_ref)
        dz_ref[:, 2 * a:3 * a] = total(2).astype(BF16)

    in_specs, args = [_row_spec(tr, a, 2), _row_spec(tr, a, 3)], [z, z]
    for d in dils:
        in_specs += [_row_spec(tr, a) if d == 1 else _residue_spec(dims, d, tr, a)] * 3
        args += list(grads[d])
    in_specs += [_vec_spec(a), _vec_spec(a), pl.BlockSpec((a, LANES), lambda i: (0, 0)),
                 pl.BlockSpec((LANES, a), lambda i: (0, 0))] + [pl.BlockSpec((tr, tr), lambda i: (0, 0))] * nres
    return pl.pallas_call(
        body, name=name, grid=(t // tr,), in_specs=in_specs,
        out_specs=[_row_spec(tr, 3 * a), _vec_spec(a), _vec_spec(a)],
        out_shape=[jax.ShapeDtypeStruct((t, 3 * a), BF16)] + [jax.ShapeDtypeStruct((1, a), F32)] * 2,
        compiler_params=_params("arbitrary"),
    )(*args, gq, gk, *head_ones, *[jnp.transpose(p) for p in _residue_permutations(tr)])


def _mix_fwd(a3, o, w_a, w_b, w_out, z, gate_b, x, g2, dims, *, name, tr=512):
    t, d = x.shape
    tr = _pick(t, tr, 8)
    first_gate_col = z.shape[1] // d - 2

    def body(a_ref, o_ref, wa_ref, wb_ref, wo_ref, ga_ref, gb_ref, ba_ref, bb_ref, x_ref, g2_ref,
             ya_ref, yb_ref, mix_ref, x1_ref, h2_ref):
        ya = jnp.dot(a_ref[...], wa_ref[...], preferred_element_type=F32)
        yb = jnp.dot(o_ref[...], wb_ref[...], preferred_element_type=F32)
        ya_ref[...] = ya
        yb_ref[...] = yb
        g_a = _sigmoid(ga_ref[...].astype(F32) + ba_ref[...])
        g_b = _sigmoid(gb_ref[...].astype(F32) + bb_ref[...])
        mixed = (g_a * ya + g_b * yb).astype(BF16)
        mix_ref[...] = mixed
        x1 = x_ref[...] + jnp.dot(mixed, wo_ref[...], preferred_element_type=F32)
        x1_ref[...] = x1
        h2_ref[...] = (x1 * lax.rsqrt(jnp.mean(x1 * x1, axis=-1, keepdims=True) + RMS_EPS) * g2_ref[...]).astype(BF16)

    weight = pl.BlockSpec((None, d, d), lambda i: (0, 0, 0))
    rows = _row_spec(tr, d)
    return pl.pallas_call(
        body, name=name, grid=(t // tr,),
        in_specs=[rows, rows, weight, weight, weight, _row_spec(tr, d, first_gate_col),
                  _row_spec(tr, d, first_gate_col + 1), _vec_spec(d, 0), _vec_spec(d, 1), rows, _vec_spec(d)],
        out_specs=[rows] * 5,
        out_shape=[jax.ShapeDtypeStruct((t, d), dt) for dt in (F32, F32, BF16, F32, BF16)],
        compiler_params=_params("parallel"),
    )(a3, o, w_a, w_b, w_out, z, z, gate_b, gate_b, x, g2)


def _mix_bwd(dx, w, ya, yb, z, gate_b, dims, *, name, after=None, tr=512):
    t, d = ya.shape
    tr = _pick(t, tr, 8)
    first_gate_col = z.shape[1] // d - 2

    def body(dx_ref, w_ref, ya_ref, yb_ref, ga_ref, gb_ref, ba_ref, bb_ref, dya_ref, dyb_ref, dz_ref, db_ref):
        dm = _dot_nt(dx_ref[...], w_ref[...])
        g_a = _sigmoid(ga_ref[...].astype(F32) + ba_ref[...])
        g_b = _sigmoid(gb_ref[...].astype(F32) + bb_ref[...])
        dya_ref[...] = (dm * g_a).astype(BF16)
        dyb_ref[...] = (dm * g_b).astype(BF16)
        dl_a = dm * ya_ref[...] * g_a * (1.0 - g_a)
        dl_b = dm * yb_ref[...] * g_b * (1.0 - g_b)
        dz_ref[:, 0:d] = dl_a.astype(BF16)
        dz_ref[:, d:2 * d] = dl_b.astype(BF16)
        first = pl.program_id(0) == 0
        sums = jnp.concatenate([jnp.sum(dl_a, axis=0, keepdims=True), jnp.sum(dl_b, axis=0, keepdims=True)], axis=1)
        _accumulate(db_ref, sums, first)

    body, more_specs, more_args = _ordered(body, 8, after)
    return pl.pallas_call(
        body, name=name, grid=(t // tr,),
        in_specs=[_row_spec(tr, d), pl.BlockSpec((None, d, d), lambda i: (0, 0, 0)), _row_spec(tr, d), _row_spec(tr, d),
                  _row_spec(tr, d, first_gate_col), _row_spec(tr, d, first_gate_col + 1), _vec_spec(d, 0),
                  _vec_spec(d, 1)] + more_specs,
        out_specs=[_row_spec(tr, d), _row_spec(tr, d), _row_spec(tr, 2 * d), _vec_spec(2 * d)],
        out_shape=[jax.ShapeDtypeStruct((t, d), BF16)] * 2 + [jax.ShapeDtypeStruct((t, 2 * d), BF16),
                                                              jax.ShapeDtypeStruct((1, 2 * d), F32)],
        compiler_params=_params("arbitrary"),
    )(dx, w, ya, yb, z, z, gate_b, gate_b, *more_args)


def _adamw(w, grads, m, v, *, name, tr=256):
    r, c = w.shape
    tr = _pick(r, tr, 8)
    ng = len(grads)
    c1 = 1.0 - ADAM_B1 ** ADAM_STEP
    c2 = 1.0 - ADAM_B2 ** ADAM_STEP

    def body(*refs):
        w_ref, g_refs, m_ref, v_ref = refs[0], refs[1:1 + ng], refs[1 + ng], refs[2 + ng]
        g_out, d_out, m_out, v_out = refs[3 + ng:]
        g = g_refs[0][...]
        for extra in g_refs[1:]:
            g = g + extra[...]
        m_new = ADAM_B1 * m_ref[...] + (1.0 - ADAM_B1) * g
        v_new = ADAM_B2 * v_ref[...] + (1.0 - ADAM_B2) * (g * g)
        g_out[...] = g
        m_out[...] = m_new
        v_out[...] = v_new
        d_out[...] = -ADAM_LR * ((m_new / c1) / (jnp.sqrt(v_new / c2) + ADAM_EPS) + ADAM_WD * w_ref[...])

    spec = pl.BlockSpec((tr, c), lambda i: (i, 0))
    return pl.pallas_call(
        body, name=name, grid=(r // tr,),
        in_specs=[spec] * (3 + ng), out_specs=[spec] * 4, out_shape=[jax.ShapeDtypeStruct((r, c), F32)] * 4,
        compiler_params=_params("parallel"),
    )(w, *grads, m, v)


CHIP_PEERS = ((1, 0), (0, 1), (1, 1))


def _place():
    return lax.axis_index("x"), lax.axis_index("y"), lax.axis_index("c")


HBM = pl.BlockSpec(memory_space=pltpu.HBM)
SEM = pl.BlockSpec(memory_space=pltpu.SEMAPHORE)
IN_FLIGHT = pltpu.SideEffectType.DATAFLOW_SIDE_EFFECTING


def _in_hbm(a):
    return pltpu.with_memory_space_constraint(a, pltpu.HBM)


def _cast_to_lands(shards, dtypes, *, name, after=None):
    n = len(shards)

    def body(*refs):
        ins, outs, bufs, sems = refs[:n], refs[n:2 * n], refs[2 * n:3 * n], refs[3 * n]
        x, y, _ = _place()
        copies = []
        for a in range(n):
            bufs[a][...] = ins[a][...].astype(dtypes[a])
            cp = pltpu.make_async_copy(bufs[a], outs[a].at[2 * x + y], sems.at[a])
            cp.start()
            copies.append(cp)
        for cp in copies:
            cp.wait()

    body, more_specs, more_args = _ordered(body, n, after)
    return pl.pallas_call(
        body, name=name, in_specs=[pl.BlockSpec(memory_space=pltpu.VMEM)] * n + more_specs, out_specs=[ANY] * n,
        out_shape=[jax.ShapeDtypeStruct((N_CHIPS,) + s.shape, dt) for s, dt in zip(shards, dtypes)],
        scratch_shapes=[pltpu.VMEM(s.shape, dt) for s, dt in zip(shards, dtypes)] + [pltpu.SemaphoreType.DMA((n,))],
        compiler_params=pltpu.CompilerParams(vmem_limit_bytes=V7X_VMEM_LIMIT_BYTES),
    )(*shards, *more_args)


def _chip_copy(src, dst, send, recv, flip, place):
    x, y, c = place
    return pltpu.make_async_remote_copy(src_ref=src, dst_ref=dst, send_sem=send, recv_sem=recv,
                                        device_id=(x ^ flip[0], y ^ flip[1], c), device_id_type=MESH)


def _my_part(land, place, halved):
    block = land.at[2 * place[0] + place[1]]
    if not halved:
        return block
    rows = land.shape[1] // 2
    return block.at[pl.ds(pl.multiple_of(place[2] * rows, rows), rows)]


def _gather_start(lands, after, *, name, halved=()):
    n = len(lands)

    def body(*refs):
        ins, send, recv, token = refs[:n], refs[n + 1], refs[n + 2], refs[-1]
        place = _place()
        for a in range(n):
            part = _my_part(ins[a], place, a in halved)
            for p, flip in enumerate(CHIP_PEERS):
                k = 3 * a + p
                _chip_copy(part, part, send.at[k], recv.at[k], flip, place).start()
        token[...] = jnp.zeros_like(token)

    outs = pl.pallas_call(
        body, name=name, in_specs=[HBM] * n + [ANY],
        out_specs=(SEM, SEM, *[HBM] * n, pl.BlockSpec(memory_space=pltpu.VMEM)),
        out_shape=(pltpu.SemaphoreType.DMA((3 * n,)), pltpu.SemaphoreType.DMA((3 * n,)),
                   *[pltpu.HBM(l.shape, l.dtype) for l in lands], jax.ShapeDtypeStruct((8, 128), F32)),
        input_output_aliases={a: 2 + a for a in range(n)},
        compiler_params=pltpu.CompilerParams(has_side_effects=IN_FLIGHT),
    )(*[_in_hbm(l) for l in lands], after)
    return outs[0], outs[1], list(outs[2:2 + n]), outs[-1]


def _gather_wait(send, recv, lands, after, *, name, halved=()):
    n = len(lands)

    def body(*refs):
        ins, send_ref, recv_ref = refs[:n], refs[n], refs[n + 1]
        place = _place()
        for a in range(n):
            part = _my_part(ins[a], place, a in halved)
            for p, flip in enumerate(CHIP_PEERS):
                k = 3 * a + p
                cp = _chip_copy(part, part, send_ref.at[k], recv_ref.at[k], flip, place)
                cp.wait_send()
                cp.wait_recv()

    after = list(after) if isinstance(after, (list, tuple)) else [after]
    return pl.pallas_call(
        body, name=name, in_specs=[HBM] * n + [SEM, SEM] + [ANY] * len(after), out_specs=[HBM] * n,
        out_shape=[pltpu.HBM(l.shape, l.dtype) for l in lands],
        input_output_aliases={a: a for a in range(n)},
        compiler_params=pltpu.CompilerParams(has_side_effects=IN_FLIGHT),
    )(*lands, send, recv, *after)


def _forward_to_sibling(land, *, name):
    rows = land.shape[1] // 2

    def body(land_ref, out_ref, send, recv):
        x, y, c = _place()
        copies = []
        for p, (fx, fy) in enumerate(CHIP_PEERS):
            chip = 2 * (x ^ fx) + (y ^ fy)
            mine = pl.ds(pl.multiple_of(c * rows, rows), rows)
            theirs = pl.ds(pl.multiple_of((1 - c) * rows, rows), rows)
            out = pltpu.make_async_remote_copy(
                src_ref=land_ref.at[chip].at[mine], dst_ref=out_ref.at[chip].at[mine], send_sem=send.at[p],
                recv_sem=recv.at[p], device_id=(x, y, 1 - c), device_id_type=MESH)
            out.start()
            copies.append((out, pltpu.make_async_remote_copy(
                src_ref=land_ref.at[chip].at[theirs], dst_ref=out_ref.at[chip].at[theirs], send_sem=send.at[p],
                recv_sem=recv.at[p], device_id=(x, y, 1 - c), device_id_type=MESH)))
        for out, arriving in copies:
            out.wait_send()
            arriving.wait_recv()

    return pl.pallas_call(
        body, name=name, in_specs=[ANY], out_specs=ANY, out_shape=jax.ShapeDtypeStruct(land.shape, land.dtype),
        input_output_aliases={0: 0},
        scratch_shapes=[pltpu.SemaphoreType.DMA((3,)), pltpu.SemaphoreType.DMA((3,))],
    )(land)


def _scatter_start(grad, *, name):
    def body(g_ref, land_ref, send, recv, g_thru, land_thru, token):
        place = _place()
        for p, flip in enumerate(CHIP_PEERS):
            peer_chip = 2 * (place[0] ^ flip[0]) + (place[1] ^ flip[1])
            _chip_copy(g_ref.at[peer_chip], land_ref.at[p], send.at[p], recv.at[p], flip, place).start()
        token[...] = jnp.zeros_like(token)

    land = lax.empty((3,) + grad.shape[1:], grad.dtype)
    return pl.pallas_call(
        body, name=name, in_specs=[HBM, HBM],
        out_specs=(SEM, SEM, HBM, HBM, pl.BlockSpec(memory_space=pltpu.VMEM)),
        out_shape=(pltpu.SemaphoreType.DMA((3,)), pltpu.SemaphoreType.DMA((3,)), pltpu.HBM(grad.shape, grad.dtype),
                   pltpu.HBM(land.shape, land.dtype), jax.ShapeDtypeStruct((8, 128), F32)),
        input_output_aliases={0: 2, 1: 3},
        compiler_params=pltpu.CompilerParams(has_side_effects=IN_FLIGHT),
    )(_in_hbm(grad), _in_hbm(land))


def _scatter_wait(started, after, *, name):
    n = len(started)

    def body(*refs):
        grads, lands = refs[:n], refs[n:2 * n]
        sends, recvs = refs[2 * n:3 * n], refs[3 * n:4 * n]
        place = _place()
        for a in range(n):
            for p, flip in enumerate(CHIP_PEERS):
                cp = _chip_copy(grads[a].at[0], lands[a].at[p], sends[a].at[p], recvs[a].at[p], flip, place)
                cp.wait_send()
                cp.wait_recv()

    grads, lands = [s[2] for s in started], [s[3] for s in started]
    after = list(after) if isinstance(after, (list, tuple)) else [after]
    outs = pl.pallas_call(
        body, name=name, in_specs=[HBM] * (2 * n) + [SEM] * (2 * n) + [ANY] * len(after), out_specs=[HBM] * (2 * n),
        out_shape=[pltpu.HBM(a.shape, a.dtype) for a in grads + lands],
        input_output_aliases={a: a for a in range(2 * n)},
        compiler_params=pltpu.CompilerParams(has_side_effects=IN_FLIGHT),
    )(*grads, *lands, *[s[0] for s in started], *[s[1] for s in started], *after)
    return list(zip(outs[:n], outs[n:]))


def _sibling_copy(src, dst, send, recv, place):
    x, y, c = place
    return pltpu.make_async_remote_copy(src_ref=src, dst_ref=dst, send_sem=send, recv_sem=recv,
                                        device_id=(x, y, 1 - c), device_id_type=MESH)


def _swap_start(arrays, *, name):
    n = len(arrays)

    def body(*refs):
        ins, lands, send, recv, token = refs[:n], refs[n:2 * n], refs[2 * n], refs[2 * n + 1], refs[-1]
        place = _place()
        for a in range(n):
            _sibling_copy(ins[a], lands[a], send.at[a], recv.at[a], place).start()
        token[...] = jnp.zeros_like(token)

    both = [_in_hbm(a) for a in arrays] + [_in_hbm(lax.empty(a.shape, a.dtype)) for a in arrays]
    outs = pl.pallas_call(
        body, name=name, in_specs=[HBM] * (2 * n),
        out_specs=(SEM, SEM, *[HBM] * (2 * n), pl.BlockSpec(memory_space=pltpu.VMEM)),
        out_shape=(pltpu.SemaphoreType.DMA((n,)), pltpu.SemaphoreType.DMA((n,)),
                   *[pltpu.HBM(a.shape, a.dtype) for a in both], jax.ShapeDtypeStruct((8, 128), F32)),
        input_output_aliases={a: 2 + a for a in range(2 * n)},
        compiler_params=pltpu.CompilerParams(has_side_effects=IN_FLIGHT),
    )(*both)
    return outs[0], outs[1], list(outs[2:2 + n]), list(outs[2 + n:2 + 2 * n]), outs[-1]


def _swap_wait(started, after, *, name):
    send, recv, arrays, lands = started[:4]
    n = len(arrays)

    def body(*refs):
        ins, zones, send_ref, recv_ref = refs[:n], refs[n:2 * n], refs[2 * n], refs[2 * n + 1]
        place = _place()
        for a in range(n):
            cp = _sibling_copy(ins[a], zones[a], send_ref.at[a], recv_ref.at[a], place)
            cp.wait_send()
            cp.wait_recv()

    after = list(after) if isinstance(after, (list, tuple)) else [after]
    outs = pl.pallas_call(
        body, name=name, in_specs=[HBM] * (2 * n) + [SEM, SEM] + [ANY] * len(after), out_specs=[HBM] * (2 * n),
        out_shape=[pltpu.HBM(a.shape, a.dtype) for a in arrays + lands],
        input_output_aliases={a: a for a in range(2 * n)},
        compiler_params=pltpu.CompilerParams(has_side_effects=IN_FLIGHT),
    )(*arrays, *lands, send, recv, *after)
    return list(outs[:n]), list(outs[n:])


def _allreduce_start(packed, *, name):
    n_dev = 8

    def body(src_ref, land_ref, send, recv, src_thru, land_thru, token):
        x, y, c = _place()
        me = 4 * x + 2 * y + c
        for p in range(1, n_dev):
            pltpu.make_async_remote_copy(
                src_ref=src_ref, dst_ref=land_ref.at[me], send_sem=send.at[p - 1], recv_sem=recv.at[p - 1],
                device_id=(x ^ (p >> 2), y ^ ((p >> 1) & 1), c ^ (p & 1)), device_id_type=MESH).start()
        token[...] = jnp.zeros_like(token)

    land = lax.empty((n_dev,) + packed.shape, packed.dtype)
    return pl.pallas_call(
        body, name=name, in_specs=[HBM, HBM],
        out_specs=(SEM, SEM, HBM, HBM, pl.BlockSpec(memory_space=pltpu.VMEM)),
        out_shape=(pltpu.SemaphoreType.DMA((n_dev - 1,)), pltpu.SemaphoreType.DMA((n_dev - 1,)),
                   pltpu.HBM(packed.shape, packed.dtype), pltpu.HBM(land.shape, land.dtype),
                   jax.ShapeDtypeStruct((8, 128), F32)),
        input_output_aliases={0: 2, 1: 3},
        compiler_params=pltpu.CompilerParams(has_side_effects=IN_FLIGHT),
    )(_in_hbm(packed), _in_hbm(land))


def _allreduce_wait(started, after, *, name):
    send, recv, packed, land = started[:4]
    n_dev = 8

    def body(src_ref, land_ref, send_ref, recv_ref, *_):
        x, y, c = _place()
        for p in range(1, n_dev):
            cp = pltpu.make_async_remote_copy(
                src_ref=src_ref, dst_ref=land_ref.at[0], send_sem=send_ref.at[p - 1], recv_sem=recv_ref.at[p - 1],
                device_id=(x ^ (p >> 2), y ^ ((p >> 1) & 1), c ^ (p & 1)), device_id_type=MESH)
            cp.wait_send()
            cp.wait_recv()

    after = list(after) if isinstance(after, (list, tuple)) else [after]
    return pl.pallas_call(
        body, name=name, in_specs=[HBM, HBM, SEM, SEM] + [ANY] * len(after), out_specs=[HBM, HBM],
        out_shape=[pltpu.HBM(packed.shape, packed.dtype), pltpu.HBM(land.shape, land.dtype)],
        input_output_aliases={0: 0, 1: 1},
        compiler_params=pltpu.CompilerParams(has_side_effects=IN_FLIGHT),
    )(packed, land, send, recv, *after)


def _sum_devices(mine, land, *, name):
    n_dev = land.shape[0]

    def body(mine_ref, land_ref, out_ref):
        x, y, c = _place()
        me = 4 * x + 2 * y + c
        total = None
        for s in range(n_dev):
            part = jnp.where(me == s, mine_ref[...], land_ref[s])
            total = part if total is None else total + part
        out_ref[...] = total

    return pl.pallas_call(body, name=name, out_shape=jax.ShapeDtypeStruct(mine.shape, mine.dtype))(mine, land)


def _sum_received(grad, land, *, name, tr=256):
    _, r, c = grad.shape
    tr = _pick(r, tr, 8)

    def body(chip_ref, g_ref, l_ref, o_ref):
        o_ref[...] = ((g_ref[...] + l_ref[0].astype(F32)) + l_ref[1].astype(F32)) + l_ref[2].astype(F32)

    chip = (2 * lax.axis_index("x") + lax.axis_index("y")).astype(jnp.int32).reshape(1)
    return pl.pallas_call(
        body, name=name,
        grid_spec=pltpu.PrefetchScalarGridSpec(
            num_scalar_prefetch=1, grid=(r // tr,),
            in_specs=[pl.BlockSpec((None, tr, c), lambda i, chip_ref: (chip_ref[0], i, 0)),
                      pl.BlockSpec((3, tr, c), lambda i, chip_ref: (0, i, 0))],
            out_specs=pl.BlockSpec((tr, c), lambda i, chip_ref: (i, 0))),
        out_shape=jax.ShapeDtypeStruct((r, c), F32), compiler_params=_params("parallel"),
    )(chip, grad, land)


def _packed_rows(size, d):
    return -(-size // (8 * d)) * 8


def _pack_rows(arrays, d):
    rows = []
    for arr in arrays:
        flat = arr.reshape(-1).astype(F32)
        n = _packed_rows(flat.shape[0], d)
        rows.append(jnp.pad(flat, (0, n * d - flat.shape[0])).reshape(n, d))
    return jnp.concatenate(rows, axis=0)


def _unpack_rows(packed, shapes, d):
    out, row = [], 0
    for shape in shapes:
        size = math.prod(shape)
        n = _packed_rows(size, d)
        out.append(packed[row:row + n].reshape(-1)[:size].reshape(shape))
        row += n
    return out


SMALL = ("norm1_g", "gate_b", "conv_b", "conv_norm_g", "q_norm_g", "k_norm_g", "norm2_g", "ffn_conv_b")
LARGE = ("w_in", "w_conv_out", "w_attn_out", "w_out", "w_up", "w_down")
WEIGHTS = ("norm1_g", "w_in", "gate_b", "conv_w", "conv_b", "conv_norm_g", "w_conv_out", "q_norm_g", "k_norm_g",
           "w_attn_out", "w_out", "norm2_g", "w_up", "ffn_conv_w", "ffn_conv_b", "w_down")


def _after(vec, token):
    return vec if token is None else vec + token[0:1, 0:1]


def _local_step(dims, x, target, small, first_weights, other_weights, send_grad):
    d, f, heads = dims.d_model, dims.d_ff, dims.n_heads
    small = dict(small)
    row = lambda name: small[name].reshape(1, -1)
    head_sum = _head_sum_matrix(dims)
    head_spread = jnp.transpose(head_sum)
    ones = (head_sum, head_spread)
    gq = jnp.tile(row("q_norm_g"), (1, heads))
    gk = jnp.tile(row("k_norm_g"), (1, heads))
    one_shard = lambda w: w.reshape(1, -1, w.shape[-1])

    h = _rmsnorm_fwd(x, row("norm1_g"), name="norm1")
    full = first_weights(h)
    w_in = full["w_in"]
    conv_w = jnp.pad(full["conv_w"], ((0, CONV_HALO - dims.conv_width), (0, 0)))
    ffn_w = jnp.pad(full["ffn_conv_w"], ((0, FFN_HALO - dims.ffn_conv_width), (0, 0)))
    z = _mm_nn(h, w_in, out_dtype=BF16, after=full.get("token"), tm=2048, tn=1792, name="in_proj")
    a1, a3 = _conv_branch_fwd(z, conv_w, row("conv_b"), row("conv_norm_g"), dims, name="conv_branch")
    qkv = _qkv_layouts_fwd(z, gq, gk, ones, dims, name="qk_norm")
    per_group = {dil: _attn_fwd(*qkv[dil], dims, dil, name=f"attn_fwd_d{dil}") for dil in DILATIONS}
    o, lse = _attn_combine(per_group, head_spread, dims, name="attn_combine")
    full = other_weights(o)
    w_up = full["w_up"]
    w_co, w_ao, w_o, w_dn = (one_shard(full[k]) for k in ("w_conv_out", "w_attn_out", "w_out", "w_down"))
    ya, yb, mixed, x1, h2 = _mix_fwd(a3, o, w_co, w_ao, w_o, z, row("gate_b"), x, row("norm2_g"), dims,
                                     name="branch_projs_mix_out_proj_norm2")
    up = _mm_nn(h2, w_up, out_dtype=F32, tm=2048, name="up_proj")
    act = _ffn_act_fwd(up, ffn_w, row("ffn_conv_b"), dims, name="ffn_act")
    dy, dy_b, loss = _proj_residual_loss(act, w_dn, x1, target, tm=512, name="down_proj_loss")

    grads = {}

    def large(name, g):
        grads[name], g_bf16 = g
        return send_grad(name, g_bf16)

    sent = large("w_down", _mm_tn(act, dy_b, n_shards=1, name="dw_down"))
    dact = _mm_nt(dy_b, w_dn, out_dtype=BF16, after=sent, name="d_act")
    dup, dfw, dfb = _ffn_bwd(dact, up, ffn_w, row("ffn_conv_b"), dims, name="ffn_bwd")
    grads["ffn_conv_w"], grads["ffn_conv_b"] = dfw[:dims.ffn_conv_width], dfb
    sent = large("w_up", _mm_tn(h2, dup, n_shards=N_CHIPS, name="dw_up"))
    dx1, dx1_b, grads["norm2_g"] = _mm_nt_rmsnorm_bwd(dup, w_up, x1, row("norm2_g"), dy, want_bf16=True, after=sent,
                                                     name="d_h2_norm2_bwd")
    sent = large("w_out", _mm_tn(mixed, dx1_b, n_shards=1, name="dw_out"))
    dya, dyb, dz_gate, grads["gate_b"] = _mix_bwd(dx1_b, w_o, ya, yb, z, row("gate_b"), dims, after=sent,
                                                  name="d_mix_gate_mix_bwd")
    sent = large("w_attn_out", _mm_tn(o, dyb, n_shards=1, name="dw_attn_out"))
    dos, deltas = _attn_bwd_prep(dyb, w_ao, o, head_sum, dims, after=sent, name="d_attn_bwd_prep")
    dqkv = {dil: _attn_bwd(*qkv[dil], dos[dil], lse[dil], deltas[dil], dims, dil, name=f"attn_bwd_d{dil}")
            for dil in DILATIONS}
    dz_qkv, dgq, dgk = _qkv_layouts_bwd(z, dqkv, gq, gk, ones, dims, name="qk_norm_bwd")
    grads["q_norm_g"] = dgq.reshape(heads, dims.head_dim).sum(axis=0)
    grads["k_norm_g"] = dgk.reshape(heads, dims.head_dim).sum(axis=0)
    sent = large("w_conv_out", _mm_tn(a3, dya, n_shards=1, name="dw_conv_out"))
    da1, grads["conv_norm_g"] = _mm_nt_rmsnorm_bwd(dya, w_co, a1, row("conv_norm_g"), None, want_bf16=False, silu=True,
                                                   after=sent, name="d_conv_act_norm_bwd")
    dz, dcw, grads["conv_b"] = _conv_branch_bwd(da1, z, conv_w, [dz_qkv, dz_gate], dims, name="conv_branch_bwd")
    grads["conv_w"] = dcw[:dims.conv_width]
    sent = large("w_in", _mm_tn(h, dz, n_shards=N_CHIPS, name="dw_in"))
    dx, grads["norm1_g"] = _mm_nt_rmsnorm_bwd(dz, w_in, x, row("norm1_g"), dx1, want_bf16=False, after=sent,
                                              name="d_h_norm1_bwd")
    return loss, dx, grads


def _step(dims, x, target, w, m, v):
    d = dims.d_model
    t = dims.tokens
    sq = lambda a: a.reshape(a.shape[1:])
    w2, m2, v2 = ({k: sq(a) for k, a in grp.items()} for grp in (w, m, v))

    conv_pad = jnp.pad(w2["conv_w"], ((0, CONV_HALO - dims.conv_width), (0, 0)))
    ffn_pad = jnp.pad(w2["ffn_conv_w"], ((0, FFN_HALO - dims.ffn_conv_width), (0, 0)))
    first_names = ("w_in", "conv_w", "ffn_conv_w")
    other_names = tuple(k for k in LARGE if k not in first_names)
    lands = dict(zip(first_names, _cast_to_lands([w2["w_in"], conv_pad, ffn_pad], [BF16, F32, F32], name="cast_first")))
    first = _gather_start([lands[k] for k in first_names], x, halved=(0,), name="gather_start_first")
    lands.update(zip(other_names, _cast_to_lands([w2[k] for k in other_names], [BF16] * len(other_names),
                                                 after=first[3], name="cast_other")))
    other = []
    cols = lambda g, rows: jnp.moveaxis(g, 0, 1).reshape(g.shape[1], -1)[:rows]

    def first_weights(after):
        got = dict(zip(first_names, _gather_wait(*first[:3], [after] + [lands[k] for k in other_names], halved=(0,),
                                                 name="gather_wait_first")))
        got["w_in"] = _forward_to_sibling(got["w_in"], name="forward_w_in")
        other.extend(_gather_start([lands[k] for k in other_names], got["w_in"], name="gather_start_other"))
        got["conv_w"] = cols(got["conv_w"], dims.conv_width)
        got["ffn_conv_w"] = cols(got["ffn_conv_w"], dims.ffn_conv_width)
        got["token"] = other[3]
        return got

    def other_weights(after):
        return dict(zip(other_names, _gather_wait(*other[:3], after, name="gather_wait_other")))

    started = {}

    def send_grad(name, g):
        send, recv, g_thru, land, token = _scatter_start(g.reshape(N_CHIPS, -1, g.shape[-1]), name=f"scatter_start_{name}")
        started[name] = (send, recv, g_thru, land)
        return token

    small = {k: w2[k] for k in SMALL}
    small["norm1_g"] = _after(small["norm1_g"].reshape(1, -1), first[3])
    loss, dx, grads = _local_step(dims, x.reshape(t, d), target.reshape(t, d), small, first_weights, other_weights, send_grad)

    def my_sums(names, after, tag):
        arrived = _scatter_wait([started[k] for k in names], after, name=f"scatter_wait_{tag}")
        blocks = [grads[k].reshape(N_CHIPS, -1, grads[k].shape[-1]) for k in names]
        return [_sum_received(g, land, name=f"sum_{k}") for k, g, (_, land) in zip(names, blocks, arrived)]

    def updates(names, mine, theirs):
        return {k: _adamw(w2[k], [a, b], m2[k], v2[k], name=f"adamw_{k}") for k, a, b in zip(names, mine, theirs)}

    small_names = SMALL + ("conv_w", "ffn_conv_w")
    packed = _pack_rows([grads[k] for k in small_names] + [loss[0, 0]], d)
    reducing = _allreduce_start(packed, name="allreduce_start")
    others = [k for k in LARGE if k != "w_in"]
    mine_others = my_sums(others, [dx, reducing[4]], "others")
    swapping_others = _swap_start(mine_others, name="swap_start_others")
    mine_w_in = my_sums(["w_in"], swapping_others[4], "w_in")
    swapping_w_in = _swap_start(mine_w_in, name="swap_start_w_in")
    out = updates(others, *_swap_wait(swapping_others, swapping_w_in[4], name="swap_wait_others"))
    last_updates = [out[k][1] for k in others]
    reduced = _sum_devices(*_allreduce_wait(reducing, last_updates, name="allreduce_wait"), name="allreduce_sum")
    vector_rows = sum(_packed_rows(math.prod(grads[k].shape), d) for k in SMALL)
    tail_shapes = [grads[k].shape for k in ("conv_w", "ffn_conv_w")] + [()]
    conv_g, ffn_g, loss_total = _unpack_rows(reduced[vector_rows:], tail_shapes, d)
    chip = 2 * lax.axis_index("x") + lax.axis_index("y")
    sharded_g = [lax.dynamic_slice_in_dim(g, chip * w2[k].shape[1], w2[k].shape[1], axis=1)
                 for k, g in (("conv_w", conv_g), ("ffn_conv_w", ffn_g))]
    packed_g = jnp.concatenate([reduced[:vector_rows], _pack_rows(sharded_g, d)], axis=0)

    small_shapes = [w2[k].shape for k in small_names]
    pack = lambda grp: _pack_rows([grp[k] for k in small_names], d)
    results = _adamw(pack(w2), [packed_g], pack(m2), pack(v2), name="adamw_small")
    unpacked = [_unpack_rows(r, small_shapes, d) for r in results]
    out.update({k: tuple(u[i] for u in unpacked) for i, k in enumerate(small_names)})
    out.update(updates(["w_in"], *_swap_wait(swapping_w_in, results[1], name="swap_wait_w_in")))

    lead = lambda a: a.reshape((1,) + a.shape)
    ordered = [[lead(out[k][j].reshape(w2[k].shape)) for k in WEIGHTS] for j in range(4)]
    return (loss_total, dx.reshape(x.shape), *ordered[0], *ordered[1], *ordered[2], *ordered[3])


def kernel(x, norm1_g, w_in, gate_b, conv_w, conv_b, conv_norm_g, w_conv_out, q_norm_g, k_norm_g, w_attn_out, w_out, norm2_g, w_up, ffn_conv_w, ffn_conv_b, w_down, loss_target, m_norm1_g, m_w_in, m_gate_b, m_conv_w, m_conv_b, m_conv_norm_g, m_w_conv_out, m_q_norm_g, m_k_norm_g, m_w_attn_out, m_w_out, m_norm2_g, m_w_up, m_ffn_conv_w, m_ffn_conv_b, m_w_down, v_norm1_g, v_w_in, v_gate_b, v_conv_w, v_conv_b, v_conv_norm_g, v_w_conv_out, v_q_norm_g, v_k_norm_g, v_w_attn_out, v_w_out, v_norm2_g, v_w_up, v_ffn_conv_w, v_ffn_conv_b, v_w_down):
    w = dict(zip(WEIGHTS, (norm1_g, w_in, gate_b, conv_w, conv_b, conv_norm_g, w_conv_out, q_norm_g, k_norm_g,
                           w_attn_out, w_out, norm2_g, w_up, ffn_conv_w, ffn_conv_b, w_down)))
    m = dict(zip(WEIGHTS, (m_norm1_g, m_w_in, m_gate_b, m_conv_w, m_conv_b, m_conv_norm_g, m_w_conv_out, m_q_norm_g,
                           m_k_norm_g, m_w_attn_out, m_w_out, m_norm2_g, m_w_up, m_ffn_conv_w, m_ffn_conv_b, m_w_down)))
    v = dict(zip(WEIGHTS, (v_norm1_g, v_w_in, v_gate_b, v_conv_w, v_conv_b, v_conv_norm_g, v_w_conv_out, v_q_norm_g,
                           v_k_norm_g, v_w_attn_out, v_w_out, v_norm2_g, v_w_up, v_ffn_conv_w, v_ffn_conv_b, v_w_down)))
    dims = Dims(d_model=x.shape[-1], batch_local=x.shape[0], seq=x.shape[1], d_ff=w_down.shape[1] * N_CHIPS)
    return _step(dims, x, loss_target, w, m, v)
```

```python
import functools
import math
from typing import NamedTuple

import jax
import jax.numpy as jnp
from jax import lax
from jax.experimental import pallas as pl
from jax.experimental.pallas import tpu as pltpu

F32 = jnp.float32
BF16 = jnp.bfloat16

RMS_EPS = 1e-6
ATTN_BLOCK = 128
DILATIONS = (1, 4, 16)
CONV_HALO = 32
FFN_HALO = 8
ADAM_LR, ADAM_B1, ADAM_B2, ADAM_EPS, ADAM_WD, ADAM_STEP = 0.001, 0.9, 0.999, 1e-08, 0.01, 10
V7X_VMEM_LIMIT_BYTES = 56 * 2 ** 20
N_CHIPS = 4
MESH = pl.DeviceIdType.MESH


class Dims(NamedTuple):
    d_model: int = 1024
    n_heads: int = 16
    head_dim: int = 64
    d_ff: int = 2816
    seq: int = 2048
    batch_local: int = 2
    conv_width: int = 31
    ffn_conv_width: int = 3

    @property
    def tokens(self):
        return self.seq * self.batch_local


def _params(*semantics):
    return pltpu.CompilerParams(dimension_semantics=semantics, vmem_limit_bytes=V7X_VMEM_LIMIT_BYTES)


ANY = pl.BlockSpec(memory_space=pl.ANY)


def _ordered(body, n_inputs, after):
    after = [] if after is None else list(after) if isinstance(after, (list, tuple)) else [after]
    if not after:
        return body, [], []

    def wrapped(*refs):
        return body(*refs[:n_inputs], *refs[n_inputs + len(after):])

    return wrapped, [ANY] * len(after), after


def _pick(n, target, mult=128):
    if n <= target:
        return n
    best = None
    for t in range(mult, target + 1, mult):
        if n % t == 0:
            best = t
    assert best is not None, (n, target, mult)
    return best


def _sigmoid(v):
    return 1.0 / (1.0 + jnp.exp(-v))


def _mm_nn(a, w, *, out_dtype, name, residual=None, after=None, tm=1024, tn=1408, tk=2816):
    m, k = a.shape
    nsh, k2, c = w.shape
    assert k == k2 and a.dtype == BF16 and w.dtype == BF16
    n = nsh * c
    tm, tn, tk = _pick(m, tm, 8), _pick(c, tn), _pick(k, tk)
    nk, cpn = k // tk, c // tn

    def body(*refs):
        if residual is None:
            a_ref, w_ref, o_ref, acc = refs
        else:
            a_ref, w_ref, r_ref, o_ref, acc = refs
        prod = jnp.dot(a_ref[...], w_ref[...], preferred_element_type=F32)

        def finish(total):
            if residual is not None:
                total = total + r_ref[...]
            o_ref[...] = total.astype(out_dtype)

        if nk == 1:
            finish(prod)
        else:
            kk = pl.program_id(2)

            @pl.when(kk == 0)
            def _():
                acc[...] = prod

            @pl.when(kk > 0)
            def _():
                acc[...] += prod

            @pl.when(kk == nk - 1)
            def _():
                finish(acc[...])

    in_specs = [pl.BlockSpec((tm, tk), lambda i, j, kk: (i, kk)),
                pl.BlockSpec((None, tk, tn), lambda i, j, kk: (j // cpn, kk, j % cpn))]
    args = [a, w]
    if residual is not None:
        in_specs.append(pl.BlockSpec((tm, tn), lambda i, j, kk: (i, j)))
        args.append(residual)
    body, more_specs, more_args = _ordered(body, len(args), after)
    return pl.pallas_call(
        body, name=name, grid=(m // tm, n // tn, nk),
        in_specs=in_specs + more_specs, out_specs=pl.BlockSpec((tm, tn), lambda i, j, kk: (i, j)),
        out_shape=jax.ShapeDtypeStruct((m, n), out_dtype),
        scratch_shapes=[pltpu.VMEM((tm, tn) if nk > 1 else (8, 128), F32)],
        compiler_params=_params("parallel", "parallel", "arbitrary"),
    )(*args, *more_args)


def _proj_residual_loss(a, w, residual, target, *, name, tm=1024):
    m, k = a.shape
    _, k2, n = w.shape
    assert w.shape[0] == 1 and k == k2 and a.dtype == BF16 and w.dtype == BF16
    tm = _pick(m, tm, 8)

    def body(a_ref, w_ref, r_ref, t_ref, dy_ref, dyb_ref, loss_ref):
        err = r_ref[...] + jnp.dot(a_ref[...], w_ref[...], preferred_element_type=F32) - t_ref[...]
        dy = err * (1.0 / n)
        dy_ref[...] = dy
        dyb_ref[...] = dy.astype(BF16)
        part = jnp.sum(jnp.sum(err * err, axis=-1, keepdims=True), axis=0, keepdims=True) * (0.5 / n)
        _accumulate(loss_ref, jnp.broadcast_to(part, (8, 128)), pl.program_id(0) == 0)

    rows = lambda width: pl.BlockSpec((tm, width), lambda i: (i, 0))
    return pl.pallas_call(
        body, name=name, grid=(m // tm,),
        in_specs=[rows(k), pl.BlockSpec((None, k, n), lambda i: (0, 0, 0)), rows(n), rows(n)],
        out_specs=[rows(n), rows(n), pl.BlockSpec((8, 128), lambda i: (0, 0))],
        out_shape=[jax.ShapeDtypeStruct((m, n), F32), jax.ShapeDtypeStruct((m, n), BF16),
                   jax.ShapeDtypeStruct((8, 128), F32)],
        compiler_params=_params("arbitrary"),
    )(a, w, residual, target)


def _mm_nt(a, w, *, out_dtype, name, after=None, tm=1024, tn=1408, tk=1792):
    m, k = a.shape
    nsh, r, c = w.shape
    assert k == nsh * c and a.dtype == BF16 and w.dtype == BF16
    tm, tn, tk = _pick(m, tm, 8), _pick(r, tn), _pick(c, tk)
    nk, cpk = k // tk, c // tk

    def body(a_ref, w_ref, o_ref, acc):
        prod = lax.dot_general(a_ref[...], w_ref[...], (((1,), (1,)), ((), ())), preferred_element_type=F32)
        if nk == 1:
            o_ref[...] = prod.astype(out_dtype)
        else:
            kk = pl.program_id(2)

            @pl.when(kk == 0)
            def _():
                acc[...] = prod

            @pl.when(kk > 0)
            def _():
                acc[...] += prod

            @pl.when(kk == nk - 1)
            def _():
                o_ref[...] = acc[...].astype(out_dtype)

    body, more_specs, more_args = _ordered(body, 2, after)
    return pl.pallas_call(
        body, name=name, grid=(m // tm, r // tn, nk),
        in_specs=[pl.BlockSpec((tm, tk), lambda i, j, kk: (i, kk)),
                  pl.BlockSpec((None, tn, tk), lambda i, j, kk: (kk // cpk, j, kk % cpk))] + more_specs,
        out_specs=pl.BlockSpec((tm, tn), lambda i, j, kk: (i, j)),
        out_shape=jax.ShapeDtypeStruct((m, r), out_dtype),
        scratch_shapes=[pltpu.VMEM((tm, tn) if nk > 1 else (8, 128), F32)],
        compiler_params=_params("parallel", "parallel", "arbitrary"),
    )(a, w, *more_args)


NORM_BWD_ROWS = 256


def _mm_nt_rmsnorm_bwd(a, w, x, g, dres, *, name, want_bf16, silu=False, after=None, tm=1024, tk=1792):
    m, k = a.shape
    nsh, r, c = w.shape
    assert k == nsh * c and a.dtype == BF16 and w.dtype == BF16 and x.shape == (m, r)
    tm, tk = _pick(m, tm, 8), _pick(c, tk)
    nk, cpk = k // tk, c // tk
    rows = _pick(tm, NORM_BWD_ROWS, 8)
    n_in = 4 if dres is None else 5

    def body(a_ref, w_ref, x_ref, g_ref, *rest):
        dres_ref = None if dres is None else rest[0]
        outs, acc = rest[n_in - 4:-1], rest[-1]
        dx_ref, dg_ref = outs[0], outs[-1]
        kk = pl.program_id(1)
        prod = lax.dot_general(a_ref[...], w_ref[...], (((1,), (1,)), ((), ())), preferred_element_type=F32)

        @pl.when(kk == 0)
        def _():
            acc[...] = prod

        @pl.when(kk > 0)
        def _():
            acc[...] += prod

        @pl.when(kk == nk - 1)
        def _():
            dg = jnp.zeros((1, r), F32)
            for r0 in range(0, tm, rows):
                part = slice(r0, r0 + rows)
                xv, dyv = x_ref[part, :], acc[part, :]
                inv = lax.rsqrt(jnp.mean(xv * xv, axis=-1, keepdims=True) + RMS_EPS)
                if silu:
                    y = xv * inv * g_ref[...]
                    sg = _sigmoid(y)
                    dyv = dyv * sg * (1.0 + y * (1.0 - sg))
                gy = dyv * g_ref[...]
                dx = inv * gy - xv * (inv * inv * inv) * jnp.mean(xv * gy, axis=-1, keepdims=True)
                if dres is not None:
                    dx = dx + dres_ref[part, :]
                dx_ref[part, :] = dx
                if want_bf16:
                    outs[1][part, :] = dx.astype(BF16)
                dg = dg + jnp.sum(dyv * xv * inv, axis=0, keepdims=True)
            _accumulate(dg_ref, dg, pl.program_id(0) == 0)

    whole = lambda: pl.BlockSpec((tm, r), lambda i, kk: (i, 0))
    vec = pl.BlockSpec((1, r), lambda i, kk: (0, 0))
    out_shape, out_specs = [jax.ShapeDtypeStruct((m, r), F32)], [whole()]
    if want_bf16:
        out_shape.append(jax.ShapeDtypeStruct((m, r), BF16))
        out_specs.append(whole())
    out_shape.append(jax.ShapeDtypeStruct((1, r), F32))
    out_specs.append(vec)
    body, more_specs, more_args = _ordered(body, n_in, after)
    residual_specs, residual_args = ([], []) if dres is None else ([whole()], [dres])
    return pl.pallas_call(
        body, name=name, grid=(m // tm, nk),
        in_specs=[pl.BlockSpec((tm, tk), lambda i, kk: (i, kk)),
                  pl.BlockSpec((None, r, tk), lambda i, kk: (kk // cpk, 0, kk % cpk)), whole(), vec]
        + residual_specs + more_specs,
        out_specs=out_specs, out_shape=out_shape,
        scratch_shapes=[pltpu.VMEM((tm, r), F32)],
        compiler_params=_params("arbitrary", "arbitrary"),
    )(a, w, x, g, *residual_args, *more_args)


MM_TN_VMEM_BYTES = 44 * 2 ** 20


def _mm_tn(a, b, *, n_shards, name, tm=1408, tn=1408):
    t, m = a.shape
    t2, n = b.shape
    assert t == t2 and a.dtype == BF16 and b.dtype == BF16
    c = n // n_shards
    tm, tn = _pick(m, tm), _pick(c, tn)
    if m // tm == 1 and n // tn == 1 and tn % (2 * LANES) == 0:
        tn //= 2
    fixed = 2 * tm * tn * 6
    if 4 * t * (tm + tn) + fixed <= MM_TN_VMEM_BYTES:
        tk = t
    else:
        tk = _pick(t, (MM_TN_VMEM_BYTES - fixed - 4 * tm * tn) // (4 * (tm + tn)), 8)
    nk, cpn = t // tk, c // tn

    def body(a_ref, b_ref, o_ref, ob_ref, acc):
        kk = pl.program_id(2)
        prod = lax.dot_general(a_ref[...], b_ref[...], (((0,), (0,)), ((), ())), preferred_element_type=F32)

        def finish(total):
            o_ref[...] = total
            ob_ref[...] = total.astype(BF16)

        if nk == 1:
            finish(prod)
        else:
            @pl.when(kk == 0)
            def _():
                acc[...] = prod

            @pl.when(kk > 0)
            def _():
                acc[...] += prod

            @pl.when(kk == nk - 1)
            def _():
                finish(acc[...])

    out_spec = pl.BlockSpec((None, tm, tn), lambda i, j, kk: (j // cpn, i, j % cpn))
    return pl.pallas_call(
        body, name=name, grid=(m // tm, n // tn, nk),
        in_specs=[pl.BlockSpec((tk, tm), lambda i, j, kk: (kk, i)),
                  pl.BlockSpec((tk, tn), lambda i, j, kk: (kk, j))],
        out_specs=[out_spec, out_spec],
        out_shape=[jax.ShapeDtypeStruct((n_shards, m, c), F32), jax.ShapeDtypeStruct((n_shards, m, c), BF16)],
        scratch_shapes=[pltpu.VMEM((tm, tn) if nk > 1 else (8, 128), F32)],
        compiler_params=_params("parallel", "parallel", "arbitrary"),
    )(a, b)


def _row_spec(tr, width, col=0):
    return pl.BlockSpec((tr, width), lambda i, col=col: (i, col))


def _vec_spec(width, col=0):
    return pl.BlockSpec((1, width), lambda i, col=col: (0, col))


def _accumulate(ref, value, first):
    @pl.when(first)
    def _():
        ref[...] = value

    @pl.when(jnp.logical_not(first))
    def _():
        ref[...] += value


def _rmsnorm_fwd(x, g, *, name, tr=512):
    t, d = x.shape
    tr = _pick(t, tr, 8)

    def body(x_ref, g_ref, o_ref):
        xv = x_ref[...]
        r = lax.rsqrt(jnp.mean(xv * xv, axis=-1, keepdims=True) + RMS_EPS)
        o_ref[...] = (xv * r * g_ref[...]).astype(BF16)

    return pl.pallas_call(
        body, name=name, grid=(t // tr,),
        in_specs=[_row_spec(tr, d), _vec_spec(d)], out_specs=_row_spec(tr, d),
        out_shape=jax.ShapeDtypeStruct((t, d), BF16), compiler_params=_params("parallel"),
    )(x, g)


CONV_ROWS = 16


def _seq_specs(dims, ts, width, halo, col, *, nxt=False):
    nst, per = dims.seq // ts, ts // halo
    last = dims.tokens // halo - 1
    cur = pl.BlockSpec((ts, width), lambda b, i: (b * nst + i, col))
    if nxt:
        edge = pl.BlockSpec((halo, width), lambda b, i: (jnp.minimum((b * nst + i + 1) * per, last), col))
    else:
        edge = pl.BlockSpec((halo, width), lambda b, i: (jnp.maximum((b * nst + i) * per - 1, 0), col))
    return cur, edge


SUBLANES = 8


def _shifted_copies(buf, shifted):
    rows = shifted.shape[1]
    for s in range(1, SUBLANES):
        shifted[s - 1] = buf[pl.ds(s, rows), :]


def _window(buf, shifted, start, size):
    a, s = divmod(start, SUBLANES)
    src = buf if s == 0 else shifted.at[s - 1]
    return src[pl.ds(SUBLANES * a, size), :]


def _conv_branch_fwd(z, w, b, g, dims, *, name, ts=128):
    t, c, kw = z.shape[0], dims.d_model, dims.conv_width
    base = CONV_HALO - (kw - 1)

    def body(av_ref, hv_ref, ag_ref, hg_ref, w_ref, b_ref, g_ref, a1_ref, a3_ref, buf, shifted):
        i = pl.program_id(1)
        buf[CONV_HALO:, :] = av_ref[...].astype(F32) * _sigmoid(ag_ref[...].astype(F32))
        buf[0:CONV_HALO, :] = jnp.where(i > 0, hv_ref[...].astype(F32) * _sigmoid(hg_ref[...].astype(F32)), 0.0)
        _shifted_copies(buf, shifted)
        for r0 in range(0, ts, CONV_ROWS):
            acc = jnp.broadcast_to(b_ref[...], (CONV_ROWS, c))
            for k in range(kw):
                acc = acc + w_ref[k:k + 1, :] * _window(buf, shifted, r0 + base + k, CONV_ROWS)
            a1_ref[r0:r0 + CONV_ROWS, :] = acc
            a2 = acc * lax.rsqrt(jnp.mean(acc * acc, axis=-1, keepdims=True) + RMS_EPS) * g_ref[...]
            a3_ref[r0:r0 + CONV_ROWS, :] = (a2 * _sigmoid(a2)).astype(BF16)

    vec = pl.BlockSpec((1, c), lambda b, i: (0, 0))
    out = pl.BlockSpec((ts, c), lambda b, i: (b * (dims.seq // ts) + i, 0))
    return pl.pallas_call(
        body, name=name, grid=(dims.batch_local, dims.seq // ts),
        in_specs=[*_seq_specs(dims, ts, c, CONV_HALO, 0), *_seq_specs(dims, ts, c, CONV_HALO, 1),
                  pl.BlockSpec((CONV_HALO, c), lambda b, i: (0, 0)), vec, vec],
        out_specs=[out, out],
        out_shape=[jax.ShapeDtypeStruct((t, c), F32), jax.ShapeDtypeStruct((t, c), BF16)],
        scratch_shapes=[pltpu.VMEM((CONV_HALO + ts, c), F32),
                        pltpu.VMEM((SUBLANES - 1, CONV_HALO + ts - SUBLANES, c), F32)],
        compiler_params=_params("parallel", "parallel"),
    )(z, z, z, z, w, b, g)


def _conv_branch_bwd(da1, z, w, rest_of_dz, dims, *, name, ts=128):
    t, c, kw = z.shape[0], dims.d_model, dims.conv_width
    nst = dims.seq // ts
    base = CONV_HALO - (kw - 1)
    n_rest = len(rest_of_dz)
    total = 2 * c + sum(r.shape[1] for r in rest_of_dz)

    def body(d_ref, dn_ref, av_ref, hv_ref, ag_ref, hg_ref, w_ref, *more):
        rest_refs = more[:n_rest]
        dz_ref, dw_ref, db_ref, abuf, dbuf, ashift, dshift = more[n_rest:]
        col = 2 * c
        for r in rest_refs:
            dz_ref[:, col:col + r.shape[1]] = r[...]
            col += r.shape[1]
        i = pl.program_id(1)
        first = jnp.logical_and(pl.program_id(0) == 0, i == 0)
        abuf[CONV_HALO:, :] = av_ref[...].astype(F32) * _sigmoid(ag_ref[...].astype(F32))
        abuf[0:CONV_HALO, :] = jnp.where(i > 0, hv_ref[...].astype(F32) * _sigmoid(hg_ref[...].astype(F32)), 0.0)
        d1 = d_ref[...]
        dbuf[0:ts, :] = d1
        dbuf[ts:, :] = jnp.where(i < nst - 1, dn_ref[...], 0.0)
        _shifted_copies(abuf, ashift)
        _shifted_copies(dbuf, dshift)

        @pl.when(first)
        def _():
            dw_ref[...] = jnp.zeros_like(dw_ref)
            db_ref[...] = jnp.zeros_like(db_ref)

        db_ref[...] += jnp.sum(d1, axis=0, keepdims=True)
        for k in range(kw):
            dw_ref[k:k + 1, :] += jnp.sum(d1 * _window(abuf, ashift, base + k, ts), axis=0, keepdims=True)
        for r0 in range(0, ts, CONV_ROWS):
            acc = jnp.zeros((CONV_ROWS, c), F32)
            for k in range(kw):
                acc = acc + w_ref[k:k + 1, :] * _window(dbuf, dshift, r0 + (kw - 1) - k, CONV_ROWS)
            av = av_ref[r0:r0 + CONV_ROWS, :].astype(F32)
            sg = _sigmoid(ag_ref[r0:r0 + CONV_ROWS, :].astype(F32))
            dz_ref[r0:r0 + CONV_ROWS, 0:c] = (acc * sg).astype(BF16)
            dz_ref[r0:r0 + CONV_ROWS, c:2 * c] = (acc * av * sg * (1.0 - sg)).astype(BF16)

    cur, nxt = _seq_specs(dims, ts, c, CONV_HALO, 0, nxt=True)
    return pl.pallas_call(
        body, name=name, grid=(dims.batch_local, nst),
        in_specs=[cur, nxt, *_seq_specs(dims, ts, c, CONV_HALO, 0), *_seq_specs(dims, ts, c, CONV_HALO, 1),
                  pl.BlockSpec((CONV_HALO, c), lambda b, i: (0, 0))]
        + [pl.BlockSpec((ts, r.shape[1]), lambda b, i: (b * nst + i, 0)) for r in rest_of_dz],
        out_specs=[pl.BlockSpec((ts, total), lambda b, i: (b * nst + i, 0)),
                   pl.BlockSpec((CONV_HALO, c), lambda b, i: (0, 0)), pl.BlockSpec((1, c), lambda b, i: (0, 0))],
        out_shape=[jax.ShapeDtypeStruct((t, total), BF16), jax.ShapeDtypeStruct((CONV_HALO, c), F32),
                   jax.ShapeDtypeStruct((1, c), F32)],
        scratch_shapes=[pltpu.VMEM((CONV_HALO + ts, c), F32)] * 2
        + [pltpu.VMEM((SUBLANES - 1, CONV_HALO + ts - SUBLANES, c), F32)] * 2,
        compiler_params=_params("arbitrary", "arbitrary"),
    )(da1, da1, z, z, z, z, w, *rest_of_dz)


FFN_ROWS = 16
FFN_COLS = 256


def _ffn_chunks(ts, f):
    cw = _pick(f, FFN_COLS)
    return [(r0, c0, cw) for r0 in range(0, ts, FFN_ROWS) for c0 in range(0, f, cw)]


def _tap_sources(buf, moved, offsets, rows):
    taps, used = [], 0
    for off in offsets:
        if off % SUBLANES:
            moved[used] = buf[pl.ds(off, rows), :]
            taps.append((moved.at[used], 0))
            used += 1
        else:
            taps.append((buf, off))
    return taps


def _moved_copies(offsets):
    return sum(1 for off in offsets if off % SUBLANES)


def _taps_sum(taps, w_ref, init, r0, cols):
    for k, (src, off) in enumerate(taps):
        init = init + w_ref[k:k + 1, cols] * src[pl.ds(off + r0, init.shape[0]), cols]
    return init


def _ffn_bwd(dact, up, w, b, dims, *, name, ts=128):
    t, f, kw = up.shape[0], dims.d_ff, dims.ffn_conv_width
    nst = dims.seq // ts
    fwd_offsets = [FFN_HALO - (kw - 1) + k for k in range(kw)]
    bwd_offsets = [(kw - 1) - k for k in range(kw)]
    dact_halo = 2 * FFN_HALO

    def body(d_ref, dn_ref, up_ref, hp_ref, hn_ref, w_ref, b_ref, o_ref, dw_ref, db_ref, buf, moved, dbuf, dmoved):
        i = pl.program_id(1)
        first = jnp.logical_and(pl.program_id(0) == 0, i == 0)
        more = i < nst - 1
        buf[0:FFN_HALO, :] = jnp.where(i > 0, hp_ref[...], 0.0)
        buf[FFN_HALO:FFN_HALO + ts, :] = up_ref[...]
        buf[FFN_HALO + ts:, :] = hn_ref[...]
        taps = _tap_sources(buf, moved, fwd_offsets, ts + FFN_HALO)

        def du_chunk(r0, rows, c0, cw, d):
            vcols, gcols = slice(c0, c0 + cw), slice(f + c0, f + c0 + cw)
            uv = _taps_sum(taps, w_ref, jnp.broadcast_to(b_ref[:, vcols], (rows, cw)), r0, vcols)
            ug = _taps_sum(taps, w_ref, jnp.broadcast_to(b_ref[:, gcols], (rows, cw)), r0, gcols)
            sg = _sigmoid(ug)
            dbuf[r0:r0 + rows, vcols] = d * ug * sg
            dbuf[r0:r0 + rows, gcols] = d * uv * sg * (1.0 + ug * (1.0 - sg))

        for r0, c0, cw in _ffn_chunks(ts, f):
            du_chunk(r0, FFN_ROWS, c0, cw, d_ref[r0:r0 + FFN_ROWS, c0:c0 + cw].astype(F32))
        for _, c0, cw in _ffn_chunks(FFN_ROWS, f):
            d_next = dn_ref[:, c0:c0 + cw].astype(F32)[0:FFN_HALO]
            du_chunk(ts, FFN_HALO, c0, cw, jnp.where(more, d_next, 0.0))

        @pl.when(first)
        def _():
            dw_ref[...] = jnp.zeros_like(dw_ref)
            db_ref[...] = jnp.zeros_like(db_ref)

        du = dbuf[0:ts, :]
        db_ref[...] += jnp.sum(du, axis=0, keepdims=True)
        for k, (src, off) in enumerate(taps):
            dw_ref[k:k + 1, :] += jnp.sum(du * src[pl.ds(off, ts), :], axis=0, keepdims=True)

        dtaps = _tap_sources(dbuf, dmoved, bwd_offsets, ts)
        for r0, c0, cw in _ffn_chunks(ts, 2 * f):
            cols = slice(c0, c0 + cw)
            o_ref[r0:r0 + FFN_ROWS, cols] = _taps_sum(dtaps, w_ref, jnp.zeros((FFN_ROWS, cw), F32), r0, cols).astype(BF16)

    up_cur, up_prev = _seq_specs(dims, ts, 2 * f, FFN_HALO, 0)
    _, up_next = _seq_specs(dims, ts, 2 * f, FFN_HALO, 0, nxt=True)
    d_cur, d_next = _seq_specs(dims, ts, f, dact_halo, 0, nxt=True)
    full = lambda rows: pl.BlockSpec((rows, 2 * f), lambda b_, i: (0, 0))
    return pl.pallas_call(
        body, name=name, grid=(dims.batch_local, nst),
        in_specs=[d_cur, d_next, up_cur, up_prev, up_next, full(FFN_HALO), full(1)],
        out_specs=[pl.BlockSpec((ts, 2 * f), lambda b_, i: (b_ * nst + i, 0)), full(FFN_HALO), full(1)],
        out_shape=[jax.ShapeDtypeStruct((t, 2 * f), BF16), jax.ShapeDtypeStruct((FFN_HALO, 2 * f), F32),
                   jax.ShapeDtypeStruct((1, 2 * f), F32)],
        scratch_shapes=[pltpu.VMEM((ts + 2 * FFN_HALO, 2 * f), F32),
                        pltpu.VMEM((_moved_copies(fwd_offsets), ts + FFN_HALO, 2 * f), F32),
                        pltpu.VMEM((ts + FFN_HALO, 2 * f), F32),
                        pltpu.VMEM((_moved_copies(bwd_offsets), ts, 2 * f), F32)],
        compiler_params=_params("arbitrary", "arbitrary"),
    )(dact, dact, up, up, up, w, b)


def _ffn_act_fwd(up, w, b, dims, *, name, ts=128):
    t, f, kw = up.shape[0], dims.d_ff, dims.ffn_conv_width
    offsets = [FFN_HALO - (kw - 1) + k for k in range(kw)]

    def body(up_ref, h_ref, w_ref, b_ref, o_ref, buf, moved):
        buf[FFN_HALO:, :] = up_ref[...]
        buf[0:FFN_HALO, :] = jnp.where(pl.program_id(1) > 0, h_ref[...], 0.0)
        taps = _tap_sources(buf, moved, offsets, ts)
        for r0, c0, cw in _ffn_chunks(ts, f):
            vcols, gcols = slice(c0, c0 + cw), slice(f + c0, f + c0 + cw)
            uv = _taps_sum(taps, w_ref, jnp.broadcast_to(b_ref[:, vcols], (FFN_ROWS, cw)), r0, vcols)
            ug = _taps_sum(taps, w_ref, jnp.broadcast_to(b_ref[:, gcols], (FFN_ROWS, cw)), r0, gcols)
            o_ref[r0:r0 + FFN_ROWS, vcols] = (ug * _sigmoid(ug) * uv).astype(BF16)

    full = lambda rows: pl.BlockSpec((rows, 2 * f), lambda b_, i: (0, 0))
    return pl.pallas_call(
        body, name=name, grid=(dims.batch_local, dims.seq // ts),
        in_specs=[*_seq_specs(dims, ts, 2 * f, FFN_HALO, 0), full(FFN_HALO), full(1)],
        out_specs=pl.BlockSpec((ts, f), lambda b_, i: (b_ * (dims.seq // ts) + i, 0)),
        out_shape=jax.ShapeDtypeStruct((t, f), BF16),
        scratch_shapes=[pltpu.VMEM((FFN_HALO + ts, 2 * f), F32), pltpu.VMEM((_moved_copies(offsets), ts, 2 * f), F32)],
        compiler_params=_params("parallel", "parallel"),
    )(up, up, w, b)


def _dot_nt(a, b):
    return lax.dot_general(a, b, (((1,), (1,)), ((), ())), preferred_element_type=F32)


def _dot_tn(a, b):
    return lax.dot_general(a, b, (((0,), (0,)), ((), ())), preferred_element_type=F32)


LANES = 128
MASK_BIAS = 1e30
RESIDUE_DILATIONS = tuple(d for d in DILATIONS if d > 1)


def _rows_to_residues(value, out_ref, scr, d):
    rows, width = value.shape
    for c in range(width // LANES):
        cols = slice(LANES * c, LANES * (c + 1))
        scr[c] = value[:, cols]
        for r in range(d):
            out_ref[r, :, cols] = scr[c, pl.ds(r, rows // d, stride=d), :].astype(out_ref.dtype)


def _residues_to_rows(in_ref, scr, d):
    _, n, width = in_ref.shape
    slabs = []
    for c in range(width // LANES):
        cols = slice(LANES * c, LANES * (c + 1))
        for r in range(d):
            scr[c, pl.ds(r, n, stride=d), :] = in_ref[r, :, cols].astype(F32)
        slabs.append(scr[c])
    return slabs[0] if len(slabs) == 1 else jnp.concatenate(slabs, axis=1)


def _residue_shape(dims, d, width, dtype):
    return jax.ShapeDtypeStruct((dims.batch_local, d, dims.seq // d, width), dtype)


def _residue_spec(dims, d, tr, width):
    tiles = dims.seq // tr
    return pl.BlockSpec((None, d, tr // d, width), lambda i: (i // tiles, 0, i % tiles, 0))


def _head_sum_matrix(dims):
    a = dims.n_heads * dims.head_dim
    head = jnp.arange(a, dtype=jnp.int32) // dims.head_dim
    return (head[:, None] == jnp.arange(LANES, dtype=jnp.int32)[None, :]).astype(BF16)


def _two_pass_dot(v, m):
    hi = v.astype(BF16)
    lo = (v - hi.astype(F32)).astype(BF16)
    return jnp.dot(hi, m, preferred_element_type=F32) + jnp.dot(lo, m, preferred_element_type=F32)


def _residue_permutations(tr):
    out = []
    for d in RESIDUE_DILATIONS:
        dst = jnp.arange(tr, dtype=jnp.int32)
        src = d * (dst % (tr // d)) + dst // (tr // d)
        out.append((src[:, None] == jnp.arange(tr, dtype=jnp.int32)[None, :]).astype(BF16))
    return out


def _bf16_rows_to_residues(value, out_ref, perm_ref, d):
    n = value.shape[0] // d
    moved = jnp.dot(perm_ref[...], value, preferred_element_type=F32).astype(out_ref.dtype)
    for r in range(d):
        out_ref[r] = moved[r * n:(r + 1) * n]


def _bf16_residues_to_rows(in_ref, back_ref):
    d = in_ref.shape[0]
    stacked = jnp.concatenate([in_ref[r] for r in range(d)], axis=0)
    return jnp.dot(back_ref[...], stacked, preferred_element_type=F32)


def _qkv_layouts_fwd(z, gq, gk, head_ones, dims, *, name, tr=256):
    t = z.shape[0]
    a = dims.n_heads * dims.head_dim
    q_scale = dims.head_dim ** -0.5
    nres = len(RESIDUE_DILATIONS)

    def body(q_ref, k_ref, v_ref, gq_ref, gk_ref, sum_ref, spread_ref, *rest):
        perm_refs, outs = rest[:nres], rest[nres:]
        qv, kv = q_ref[...].astype(F32), k_ref[...].astype(F32)
        mean = lambda val: _two_pass_dot(_two_pass_dot(val, sum_ref[...]), spread_ref[...]) * (1.0 / dims.head_dim)
        rq = lax.rsqrt(mean(qv * qv) + RMS_EPS)
        rk = lax.rsqrt(mean(kv * kv) + RMS_EPS)
        values = ((qv * rq * gq_ref[...] * q_scale).astype(BF16), (kv * rk * gk_ref[...]).astype(BF16), v_ref[...])
        for j, val in enumerate(values):
            outs[j][...] = val
            for g, d in enumerate(RESIDUE_DILATIONS):
                _bf16_rows_to_residues(val, outs[3 * (g + 1) + j], perm_refs[g], d)

    out_specs = [_row_spec(tr, a)] * 3
    out_shape = [jax.ShapeDtypeStruct((t, a), BF16)] * 3
    for d in RESIDUE_DILATIONS:
        out_specs += [_residue_spec(dims, d, tr, a)] * 3
        out_shape += [_residue_shape(dims, d, a, BF16)] * 3
    outs = pl.pallas_call(
        body, name=name, grid=(t // tr,),
        in_specs=[_row_spec(tr, a, 2), _row_spec(tr, a, 3), _row_spec(tr, a, 4), _vec_spec(a), _vec_spec(a),
                  pl.BlockSpec((a, LANES), lambda i: (0, 0)), pl.BlockSpec((LANES, a), lambda i: (0, 0))]
        + [pl.BlockSpec((tr, tr), lambda i: (0, 0))] * nres,
        out_specs=out_specs, out_shape=out_shape,
        compiler_params=_params("parallel"),
    )(z, z, z, gq, gk, *head_ones, *_residue_permutations(tr))
    return {d: tuple(outs[3 * g:3 * g + 3]) for g, d in enumerate((1,) + RESIDUE_DILATIONS)}


ATTN_RESIDUES_PER_STEP = 4
ATTN_RESIDUES_PER_STEP_WINDOWED = 2


def _attn_groups(dims, dil):
    return (dims.batch_local, dil) if dil > 1 else (1, dims.batch_local)


def _attn_array(x, dims, dil):
    return x if dil > 1 else x.reshape(1, dims.batch_local, dims.seq, x.shape[-1])


def _attn_residues(dims, dil):
    one_block = dims.seq // dil == ATTN_BLOCK
    return math.gcd(_attn_groups(dims, dil)[1], ATTN_RESIDUES_PER_STEP if one_block else ATTN_RESIDUES_PER_STEP_WINDOWED)


def _per_residue(body, rs):
    if rs == 1:
        return body

    def stepped(*refs):
        for r in range(rs):
            body(*[ref.at[r] for ref in refs])

    return stepped


def _attn_specs(dims, dil, width):
    blk = ATTN_BLOCK
    nb = dims.seq // dil // blk
    rs = _attn_residues(dims, dil)
    lead, groups = _attn_groups(dims, dil)
    if rs > 1 and nb == 1:
        grid = (lead, groups // rs)
        at = lambda f: pl.BlockSpec((None, rs, blk, width), lambda b, r: (b, r, 0, 0))
    elif rs > 1:
        grid = (lead, groups // rs, nb)
        at = lambda f: pl.BlockSpec((None, rs, blk, width), lambda b, r, i: (b, r, f(i), 0))
    else:
        grid = (lead, groups, nb)
        at = lambda f: pl.BlockSpec((None, None, blk, width), lambda b, r, i: (b, r, f(i), 0))
    return grid, at(lambda i: i), at(lambda i: jnp.maximum(i - 1, 0)), at(lambda i: jnp.minimum(i + 1, nb - 1))


def _head_slopes(n_heads):
    h = lax.broadcasted_iota(jnp.int32, (n_heads, 1, 1), 0).astype(F32)
    return jnp.exp((h + 1.0) * (-8.0 / n_heads * math.log(2.0)))


def _pair_masks(hd):
    low = lax.broadcasted_iota(jnp.int32, (1, 2 * hd), 1) < hd
    return low, jnp.logical_not(low)


def _attn_fwd(q, k, v, dims, dil, *, name):
    a = dims.n_heads * dims.head_dim
    heads, hd, blk = dims.n_heads, dims.head_dim, ATTN_BLOCK
    assert 2 * hd == LANES and heads % 2 == 0 and heads <= LANES
    nb = dims.seq // dil // blk
    has_prev = nb > 1
    nkeys = 2 * blk if has_prev else blk
    grid, cur, prev, _ = _attn_specs(dims, dil, a)
    _, cur_stat, _, _ = _attn_specs(dims, dil, LANES)

    def body(*refs):
        if has_prev:
            q_ref, kc_ref, vc_ref, kp_ref, vp_ref, o_ref, lse_ref, s_scr, p_scr, k_st, v_st = refs
            k_st[0:blk, :], k_st[blk:, :] = kp_ref[...], kc_ref[...]
            v_st[0:blk, :], v_st[blk:, :] = vp_ref[...], vc_ref[...]
        else:
            q_ref, k_st, v_st, o_ref, lse_ref, s_scr, p_scr = refs
        low, high = _pair_masks(hd)

        for hp in range(heads // 2):
            sl = slice(LANES * hp, LANES * (hp + 1))
            q2 = q_ref[:, sl]
            kcat = k_st[:, sl]
            s_scr[2 * hp] = _dot_nt(jnp.where(low, q2, jnp.zeros_like(q2)), kcat)
            s_scr[2 * hp + 1] = _dot_nt(jnp.where(high, q2, jnp.zeros_like(q2)), kcat)

        iq = lax.broadcasted_iota(jnp.int32, (blk, nkeys), 0)
        jk = lax.broadcasted_iota(jnp.int32, (blk, nkeys), 1)
        if has_prev:
            steps = iq + blk - jk
            valid = (steps >= 0) & (steps <= blk) & ((jk >= blk) | (pl.program_id(len(grid) - 1) > 0))
        else:
            steps = iq - jk
            valid = steps >= 0
        bias = jnp.where(valid, steps.astype(F32) * (-float(dil)), -MASK_BIAS)
        s = s_scr[...] + _head_slopes(heads) * bias[None]
        m = jnp.max(s, axis=-1, keepdims=True)
        p = jnp.exp(s - m)
        l = jnp.sum(p, axis=-1, keepdims=True)
        p_scr[...] = p.astype(BF16)
        inv = 1.0 / l
        lse = m + jnp.log(l)

        lane = lax.broadcasted_iota(jnp.int32, (blk, LANES), 1)
        stat = jnp.zeros((blk, LANES), F32)
        for hp in range(heads // 2):
            sl = slice(LANES * hp, LANES * (hp + 1))
            vcat = v_st[:, sl]
            pv_a = jnp.dot(p_scr[2 * hp], vcat, preferred_element_type=F32) * inv[2 * hp]
            pv_b = jnp.dot(p_scr[2 * hp + 1], vcat, preferred_element_type=F32) * inv[2 * hp + 1]
            o_ref[:, sl] = jnp.where(low, pv_a, pv_b).astype(BF16)
            stat = jnp.where(lane == 2 * hp, lse[2 * hp], stat)
            stat = jnp.where(lane == 2 * hp + 1, lse[2 * hp + 1], stat)
        lse_ref[...] = stat

    q4, k4, v4 = (_attn_array(x, dims, dil) for x in (q, k, v))
    rs = _attn_residues(dims, dil)
    per_step = lambda shape: shape if rs == 1 else (rs,) + shape
    o, lse = pl.pallas_call(
        _per_residue(body, rs), name=name, grid=grid,
        in_specs=[cur, cur, cur] + ([prev, prev] if has_prev else []),
        out_specs=[cur, cur_stat],
        out_shape=[jax.ShapeDtypeStruct(q4.shape, BF16), jax.ShapeDtypeStruct(q4.shape[:-1] + (LANES,), F32)],
        scratch_shapes=[pltpu.VMEM(per_step((heads, blk, nkeys)), F32), pltpu.VMEM(per_step((heads, blk, nkeys)), BF16)]
        + ([pltpu.VMEM(per_step((nkeys, a)), BF16)] * 2 if has_prev else []),
        compiler_params=_params(*["parallel"] * len(grid)),
    )(q4, k4, v4, *([k4, v4] if has_prev else []))
    return o.reshape(q.shape), lse.reshape(q.shape[:-1] + (LANES,))


def _attn_combine(groups, head_spread, dims, *, name, tr=256):
    t = dims.tokens
    a = dims.n_heads * dims.head_dim
    dils = tuple(groups)

    nres = len(RESIDUE_DILATIONS)

    def body(*refs):
        ins = refs[:2 * len(dils)]
        x_ref = refs[2 * len(dils)]
        back_refs = dict(zip(RESIDUE_DILATIONS, refs[2 * len(dils) + 1:2 * len(dils) + 1 + nres]))
        o_ref = refs[2 * len(dils) + 1 + nres]
        lse_refs = refs[2 * len(dils) + 2 + nres:-1]
        scr_stat = refs[-1]
        outs, stats = [], []
        for g, d in enumerate(dils):
            if d == 1:
                outs.append(ins[2 * g][...].astype(F32))
                stats.append(ins[2 * g + 1][...])
            else:
                outs.append(_bf16_residues_to_rows(ins[2 * g], back_refs[d]))
                stats.append(_residues_to_rows(ins[2 * g + 1], scr_stat, d))
        top = functools.reduce(jnp.maximum, stats)
        weights = [jnp.exp(s - top) for s in stats]
        total = functools.reduce(jnp.add, weights)
        joint = top + jnp.log(total)
        inv = 1.0 / total
        acc = None
        for w, o in zip(weights, outs):
            term = _two_pass_dot(w * inv, x_ref[...]) * o
            acc = term if acc is None else acc + term
        o_ref[...] = acc.astype(BF16)
        for g, d in enumerate(dils):
            if d == 1:
                lse_refs[g][...] = joint
            else:
                _rows_to_residues(joint, lse_refs[g], scr_stat, d)

    in_specs, args, lse_specs, lse_shapes = [], [], [], []
    for d in dils:
        if d == 1:
            in_specs += [_row_spec(tr, a), _row_spec(tr, LANES)]
            lse_specs.append(_row_spec(tr, LANES))
            lse_shapes.append(jax.ShapeDtypeStruct((t, LANES), F32))
        else:
            in_specs += [_residue_spec(dims, d, tr, a), _residue_spec(dims, d, tr, LANES)]
            lse_specs.append(_residue_spec(dims, d, tr, LANES))
            lse_shapes.append(_residue_shape(dims, d, LANES, F32))
        args += list(groups[d])
    outs = pl.pallas_call(
        body, name=name, grid=(t // tr,),
        in_specs=in_specs + [pl.BlockSpec((LANES, a), lambda i: (0, 0))] + [pl.BlockSpec((tr, tr), lambda i: (0, 0))] * nres,
        out_specs=[_row_spec(tr, a)] + lse_specs,
        out_shape=[jax.ShapeDtypeStruct((t, a), BF16)] + lse_shapes,
        scratch_shapes=[pltpu.VMEM((1, tr, LANES), F32)],
        compiler_params=_params("parallel"),
    )(*args, head_spread, *[jnp.transpose(p) for p in _residue_permutations(tr)])
    return outs[0], dict(zip(dils, outs[1:]))


def _attn_bwd_prep(dy, w, o, head_sum, dims, *, name, after=None, tr=256):
    t, a = o.shape
    nres = len(RESIDUE_DILATIONS)

    def body(dy_ref, w_ref, o_ref, e_ref, *rest):
        perm_refs, outs, scr_stat = rest[:nres], rest[nres:-1], rest[-1]
        do = _dot_nt(dy_ref[...], w_ref[...]).astype(BF16)
        outs[0][...] = do
        delta = _two_pass_dot(do.astype(F32) * o_ref[...].astype(F32), e_ref[...])
        outs[1][...] = delta
        for g, d in enumerate(RESIDUE_DILATIONS):
            _bf16_rows_to_residues(do, outs[2 + 2 * g], perm_refs[g], d)
            _rows_to_residues(delta, outs[3 + 2 * g], scr_stat, d)

    out_specs = [_row_spec(tr, a), _row_spec(tr, LANES)]
    out_shape = [jax.ShapeDtypeStruct((t, a), BF16), jax.ShapeDtypeStruct((t, LANES), F32)]
    for d in RESIDUE_DILATIONS:
        out_specs += [_residue_spec(dims, d, tr, a), _residue_spec(dims, d, tr, LANES)]
        out_shape += [_residue_shape(dims, d, a, BF16), _residue_shape(dims, d, LANES, F32)]
    n_in = 4 + nres
    body, more_specs, more_args = _ordered(body, n_in, after)
    outs = pl.pallas_call(
        body, name=name, grid=(t // tr,),
        in_specs=[_row_spec(tr, dy.shape[1]), pl.BlockSpec((None,) + w.shape[1:], lambda i: (0, 0, 0)), _row_spec(tr, a),
                  pl.BlockSpec((a, LANES), lambda i: (0, 0))] + [pl.BlockSpec((tr, tr), lambda i: (0, 0))] * nres + more_specs,
        out_specs=out_specs, out_shape=out_shape,
        scratch_shapes=[pltpu.VMEM((1, tr, LANES), F32)],
        compiler_params=_params("parallel"),
    )(dy, w, o, head_sum, *_residue_permutations(tr), *more_args)
    dos, deltas = {1: outs[0]}, {1: outs[1]}
    for g, d in enumerate(RESIDUE_DILATIONS):
        dos[d], deltas[d] = outs[2 + 2 * g], outs[3 + 2 * g]
    return dos, deltas


def _attn_bwd(q, k, v, do, lse, delta, dims, dil, *, name):
    a = dims.n_heads * dims.head_dim
    heads, hd, blk = dims.n_heads, dims.head_dim, ATTN_BLOCK
    nb = dims.seq // dil // blk
    has_next = nb > 1
    nq = 2 * blk if has_next else blk
    grid, cur, _, nxt = _attn_specs(dims, dil, a)
    _, cur_stat, _, nxt_stat = _attn_specs(dims, dil, LANES)

    def body(*refs):
        k_ref, v_ref, q_ref, do_ref, lse_ref, dl_ref = refs[:6]
        if has_next:
            qn_ref, don_ref, lsen_ref, dln_ref = refs[6:10]
            dq_ref, dk_ref, dv_ref, q_st, do_st, s_scr, dp_scr, p_scr, ds_scr, carry = refs[10:]
        else:
            dq_ref, dk_ref, dv_ref, q_st, do_st, s_scr, dp_scr, p_scr, ds_scr = refs[6:]
        j = pl.program_id(len(grid) - 1)
        low, high = _pair_masks(hd)
        q_st[0:blk, :] = q_ref[...]
        do_st[0:blk, :] = do_ref[...]
        if has_next:
            q_st[blk:, :] = qn_ref[...]
            do_st[blk:, :] = don_ref[...]
            lse_all = jnp.concatenate([lse_ref[...], lsen_ref[...]], axis=0)
            dl_all = jnp.concatenate([dl_ref[...], dln_ref[...]], axis=0)
        else:
            lse_all, dl_all = lse_ref[...], dl_ref[...]
        lse_t, dl_t = jnp.transpose(lse_all), jnp.transpose(dl_all)
        lse3 = jnp.stack([lse_t[h:h + 1, :] for h in range(heads)])
        dl3 = jnp.stack([dl_t[h:h + 1, :] for h in range(heads)])

        def halves(x):
            return jnp.where(low, x, jnp.zeros_like(x)), jnp.where(high, x, jnp.zeros_like(x))

        for hp in range(heads // 2):
            sl = slice(LANES * hp, LANES * (hp + 1))
            k2, v2 = k_ref[:, sl], v_ref[:, sl]
            q_a, q_b = halves(q_st[:, sl])
            do_a, do_b = halves(do_st[:, sl])
            s_scr[2 * hp], s_scr[2 * hp + 1] = _dot_nt(k2, q_a), _dot_nt(k2, q_b)
            dp_scr[2 * hp], dp_scr[2 * hp + 1] = _dot_nt(v2, do_a), _dot_nt(v2, do_b)

        jk = lax.broadcasted_iota(jnp.int32, (blk, nq), 0)
        rq = lax.broadcasted_iota(jnp.int32, (blk, nq), 1)
        if has_next:
            iq = jnp.where(rq < blk, rq, rq - blk)
            steps = jnp.where(rq < blk, iq - jk, iq - jk + blk)
            valid = ((rq < blk) & (iq >= jk)) | ((rq >= blk) & (jk >= iq) & (j + 1 < nb))
        else:
            steps, valid = rq - jk, rq >= jk
        bias = jnp.where(valid, steps.astype(F32) * (-float(dil)), -MASK_BIAS)
        p = jnp.exp(s_scr[...] + _head_slopes(heads) * bias[None] - lse3)
        p_scr[...] = p.astype(BF16)
        ds_scr[...] = (p * (dp_scr[...] - dl3)).astype(BF16)

        if has_next:
            @pl.when(j == 0)
            def _():
                carry[...] = jnp.zeros_like(carry)

        for hp in range(heads // 2):
            sl = slice(LANES * hp, LANES * (hp + 1))
            k2 = k_ref[:, sl]
            q_a, q_b = halves(q_st[:, sl])
            do_a, do_b = halves(do_st[:, sl])
            ds_a, ds_b = ds_scr[2 * hp], ds_scr[2 * hp + 1]
            dk_ref[:, sl] = (jnp.dot(ds_a, q_a, preferred_element_type=F32)
                             + jnp.dot(ds_b, q_b, preferred_element_type=F32)).astype(BF16)
            dv_ref[:, sl] = (jnp.dot(p_scr[2 * hp], do_a, preferred_element_type=F32)
                             + jnp.dot(p_scr[2 * hp + 1], do_b, preferred_element_type=F32)).astype(BF16)
            dq2 = jnp.where(low, _dot_tn(ds_a, k2), _dot_tn(ds_b, k2))
            if has_next:
                dq_ref[:, sl] = (carry[:, sl] + dq2[:blk]).astype(BF16)
                carry[:, sl] = dq2[blk:]
            else:
                dq_ref[:, sl] = dq2.astype(BF16)

    args, in_specs = [_attn_array(x, dims, dil) for x in (k, v, q, do, lse, delta)], [cur] * 4 + [cur_stat] * 2
    if has_next:
        args += [args[2], args[3], args[4], args[5]]
        in_specs += [nxt] * 2 + [nxt_stat] * 2
    shape = jax.ShapeDtypeStruct(args[2].shape, BF16)
    rs = _attn_residues(dims, dil)
    per_step = lambda dims_: dims_ if rs == 1 else (rs,) + dims_
    scratch = ([pltpu.VMEM(per_step((nq, a)), BF16)] * 2 + [pltpu.VMEM(per_step((heads, blk, nq)), F32)] * 2
               + [pltpu.VMEM(per_step((heads, blk, nq)), BF16)] * 2)
    if has_next:
        scratch.append(pltpu.VMEM(per_step((blk, a)), F32))
    grads = pl.pallas_call(
        _per_residue(body, rs), name=name, grid=grid, in_specs=in_specs, out_specs=[cur] * 3, out_shape=[shape] * 3,
        scratch_shapes=scratch,
        compiler_params=_params(*["parallel"] * (len(grid) - 1), "arbitrary"),
    )(*args)
    return tuple(g.reshape(q.shape) for g in grads)


def _qkv_layouts_bwd(z, grads, gq, gk, head_ones, dims, *, name, tr=256):
    t = z.shape[0]
    a = dims.n_heads * dims.head_dim
    q_scale = dims.head_dim ** -0.5
    dils = tuple(grads)
    nres = len(RESIDUE_DILATIONS)

    def body(q_ref, k_ref, *rest):
        d_refs = rest[:3 * len(dils)]
        gq_ref, gk_ref, sum_ref, spread_ref = rest[3 * len(dils):3 * len(dils) + 4]
        back_refs = dict(zip(RESIDUE_DILATIONS, rest[3 * len(dils) + 4:3 * len(dils) + 4 + nres]))
        dz_ref, dgq_ref, dgk_ref = rest[3 * len(dils) + 4 + nres:]
        first = pl.program_id(0) == 0
        mean = lambda val: _two_pass_dot(_two_pass_dot(val, sum_ref[...]), spread_ref[...]) * (1.0 / dims.head_dim)

        def total(j):
            acc = None
            for g, d in enumerate(dils):
                ref = d_refs[3 * g + j]
                part = ref[...].astype(F32) if d == 1 else _bf16_residues_to_rows(ref, back_refs[d])
                acc = part if acc is None else acc + part
            return acc

        def norm_bwd(x_ref, dy, g_ref, scale, col, dg_ref):
            xv = x_ref[...].astype(F32)
            dy = dy * scale
            r = lax.rsqrt(mean(xv * xv) + RMS_EPS)
            gy = dy * g_ref[...]
            dx = r * gy - xv * (r * r * r) * mean(xv * gy)
            dz_ref[:, col * a:(col + 1) * a] = dx.astype(BF16)
            _accumulate(dg_ref, jnp.sum(dy * xv * r, axis=0, keepdims=True), first)

        norm_bwd(q_ref, total(0), gq_ref, q_scale, 0, dgq_ref)
        norm_bwd(k_ref, total(1), gk_ref, 1.0, 1, dgk_ref)
        dz_ref[:, 2 * a:3 * a] = total(2).astype(BF16)

    in_specs, args = [_row_spec(tr, a, 2), _row_spec(tr, a, 3)], [z, z]
    for d in dils:
        in_specs += [_row_spec(tr, a) if d == 1 else _residue_spec(dims, d, tr, a)] * 3
        args += list(grads[d])
    in_specs += [_vec_spec(a), _vec_spec(a), pl.BlockSpec((a, LANES), lambda i: (0, 0)),
                 pl.BlockSpec((LANES, a), lambda i: (0, 0))] + [pl.BlockSpec((tr, tr), lambda i: (0, 0))] * nres
    return pl.pallas_call(
        body, name=name, grid=(t // tr,), in_specs=in_specs,
        out_specs=[_row_spec(tr, 3 * a), _vec_spec(a), _vec_spec(a)],
        out_shape=[jax.ShapeDtypeStruct((t, 3 * a), BF16)] + [jax.ShapeDtypeStruct((1, a), F32)] * 2,
        compiler_params=_params("arbitrary"),
    )(*args, gq, gk, *head_ones, *[jnp.transpose(p) for p in _residue_permutations(tr)])


def _mix_fwd(a3, o, w_a, w_b, w_out, z, gate_b, x, g2, dims, *, name, tr=512):
    t, d = x.shape
    tr = _pick(t, tr, 8)
    first_gate_col = z.shape[1] // d - 2

    def body(a_ref, o_ref, wa_ref, wb_ref, wo_ref, ga_ref, gb_ref, ba_ref, bb_ref, x_ref, g2_ref,
             ya_ref, yb_ref, mix_ref, x1_ref, h2_ref):
        ya = jnp.dot(a_ref[...], wa_ref[...], preferred_element_type=F32)
        yb = jnp.dot(o_ref[...], wb_ref[...], preferred_element_type=F32)
        ya_ref[...] = ya
        yb_ref[...] = yb
        g_a = _sigmoid(ga_ref[...].astype(F32) + ba_ref[...])
        g_b = _sigmoid(gb_ref[...].astype(F32) + bb_ref[...])
        mixed = (g_a * ya + g_b * yb).astype(BF16)
        mix_ref[...] = mixed
        x1 = x_ref[...] + jnp.dot(mixed, wo_ref[...], preferred_element_type=F32)
        x1_ref[...] = x1
        h2_ref[...] = (x1 * lax.rsqrt(jnp.mean(x1 * x1, axis=-1, keepdims=True) + RMS_EPS) * g2_ref[...]).astype(BF16)

    weight = pl.BlockSpec((None, d, d), lambda i: (0, 0, 0))
    rows = _row_spec(tr, d)
    return pl.pallas_call(
        body, name=name, grid=(t // tr,),
        in_specs=[rows, rows, weight, weight, weight, _row_spec(tr, d, first_gate_col),
                  _row_spec(tr, d, first_gate_col + 1), _vec_spec(d, 0), _vec_spec(d, 1), rows, _vec_spec(d)],
        out_specs=[rows] * 5,
        out_shape=[jax.ShapeDtypeStruct((t, d), dt) for dt in (F32, F32, BF16, F32, BF16)],
        compiler_params=_params("parallel"),
    )(a3, o, w_a, w_b, w_out, z, z, gate_b, gate_b, x, g2)


def _mix_bwd(dx, w, ya, yb, z, gate_b, dims, *, name, after=None, tr=512):
    t, d = ya.shape
    tr = _pick(t, tr, 8)
    first_gate_col = z.shape[1] // d - 2

    def body(dx_ref, w_ref, ya_ref, yb_ref, ga_ref, gb_ref, ba_ref, bb_ref, dya_ref, dyb_ref, dz_ref, db_ref):
        dm = _dot_nt(dx_ref[...], w_ref[...])
        g_a = _sigmoid(ga_ref[...].astype(F32) + ba_ref[...])
        g_b = _sigmoid(gb_ref[...].astype(F32) + bb_ref[...])
        dya_ref[...] = (dm * g_a).astype(BF16)
        dyb_ref[...] = (dm * g_b).astype(BF16)
        dl_a = dm * ya_ref[...] * g_a * (1.0 - g_a)
        dl_b = dm * yb_ref[...] * g_b * (1.0 - g_b)
        dz_ref[:, 0:d] = dl_a.astype(BF16)
        dz_ref[:, d:2 * d] = dl_b.astype(BF16)
        first = pl.program_id(0) == 0
        sums = jnp.concatenate([jnp.sum(dl_a, axis=0, keepdims=True), jnp.sum(dl_b, axis=0, keepdims=True)], axis=1)
        _accumulate(db_ref, sums, first)

    body, more_specs, more_args = _ordered(body, 8, after)
    return pl.pallas_call(
        body, name=name, grid=(t // tr,),
        in_specs=[_row_spec(tr, d), pl.BlockSpec((None, d, d), lambda i: (0, 0, 0)), _row_spec(tr, d), _row_spec(tr, d),
                  _row_spec(tr, d, first_gate_col), _row_spec(tr, d, first_gate_col + 1), _vec_spec(d, 0),
                  _vec_spec(d, 1)] + more_specs,
        out_specs=[_row_spec(tr, d), _row_spec(tr, d), _row_spec(tr, 2 * d), _vec_spec(2 * d)],
        out_shape=[jax.ShapeDtypeStruct((t, d), BF16)] * 2 + [jax.ShapeDtypeStruct((t, 2 * d), BF16),
                                                              jax.ShapeDtypeStruct((1, 2 * d), F32)],
        compiler_params=_params("arbitrary"),
    )(dx, w, ya, yb, z, z, gate_b, gate_b, *more_args)


def _adamw(w, grads, m, v, *, name, tr=256):
    r, c = w.shape
    tr = _pick(r, tr, 8)
    ng = len(grads)
    c1 = 1.0 - ADAM_B1 ** ADAM_STEP
    c2 = 1.0 - ADAM_B2 ** ADAM_STEP

    def body(*refs):
        w_ref, g_refs, m_ref, v_ref = refs[0], refs[1:1 + ng], refs[1 + ng], refs[2 + ng]
        g_out, d_out, m_out, v_out = refs[3 + ng:]
        g = g_refs[0][...]
        for extra in g_refs[1:]:
            g = g + extra[...]
        m_new = ADAM_B1 * m_ref[...] + (1.0 - ADAM_B1) * g
        v_new = ADAM_B2 * v_ref[...] + (1.0 - ADAM_B2) * (g * g)
        g_out[...] = g
        m_out[...] = m_new
        v_out[...] = v_new
        d_out[...] = -ADAM_LR * ((m_new / c1) / (jnp.sqrt(v_new / c2) + ADAM_EPS) + ADAM_WD * w_ref[...])

    spec = pl.BlockSpec((tr, c), lambda i: (i, 0))
    return pl.pallas_call(
        body, name=name, grid=(r // tr,),
        in_specs=[spec] * (3 + ng), out_specs=[spec] * 4, out_shape=[jax.ShapeDtypeStruct((r, c), F32)] * 4,
        compiler_params=_params("parallel"),
    )(w, *grads, m, v)


CHIP_PEERS = ((1, 0), (0, 1), (1, 1))


def _place():
    return lax.axis_index("x"), lax.axis_index("y"), lax.axis_index("c")


HBM = pl.BlockSpec(memory_space=pltpu.HBM)
SEM = pl.BlockSpec(memory_space=pltpu.SEMAPHORE)
IN_FLIGHT = pltpu.SideEffectType.DATAFLOW_SIDE_EFFECTING


def _in_hbm(a):
    return pltpu.with_memory_space_constraint(a, pltpu.HBM)


def _cast_to_lands(shards, dtypes, *, name, after=None):
    n = len(shards)

    def body(*refs):
        ins, outs, bufs, sems = refs[:n], refs[n:2 * n], refs[2 * n:3 * n], refs[3 * n]
        x, y, _ = _place()
        copies = []
        for a in range(n):
            bufs[a][...] = ins[a][...].astype(dtypes[a])
            cp = pltpu.make_async_copy(bufs[a], outs[a].at[2 * x + y], sems.at[a])
            cp.start()
            copies.append(cp)
        for cp in copies:
            cp.wait()

    body, more_specs, more_args = _ordered(body, n, after)
    return pl.pallas_call(
        body, name=name, in_specs=[pl.BlockSpec(memory_space=pltpu.VMEM)] * n + more_specs, out_specs=[ANY] * n,
        out_shape=[jax.ShapeDtypeStruct((N_CHIPS,) + s.shape, dt) for s, dt in zip(shards, dtypes)],
        scratch_shapes=[pltpu.VMEM(s.shape, dt) for s, dt in zip(shards, dtypes)] + [pltpu.SemaphoreType.DMA((n,))],
        compiler_params=pltpu.CompilerParams(vmem_limit_bytes=V7X_VMEM_LIMIT_BYTES),
    )(*shards, *more_args)


def _chip_copy(src, dst, send, recv, flip, place):
    x, y, c = place
    return pltpu.make_async_remote_copy(src_ref=src, dst_ref=dst, send_sem=send, recv_sem=recv,
                                        device_id=(x ^ flip[0], y ^ flip[1], c), device_id_type=MESH)


def _my_part(land, place, halved):
    block = land.at[2 * place[0] + place[1]]
    if not halved:
        return block
    rows = land.shape[1] // 2
    return block.at[pl.ds(pl.multiple_of(place[2] * rows, rows), rows)]


def _gather_start(lands, after, *, name, halved=()):
    n = len(lands)

    def body(*refs):
        ins, send, recv, token = refs[:n], refs[n + 1], refs[n + 2], refs[-1]
        place = _place()
        for a in range(n):
            part = _my_part(ins[a], place, a in halved)
            for p, flip in enumerate(CHIP_PEERS):
                k = 3 * a + p
                _chip_copy(part, part, send.at[k], recv.at[k], flip, place).start()
        token[...] = jnp.zeros_like(token)

    outs = pl.pallas_call(
        body, name=name, in_specs=[HBM] * n + [ANY],
        out_specs=(SEM, SEM, *[HBM] * n, pl.BlockSpec(memory_space=pltpu.VMEM)),
        out_shape=(pltpu.SemaphoreType.DMA((3 * n,)), pltpu.SemaphoreType.DMA((3 * n,)),
                   *[pltpu.HBM(l.shape, l.dtype) for l in lands], jax.ShapeDtypeStruct((8, 128), F32)),
        input_output_aliases={a: 2 + a for a in range(n)},
        compiler_params=pltpu.CompilerParams(has_side_effects=IN_FLIGHT),
    )(*[_in_hbm(l) for l in lands], after)
    return outs[0], outs[1], list(outs[2:2 + n]), outs[-1]


def _gather_wait(send, recv, lands, after, *, name, halved=()):
    n = len(lands)

    def body(*refs):
        ins, send_ref, recv_ref = refs[:n], refs[n], refs[n + 1]
        place = _place()
        for a in range(n):
            part = _my_part(ins[a], place, a in halved)
            for p, flip in enumerate(CHIP_PEERS):
                k = 3 * a + p
                cp = _chip_copy(part, part, send_ref.at[k], recv_ref.at[k], flip, place)
                cp.wait_send()
                cp.wait_recv()

    after = list(after) if isinstance(after, (list, tuple)) else [after]
    return pl.pallas_call(
        body, name=name, in_specs=[HBM] * n + [SEM, SEM] + [ANY] * len(after), out_specs=[HBM] * n,
        out_shape=[pltpu.HBM(l.shape, l.dtype) for l in lands],
        input_output_aliases={a: a for a in range(n)},
        compiler_params=pltpu.CompilerParams(has_side_effects=IN_FLIGHT),
    )(*lands, send, recv, *after)


def _forward_to_sibling(land, *, name):
    rows = land.shape[1] // 2

    def body(land_ref, out_ref, send, recv):
        x, y, c = _place()
        copies = []
        for p, (fx, fy) in enumerate(CHIP_PEERS):
            chip = 2 * (x ^ fx) + (y ^ fy)
            mine = pl.ds(pl.multiple_of(c * rows, rows), rows)
            theirs = pl.ds(pl.multiple_of((1 - c) * rows, rows), rows)
            out = pltpu.make_async_remote_copy(
                src_ref=land_ref.at[chip].at[mine], dst_ref=out_ref.at[chip].at[mine], send_sem=send.at[p],
                recv_sem=recv.at[p], device_id=(x, y, 1 - c), device_id_type=MESH)
            out.start()
            copies.append((out, pltpu.make_async_remote_copy(
                src_ref=land_ref.at[chip].at[theirs], dst_ref=out_ref.at[chip].at[theirs], send_sem=send.at[p],
                recv_sem=recv.at[p], device_id=(x, y, 1 - c), device_id_type=MESH)))
        for out, arriving in copies:
            out.wait_send()
            arriving.wait_recv()

    return pl.pallas_call(
        body, name=name, in_specs=[ANY], out_specs=ANY, out_shape=jax.ShapeDtypeStruct(land.shape, land.dtype),
        input_output_aliases={0: 0},
        scratch_shapes=[pltpu.SemaphoreType.DMA((3,)), pltpu.SemaphoreType.DMA((3,))],
    )(land)


def _scatter_start(grad, *, name):
    def body(g_ref, land_ref, send, recv, g_thru, land_thru, token):
        place = _place()
        for p, flip in enumerate(CHIP_PEERS):
            peer_chip = 2 * (place[0] ^ flip[0]) + (place[1] ^ flip[1])
            _chip_copy(g_ref.at[peer_chip], land_ref.at[p], send.at[p], recv.at[p], flip, place).start()
        token[...] = jnp.zeros_like(token)

    land = lax.empty((3,) + grad.shape[1:], grad.dtype)
    return pl.pallas_call(
        body, name=name, in_specs=[HBM, HBM],
        out_specs=(SEM, SEM, HBM, HBM, pl.BlockSpec(memory_space=pltpu.VMEM)),
        out_shape=(pltpu.SemaphoreType.DMA((3,)), pltpu.SemaphoreType.DMA((3,)), pltpu.HBM(grad.shape, grad.dtype),
                   pltpu.HBM(land.shape, land.dtype), jax.ShapeDtypeStruct((8, 128), F32)),
        input_output_aliases={0: 2, 1: 3},
        compiler_params=pltpu.CompilerParams(has_side_effects=IN_FLIGHT),
    )(_in_hbm(grad), _in_hbm(land))


def _scatter_wait(started, after, *, name):
    n = len(started)

    def body(*refs):
        grads, lands = refs[:n], refs[n:2 * n]
        sends, recvs = refs[2 * n:3 * n], refs[3 * n:4 * n]
        place = _place()
        for a in range(n):
            for p, flip in enumerate(CHIP_PEERS):
                cp = _chip_copy(grads[a].at[0], lands[a].at[p], sends[a].at[p], recvs[a].at[p], flip, place)
                cp.wait_send()
                cp.wait_recv()

    grads, lands = [s[2] for s in started], [s[3] for s in started]
    after = list(after) if isinstance(after, (list, tuple)) else [after]
    outs = pl.pallas_call(
        body, name=name, in_specs=[HBM] * (2 * n) + [SEM] * (2 * n) + [ANY] * len(after), out_specs=[HBM] * (2 * n),
        out_shape=[pltpu.HBM(a.shape, a.dtype) for a in grads + lands],
        input_output_aliases={a: a for a in range(2 * n)},
        compiler_params=pltpu.CompilerParams(has_side_effects=IN_FLIGHT),
    )(*grads, *lands, *[s[0] for s in started], *[s[1] for s in started], *after)
    return list(zip(outs[:n], outs[n:]))


def _sibling_copy(src, dst, send, recv, place):
    x, y, c = place
    return pltpu.make_async_remote_copy(src_ref=src, dst_ref=dst, send_sem=send, recv_sem=recv,
                                        device_id=(x, y, 1 - c), device_id_type=MESH)


def _swap_start(arrays, *, name):
    n = len(arrays)

    def body(*refs):
        ins, lands, send, recv, token = refs[:n], refs[n:2 * n], refs[2 * n], refs[2 * n + 1], refs[-1]
        place = _place()
        for a in range(n):
            _sibling_copy(ins[a], lands[a], send.at[a], recv.at[a], place).start()
        token[...] = jnp.zeros_like(token)

    both = [_in_hbm(a) for a in arrays] + [_in_hbm(lax.empty(a.shape, a.dtype)) for a in arrays]
    outs = pl.pallas_call(
        body, name=name, in_specs=[HBM] * (2 * n),
        out_specs=(SEM, SEM, *[HBM] * (2 * n), pl.BlockSpec(memory_space=pltpu.VMEM)),
        out_shape=(pltpu.SemaphoreType.DMA((n,)), pltpu.SemaphoreType.DMA((n,)),
                   *[pltpu.HBM(a.shape, a.dtype) for a in both], jax.ShapeDtypeStruct((8, 128), F32)),
        input_output_aliases={a: 2 + a for a in range(2 * n)},
        compiler_params=pltpu.CompilerParams(has_side_effects=IN_FLIGHT),
    )(*both)
    return outs[0], outs[1], list(outs[2:2 + n]), list(outs[2 + n:2 + 2 * n]), outs[-1]


def _swap_wait(started, after, *, name):
    send, recv, arrays, lands = started[:4]
    n = len(arrays)

    def body(*refs):
        ins, zones, send_ref, recv_ref = refs[:n], refs[n:2 * n], refs[2 * n], refs[2 * n + 1]
        place = _place()
        for a in range(n):
            cp = _sibling_copy(ins[a], zones[a], send_ref.at[a], recv_ref.at[a], place)
            cp.wait_send()
            cp.wait_recv()

    after = list(after) if isinstance(after, (list, tuple)) else [after]
    outs = pl.pallas_call(
        body, name=name, in_specs=[HBM] * (2 * n) + [SEM, SEM] + [ANY] * len(after), out_specs=[HBM] * (2 * n),
        out_shape=[pltpu.HBM(a.shape, a.dtype) for a in arrays + lands],
        input_output_aliases={a: a for a in range(2 * n)},
        compiler_params=pltpu.CompilerParams(has_side_effects=IN_FLIGHT),
    )(*arrays, *lands, send, recv, *after)
    return list(outs[:n]), list(outs[n:])


def _allreduce_start(packed, *, name):
    n_dev = 8

    def body(src_ref, land_ref, send, recv, src_thru, land_thru, token):
        x, y, c = _place()
        me = 4 * x + 2 * y + c
        for p in range(1, n_dev):
            pltpu.make_async_remote_copy(
                src_ref=src_ref, dst_ref=land_ref.at[me], send_sem=send.at[p - 1], recv_sem=recv.at[p - 1],
                device_id=(x ^ (p >> 2), y ^ ((p >> 1) & 1), c ^ (p & 1)), device_id_type=MESH).start()
        token[...] = jnp.zeros_like(token)

    land = lax.empty((n_dev,) + packed.shape, packed.dtype)
    return pl.pallas_call(
        body, name=name, in_specs=[HBM, HBM],
        out_specs=(SEM, SEM, HBM, HBM, pl.BlockSpec(memory_space=pltpu.VMEM)),
        out_shape=(pltpu.SemaphoreType.DMA((n_dev - 1,)), pltpu.SemaphoreType.DMA((n_dev - 1,)),
                   pltpu.HBM(packed.shape, packed.dtype), pltpu.HBM(land.shape, land.dtype),
                   jax.ShapeDtypeStruct((8, 128), F32)),
        input_output_aliases={0: 2, 1: 3},
        compiler_params=pltpu.CompilerParams(has_side_effects=IN_FLIGHT),
    )(_in_hbm(packed), _in_hbm(land))


def _allreduce_wait(started, after, *, name):
    send, recv, packed, land = started[:4]
    n_dev = 8

    def body(src_ref, land_ref, send_ref, recv_ref, *_):
        x, y, c = _place()
        for p in range(1, n_dev):
            cp = pltpu.make_async_remote_copy(
                src_ref=src_ref, dst_ref=land_ref.at[0], send_sem=send_ref.at[p - 1], recv_sem=recv_ref.at[p - 1],
                device_id=(x ^ (p >> 2), y ^ ((p >> 1) & 1), c ^ (p & 1)), device_id_type=MESH)
            cp.wait_send()
            cp.wait_recv()

    after = list(after) if isinstance(after, (list, tuple)) else [after]
    return pl.pallas_call(
        body, name=name, in_specs=[HBM, HBM, SEM, SEM] + [ANY] * len(after), out_specs=[HBM, HBM],
        out_shape=[pltpu.HBM(packed.shape, packed.dtype), pltpu.HBM(land.shape, land.dtype)],
        input_output_aliases={0: 0, 1: 1},
        compiler_params=pltpu.CompilerParams(has_side_effects=IN_FLIGHT),
    )(packed, land, send, recv, *after)


def _sum_devices(mine, land, *, name):
    n_dev = land.shape[0]

    def body(mine_ref, land_ref, out_ref):
        x, y, c = _place()
        me = 4 * x + 2 * y + c
        total = None
        for s in range(n_dev):
            part = jnp.where(me == s, mine_ref[...], land_ref[s])
            total = part if total is None else total + part
        out_ref[...] = total

    return pl.pallas_call(body, name=name, out_shape=jax.ShapeDtypeStruct(mine.shape, mine.dtype))(mine, land)


def _sum_received(grad, land, *, name, tr=256):
    _, r, c = grad.shape
    tr = _pick(r, tr, 8)

    def body(chip_ref, g_ref, l_ref, o_ref):
        o_ref[...] = ((g_ref[...] + l_ref[0].astype(F32)) + l_ref[1].astype(F32)) + l_ref[2].astype(F32)

    chip = (2 * lax.axis_index("x") + lax.axis_index("y")).astype(jnp.int32).reshape(1)
    return pl.pallas_call(
        body, name=name,
        grid_spec=pltpu.PrefetchScalarGridSpec(
            num_scalar_prefetch=1, grid=(r // tr,),
            in_specs=[pl.BlockSpec((None, tr, c), lambda i, chip_ref: (chip_ref[0], i, 0)),
                      pl.BlockSpec((3, tr, c), lambda i, chip_ref: (0, i, 0))],
            out_specs=pl.BlockSpec((tr, c), lambda i, chip_ref: (i, 0))),
        out_shape=jax.ShapeDtypeStruct((r, c), F32), compiler_params=_params("parallel"),
    )(chip, grad, land)


def _packed_rows(size, d):
    return -(-size // (8 * d)) * 8


def _pack_rows(arrays, d):
    rows = []
    for arr in arrays:
        flat = arr.reshape(-1).astype(F32)
        n = _packed_rows(flat.shape[0], d)
        rows.append(jnp.pad(flat, (0, n * d - flat.shape[0])).reshape(n, d))
    return jnp.concatenate(rows, axis=0)


def _unpack_rows(packed, shapes, d):
    out, row = [], 0
    for shape in shapes:
        size = math.prod(shape)
        n = _packed_rows(size, d)
        out.append(packed[row:row + n].reshape(-1)[:size].reshape(shape))
        row += n
    return out


SMALL = ("norm1_g", "gate_b", "conv_b", "conv_norm_g", "q_norm_g", "k_norm_g", "norm2_g", "ffn_conv_b")
LARGE = ("w_in", "w_conv_out", "w_attn_out", "w_out", "w_up", "w_down")
WEIGHTS = ("norm1_g", "w_in", "gate_b", "conv_w", "conv_b", "conv_norm_g", "w_conv_out", "q_norm_g", "k_norm_g",
           "w_attn_out", "w_out", "norm2_g", "w_up", "ffn_conv_w", "ffn_conv_b", "w_down")


def _after(vec, token):
    return vec if token is None else vec + token[0:1, 0:1]


def _local_step(dims, x, target, small, first_weights, other_weights, send_grad):
    d, f, heads = dims.d_model, dims.d_ff, dims.n_heads
    small = dict(small)
    row = lambda name: small[name].reshape(1, -1)
    head_sum = _head_sum_matrix(dims)
    head_spread = jnp.transpose(head_sum)
    ones = (head_sum, head_spread)
    gq = jnp.tile(row("q_norm_g"), (1, heads))
    gk = jnp.tile(row("k_norm_g"), (1, heads))
    one_shard = lambda w: w.reshape(1, -1, w.shape[-1])

    h = _rmsnorm_fwd(x, row("norm1_g"), name="norm1")
    full = first_weights(h)
    w_in = full["w_in"]
    conv_w = jnp.pad(full["conv_w"], ((0, CONV_HALO - dims.conv_width), (0, 0)))
    ffn_w = jnp.pad(full["ffn_conv_w"], ((0, FFN_HALO - dims.ffn_conv_width), (0, 0)))
    z = _mm_nn(h, w_in, out_dtype=BF16, after=full.get("token"), tm=2048, tn=1792, name="in_proj")
    a1, a3 = _conv_branch_fwd(z, conv_w, row("conv_b"), row("conv_norm_g"), dims, name="conv_branch")
    qkv = _qkv_layouts_fwd(z, gq, gk, ones, dims, name="qk_norm")
    per_group = {dil: _attn_fwd(*qkv[dil], dims, dil, name=f"attn_fwd_d{dil}") for dil in DILATIONS}
    o, lse = _attn_combine(per_group, head_spread, dims, name="attn_combine")
    full = other_weights(o)
    w_up = full["w_up"]
    w_co, w_ao, w_o, w_dn = (one_shard(full[k]) for k in ("w_conv_out", "w_attn_out", "w_out", "w_down"))
    ya, yb, mixed, x1, h2 = _mix_fwd(a3, o, w_co, w_ao, w_o, z, row("gate_b"), x, row("norm2_g"), dims,
                                     name="branch_projs_mix_out_proj_norm2")
    up = _mm_nn(h2, w_up, out_dtype=F32, tm=2048, name="up_proj")
    act = _ffn_act_fwd(up, ffn_w, row("ffn_conv_b"), dims, name="ffn_act")
    dy, dy_b, loss = _proj_residual_loss(act, w_dn, x1, target, tm=512, name="down_proj_loss")

    grads = {}

    def large(name, g):
        grads[name], g_bf16 = g
        return send_grad(name, g_bf16)

    sent = large("w_down", _mm_tn(act, dy_b, n_shards=1, name="dw_down"))
    dact = _mm_nt(dy_b, w_dn, out_dtype=BF16, after=sent, name="d_act")
    dup, dfw, dfb = _ffn_bwd(dact, up, ffn_w, row("ffn_conv_b"), dims, name="ffn_bwd")
    grads["ffn_conv_w"], grads["ffn_conv_b"] = dfw[:dims.ffn_conv_width], dfb
    sent = large("w_up", _mm_tn(h2, dup, n_shards=N_CHIPS, name="dw_up"))
    dx1, dx1_b, grads["norm2_g"] = _mm_nt_rmsnorm_bwd(dup, w_up, x1, row("norm2_g"), dy, want_bf16=True, after=sent,
                                                     name="d_h2_norm2_bwd")
    sent = large("w_out", _mm_tn(mixed, dx1_b, n_shards=1, name="dw_out"))
    dya, dyb, dz_gate, grads["gate_b"] = _mix_bwd(dx1_b, w_o, ya, yb, z, row("gate_b"), dims, after=sent,
                                                  name="d_mix_gate_mix_bwd")
    sent = large("w_attn_out", _mm_tn(o, dyb, n_shards=1, name="dw_attn_out"))
    dos, deltas = _attn_bwd_prep(dyb, w_ao, o, head_sum, dims, after=sent, name="d_attn_bwd_prep")
    dqkv = {dil: _attn_bwd(*qkv[dil], dos[dil], lse[dil], deltas[dil], dims, dil, name=f"attn_bwd_d{dil}")
            for dil in DILATIONS}
    dz_qkv, dgq, dgk = _qkv_layouts_bwd(z, dqkv, gq, gk, ones, dims, name="qk_norm_bwd")
    grads["q_norm_g"] = dgq.reshape(heads, dims.head_dim).sum(axis=0)
    grads["k_norm_g"] = dgk.reshape(heads, dims.head_dim).sum(axis=0)
    sent = large("w_conv_out", _mm_tn(a3, dya, n_shards=1, name="dw_conv_out"))
    da1, grads["conv_norm_g"] = _mm_nt_rmsnorm_bwd(dya, w_co, a1, row("conv_norm_g"), None, want_bf16=False, silu=True,
                                                   after=sent, name="d_conv_act_norm_bwd")
    dz, dcw, grads["conv_b"] = _conv_branch_bwd(da1, z, conv_w, [dz_qkv, dz_gate], dims, name="conv_branch_bwd")
    grads["conv_w"] = dcw[:dims.conv_width]
    sent = large("w_in", _mm_tn(h, dz, n_shards=N_CHIPS, name="dw_in"))
    dx, grads["norm1_g"] = _mm_nt_rmsnorm_bwd(dz, w_in, x, row("norm1_g"), dx1, want_bf16=False, after=sent,
                                              name="d_h_norm1_bwd")
    return loss, dx, grads


def _step(dims, x, target, w, m, v):
    d = dims.d_model
    t = dims.tokens
    sq = lambda a: a.reshape(a.shape[1:])
    w2, m2, v2 = ({k: sq(a) for k, a in grp.items()} for grp in (w, m, v))

    conv_pad = jnp.pad(w2["conv_w"], ((0, CONV_HALO - dims.conv_width), (0, 0)))
    ffn_pad = jnp.pad(w2["ffn_conv_w"], ((0, FFN_HALO - dims.ffn_conv_width), (0, 0)))
    first_names = ("w_in", "conv_w", "ffn_conv_w")
    other_names = tuple(k for k in LARGE if k not in first_names)
    lands = dict(zip(first_names, _cast_to_lands([w2["w_in"], conv_pad, ffn_pad], [BF16, F32, F32], name="cast_first")))
    first = _gather_start([lands[k] for k in first_names], x, halved=(0,), name="gather_start_first")
    lands.update(zip(other_names, _cast_to_lands([w2[k] for k in other_names], [BF16] * len(other_names),
                                                 after=first[3], name="cast_other")))
    other = []
    cols = lambda g, rows: jnp.moveaxis(g, 0, 1).reshape(g.shape[1], -1)[:rows]

    def first_weights(after):
        got = dict(zip(first_names, _gather_wait(*first[:3], [after] + [lands[k] for k in other_names], halved=(0,),
                                                 name="gather_wait_first")))
        got["w_in"] = _forward_to_sibling(got["w_in"], name="forward_w_in")
        other.extend(_gather_start([lands[k] for k in other_names], got["w_in"], name="gather_start_other"))
        got["conv_w"] = cols(got["conv_w"], dims.conv_width)
        got["ffn_conv_w"] = cols(got["ffn_conv_w"], dims.ffn_conv_width)
        got["token"] = other[3]
        return got

    def other_weights(after):
        return dict(zip(other_names, _gather_wait(*other[:3], after, name="gather_wait_other")))

    started = {}

    def send_grad(name, g):
        send, recv, g_thru, land, token = _scatter_start(g.reshape(N_CHIPS, -1, g.shape[-1]), name=f"scatter_start_{name}")
        started[name] = (send, recv, g_thru, land)
        return token

    small = {k: w2[k] for k in SMALL}
    small["norm1_g"] = _after(small["norm1_g"].reshape(1, -1), first[3])
    loss, dx, grads = _local_step(dims, x.reshape(t, d), target.reshape(t, d), small, first_weights, other_weights, send_grad)

    def my_sums(names, after, tag):
        arrived = _scatter_wait([started[k] for k in names], after, name=f"scatter_wait_{tag}")
        blocks = [grads[k].reshape(N_CHIPS, -1, grads[k].shape[-1]) for k in names]
        return [_sum_received(g, land, name=f"sum_{k}") for k, g, (_, land) in zip(names, blocks, arrived)]

    def updates(names, mine, theirs):
        return {k: _adamw(w2[k], [a, b], m2[k], v2[k], name=f"adamw_{k}") for k, a, b in zip(names, mine, theirs)}

    small_names = SMALL + ("conv_w", "ffn_conv_w")
    packed = _pack_rows([grads[k] for k in small_names] + [loss[0, 0]], d)
    reducing = _allreduce_start(packed, name="allreduce_start")
    others = [k for k in LARGE if k != "w_in"]
    mine_others = my_sums(others, [dx, reducing[4]], "others")
    swapping_others = _swap_start(mine_others, name="swap_start_others")
    out = updates(others, *_swap_wait(swapping_others, [], name="swap_wait_others"))
    mine_w_in = my_sums(["w_in"], [out[k][1] for k in others], "w_in")
    swapping_w_in = _swap_start(mine_w_in, name="swap_start_w_in")
    reduced = _sum_devices(*_allreduce_wait(reducing, swapping_w_in[4], name="allreduce_wait"), name="allreduce_sum")
    vector_rows = sum(_packed_rows(math.prod(grads[k].shape), d) for k in SMALL)
    tail_shapes = [grads[k].shape for k in ("conv_w", "ffn_conv_w")] + [()]
    conv_g, ffn_g, loss_total = _unpack_rows(reduced[vector_rows:], tail_shapes, d)
    chip = 2 * lax.axis_index("x") + lax.axis_index("y")
    sharded_g = [lax.dynamic_slice_in_dim(g, chip * w2[k].shape[1], w2[k].shape[1], axis=1)
                 for k, g in (("conv_w", conv_g), ("ffn_conv_w", ffn_g))]
    packed_g = jnp.concatenate([reduced[:vector_rows], _pack_rows(sharded_g, d)], axis=0)

    small_shapes = [w2[k].shape for k in small_names]
    pack = lambda grp: _pack_rows([grp[k] for k in small_names], d)
    results = _adamw(pack(w2), [packed_g], pack(m2), pack(v2), name="adamw_small")
    unpacked = [_unpack_rows(r, small_shapes, d) for r in results]
    out.update({k: tuple(u[i] for u in unpacked) for i, k in enumerate(small_names)})
    out.update(updates(["w_in"], *_swap_wait(swapping_w_in, results[1], name="swap_wait_w_in")))

    lead = lambda a: a.reshape((1,) + a.shape)
    ordered = [[lead(out[k][j].reshape(w2[k].shape)) for k in WEIGHTS] for j in range(4)]
    return (loss_total, dx.reshape(x.shape), *ordered[0], *ordered[1], *ordered[2], *ordered[3])


def kernel(x, norm1_g, w_in, gate_b, conv_w, conv_b, conv_norm_g, w_conv_out, q_norm_g, k_norm_g, w_attn_out, w_out, norm2_g, w_up, ffn_conv_w, ffn_conv_b, w_down, loss_target, m_norm1_g, m_w_in, m_gate_b, m_conv_w, m_conv_b, m_conv_norm_g, m_w_conv_out, m_q_norm_g, m_k_norm_g, m_w_attn_out, m_w_out, m_norm2_g, m_w_up, m_ffn_conv_w, m_ffn_conv_b, m_w_down, v_norm1_g, v_w_in, v_gate_b, v_conv_w, v_conv_b, v_conv_norm_g, v_w_conv_out, v_q_norm_g, v_k_norm_g, v_w_attn_out, v_w_out, v_norm2_g, v_w_up, v_ffn_conv_w, v_ffn_conv_b, v_w_down):
    w = dict(zip(WEIGHTS, (norm1_g, w_in, gate_b, conv_w, conv_b, conv_norm_g, w_conv_out, q_norm_g, k_norm_g,
                           w_attn_out, w_out, norm2_g, w_up, ffn_conv_w, ffn_conv_b, w_down)))
    m = dict(zip(WEIGHTS, (m_norm1_g, m_w_in, m_gate_b, m_conv_w, m_conv_b, m_conv_norm_g, m_w_conv_out, m_q_norm_g,
                           m_k_norm_g, m_w_attn_out, m_w_out, m_norm2_g, m_w_up, m_ffn_conv_w, m_ffn_conv_b, m_w_down)))
    v = dict(zip(WEIGHTS, (v_norm1_g, v_w_in, v_gate_b, v_conv_w, v_conv_b, v_conv_norm_g, v_w_conv_out, v_q_norm_g,
                           v_k_norm_g, v_w_attn_out, v_w_out, v_norm2_g, v_w_up, v_ffn_conv_w, v_ffn_conv_b, v_w_down)))
    dims = Dims(d_model=x.shape[-1], batch_local=x.shape[0], seq=x.shape[1], d_ff=w_down.shape[1] * N_CHIPS)
    return _step(dims, x, loss_target, w, m, v)
```

```python
import functools
import math
from typing import NamedTuple

import jax
import jax.numpy as jnp
from jax import lax
from jax.experimental import pallas as pl
from jax.experimental.pallas import tpu as pltpu

F32 = jnp.float32
BF16 = jnp.bfloat16

RMS_EPS = 1e-6
ATTN_BLOCK = 128
DILATIONS = (1, 4, 16)
CONV_HALO = 32
FFN_HALO = 8
ADAM_LR, ADAM_B1, ADAM_B2, ADAM_EPS, ADAM_WD, ADAM_STEP = 0.001, 0.9, 0.999, 1e-08, 0.01, 10
V7X_VMEM_LIMIT_BYTES = 56 * 2 ** 20
N_CHIPS = 4
MESH = pl.DeviceIdType.MESH


class Dims(NamedTuple):
    d_model: int = 1024
    n_heads: int = 16
    head_dim: int = 64
    d_ff: int = 2816
    seq: int = 2048
    batch_local: int = 2
    conv_width: int = 31
    ffn_conv_width: int = 3

    @property
    def tokens(self):
        return self.seq * self.batch_local


def _params(*semantics):
    return pltpu.CompilerParams(dimension_semantics=semantics, vmem_limit_bytes=V7X_VMEM_LIMIT_BYTES)


ANY = pl.BlockSpec(memory_space=pl.ANY)


def _ordered(body, n_inputs, after):
    after = [] if after is None else list(after) if isinstance(after, (list, tuple)) else [after]
    if not after:
        return body, [], []

    def wrapped(*refs):
        return body(*refs[:n_inputs], *refs[n_inputs + len(after):])

    return wrapped, [ANY] * len(after), after


def _pick(n, target, mult=128):
    if n <= target:
        return n
    best = None
    for t in range(mult, target + 1, mult):
        if n % t == 0:
            best = t
    assert best is not None, (n, target, mult)
    return best


def _sigmoid(v):
    return 1.0 / (1.0 + jnp.exp(-v))


def _mm_nn(a, w, *, out_dtype, name, residual=None, after=None, tm=1024, tn=1408, tk=2816):
    m, k = a.shape
    nsh, k2, c = w.shape
    assert k == k2 and a.dtype == BF16 and w.dtype == BF16
    n = nsh * c
    tm, tn, tk = _pick(m, tm, 8), _pick(c, tn), _pick(k, tk)
    nk, cpn = k // tk, c // tn

    def body(*refs):
        if residual is None:
            a_ref, w_ref, o_ref, acc = refs
        else:
            a_ref, w_ref, r_ref, o_ref, acc = refs
        prod = jnp.dot(a_ref[...], w_ref[...], preferred_element_type=F32)

        def finish(total):
            if residual is not None:
                total = total + r_ref[...]
            o_ref[...] = total.astype(out_dtype)

        if nk == 1:
            finish(prod)
        else:
            kk = pl.program_id(2)

            @pl.when(kk == 0)
            def _():
                acc[...] = prod

            @pl.when(kk > 0)
            def _():
                acc[...] += prod

            @pl.when(kk == nk - 1)
            def _():
                finish(acc[...])

    in_specs = [pl.BlockSpec((tm, tk), lambda i, j, kk: (i, kk)),
                pl.BlockSpec((None, tk, tn), lambda i, j, kk: (j // cpn, kk, j % cpn))]
    args = [a, w]
    if residual is not None:
        in_specs.append(pl.BlockSpec((tm, tn), lambda i, j, kk: (i, j)))
        args.append(residual)
    body, more_specs, more_args = _ordered(body, len(args), after)
    return pl.pallas_call(
        body, name=name, grid=(m // tm, n // tn, nk),
        in_specs=in_specs + more_specs, out_specs=pl.BlockSpec((tm, tn), lambda i, j, kk: (i, j)),
        out_shape=jax.ShapeDtypeStruct((m, n), out_dtype),
        scratch_shapes=[pltpu.VMEM((tm, tn) if nk > 1 else (8, 128), F32)],
        compiler_params=_params("parallel", "parallel", "arbitrary"),
    )(*args, *more_args)


def _proj_residual_loss(a, w, residual, target, *, name, tm=1024):
    m, k = a.shape
    _, k2, n = w.shape
    assert w.shape[0] == 1 and k == k2 and a.dtype == BF16 and w.dtype == BF16
    tm = _pick(m, tm, 8)

    def body(a_ref, w_ref, r_ref, t_ref, dy_ref, dyb_ref, loss_ref):
        err = r_ref[...] + jnp.dot(a_ref[...], w_ref[...], preferred_element_type=F32) - t_ref[...]
        dy = err * (1.0 / n)
        dy_ref[...] = dy
        dyb_ref[...] = dy.astype(BF16)
        part = jnp.sum(jnp.sum(err * err, axis=-1, keepdims=True), axis=0, keepdims=True) * (0.5 / n)
        _accumulate(loss_ref, jnp.broadcast_to(part, (8, 128)), pl.program_id(0) == 0)

    rows = lambda width: pl.BlockSpec((tm, width), lambda i: (i, 0))
    return pl.pallas_call(
        body, name=name, grid=(m // tm,),
        in_specs=[rows(k), pl.BlockSpec((None, k, n), lambda i: (0, 0, 0)), rows(n), rows(n)],
        out_specs=[rows(n), rows(n), pl.BlockSpec((8, 128), lambda i: (0, 0))],
        out_shape=[jax.ShapeDtypeStruct((m, n), F32), jax.ShapeDtypeStruct((m, n), BF16),
                   jax.ShapeDtypeStruct((8, 128), F32)],
        compiler_params=_params("arbitrary"),
    )(a, w, residual, target)


def _mm_nt(a, w, *, out_dtype, name, after=None, tm=1024, tn=1408, tk=1792):
    m, k = a.shape
    nsh, r, c = w.shape
    assert k == nsh * c and a.dtype == BF16 and w.dtype == BF16
    tm, tn, tk = _pick(m, tm, 8), _pick(r, tn), _pick(c, tk)
    nk, cpk = k // tk, c // tk

    def body(a_ref, w_ref, o_ref, acc):
        prod = lax.dot_general(a_ref[...], w_ref[...], (((1,), (1,)), ((), ())), preferred_element_type=F32)
        if nk == 1:
            o_ref[...] = prod.astype(out_dtype)
        else:
            kk = pl.program_id(2)

            @pl.when(kk == 0)
            def _():
                acc[...] = prod

            @pl.when(kk > 0)
            def _():
                acc[...] += prod

            @pl.when(kk == nk - 1)
            def _():
                o_ref[...] = acc[...].astype(out_dtype)

    body, more_specs, more_args = _ordered(body, 2, after)
    return pl.pallas_call(
        body, name=name, grid=(m // tm, r // tn, nk),
        in_specs=[pl.BlockSpec((tm, tk), lambda i, j, kk: (i, kk)),
                  pl.BlockSpec((None, tn, tk), lambda i, j, kk: (kk // cpk, j, kk % cpk))] + more_specs,
        out_specs=pl.BlockSpec((tm, tn), lambda i, j, kk: (i, j)),
        out_shape=jax.ShapeDtypeStruct((m, r), out_dtype),
        scratch_shapes=[pltpu.VMEM((tm, tn) if nk > 1 else (8, 128), F32)],
        compiler_params=_params("parallel", "parallel", "arbitrary"),
    )(a, w, *more_args)


NORM_BWD_ROWS = 256


def _mm_nt_rmsnorm_bwd(a, w, x, g, dres, *, name, want_bf16, silu=False, after=None, tm=1024, tk=1792):
    m, k = a.shape
    nsh, r, c = w.shape
    assert k == nsh * c and a.dtype == BF16 and w.dtype == BF16 and x.shape == (m, r)
    tm, tk = _pick(m, tm, 8), _pick(c, tk)
    nk, cpk = k // tk, c // tk
    rows = _pick(tm, NORM_BWD_ROWS, 8)
    n_in = 4 if dres is None else 5

    def body(a_ref, w_ref, x_ref, g_ref, *rest):
        dres_ref = None if dres is None else rest[0]
        outs, acc = rest[n_in - 4:-1], rest[-1]
        dx_ref, dg_ref = outs[0], outs[-1]
        kk = pl.program_id(1)
        prod = lax.dot_general(a_ref[...], w_ref[...], (((1,), (1,)), ((), ())), preferred_element_type=F32)

        @pl.when(kk == 0)
        def _():
            acc[...] = prod

        @pl.when(kk > 0)
        def _():
            acc[...] += prod

        @pl.when(kk == nk - 1)
        def _():
            dg = jnp.zeros((1, r), F32)
            for r0 in range(0, tm, rows):
                part = slice(r0, r0 + rows)
                xv, dyv = x_ref[part, :], acc[part, :]
                inv = lax.rsqrt(jnp.mean(xv * xv, axis=-1, keepdims=True) + RMS_EPS)
                if silu:
                    y = xv * inv * g_ref[...]
                    sg = _sigmoid(y)
                    dyv = dyv * sg * (1.0 + y * (1.0 - sg))
                gy = dyv * g_ref[...]
                dx = inv * gy - xv * (inv * inv * inv) * jnp.mean(xv * gy, axis=-1, keepdims=True)
                if dres is not None:
                    dx = dx + dres_ref[part, :]
                dx_ref[part, :] = dx
                if want_bf16:
                    outs[1][part, :] = dx.astype(BF16)
                dg = dg + jnp.sum(dyv * xv * inv, axis=0, keepdims=True)
            _accumulate(dg_ref, dg, pl.program_id(0) == 0)

    whole = lambda: pl.BlockSpec((tm, r), lambda i, kk: (i, 0))
    vec = pl.BlockSpec((1, r), lambda i, kk: (0, 0))
    out_shape, out_specs = [jax.ShapeDtypeStruct((m, r), F32)], [whole()]
    if want_bf16:
        out_shape.append(jax.ShapeDtypeStruct((m, r), BF16))
        out_specs.append(whole())
    out_shape.append(jax.ShapeDtypeStruct((1, r), F32))
    out_specs.append(vec)
    body, more_specs, more_args = _ordered(body, n_in, after)
    residual_specs, residual_args = ([], []) if dres is None else ([whole()], [dres])
    return pl.pallas_call(
        body, name=name, grid=(m // tm, nk),
        in_specs=[pl.BlockSpec((tm, tk), lambda i, kk: (i, kk)),
                  pl.BlockSpec((None, r, tk), lambda i, kk: (kk // cpk, 0, kk % cpk)), whole(), vec]
        + residual_specs + more_specs,
        out_specs=out_specs, out_shape=out_shape,
        scratch_shapes=[pltpu.VMEM((tm, r), F32)],
        compiler_params=_params("arbitrary", "arbitrary"),
    )(a, w, x, g, *residual_args, *more_args)


MM_TN_VMEM_BYTES = 44 * 2 ** 20


def _mm_tn(a, b, *, n_shards, name, tm=1408, tn=1408):
    t, m = a.shape
    t2, n = b.shape
    assert t == t2 and a.dtype == BF16 and b.dtype == BF16
    c = n // n_shards
    tm, tn = _pick(m, tm), _pick(c, tn)
    if m // tm == 1 and n // tn == 1 and tn % (2 * LANES) == 0:
        tn //= 2
    fixed = 2 * tm * tn * 6
    if 4 * t * (tm + tn) + fixed <= MM_TN_VMEM_BYTES:
        tk = t
    else:
        tk = _pick(t, (MM_TN_VMEM_BYTES - fixed - 4 * tm * tn) // (4 * (tm + tn)), 8)
    nk, cpn = t // tk, c // tn

    def body(a_ref, b_ref, o_ref, ob_ref, acc):
        kk = pl.program_id(2)
        prod = lax.dot_general(a_ref[...], b_ref[...], (((0,), (0,)), ((), ())), preferred_element_type=F32)

        def finish(total):
            o_ref[...] = total
            ob_ref[...] = total.astype(BF16)

        if nk == 1:
            finish(prod)
        else:
            @pl.when(kk == 0)
            def _():
                acc[...] = prod

            @pl.when(kk > 0)
            def _():
                acc[...] += prod

            @pl.when(kk == nk - 1)
            def _():
                finish(acc[...])

    out_spec = pl.BlockSpec((None, tm, tn), lambda i, j, kk: (j // cpn, i, j % cpn))
    return pl.pallas_call(
        body, name=name, grid=(m // tm, n // tn, nk),
        in_specs=[pl.BlockSpec((tk, tm), lambda i, j, kk: (kk, i)),
                  pl.BlockSpec((tk, tn), lambda i, j, kk: (kk, j))],
        out_specs=[out_spec, out_spec],
        out_shape=[jax.ShapeDtypeStruct((n_shards, m, c), F32), jax.ShapeDtypeStruct((n_shards, m, c), BF16)],
        scratch_shapes=[pltpu.VMEM((tm, tn) if nk > 1 else (8, 128), F32)],
        compiler_params=_params("parallel", "parallel", "arbitrary"),
    )(a, b)


def _row_spec(tr, width, col=0):
    return pl.BlockSpec((tr, width), lambda i, col=col: (i, col))


def _vec_spec(width, col=0):
    return pl.BlockSpec((1, width), lambda i, col=col: (0, col))


def _accumulate(ref, value, first):
    @pl.when(first)
    def _():
        ref[...] = value

    @pl.when(jnp.logical_not(first))
    def _():
        ref[...] += value


def _rmsnorm_fwd(x, g, *, name, tr=512):
    t, d = x.shape
    tr = _pick(t, tr, 8)

    def body(x_ref, g_ref, o_ref):
        xv = x_ref[...]
        r = lax.rsqrt(jnp.mean(xv * xv, axis=-1, keepdims=True) + RMS_EPS)
        o_ref[...] = (xv * r * g_ref[...]).astype(BF16)

    return pl.pallas_call(
        body, name=name, grid=(t // tr,),
        in_specs=[_row_spec(tr, d), _vec_spec(d)], out_specs=_row_spec(tr, d),
        out_shape=jax.ShapeDtypeStruct((t, d), BF16), compiler_params=_params("parallel"),
    )(x, g)


CONV_ROWS = 16


def _seq_specs(dims, ts, width, halo, col, *, nxt=False):
    nst, per = dims.seq // ts, ts // halo
    last = dims.tokens // halo - 1
    cur = pl.BlockSpec((ts, width), lambda b, i: (b * nst + i, col))
    if nxt:
        edge = pl.BlockSpec((halo, width), lambda b, i: (jnp.minimum((b * nst + i + 1) * per, last), col))
    else:
        edge = pl.BlockSpec((halo, width), lambda b, i: (jnp.maximum((b * nst + i) * per - 1, 0), col))
    return cur, edge


SUBLANES = 8


def _shifted_copies(buf, shifted):
    rows = shifted.shape[1]
    for s in range(1, SUBLANES):
        shifted[s - 1] = buf[pl.ds(s, rows), :]


def _window(buf, shifted, start, size):
    a, s = divmod(start, SUBLANES)
    src = buf if s == 0 else shifted.at[s - 1]
    return src[pl.ds(SUBLANES * a, size), :]


def _conv_branch_fwd(z, w, b, g, dims, *, name, ts=128):
    t, c, kw = z.shape[0], dims.d_model, dims.conv_width
    base = CONV_HALO - (kw - 1)

    def body(av_ref, hv_ref, ag_ref, hg_ref, w_ref, b_ref, g_ref, a1_ref, a3_ref, buf, shifted):
        i = pl.program_id(1)
        buf[CONV_HALO:, :] = av_ref[...].astype(F32) * _sigmoid(ag_ref[...].astype(F32))
        buf[0:CONV_HALO, :] = jnp.where(i > 0, hv_ref[...].astype(F32) * _sigmoid(hg_ref[...].astype(F32)), 0.0)
        _shifted_copies(buf, shifted)
        for r0 in range(0, ts, CONV_ROWS):
            acc = jnp.broadcast_to(b_ref[...], (CONV_ROWS, c))
            for k in range(kw):
                acc = acc + w_ref[k:k + 1, :] * _window(buf, shifted, r0 + base + k, CONV_ROWS)
            a1_ref[r0:r0 + CONV_ROWS, :] = acc
            a2 = acc * lax.rsqrt(jnp.mean(acc * acc, axis=-1, keepdims=True) + RMS_EPS) * g_ref[...]
            a3_ref[r0:r0 + CONV_ROWS, :] = (a2 * _sigmoid(a2)).astype(BF16)

    vec = pl.BlockSpec((1, c), lambda b, i: (0, 0))
    out = pl.BlockSpec((ts, c), lambda b, i: (b * (dims.seq // ts) + i, 0))
    return pl.pallas_call(
        body, name=name, grid=(dims.batch_local, dims.seq // ts),
        in_specs=[*_seq_specs(dims, ts, c, CONV_HALO, 0), *_seq_specs(dims, ts, c, CONV_HALO, 1),
                  pl.BlockSpec((CONV_HALO, c), lambda b, i: (0, 0)), vec, vec],
        out_specs=[out, out],
        out_shape=[jax.ShapeDtypeStruct((t, c), F32), jax.ShapeDtypeStruct((t, c), BF16)],
        scratch_shapes=[pltpu.VMEM((CONV_HALO + ts, c), F32),
                        pltpu.VMEM((SUBLANES - 1, CONV_HALO + ts - SUBLANES, c), F32)],
        compiler_params=_params("parallel", "parallel"),
    )(z, z, z, z, w, b, g)


def _conv_branch_bwd(da1, z, w, rest_of_dz, dims, *, name, ts=128):
    t, c, kw = z.shape[0], dims.d_model, dims.conv_width
    nst = dims.seq // ts
    base = CONV_HALO - (kw - 1)
    n_rest = len(rest_of_dz)
    total = 2 * c + sum(r.shape[1] for r in rest_of_dz)

    def body(d_ref, dn_ref, av_ref, hv_ref, ag_ref, hg_ref, w_ref, *more):
        rest_refs = more[:n_rest]
        dz_ref, dw_ref, db_ref, abuf, dbuf, ashift, dshift = more[n_rest:]
        col = 2 * c
        for r in rest_refs:
            dz_ref[:, col:col + r.shape[1]] = r[...]
            col += r.shape[1]
        i = pl.program_id(1)
        first = jnp.logical_and(pl.program_id(0) == 0, i == 0)
        abuf[CONV_HALO:, :] = av_ref[...].astype(F32) * _sigmoid(ag_ref[...].astype(F32))
        abuf[0:CONV_HALO, :] = jnp.where(i > 0, hv_ref[...].astype(F32) * _sigmoid(hg_ref[...].astype(F32)), 0.0)
        d1 = d_ref[...]
        dbuf[0:ts, :] = d1
        dbuf[ts:, :] = jnp.where(i < nst - 1, dn_ref[...], 0.0)
        _shifted_copies(abuf, ashift)
        _shifted_copies(dbuf, dshift)

        @pl.when(first)
        def _():
            dw_ref[...] = jnp.zeros_like(dw_ref)
            db_ref[...] = jnp.zeros_like(db_ref)

        db_ref[...] += jnp.sum(d1, axis=0, keepdims=True)
        for k in range(kw):
            dw_ref[k:k + 1, :] += jnp.sum(d1 * _window(abuf, ashift, base + k, ts), axis=0, keepdims=True)
        for r0 in range(0, ts, CONV_ROWS):
            acc = jnp.zeros((CONV_ROWS, c), F32)
            for k in range(kw):
                acc = acc + w_ref[k:k + 1, :] * _window(dbuf, dshift, r0 + (kw - 1) - k, CONV_ROWS)
            av = av_ref[r0:r0 + CONV_ROWS, :].astype(F32)
            sg = _sigmoid(ag_ref[r0:r0 + CONV_ROWS, :].astype(F32))
            dz_ref[r0:r0 + CONV_ROWS, 0:c] = (acc * sg).astype(BF16)
            dz_ref[r0:r0 + CONV_ROWS, c:2 * c] = (acc * av * sg * (1.0 - sg)).astype(BF16)

    cur, nxt = _seq_specs(dims, ts, c, CONV_HALO, 0, nxt=True)
    return pl.pallas_call(
        body, name=name, grid=(dims.batch_local, nst),
        in_specs=[cur, nxt, *_seq_specs(dims, ts, c, CONV_HALO, 0), *_seq_specs(dims, ts, c, CONV_HALO, 1),
                  pl.BlockSpec((CONV_HALO, c), lambda b, i: (0, 0))]
        + [pl.BlockSpec((ts, r.shape[1]), lambda b, i: (b * nst + i, 0)) for r in rest_of_dz],
        out_specs=[pl.BlockSpec((ts, total), lambda b, i: (b * nst + i, 0)),
                   pl.BlockSpec((CONV_HALO, c), lambda b, i: (0, 0)), pl.BlockSpec((1, c), lambda b, i: (0, 0))],
        out_shape=[jax.ShapeDtypeStruct((t, total), BF16), jax.ShapeDtypeStruct((CONV_HALO, c), F32),
                   jax.ShapeDtypeStruct((1, c), F32)],
        scratch_shapes=[pltpu.VMEM((CONV_HALO + ts, c), F32)] * 2
        + [pltpu.VMEM((SUBLANES - 1, CONV_HALO + ts - SUBLANES, c), F32)] * 2,
        compiler_params=_params("arbitrary", "arbitrary"),
    )(da1, da1, z, z, z, z, w, *rest_of_dz)


FFN_ROWS = 16
FFN_COLS = 256


def _ffn_chunks(ts, f):
    cw = _pick(f, FFN_COLS)
    return [(r0, c0, cw) for r0 in range(0, ts, FFN_ROWS) for c0 in range(0, f, cw)]


def _tap_sources(buf, moved, offsets, rows):
    taps, used = [], 0
    for off in offsets:
        if off % SUBLANES:
            moved[used] = buf[pl.ds(off, rows), :]
            taps.append((moved.at[used], 0))
            used += 1
        else:
            taps.append((buf, off))
    return taps


def _moved_copies(offsets):
    return sum(1 for off in offsets if off % SUBLANES)


def _taps_sum(taps, w_ref, init, r0, cols):
    for k, (src, off) in enumerate(taps):
        init = init + w_ref[k:k + 1, cols] * src[pl.ds(off + r0, init.shape[0]), cols]
    return init


def _ffn_bwd(dact, up, w, b, dims, *, name, ts=128):
    t, f, kw = up.shape[0], dims.d_ff, dims.ffn_conv_width
    nst = dims.seq // ts
    fwd_offsets = [FFN_HALO - (kw - 1) + k for k in range(kw)]
    bwd_offsets = [(kw - 1) - k for k in range(kw)]
    dact_halo = 2 * FFN_HALO

    def body(d_ref, dn_ref, up_ref, hp_ref, hn_ref, w_ref, b_ref, o_ref, dw_ref, db_ref, buf, moved, dbuf, dmoved):
        i = pl.program_id(1)
        first = jnp.logical_and(pl.program_id(0) == 0, i == 0)
        more = i < nst - 1
        buf[0:FFN_HALO, :] = jnp.where(i > 0, hp_ref[...], 0.0)
        buf[FFN_HALO:FFN_HALO + ts, :] = up_ref[...]
        buf[FFN_HALO + ts:, :] = hn_ref[...]
        taps = _tap_sources(buf, moved, fwd_offsets, ts + FFN_HALO)

        def du_chunk(r0, rows, c0, cw, d):
            vcols, gcols = slice(c0, c0 + cw), slice(f + c0, f + c0 + cw)
            uv = _taps_sum(taps, w_ref, jnp.broadcast_to(b_ref[:, vcols], (rows, cw)), r0, vcols)
            ug = _taps_sum(taps, w_ref, jnp.broadcast_to(b_ref[:, gcols], (rows, cw)), r0, gcols)
            sg = _sigmoid(ug)
            dbuf[r0:r0 + rows, vcols] = d * ug * sg
            dbuf[r0:r0 + rows, gcols] = d * uv * sg * (1.0 + ug * (1.0 - sg))

        for r0, c0, cw in _ffn_chunks(ts, f):
            du_chunk(r0, FFN_ROWS, c0, cw, d_ref[r0:r0 + FFN_ROWS, c0:c0 + cw].astype(F32))
        for _, c0, cw in _ffn_chunks(FFN_ROWS, f):
            d_next = dn_ref[:, c0:c0 + cw].astype(F32)[0:FFN_HALO]
            du_chunk(ts, FFN_HALO, c0, cw, jnp.where(more, d_next, 0.0))

        @pl.when(first)
        def _():
            dw_ref[...] = jnp.zeros_like(dw_ref)
            db_ref[...] = jnp.zeros_like(db_ref)

        du = dbuf[0:ts, :]
        db_ref[...] += jnp.sum(du, axis=0, keepdims=True)
        for k, (src, off) in enumerate(taps):
            dw_ref[k:k + 1, :] += jnp.sum(du * src[pl.ds(off, ts), :], axis=0, keepdims=True)

        dtaps = _tap_sources(dbuf, dmoved, bwd_offsets, ts)
        for r0, c0, cw in _ffn_chunks(ts, 2 * f):
            cols = slice(c0, c0 + cw)
            o_ref[r0:r0 + FFN_ROWS, cols] = _taps_sum(dtaps, w_ref, jnp.zeros((FFN_ROWS, cw), F32), r0, cols).astype(BF16)

    up_cur, up_prev = _seq_specs(dims, ts, 2 * f, FFN_HALO, 0)
    _, up_next = _seq_specs(dims, ts, 2 * f, FFN_HALO, 0, nxt=True)
    d_cur, d_next = _seq_specs(dims, ts, f, dact_halo, 0, nxt=True)
    full = lambda rows: pl.BlockSpec((rows, 2 * f), lambda b_, i: (0, 0))
    return pl.pallas_call(
        body, name=name, grid=(dims.batch_local, nst),
        in_specs=[d_cur, d_next, up_cur, up_prev, up_next, full(FFN_HALO), full(1)],
        out_specs=[pl.BlockSpec((ts, 2 * f), lambda b_, i: (b_ * nst + i, 0)), full(FFN_HALO), full(1)],
        out_shape=[jax.ShapeDtypeStruct((t, 2 * f), BF16), jax.ShapeDtypeStruct((FFN_HALO, 2 * f), F32),
                   jax.ShapeDtypeStruct((1, 2 * f), F32)],
        scratch_shapes=[pltpu.VMEM((ts + 2 * FFN_HALO, 2 * f), F32),
                        pltpu.VMEM((_moved_copies(fwd_offsets), ts + FFN_HALO, 2 * f), F32),
                        pltpu.VMEM((ts + FFN_HALO, 2 * f), F32),
                        pltpu.VMEM((_moved_copies(bwd_offsets), ts, 2 * f), F32)],
        compiler_params=_params("arbitrary", "arbitrary"),
    )(dact, dact, up, up, up, w, b)


def _ffn_act_fwd(up, w, b, dims, *, name, ts=128):
    t, f, kw = up.shape[0], dims.d_ff, dims.ffn_conv_width
    offsets = [FFN_HALO - (kw - 1) + k for k in range(kw)]

    def body(up_ref, h_ref, w_ref, b_ref, o_ref, buf, moved):
        buf[FFN_HALO:, :] = up_ref[...]
        buf[0:FFN_HALO, :] = jnp.where(pl.program_id(1) > 0, h_ref[...], 0.0)
        taps = _tap_sources(buf, moved, offsets, ts)
        for r0, c0, cw in _ffn_chunks(ts, f):
            vcols, gcols = slice(c0, c0 + cw), slice(f + c0, f + c0 + cw)
            uv = _taps_sum(taps, w_ref, jnp.broadcast_to(b_ref[:, vcols], (FFN_ROWS, cw)), r0, vcols)
            ug = _taps_sum(taps, w_ref, jnp.broadcast_to(b_ref[:, gcols], (FFN_ROWS, cw)), r0, gcols)
            o_ref[r0:r0 + FFN_ROWS, vcols] = (ug * _sigmoid(ug) * uv).astype(BF16)

    full = lambda rows: pl.BlockSpec((rows, 2 * f), lambda b_, i: (0, 0))
    return pl.pallas_call(
        body, name=name, grid=(dims.batch_local, dims.seq // ts),
        in_specs=[*_seq_specs(dims, ts, 2 * f, FFN_HALO, 0), full(FFN_HALO), full(1)],
        out_specs=pl.BlockSpec((ts, f), lambda b_, i: (b_ * (dims.seq // ts) + i, 0)),
        out_shape=jax.ShapeDtypeStruct((t, f), BF16),
        scratch_shapes=[pltpu.VMEM((FFN_HALO + ts, 2 * f), F32), pltpu.VMEM((_moved_copies(offsets), ts, 2 * f), F32)],
        compiler_params=_params("parallel", "parallel"),
    )(up, up, w, b)


def _dot_nt(a, b):
    return lax.dot_general(a, b, (((1,), (1,)), ((), ())), preferred_element_type=F32)


def _dot_tn(a, b):
    return lax.dot_general(a, b, (((0,), (0,)), ((), ())), preferred_element_type=F32)


LANES = 128
MASK_BIAS = 1e30
RESIDUE_DILATIONS = tuple(d for d in DILATIONS if d > 1)


def _rows_to_residues(value, out_ref, scr, d):
    rows, width = value.shape
    for c in range(width // LANES):
        cols = slice(LANES * c, LANES * (c + 1))
        scr[c] = value[:, cols]
        for r in range(d):
            out_ref[r, :, cols] = scr[c, pl.ds(r, rows // d, stride=d), :].astype(out_ref.dtype)


def _residues_to_rows(in_ref, scr, d):
    _, n, width = in_ref.shape
    slabs = []
    for c in range(width // LANES):
        cols = slice(LANES * c, LANES * (c + 1))
        for r in range(d):
            scr[c, pl.ds(r, n, stride=d), :] = in_ref[r, :, cols].astype(F32)
        slabs.append(scr[c])
    return slabs[0] if len(slabs) == 1 else jnp.concatenate(slabs, axis=1)


def _residue_shape(dims, d, width, dtype):
    return jax.ShapeDtypeStruct((dims.batch_local, d, dims.seq // d, width), dtype)


def _residue_spec(dims, d, tr, width):
    tiles = dims.seq // tr
    return pl.BlockSpec((None, d, tr // d, width), lambda i: (i // tiles, 0, i % tiles, 0))


def _head_sum_matrix(dims):
    a = dims.n_heads * dims.head_dim
    head = jnp.arange(a, dtype=jnp.int32) // dims.head_dim
    return (head[:, None] == jnp.arange(LANES, dtype=jnp.int32)[None, :]).astype(BF16)


def _two_pass_dot(v, m):
    hi = v.astype(BF16)
    lo = (v - hi.astype(F32)).astype(BF16)
    return jnp.dot(hi, m, preferred_element_type=F32) + jnp.dot(lo, m, preferred_element_type=F32)


def _residue_permutations(tr):
    out = []
    for d in RESIDUE_DILATIONS:
        dst = jnp.arange(tr, dtype=jnp.int32)
        src = d * (dst % (tr // d)) + dst // (tr // d)
        out.append((src[:, None] == jnp.arange(tr, dtype=jnp.int32)[None, :]).astype(BF16))
    return out


def _bf16_rows_to_residues(value, out_ref, perm_ref, d):
    n = value.shape[0] // d
    moved = jnp.dot(perm_ref[...], value, preferred_element_type=F32).astype(out_ref.dtype)
    for r in range(d):
        out_ref[r] = moved[r * n:(r + 1) * n]


def _bf16_residues_to_rows(in_ref, back_ref):
    d = in_ref.shape[0]
    stacked = jnp.concatenate([in_ref[r] for r in range(d)], axis=0)
    return jnp.dot(back_ref[...], stacked, preferred_element_type=F32)


def _qkv_layouts_fwd(z, gq, gk, head_ones, dims, *, name, tr=256):
    t = z.shape[0]
    a = dims.n_heads * dims.head_dim
    q_scale = dims.head_dim ** -0.5
    nres = len(RESIDUE_DILATIONS)

    def body(q_ref, k_ref, v_ref, gq_ref, gk_ref, sum_ref, spread_ref, *rest):
        perm_refs, outs = rest[:nres], rest[nres:]
        qv, kv = q_ref[...].astype(F32), k_ref[...].astype(F32)
        mean = lambda val: _two_pass_dot(_two_pass_dot(val, sum_ref[...]), spread_ref[...]) * (1.0 / dims.head_dim)
        rq = lax.rsqrt(mean(qv * qv) + RMS_EPS)
        rk = lax.rsqrt(mean(kv * kv) + RMS_EPS)
        values = ((qv * rq * gq_ref[...] * q_scale).astype(BF16), (kv * rk * gk_ref[...]).astype(BF16), v_ref[...])
        for j, val in enumerate(values):
            outs[j][...] = val
            for g, d in enumerate(RESIDUE_DILATIONS):
                _bf16_rows_to_residues(val, outs[3 * (g + 1) + j], perm_refs[g], d)

    out_specs = [_row_spec(tr, a)] * 3
    out_shape = [jax.ShapeDtypeStruct((t, a), BF16)] * 3
    for d in RESIDUE_DILATIONS:
        out_specs += [_residue_spec(dims, d, tr, a)] * 3
        out_shape += [_residue_shape(dims, d, a, BF16)] * 3
    outs = pl.pallas_call(
        body, name=name, grid=(t // tr,),
        in_specs=[_row_spec(tr, a, 2), _row_spec(tr, a, 3), _row_spec(tr, a, 4), _vec_spec(a), _vec_spec(a),
                  pl.BlockSpec((a, LANES), lambda i: (0, 0)), pl.BlockSpec((LANES, a), lambda i: (0, 0))]
        + [pl.BlockSpec((tr, tr), lambda i: (0, 0))] * nres,
        out_specs=out_specs, out_shape=out_shape,
        compiler_params=_params("parallel"),
    )(z, z, z, gq, gk, *head_ones, *_residue_permutations(tr))
    return {d: tuple(outs[3 * g:3 * g + 3]) for g, d in enumerate((1,) + RESIDUE_DILATIONS)}


ATTN_RESIDUES_PER_STEP = 4
ATTN_RESIDUES_PER_STEP_WINDOWED = 2


def _attn_groups(dims, dil):
    return (dims.batch_local, dil) if dil > 1 else (1, dims.batch_local)


def _attn_array(x, dims, dil):
    return x if dil > 1 else x.reshape(1, dims.batch_local, dims.seq, x.shape[-1])


def _attn_residues(dims, dil):
    one_block = dims.seq // dil == ATTN_BLOCK
    return math.gcd(_attn_groups(dims, dil)[1], ATTN_RESIDUES_PER_STEP if one_block else ATTN_RESIDUES_PER_STEP_WINDOWED)


def _per_residue(body, rs):
    if rs == 1:
        return body

    def stepped(*refs):
        for r in range(rs):
            body(*[ref.at[r] for ref in refs])

    return stepped


def _attn_specs(dims, dil, width):
    blk = ATTN_BLOCK
    nb = dims.seq // dil // blk
    rs = _attn_residues(dims, dil)
    lead, groups = _attn_groups(dims, dil)
    if rs > 1 and nb == 1:
        grid = (lead, groups // rs)
        at = lambda f: pl.BlockSpec((None, rs, blk, width), lambda b, r: (b, r, 0, 0))
    elif rs > 1:
        grid = (lead, groups // rs, nb)
        at = lambda f: pl.BlockSpec((None, rs, blk, width), lambda b, r, i: (b, r, f(i), 0))
    else:
        grid = (lead, groups, nb)
        at = lambda f: pl.BlockSpec((None, None, blk, width), lambda b, r, i: (b, r, f(i), 0))
    return grid, at(lambda i: i), at(lambda i: jnp.maximum(i - 1, 0)), at(lambda i: jnp.minimum(i + 1, nb - 1))


def _head_slopes(n_heads):
    h = lax.broadcasted_iota(jnp.int32, (n_heads, 1, 1), 0).astype(F32)
    return jnp.exp((h + 1.0) * (-8.0 / n_heads * math.log(2.0)))


def _pair_masks(hd):
    low = lax.broadcasted_iota(jnp.int32, (1, 2 * hd), 1) < hd
    return low, jnp.logical_not(low)


def _attn_fwd(q, k, v, dims, dil, *, name):
    a = dims.n_heads * dims.head_dim
    heads, hd, blk = dims.n_heads, dims.head_dim, ATTN_BLOCK
    assert 2 * hd == LANES and heads % 2 == 0 and heads <= LANES
    nb = dims.seq // dil // blk
    has_prev = nb > 1
    nkeys = 2 * blk if has_prev else blk
    grid, cur, prev, _ = _attn_specs(dims, dil, a)
    _, cur_stat, _, _ = _attn_specs(dims, dil, LANES)

    def body(*refs):
        if has_prev:
            q_ref, kc_ref, vc_ref, kp_ref, vp_ref, o_ref, lse_ref, s_scr, p_scr, k_st, v_st = refs
            k_st[0:blk, :], k_st[blk:, :] = kp_ref[...], kc_ref[...]
            v_st[0:blk, :], v_st[blk:, :] = vp_ref[...], vc_ref[...]
        else:
            q_ref, k_st, v_st, o_ref, lse_ref, s_scr, p_scr = refs
        low, high = _pair_masks(hd)

        for hp in range(heads // 2):
            sl = slice(LANES * hp, LANES * (hp + 1))
            q2 = q_ref[:, sl]
            kcat = k_st[:, sl]
            s_scr[2 * hp] = _dot_nt(jnp.where(low, q2, jnp.zeros_like(q2)), kcat)
            s_scr[2 * hp + 1] = _dot_nt(jnp.where(high, q2, jnp.zeros_like(q2)), kcat)

        iq = lax.broadcasted_iota(jnp.int32, (blk, nkeys), 0)
        jk = lax.broadcasted_iota(jnp.int32, (blk, nkeys), 1)
        if has_prev:
            steps = iq + blk - jk
            valid = (steps >= 0) & (steps <= blk) & ((jk >= blk) | (pl.program_id(len(grid) - 1) > 0))
        else:
            steps = iq - jk
            valid = steps >= 0
        bias = jnp.where(valid, steps.astype(F32) * (-float(dil)), -MASK_BIAS)
        s = s_scr[...] + _head_slopes(heads) * bias[None]
        m = jnp.max(s, axis=-1, keepdims=True)
        p = jnp.exp(s - m)
        l = jnp.sum(p, axis=-1, keepdims=True)
        p_scr[...] = p.astype(BF16)
        inv = 1.0 / l
        lse = m + jnp.log(l)

        lane = lax.broadcasted_iota(jnp.int32, (blk, LANES), 1)
        stat = jnp.zeros((blk, LANES), F32)
        for hp in range(heads // 2):
            sl = slice(LANES * hp, LANES * (hp + 1))
            vcat = v_st[:, sl]
            pv_a = jnp.dot(p_scr[2 * hp], vcat, preferred_element_type=F32) * inv[2 * hp]
            pv_b = jnp.dot(p_scr[2 * hp + 1], vcat, preferred_element_type=F32) * inv[2 * hp + 1]
            o_ref[:, sl] = jnp.where(low, pv_a, pv_b).astype(BF16)
            stat = jnp.where(lane == 2 * hp, lse[2 * hp], stat)
            stat = jnp.where(lane == 2 * hp + 1, lse[2 * hp + 1], stat)
        lse_ref[...] = stat

    q4, k4, v4 = (_attn_array(x, dims, dil) for x in (q, k, v))
    rs = _attn_residues(dims, dil)
    per_step = lambda shape: shape if rs == 1 else (rs,) + shape
    o, lse = pl.pallas_call(
        _per_residue(body, rs), name=name, grid=grid,
        in_specs=[cur, cur, cur] + ([prev, prev] if has_prev else []),
        out_specs=[cur, cur_stat],
        out_shape=[jax.ShapeDtypeStruct(q4.shape, BF16), jax.ShapeDtypeStruct(q4.shape[:-1] + (LANES,), F32)],
        scratch_shapes=[pltpu.VMEM(per_step((heads, blk, nkeys)), F32), pltpu.VMEM(per_step((heads, blk, nkeys)), BF16)]
        + ([pltpu.VMEM(per_step((nkeys, a)), BF16)] * 2 if has_prev else []),
        compiler_params=_params(*["parallel"] * len(grid)),
    )(q4, k4, v4, *([k4, v4] if has_prev else []))
    return o.reshape(q.shape), lse.reshape(q.shape[:-1] + (LANES,))


def _attn_combine(groups, head_spread, dims, *, name, tr=256):
    t = dims.tokens
    a = dims.n_heads * dims.head_dim
    dils = tuple(groups)

    nres = len(RESIDUE_DILATIONS)

    def body(*refs):
        ins = refs[:2 * len(dils)]
        x_ref = refs[2 * len(dils)]
        back_refs = dict(zip(RESIDUE_DILATIONS, refs[2 * len(dils) + 1:2 * len(dils) + 1 + nres]))
        o_ref = refs[2 * len(dils) + 1 + nres]
        lse_refs = refs[2 * len(dils) + 2 + nres:-1]
        scr_stat = refs[-1]
        outs, stats = [], []
        for g, d in enumerate(dils):
            if d == 1:
                outs.append(ins[2 * g][...].astype(F32))
                stats.append(ins[2 * g + 1][...])
            else:
                outs.append(_bf16_residues_to_rows(ins[2 * g], back_refs[d]))
                stats.append(_residues_to_rows(ins[2 * g + 1], scr_stat, d))
        top = functools.reduce(jnp.maximum, stats)
        weights = [jnp.exp(s - top) for s in stats]
        total = functools.reduce(jnp.add, weights)
        joint = top + jnp.log(total)
        inv = 1.0 / total
        acc = None
        for w, o in zip(weights, outs):
            term = _two_pass_dot(w * inv, x_ref[...]) * o
            acc = term if acc is None else acc + term
        o_ref[...] = acc.astype(BF16)
        for g, d in enumerate(dils):
            if d == 1:
                lse_refs[g][...] = joint
            else:
                _rows_to_residues(joint, lse_refs[g], scr_stat, d)

    in_specs, args, lse_specs, lse_shapes = [], [], [], []
    for d in dils:
        if d == 1:
            in_specs += [_row_spec(tr, a), _row_spec(tr, LANES)]
            lse_specs.append(_row_spec(tr, LANES))
            lse_shapes.append(jax.ShapeDtypeStruct((t, LANES), F32))
        else:
            in_specs += [_residue_spec(dims, d, tr, a), _residue_spec(dims, d, tr, LANES)]
            lse_specs.append(_residue_spec(dims, d, tr, LANES))
            lse_shapes.append(_residue_shape(dims, d, LANES, F32))
        args += list(groups[d])
    outs = pl.pallas_call(
        body, name=name, grid=(t // tr,),
        in_specs=in_specs + [pl.BlockSpec((LANES, a), lambda i: (0, 0))] + [pl.BlockSpec((tr, tr), lambda i: (0, 0))] * nres,
        out_specs=[_row_spec(tr, a)] + lse_specs,
        out_shape=[jax.ShapeDtypeStruct((t, a), BF16)] + lse_shapes,
        scratch_shapes=[pltpu.VMEM((1, tr, LANES), F32)],
        compiler_params=_params("parallel"),
    )(*args, head_spread, *[jnp.transpose(p) for p in _residue_permutations(tr)])
    return outs[0], dict(zip(dils, outs[1:]))


def _attn_bwd_prep(dy, w, o, head_sum, dims, *, name, after=None, tr=256):
    t, a = o.shape
    nres = len(RESIDUE_DILATIONS)

    def body(dy_ref, w_ref, o_ref, e_ref, *rest):
        perm_refs, outs, scr_stat = rest[:nres], rest[nres:-1], rest[-1]
        do = _dot_nt(dy_ref[...], w_ref[...]).astype(BF16)
        outs[0][...] = do
        delta = _two_pass_dot(do.astype(F32) * o_ref[...].astype(F32), e_ref[...])
        outs[1][...] = delta
        for g, d in enumerate(RESIDUE_DILATIONS):
            _bf16_rows_to_residues(do, outs[2 + 2 * g], perm_refs[g], d)
            _rows_to_residues(delta, outs[3 + 2 * g], scr_stat, d)

    out_specs = [_row_spec(tr, a), _row_spec(tr, LANES)]
    out_shape = [jax.ShapeDtypeStruct((t, a), BF16), jax.ShapeDtypeStruct((t, LANES), F32)]
    for d in RESIDUE_DILATIONS:
        out_specs += [_residue_spec(dims, d, tr, a), _residue_spec(dims, d, tr, LANES)]
        out_shape += [_residue_shape(dims, d, a, BF16), _residue_shape(dims, d, LANES, F32)]
    n_in = 4 + nres
    body, more_specs, more_args = _ordered(body, n_in, after)
    outs = pl.pallas_call(
        body, name=name, grid=(t // tr,),
        in_specs=[_row_spec(tr, dy.shape[1]), pl.BlockSpec((None,) + w.shape[1:], lambda i: (0, 0, 0)), _row_spec(tr, a),
                  pl.BlockSpec((a, LANES), lambda i: (0, 0))] + [pl.BlockSpec((tr, tr), lambda i: (0, 0))] * nres + more_specs,
        out_specs=out_specs, out_shape=out_shape,
        scratch_shapes=[pltpu.VMEM((1, tr, LANES), F32)],
        compiler_params=_params("parallel"),
    )(dy, w, o, head_sum, *_residue_permutations(tr), *more_args)
    dos, deltas = {1: outs[0]}, {1: outs[1]}
    for g, d in enumerate(RESIDUE_DILATIONS):
        dos[d], deltas[d] = outs[2 + 2 * g], outs[3 + 2 * g]
    return dos, deltas


def _attn_bwd(q, k, v, do, lse, delta, dims, dil, *, name):
    a = dims.n_heads * dims.head_dim
    heads, hd, blk = dims.n_heads, dims.head_dim, ATTN_BLOCK
    nb = dims.seq // dil // blk
    has_next = nb > 1
    nq = 2 * blk if has_next else blk
    grid, cur, _, nxt = _attn_specs(dims, dil, a)
    _, cur_stat, _, nxt_stat = _attn_specs(dims, dil, LANES)

    def body(*refs):
        k_ref, v_ref, q_ref, do_ref, lse_ref, dl_ref = refs[:6]
        if has_next:
            qn_ref, don_ref, lsen_ref, dln_ref = refs[6:10]
            dq_ref, dk_ref, dv_ref, q_st, do_st, s_scr, dp_scr, p_scr, ds_scr, carry = refs[10:]
        else:
            dq_ref, dk_ref, dv_ref, q_st, do_st, s_scr, dp_scr, p_scr, ds_scr = refs[6:]
        j = pl.program_id(len(grid) - 1)
        low, high = _pair_masks(hd)
        q_st[0:blk, :] = q_ref[...]
        do_st[0:blk, :] = do_ref[...]
        if has_next:
            q_st[blk:, :] = qn_ref[...]
            do_st[blk:, :] = don_ref[...]
            lse_all = jnp.concatenate([lse_ref[...], lsen_ref[...]], axis=0)
            dl_all = jnp.concatenate([dl_ref[...], dln_ref[...]], axis=0)
        else:
            lse_all, dl_all = lse_ref[...], dl_ref[...]
        lse_t, dl_t = jnp.transpose(lse_all), jnp.transpose(dl_all)
        lse3 = jnp.stack([lse_t[h:h + 1, :] for h in range(heads)])
        dl3 = jnp.stack([dl_t[h:h + 1, :] for h in range(heads)])

        def halves(x):
            return jnp.where(low, x, jnp.zeros_like(x)), jnp.where(high, x, jnp.zeros_like(x))

        for hp in range(heads // 2):
            sl = slice(LANES * hp, LANES * (hp + 1))
            k2, v2 = k_ref[:, sl], v_ref[:, sl]
            q_a, q_b = halves(q_st[:, sl])
            do_a, do_b = halves(do_st[:, sl])
            s_scr[2 * hp], s_scr[2 * hp + 1] = _dot_nt(k2, q_a), _dot_nt(k2, q_b)
            dp_scr[2 * hp], dp_scr[2 * hp + 1] = _dot_nt(v2, do_a), _dot_nt(v2, do_b)

        jk = lax.broadcasted_iota(jnp.int32, (blk, nq), 0)
        rq = lax.broadcasted_iota(jnp.int32, (blk, nq), 1)
        if has_next:
            iq = jnp.where(rq < blk, rq, rq - blk)
            steps = jnp.where(rq < blk, iq - jk, iq - jk + blk)
            valid = ((rq < blk) & (iq >= jk)) | ((rq >= blk) & (jk >= iq) & (j + 1 < nb))
        else:
            steps, valid = rq - jk, rq >= jk
        bias = jnp.where(valid, steps.astype(F32) * (-float(dil)), -MASK_BIAS)
        p = jnp.exp(s_scr[...] + _head_slopes(heads) * bias[None] - lse3)
        p_scr[...] = p.astype(BF16)
        ds_scr[...] = (p * (dp_scr[...] - dl3)).astype(BF16)

        if has_next:
            @pl.when(j == 0)
            def _():
                carry[...] = jnp.zeros_like(carry)

        for hp in range(heads // 2):
            sl = slice(LANES * hp, LANES * (hp + 1))
            k2 = k_ref[:, sl]
            q_a, q_b = halves(q_st[:, sl])
            do_a, do_b = halves(do_st[:, sl])
            ds_a, ds_b = ds_scr[2 * hp], ds_scr[2 * hp + 1]
            dk_ref[:, sl] = (jnp.dot(ds_a, q_a, preferred_element_type=F32)
                             + jnp.dot(ds_b, q_b, preferred_element_type=F32)).astype(BF16)
            dv_ref[:, sl] = (jnp.dot(p_scr[2 * hp], do_a, preferred_element_type=F32)
                             + jnp.dot(p_scr[2 * hp + 1], do_b, preferred_element_type=F32)).astype(BF16)
            dq2 = jnp.where(low, _dot_tn(ds_a, k2), _dot_tn(ds_b, k2))
            if has_next:
                dq_ref[:, sl] = (carry[:, sl] + dq2[:blk]).astype(BF16)
                carry[:, sl] = dq2[blk:]
            else:
                dq_ref[:, sl] = dq2.astype(BF16)

    args, in_specs = [_attn_array(x, dims, dil) for x in (k, v, q, do, lse, delta)], [cur] * 4 + [cur_stat] * 2
    if has_next:
        args += [args[2], args[3], args[4], args[5]]
        in_specs += [nxt] * 2 + [nxt_stat] * 2
    shape = jax.ShapeDtypeStruct(args[2].shape, BF16)
    rs = _attn_residues(dims, dil)
    per_step = lambda dims_: dims_ if rs == 1 else (rs,) + dims_
    scratch = ([pltpu.VMEM(per_step((nq, a)), BF16)] * 2 + [pltpu.VMEM(per_step((heads, blk, nq)), F32)] * 2
               + [pltpu.VMEM(per_step((heads, blk, nq)), BF16)] * 2)
    if has_next:
        scratch.append(pltpu.VMEM(per_step((blk, a)), F32))
    grads = pl.pallas_call(
        _per_residue(body, rs), name=name, grid=grid, in_specs=in_specs, out_specs=[cur] * 3, out_shape=[shape] * 3,
        scratch_shapes=scratch,
        compiler_params=_params(*["parallel"] * (len(grid) - 1), "arbitrary"),
    )(*args)
    return tuple(g.reshape(q.shape) for g in grads)


def _qkv_layouts_bwd(z, grads, gq, gk, head_ones, dims, *, name, tr=256):
    t = z.shape[0]
    a = dims.n_heads * dims.head_dim
    q_scale = dims.head_dim ** -0.5
    dils = tuple(grads)
    nres = len(RESIDUE_DILATIONS)

    def body(q_ref, k_ref, *rest):
        d_refs = rest[:3 * len(dils)]
        gq_ref, gk_ref, sum_ref, spread_ref = rest[3 * len(dils):3 * len(dils) + 4]
        back_refs = dict(zip(RESIDUE_DILATIONS, rest[3 * len(dils) + 4:3 * len(dils) + 4 + nres]))
        dz_ref, dgq_ref, dgk_ref = rest[3 * len(dils) + 4 + nres:]
        first = pl.program_id(0) == 0
        mean = lambda val: _two_pass_dot(_two_pass_dot(val, sum_ref[...]), spread_ref[...]) * (1.0 / dims.head_dim)

        def total(j):
            acc = None
            for g, d in enumerate(dils):
                ref = d_refs[3 * g + j]
                part = ref[...].astype(F32) if d == 1 else _bf16_residues_to_rows(ref, back_refs[d])
                acc = part if acc is None else acc + part
            return acc

        def norm_bwd(x_ref, dy, g_ref, scale, col, dg_ref):
            xv = x_ref[...].astype(F32)
            dy = dy * scale
            r = lax.rsqrt(mean(xv * xv) + RMS_EPS)
            gy = dy * g_ref[...]
            dx = r * gy - xv * (r * r * r) * mean(xv * gy)
            dz_ref[:, col * a:(col + 1) * a] = dx.astype(BF16)
            _accumulate(dg_ref, jnp.sum(dy * xv * r, axis=0, keepdims=True), first)

        norm_bwd(q_ref, total(0), gq_ref, q_scale, 0, dgq_ref)
        norm_bwd(k_ref, total(1), gk_ref, 1.0, 1, dgk_ref)
        dz_ref[:, 2 * a:3 * a] = total(2).astype(BF16)

    in_specs, args = [_row_spec(tr, a, 2), _row_spec(tr, a, 3)], [z, z]
    for d in dils:
        in_specs += [_row_spec(tr, a) if d == 1 else _residue_spec(dims, d, tr, a)] * 3
        args += list(grads[d])
    in_specs += [_vec_spec(a), _vec_spec(a), pl.BlockSpec((a, LANES), lambda i: (0, 0)),
                 pl.BlockSpec((LANES, a), lambda i: (0, 0))] + [pl.BlockSpec((tr, tr), lambda i: (0, 0))] * nres
    return pl.pallas_call(
        body, name=name, grid=(t // tr,), in_specs=in_specs,
        out_specs=[_row_spec(tr, 3 * a), _vec_spec(a), _vec_spec(a)],
        out_shape=[jax.ShapeDtypeStruct((t, 3 * a), BF16)] + [jax.ShapeDtypeStruct((1, a), F32)] * 2,
        compiler_params=_params("arbitrary"),
    )(*args, gq, gk, *head_ones, *[jnp.transpose(p) for p in _residue_permutations(tr)])


def _mix_fwd(a3, o, w_a, w_b, w_out, z, gate_b, x, g2, dims, *, name, tr=512):
    t, d = x.shape
    tr = _pick(t, tr, 8)
    first_gate_col = z.shape[1] // d - 2

    def body(a_ref, o_ref, wa_ref, wb_ref, wo_ref, ga_ref, gb_ref, ba_ref, bb_ref, x_ref, g2_ref,
             ya_ref, yb_ref, mix_ref, x1_ref, h2_ref):
        ya = jnp.dot(a_ref[...], wa_ref[...], preferred_element_type=F32)
        yb = jnp.dot(o_ref[...], wb_ref[...], preferred_element_type=F32)
        ya_ref[...] = ya
        yb_ref[...] = yb
        g_a = _sigmoid(ga_ref[...].astype(F32) + ba_ref[...])
        g_b = _sigmoid(gb_ref[...].astype(F32) + bb_ref[...])
        mixed = (g_a * ya + g_b * yb).astype(BF16)
        mix_ref[...] = mixed
        x1 = x_ref[...] + jnp.dot(mixed, wo_ref[...], preferred_element_type=F32)
        x1_ref[...] = x1
        h2_ref[...] = (x1 * lax.rsqrt(jnp.mean(x1 * x1, axis=-1, keepdims=True) + RMS_EPS) * g2_ref[...]).astype(BF16)

    weight = pl.BlockSpec((None, d, d), lambda i: (0, 0, 0))
    rows = _row_spec(tr, d)
    return pl.pallas_call(
        body, name=name, grid=(t // tr,),
        in_specs=[rows, rows, weight, weight, weight, _row_spec(tr, d, first_gate_col),
                  _row_spec(tr, d, first_gate_col + 1), _vec_spec(d, 0), _vec_spec(d, 1), rows, _vec_spec(d)],
        out_specs=[rows] * 5,
        out_shape=[jax.ShapeDtypeStruct((t, d), dt) for dt in (F32, F32, BF16, F32, BF16)],
        compiler_params=_params("parallel"),
    )(a3, o, w_a, w_b, w_out, z, z, gate_b, gate_b, x, g2)


def _mix_bwd(dx, w, ya, yb, z, gate_b, dims, *, name, after=None, tr=512):
    t, d = ya.shape
    tr = _pick(t, tr, 8)
    first_gate_col = z.shape[1] // d - 2

    def body(dx_ref, w_ref, ya_ref, yb_ref, ga_ref, gb_ref, ba_ref, bb_ref, dya_ref, dyb_ref, dz_ref, db_ref):
        dm = _dot_nt(dx_ref[...], w_ref[...])
        g_a = _sigmoid(ga_ref[...].astype(F32) + ba_ref[...])
        g_b = _sigmoid(gb_ref[...].astype(F32) + bb_ref[...])
        dya_ref[...] = (dm * g_a).astype(BF16)
        dyb_ref[...] = (dm * g_b).astype(BF16)
        dl_a = dm * ya_ref[...] * g_a * (1.0 - g_a)
        dl_b = dm * yb_ref[...] * g_b * (1.0 - g_b)
        dz_ref[:, 0:d] = dl_a.astype(BF16)
        dz_ref[:, d:2 * d] = dl_b.astype(BF16)
        first = pl.program_id(0) == 0
        sums = jnp.concatenate([jnp.sum(dl_a, axis=0, keepdims=True), jnp.sum(dl_b, axis=0, keepdims=True)], axis=1)
        _accumulate(db_ref, sums, first)

    body, more_specs, more_args = _ordered(body, 8, after)
    return pl.pallas_call(
        body, name=name, grid=(t // tr,),
        in_specs=[_row_spec(tr, d), pl.BlockSpec((None, d, d), lambda i: (0, 0, 0)), _row_spec(tr, d), _row_spec(tr, d),
                  _row_spec(tr, d, first_gate_col), _row_spec(tr, d, first_gate_col + 1), _vec_spec(d, 0),
                  _vec_spec(d, 1)] + more_specs,
        out_specs=[_row_spec(tr, d), _row_spec(tr, d), _row_spec(tr, 2 * d), _vec_spec(2 * d)],
        out_shape=[jax.ShapeDtypeStruct((t, d), BF16)] * 2 + [jax.ShapeDtypeStruct((t, 2 * d), BF16),
                                                              jax.ShapeDtypeStruct((1, 2 * d), F32)],
        compiler_params=_params("arbitrary"),
    )(dx, w, ya, yb, z, z, gate_b, gate_b, *more_args)


def _adamw(w, grads, m, v, *, name, tr=256):
    r, c = w.shape
    tr = _pick(r, tr, 8)
    ng = len(grads)
    c1 = 1.0 - ADAM_B1 ** ADAM_STEP
    c2 = 1.0 - ADAM_B2 ** ADAM_STEP

    def body(*refs):
        w_ref, g_refs, m_ref, v_ref = refs[0], refs[1:1 + ng], refs[1 + ng], refs[2 + ng]
        g_out, d_out, m_out, v_out = refs[3 + ng:]
        g = g_refs[0][...]
        for extra in g_refs[1:]:
            g = g + extra[...]
        m_new = ADAM_B1 * m_ref[...] + (1.0 - ADAM_B1) * g
        v_new = ADAM_B2 * v_ref[...] + (1.0 - ADAM_B2) * (g * g)
        g_out[...] = g
        m_out[...] = m_new
        v_out[...] = v_new
        d_out[...] = -ADAM_LR * ((m_new / c1) / (jnp.sqrt(v_new / c2) + ADAM_EPS) + ADAM_WD * w_ref[...])

    spec = pl.BlockSpec((tr, c), lambda i: (i, 0))
    return pl.pallas_call(
        body, name=name, grid=(r // tr,),
        in_specs=[spec] * (3 + ng), out_specs=[spec] * 4, out_shape=[jax.ShapeDtypeStruct((r, c), F32)] * 4,
        compiler_params=_params("parallel"),
    )(w, *grads, m, v)


CHIP_PEERS = ((1, 0), (0, 1), (1, 1))


def _place():
    return lax.axis_index("x"), lax.axis_index("y"), lax.axis_index("c")


HBM = pl.BlockSpec(memory_space=pltpu.HBM)
SEM = pl.BlockSpec(memory_space=pltpu.SEMAPHORE)
IN_FLIGHT = pltpu.SideEffectType.DATAFLOW_SIDE_EFFECTING


def _in_hbm(a):
    return pltpu.with_memory_space_constraint(a, pltpu.HBM)


def _cast_to_lands(shards, dtypes, *, name, after=None):
    n = len(shards)

    def body(*refs):
        ins, outs, bufs, sems = refs[:n], refs[n:2 * n], refs[2 * n:3 * n], refs[3 * n]
        x, y, _ = _place()
        copies = []
        for a in range(n):
            bufs[a][...] = ins[a][...].astype(dtypes[a])
            cp = pltpu.make_async_copy(bufs[a], outs[a].at[2 * x + y], sems.at[a])
            cp.start()
            copies.append(cp)
        for cp in copies:
            cp.wait()

    body, more_specs, more_args = _ordered(body, n, after)
    return pl.pallas_call(
        body, name=name, in_specs=[pl.BlockSpec(memory_space=pltpu.VMEM)] * n + more_specs, out_specs=[ANY] * n,
        out_shape=[jax.ShapeDtypeStruct((N_CHIPS,) + s.shape, dt) for s, dt in zip(shards, dtypes)],
        scratch_shapes=[pltpu.VMEM(s.shape, dt) for s, dt in zip(shards, dtypes)] + [pltpu.SemaphoreType.DMA((n,))],
        compiler_params=pltpu.CompilerParams(vmem_limit_bytes=V7X_VMEM_LIMIT_BYTES),
    )(*shards, *more_args)


def _chip_copy(src, dst, send, recv, flip, place):
    x, y, c = place
    return pltpu.make_async_remote_copy(src_ref=src, dst_ref=dst, send_sem=send, recv_sem=recv,
                                        device_id=(x ^ flip[0], y ^ flip[1], c), device_id_type=MESH)


def _my_part(land, place, halved):
    block = land.at[2 * place[0] + place[1]]
    if not halved:
        return block
    rows = land.shape[1] // 2
    return block.at[pl.ds(pl.multiple_of(place[2] * rows, rows), rows)]


def _gather_start(lands, after, *, name, halved=()):
    n = len(lands)

    def body(*refs):
        ins, send, recv, token = refs[:n], refs[n + 1], refs[n + 2], refs[-1]
        place = _place()
        for a in range(n):
            part = _my_part(ins[a], place, a in halved)
            for p, flip in enumerate(CHIP_PEERS):
                k = 3 * a + p
                _chip_copy(part, part, send.at[k], recv.at[k], flip, place).start()
        token[...] = jnp.zeros_like(token)

    outs = pl.pallas_call(
        body, name=name, in_specs=[HBM] * n + [ANY],
        out_specs=(SEM, SEM, *[HBM] * n, pl.BlockSpec(memory_space=pltpu.VMEM)),
        out_shape=(pltpu.SemaphoreType.DMA((3 * n,)), pltpu.SemaphoreType.DMA((3 * n,)),
                   *[pltpu.HBM(l.shape, l.dtype) for l in lands], jax.ShapeDtypeStruct((8, 128), F32)),
        input_output_aliases={a: 2 + a for a in range(n)},
        compiler_params=pltpu.CompilerParams(has_side_effects=IN_FLIGHT),
    )(*[_in_hbm(l) for l in lands], after)
    return outs[0], outs[1], list(outs[2:2 + n]), outs[-1]


def _gather_wait(send, recv, lands, after, *, name, halved=()):
    n = len(lands)

    def body(*refs):
        ins, send_ref, recv_ref = refs[:n], refs[n], refs[n + 1]
        place = _place()
        for a in range(n):
            part = _my_part(ins[a], place, a in halved)
            for p, flip in enumerate(CHIP_PEERS):
                k = 3 * a + p
                cp = _chip_copy(part, part, send_ref.at[k], recv_ref.at[k], flip, place)
                cp.wait_send()
                cp.wait_recv()

    after = list(after) if isinstance(after, (list, tuple)) else [after]
    return pl.pallas_call(
        body, name=name, in_specs=[HBM] * n + [SEM, SEM] + [ANY] * len(after), out_specs=[HBM] * n,
        out_shape=[pltpu.HBM(l.shape, l.dtype) for l in lands],
        input_output_aliases={a: a for a in range(n)},
        compiler_params=pltpu.CompilerParams(has_side_effects=IN_FLIGHT),
    )(*lands, send, recv, *after)


def _forward_to_sibling(land, *, name):
    rows = land.shape[1] // 2

    def body(land_ref, out_ref, send, recv):
        x, y, c = _place()
        copies = []
        for p, (fx, fy) in enumerate(CHIP_PEERS):
            chip = 2 * (x ^ fx) + (y ^ fy)
            mine = pl.ds(pl.multiple_of(c * rows, rows), rows)
            theirs = pl.ds(pl.multiple_of((1 - c) * rows, rows), rows)
            out = pltpu.make_async_remote_copy(
                src_ref=land_ref.at[chip].at[mine], dst_ref=out_ref.at[chip].at[mine], send_sem=send.at[p],
                recv_sem=recv.at[p], device_id=(x, y, 1 - c), device_id_type=MESH)
            out.start()
            copies.append((out, pltpu.make_async_remote_copy(
                src_ref=land_ref.at[chip].at[theirs], dst_ref=out_ref.at[chip].at[theirs], send_sem=send.at[p],
                recv_sem=recv.at[p], device_id=(x, y, 1 - c), device_id_type=MESH)))
        for out, arriving in copies:
            out.wait_send()
            arriving.wait_recv()

    return pl.pallas_call(
        body, name=name, in_specs=[ANY], out_specs=ANY, out_shape=jax.ShapeDtypeStruct(land.shape, land.dtype),
        input_output_aliases={0: 0},
        scratch_shapes=[pltpu.SemaphoreType.DMA((3,)), pltpu.SemaphoreType.DMA((3,))],
    )(land)


def _scatter_start(grad, *, name):
    def body(g_ref, land_ref, send, recv, g_thru, land_thru, token):
        place = _place()
        for p, flip in enumerate(CHIP_PEERS):
            peer_chip = 2 * (place[0] ^ flip[0]) + (place[1] ^ flip[1])
            _chip_copy(g_ref.at[peer_chip], land_ref.at[p], send.at[p], recv.at[p], flip, place).start()
        token[...] = jnp.zeros_like(token)

    land = lax.empty((3,) + grad.shape[1:], grad.dtype)
    return pl.pallas_call(
        body, name=name, in_specs=[HBM, HBM],
        out_specs=(SEM, SEM, HBM, HBM, pl.BlockSpec(memory_space=pltpu.VMEM)),
        out_shape=(pltpu.SemaphoreType.DMA((3,)), pltpu.SemaphoreType.DMA((3,)), pltpu.HBM(grad.shape, grad.dtype),
                   pltpu.HBM(land.shape, land.dtype), jax.ShapeDtypeStruct((8, 128), F32)),
        input_output_aliases={0: 2, 1: 3},
        compiler_params=pltpu.CompilerParams(has_side_effects=IN_FLIGHT),
    )(_in_hbm(grad), _in_hbm(land))


def _scatter_wait(started, after, *, name):
    n = len(started)

    def body(*refs):
        grads, lands = refs[:n], refs[n:2 * n]
        sends, recvs = refs[2 * n:3 * n], refs[3 * n:4 * n]
        place = _place()
        for a in range(n):
            for p, flip in enumerate(CHIP_PEERS):
                cp = _chip_copy(grads[a].at[0], lands[a].at[p], sends[a].at[p], recvs[a].at[p], flip, place)
                cp.wait_send()
                cp.wait_recv()

    grads, lands = [s[2] for s in started], [s[3] for s in started]
    after = list(after) if isinstance(after, (list, tuple)) else [after]
    outs = pl.pallas_call(
        body, name=name, in_specs=[HBM] * (2 * n) + [SEM] * (2 * n) + [ANY] * len(after), out_specs=[HBM] * (2 * n),
        out_shape=[pltpu.HBM(a.shape, a.dtype) for a in grads + lands],
        input_output_aliases={a: a for a in range(2 * n)},
        compiler_params=pltpu.CompilerParams(has_side_effects=IN_FLIGHT),
    )(*grads, *lands, *[s[0] for s in started], *[s[1] for s in started], *after)
    return list(zip(outs[:n], outs[n:]))


def _sibling_copy(src, dst, send, recv, place):
    x, y, c = place
    return pltpu.make_async_remote_copy(src_ref=src, dst_ref=dst, send_sem=send, recv_sem=recv,
                                        device_id=(x, y, 1 - c), device_id_type=MESH)


def _swap_start(arrays, *, name):
    n = len(arrays)

    def body(*refs):
        ins, lands, send, recv, token = refs[:n], refs[n:2 * n], refs[2 * n], refs[2 * n + 1], refs[-1]
        place = _place()
        for a in range(n):
            _sibling_copy(ins[a], lands[a], send.at[a], recv.at[a], place).start()
        token[...] = jnp.zeros_like(token)

    both = [_in_hbm(a) for a in arrays] + [_in_hbm(lax.empty(a.shape, a.dtype)) for a in arrays]
    outs = pl.pallas_call(
        body, name=name, in_specs=[HBM] * (2 * n),
        out_specs=(SEM, SEM, *[HBM] * (2 * n), pl.BlockSpec(memory_space=pltpu.VMEM)),
        out_shape=(pltpu.SemaphoreType.DMA((n,)), pltpu.SemaphoreType.DMA((n,)),
                   *[pltpu.HBM(a.shape, a.dtype) for a in both], jax.ShapeDtypeStruct((8, 128), F32)),
        input_output_aliases={a: 2 + a for a in range(2 * n)},
        compiler_params=pltpu.CompilerParams(has_side_effects=IN_FLIGHT),
    )(*both)
    return outs[0], outs[1], list(outs[2:2 + n]), list(outs[2 + n:2 + 2 * n]), outs[-1]


def _swap_wait(started, after, *, name):
    send, recv, arrays, lands = started[:4]
    n = len(arrays)

    def body(*refs):
        ins, zones, send_ref, recv_ref = refs[:n], refs[n:2 * n], refs[2 * n], refs[2 * n + 1]
        place = _place()
        for a in range(n):
            cp = _sibling_copy(ins[a], zones[a], send_ref.at[a], recv_ref.at[a], place)
            cp.wait_send()
            cp.wait_recv()

    after = list(after) if isinstance(after, (list, tuple)) else [after]
    outs = pl.pallas_call(
        body, name=name, in_specs=[HBM] * (2 * n) + [SEM, SEM] + [ANY] * len(after), out_specs=[HBM] * (2 * n),
        out_shape=[pltpu.HBM(a.shape, a.dtype) for a in arrays + lands],
        input_output_aliases={a: a for a in range(2 * n)},
        compiler_params=pltpu.CompilerParams(has_side_effects=IN_FLIGHT),
    )(*arrays, *lands, send, recv, *after)
    return list(outs[:n]), list(outs[n:])


def _allreduce_start(packed, *, name):
    n_dev = 8

    def body(src_ref, land_ref, send, recv, src_thru, land_thru, token):
        x, y, c = _place()
        me = 4 * x + 2 * y + c
        for p in range(1, n_dev):
            pltpu.make_async_remote_copy(
                src_ref=src_ref, dst_ref=land_ref.at[me], send_sem=send.at[p - 1], recv_sem=recv.at[p - 1],
                device_id=(x ^ (p >> 2), y ^ ((p >> 1) & 1), c ^ (p & 1)), device_id_type=MESH).start()
        token[...] = jnp.zeros_like(token)

    land = lax.empty((n_dev,) + packed.shape, packed.dtype)
    return pl.pallas_call(
        body, name=name, in_specs=[HBM, HBM],
        out_specs=(SEM, SEM, HBM, HBM, pl.BlockSpec(memory_space=pltpu.VMEM)),
        out_shape=(pltpu.SemaphoreType.DMA((n_dev - 1,)), pltpu.SemaphoreType.DMA((n_dev - 1,)),
                   pltpu.HBM(packed.shape, packed.dtype), pltpu.HBM(land.shape, land.dtype),
                   jax.ShapeDtypeStruct((8, 128), F32)),
        input_output_aliases={0: 2, 1: 3},
        compiler_params=pltpu.CompilerParams(has_side_effects=IN_FLIGHT),
    )(_in_hbm(packed), _in_hbm(land))


def _allreduce_wait(started, after, *, name):
    send, recv, packed, land = started[:4]
    n_dev = 8

    def body(src_ref, land_ref, send_ref, recv_ref, *_):
        x, y, c = _place()
        for p in range(1, n_dev):
            cp = pltpu.make_async_remote_copy(
                src_ref=src_ref, dst_ref=land_ref.at[0], send_sem=send_ref.at[p - 1], recv_sem=recv_ref.at[p - 1],
                device_id=(x ^ (p >> 2), y ^ ((p >> 1) & 1), c ^ (p & 1)), device_id_type=MESH)
            cp.wait_send()
            cp.wait_recv()

    after = list(after) if isinstance(after, (list, tuple)) else [after]
    return pl.pallas_call(
        body, name=name, in_specs=[HBM, HBM, SEM, SEM] + [ANY] * len(after), out_specs=[HBM, HBM],
        out_shape=[pltpu.HBM(packed.shape, packed.dtype), pltpu.HBM(land.shape, land.dtype)],
        input_output_aliases={0: 0, 1: 1},
        compiler_params=pltpu.CompilerParams(has_side_effects=IN_FLIGHT),
    )(packed, land, send, recv, *after)


def _sum_devices(mine, land, *, name):
    n_dev = land.shape[0]

    def body(mine_ref, land_ref, out_ref):
        x, y, c = _place()
        me = 4 * x + 2 * y + c
        total = None
        for s in range(n_dev):
            part = jnp.where(me == s, mine_ref[...], land_ref[s])
            total = part if total is None else total + part
        out_ref[...] = total

    return pl.pallas_call(body, name=name, out_shape=jax.ShapeDtypeStruct(mine.shape, mine.dtype))(mine, land)


def _sum_received(grad, land, *, name, tr=256):
    _, r, c = grad.shape
    tr = _pick(r, tr, 8)

    def body(chip_ref, g_ref, l_ref, o_ref):
        o_ref[...] = ((g_ref[...] + l_ref[0].astype(F32)) + l_ref[1].astype(F32)) + l_ref[2].astype(F32)

    chip = (2 * lax.axis_index("x") + lax.axis_index("y")).astype(jnp.int32).reshape(1)
    return pl.pallas_call(
        body, name=name,
        grid_spec=pltpu.PrefetchScalarGridSpec(
            num_scalar_prefetch=1, grid=(r // tr,),
            in_specs=[pl.BlockSpec((None, tr, c), lambda i, chip_ref: (chip_ref[0], i, 0)),
                      pl.BlockSpec((3, tr, c), lambda i, chip_ref: (0, i, 0))],
            out_specs=pl.BlockSpec((tr, c), lambda i, chip_ref: (i, 0))),
        out_shape=jax.ShapeDtypeStruct((r, c), F32), compiler_params=_params("parallel"),
    )(chip, grad, land)


def _packed_rows(size, d):
    return -(-size // (8 * d)) * 8


def _pack_rows(arrays, d):
    rows = []
    for arr in arrays:
        flat = arr.reshape(-1).astype(F32)
        n = _packed_rows(flat.shape[0], d)
        rows.append(jnp.pad(flat, (0, n * d - flat.shape[0])).reshape(n, d))
    return jnp.concatenate(rows, axis=0)


def _unpack_rows(packed, shapes, d):
    out, row = [], 0
    for shape in shapes:
        size = math.prod(shape)
        n = _packed_rows(size, d)
        out.append(packed[row:row + n].reshape(-1)[:size].reshape(shape))
        row += n
    return out


SMALL = ("norm1_g", "gate_b", "conv_b", "conv_norm_g", "q_norm_g", "k_norm_g", "norm2_g", "ffn_conv_b")
LARGE = ("w_in", "w_conv_out", "w_attn_out", "w_out", "w_up", "w_down")
WEIGHTS = ("norm1_g", "w_in", "gate_b", "conv_w", "conv_b", "conv_norm_g", "w_conv_out", "q_norm_g", "k_norm_g",
           "w_attn_out", "w_out", "norm2_g", "w_up", "ffn_conv_w", "ffn_conv_b", "w_down")


def _after(vec, token):
    return vec if token is None else vec + token[0:1, 0:1]


def _local_step(dims, x, target, small, first_weights, other_weights, send_grad):
    d, f, heads = dims.d_model, dims.d_ff, dims.n_heads
    small = dict(small)
    row = lambda name: small[name].reshape(1, -1)
    head_sum = _head_sum_matrix(dims)
    head_spread = jnp.transpose(head_sum)
    ones = (head_sum, head_spread)
    gq = jnp.tile(row("q_norm_g"), (1, heads))
    gk = jnp.tile(row("k_norm_g"), (1, heads))
    one_shard = lambda w: w.reshape(1, -1, w.shape[-1])

    h = _rmsnorm_fwd(x, row("norm1_g"), name="norm1")
    full = first_weights(h)
    w_in = full["w_in"]
    conv_w = jnp.pad(full["conv_w"], ((0, CONV_HALO - dims.conv_width), (0, 0)))
    ffn_w = jnp.pad(full["ffn_conv_w"], ((0, FFN_HALO - dims.ffn_conv_width), (0, 0)))
    z = _mm_nn(h, w_in, out_dtype=BF16, after=full.get("token"), tm=2048, tn=1792, name="in_proj")
    a1, a3 = _conv_branch_fwd(z, conv_w, row("conv_b"), row("conv_norm_g"), dims, name="conv_branch")
    qkv = _qkv_layouts_fwd(z, gq, gk, ones, dims, name="qk_norm")
    per_group = {dil: _attn_fwd(*qkv[dil], dims, dil, name=f"attn_fwd_d{dil}") for dil in DILATIONS}
    o, lse = _attn_combine(per_group, head_spread, dims, name="attn_combine")
    full = other_weights(o)
    w_up = full["w_up"]
    w_co, w_ao, w_o, w_dn = (one_shard(full[k]) for k in ("w_conv_out", "w_attn_out", "w_out", "w_down"))
    ya, yb, mixed, x1, h2 = _mix_fwd(a3, o, w_co, w_ao, w_o, z, row("gate_b"), x, row("norm2_g"), dims,
                                     name="branch_projs_mix_out_proj_norm2")
    up = _mm_nn(h2, w_up, out_dtype=F32, tm=2048, name="up_proj")
    act = _ffn_act_fwd(up, ffn_w, row("ffn_conv_b"), dims, name="ffn_act")
    dy, dy_b, loss = _proj_residual_loss(act, w_dn, x1, target, tm=512, name="down_proj_loss")

    grads = {}

    def large(name, g):
        grads[name], g_bf16 = g
        return send_grad(name, grads[name], g_bf16)

    sent = large("w_down", _mm_tn(act, dy_b, n_shards=1, name="dw_down"))
    dact = _mm_nt(dy_b, w_dn, out_dtype=BF16, after=sent, name="d_act")
    dup, dfw, dfb = _ffn_bwd(dact, up, ffn_w, row("ffn_conv_b"), dims, name="ffn_bwd")
    grads["ffn_conv_w"], grads["ffn_conv_b"] = dfw[:dims.ffn_conv_width], dfb
    sent = large("w_up", _mm_tn(h2, dup, n_shards=N_CHIPS, name="dw_up"))
    dx1, dx1_b, grads["norm2_g"] = _mm_nt_rmsnorm_bwd(dup, w_up, x1, row("norm2_g"), dy, want_bf16=True, after=sent,
                                                     name="d_h2_norm2_bwd")
    sent = large("w_out", _mm_tn(mixed, dx1_b, n_shards=1, name="dw_out"))
    dya, dyb, dz_gate, grads["gate_b"] = _mix_bwd(dx1_b, w_o, ya, yb, z, row("gate_b"), dims, after=sent,
                                                  name="d_mix_gate_mix_bwd")
    sent = large("w_attn_out", _mm_tn(o, dyb, n_shards=1, name="dw_attn_out"))
    dos, deltas = _attn_bwd_prep(dyb, w_ao, o, head_sum, dims, after=sent, name="d_attn_bwd_prep")
    dqkv = {dil: _attn_bwd(*qkv[dil], dos[dil], lse[dil], deltas[dil], dims, dil, name=f"attn_bwd_d{dil}")
            for dil in DILATIONS}
    dz_qkv, dgq, dgk = _qkv_layouts_bwd(z, dqkv, gq, gk, ones, dims, name="qk_norm_bwd")
    grads["q_norm_g"] = dgq.reshape(heads, dims.head_dim).sum(axis=0)
    grads["k_norm_g"] = dgk.reshape(heads, dims.head_dim).sum(axis=0)
    sent = large("w_conv_out", _mm_tn(a3, dya, n_shards=1, name="dw_conv_out"))
    da1, grads["conv_norm_g"] = _mm_nt_rmsnorm_bwd(dya, w_co, a1, row("conv_norm_g"), None, want_bf16=False, silu=True,
                                                   after=sent, name="d_conv_act_norm_bwd")
    dz, dcw, grads["conv_b"] = _conv_branch_bwd(da1, z, conv_w, [dz_qkv, dz_gate], dims, name="conv_branch_bwd")
    grads["conv_w"] = dcw[:dims.conv_width]
    sent = large("w_in", _mm_tn(h, dz, n_shards=N_CHIPS, name="dw_in"))
    dx, grads["norm1_g"] = _mm_nt_rmsnorm_bwd(dz, w_in, x, row("norm1_g"), dx1, want_bf16=False, after=sent,
                                              name="d_h_norm1_bwd")
    return loss, dx, grads


def _step(dims, x, target, w, m, v):
    d = dims.d_model
    t = dims.tokens
    sq = lambda a: a.reshape(a.shape[1:])
    w2, m2, v2 = ({k: sq(a) for k, a in grp.items()} for grp in (w, m, v))

    conv_pad = jnp.pad(w2["conv_w"], ((0, CONV_HALO - dims.conv_width), (0, 0)))
    ffn_pad = jnp.pad(w2["ffn_conv_w"], ((0, FFN_HALO - dims.ffn_conv_width), (0, 0)))
    first_names = ("w_in", "conv_w", "ffn_conv_w")
    other_names = tuple(k for k in LARGE if k not in first_names)
    lands = dict(zip(first_names, _cast_to_lands([w2["w_in"], conv_pad, ffn_pad], [BF16, F32, F32], name="cast_first")))
    first = _gather_start([lands[k] for k in first_names], x, halved=(0,), name="gather_start_first")
    lands.update(zip(other_names, _cast_to_lands([w2[k] for k in other_names], [BF16] * len(other_names),
                                                 after=first[3], name="cast_other")))
    other = []
    cols = lambda g, rows: jnp.moveaxis(g, 0, 1).reshape(g.shape[1], -1)[:rows]

    def first_weights(after):
        got = dict(zip(first_names, _gather_wait(*first[:3], [after] + [lands[k] for k in other_names], halved=(0,),
                                                 name="gather_wait_first")))
        got["w_in"] = _forward_to_sibling(got["w_in"], name="forward_w_in")
        other.extend(_gather_start([lands[k] for k in other_names], got["w_in"], name="gather_start_other"))
        got["conv_w"] = cols(got["conv_w"], dims.conv_width)
        got["ffn_conv_w"] = cols(got["ffn_conv_w"], dims.ffn_conv_width)
        got["token"] = other[3]
        return got

    def other_weights(after):
        return dict(zip(other_names, _gather_wait(*other[:3], after, name="gather_wait_other")))

    started, full, swapping_others = {}, {}, []
    others = [k for k in LARGE if k != "w_in"]

    def my_sums(names, after, tag):
        arrived = _scatter_wait([started[k] for k in names], after, name=f"scatter_wait_{tag}")
        return [_sum_received(full[k], land, name=f"sum_{k}") for k, (_, land) in zip(names, arrived)]

    def send_grad(name, g, g_bf16):
        blocks = lambda a: a.reshape(N_CHIPS, -1, a.shape[-1])
        send, recv, g_thru, land, token = _scatter_start(blocks(g_bf16), name=f"scatter_start_{name}")
        started[name], full[name] = (send, recv, g_thru, land), blocks(g)
        if name != "w_in":
            return token
        swapping_others.extend(_swap_start(my_sums(others, token, "others"), name="swap_start_others"))
        return swapping_others[4]

    small = {k: w2[k] for k in SMALL}
    small["norm1_g"] = _after(small["norm1_g"].reshape(1, -1), first[3])
    loss, dx, grads = _local_step(dims, x.reshape(t, d), target.reshape(t, d), small, first_weights, other_weights, send_grad)

    def updates(names, mine, theirs):
        return {k: _adamw(w2[k], [a, b], m2[k], v2[k], name=f"adamw_{k}") for k, a, b in zip(names, mine, theirs)}

    small_names = SMALL + ("conv_w", "ffn_conv_w")
    packed = _pack_rows([grads[k] for k in small_names] + [loss[0, 0]], d)
    reducing = _allreduce_start(packed, name="allreduce_start")
    out = updates(others, *_swap_wait(swapping_others, [dx, reducing[4]], name="swap_wait_others"))
    mine_w_in = my_sums(["w_in"], [out[k][1] for k in others], "w_in")
    swapping_w_in = _swap_start(mine_w_in, name="swap_start_w_in")
    reduced = _sum_devices(*_allreduce_wait(reducing, swapping_w_in[4], name="allreduce_wait"), name="allreduce_sum")
    vector_rows = sum(_packed_rows(math.prod(grads[k].shape), d) for k in SMALL)
    tail_shapes = [grads[k].shape for k in ("conv_w", "ffn_conv_w")] + [()]
    conv_g, ffn_g, loss_total = _unpack_rows(reduced[vector_rows:], tail_shapes, d)
    chip = 2 * lax.axis_index("x") + lax.axis_index("y")
    sharded_g = [lax.dynamic_slice_in_dim(g, chip * w2[k].shape[1], w2[k].shape[1], axis=1)
                 for k, g in (("conv_w", conv_g), ("ffn_conv_w", ffn_g))]
    packed_g = jnp.concatenate([reduced[:vector_rows], _pack_rows(sharded_g, d)], axis=0)

    small_shapes = [w2[k].shape for k in small_names]
    pack = lambda grp: _pack_rows([grp[k] for k in small_names], d)
    results = _adamw(pack(w2), [packed_g], pack(m2), pack(v2), name="adamw_small")
    unpacked = [_unpack_rows(r, small_shapes, d) for r in results]
    out.update({k: tuple(u[i] for u in unpacked) for i, k in enumerate(small_names)})
    out.update(updates(["w_in"], *_swap_wait(swapping_w_in, results[1], name="swap_wait_w_in")))

    lead = lambda a: a.reshape((1,) + a.shape)
    ordered = [[lead(out[k][j].reshape(w2[k].shape)) for k in WEIGHTS] for j in range(4)]
    return (loss_total, dx.reshape(x.shape), *ordered[0], *ordered[1], *ordered[2], *ordered[3])


def kernel(x, norm1_g, w_in, gate_b, conv_w, conv_b, conv_norm_g, w_conv_out, q_norm_g, k_norm_g, w_attn_out, w_out, norm2_g, w_up, ffn_conv_w, ffn_conv_b, w_down, loss_target, m_norm1_g, m_w_in, m_gate_b, m_conv_w, m_conv_b, m_conv_norm_g, m_w_conv_out, m_q_norm_g, m_k_norm_g, m_w_attn_out, m_w_out, m_norm2_g, m_w_up, m_ffn_conv_w, m_ffn_conv_b, m_w_down, v_norm1_g, v_w_in, v_gate_b, v_conv_w, v_conv_b, v_conv_norm_g, v_w_conv_out, v_q_norm_g, v_k_norm_g, v_w_attn_out, v_w_out, v_norm2_g, v_w_up, v_ffn_conv_w, v_ffn_conv_b, v_w_down):
    w = dict(zip(WEIGHTS, (norm1_g, w_in, gate_b, conv_w, conv_b, conv_norm_g, w_conv_out, q_norm_g, k_norm_g,
                           w_attn_out, w_out, norm2_g, w_up, ffn_conv_w, ffn_conv_b, w_down)))
    m = dict(zip(WEIGHTS, (m_norm1_g, m_w_in, m_gate_b, m_conv_w, m_conv_b, m_conv_norm_g, m_w_conv_out, m_q_norm_g,
                           m_k_norm_g, m_w_attn_out, m_w_out, m_norm2_g, m_w_up, m_ffn_conv_w, m_ffn_conv_b, m_w_down)))
    v = dict(zip(WEIGHTS, (v_norm1_g, v_w_in, v_gate_b, v_conv_w, v_conv_b, v_conv_norm_g, v_w_conv_out, v_q_norm_g,
                           v_k_norm_g, v_w_attn_out, v_w_out, v_norm2_g, v_w_up, v_ffn_conv_w, v_ffn_conv_b, v_w_down)))
    dims = Dims(d_model=x.shape[-1], batch_local=x.shape[0], seq=x.shape[1], d_ff=w_down.shape[1] * N_CHIPS)
    return _step(dims, x, loss_target, w, m, v)
```

```python
import functools
import math
from typing import NamedTuple

import jax
import jax.numpy as jnp
from jax import lax
from jax.experimental import pallas as pl
from jax.experimental.pallas import tpu as pltpu

F32 = jnp.float32
BF16 = jnp.bfloat16

RMS_EPS = 1e-6
ATTN_BLOCK = 128
DILATIONS = (1, 4, 16)
CONV_HALO = 32
FFN_HALO = 8
ADAM_LR, ADAM_B1, ADAM_B2, ADAM_EPS, ADAM_WD, ADAM_STEP = 0.001, 0.9, 0.999, 1e-08, 0.01, 10
V7X_VMEM_LIMIT_BYTES = 56 * 2 ** 20
N_CHIPS = 4
MESH = pl.DeviceIdType.MESH


class Dims(NamedTuple):
    d_model: int = 1024
    n_heads: int = 16
    head_dim: int = 64
    d_ff: int = 2816
    seq: int = 2048
    batch_local: int = 2
    conv_width: int = 31
    ffn_conv_width: int = 3

    @property
    def tokens(self):
        return self.seq * self.batch_local


def _params(*semantics):
    return pltpu.CompilerParams(dimension_semantics=semantics, vmem_limit_bytes=V7X_VMEM_LIMIT_BYTES)


ANY = pl.BlockSpec(memory_space=pl.ANY)


def _ordered(body, n_inputs, after):
    after = [] if after is None else list(after) if isinstance(after, (list, tuple)) else [after]
    if not after:
        return body, [], []

    def wrapped(*refs):
        return body(*refs[:n_inputs], *refs[n_inputs + len(after):])

    return wrapped, [ANY] * len(after), after


def _pick(n, target, mult=128):
    if n <= target:
        return n
    best = None
    for t in range(mult, target + 1, mult):
        if n % t == 0:
            best = t
    assert best is not None, (n, target, mult)
    return best


def _sigmoid(v):
    return 1.0 / (1.0 + jnp.exp(-v))


def _mm_nn(a, w, *, out_dtype, name, residual=None, after=None, tm=1024, tn=1408, tk=2816):
    m, k = a.shape
    nsh, k2, c = w.shape
    assert k == k2 and a.dtype == BF16 and w.dtype == BF16
    n = nsh * c
    tm, tn, tk = _pick(m, tm, 8), _pick(c, tn), _pick(k, tk)
    nk, cpn = k // tk, c // tn

    def body(*refs):
        if residual is None:
            a_ref, w_ref, o_ref, acc = refs
        else:
            a_ref, w_ref, r_ref, o_ref, acc = refs
        prod = jnp.dot(a_ref[...], w_ref[...], preferred_element_type=F32)

        def finish(total):
            if residual is not None:
                total = total + r_ref[...]
            o_ref[...] = total.astype(out_dtype)

        if nk == 1:
            finish(prod)
        else:
            kk = pl.program_id(2)

            @pl.when(kk == 0)
            def _():
                acc[...] = prod

            @pl.when(kk > 0)
            def _():
                acc[...] += prod

            @pl.when(kk == nk - 1)
            def _():
                finish(acc[...])

    in_specs = [pl.BlockSpec((tm, tk), lambda i, j, kk: (i, kk)),
                pl.BlockSpec((None, tk, tn), lambda i, j, kk: (j // cpn, kk, j % cpn))]
    args = [a, w]
    if residual is not None:
        in_specs.append(pl.BlockSpec((tm, tn), lambda i, j, kk: (i, j)))
        args.append(residual)
    body, more_specs, more_args = _ordered(body, len(args), after)
    return pl.pallas_call(
        body, name=name, grid=(m // tm, n // tn, nk),
        in_specs=in_specs + more_specs, out_specs=pl.BlockSpec((tm, tn), lambda i, j, kk: (i, j)),
        out_shape=jax.ShapeDtypeStruct((m, n), out_dtype),
        scratch_shapes=[pltpu.VMEM((tm, tn) if nk > 1 else (8, 128), F32)],
        compiler_params=_params("parallel", "parallel", "arbitrary"),
    )(*args, *more_args)


def _proj_residual_loss(a, w, residual, target, *, name, tm=1024):
    m, k = a.shape
    _, k2, n = w.shape
    assert w.shape[0] == 1 and k == k2 and a.dtype == BF16 and w.dtype == BF16
    tm = _pick(m, tm, 8)

    def body(a_ref, w_ref, r_ref, t_ref, dy_ref, dyb_ref, loss_ref):
        err = r_ref[...] + jnp.dot(a_ref[...], w_ref[...], preferred_element_type=F32) - t_ref[...]
        dy = err * (1.0 / n)
        dy_ref[...] = dy
        dyb_ref[...] = dy.astype(BF16)
        part = jnp.sum(jnp.sum(err * err, axis=-1, keepdims=True), axis=0, keepdims=True) * (0.5 / n)
        _accumulate(loss_ref, jnp.broadcast_to(part, (8, 128)), pl.program_id(0) == 0)

    rows = lambda width: pl.BlockSpec((tm, width), lambda i: (i, 0))
    return pl.pallas_call(
        body, name=name, grid=(m // tm,),
        in_specs=[rows(k), pl.BlockSpec((None, k, n), lambda i: (0, 0, 0)), rows(n), rows(n)],
        out_specs=[rows(n), rows(n), pl.BlockSpec((8, 128), lambda i: (0, 0))],
        out_shape=[jax.ShapeDtypeStruct((m, n), F32), jax.ShapeDtypeStruct((m, n), BF16),
                   jax.ShapeDtypeStruct((8, 128), F32)],
        compiler_params=_params("arbitrary"),
    )(a, w, residual, target)


def _mm_nt(a, w, *, out_dtype, name, after=None, tm=1024, tn=1408, tk=1792):
    m, k = a.shape
    nsh, r, c = w.shape
    assert k == nsh * c and a.dtype == BF16 and w.dtype == BF16
    tm, tn, tk = _pick(m, tm, 8), _pick(r, tn), _pick(c, tk)
    nk, cpk = k // tk, c // tk

    def body(a_ref, w_ref, o_ref, acc):
        prod = lax.dot_general(a_ref[...], w_ref[...], (((1,), (1,)), ((), ())), preferred_element_type=F32)
        if nk == 1:
            o_ref[...] = prod.astype(out_dtype)
        else:
            kk = pl.program_id(2)

            @pl.when(kk == 0)
            def _():
                acc[...] = prod

            @pl.when(kk > 0)
            def _():
                acc[...] += prod

            @pl.when(kk == nk - 1)
            def _():
                o_ref[...] = acc[...].astype(out_dtype)

    body, more_specs, more_args = _ordered(body, 2, after)
    return pl.pallas_call(
        body, name=name, grid=(m // tm, r // tn, nk),
        in_specs=[pl.BlockSpec((tm, tk), lambda i, j, kk: (i, kk)),
                  pl.BlockSpec((None, tn, tk), lambda i, j, kk: (kk // cpk, j, kk % cpk))] + more_specs,
        out_specs=pl.BlockSpec((tm, tn), lambda i, j, kk: (i, j)),
        out_shape=jax.ShapeDtypeStruct((m, r), out_dtype),
        scratch_shapes=[pltpu.VMEM((tm, tn) if nk > 1 else (8, 128), F32)],
        compiler_params=_params("parallel", "parallel", "arbitrary"),
    )(a, w, *more_args)


NORM_BWD_ROWS = 256


def _mm_nt_rmsnorm_bwd(a, w, x, g, dres, *, name, want_bf16, silu=False, after=None, tm=1024, tk=1792):
    m, k = a.shape
    nsh, r, c = w.shape
    assert k == nsh * c and a.dtype == BF16 and w.dtype == BF16 and x.shape == (m, r)
    tm, tk = _pick(m, tm, 8), _pick(c, tk)
    nk, cpk = k // tk, c // tk
    rows = _pick(tm, NORM_BWD_ROWS, 8)
    n_in = 4 if dres is None else 5

    def body(a_ref, w_ref, x_ref, g_ref, *rest):
        dres_ref = None if dres is None else rest[0]
        outs, acc = rest[n_in - 4:-1], rest[-1]
        dx_ref, dg_ref = outs[0], outs[-1]
        kk = pl.program_id(1)
        prod = lax.dot_general(a_ref[...], w_ref[...], (((1,), (1,)), ((), ())), preferred_element_type=F32)

        @pl.when(kk == 0)
        def _():
            acc[...] = prod

        @pl.when(kk > 0)
        def _():
            acc[...] += prod

        @pl.when(kk == nk - 1)
        def _():
            dg = jnp.zeros((1, r), F32)
            for r0 in range(0, tm, rows):
                part = slice(r0, r0 + rows)
                xv, dyv = x_ref[part, :], acc[part, :]
                inv = lax.rsqrt(jnp.mean(xv * xv, axis=-1, keepdims=True) + RMS_EPS)
                if silu:
                    y = xv * inv * g_ref[...]
                    sg = _sigmoid(y)
                    dyv = dyv * sg * (1.0 + y * (1.0 - sg))
                gy = dyv * g_ref[...]
                dx = inv * gy - xv * (inv * inv * inv) * jnp.mean(xv * gy, axis=-1, keepdims=True)
                if dres is not None:
                    dx = dx + dres_ref[part, :]
                dx_ref[part, :] = dx
                if want_bf16:
                    outs[1][part, :] = dx.astype(BF16)
                dg = dg + jnp.sum(dyv * xv * inv, axis=0, keepdims=True)
            _accumulate(dg_ref, dg, pl.program_id(0) == 0)

    whole = lambda: pl.BlockSpec((tm, r), lambda i, kk: (i, 0))
    vec = pl.BlockSpec((1, r), lambda i, kk: (0, 0))
    out_shape, out_specs = [jax.ShapeDtypeStruct((m, r), F32)], [whole()]
    if want_bf16:
        out_shape.append(jax.ShapeDtypeStruct((m, r), BF16))
        out_specs.append(whole())
    out_shape.append(jax.ShapeDtypeStruct((1, r), F32))
    out_specs.append(vec)
    body, more_specs, more_args = _ordered(body, n_in, after)
    residual_specs, residual_args = ([], []) if dres is None else ([whole()], [dres])
    return pl.pallas_call(
        body, name=name, grid=(m // tm, nk),
        in_specs=[pl.BlockSpec((tm, tk), lambda i, kk: (i, kk)),
                  pl.BlockSpec((None, r, tk), lambda i, kk: (kk // cpk, 0, kk % cpk)), whole(), vec]
        + residual_specs + more_specs,
        out_specs=out_specs, out_shape=out_shape,
        scratch_shapes=[pltpu.VMEM((tm, r), F32)],
        compiler_params=_params("arbitrary", "arbitrary"),
    )(a, w, x, g, *residual_args, *more_args)


MM_TN_VMEM_BYTES = 44 * 2 ** 20


def _mm_tn(a, b, *, n_shards, name, tm=1408, tn=1408):
    t, m = a.shape
    t2, n = b.shape
    assert t == t2 and a.dtype == BF16 and b.dtype == BF16
    c = n // n_shards
    tm, tn = _pick(m, tm), _pick(c, tn)
    if m // tm == 1 and n // tn == 1 and tn % (2 * LANES) == 0:
        tn //= 2
    fixed = 2 * tm * tn * 6
    if 4 * t * (tm + tn) + fixed <= MM_TN_VMEM_BYTES:
        tk = t
    else:
        tk = _pick(t, (MM_TN_VMEM_BYTES - fixed - 4 * tm * tn) // (4 * (tm + tn)), 8)
    nk, cpn = t // tk, c // tn

    def body(a_ref, b_ref, o_ref, ob_ref, acc):
        kk = pl.program_id(2)
        prod = lax.dot_general(a_ref[...], b_ref[...], (((0,), (0,)), ((), ())), preferred_element_type=F32)

        def finish(total):
            o_ref[...] = total
            ob_ref[...] = total.astype(BF16)

        if nk == 1:
            finish(prod)
        else:
            @pl.when(kk == 0)
            def _():
                acc[...] = prod

            @pl.when(kk > 0)
            def _():
                acc[...] += prod

            @pl.when(kk == nk - 1)
            def _():
                finish(acc[...])

    out_spec = pl.BlockSpec((None, tm, tn), lambda i, j, kk: (j // cpn, i, j % cpn))
    return pl.pallas_call(
        body, name=name, grid=(m // tm, n // tn, nk),
        in_specs=[pl.BlockSpec((tk, tm), lambda i, j, kk: (kk, i)),
                  pl.BlockSpec((tk, tn), lambda i, j, kk: (kk, j))],
        out_specs=[out_spec, out_spec],
        out_shape=[jax.ShapeDtypeStruct((n_shards, m, c), F32), jax.ShapeDtypeStruct((n_shards, m, c), BF16)],
        scratch_shapes=[pltpu.VMEM((tm, tn) if nk > 1 else (8, 128), F32)],
        compiler_params=_params("parallel", "parallel", "arbitrary"),
    )(a, b)


def _row_spec(tr, width, col=0):
    return pl.BlockSpec((tr, width), lambda i, col=col: (i, col))


def _vec_spec(width, col=0):
    return pl.BlockSpec((1, width), lambda i, col=col: (0, col))


def _accumulate(ref, value, first):
    @pl.when(first)
    def _():
        ref[...] = value

    @pl.when(jnp.logical_not(first))
    def _():
        ref[...] += value


def _rmsnorm_fwd(x, g, *, name, tr=512):
    t, d = x.shape
    tr = _pick(t, tr, 8)

    def body(x_ref, g_ref, o_ref):
        xv = x_ref[...]
        r = lax.rsqrt(jnp.mean(xv * xv, axis=-1, keepdims=True) + RMS_EPS)
        o_ref[...] = (xv * r * g_ref[...]).astype(BF16)

    return pl.pallas_call(
        body, name=name, grid=(t // tr,),
        in_specs=[_row_spec(tr, d), _vec_spec(d)], out_specs=_row_spec(tr, d),
        out_shape=jax.ShapeDtypeStruct((t, d), BF16), compiler_params=_params("parallel"),
    )(x, g)


CONV_ROWS = 16


def _seq_specs(dims, ts, width, halo, col, *, nxt=False):
    nst, per = dims.seq // ts, ts // halo
    last = dims.tokens // halo - 1
    cur = pl.BlockSpec((ts, width), lambda b, i: (b * nst + i, col))
    if nxt:
        edge = pl.BlockSpec((halo, width), lambda b, i: (jnp.minimum((b * nst + i + 1) * per, last), col))
    else:
        edge = pl.BlockSpec((halo, width), lambda b, i: (jnp.maximum((b * nst + i) * per - 1, 0), col))
    return cur, edge


SUBLANES = 8


def _shifted_copies(buf, shifted):
    rows = shifted.shape[1]
    for s in range(1, SUBLANES):
        shifted[s - 1] = buf[pl.ds(s, rows), :]


def _window(buf, shifted, start, size):
    a, s = divmod(start, SUBLANES)
    src = buf if s == 0 else shifted.at[s - 1]
    return src[pl.ds(SUBLANES * a, size), :]


def _conv_branch_fwd(z, w, b, g, dims, *, name, ts=128):
    t, c, kw = z.shape[0], dims.d_model, dims.conv_width
    base = CONV_HALO - (kw - 1)

    def body(av_ref, hv_ref, ag_ref, hg_ref, w_ref, b_ref, g_ref, a1_ref, a3_ref, buf, shifted):
        i = pl.program_id(1)
        buf[CONV_HALO:, :] = av_ref[...].astype(F32) * _sigmoid(ag_ref[...].astype(F32))
        buf[0:CONV_HALO, :] = jnp.where(i > 0, hv_ref[...].astype(F32) * _sigmoid(hg_ref[...].astype(F32)), 0.0)
        _shifted_copies(buf, shifted)
        for r0 in range(0, ts, CONV_ROWS):
            acc = jnp.broadcast_to(b_ref[...], (CONV_ROWS, c))
            for k in range(kw):
                acc = acc + w_ref[k:k + 1, :] * _window(buf, shifted, r0 + base + k, CONV_ROWS)
            a1_ref[r0:r0 + CONV_ROWS, :] = acc
            a2 = acc * lax.rsqrt(jnp.mean(acc * acc, axis=-1, keepdims=True) + RMS_EPS) * g_ref[...]
            a3_ref[r0:r0 + CONV_ROWS, :] = (a2 * _sigmoid(a2)).astype(BF16)

    vec = pl.BlockSpec((1, c), lambda b, i: (0, 0))
    out = pl.BlockSpec((ts, c), lambda b, i: (b * (dims.seq // ts) + i, 0))
    return pl.pallas_call(
        body, name=name, grid=(dims.batch_local, dims.seq // ts),
        in_specs=[*_seq_specs(dims, ts, c, CONV_HALO, 0), *_seq_specs(dims, ts, c, CONV_HALO, 1),
                  pl.BlockSpec((CONV_HALO, c), lambda b, i: (0, 0)), vec, vec],
        out_specs=[out, out],
        out_shape=[jax.ShapeDtypeStruct((t, c), F32), jax.ShapeDtypeStruct((t, c), BF16)],
        scratch_shapes=[pltpu.VMEM((CONV_HALO + ts, c), F32),
                        pltpu.VMEM((SUBLANES - 1, CONV_HALO + ts - SUBLANES, c), F32)],
        compiler_params=_params("parallel", "parallel"),
    )(z, z, z, z, w, b, g)


def _conv_branch_bwd(da1, z, w, rest_of_dz, dims, *, name, ts=128):
    t, c, kw = z.shape[0], dims.d_model, dims.conv_width
    nst = dims.seq // ts
    base = CONV_HALO - (kw - 1)
    n_rest = len(rest_of_dz)
    total = 2 * c + sum(r.shape[1] for r in rest_of_dz)

    def body(d_ref, dn_ref, av_ref, hv_ref, ag_ref, hg_ref, w_ref, *more):
        rest_refs = more[:n_rest]
        dz_ref, dw_ref, db_ref, abuf, dbuf, ashift, dshift = more[n_rest:]
        col = 2 * c
        for r in rest_refs:
            dz_ref[:, col:col + r.shape[1]] = r[...]
            col += r.shape[1]
        i = pl.program_id(1)
        first = jnp.logical_and(pl.program_id(0) == 0, i == 0)
        abuf[CONV_HALO:, :] = av_ref[...].astype(F32) * _sigmoid(ag_ref[...].astype(F32))
        abuf[0:CONV_HALO, :] = jnp.where(i > 0, hv_ref[...].astype(F32) * _sigmoid(hg_ref[...].astype(F32)), 0.0)
        d1 = d_ref[...]
        dbuf[0:ts, :] = d1
        dbuf[ts:, :] = jnp.where(i < nst - 1, dn_ref[...], 0.0)
        _shifted_copies(abuf, ashift)
        _shifted_copies(dbuf, dshift)

        @pl.when(first)
        def _():
            dw_ref[...] = jnp.zeros_like(dw_ref)
            db_ref[...] = jnp.zeros_like(db_ref)

        db_ref[...] += jnp.sum(d1, axis=0, keepdims=True)
        for k in range(kw):
            dw_ref[k:k + 1, :] += jnp.sum(d1 * _window(abuf, ashift, base + k, ts), axis=0, keepdims=True)
        for r0 in range(0, ts, CONV_ROWS):
            acc = jnp.zeros((CONV_ROWS, c), F32)
            for k in range(kw):
                acc = acc + w_ref[k:k + 1, :] * _window(dbuf, dshift, r0 + (kw - 1) - k, CONV_ROWS)
            av = av_ref[r0:r0 + CONV_ROWS, :].astype(F32)
            sg = _sigmoid(ag_ref[r0:r0 + CONV_ROWS, :].astype(F32))
            dz_ref[r0:r0 + CONV_ROWS, 0:c] = (acc * sg).astype(BF16)
            dz_ref[r0:r0 + CONV_ROWS, c:2 * c] = (acc * av * sg * (1.0 - sg)).astype(BF16)

    cur, nxt = _seq_specs(dims, ts, c, CONV_HALO, 0, nxt=True)
    return pl.pallas_call(
        body, name=name, grid=(dims.batch_local, nst),
        in_specs=[cur, nxt, *_seq_specs(dims, ts, c, CONV_HALO, 0), *_seq_specs(dims, ts, c, CONV_HALO, 1),
                  pl.BlockSpec((CONV_HALO, c), lambda b, i: (0, 0))]
        + [pl.BlockSpec((ts, r.shape[1]), lambda b, i: (b * nst + i, 0)) for r in rest_of_dz],
        out_specs=[pl.BlockSpec((ts, total), lambda b, i: (b * nst + i, 0)),
                   pl.BlockSpec((CONV_HALO, c), lambda b, i: (0, 0)), pl.BlockSpec((1, c), lambda b, i: (0, 0))],
        out_shape=[jax.ShapeDtypeStruct((t, total), BF16), jax.ShapeDtypeStruct((CONV_HALO, c), F32),
                   jax.ShapeDtypeStruct((1, c), F32)],
        scratch_shapes=[pltpu.VMEM((CONV_HALO + ts, c), F32)] * 2
        + [pltpu.VMEM((SUBLANES - 1, CONV_HALO + ts - SUBLANES, c), F32)] * 2,
        compiler_params=_params("arbitrary", "arbitrary"),
    )(da1, da1, z, z, z, z, w, *rest_of_dz)


FFN_ROWS = 16
FFN_COLS = 256


def _ffn_chunks(ts, f):
    cw = _pick(f, FFN_COLS)
    return [(r0, c0, cw) for r0 in range(0, ts, FFN_ROWS) for c0 in range(0, f, cw)]


def _tap_sources(buf, moved, offsets, rows):
    taps, used = [], 0
    for off in offsets:
        if off % SUBLANES:
            moved[used] = buf[pl.ds(off, rows), :]
            taps.append((moved.at[used], 0))
            used += 1
        else:
            taps.append((buf, off))
    return taps


def _moved_copies(offsets):
    return sum(1 for off in offsets if off % SUBLANES)


def _taps_sum(taps, w_ref, init, r0, cols):
    for k, (src, off) in enumerate(taps):
        init = init + w_ref[k:k + 1, cols] * src[pl.ds(off + r0, init.shape[0]), cols]
    return init


def _ffn_bwd(dact, up, w, b, dims, *, name, ts=128):
    t, f, kw = up.shape[0], dims.d_ff, dims.ffn_conv_width
    nst = dims.seq // ts
    fwd_offsets = [FFN_HALO - (kw - 1) + k for k in range(kw)]
    bwd_offsets = [(kw - 1) - k for k in range(kw)]
    dact_halo = 2 * FFN_HALO

    def body(d_ref, dn_ref, up_ref, hp_ref, hn_ref, w_ref, b_ref, o_ref, dw_ref, db_ref, buf, moved, dbuf, dmoved):
        i = pl.program_id(1)
        first = jnp.logical_and(pl.program_id(0) == 0, i == 0)
        more = i < nst - 1
        buf[0:FFN_HALO, :] = jnp.where(i > 0, hp_ref[...], 0.0)
        buf[FFN_HALO:FFN_HALO + ts, :] = up_ref[...]
        buf[FFN_HALO + ts:, :] = hn_ref[...]
        taps = _tap_sources(buf, moved, fwd_offsets, ts + FFN_HALO)

        def du_chunk(r0, rows, c0, cw, d):
            vcols, gcols = slice(c0, c0 + cw), slice(f + c0, f + c0 + cw)
            uv = _taps_sum(taps, w_ref, jnp.broadcast_to(b_ref[:, vcols], (rows, cw)), r0, vcols)
            ug = _taps_sum(taps, w_ref, jnp.broadcast_to(b_ref[:, gcols], (rows, cw)), r0, gcols)
            sg = _sigmoid(ug)
            dbuf[r0:r0 + rows, vcols] = d * ug * sg
            dbuf[r0:r0 + rows, gcols] = d * uv * sg * (1.0 + ug * (1.0 - sg))

        for r0, c0, cw in _ffn_chunks(ts, f):
            du_chunk(r0, FFN_ROWS, c0, cw, d_ref[r0:r0 + FFN_ROWS, c0:c0 + cw].astype(F32))
        for _, c0, cw in _ffn_chunks(FFN_ROWS, f):
            d_next = dn_ref[:, c0:c0 + cw].astype(F32)[0:FFN_HALO]
            du_chunk(ts, FFN_HALO, c0, cw, jnp.where(more, d_next, 0.0))

        @pl.when(first)
        def _():
            dw_ref[...] = jnp.zeros_like(dw_ref)
            db_ref[...] = jnp.zeros_like(db_ref)

        du = dbuf[0:ts, :]
        db_ref[...] += jnp.sum(du, axis=0, keepdims=True)
        for k, (src, off) in enumerate(taps):
            dw_ref[k:k + 1, :] += jnp.sum(du * src[pl.ds(off, ts), :], axis=0, keepdims=True)

        dtaps = _tap_sources(dbuf, dmoved, bwd_offsets, ts)
        for r0, c0, cw in _ffn_chunks(ts, 2 * f):
            cols = slice(c0, c0 + cw)
            o_ref[r0:r0 + FFN_ROWS, cols] = _taps_sum(dtaps, w_ref, jnp.zeros((FFN_ROWS, cw), F32), r0, cols).astype(BF16)

    up_cur, up_prev = _seq_specs(dims, ts, 2 * f, FFN_HALO, 0)
    _, up_next = _seq_specs(dims, ts, 2 * f, FFN_HALO, 0, nxt=True)
    d_cur, d_next = _seq_specs(dims, ts, f, dact_halo, 0, nxt=True)
    full = lambda rows: pl.BlockSpec((rows, 2 * f), lambda b_, i: (0, 0))
    return pl.pallas_call(
        body, name=name, grid=(dims.batch_local, nst),
        in_specs=[d_cur, d_next, up_cur, up_prev, up_next, full(FFN_HALO), full(1)],
        out_specs=[pl.BlockSpec((ts, 2 * f), lambda b_, i: (b_ * nst + i, 0)), full(FFN_HALO), full(1)],
        out_shape=[jax.ShapeDtypeStruct((t, 2 * f), BF16), jax.ShapeDtypeStruct((FFN_HALO, 2 * f), F32),
                   jax.ShapeDtypeStruct((1, 2 * f), F32)],
        scratch_shapes=[pltpu.VMEM((ts + 2 * FFN_HALO, 2 * f), F32),
                        pltpu.VMEM((_moved_copies(fwd_offsets), ts + FFN_HALO, 2 * f), F32),
                        pltpu.VMEM((ts + FFN_HALO, 2 * f), F32),
                        pltpu.VMEM((_moved_copies(bwd_offsets), ts, 2 * f), F32)],
        compiler_params=_params("arbitrary", "arbitrary"),
    )(dact, dact, up, up, up, w, b)


def _ffn_act_fwd(up, w, b, dims, *, name, ts=128):
    t, f, kw = up.shape[0], dims.d_ff, dims.ffn_conv_width
    offsets = [FFN_HALO - (kw - 1) + k for k in range(kw)]

    def body(up_ref, h_ref, w_ref, b_ref, o_ref, buf, moved):
        buf[FFN_HALO:, :] = up_ref[...]
        buf[0:FFN_HALO, :] = jnp.where(pl.program_id(1) > 0, h_ref[...], 0.0)
        taps = _tap_sources(buf, moved, offsets, ts)
        for r0, c0, cw in _ffn_chunks(ts, f):
            vcols, gcols = slice(c0, c0 + cw), slice(f + c0, f + c0 + cw)
            uv = _taps_sum(taps, w_ref, jnp.broadcast_to(b_ref[:, vcols], (FFN_ROWS, cw)), r0, vcols)
            ug = _taps_sum(taps, w_ref, jnp.broadcast_to(b_ref[:, gcols], (FFN_ROWS, cw)), r0, gcols)
            o_ref[r0:r0 + FFN_ROWS, vcols] = (ug * _sigmoid(ug) * uv).astype(BF16)

    full = lambda rows: pl.BlockSpec((rows, 2 * f), lambda b_, i: (0, 0))
    return pl.pallas_call(
        body, name=name, grid=(dims.batch_local, dims.seq // ts),
        in_specs=[*_seq_specs(dims, ts, 2 * f, FFN_HALO, 0), full(FFN_HALO), full(1)],
        out_specs=pl.BlockSpec((ts, f), lambda b_, i: (b_ * (dims.seq // ts) + i, 0)),
        out_shape=jax.ShapeDtypeStruct((t, f), BF16),
        scratch_shapes=[pltpu.VMEM((FFN_HALO + ts, 2 * f), F32), pltpu.VMEM((_moved_copies(offsets), ts, 2 * f), F32)],
        compiler_params=_params("parallel", "parallel"),
    )(up, up, w, b)


def _dot_nt(a, b):
    return lax.dot_general(a, b, (((1,), (1,)), ((), ())), preferred_element_type=F32)


def _dot_tn(a, b):
    return lax.dot_general(a, b, (((0,), (0,)), ((), ())), preferred_element_type=F32)


LANES = 128
MASK_BIAS = 1e30
RESIDUE_DILATIONS = tuple(d for d in DILATIONS if d > 1)


def _rows_to_residues(value, out_ref, scr, d):
    rows, width = value.shape
    for c in range(width // LANES):
        cols = slice(LANES * c, LANES * (c + 1))
        scr[c] = value[:, cols]
        for r in range(d):
            out_ref[r, :, cols] = scr[c, pl.ds(r, rows // d, stride=d), :].astype(out_ref.dtype)


def _residues_to_rows(in_ref, scr, d):
    _, n, width = in_ref.shape
    slabs = []
    for c in range(width // LANES):
        cols = slice(LANES * c, LANES * (c + 1))
        for r in range(d):
            scr[c, pl.ds(r, n, stride=d), :] = in_ref[r, :, cols].astype(F32)
        slabs.append(scr[c])
    return slabs[0] if len(slabs) == 1 else jnp.concatenate(slabs, axis=1)


def _residue_shape(dims, d, width, dtype):
    return jax.ShapeDtypeStruct((dims.batch_local, d, dims.seq // d, width), dtype)


def _residue_spec(dims, d, tr, width):
    tiles = dims.seq // tr
    return pl.BlockSpec((None, d, tr // d, width), lambda i: (i // tiles, 0, i % tiles, 0))


def _head_sum_matrix(dims):
    a = dims.n_heads * dims.head_dim
    head = jnp.arange(a, dtype=jnp.int32) // dims.head_dim
    return (head[:, None] == jnp.arange(LANES, dtype=jnp.int32)[None, :]).astype(BF16)


def _two_pass_dot(v, m):
    hi = v.astype(BF16)
    lo = (v - hi.astype(F32)).astype(BF16)
    return jnp.dot(hi, m, preferred_element_type=F32) + jnp.dot(lo, m, preferred_element_type=F32)


def _residue_permutations(tr):
    out = []
    for d in RESIDUE_DILATIONS:
        dst = jnp.arange(tr, dtype=jnp.int32)
        src = d * (dst % (tr // d)) + dst // (tr // d)
        out.append((src[:, None] == jnp.arange(tr, dtype=jnp.int32)[None, :]).astype(BF16))
    return out


def _bf16_rows_to_residues(value, out_ref, perm_ref, d):
    n = value.shape[0] // d
    moved = jnp.dot(perm_ref[...], value, preferred_element_type=F32).astype(out_ref.dtype)
    for r in range(d):
        out_ref[r] = moved[r * n:(r + 1) * n]


def _bf16_residues_to_rows(in_ref, back_ref):
    d = in_ref.shape[0]
    stacked = jnp.concatenate([in_ref[r] for r in range(d)], axis=0)
    return jnp.dot(back_ref[...], stacked, preferred_element_type=F32)


def _qkv_layouts_fwd(z, gq, gk, head_ones, dims, *, name, tr=256):
    t = z.shape[0]
    a = dims.n_heads * dims.head_dim
    q_scale = dims.head_dim ** -0.5
    nres = len(RESIDUE_DILATIONS)

    def body(q_ref, k_ref, v_ref, gq_ref, gk_ref, sum_ref, spread_ref, *rest):
        perm_refs, outs = rest[:nres], rest[nres:]
        qv, kv = q_ref[...].astype(F32), k_ref[...].astype(F32)
        mean = lambda val: _two_pass_dot(_two_pass_dot(val, sum_ref[...]), spread_ref[...]) * (1.0 / dims.head_dim)
        rq = lax.rsqrt(mean(qv * qv) + RMS_EPS)
        rk = lax.rsqrt(mean(kv * kv) + RMS_EPS)
        values = ((qv * rq * gq_ref[...] * q_scale).astype(BF16), (kv * rk * gk_ref[...]).astype(BF16), v_ref[...])
        for j, val in enumerate(values):
            outs[j][...] = val
            for g, d in enumerate(RESIDUE_DILATIONS):
                _bf16_rows_to_residues(val, outs[3 * (g + 1) + j], perm_refs[g], d)

    out_specs = [_row_spec(tr, a)] * 3
    out_shape = [jax.ShapeDtypeStruct((t, a), BF16)] * 3
    for d in RESIDUE_DILATIONS:
        out_specs += [_residue_spec(dims, d, tr, a)] * 3
        out_shape += [_residue_shape(dims, d, a, BF16)] * 3
    outs = pl.pallas_call(
        body, name=name, grid=(t // tr,),
        in_specs=[_row_spec(tr, a, 2), _row_spec(tr, a, 3), _row_spec(tr, a, 4), _vec_spec(a), _vec_spec(a),
                  pl.BlockSpec((a, LANES), lambda i: (0, 0)), pl.BlockSpec((LANES, a), lambda i: (0, 0))]
        + [pl.BlockSpec((tr, tr), lambda i: (0, 0))] * nres,
        out_specs=out_specs, out_shape=out_shape,
        compiler_params=_params("parallel"),
    )(z, z, z, gq, gk, *head_ones, *_residue_permutations(tr))
    return {d: tuple(outs[3 * g:3 * g + 3]) for g, d in enumerate((1,) + RESIDUE_DILATIONS)}


ATTN_RESIDUES_PER_STEP = 4
ATTN_RESIDUES_PER_STEP_WINDOWED = 2


def _attn_groups(dims, dil):
    return (dims.batch_local, dil) if dil > 1 else (1, dims.batch_local)


def _attn_array(x, dims, dil):
    return x if dil > 1 else x.reshape(1, dims.batch_local, dims.seq, x.shape[-1])


def _attn_residues(dims, dil):
    one_block = dims.seq // dil == ATTN_BLOCK
    return math.gcd(_attn_groups(dims, dil)[1], ATTN_RESIDUES_PER_STEP if one_block else ATTN_RESIDUES_PER_STEP_WINDOWED)


def _per_residue(body, rs):
    if rs == 1:
        return body

    def stepped(*refs):
        for r in range(rs):
            body(*[ref.at[r] for ref in refs])

    return stepped


def _attn_specs(dims, dil, width):
    blk = ATTN_BLOCK
    nb = dims.seq // dil // blk
    rs = _attn_residues(dims, dil)
    lead, groups = _attn_groups(dims, dil)
    if rs > 1 and nb == 1:
        grid = (lead, groups // rs)
        at = lambda f: pl.BlockSpec((None, rs, blk, width), lambda b, r: (b, r, 0, 0))
    elif rs > 1:
        grid = (lead, groups // rs, nb)
        at = lambda f: pl.BlockSpec((None, rs, blk, width), lambda b, r, i: (b, r, f(i), 0))
    else:
        grid = (lead, groups, nb)
        at = lambda f: pl.BlockSpec((None, None, blk, width), lambda b, r, i: (b, r, f(i), 0))
    return grid, at(lambda i: i), at(lambda i: jnp.maximum(i - 1, 0)), at(lambda i: jnp.minimum(i + 1, nb - 1))


def _head_slopes(n_heads):
    h = lax.broadcasted_iota(jnp.int32, (n_heads, 1, 1), 0).astype(F32)
    return jnp.exp((h + 1.0) * (-8.0 / n_heads * math.log(2.0)))


def _pair_masks(hd):
    low = lax.broadcasted_iota(jnp.int32, (1, 2 * hd), 1) < hd
    return low, jnp.logical_not(low)


def _attn_fwd(q, k, v, dims, dil, *, name):
    a = dims.n_heads * dims.head_dim
    heads, hd, blk = dims.n_heads, dims.head_dim, ATTN_BLOCK
    assert 2 * hd == LANES and heads % 2 == 0 and heads <= LANES
    nb = dims.seq // dil // blk
    has_prev = nb > 1
    nkeys = 2 * blk if has_prev else blk
    grid, cur, prev, _ = _attn_specs(dims, dil, a)
    _, cur_stat, _, _ = _attn_specs(dims, dil, LANES)

    def body(*refs):
        if has_prev:
            q_ref, kc_ref, vc_ref, kp_ref, vp_ref, o_ref, lse_ref, s_scr, p_scr, k_st, v_st = refs
            k_st[0:blk, :], k_st[blk:, :] = kp_ref[...], kc_ref[...]
            v_st[0:blk, :], v_st[blk:, :] = vp_ref[...], vc_ref[...]
        else:
            q_ref, k_st, v_st, o_ref, lse_ref, s_scr, p_scr = refs
        low, high = _pair_masks(hd)

        for hp in range(heads // 2):
            sl = slice(LANES * hp, LANES * (hp + 1))
            q2 = q_ref[:, sl]
            kcat = k_st[:, sl]
            s_scr[2 * hp] = _dot_nt(jnp.where(low, q2, jnp.zeros_like(q2)), kcat)
            s_scr[2 * hp + 1] = _dot_nt(jnp.where(high, q2, jnp.zeros_like(q2)), kcat)

        iq = lax.broadcasted_iota(jnp.int32, (blk, nkeys), 0)
        jk = lax.broadcasted_iota(jnp.int32, (blk, nkeys), 1)
        if has_prev:
            steps = iq + blk - jk
            valid = (steps >= 0) & (steps <= blk) & ((jk >= blk) | (pl.program_id(len(grid) - 1) > 0))
        else:
            steps = iq - jk
            valid = steps >= 0
        bias = jnp.where(valid, steps.astype(F32) * (-float(dil)), -MASK_BIAS)
        s = s_scr[...] + _head_slopes(heads) * bias[None]
        m = jnp.max(s, axis=-1, keepdims=True)
        p = jnp.exp(s - m)
        l = jnp.sum(p, axis=-1, keepdims=True)
        p_scr[...] = p.astype(BF16)
        inv = 1.0 / l
        lse = m + jnp.log(l)

        lane = lax.broadcasted_iota(jnp.int32, (blk, LANES), 1)
        stat = jnp.zeros((blk, LANES), F32)
        for hp in range(heads // 2):
            sl = slice(LANES * hp, LANES * (hp + 1))
            vcat = v_st[:, sl]
            pv_a = jnp.dot(p_scr[2 * hp], vcat, preferred_element_type=F32) * inv[2 * hp]
            pv_b = jnp.dot(p_scr[2 * hp + 1], vcat, preferred_element_type=F32) * inv[2 * hp + 1]
            o_ref[:, sl] = jnp.where(low, pv_a, pv_b).astype(BF16)
            stat = jnp.where(lane == 2 * hp, lse[2 * hp], stat)
            stat = jnp.where(lane == 2 * hp + 1, lse[2 * hp + 1], stat)
        lse_ref[...] = stat

    q4, k4, v4 = (_attn_array(x, dims, dil) for x in (q, k, v))
    rs = _attn_residues(dims, dil)
    per_step = lambda shape: shape if rs == 1 else (rs,) + shape
    o, lse = pl.pallas_call(
        _per_residue(body, rs), name=name, grid=grid,
        in_specs=[cur, cur, cur] + ([prev, prev] if has_prev else []),
        out_specs=[cur, cur_stat],
        out_shape=[jax.ShapeDtypeStruct(q4.shape, BF16), jax.ShapeDtypeStruct(q4.shape[:-1] + (LANES,), F32)],
        scratch_shapes=[pltpu.VMEM(per_step((heads, blk, nkeys)), F32), pltpu.VMEM(per_step((heads, blk, nkeys)), BF16)]
        + ([pltpu.VMEM(per_step((nkeys, a)), BF16)] * 2 if has_prev else []),
        compiler_params=_params(*["parallel"] * len(grid)),
    )(q4, k4, v4, *([k4, v4] if has_prev else []))
    return o.reshape(q.shape), lse.reshape(q.shape[:-1] + (LANES,))


def _attn_combine(groups, head_spread, dims, *, name, tr=256):
    t = dims.tokens
    a = dims.n_heads * dims.head_dim
    dils = tuple(groups)

    nres = len(RESIDUE_DILATIONS)

    def body(*refs):
        ins = refs[:2 * len(dils)]
        x_ref = refs[2 * len(dils)]
        back_refs = dict(zip(RESIDUE_DILATIONS, refs[2 * len(dils) + 1:2 * len(dils) + 1 + nres]))
        o_ref = refs[2 * len(dils) + 1 + nres]
        lse_refs = refs[2 * len(dils) + 2 + nres:-1]
        scr_stat = refs[-1]
        outs, stats = [], []
        for g, d in enumerate(dils):
            if d == 1:
                outs.append(ins[2 * g][...].astype(F32))
                stats.append(ins[2 * g + 1][...])
            else:
                outs.append(_bf16_residues_to_rows(ins[2 * g], back_refs[d]))
                stats.append(_residues_to_rows(ins[2 * g + 1], scr_stat, d))
        top = functools.reduce(jnp.maximum, stats)
        weights = [jnp.exp(s - top) for s in stats]
        total = functools.reduce(jnp.add, weights)
        joint = top + jnp.log(total)
        inv = 1.0 / total
        acc = None
        for w, o in zip(weights, outs):
            term = _two_pass_dot(w * inv, x_ref[...]) * o
            acc = term if acc is None else acc + term
        o_ref[...] = acc.astype(BF16)
        for g, d in enumerate(dils):
            if d == 1:
                lse_refs[g][...] = joint
            else:
                _rows_to_residues(joint, lse_refs[g], scr_stat, d)

    in_specs, args, lse_specs, lse_shapes = [], [], [], []
    for d in dils:
        if d == 1:
            in_specs += [_row_spec(tr, a), _row_spec(tr, LANES)]
            lse_specs.append(_row_spec(tr, LANES))
            lse_shapes.append(jax.ShapeDtypeStruct((t, LANES), F32))
        else:
            in_specs += [_residue_spec(dims, d, tr, a), _residue_spec(dims, d, tr, LANES)]
            lse_specs.append(_residue_spec(dims, d, tr, LANES))
            lse_shapes.append(_residue_shape(dims, d, LANES, F32))
        args += list(groups[d])
    outs = pl.pallas_call(
        body, name=name, grid=(t // tr,),
        in_specs=in_specs + [pl.BlockSpec((LANES, a), lambda i: (0, 0))] + [pl.BlockSpec((tr, tr), lambda i: (0, 0))] * nres,
        out_specs=[_row_spec(tr, a)] + lse_specs,
        out_shape=[jax.ShapeDtypeStruct((t, a), BF16)] + lse_shapes,
        scratch_shapes=[pltpu.VMEM((1, tr, LANES), F32)],
        compiler_params=_params("parallel"),
    )(*args, head_spread, *[jnp.transpose(p) for p in _residue_permutations(tr)])
    return outs[0], dict(zip(dils, outs[1:]))


def _attn_bwd_prep(dy, w, o, head_sum, dims, *, name, after=None, tr=256):
    t, a = o.shape
    nres = len(RESIDUE_DILATIONS)

    def body(dy_ref, w_ref, o_ref, e_ref, *rest):
        perm_refs, outs, scr_stat = rest[:nres], rest[nres:-1], rest[-1]
        do = _dot_nt(dy_ref[...], w_ref[...]).astype(BF16)
        outs[0][...] = do
        delta = _two_pass_dot(do.astype(F32) * o_ref[...].astype(F32), e_ref[...])
        outs[1][...] = delta
        for g, d in enumerate(RESIDUE_DILATIONS):
            _bf16_rows_to_residues(do, outs[2 + 2 * g], perm_refs[g], d)
            _rows_to_residues(delta, outs[3 + 2 * g], scr_stat, d)

    out_specs = [_row_spec(tr, a), _row_spec(tr, LANES)]
    out_shape = [jax.ShapeDtypeStruct((t, a), BF16), jax.ShapeDtypeStruct((t, LANES), F32)]
    for d in RESIDUE_DILATIONS:
        out_specs += [_residue_spec(dims, d, tr, a), _residue_spec(dims, d, tr, LANES)]
        out_shape += [_residue_shape(dims, d, a, BF16), _residue_shape(dims, d, LANES, F32)]
    n_in = 4 + nres
    body, more_specs, more_args = _ordered(body, n_in, after)
    outs = pl.pallas_call(
        body, name=name, grid=(t // tr,),
        in_specs=[_row_spec(tr, dy.shape[1]), pl.BlockSpec((None,) + w.shape[1:], lambda i: (0, 0, 0)), _row_spec(tr, a),
                  pl.BlockSpec((a, LANES), lambda i: (0, 0))] + [pl.BlockSpec((tr, tr), lambda i: (0, 0))] * nres + more_specs,
        out_specs=out_specs, out_shape=out_shape,
        scratch_shapes=[pltpu.VMEM((1, tr, LANES), F32)],
        compiler_params=_params("parallel"),
    )(dy, w, o, head_sum, *_residue_permutations(tr), *more_args)
    dos, deltas = {1: outs[0]}, {1: outs[1]}
    for g, d in enumerate(RESIDUE_DILATIONS):
        dos[d], deltas[d] = outs[2 + 2 * g], outs[3 + 2 * g]
    return dos, deltas


def _attn_bwd(q, k, v, do, lse, delta, dims, dil, *, name):
    a = dims.n_heads * dims.head_dim
    heads, hd, blk = dims.n_heads, dims.head_dim, ATTN_BLOCK
    nb = dims.seq // dil // blk
    has_next = nb > 1
    nq = 2 * blk if has_next else blk
    grid, cur, _, nxt = _attn_specs(dims, dil, a)
    _, cur_stat, _, nxt_stat = _attn_specs(dims, dil, LANES)

    def body(*refs):
        k_ref, v_ref, q_ref, do_ref, lse_ref, dl_ref = refs[:6]
        if has_next:
            qn_ref, don_ref, lsen_ref, dln_ref = refs[6:10]
            dq_ref, dk_ref, dv_ref, q_st, do_st, s_scr, dp_scr, p_scr, ds_scr, carry = refs[10:]
        else:
            dq_ref, dk_ref, dv_ref, q_st, do_st, s_scr, dp_scr, p_scr, ds_scr = refs[6:]
        j = pl.program_id(len(grid) - 1)
        low, high = _pair_masks(hd)
        q_st[0:blk, :] = q_ref[...]
        do_st[0:blk, :] = do_ref[...]
        if has_next:
            q_st[blk:, :] = qn_ref[...]
            do_st[blk:, :] = don_ref[...]
            lse_all = jnp.concatenate([lse_ref[...], lsen_ref[...]], axis=0)
            dl_all = jnp.concatenate([dl_ref[...], dln_ref[...]], axis=0)
        else:
            lse_all, dl_all = lse_ref[...], dl_ref[...]
        lse_t, dl_t = jnp.transpose(lse_all), jnp.transpose(dl_all)
        lse3 = jnp.stack([lse_t[h:h + 1, :] for h in range(heads)])
        dl3 = jnp.stack([dl_t[h:h + 1, :] for h in range(heads)])

        def halves(x):
            return jnp.where(low, x, jnp.zeros_like(x)), jnp.where(high, x, jnp.zeros_like(x))

        for hp in range(heads // 2):
            sl = slice(LANES * hp, LANES * (hp + 1))
            k2, v2 = k_ref[:, sl], v_ref[:, sl]
            q_a, q_b = halves(q_st[:, sl])
            do_a, do_b = halves(do_st[:, sl])
            s_scr[2 * hp], s_scr[2 * hp + 1] = _dot_nt(k2, q_a), _dot_nt(k2, q_b)
            dp_scr[2 * hp], dp_scr[2 * hp + 1] = _dot_nt(v2, do_a), _dot_nt(v2, do_b)

        jk = lax.broadcasted_iota(jnp.int32, (blk, nq), 0)
        rq = lax.broadcasted_iota(jnp.int32, (blk, nq), 1)
        if has_next:
            iq = jnp.where(rq < blk, rq, rq - blk)
            steps = jnp.where(rq < blk, iq - jk, iq - jk + blk)
            valid = ((rq < blk) & (iq >= jk)) | ((rq >= blk) & (jk >= iq) & (j + 1 < nb))
        else:
            steps, valid = rq - jk, rq >= jk
        bias = jnp.where(valid, steps.astype(F32) * (-float(dil)), -MASK_BIAS)
        p = jnp.exp(s_scr[...] + _head_slopes(heads) * bias[None] - lse3)
        p_scr[...] = p.astype(BF16)
        ds_scr[...] = (p * (dp_scr[...] - dl3)).astype(BF16)

        if has_next:
            @pl.when(j == 0)
            def _():
                carry[...] = jnp.zeros_like(carry)

        for hp in range(heads // 2):
            sl = slice(LANES * hp, LANES * (hp + 1))
            k2 = k_ref[:, sl]
            q_a, q_b = halves(q_st[:, sl])
            do_a, do_b = halves(do_st[:, sl])
            ds_a, ds_b = ds_scr[2 * hp], ds_scr[2 * hp + 1]
            dk_ref[:, sl] = (jnp.dot(ds_a, q_a, preferred_element_type=F32)
                             + jnp.dot(ds_b, q_b, preferred_element_type=F32)).astype(BF16)
            dv_ref[:, sl] = (jnp.dot(p_scr[2 * hp], do_a, preferred_element_type=F32)
                             + jnp.dot(p_scr[2 * hp + 1], do_b, preferred_element_type=F32)).astype(BF16)
            dq2 = jnp.where(low, _dot_tn(ds_a, k2), _dot_tn(ds_b, k2))
            if has_next:
                dq_ref[:, sl] = (carry[:, sl] + dq2[:blk]).astype(BF16)
                carry[:, sl] = dq2[blk:]
            else:
                dq_ref[:, sl] = dq2.astype(BF16)

    args, in_specs = [_attn_array(x, dims, dil) for x in (k, v, q, do, lse, delta)], [cur] * 4 + [cur_stat] * 2
    if has_next:
        args += [args[2], args[3], args[4], args[5]]
        in_specs += [nxt] * 2 + [nxt_stat] * 2
    shape = jax.ShapeDtypeStruct(args[2].shape, BF16)
    rs = _attn_residues(dims, dil)
    per_step = lambda dims_: dims_ if rs == 1 else (rs,) + dims_
    scratch = ([pltpu.VMEM(per_step((nq, a)), BF16)] * 2 + [pltpu.VMEM(per_step((heads, blk, nq)), F32)] * 2
               + [pltpu.VMEM(per_step((heads, blk, nq)), BF16)] * 2)
    if has_next:
        scratch.append(pltpu.VMEM(per_step((blk, a)), F32))
    grads = pl.pallas_call(
        _per_residue(body, rs), name=name, grid=grid, in_specs=in_specs, out_specs=[cur] * 3, out_shape=[shape] * 3,
        scratch_shapes=scratch,
        compiler_params=_params(*["parallel"] * (len(grid) - 1), "arbitrary"),
    )(*args)
    return tuple(g.reshape(q.shape) for g in grads)


def _qkv_layouts_bwd(z, grads, gq, gk, head_ones, dims, *, name, tr=256):
    t = z.shape[0]
    a = dims.n_heads * dims.head_dim
    q_scale = dims.head_dim ** -0.5
    dils = tuple(grads)
    nres = len(RESIDUE_DILATIONS)

    def body(q_ref, k_ref, *rest):
        d_refs = rest[:3 * len(dils)]
        gq_ref, gk_ref, sum_ref, spread_ref = rest[3 * len(dils):3 * len(dils) + 4]
        back_refs = dict(zip(RESIDUE_DILATIONS, rest[3 * len(dils) + 4:3 * len(dils) + 4 + nres]))
        dz_ref, dgq_ref, dgk_ref = rest[3 * len(dils) + 4 + nres:]
        first = pl.program_id(0) == 0
        mean = lambda val: _two_pass_dot(_two_pass_dot(val, sum_ref[...]), spread_ref[...]) * (1.0 / dims.head_dim)

        def total(j):
            acc = None
            for g, d in enumerate(dils):
                ref = d_refs[3 * g + j]
                part = ref[...].astype(F32) if d == 1 else _bf16_residues_to_rows(ref, back_refs[d])
                acc = part if acc is None else acc + part
            return acc

        def norm_bwd(x_ref, dy, g_ref, scale, col, dg_ref):
            xv = x_ref[...].astype(F32)
            dy = dy * scale
            r = lax.rsqrt(mean(xv * xv) + RMS_EPS)
            gy = dy * g_ref[...]
            dx = r * gy - xv * (r * r * r) * mean(xv * gy)
            dz_ref[:, col * a:(col + 1) * a] = dx.astype(BF16)
            _accumulate(dg_ref, jnp.sum(dy * xv * r, axis=0, keepdims=True), first)

        norm_bwd(q_ref, total(0), gq_ref, q_scale, 0, dgq_ref)
        norm_bwd(k_ref, total(1), gk_ref, 1.0, 1, dgk_ref)
        dz_ref[:, 2 * a:3 * a] = total(2).astype(BF16)

    in_specs, args = [_row_spec(tr, a, 2), _row_spec(tr, a, 3)], [z, z]
    for d in dils:
        in_specs += [_row_spec(tr, a) if d == 1 else _residue_spec(dims, d, tr, a)] * 3
        args += list(grads[d])
    in_specs += [_vec_spec(a), _vec_spec(a), pl.BlockSpec((a, LANES), lambda i: (0, 0)),
                 pl.BlockSpec((LANES, a), lambda i: (0, 0))] + [pl.BlockSpec((tr, tr), lambda i: (0, 0))] * nres
    return pl.pallas_call(
        body, name=name, grid=(t // tr,), in_specs=in_specs,
        out_specs=[_row_spec(tr, 3 * a), _vec_spec(a), _vec_spec(a)],
        out_shape=[jax.ShapeDtypeStruct((t, 3 * a), BF16)] + [jax.ShapeDtypeStruct((1, a), F32)] * 2,
        compiler_params=_params("arbitrary"),
    )(*args, gq, gk, *head_ones, *[jnp.transpose(p) for p in _residue_permutations(tr)])


def _mix_fwd(a3, o, w_a, w_b, w_out, z, gate_b, x, g2, dims, *, name, tr=512):
    t, d = x.shape
    tr = _pick(t, tr, 8)
    first_gate_col = z.shape[1] // d - 2

    def body(a_ref, o_ref, wa_ref, wb_ref, wo_ref, ga_ref, gb_ref, ba_ref, bb_ref, x_ref, g2_ref,
             ya_ref, yb_ref, mix_ref, x1_ref, h2_ref):
        ya = jnp.dot(a_ref[...], wa_ref[...], preferred_element_type=F32)
        yb = jnp.dot(o_ref[...], wb_ref[...], preferred_element_type=F32)
        ya_ref[...] = ya
        yb_ref[...] = yb
        g_a = _sigmoid(ga_ref[...].astype(F32) + ba_ref[...])
        g_b = _sigmoid(gb_ref[...].astype(F32) + bb_ref[...])
        mixed = (g_a * ya + g_b * yb).astype(BF16)
        mix_ref[...] = mixed
        x1 = x_ref[...] + jnp.dot(mixed, wo_ref[...], preferred_element_type=F32)
        x1_ref[...] = x1
        h2_ref[...] = (x1 * lax.rsqrt(jnp.mean(x1 * x1, axis=-1, keepdims=True) + RMS_EPS) * g2_ref[...]).astype(BF16)

    weight = pl.BlockSpec((None, d, d), lambda i: (0, 0, 0))
    rows = _row_spec(tr, d)
    return pl.pallas_call(
        body, name=name, grid=(t // tr,),
        in_specs=[rows, rows, weight, weight, weight, _row_spec(tr, d, first_gate_col),
                  _row_spec(tr, d, first_gate_col + 1), _vec_spec(d, 0), _vec_spec(d, 1), rows, _vec_spec(d)],
        out_specs=[rows] * 5,
        out_shape=[jax.ShapeDtypeStruct((t, d), dt) for dt in (F32, F32, BF16, F32, BF16)],
        compiler_params=_params("parallel"),
    )(a3, o, w_a, w_b, w_out, z, z, gate_b, gate_b, x, g2)


def _mix_bwd(dx, w, ya, yb, z, gate_b, dims, *, name, after=None, tr=512):
    t, d = ya.shape
    tr = _pick(t, tr, 8)
    first_gate_col = z.shape[1] // d - 2

    def body(dx_ref, w_ref, ya_ref, yb_ref, ga_ref, gb_ref, ba_ref, bb_ref, dya_ref, dyb_ref, dz_ref, db_ref):
        dm = _dot_nt(dx_ref[...], w_ref[...])
        g_a = _sigmoid(ga_ref[...].astype(F32) + ba_ref[...])
        g_b = _sigmoid(gb_ref[...].astype(F32) + bb_ref[...])
        dya_ref[...] = (dm * g_a).astype(BF16)
        dyb_ref[...] = (dm * g_b).astype(BF16)
        dl_a = dm * ya_ref[...] * g_a * (1.0 - g_a)
        dl_b = dm * yb_ref[...] * g_b * (1.0 - g_b)
        dz_ref[:, 0:d] = dl_a.astype(BF16)
        dz_ref[:, d:2 * d] = dl_b.astype(BF16)
        first = pl.program_id(0) == 0
        sums = jnp.concatenate([jnp.sum(dl_a, axis=0, keepdims=True), jnp.sum(dl_b, axis=0, keepdims=True)], axis=1)
        _accumulate(db_ref, sums, first)

    body, more_specs, more_args = _ordered(body, 8, after)
    return pl.pallas_call(
        body, name=name, grid=(t // tr,),
        in_specs=[_row_spec(tr, d), pl.BlockSpec((None, d, d), lambda i: (0, 0, 0)), _row_spec(tr, d), _row_spec(tr, d),
                  _row_spec(tr, d, first_gate_col), _row_spec(tr, d, first_gate_col + 1), _vec_spec(d, 0),
                  _vec_spec(d, 1)] + more_specs,
        out_specs=[_row_spec(tr, d), _row_spec(tr, d), _row_spec(tr, 2 * d), _vec_spec(2 * d)],
        out_shape=[jax.ShapeDtypeStruct((t, d), BF16)] * 2 + [jax.ShapeDtypeStruct((t, 2 * d), BF16),
                                                              jax.ShapeDtypeStruct((1, 2 * d), F32)],
        compiler_params=_params("arbitrary"),
    )(dx, w, ya, yb, z, z, gate_b, gate_b, *more_args)


def _adamw(w, grads, m, v, *, name, tr=256):
    r, c = w.shape
    tr = _pick(r, tr, 8)
    ng = len(grads)
    c1 = 1.0 - ADAM_B1 ** ADAM_STEP
    c2 = 1.0 - ADAM_B2 ** ADAM_STEP

    def body(*refs):
        w_ref, g_refs, m_ref, v_ref = refs[0], refs[1:1 + ng], refs[1 + ng], refs[2 + ng]
        g_out, d_out, m_out, v_out = refs[3 + ng:]
        g = g_refs[0][...]
        for extra in g_refs[1:]:
            g = g + extra[...]
        m_new = ADAM_B1 * m_ref[...] + (1.0 - ADAM_B1) * g
        v_new = ADAM_B2 * v_ref[...] + (1.0 - ADAM_B2) * (g * g)
        g_out[...] = g
        m_out[...] = m_new
        v_out[...] = v_new
        d_out[...] = -ADAM_LR * ((m_new / c1) / (jnp.sqrt(v_new / c2) + ADAM_EPS) + ADAM_WD * w_ref[...])

    spec = pl.BlockSpec((tr, c), lambda i: (i, 0))
    return pl.pallas_call(
        body, name=name, grid=(r // tr,),
        in_specs=[spec] * (3 + ng), out_specs=[spec] * 4, out_shape=[jax.ShapeDtypeStruct((r, c), F32)] * 4,
        compiler_params=_params("parallel"),
    )(w, *grads, m, v)


def _adamw_unpacked(w, g, m, v, shapes, *, name):
    r, d = w.shape
    c1 = 1.0 - ADAM_B1 ** ADAM_STEP
    c2 = 1.0 - ADAM_B2 ** ADAM_STEP
    views = [tuple(s) if len(s) == 2 else (1, s[0]) for s in shapes]
    n = len(views)

    def body(w_ref, g_ref, m_ref, v_ref, *refs):
        outs, scr = refs[:-1], refs[-1]
        grad = g_ref[...]
        m_new = ADAM_B1 * m_ref[...] + (1.0 - ADAM_B1) * grad
        v_new = ADAM_B2 * v_ref[...] + (1.0 - ADAM_B2) * (grad * grad)
        scr[0] = grad
        scr[1] = -ADAM_LR * ((m_new / c1) / (jnp.sqrt(v_new / c2) + ADAM_EPS) + ADAM_WD * w_ref[...])
        scr[2] = m_new
        scr[3] = v_new
        for kind in range(4):
            row = 0
            for i, (rows, width) in enumerate(views):
                out = outs[kind * n + i]
                for a in range(rows):
                    pos = 0
                    while pos < width:
                        at, lane = row + (a * width + pos) // d, (a * width + pos) % d
                        piece = min(width - pos, d - lane)
                        out[a:a + 1, pos:pos + piece] = scr[kind, at:at + 1, lane:lane + piece]
                        pos += piece
                row += _packed_rows(rows * width, d)

    whole = pl.BlockSpec(memory_space=pltpu.VMEM)
    outs = pl.pallas_call(
        body, name=name, in_specs=[whole] * 4, out_specs=[whole] * (4 * n),
        out_shape=[jax.ShapeDtypeStruct(view, F32) for _ in range(4) for view in views],
        scratch_shapes=[pltpu.VMEM((4, r, d), F32)], compiler_params=_params(),
    )(w, g, m, v)
    return [[o.reshape(s) for o, s in zip(outs[kind * n:(kind + 1) * n], shapes)] for kind in range(4)]


CHIP_PEERS = ((1, 0), (0, 1), (1, 1))


def _place():
    return lax.axis_index("x"), lax.axis_index("y"), lax.axis_index("c")


HBM = pl.BlockSpec(memory_space=pltpu.HBM)
SEM = pl.BlockSpec(memory_space=pltpu.SEMAPHORE)
IN_FLIGHT = pltpu.SideEffectType.DATAFLOW_SIDE_EFFECTING


def _in_hbm(a):
    return pltpu.with_memory_space_constraint(a, pltpu.HBM)


def _cast_to_lands(shards, dtypes, *, name, after=None):
    n = len(shards)

    def body(*refs):
        ins, outs, bufs, sems = refs[:n], refs[n:2 * n], refs[2 * n:3 * n], refs[3 * n]
        x, y, _ = _place()
        copies = []
        for a in range(n):
            bufs[a][...] = ins[a][...].astype(dtypes[a])
            cp = pltpu.make_async_copy(bufs[a], outs[a].at[2 * x + y], sems.at[a])
            cp.start()
            copies.append(cp)
        for cp in copies:
            cp.wait()

    body, more_specs, more_args = _ordered(body, n, after)
    return pl.pallas_call(
        body, name=name, in_specs=[pl.BlockSpec(memory_space=pltpu.VMEM)] * n + more_specs, out_specs=[ANY] * n,
        out_shape=[jax.ShapeDtypeStruct((N_CHIPS,) + s.shape, dt) for s, dt in zip(shards, dtypes)],
        scratch_shapes=[pltpu.VMEM(s.shape, dt) for s, dt in zip(shards, dtypes)] + [pltpu.SemaphoreType.DMA((n,))],
        compiler_params=pltpu.CompilerParams(vmem_limit_bytes=V7X_VMEM_LIMIT_BYTES),
    )(*shards, *more_args)


def _chip_copy(src, dst, send, recv, flip, place):
    x, y, c = place
    return pltpu.make_async_remote_copy(src_ref=src, dst_ref=dst, send_sem=send, recv_sem=recv,
                                        device_id=(x ^ flip[0], y ^ flip[1], c), device_id_type=MESH)


def _my_part(land, place, halved):
    block = land.at[2 * place[0] + place[1]]
    if not halved:
        return block
    rows = land.shape[1] // 2
    return block.at[pl.ds(pl.multiple_of(place[2] * rows, rows), rows)]


def _gather_start(lands, after, *, name, halved=()):
    n = len(lands)

    def body(*refs):
        ins, send, recv, token = refs[:n], refs[n + 1], refs[n + 2], refs[-1]
        place = _place()
        for a in range(n):
            part = _my_part(ins[a], place, a in halved)
            for p, flip in enumerate(CHIP_PEERS):
                k = 3 * a + p
                _chip_copy(part, part, send.at[k], recv.at[k], flip, place).start()
        token[...] = jnp.zeros_like(token)

    outs = pl.pallas_call(
        body, name=name, in_specs=[HBM] * n + [ANY],
        out_specs=(SEM, SEM, *[HBM] * n, pl.BlockSpec(memory_space=pltpu.VMEM)),
        out_shape=(pltpu.SemaphoreType.DMA((3 * n,)), pltpu.SemaphoreType.DMA((3 * n,)),
                   *[pltpu.HBM(l.shape, l.dtype) for l in lands], jax.ShapeDtypeStruct((8, 128), F32)),
        input_output_aliases={a: 2 + a for a in range(n)},
        compiler_params=pltpu.CompilerParams(has_side_effects=IN_FLIGHT),
    )(*[_in_hbm(l) for l in lands], after)
    return outs[0], outs[1], list(outs[2:2 + n]), outs[-1]


def _gather_wait(send, recv, lands, after, *, name, halved=()):
    n = len(lands)

    def body(*refs):
        ins, send_ref, recv_ref = refs[:n], refs[n], refs[n + 1]
        place = _place()
        for a in range(n):
            part = _my_part(ins[a], place, a in halved)
            for p, flip in enumerate(CHIP_PEERS):
                k = 3 * a + p
                cp = _chip_copy(part, part, send_ref.at[k], recv_ref.at[k], flip, place)
                cp.wait_send()
                cp.wait_recv()

    after = list(after) if isinstance(after, (list, tuple)) else [after]
    return pl.pallas_call(
        body, name=name, in_specs=[HBM] * n + [SEM, SEM] + [ANY] * len(after), out_specs=[HBM] * n,
        out_shape=[pltpu.HBM(l.shape, l.dtype) for l in lands],
        input_output_aliases={a: a for a in range(n)},
        compiler_params=pltpu.CompilerParams(has_side_effects=IN_FLIGHT),
    )(*lands, send, recv, *after)


def _forward_to_sibling(land, *, name):
    rows = land.shape[1] // 2

    def body(land_ref, out_ref, send, recv):
        x, y, c = _place()
        copies = []
        for p, (fx, fy) in enumerate(CHIP_PEERS):
            chip = 2 * (x ^ fx) + (y ^ fy)
            mine = pl.ds(pl.multiple_of(c * rows, rows), rows)
            theirs = pl.ds(pl.multiple_of((1 - c) * rows, rows), rows)
            out = pltpu.make_async_remote_copy(
                src_ref=land_ref.at[chip].at[mine], dst_ref=out_ref.at[chip].at[mine], send_sem=send.at[p],
                recv_sem=recv.at[p], device_id=(x, y, 1 - c), device_id_type=MESH)
            out.start()
            copies.append((out, pltpu.make_async_remote_copy(
                src_ref=land_ref.at[chip].at[theirs], dst_ref=out_ref.at[chip].at[theirs], send_sem=send.at[p],
                recv_sem=recv.at[p], device_id=(x, y, 1 - c), device_id_type=MESH)))
        for out, arriving in copies:
            out.wait_send()
            arriving.wait_recv()

    return pl.pallas_call(
        body, name=name, in_specs=[ANY], out_specs=ANY, out_shape=jax.ShapeDtypeStruct(land.shape, land.dtype),
        input_output_aliases={0: 0},
        scratch_shapes=[pltpu.SemaphoreType.DMA((3,)), pltpu.SemaphoreType.DMA((3,))],
    )(land)


def _scatter_start(grad, *, name):
    def body(g_ref, land_ref, send, recv, g_thru, land_thru, token):
        place = _place()
        for p, flip in enumerate(CHIP_PEERS):
            peer_chip = 2 * (place[0] ^ flip[0]) + (place[1] ^ flip[1])
            _chip_copy(g_ref.at[peer_chip], land_ref.at[p], send.at[p], recv.at[p], flip, place).start()
        token[...] = jnp.zeros_like(token)

    land = lax.empty((3,) + grad.shape[1:], grad.dtype)
    return pl.pallas_call(
        body, name=name, in_specs=[HBM, HBM],
        out_specs=(SEM, SEM, HBM, HBM, pl.BlockSpec(memory_space=pltpu.VMEM)),
        out_shape=(pltpu.SemaphoreType.DMA((3,)), pltpu.SemaphoreType.DMA((3,)), pltpu.HBM(grad.shape, grad.dtype),
                   pltpu.HBM(land.shape, land.dtype), jax.ShapeDtypeStruct((8, 128), F32)),
        input_output_aliases={0: 2, 1: 3},
        compiler_params=pltpu.CompilerParams(has_side_effects=IN_FLIGHT),
    )(_in_hbm(grad), _in_hbm(land))


def _scatter_wait(started, after, *, name):
    n = len(started)

    def body(*refs):
        grads, lands = refs[:n], refs[n:2 * n]
        sends, recvs = refs[2 * n:3 * n], refs[3 * n:4 * n]
        place = _place()
        for a in range(n):
            for p, flip in enumerate(CHIP_PEERS):
                cp = _chip_copy(grads[a].at[0], lands[a].at[p], sends[a].at[p], recvs[a].at[p], flip, place)
                cp.wait_send()
                cp.wait_recv()

    grads, lands = [s[2] for s in started], [s[3] for s in started]
    after = list(after) if isinstance(after, (list, tuple)) else [after]
    outs = pl.pallas_call(
        body, name=name, in_specs=[HBM] * (2 * n) + [SEM] * (2 * n) + [ANY] * len(after), out_specs=[HBM] * (2 * n),
        out_shape=[pltpu.HBM(a.shape, a.dtype) for a in grads + lands],
        input_output_aliases={a: a for a in range(2 * n)},
        compiler_params=pltpu.CompilerParams(has_side_effects=IN_FLIGHT),
    )(*grads, *lands, *[s[0] for s in started], *[s[1] for s in started], *after)
    return list(zip(outs[:n], outs[n:]))


def _sibling_copy(src, dst, send, recv, place):
    x, y, c = place
    return pltpu.make_async_remote_copy(src_ref=src, dst_ref=dst, send_sem=send, recv_sem=recv,
                                        device_id=(x, y, 1 - c), device_id_type=MESH)


def _swap_start(arrays, *, name):
    n = len(arrays)

    def body(*refs):
        ins, lands, send, recv, token = refs[:n], refs[n:2 * n], refs[2 * n], refs[2 * n + 1], refs[-1]
        place = _place()
        for a in range(n):
            _sibling_copy(ins[a], lands[a], send.at[a], recv.at[a], place).start()
        token[...] = jnp.zeros_like(token)

    both = [_in_hbm(a) for a in arrays] + [_in_hbm(lax.empty(a.shape, a.dtype)) for a in arrays]
    outs = pl.pallas_call(
        body, name=name, in_specs=[HBM] * (2 * n),
        out_specs=(SEM, SEM, *[HBM] * (2 * n), pl.BlockSpec(memory_space=pltpu.VMEM)),
        out_shape=(pltpu.SemaphoreType.DMA((n,)), pltpu.SemaphoreType.DMA((n,)),
                   *[pltpu.HBM(a.shape, a.dtype) for a in both], jax.ShapeDtypeStruct((8, 128), F32)),
        input_output_aliases={a: 2 + a for a in range(2 * n)},
        compiler_params=pltpu.CompilerParams(has_side_effects=IN_FLIGHT),
    )(*both)
    return outs[0], outs[1], list(outs[2:2 + n]), list(outs[2 + n:2 + 2 * n]), outs[-1]


def _swap_wait(started, after, *, name):
    send, recv, arrays, lands = started[:4]
    n = len(arrays)

    def body(*refs):
        ins, zones, send_ref, recv_ref = refs[:n], refs[n:2 * n], refs[2 * n], refs[2 * n + 1]
        place = _place()
        for a in range(n):
            cp = _sibling_copy(ins[a], zones[a], send_ref.at[a], recv_ref.at[a], place)
            cp.wait_send()
            cp.wait_recv()

    after = list(after) if isinstance(after, (list, tuple)) else [after]
    outs = pl.pallas_call(
        body, name=name, in_specs=[HBM] * (2 * n) + [SEM, SEM] + [ANY] * len(after), out_specs=[HBM] * (2 * n),
        out_shape=[pltpu.HBM(a.shape, a.dtype) for a in arrays + lands],
        input_output_aliases={a: a for a in range(2 * n)},
        compiler_params=pltpu.CompilerParams(has_side_effects=IN_FLIGHT),
    )(*arrays, *lands, send, recv, *after)
    return list(outs[:n]), list(outs[n:])


def _allreduce_start(packed, *, name):
    n_dev = 8

    def body(src_ref, land_ref, send, recv, src_thru, land_thru, token):
        x, y, c = _place()
        me = 4 * x + 2 * y + c
        for p in range(1, n_dev):
            pltpu.make_async_remote_copy(
                src_ref=src_ref, dst_ref=land_ref.at[me], send_sem=send.at[p - 1], recv_sem=recv.at[p - 1],
                device_id=(x ^ (p >> 2), y ^ ((p >> 1) & 1), c ^ (p & 1)), device_id_type=MESH).start()
        token[...] = jnp.zeros_like(token)

    land = lax.empty((n_dev,) + packed.shape, packed.dtype)
    return pl.pallas_call(
        body, name=name, in_specs=[HBM, HBM],
        out_specs=(SEM, SEM, HBM, HBM, pl.BlockSpec(memory_space=pltpu.VMEM)),
        out_shape=(pltpu.SemaphoreType.DMA((n_dev - 1,)), pltpu.SemaphoreType.DMA((n_dev - 1,)),
                   pltpu.HBM(packed.shape, packed.dtype), pltpu.HBM(land.shape, land.dtype),
                   jax.ShapeDtypeStruct((8, 128), F32)),
        input_output_aliases={0: 2, 1: 3},
        compiler_params=pltpu.CompilerParams(has_side_effects=IN_FLIGHT),
    )(_in_hbm(packed), _in_hbm(land))


def _allreduce_wait(started, after, *, name):
    send, recv, packed, land = started[:4]
    n_dev = 8

    def body(src_ref, land_ref, send_ref, recv_ref, *_):
        x, y, c = _place()
        for p in range(1, n_dev):
            cp = pltpu.make_async_remote_copy(
                src_ref=src_ref, dst_ref=land_ref.at[0], send_sem=send_ref.at[p - 1], recv_sem=recv_ref.at[p - 1],
                device_id=(x ^ (p >> 2), y ^ ((p >> 1) & 1), c ^ (p & 1)), device_id_type=MESH)
            cp.wait_send()
            cp.wait_recv()

    after = list(after) if isinstance(after, (list, tuple)) else [after]
    return pl.pallas_call(
        body, name=name, in_specs=[HBM, HBM, SEM, SEM] + [ANY] * len(after), out_specs=[HBM, HBM],
        out_shape=[pltpu.HBM(packed.shape, packed.dtype), pltpu.HBM(land.shape, land.dtype)],
        input_output_aliases={0: 0, 1: 1},
        compiler_params=pltpu.CompilerParams(has_side_effects=IN_FLIGHT),
    )(packed, land, send, recv, *after)


def _sum_devices(mine, land, *, name):
    n_dev = land.shape[0]

    def body(mine_ref, land_ref, out_ref):
        x, y, c = _place()
        me = 4 * x + 2 * y + c
        total = None
        for s in range(n_dev):
            part = jnp.where(me == s, mine_ref[...], land_ref[s])
            total = part if total is None else total + part
        out_ref[...] = total

    return pl.pallas_call(body, name=name, out_shape=jax.ShapeDtypeStruct(mine.shape, mine.dtype))(mine, land)


def _sum_received(grad, land, *, name, tr=256):
    _, r, c = grad.shape
    tr = _pick(r, tr, 8)

    def body(chip_ref, g_ref, l_ref, o_ref):
        o_ref[...] = ((g_ref[...] + l_ref[0].astype(F32)) + l_ref[1].astype(F32)) + l_ref[2].astype(F32)

    chip = (2 * lax.axis_index("x") + lax.axis_index("y")).astype(jnp.int32).reshape(1)
    return pl.pallas_call(
        body, name=name,
        grid_spec=pltpu.PrefetchScalarGridSpec(
            num_scalar_prefetch=1, grid=(r // tr,),
            in_specs=[pl.BlockSpec((None, tr, c), lambda i, chip_ref: (chip_ref[0], i, 0)),
                      pl.BlockSpec((3, tr, c), lambda i, chip_ref: (0, i, 0))],
            out_specs=pl.BlockSpec((tr, c), lambda i, chip_ref: (i, 0))),
        out_shape=jax.ShapeDtypeStruct((r, c), F32), compiler_params=_params("parallel"),
    )(chip, grad, land)


def _packed_rows(size, d):
    return -(-size // (8 * d)) * 8


def _pack_rows(arrays, d):
    rows = []
    for arr in arrays:
        flat = arr.reshape(-1).astype(F32)
        n = _packed_rows(flat.shape[0], d)
        rows.append(jnp.pad(flat, (0, n * d - flat.shape[0])).reshape(n, d))
    return jnp.concatenate(rows, axis=0)


def _unpack_rows(packed, shapes, d):
    out, row = [], 0
    for shape in shapes:
        size = math.prod(shape)
        n = _packed_rows(size, d)
        out.append(packed[row:row + n].reshape(-1)[:size].reshape(shape))
        row += n
    return out


SMALL = ("norm1_g", "gate_b", "conv_b", "conv_norm_g", "q_norm_g", "k_norm_g", "norm2_g", "ffn_conv_b")
LARGE = ("w_in", "w_conv_out", "w_attn_out", "w_out", "w_up", "w_down")
WEIGHTS = ("norm1_g", "w_in", "gate_b", "conv_w", "conv_b", "conv_norm_g", "w_conv_out", "q_norm_g", "k_norm_g",
           "w_attn_out", "w_out", "norm2_g", "w_up", "ffn_conv_w", "ffn_conv_b", "w_down")


def _after(vec, token):
    return vec if token is None else vec + token[0:1, 0:1]


def _local_step(dims, x, target, small, first_weights, other_weights, send_grad):
    d, f, heads = dims.d_model, dims.d_ff, dims.n_heads
    small = dict(small)
    row = lambda name: small[name].reshape(1, -1)
    head_sum = _head_sum_matrix(dims)
    head_spread = jnp.transpose(head_sum)
    ones = (head_sum, head_spread)
    gq = jnp.tile(row("q_norm_g"), (1, heads))
    gk = jnp.tile(row("k_norm_g"), (1, heads))
    one_shard = lambda w: w.reshape(1, -1, w.shape[-1])

    h = _rmsnorm_fwd(x, row("norm1_g"), name="norm1")
    full = first_weights(h)
    w_in = full["w_in"]
    conv_w = jnp.pad(full["conv_w"], ((0, CONV_HALO - dims.conv_width), (0, 0)))
    ffn_w = jnp.pad(full["ffn_conv_w"], ((0, FFN_HALO - dims.ffn_conv_width), (0, 0)))
    z = _mm_nn(h, w_in, out_dtype=BF16, after=full.get("token"), tm=2048, tn=1792, name="in_proj")
    a1, a3 = _conv_branch_fwd(z, conv_w, row("conv_b"), row("conv_norm_g"), dims, name="conv_branch")
    qkv = _qkv_layouts_fwd(z, gq, gk, ones, dims, name="qk_norm")
    per_group = {dil: _attn_fwd(*qkv[dil], dims, dil, name=f"attn_fwd_d{dil}") for dil in DILATIONS}
    o, lse = _attn_combine(per_group, head_spread, dims, name="attn_combine")
    full = other_weights(o)
    w_up = full["w_up"]
    w_co, w_ao, w_o, w_dn = (one_shard(full[k]) for k in ("w_conv_out", "w_attn_out", "w_out", "w_down"))
    ya, yb, mixed, x1, h2 = _mix_fwd(a3, o, w_co, w_ao, w_o, z, row("gate_b"), x, row("norm2_g"), dims,
                                     name="branch_projs_mix_out_proj_norm2")
    up = _mm_nn(h2, w_up, out_dtype=F32, tm=2048, name="up_proj")
    act = _ffn_act_fwd(up, ffn_w, row("ffn_conv_b"), dims, name="ffn_act")
    dy, dy_b, loss = _proj_residual_loss(act, w_dn, x1, target, tm=512, name="down_proj_loss")

    grads = {}

    def large(name, g):
        grads[name], g_bf16 = g
        return send_grad(name, grads[name], g_bf16)

    sent = large("w_down", _mm_tn(act, dy_b, n_shards=1, name="dw_down"))
    dact = _mm_nt(dy_b, w_dn, out_dtype=BF16, after=sent, name="d_act")
    dup, dfw, dfb = _ffn_bwd(dact, up, ffn_w, row("ffn_conv_b"), dims, name="ffn_bwd")
    grads["ffn_conv_w"], grads["ffn_conv_b"] = dfw[:dims.ffn_conv_width], dfb
    sent = large("w_up", _mm_tn(h2, dup, n_shards=N_CHIPS, name="dw_up"))
    dx1, dx1_b, grads["norm2_g"] = _mm_nt_rmsnorm_bwd(dup, w_up, x1, row("norm2_g"), dy, want_bf16=True, after=sent,
                                                     name="d_h2_norm2_bwd")
    sent = large("w_out", _mm_tn(mixed, dx1_b, n_shards=1, name="dw_out"))
    dya, dyb, dz_gate, grads["gate_b"] = _mix_bwd(dx1_b, w_o, ya, yb, z, row("gate_b"), dims, after=sent,
                                                  name="d_mix_gate_mix_bwd")
    sent = large("w_attn_out", _mm_tn(o, dyb, n_shards=1, name="dw_attn_out"))
    dos, deltas = _attn_bwd_prep(dyb, w_ao, o, head_sum, dims, after=sent, name="d_attn_bwd_prep")
    dqkv = {dil: _attn_bwd(*qkv[dil], dos[dil], lse[dil], deltas[dil], dims, dil, name=f"attn_bwd_d{dil}")
            for dil in DILATIONS}
    dz_qkv, dgq, dgk = _qkv_layouts_bwd(z, dqkv, gq, gk, ones, dims, name="qk_norm_bwd")
    grads["q_norm_g"] = dgq.reshape(heads, dims.head_dim).sum(axis=0)
    grads["k_norm_g"] = dgk.reshape(heads, dims.head_dim).sum(axis=0)
    sent = large("w_conv_out", _mm_tn(a3, dya, n_shards=1, name="dw_conv_out"))
    da1, grads["conv_norm_g"] = _mm_nt_rmsnorm_bwd(dya, w_co, a1, row("conv_norm_g"), None, want_bf16=False, silu=True,
                                                   after=sent, name="d_conv_act_norm_bwd")
    dz, dcw, grads["conv_b"] = _conv_branch_bwd(da1, z, conv_w, [dz_qkv, dz_gate], dims, name="conv_branch_bwd")
    grads["conv_w"] = dcw[:dims.conv_width]
    sent = large("w_in", _mm_tn(h, dz, n_shards=N_CHIPS, name="dw_in"))
    dx, grads["norm1_g"] = _mm_nt_rmsnorm_bwd(dz, w_in, x, row("norm1_g"), dx1, want_bf16=False, after=sent,
                                              name="d_h_norm1_bwd")
    return loss, dx, grads


def _step(dims, x, target, w, m, v):
    d = dims.d_model
    t = dims.tokens
    sq = lambda a: a.reshape(a.shape[1:])
    w2, m2, v2 = ({k: sq(a) for k, a in grp.items()} for grp in (w, m, v))

    conv_pad = jnp.pad(w2["conv_w"], ((0, CONV_HALO - dims.conv_width), (0, 0)))
    ffn_pad = jnp.pad(w2["ffn_conv_w"], ((0, FFN_HALO - dims.ffn_conv_width), (0, 0)))
    first_names = ("w_in", "conv_w", "ffn_conv_w")
    other_names = tuple(k for k in LARGE if k not in first_names)
    lands = dict(zip(first_names, _cast_to_lands([w2["w_in"], conv_pad, ffn_pad], [BF16, F32, F32], name="cast_first")))
    first = _gather_start([lands[k] for k in first_names], x, halved=(0,), name="gather_start_first")
    lands.update(zip(other_names, _cast_to_lands([w2[k] for k in other_names], [BF16] * len(other_names),
                                                 after=first[3], name="cast_other")))
    other = []
    cols = lambda g, rows: jnp.moveaxis(g, 0, 1).reshape(g.shape[1], -1)[:rows]

    def first_weights(after):
        got = dict(zip(first_names, _gather_wait(*first[:3], [after] + [lands[k] for k in other_names], halved=(0,),
                                                 name="gather_wait_first")))
        got["w_in"] = _forward_to_sibling(got["w_in"], name="forward_w_in")
        other.extend(_gather_start([lands[k] for k in other_names], got["w_in"], name="gather_start_other"))
        got["conv_w"] = cols(got["conv_w"], dims.conv_width)
        got["ffn_conv_w"] = cols(got["ffn_conv_w"], dims.ffn_conv_width)
        got["token"] = other[3]
        return got

    def other_weights(after):
        return dict(zip(other_names, _gather_wait(*other[:3], after, name="gather_wait_other")))

    started, full, swapping_others = {}, {}, []
    others = [k for k in LARGE if k != "w_in"]

    def my_sums(names, after, tag):
        arrived = _scatter_wait([started[k] for k in names], after, name=f"scatter_wait_{tag}")
        return [_sum_received(full[k], land, name=f"sum_{k}") for k, (_, land) in zip(names, arrived)]

    def send_grad(name, g, g_bf16):
        blocks = lambda a: a.reshape(N_CHIPS, -1, a.shape[-1])
        send, recv, g_thru, land, token = _scatter_start(blocks(g_bf16), name=f"scatter_start_{name}")
        started[name], full[name] = (send, recv, g_thru, land), blocks(g)
        if name != "w_in":
            return token
        swapping_others.extend(_swap_start(my_sums(others, token, "others"), name="swap_start_others"))
        return swapping_others[4]

    small = {k: w2[k] for k in SMALL}
    small["norm1_g"] = _after(small["norm1_g"].reshape(1, -1), first[3])
    loss, dx, grads = _local_step(dims, x.reshape(t, d), target.reshape(t, d), small, first_weights, other_weights, send_grad)

    def updates(names, mine, theirs):
        return {k: _adamw(w2[k], [a, b], m2[k], v2[k], name=f"adamw_{k}") for k, a, b in zip(names, mine, theirs)}

    small_names = SMALL + ("conv_w", "ffn_conv_w")
    packed = _pack_rows([grads[k] for k in small_names] + [loss[0, 0]], d)
    reducing = _allreduce_start(packed, name="allreduce_start")
    out = updates(others, *_swap_wait(swapping_others, [dx, reducing[4]], name="swap_wait_others"))
    mine_w_in = my_sums(["w_in"], [out[k][1] for k in others], "w_in")
    swapping_w_in = _swap_start(mine_w_in, name="swap_start_w_in")
    reduced = _sum_devices(*_allreduce_wait(reducing, swapping_w_in[4], name="allreduce_wait"), name="allreduce_sum")
    vector_rows = sum(_packed_rows(math.prod(grads[k].shape), d) for k in SMALL)
    tail_shapes = [grads[k].shape for k in ("conv_w", "ffn_conv_w")] + [()]
    conv_g, ffn_g, loss_total = _unpack_rows(reduced[vector_rows:], tail_shapes, d)
    chip = 2 * lax.axis_index("x") + lax.axis_index("y")
    sharded_g = [lax.dynamic_slice_in_dim(g, chip * w2[k].shape[1], w2[k].shape[1], axis=1)
                 for k, g in (("conv_w", conv_g), ("ffn_conv_w", ffn_g))]
    packed_g = jnp.concatenate([reduced[:vector_rows], _pack_rows(sharded_g, d)], axis=0)

    small_shapes = [w2[k].shape for k in small_names]
    pack = lambda grp: _pack_rows([grp[k] for k in small_names], d)
    results = _adamw_unpacked(pack(w2), packed_g, pack(m2), pack(v2), small_shapes, name="adamw_small")
    out.update({k: tuple(r[i] for r in results) for i, k in enumerate(small_names)})
    out.update(updates(["w_in"], *_swap_wait(swapping_w_in, results[1][0], name="swap_wait_w_in")))

    lead = lambda a: a.reshape((1,) + a.shape)
    ordered = [[lead(out[k][j].reshape(w2[k].shape)) for k in WEIGHTS] for j in range(4)]
    return (loss_total, dx.reshape(x.shape), *ordered[0], *ordered[1], *ordered[2], *ordered[3])


def kernel(x, norm1_g, w_in, gate_b, conv_w, conv_b, conv_norm_g, w_conv_out, q_norm_g, k_norm_g, w_attn_out, w_out, norm2_g, w_up, ffn_conv_w, ffn_conv_b, w_down, loss_target, m_norm1_g, m_w_in, m_gate_b, m_conv_w, m_conv_b, m_conv_norm_g, m_w_conv_out, m_q_norm_g, m_k_norm_g, m_w_attn_out, m_w_out, m_norm2_g, m_w_up, m_ffn_conv_w, m_ffn_conv_b, m_w_down, v_norm1_g, v_w_in, v_gate_b, v_conv_w, v_conv_b, v_conv_norm_g, v_w_conv_out, v_q_norm_g, v_k_norm_g, v_w_attn_out, v_w_out, v_norm2_g, v_w_up, v_ffn_conv_w, v_ffn_conv_b, v_w_down):
    w = dict(zip(WEIGHTS, (norm1_g, w_in, gate_b, conv_w, conv_b, conv_norm_g, w_conv_out, q_norm_g, k_norm_g,
                           w_attn_out, w_out, norm2_g, w_up, ffn_conv_w, ffn_conv_b, w_down)))
    m = dict(zip(WEIGHTS, (m_norm1_g, m_w_in, m_gate_b, m_conv_w, m_conv_b, m_conv_norm_g, m_w_conv_out, m_q_norm_g,
                           m_k_norm_g, m_w_attn_out, m_w_out, m_norm2_g, m_w_up, m_ffn_conv_w, m_ffn_conv_b, m_w_down)))
    v = dict(zip(WEIGHTS, (v_norm1_g, v_w_in, v_gate_b, v_conv_w, v_conv_b, v_conv_norm_g, v_w_conv_out, v_q_norm_g,
                           v_k_norm_g, v_w_attn_out, v_w_out, v_norm2_g, v_w_up, v_ffn_conv_w, v_ffn_conv_b, v_w_down)))
    dims = Dims(d_model=x.shape[-1], batch_local=x.shape[0], seq=x.shape[1], d_ff=w_down.shape[1] * N_CHIPS)
    return _step(dims, x, loss_target, w, m, v)
```

```python
import functools
import math
from typing import NamedTuple

import jax
import jax.numpy as jnp
from jax import lax
from jax.experimental import pallas as pl
from jax.experimental.pallas import tpu as pltpu

F32 = jnp.float32
BF16 = jnp.bfloat16

RMS_EPS = 1e-6
ATTN_BLOCK = 128
DILATIONS = (1, 4, 16)
CONV_HALO = 32
FFN_HALO = 8
ADAM_LR, ADAM_B1, ADAM_B2, ADAM_EPS, ADAM_WD, ADAM_STEP = 0.001, 0.9, 0.999, 1e-08, 0.01, 10
V7X_VMEM_LIMIT_BYTES = 56 * 2 ** 20
N_CHIPS = 4
MESH = pl.DeviceIdType.MESH


class Dims(NamedTuple):
    d_model: int = 1024
    n_heads: int = 16
    head_dim: int = 64
    d_ff: int = 2816
    seq: int = 2048
    batch_local: int = 2
    conv_width: int = 31
    ffn_conv_width: int = 3

    @property
    def tokens(self):
        return self.seq * self.batch_local


def _params(*semantics):
    return pltpu.CompilerParams(dimension_semantics=semantics, vmem_limit_bytes=V7X_VMEM_LIMIT_BYTES)


ANY = pl.BlockSpec(memory_space=pl.ANY)


def _ordered(body, n_inputs, after):
    after = [] if after is None else list(after) if isinstance(after, (list, tuple)) else [after]
    if not after:
        return body, [], []

    def wrapped(*refs):
        return body(*refs[:n_inputs], *refs[n_inputs + len(after):])

    return wrapped, [ANY] * len(after), after


def _pick(n, target, mult=128):
    if n <= target:
        return n
    best = None
    for t in range(mult, target + 1, mult):
        if n % t == 0:
            best = t
    assert best is not None, (n, target, mult)
    return best


def _sigmoid(v):
    return 1.0 / (1.0 + jnp.exp(-v))


def _mm_nn(a, w, *, out_dtype, name, residual=None, after=None, tm=1024, tn=1408, tk=2816):
    m, k = a.shape
    nsh, k2, c = w.shape
    assert k == k2 and a.dtype == BF16 and w.dtype == BF16
    n = nsh * c
    tm, tn, tk = _pick(m, tm, 8), _pick(c, tn), _pick(k, tk)
    nk, cpn = k // tk, c // tn

    def body(*refs):
        if residual is None:
            a_ref, w_ref, o_ref, acc = refs
        else:
            a_ref, w_ref, r_ref, o_ref, acc = refs
        prod = jnp.dot(a_ref[...], w_ref[...], preferred_element_type=F32)

        def finish(total):
            if residual is not None:
                total = total + r_ref[...]
            o_ref[...] = total.astype(out_dtype)

        if nk == 1:
            finish(prod)
        else:
            kk = pl.program_id(2)

            @pl.when(kk == 0)
            def _():
                acc[...] = prod

            @pl.when(kk > 0)
            def _():
                acc[...] += prod

            @pl.when(kk == nk - 1)
            def _():
                finish(acc[...])

    in_specs = [pl.BlockSpec((tm, tk), lambda i, j, kk: (i, kk)),
                pl.BlockSpec((None, tk, tn), lambda i, j, kk: (j // cpn, kk, j % cpn))]
    args = [a, w]
    if residual is not None:
        in_specs.append(pl.BlockSpec((tm, tn), lambda i, j, kk: (i, j)))
        args.append(residual)
    body, more_specs, more_args = _ordered(body, len(args), after)
    return pl.pallas_call(
        body, name=name, grid=(m // tm, n // tn, nk),
        in_specs=in_specs + more_specs, out_specs=pl.BlockSpec((tm, tn), lambda i, j, kk: (i, j)),
        out_shape=jax.ShapeDtypeStruct((m, n), out_dtype),
        scratch_shapes=[pltpu.VMEM((tm, tn) if nk > 1 else (8, 128), F32)],
        compiler_params=_params("parallel", "parallel", "arbitrary"),
    )(*args, *more_args)


def _proj_residual_loss(a, w, residual, target, *, name, tm=1024):
    m, k = a.shape
    _, k2, n = w.shape
    assert w.shape[0] == 1 and k == k2 and a.dtype == BF16 and w.dtype == BF16
    tm = _pick(m, tm, 8)

    def body(a_ref, w_ref, r_ref, t_ref, dy_ref, dyb_ref, loss_ref):
        err = r_ref[...] + jnp.dot(a_ref[...], w_ref[...], preferred_element_type=F32) - t_ref[...]
        dy = err * (1.0 / n)
        dy_ref[...] = dy
        dyb_ref[...] = dy.astype(BF16)
        part = jnp.sum(jnp.sum(err * err, axis=-1, keepdims=True), axis=0, keepdims=True) * (0.5 / n)
        _accumulate(loss_ref, jnp.broadcast_to(part, (8, 128)), pl.program_id(0) == 0)

    rows = lambda width: pl.BlockSpec((tm, width), lambda i: (i, 0))
    return pl.pallas_call(
        body, name=name, grid=(m // tm,),
        in_specs=[rows(k), pl.BlockSpec((None, k, n), lambda i: (0, 0, 0)), rows(n), rows(n)],
        out_specs=[rows(n), rows(n), pl.BlockSpec((8, 128), lambda i: (0, 0))],
        out_shape=[jax.ShapeDtypeStruct((m, n), F32), jax.ShapeDtypeStruct((m, n), BF16),
                   jax.ShapeDtypeStruct((8, 128), F32)],
        compiler_params=_params("arbitrary"),
    )(a, w, residual, target)


def _mm_nt(a, w, *, out_dtype, name, after=None, tm=1024, tn=1408, tk=1792):
    m, k = a.shape
    nsh, r, c = w.shape
    assert k == nsh * c and a.dtype == BF16 and w.dtype == BF16
    tm, tn, tk = _pick(m, tm, 8), _pick(r, tn), _pick(c, tk)
    nk, cpk = k // tk, c // tk

    def body(a_ref, w_ref, o_ref, acc):
        prod = lax.dot_general(a_ref[...], w_ref[...], (((1,), (1,)), ((), ())), preferred_element_type=F32)
        if nk == 1:
            o_ref[...] = prod.astype(out_dtype)
        else:
            kk = pl.program_id(2)

            @pl.when(kk == 0)
            def _():
                acc[...] = prod

            @pl.when(kk > 0)
            def _():
                acc[...] += prod

            @pl.when(kk == nk - 1)
            def _():
                o_ref[...] = acc[...].astype(out_dtype)

    body, more_specs, more_args = _ordered(body, 2, after)
    return pl.pallas_call(
        body, name=name, grid=(m // tm, r // tn, nk),
        in_specs=[pl.BlockSpec((tm, tk), lambda i, j, kk: (i, kk)),
                  pl.BlockSpec((None, tn, tk), lambda i, j, kk: (kk // cpk, j, kk % cpk))] + more_specs,
        out_specs=pl.BlockSpec((tm, tn), lambda i, j, kk: (i, j)),
        out_shape=jax.ShapeDtypeStruct((m, r), out_dtype),
        scratch_shapes=[pltpu.VMEM((tm, tn) if nk > 1 else (8, 128), F32)],
        compiler_params=_params("parallel", "parallel", "arbitrary"),
    )(a, w, *more_args)


NORM_BWD_ROWS = 256


def _mm_nt_rmsnorm_bwd(a, w, x, g, dres, *, name, want_bf16, silu=False, after=None, tm=1024, tk=1792):
    m, k = a.shape
    nsh, r, c = w.shape
    assert k == nsh * c and a.dtype == BF16 and w.dtype == BF16 and x.shape == (m, r)
    tm, tk = _pick(m, tm, 8), _pick(c, tk)
    nk, cpk = k // tk, c // tk
    rows = _pick(tm, NORM_BWD_ROWS, 8)
    n_in = 4 if dres is None else 5

    def body(a_ref, w_ref, x_ref, g_ref, *rest):
        dres_ref = None if dres is None else rest[0]
        outs, acc = rest[n_in - 4:-1], rest[-1]
        dx_ref, dg_ref = outs[0], outs[-1]
        kk = pl.program_id(1)
        prod = lax.dot_general(a_ref[...], w_ref[...], (((1,), (1,)), ((), ())), preferred_element_type=F32)

        @pl.when(kk == 0)
        def _():
            acc[...] = prod

        @pl.when(kk > 0)
        def _():
            acc[...] += prod

        @pl.when(kk == nk - 1)
        def _():
            dg = jnp.zeros((1, r), F32)
            for r0 in range(0, tm, rows):
                part = slice(r0, r0 + rows)
                xv, dyv = x_ref[part, :], acc[part, :]
                inv = lax.rsqrt(jnp.mean(xv * xv, axis=-1, keepdims=True) + RMS_EPS)
                if silu:
                    y = xv * inv * g_ref[...]
                    sg = _sigmoid(y)
                    dyv = dyv * sg * (1.0 + y * (1.0 - sg))
                gy = dyv * g_ref[...]
                dx = inv * gy - xv * (inv * inv * inv) * jnp.mean(xv * gy, axis=-1, keepdims=True)
                if dres is not None:
                    dx = dx + dres_ref[part, :]
                dx_ref[part, :] = dx
                if want_bf16:
                    outs[1][part, :] = dx.astype(BF16)
                dg = dg + jnp.sum(dyv * xv * inv, axis=0, keepdims=True)
            _accumulate(dg_ref, dg, pl.program_id(0) == 0)

    whole = lambda: pl.BlockSpec((tm, r), lambda i, kk: (i, 0))
    vec = pl.BlockSpec((1, r), lambda i, kk: (0, 0))
    out_shape, out_specs = [jax.ShapeDtypeStruct((m, r), F32)], [whole()]
    if want_bf16:
        out_shape.append(jax.ShapeDtypeStruct((m, r), BF16))
        out_specs.append(whole())
    out_shape.append(jax.ShapeDtypeStruct((1, r), F32))
    out_specs.append(vec)
    body, more_specs, more_args = _ordered(body, n_in, after)
    residual_specs, residual_args = ([], []) if dres is None else ([whole()], [dres])
    return pl.pallas_call(
        body, name=name, grid=(m // tm, nk),
        in_specs=[pl.BlockSpec((tm, tk), lambda i, kk: (i, kk)),
                  pl.BlockSpec((None, r, tk), lambda i, kk: (kk // cpk, 0, kk % cpk)), whole(), vec]
        + residual_specs + more_specs,
        out_specs=out_specs, out_shape=out_shape,
        scratch_shapes=[pltpu.VMEM((tm, r), F32)],
        compiler_params=_params("arbitrary", "arbitrary"),
    )(a, w, x, g, *residual_args, *more_args)


MM_TN_VMEM_BYTES = 44 * 2 ** 20


def _mm_tn(a, b, *, n_shards, name, tm=1408, tn=1408):
    t, m = a.shape
    t2, n = b.shape
    assert t == t2 and a.dtype == BF16 and b.dtype == BF16
    c = n // n_shards
    tm, tn = _pick(m, tm), _pick(c, tn)
    if m // tm == 1 and n // tn == 1 and tn % (2 * LANES) == 0:
        tn //= 2
    fixed = 2 * tm * tn * 6
    if 4 * t * (tm + tn) + fixed <= MM_TN_VMEM_BYTES:
        tk = t
    else:
        tk = _pick(t, (MM_TN_VMEM_BYTES - fixed - 4 * tm * tn) // (4 * (tm + tn)), 8)
    nk, cpn = t // tk, c // tn

    def body(a_ref, b_ref, o_ref, ob_ref, acc):
        kk = pl.program_id(2)
        prod = lax.dot_general(a_ref[...], b_ref[...], (((0,), (0,)), ((), ())), preferred_element_type=F32)

        def finish(total):
            o_ref[...] = total
            ob_ref[...] = total.astype(BF16)

        if nk == 1:
            finish(prod)
        else:
            @pl.when(kk == 0)
            def _():
                acc[...] = prod

            @pl.when(kk > 0)
            def _():
                acc[...] += prod

            @pl.when(kk == nk - 1)
            def _():
                finish(acc[...])

    out_spec = pl.BlockSpec((None, tm, tn), lambda i, j, kk: (j // cpn, i, j % cpn))
    return pl.pallas_call(
        body, name=name, grid=(m // tm, n // tn, nk),
        in_specs=[pl.BlockSpec((tk, tm), lambda i, j, kk: (kk, i)),
                  pl.BlockSpec((tk, tn), lambda i, j, kk: (kk, j))],
        out_specs=[out_spec, out_spec],
        out_shape=[jax.ShapeDtypeStruct((n_shards, m, c), F32), jax.ShapeDtypeStruct((n_shards, m, c), BF16)],
        scratch_shapes=[pltpu.VMEM((tm, tn) if nk > 1 else (8, 128), F32)],
        compiler_params=_params("parallel", "parallel", "arbitrary"),
    )(a, b)


def _row_spec(tr, width, col=0):
    return pl.BlockSpec((tr, width), lambda i, col=col: (i, col))


def _vec_spec(width, col=0):
    return pl.BlockSpec((1, width), lambda i, col=col: (0, col))


def _accumulate(ref, value, first):
    @pl.when(first)
    def _():
        ref[...] = value

    @pl.when(jnp.logical_not(first))
    def _():
        ref[...] += value


def _rmsnorm_fwd(x, g, *, name, tr=512):
    t, d = x.shape
    tr = _pick(t, tr, 8)

    def body(x_ref, g_ref, o_ref):
        xv = x_ref[...]
        r = lax.rsqrt(jnp.mean(xv * xv, axis=-1, keepdims=True) + RMS_EPS)
        o_ref[...] = (xv * r * g_ref[...]).astype(BF16)

    return pl.pallas_call(
        body, name=name, grid=(t // tr,),
        in_specs=[_row_spec(tr, d), _vec_spec(d)], out_specs=_row_spec(tr, d),
        out_shape=jax.ShapeDtypeStruct((t, d), BF16), compiler_params=_params("parallel"),
    )(x, g)


CONV_ROWS = 16


def _seq_specs(dims, ts, width, halo, col, *, nxt=False):
    nst, per = dims.seq // ts, ts // halo
    last = dims.tokens // halo - 1
    cur = pl.BlockSpec((ts, width), lambda b, i: (b * nst + i, col))
    if nxt:
        edge = pl.BlockSpec((halo, width), lambda b, i: (jnp.minimum((b * nst + i + 1) * per, last), col))
    else:
        edge = pl.BlockSpec((halo, width), lambda b, i: (jnp.maximum((b * nst + i) * per - 1, 0), col))
    return cur, edge


SUBLANES = 8


def _shifted_copies(buf, shifted):
    rows = shifted.shape[1]
    for s in range(1, SUBLANES):
        shifted[s - 1] = buf[pl.ds(s, rows), :]


def _window(buf, shifted, start, size):
    a, s = divmod(start, SUBLANES)
    src = buf if s == 0 else shifted.at[s - 1]
    return src[pl.ds(SUBLANES * a, size), :]


def _conv_branch_fwd(z, w, b, g, dims, *, name, ts=128):
    t, c, kw = z.shape[0], dims.d_model, dims.conv_width
    base = CONV_HALO - (kw - 1)

    def body(av_ref, hv_ref, ag_ref, hg_ref, w_ref, b_ref, g_ref, a1_ref, a3_ref, buf, shifted):
        i = pl.program_id(1)
        buf[CONV_HALO:, :] = av_ref[...].astype(F32) * _sigmoid(ag_ref[...].astype(F32))
        buf[0:CONV_HALO, :] = jnp.where(i > 0, hv_ref[...].astype(F32) * _sigmoid(hg_ref[...].astype(F32)), 0.0)
        _shifted_copies(buf, shifted)
        for r0 in range(0, ts, CONV_ROWS):
            acc = jnp.broadcast_to(b_ref[...], (CONV_ROWS, c))
            for k in range(kw):
                acc = acc + w_ref[k:k + 1, :] * _window(buf, shifted, r0 + base + k, CONV_ROWS)
            a1_ref[r0:r0 + CONV_ROWS, :] = acc
            a2 = acc * lax.rsqrt(jnp.mean(acc * acc, axis=-1, keepdims=True) + RMS_EPS) * g_ref[...]
            a3_ref[r0:r0 + CONV_ROWS, :] = (a2 * _sigmoid(a2)).astype(BF16)

    vec = pl.BlockSpec((1, c), lambda b, i: (0, 0))
    out = pl.BlockSpec((ts, c), lambda b, i: (b * (dims.seq // ts) + i, 0))
    return pl.pallas_call(
        body, name=name, grid=(dims.batch_local, dims.seq // ts),
        in_specs=[*_seq_specs(dims, ts, c, CONV_HALO, 0), *_seq_specs(dims, ts, c, CONV_HALO, 1),
                  pl.BlockSpec((CONV_HALO, c), lambda b, i: (0, 0)), vec, vec],
        out_specs=[out, out],
        out_shape=[jax.ShapeDtypeStruct((t, c), F32), jax.ShapeDtypeStruct((t, c), BF16)],
        scratch_shapes=[pltpu.VMEM((CONV_HALO + ts, c), F32),
                        pltpu.VMEM((SUBLANES - 1, CONV_HALO + ts - SUBLANES, c), F32)],
        compiler_params=_params("parallel", "parallel"),
    )(z, z, z, z, w, b, g)


def _conv_branch_bwd(da1, z, w, rest_of_dz, dims, *, name, ts=128):
    t, c, kw = z.shape[0], dims.d_model, dims.conv_width
    nst = dims.seq // ts
    base = CONV_HALO - (kw - 1)
    n_rest = len(rest_of_dz)
    total = 2 * c + sum(r.shape[1] for r in rest_of_dz)

    def body(d_ref, dn_ref, av_ref, hv_ref, ag_ref, hg_ref, w_ref, *more):
        rest_refs = more[:n_rest]
        dz_ref, dw_ref, db_ref, abuf, dbuf, ashift, dshift = more[n_rest:]
        col = 2 * c
        for r in rest_refs:
            dz_ref[:, col:col + r.shape[1]] = r[...]
            col += r.shape[1]
        i = pl.program_id(1)
        first = jnp.logical_and(pl.program_id(0) == 0, i == 0)
        abuf[CONV_HALO:, :] = av_ref[...].astype(F32) * _sigmoid(ag_ref[...].astype(F32))
        abuf[0:CONV_HALO, :] = jnp.where(i > 0, hv_ref[...].astype(F32) * _sigmoid(hg_ref[...].astype(F32)), 0.0)
        d1 = d_ref[...]
        dbuf[0:ts, :] = d1
        dbuf[ts:, :] = jnp.where(i < nst - 1, dn_ref[...], 0.0)
        _shifted_copies(abuf, ashift)
        _shifted_copies(dbuf, dshift)

        @pl.when(first)
        def _():
            dw_ref[...] = jnp.zeros_like(dw_ref)
            db_ref[...] = jnp.zeros_like(db_ref)

        db_ref[...] += jnp.sum(d1, axis=0, keepdims=True)
        for k in range(kw):
            dw_ref[k:k + 1, :] += jnp.sum(d1 * _window(abuf, ashift, base + k, ts), axis=0, keepdims=True)
        for r0 in range(0, ts, CONV_ROWS):
            acc = jnp.zeros((CONV_ROWS, c), F32)
            for k in range(kw):
                acc = acc + w_ref[k:k + 1, :] * _window(dbuf, dshift, r0 + (kw - 1) - k, CONV_ROWS)
            av = av_ref[r0:r0 + CONV_ROWS, :].astype(F32)
            sg = _sigmoid(ag_ref[r0:r0 + CONV_ROWS, :].astype(F32))
            dz_ref[r0:r0 + CONV_ROWS, 0:c] = (acc * sg).astype(BF16)
            dz_ref[r0:r0 + CONV_ROWS, c:2 * c] = (acc * av * sg * (1.0 - sg)).astype(BF16)

    cur, nxt = _seq_specs(dims, ts, c, CONV_HALO, 0, nxt=True)
    return pl.pallas_call(
        body, name=name, grid=(dims.batch_local, nst),
        in_specs=[cur, nxt, *_seq_specs(dims, ts, c, CONV_HALO, 0), *_seq_specs(dims, ts, c, CONV_HALO, 1),
                  pl.BlockSpec((CONV_HALO, c), lambda b, i: (0, 0))]
        + [pl.BlockSpec((ts, r.shape[1]), lambda b, i: (b * nst + i, 0)) for r in rest_of_dz],
        out_specs=[pl.BlockSpec((ts, total), lambda b, i: (b * nst + i, 0)),
                   pl.BlockSpec((CONV_HALO, c), lambda b, i: (0, 0)), pl.BlockSpec((1, c), lambda b, i: (0, 0))],
        out_shape=[jax.ShapeDtypeStruct((t, total), BF16), jax.ShapeDtypeStruct((CONV_HALO, c), F32),
                   jax.ShapeDtypeStruct((1, c), F32)],
        scratch_shapes=[pltpu.VMEM((CONV_HALO + ts, c), F32)] * 2
        + [pltpu.VMEM((SUBLANES - 1, CONV_HALO + ts - SUBLANES, c), F32)] * 2,
        compiler_params=_params("arbitrary", "arbitrary"),
    )(da1, da1, z, z, z, z, w, *rest_of_dz)


FFN_ROWS = 16
FFN_COLS = 256


def _ffn_chunks(ts, f):
    cw = _pick(f, FFN_COLS)
    return [(r0, c0, cw) for r0 in range(0, ts, FFN_ROWS) for c0 in range(0, f, cw)]


def _tap_sources(buf, moved, offsets, rows):
    taps, used = [], 0
    for off in offsets:
        if off % SUBLANES:
            moved[used] = buf[pl.ds(off, rows), :]
            taps.append((moved.at[used], 0))
            used += 1
        else:
            taps.append((buf, off))
    return taps


def _moved_copies(offsets):
    return sum(1 for off in offsets if off % SUBLANES)


def _taps_sum(taps, w_ref, init, r0, cols):
    for k, (src, off) in enumerate(taps):
        init = init + w_ref[k:k + 1, cols] * src[pl.ds(off + r0, init.shape[0]), cols]
    return init


def _ffn_bwd(dact, up, w, b, dims, *, name, ts=128):
    t, f, kw = up.shape[0], dims.d_ff, dims.ffn_conv_width
    nst = dims.seq // ts
    fwd_offsets = [FFN_HALO - (kw - 1) + k for k in range(kw)]
    bwd_offsets = [(kw - 1) - k for k in range(kw)]
    dact_halo = 2 * FFN_HALO

    def body(d_ref, dn_ref, up_ref, hp_ref, hn_ref, w_ref, b_ref, o_ref, dw_ref, db_ref, buf, moved, dbuf, dmoved):
        i = pl.program_id(1)
        first = jnp.logical_and(pl.program_id(0) == 0, i == 0)
        more = i < nst - 1
        buf[0:FFN_HALO, :] = jnp.where(i > 0, hp_ref[...], 0.0)
        buf[FFN_HALO:FFN_HALO + ts, :] = up_ref[...]
        buf[FFN_HALO + ts:, :] = hn_ref[...]
        taps = _tap_sources(buf, moved, fwd_offsets, ts + FFN_HALO)

        def du_chunk(r0, rows, c0, cw, d):
            vcols, gcols = slice(c0, c0 + cw), slice(f + c0, f + c0 + cw)
            uv = _taps_sum(taps, w_ref, jnp.broadcast_to(b_ref[:, vcols], (rows, cw)), r0, vcols)
            ug = _taps_sum(taps, w_ref, jnp.broadcast_to(b_ref[:, gcols], (rows, cw)), r0, gcols)
            sg = _sigmoid(ug)
            dbuf[r0:r0 + rows, vcols] = d * ug * sg
            dbuf[r0:r0 + rows, gcols] = d * uv * sg * (1.0 + ug * (1.0 - sg))

        for r0, c0, cw in _ffn_chunks(ts, f):
            du_chunk(r0, FFN_ROWS, c0, cw, d_ref[r0:r0 + FFN_ROWS, c0:c0 + cw].astype(F32))
        for _, c0, cw in _ffn_chunks(FFN_ROWS, f):
            d_next = dn_ref[:, c0:c0 + cw].astype(F32)[0:FFN_HALO]
            du_chunk(ts, FFN_HALO, c0, cw, jnp.where(more, d_next, 0.0))

        @pl.when(first)
        def _():
            dw_ref[...] = jnp.zeros_like(dw_ref)
            db_ref[...] = jnp.zeros_like(db_ref)

        du = dbuf[0:ts, :]
        db_ref[...] += jnp.sum(du, axis=0, keepdims=True)
        for k, (src, off) in enumerate(taps):
            dw_ref[k:k + 1, :] += jnp.sum(du * src[pl.ds(off, ts), :], axis=0, keepdims=True)

        dtaps = _tap_sources(dbuf, dmoved, bwd_offsets, ts)
        for r0, c0, cw in _ffn_chunks(ts, 2 * f):
            cols = slice(c0, c0 + cw)
            o_ref[r0:r0 + FFN_ROWS, cols] = _taps_sum(dtaps, w_ref, jnp.zeros((FFN_ROWS, cw), F32), r0, cols).astype(BF16)

    up_cur, up_prev = _seq_specs(dims, ts, 2 * f, FFN_HALO, 0)
    _, up_next = _seq_specs(dims, ts, 2 * f, FFN_HALO, 0, nxt=True)
    d_cur, d_next = _seq_specs(dims, ts, f, dact_halo, 0, nxt=True)
    full = lambda rows: pl.BlockSpec((rows, 2 * f), lambda b_, i: (0, 0))
    return pl.pallas_call(
        body, name=name, grid=(dims.batch_local, nst),
        in_specs=[d_cur, d_next, up_cur, up_prev, up_next, full(FFN_HALO), full(1)],
        out_specs=[pl.BlockSpec((ts, 2 * f), lambda b_, i: (b_ * nst + i, 0)), full(FFN_HALO), full(1)],
        out_shape=[jax.ShapeDtypeStruct((t, 2 * f), BF16), jax.ShapeDtypeStruct((FFN_HALO, 2 * f), F32),
                   jax.ShapeDtypeStruct((1, 2 * f), F32)],
        scratch_shapes=[pltpu.VMEM((ts + 2 * FFN_HALO, 2 * f), F32),
                        pltpu.VMEM((_moved_copies(fwd_offsets), ts + FFN_HALO, 2 * f), F32),
                        pltpu.VMEM((ts + FFN_HALO, 2 * f), F32),
                        pltpu.VMEM((_moved_copies(bwd_offsets), ts, 2 * f), F32)],
        compiler_params=_params("arbitrary", "arbitrary"),
    )(dact, dact, up, up, up, w, b)


def _ffn_act_fwd(up, w, b, dims, *, name, ts=128):
    t, f, kw = up.shape[0], dims.d_ff, dims.ffn_conv_width
    offsets = [FFN_HALO - (kw - 1) + k for k in range(kw)]

    def body(up_ref, h_ref, w_ref, b_ref, o_ref, buf, moved):
        buf[FFN_HALO:, :] = up_ref[...]
        buf[0:FFN_HALO, :] = jnp.where(pl.program_id(1) > 0, h_ref[...], 0.0)
        taps = _tap_sources(buf, moved, offsets, ts)
        for r0, c0, cw in _ffn_chunks(ts, f):
            vcols, gcols = slice(c0, c0 + cw), slice(f + c0, f + c0 + cw)
            uv = _taps_sum(taps, w_ref, jnp.broadcast_to(b_ref[:, vcols], (FFN_ROWS, cw)), r0, vcols)
            ug = _taps_sum(taps, w_ref, jnp.broadcast_to(b_ref[:, gcols], (FFN_ROWS, cw)), r0, gcols)
            o_ref[r0:r0 + FFN_ROWS, vcols] = (ug * _sigmoid(ug) * uv).astype(BF16)

    full = lambda rows: pl.BlockSpec((rows, 2 * f), lambda b_, i: (0, 0))
    return pl.pallas_call(
        body, name=name, grid=(dims.batch_local, dims.seq // ts),
        in_specs=[*_seq_specs(dims, ts, 2 * f, FFN_HALO, 0), full(FFN_HALO), full(1)],
        out_specs=pl.BlockSpec((ts, f), lambda b_, i: (b_ * (dims.seq // ts) + i, 0)),
        out_shape=jax.ShapeDtypeStruct((t, f), BF16),
        scratch_shapes=[pltpu.VMEM((FFN_HALO + ts, 2 * f), F32), pltpu.VMEM((_moved_copies(offsets), ts, 2 * f), F32)],
        compiler_params=_params("parallel", "parallel"),
    )(up, up, w, b)


def _dot_nt(a, b):
    return lax.dot_general(a, b, (((1,), (1,)), ((), ())), preferred_element_type=F32)


def _dot_tn(a, b):
    return lax.dot_general(a, b, (((0,), (0,)), ((), ())), preferred_element_type=F32)


LANES = 128
MASK_BIAS = 1e30
RESIDUE_DILATIONS = tuple(d for d in DILATIONS if d > 1)


def _rows_to_residues(value, out_ref, scr, d):
    rows, width = value.shape
    for c in range(width // LANES):
        cols = slice(LANES * c, LANES * (c + 1))
        scr[c] = value[:, cols]
        for r in range(d):
            out_ref[r, :, cols] = scr[c, pl.ds(r, rows // d, stride=d), :].astype(out_ref.dtype)


def _residues_to_rows(in_ref, scr, d):
    _, n, width = in_ref.shape
    slabs = []
    for c in range(width // LANES):
        cols = slice(LANES * c, LANES * (c + 1))
        for r in range(d):
            scr[c, pl.ds(r, n, stride=d), :] = in_ref[r, :, cols].astype(F32)
        slabs.append(scr[c])
    return slabs[0] if len(slabs) == 1 else jnp.concatenate(slabs, axis=1)


def _residue_shape(dims, d, width, dtype):
    return jax.ShapeDtypeStruct((dims.batch_local, d, dims.seq // d, width), dtype)


def _residue_spec(dims, d, tr, width):
    tiles = dims.seq // tr
    return pl.BlockSpec((None, d, tr // d, width), lambda i: (i // tiles, 0, i % tiles, 0))


def _head_sum_matrix(dims):
    a = dims.n_heads * dims.head_dim
    head = jnp.arange(a, dtype=jnp.int32) // dims.head_dim
    return (head[:, None] == jnp.arange(LANES, dtype=jnp.int32)[None, :]).astype(BF16)


def _two_pass_dot(v, m):
    hi = v.astype(BF16)
    lo = (v - hi.astype(F32)).astype(BF16)
    return jnp.dot(hi, m, preferred_element_type=F32) + jnp.dot(lo, m, preferred_element_type=F32)


def _residue_permutations(tr):
    out = []
    for d in RESIDUE_DILATIONS:
        dst = jnp.arange(tr, dtype=jnp.int32)
        src = d * (dst % (tr // d)) + dst // (tr // d)
        out.append((src[:, None] == jnp.arange(tr, dtype=jnp.int32)[None, :]).astype(BF16))
    return out


def _bf16_rows_to_residues(value, out_ref, perm_ref, d):
    n = value.shape[0] // d
    moved = jnp.dot(perm_ref[...], value, preferred_element_type=F32).astype(out_ref.dtype)
    for r in range(d):
        out_ref[r] = moved[r * n:(r + 1) * n]


def _bf16_residues_to_rows(in_ref, back_ref):
    d = in_ref.shape[0]
    stacked = jnp.concatenate([in_ref[r] for r in range(d)], axis=0)
    return jnp.dot(back_ref[...], stacked, preferred_element_type=F32)


def _qkv_layouts_fwd(z, gq, gk, head_ones, dims, *, name, tr=256):
    t = z.shape[0]
    a = dims.n_heads * dims.head_dim
    q_scale = dims.head_dim ** -0.5
    nres = len(RESIDUE_DILATIONS)

    def body(q_ref, k_ref, v_ref, gq_ref, gk_ref, sum_ref, spread_ref, *rest):
        perm_refs, outs = rest[:nres], rest[nres:]
        qv, kv = q_ref[...].astype(F32), k_ref[...].astype(F32)
        mean = lambda val: _two_pass_dot(_two_pass_dot(val, sum_ref[...]), spread_ref[...]) * (1.0 / dims.head_dim)
        rq = lax.rsqrt(mean(qv * qv) + RMS_EPS)
        rk = lax.rsqrt(mean(kv * kv) + RMS_EPS)
        values = ((qv * rq * gq_ref[...] * q_scale).astype(BF16), (kv * rk * gk_ref[...]).astype(BF16), v_ref[...])
        for j, val in enumerate(values):
            outs[j][...] = val
            for g, d in enumerate(RESIDUE_DILATIONS):
                _bf16_rows_to_residues(val, outs[3 * (g + 1) + j], perm_refs[g], d)

    out_specs = [_row_spec(tr, a)] * 3
    out_shape = [jax.ShapeDtypeStruct((t, a), BF16)] * 3
    for d in RESIDUE_DILATIONS:
        out_specs += [_residue_spec(dims, d, tr, a)] * 3
        out_shape += [_residue_shape(dims, d, a, BF16)] * 3
    outs = pl.pallas_call(
        body, name=name, grid=(t // tr,),
        in_specs=[_row_spec(tr, a, 2), _row_spec(tr, a, 3), _row_spec(tr, a, 4), _vec_spec(a), _vec_spec(a),
                  pl.BlockSpec((a, LANES), lambda i: (0, 0)), pl.BlockSpec((LANES, a), lambda i: (0, 0))]
        + [pl.BlockSpec((tr, tr), lambda i: (0, 0))] * nres,
        out_specs=out_specs, out_shape=out_shape,
        compiler_params=_params("parallel"),
    )(z, z, z, gq, gk, *head_ones, *_residue_permutations(tr))
    return {d: tuple(outs[3 * g:3 * g + 3]) for g, d in enumerate((1,) + RESIDUE_DILATIONS)}


ATTN_RESIDUES_PER_STEP = 4
ATTN_RESIDUES_PER_STEP_WINDOWED = 2


def _attn_groups(dims, dil):
    return (dims.batch_local, dil) if dil > 1 else (1, dims.batch_local)


def _attn_array(x, dims, dil):
    return x if dil > 1 else x.reshape(1, dims.batch_local, dims.seq, x.shape[-1])


def _attn_residues(dims, dil):
    one_block = dims.seq // dil == ATTN_BLOCK
    return math.gcd(_attn_groups(dims, dil)[1], ATTN_RESIDUES_PER_STEP if one_block else ATTN_RESIDUES_PER_STEP_WINDOWED)


def _per_residue(body, rs):
    if rs == 1:
        return body

    def stepped(*refs):
        for r in range(rs):
            body(*[ref.at[r] for ref in refs])

    return stepped


def _attn_specs(dims, dil, width):
    blk = ATTN_BLOCK
    nb = dims.seq // dil // blk
    rs = _attn_residues(dims, dil)
    lead, groups = _attn_groups(dims, dil)
    if rs > 1 and nb == 1:
        grid = (lead, groups // rs)
        at = lambda f: pl.BlockSpec((None, rs, blk, width), lambda b, r: (b, r, 0, 0))
    elif rs > 1:
        grid = (lead, groups // rs, nb)
        at = lambda f: pl.BlockSpec((None, rs, blk, width), lambda b, r, i: (b, r, f(i), 0))
    else:
        grid = (lead, groups, nb)
        at = lambda f: pl.BlockSpec((None, None, blk, width), lambda b, r, i: (b, r, f(i), 0))
    return grid, at(lambda i: i), at(lambda i: jnp.maximum(i - 1, 0)), at(lambda i: jnp.minimum(i + 1, nb - 1))


def _head_slopes(n_heads):
    h = lax.broadcasted_iota(jnp.int32, (n_heads, 1, 1), 0).astype(F32)
    return jnp.exp((h + 1.0) * (-8.0 / n_heads * math.log(2.0)))


def _pair_masks(hd):
    low = lax.broadcasted_iota(jnp.int32, (1, 2 * hd), 1) < hd
    return low, jnp.logical_not(low)


def _attn_fwd(q, k, v, dims, dil, *, name):
    a = dims.n_heads * dims.head_dim
    heads, hd, blk = dims.n_heads, dims.head_dim, ATTN_BLOCK
    assert 2 * hd == LANES and heads % 2 == 0 and heads <= LANES
    nb = dims.seq // dil // blk
    has_prev = nb > 1
    nkeys = 2 * blk if has_prev else blk
    grid, cur, prev, _ = _attn_specs(dims, dil, a)
    _, cur_stat, _, _ = _attn_specs(dims, dil, LANES)

    def body(*refs):
        if has_prev:
            q_ref, kc_ref, vc_ref, kp_ref, vp_ref, o_ref, lse_ref, s_scr, p_scr, k_st, v_st = refs
            k_st[0:blk, :], k_st[blk:, :] = kp_ref[...], kc_ref[...]
            v_st[0:blk, :], v_st[blk:, :] = vp_ref[...], vc_ref[...]
        else:
            q_ref, k_st, v_st, o_ref, lse_ref, s_scr, p_scr = refs
        low, high = _pair_masks(hd)

        for hp in range(heads // 2):
            sl = slice(LANES * hp, LANES * (hp + 1))
            q2 = q_ref[:, sl]
            kcat = k_st[:, sl]
            s_scr[2 * hp] = _dot_nt(jnp.where(low, q2, jnp.zeros_like(q2)), kcat)
            s_scr[2 * hp + 1] = _dot_nt(jnp.where(high, q2, jnp.zeros_like(q2)), kcat)

        iq = lax.broadcasted_iota(jnp.int32, (blk, nkeys), 0)
        jk = lax.broadcasted_iota(jnp.int32, (blk, nkeys), 1)
        if has_prev:
            steps = iq + blk - jk
            valid = (steps >= 0) & (steps <= blk) & ((jk >= blk) | (pl.program_id(len(grid) - 1) > 0))
        else:
            steps = iq - jk
            valid = steps >= 0
        bias = jnp.where(valid, steps.astype(F32) * (-float(dil)), -MASK_BIAS)
        s = s_scr[...] + _head_slopes(heads) * bias[None]
        m = jnp.max(s, axis=-1, keepdims=True)
        p = jnp.exp(s - m)
        l = jnp.sum(p, axis=-1, keepdims=True)
        p_scr[...] = p.astype(BF16)
        inv = 1.0 / l
        lse = m + jnp.log(l)

        lane = lax.broadcasted_iota(jnp.int32, (blk, LANES), 1)
        stat = jnp.zeros((blk, LANES), F32)
        for hp in range(heads // 2):
            sl = slice(LANES * hp, LANES * (hp + 1))
            vcat = v_st[:, sl]
            pv_a = jnp.dot(p_scr[2 * hp], vcat, preferred_element_type=F32) * inv[2 * hp]
            pv_b = jnp.dot(p_scr[2 * hp + 1], vcat, preferred_element_type=F32) * inv[2 * hp + 1]
            o_ref[:, sl] = jnp.where(low, pv_a, pv_b).astype(BF16)
            stat = jnp.where(lane == 2 * hp, lse[2 * hp], stat)
            stat = jnp.where(lane == 2 * hp + 1, lse[2 * hp + 1], stat)
        lse_ref[...] = stat

    q4, k4, v4 = (_attn_array(x, dims, dil) for x in (q, k, v))
    rs = _attn_residues(dims, dil)
    per_step = lambda shape: shape if rs == 1 else (rs,) + shape
    o, lse = pl.pallas_call(
        _per_residue(body, rs), name=name, grid=grid,
        in_specs=[cur, cur, cur] + ([prev, prev] if has_prev else []),
        out_specs=[cur, cur_stat],
        out_shape=[jax.ShapeDtypeStruct(q4.shape, BF16), jax.ShapeDtypeStruct(q4.shape[:-1] + (LANES,), F32)],
        scratch_shapes=[pltpu.VMEM(per_step((heads, blk, nkeys)), F32), pltpu.VMEM(per_step((heads, blk, nkeys)), BF16)]
        + ([pltpu.VMEM(per_step((nkeys, a)), BF16)] * 2 if has_prev else []),
        compiler_params=_params(*["parallel"] * len(grid)),
    )(q4, k4, v4, *([k4, v4] if has_prev else []))
    return o.reshape(q.shape), lse.reshape(q.shape[:-1] + (LANES,))


def _attn_combine(groups, head_spread, dims, *, name, tr=256):
    t = dims.tokens
    a = dims.n_heads * dims.head_dim
    dils = tuple(groups)

    nres = len(RESIDUE_DILATIONS)

    def body(*refs):
        ins = refs[:2 * len(dils)]
        x_ref = refs[2 * len(dils)]
        back_refs = dict(zip(RESIDUE_DILATIONS, refs[2 * len(dils) + 1:2 * len(dils) + 1 + nres]))
        o_ref = refs[2 * len(dils) + 1 + nres]
        lse_refs = refs[2 * len(dils) + 2 + nres:-1]
        scr_stat = refs[-1]
        outs, stats = [], []
        for g, d in enumerate(dils):
            if d == 1:
                outs.append(ins[2 * g][...].astype(F32))
                stats.append(ins[2 * g + 1][...])
            else:
                outs.append(_bf16_residues_to_rows(ins[2 * g], back_refs[d]))
                stats.append(_residues_to_rows(ins[2 * g + 1], scr_stat, d))
        top = functools.reduce(jnp.maximum, stats)
        weights = [jnp.exp(s - top) for s in stats]
        total = functools.reduce(jnp.add, weights)
        joint = top + jnp.log(total)
        inv = 1.0 / total
        acc = None
        for w, o in zip(weights, outs):
            term = _two_pass_dot(w * inv, x_ref[...]) * o
            acc = term if acc is None else acc + term
        o_ref[...] = acc.astype(BF16)
        for g, d in enumerate(dils):
            if d == 1:
                lse_refs[g][...] = joint
            else:
                _rows_to_residues(joint, lse_refs[g], scr_stat, d)

    in_specs, args, lse_specs, lse_shapes = [], [], [], []
    for d in dils:
        if d == 1:
            in_specs += [_row_spec(tr, a), _row_spec(tr, LANES)]
            lse_specs.append(_row_spec(tr, LANES))
            lse_shapes.append(jax.ShapeDtypeStruct((t, LANES), F32))
        else:
            in_specs += [_residue_spec(dims, d, tr, a), _residue_spec(dims, d, tr, LANES)]
            lse_specs.append(_residue_spec(dims, d, tr, LANES))
            lse_shapes.append(_residue_shape(dims, d, LANES, F32))
        args += list(groups[d])
    outs = pl.pallas_call(
        body, name=name, grid=(t // tr,),
        in_specs=in_specs + [pl.BlockSpec((LANES, a), lambda i: (0, 0))] + [pl.BlockSpec((tr, tr), lambda i: (0, 0))] * nres,
        out_specs=[_row_spec(tr, a)] + lse_specs,
        out_shape=[jax.ShapeDtypeStruct((t, a), BF16)] + lse_shapes,
        scratch_shapes=[pltpu.VMEM((1, tr, LANES), F32)],
        compiler_params=_params("parallel"),
    )(*args, head_spread, *[jnp.transpose(p) for p in _residue_permutations(tr)])
    return outs[0], dict(zip(dils, outs[1:]))


def _attn_bwd_prep(dy, w, o, head_sum, dims, *, name, after=None, tr=256):
    t, a = o.shape
    nres = len(RESIDUE_DILATIONS)

    def body(dy_ref, w_ref, o_ref, e_ref, *rest):
        perm_refs, outs, scr_stat = rest[:nres], rest[nres:-1], rest[-1]
        do = _dot_nt(dy_ref[...], w_ref[...]).astype(BF16)
        outs[0][...] = do
        delta = _two_pass_dot(do.astype(F32) * o_ref[...].astype(F32), e_ref[...])
        outs[1][...] = delta
        for g, d in enumerate(RESIDUE_DILATIONS):
            _bf16_rows_to_residues(do, outs[2 + 2 * g], perm_refs[g], d)
            _rows_to_residues(delta, outs[3 + 2 * g], scr_stat, d)

    out_specs = [_row_spec(tr, a), _row_spec(tr, LANES)]
    out_shape = [jax.ShapeDtypeStruct((t, a), BF16), jax.ShapeDtypeStruct((t, LANES), F32)]
    for d in RESIDUE_DILATIONS:
        out_specs += [_residue_spec(dims, d, tr, a), _residue_spec(dims, d, tr, LANES)]
        out_shape += [_residue_shape(dims, d, a, BF16), _residue_shape(dims, d, LANES, F32)]
    n_in = 4 + nres
    body, more_specs, more_args = _ordered(body, n_in, after)
    outs = pl.pallas_call(
        body, name=name, grid=(t // tr,),
        in_specs=[_row_spec(tr, dy.shape[1]), pl.BlockSpec((None,) + w.shape[1:], lambda i: (0, 0, 0)), _row_spec(tr, a),
                  pl.BlockSpec((a, LANES), lambda i: (0, 0))] + [pl.BlockSpec((tr, tr), lambda i: (0, 0))] * nres + more_specs,
        out_specs=out_specs, out_shape=out_shape,
        scratch_shapes=[pltpu.VMEM((1, tr, LANES), F32)],
        compiler_params=_params("parallel"),
    )(dy, w, o, head_sum, *_residue_permutations(tr), *more_args)
    dos, deltas = {1: outs[0]}, {1: outs[1]}
    for g, d in enumerate(RESIDUE_DILATIONS):
        dos[d], deltas[d] = outs[2 + 2 * g], outs[3 + 2 * g]
    return dos, deltas


def _attn_bwd(q, k, v, do, lse, delta, dims, dil, *, name):
    a = dims.n_heads * dims.head_dim
    heads, hd, blk = dims.n_heads, dims.head_dim, ATTN_BLOCK
    nb = dims.seq // dil // blk
    has_next = nb > 1
    nq = 2 * blk if has_next else blk
    grid, cur, _, nxt = _attn_specs(dims, dil, a)
    _, cur_stat, _, nxt_stat = _attn_specs(dims, dil, LANES)

    def body(*refs):
        k_ref, v_ref, q_ref, do_ref, lse_ref, dl_ref = refs[:6]
        if has_next:
            qn_ref, don_ref, lsen_ref, dln_ref = refs[6:10]
            dq_ref, dk_ref, dv_ref, q_st, do_st, s_scr, dp_scr, p_scr, ds_scr, carry = refs[10:]
        else:
            dq_ref, dk_ref, dv_ref, q_st, do_st, s_scr, dp_scr, p_scr, ds_scr = refs[6:]
        j = pl.program_id(len(grid) - 1)
        low, high = _pair_masks(hd)
        q_st[0:blk, :] = q_ref[...]
        do_st[0:blk, :] = do_ref[...]
        if has_next:
            q_st[blk:, :] = qn_ref[...]
            do_st[blk:, :] = don_ref[...]
            lse_all = jnp.concatenate([lse_ref[...], lsen_ref[...]], axis=0)
            dl_all = jnp.concatenate([dl_ref[...], dln_ref[...]], axis=0)
        else:
            lse_all, dl_all = lse_ref[...], dl_ref[...]
        lse_t, dl_t = jnp.transpose(lse_all), jnp.transpose(dl_all)
        lse3 = jnp.stack([lse_t[h:h + 1, :] for h in range(heads)])
        dl3 = jnp.stack([dl_t[h:h + 1, :] for h in range(heads)])

        def halves(x):
            return jnp.where(low, x, jnp.zeros_like(x)), jnp.where(high, x, jnp.zeros_like(x))

        for hp in range(heads // 2):
            sl = slice(LANES * hp, LANES * (hp + 1))
            k2, v2 = k_ref[:, sl], v_ref[:, sl]
            q_a, q_b = halves(q_st[:, sl])
            do_a, do_b = halves(do_st[:, sl])
            s_scr[2 * hp], s_scr[2 * hp + 1] = _dot_nt(k2, q_a), _dot_nt(k2, q_b)
            dp_scr[2 * hp], dp_scr[2 * hp + 1] = _dot_nt(v2, do_a), _dot_nt(v2, do_b)

        jk = lax.broadcasted_iota(jnp.int32, (blk, nq), 0)
        rq = lax.broadcasted_iota(jnp.int32, (blk, nq), 1)
        if has_next:
            iq = jnp.where(rq < blk, rq, rq - blk)
            steps = jnp.where(rq < blk, iq - jk, iq - jk + blk)
            valid = ((rq < blk) & (iq >= jk)) | ((rq >= blk) & (jk >= iq) & (j + 1 < nb))
        else:
            steps, valid = rq - jk, rq >= jk
        bias = jnp.where(valid, steps.astype(F32) * (-float(dil)), -MASK_BIAS)
        p = jnp.exp(s_scr[...] + _head_slopes(heads) * bias[None] - lse3)
        p_scr[...] = p.astype(BF16)
        ds_scr[...] = (p * (dp_scr[...] - dl3)).astype(BF16)

        if has_next:
            @pl.when(j == 0)
            def _():
                carry[...] = jnp.zeros_like(carry)

        for hp in range(heads // 2):
            sl = slice(LANES * hp, LANES * (hp + 1))
            k2 = k_ref[:, sl]
            q_a, q_b = halves(q_st[:, sl])
            do_a, do_b = halves(do_st[:, sl])
            ds_a, ds_b = ds_scr[2 * hp], ds_scr[2 * hp + 1]
            dk_ref[:, sl] = (jnp.dot(ds_a, q_a, preferred_element_type=F32)
                             + jnp.dot(ds_b, q_b, preferred_element_type=F32)).astype(BF16)
            dv_ref[:, sl] = (jnp.dot(p_scr[2 * hp], do_a, preferred_element_type=F32)
                             + jnp.dot(p_scr[2 * hp + 1], do_b, preferred_element_type=F32)).astype(BF16)
            dq2 = jnp.where(low, _dot_tn(ds_a, k2), _dot_tn(ds_b, k2))
            if has_next:
                dq_ref[:, sl] = (carry[:, sl] + dq2[:blk]).astype(BF16)
                carry[:, sl] = dq2[blk:]
            else:
                dq_ref[:, sl] = dq2.astype(BF16)

    args, in_specs = [_attn_array(x, dims, dil) for x in (k, v, q, do, lse, delta)], [cur] * 4 + [cur_stat] * 2
    if has_next:
        args += [args[2], args[3], args[4], args[5]]
        in_specs += [nxt] * 2 + [nxt_stat] * 2
    shape = jax.ShapeDtypeStruct(args[2].shape, BF16)
    rs = _attn_residues(dims, dil)
    per_step = lambda dims_: dims_ if rs == 1 else (rs,) + dims_
    scratch = ([pltpu.VMEM(per_step((nq, a)), BF16)] * 2 + [pltpu.VMEM(per_step((heads, blk, nq)), F32)] * 2
               + [pltpu.VMEM(per_step((heads, blk, nq)), BF16)] * 2)
    if has_next:
        scratch.append(pltpu.VMEM(per_step((blk, a)), F32))
    grads = pl.pallas_call(
        _per_residue(body, rs), name=name, grid=grid, in_specs=in_specs, out_specs=[cur] * 3, out_shape=[shape] * 3,
        scratch_shapes=scratch,
        compiler_params=_params(*["parallel"] * (len(grid) - 1), "arbitrary"),
    )(*args)
    return tuple(g.reshape(q.shape) for g in grads)


def _qkv_layouts_bwd(z, grads, gq, gk, head_ones, dims, *, name, tr=256):
    t = z.shape[0]
    a = dims.n_heads * dims.head_dim
    q_scale = dims.head_dim ** -0.5
    dils = tuple(grads)
    nres = len(RESIDUE_DILATIONS)

    def body(q_ref, k_ref, *rest):
        d_refs = rest[:3 * len(dils)]
        gq_ref, gk_ref, sum_ref, spread_ref = rest[3 * len(dils):3 * len(dils) + 4]
        back_refs = dict(zip(RESIDUE_DILATIONS, rest[3 * len(dils) + 4:3 * len(dils) + 4 + nres]))
        dz_ref, dgq_ref, dgk_ref = rest[3 * len(dils) + 4 + nres:]
        first = pl.program_id(0) == 0
        mean = lambda val: _two_pass_dot(_two_pass_dot(val, sum_ref[...]), spread_ref[...]) * (1.0 / dims.head_dim)

        def total(j):
            acc = None
            for g, d in enumerate(dils):
                ref = d_refs[3 * g + j]
                part = ref[...].astype(F32) if d == 1 else _bf16_residues_to_rows(ref, back_refs[d])
                acc = part if acc is None else acc + part
            return acc

        def norm_bwd(x_ref, dy, g_ref, scale, col, dg_ref):
            xv = x_ref[...].astype(F32)
            dy = dy * scale
            r = lax.rsqrt(mean(xv * xv) + RMS_EPS)
            gy = dy * g_ref[...]
            dx = r * gy - xv * (r * r * r) * mean(xv * gy)
            dz_ref[:, col * a:(col + 1) * a] = dx.astype(BF16)
            _accumulate(dg_ref, jnp.sum(dy * xv * r, axis=0, keepdims=True), first)

        norm_bwd(q_ref, total(0), gq_ref, q_scale, 0, dgq_ref)
        norm_bwd(k_ref, total(1), gk_ref, 1.0, 1, dgk_ref)
        dz_ref[:, 2 * a:3 * a] = total(2).astype(BF16)

    in_specs, args = [_row_spec(tr, a, 2), _row_spec(tr, a, 3)], [z, z]
    for d in dils:
        in_specs += [_row_spec(tr, a) if d == 1 else _residue_spec(dims, d, tr, a)] * 3
        args += list(grads[d])
    in_specs += [_vec_spec(a), _vec_spec(a), pl.BlockSpec((a, LANES), lambda i: (0, 0)),
                 pl.BlockSpec((LANES, a), lambda i: (0, 0))] + [pl.BlockSpec((tr, tr), lambda i: (0, 0))] * nres
    return pl.pallas_call(
        body, name=name, grid=(t // tr,), in_specs=in_specs,
        out_specs=[_row_spec(tr, 3 * a), _vec_spec(a), _vec_spec(a)],
        out_shape=[jax.ShapeDtypeStruct((t, 3 * a), BF16)] + [jax.ShapeDtypeStruct((1, a), F32)] * 2,
        compiler_params=_params("arbitrary"),
    )(*args, gq, gk, *head_ones, *[jnp.transpose(p) for p in _residue_permutations(tr)])


def _mix_fwd(a3, o, w_a, w_b, w_out, z, gate_b, x, g2, dims, *, name, tr=512):
    t, d = x.shape
    tr = _pick(t, tr, 8)
    first_gate_col = z.shape[1] // d - 2

    def body(a_ref, o_ref, wa_ref, wb_ref, wo_ref, ga_ref, gb_ref, ba_ref, bb_ref, x_ref, g2_ref,
             ya_ref, yb_ref, mix_ref, x1_ref, h2_ref):
        ya = jnp.dot(a_ref[...], wa_ref[...], preferred_element_type=F32)
        yb = jnp.dot(o_ref[...], wb_ref[...], preferred_element_type=F32)
        ya_ref[...] = ya
        yb_ref[...] = yb
        g_a = _sigmoid(ga_ref[...].astype(F32) + ba_ref[...])
        g_b = _sigmoid(gb_ref[...].astype(F32) + bb_ref[...])
        mixed = (g_a * ya + g_b * yb).astype(BF16)
        mix_ref[...] = mixed
        x1 = x_ref[...] + jnp.dot(mixed, wo_ref[...], preferred_element_type=F32)
        x1_ref[...] = x1
        h2_ref[...] = (x1 * lax.rsqrt(jnp.mean(x1 * x1, axis=-1, keepdims=True) + RMS_EPS) * g2_ref[...]).astype(BF16)

    weight = pl.BlockSpec((None, d, d), lambda i: (0, 0, 0))
    rows = _row_spec(tr, d)
    return pl.pallas_call(
        body, name=name, grid=(t // tr,),
        in_specs=[rows, rows, weight, weight, weight, _row_spec(tr, d, first_gate_col),
                  _row_spec(tr, d, first_gate_col + 1), _vec_spec(d, 0), _vec_spec(d, 1), rows, _vec_spec(d)],
        out_specs=[rows] * 5,
        out_shape=[jax.ShapeDtypeStruct((t, d), dt) for dt in (F32, F32, BF16, F32, BF16)],
        compiler_params=_params("parallel"),
    )(a3, o, w_a, w_b, w_out, z, z, gate_b, gate_b, x, g2)


def _mix_bwd(dx, w, ya, yb, z, gate_b, dims, *, name, after=None, tr=512):
    t, d = ya.shape
    tr = _pick(t, tr, 8)
    first_gate_col = z.shape[1] // d - 2

    def body(dx_ref, w_ref, ya_ref, yb_ref, ga_ref, gb_ref, ba_ref, bb_ref, dya_ref, dyb_ref, dz_ref, db_ref):
        dm = _dot_nt(dx_ref[...], w_ref[...])
        g_a = _sigmoid(ga_ref[...].astype(F32) + ba_ref[...])
        g_b = _sigmoid(gb_ref[...].astype(F32) + bb_ref[...])
        dya_ref[...] = (dm * g_a).astype(BF16)
        dyb_ref[...] = (dm * g_b).astype(BF16)
        dl_a = dm * ya_ref[...] * g_a * (1.0 - g_a)
        dl_b = dm * yb_ref[...] * g_b * (1.0 - g_b)
        dz_ref[:, 0:d] = dl_a.astype(BF16)
        dz_ref[:, d:2 * d] = dl_b.astype(BF16)
        first = pl.program_id(0) == 0
        sums = jnp.concatenate([jnp.sum(dl_a, axis=0, keepdims=True), jnp.sum(dl_b, axis=0, keepdims=True)], axis=1)
        _accumulate(db_ref, sums, first)

    body, more_specs, more_args = _ordered(body, 8, after)
    return pl.pallas_call(
        body, name=name, grid=(t // tr,),
        in_specs=[_row_spec(tr, d), pl.BlockSpec((None, d, d), lambda i: (0, 0, 0)), _row_spec(tr, d), _row_spec(tr, d),
                  _row_spec(tr, d, first_gate_col), _row_spec(tr, d, first_gate_col + 1), _vec_spec(d, 0),
                  _vec_spec(d, 1)] + more_specs,
        out_specs=[_row_spec(tr, d), _row_spec(tr, d), _row_spec(tr, 2 * d), _vec_spec(2 * d)],
        out_shape=[jax.ShapeDtypeStruct((t, d), BF16)] * 2 + [jax.ShapeDtypeStruct((t, 2 * d), BF16),
                                                              jax.ShapeDtypeStruct((1, 2 * d), F32)],
        compiler_params=_params("arbitrary"),
    )(dx, w, ya, yb, z, z, gate_b, gate_b, *more_args)


def _adamw(w, grads, m, v, *, name, tr=256):
    r, c = w.shape
    tr = _pick(r, tr, 8)
    ng = len(grads)
    c1 = 1.0 - ADAM_B1 ** ADAM_STEP
    c2 = 1.0 - ADAM_B2 ** ADAM_STEP

    def body(*refs):
        w_ref, g_refs, m_ref, v_ref = refs[0], refs[1:1 + ng], refs[1 + ng], refs[2 + ng]
        g_out, d_out, m_out, v_out = refs[3 + ng:]
        g = g_refs[0][...]
        for extra in g_refs[1:]:
            g = g + extra[...]
        m_new = ADAM_B1 * m_ref[...] + (1.0 - ADAM_B1) * g
        v_new = ADAM_B2 * v_ref[...] + (1.0 - ADAM_B2) * (g * g)
        g_out[...] = g
        m_out[...] = m_new
        v_out[...] = v_new
        d_out[...] = -ADAM_LR * ((m_new / c1) / (jnp.sqrt(v_new / c2) + ADAM_EPS) + ADAM_WD * w_ref[...])

    spec = pl.BlockSpec((tr, c), lambda i: (i, 0))
    return pl.pallas_call(
        body, name=name, grid=(r // tr,),
        in_specs=[spec] * (3 + ng), out_specs=[spec] * 4, out_shape=[jax.ShapeDtypeStruct((r, c), F32)] * 4,
        compiler_params=_params("parallel"),
    )(w, *grads, m, v)


def _adamw_unpacked(ws, g, ms, vs, *, name):
    r, d = g.shape
    c1 = 1.0 - ADAM_B1 ** ADAM_STEP
    c2 = 1.0 - ADAM_B2 ** ADAM_STEP
    shapes = [a.shape for a in ws]
    views = [tuple(s) if len(s) == 2 else (1, s[0]) for s in shapes]
    n = len(views)

    def pieces():
        row = 0
        for i, (rows, width) in enumerate(views):
            for a in range(rows):
                pos = 0
                while pos < width:
                    at, lane = row + (a * width + pos) // d, (a * width + pos) % d
                    piece = min(width - pos, d - lane)
                    yield i, a, pos, at, lane, piece
                    pos += piece
            row += _packed_rows(rows * width, d)

    def body(*refs):
        ins, g_ref, outs, packed, scr = refs[:3 * n], refs[3 * n], refs[3 * n + 1:-2], refs[-2], refs[-1]
        packed[...] = jnp.zeros_like(packed)
        for kind in range(3):
            for i, a, pos, at, lane, piece in pieces():
                packed[kind, at:at + 1, lane:lane + piece] = ins[kind * n + i][a:a + 1, pos:pos + piece]
        grad = g_ref[...]
        m_new = ADAM_B1 * packed[1] + (1.0 - ADAM_B1) * grad
        v_new = ADAM_B2 * packed[2] + (1.0 - ADAM_B2) * (grad * grad)
        scr[0] = grad
        scr[1] = -ADAM_LR * ((m_new / c1) / (jnp.sqrt(v_new / c2) + ADAM_EPS) + ADAM_WD * packed[0])
        scr[2] = m_new
        scr[3] = v_new
        for kind in range(4):
            for i, a, pos, at, lane, piece in pieces():
                outs[kind * n + i][a:a + 1, pos:pos + piece] = scr[kind, at:at + 1, lane:lane + piece]

    whole = pl.BlockSpec(memory_space=pltpu.VMEM)
    outs = pl.pallas_call(
        body, name=name, in_specs=[whole] * (3 * n + 1), out_specs=[whole] * (4 * n),
        out_shape=[jax.ShapeDtypeStruct(view, F32) for _ in range(4) for view in views],
        scratch_shapes=[pltpu.VMEM((3, r, d), F32), pltpu.VMEM((4, r, d), F32)], compiler_params=_params(),
    )(*[a.reshape(view) for grp in (ws, ms, vs) for a, view in zip(grp, views)], g)
    return [[o.reshape(s) for o, s in zip(outs[kind * n:(kind + 1) * n], shapes)] for kind in range(4)]


CHIP_PEERS = ((1, 0), (0, 1), (1, 1))


def _place():
    return lax.axis_index("x"), lax.axis_index("y"), lax.axis_index("c")


HBM = pl.BlockSpec(memory_space=pltpu.HBM)
SEM = pl.BlockSpec(memory_space=pltpu.SEMAPHORE)
IN_FLIGHT = pltpu.SideEffectType.DATAFLOW_SIDE_EFFECTING


def _in_hbm(a):
    return pltpu.with_memory_space_constraint(a, pltpu.HBM)


def _cast_to_lands(shards, dtypes, *, name, after=None):
    n = len(shards)

    def body(*refs):
        ins, outs, bufs, sems = refs[:n], refs[n:2 * n], refs[2 * n:3 * n], refs[3 * n]
        x, y, _ = _place()
        copies = []
        for a in range(n):
            bufs[a][...] = ins[a][...].astype(dtypes[a])
            cp = pltpu.make_async_copy(bufs[a], outs[a].at[2 * x + y], sems.at[a])
            cp.start()
            copies.append(cp)
        for cp in copies:
            cp.wait()

    body, more_specs, more_args = _ordered(body, n, after)
    return pl.pallas_call(
        body, name=name, in_specs=[pl.BlockSpec(memory_space=pltpu.VMEM)] * n + more_specs, out_specs=[ANY] * n,
        out_shape=[jax.ShapeDtypeStruct((N_CHIPS,) + s.shape, dt) for s, dt in zip(shards, dtypes)],
        scratch_shapes=[pltpu.VMEM(s.shape, dt) for s, dt in zip(shards, dtypes)] + [pltpu.SemaphoreType.DMA((n,))],
        compiler_params=pltpu.CompilerParams(vmem_limit_bytes=V7X_VMEM_LIMIT_BYTES),
    )(*shards, *more_args)


def _chip_copy(src, dst, send, recv, flip, place):
    x, y, c = place
    return pltpu.make_async_remote_copy(src_ref=src, dst_ref=dst, send_sem=send, recv_sem=recv,
                                        device_id=(x ^ flip[0], y ^ flip[1], c), device_id_type=MESH)


def _my_part(land, place, halved):
    block = land.at[2 * place[0] + place[1]]
    if not halved:
        return block
    rows = land.shape[1] // 2
    return block.at[pl.ds(pl.multiple_of(place[2] * rows, rows), rows)]


def _gather_start(lands, after, *, name, halved=()):
    n = len(lands)

    def body(*refs):
        ins, send, recv, token = refs[:n], refs[n + 1], refs[n + 2], refs[-1]
        place = _place()
        for a in range(n):
            part = _my_part(ins[a], place, a in halved)
            for p, flip in enumerate(CHIP_PEERS):
                k = 3 * a + p
                _chip_copy(part, part, send.at[k], recv.at[k], flip, place).start()
        token[...] = jnp.zeros_like(token)

    outs = pl.pallas_call(
        body, name=name, in_specs=[HBM] * n + [ANY],
        out_specs=(SEM, SEM, *[HBM] * n, pl.BlockSpec(memory_space=pltpu.VMEM)),
        out_shape=(pltpu.SemaphoreType.DMA((3 * n,)), pltpu.SemaphoreType.DMA((3 * n,)),
                   *[pltpu.HBM(l.shape, l.dtype) for l in lands], jax.ShapeDtypeStruct((8, 128), F32)),
        input_output_aliases={a: 2 + a for a in range(n)},
        compiler_params=pltpu.CompilerParams(has_side_effects=IN_FLIGHT),
    )(*[_in_hbm(l) for l in lands], after)
    return outs[0], outs[1], list(outs[2:2 + n]), outs[-1]


def _gather_wait(send, recv, lands, after, *, name, halved=()):
    n = len(lands)

    def body(*refs):
        ins, send_ref, recv_ref = refs[:n], refs[n], refs[n + 1]
        place = _place()
        for a in range(n):
            part = _my_part(ins[a], place, a in halved)
            for p, flip in enumerate(CHIP_PEERS):
                k = 3 * a + p
                cp = _chip_copy(part, part, send_ref.at[k], recv_ref.at[k], flip, place)
                cp.wait_send()
                cp.wait_recv()

    after = list(after) if isinstance(after, (list, tuple)) else [after]
    return pl.pallas_call(
        body, name=name, in_specs=[HBM] * n + [SEM, SEM] + [ANY] * len(after), out_specs=[HBM] * n,
        out_shape=[pltpu.HBM(l.shape, l.dtype) for l in lands],
        input_output_aliases={a: a for a in range(n)},
        compiler_params=pltpu.CompilerParams(has_side_effects=IN_FLIGHT),
    )(*lands, send, recv, *after)


def _forward_to_sibling(land, *, name):
    rows = land.shape[1] // 2

    def body(land_ref, out_ref, send, recv):
        x, y, c = _place()
        copies = []
        for p, (fx, fy) in enumerate(CHIP_PEERS):
            chip = 2 * (x ^ fx) + (y ^ fy)
            mine = pl.ds(pl.multiple_of(c * rows, rows), rows)
            theirs = pl.ds(pl.multiple_of((1 - c) * rows, rows), rows)
            out = pltpu.make_async_remote_copy(
                src_ref=land_ref.at[chip].at[mine], dst_ref=out_ref.at[chip].at[mine], send_sem=send.at[p],
                recv_sem=recv.at[p], device_id=(x, y, 1 - c), device_id_type=MESH)
            out.start()
            copies.append((out, pltpu.make_async_remote_copy(
                src_ref=land_ref.at[chip].at[theirs], dst_ref=out_ref.at[chip].at[theirs], send_sem=send.at[p],
                recv_sem=recv.at[p], device_id=(x, y, 1 - c), device_id_type=MESH)))
        for out, arriving in copies:
            out.wait_send()
            arriving.wait_recv()

    return pl.pallas_call(
        body, name=name, in_specs=[ANY], out_specs=ANY, out_shape=jax.ShapeDtypeStruct(land.shape, land.dtype),
        input_output_aliases={0: 0},
        scratch_shapes=[pltpu.SemaphoreType.DMA((3,)), pltpu.SemaphoreType.DMA((3,))],
    )(land)


def _scatter_start(grad, *, name):
    def body(g_ref, land_ref, send, recv, g_thru, land_thru, token):
        place = _place()
        for p, flip in enumerate(CHIP_PEERS):
            peer_chip = 2 * (place[0] ^ flip[0]) + (place[1] ^ flip[1])
            _chip_copy(g_ref.at[peer_chip], land_ref.at[p], send.at[p], recv.at[p], flip, place).start()
        token[...] = jnp.zeros_like(token)

    land = lax.empty((3,) + grad.shape[1:], grad.dtype)
    return pl.pallas_call(
        body, name=name, in_specs=[HBM, HBM],
        out_specs=(SEM, SEM, HBM, HBM, pl.BlockSpec(memory_space=pltpu.VMEM)),
        out_shape=(pltpu.SemaphoreType.DMA((3,)), pltpu.SemaphoreType.DMA((3,)), pltpu.HBM(grad.shape, grad.dtype),
                   pltpu.HBM(land.shape, land.dtype), jax.ShapeDtypeStruct((8, 128), F32)),
        input_output_aliases={0: 2, 1: 3},
        compiler_params=pltpu.CompilerParams(has_side_effects=IN_FLIGHT),
    )(_in_hbm(grad), _in_hbm(land))


def _scatter_wait(started, after, *, name):
    n = len(started)

    def body(*refs):
        grads, lands = refs[:n], refs[n:2 * n]
        sends, recvs = refs[2 * n:3 * n], refs[3 * n:4 * n]
        place = _place()
        for a in range(n):
            for p, flip in enumerate(CHIP_PEERS):
                cp = _chip_copy(grads[a].at[0], lands[a].at[p], sends[a].at[p], recvs[a].at[p], flip, place)
                cp.wait_send()
                cp.wait_recv()

    grads, lands = [s[2] for s in started], [s[3] for s in started]
    after = list(after) if isinstance(after, (list, tuple)) else [after]
    outs = pl.pallas_call(
        body, name=name, in_specs=[HBM] * (2 * n) + [SEM] * (2 * n) + [ANY] * len(after), out_specs=[HBM] * (2 * n),
        out_shape=[pltpu.HBM(a.shape, a.dtype) for a in grads + lands],
        input_output_aliases={a: a for a in range(2 * n)},
        compiler_params=pltpu.CompilerParams(has_side_effects=IN_FLIGHT),
    )(*grads, *lands, *[s[0] for s in started], *[s[1] for s in started], *after)
    return list(zip(outs[:n], outs[n:]))


def _sibling_copy(src, dst, send, recv, place):
    x, y, c = place
    return pltpu.make_async_remote_copy(src_ref=src, dst_ref=dst, send_sem=send, recv_sem=recv,
                                        device_id=(x, y, 1 - c), device_id_type=MESH)


def _swap_start(arrays, *, name):
    n = len(arrays)

    def body(*refs):
        ins, lands, send, recv, token = refs[:n], refs[n:2 * n], refs[2 * n], refs[2 * n + 1], refs[-1]
        place = _place()
        for a in range(n):
            _sibling_copy(ins[a], lands[a], send.at[a], recv.at[a], place).start()
        token[...] = jnp.zeros_like(token)

    both = [_in_hbm(a) for a in arrays] + [_in_hbm(lax.empty(a.shape, a.dtype)) for a in arrays]
    outs = pl.pallas_call(
        body, name=name, in_specs=[HBM] * (2 * n),
        out_specs=(SEM, SEM, *[HBM] * (2 * n), pl.BlockSpec(memory_space=pltpu.VMEM)),
        out_shape=(pltpu.SemaphoreType.DMA((n,)), pltpu.SemaphoreType.DMA((n,)),
                   *[pltpu.HBM(a.shape, a.dtype) for a in both], jax.ShapeDtypeStruct((8, 128), F32)),
        input_output_aliases={a: 2 + a for a in range(2 * n)},
        compiler_params=pltpu.CompilerParams(has_side_effects=IN_FLIGHT),
    )(*both)
    return outs[0], outs[1], list(outs[2:2 + n]), list(outs[2 + n:2 + 2 * n]), outs[-1]


def _swap_wait(started, after, *, name):
    send, recv, arrays, lands = started[:4]
    n = len(arrays)

    def body(*refs):
        ins, zones, send_ref, recv_ref = refs[:n], refs[n:2 * n], refs[2 * n], refs[2 * n + 1]
        place = _place()
        for a in range(n):
            cp = _sibling_copy(ins[a], zones[a], send_ref.at[a], recv_ref.at[a], place)
            cp.wait_send()
            cp.wait_recv()

    after = list(after) if isinstance(after, (list, tuple)) else [after]
    outs = pl.pallas_call(
        body, name=name, in_specs=[HBM] * (2 * n) + [SEM, SEM] + [ANY] * len(after), out_specs=[HBM] * (2 * n),
        out_shape=[pltpu.HBM(a.shape, a.dtype) for a in arrays + lands],
        input_output_aliases={a: a for a in range(2 * n)},
        compiler_params=pltpu.CompilerParams(has_side_effects=IN_FLIGHT),
    )(*arrays, *lands, send, recv, *after)
    return list(outs[:n]), list(outs[n:])


def _allreduce_start(packed, *, name):
    n_dev = 8

    def body(src_ref, land_ref, send, recv, src_thru, land_thru, token):
        x, y, c = _place()
        me = 4 * x + 2 * y + c
        for p in range(1, n_dev):
            pltpu.make_async_remote_copy(
                src_ref=src_ref, dst_ref=land_ref.at[me], send_sem=send.at[p - 1], recv_sem=recv.at[p - 1],
                device_id=(x ^ (p >> 2), y ^ ((p >> 1) & 1), c ^ (p & 1)), device_id_type=MESH).start()
        token[...] = jnp.zeros_like(token)

    land = lax.empty((n_dev,) + packed.shape, packed.dtype)
    return pl.pallas_call(
        body, name=name, in_specs=[HBM, HBM],
        out_specs=(SEM, SEM, HBM, HBM, pl.BlockSpec(memory_space=pltpu.VMEM)),
        out_shape=(pltpu.SemaphoreType.DMA((n_dev - 1,)), pltpu.SemaphoreType.DMA((n_dev - 1,)),
                   pltpu.HBM(packed.shape, packed.dtype), pltpu.HBM(land.shape, land.dtype),
                   jax.ShapeDtypeStruct((8, 128), F32)),
        input_output_aliases={0: 2, 1: 3},
        compiler_params=pltpu.CompilerParams(has_side_effects=IN_FLIGHT),
    )(_in_hbm(packed), _in_hbm(land))


def _allreduce_wait(started, after, *, name):
    send, recv, packed, land = started[:4]
    n_dev = 8

    def body(src_ref, land_ref, send_ref, recv_ref, *_):
        x, y, c = _place()
        for p in range(1, n_dev):
            cp = pltpu.make_async_remote_copy(
                src_ref=src_ref, dst_ref=land_ref.at[0], send_sem=send_ref.at[p - 1], recv_sem=recv_ref.at[p - 1],
                device_id=(x ^ (p >> 2), y ^ ((p >> 1) & 1), c ^ (p & 1)), device_id_type=MESH)
            cp.wait_send()
            cp.wait_recv()

    after = list(after) if isinstance(after, (list, tuple)) else [after]
    return pl.pallas_call(
        body, name=name, in_specs=[HBM, HBM, SEM, SEM] + [ANY] * len(after), out_specs=[HBM, HBM],
        out_shape=[pltpu.HBM(packed.shape, packed.dtype), pltpu.HBM(land.shape, land.dtype)],
        input_output_aliases={0: 0, 1: 1},
        compiler_params=pltpu.CompilerParams(has_side_effects=IN_FLIGHT),
    )(packed, land, send, recv, *after)


def _sum_devices(mine, land, *, name):
    n_dev = land.shape[0]

    def body(mine_ref, land_ref, out_ref):
        x, y, c = _place()
        me = 4 * x + 2 * y + c
        total = None
        for s in range(n_dev):
            part = jnp.where(me == s, mine_ref[...], land_ref[s])
            total = part if total is None else total + part
        out_ref[...] = total

    return pl.pallas_call(body, name=name, out_shape=jax.ShapeDtypeStruct(mine.shape, mine.dtype))(mine, land)


def _sum_received(grad, land, *, name, tr=256):
    _, r, c = grad.shape
    tr = _pick(r, tr, 8)

    def body(chip_ref, g_ref, l_ref, o_ref):
        o_ref[...] = ((g_ref[...] + l_ref[0].astype(F32)) + l_ref[1].astype(F32)) + l_ref[2].astype(F32)

    chip = (2 * lax.axis_index("x") + lax.axis_index("y")).astype(jnp.int32).reshape(1)
    return pl.pallas_call(
        body, name=name,
        grid_spec=pltpu.PrefetchScalarGridSpec(
            num_scalar_prefetch=1, grid=(r // tr,),
            in_specs=[pl.BlockSpec((None, tr, c), lambda i, chip_ref: (chip_ref[0], i, 0)),
                      pl.BlockSpec((3, tr, c), lambda i, chip_ref: (0, i, 0))],
            out_specs=pl.BlockSpec((tr, c), lambda i, chip_ref: (i, 0))),
        out_shape=jax.ShapeDtypeStruct((r, c), F32), compiler_params=_params("parallel"),
    )(chip, grad, land)


def _packed_rows(size, d):
    return -(-size // (8 * d)) * 8


def _pack_rows(arrays, d):
    rows = []
    for arr in arrays:
        flat = arr.reshape(-1).astype(F32)
        n = _packed_rows(flat.shape[0], d)
        rows.append(jnp.pad(flat, (0, n * d - flat.shape[0])).reshape(n, d))
    return jnp.concatenate(rows, axis=0)


def _unpack_rows(packed, shapes, d):
    out, row = [], 0
    for shape in shapes:
        size = math.prod(shape)
        n = _packed_rows(size, d)
        out.append(packed[row:row + n].reshape(-1)[:size].reshape(shape))
        row += n
    return out


SMALL = ("norm1_g", "gate_b", "conv_b", "conv_norm_g", "q_norm_g", "k_norm_g", "norm2_g", "ffn_conv_b")
LARGE = ("w_in", "w_conv_out", "w_attn_out", "w_out", "w_up", "w_down")
WEIGHTS = ("norm1_g", "w_in", "gate_b", "conv_w", "conv_b", "conv_norm_g", "w_conv_out", "q_norm_g", "k_norm_g",
           "w_attn_out", "w_out", "norm2_g", "w_up", "ffn_conv_w", "ffn_conv_b", "w_down")


def _after(vec, token):
    return vec if token is None else vec + token[0:1, 0:1]


def _local_step(dims, x, target, small, first_weights, other_weights, send_grad):
    d, f, heads = dims.d_model, dims.d_ff, dims.n_heads
    small = dict(small)
    row = lambda name: small[name].reshape(1, -1)
    head_sum = _head_sum_matrix(dims)
    head_spread = jnp.transpose(head_sum)
    ones = (head_sum, head_spread)
    gq = jnp.tile(row("q_norm_g"), (1, heads))
    gk = jnp.tile(row("k_norm_g"), (1, heads))
    one_shard = lambda w: w.reshape(1, -1, w.shape[-1])

    h = _rmsnorm_fwd(x, row("norm1_g"), name="norm1")
    full = first_weights(h)
    w_in = full["w_in"]
    conv_w = jnp.pad(full["conv_w"], ((0, CONV_HALO - dims.conv_width), (0, 0)))
    ffn_w = jnp.pad(full["ffn_conv_w"], ((0, FFN_HALO - dims.ffn_conv_width), (0, 0)))
    z = _mm_nn(h, w_in, out_dtype=BF16, after=full.get("token"), tm=2048, tn=1792, name="in_proj")
    a1, a3 = _conv_branch_fwd(z, conv_w, row("conv_b"), row("conv_norm_g"), dims, name="conv_branch")
    qkv = _qkv_layouts_fwd(z, gq, gk, ones, dims, name="qk_norm")
    per_group = {dil: _attn_fwd(*qkv[dil], dims, dil, name=f"attn_fwd_d{dil}") for dil in DILATIONS}
    o, lse = _attn_combine(per_group, head_spread, dims, name="attn_combine")
    full = other_weights(o)
    w_up = full["w_up"]
    w_co, w_ao, w_o, w_dn = (one_shard(full[k]) for k in ("w_conv_out", "w_attn_out", "w_out", "w_down"))
    ya, yb, mixed, x1, h2 = _mix_fwd(a3, o, w_co, w_ao, w_o, z, row("gate_b"), x, row("norm2_g"), dims,
                                     name="branch_projs_mix_out_proj_norm2")
    up = _mm_nn(h2, w_up, out_dtype=F32, tm=2048, name="up_proj")
    act = _ffn_act_fwd(up, ffn_w, row("ffn_conv_b"), dims, name="ffn_act")
    dy, dy_b, loss = _proj_residual_loss(act, w_dn, x1, target, tm=512, name="down_proj_loss")

    grads = {}

    def large(name, g):
        grads[name], g_bf16 = g
        return send_grad(name, grads[name], g_bf16)

    sent = large("w_down", _mm_tn(act, dy_b, n_shards=1, name="dw_down"))
    dact = _mm_nt(dy_b, w_dn, out_dtype=BF16, after=sent, name="d_act")
    dup, dfw, dfb = _ffn_bwd(dact, up, ffn_w, row("ffn_conv_b"), dims, name="ffn_bwd")
    grads["ffn_conv_w"], grads["ffn_conv_b"] = dfw[:dims.ffn_conv_width], dfb
    sent = large("w_up", _mm_tn(h2, dup, n_shards=N_CHIPS, name="dw_up"))
    dx1, dx1_b, grads["norm2_g"] = _mm_nt_rmsnorm_bwd(dup, w_up, x1, row("norm2_g"), dy, want_bf16=True, after=sent,
                                                     name="d_h2_norm2_bwd")
    sent = large("w_out", _mm_tn(mixed, dx1_b, n_shards=1, name="dw_out"))
    dya, dyb, dz_gate, grads["gate_b"] = _mix_bwd(dx1_b, w_o, ya, yb, z, row("gate_b"), dims, after=sent,
                                                  name="d_mix_gate_mix_bwd")
    sent = large("w_attn_out", _mm_tn(o, dyb, n_shards=1, name="dw_attn_out"))
    dos, deltas = _attn_bwd_prep(dyb, w_ao, o, head_sum, dims, after=sent, name="d_attn_bwd_prep")
    dqkv = {dil: _attn_bwd(*qkv[dil], dos[dil], lse[dil], deltas[dil], dims, dil, name=f"attn_bwd_d{dil}")
            for dil in DILATIONS}
    dz_qkv, dgq, dgk = _qkv_layouts_bwd(z, dqkv, gq, gk, ones, dims, name="qk_norm_bwd")
    grads["q_norm_g"] = dgq.reshape(heads, dims.head_dim).sum(axis=0)
    grads["k_norm_g"] = dgk.reshape(heads, dims.head_dim).sum(axis=0)
    sent = large("w_conv_out", _mm_tn(a3, dya, n_shards=1, name="dw_conv_out"))
    da1, grads["conv_norm_g"] = _mm_nt_rmsnorm_bwd(dya, w_co, a1, row("conv_norm_g"), None, want_bf16=False, silu=True,
                                                   after=sent, name="d_conv_act_norm_bwd")
    dz, dcw, grads["conv_b"] = _conv_branch_bwd(da1, z, conv_w, [dz_qkv, dz_gate], dims, name="conv_branch_bwd")
    grads["conv_w"] = dcw[:dims.conv_width]
    sent = large("w_in", _mm_tn(h, dz, n_shards=N_CHIPS, name="dw_in"))
    dx, grads["norm1_g"] = _mm_nt_rmsnorm_bwd(dz, w_in, x, row("norm1_g"), dx1, want_bf16=False, after=sent,
                                              name="d_h_norm1_bwd")
    return loss, dx, grads


def _step(dims, x, target, w, m, v):
    d = dims.d_model
    t = dims.tokens
    sq = lambda a: a.reshape(a.shape[1:])
    w2, m2, v2 = ({k: sq(a) for k, a in grp.items()} for grp in (w, m, v))

    conv_pad = jnp.pad(w2["conv_w"], ((0, CONV_HALO - dims.conv_width), (0, 0)))
    ffn_pad = jnp.pad(w2["ffn_conv_w"], ((0, FFN_HALO - dims.ffn_conv_width), (0, 0)))
    first_names = ("w_in", "conv_w", "ffn_conv_w")
    other_names = tuple(k for k in LARGE if k not in first_names)
    lands = dict(zip(first_names, _cast_to_lands([w2["w_in"], conv_pad, ffn_pad], [BF16, F32, F32], name="cast_first")))
    first = _gather_start([lands[k] for k in first_names], x, halved=(0,), name="gather_start_first")
    lands.update(zip(other_names, _cast_to_lands([w2[k] for k in other_names], [BF16] * len(other_names),
                                                 after=first[3], name="cast_other")))
    other = []
    cols = lambda g, rows: jnp.moveaxis(g, 0, 1).reshape(g.shape[1], -1)[:rows]

    def first_weights(after):
        got = dict(zip(first_names, _gather_wait(*first[:3], [after] + [lands[k] for k in other_names], halved=(0,),
                                                 name="gather_wait_first")))
        got["w_in"] = _forward_to_sibling(got["w_in"], name="forward_w_in")
        other.extend(_gather_start([lands[k] for k in other_names], got["w_in"], name="gather_start_other"))
        got["conv_w"] = cols(got["conv_w"], dims.conv_width)
        got["ffn_conv_w"] = cols(got["ffn_conv_w"], dims.ffn_conv_width)
        got["token"] = other[3]
        return got

    def other_weights(after):
        return dict(zip(other_names, _gather_wait(*other[:3], after, name="gather_wait_other")))

    started, full, swapping_others = {}, {}, []
    others = [k for k in LARGE if k != "w_in"]

    def my_sums(names, after, tag):
        arrived = _scatter_wait([started[k] for k in names], after, name=f"scatter_wait_{tag}")
        return [_sum_received(full[k], land, name=f"sum_{k}") for k, (_, land) in zip(names, arrived)]

    def send_grad(name, g, g_bf16):
        blocks = lambda a: a.reshape(N_CHIPS, -1, a.shape[-1])
        send, recv, g_thru, land, token = _scatter_start(blocks(g_bf16), name=f"scatter_start_{name}")
        started[name], full[name] = (send, recv, g_thru, land), blocks(g)
        if name != "w_in":
            return token
        swapping_others.extend(_swap_start(my_sums(others, token, "others"), name="swap_start_others"))
        return swapping_others[4]

    small = {k: w2[k] for k in SMALL}
    small["norm1_g"] = _after(small["norm1_g"].reshape(1, -1), first[3])
    loss, dx, grads = _local_step(dims, x.reshape(t, d), target.reshape(t, d), small, first_weights, other_weights, send_grad)

    def updates(names, mine, theirs):
        return {k: _adamw(w2[k], [a, b], m2[k], v2[k], name=f"adamw_{k}") for k, a, b in zip(names, mine, theirs)}

    small_names = SMALL + ("conv_w", "ffn_conv_w")
    packed = _pack_rows([grads[k] for k in small_names] + [loss[0, 0]], d)
    reducing = _allreduce_start(packed, name="allreduce_start")
    out = updates(others, *_swap_wait(swapping_others, [dx, reducing[4]], name="swap_wait_others"))
    mine_w_in = my_sums(["w_in"], [out[k][1] for k in others], "w_in")
    swapping_w_in = _swap_start(mine_w_in, name="swap_start_w_in")
    reduced = _sum_devices(*_allreduce_wait(reducing, swapping_w_in[4], name="allreduce_wait"), name="allreduce_sum")
    vector_rows = sum(_packed_rows(math.prod(grads[k].shape), d) for k in SMALL)
    tail_shapes = [grads[k].shape for k in ("conv_w", "ffn_conv_w")] + [()]
    conv_g, ffn_g, loss_total = _unpack_rows(reduced[vector_rows:], tail_shapes, d)
    chip = 2 * lax.axis_index("x") + lax.axis_index("y")
    sharded_g = [lax.dynamic_slice_in_dim(g, chip * w2[k].shape[1], w2[k].shape[1], axis=1)
                 for k, g in (("conv_w", conv_g), ("ffn_conv_w", ffn_g))]
    packed_g = jnp.concatenate([reduced[:vector_rows], _pack_rows(sharded_g, d)], axis=0)

    smalls = lambda grp: [grp[k] for k in small_names]
    results = _adamw_unpacked(smalls(w2), packed_g, smalls(m2), smalls(v2), name="adamw_small")
    out.update({k: tuple(r[i] for r in results) for i, k in enumerate(small_names)})
    out.update(updates(["w_in"], *_swap_wait(swapping_w_in, results[1][0], name="swap_wait_w_in")))

    lead = lambda a: a.reshape((1,) + a.shape)
    ordered = [[lead(out[k][j].reshape(w2[k].shape)) for k in WEIGHTS] for j in range(4)]
    return (loss_total, dx.reshape(x.shape), *ordered[0], *ordered[1], *ordered[2], *ordered[3])


def kernel(x, norm1_g, w_in, gate_b, conv_w, conv_b, conv_norm_g, w_conv_out, q_norm_g, k_norm_g, w_attn_out, w_out, norm2_g, w_up, ffn_conv_w, ffn_conv_b, w_down, loss_target, m_norm1_g, m_w_in, m_gate_b, m_conv_w, m_conv_b, m_conv_norm_g, m_w_conv_out, m_q_norm_g, m_k_norm_g, m_w_attn_out, m_w_out, m_norm2_g, m_w_up, m_ffn_conv_w, m_ffn_conv_b, m_w_down, v_norm1_g, v_w_in, v_gate_b, v_conv_w, v_conv_b, v_conv_norm_g, v_w_conv_out, v_q_norm_g, v_k_norm_g, v_w_attn_out, v_w_out, v_norm2_g, v_w_up, v_ffn_conv_w, v_ffn_conv_b, v_w_down):
    w = dict(zip(WEIGHTS, (norm1_g, w_in, gate_b, conv_w, conv_b, conv_norm_g, w_conv_out, q_norm_g, k_norm_g,
                           w_attn_out, w_out, norm2_g, w_up, ffn_conv_w, ffn_conv_b, w_down)))
    m = dict(zip(WEIGHTS, (m_norm1_g, m_w_in, m_gate_b, m_conv_w, m_conv_b, m_conv_norm_g, m_w_conv_out, m_q_norm_g,
                           m_k_norm_g, m_w_attn_out, m_w_out, m_norm2_g, m_w_up, m_ffn_conv_w, m_ffn_conv_b, m_w_down)))
    v = dict(zip(WEIGHTS, (v_norm1_g, v_w_in, v_gate_b, v_conv_w, v_conv_b, v_conv_norm_g, v_w_conv_out, v_q_norm_g,
                           v_k_norm_g, v_w_attn_out, v_w_out, v_norm2_g, v_w_up, v_ffn_conv_w, v_ffn_conv_b, v_w_down)))
    dims = Dims(d_model=x.shape[-1], batch_local=x.shape[0], seq=x.shape[1], d_ff=w_down.shape[1] * N_CHIPS)
    return _step(dims, x, loss_target, w, m, v)
```

```python
import functools
import math
from typing import NamedTuple

import jax
import jax.numpy as jnp
from jax import lax
from jax.experimental import pallas as pl
from jax.experimental.pallas import tpu as pltpu

F32 = jnp.float32
BF16 = jnp.bfloat16

RMS_EPS = 1e-6
ATTN_BLOCK = 128
DILATIONS = (1, 4, 16)
CONV_HALO = 32
FFN_HALO = 8
ADAM_LR, ADAM_B1, ADAM_B2, ADAM_EPS, ADAM_WD, ADAM_STEP = 0.001, 0.9, 0.999, 1e-08, 0.01, 10
V7X_VMEM_LIMIT_BYTES = 56 * 2 ** 20
N_CHIPS = 4
MESH = pl.DeviceIdType.MESH


class Dims(NamedTuple):
    d_model: int = 1024
    n_heads: int = 16
    head_dim: int = 64
    d_ff: int = 2816
    seq: int = 2048
    batch_local: int = 2
    conv_width: int = 31
    ffn_conv_width: int = 3

    @property
    def tokens(self):
        return self.seq * self.batch_local


def _params(*semantics):
    return pltpu.CompilerParams(dimension_semantics=semantics, vmem_limit_bytes=V7X_VMEM_LIMIT_BYTES)


ANY = pl.BlockSpec(memory_space=pl.ANY)


def _ordered(body, n_inputs, after):
    after = [] if after is None else list(after) if isinstance(after, (list, tuple)) else [after]
    if not after:
        return body, [], []

    def wrapped(*refs):
        return body(*refs[:n_inputs], *refs[n_inputs + len(after):])

    return wrapped, [ANY] * len(after), after


def _pick(n, target, mult=128):
    if n <= target:
        return n
    best = None
    for t in range(mult, target + 1, mult):
        if n % t == 0:
            best = t
    assert best is not None, (n, target, mult)
    return best


def _sigmoid(v):
    return 1.0 / (1.0 + jnp.exp(-v))


def _mm_nn(a, w, *, out_dtype, name, residual=None, after=None, tm=1024, tn=1408, tk=2816):
    m, k = a.shape
    nsh, k2, c = w.shape
    assert k == k2 and a.dtype == BF16 and w.dtype == BF16
    n = nsh * c
    tm, tn, tk = _pick(m, tm, 8), _pick(c, tn), _pick(k, tk)
    nk, cpn = k // tk, c // tn

    def body(*refs):
        if residual is None:
            a_ref, w_ref, o_ref, acc = refs
        else:
            a_ref, w_ref, r_ref, o_ref, acc = refs
        prod = jnp.dot(a_ref[...], w_ref[...], preferred_element_type=F32)

        def finish(total):
            if residual is not None:
                total = total + r_ref[...]
            o_ref[...] = total.astype(out_dtype)

        if nk == 1:
            finish(prod)
        else:
            kk = pl.program_id(2)

            @pl.when(kk == 0)
            def _():
                acc[...] = prod

            @pl.when(kk > 0)
            def _():
                acc[...] += prod

            @pl.when(kk == nk - 1)
            def _():
                finish(acc[...])

    in_specs = [pl.BlockSpec((tm, tk), lambda i, j, kk: (i, kk)),
                pl.BlockSpec((None, tk, tn), lambda i, j, kk: (j // cpn, kk, j % cpn))]
    args = [a, w]
    if residual is not None:
        in_specs.append(pl.BlockSpec((tm, tn), lambda i, j, kk: (i, j)))
        args.append(residual)
    body, more_specs, more_args = _ordered(body, len(args), after)
    return pl.pallas_call(
        body, name=name, grid=(m // tm, n // tn, nk),
        in_specs=in_specs + more_specs, out_specs=pl.BlockSpec((tm, tn), lambda i, j, kk: (i, j)),
        out_shape=jax.ShapeDtypeStruct((m, n), out_dtype),
        scratch_shapes=[pltpu.VMEM((tm, tn) if nk > 1 else (8, 128), F32)],
        compiler_params=_params("parallel", "parallel", "arbitrary"),
    )(*args, *more_args)


def _proj_residual_loss(a, w, residual, target, *, name, tm=1024):
    m, k = a.shape
    _, k2, n = w.shape
    assert w.shape[0] == 1 and k == k2 and a.dtype == BF16 and w.dtype == BF16
    tm = _pick(m, tm, 8)

    def body(a_ref, w_ref, r_ref, t_ref, dy_ref, dyb_ref, loss_ref):
        err = r_ref[...] + jnp.dot(a_ref[...], w_ref[...], preferred_element_type=F32) - t_ref[...]
        dy = err * (1.0 / n)
        dy_ref[...] = dy
        dyb_ref[...] = dy.astype(BF16)
        part = jnp.sum(jnp.sum(err * err, axis=-1, keepdims=True), axis=0, keepdims=True) * (0.5 / n)
        _accumulate(loss_ref, jnp.broadcast_to(part, (8, 128)), pl.program_id(0) == 0)

    rows = lambda width: pl.BlockSpec((tm, width), lambda i: (i, 0))
    return pl.pallas_call(
        body, name=name, grid=(m // tm,),
        in_specs=[rows(k), pl.BlockSpec((None, k, n), lambda i: (0, 0, 0)), rows(n), rows(n)],
        out_specs=[rows(n), rows(n), pl.BlockSpec((8, 128), lambda i: (0, 0))],
        out_shape=[jax.ShapeDtypeStruct((m, n), F32), jax.ShapeDtypeStruct((m, n), BF16),
                   jax.ShapeDtypeStruct((8, 128), F32)],
        compiler_params=_params("arbitrary"),
    )(a, w, residual, target)


def _mm_nt(a, w, *, out_dtype, name, after=None, tm=1024, tn=1408, tk=1792):
    m, k = a.shape
    nsh, r, c = w.shape
    assert k == nsh * c and a.dtype == BF16 and w.dtype == BF16
    tm, tn, tk = _pick(m, tm, 8), _pick(r, tn), _pick(c, tk)
    nk, cpk = k // tk, c // tk

    def body(a_ref, w_ref, o_ref, acc):
        prod = lax.dot_general(a_ref[...], w_ref[...], (((1,), (1,)), ((), ())), preferred_element_type=F32)
        if nk == 1:
            o_ref[...] = prod.astype(out_dtype)
        else:
            kk = pl.program_id(2)

            @pl.when(kk == 0)
            def _():
                acc[...] = prod

            @pl.when(kk > 0)
            def _():
                acc[...] += prod

            @pl.when(kk == nk - 1)
            def _():
                o_ref[...] = acc[...].astype(out_dtype)

    body, more_specs, more_args = _ordered(body, 2, after)
    return pl.pallas_call(
        body, name=name, grid=(m // tm, r // tn, nk),
        in_specs=[pl.BlockSpec((tm, tk), lambda i, j, kk: (i, kk)),
                  pl.BlockSpec((None, tn, tk), lambda i, j, kk: (kk // cpk, j, kk % cpk))] + more_specs,
        out_specs=pl.BlockSpec((tm, tn), lambda i, j, kk: (i, j)),
        out_shape=jax.ShapeDtypeStruct((m, r), out_dtype),
        scratch_shapes=[pltpu.VMEM((tm, tn) if nk > 1 else (8, 128), F32)],
        compiler_params=_params("parallel", "parallel", "arbitrary"),
    )(a, w, *more_args)


NORM_BWD_ROWS = 256


def _mm_nt_rmsnorm_bwd(a, w, x, g, dres, *, name, want_bf16, silu=False, after=None, tm=1024, tk=1792):
    m, k = a.shape
    nsh, r, c = w.shape
    assert k == nsh * c and a.dtype == BF16 and w.dtype == BF16 and x.shape == (m, r)
    tm, tk = _pick(m, tm, 8), _pick(c, tk)
    nk, cpk = k // tk, c // tk
    rows = _pick(tm, NORM_BWD_ROWS, 8)
    n_in = 4 if dres is None else 5

    def body(a_ref, w_ref, x_ref, g_ref, *rest):
        dres_ref = None if dres is None else rest[0]
        outs, acc = rest[n_in - 4:-1], rest[-1]
        dx_ref, dg_ref = outs[0], outs[-1]
        kk = pl.program_id(1)
        prod = lax.dot_general(a_ref[...], w_ref[...], (((1,), (1,)), ((), ())), preferred_element_type=F32)

        @pl.when(kk == 0)
        def _():
            acc[...] = prod

        @pl.when(kk > 0)
        def _():
            acc[...] += prod

        @pl.when(kk == nk - 1)
        def _():
            dg = jnp.zeros((1, r), F32)
            for r0 in range(0, tm, rows):
                part = slice(r0, r0 + rows)
                xv, dyv = x_ref[part, :], acc[part, :]
                inv = lax.rsqrt(jnp.mean(xv * xv, axis=-1, keepdims=True) + RMS_EPS)
                if silu:
                    y = xv * inv * g_ref[...]
                    sg = _sigmoid(y)
                    dyv = dyv * sg * (1.0 + y * (1.0 - sg))
                gy = dyv * g_ref[...]
                dx = inv * gy - xv * (inv * inv * inv) * jnp.mean(xv * gy, axis=-1, keepdims=True)
                if dres is not None:
                    dx = dx + dres_ref[part, :]
                dx_ref[part, :] = dx
                if want_bf16:
                    outs[1][part, :] = dx.astype(BF16)
                dg = dg + jnp.sum(dyv * xv * inv, axis=0, keepdims=True)
            _accumulate(dg_ref, dg, pl.program_id(0) == 0)

    whole = lambda: pl.BlockSpec((tm, r), lambda i, kk: (i, 0))
    vec = pl.BlockSpec((1, r), lambda i, kk: (0, 0))
    out_shape, out_specs = [jax.ShapeDtypeStruct((m, r), F32)], [whole()]
    if want_bf16:
        out_shape.append(jax.ShapeDtypeStruct((m, r), BF16))
        out_specs.append(whole())
    out_shape.append(jax.ShapeDtypeStruct((1, r), F32))
    out_specs.append(vec)
    body, more_specs, more_args = _ordered(body, n_in, after)
    residual_specs, residual_args = ([], []) if dres is None else ([whole()], [dres])
    return pl.pallas_call(
        body, name=name, grid=(m // tm, nk),
        in_specs=[pl.BlockSpec((tm, tk), lambda i, kk: (i, kk)),
                  pl.BlockSpec((None, r, tk), lambda i, kk: (kk // cpk, 0, kk % cpk)), whole(), vec]
        + residual_specs + more_specs,
        out_specs=out_specs, out_shape=out_shape,
        scratch_shapes=[pltpu.VMEM((tm, r), F32)],
        compiler_params=_params("arbitrary", "arbitrary"),
    )(a, w, x, g, *residual_args, *more_args)


MM_TN_VMEM_BYTES = 44 * 2 ** 20


def _mm_tn(a, b, *, n_shards, name, tm=1408, tn=1408):
    t, m = a.shape
    t2, n = b.shape
    assert t == t2 and a.dtype == BF16 and b.dtype == BF16
    c = n // n_shards
    tm, tn = _pick(m, tm), _pick(c, tn)
    if m // tm == 1 and n // tn == 1 and tn % (2 * LANES) == 0:
        tn //= 2
    fixed = 2 * tm * tn * 6
    if 4 * t * (tm + tn) + fixed <= MM_TN_VMEM_BYTES:
        tk = t
    else:
        tk = _pick(t, (MM_TN_VMEM_BYTES - fixed - 4 * tm * tn) // (4 * (tm + tn)), 8)
    nk, cpn = t // tk, c // tn

    def body(a_ref, b_ref, o_ref, ob_ref, acc):
        kk = pl.program_id(2)
        prod = lax.dot_general(a_ref[...], b_ref[...], (((0,), (0,)), ((), ())), preferred_element_type=F32)

        def finish(total):
            o_ref[...] = total
            ob_ref[...] = total.astype(BF16)

        if nk == 1:
            finish(prod)
        else:
            @pl.when(kk == 0)
            def _():
                acc[...] = prod

            @pl.when(kk > 0)
            def _():
                acc[...] += prod

            @pl.when(kk == nk - 1)
            def _():
                finish(acc[...])

    out_spec = pl.BlockSpec((None, tm, tn), lambda i, j, kk: (j // cpn, i, j % cpn))
    return pl.pallas_call(
        body, name=name, grid=(m // tm, n // tn, nk),
        in_specs=[pl.BlockSpec((tk, tm), lambda i, j, kk: (kk, i)),
                  pl.BlockSpec((tk, tn), lambda i, j, kk: (kk, j))],
        out_specs=[out_spec, out_spec],
        out_shape=[jax.ShapeDtypeStruct((n_shards, m, c), F32), jax.ShapeDtypeStruct((n_shards, m, c), BF16)],
        scratch_shapes=[pltpu.VMEM((tm, tn) if nk > 1 else (8, 128), F32)],
        compiler_params=_params("parallel", "parallel", "arbitrary"),
    )(a, b)


def _row_spec(tr, width, col=0):
    return pl.BlockSpec((tr, width), lambda i, col=col: (i, col))


def _vec_spec(width, col=0):
    return pl.BlockSpec((1, width), lambda i, col=col: (0, col))


def _accumulate(ref, value, first):
    @pl.when(first)
    def _():
        ref[...] = value

    @pl.when(jnp.logical_not(first))
    def _():
        ref[...] += value


def _rmsnorm_fwd(x, g, *, name, tr=512):
    t, d = x.shape
    tr = _pick(t, tr, 8)

    def body(x_ref, g_ref, o_ref):
        xv = x_ref[...]
        r = lax.rsqrt(jnp.mean(xv * xv, axis=-1, keepdims=True) + RMS_EPS)
        o_ref[...] = (xv * r * g_ref[...]).astype(BF16)

    return pl.pallas_call(
        body, name=name, grid=(t // tr,),
        in_specs=[_row_spec(tr, d), _vec_spec(d)], out_specs=_row_spec(tr, d),
        out_shape=jax.ShapeDtypeStruct((t, d), BF16), compiler_params=_params("parallel"),
    )(x, g)


CONV_ROWS = 16


def _seq_specs(dims, ts, width, halo, col, *, nxt=False):
    nst, per = dims.seq // ts, ts // halo
    last = dims.tokens // halo - 1
    cur = pl.BlockSpec((ts, width), lambda b, i: (b * nst + i, col))
    if nxt:
        edge = pl.BlockSpec((halo, width), lambda b, i: (jnp.minimum((b * nst + i + 1) * per, last), col))
    else:
        edge = pl.BlockSpec((halo, width), lambda b, i: (jnp.maximum((b * nst + i) * per - 1, 0), col))
    return cur, edge


SUBLANES = 8


def _shifted_copies(buf, shifted):
    rows = shifted.shape[1]
    for s in range(1, SUBLANES):
        shifted[s - 1] = buf[pl.ds(s, rows), :]


def _window(buf, shifted, start, size):
    a, s = divmod(start, SUBLANES)
    src = buf if s == 0 else shifted.at[s - 1]
    return src[pl.ds(SUBLANES * a, size), :]


def _conv_branch_fwd(z, w, b, g, dims, *, name, ts=128):
    t, c, kw = z.shape[0], dims.d_model, dims.conv_width
    base = CONV_HALO - (kw - 1)

    def body(av_ref, hv_ref, ag_ref, hg_ref, w_ref, b_ref, g_ref, a1_ref, a3_ref, buf, shifted):
        i = pl.program_id(1)
        buf[CONV_HALO:, :] = av_ref[...].astype(F32) * _sigmoid(ag_ref[...].astype(F32))
        buf[0:CONV_HALO, :] = jnp.where(i > 0, hv_ref[...].astype(F32) * _sigmoid(hg_ref[...].astype(F32)), 0.0)
        _shifted_copies(buf, shifted)
        for r0 in range(0, ts, CONV_ROWS):
            acc = jnp.broadcast_to(b_ref[...], (CONV_ROWS, c))
            for k in range(kw):
                acc = acc + w_ref[k:k + 1, :] * _window(buf, shifted, r0 + base + k, CONV_ROWS)
            a1_ref[r0:r0 + CONV_ROWS, :] = acc
            a2 = acc * lax.rsqrt(jnp.mean(acc * acc, axis=-1, keepdims=True) + RMS_EPS) * g_ref[...]
            a3_ref[r0:r0 + CONV_ROWS, :] = (a2 * _sigmoid(a2)).astype(BF16)

    vec = pl.BlockSpec((1, c), lambda b, i: (0, 0))
    out = pl.BlockSpec((ts, c), lambda b, i: (b * (dims.seq // ts) + i, 0))
    return pl.pallas_call(
        body, name=name, grid=(dims.batch_local, dims.seq // ts),
        in_specs=[*_seq_specs(dims, ts, c, CONV_HALO, 0), *_seq_specs(dims, ts, c, CONV_HALO, 1),
                  pl.BlockSpec((CONV_HALO, c), lambda b, i: (0, 0)), vec, vec],
        out_specs=[out, out],
        out_shape=[jax.ShapeDtypeStruct((t, c), F32), jax.ShapeDtypeStruct((t, c), BF16)],
        scratch_shapes=[pltpu.VMEM((CONV_HALO + ts, c), F32),
                        pltpu.VMEM((SUBLANES - 1, CONV_HALO + ts - SUBLANES, c), F32)],
        compiler_params=_params("parallel", "parallel"),
    )(z, z, z, z, w, b, g)


def _conv_branch_bwd(da1, z, w, rest_of_dz, dims, *, name, ts=128):
    t, c, kw = z.shape[0], dims.d_model, dims.conv_width
    nst = dims.seq // ts
    base = CONV_HALO - (kw - 1)
    n_rest = len(rest_of_dz)
    total = 2 * c + sum(r.shape[1] for r in rest_of_dz)

    def body(d_ref, dn_ref, av_ref, hv_ref, ag_ref, hg_ref, w_ref, *more):
        rest_refs = more[:n_rest]
        dz_ref, dw_ref, db_ref, abuf, dbuf, ashift, dshift = more[n_rest:]
        col = 2 * c
        for r in rest_refs:
            dz_ref[:, col:col + r.shape[1]] = r[...]
            col += r.shape[1]
        i = pl.program_id(1)
        first = jnp.logical_and(pl.program_id(0) == 0, i == 0)
        abuf[CONV_HALO:, :] = av_ref[...].astype(F32) * _sigmoid(ag_ref[...].astype(F32))
        abuf[0:CONV_HALO, :] = jnp.where(i > 0, hv_ref[...].astype(F32) * _sigmoid(hg_ref[...].astype(F32)), 0.0)
        d1 = d_ref[...]
        dbuf[0:ts, :] = d1
        dbuf[ts:, :] = jnp.where(i < nst - 1, dn_ref[...], 0.0)
        _shifted_copies(abuf, ashift)
        _shifted_copies(dbuf, dshift)

        @pl.when(first)
        def _():
            dw_ref[...] = jnp.zeros_like(dw_ref)
            db_ref[...] = jnp.zeros_like(db_ref)

        db_ref[...] += jnp.sum(d1, axis=0, keepdims=True)
        for k in range(kw):
            dw_ref[k:k + 1, :] += jnp.sum(d1 * _window(abuf, ashift, base + k, ts), axis=0, keepdims=True)
        for r0 in range(0, ts, CONV_ROWS):
            acc = jnp.zeros((CONV_ROWS, c), F32)
            for k in range(kw):
                acc = acc + w_ref[k:k + 1, :] * _window(dbuf, dshift, r0 + (kw - 1) - k, CONV_ROWS)
            av = av_ref[r0:r0 + CONV_ROWS, :].astype(F32)
            sg = _sigmoid(ag_ref[r0:r0 + CONV_ROWS, :].astype(F32))
            dz_ref[r0:r0 + CONV_ROWS, 0:c] = (acc * sg).astype(BF16)
            dz_ref[r0:r0 + CONV_ROWS, c:2 * c] = (acc * av * sg * (1.0 - sg)).astype(BF16)

    cur, nxt = _seq_specs(dims, ts, c, CONV_HALO, 0, nxt=True)
    return pl.pallas_call(
        body, name=name, grid=(dims.batch_local, nst),
        in_specs=[cur, nxt, *_seq_specs(dims, ts, c, CONV_HALO, 0), *_seq_specs(dims, ts, c, CONV_HALO, 1),
                  pl.BlockSpec((CONV_HALO, c), lambda b, i: (0, 0))]
        + [pl.BlockSpec((ts, r.shape[1]), lambda b, i: (b * nst + i, 0)) for r in rest_of_dz],
        out_specs=[pl.BlockSpec((ts, total), lambda b, i: (b * nst + i, 0)),
                   pl.BlockSpec((CONV_HALO, c), lambda b, i: (0, 0)), pl.BlockSpec((1, c), lambda b, i: (0, 0))],
        out_shape=[jax.ShapeDtypeStruct((t, total), BF16), jax.ShapeDtypeStruct((CONV_HALO, c), F32),
                   jax.ShapeDtypeStruct((1, c), F32)],
        scratch_shapes=[pltpu.VMEM((CONV_HALO + ts, c), F32)] * 2
        + [pltpu.VMEM((SUBLANES - 1, CONV_HALO + ts - SUBLANES, c), F32)] * 2,
        compiler_params=_params("arbitrary", "arbitrary"),
    )(da1, da1, z, z, z, z, w, *rest_of_dz)


FFN_ROWS = 16
FFN_COLS = 256


def _ffn_chunks(ts, f):
    cw = _pick(f, FFN_COLS)
    return [(r0, c0, cw) for r0 in range(0, ts, FFN_ROWS) for c0 in range(0, f, cw)]


def _tap_sources(buf, moved, offsets, rows):
    taps, used = [], 0
    for off in offsets:
        if off % SUBLANES:
            moved[used] = buf[pl.ds(off, rows), :]
            taps.append((moved.at[used], 0))
            used += 1
        else:
            taps.append((buf, off))
    return taps


def _moved_copies(offsets):
    return sum(1 for off in offsets if off % SUBLANES)


def _taps_sum(taps, w_ref, init, r0, cols):
    for k, (src, off) in enumerate(taps):
        init = init + w_ref[k:k + 1, cols] * src[pl.ds(off + r0, init.shape[0]), cols]
    return init


def _ffn_bwd(dact, up, w, b, dims, *, name, ts=128):
    t, f, kw = up.shape[0], dims.d_ff, dims.ffn_conv_width
    nst = dims.seq // ts
    fwd_offsets = [FFN_HALO - (kw - 1) + k for k in range(kw)]
    bwd_offsets = [(kw - 1) - k for k in range(kw)]
    dact_halo = 2 * FFN_HALO

    def body(d_ref, dn_ref, up_ref, hp_ref, hn_ref, w_ref, b_ref, o_ref, dw_ref, db_ref, buf, moved, dbuf, dmoved):
        i = pl.program_id(1)
        first = jnp.logical_and(pl.program_id(0) == 0, i == 0)
        more = i < nst - 1
        buf[0:FFN_HALO, :] = jnp.where(i > 0, hp_ref[...], 0.0)
        buf[FFN_HALO:FFN_HALO + ts, :] = up_ref[...]
        buf[FFN_HALO + ts:, :] = hn_ref[...]
        taps = _tap_sources(buf, moved, fwd_offsets, ts + FFN_HALO)

        def du_chunk(r0, rows, c0, cw, d):
            vcols, gcols = slice(c0, c0 + cw), slice(f + c0, f + c0 + cw)
            uv = _taps_sum(taps, w_ref, jnp.broadcast_to(b_ref[:, vcols], (rows, cw)), r0, vcols)
            ug = _taps_sum(taps, w_ref, jnp.broadcast_to(b_ref[:, gcols], (rows, cw)), r0, gcols)
            sg = _sigmoid(ug)
            dbuf[r0:r0 + rows, vcols] = d * ug * sg
            dbuf[r0:r0 + rows, gcols] = d * uv * sg * (1.0 + ug * (1.0 - sg))

        for r0, c0, cw in _ffn_chunks(ts, f):
            du_chunk(r0, FFN_ROWS, c0, cw, d_ref[r0:r0 + FFN_ROWS, c0:c0 + cw].astype(F32))
        for _, c0, cw in _ffn_chunks(FFN_ROWS, f):
            d_next = dn_ref[:, c0:c0 + cw].astype(F32)[0:FFN_HALO]
            du_chunk(ts, FFN_HALO, c0, cw, jnp.where(more, d_next, 0.0))

        @pl.when(first)
        def _():
            dw_ref[...] = jnp.zeros_like(dw_ref)
            db_ref[...] = jnp.zeros_like(db_ref)

        du = dbuf[0:ts, :]
        db_ref[...] += jnp.sum(du, axis=0, keepdims=True)
        for k, (src, off) in enumerate(taps):
            dw_ref[k:k + 1, :] += jnp.sum(du * src[pl.ds(off, ts), :], axis=0, keepdims=True)

        dtaps = _tap_sources(dbuf, dmoved, bwd_offsets, ts)
        for r0, c0, cw in _ffn_chunks(ts, 2 * f):
            cols = slice(c0, c0 + cw)
            o_ref[r0:r0 + FFN_ROWS, cols] = _taps_sum(dtaps, w_ref, jnp.zeros((FFN_ROWS, cw), F32), r0, cols).astype(BF16)

    up_cur, up_prev = _seq_specs(dims, ts, 2 * f, FFN_HALO, 0)
    _, up_next = _seq_specs(dims, ts, 2 * f, FFN_HALO, 0, nxt=True)
    d_cur, d_next = _seq_specs(dims, ts, f, dact_halo, 0, nxt=True)
    full = lambda rows: pl.BlockSpec((rows, 2 * f), lambda b_, i: (0, 0))
    return pl.pallas_call(
        body, name=name, grid=(dims.batch_local, nst),
        in_specs=[d_cur, d_next, up_cur, up_prev, up_next, full(FFN_HALO), full(1)],
        out_specs=[pl.BlockSpec((ts, 2 * f), lambda b_, i: (b_ * nst + i, 0)), full(FFN_HALO), full(1)],
        out_shape=[jax.ShapeDtypeStruct((t, 2 * f), BF16), jax.ShapeDtypeStruct((FFN_HALO, 2 * f), F32),
                   jax.ShapeDtypeStruct((1, 2 * f), F32)],
        scratch_shapes=[pltpu.VMEM((ts + 2 * FFN_HALO, 2 * f), F32),
                        pltpu.VMEM((_moved_copies(fwd_offsets), ts + FFN_HALO, 2 * f), F32),
                        pltpu.VMEM((ts + FFN_HALO, 2 * f), F32),
                        pltpu.VMEM((_moved_copies(bwd_offsets), ts, 2 * f), F32)],
        compiler_params=_params("arbitrary", "arbitrary"),
    )(dact, dact, up, up, up, w, b)


def _ffn_act_fwd(up, w, b, dims, *, name, ts=128):
    t, f, kw = up.shape[0], dims.d_ff, dims.ffn_conv_width
    offsets = [FFN_HALO - (kw - 1) + k for k in range(kw)]

    def body(up_ref, h_ref, w_ref, b_ref, o_ref, buf, moved):
        buf[FFN_HALO:, :] = up_ref[...]
        buf[0:FFN_HALO, :] = jnp.where(pl.program_id(1) > 0, h_ref[...], 0.0)
        taps = _tap_sources(buf, moved, offsets, ts)
        for r0, c0, cw in _ffn_chunks(ts, f):
            vcols, gcols = slice(c0, c0 + cw), slice(f + c0, f + c0 + cw)
            uv = _taps_sum(taps, w_ref, jnp.broadcast_to(b_ref[:, vcols], (FFN_ROWS, cw)), r0, vcols)
            ug = _taps_sum(taps, w_ref, jnp.broadcast_to(b_ref[:, gcols], (FFN_ROWS, cw)), r0, gcols)
            o_ref[r0:r0 + FFN_ROWS, vcols] = (ug * _sigmoid(ug) * uv).astype(BF16)

    full = lambda rows: pl.BlockSpec((rows, 2 * f), lambda b_, i: (0, 0))
    return pl.pallas_call(
        body, name=name, grid=(dims.batch_local, dims.seq // ts),
        in_specs=[*_seq_specs(dims, ts, 2 * f, FFN_HALO, 0), full(FFN_HALO), full(1)],
        out_specs=pl.BlockSpec((ts, f), lambda b_, i: (b_ * (dims.seq // ts) + i, 0)),
        out_shape=jax.ShapeDtypeStruct((t, f), BF16),
        scratch_shapes=[pltpu.VMEM((FFN_HALO + ts, 2 * f), F32), pltpu.VMEM((_moved_copies(offsets), ts, 2 * f), F32)],
        compiler_params=_params("parallel", "parallel"),
    )(up, up, w, b)


def _dot_nt(a, b):
    return lax.dot_general(a, b, (((1,), (1,)), ((), ())), preferred_element_type=F32)


def _dot_tn(a, b):
    return lax.dot_general(a, b, (((0,), (0,)), ((), ())), preferred_element_type=F32)


LANES = 128
MASK_BIAS = 1e30
RESIDUE_DILATIONS = tuple(d for d in DILATIONS if d > 1)


def _rows_to_residues(value, out_ref, scr, d):
    rows, width = value.shape
    for c in range(width // LANES):
        cols = slice(LANES * c, LANES * (c + 1))
        scr[c] = value[:, cols]
        for r in range(d):
            out_ref[r, :, cols] = scr[c, pl.ds(r, rows // d, stride=d), :].astype(out_ref.dtype)


def _residues_to_rows(in_ref, scr, d):
    _, n, width = in_ref.shape
    slabs = []
    for c in range(width // LANES):
        cols = slice(LANES * c, LANES * (c + 1))
        for r in range(d):
            scr[c, pl.ds(r, n, stride=d), :] = in_ref[r, :, cols].astype(F32)
        slabs.append(scr[c])
    return slabs[0] if len(slabs) == 1 else jnp.concatenate(slabs, axis=1)


def _residue_shape(dims, d, width, dtype):
    return jax.ShapeDtypeStruct((dims.batch_local, d, dims.seq // d, width), dtype)


def _residue_spec(dims, d, tr, width):
    tiles = dims.seq // tr
    return pl.BlockSpec((None, d, tr // d, width), lambda i: (i // tiles, 0, i % tiles, 0))


def _head_sum_matrix(dims):
    a = dims.n_heads * dims.head_dim
    head = jnp.arange(a, dtype=jnp.int32) // dims.head_dim
    return (head[:, None] == jnp.arange(LANES, dtype=jnp.int32)[None, :]).astype(BF16)


def _two_pass_dot(v, m):
    hi = v.astype(BF16)
    lo = (v - hi.astype(F32)).astype(BF16)
    return jnp.dot(hi, m, preferred_element_type=F32) + jnp.dot(lo, m, preferred_element_type=F32)


def _residue_permutations(tr):
    out = []
    for d in RESIDUE_DILATIONS:
        dst = jnp.arange(tr, dtype=jnp.int32)
        src = d * (dst % (tr // d)) + dst // (tr // d)
        out.append((src[:, None] == jnp.arange(tr, dtype=jnp.int32)[None, :]).astype(BF16))
    return out


def _bf16_rows_to_residues(value, out_ref, perm_ref, d):
    n = value.shape[0] // d
    moved = jnp.dot(perm_ref[...], value, preferred_element_type=F32).astype(out_ref.dtype)
    for r in range(d):
        out_ref[r] = moved[r * n:(r + 1) * n]


def _bf16_residues_to_rows(in_ref, back_ref):
    d = in_ref.shape[0]
    stacked = jnp.concatenate([in_ref[r] for r in range(d)], axis=0)
    return jnp.dot(back_ref[...], stacked, preferred_element_type=F32)


def _qkv_layouts_fwd(z, gq, gk, head_ones, dims, *, name, tr=256):
    t = z.shape[0]
    a = dims.n_heads * dims.head_dim
    q_scale = dims.head_dim ** -0.5
    nres = len(RESIDUE_DILATIONS)

    def body(q_ref, k_ref, v_ref, gq_ref, gk_ref, sum_ref, spread_ref, *rest):
        perm_refs, outs = rest[:nres], rest[nres:]
        qv, kv = q_ref[...].astype(F32), k_ref[...].astype(F32)
        mean = lambda val: _two_pass_dot(_two_pass_dot(val, sum_ref[...]), spread_ref[...]) * (1.0 / dims.head_dim)
        rq = lax.rsqrt(mean(qv * qv) + RMS_EPS)
        rk = lax.rsqrt(mean(kv * kv) + RMS_EPS)
        values = ((qv * rq * gq_ref[...] * q_scale).astype(BF16), (kv * rk * gk_ref[...]).astype(BF16), v_ref[...])
        for j, val in enumerate(values):
            outs[j][...] = val
            for g, d in enumerate(RESIDUE_DILATIONS):
                _bf16_rows_to_residues(val, outs[3 * (g + 1) + j], perm_refs[g], d)

    out_specs = [_row_spec(tr, a)] * 3
    out_shape = [jax.ShapeDtypeStruct((t, a), BF16)] * 3
    for d in RESIDUE_DILATIONS:
        out_specs += [_residue_spec(dims, d, tr, a)] * 3
        out_shape += [_residue_shape(dims, d, a, BF16)] * 3
    outs = pl.pallas_call(
        body, name=name, grid=(t // tr,),
        in_specs=[_row_spec(tr, a, 2), _row_spec(tr, a, 3), _row_spec(tr, a, 4), _vec_spec(a), _vec_spec(a),
                  pl.BlockSpec((a, LANES), lambda i: (0, 0)), pl.BlockSpec((LANES, a), lambda i: (0, 0))]
        + [pl.BlockSpec((tr, tr), lambda i: (0, 0))] * nres,
        out_specs=out_specs, out_shape=out_shape,
        compiler_params=_params("parallel"),
    )(z, z, z, gq, gk, *head_ones, *_residue_permutations(tr))
    return {d: tuple(outs[3 * g:3 * g + 3]) for g, d in enumerate((1,) + RESIDUE_DILATIONS)}


ATTN_RESIDUES_PER_STEP = 4
ATTN_RESIDUES_PER_STEP_WINDOWED = 2


def _attn_groups(dims, dil):
    return (dims.batch_local, dil) if dil > 1 else (1, dims.batch_local)


def _attn_array(x, dims, dil):
    return x if dil > 1 else x.reshape(1, dims.batch_local, dims.seq, x.shape[-1])


def _attn_residues(dims, dil):
    one_block = dims.seq // dil == ATTN_BLOCK
    return math.gcd(_attn_groups(dims, dil)[1], ATTN_RESIDUES_PER_STEP if one_block else ATTN_RESIDUES_PER_STEP_WINDOWED)


def _per_residue(body, rs):
    if rs == 1:
        return body

    def stepped(*refs):
        for r in range(rs):
            body(*[ref.at[r] for ref in refs])

    return stepped


def _attn_specs(dims, dil, width):
    blk = ATTN_BLOCK
    nb = dims.seq // dil // blk
    rs = _attn_residues(dims, dil)
    lead, groups = _attn_groups(dims, dil)
    if rs > 1 and nb == 1:
        grid = (lead, groups // rs)
        at = lambda f: pl.BlockSpec((None, rs, blk, width), lambda b, r: (b, r, 0, 0))
    elif rs > 1:
        grid = (lead, groups // rs, nb)
        at = lambda f: pl.BlockSpec((None, rs, blk, width), lambda b, r, i: (b, r, f(i), 0))
    else:
        grid = (lead, groups, nb)
        at = lambda f: pl.BlockSpec((None, None, blk, width), lambda b, r, i: (b, r, f(i), 0))
    return grid, at(lambda i: i), at(lambda i: jnp.maximum(i - 1, 0)), at(lambda i: jnp.minimum(i + 1, nb - 1))


def _head_slopes(n_heads):
    h = lax.broadcasted_iota(jnp.int32, (n_heads, 1, 1), 0).astype(F32)
    return jnp.exp((h + 1.0) * (-8.0 / n_heads * math.log(2.0)))


def _pair_masks(hd):
    low = lax.broadcasted_iota(jnp.int32, (1, 2 * hd), 1) < hd
    return low, jnp.logical_not(low)


def _attn_fwd(q, k, v, dims, dil, *, name):
    a = dims.n_heads * dims.head_dim
    heads, hd, blk = dims.n_heads, dims.head_dim, ATTN_BLOCK
    assert 2 * hd == LANES and heads % 2 == 0 and heads <= LANES
    nb = dims.seq // dil // blk
    has_prev = nb > 1
    nkeys = 2 * blk if has_prev else blk
    grid, cur, prev, _ = _attn_specs(dims, dil, a)
    _, cur_stat, _, _ = _attn_specs(dims, dil, LANES)

    def body(*refs):
        if has_prev:
            q_ref, kc_ref, vc_ref, kp_ref, vp_ref, o_ref, lse_ref, s_scr, p_scr, k_st, v_st = refs
            k_st[0:blk, :], k_st[blk:, :] = kp_ref[...], kc_ref[...]
            v_st[0:blk, :], v_st[blk:, :] = vp_ref[...], vc_ref[...]
        else:
            q_ref, k_st, v_st, o_ref, lse_ref, s_scr, p_scr = refs
        low, high = _pair_masks(hd)

        for hp in range(heads // 2):
            sl = slice(LANES * hp, LANES * (hp + 1))
            q2 = q_ref[:, sl]
            kcat = k_st[:, sl]
            s_scr[2 * hp] = _dot_nt(jnp.where(low, q2, jnp.zeros_like(q2)), kcat)
            s_scr[2 * hp + 1] = _dot_nt(jnp.where(high, q2, jnp.zeros_like(q2)), kcat)

        iq = lax.broadcasted_iota(jnp.int32, (blk, nkeys), 0)
        jk = lax.broadcasted_iota(jnp.int32, (blk, nkeys), 1)
        if has_prev:
            steps = iq + blk - jk
            valid = (steps >= 0) & (steps <= blk) & ((jk >= blk) | (pl.program_id(len(grid) - 1) > 0))
        else:
            steps = iq - jk
            valid = steps >= 0
        bias = jnp.where(valid, steps.astype(F32) * (-float(dil)), -MASK_BIAS)
        s = s_scr[...] + _head_slopes(heads) * bias[None]
        m = jnp.max(s, axis=-1, keepdims=True)
        p = jnp.exp(s - m)
        l = jnp.sum(p, axis=-1, keepdims=True)
        p_scr[...] = p.astype(BF16)
        inv = 1.0 / l
        lse = m + jnp.log(l)

        lane = lax.broadcasted_iota(jnp.int32, (blk, LANES), 1)
        stat = jnp.zeros((blk, LANES), F32)
        for hp in range(heads // 2):
            sl = slice(LANES * hp, LANES * (hp + 1))
            vcat = v_st[:, sl]
            pv_a = jnp.dot(p_scr[2 * hp], vcat, preferred_element_type=F32) * inv[2 * hp]
            pv_b = jnp.dot(p_scr[2 * hp + 1], vcat, preferred_element_type=F32) * inv[2 * hp + 1]
            o_ref[:, sl] = jnp.where(low, pv_a, pv_b).astype(BF16)
            stat = jnp.where(lane == 2 * hp, lse[2 * hp], stat)
            stat = jnp.where(lane == 2 * hp + 1, lse[2 * hp + 1], stat)
        lse_ref[...] = stat

    q4, k4, v4 = (_attn_array(x, dims, dil) for x in (q, k, v))
    rs = _attn_residues(dims, dil)
    per_step = lambda shape: shape if rs == 1 else (rs,) + shape
    o, lse = pl.pallas_call(
        _per_residue(body, rs), name=name, grid=grid,
        in_specs=[cur, cur, cur] + ([prev, prev] if has_prev else []),
        out_specs=[cur, cur_stat],
        out_shape=[jax.ShapeDtypeStruct(q4.shape, BF16), jax.ShapeDtypeStruct(q4.shape[:-1] + (LANES,), F32)],
        scratch_shapes=[pltpu.VMEM(per_step((heads, blk, nkeys)), F32), pltpu.VMEM(per_step((heads, blk, nkeys)), BF16)]
        + ([pltpu.VMEM(per_step((nkeys, a)), BF16)] * 2 if has_prev else []),
        compiler_params=_params(*["parallel"] * len(grid)),
    )(q4, k4, v4, *([k4, v4] if has_prev else []))
    return o.reshape(q.shape), lse.reshape(q.shape[:-1] + (LANES,))


def _attn_combine(groups, head_spread, dims, *, name, tr=256):
    t = dims.tokens
    a = dims.n_heads * dims.head_dim
    dils = tuple(groups)

    nres = len(RESIDUE_DILATIONS)

    def body(*refs):
        ins = refs[:2 * len(dils)]
        x_ref = refs[2 * len(dils)]
        back_refs = dict(zip(RESIDUE_DILATIONS, refs[2 * len(dils) + 1:2 * len(dils) + 1 + nres]))
        o_ref = refs[2 * len(dils) + 1 + nres]
        lse_refs = refs[2 * len(dils) + 2 + nres:-1]
        scr_stat = refs[-1]
        outs, stats = [], []
        for g, d in enumerate(dils):
            if d == 1:
                outs.append(ins[2 * g][...].astype(F32))
                stats.append(ins[2 * g + 1][...])
            else:
                outs.append(_bf16_residues_to_rows(ins[2 * g], back_refs[d]))
                stats.append(_residues_to_rows(ins[2 * g + 1], scr_stat, d))
        top = functools.reduce(jnp.maximum, stats)
        weights = [jnp.exp(s - top) for s in stats]
        total = functools.reduce(jnp.add, weights)
        joint = top + jnp.log(total)
        inv = 1.0 / total
        acc = None
        for w, o in zip(weights, outs):
            term = _two_pass_dot(w * inv, x_ref[...]) * o
            acc = term if acc is None else acc + term
        o_ref[...] = acc.astype(BF16)
        for g, d in enumerate(dils):
            if d == 1:
                lse_refs[g][...] = joint
            else:
                _rows_to_residues(joint, lse_refs[g], scr_stat, d)

    in_specs, args, lse_specs, lse_shapes = [], [], [], []
    for d in dils:
        if d == 1:
            in_specs += [_row_spec(tr, a), _row_spec(tr, LANES)]
            lse_specs.append(_row_spec(tr, LANES))
            lse_shapes.append(jax.ShapeDtypeStruct((t, LANES), F32))
        else:
            in_specs += [_residue_spec(dims, d, tr, a), _residue_spec(dims, d, tr, LANES)]
            lse_specs.append(_residue_spec(dims, d, tr, LANES))
            lse_shapes.append(_residue_shape(dims, d, LANES, F32))
        args += list(groups[d])
    outs = pl.pallas_call(
        body, name=name, grid=(t // tr,),
        in_specs=in_specs + [pl.BlockSpec((LANES, a), lambda i: (0, 0))] + [pl.BlockSpec((tr, tr), lambda i: (0, 0))] * nres,
        out_specs=[_row_spec(tr, a)] + lse_specs,
        out_shape=[jax.ShapeDtypeStruct((t, a), BF16)] + lse_shapes,
        scratch_shapes=[pltpu.VMEM((1, tr, LANES), F32)],
        compiler_params=_params("parallel"),
    )(*args, head_spread, *[jnp.transpose(p) for p in _residue_permutations(tr)])
    return outs[0], dict(zip(dils, outs[1:]))


def _attn_bwd_prep(dy, w, o, head_sum, dims, *, name, after=None, tr=256):
    t, a = o.shape
    nres = len(RESIDUE_DILATIONS)

    def body(dy_ref, w_ref, o_ref, e_ref, *rest):
        perm_refs, outs, scr_stat = rest[:nres], rest[nres:-1], rest[-1]
        do = _dot_nt(dy_ref[...], w_ref[...]).astype(BF16)
        outs[0][...] = do
        delta = _two_pass_dot(do.astype(F32) * o_ref[...].astype(F32), e_ref[...])
        outs[1][...] = delta
        for g, d in enumerate(RESIDUE_DILATIONS):
            _bf16_rows_to_residues(do, outs[2 + 2 * g], perm_refs[g], d)
            _rows_to_residues(delta, outs[3 + 2 * g], scr_stat, d)

    out_specs = [_row_spec(tr, a), _row_spec(tr, LANES)]
    out_shape = [jax.ShapeDtypeStruct((t, a), BF16), jax.ShapeDtypeStruct((t, LANES), F32)]
    for d in RESIDUE_DILATIONS:
        out_specs += [_residue_spec(dims, d, tr, a), _residue_spec(dims, d, tr, LANES)]
        out_shape += [_residue_shape(dims, d, a, BF16), _residue_shape(dims, d, LANES, F32)]
    n_in = 4 + nres
    body, more_specs, more_args = _ordered(body, n_in, after)
    outs = pl.pallas_call(
        body, name=name, grid=(t // tr,),
        in_specs=[_row_spec(tr, dy.shape[1]), pl.BlockSpec((None,) + w.shape[1:], lambda i: (0, 0, 0)), _row_spec(tr, a),
                  pl.BlockSpec((a, LANES), lambda i: (0, 0))] + [pl.BlockSpec((tr, tr), lambda i: (0, 0))] * nres + more_specs,
        out_specs=out_specs, out_shape=out_shape,
        scratch_shapes=[pltpu.VMEM((1, tr, LANES), F32)],
        compiler_params=_params("parallel"),
    )(dy, w, o, head_sum, *_residue_permutations(tr), *more_args)
    dos, deltas = {1: outs[0]}, {1: outs[1]}
    for g, d in enumerate(RESIDUE_DILATIONS):
        dos[d], deltas[d] = outs[2 + 2 * g], outs[3 + 2 * g]
    return dos, deltas


def _attn_bwd(q, k, v, do, lse, delta, dims, dil, *, name):
    a = dims.n_heads * dims.head_dim
    heads, hd, blk = dims.n_heads, dims.head_dim, ATTN_BLOCK
    nb = dims.seq // dil // blk
    has_next = nb > 1
    nq = 2 * blk if has_next else blk
    grid, cur, _, nxt = _attn_specs(dims, dil, a)
    _, cur_stat, _, nxt_stat = _attn_specs(dims, dil, LANES)

    def body(*refs):
        k_ref, v_ref, q_ref, do_ref, lse_ref, dl_ref = refs[:6]
        if has_next:
            qn_ref, don_ref, lsen_ref, dln_ref = refs[6:10]
            dq_ref, dk_ref, dv_ref, q_st, do_st, s_scr, dp_scr, p_scr, ds_scr, carry = refs[10:]
        else:
            dq_ref, dk_ref, dv_ref, q_st, do_st, s_scr, dp_scr, p_scr, ds_scr = refs[6:]
        j = pl.program_id(len(grid) - 1)
        low, high = _pair_masks(hd)
        q_st[0:blk, :] = q_ref[...]
        do_st[0:blk, :] = do_ref[...]
        if has_next:
            q_st[blk:, :] = qn_ref[...]
            do_st[blk:, :] = don_ref[...]
            lse_all = jnp.concatenate([lse_ref[...], lsen_ref[...]], axis=0)
            dl_all = jnp.concatenate([dl_ref[...], dln_ref[...]], axis=0)
        else:
            lse_all, dl_all = lse_ref[...], dl_ref[...]
        lse_t, dl_t = jnp.transpose(lse_all), jnp.transpose(dl_all)
        lse3 = jnp.stack([lse_t[h:h + 1, :] for h in range(heads)])
        dl3 = jnp.stack([dl_t[h:h + 1, :] for h in range(heads)])

        def halves(x):
            return jnp.where(low, x, jnp.zeros_like(x)), jnp.where(high, x, jnp.zeros_like(x))

        for hp in range(heads // 2):
            sl = slice(LANES * hp, LANES * (hp + 1))
            k2, v2 = k_ref[:, sl], v_ref[:, sl]
            q_a, q_b = halves(q_st[:, sl])
            do_a, do_b = halves(do_st[:, sl])
            s_scr[2 * hp], s_scr[2 * hp + 1] = _dot_nt(k2, q_a), _dot_nt(k2, q_b)
            dp_scr[2 * hp], dp_scr[2 * hp + 1] = _dot_nt(v2, do_a), _dot_nt(v2, do_b)

        jk = lax.broadcasted_iota(jnp.int32, (blk, nq), 0)
        rq = lax.broadcasted_iota(jnp.int32, (blk, nq), 1)
        if has_next:
            iq = jnp.where(rq < blk, rq, rq - blk)
            steps = jnp.where(rq < blk, iq - jk, iq - jk + blk)
            valid = ((rq < blk) & (iq >= jk)) | ((rq >= blk) & (jk >= iq) & (j + 1 < nb))
        else:
            steps, valid = rq - jk, rq >= jk
        bias = jnp.where(valid, steps.astype(F32) * (-float(dil)), -MASK_BIAS)
        p = jnp.exp(s_scr[...] + _head_slopes(heads) * bias[None] - lse3)
        p_scr[...] = p.astype(BF16)
        ds_scr[...] = (p * (dp_scr[...] - dl3)).astype(BF16)

        if has_next:
            @pl.when(j == 0)
            def _():
                carry[...] = jnp.zeros_like(carry)

        for hp in range(heads // 2):
            sl = slice(LANES * hp, LANES * (hp + 1))
            k2 = k_ref[:, sl]
            q_a, q_b = halves(q_st[:, sl])
            do_a, do_b = halves(do_st[:, sl])
            ds_a, ds_b = ds_scr[2 * hp], ds_scr[2 * hp + 1]
            dk_ref[:, sl] = (jnp.dot(ds_a, q_a, preferred_element_type=F32)
                             + jnp.dot(ds_b, q_b, preferred_element_type=F32)).astype(BF16)
            dv_ref[:, sl] = (jnp.dot(p_scr[2 * hp], do_a, preferred_element_type=F32)
                             + jnp.dot(p_scr[2 * hp + 1], do_b, preferred_element_type=F32)).astype(BF16)
            dq2 = jnp.where(low, _dot_tn(ds_a, k2), _dot_tn(ds_b, k2))
            if has_next:
                dq_ref[:, sl] = (carry[:, sl] + dq2[:blk]).astype(BF16)
                carry[:, sl] = dq2[blk:]
            else:
                dq_ref[:, sl] = dq2.astype(BF16)

    args, in_specs = [_attn_array(x, dims, dil) for x in (k, v, q, do, lse, delta)], [cur] * 4 + [cur_stat] * 2
    if has_next:
        args += [args[2], args[3], args[4], args[5]]
        in_specs += [nxt] * 2 + [nxt_stat] * 2
    shape = jax.ShapeDtypeStruct(args[2].shape, BF16)
    rs = _attn_residues(dims, dil)
    per_step = lambda dims_: dims_ if rs == 1 else (rs,) + dims_
    scratch = ([pltpu.VMEM(per_step((nq, a)), BF16)] * 2 + [pltpu.VMEM(per_step((heads, blk, nq)), F32)] * 2
               + [pltpu.VMEM(per_step((heads, blk, nq)), BF16)] * 2)
    if has_next:
        scratch.append(pltpu.VMEM(per_step((blk, a)), F32))
    grads = pl.pallas_call(
        _per_residue(body, rs), name=name, grid=grid, in_specs=in_specs, out_specs=[cur] * 3, out_shape=[shape] * 3,
        scratch_shapes=scratch,
        compiler_params=_params(*["parallel"] * (len(grid) - 1), "arbitrary"),
    )(*args)
    return tuple(g.reshape(q.shape) for g in grads)


def _qkv_layouts_bwd(z, grads, gq, gk, head_ones, dims, *, name, tr=256):
    t = z.shape[0]
    a = dims.n_heads * dims.head_dim
    q_scale = dims.head_dim ** -0.5
    dils = tuple(grads)
    nres = len(RESIDUE_DILATIONS)

    def body(q_ref, k_ref, *rest):
        d_refs = rest[:3 * len(dils)]
        gq_ref, gk_ref, sum_ref, spread_ref = rest[3 * len(dils):3 * len(dils) + 4]
        back_refs = dict(zip(RESIDUE_DILATIONS, rest[3 * len(dils) + 4:3 * len(dils) + 4 + nres]))
        dz_ref, dgq_ref, dgk_ref = rest[3 * len(dils) + 4 + nres:]
        first = pl.program_id(0) == 0
        mean = lambda val: _two_pass_dot(_two_pass_dot(val, sum_ref[...]), spread_ref[...]) * (1.0 / dims.head_dim)

        def total(j):
            acc = None
            for g, d in enumerate(dils):
                ref = d_refs[3 * g + j]
                part = ref[...].astype(F32) if d == 1 else _bf16_residues_to_rows(ref, back_refs[d])
                acc = part if acc is None else acc + part
            return acc

        def norm_bwd(x_ref, dy, g_ref, scale, col, dg_ref):
            xv = x_ref[...].astype(F32)
            dy = dy * scale
            r = lax.rsqrt(mean(xv * xv) + RMS_EPS)
            gy = dy * g_ref[...]
            dx = r * gy - xv * (r * r * r) * mean(xv * gy)
            dz_ref[:, col * a:(col + 1) * a] = dx.astype(BF16)
            _accumulate(dg_ref, jnp.sum(dy * xv * r, axis=0, keepdims=True), first)

        norm_bwd(q_ref, total(0), gq_ref, q_scale, 0, dgq_ref)
        norm_bwd(k_ref, total(1), gk_ref, 1.0, 1, dgk_ref)
        dz_ref[:, 2 * a:3 * a] = total(2).astype(BF16)

    in_specs, args = [_row_spec(tr, a, 2), _row_spec(tr, a, 3)], [z, z]
    for d in dils:
        in_specs += [_row_spec(tr, a) if d == 1 else _residue_spec(dims, d, tr, a)] * 3
        args += list(grads[d])
    in_specs += [_vec_spec(a), _vec_spec(a), pl.BlockSpec((a, LANES), lambda i: (0, 0)),
                 pl.BlockSpec((LANES, a), lambda i: (0, 0))] + [pl.BlockSpec((tr, tr), lambda i: (0, 0))] * nres
    return pl.pallas_call(
        body, name=name, grid=(t // tr,), in_specs=in_specs,
        out_specs=[_row_spec(tr, 3 * a), _vec_spec(a), _vec_spec(a)],
        out_shape=[jax.ShapeDtypeStruct((t, 3 * a), BF16)] + [jax.ShapeDtypeStruct((1, a), F32)] * 2,
        compiler_params=_params("arbitrary"),
    )(*args, gq, gk, *head_ones, *[jnp.transpose(p) for p in _residue_permutations(tr)])


def _mix_fwd(a3, o, w_a, w_b, w_out, z, gate_b, x, g2, dims, *, name, tr=512):
    t, d = x.shape
    tr = _pick(t, tr, 8)
    first_gate_col = z.shape[1] // d - 2

    def body(a_ref, o_ref, wa_ref, wb_ref, wo_ref, ga_ref, gb_ref, ba_ref, bb_ref, x_ref, g2_ref,
             ya_ref, yb_ref, mix_ref, x1_ref, h2_ref):
        ya = jnp.dot(a_ref[...], wa_ref[...], preferred_element_type=F32)
        yb = jnp.dot(o_ref[...], wb_ref[...], preferred_element_type=F32)
        ya_ref[...] = ya
        yb_ref[...] = yb
        g_a = _sigmoid(ga_ref[...].astype(F32) + ba_ref[...])
        g_b = _sigmoid(gb_ref[...].astype(F32) + bb_ref[...])
        mixed = (g_a * ya + g_b * yb).astype(BF16)
        mix_ref[...] = mixed
        x1 = x_ref[...] + jnp.dot(mixed, wo_ref[...], preferred_element_type=F32)
        x1_ref[...] = x1
        h2_ref[...] = (x1 * lax.rsqrt(jnp.mean(x1 * x1, axis=-1, keepdims=True) + RMS_EPS) * g2_ref[...]).astype(BF16)

    weight = pl.BlockSpec((None, d, d), lambda i: (0, 0, 0))
    rows = _row_spec(tr, d)
    return pl.pallas_call(
        body, name=name, grid=(t // tr,),
        in_specs=[rows, rows, weight, weight, weight, _row_spec(tr, d, first_gate_col),
                  _row_spec(tr, d, first_gate_col + 1), _vec_spec(d, 0), _vec_spec(d, 1), rows, _vec_spec(d)],
        out_specs=[rows] * 5,
        out_shape=[jax.ShapeDtypeStruct((t, d), dt) for dt in (F32, F32, BF16, F32, BF16)],
        compiler_params=_params("parallel"),
    )(a3, o, w_a, w_b, w_out, z, z, gate_b, gate_b, x, g2)


def _mix_bwd(dx, w, ya, yb, z, gate_b, dims, *, name, after=None, tr=512):
    t, d = ya.shape
    tr = _pick(t, tr, 8)
    first_gate_col = z.shape[1] // d - 2

    def body(dx_ref, w_ref, ya_ref, yb_ref, ga_ref, gb_ref, ba_ref, bb_ref, dya_ref, dyb_ref, dz_ref, db_ref):
        dm = _dot_nt(dx_ref[...], w_ref[...])
        g_a = _sigmoid(ga_ref[...].astype(F32) + ba_ref[...])
        g_b = _sigmoid(gb_ref[...].astype(F32) + bb_ref[...])
        dya_ref[...] = (dm * g_a).astype(BF16)
        dyb_ref[...] = (dm * g_b).astype(BF16)
        dl_a = dm * ya_ref[...] * g_a * (1.0 - g_a)
        dl_b = dm * yb_ref[...] * g_b * (1.0 - g_b)
        dz_ref[:, 0:d] = dl_a.astype(BF16)
        dz_ref[:, d:2 * d] = dl_b.astype(BF16)
        first = pl.program_id(0) == 0
        sums = jnp.concatenate([jnp.sum(dl_a, axis=0, keepdims=True), jnp.sum(dl_b, axis=0, keepdims=True)], axis=1)
        _accumulate(db_ref, sums, first)

    body, more_specs, more_args = _ordered(body, 8, after)
    return pl.pallas_call(
        body, name=name, grid=(t // tr,),
        in_specs=[_row_spec(tr, d), pl.BlockSpec((None, d, d), lambda i: (0, 0, 0)), _row_spec(tr, d), _row_spec(tr, d),
                  _row_spec(tr, d, first_gate_col), _row_spec(tr, d, first_gate_col + 1), _vec_spec(d, 0),
                  _vec_spec(d, 1)] + more_specs,
        out_specs=[_row_spec(tr, d), _row_spec(tr, d), _row_spec(tr, 2 * d), _vec_spec(2 * d)],
        out_shape=[jax.ShapeDtypeStruct((t, d), BF16)] * 2 + [jax.ShapeDtypeStruct((t, 2 * d), BF16),
                                                              jax.ShapeDtypeStruct((1, 2 * d), F32)],
        compiler_params=_params("arbitrary"),
    )(dx, w, ya, yb, z, z, gate_b, gate_b, *more_args)


def _adamw(w, grads, m, v, *, name, tr=256):
    r, c = w.shape
    tr = _pick(r, tr, 8)
    ng = len(grads)
    c1 = 1.0 - ADAM_B1 ** ADAM_STEP
    c2 = 1.0 - ADAM_B2 ** ADAM_STEP

    def body(*refs):
        w_ref, g_refs, m_ref, v_ref = refs[0], refs[1:1 + ng], refs[1 + ng], refs[2 + ng]
        g_out, d_out, m_out, v_out = refs[3 + ng:]
        g = g_refs[0][...]
        for extra in g_refs[1:]:
            g = g + extra[...]
        m_new = ADAM_B1 * m_ref[...] + (1.0 - ADAM_B1) * g
        v_new = ADAM_B2 * v_ref[...] + (1.0 - ADAM_B2) * (g * g)
        g_out[...] = g
        m_out[...] = m_new
        v_out[...] = v_new
        d_out[...] = -ADAM_LR * ((m_new / c1) / (jnp.sqrt(v_new / c2) + ADAM_EPS) + ADAM_WD * w_ref[...])

    spec = pl.BlockSpec((tr, c), lambda i: (i, 0))
    return pl.pallas_call(
        body, name=name, grid=(r // tr,),
        in_specs=[spec] * (3 + ng), out_specs=[spec] * 4, out_shape=[jax.ShapeDtypeStruct((r, c), F32)] * 4,
        compiler_params=_params("parallel"),
    )(w, *grads, m, v)


def _adamw_unpacked(ws, g, ms, vs, *, name):
    r, d = g.shape
    c1 = 1.0 - ADAM_B1 ** ADAM_STEP
    c2 = 1.0 - ADAM_B2 ** ADAM_STEP
    shapes = [a.shape for a in ws]
    views = [tuple(s) if len(s) == 2 else (1, s[0]) for s in shapes]
    n = len(views)

    def pieces():
        row = 0
        for i, (rows, width) in enumerate(views):
            for a in range(rows):
                pos = 0
                while pos < width:
                    at, lane = row + (a * width + pos) // d, (a * width + pos) % d
                    piece = min(width - pos, d - lane)
                    yield i, a, pos, at, lane, piece
                    pos += piece
            row += _packed_rows(rows * width, d)

    def body(*refs):
        ins, g_ref, outs, packed, scr = refs[:3 * n], refs[3 * n], refs[3 * n + 1:-2], refs[-2], refs[-1]
        packed[...] = jnp.zeros_like(packed)
        for kind in range(3):
            for i, a, pos, at, lane, piece in pieces():
                packed[kind, at:at + 1, lane:lane + piece] = ins[kind * n + i][a:a + 1, pos:pos + piece]
        grad = g_ref[...]
        m_new = ADAM_B1 * packed[1] + (1.0 - ADAM_B1) * grad
        v_new = ADAM_B2 * packed[2] + (1.0 - ADAM_B2) * (grad * grad)
        scr[0] = grad
        scr[1] = -ADAM_LR * ((m_new / c1) / (jnp.sqrt(v_new / c2) + ADAM_EPS) + ADAM_WD * packed[0])
        scr[2] = m_new
        scr[3] = v_new
        for kind in range(4):
            for i, a, pos, at, lane, piece in pieces():
                outs[kind * n + i][a:a + 1, pos:pos + piece] = scr[kind, at:at + 1, lane:lane + piece]

    whole = pl.BlockSpec(memory_space=pltpu.VMEM)
    outs = pl.pallas_call(
        body, name=name, in_specs=[whole] * (3 * n + 1), out_specs=[whole] * (4 * n),
        out_shape=[jax.ShapeDtypeStruct(view, F32) for _ in range(4) for view in views],
        scratch_shapes=[pltpu.VMEM((3, r, d), F32), pltpu.VMEM((4, r, d), F32)], compiler_params=_params(),
    )(*[a.reshape(view) for grp in (ws, ms, vs) for a, view in zip(grp, views)], g)
    return [[o.reshape(s) for o, s in zip(outs[kind * n:(kind + 1) * n], shapes)] for kind in range(4)]


CHIP_PEERS = ((1, 0), (0, 1), (1, 1))


def _place():
    return lax.axis_index("x"), lax.axis_index("y"), lax.axis_index("c")


HBM = pl.BlockSpec(memory_space=pltpu.HBM)
SEM = pl.BlockSpec(memory_space=pltpu.SEMAPHORE)
IN_FLIGHT = pltpu.SideEffectType.DATAFLOW_SIDE_EFFECTING


def _in_hbm(a):
    return pltpu.with_memory_space_constraint(a, pltpu.HBM)


def _cast_to_lands(shards, dtypes, *, name, after=None):
    n = len(shards)

    def body(*refs):
        ins, outs, bufs, sems = refs[:n], refs[n:2 * n], refs[2 * n:3 * n], refs[3 * n]
        x, y, _ = _place()
        copies = []
        for a in range(n):
            bufs[a][...] = ins[a][...].astype(dtypes[a])
            cp = pltpu.make_async_copy(bufs[a], outs[a].at[2 * x + y], sems.at[a])
            cp.start()
            copies.append(cp)
        for cp in copies:
            cp.wait()

    body, more_specs, more_args = _ordered(body, n, after)
    return pl.pallas_call(
        body, name=name, in_specs=[pl.BlockSpec(memory_space=pltpu.VMEM)] * n + more_specs, out_specs=[ANY] * n,
        out_shape=[jax.ShapeDtypeStruct((N_CHIPS,) + s.shape, dt) for s, dt in zip(shards, dtypes)],
        scratch_shapes=[pltpu.VMEM(s.shape, dt) for s, dt in zip(shards, dtypes)] + [pltpu.SemaphoreType.DMA((n,))],
        compiler_params=pltpu.CompilerParams(vmem_limit_bytes=V7X_VMEM_LIMIT_BYTES),
    )(*shards, *more_args)


def _chip_copy(src, dst, send, recv, flip, place):
    x, y, c = place
    return pltpu.make_async_remote_copy(src_ref=src, dst_ref=dst, send_sem=send, recv_sem=recv,
                                        device_id=(x ^ flip[0], y ^ flip[1], c), device_id_type=MESH)


def _my_part(land, place, halved):
    block = land.at[2 * place[0] + place[1]]
    if not halved:
        return block
    rows = land.shape[1] // 2
    return block.at[pl.ds(pl.multiple_of(place[2] * rows, rows), rows)]


def _gather_start(lands, after, *, name, halved=()):
    n = len(lands)

    def body(*refs):
        ins, send, recv, token = refs[:n], refs[n + 1], refs[n + 2], refs[-1]
        place = _place()
        for a in range(n):
            part = _my_part(ins[a], place, a in halved)
            for p, flip in enumerate(CHIP_PEERS):
                k = 3 * a + p
                _chip_copy(part, part, send.at[k], recv.at[k], flip, place).start()
        token[...] = jnp.zeros_like(token)

    outs = pl.pallas_call(
        body, name=name, in_specs=[HBM] * n + [ANY],
        out_specs=(SEM, SEM, *[HBM] * n, pl.BlockSpec(memory_space=pltpu.VMEM)),
        out_shape=(pltpu.SemaphoreType.DMA((3 * n,)), pltpu.SemaphoreType.DMA((3 * n,)),
                   *[pltpu.HBM(l.shape, l.dtype) for l in lands], jax.ShapeDtypeStruct((8, 128), F32)),
        input_output_aliases={a: 2 + a for a in range(n)},
        compiler_params=pltpu.CompilerParams(has_side_effects=IN_FLIGHT),
    )(*[_in_hbm(l) for l in lands], after)
    return outs[0], outs[1], list(outs[2:2 + n]), outs[-1]


def _gather_wait(send, recv, lands, after, *, name, halved=()):
    n = len(lands)

    def body(*refs):
        ins, send_ref, recv_ref = refs[:n], refs[n], refs[n + 1]
        place = _place()
        for a in range(n):
            part = _my_part(ins[a], place, a in halved)
            for p, flip in enumerate(CHIP_PEERS):
                k = 3 * a + p
                cp = _chip_copy(part, part, send_ref.at[k], recv_ref.at[k], flip, place)
                cp.wait_send()
                cp.wait_recv()

    after = list(after) if isinstance(after, (list, tuple)) else [after]
    return pl.pallas_call(
        body, name=name, in_specs=[HBM] * n + [SEM, SEM] + [ANY] * len(after), out_specs=[HBM] * n,
        out_shape=[pltpu.HBM(l.shape, l.dtype) for l in lands],
        input_output_aliases={a: a for a in range(n)},
        compiler_params=pltpu.CompilerParams(has_side_effects=IN_FLIGHT),
    )(*lands, send, recv, *after)


def _gather_wait_forward(send, recv, lands, after, *, name):
    n = len(lands)
    rows = lands[0].shape[1] // 2

    def body(*refs):
        ins, send_ref, recv_ref = refs[:n], refs[n], refs[n + 1]
        outs, on_send, on_recv = refs[-n - 2:-2], refs[-2], refs[-1]
        place = _place()
        x, y, c = place
        mine = pl.ds(pl.multiple_of(c * rows, rows), rows)
        theirs = pl.ds(pl.multiple_of((1 - c) * rows, rows), rows)

        def onward(chip, half, p):
            return pltpu.make_async_remote_copy(
                src_ref=ins[0].at[chip].at[half], dst_ref=outs[0].at[chip].at[half], send_sem=on_send.at[p],
                recv_sem=on_recv.at[p], device_id=(x, y, 1 - c), device_id_type=MESH)

        half = _my_part(ins[0], place, True)
        chips = [2 * (x ^ fx) + (y ^ fy) for fx, fy in CHIP_PEERS]
        for p, flip in enumerate(CHIP_PEERS):
            _chip_copy(half, half, send_ref.at[p], recv_ref.at[p], flip, place).wait_recv()
            onward(chips[p], mine, p).start()
        for a in range(n):
            part = _my_part(ins[a], place, a == 0)
            for p, flip in enumerate(CHIP_PEERS):
                cp = _chip_copy(part, part, send_ref.at[3 * a + p], recv_ref.at[3 * a + p], flip, place)
                cp.wait_send()
                if a > 0:
                    cp.wait_recv()

    after = list(after) if isinstance(after, (list, tuple)) else [after]
    *zones, on_send, on_recv = pl.pallas_call(
        body, name=name, in_specs=[HBM] * n + [SEM, SEM] + [ANY] * len(after), out_specs=[HBM] * n + [SEM, SEM],
        out_shape=[pltpu.HBM(l.shape, l.dtype) for l in lands] + [pltpu.SemaphoreType.DMA((3,))] * 2,
        input_output_aliases={a: a for a in range(n)},
        compiler_params=pltpu.CompilerParams(has_side_effects=IN_FLIGHT),
    )(*lands, send, recv, *after)

    def wait_body(land_ref, send_ref, recv_ref, out_ref):
        x, y, c = _place()
        for p, (fx, fy) in enumerate(CHIP_PEERS):
            chip = 2 * (x ^ fx) + (y ^ fy)
            for half, arriving in ((c, False), (1 - c, True)):
                at = pl.ds(pl.multiple_of(half * rows, rows), rows)
                cp = pltpu.make_async_remote_copy(
                    src_ref=land_ref.at[chip].at[at], dst_ref=out_ref.at[chip].at[at], send_sem=send_ref.at[p],
                    recv_sem=recv_ref.at[p], device_id=(x, y, 1 - c), device_id_type=MESH)
                cp.wait_recv() if arriving else cp.wait_send()

    zones[0] = pl.pallas_call(
        wait_body, name=f"{name}_forwarded", in_specs=[HBM, SEM, SEM], out_specs=HBM,
        out_shape=pltpu.HBM(zones[0].shape, zones[0].dtype), input_output_aliases={0: 0},
        compiler_params=pltpu.CompilerParams(has_side_effects=IN_FLIGHT),
    )(zones[0], on_send, on_recv)
    return zones


def _scatter_start(grad, *, name):
    def body(g_ref, land_ref, send, recv, g_thru, land_thru, token):
        place = _place()
        for p, flip in enumerate(CHIP_PEERS):
            peer_chip = 2 * (place[0] ^ flip[0]) + (place[1] ^ flip[1])
            _chip_copy(g_ref.at[peer_chip], land_ref.at[p], send.at[p], recv.at[p], flip, place).start()
        token[...] = jnp.zeros_like(token)

    land = lax.empty((3,) + grad.shape[1:], grad.dtype)
    return pl.pallas_call(
        body, name=name, in_specs=[HBM, HBM],
        out_specs=(SEM, SEM, HBM, HBM, pl.BlockSpec(memory_space=pltpu.VMEM)),
        out_shape=(pltpu.SemaphoreType.DMA((3,)), pltpu.SemaphoreType.DMA((3,)), pltpu.HBM(grad.shape, grad.dtype),
                   pltpu.HBM(land.shape, land.dtype), jax.ShapeDtypeStruct((8, 128), F32)),
        input_output_aliases={0: 2, 1: 3},
        compiler_params=pltpu.CompilerParams(has_side_effects=IN_FLIGHT),
    )(_in_hbm(grad), _in_hbm(land))


def _scatter_wait(started, after, *, name):
    n = len(started)

    def body(*refs):
        grads, lands = refs[:n], refs[n:2 * n]
        sends, recvs = refs[2 * n:3 * n], refs[3 * n:4 * n]
        place = _place()
        for a in range(n):
            for p, flip in enumerate(CHIP_PEERS):
                cp = _chip_copy(grads[a].at[0], lands[a].at[p], sends[a].at[p], recvs[a].at[p], flip, place)
                cp.wait_send()
                cp.wait_recv()

    grads, lands = [s[2] for s in started], [s[3] for s in started]
    after = list(after) if isinstance(after, (list, tuple)) else [after]
    outs = pl.pallas_call(
        body, name=name, in_specs=[HBM] * (2 * n) + [SEM] * (2 * n) + [ANY] * len(after), out_specs=[HBM] * (2 * n),
        out_shape=[pltpu.HBM(a.shape, a.dtype) for a in grads + lands],
        input_output_aliases={a: a for a in range(2 * n)},
        compiler_params=pltpu.CompilerParams(has_side_effects=IN_FLIGHT),
    )(*grads, *lands, *[s[0] for s in started], *[s[1] for s in started], *after)
    return list(zip(outs[:n], outs[n:]))


def _sibling_copy(src, dst, send, recv, place):
    x, y, c = place
    return pltpu.make_async_remote_copy(src_ref=src, dst_ref=dst, send_sem=send, recv_sem=recv,
                                        device_id=(x, y, 1 - c), device_id_type=MESH)


def _swap_start(arrays, *, name):
    n = len(arrays)

    def body(*refs):
        ins, lands, send, recv, token = refs[:n], refs[n:2 * n], refs[2 * n], refs[2 * n + 1], refs[-1]
        place = _place()
        for a in range(n):
            _sibling_copy(ins[a], lands[a], send.at[a], recv.at[a], place).start()
        token[...] = jnp.zeros_like(token)

    both = [_in_hbm(a) for a in arrays] + [_in_hbm(lax.empty(a.shape, a.dtype)) for a in arrays]
    outs = pl.pallas_call(
        body, name=name, in_specs=[HBM] * (2 * n),
        out_specs=(SEM, SEM, *[HBM] * (2 * n), pl.BlockSpec(memory_space=pltpu.VMEM)),
        out_shape=(pltpu.SemaphoreType.DMA((n,)), pltpu.SemaphoreType.DMA((n,)),
                   *[pltpu.HBM(a.shape, a.dtype) for a in both], jax.ShapeDtypeStruct((8, 128), F32)),
        input_output_aliases={a: 2 + a for a in range(2 * n)},
        compiler_params=pltpu.CompilerParams(has_side_effects=IN_FLIGHT),
    )(*both)
    return outs[0], outs[1], list(outs[2:2 + n]), list(outs[2 + n:2 + 2 * n]), outs[-1]


def _swap_wait(started, after, *, name):
    send, recv, arrays, lands = started[:4]
    n = len(arrays)

    def body(*refs):
        ins, zones, send_ref, recv_ref = refs[:n], refs[n:2 * n], refs[2 * n], refs[2 * n + 1]
        place = _place()
        for a in range(n):
            cp = _sibling_copy(ins[a], zones[a], send_ref.at[a], recv_ref.at[a], place)
            cp.wait_send()
            cp.wait_recv()

    after = list(after) if isinstance(after, (list, tuple)) else [after]
    outs = pl.pallas_call(
        body, name=name, in_specs=[HBM] * (2 * n) + [SEM, SEM] + [ANY] * len(after), out_specs=[HBM] * (2 * n),
        out_shape=[pltpu.HBM(a.shape, a.dtype) for a in arrays + lands],
        input_output_aliases={a: a for a in range(2 * n)},
        compiler_params=pltpu.CompilerParams(has_side_effects=IN_FLIGHT),
    )(*arrays, *lands, send, recv, *after)
    return list(outs[:n]), list(outs[n:])


def _allreduce_start(packed, *, name):
    n_dev = 8

    def body(src_ref, land_ref, send, recv, src_thru, land_thru, token):
        x, y, c = _place()
        me = 4 * x + 2 * y + c
        for p in range(1, n_dev):
            pltpu.make_async_remote_copy(
                src_ref=src_ref, dst_ref=land_ref.at[me], send_sem=send.at[p - 1], recv_sem=recv.at[p - 1],
                device_id=(x ^ (p >> 2), y ^ ((p >> 1) & 1), c ^ (p & 1)), device_id_type=MESH).start()
        token[...] = jnp.zeros_like(token)

    land = lax.empty((n_dev,) + packed.shape, packed.dtype)
    return pl.pallas_call(
        body, name=name, in_specs=[HBM, HBM],
        out_specs=(SEM, SEM, HBM, HBM, pl.BlockSpec(memory_space=pltpu.VMEM)),
        out_shape=(pltpu.SemaphoreType.DMA((n_dev - 1,)), pltpu.SemaphoreType.DMA((n_dev - 1,)),
                   pltpu.HBM(packed.shape, packed.dtype), pltpu.HBM(land.shape, land.dtype),
                   jax.ShapeDtypeStruct((8, 128), F32)),
        input_output_aliases={0: 2, 1: 3},
        compiler_params=pltpu.CompilerParams(has_side_effects=IN_FLIGHT),
    )(_in_hbm(packed), _in_hbm(land))


def _allreduce_wait(started, after, *, name):
    send, recv, packed, land = started[:4]
    n_dev = 8

    def body(src_ref, land_ref, send_ref, recv_ref, *_):
        x, y, c = _place()
        for p in range(1, n_dev):
            cp = pltpu.make_async_remote_copy(
                src_ref=src_ref, dst_ref=land_ref.at[0], send_sem=send_ref.at[p - 1], recv_sem=recv_ref.at[p - 1],
                device_id=(x ^ (p >> 2), y ^ ((p >> 1) & 1), c ^ (p & 1)), device_id_type=MESH)
            cp.wait_send()
            cp.wait_recv()

    after = list(after) if isinstance(after, (list, tuple)) else [after]
    return pl.pallas_call(
        body, name=name, in_specs=[HBM, HBM, SEM, SEM] + [ANY] * len(after), out_specs=[HBM, HBM],
        out_shape=[pltpu.HBM(packed.shape, packed.dtype), pltpu.HBM(land.shape, land.dtype)],
        input_output_aliases={0: 0, 1: 1},
        compiler_params=pltpu.CompilerParams(has_side_effects=IN_FLIGHT),
    )(packed, land, send, recv, *after)


def _sum_devices(mine, land, *, name):
    n_dev = land.shape[0]

    def body(mine_ref, land_ref, out_ref):
        x, y, c = _place()
        me = 4 * x + 2 * y + c
        total = None
        for s in range(n_dev):
            part = jnp.where(me == s, mine_ref[...], land_ref[s])
            total = part if total is None else total + part
        out_ref[...] = total

    return pl.pallas_call(body, name=name, out_shape=jax.ShapeDtypeStruct(mine.shape, mine.dtype))(mine, land)


def _sum_received(grad, land, *, name, tr=256):
    _, r, c = grad.shape
    tr = _pick(r, tr, 8)

    def body(chip_ref, g_ref, l_ref, o_ref):
        o_ref[...] = ((g_ref[...] + l_ref[0].astype(F32)) + l_ref[1].astype(F32)) + l_ref[2].astype(F32)

    chip = (2 * lax.axis_index("x") + lax.axis_index("y")).astype(jnp.int32).reshape(1)
    return pl.pallas_call(
        body, name=name,
        grid_spec=pltpu.PrefetchScalarGridSpec(
            num_scalar_prefetch=1, grid=(r // tr,),
            in_specs=[pl.BlockSpec((None, tr, c), lambda i, chip_ref: (chip_ref[0], i, 0)),
                      pl.BlockSpec((3, tr, c), lambda i, chip_ref: (0, i, 0))],
            out_specs=pl.BlockSpec((tr, c), lambda i, chip_ref: (i, 0))),
        out_shape=jax.ShapeDtypeStruct((r, c), F32), compiler_params=_params("parallel"),
    )(chip, grad, land)


def _packed_rows(size, d):
    return -(-size // (8 * d)) * 8


def _pack_rows(arrays, d):
    rows = []
    for arr in arrays:
        flat = arr.reshape(-1).astype(F32)
        n = _packed_rows(flat.shape[0], d)
        rows.append(jnp.pad(flat, (0, n * d - flat.shape[0])).reshape(n, d))
    return jnp.concatenate(rows, axis=0)


def _unpack_rows(packed, shapes, d):
    out, row = [], 0
    for shape in shapes:
        size = math.prod(shape)
        n = _packed_rows(size, d)
        out.append(packed[row:row + n].reshape(-1)[:size].reshape(shape))
        row += n
    return out


SMALL = ("norm1_g", "gate_b", "conv_b", "conv_norm_g", "q_norm_g", "k_norm_g", "norm2_g", "ffn_conv_b")
LARGE = ("w_in", "w_conv_out", "w_attn_out", "w_out", "w_up", "w_down")
WEIGHTS = ("norm1_g", "w_in", "gate_b", "conv_w", "conv_b", "conv_norm_g", "w_conv_out", "q_norm_g", "k_norm_g",
           "w_attn_out", "w_out", "norm2_g", "w_up", "ffn_conv_w", "ffn_conv_b", "w_down")


def _after(vec, token):
    return vec if token is None else vec + token[0:1, 0:1]


def _local_step(dims, x, target, small, first_weights, other_weights, send_grad):
    d, f, heads = dims.d_model, dims.d_ff, dims.n_heads
    small = dict(small)
    row = lambda name: small[name].reshape(1, -1)
    head_sum = _head_sum_matrix(dims)
    head_spread = jnp.transpose(head_sum)
    ones = (head_sum, head_spread)
    gq = jnp.tile(row("q_norm_g"), (1, heads))
    gk = jnp.tile(row("k_norm_g"), (1, heads))
    one_shard = lambda w: w.reshape(1, -1, w.shape[-1])

    h = _rmsnorm_fwd(x, row("norm1_g"), name="norm1")
    full = first_weights(h)
    w_in = full["w_in"]
    conv_w = jnp.pad(full["conv_w"], ((0, CONV_HALO - dims.conv_width), (0, 0)))
    ffn_w = jnp.pad(full["ffn_conv_w"], ((0, FFN_HALO - dims.ffn_conv_width), (0, 0)))
    z = _mm_nn(h, w_in, out_dtype=BF16, after=full.get("token"), tm=2048, tn=1792, name="in_proj")
    a1, a3 = _conv_branch_fwd(z, conv_w, row("conv_b"), row("conv_norm_g"), dims, name="conv_branch")
    qkv = _qkv_layouts_fwd(z, gq, gk, ones, dims, name="qk_norm")
    per_group = {dil: _attn_fwd(*qkv[dil], dims, dil, name=f"attn_fwd_d{dil}") for dil in DILATIONS}
    o, lse = _attn_combine(per_group, head_spread, dims, name="attn_combine")
    full = other_weights(o)
    w_up = full["w_up"]
    w_co, w_ao, w_o, w_dn = (one_shard(full[k]) for k in ("w_conv_out", "w_attn_out", "w_out", "w_down"))
    ya, yb, mixed, x1, h2 = _mix_fwd(a3, o, w_co, w_ao, w_o, z, row("gate_b"), x, row("norm2_g"), dims,
                                     name="branch_projs_mix_out_proj_norm2")
    up = _mm_nn(h2, w_up, out_dtype=F32, tm=2048, name="up_proj")
    act = _ffn_act_fwd(up, ffn_w, row("ffn_conv_b"), dims, name="ffn_act")
    dy, dy_b, loss = _proj_residual_loss(act, w_dn, x1, target, tm=512, name="down_proj_loss")

    grads = {}

    def large(name, g):
        grads[name], g_bf16 = g
        return send_grad(name, grads[name], g_bf16)

    sent = large("w_down", _mm_tn(act, dy_b, n_shards=1, name="dw_down"))
    dact = _mm_nt(dy_b, w_dn, out_dtype=BF16, after=sent, name="d_act")
    dup, dfw, dfb = _ffn_bwd(dact, up, ffn_w, row("ffn_conv_b"), dims, name="ffn_bwd")
    grads["ffn_conv_w"], grads["ffn_conv_b"] = dfw[:dims.ffn_conv_width], dfb
    sent = large("w_up", _mm_tn(h2, dup, n_shards=N_CHIPS, name="dw_up"))
    dx1, dx1_b, grads["norm2_g"] = _mm_nt_rmsnorm_bwd(dup, w_up, x1, row("norm2_g"), dy, want_bf16=True, after=sent,
                                                     name="d_h2_norm2_bwd")
    sent = large("w_out", _mm_tn(mixed, dx1_b, n_shards=1, name="dw_out"))
    dya, dyb, dz_gate, grads["gate_b"] = _mix_bwd(dx1_b, w_o, ya, yb, z, row("gate_b"), dims, after=sent,
                                                  name="d_mix_gate_mix_bwd")
    sent = large("w_attn_out", _mm_tn(o, dyb, n_shards=1, name="dw_attn_out"))
    dos, deltas = _attn_bwd_prep(dyb, w_ao, o, head_sum, dims, after=sent, name="d_attn_bwd_prep")
    dqkv = {dil: _attn_bwd(*qkv[dil], dos[dil], lse[dil], deltas[dil], dims, dil, name=f"attn_bwd_d{dil}")
            for dil in DILATIONS}
    dz_qkv, dgq, dgk = _qkv_layouts_bwd(z, dqkv, gq, gk, ones, dims, name="qk_norm_bwd")
    grads["q_norm_g"] = dgq.reshape(heads, dims.head_dim).sum(axis=0)
    grads["k_norm_g"] = dgk.reshape(heads, dims.head_dim).sum(axis=0)
    sent = large("w_conv_out", _mm_tn(a3, dya, n_shards=1, name="dw_conv_out"))
    da1, grads["conv_norm_g"] = _mm_nt_rmsnorm_bwd(dya, w_co, a1, row("conv_norm_g"), None, want_bf16=False, silu=True,
                                                   after=sent, name="d_conv_act_norm_bwd")
    dz, dcw, grads["conv_b"] = _conv_branch_bwd(da1, z, conv_w, [dz_qkv, dz_gate], dims, name="conv_branch_bwd")
    grads["conv_w"] = dcw[:dims.conv_width]
    sent = large("w_in", _mm_tn(h, dz, n_shards=N_CHIPS, name="dw_in"))
    dx, grads["norm1_g"] = _mm_nt_rmsnorm_bwd(dz, w_in, x, row("norm1_g"), dx1, want_bf16=False, after=sent,
                                              name="d_h_norm1_bwd")
    return loss, dx, grads


def _step(dims, x, target, w, m, v):
    d = dims.d_model
    t = dims.tokens
    sq = lambda a: a.reshape(a.shape[1:])
    w2, m2, v2 = ({k: sq(a) for k, a in grp.items()} for grp in (w, m, v))

    conv_pad = jnp.pad(w2["conv_w"], ((0, CONV_HALO - dims.conv_width), (0, 0)))
    ffn_pad = jnp.pad(w2["ffn_conv_w"], ((0, FFN_HALO - dims.ffn_conv_width), (0, 0)))
    first_names = ("w_in", "conv_w", "ffn_conv_w")
    other_names = tuple(k for k in LARGE if k not in first_names)
    lands = dict(zip(first_names, _cast_to_lands([w2["w_in"], conv_pad, ffn_pad], [BF16, F32, F32], name="cast_first")))
    first = _gather_start([lands[k] for k in first_names], x, halved=(0,), name="gather_start_first")
    lands.update(zip(other_names, _cast_to_lands([w2[k] for k in other_names], [BF16] * len(other_names),
                                                 after=first[3], name="cast_other")))
    other = []
    cols = lambda g, rows: jnp.moveaxis(g, 0, 1).reshape(g.shape[1], -1)[:rows]

    def first_weights(after):
        got = dict(zip(first_names, _gather_wait_forward(*first[:3], [after] + [lands[k] for k in other_names],
                                                         name="gather_wait_first")))
        other.extend(_gather_start([lands[k] for k in other_names], got["w_in"], name="gather_start_other"))
        got["conv_w"] = cols(got["conv_w"], dims.conv_width)
        got["ffn_conv_w"] = cols(got["ffn_conv_w"], dims.ffn_conv_width)
        got["token"] = other[3]
        return got

    def other_weights(after):
        return dict(zip(other_names, _gather_wait(*other[:3], after, name="gather_wait_other")))

    started, full, swapping_others = {}, {}, []
    others = [k for k in LARGE if k != "w_in"]

    def my_sums(names, after, tag):
        arrived = _scatter_wait([started[k] for k in names], after, name=f"scatter_wait_{tag}")
        return [_sum_received(full[k], land, name=f"sum_{k}") for k, (_, land) in zip(names, arrived)]

    def send_grad(name, g, g_bf16):
        blocks = lambda a: a.reshape(N_CHIPS, -1, a.shape[-1])
        send, recv, g_thru, land, token = _scatter_start(blocks(g_bf16), name=f"scatter_start_{name}")
        started[name], full[name] = (send, recv, g_thru, land), blocks(g)
        if name != "w_in":
            return token
        swapping_others.extend(_swap_start(my_sums(others, token, "others"), name="swap_start_others"))
        return swapping_others[4]

    small = {k: w2[k] for k in SMALL}
    small["norm1_g"] = _after(small["norm1_g"].reshape(1, -1), first[3])
    loss, dx, grads = _local_step(dims, x.reshape(t, d), target.reshape(t, d), small, first_weights, other_weights, send_grad)

    def updates(names, mine, theirs):
        return {k: _adamw(w2[k], [a, b], m2[k], v2[k], name=f"adamw_{k}") for k, a, b in zip(names, mine, theirs)}

    small_names = SMALL + ("conv_w", "ffn_conv_w")
    packed = _pack_rows([grads[k] for k in small_names] + [loss[0, 0]], d)
    reducing = _allreduce_start(packed, name="allreduce_start")
    out = updates(others, *_swap_wait(swapping_others, [dx, reducing[4]], name="swap_wait_others"))
    mine_w_in = my_sums(["w_in"], [out[k][1] for k in others], "w_in")
    swapping_w_in = _swap_start(mine_w_in, name="swap_start_w_in")
    reduced = _sum_devices(*_allreduce_wait(reducing, swapping_w_in[4], name="allreduce_wait"), name="allreduce_sum")
    vector_rows = sum(_packed_rows(math.prod(grads[k].shape), d) for k in SMALL)
    tail_shapes = [grads[k].shape for k in ("conv_w", "ffn_conv_w")] + [()]
    conv_g, ffn_g, loss_total = _unpack_rows(reduced[vector_rows:], tail_shapes, d)
    chip = 2 * lax.axis_index("x") + lax.axis_index("y")
    sharded_g = [lax.dynamic_slice_in_dim(g, chip * w2[k].shape[1], w2[k].shape[1], axis=1)
                 for k, g in (("conv_w", conv_g), ("ffn_conv_w", ffn_g))]
    packed_g = jnp.concatenate([reduced[:vector_rows], _pack_rows(sharded_g, d)], axis=0)

    smalls = lambda grp: [grp[k] for k in small_names]
    results = _adamw_unpacked(smalls(w2), packed_g, smalls(m2), smalls(v2), name="adamw_small")
    out.update({k: tuple(r[i] for r in results) for i, k in enumerate(small_names)})
    out.update(updates(["w_in"], *_swap_wait(swapping_w_in, results[1][0], name="swap_wait_w_in")))

    lead = lambda a: a.reshape((1,) + a.shape)
    ordered = [[lead(out[k][j].reshape(w2[k].shape)) for k in WEIGHTS] for j in range(4)]
    return (loss_total, dx.reshape(x.shape), *ordered[0], *ordered[1], *ordered[2], *ordered[3])


def kernel(x, norm1_g, w_in, gate_b, conv_w, conv_b, conv_norm_g, w_conv_out, q_norm_g, k_norm_g, w_attn_out, w_out, norm2_g, w_up, ffn_conv_w, ffn_conv_b, w_down, loss_target, m_norm1_g, m_w_in, m_gate_b, m_conv_w, m_conv_b, m_conv_norm_g, m_w_conv_out, m_q_norm_g, m_k_norm_g, m_w_attn_out, m_w_out, m_norm2_g, m_w_up, m_ffn_conv_w, m_ffn_conv_b, m_w_down, v_norm1_g, v_w_in, v_gate_b, v_conv_w, v_conv_b, v_conv_norm_g, v_w_conv_out, v_q_norm_g, v_k_norm_g, v_w_attn_out, v_w_out, v_norm2_g, v_w_up, v_ffn_conv_w, v_ffn_conv_b, v_w_down):
    w = dict(zip(WEIGHTS, (norm1_g, w_in, gate_b, conv_w, conv_b, conv_norm_g, w_conv_out, q_norm_g, k_norm_g,
                           w_attn_out, w_out, norm2_g, w_up, ffn_conv_w, ffn_conv_b, w_down)))
    m = dict(zip(WEIGHTS, (m_norm1_g, m_w_in, m_gate_b, m_conv_w, m_conv_b, m_conv_norm_g, m_w_conv_out, m_q_norm_g,
                           m_k_norm_g, m_w_attn_out, m_w_out, m_norm2_g, m_w_up, m_ffn_conv_w, m_ffn_conv_b, m_w_down)))
    v = dict(zip(WEIGHTS, (v_norm1_g, v_w_in, v_gate_b, v_conv_w, v_conv_b, v_conv_norm_g, v_w_conv_out, v_q_norm_g,
                           v_k_norm_g, v_w_attn_out, v_w_out, v_norm2_g, v_w_up, v_ffn_conv_w, v_ffn_conv_b, v_w_down)))
    dims = Dims(d_model=x.shape[-1], batch_local=x.shape[0], seq=x.shape[1], d_ff=w_down.shape[1] * N_CHIPS)
    return _step(dims, x, loss_target, w, m, v)
```

```python
import functools
import math
from typing import NamedTuple

import jax
import jax.numpy as jnp
from jax import lax
from jax.experimental import pallas as pl
from jax.experimental.pallas import tpu as pltpu

F32 = jnp.float32
BF16 = jnp.bfloat16

RMS_EPS = 1e-6
ATTN_BLOCK = 128
DILATIONS = (1, 4, 16)
CONV_HALO = 32
FFN_HALO = 8
ADAM_LR, ADAM_B1, ADAM_B2, ADAM_EPS, ADAM_WD, ADAM_STEP = 0.001, 0.9, 0.999, 1e-08, 0.01, 10
V7X_VMEM_LIMIT_BYTES = 56 * 2 ** 20
N_CHIPS = 4
MESH = pl.DeviceIdType.MESH


class Dims(NamedTuple):
    d_model: int = 1024
    n_heads: int = 16
    head_dim: int = 64
    d_ff: int = 2816
    seq: int = 2048
    batch_local: int = 2
    conv_width: int = 31
    ffn_conv_width: int = 3

    @property
    def tokens(self):
        return self.seq * self.batch_local


def _params(*semantics):
    return pltpu.CompilerParams(dimension_semantics=semantics, vmem_limit_bytes=V7X_VMEM_LIMIT_BYTES)


ANY = pl.BlockSpec(memory_space=pl.ANY)


def _ordered(body, n_inputs, after):
    after = [] if after is None else list(after) if isinstance(after, (list, tuple)) else [after]
    if not after:
        return body, [], []

    def wrapped(*refs):
        return body(*refs[:n_inputs], *refs[n_inputs + len(after):])

    return wrapped, [ANY] * len(after), after


def _pick(n, target, mult=128):
    if n <= target:
        return n
    best = None
    for t in range(mult, target + 1, mult):
        if n % t == 0:
            best = t
    assert best is not None, (n, target, mult)
    return best


def _sigmoid(v):
    return 1.0 / (1.0 + jnp.exp(-v))


def _mm_nn(a, w, *, out_dtype, name, residual=None, after=None, tm=1024, tn=1408, tk=2816):
    m, k = a.shape
    nsh, k2, c = w.shape
    assert k == k2 and a.dtype == BF16 and w.dtype == BF16
    n = nsh * c
    tm, tn, tk = _pick(m, tm, 8), _pick(c, tn), _pick(k, tk)
    nk, cpn = k // tk, c // tn

    def body(*refs):
        if residual is None:
            a_ref, w_ref, o_ref, acc = refs
        else:
            a_ref, w_ref, r_ref, o_ref, acc = refs
        prod = jnp.dot(a_ref[...], w_ref[...], preferred_element_type=F32)

        def finish(total):
            if residual is not None:
                total = total + r_ref[...]
            o_ref[...] = total.astype(out_dtype)

        if nk == 1:
            finish(prod)
        else:
            kk = pl.program_id(2)

            @pl.when(kk == 0)
            def _():
                acc[...] = prod

            @pl.when(kk > 0)
            def _():
                acc[...] += prod

            @pl.when(kk == nk - 1)
            def _():
                finish(acc[...])

    in_specs = [pl.BlockSpec((tm, tk), lambda i, j, kk: (i, kk)),
                pl.BlockSpec((None, tk, tn), lambda i, j, kk: (j // cpn, kk, j % cpn))]
    args = [a, w]
    if residual is not None:
        in_specs.append(pl.BlockSpec((tm, tn), lambda i, j, kk: (i, j)))
        args.append(residual)
    body, more_specs, more_args = _ordered(body, len(args), after)
    return pl.pallas_call(
        body, name=name, grid=(m // tm, n // tn, nk),
        in_specs=in_specs + more_specs, out_specs=pl.BlockSpec((tm, tn), lambda i, j, kk: (i, j)),
        out_shape=jax.ShapeDtypeStruct((m, n), out_dtype),
        scratch_shapes=[pltpu.VMEM((tm, tn) if nk > 1 else (8, 128), F32)],
        compiler_params=_params("parallel", "parallel", "arbitrary"),
    )(*args, *more_args)


def _proj_residual_loss(a, w, residual, target, *, name, tm=1024):
    m, k = a.shape
    _, k2, n = w.shape
    assert w.shape[0] == 1 and k == k2 and a.dtype == BF16 and w.dtype == BF16
    tm = _pick(m, tm, 8)

    def body(a_ref, w_ref, r_ref, t_ref, dy_ref, dyb_ref, loss_ref):
        err = r_ref[...] + jnp.dot(a_ref[...], w_ref[...], preferred_element_type=F32) - t_ref[...]
        dy = err * (1.0 / n)
        dy_ref[...] = dy
        dyb_ref[...] = dy.astype(BF16)
        part = jnp.sum(jnp.sum(err * err, axis=-1, keepdims=True), axis=0, keepdims=True) * (0.5 / n)
        _accumulate(loss_ref, jnp.broadcast_to(part, (8, 128)), pl.program_id(0) == 0)

    rows = lambda width: pl.BlockSpec((tm, width), lambda i: (i, 0))
    return pl.pallas_call(
        body, name=name, grid=(m // tm,),
        in_specs=[rows(k), pl.BlockSpec((None, k, n), lambda i: (0, 0, 0)), rows(n), rows(n)],
        out_specs=[rows(n), rows(n), pl.BlockSpec((8, 128), lambda i: (0, 0))],
        out_shape=[jax.ShapeDtypeStruct((m, n), F32), jax.ShapeDtypeStruct((m, n), BF16),
                   jax.ShapeDtypeStruct((8, 128), F32)],
        compiler_params=_params("arbitrary"),
    )(a, w, residual, target)


def _mm_nt(a, w, *, out_dtype, name, after=None, tm=1024, tn=1408, tk=1792):
    m, k = a.shape
    nsh, r, c = w.shape
    assert k == nsh * c and a.dtype == BF16 and w.dtype == BF16
    tm, tn, tk = _pick(m, tm, 8), _pick(r, tn), _pick(c, tk)
    nk, cpk = k // tk, c // tk

    def body(a_ref, w_ref, o_ref, acc):
        prod = lax.dot_general(a_ref[...], w_ref[...], (((1,), (1,)), ((), ())), preferred_element_type=F32)
        if nk == 1:
            o_ref[...] = prod.astype(out_dtype)
        else:
            kk = pl.program_id(2)

            @pl.when(kk == 0)
            def _():
                acc[...] = prod

            @pl.when(kk > 0)
            def _():
                acc[...] += prod

            @pl.when(kk == nk - 1)
            def _():
                o_ref[...] = acc[...].astype(out_dtype)

    body, more_specs, more_args = _ordered(body, 2, after)
    return pl.pallas_call(
        body, name=name, grid=(m // tm, r // tn, nk),
        in_specs=[pl.BlockSpec((tm, tk), lambda i, j, kk: (i, kk)),
                  pl.BlockSpec((None, tn, tk), lambda i, j, kk: (kk // cpk, j, kk % cpk))] + more_specs,
        out_specs=pl.BlockSpec((tm, tn), lambda i, j, kk: (i, j)),
        out_shape=jax.ShapeDtypeStruct((m, r), out_dtype),
        scratch_shapes=[pltpu.VMEM((tm, tn) if nk > 1 else (8, 128), F32)],
        compiler_params=_params("parallel", "parallel", "arbitrary"),
    )(a, w, *more_args)


NORM_BWD_ROWS = 256


def _mm_nt_rmsnorm_bwd(a, w, x, g, dres, *, name, want_bf16, silu=False, after=None, tm=1024, tk=1792):
    m, k = a.shape
    nsh, r, c = w.shape
    assert k == nsh * c and a.dtype == BF16 and w.dtype == BF16 and x.shape == (m, r)
    tm, tk = _pick(m, tm, 8), _pick(c, tk)
    nk, cpk = k // tk, c // tk
    rows = _pick(tm, NORM_BWD_ROWS, 8)
    n_in = 4 if dres is None else 5

    def body(a_ref, w_ref, x_ref, g_ref, *rest):
        dres_ref = None if dres is None else rest[0]
        outs, acc = rest[n_in - 4:-1], rest[-1]
        dx_ref, dg_ref = outs[0], outs[-1]
        kk = pl.program_id(1)
        prod = lax.dot_general(a_ref[...], w_ref[...], (((1,), (1,)), ((), ())), preferred_element_type=F32)

        @pl.when(kk == 0)
        def _():
            acc[...] = prod

        @pl.when(kk > 0)
        def _():
            acc[...] += prod

        @pl.when(kk == nk - 1)
        def _():
            dg = jnp.zeros((1, r), F32)
            for r0 in range(0, tm, rows):
                part = slice(r0, r0 + rows)
                xv, dyv = x_ref[part, :], acc[part, :]
                inv = lax.rsqrt(jnp.mean(xv * xv, axis=-1, keepdims=True) + RMS_EPS)
                if silu:
                    y = xv * inv * g_ref[...]
                    sg = _sigmoid(y)
                    dyv = dyv * sg * (1.0 + y * (1.0 - sg))
                gy = dyv * g_ref[...]
                dx = inv * gy - xv * (inv * inv * inv) * jnp.mean(xv * gy, axis=-1, keepdims=True)
                if dres is not None:
                    dx = dx + dres_ref[part, :]
                dx_ref[part, :] = dx
                if want_bf16:
                    outs[1][part, :] = dx.astype(BF16)
                dg = dg + jnp.sum(dyv * xv * inv, axis=0, keepdims=True)
            _accumulate(dg_ref, dg, pl.program_id(0) == 0)

    whole = lambda: pl.BlockSpec((tm, r), lambda i, kk: (i, 0))
    vec = pl.BlockSpec((1, r), lambda i, kk: (0, 0))
    out_shape, out_specs = [jax.ShapeDtypeStruct((m, r), F32)], [whole()]
    if want_bf16:
        out_shape.append(jax.ShapeDtypeStruct((m, r), BF16))
        out_specs.append(whole())
    out_shape.append(jax.ShapeDtypeStruct((1, r), F32))
    out_specs.append(vec)
    body, more_specs, more_args = _ordered(body, n_in, after)
    residual_specs, residual_args = ([], []) if dres is None else ([whole()], [dres])
    return pl.pallas_call(
        body, name=name, grid=(m // tm, nk),
        in_specs=[pl.BlockSpec((tm, tk), lambda i, kk: (i, kk)),
                  pl.BlockSpec((None, r, tk), lambda i, kk: (kk // cpk, 0, kk % cpk)), whole(), vec]
        + residual_specs + more_specs,
        out_specs=out_specs, out_shape=out_shape,
        scratch_shapes=[pltpu.VMEM((tm, r), F32)],
        compiler_params=_params("arbitrary", "arbitrary"),
    )(a, w, x, g, *residual_args, *more_args)


MM_TN_VMEM_BYTES = 44 * 2 ** 20


def _mm_tn(a, b, *, n_shards, name, tm=1408, tn=1408):
    t, m = a.shape
    t2, n = b.shape
    assert t == t2 and a.dtype == BF16 and b.dtype == BF16
    c = n // n_shards
    tm, tn = _pick(m, tm), _pick(c, tn)
    if m // tm == 1 and n // tn == 1 and tn % (2 * LANES) == 0:
        tn //= 2
    fixed = 2 * tm * tn * 6
    if 4 * t * (tm + tn) + fixed <= MM_TN_VMEM_BYTES:
        tk = t
    else:
        tk = _pick(t, (MM_TN_VMEM_BYTES - fixed - 4 * tm * tn) // (4 * (tm + tn)), 8)
    nk, cpn = t // tk, c // tn

    def body(a_ref, b_ref, o_ref, ob_ref, acc):
        kk = pl.program_id(2)
        prod = lax.dot_general(a_ref[...], b_ref[...], (((0,), (0,)), ((), ())), preferred_element_type=F32)

        def finish(total):
            o_ref[...] = total
            ob_ref[...] = total.astype(BF16)

        if nk == 1:
            finish(prod)
        else:
            @pl.when(kk == 0)
            def _():
                acc[...] = prod

            @pl.when(kk > 0)
            def _():
                acc[...] += prod

            @pl.when(kk == nk - 1)
            def _():
                finish(acc[...])

    out_spec = pl.BlockSpec((None, tm, tn), lambda i, j, kk: (j // cpn, i, j % cpn))
    return pl.pallas_call(
        body, name=name, grid=(m // tm, n // tn, nk),
        in_specs=[pl.BlockSpec((tk, tm), lambda i, j, kk: (kk, i)),
                  pl.BlockSpec((tk, tn), lambda i, j, kk: (kk, j))],
        out_specs=[out_spec, out_spec],
        out_shape=[jax.ShapeDtypeStruct((n_shards, m, c), F32), jax.ShapeDtypeStruct((n_shards, m, c), BF16)],
        scratch_shapes=[pltpu.VMEM((tm, tn) if nk > 1 else (8, 128), F32)],
        compiler_params=_params("parallel", "parallel", "arbitrary"),
    )(a, b)


def _row_spec(tr, width, col=0):
    return pl.BlockSpec((tr, width), lambda i, col=col: (i, col))


def _vec_spec(width, col=0):
    return pl.BlockSpec((1, width), lambda i, col=col: (0, col))


def _accumulate(ref, value, first):
    @pl.when(first)
    def _():
        ref[...] = value

    @pl.when(jnp.logical_not(first))
    def _():
        ref[...] += value


def _rmsnorm_fwd(x, g, *, name, tr=512):
    t, d = x.shape
    tr = _pick(t, tr, 8)

    def body(x_ref, g_ref, o_ref):
        xv = x_ref[...]
        r = lax.rsqrt(jnp.mean(xv * xv, axis=-1, keepdims=True) + RMS_EPS)
        o_ref[...] = (xv * r * g_ref[...]).astype(BF16)

    return pl.pallas_call(
        body, name=name, grid=(t // tr,),
        in_specs=[_row_spec(tr, d), _vec_spec(d)], out_specs=_row_spec(tr, d),
        out_shape=jax.ShapeDtypeStruct((t, d), BF16), compiler_params=_params("parallel"),
    )(x, g)


CONV_ROWS = 16


def _seq_specs(dims, ts, width, halo, col, *, nxt=False):
    nst, per = dims.seq // ts, ts // halo
    last = dims.tokens // halo - 1
    cur = pl.BlockSpec((ts, width), lambda b, i: (b * nst + i, col))
    if nxt:
        edge = pl.BlockSpec((halo, width), lambda b, i: (jnp.minimum((b * nst + i + 1) * per, last), col))
    else:
        edge = pl.BlockSpec((halo, width), lambda b, i: (jnp.maximum((b * nst + i) * per - 1, 0), col))
    return cur, edge


SUBLANES = 8


def _shifted_copies(buf, shifted):
    rows = shifted.shape[1]
    for s in range(1, SUBLANES):
        shifted[s - 1] = buf[pl.ds(s, rows), :]


def _window(buf, shifted, start, size):
    a, s = divmod(start, SUBLANES)
    src = buf if s == 0 else shifted.at[s - 1]
    return src[pl.ds(SUBLANES * a, size), :]


def _conv_branch_fwd(z, w, b, g, dims, *, name, ts=128):
    t, c, kw = z.shape[0], dims.d_model, dims.conv_width
    base = CONV_HALO - (kw - 1)

    def body(av_ref, hv_ref, ag_ref, hg_ref, w_ref, b_ref, g_ref, a1_ref, a3_ref, buf, shifted):
        i = pl.program_id(1)
        buf[CONV_HALO:, :] = av_ref[...].astype(F32) * _sigmoid(ag_ref[...].astype(F32))
        buf[0:CONV_HALO, :] = jnp.where(i > 0, hv_ref[...].astype(F32) * _sigmoid(hg_ref[...].astype(F32)), 0.0)
        _shifted_copies(buf, shifted)
        for r0 in range(0, ts, CONV_ROWS):
            acc = jnp.broadcast_to(b_ref[...], (CONV_ROWS, c))
            for k in range(kw):
                acc = acc + w_ref[k:k + 1, :] * _window(buf, shifted, r0 + base + k, CONV_ROWS)
            a1_ref[r0:r0 + CONV_ROWS, :] = acc
            a2 = acc * lax.rsqrt(jnp.mean(acc * acc, axis=-1, keepdims=True) + RMS_EPS) * g_ref[...]
            a3_ref[r0:r0 + CONV_ROWS, :] = (a2 * _sigmoid(a2)).astype(BF16)

    vec = pl.BlockSpec((1, c), lambda b, i: (0, 0))
    out = pl.BlockSpec((ts, c), lambda b, i: (b * (dims.seq // ts) + i, 0))
    return pl.pallas_call(
        body, name=name, grid=(dims.batch_local, dims.seq // ts),
        in_specs=[*_seq_specs(dims, ts, c, CONV_HALO, 0), *_seq_specs(dims, ts, c, CONV_HALO, 1),
                  pl.BlockSpec((CONV_HALO, c), lambda b, i: (0, 0)), vec, vec],
        out_specs=[out, out],
        out_shape=[jax.ShapeDtypeStruct((t, c), F32), jax.ShapeDtypeStruct((t, c), BF16)],
        scratch_shapes=[pltpu.VMEM((CONV_HALO + ts, c), F32),
                        pltpu.VMEM((SUBLANES - 1, CONV_HALO + ts - SUBLANES, c), F32)],
        compiler_params=_params("parallel", "parallel"),
    )(z, z, z, z, w, b, g)


def _conv_branch_bwd(da1, z, w, rest_of_dz, dims, *, name, ts=128):
    t, c, kw = z.shape[0], dims.d_model, dims.conv_width
    nst = dims.seq // ts
    base = CONV_HALO - (kw - 1)
    n_rest = len(rest_of_dz)
    total = 2 * c + sum(r.shape[1] for r in rest_of_dz)

    def body(d_ref, dn_ref, av_ref, hv_ref, ag_ref, hg_ref, w_ref, *more):
        rest_refs = more[:n_rest]
        dz_ref, dw_ref, db_ref, abuf, dbuf, ashift, dshift = more[n_rest:]
        col = 2 * c
        for r in rest_refs:
            dz_ref[:, col:col + r.shape[1]] = r[...]
            col += r.shape[1]
        i = pl.program_id(1)
        first = jnp.logical_and(pl.program_id(0) == 0, i == 0)
        abuf[CONV_HALO:, :] = av_ref[...].astype(F32) * _sigmoid(ag_ref[...].astype(F32))
        abuf[0:CONV_HALO, :] = jnp.where(i > 0, hv_ref[...].astype(F32) * _sigmoid(hg_ref[...].astype(F32)), 0.0)
        d1 = d_ref[...]
        dbuf[0:ts, :] = d1
        dbuf[ts:, :] = jnp.where(i < nst - 1, dn_ref[...], 0.0)
        _shifted_copies(abuf, ashift)
        _shifted_copies(dbuf, dshift)

        @pl.when(first)
        def _():
            dw_ref[...] = jnp.zeros_like(dw_ref)
            db_ref[...] = jnp.zeros_like(db_ref)

        db_ref[...] += jnp.sum(d1, axis=0, keepdims=True)
        for k in range(kw):
            dw_ref[k:k + 1, :] += jnp.sum(d1 * _window(abuf, ashift, base + k, ts), axis=0, keepdims=True)
        for r0 in range(0, ts, CONV_ROWS):
            acc = jnp.zeros((CONV_ROWS, c), F32)
            for k in range(kw):
                acc = acc + w_ref[k:k + 1, :] * _window(dbuf, dshift, r0 + (kw - 1) - k, CONV_ROWS)
            av = av_ref[r0:r0 + CONV_ROWS, :].astype(F32)
            sg = _sigmoid(ag_ref[r0:r0 + CONV_ROWS, :].astype(F32))
            dz_ref[r0:r0 + CONV_ROWS, 0:c] = (acc * sg).astype(BF16)
            dz_ref[r0:r0 + CONV_ROWS, c:2 * c] = (acc * av * sg * (1.0 - sg)).astype(BF16)

    cur, nxt = _seq_specs(dims, ts, c, CONV_HALO, 0, nxt=True)
    return pl.pallas_call(
        body, name=name, grid=(dims.batch_local, nst),
        in_specs=[cur, nxt, *_seq_specs(dims, ts, c, CONV_HALO, 0), *_seq_specs(dims, ts, c, CONV_HALO, 1),
                  pl.BlockSpec((CONV_HALO, c), lambda b, i: (0, 0))]
        + [pl.BlockSpec((ts, r.shape[1]), lambda b, i: (b * nst + i, 0)) for r in rest_of_dz],
        out_specs=[pl.BlockSpec((ts, total), lambda b, i: (b * nst + i, 0)),
                   pl.BlockSpec((CONV_HALO, c), lambda b, i: (0, 0)), pl.BlockSpec((1, c), lambda b, i: (0, 0))],
        out_shape=[jax.ShapeDtypeStruct((t, total), BF16), jax.ShapeDtypeStruct((CONV_HALO, c), F32),
                   jax.ShapeDtypeStruct((1, c), F32)],
        scratch_shapes=[pltpu.VMEM((CONV_HALO + ts, c), F32)] * 2
        + [pltpu.VMEM((SUBLANES - 1, CONV_HALO + ts - SUBLANES, c), F32)] * 2,
        compiler_params=_params("arbitrary", "arbitrary"),
    )(da1, da1, z, z, z, z, w, *rest_of_dz)


FFN_ROWS = 16
FFN_COLS = 256


def _ffn_chunks(ts, f):
    cw = _pick(f, FFN_COLS)
    return [(r0, c0, cw) for r0 in range(0, ts, FFN_ROWS) for c0 in range(0, f, cw)]


def _tap_sources(buf, moved, offsets, rows):
    taps, used = [], 0
    for off in offsets:
        if off % SUBLANES:
            moved[used] = buf[pl.ds(off, rows), :]
            taps.append((moved.at[used], 0))
            used += 1
        else:
            taps.append((buf, off))
    return taps


def _moved_copies(offsets):
    return sum(1 for off in offsets if off % SUBLANES)


def _taps_sum(taps, w_ref, init, r0, cols):
    for k, (src, off) in enumerate(taps):
        init = init + w_ref[k:k + 1, cols] * src[pl.ds(off + r0, init.shape[0]), cols]
    return init


def _ffn_bwd(dact, up, w, b, dims, *, name, ts=128):
    t, f, kw = up.shape[0], dims.d_ff, dims.ffn_conv_width
    nst = dims.seq // ts
    fwd_offsets = [FFN_HALO - (kw - 1) + k for k in range(kw)]
    bwd_offsets = [(kw - 1) - k for k in range(kw)]
    dact_halo = 2 * FFN_HALO

    def body(d_ref, dn_ref, up_ref, hp_ref, hn_ref, w_ref, b_ref, o_ref, dw_ref, db_ref, buf, moved, dbuf, dmoved):
        i = pl.program_id(1)
        first = jnp.logical_and(pl.program_id(0) == 0, i == 0)
        more = i < nst - 1
        buf[0:FFN_HALO, :] = jnp.where(i > 0, hp_ref[...], 0.0)
        buf[FFN_HALO:FFN_HALO + ts, :] = up_ref[...]
        buf[FFN_HALO + ts:, :] = hn_ref[...]
        taps = _tap_sources(buf, moved, fwd_offsets, ts + FFN_HALO)

        def du_chunk(r0, rows, c0, cw, d):
            vcols, gcols = slice(c0, c0 + cw), slice(f + c0, f + c0 + cw)
            uv = _taps_sum(taps, w_ref, jnp.broadcast_to(b_ref[:, vcols], (rows, cw)), r0, vcols)
            ug = _taps_sum(taps, w_ref, jnp.broadcast_to(b_ref[:, gcols], (rows, cw)), r0, gcols)
            sg = _sigmoid(ug)
            dbuf[r0:r0 + rows, vcols] = d * ug * sg
            dbuf[r0:r0 + rows, gcols] = d * uv * sg * (1.0 + ug * (1.0 - sg))

        for r0, c0, cw in _ffn_chunks(ts, f):
            du_chunk(r0, FFN_ROWS, c0, cw, d_ref[r0:r0 + FFN_ROWS, c0:c0 + cw].astype(F32))
        for _, c0, cw in _ffn_chunks(FFN_ROWS, f):
            d_next = dn_ref[:, c0:c0 + cw].astype(F32)[0:FFN_HALO]
            du_chunk(ts, FFN_HALO, c0, cw, jnp.where(more, d_next, 0.0))

        @pl.when(first)
        def _():
            dw_ref[...] = jnp.zeros_like(dw_ref)
            db_ref[...] = jnp.zeros_like(db_ref)

        du = dbuf[0:ts, :]
        db_ref[...] += jnp.sum(du, axis=0, keepdims=True)
        for k, (src, off) in enumerate(taps):
            dw_ref[k:k + 1, :] += jnp.sum(du * src[pl.ds(off, ts), :], axis=0, keepdims=True)

        dtaps = _tap_sources(dbuf, dmoved, bwd_offsets, ts)
        for r0, c0, cw in _ffn_chunks(ts, 2 * f):
            cols = slice(c0, c0 + cw)
            o_ref[r0:r0 + FFN_ROWS, cols] = _taps_sum(dtaps, w_ref, jnp.zeros((FFN_ROWS, cw), F32), r0, cols).astype(BF16)

    up_cur, up_prev = _seq_specs(dims, ts, 2 * f, FFN_HALO, 0)
    _, up_next = _seq_specs(dims, ts, 2 * f, FFN_HALO, 0, nxt=True)
    d_cur, d_next = _seq_specs(dims, ts, f, dact_halo, 0, nxt=True)
    full = lambda rows: pl.BlockSpec((rows, 2 * f), lambda b_, i: (0, 0))
    return pl.pallas_call(
        body, name=name, grid=(dims.batch_local, nst),
        in_specs=[d_cur, d_next, up_cur, up_prev, up_next, full(FFN_HALO), full(1)],
        out_specs=[pl.BlockSpec((ts, 2 * f), lambda b_, i: (b_ * nst + i, 0)), full(FFN_HALO), full(1)],
        out_shape=[jax.ShapeDtypeStruct((t, 2 * f), BF16), jax.ShapeDtypeStruct((FFN_HALO, 2 * f), F32),
                   jax.ShapeDtypeStruct((1, 2 * f), F32)],
        scratch_shapes=[pltpu.VMEM((ts + 2 * FFN_HALO, 2 * f), F32),
                        pltpu.VMEM((_moved_copies(fwd_offsets), ts + FFN_HALO, 2 * f), F32),
                        pltpu.VMEM((ts + FFN_HALO, 2 * f), F32),
                        pltpu.VMEM((_moved_copies(bwd_offsets), ts, 2 * f), F32)],
        compiler_params=_params("arbitrary", "arbitrary"),
    )(dact, dact, up, up, up, w, b)


def _ffn_act_fwd(up, w, b, dims, *, name, ts=128):
    t, f, kw = up.shape[0], dims.d_ff, dims.ffn_conv_width
    offsets = [FFN_HALO - (kw - 1) + k for k in range(kw)]

    def body(up_ref, h_ref, w_ref, b_ref, o_ref, buf, moved):
        buf[FFN_HALO:, :] = up_ref[...]
        buf[0:FFN_HALO, :] = jnp.where(pl.program_id(1) > 0, h_ref[...], 0.0)
        taps = _tap_sources(buf, moved, offsets, ts)
        for r0, c0, cw in _ffn_chunks(ts, f):
            vcols, gcols = slice(c0, c0 + cw), slice(f + c0, f + c0 + cw)
            uv = _taps_sum(taps, w_ref, jnp.broadcast_to(b_ref[:, vcols], (FFN_ROWS, cw)), r0, vcols)
            ug = _taps_sum(taps, w_ref, jnp.broadcast_to(b_ref[:, gcols], (FFN_ROWS, cw)), r0, gcols)
            o_ref[r0:r0 + FFN_ROWS, vcols] = (ug * _sigmoid(ug) * uv).astype(BF16)

    full = lambda rows: pl.BlockSpec((rows, 2 * f), lambda b_, i: (0, 0))
    return pl.pallas_call(
        body, name=name, grid=(dims.batch_local, dims.seq // ts),
        in_specs=[*_seq_specs(dims, ts, 2 * f, FFN_HALO, 0), full(FFN_HALO), full(1)],
        out_specs=pl.BlockSpec((ts, f), lambda b_, i: (b_ * (dims.seq // ts) + i, 0)),
        out_shape=jax.ShapeDtypeStruct((t, f), BF16),
        scratch_shapes=[pltpu.VMEM((FFN_HALO + ts, 2 * f), F32), pltpu.VMEM((_moved_copies(offsets), ts, 2 * f), F32)],
        compiler_params=_params("parallel", "parallel"),
    )(up, up, w, b)


def _dot_nt(a, b):
    return lax.dot_general(a, b, (((1,), (1,)), ((), ())), preferred_element_type=F32)


def _dot_tn(a, b):
    return lax.dot_general(a, b, (((0,), (0,)), ((), ())), preferred_element_type=F32)


LANES = 128
MASK_BIAS = 1e30
RESIDUE_DILATIONS = tuple(d for d in DILATIONS if d > 1)


def _rows_to_residues(value, out_ref, scr, d):
    rows, width = value.shape
    for c in range(width // LANES):
        cols = slice(LANES * c, LANES * (c + 1))
        scr[c] = value[:, cols]
        for r in range(d):
            out_ref[r, :, cols] = scr[c, pl.ds(r, rows // d, stride=d), :].astype(out_ref.dtype)


def _residues_to_rows(in_ref, scr, d):
    _, n, width = in_ref.shape
    slabs = []
    for c in range(width // LANES):
        cols = slice(LANES * c, LANES * (c + 1))
        for r in range(d):
            scr[c, pl.ds(r, n, stride=d), :] = in_ref[r, :, cols].astype(F32)
        slabs.append(scr[c])
    return slabs[0] if len(slabs) == 1 else jnp.concatenate(slabs, axis=1)


def _residue_shape(dims, d, width, dtype):
    return jax.ShapeDtypeStruct((dims.batch_local, d, dims.seq // d, width), dtype)


def _residue_spec(dims, d, tr, width):
    tiles = dims.seq // tr
    return pl.BlockSpec((None, d, tr // d, width), lambda i: (i // tiles, 0, i % tiles, 0))


def _head_sum_matrix(dims):
    a = dims.n_heads * dims.head_dim
    head = jnp.arange(a, dtype=jnp.int32) // dims.head_dim
    return (head[:, None] == jnp.arange(LANES, dtype=jnp.int32)[None, :]).astype(BF16)


def _two_pass_dot(v, m):
    hi = v.astype(BF16)
    lo = (v - hi.astype(F32)).astype(BF16)
    return jnp.dot(hi, m, preferred_element_type=F32) + jnp.dot(lo, m, preferred_element_type=F32)


def _residue_permutations(tr):
    out = []
    for d in RESIDUE_DILATIONS:
        dst = jnp.arange(tr, dtype=jnp.int32)
        src = d * (dst % (tr // d)) + dst // (tr // d)
        out.append((src[:, None] == jnp.arange(tr, dtype=jnp.int32)[None, :]).astype(BF16))
    return out


def _bf16_rows_to_residues(value, out_ref, perm_ref, d):
    n = value.shape[0] // d
    moved = jnp.dot(perm_ref[...], value, preferred_element_type=F32).astype(out_ref.dtype)
    for r in range(d):
        out_ref[r] = moved[r * n:(r + 1) * n]


def _bf16_residues_to_rows(in_ref, back_ref):
    d = in_ref.shape[0]
    stacked = jnp.concatenate([in_ref[r] for r in range(d)], axis=0)
    return jnp.dot(back_ref[...], stacked, preferred_element_type=F32)


def _qkv_layouts_fwd(z, gq, gk, head_ones, dims, *, name, tr=256):
    t = z.shape[0]
    a = dims.n_heads * dims.head_dim
    q_scale = dims.head_dim ** -0.5
    nres = len(RESIDUE_DILATIONS)

    def body(q_ref, k_ref, v_ref, gq_ref, gk_ref, sum_ref, spread_ref, *rest):
        perm_refs, outs = rest[:nres], rest[nres:]
        qv, kv = q_ref[...].astype(F32), k_ref[...].astype(F32)
        mean = lambda val: _two_pass_dot(_two_pass_dot(val, sum_ref[...]), spread_ref[...]) * (1.0 / dims.head_dim)
        rq = lax.rsqrt(mean(qv * qv) + RMS_EPS)
        rk = lax.rsqrt(mean(kv * kv) + RMS_EPS)
        values = ((qv * rq * gq_ref[...] * q_scale).astype(BF16), (kv * rk * gk_ref[...]).astype(BF16), v_ref[...])
        for j, val in enumerate(values):
            outs[j][...] = val
            for g, d in enumerate(RESIDUE_DILATIONS):
                _bf16_rows_to_residues(val, outs[3 * (g + 1) + j], perm_refs[g], d)

    out_specs = [_row_spec(tr, a)] * 3
    out_shape = [jax.ShapeDtypeStruct((t, a), BF16)] * 3
    for d in RESIDUE_DILATIONS:
        out_specs += [_residue_spec(dims, d, tr, a)] * 3
        out_shape += [_residue_shape(dims, d, a, BF16)] * 3
    outs = pl.pallas_call(
        body, name=name, grid=(t // tr,),
        in_specs=[_row_spec(tr, a, 2), _row_spec(tr, a, 3), _row_spec(tr, a, 4), _vec_spec(a), _vec_spec(a),
                  pl.BlockSpec((a, LANES), lambda i: (0, 0)), pl.BlockSpec((LANES, a), lambda i: (0, 0))]
        + [pl.BlockSpec((tr, tr), lambda i: (0, 0))] * nres,
        out_specs=out_specs, out_shape=out_shape,
        compiler_params=_params("parallel"),
    )(z, z, z, gq, gk, *head_ones, *_residue_permutations(tr))
    return {d: tuple(outs[3 * g:3 * g + 3]) for g, d in enumerate((1,) + RESIDUE_DILATIONS)}


ATTN_RESIDUES_PER_STEP = 4
ATTN_RESIDUES_PER_STEP_WINDOWED = 2


def _attn_groups(dims, dil):
    return (dims.batch_local, dil) if dil > 1 else (1, dims.batch_local)


def _attn_array(x, dims, dil):
    return x if dil > 1 else x.reshape(1, dims.batch_local, dims.seq, x.shape[-1])


def _attn_residues(dims, dil):
    one_block = dims.seq // dil == ATTN_BLOCK
    return math.gcd(_attn_groups(dims, dil)[1], ATTN_RESIDUES_PER_STEP if one_block else ATTN_RESIDUES_PER_STEP_WINDOWED)


def _per_residue(body, rs):
    if rs == 1:
        return body

    def stepped(*refs):
        for r in range(rs):
            body(*[ref.at[r] for ref in refs])

    return stepped


def _attn_specs(dims, dil, width):
    blk = ATTN_BLOCK
    nb = dims.seq // dil // blk
    rs = _attn_residues(dims, dil)
    lead, groups = _attn_groups(dims, dil)
    if rs > 1 and nb == 1:
        grid = (lead, groups // rs)
        at = lambda f: pl.BlockSpec((None, rs, blk, width), lambda b, r: (b, r, 0, 0))
    elif rs > 1:
        grid = (lead, groups // rs, nb)
        at = lambda f: pl.BlockSpec((None, rs, blk, width), lambda b, r, i: (b, r, f(i), 0))
    else:
        grid = (lead, groups, nb)
        at = lambda f: pl.BlockSpec((None, None, blk, width), lambda b, r, i: (b, r, f(i), 0))
    return grid, at(lambda i: i), at(lambda i: jnp.maximum(i - 1, 0)), at(lambda i: jnp.minimum(i + 1, nb - 1))


def _head_slopes(n_heads):
    h = lax.broadcasted_iota(jnp.int32, (n_heads, 1, 1), 0).astype(F32)
    return jnp.exp((h + 1.0) * (-8.0 / n_heads * math.log(2.0)))


def _pair_masks(hd):
    low = lax.broadcasted_iota(jnp.int32, (1, 2 * hd), 1) < hd
    return low, jnp.logical_not(low)


def _attn_fwd(q, k, v, dims, dil, *, name):
    a = dims.n_heads * dims.head_dim
    heads, hd, blk = dims.n_heads, dims.head_dim, ATTN_BLOCK
    assert 2 * hd == LANES and heads % 2 == 0 and heads <= LANES
    nb = dims.seq // dil // blk
    has_prev = nb > 1
    nkeys = 2 * blk if has_prev else blk
    grid, cur, prev, _ = _attn_specs(dims, dil, a)
    _, cur_stat, _, _ = _attn_specs(dims, dil, LANES)

    def body(*refs):
        if has_prev:
            q_ref, kc_ref, vc_ref, kp_ref, vp_ref, o_ref, lse_ref, s_scr, p_scr, k_st, v_st = refs
            k_st[0:blk, :], k_st[blk:, :] = kp_ref[...], kc_ref[...]
            v_st[0:blk, :], v_st[blk:, :] = vp_ref[...], vc_ref[...]
        else:
            q_ref, k_st, v_st, o_ref, lse_ref, s_scr, p_scr = refs
        low, high = _pair_masks(hd)

        for hp in range(heads // 2):
            sl = slice(LANES * hp, LANES * (hp + 1))
            q2 = q_ref[:, sl]
            kcat = k_st[:, sl]
            s_scr[2 * hp] = _dot_nt(jnp.where(low, q2, jnp.zeros_like(q2)), kcat)
            s_scr[2 * hp + 1] = _dot_nt(jnp.where(high, q2, jnp.zeros_like(q2)), kcat)

        iq = lax.broadcasted_iota(jnp.int32, (blk, nkeys), 0)
        jk = lax.broadcasted_iota(jnp.int32, (blk, nkeys), 1)
        if has_prev:
            steps = iq + blk - jk
            valid = (steps >= 0) & (steps <= blk) & ((jk >= blk) | (pl.program_id(len(grid) - 1) > 0))
        else:
            steps = iq - jk
            valid = steps >= 0
        bias = jnp.where(valid, steps.astype(F32) * (-float(dil)), -MASK_BIAS)
        s = s_scr[...] + _head_slopes(heads) * bias[None]
        m = jnp.max(s, axis=-1, keepdims=True)
        p = jnp.exp(s - m)
        l = jnp.sum(p, axis=-1, keepdims=True)
        p_scr[...] = p.astype(BF16)
        inv = 1.0 / l
        lse = m + jnp.log(l)

        lane = lax.broadcasted_iota(jnp.int32, (blk, LANES), 1)
        stat = jnp.zeros((blk, LANES), F32)
        for hp in range(heads // 2):
            sl = slice(LANES * hp, LANES * (hp + 1))
            vcat = v_st[:, sl]
            pv_a = jnp.dot(p_scr[2 * hp], vcat, preferred_element_type=F32) * inv[2 * hp]
            pv_b = jnp.dot(p_scr[2 * hp + 1], vcat, preferred_element_type=F32) * inv[2 * hp + 1]
            o_ref[:, sl] = jnp.where(low, pv_a, pv_b).astype(BF16)
            stat = jnp.where(lane == 2 * hp, lse[2 * hp], stat)
            stat = jnp.where(lane == 2 * hp + 1, lse[2 * hp + 1], stat)
        lse_ref[...] = stat

    q4, k4, v4 = (_attn_array(x, dims, dil) for x in (q, k, v))
    rs = _attn_residues(dims, dil)
    per_step = lambda shape: shape if rs == 1 else (rs,) + shape
    o, lse = pl.pallas_call(
        _per_residue(body, rs), name=name, grid=grid,
        in_specs=[cur, cur, cur] + ([prev, prev] if has_prev else []),
        out_specs=[cur, cur_stat],
        out_shape=[jax.ShapeDtypeStruct(q4.shape, BF16), jax.ShapeDtypeStruct(q4.shape[:-1] + (LANES,), F32)],
        scratch_shapes=[pltpu.VMEM(per_step((heads, blk, nkeys)), F32), pltpu.VMEM(per_step((heads, blk, nkeys)), BF16)]
        + ([pltpu.VMEM(per_step((nkeys, a)), BF16)] * 2 if has_prev else []),
        compiler_params=_params(*["parallel"] * len(grid)),
    )(q4, k4, v4, *([k4, v4] if has_prev else []))
    return o.reshape(q.shape), lse.reshape(q.shape[:-1] + (LANES,))


def _attn_combine(groups, head_spread, dims, *, name, tr=256):
    t = dims.tokens
    a = dims.n_heads * dims.head_dim
    dils = tuple(groups)

    nres = len(RESIDUE_DILATIONS)

    def body(*refs):
        ins = refs[:2 * len(dils)]
        x_ref = refs[2 * len(dils)]
        back_refs = dict(zip(RESIDUE_DILATIONS, refs[2 * len(dils) + 1:2 * len(dils) + 1 + nres]))
        o_ref = refs[2 * len(dils) + 1 + nres]
        lse_refs = refs[2 * len(dils) + 2 + nres:-1]
        scr_stat = refs[-1]
        outs, stats = [], []
        for g, d in enumerate(dils):
            if d == 1:
                outs.append(ins[2 * g][...].astype(F32))
                stats.append(ins[2 * g + 1][...])
            else:
                outs.append(_bf16_residues_to_rows(ins[2 * g], back_refs[d]))
                stats.append(_residues_to_rows(ins[2 * g + 1], scr_stat, d))
        top = functools.reduce(jnp.maximum, stats)
        weights = [jnp.exp(s - top) for s in stats]
        total = functools.reduce(jnp.add, weights)
        joint = top + jnp.log(total)
        inv = 1.0 / total
        acc = None
        for w, o in zip(weights, outs):
            term = _two_pass_dot(w * inv, x_ref[...]) * o
            acc = term if acc is None else acc + term
        o_ref[...] = acc.astype(BF16)
        for g, d in enumerate(dils):
            if d == 1:
                lse_refs[g][...] = joint
            else:
                _rows_to_residues(joint, lse_refs[g], scr_stat, d)

    in_specs, args, lse_specs, lse_shapes = [], [], [], []
    for d in dils:
        if d == 1:
            in_specs += [_row_spec(tr, a), _row_spec(tr, LANES)]
            lse_specs.append(_row_spec(tr, LANES))
            lse_shapes.append(jax.ShapeDtypeStruct((t, LANES), F32))
        else:
            in_specs += [_residue_spec(dims, d, tr, a), _residue_spec(dims, d, tr, LANES)]
            lse_specs.append(_residue_spec(dims, d, tr, LANES))
            lse_shapes.append(_residue_shape(dims, d, LANES, F32))
        args += list(groups[d])
    outs = pl.pallas_call(
        body, name=name, grid=(t // tr,),
        in_specs=in_specs + [pl.BlockSpec((LANES, a), lambda i: (0, 0))] + [pl.BlockSpec((tr, tr), lambda i: (0, 0))] * nres,
        out_specs=[_row_spec(tr, a)] + lse_specs,
        out_shape=[jax.ShapeDtypeStruct((t, a), BF16)] + lse_shapes,
        scratch_shapes=[pltpu.VMEM((1, tr, LANES), F32)],
        compiler_params=_params("parallel"),
    )(*args, head_spread, *[jnp.transpose(p) for p in _residue_permutations(tr)])
    return outs[0], dict(zip(dils, outs[1:]))


def _attn_bwd_prep(dy, w, o, head_sum, dims, *, name, after=None, tr=256):
    t, a = o.shape
    nres = len(RESIDUE_DILATIONS)

    def body(dy_ref, w_ref, o_ref, e_ref, *rest):
        perm_refs, outs, scr_stat = rest[:nres], rest[nres:-1], rest[-1]
        do = _dot_nt(dy_ref[...], w_ref[...]).astype(BF16)
        outs[0][...] = do
        delta = _two_pass_dot(do.astype(F32) * o_ref[...].astype(F32), e_ref[...])
        outs[1][...] = delta
        for g, d in enumerate(RESIDUE_DILATIONS):
            _bf16_rows_to_residues(do, outs[2 + 2 * g], perm_refs[g], d)
            _rows_to_residues(delta, outs[3 + 2 * g], scr_stat, d)

    out_specs = [_row_spec(tr, a), _row_spec(tr, LANES)]
    out_shape = [jax.ShapeDtypeStruct((t, a), BF16), jax.ShapeDtypeStruct((t, LANES), F32)]
    for d in RESIDUE_DILATIONS:
        out_specs += [_residue_spec(dims, d, tr, a), _residue_spec(dims, d, tr, LANES)]
        out_shape += [_residue_shape(dims, d, a, BF16), _residue_shape(dims, d, LANES, F32)]
    n_in = 4 + nres
    body, more_specs, more_args = _ordered(body, n_in, after)
    outs = pl.pallas_call(
        body, name=name, grid=(t // tr,),
        in_specs=[_row_spec(tr, dy.shape[1]), pl.BlockSpec((None,) + w.shape[1:], lambda i: (0, 0, 0)), _row_spec(tr, a),
                  pl.BlockSpec((a, LANES), lambda i: (0, 0))] + [pl.BlockSpec((tr, tr), lambda i: (0, 0))] * nres + more_specs,
        out_specs=out_specs, out_shape=out_shape,
        scratch_shapes=[pltpu.VMEM((1, tr, LANES), F32)],
        compiler_params=_params("parallel"),
    )(dy, w, o, head_sum, *_residue_permutations(tr), *more_args)
    dos, deltas = {1: outs[0]}, {1: outs[1]}
    for g, d in enumerate(RESIDUE_DILATIONS):
        dos[d], deltas[d] = outs[2 + 2 * g], outs[3 + 2 * g]
    return dos, deltas


def _attn_bwd(q, k, v, do, lse, delta, dims, dil, *, name):
    a = dims.n_heads * dims.head_dim
    heads, hd, blk = dims.n_heads, dims.head_dim, ATTN_BLOCK
    nb = dims.seq // dil // blk
    has_next = nb > 1
    nq = 2 * blk if has_next else blk
    grid, cur, _, nxt = _attn_specs(dims, dil, a)
    _, cur_stat, _, nxt_stat = _attn_specs(dims, dil, LANES)

    def body(*refs):
        k_ref, v_ref, q_ref, do_ref, lse_ref, dl_ref = refs[:6]
        if has_next:
            qn_ref, don_ref, lsen_ref, dln_ref = refs[6:10]
            dq_ref, dk_ref, dv_ref, q_st, do_st, s_scr, dp_scr, p_scr, ds_scr, carry = refs[10:]
        else:
            dq_ref, dk_ref, dv_ref, q_st, do_st, s_scr, dp_scr, p_scr, ds_scr = refs[6:]
        j = pl.program_id(len(grid) - 1)
        low, high = _pair_masks(hd)
        q_st[0:blk, :] = q_ref[...]
        do_st[0:blk, :] = do_ref[...]
        if has_next:
            q_st[blk:, :] = qn_ref[...]
            do_st[blk:, :] = don_ref[...]
            lse_all = jnp.concatenate([lse_ref[...], lsen_ref[...]], axis=0)
            dl_all = jnp.concatenate([dl_ref[...], dln_ref[...]], axis=0)
        else:
            lse_all, dl_all = lse_ref[...], dl_ref[...]
        lse_t, dl_t = jnp.transpose(lse_all), jnp.transpose(dl_all)
        lse3 = jnp.stack([lse_t[h:h + 1, :] for h in range(heads)])
        dl3 = jnp.stack([dl_t[h:h + 1, :] for h in range(heads)])

        def halves(x):
            return jnp.where(low, x, jnp.zeros_like(x)), jnp.where(high, x, jnp.zeros_like(x))

        for hp in range(heads // 2):
            sl = slice(LANES * hp, LANES * (hp + 1))
            k2, v2 = k_ref[:, sl], v_ref[:, sl]
            q_a, q_b = halves(q_st[:, sl])
            do_a, do_b = halves(do_st[:, sl])
            s_scr[2 * hp], s_scr[2 * hp + 1] = _dot_nt(k2, q_a), _dot_nt(k2, q_b)
            dp_scr[2 * hp], dp_scr[2 * hp + 1] = _dot_nt(v2, do_a), _dot_nt(v2, do_b)

        jk = lax.broadcasted_iota(jnp.int32, (blk, nq), 0)
        rq = lax.broadcasted_iota(jnp.int32, (blk, nq), 1)
        if has_next:
            iq = jnp.where(rq < blk, rq, rq - blk)
            steps = jnp.where(rq < blk, iq - jk, iq - jk + blk)
            valid = ((rq < blk) & (iq >= jk)) | ((rq >= blk) & (jk >= iq) & (j + 1 < nb))
        else:
            steps, valid = rq - jk, rq >= jk
        bias = jnp.where(valid, steps.astype(F32) * (-float(dil)), -MASK_BIAS)
        p = jnp.exp(s_scr[...] + _head_slopes(heads) * bias[None] - lse3)
        p_scr[...] = p.astype(BF16)
        ds_scr[...] = (p * (dp_scr[...] - dl3)).astype(BF16)

        if has_next:
            @pl.when(j == 0)
            def _():
                carry[...] = jnp.zeros_like(carry)

        for hp in range(heads // 2):
            sl = slice(LANES * hp, LANES * (hp + 1))
            k2 = k_ref[:, sl]
            q_a, q_b = halves(q_st[:, sl])
            do_a, do_b = halves(do_st[:, sl])
            ds_a, ds_b = ds_scr[2 * hp], ds_scr[2 * hp + 1]
            dk_ref[:, sl] = (jnp.dot(ds_a, q_a, preferred_element_type=F32)
                             + jnp.dot(ds_b, q_b, preferred_element_type=F32)).astype(BF16)
            dv_ref[:, sl] = (jnp.dot(p_scr[2 * hp], do_a, preferred_element_type=F32)
                             + jnp.dot(p_scr[2 * hp + 1], do_b, preferred_element_type=F32)).astype(BF16)
            dq2 = jnp.where(low, _dot_tn(ds_a, k2), _dot_tn(ds_b, k2))
            if has_next:
                dq_ref[:, sl] = (carry[:, sl] + dq2[:blk]).astype(BF16)
                carry[:, sl] = dq2[blk:]
            else:
                dq_ref[:, sl] = dq2.astype(BF16)

    args, in_specs = [_attn_array(x, dims, dil) for x in (k, v, q, do, lse, delta)], [cur] * 4 + [cur_stat] * 2
    if has_next:
        args += [args[2], args[3], args[4], args[5]]
        in_specs += [nxt] * 2 + [nxt_stat] * 2
    shape = jax.ShapeDtypeStruct(args[2].shape, BF16)
    rs = _attn_residues(dims, dil)
    per_step = lambda dims_: dims_ if rs == 1 else (rs,) + dims_
    scratch = ([pltpu.VMEM(per_step((nq, a)), BF16)] * 2 + [pltpu.VMEM(per_step((heads, blk, nq)), F32)] * 2
               + [pltpu.VMEM(per_step((heads, blk, nq)), BF16)] * 2)
    if has_next:
        scratch.append(pltpu.VMEM(per_step((blk, a)), F32))
    grads = pl.pallas_call(
        _per_residue(body, rs), name=name, grid=grid, in_specs=in_specs, out_specs=[cur] * 3, out_shape=[shape] * 3,
        scratch_shapes=scratch,
        compiler_params=_params(*["parallel"] * (len(grid) - 1), "arbitrary"),
    )(*args)
    return tuple(g.reshape(q.shape) for g in grads)


def _qkv_layouts_bwd(z, grads, gq, gk, head_ones, dims, *, name, tr=256):
    t = z.shape[0]
    a = dims.n_heads * dims.head_dim
    q_scale = dims.head_dim ** -0.5
    dils = tuple(grads)
    nres = len(RESIDUE_DILATIONS)

    def body(q_ref, k_ref, *rest):
        d_refs = rest[:3 * len(dils)]
        gq_ref, gk_ref, sum_ref, spread_ref = rest[3 * len(dils):3 * len(dils) + 4]
        back_refs = dict(zip(RESIDUE_DILATIONS, rest[3 * len(dils) + 4:3 * len(dils) + 4 + nres]))
        dz_ref, dgq_ref, dgk_ref = rest[3 * len(dils) + 4 + nres:]
        first = pl.program_id(0) == 0
        mean = lambda val: _two_pass_dot(_two_pass_dot(val, sum_ref[...]), spread_ref[...]) * (1.0 / dims.head_dim)

        def total(j):
            acc = None
            for g, d in enumerate(dils):
                ref = d_refs[3 * g + j]
                part = ref[...].astype(F32) if d == 1 else _bf16_residues_to_rows(ref, back_refs[d])
                acc = part if acc is None else acc + part
            return acc

        def norm_bwd(x_ref, dy, g_ref, scale, col, dg_ref):
            xv = x_ref[...].astype(F32)
            dy = dy * scale
            r = lax.rsqrt(mean(xv * xv) + RMS_EPS)
            gy = dy * g_ref[...]
            dx = r * gy - xv * (r * r * r) * mean(xv * gy)
            dz_ref[:, col * a:(col + 1) * a] = dx.astype(BF16)
            _accumulate(dg_ref, jnp.sum(dy * xv * r, axis=0, keepdims=True), first)

        norm_bwd(q_ref, total(0), gq_ref, q_scale, 0, dgq_ref)
        norm_bwd(k_ref, total(1), gk_ref, 1.0, 1, dgk_ref)
        dz_ref[:, 2 * a:3 * a] = total(2).astype(BF16)

    in_specs, args = [_row_spec(tr, a, 2), _row_spec(tr, a, 3)], [z, z]
    for d in dils:
        in_specs += [_row_spec(tr, a) if d == 1 else _residue_spec(dims, d, tr, a)] * 3
        args += list(grads[d])
    in_specs += [_vec_spec(a), _vec_spec(a), pl.BlockSpec((a, LANES), lambda i: (0, 0)),
                 pl.BlockSpec((LANES, a), lambda i: (0, 0))] + [pl.BlockSpec((tr, tr), lambda i: (0, 0))] * nres
    return pl.pallas_call(
        body, name=name, grid=(t // tr,), in_specs=in_specs,
        out_specs=[_row_spec(tr, 3 * a), _vec_spec(a), _vec_spec(a)],
        out_shape=[jax.ShapeDtypeStruct((t, 3 * a), BF16)] + [jax.ShapeDtypeStruct((1, a), F32)] * 2,
        compiler_params=_params("arbitrary"),
    )(*args, gq, gk, *head_ones, *[jnp.transpose(p) for p in _residue_permutations(tr)])


def _mix_fwd(a3, o, w_a, w_b, w_out, z, gate_b, x, g2, dims, *, name, tr=512):
    t, d = x.shape
    tr = _pick(t, tr, 8)
    first_gate_col = z.shape[1] // d - 2

    def body(a_ref, o_ref, wa_ref, wb_ref, wo_ref, ga_ref, gb_ref, ba_ref, bb_ref, x_ref, g2_ref,
             ya_ref, yb_ref, mix_ref, x1_ref, h2_ref):
        ya = jnp.dot(a_ref[...], wa_ref[...], preferred_element_type=F32)
        yb = jnp.dot(o_ref[...], wb_ref[...], preferred_element_type=F32)
        ya_ref[...] = ya
        yb_ref[...] = yb
        g_a = _sigmoid(ga_ref[...].astype(F32) + ba_ref[...])
        g_b = _sigmoid(gb_ref[...].astype(F32) + bb_ref[...])
        mixed = (g_a * ya + g_b * yb).astype(BF16)
        mix_ref[...] = mixed
        x1 = x_ref[...] + jnp.dot(mixed, wo_ref[...], preferred_element_type=F32)
        x1_ref[...] = x1
        h2_ref[...] = (x1 * lax.rsqrt(jnp.mean(x1 * x1, axis=-1, keepdims=True) + RMS_EPS) * g2_ref[...]).astype(BF16)

    weight = pl.BlockSpec((None, d, d), lambda i: (0, 0, 0))
    rows = _row_spec(tr, d)
    return pl.pallas_call(
        body, name=name, grid=(t // tr,),
        in_specs=[rows, rows, weight, weight, weight, _row_spec(tr, d, first_gate_col),
                  _row_spec(tr, d, first_gate_col + 1), _vec_spec(d, 0), _vec_spec(d, 1), rows, _vec_spec(d)],
        out_specs=[rows] * 5,
        out_shape=[jax.ShapeDtypeStruct((t, d), dt) for dt in (F32, F32, BF16, F32, BF16)],
        compiler_params=_params("parallel"),
    )(a3, o, w_a, w_b, w_out, z, z, gate_b, gate_b, x, g2)


def _mix_bwd(dx, w, ya, yb, z, gate_b, dims, *, name, after=None, tr=512):
    t, d = ya.shape
    tr = _pick(t, tr, 8)
    first_gate_col = z.shape[1] // d - 2

    def body(dx_ref, w_ref, ya_ref, yb_ref, ga_ref, gb_ref, ba_ref, bb_ref, dya_ref, dyb_ref, dz_ref, db_ref):
        dm = _dot_nt(dx_ref[...], w_ref[...])
        g_a = _sigmoid(ga_ref[...].astype(F32) + ba_ref[...])
        g_b = _sigmoid(gb_ref[...].astype(F32) + bb_ref[...])
        dya_ref[...] = (dm * g_a).astype(BF16)
        dyb_ref[...] = (dm * g_b).astype(BF16)
        dl_a = dm * ya_ref[...] * g_a * (1.0 - g_a)
        dl_b = dm * yb_ref[...] * g_b * (1.0 - g_b)
        dz_ref[:, 0:d] = dl_a.astype(BF16)
        dz_ref[:, d:2 * d] = dl_b.astype(BF16)
        first = pl.program_id(0) == 0
        sums = jnp.concatenate([jnp.sum(dl_a, axis=0, keepdims=True), jnp.sum(dl_b, axis=0, keepdims=True)], axis=1)
        _accumulate(db_ref, sums, first)

    body, more_specs, more_args = _ordered(body, 8, after)
    return pl.pallas_call(
        body, name=name, grid=(t // tr,),
        in_specs=[_row_spec(tr, d), pl.BlockSpec((None, d, d), lambda i: (0, 0, 0)), _row_spec(tr, d), _row_spec(tr, d),
                  _row_spec(tr, d, first_gate_col), _row_spec(tr, d, first_gate_col + 1), _vec_spec(d, 0),
                  _vec_spec(d, 1)] + more_specs,
        out_specs=[_row_spec(tr, d), _row_spec(tr, d), _row_spec(tr, 2 * d), _vec_spec(2 * d)],
        out_shape=[jax.ShapeDtypeStruct((t, d), BF16)] * 2 + [jax.ShapeDtypeStruct((t, 2 * d), BF16),
                                                              jax.ShapeDtypeStruct((1, 2 * d), F32)],
        compiler_params=_params("arbitrary"),
    )(dx, w, ya, yb, z, z, gate_b, gate_b, *more_args)


def _adamw(w, grads, m, v, *, name, tr=256):
    r, c = w.shape
    tr = _pick(r, tr, 8)
    ng = len(grads)
    c1 = 1.0 - ADAM_B1 ** ADAM_STEP
    c2 = 1.0 - ADAM_B2 ** ADAM_STEP

    def body(*refs):
        w_ref, g_refs, m_ref, v_ref = refs[0], refs[1:1 + ng], refs[1 + ng], refs[2 + ng]
        g_out, d_out, m_out, v_out = refs[3 + ng:]
        g = g_refs[0][...]
        for extra in g_refs[1:]:
            g = g + extra[...]
        m_new = ADAM_B1 * m_ref[...] + (1.0 - ADAM_B1) * g
        v_new = ADAM_B2 * v_ref[...] + (1.0 - ADAM_B2) * (g * g)
        g_out[...] = g
        m_out[...] = m_new
        v_out[...] = v_new
        d_out[...] = -ADAM_LR * ((m_new / c1) / (jnp.sqrt(v_new / c2) + ADAM_EPS) + ADAM_WD * w_ref[...])

    spec = pl.BlockSpec((tr, c), lambda i: (i, 0))
    return pl.pallas_call(
        body, name=name, grid=(r // tr,),
        in_specs=[spec] * (3 + ng), out_specs=[spec] * 4, out_shape=[jax.ShapeDtypeStruct((r, c), F32)] * 4,
        compiler_params=_params("parallel"),
    )(w, *grads, m, v)


def _adamw_unpacked(ws, g, ms, vs, *, name):
    r, d = g.shape
    c1 = 1.0 - ADAM_B1 ** ADAM_STEP
    c2 = 1.0 - ADAM_B2 ** ADAM_STEP
    shapes = [a.shape for a in ws]
    views = [tuple(s) if len(s) == 2 else (1, s[0]) for s in shapes]
    n = len(views)

    def pieces():
        row = 0
        for i, (rows, width) in enumerate(views):
            for a in range(rows):
                pos = 0
                while pos < width:
                    at, lane = row + (a * width + pos) // d, (a * width + pos) % d
                    piece = min(width - pos, d - lane)
                    yield i, a, pos, at, lane, piece
                    pos += piece
            row += _packed_rows(rows * width, d)

    def body(*refs):
        ins, g_ref, outs, packed, scr = refs[:3 * n], refs[3 * n], refs[3 * n + 1:-2], refs[-2], refs[-1]
        packed[...] = jnp.zeros_like(packed)
        for kind in range(3):
            for i, a, pos, at, lane, piece in pieces():
                packed[kind, at:at + 1, lane:lane + piece] = ins[kind * n + i][a:a + 1, pos:pos + piece]
        grad = g_ref[...]
        m_new = ADAM_B1 * packed[1] + (1.0 - ADAM_B1) * grad
        v_new = ADAM_B2 * packed[2] + (1.0 - ADAM_B2) * (grad * grad)
        scr[0] = grad
        scr[1] = -ADAM_LR * ((m_new / c1) / (jnp.sqrt(v_new / c2) + ADAM_EPS) + ADAM_WD * packed[0])
        scr[2] = m_new
        scr[3] = v_new
        for kind in range(4):
            for i, a, pos, at, lane, piece in pieces():
                outs[kind * n + i][a:a + 1, pos:pos + piece] = scr[kind, at:at + 1, lane:lane + piece]

    whole = pl.BlockSpec(memory_space=pltpu.VMEM)
    outs = pl.pallas_call(
        body, name=name, in_specs=[whole] * (3 * n + 1), out_specs=[whole] * (4 * n),
        out_shape=[jax.ShapeDtypeStruct(view, F32) for _ in range(4) for view in views],
        scratch_shapes=[pltpu.VMEM((3, r, d), F32), pltpu.VMEM((4, r, d), F32)], compiler_params=_params(),
    )(*[a.reshape(view) for grp in (ws, ms, vs) for a, view in zip(grp, views)], g)
    return [[o.reshape(s) for o, s in zip(outs[kind * n:(kind + 1) * n], shapes)] for kind in range(4)]


CHIP_PEERS = ((1, 0), (0, 1), (1, 1))


def _place():
    return lax.axis_index("x"), lax.axis_index("y"), lax.axis_index("c")


HBM = pl.BlockSpec(memory_space=pltpu.HBM)
SEM = pl.BlockSpec(memory_space=pltpu.SEMAPHORE)
IN_FLIGHT = pltpu.SideEffectType.DATAFLOW_SIDE_EFFECTING


def _in_hbm(a):
    return pltpu.with_memory_space_constraint(a, pltpu.HBM)


def _cast_to_lands(shards, dtypes, *, name, after=None):
    n = len(shards)

    def body(*refs):
        ins, outs, bufs, sems = refs[:n], refs[n:2 * n], refs[2 * n:3 * n], refs[3 * n]
        x, y, _ = _place()
        copies = []
        for a in range(n):
            bufs[a][...] = ins[a][...].astype(dtypes[a])
            cp = pltpu.make_async_copy(bufs[a], outs[a].at[2 * x + y], sems.at[a])
            cp.start()
            copies.append(cp)
        for cp in copies:
            cp.wait()

    body, more_specs, more_args = _ordered(body, n, after)
    return pl.pallas_call(
        body, name=name, in_specs=[pl.BlockSpec(memory_space=pltpu.VMEM)] * n + more_specs, out_specs=[ANY] * n,
        out_shape=[jax.ShapeDtypeStruct((N_CHIPS,) + s.shape, dt) for s, dt in zip(shards, dtypes)],
        scratch_shapes=[pltpu.VMEM(s.shape, dt) for s, dt in zip(shards, dtypes)] + [pltpu.SemaphoreType.DMA((n,))],
        compiler_params=pltpu.CompilerParams(vmem_limit_bytes=V7X_VMEM_LIMIT_BYTES),
    )(*shards, *more_args)


def _chip_copy(src, dst, send, recv, flip, place):
    x, y, c = place
    return pltpu.make_async_remote_copy(src_ref=src, dst_ref=dst, send_sem=send, recv_sem=recv,
                                        device_id=(x ^ flip[0], y ^ flip[1], c), device_id_type=MESH)


def _my_part(land, place, halved):
    block = land.at[2 * place[0] + place[1]]
    if not halved:
        return block
    rows = land.shape[1] // 2
    return block.at[pl.ds(pl.multiple_of(place[2] * rows, rows), rows)]


def _gather_start(lands, after, *, name, halved=()):
    n = len(lands)

    def body(*refs):
        ins, send, recv, token = refs[:n], refs[n + 1], refs[n + 2], refs[-1]
        place = _place()
        for a in range(n):
            part = _my_part(ins[a], place, a in halved)
            for p, flip in enumerate(CHIP_PEERS):
                k = 3 * a + p
                _chip_copy(part, part, send.at[k], recv.at[k], flip, place).start()
        token[...] = jnp.zeros_like(token)

    outs = pl.pallas_call(
        body, name=name, in_specs=[HBM] * n + [ANY],
        out_specs=(SEM, SEM, *[HBM] * n, pl.BlockSpec(memory_space=pltpu.VMEM)),
        out_shape=(pltpu.SemaphoreType.DMA((3 * n,)), pltpu.SemaphoreType.DMA((3 * n,)),
                   *[pltpu.HBM(l.shape, l.dtype) for l in lands], jax.ShapeDtypeStruct((8, 128), F32)),
        input_output_aliases={a: 2 + a for a in range(n)},
        compiler_params=pltpu.CompilerParams(has_side_effects=IN_FLIGHT),
    )(*[_in_hbm(l) for l in lands], after)
    return outs[0], outs[1], list(outs[2:2 + n]), outs[-1]


def _gather_wait(send, recv, lands, after, *, name):
    n = len(lands)

    def body(*refs):
        ins, send_ref, recv_ref = refs[:n], refs[n], refs[n + 1]
        place = _place()
        for a in range(n):
            part = _my_part(ins[a], place, False)
            for p, flip in enumerate(CHIP_PEERS):
                k = 3 * a + p
                cp = _chip_copy(part, part, send_ref.at[k], recv_ref.at[k], flip, place)
                cp.wait_send()
                cp.wait_recv()

    after = list(after) if isinstance(after, (list, tuple)) else [after]
    return pl.pallas_call(
        body, name=name, in_specs=[HBM] * n + [SEM, SEM] + [ANY] * len(after), out_specs=[HBM] * n,
        out_shape=[pltpu.HBM(l.shape, l.dtype) for l in lands],
        input_output_aliases={a: a for a in range(n)},
        compiler_params=pltpu.CompilerParams(has_side_effects=IN_FLIGHT),
    )(*lands, send, recv, *after)


def _gather_wait_forward(send, recv, lands, after, *, name):
    n = len(lands)
    rows = lands[0].shape[1] // 2

    def body(*refs):
        ins, send_ref, recv_ref = refs[:n], refs[n], refs[n + 1]
        outs, on_send, on_recv = refs[-n - 2:-2], refs[-2], refs[-1]
        place = _place()
        x, y, c = place
        mine = pl.ds(pl.multiple_of(c * rows, rows), rows)
        half = _my_part(ins[0], place, True)
        for p, flip in enumerate(CHIP_PEERS):
            _chip_copy(half, half, send_ref.at[p], recv_ref.at[p], flip, place).wait_recv()
            chip = 2 * (x ^ flip[0]) + (y ^ flip[1])
            pltpu.make_async_remote_copy(
                src_ref=ins[0].at[chip].at[mine], dst_ref=outs[0].at[chip].at[mine], send_sem=on_send.at[p],
                recv_sem=on_recv.at[p], device_id=(x, y, 1 - c), device_id_type=MESH).start()
        for a in range(n):
            part = _my_part(ins[a], place, a == 0)
            for p, flip in enumerate(CHIP_PEERS):
                cp = _chip_copy(part, part, send_ref.at[3 * a + p], recv_ref.at[3 * a + p], flip, place)
                cp.wait_send()
                if a > 0:
                    cp.wait_recv()

    after = list(after) if isinstance(after, (list, tuple)) else [after]
    *zones, on_send, on_recv = pl.pallas_call(
        body, name=name, in_specs=[HBM] * n + [SEM, SEM] + [ANY] * len(after), out_specs=[HBM] * n + [SEM, SEM],
        out_shape=[pltpu.HBM(l.shape, l.dtype) for l in lands] + [pltpu.SemaphoreType.DMA((3,))] * 2,
        input_output_aliases={a: a for a in range(n)},
        compiler_params=pltpu.CompilerParams(has_side_effects=IN_FLIGHT),
    )(*lands, send, recv, *after)

    def wait_body(land_ref, send_ref, recv_ref, out_ref):
        x, y, c = _place()
        for p, (fx, fy) in enumerate(CHIP_PEERS):
            chip = 2 * (x ^ fx) + (y ^ fy)
            for half, arriving in ((c, False), (1 - c, True)):
                at = pl.ds(pl.multiple_of(half * rows, rows), rows)
                cp = pltpu.make_async_remote_copy(
                    src_ref=land_ref.at[chip].at[at], dst_ref=out_ref.at[chip].at[at], send_sem=send_ref.at[p],
                    recv_sem=recv_ref.at[p], device_id=(x, y, 1 - c), device_id_type=MESH)
                cp.wait_recv() if arriving else cp.wait_send()

    zones[0] = pl.pallas_call(
        wait_body, name=f"{name}_forwarded", in_specs=[HBM, SEM, SEM], out_specs=HBM,
        out_shape=pltpu.HBM(zones[0].shape, zones[0].dtype), input_output_aliases={0: 0},
        compiler_params=pltpu.CompilerParams(has_side_effects=IN_FLIGHT),
    )(zones[0], on_send, on_recv)
    return zones


def _scatter_start(grad, *, name):
    def body(g_ref, land_ref, send, recv, g_thru, land_thru, token):
        place = _place()
        for p, flip in enumerate(CHIP_PEERS):
            peer_chip = 2 * (place[0] ^ flip[0]) + (place[1] ^ flip[1])
            _chip_copy(g_ref.at[peer_chip], land_ref.at[p], send.at[p], recv.at[p], flip, place).start()
        token[...] = jnp.zeros_like(token)

    land = lax.empty((3,) + grad.shape[1:], grad.dtype)
    return pl.pallas_call(
        body, name=name, in_specs=[HBM, HBM],
        out_specs=(SEM, SEM, HBM, HBM, pl.BlockSpec(memory_space=pltpu.VMEM)),
        out_shape=(pltpu.SemaphoreType.DMA((3,)), pltpu.SemaphoreType.DMA((3,)), pltpu.HBM(grad.shape, grad.dtype),
                   pltpu.HBM(land.shape, land.dtype), jax.ShapeDtypeStruct((8, 128), F32)),
        input_output_aliases={0: 2, 1: 3},
        compiler_params=pltpu.CompilerParams(has_side_effects=IN_FLIGHT),
    )(_in_hbm(grad), _in_hbm(land))


def _scatter_wait(started, after, *, name):
    n = len(started)

    def body(*refs):
        grads, lands = refs[:n], refs[n:2 * n]
        sends, recvs = refs[2 * n:3 * n], refs[3 * n:4 * n]
        place = _place()
        for a in range(n):
            for p, flip in enumerate(CHIP_PEERS):
                cp = _chip_copy(grads[a].at[0], lands[a].at[p], sends[a].at[p], recvs[a].at[p], flip, place)
                cp.wait_send()
                cp.wait_recv()

    grads, lands = [s[2] for s in started], [s[3] for s in started]
    after = list(after) if isinstance(after, (list, tuple)) else [after]
    outs = pl.pallas_call(
        body, name=name, in_specs=[HBM] * (2 * n) + [SEM] * (2 * n) + [ANY] * len(after), out_specs=[HBM] * (2 * n),
        out_shape=[pltpu.HBM(a.shape, a.dtype) for a in grads + lands],
        input_output_aliases={a: a for a in range(2 * n)},
        compiler_params=pltpu.CompilerParams(has_side_effects=IN_FLIGHT),
    )(*grads, *lands, *[s[0] for s in started], *[s[1] for s in started], *after)
    return list(zip(outs[:n], outs[n:]))


def _sibling_copy(src, dst, send, recv, place):
    x, y, c = place
    return pltpu.make_async_remote_copy(src_ref=src, dst_ref=dst, send_sem=send, recv_sem=recv,
                                        device_id=(x, y, 1 - c), device_id_type=MESH)


def _swap_start(arrays, *, name):
    n = len(arrays)

    def body(*refs):
        ins, lands, send, recv, token = refs[:n], refs[n:2 * n], refs[2 * n], refs[2 * n + 1], refs[-1]
        place = _place()
        for a in range(n):
            _sibling_copy(ins[a], lands[a], send.at[a], recv.at[a], place).start()
        token[...] = jnp.zeros_like(token)

    both = [_in_hbm(a) for a in arrays] + [_in_hbm(lax.empty(a.shape, a.dtype)) for a in arrays]
    outs = pl.pallas_call(
        body, name=name, in_specs=[HBM] * (2 * n),
        out_specs=(SEM, SEM, *[HBM] * (2 * n), pl.BlockSpec(memory_space=pltpu.VMEM)),
        out_shape=(pltpu.SemaphoreType.DMA((n,)), pltpu.SemaphoreType.DMA((n,)),
                   *[pltpu.HBM(a.shape, a.dtype) for a in both], jax.ShapeDtypeStruct((8, 128), F32)),
        input_output_aliases={a: 2 + a for a in range(2 * n)},
        compiler_params=pltpu.CompilerParams(has_side_effects=IN_FLIGHT),
    )(*both)
    return outs[0], outs[1], list(outs[2:2 + n]), list(outs[2 + n:2 + 2 * n]), outs[-1]


def _swap_wait(started, after, *, name):
    send, recv, arrays, lands = started[:4]
    n = len(arrays)

    def body(*refs):
        ins, zones, send_ref, recv_ref = refs[:n], refs[n:2 * n], refs[2 * n], refs[2 * n + 1]
        place = _place()
        for a in range(n):
            cp = _sibling_copy(ins[a], zones[a], send_ref.at[a], recv_ref.at[a], place)
            cp.wait_send()
            cp.wait_recv()

    after = list(after) if isinstance(after, (list, tuple)) else [after]
    outs = pl.pallas_call(
        body, name=name, in_specs=[HBM] * (2 * n) + [SEM, SEM] + [ANY] * len(after), out_specs=[HBM] * (2 * n),
        out_shape=[pltpu.HBM(a.shape, a.dtype) for a in arrays + lands],
        input_output_aliases={a: a for a in range(2 * n)},
        compiler_params=pltpu.CompilerParams(has_side_effects=IN_FLIGHT),
    )(*arrays, *lands, send, recv, *after)
    return list(outs[:n]), list(outs[n:])


def _allreduce_start(packed, *, name):
    n_dev = 8

    def body(src_ref, land_ref, send, recv, src_thru, land_thru, token):
        x, y, c = _place()
        me = 4 * x + 2 * y + c
        for p in range(1, n_dev):
            pltpu.make_async_remote_copy(
                src_ref=src_ref, dst_ref=land_ref.at[me], send_sem=send.at[p - 1], recv_sem=recv.at[p - 1],
                device_id=(x ^ (p >> 2), y ^ ((p >> 1) & 1), c ^ (p & 1)), device_id_type=MESH).start()
        token[...] = jnp.zeros_like(token)

    land = lax.empty((n_dev,) + packed.shape, packed.dtype)
    return pl.pallas_call(
        body, name=name, in_specs=[HBM, HBM],
        out_specs=(SEM, SEM, HBM, HBM, pl.BlockSpec(memory_space=pltpu.VMEM)),
        out_shape=(pltpu.SemaphoreType.DMA((n_dev - 1,)), pltpu.SemaphoreType.DMA((n_dev - 1,)),
                   pltpu.HBM(packed.shape, packed.dtype), pltpu.HBM(land.shape, land.dtype),
                   jax.ShapeDtypeStruct((8, 128), F32)),
        input_output_aliases={0: 2, 1: 3},
        compiler_params=pltpu.CompilerParams(has_side_effects=IN_FLIGHT),
    )(_in_hbm(packed), _in_hbm(land))


def _allreduce_wait(started, after, *, name):
    send, recv, packed, land = started[:4]
    n_dev = 8

    def body(src_ref, land_ref, send_ref, recv_ref, *_):
        x, y, c = _place()
        for p in range(1, n_dev):
            cp = pltpu.make_async_remote_copy(
                src_ref=src_ref, dst_ref=land_ref.at[0], send_sem=send_ref.at[p - 1], recv_sem=recv_ref.at[p - 1],
                device_id=(x ^ (p >> 2), y ^ ((p >> 1) & 1), c ^ (p & 1)), device_id_type=MESH)
            cp.wait_send()
            cp.wait_recv()

    after = list(after) if isinstance(after, (list, tuple)) else [after]
    return pl.pallas_call(
        body, name=name, in_specs=[HBM, HBM, SEM, SEM] + [ANY] * len(after), out_specs=[HBM, HBM],
        out_shape=[pltpu.HBM(packed.shape, packed.dtype), pltpu.HBM(land.shape, land.dtype)],
        input_output_aliases={0: 0, 1: 1},
        compiler_params=pltpu.CompilerParams(has_side_effects=IN_FLIGHT),
    )(packed, land, send, recv, *after)


def _sum_devices(mine, land, *, name):
    n_dev = land.shape[0]

    def body(mine_ref, land_ref, out_ref):
        x, y, c = _place()
        me = 4 * x + 2 * y + c
        total = None
        for s in range(n_dev):
            part = jnp.where(me == s, mine_ref[...], land_ref[s])
            total = part if total is None else total + part
        out_ref[...] = total

    return pl.pallas_call(body, name=name, out_shape=jax.ShapeDtypeStruct(mine.shape, mine.dtype))(mine, land)


def _sum_received(grad, land, *, name, tr=256):
    _, r, c = grad.shape
    tr = _pick(r, tr, 8)

    def body(chip_ref, g_ref, l_ref, o_ref):
        o_ref[...] = ((g_ref[...] + l_ref[0].astype(F32)) + l_ref[1].astype(F32)) + l_ref[2].astype(F32)

    chip = (2 * lax.axis_index("x") + lax.axis_index("y")).astype(jnp.int32).reshape(1)
    return pl.pallas_call(
        body, name=name,
        grid_spec=pltpu.PrefetchScalarGridSpec(
            num_scalar_prefetch=1, grid=(r // tr,),
            in_specs=[pl.BlockSpec((None, tr, c), lambda i, chip_ref: (chip_ref[0], i, 0)),
                      pl.BlockSpec((3, tr, c), lambda i, chip_ref: (0, i, 0))],
            out_specs=pl.BlockSpec((tr, c), lambda i, chip_ref: (i, 0))),
        out_shape=jax.ShapeDtypeStruct((r, c), F32), compiler_params=_params("parallel"),
    )(chip, grad, land)


def _packed_rows(size, d):
    return -(-size // (8 * d)) * 8


def _pack_rows(arrays, d):
    rows = []
    for arr in arrays:
        flat = arr.reshape(-1).astype(F32)
        n = _packed_rows(flat.shape[0], d)
        rows.append(jnp.pad(flat, (0, n * d - flat.shape[0])).reshape(n, d))
    return jnp.concatenate(rows, axis=0)


def _unpack_rows(packed, shapes, d):
    out, row = [], 0
    for shape in shapes:
        size = math.prod(shape)
        n = _packed_rows(size, d)
        out.append(packed[row:row + n].reshape(-1)[:size].reshape(shape))
        row += n
    return out


SMALL = ("norm1_g", "gate_b", "conv_b", "conv_norm_g", "q_norm_g", "k_norm_g", "norm2_g", "ffn_conv_b")
LARGE = ("w_in", "w_conv_out", "w_attn_out", "w_out", "w_up", "w_down")
WEIGHTS = ("norm1_g", "w_in", "gate_b", "conv_w", "conv_b", "conv_norm_g", "w_conv_out", "q_norm_g", "k_norm_g",
           "w_attn_out", "w_out", "norm2_g", "w_up", "ffn_conv_w", "ffn_conv_b", "w_down")


def _after(vec, token):
    return vec if token is None else vec + token[0:1, 0:1]


def _local_step(dims, x, target, small, first_weights, other_weights, send_grad):
    d, f, heads = dims.d_model, dims.d_ff, dims.n_heads
    small = dict(small)
    row = lambda name: small[name].reshape(1, -1)
    head_sum = _head_sum_matrix(dims)
    head_spread = jnp.transpose(head_sum)
    ones = (head_sum, head_spread)
    gq = jnp.tile(row("q_norm_g"), (1, heads))
    gk = jnp.tile(row("k_norm_g"), (1, heads))
    one_shard = lambda w: w.reshape(1, -1, w.shape[-1])

    h = _rmsnorm_fwd(x, row("norm1_g"), name="norm1")
    full = first_weights(h)
    w_in = full["w_in"]
    conv_w = jnp.pad(full["conv_w"], ((0, CONV_HALO - dims.conv_width), (0, 0)))
    ffn_w = jnp.pad(full["ffn_conv_w"], ((0, FFN_HALO - dims.ffn_conv_width), (0, 0)))
    z = _mm_nn(h, w_in, out_dtype=BF16, after=full.get("token"), tm=2048, tn=1792, name="in_proj")
    a1, a3 = _conv_branch_fwd(z, conv_w, row("conv_b"), row("conv_norm_g"), dims, name="conv_branch")
    qkv = _qkv_layouts_fwd(z, gq, gk, ones, dims, name="qk_norm")
    per_group = {dil: _attn_fwd(*qkv[dil], dims, dil, name=f"attn_fwd_d{dil}") for dil in DILATIONS}
    o, lse = _attn_combine(per_group, head_spread, dims, name="attn_combine")
    full = other_weights(o)
    w_up = full["w_up"]
    w_co, w_ao, w_o, w_dn = (one_shard(full[k]) for k in ("w_conv_out", "w_attn_out", "w_out", "w_down"))
    ya, yb, mixed, x1, h2 = _mix_fwd(a3, o, w_co, w_ao, w_o, z, row("gate_b"), x, row("norm2_g"), dims,
                                     name="branch_projs_mix_out_proj_norm2")
    up = _mm_nn(h2, w_up, out_dtype=F32, tm=2048, name="up_proj")
    act = _ffn_act_fwd(up, ffn_w, row("ffn_conv_b"), dims, name="ffn_act")
    dy, dy_b, loss = _proj_residual_loss(act, w_dn, x1, target, tm=512, name="down_proj_loss")

    grads = {}

    def large(name, g):
        grads[name], g_bf16 = g
        return send_grad(name, grads[name], g_bf16)

    sent = large("w_down", _mm_tn(act, dy_b, n_shards=1, name="dw_down"))
    dact = _mm_nt(dy_b, w_dn, out_dtype=BF16, after=sent, name="d_act")
    dup, dfw, dfb = _ffn_bwd(dact, up, ffn_w, row("ffn_conv_b"), dims, name="ffn_bwd")
    grads["ffn_conv_w"], grads["ffn_conv_b"] = dfw[:dims.ffn_conv_width], dfb
    sent = large("w_up", _mm_tn(h2, dup, n_shards=N_CHIPS, name="dw_up"))
    dx1, dx1_b, grads["norm2_g"] = _mm_nt_rmsnorm_bwd(dup, w_up, x1, row("norm2_g"), dy, want_bf16=True, after=sent,
                                                     name="d_h2_norm2_bwd")
    sent = large("w_out", _mm_tn(mixed, dx1_b, n_shards=1, name="dw_out"))
    dya, dyb, dz_gate, grads["gate_b"] = _mix_bwd(dx1_b, w_o, ya, yb, z, row("gate_b"), dims, after=sent,
                                                  name="d_mix_gate_mix_bwd")
    sent = large("w_attn_out", _mm_tn(o, dyb, n_shards=1, name="dw_attn_out"))
    dos, deltas = _attn_bwd_prep(dyb, w_ao, o, head_sum, dims, after=sent, name="d_attn_bwd_prep")
    dqkv = {dil: _attn_bwd(*qkv[dil], dos[dil], lse[dil], deltas[dil], dims, dil, name=f"attn_bwd_d{dil}")
            for dil in DILATIONS}
    dz_qkv, dgq, dgk = _qkv_layouts_bwd(z, dqkv, gq, gk, ones, dims, name="qk_norm_bwd")
    grads["q_norm_g"] = dgq.reshape(heads, dims.head_dim).sum(axis=0)
    grads["k_norm_g"] = dgk.reshape(heads, dims.head_dim).sum(axis=0)
    sent = large("w_conv_out", _mm_tn(a3, dya, n_shards=1, name="dw_conv_out"))
    da1, grads["conv_norm_g"] = _mm_nt_rmsnorm_bwd(dya, w_co, a1, row("conv_norm_g"), None, want_bf16=False, silu=True,
                                                   after=sent, name="d_conv_act_norm_bwd")
    dz, dcw, grads["conv_b"] = _conv_branch_bwd(da1, z, conv_w, [dz_qkv, dz_gate], dims, name="conv_branch_bwd")
    grads["conv_w"] = dcw[:dims.conv_width]
    sent = large("w_in", _mm_tn(h, dz, n_shards=N_CHIPS, name="dw_in"))
    dx, grads["norm1_g"] = _mm_nt_rmsnorm_bwd(dz, w_in, x, row("norm1_g"), dx1, want_bf16=False, after=sent,
                                              name="d_h_norm1_bwd")
    return loss, dx, grads


def _step(dims, x, target, w, m, v):
    d = dims.d_model
    t = dims.tokens
    sq = lambda a: a.reshape(a.shape[1:])
    w2, m2, v2 = ({k: sq(a) for k, a in grp.items()} for grp in (w, m, v))

    conv_pad = jnp.pad(w2["conv_w"], ((0, CONV_HALO - dims.conv_width), (0, 0)))
    ffn_pad = jnp.pad(w2["ffn_conv_w"], ((0, FFN_HALO - dims.ffn_conv_width), (0, 0)))
    first_names = ("w_in", "conv_w", "ffn_conv_w")
    other_names = tuple(k for k in LARGE if k not in first_names)
    lands = dict(zip(first_names, _cast_to_lands([w2["w_in"], conv_pad, ffn_pad], [BF16, F32, F32], name="cast_first")))
    first = _gather_start([lands[k] for k in first_names], x, halved=(0,), name="gather_start_first")
    lands.update(zip(other_names, _cast_to_lands([w2[k] for k in other_names], [BF16] * len(other_names),
                                                 after=first[3], name="cast_other")))
    other = []
    cols = lambda g, rows: jnp.moveaxis(g, 0, 1).reshape(g.shape[1], -1)[:rows]

    def first_weights(after):
        got = dict(zip(first_names, _gather_wait_forward(*first[:3], [after] + [lands[k] for k in other_names],
                                                         name="gather_wait_first")))
        other.extend(_gather_start([lands[k] for k in other_names], got["w_in"], name="gather_start_other"))
        got["conv_w"] = cols(got["conv_w"], dims.conv_width)
        got["ffn_conv_w"] = cols(got["ffn_conv_w"], dims.ffn_conv_width)
        got["token"] = other[3]
        return got

    def other_weights(after):
        return dict(zip(other_names, _gather_wait(*other[:3], after, name="gather_wait_other")))

    started, full, swapping_others = {}, {}, []
    others = [k for k in LARGE if k != "w_in"]

    def my_sums(names, after, tag):
        arrived = _scatter_wait([started[k] for k in names], after, name=f"scatter_wait_{tag}")
        return [_sum_received(full[k], land, name=f"sum_{k}") for k, (_, land) in zip(names, arrived)]

    def send_grad(name, g, g_bf16):
        blocks = lambda a: a.reshape(N_CHIPS, -1, a.shape[-1])
        send, recv, g_thru, land, token = _scatter_start(blocks(g_bf16), name=f"scatter_start_{name}")
        started[name], full[name] = (send, recv, g_thru, land), blocks(g)
        if name != "w_in":
            return token
        swapping_others.extend(_swap_start(my_sums(others, token, "others"), name="swap_start_others"))
        return swapping_others[4]

    small = {k: w2[k] for k in SMALL}
    small["norm1_g"] = _after(small["norm1_g"].reshape(1, -1), first[3])
    loss, dx, grads = _local_step(dims, x.reshape(t, d), target.reshape(t, d), small, first_weights, other_weights, send_grad)

    def updates(names, mine, theirs):
        return {k: _adamw(w2[k], [a, b], m2[k], v2[k], name=f"adamw_{k}") for k, a, b in zip(names, mine, theirs)}

    small_names = SMALL + ("conv_w", "ffn_conv_w")
    packed = _pack_rows([grads[k] for k in small_names] + [loss[0, 0]], d)
    reducing = _allreduce_start(packed, name="allreduce_start")
    out = updates(others, *_swap_wait(swapping_others, [dx, reducing[4]], name="swap_wait_others"))
    mine_w_in = my_sums(["w_in"], [out[k][1] for k in others], "w_in")
    swapping_w_in = _swap_start(mine_w_in, name="swap_start_w_in")
    reduced = _sum_devices(*_allreduce_wait(reducing, swapping_w_in[4], name="allreduce_wait"), name="allreduce_sum")
    vector_rows = sum(_packed_rows(math.prod(grads[k].shape), d) for k in SMALL)
    tail_shapes = [grads[k].shape for k in ("conv_w", "ffn_conv_w")] + [()]
    conv_g, ffn_g, loss_total = _unpack_rows(reduced[vector_rows:], tail_shapes, d)
    chip = 2 * lax.axis_index("x") + lax.axis_index("y")
    sharded_g = [lax.dynamic_slice_in_dim(g, chip * w2[k].shape[1], w2[k].shape[1], axis=1)
                 for k, g in (("conv_w", conv_g), ("ffn_conv_w", ffn_g))]
    packed_g = jnp.concatenate([reduced[:vector_rows], _pack_rows(sharded_g, d)], axis=0)

    smalls = lambda grp: [grp[k] for k in small_names]
    results = _adamw_unpacked(smalls(w2), packed_g, smalls(m2), smalls(v2), name="adamw_small")
    out.update({k: tuple(r[i] for r in results) for i, k in enumerate(small_names)})
    out.update(updates(["w_in"], *_swap_wait(swapping_w_in, results[1][0], name="swap_wait_w_in")))

    lead = lambda a: a.reshape((1,) + a.shape)
    ordered = [[lead(out[k][j].reshape(w2[k].shape)) for k in WEIGHTS] for j in range(4)]
    return (loss_total, dx.reshape(x.shape), *ordered[0], *ordered[1], *ordered[2], *ordered[3])


def kernel(x, norm1_g, w_in, gate_b, conv_w, conv_b, conv_norm_g, w_conv_out, q_norm_g, k_norm_g, w_attn_out, w_out, norm2_g, w_up, ffn_conv_w, ffn_conv_b, w_down, loss_target, m_norm1_g, m_w_in, m_gate_b, m_conv_w, m_conv_b, m_conv_norm_g, m_w_conv_out, m_q_norm_g, m_k_norm_g, m_w_attn_out, m_w_out, m_norm2_g, m_w_up, m_ffn_conv_w, m_ffn_conv_b, m_w_down, v_norm1_g, v_w_in, v_gate_b, v_conv_w, v_conv_b, v_conv_norm_g, v_w_conv_out, v_q_norm_g, v_k_norm_g, v_w_attn_out, v_w_out, v_norm2_g, v_w_up, v_ffn_conv_w, v_ffn_conv_b, v_w_down):
    w = dict(zip(WEIGHTS, (norm1_g, w_in, gate_b, conv_w, conv_b, conv_norm_g, w_conv_out, q_norm_g, k_norm_g,
                           w_attn_out, w_out, norm2_g, w_up, ffn_conv_w, ffn_conv_b, w_down)))
    m = dict(zip(WEIGHTS, (m_norm1_g, m_w_in, m_gate_b, m_conv_w, m_conv_b, m_conv_norm_g, m_w_conv_out, m_q_norm_g,
                           m_k_norm_g, m_w_attn_out, m_w_out, m_norm2_g, m_w_up, m_ffn_conv_w, m_ffn_conv_b, m_w_down)))
    v = dict(zip(WEIGHTS, (v_norm1_g, v_w_in, v_gate_b, v_conv_w, v_conv_b, v_conv_norm_g, v_w_conv_out, v_q_norm_g,
                           v_k_norm_g, v_w_attn_out, v_w_out, v_norm2_g, v_w_up, v_ffn_conv_w, v_ffn_conv_b, v_w_down)))
    dims = Dims(d_model=x.shape[-1], batch_local=x.shape[0], seq=x.shape[1], d_ff=w_down.shape[1] * N_CHIPS)
    return _step(dims, x, loss_target, w, m, v)
```
